```python
import math
import jax, jax.numpy as jnp
from jax import lax
import numpy as np

D_MODEL = 1024
BATCH = 8
SEQ = 8192
DEPTH = 2

PLE_DIM = 256
HEAD_DIM = 64
N_ATTN_HEADS = 8
ATTN_WIDTH = N_ATTN_HEADS * HEAD_DIM
DILATED_BRANCHES = ((128, 1), (512, 4), (2048, 16))
ATTN_BLOCK = 128
N_REL_BUCKETS = 32
REL_MAX_DISTANCE = 2048
SGU_GROUPS = 4
SGU_GROUP_WIDTH = 64
SGU_WIDTH = SGU_GROUPS * SGU_GROUP_WIDTH
SGU_CHUNK = 128
SSM_GROUP_CH = 16
SSM_WIDTH = 256
SSM_GROUPS = SSM_WIDTH // SSM_GROUP_CH
SSM_STATE = 64
MIX_WIDTH = ATTN_WIDTH + SGU_WIDTH + SSM_WIDTH
IN_WIDTH = 3 * ATTN_WIDTH + 2 * SGU_WIDTH + SSM_WIDTH
D_FF = 2816
FFN_CONV = 3
EPS = 1e-6
NEG_INF = -1e30

kernel_name = "hybrid_dilated_sgu_s5_block"


def rms_scale(x):
    xf = x.astype(jnp.float32)
    return (xf * lax.rsqrt(jnp.mean(xf * xf, axis=-1, keepdims=True) + EPS)).astype(x.dtype)


def rms_norm(x, g):
    return rms_scale(x) * g.astype(x.dtype)


def t5_bucket(dist):
    max_exact = N_REL_BUCKETS // 2
    d = np.maximum(dist, 0)
    large = max_exact + (np.log(np.maximum(d, 1) / max_exact)
                         / np.log(REL_MAX_DISTANCE / max_exact)
                         * (N_REL_BUCKETS - max_exact)).astype(np.int32)
    large = np.minimum(large, N_REL_BUCKETS - 1)
    return np.where(d < max_exact, d, large).astype(np.int32)


def dilated_branch(q, k, v, rel_bias, window, dil):
    B, S, H, Dh = q.shape
    blk = ATTN_BLOCK
    steps = window // dil
    n = S // dil
    nblk = -(-n // blk)
    pad = nblk * blk - n

    def to_blocks(t):
        t = t.reshape(B, n, dil, H, Dh).transpose(0, 2, 1, 3, 4)
        t = jnp.pad(t, ((0, 0), (0, 0), (0, pad), (0, 0), (0, 0)))
        return t.reshape(B, dil, nblk, blk, H, Dh)

    def with_prev(t):
        prev = jnp.concatenate([jnp.zeros_like(t[:, :, :1]), t[:, :, :-1]], axis=2)
        return jnp.concatenate([prev, t], axis=3)

    qb = to_blocks(q)
    kw = with_prev(to_blocks(k))
    vw = with_prev(to_blocks(v))

    q_loc = np.arange(blk)[:, None]
    k_loc = np.arange(2 * blk)[None, :]
    rel = q_loc + blk - k_loc
    band = (rel >= 0) & (rel <= steps)
    first = (np.arange(nblk)[:, None, None] > 0) | (k_loc[None] >= blk)
    valid = (band[None] & first)[:, None]
    bias = rel_bias.astype(jnp.float32)[t5_bucket(rel * dil)]

    s = jnp.einsum('brcqhd,brckhd->brchqk', qb, kw).astype(jnp.float32) * (Dh ** -0.5)
    s = s + jnp.transpose(bias, (2, 0, 1))
    s = jnp.where(valid, s, NEG_INF)
    m = jnp.max(s, axis=-1, keepdims=True)
    e = jnp.exp(s - m)
    den = jnp.sum(e, axis=-1)
    o = jnp.einsum('brchqk,brckhd->brcqhd', e, vw.astype(jnp.float32))
    o = o / jnp.swapaxes(den, -1, -2)[..., None]
    lse = jnp.swapaxes(m[..., 0] + jnp.log(den), -1, -2)

    def from_blocks(t):
        t = t.reshape((B, dil, nblk * blk) + t.shape[4:])[:, :, :n]
        t = jnp.swapaxes(t, 1, 2)
        return t.reshape((B, S) + t.shape[3:])

    return from_blocks(o), from_blocks(lse)


def dilated_attention(q, k, v, rel_bias):
    B, S, H, Dh = q.shape
    outs, lses = [], []
    for window, dil in DILATED_BRANCHES:
        o, l = dilated_branch(q, k, v, rel_bias, window, dil)
        outs.append(o)
        lses.append(l)
    wts = jax.nn.softmax(jnp.stack(lses), axis=0)
    o = jnp.sum(wts[..., None] * jnp.stack(outs), axis=0)
    return o.reshape(B, S, H * Dh)


def spatial_gating(z, ln_g, ln_b, w_s, b_s):
    B, S, _ = z.shape
    z = jax.nn.gelu(z)
    u, v = jnp.split(z, 2, axis=-1)
    vf = v.astype(jnp.float32).reshape(B, S // SGU_CHUNK, SGU_CHUNK, SGU_GROUPS, SGU_GROUP_WIDTH)
    mu = jnp.mean(vf, axis=-1, keepdims=True)
    var = jnp.mean(jnp.square(vf - mu), axis=-1, keepdims=True)
    vn = ((vf - mu) * lax.rsqrt(var + EPS)
          * ln_g.astype(jnp.float32).reshape(SGU_GROUPS, SGU_GROUP_WIDTH)
          + ln_b.astype(jnp.float32).reshape(SGU_GROUPS, SGU_GROUP_WIDTH))
    causal = np.tril(np.ones((SGU_CHUNK, SGU_CHUNK), dtype=bool))
    w = jnp.where(causal, w_s.astype(jnp.float32), 0.0)
    mixed = jnp.einsum('gts,bnsgc->bntgc', w, vn) + b_s.astype(jnp.float32).T[:, :, None]
    return (u.astype(jnp.float32) * mixed.reshape(B, S, SGU_WIDTH)).astype(z.dtype)


def s5_ssm(u, a_re, a_im, log_dt, b_re, b_im, c_re, c_im, d_skip, glu_w, glu_b):
    B, S, _ = u.shape
    f32 = lambda t: t.astype(jnp.float32)
    lam = lax.complex(f32(a_re), f32(a_im))
    dt = jnp.exp(f32(log_dt))[:, None]
    a_bar = jnp.exp(lam * dt)
    b_bar = ((a_bar - 1.0) / lam)[:, :, None] * lax.complex(f32(b_re), f32(b_im))
    c_mat = lax.complex(f32(c_re), f32(c_im))
    ug = f32(u).reshape(B, S, SSM_GROUPS, SSM_GROUP_CH)
    bu = jnp.einsum('gnc,bsgc->bsgn', b_bar, ug.astype(jnp.complex64))
    a_seq = jnp.broadcast_to(a_bar, bu.shape)

    def combine(left, right):
        a_l, b_l = left
        a_r, b_r = right
        return a_r * a_l, a_r * b_l + b_r

    _, xs = lax.associative_scan(combine, (a_seq, bu), axis=1)
    y = jnp.real(jnp.einsum('gcn,bsgn->bsgc', c_mat, xs)) \
        + f32(d_skip).reshape(SSM_GROUPS, SSM_GROUP_CH) * ug
    y = jax.nn.gelu(y.reshape(B, S, SSM_WIDTH))
    y = y * jax.nn.sigmoid(y @ f32(glu_w) + f32(glu_b))
    return y.astype(u.dtype)


def conv_ffn(x, w_up, conv_w, conv_b, w_down):
    h = x @ w_up
    ch = h.shape[-1]
    h = lax.conv_general_dilated(h, conv_w[:, None, :].astype(h.dtype), window_strides=(1,),
                                 padding=((FFN_CONV - 1, 0),),
                                 dimension_numbers=('NWC', 'WIO', 'NWC'),
                                 feature_group_count=ch) + conv_b
    val, gate = jnp.split(h, 2, axis=-1)
    return (jax.nn.gelu(gate) * val) @ w_down


def _fwd_setup_inputs(seed: int = 0) -> dict:
    key = jax.random.key(seed)
    ks = jax.random.split(key, 30)
    nrm = lambda k, shape, scale: jax.random.normal(k, shape, jnp.float32) * scale
    gain = lambda k, shape: 1.0 + 0.05 * jax.random.normal(k, shape, jnp.float32)
    L = DEPTH
    return {
        "x": nrm(ks[0], (BATCH, SEQ, D_MODEL), 1.0),
        "p": nrm(ks[1], (DEPTH, BATCH, SEQ, PLE_DIM), 1.0),
        "rel_bias": nrm(ks[2], (N_REL_BUCKETS, N_ATTN_HEADS), 0.5),
        "norm_attn_g": gain(ks[3], (L, D_MODEL)),
        "w_in": nrm(ks[4], (L, D_MODEL, IN_WIDTH), D_MODEL ** -0.5),
        "sgu_ln_g": gain(ks[5], (L, SGU_WIDTH)),
        "sgu_ln_b": nrm(ks[6], (L, SGU_WIDTH), 0.02),
        "sgu_w": nrm(ks[7], (L, SGU_GROUPS, SGU_CHUNK, SGU_CHUNK), 0.5 * SGU_CHUNK ** -0.5),
        "sgu_b": 1.0 + nrm(ks[8], (L, SGU_GROUPS, SGU_CHUNK), 0.01),
        "ssm_a_re": -0.5 + nrm(ks[9], (L, SSM_GROUPS, SSM_STATE), 0.01),
        "ssm_a_im": math.pi * jnp.arange(SSM_STATE, dtype=jnp.float32)
                    + nrm(ks[10], (L, SSM_GROUPS, SSM_STATE), 0.01),
        "ssm_log_dt": jax.random.uniform(ks[11], (L, SSM_GROUPS), jnp.float32,
                                         math.log(1e-3), math.log(1e-1)),
        "ssm_b_re": nrm(ks[12], (L, SSM_GROUPS, SSM_STATE, SSM_GROUP_CH), (2 * SSM_GROUP_CH) ** -0.5),
        "ssm_b_im": nrm(ks[13], (L, SSM_GROUPS, SSM_STATE, SSM_GROUP_CH), (2 * SSM_GROUP_CH) ** -0.5),
        "ssm_c_re": nrm(ks[14], (L, SSM_GROUPS, SSM_GROUP_CH, SSM_STATE), SSM_STATE ** -0.5),
        "ssm_c_im": nrm(ks[15], (L, SSM_GROUPS, SSM_GROUP_CH, SSM_STATE), SSM_STATE ** -0.5),
        "ssm_d": nrm(ks[16], (L, SSM_WIDTH), 1.0),
        "ssm_glu_w": nrm(ks[17], (L, SSM_WIDTH, SSM_WIDTH), SSM_WIDTH ** -0.5),
        "ssm_glu_b": nrm(ks[18], (L, SSM_WIDTH), 0.01),
        "branch_norm_g": gain(ks[19], (L, MIX_WIDTH)),
        "w_out": nrm(ks[20], (L, MIX_WIDTH, D_MODEL), MIX_WIDTH ** -0.5),
        "norm_ffn_g": gain(ks[21], (L, D_MODEL)),
        "ffn_w_up": nrm(ks[22], (L, D_MODEL, 2 * D_FF), D_MODEL ** -0.5),
        "ffn_conv_w": nrm(ks[23], (L, FFN_CONV, 2 * D_FF), FFN_CONV ** -0.5),
        "ffn_conv_b": nrm(ks[24], (L, 2 * D_FF), 0.01),
        "ffn_w_down": nrm(ks[25], (L, D_FF, D_MODEL), D_FF ** -0.5),
        "norm_ple_g": gain(ks[26], (L, D_MODEL)),
        "ple_w_gate": nrm(ks[27], (L, D_MODEL, D_MODEL), D_MODEL ** -0.5),
        "ple_w_proj": nrm(ks[28], (L, PLE_DIM, D_MODEL), PLE_DIM ** -0.5),
        "final_norm_g": gain(ks[29], (D_MODEL,)),
    }


def _fwd_reference(x, p, rel_bias, norm_attn_g, w_in, sgu_ln_g, sgu_ln_b, sgu_w, sgu_b,
              ssm_a_re, ssm_a_im, ssm_log_dt, ssm_b_re, ssm_b_im, ssm_c_re, ssm_c_im,
              ssm_d, ssm_glu_w, ssm_glu_b, branch_norm_g, w_out, norm_ffn_g,
              ffn_w_up, ffn_conv_w, ffn_conv_b, ffn_w_down, norm_ple_g, ple_w_gate,
              ple_w_proj, final_norm_g):
    B, S, _ = x.shape
    o_k = ATTN_WIDTH
    o_v = 2 * ATTN_WIDTH
    o_g = 3 * ATTN_WIDTH
    o_s = o_g + 2 * SGU_WIDTH
    h = x
    for i in range(DEPTH):
        z = rms_norm(h, norm_attn_g[i]) @ w_in[i]
        q = z[..., :o_k].reshape(B, S, N_ATTN_HEADS, HEAD_DIM)
        k = z[..., o_k:o_v].reshape(B, S, N_ATTN_HEADS, HEAD_DIM)
        v = z[..., o_v:o_g].reshape(B, S, N_ATTN_HEADS, HEAD_DIM)
        y_attn = dilated_attention(q, k, v, rel_bias).astype(h.dtype)
        y_sgu = spatial_gating(z[..., o_g:o_s], sgu_ln_g[i], sgu_ln_b[i], sgu_w[i], sgu_b[i])
        y_ssm = s5_ssm(z[..., o_s:], ssm_a_re[i], ssm_a_im[i], ssm_log_dt[i],
                       ssm_b_re[i], ssm_b_im[i], ssm_c_re[i], ssm_c_im[i],
                       ssm_d[i], ssm_glu_w[i], ssm_glu_b[i])
        mix = jnp.concatenate([rms_scale(y_attn), rms_scale(y_sgu), rms_scale(y_ssm)],
                              axis=-1) * branch_norm_g[i]
        h = h + mix @ w_out[i]
        h = h + conv_ffn(rms_norm(h, norm_ffn_g[i]), ffn_w_up[i], ffn_conv_w[i],
                         ffn_conv_b[i], ffn_w_down[i])
        gate = jax.nn.sigmoid(rms_norm(h, norm_ple_g[i]) @ ple_w_gate[i])
        h = h + gate * (p[i] @ ple_w_proj[i])
    return rms_norm(h, final_norm_g)


import jax as _jax
import jax.numpy as _jnp

TWIN_FORMAT = 'train_step'
FWD_PARAMS = ['x', 'p', 'rel_bias', 'norm_attn_g', 'w_in', 'sgu_ln_g', 'sgu_ln_b', 'sgu_w', 'sgu_b', 'ssm_a_re', 'ssm_a_im', 'ssm_log_dt', 'ssm_b_re', 'ssm_b_im', 'ssm_c_re', 'ssm_c_im', 'ssm_d', 'ssm_glu_w', 'ssm_glu_b', 'branch_norm_g', 'w_out', 'norm_ffn_g', 'ffn_w_up', 'ffn_conv_w', 'ffn_conv_b', 'ffn_w_down', 'norm_ple_g', 'ple_w_gate', 'ple_w_proj', 'final_norm_g']
TWIN_WEIGHTS = ['rel_bias', 'norm_attn_g', 'w_in', 'sgu_ln_g', 'sgu_ln_b', 'sgu_w', 'sgu_b', 'ssm_a_re', 'ssm_a_im', 'ssm_log_dt', 'ssm_b_re', 'ssm_b_im', 'ssm_c_re', 'ssm_c_im', 'ssm_d', 'ssm_glu_w', 'ssm_glu_b', 'branch_norm_g', 'w_out', 'norm_ffn_g', 'ffn_w_up', 'ffn_conv_w', 'ffn_conv_b', 'ffn_w_down', 'norm_ple_g', 'ple_w_gate', 'ple_w_proj', 'final_norm_g']
TWIN_DIFF_INPUT = 'x'
TWIN_INPUTS = ['x', 'p', 'rel_bias', 'norm_attn_g', 'w_in', 'sgu_ln_g', 'sgu_ln_b', 'sgu_w', 'sgu_b', 'ssm_a_re', 'ssm_a_im', 'ssm_log_dt', 'ssm_b_re', 'ssm_b_im', 'ssm_c_re', 'ssm_c_im', 'ssm_d', 'ssm_glu_w', 'ssm_glu_b', 'branch_norm_g', 'w_out', 'norm_ffn_g', 'ffn_w_up', 'ffn_conv_w', 'ffn_conv_b', 'ffn_w_down', 'norm_ple_g', 'ple_w_gate', 'ple_w_proj', 'final_norm_g', 'loss_target', 'm_rel_bias', 'm_norm_attn_g', 'm_w_in', 'm_sgu_ln_g', 'm_sgu_ln_b', 'm_sgu_w', 'm_sgu_b', 'm_ssm_a_re', 'm_ssm_a_im', 'm_ssm_log_dt', 'm_ssm_b_re', 'm_ssm_b_im', 'm_ssm_c_re', 'm_ssm_c_im', 'm_ssm_d', 'm_ssm_glu_w', 'm_ssm_glu_b', 'm_branch_norm_g', 'm_w_out', 'm_norm_ffn_g', 'm_ffn_w_up', 'm_ffn_conv_w', 'm_ffn_conv_b', 'm_ffn_w_down', 'm_norm_ple_g', 'm_ple_w_gate', 'm_ple_w_proj', 'm_final_norm_g', 'v_rel_bias', 'v_norm_attn_g', 'v_w_in', 'v_sgu_ln_g', 'v_sgu_ln_b', 'v_sgu_w', 'v_sgu_b', 'v_ssm_a_re', 'v_ssm_a_im', 'v_ssm_log_dt', 'v_ssm_b_re', 'v_ssm_b_im', 'v_ssm_c_re', 'v_ssm_c_im', 'v_ssm_d', 'v_ssm_glu_w', 'v_ssm_glu_b', 'v_branch_norm_g', 'v_w_out', 'v_norm_ffn_g', 'v_ffn_w_up', 'v_ffn_conv_w', 'v_ffn_conv_b', 'v_ffn_w_down', 'v_norm_ple_g', 'v_ple_w_gate', 'v_ple_w_proj', 'v_final_norm_g']
TWIN_OUTPUTS = ['loss', 'grad_x', 'grad_rel_bias', 'grad_norm_attn_g', 'grad_w_in', 'grad_sgu_ln_g', 'grad_sgu_ln_b', 'grad_sgu_w', 'grad_sgu_b', 'grad_ssm_a_re', 'grad_ssm_a_im', 'grad_ssm_log_dt', 'grad_ssm_b_re', 'grad_ssm_b_im', 'grad_ssm_c_re', 'grad_ssm_c_im', 'grad_ssm_d', 'grad_ssm_glu_w', 'grad_ssm_glu_b', 'grad_branch_norm_g', 'grad_w_out', 'grad_norm_ffn_g', 'grad_ffn_w_up', 'grad_ffn_conv_w', 'grad_ffn_conv_b', 'grad_ffn_w_down', 'grad_norm_ple_g', 'grad_ple_w_gate', 'grad_ple_w_proj', 'grad_final_norm_g', 'delta_rel_bias', 'delta_norm_attn_g', 'delta_w_in', 'delta_sgu_ln_g', 'delta_sgu_ln_b', 'delta_sgu_w', 'delta_sgu_b', 'delta_ssm_a_re', 'delta_ssm_a_im', 'delta_ssm_log_dt', 'delta_ssm_b_re', 'delta_ssm_b_im', 'delta_ssm_c_re', 'delta_ssm_c_im', 'delta_ssm_d', 'delta_ssm_glu_w', 'delta_ssm_glu_b', 'delta_branch_norm_g', 'delta_w_out', 'delta_norm_ffn_g', 'delta_ffn_w_up', 'delta_ffn_conv_w', 'delta_ffn_conv_b', 'delta_ffn_w_down', 'delta_norm_ple_g', 'delta_ple_w_gate', 'delta_ple_w_proj', 'delta_final_norm_g', 'new_m_rel_bias', 'new_m_norm_attn_g', 'new_m_w_in', 'new_m_sgu_ln_g', 'new_m_sgu_ln_b', 'new_m_sgu_w', 'new_m_sgu_b', 'new_m_ssm_a_re', 'new_m_ssm_a_im', 'new_m_ssm_log_dt', 'new_m_ssm_b_re', 'new_m_ssm_b_im', 'new_m_ssm_c_re', 'new_m_ssm_c_im', 'new_m_ssm_d', 'new_m_ssm_glu_w', 'new_m_ssm_glu_b', 'new_m_branch_norm_g', 'new_m_w_out', 'new_m_norm_ffn_g', 'new_m_ffn_w_up', 'new_m_ffn_conv_w', 'new_m_ffn_conv_b', 'new_m_ffn_w_down', 'new_m_norm_ple_g', 'new_m_ple_w_gate', 'new_m_ple_w_proj', 'new_m_final_norm_g', 'new_v_rel_bias', 'new_v_norm_attn_g', 'new_v_w_in', 'new_v_sgu_ln_g', 'new_v_sgu_ln_b', 'new_v_sgu_w', 'new_v_sgu_b', 'new_v_ssm_a_re', 'new_v_ssm_a_im', 'new_v_ssm_log_dt', 'new_v_ssm_b_re', 'new_v_ssm_b_im', 'new_v_ssm_c_re', 'new_v_ssm_c_im', 'new_v_ssm_d', 'new_v_ssm_glu_w', 'new_v_ssm_glu_b', 'new_v_branch_norm_g', 'new_v_w_out', 'new_v_norm_ffn_g', 'new_v_ffn_w_up', 'new_v_ffn_conv_w', 'new_v_ffn_conv_b', 'new_v_ffn_w_down', 'new_v_norm_ple_g', 'new_v_ple_w_gate', 'new_v_ple_w_proj', 'new_v_final_norm_g']
TWIN_LEAF_KINDS = {'loss': 'loss', 'grad_x': 'grad_x', 'grad_rel_bias': 'grad_w', 'grad_norm_attn_g': 'grad_w', 'grad_w_in': 'grad_w', 'grad_sgu_ln_g': 'grad_w', 'grad_sgu_ln_b': 'grad_w', 'grad_sgu_w': 'grad_w', 'grad_sgu_b': 'grad_w', 'grad_ssm_a_re': 'grad_w', 'grad_ssm_a_im': 'grad_w', 'grad_ssm_log_dt': 'grad_w', 'grad_ssm_b_re': 'grad_w', 'grad_ssm_b_im': 'grad_w', 'grad_ssm_c_re': 'grad_w', 'grad_ssm_c_im': 'grad_w', 'grad_ssm_d': 'grad_w', 'grad_ssm_glu_w': 'grad_w', 'grad_ssm_glu_b': 'grad_w', 'grad_branch_norm_g': 'grad_w', 'grad_w_out': 'grad_w', 'grad_norm_ffn_g': 'grad_w', 'grad_ffn_w_up': 'grad_w', 'grad_ffn_conv_w': 'grad_w', 'grad_ffn_conv_b': 'grad_w', 'grad_ffn_w_down': 'grad_w', 'grad_norm_ple_g': 'grad_w', 'grad_ple_w_gate': 'grad_w', 'grad_ple_w_proj': 'grad_w', 'grad_final_norm_g': 'grad_w', 'delta_rel_bias': 'delta_w', 'delta_norm_attn_g': 'delta_w', 'delta_w_in': 'delta_w', 'delta_sgu_ln_g': 'delta_w', 'delta_sgu_ln_b': 'delta_w', 'delta_sgu_w': 'delta_w', 'delta_sgu_b': 'delta_w', 'delta_ssm_a_re': 'delta_w', 'delta_ssm_a_im': 'delta_w', 'delta_ssm_log_dt': 'delta_w', 'delta_ssm_b_re': 'delta_w', 'delta_ssm_b_im': 'delta_w', 'delta_ssm_c_re': 'delta_w', 'delta_ssm_c_im': 'delta_w', 'delta_ssm_d': 'delta_w', 'delta_ssm_glu_w': 'delta_w', 'delta_ssm_glu_b': 'delta_w', 'delta_branch_norm_g': 'delta_w', 'delta_w_out': 'delta_w', 'delta_norm_ffn_g': 'delta_w', 'delta_ffn_w_up': 'delta_w', 'delta_ffn_conv_w': 'delta_w', 'delta_ffn_conv_b': 'delta_w', 'delta_ffn_w_down': 'delta_w', 'delta_norm_ple_g': 'delta_w', 'delta_ple_w_gate': 'delta_w', 'delta_ple_w_proj': 'delta_w', 'delta_final_norm_g': 'delta_w', 'new_m_rel_bias': 'new_m', 'new_m_norm_attn_g': 'new_m', 'new_m_w_in': 'new_m', 'new_m_sgu_ln_g': 'new_m', 'new_m_sgu_ln_b': 'new_m', 'new_m_sgu_w': 'new_m', 'new_m_sgu_b': 'new_m', 'new_m_ssm_a_re': 'new_m', 'new_m_ssm_a_im': 'new_m', 'new_m_ssm_log_dt': 'new_m', 'new_m_ssm_b_re': 'new_m', 'new_m_ssm_b_im': 'new_m', 'new_m_ssm_c_re': 'new_m', 'new_m_ssm_c_im': 'new_m', 'new_m_ssm_d': 'new_m', 'new_m_ssm_glu_w': 'new_m', 'new_m_ssm_glu_b': 'new_m', 'new_m_branch_norm_g': 'new_m', 'new_m_w_out': 'new_m', 'new_m_norm_ffn_g': 'new_m', 'new_m_ffn_w_up': 'new_m', 'new_m_ffn_conv_w': 'new_m', 'new_m_ffn_conv_b': 'new_m', 'new_m_ffn_w_down': 'new_m', 'new_m_norm_ple_g': 'new_m', 'new_m_ple_w_gate': 'new_m', 'new_m_ple_w_proj': 'new_m', 'new_m_final_norm_g': 'new_m', 'new_v_rel_bias': 'new_v', 'new_v_norm_attn_g': 'new_v', 'new_v_w_in': 'new_v', 'new_v_sgu_ln_g': 'new_v', 'new_v_sgu_ln_b': 'new_v', 'new_v_sgu_w': 'new_v', 'new_v_sgu_b': 'new_v', 'new_v_ssm_a_re': 'new_v', 'new_v_ssm_a_im': 'new_v', 'new_v_ssm_log_dt': 'new_v', 'new_v_ssm_b_re': 'new_v', 'new_v_ssm_b_im': 'new_v', 'new_v_ssm_c_re': 'new_v', 'new_v_ssm_c_im': 'new_v', 'new_v_ssm_d': 'new_v', 'new_v_ssm_glu_w': 'new_v', 'new_v_ssm_glu_b': 'new_v', 'new_v_branch_norm_g': 'new_v', 'new_v_w_out': 'new_v', 'new_v_norm_ffn_g': 'new_v', 'new_v_ffn_w_up': 'new_v', 'new_v_ffn_conv_w': 'new_v', 'new_v_ffn_conv_b': 'new_v', 'new_v_ffn_w_down': 'new_v', 'new_v_norm_ple_g': 'new_v', 'new_v_ple_w_gate': 'new_v', 'new_v_ple_w_proj': 'new_v', 'new_v_final_norm_g': 'new_v'}


def _forward(args):
    return _fwd_reference(*[args[k] for k in FWD_PARAMS])


def _output_shape():
    def fwd():
        inp = _fwd_setup_inputs(0)
        return _fwd_reference(*[inp[k] for k in FWD_PARAMS])
    out = _jax.eval_shape(fwd)
    return out.shape, out.dtype

N_MICROBATCH = 1
ADAM_LR = 0.001
ADAM_B1 = 0.9
ADAM_B2 = 0.999
ADAM_EPS = 1e-08
ADAM_WD = 0.01
ADAM_STEP = 10
PER_EXAMPLE_BATCH_AXIS = {'x': 0, 'p': 1, 'loss_target': 0}
SHARED_INPUTS = []
_WEIGHT_DTYPES = {'rel_bias': _jnp.float32, 'norm_attn_g': _jnp.float32, 'w_in': _jnp.float32, 'sgu_ln_g': _jnp.float32, 'sgu_ln_b': _jnp.float32, 'sgu_w': _jnp.float32, 'sgu_b': _jnp.float32, 'ssm_a_re': _jnp.float32, 'ssm_a_im': _jnp.float32, 'ssm_log_dt': _jnp.float32, 'ssm_b_re': _jnp.float32, 'ssm_b_im': _jnp.float32, 'ssm_c_re': _jnp.float32, 'ssm_c_im': _jnp.float32, 'ssm_d': _jnp.float32, 'ssm_glu_w': _jnp.float32, 'ssm_glu_b': _jnp.float32, 'branch_norm_g': _jnp.float32, 'w_out': _jnp.float32, 'norm_ffn_g': _jnp.float32, 'ffn_w_up': _jnp.float32, 'ffn_conv_w': _jnp.float32, 'ffn_conv_b': _jnp.float32, 'ffn_w_down': _jnp.float32, 'norm_ple_g': _jnp.float32, 'ple_w_gate': _jnp.float32, 'ple_w_proj': _jnp.float32, 'final_norm_g': _jnp.float32}
MOMENT_SCALE = {'rel_bias': 2.719912e-01, 'norm_attn_g': 3.105974e-01, 'w_in': 1.955230e-01, 'sgu_ln_g': 5.445403e-02, 'sgu_ln_b': 5.803787e-02, 'sgu_w': 7.590255e-02, 'sgu_b': 9.533708e-02, 'ssm_a_re': 2.681160e-02, 'ssm_a_im': 2.106980e-02, 'ssm_log_dt': 1.296047e+01, 'ssm_b_re': 1.090659e-02, 'ssm_b_im': 1.313401e-02, 'ssm_c_re': 1.639541e-02, 'ssm_c_im': 1.776020e-02, 'ssm_d': 5.893274e-01, 'ssm_glu_w': 9.121504e-02, 'ssm_glu_b': 2.912771e-01, 'branch_norm_g': 4.622899e-01, 'w_out': 4.207770e-01, 'norm_ffn_g': 1.668776e-01, 'ffn_w_up': 6.479542e-02, 'ffn_conv_w': 6.787619e-02, 'ffn_conv_b': 1.650147e-01, 'ffn_w_down': 1.172318e-01, 'norm_ple_g': 3.362102e-02, 'ple_w_gate': 3.611319e-02, 'ple_w_proj': 7.539498e-02, 'final_norm_g': 6.465338e+01}


def _to_microbatches(a, axis):
    t = _jnp.moveaxis(a, axis, 0)
    t = t.reshape((N_MICROBATCH, t.shape[0] // N_MICROBATCH) + t.shape[1:])
    return _jnp.moveaxis(t, 1, axis + 1)


def setup_inputs(seed: int = 0) -> dict:
    inp = _fwd_setup_inputs(seed)
    key = _jax.random.fold_in(_jax.random.key(seed), 7919)
    shape, _ = _output_shape()
    out = dict(inp)
    out["loss_target"] = _jax.random.normal(_jax.random.fold_in(key, 0), shape, _jnp.float32)
    for i, name in enumerate(TWIN_WEIGHTS):
        w = inp[name].astype(_jnp.float32)
        if MOMENT_SCALE is None:
            s = _jnp.sqrt(_jnp.mean(_jnp.square(w)) + 1e-30)
        else:
            s = MOMENT_SCALE[name]
        km, kv = _jax.random.split(_jax.random.fold_in(key, i + 1))
        out[name] = w
        out["m_" + name] = s * _jax.random.normal(km, w.shape, _jnp.float32)
        out["v_" + name] = (s * s) * _jax.random.uniform(kv, w.shape, _jnp.float32, 0.5, 1.5)
    if N_MICROBATCH > 1:
        for name, axis in PER_EXAMPLE_BATCH_AXIS.items():
            out[name] = _to_microbatches(out[name], axis)
    return {'x': out['x'], 'p': out['p'], 'rel_bias': out['rel_bias'], 'norm_attn_g': out['norm_attn_g'], 'w_in': out['w_in'], 'sgu_ln_g': out['sgu_ln_g'], 'sgu_ln_b': out['sgu_ln_b'], 'sgu_w': out['sgu_w'], 'sgu_b': out['sgu_b'], 'ssm_a_re': out['ssm_a_re'], 'ssm_a_im': out['ssm_a_im'], 'ssm_log_dt': out['ssm_log_dt'], 'ssm_b_re': out['ssm_b_re'], 'ssm_b_im': out['ssm_b_im'], 'ssm_c_re': out['ssm_c_re'], 'ssm_c_im': out['ssm_c_im'], 'ssm_d': out['ssm_d'], 'ssm_glu_w': out['ssm_glu_w'], 'ssm_glu_b': out['ssm_glu_b'], 'branch_norm_g': out['branch_norm_g'], 'w_out': out['w_out'], 'norm_ffn_g': out['norm_ffn_g'], 'ffn_w_up': out['ffn_w_up'], 'ffn_conv_w': out['ffn_conv_w'], 'ffn_conv_b': out['ffn_conv_b'], 'ffn_w_down': out['ffn_w_down'], 'norm_ple_g': out['norm_ple_g'], 'ple_w_gate': out['ple_w_gate'], 'ple_w_proj': out['ple_w_proj'], 'final_norm_g': out['final_norm_g'], 'loss_target': out['loss_target'], 'm_rel_bias': out['m_rel_bias'], 'm_norm_attn_g': out['m_norm_attn_g'], 'm_w_in': out['m_w_in'], 'm_sgu_ln_g': out['m_sgu_ln_g'], 'm_sgu_ln_b': out['m_sgu_ln_b'], 'm_sgu_w': out['m_sgu_w'], 'm_sgu_b': out['m_sgu_b'], 'm_ssm_a_re': out['m_ssm_a_re'], 'm_ssm_a_im': out['m_ssm_a_im'], 'm_ssm_log_dt': out['m_ssm_log_dt'], 'm_ssm_b_re': out['m_ssm_b_re'], 'm_ssm_b_im': out['m_ssm_b_im'], 'm_ssm_c_re': out['m_ssm_c_re'], 'm_ssm_c_im': out['m_ssm_c_im'], 'm_ssm_d': out['m_ssm_d'], 'm_ssm_glu_w': out['m_ssm_glu_w'], 'm_ssm_glu_b': out['m_ssm_glu_b'], 'm_branch_norm_g': out['m_branch_norm_g'], 'm_w_out': out['m_w_out'], 'm_norm_ffn_g': out['m_norm_ffn_g'], 'm_ffn_w_up': out['m_ffn_w_up'], 'm_ffn_conv_w': out['m_ffn_conv_w'], 'm_ffn_conv_b': out['m_ffn_conv_b'], 'm_ffn_w_down': out['m_ffn_w_down'], 'm_norm_ple_g': out['m_norm_ple_g'], 'm_ple_w_gate': out['m_ple_w_gate'], 'm_ple_w_proj': out['m_ple_w_proj'], 'm_final_norm_g': out['m_final_norm_g'], 'v_rel_bias': out['v_rel_bias'], 'v_norm_attn_g': out['v_norm_attn_g'], 'v_w_in': out['v_w_in'], 'v_sgu_ln_g': out['v_sgu_ln_g'], 'v_sgu_ln_b': out['v_sgu_ln_b'], 'v_sgu_w': out['v_sgu_w'], 'v_sgu_b': out['v_sgu_b'], 'v_ssm_a_re': out['v_ssm_a_re'], 'v_ssm_a_im': out['v_ssm_a_im'], 'v_ssm_log_dt': out['v_ssm_log_dt'], 'v_ssm_b_re': out['v_ssm_b_re'], 'v_ssm_b_im': out['v_ssm_b_im'], 'v_ssm_c_re': out['v_ssm_c_re'], 'v_ssm_c_im': out['v_ssm_c_im'], 'v_ssm_d': out['v_ssm_d'], 'v_ssm_glu_w': out['v_ssm_glu_w'], 'v_ssm_glu_b': out['v_ssm_glu_b'], 'v_branch_norm_g': out['v_branch_norm_g'], 'v_w_out': out['v_w_out'], 'v_norm_ffn_g': out['v_norm_ffn_g'], 'v_ffn_w_up': out['v_ffn_w_up'], 'v_ffn_conv_w': out['v_ffn_conv_w'], 'v_ffn_conv_b': out['v_ffn_conv_b'], 'v_ffn_w_down': out['v_ffn_w_down'], 'v_norm_ple_g': out['v_norm_ple_g'], 'v_ple_w_gate': out['v_ple_w_gate'], 'v_ple_w_proj': out['v_ple_w_proj'], 'v_final_norm_g': out['v_final_norm_g']}


def _loss(weights, diff, rest, loss_target):
    with _jax.named_scope("forward"):
        args = {**rest, TWIN_DIFF_INPUT: diff, **{k: w.astype(_WEIGHT_DTYPES[k]) for k, w in weights.items()}}
        y = _forward(args)
    with _jax.named_scope("loss_head"):
        err = _jnp.square(y.astype(_jnp.float32) - loss_target)
        return 0.5 * _jnp.sum(_jnp.mean(err, axis=-1)) if err.ndim else 0.5 * err


def _adamw(w, g, m, v):
    m = ADAM_B1 * m + (1.0 - ADAM_B1) * g
    v = ADAM_B2 * v + (1.0 - ADAM_B2) * _jnp.square(g)
    m_hat = m / (1.0 - ADAM_B1 ** ADAM_STEP)
    v_hat = v / (1.0 - ADAM_B2 ** ADAM_STEP)
    delta = -ADAM_LR * (m_hat / (_jnp.sqrt(v_hat) + ADAM_EPS) + ADAM_WD * w)
    return delta, m, v


def reference(x, p, rel_bias, norm_attn_g, w_in, sgu_ln_g, sgu_ln_b, sgu_w, sgu_b, ssm_a_re, ssm_a_im, ssm_log_dt, ssm_b_re, ssm_b_im, ssm_c_re, ssm_c_im, ssm_d, ssm_glu_w, ssm_glu_b, branch_norm_g, w_out, norm_ffn_g, ffn_w_up, ffn_conv_w, ffn_conv_b, ffn_w_down, norm_ple_g, ple_w_gate, ple_w_proj, final_norm_g, loss_target, m_rel_bias, m_norm_attn_g, m_w_in, m_sgu_ln_g, m_sgu_ln_b, m_sgu_w, m_sgu_b, m_ssm_a_re, m_ssm_a_im, m_ssm_log_dt, m_ssm_b_re, m_ssm_b_im, m_ssm_c_re, m_ssm_c_im, m_ssm_d, m_ssm_glu_w, m_ssm_glu_b, m_branch_norm_g, m_w_out, m_norm_ffn_g, m_ffn_w_up, m_ffn_conv_w, m_ffn_conv_b, m_ffn_w_down, m_norm_ple_g, m_ple_w_gate, m_ple_w_proj, m_final_norm_g, v_rel_bias, v_norm_attn_g, v_w_in, v_sgu_ln_g, v_sgu_ln_b, v_sgu_w, v_sgu_b, v_ssm_a_re, v_ssm_a_im, v_ssm_log_dt, v_ssm_b_re, v_ssm_b_im, v_ssm_c_re, v_ssm_c_im, v_ssm_d, v_ssm_glu_w, v_ssm_glu_b, v_branch_norm_g, v_w_out, v_norm_ffn_g, v_ffn_w_up, v_ffn_conv_w, v_ffn_conv_b, v_ffn_w_down, v_norm_ple_g, v_ple_w_gate, v_ple_w_proj, v_final_norm_g):
    given = dict(x=x, p=p, rel_bias=rel_bias, norm_attn_g=norm_attn_g, w_in=w_in, sgu_ln_g=sgu_ln_g, sgu_ln_b=sgu_ln_b, sgu_w=sgu_w, sgu_b=sgu_b, ssm_a_re=ssm_a_re, ssm_a_im=ssm_a_im, ssm_log_dt=ssm_log_dt, ssm_b_re=ssm_b_re, ssm_b_im=ssm_b_im, ssm_c_re=ssm_c_re, ssm_c_im=ssm_c_im, ssm_d=ssm_d, ssm_glu_w=ssm_glu_w, ssm_glu_b=ssm_glu_b, branch_norm_g=branch_norm_g, w_out=w_out, norm_ffn_g=norm_ffn_g, ffn_w_up=ffn_w_up, ffn_conv_w=ffn_conv_w, ffn_conv_b=ffn_conv_b, ffn_w_down=ffn_w_down, norm_ple_g=norm_ple_g, ple_w_gate=ple_w_gate, ple_w_proj=ple_w_proj, final_norm_g=final_norm_g, loss_target=loss_target, m_rel_bias=m_rel_bias, m_norm_attn_g=m_norm_attn_g, m_w_in=m_w_in, m_sgu_ln_g=m_sgu_ln_g, m_sgu_ln_b=m_sgu_ln_b, m_sgu_w=m_sgu_w, m_sgu_b=m_sgu_b, m_ssm_a_re=m_ssm_a_re, m_ssm_a_im=m_ssm_a_im, m_ssm_log_dt=m_ssm_log_dt, m_ssm_b_re=m_ssm_b_re, m_ssm_b_im=m_ssm_b_im, m_ssm_c_re=m_ssm_c_re, m_ssm_c_im=m_ssm_c_im, m_ssm_d=m_ssm_d, m_ssm_glu_w=m_ssm_glu_w, m_ssm_glu_b=m_ssm_glu_b, m_branch_norm_g=m_branch_norm_g, m_w_out=m_w_out, m_norm_ffn_g=m_norm_ffn_g, m_ffn_w_up=m_ffn_w_up, m_ffn_conv_w=m_ffn_conv_w, m_ffn_conv_b=m_ffn_conv_b, m_ffn_w_down=m_ffn_w_down, m_norm_ple_g=m_norm_ple_g, m_ple_w_gate=m_ple_w_gate, m_ple_w_proj=m_ple_w_proj, m_final_norm_g=m_final_norm_g, v_rel_bias=v_rel_bias, v_norm_attn_g=v_norm_attn_g, v_w_in=v_w_in, v_sgu_ln_g=v_sgu_ln_g, v_sgu_ln_b=v_sgu_ln_b, v_sgu_w=v_sgu_w, v_sgu_b=v_sgu_b, v_ssm_a_re=v_ssm_a_re, v_ssm_a_im=v_ssm_a_im, v_ssm_log_dt=v_ssm_log_dt, v_ssm_b_re=v_ssm_b_re, v_ssm_b_im=v_ssm_b_im, v_ssm_c_re=v_ssm_c_re, v_ssm_c_im=v_ssm_c_im, v_ssm_d=v_ssm_d, v_ssm_glu_w=v_ssm_glu_w, v_ssm_glu_b=v_ssm_glu_b, v_branch_norm_g=v_branch_norm_g, v_w_out=v_w_out, v_norm_ffn_g=v_norm_ffn_g, v_ffn_w_up=v_ffn_w_up, v_ffn_conv_w=v_ffn_conv_w, v_ffn_conv_b=v_ffn_conv_b, v_ffn_w_down=v_ffn_w_down, v_norm_ple_g=v_norm_ple_g, v_ple_w_gate=v_ple_w_gate, v_ple_w_proj=v_ple_w_proj, v_final_norm_g=v_final_norm_g)
    weights = {n: given[n] for n in TWIN_WEIGHTS}
    shared = {n: given[n] for n in SHARED_INPUTS}
    per_example = {n: given[n] for n in ['x', 'p']}
    grad_fn = _jax.value_and_grad(_loss, argnums=(0, 1))

    def one_microbatch(ex, loss_target):
        ex = dict(ex)
        diff = ex.pop(TWIN_DIFF_INPUT)
        return grad_fn(weights, diff, {**shared, **ex}, loss_target)

    if N_MICROBATCH == 1:
        loss, (grad_w, grad_x) = one_microbatch(per_example, given["loss_target"])
    else:
        def body(carry, xs):
            loss_sum, grad_sum = carry
            l_k, (gw_k, gx_k) = one_microbatch(xs[0], xs[1])
            with _jax.named_scope("update"):
                return (loss_sum + l_k, _jax.tree.map(_jnp.add, grad_sum, gw_k)), gx_k

        init = (_jnp.zeros((), _jnp.float32), _jax.tree.map(_jnp.zeros_like, weights))
        (loss, grad_w), grad_x = _jax.lax.scan(body, init, (per_example, given["loss_target"]))
    with _jax.named_scope("update"):
        delta_w, new_m, new_v = {}, {}, {}
        for n in TWIN_WEIGHTS:
            delta_w[n], new_m[n], new_v[n] = _adamw(weights[n], grad_w[n], given["m_" + n], given["v_" + n])
    return (loss, grad_x, *[grad_w[n] for n in TWIN_WEIGHTS], *[delta_w[n] for n in TWIN_WEIGHTS],
            *[new_m[n] for n in TWIN_WEIGHTS], *[new_v[n] for n in TWIN_WEIGHTS])
```

```python
import functools
import math

import numpy as np
import jax
import jax.numpy as jnp
from jax import lax
from jax.experimental import pallas as pl
from jax.experimental.pallas import tpu as pltpu

F32 = jnp.float32
MXU_DTYPE = jnp.bfloat16
VMEM_LIMIT_BYTES = 52 * 1024 * 1024

D_MODEL = 1024
DEPTH = 2
PLE_DIM = 256
HEAD_DIM = 64
N_HEADS = 8
ATTN_W = 512
QBLK = 128
BRANCH_DIL = (1, 4, 16)
N_BUCKETS = 32
REL_MAX_DIST = 2048
SGU_W = 256
SGU_G = 4
SGU_GW = 64
SGU_CHUNK = 128
SSM_W = 256
SSM_G = 16
SSM_C = 16
SSM_N = 64
NSTATE = SSM_G * SSM_N
D_FF = 2816
EPS = 1e-6
NEG_INF = -1e30
ATTN_SCALE = HEAD_DIM ** -0.5

ADAM_LR = 0.001
ADAM_B1 = 0.9
ADAM_B2 = 0.999
ADAM_EPS = 1e-08
ADAM_WD = 0.01
ADAM_STEP = 10

SSM_NSEG = 8
SSM_TSEG = 64
SSM_TB = SSM_NSEG * SSM_TSEG
SSM_LANE_CHUNK = 512

MESH_AXES = ("x", "y", "c")
N_CHIPS = 4
N_DEV = 8

BIG_NAMES = ("w_in", "ssm_glu_w", "w_out", "ffn_w_up", "ffn_conv_w", "ffn_w_down", "ple_w_gate", "ple_w_proj")
BIG_FULL = {
    "w_in": ((D_MODEL, 2304), 2),
    "ssm_glu_w": ((SSM_W, SSM_W), 1),
    "w_out": ((D_MODEL, D_MODEL), 1),
    "ffn_w_up": ((D_MODEL, 2 * D_FF), 2),
    "ffn_conv_w": ((3, 2 * D_FF), 2),
    "ffn_w_down": ((D_FF, D_MODEL), 1),
    "ple_w_gate": ((D_MODEL, D_MODEL), 1),
    "ple_w_proj": ((PLE_DIM, D_MODEL), 2),
}
PACK_COLS = 1024
PACK_ROWS = 6656
PACK_HALF = PACK_ROWS // 2

SMALL_NAMES = ("rel_bias", "norm_attn_g", "sgu_ln_g", "sgu_ln_b", "sgu_w", "sgu_b", "ssm_a_re", "ssm_a_im",
               "ssm_log_dt", "ssm_b_re", "ssm_b_im", "ssm_c_re", "ssm_c_im", "ssm_d", "ssm_glu_b",
               "branch_norm_g", "norm_ffn_g", "ffn_conv_b", "norm_ple_g", "final_norm_g")
SMALL_ROWS = 288

WEIGHT_NAMES = ("rel_bias", "norm_attn_g", "w_in", "sgu_ln_g", "sgu_ln_b", "sgu_w", "sgu_b", "ssm_a_re", "ssm_a_im",
                "ssm_log_dt", "ssm_b_re", "ssm_b_im", "ssm_c_re", "ssm_c_im", "ssm_d", "ssm_glu_w", "ssm_glu_b",
                "branch_norm_g", "w_out", "norm_ffn_g", "ffn_w_up", "ffn_conv_w", "ffn_conv_b", "ffn_w_down",
                "norm_ple_g", "ple_w_gate", "ple_w_proj", "final_norm_g")


def _params(sem):
    return pltpu.CompilerParams(dimension_semantics=sem, vmem_limit_bytes=VMEM_LIMIT_BYTES)


def _tile(n, cap, mult=128):
    if n <= cap:
        return n
    best = None
    for t in range(mult, cap + 1, mult):
        if n % t == 0:
            best = t
    assert best is not None, (n, cap)
    return best


def _gelu(x):
    return 0.5 * x * (1.0 + jnp.tanh(0.7978845608028654 * (x + 0.044715 * x * x * x)))


def _gelu_grad(x):
    t = jnp.tanh(0.7978845608028654 * (x + 0.044715 * x * x * x))
    return 0.5 * (1.0 + t) + 0.5 * x * (1.0 - t * t) * 0.7978845608028654 * (1.0 + 3.0 * 0.044715 * x * x)


def _dot(a, b, dims):
    return lax.dot_general(a, b, (dims, ((), ())), preferred_element_type=F32)


def _dotf(a, b, dims):
    return lax.dot_general(a, b, (dims, ((), ())), preferred_element_type=F32, precision=lax.Precision.HIGHEST)


NN = ((1,), (0,))
NT = ((1,), (1,))
TN = ((0,), (0,))


def _matmul(a, b, *, name, out_dtype, tm, tn, trans_b=False, residual=None):
    m, k = a.shape
    n = b.shape[0] if trans_b else b.shape[1]
    tm = _tile(m, tm, 8)
    tn = _tile(n, tn)
    dims = NT if trans_b else NN

    def body(*refs):
        if residual is None:
            a_ref, b_ref, o_ref = refs
        else:
            a_ref, b_ref, r_ref, o_ref = refs
        acc = _dot(a_ref[...].astype(MXU_DTYPE), b_ref[...].astype(MXU_DTYPE), dims)
        if residual is not None:
            acc = acc + r_ref[...]
        o_ref[...] = acc.astype(o_ref.dtype)

    b_spec = (pl.BlockSpec((tn, k), lambda i, j: (j, 0)) if trans_b
              else pl.BlockSpec((k, tn), lambda i, j: (0, j)))
    in_specs = [pl.BlockSpec((tm, k), lambda i, j: (i, 0)), b_spec]
    args = [a, b]
    if residual is not None:
        in_specs.append(pl.BlockSpec((tm, tn), lambda i, j: (i, j)))
        args.append(residual)
    return pl.pallas_call(
        body, name=name, grid=(m // tm, n // tn), in_specs=in_specs,
        out_specs=pl.BlockSpec((tm, tn), lambda i, j: (i, j)),
        out_shape=jax.ShapeDtypeStruct((m, n), out_dtype),
        compiler_params=_params(("parallel", "parallel")),
    )(*args)


def _matmul_tn(a, g, *, name, tk, tn, tm=512):
    m, k = a.shape
    n = g.shape[1]
    tk = _tile(k, tk)
    tn = _tile(n, tn)
    tm = _tile(m, tm, 8)

    def body(a_ref, g_ref, o_ref):
        @pl.when(pl.program_id(2) == 0)
        def _():
            o_ref[...] = jnp.zeros_like(o_ref)

        o_ref[...] += _dot(a_ref[...].astype(MXU_DTYPE), g_ref[...].astype(MXU_DTYPE), TN)

    return pl.pallas_call(
        body, name=name, grid=(k // tk, n // tn, m // tm),
        in_specs=[pl.BlockSpec((tm, tk), lambda i, j, s: (s, i)),
                  pl.BlockSpec((tm, tn), lambda i, j, s: (s, j))],
        out_specs=pl.BlockSpec((tk, tn), lambda i, j, s: (i, j)),
        out_shape=jax.ShapeDtypeStruct((k, n), F32),
        compiler_params=_params(("parallel", "parallel", "arbitrary")),
    )(a, g)


ROWS = 512


def _rms_fwd(h, g, *, name):
    s, d = h.shape

    def body(h_ref, g_ref, o_ref):
        x = h_ref[...]
        r = lax.rsqrt(jnp.mean(x * x, axis=-1, keepdims=True) + EPS)
        o_ref[...] = (x * r * g_ref[...]).astype(o_ref.dtype)

    return pl.pallas_call(
        body, name=name, grid=(s // ROWS,),
        in_specs=[pl.BlockSpec((ROWS, d), lambda i: (i, 0)), pl.BlockSpec((1, d), lambda i: (0, 0))],
        out_specs=pl.BlockSpec((ROWS, d), lambda i: (i, 0)),
        out_shape=jax.ShapeDtypeStruct((s, d), MXU_DTYPE),
        compiler_params=_params(("parallel",)),
    )(h, g.reshape(1, d))


def _rms_bwd(h, g, dxn, dres, *, name):
    s, d = h.shape

    def body(h_ref, g_ref, dxn_ref, dres_ref, dh_ref, dg_ref):
        @pl.when(pl.program_id(0) == 0)
        def _():
            dg_ref[...] = jnp.zeros_like(dg_ref)

        x = h_ref[...]
        r = lax.rsqrt(jnp.mean(x * x, axis=-1, keepdims=True) + EPS)
        xhat = x * r
        dxn = dxn_ref[...].astype(F32)
        dg_ref[...] += jnp.sum(dxn * xhat, axis=0, keepdims=True)
        dxh = dxn * g_ref[...]
        dh_ref[...] = dres_ref[...] + r * (dxh - xhat * jnp.mean(dxh * xhat, axis=-1, keepdims=True))

    row = pl.BlockSpec((ROWS, d), lambda i: (i, 0))
    vec = pl.BlockSpec((1, d), lambda i: (0, 0))
    return pl.pallas_call(
        body, name=name, grid=(s // ROWS,), in_specs=[row, vec, row, row], out_specs=[row, vec],
        out_shape=[jax.ShapeDtypeStruct((s, d), F32), jax.ShapeDtypeStruct((1, d), F32)],
        compiler_params=_params(("arbitrary",)),
    )(h, g.reshape(1, d), dxn, dres)


def _loss_head(h, g, target):
    s, d = h.shape

    def body(h_ref, g_ref, t_ref, loss_ref, dh_ref, dg_ref):
        @pl.when(pl.program_id(0) == 0)
        def _():
            loss_ref[...] = jnp.zeros_like(loss_ref)
            dg_ref[...] = jnp.zeros_like(dg_ref)

        x = h_ref[...]
        r = lax.rsqrt(jnp.mean(x * x, axis=-1, keepdims=True) + EPS)
        xhat = x * r
        err = xhat * g_ref[...] - t_ref[...]
        loss_ref[...] += 0.5 * jnp.sum(jnp.mean(err * err, axis=-1, keepdims=True), axis=0, keepdims=True)
        dy = err / d
        dg_ref[...] += jnp.sum(dy * xhat, axis=0, keepdims=True)
        dxh = dy * g_ref[...]
        dh_ref[...] = r * (dxh - xhat * jnp.mean(dxh * xhat, axis=-1, keepdims=True))

    row = pl.BlockSpec((ROWS, d), lambda i: (i, 0))
    vec = pl.BlockSpec((1, d), lambda i: (0, 0))
    one = pl.BlockSpec((1, 1), lambda i: (0, 0))
    return pl.pallas_call(
        body, name="loss_head", grid=(s // ROWS,), in_specs=[row, vec, row], out_specs=[one, row, vec],
        out_shape=[jax.ShapeDtypeStruct((1, 1), F32), jax.ShapeDtypeStruct((s, d), F32),
                   jax.ShapeDtypeStruct((1, d), F32)],
        compiler_params=_params(("arbitrary",)),
    )(h, g.reshape(1, d), target)


def _t5_bucket(dist):
    max_exact = N_BUCKETS // 2
    dd = np.maximum(dist, 0)
    large = max_exact + (np.log(np.maximum(dd, 1) / max_exact) / np.log(REL_MAX_DIST / max_exact)
                         * (N_BUCKETS - max_exact)).astype(np.int32)
    large = np.minimum(large, N_BUCKETS - 1)
    return np.where(dd < max_exact, dd, large).astype(np.int32)


def _bucket_table():
    qq = np.arange(QBLK)[:, None]
    kk = np.arange(QBLK)[None, :]
    out = np.zeros((len(BRANCH_DIL), 2, QBLK, QBLK), np.int32)
    for b, dil in enumerate(BRANCH_DIL):
        out[b, 0] = _t5_bucket((qq - kk + QBLK) * dil)
        out[b, 1] = _t5_bucket((qq - kk) * dil)
    return out


def _bias_build(rel_bias):
    idx = jnp.asarray(_bucket_table())

    def body(idx_ref, rb_ref, o_ref):
        ids = idx_ref[0, 0]
        for h in range(N_HEADS):
            acc = jnp.zeros((QBLK, QBLK), F32)
            for b in range(N_BUCKETS):
                acc = jnp.where(ids == b, rb_ref[b, h], acc)
            o_ref[0, 0, h] = acc

    return pl.pallas_call(
        body, name="attn_bias_build", grid=(len(BRANCH_DIL), 2),
        in_specs=[pl.BlockSpec((1, 1, QBLK, QBLK), lambda b, p: (b, p, 0, 0)),
                  pl.BlockSpec(memory_space=pltpu.SMEM)],
        out_specs=pl.BlockSpec((1, 1, N_HEADS, QBLK, QBLK), lambda b, p: (b, p, 0, 0, 0)),
        out_shape=jax.ShapeDtypeStruct((len(BRANCH_DIL), 2, N_HEADS, QBLK, QBLK), F32),
        compiler_params=_params(("parallel", "parallel")),
    )(idx, rel_bias)


def _bias_reduce(dbias):
    idx = jnp.asarray(_bucket_table())
    nb = len(BRANCH_DIL)

    def body(idx_ref, d_ref, o_ref):
        def per_bucket(b, carry):
            for h in range(N_HEADS):
                tot = jnp.zeros((), F32)
                for br in range(nb):
                    for p in range(2):
                        tot = tot + jnp.sum(jnp.where(idx_ref[br, p] == b, d_ref[br, p, h], 0.0))
                o_ref[b, h] = tot
            return carry

        lax.fori_loop(0, N_BUCKETS, per_bucket, 0)

    return pl.pallas_call(
        body, name="attn_bias_reduce",
        in_specs=[pl.BlockSpec(memory_space=pltpu.VMEM), pl.BlockSpec(memory_space=pltpu.VMEM)],
        out_specs=pl.BlockSpec(memory_space=pltpu.SMEM),
        out_shape=jax.ShapeDtypeStruct((N_BUCKETS, N_HEADS), F32),
        compiler_params=pltpu.CompilerParams(vmem_limit_bytes=VMEM_LIMIT_BYTES),
    )(idx, dbias)


def _band_masks(c):
    row = lax.broadcasted_iota(jnp.int32, (QBLK, QBLK), 0)
    col = lax.broadcasted_iota(jnp.int32, (QBLK, QBLK), 1)
    mask_cur = col <= row
    mask_prev = jnp.logical_and(col >= row, c > 0)
    return mask_prev, mask_cur


def _attn_specs(dil):
    blk = (QBLK, ATTN_W)
    q = pl.BlockSpec(blk, lambda r, c: (c, 3 * r))
    kp = pl.BlockSpec(blk, lambda r, c: (jnp.maximum(c - 1, 0), 3 * r + 1))
    kc = pl.BlockSpec(blk, lambda r, c: (c, 3 * r + 1))
    vp = pl.BlockSpec(blk, lambda r, c: (jnp.maximum(c - 1, 0), 3 * r + 2))
    vc = pl.BlockSpec(blk, lambda r, c: (c, 3 * r + 2))
    return [q, kp, kc, vp, vc]


def _attn_fwd_branch(qkv, bias, state, *, branch, last):
    dil = BRANCH_DIL[branch]
    s = qkv.shape[0]
    n = s // dil
    nblk = n // QBLK
    first = state is None

    def body(*refs):
        q_ref, kp_ref, kc_ref, vp_ref, vc_ref, b_ref = refs[:6]
        if first:
            outs = refs[6:]
        else:
            acc_ref, m_ref, l_ref = refs[6:9]
            outs = refs[9:]
        mask_prev, mask_cur = _band_masks(pl.program_id(1))
        for h in range(N_HEADS):
            sl = slice(HEAD_DIM * h, HEAD_DIM * (h + 1))
            qh = q_ref[:, sl]
            s_c = _dot(qh, kc_ref[:, sl], NT) * ATTN_SCALE + b_ref[0, 1, h]
            s_p = _dot(qh, kp_ref[:, sl], NT) * ATTN_SCALE + b_ref[0, 0, h]
            s_c = jnp.where(mask_cur, s_c, NEG_INF)
            s_p = jnp.where(mask_prev, s_p, NEG_INF)
            m_blk = jnp.maximum(jnp.max(s_c, axis=-1, keepdims=True), jnp.max(s_p, axis=-1, keepdims=True))
            if first:
                m_new = m_blk
            else:
                m_old = m_ref[:, sl][:, :1]
                m_new = jnp.maximum(m_old, m_blk)
            p_c = jnp.exp(s_c - m_new)
            p_p = jnp.exp(s_p - m_new)
            l_new = jnp.sum(p_c, axis=-1, keepdims=True) + jnp.sum(p_p, axis=-1, keepdims=True)
            acc = (_dot(p_c.astype(MXU_DTYPE), vc_ref[:, sl], NN)
                   + _dot(p_p.astype(MXU_DTYPE), vp_ref[:, sl], NN))
            if not first:
                alpha = jnp.exp(m_old - m_new)
                l_new = l_new + alpha * l_ref[:, sl][:, :1]
                acc = acc + alpha * acc_ref[:, sl]
            if last:
                outs[0][:, sl] = acc / l_new
                outs[1][:, sl] = jnp.broadcast_to(m_new + jnp.log(l_new), (QBLK, HEAD_DIM))
            else:
                outs[0][:, sl] = acc
                outs[1][:, sl] = jnp.broadcast_to(m_new, (QBLK, HEAD_DIM))
                outs[2][:, sl] = jnp.broadcast_to(l_new, (QBLK, HEAD_DIM))

    st_spec = pl.BlockSpec((QBLK, ATTN_W), lambda r, c: (c, r))
    in_specs = _attn_specs(dil) + [pl.BlockSpec((1, 2, N_HEADS, QBLK, QBLK), lambda r, c: (branch, 0, 0, 0, 0))]
    qv = qkv.reshape(n, dil * 3 * ATTN_W)
    args = [qv] * 5 + [bias]
    if not first:
        in_specs += [st_spec] * 3
        args += [t.reshape(n, dil * ATTN_W) for t in state]
    n_out = 2 if last else 3
    outs = pl.pallas_call(
        body, name=f"attn_fwd_b{branch}", grid=(dil, nblk), in_specs=in_specs,
        out_specs=[st_spec] * n_out,
        out_shape=[jax.ShapeDtypeStruct((n, dil * ATTN_W), F32)] * n_out,
        compiler_params=_params(("parallel", "parallel")),
    )(*args)
    return tuple(t.reshape(s, ATTN_W) for t in outs)


def _attn_fwd(qkv, bias):
    state = None
    for b in range(len(BRANCH_DIL)):
        state = _attn_fwd_branch(qkv, bias, state, branch=b, last=(b == len(BRANCH_DIL) - 1))
    return state


def _attn_bwd_branch(qkv, bias, o, lse, do, *, branch):
    dil = BRANCH_DIL[branch]
    s = qkv.shape[0]
    n = s // dil
    nblk = n // QBLK

    def body(q_ref, kp_ref, kc_ref, vp_ref, vc_ref, b_ref, o_ref, l_ref, do_ref,
             dq_ref, dka_ref, dkb_ref, dva_ref, dvb_ref, db_ref):
        @pl.when(jnp.logical_and(pl.program_id(0) == 0, pl.program_id(1) == 0))
        def _():
            db_ref[...] = jnp.zeros_like(db_ref)

        mask_prev, mask_cur = _band_masks(pl.program_id(1))
        for h in range(N_HEADS):
            sl = slice(HEAD_DIM * h, HEAD_DIM * (h + 1))
            qh = q_ref[:, sl]
            doh = do_ref[:, sl]
            lh = l_ref[:, sl][:, :1]
            delta = jnp.sum(doh * o_ref[:, sl], axis=-1, keepdims=True)
            do_m = doh.astype(MXU_DTYPE)
            s_c = _dot(qh, kc_ref[:, sl], NT) * ATTN_SCALE + b_ref[0, 1, h]
            s_p = _dot(qh, kp_ref[:, sl], NT) * ATTN_SCALE + b_ref[0, 0, h]
            p_c = jnp.exp(jnp.where(mask_cur, s_c, NEG_INF) - lh)
            p_p = jnp.exp(jnp.where(mask_prev, s_p, NEG_INF) - lh)
            ds_c = p_c * (_dot(do_m, vc_ref[:, sl], NT) - delta)
            ds_p = p_p * (_dot(do_m, vp_ref[:, sl], NT) - delta)
            db_ref[0, 1, h] += ds_c
            db_ref[0, 0, h] += ds_p
            ds_c_m = ds_c.astype(MXU_DTYPE)
            ds_p_m = ds_p.astype(MXU_DTYPE)
            dq = _dot(ds_c_m, kc_ref[:, sl], NN) + _dot(ds_p_m, kp_ref[:, sl], NN)
            dq_ref[:, sl] = (dq * ATTN_SCALE).astype(dq_ref.dtype)
            dka_ref[:, sl] = (_dot(ds_c_m, qh, TN) * ATTN_SCALE).astype(dka_ref.dtype)
            dkb_ref[:, sl] = (_dot(ds_p_m, qh, TN) * ATTN_SCALE).astype(dkb_ref.dtype)
            dva_ref[:, sl] = _dot(p_c.astype(MXU_DTYPE), do_m, TN).astype(dva_ref.dtype)
            dvb_ref[:, sl] = _dot(p_p.astype(MXU_DTYPE), do_m, TN).astype(dvb_ref.dtype)

    st_spec = pl.BlockSpec((QBLK, ATTN_W), lambda r, c: (c, r))
    b_in = pl.BlockSpec((1, 2, N_HEADS, QBLK, QBLK), lambda r, c: (branch, 0, 0, 0, 0))
    b_out = pl.BlockSpec((1, 2, N_HEADS, QBLK, QBLK), lambda r, c: (0, 0, 0, 0, 0))
    qv = qkv.reshape(n, dil * 3 * ATTN_W)
    view = lambda t: t.reshape(n, dil * ATTN_W)
    outs = pl.pallas_call(
        body, name=f"attn_bwd_b{branch}", grid=(dil, nblk),
        in_specs=_attn_specs(dil) + [b_in, st_spec, st_spec, st_spec],
        out_specs=[st_spec] * 5 + [b_out],
        out_shape=[jax.ShapeDtypeStruct((n, dil * ATTN_W), MXU_DTYPE)] * 5
        + [jax.ShapeDtypeStruct((1, 2, N_HEADS, QBLK, QBLK), F32)],
        compiler_params=_params(("arbitrary", "arbitrary")),
    )(qv, qv, qv, qv, qv, bias, view(o), view(lse), view(do))
    return tuple(t.reshape(s, ATTN_W) for t in outs[:5]) + (outs[5],)


def _attn_bwd(qkv, bias, o, lse, do):
    s = qkv.shape[0]
    nb = s // QBLK
    parts = [_attn_bwd_branch(qkv, bias, o, lse, do, branch=b) for b in range(len(BRANCH_DIL))]
    dbias = jnp.concatenate([p[5] for p in parts], axis=0)

    def body(*refs):
        o_ref = refs[-1]
        i = pl.program_id(0)
        dq = jnp.zeros((QBLK, ATTN_W), F32)
        dk = jnp.zeros((QBLK, ATTN_W), F32)
        dv = jnp.zeros((QBLK, ATTN_W), F32)
        for b, dil in enumerate(BRANCH_DIL):
            dq_ref, dka_ref, dkb_ref, dva_ref, dvb_ref = refs[5 * b:5 * b + 5]
            inside = i + dil < nb
            dq = dq + dq_ref[...].astype(F32)
            dk = dk + dka_ref[...].astype(F32) + jnp.where(inside, dkb_ref[...].astype(F32), 0.0)
            dv = dv + dva_ref[...].astype(F32) + jnp.where(inside, dvb_ref[...].astype(F32), 0.0)
        o_ref[:, 0:ATTN_W] = dq.astype(o_ref.dtype)
        o_ref[:, ATTN_W:2 * ATTN_W] = dk.astype(o_ref.dtype)
        o_ref[:, 2 * ATTN_W:3 * ATTN_W] = dv.astype(o_ref.dtype)

    in_specs, args = [], []
    for b, dil in enumerate(BRANCH_DIL):
        here = pl.BlockSpec((QBLK, ATTN_W), lambda i: (i, 0))
        ahead = pl.BlockSpec((QBLK, ATTN_W), functools.partial(lambda i, d: (jnp.minimum(i + d, nb - 1), 0), d=dil))
        in_specs += [here, here, ahead, here, ahead]
        args += list(parts[b][:5])
    dqkv = pl.pallas_call(
        body, name="attn_bwd_sum", grid=(nb,), in_specs=in_specs,
        out_specs=pl.BlockSpec((QBLK, 3 * ATTN_W), lambda i: (i, 0)),
        out_shape=jax.ShapeDtypeStruct((s, 3 * ATTN_W), MXU_DTYPE),
        compiler_params=_params(("parallel",)),
    )(*args)
    return dqkv, dbias


SGU_ROWS = 512


def _sgu_norm(v_g):
    mu = jnp.mean(v_g, axis=-1, keepdims=True)
    cen = v_g - mu
    var = jnp.mean(cen * cen, axis=-1, keepdims=True)
    rstd = lax.rsqrt(var + EPS)
    return cen * rstd, rstd


def _sgu_fwd(zs, ln_g, ln_b, w_mask, b_t):
    s = zs.shape[0]
    nch = SGU_ROWS // SGU_CHUNK

    def body(z_ref, g_ref, b_ref, w_ref, bt_ref, o_ref):
        gz = _gelu(z_ref[...])
        for g in range(SGU_G):
            sl = slice(SGU_GW * g, SGU_GW * (g + 1))
            u_g = gz[:, sl]
            xhat, _ = _sgu_norm(gz[:, SGU_W + SGU_GW * g:SGU_W + SGU_GW * (g + 1)])
            vn = (xhat * g_ref[:, sl] + b_ref[:, sl]).astype(MXU_DTYPE)
            wg = w_ref[g].astype(MXU_DTYPE)
            for ci in range(nch):
                rs = slice(SGU_CHUNK * ci, SGU_CHUNK * (ci + 1))
                mixed = _dot(wg, vn[rs], NN) + bt_ref[:, g:g + 1]
                o_ref[rs, sl] = u_g[rs] * mixed

    full = lambda shape: pl.BlockSpec(shape, lambda i: tuple(0 for _ in shape))
    return pl.pallas_call(
        body, name="sgu_fwd", grid=(s // SGU_ROWS,),
        in_specs=[pl.BlockSpec((SGU_ROWS, 2 * SGU_W), lambda i: (i, 0)), full((1, SGU_W)), full((1, SGU_W)),
                  full((SGU_G, SGU_CHUNK, SGU_CHUNK)), full((SGU_CHUNK, SGU_G))],
        out_specs=pl.BlockSpec((SGU_ROWS, SGU_W), lambda i: (i, 0)),
        out_shape=jax.ShapeDtypeStruct((s, SGU_W), F32),
        compiler_params=_params(("parallel",)),
    )(zs, ln_g.reshape(1, SGU_W), ln_b.reshape(1, SGU_W), w_mask, b_t)


def _sgu_bwd(zs, ln_g, ln_b, w_mask, b_t, dy):
    s = zs.shape[0]
    nch = SGU_ROWS // SGU_CHUNK

    def body(z_ref, g_ref, b_ref, w_ref, bt_ref, dy_ref, dz_ref, dg_ref, dbb_ref, dw_ref, dbt_ref):
        @pl.when(pl.program_id(0) == 0)
        def _():
            dg_ref[...] = jnp.zeros_like(dg_ref)
            dbb_ref[...] = jnp.zeros_like(dbb_ref)
            dw_ref[...] = jnp.zeros_like(dw_ref)
            dbt_ref[...] = jnp.zeros_like(dbt_ref)

        z = z_ref[...]
        gz = _gelu(z)
        dgelu = _gelu_grad(z)
        dy = dy_ref[...]
        for g in range(SGU_G):
            sl = slice(SGU_GW * g, SGU_GW * (g + 1))
            sv = slice(SGU_W + SGU_GW * g, SGU_W + SGU_GW * (g + 1))
            u_g = gz[:, sl]
            xhat, rstd = _sgu_norm(gz[:, sv])
            gain = g_ref[:, sl]
            vn = (xhat * gain + b_ref[:, sl]).astype(MXU_DTYPE)
            wg = w_ref[g].astype(MXU_DTYPE)
            dy_g = dy[:, sl]
            dvn_parts = []
            for ci in range(nch):
                rs = slice(SGU_CHUNK * ci, SGU_CHUNK * (ci + 1))
                mixed = _dot(wg, vn[rs], NN) + bt_ref[:, g:g + 1]
                dz_ref[rs, sl] = (dy_g[rs] * mixed * dgelu[rs, sl]).astype(dz_ref.dtype)
                dmixed = dy_g[rs] * u_g[rs]
                dm = dmixed.astype(MXU_DTYPE)
                dvn_parts.append(_dot(wg, dm, TN))
                dw_ref[g] += _dot(dm, vn[rs], NT)
                dbt_ref[:, g:g + 1] += jnp.sum(dmixed, axis=-1, keepdims=True)
            dvn = jnp.concatenate(dvn_parts, axis=0)
            dg_ref[:, sl] += jnp.sum(dvn * xhat, axis=0, keepdims=True)
            dbb_ref[:, sl] += jnp.sum(dvn, axis=0, keepdims=True)
            dxh = dvn * gain
            dv = rstd * (dxh - jnp.mean(dxh, axis=-1, keepdims=True)
                         - xhat * jnp.mean(dxh * xhat, axis=-1, keepdims=True))
            dz_ref[:, sv] = (dv * dgelu[:, sv]).astype(dz_ref.dtype)

    full = lambda shape: pl.BlockSpec(shape, lambda i: tuple(0 for _ in shape))
    return pl.pallas_call(
        body, name="sgu_bwd", grid=(s // SGU_ROWS,),
        in_specs=[pl.BlockSpec((SGU_ROWS, 2 * SGU_W), lambda i: (i, 0)), full((1, SGU_W)), full((1, SGU_W)),
                  full((SGU_G, SGU_CHUNK, SGU_CHUNK)), full((SGU_CHUNK, SGU_G)),
                  pl.BlockSpec((SGU_ROWS, SGU_W), lambda i: (i, 0))],
        out_specs=[pl.BlockSpec((SGU_ROWS, 2 * SGU_W), lambda i: (i, 0)), full((1, SGU_W)), full((1, SGU_W)),
                   full((SGU_G, SGU_CHUNK, SGU_CHUNK)), full((SGU_CHUNK, SGU_G))],
        out_shape=[jax.ShapeDtypeStruct((s, 2 * SGU_W), MXU_DTYPE), jax.ShapeDtypeStruct((1, SGU_W), F32),
                   jax.ShapeDtypeStruct((1, SGU_W), F32), jax.ShapeDtypeStruct((SGU_G, SGU_CHUNK, SGU_CHUNK), F32),
                   jax.ShapeDtypeStruct((SGU_CHUNK, SGU_G), F32)],
        compiler_params=_params(("arbitrary",)),
    )(zs, ln_g.reshape(1, SGU_W), ln_b.reshape(1, SGU_W), w_mask, b_t, dy)


def _ssm_discretize(a_re, a_im, log_dt, b_re, b_im):
    dt = jnp.exp(log_dt)[:, None]
    mag = jnp.exp(a_re * dt)
    ab_re = mag * jnp.cos(a_im * dt)
    ab_im = mag * jnp.sin(a_im * dt)
    den = a_re * a_re + a_im * a_im
    f_re = ((ab_re - 1.0) * a_re + ab_im * a_im) / den
    f_im = (ab_im * a_re - (ab_re - 1.0) * a_im) / den
    bb_re = f_re[:, :, None] * b_re - f_im[:, :, None] * b_im
    bb_im = f_re[:, :, None] * b_im + f_im[:, :, None] * b_re
    return ab_re, ab_im, bb_re, bb_im


def _ssm_operands(a_re, a_im, log_dt, b_re, b_im, c_re, c_im):
    ab_re, ab_im, bb_re, bb_im = _ssm_discretize(a_re, a_im, log_dt, b_re, b_im)
    eye = jnp.eye(SSM_G, dtype=F32)
    b_blk = jnp.einsum("pgnc,gh->gcphn", jnp.stack([bb_re, bb_im]), eye).reshape(SSM_W, 2 * NSTATE)
    c_mat = jnp.einsum("pgcn,gh->pgnhc", jnp.stack([c_re, -c_im]), eye).reshape(2 * NSTATE, SSM_W)
    a_row = jnp.stack([ab_re.reshape(NSTATE), ab_im.reshape(NSTATE)])
    p_re, p_im = a_row[0:1], a_row[1:2]
    while p_re.shape[0] < SSM_TSEG:
        l_re, l_im = p_re[-1:], p_im[-1:]
        p_re, p_im = (jnp.concatenate([p_re, p_re * l_re - p_im * l_im]),
                      jnp.concatenate([p_im, p_re * l_im + p_im * l_re]))
    p_tab = jnp.stack([p_re, p_im])
    return b_blk, c_mat, a_row, p_tab


def _lane_chunks():
    return [(lo, lo + SSM_LANE_CHUNK) for lo in range(0, NSTATE, SSM_LANE_CHUNK)]


def _seg_rows(j):
    return pl.ds(pl.multiple_of(j * SSM_NSEG, SSM_NSEG), SSM_NSEG)


def _to_segments(t):
    s, w = t.shape
    return t.reshape(s // SSM_TB, SSM_NSEG, SSM_TSEG, w).transpose(0, 2, 1, 3).reshape(s, w)


def _from_segments(t):
    s, w = t.shape
    return t.reshape(s // SSM_TB, SSM_TSEG, SSM_NSEG, w).transpose(0, 2, 1, 3).reshape(s, w)


def _ssm_local_scan(buf, a_ref, *, reverse):
    ends_re, ends_im = [], []
    for lo, hi in _lane_chunks():
        are = jnp.broadcast_to(a_ref[0:1, lo:hi], (SSM_NSEG, hi - lo))
        aim = jnp.broadcast_to(a_ref[1:2, lo:hi], (SSM_NSEG, hi - lo))
        if reverse:
            aim = -aim

        def step(jj, carry, lo=lo, hi=hi, are=are, aim=aim):
            xr, xi = carry
            j = (SSM_TSEG - 1 - jj) if reverse else jj
            tr = buf[_seg_rows(j), lo:hi]
            ti = buf[_seg_rows(j), NSTATE + lo:NSTATE + hi]
            nr = are * xr - aim * xi + tr
            ni = are * xi + aim * xr + ti
            buf[_seg_rows(j), lo:hi] = nr
            buf[_seg_rows(j), NSTATE + lo:NSTATE + hi] = ni
            return nr, ni

        zero = jnp.zeros((SSM_NSEG, hi - lo), F32)
        xr, xi = lax.fori_loop(0, SSM_TSEG, step, (zero, zero), unroll=4)
        ends_re.append(xr)
        ends_im.append(xi)
    return jnp.concatenate(ends_re, axis=1), jnp.concatenate(ends_im, axis=1)


def _ssm_entry_states(ends_re, ends_im, carry_ref, p_ref, entry_ref, *, reverse):
    at_re = p_ref[0, SSM_TSEG - 1:SSM_TSEG, :]
    at_im = p_ref[1, SSM_TSEG - 1:SSM_TSEG, :]
    if reverse:
        at_im = -at_im
    cur_re = carry_ref[0:1, 0:NSTATE]
    cur_im = carry_ref[0:1, NSTATE:2 * NSTATE]
    order = range(SSM_NSEG - 1, -1, -1) if reverse else range(SSM_NSEG)
    for i in order:
        entry_ref[0, i:i + 1, 0:NSTATE] = cur_re
        entry_ref[0, i:i + 1, NSTATE:2 * NSTATE] = cur_im
        nxt_re = ends_re[i:i + 1] + at_re * cur_re - at_im * cur_im
        nxt_im = ends_im[i:i + 1] + at_re * cur_im + at_im * cur_re
        cur_re, cur_im = nxt_re, nxt_im
    carry_ref[0:1, 0:NSTATE] = cur_re
    carry_ref[0:1, NSTATE:2 * NSTATE] = cur_im


def _ssm_fixup(buf, p_ref, entry_ref, *, reverse):
    for lo, hi in _lane_chunks():
        e_re = entry_ref[0, :, lo:hi]
        e_im = entry_ref[0, :, NSTATE + lo:NSTATE + hi]

        def step(j, carry, lo=lo, hi=hi, e_re=e_re, e_im=e_im):
            jp = (SSM_TSEG - 1 - j) if reverse else j
            pr = p_ref[0, pl.ds(jp, 1), lo:hi]
            pi = p_ref[1, pl.ds(jp, 1), lo:hi]
            if reverse:
                pi = -pi
            buf[_seg_rows(j), lo:hi] = buf[_seg_rows(j), lo:hi] + pr * e_re - pi * e_im
            buf[_seg_rows(j), NSTATE + lo:NSTATE + hi] = (buf[_seg_rows(j), NSTATE + lo:NSTATE + hi]
                                                           + pr * e_im + pi * e_re)
            return carry

        lax.fori_loop(0, SSM_TSEG, step, 0, unroll=4)


def _ssm_fwd(u, ops, d_skip, glu_w, glu_b):
    b_blk, c_mat, a_row, p_tab = ops
    s = u.shape[0]
    nblk = s // SSM_TB

    def body(u_ref, bb_ref, cm_ref, a_ref, p_ref, d_ref, gw_ref, gb_ref, y_ref, entry_ref, xbuf, carry):
        @pl.when(pl.program_id(0) == 0)
        def _():
            carry[...] = jnp.zeros_like(carry)

        uu = u_ref[...]
        xbuf[...] = _dotf(uu, bb_ref[...], NN)
        ends_re, ends_im = _ssm_local_scan(xbuf, a_ref, reverse=False)
        _ssm_entry_states(ends_re, ends_im, carry, p_ref, entry_ref, reverse=False)
        _ssm_fixup(xbuf, p_ref, entry_ref, reverse=False)
        y = _dotf(xbuf[...],cm_ref[...], NN) + d_ref[...] * uu
        y2 = _gelu(y)
        gate = jax.nn.sigmoid(_dot(y2.astype(MXU_DTYPE), gw_ref[...].astype(MXU_DTYPE), NN) + gb_ref[...])
        y_ref[...] = y2 * gate

    full = lambda shape: pl.BlockSpec(shape, lambda i: tuple(0 for _ in shape))
    y_seg, entry = pl.pallas_call(
        body, name="ssm_fwd", grid=(nblk,),
        in_specs=[pl.BlockSpec((SSM_TB, SSM_W), lambda i: (i, 0)), full(b_blk.shape), full(c_mat.shape),
                  full(a_row.shape), full(p_tab.shape), full((1, SSM_W)), full((SSM_W, SSM_W)), full((1, SSM_W))],
        out_specs=[pl.BlockSpec((SSM_TB, SSM_W), lambda i: (i, 0)),
                   pl.BlockSpec((1, SSM_NSEG, 2 * NSTATE), lambda i: (i, 0, 0))],
        out_shape=[jax.ShapeDtypeStruct((s, SSM_W), F32), jax.ShapeDtypeStruct((nblk, SSM_NSEG, 2 * NSTATE), F32)],
        scratch_shapes=[pltpu.VMEM((SSM_TB, 2 * NSTATE), F32), pltpu.VMEM((SSM_NSEG, 2 * NSTATE), F32)],
        compiler_params=_params(("arbitrary",)),
    )(_to_segments(u), b_blk, c_mat, a_row, p_tab, d_skip.reshape(1, SSM_W), glu_w, glu_b.reshape(1, SSM_W))
    return _from_segments(y_seg), entry


def _ssm_bwd(u, entry, ops, d_skip, glu_w, glu_b, dout):
    b_blk, c_mat, a_row, p_tab = ops
    s = u.shape[0]
    nblk = s // SSM_TB

    def body(u_ref, en_ref, bb_ref, cm_ref, a_ref, p_ref, d_ref, gw_ref, gb_ref, do_ref,
             du_ref, dbb_ref, dcm_ref, da_ref, dd_ref, dgw_ref, dgb_ref, xbuf, gbuf, gcarry, gentry):
        @pl.when(pl.program_id(0) == 0)
        def _():
            gcarry[...] = jnp.zeros_like(gcarry)
            for r in (dbb_ref, dcm_ref, da_ref, dd_ref, dgw_ref, dgb_ref):
                r[...] = jnp.zeros_like(r)

        uu = u_ref[...]
        xbuf[...] = _dotf(uu, bb_ref[...], NN)
        _ssm_local_scan(xbuf, a_ref, reverse=False)
        _ssm_fixup(xbuf, p_ref, en_ref, reverse=False)
        y = _dotf(xbuf[...],cm_ref[...], NN) + d_ref[...] * uu
        y2 = _gelu(y)
        y2m = y2.astype(MXU_DTYPE)
        gwm = gw_ref[...].astype(MXU_DTYPE)
        gate = jax.nn.sigmoid(_dot(y2m, gwm, NN) + gb_ref[...])
        dout = do_ref[...]
        dpre = dout * y2 * gate * (1.0 - gate)
        dprem = dpre.astype(MXU_DTYPE)
        dy2 = dout * gate + _dot(dprem, gwm, NT)
        dgw_ref[...] += _dot(y2m, dprem, TN)
        dgb_ref[...] += jnp.sum(dpre, axis=0, keepdims=True)
        dy = dy2 * _gelu_grad(y)
        dd_ref[...] += jnp.sum(dy * uu, axis=0, keepdims=True)
        dcm_ref[...] += _dotf(xbuf[...],dy, TN)
        gbuf[...] = _dotf(dy, cm_ref[...], NT)
        gs_re, gs_im = _ssm_local_scan(gbuf, a_ref, reverse=True)
        _ssm_entry_states(gs_re, gs_im, gcarry, p_ref, gentry, reverse=True)
        _ssm_fixup(gbuf, p_ref, gentry, reverse=True)
        du_ref[...] = (_dotf(gbuf[...], bb_ref[...], NT) + d_ref[...] * dy).astype(du_ref.dtype)
        dbb_ref[...] += _dotf(uu, gbuf[...], TN)
        for lo, hi in _lane_chunks():
            def step(j, carry, lo=lo, hi=hi):
                acc_re, acc_im = carry
                g_re = gbuf[_seg_rows(j), lo:hi]
                g_im = gbuf[_seg_rows(j), NSTATE + lo:NSTATE + hi]
                x_re = xbuf[_seg_rows(j - 1), lo:hi]
                x_im = xbuf[_seg_rows(j - 1), NSTATE + lo:NSTATE + hi]
                return acc_re + g_re * x_re + g_im * x_im, acc_im + g_im * x_re - g_re * x_im

            g0_re = gbuf[_seg_rows(0), lo:hi]
            g0_im = gbuf[_seg_rows(0), NSTATE + lo:NSTATE + hi]
            e_re = en_ref[0, :, lo:hi]
            e_im = en_ref[0, :, NSTATE + lo:NSTATE + hi]
            init = (g0_re * e_re + g0_im * e_im, g0_im * e_re - g0_re * e_im)
            acc_re, acc_im = lax.fori_loop(1, SSM_TSEG, step, init, unroll=4)
            da_ref[0:1, lo:hi] += jnp.sum(acc_re, axis=0, keepdims=True)
            da_ref[1:2, lo:hi] += jnp.sum(acc_im, axis=0, keepdims=True)

    full = lambda shape: pl.BlockSpec(shape, lambda i: tuple(0 for _ in shape))
    rev = pl.BlockSpec((SSM_TB, SSM_W), lambda i: (nblk - 1 - i, 0))
    outs = pl.pallas_call(
        body, name="ssm_bwd", grid=(nblk,),
        in_specs=[rev, pl.BlockSpec((1, SSM_NSEG, 2 * NSTATE), lambda i: (nblk - 1 - i, 0, 0)),
                  full(b_blk.shape), full(c_mat.shape), full(a_row.shape), full(p_tab.shape),
                  full((1, SSM_W)), full((SSM_W, SSM_W)), full((1, SSM_W)), rev],
        out_specs=[rev, full(b_blk.shape), full(c_mat.shape), full(a_row.shape), full((1, SSM_W)),
                   full((SSM_W, SSM_W)), full((1, SSM_W))],
        out_shape=[jax.ShapeDtypeStruct((s, SSM_W), MXU_DTYPE), jax.ShapeDtypeStruct(b_blk.shape, F32),
                   jax.ShapeDtypeStruct(c_mat.shape, F32), jax.ShapeDtypeStruct(a_row.shape, F32),
                   jax.ShapeDtypeStruct((1, SSM_W), F32), jax.ShapeDtypeStruct((SSM_W, SSM_W), F32),
                   jax.ShapeDtypeStruct((1, SSM_W), F32)],
        scratch_shapes=[pltpu.VMEM((SSM_TB, 2 * NSTATE), F32), pltpu.VMEM((SSM_TB, 2 * NSTATE), F32),
                        pltpu.VMEM((SSM_NSEG, 2 * NSTATE), F32), pltpu.VMEM((1, SSM_NSEG, 2 * NSTATE), F32)],
        compiler_params=_params(("arbitrary",)),
    )(_to_segments(u), entry, b_blk, c_mat, a_row, p_tab, d_skip.reshape(1, SSM_W), glu_w, glu_b.reshape(1, SSM_W),
      _to_segments(dout))
    return (_from_segments(outs[0]),) + tuple(outs[1:])


MIX_SEGS = ((0, ATTN_W), (ATTN_W, ATTN_W + SGU_W), (ATTN_W + SGU_W, D_MODEL))


def _mix_fwd(y_attn, y_sgu, y_ssm, gain):
    s = y_attn.shape[0]

    def body(a_ref, b_ref, c_ref, g_ref, o_ref):
        for ref, (lo, hi) in zip((a_ref, b_ref, c_ref), MIX_SEGS):
            x = ref[...]
            r = lax.rsqrt(jnp.mean(x * x, axis=-1, keepdims=True) + EPS)
            o_ref[:, lo:hi] = (x * r * g_ref[:, lo:hi]).astype(o_ref.dtype)

    row = lambda w: pl.BlockSpec((ROWS, w), lambda i: (i, 0))
    return pl.pallas_call(
        body, name="mix_fwd", grid=(s // ROWS,),
        in_specs=[row(ATTN_W), row(SGU_W), row(SSM_W), pl.BlockSpec((1, D_MODEL), lambda i: (0, 0))],
        out_specs=row(D_MODEL), out_shape=jax.ShapeDtypeStruct((s, D_MODEL), MXU_DTYPE),
        compiler_params=_params(("parallel",)),
    )(y_attn, y_sgu, y_ssm, gain.reshape(1, D_MODEL))


def _mix_bwd(y_attn, y_sgu, y_ssm, gain, dmix):
    s = y_attn.shape[0]

    def body(a_ref, b_ref, c_ref, g_ref, dm_ref, da_ref, db_ref, dc_ref, dg_ref):
        @pl.when(pl.program_id(0) == 0)
        def _():
            dg_ref[...] = jnp.zeros_like(dg_ref)

        for ref, dref, (lo, hi) in zip((a_ref, b_ref, c_ref), (da_ref, db_ref, dc_ref), MIX_SEGS):
            x = ref[...]
            r = lax.rsqrt(jnp.mean(x * x, axis=-1, keepdims=True) + EPS)
            xhat = x * r
            dm = dm_ref[:, lo:hi].astype(F32)
            dg_ref[:, lo:hi] += jnp.sum(dm * xhat, axis=0, keepdims=True)
            dxh = dm * g_ref[:, lo:hi]
            dref[...] = r * (dxh - xhat * jnp.mean(dxh * xhat, axis=-1, keepdims=True))

    row = lambda w: pl.BlockSpec((ROWS, w), lambda i: (i, 0))
    vec = pl.BlockSpec((1, D_MODEL), lambda i: (0, 0))
    return pl.pallas_call(
        body, name="mix_bwd", grid=(s // ROWS,),
        in_specs=[row(ATTN_W), row(SGU_W), row(SSM_W), vec, row(D_MODEL)],
        out_specs=[row(ATTN_W), row(SGU_W), row(SSM_W), vec],
        out_shape=[jax.ShapeDtypeStruct((s, ATTN_W), F32), jax.ShapeDtypeStruct((s, SGU_W), F32),
                   jax.ShapeDtypeStruct((s, SSM_W), F32), jax.ShapeDtypeStruct((1, D_MODEL), F32)],
        compiler_params=_params(("arbitrary",)),
    )(y_attn, y_sgu, y_ssm, gain.reshape(1, D_MODEL), dmix)


CONV_ROWS = 256
CONV_COLS = 1408
HALO = 8


def _causal_taps(main, halo, first):
    rows = main.shape[0]
    row = lax.broadcasted_iota(jnp.int32, main.shape, 0)
    h7 = jnp.where(first, 0.0, halo[HALO - 1:HALO, :])
    h6 = jnp.where(first, 0.0, halo[HALO - 2:HALO - 1, :])
    x1 = jnp.where(row == 0, h7, pltpu.roll(main, 1, 0))
    x2 = jnp.where(row == 0, h6, jnp.where(row == 1, h7, pltpu.roll(main, 2, 0)))
    del rows
    return x1, x2


def _conv_in_specs(ncol_half):
    halo_idx = lambda i: jnp.maximum(i * (CONV_ROWS // HALO) - 1, 0)
    specs = []
    for off in (0, ncol_half):
        specs += [pl.BlockSpec((CONV_ROWS, CONV_COLS), functools.partial(lambda j, i, o: (i, j + o), o=off)),
                  pl.BlockSpec((HALO, CONV_COLS), functools.partial(lambda j, i, o: (halo_idx(i), j + o), o=off)),
                  pl.BlockSpec((3, CONV_COLS), functools.partial(lambda j, i, o: (0, j + o), o=off)),
                  pl.BlockSpec((1, CONV_COLS), functools.partial(lambda j, i, o: (0, j + o), o=off))]
    return specs


def _ffn_act_fwd(hh, conv_w, conv_b):
    s = hh.shape[0]
    ncol = D_FF // CONV_COLS

    def body(v_ref, vh_ref, vw_ref, vb_ref, g_ref, gh_ref, gw_ref, gb_ref, o_ref):
        first = pl.program_id(1) == 0
        outs = []
        for m_ref, h_ref, w_ref, b_ref in ((v_ref, vh_ref, vw_ref, vb_ref), (g_ref, gh_ref, gw_ref, gb_ref)):
            main = m_ref[...]
            x1, x2 = _causal_taps(main, h_ref[...], first)
            outs.append(w_ref[0:1, :] * x2 + w_ref[1:2, :] * x1 + w_ref[2:3, :] * main + b_ref[...])
        o_ref[...] = (_gelu(outs[1]) * outs[0]).astype(o_ref.dtype)

    return pl.pallas_call(
        body, name="ffn_act_fwd", grid=(ncol, s // CONV_ROWS), in_specs=_conv_in_specs(ncol),
        out_specs=pl.BlockSpec((CONV_ROWS, CONV_COLS), lambda j, i: (i, j)),
        out_shape=jax.ShapeDtypeStruct((s, D_FF), MXU_DTYPE),
        compiler_params=_params(("parallel", "parallel")),
    )(hh, hh, conv_w, conv_b.reshape(1, -1), hh, hh, conv_w, conv_b.reshape(1, -1))


def _ffn_act_bwd(hh, conv_w, conv_b, da):
    s = hh.shape[0]
    ncol = D_FF // CONV_COLS

    def body(v_ref, vh_ref, vw_ref, vb_ref, g_ref, gh_ref, gw_ref, gb_ref, da_ref,
             dv_ref, dg_ref, dwv_ref, dwg_ref, dbv_ref, dbg_ref):
        first = pl.program_id(1) == 0

        @pl.when(first)
        def _():
            for r in (dwv_ref, dwg_ref, dbv_ref, dbg_ref):
                r[...] = jnp.zeros_like(r)

        taps, conv = [], []
        for m_ref, h_ref, w_ref, b_ref in ((v_ref, vh_ref, vw_ref, vb_ref), (g_ref, gh_ref, gw_ref, gb_ref)):
            main = m_ref[...]
            x1, x2 = _causal_taps(main, h_ref[...], first)
            taps.append((x2, x1, main))
            conv.append(w_ref[0:1, :] * x2 + w_ref[1:2, :] * x1 + w_ref[2:3, :] * main + b_ref[...])
        da = da_ref[...].astype(F32)
        d_val = da * _gelu(conv[1])
        d_gate = da * conv[0] * _gelu_grad(conv[1])
        for dcv, tp, d_ref, dw_ref, db_ref in ((d_val, taps[0], dv_ref, dwv_ref, dbv_ref),
                                               (d_gate, taps[1], dg_ref, dwg_ref, dbg_ref)):
            d_ref[...] = dcv.astype(d_ref.dtype)
            for t in range(3):
                dw_ref[t:t + 1, :] += jnp.sum(dcv * tp[t], axis=0, keepdims=True)
            db_ref[...] += jnp.sum(dcv, axis=0, keepdims=True)

    blk = pl.BlockSpec((CONV_ROWS, CONV_COLS), lambda j, i: (i, j))
    w_out = pl.BlockSpec((3, CONV_COLS), lambda j, i: (0, j))
    b_out = pl.BlockSpec((1, CONV_COLS), lambda j, i: (0, j))
    outs = pl.pallas_call(
        body, name="ffn_act_bwd", grid=(ncol, s // CONV_ROWS), in_specs=_conv_in_specs(ncol) + [blk],
        out_specs=[blk, blk, w_out, w_out, b_out, b_out],
        out_shape=[jax.ShapeDtypeStruct((s, D_FF), MXU_DTYPE)] * 2 + [jax.ShapeDtypeStruct((3, D_FF), F32)] * 2
        + [jax.ShapeDtypeStruct((1, D_FF), F32)] * 2,
        compiler_params=_params(("parallel", "arbitrary")),
    )(hh, hh, conv_w, conv_b.reshape(1, -1), hh, hh, conv_w, conv_b.reshape(1, -1), da)
    dval, dgate, dwv, dwg, dbv, dbg = outs
    return (jnp.concatenate([dval, dgate], axis=1), jnp.concatenate([dwv, dwg], axis=1),
            jnp.concatenate([dbv, dbg], axis=1))


def _conv_transpose(dconv, conv_w):
    s, n = dconv.shape
    nrow = s // CONV_ROWS

    def body(m_ref, nx_ref, w_ref, o_ref):
        main = m_ref[...].astype(F32)
        last = pl.program_id(1) == nrow - 1
        nx = nx_ref[...].astype(F32)
        n0 = jnp.where(last, 0.0, nx[0:1, :])
        n1 = jnp.where(last, 0.0, nx[1:2, :])
        row = lax.broadcasted_iota(jnp.int32, main.shape, 0)
        y1 = jnp.where(row == CONV_ROWS - 1, n0, pltpu.roll(main, CONV_ROWS - 1, 0))
        y2 = jnp.where(row == CONV_ROWS - 2, n0, jnp.where(row == CONV_ROWS - 1, n1, pltpu.roll(main, CONV_ROWS - 2, 0)))
        o_ref[...] = (w_ref[2:3, :] * main + w_ref[1:2, :] * y1 + w_ref[0:1, :] * y2).astype(o_ref.dtype)

    halo_rows = 16
    nxt = lambda i: jnp.minimum((i + 1) * (CONV_ROWS // halo_rows), s // halo_rows - 1)
    return pl.pallas_call(
        body, name="ffn_conv_transpose", grid=(n // CONV_COLS, nrow),
        in_specs=[pl.BlockSpec((CONV_ROWS, CONV_COLS), lambda j, i: (i, j)),
                  pl.BlockSpec((halo_rows, CONV_COLS), lambda j, i: (nxt(i), j)),
                  pl.BlockSpec((3, CONV_COLS), lambda j, i: (0, j))],
        out_specs=pl.BlockSpec((CONV_ROWS, CONV_COLS), lambda j, i: (i, j)),
        out_shape=jax.ShapeDtypeStruct((s, n), MXU_DTYPE),
        compiler_params=_params(("parallel", "parallel")),
    )(dconv, dconv, conv_w)


def _ple_fwd(xn, p, w_gate, w_proj, h):
    s = xn.shape[0]
    tm = 512

    def body(x_ref, p_ref, wg_ref, wp_ref, h_ref, o_ref):
        gate = jax.nn.sigmoid(_dot(x_ref[...].astype(MXU_DTYPE), wg_ref[...].astype(MXU_DTYPE), NN))
        proj = _dot(p_ref[...].astype(MXU_DTYPE), wp_ref[...].astype(MXU_DTYPE), NN)
        o_ref[...] = h_ref[...] + gate * proj

    return pl.pallas_call(
        body, name="ple_fwd", grid=(s // tm,),
        in_specs=[pl.BlockSpec((tm, D_MODEL), lambda i: (i, 0)), pl.BlockSpec((tm, PLE_DIM), lambda i: (i, 0)),
                  pl.BlockSpec((D_MODEL, D_MODEL), lambda i: (0, 0)), pl.BlockSpec((PLE_DIM, D_MODEL), lambda i: (0, 0)),
                  pl.BlockSpec((tm, D_MODEL), lambda i: (i, 0))],
        out_specs=pl.BlockSpec((tm, D_MODEL), lambda i: (i, 0)),
        out_shape=jax.ShapeDtypeStruct((s, D_MODEL), F32),
        compiler_params=_params(("parallel",)),
    )(xn, p, w_gate, w_proj, h)


def _ple_bwd(xn, p, w_gate, w_proj, dh):
    s = xn.shape[0]
    tm = 512

    def body(x_ref, p_ref, wg_ref, wp_ref, dh_ref, dpre_ref, dproj_ref):
        gate = jax.nn.sigmoid(_dot(x_ref[...].astype(MXU_DTYPE), wg_ref[...].astype(MXU_DTYPE), NN))
        proj = _dot(p_ref[...].astype(MXU_DTYPE), wp_ref[...].astype(MXU_DTYPE), NN)
        dh = dh_ref[...]
        dpre_ref[...] = (dh * proj * gate * (1.0 - gate)).astype(dpre_ref.dtype)
        dproj_ref[...] = (dh * gate).astype(dproj_ref.dtype)

    row = pl.BlockSpec((tm, D_MODEL), lambda i: (i, 0))
    return pl.pallas_call(
        body, name="ple_bwd", grid=(s // tm,),
        in_specs=[row, pl.BlockSpec((tm, PLE_DIM), lambda i: (i, 0)),
                  pl.BlockSpec((D_MODEL, D_MODEL), lambda i: (0, 0)), pl.BlockSpec((PLE_DIM, D_MODEL), lambda i: (0, 0)),
                  row],
        out_specs=[row, row],
        out_shape=[jax.ShapeDtypeStruct((s, D_MODEL), MXU_DTYPE)] * 2,
        compiler_params=_params(("parallel",)),
    )(xn, p, w_gate, w_proj, dh)


O_SGU = 3 * ATTN_W
O_SSM = O_SGU + 2 * SGU_W


def _layer_consts(w, i):
    causal = jnp.asarray(np.tril(np.ones((SGU_CHUNK, SGU_CHUNK), np.float32)))
    return {
        "sgu_w_mask": w["sgu_w"][i] * causal,
        "sgu_b_t": w["sgu_b"][i].T,
        "ssm_ops": _ssm_operands(w["ssm_a_re"][i], w["ssm_a_im"][i], w["ssm_log_dt"][i], w["ssm_b_re"][i],
                                 w["ssm_b_im"][i], w["ssm_c_re"][i], w["ssm_c_im"][i]),
    }


def _layer_fwd(h0, p_i, w, i, bias):
    c = _layer_consts(w, i)
    w_in = w["w_in"][i]
    xn1 = _rms_fwd(h0, w["norm_attn_g"][i], name="rms_attn_fwd")
    qkv = _matmul(xn1, w_in[:, :O_SGU], name="in_proj_qkv", out_dtype=MXU_DTYPE, tm=1024, tn=1536)
    zs = _matmul(xn1, w_in[:, O_SGU:O_SSM], name="in_proj_sgu", out_dtype=F32, tm=1024, tn=512)
    us = _matmul(xn1, w_in[:, O_SSM:], name="in_proj_ssm", out_dtype=F32, tm=1024, tn=256)
    y_attn, lse = _attn_fwd(qkv, bias)
    y_sgu = _sgu_fwd(zs, w["sgu_ln_g"][i], w["sgu_ln_b"][i], c["sgu_w_mask"], c["sgu_b_t"])
    y_ssm, entry = _ssm_fwd(us, c["ssm_ops"], w["ssm_d"][i], w["ssm_glu_w"][i], w["ssm_glu_b"][i])
    mix = _mix_fwd(y_attn, y_sgu, y_ssm, w["branch_norm_g"][i])
    h1 = _matmul(mix, w["w_out"][i], name="out_proj", out_dtype=F32, tm=512, tn=1024, residual=h0)
    xn2 = _rms_fwd(h1, w["norm_ffn_g"][i], name="rms_ffn_fwd")
    hh = _matmul(xn2, w["ffn_w_up"][i], name="ffn_up", out_dtype=F32, tm=1024, tn=1408)
    act = _ffn_act_fwd(hh, w["ffn_conv_w"][i], w["ffn_conv_b"][i])
    h2 = _matmul(act, w["ffn_w_down"][i], name="ffn_down", out_dtype=F32, tm=512, tn=1024, residual=h1)
    xn3 = _rms_fwd(h2, w["norm_ple_g"][i], name="rms_ple_fwd")
    h3 = _ple_fwd(xn3, p_i, w["ple_w_gate"][i], w["ple_w_proj"][i], h2)
    saved = dict(h0=h0, xn1=xn1, qkv=qkv, zs=zs, us=us, y_attn=y_attn, lse=lse, y_sgu=y_sgu, y_ssm=y_ssm,
                 entry=entry, mix=mix, h1=h1, xn2=xn2, hh=hh, act=act, h2=h2, xn3=xn3, consts=c)
    return h3, saved


def _layer_bwd(dh3, sv, p_i, w, i, bias):
    c = sv["consts"]
    g = {}
    dpre, dproj = _ple_bwd(sv["xn3"], p_i, w["ple_w_gate"][i], w["ple_w_proj"][i], dh3)
    g["ple_w_gate"] = _matmul_tn(sv["xn3"], dpre, name="d_ple_w_gate", tk=1024, tn=1024)
    g["ple_w_proj"] = _matmul_tn(p_i, dproj, name="d_ple_w_proj", tk=256, tn=1024)
    dxn3 = _matmul(dpre, w["ple_w_gate"][i], name="d_xn_ple", out_dtype=F32, tm=512, tn=1024, trans_b=True)
    dh2, g["norm_ple_g"] = _rms_bwd(sv["h2"], w["norm_ple_g"][i], dxn3, dh3, name="rms_ple_bwd")
    g["ffn_w_down"] = _matmul_tn(sv["act"], dh2, name="d_ffn_w_down", tk=1408, tn=1024)
    dact = _matmul(dh2, w["ffn_w_down"][i], name="d_ffn_act", out_dtype=MXU_DTYPE, tm=512, tn=1408, trans_b=True)
    dconv, g["ffn_conv_w"], g["ffn_conv_b"] = _ffn_act_bwd(sv["hh"], w["ffn_conv_w"][i], w["ffn_conv_b"][i], dact)
    dhh = _conv_transpose(dconv, w["ffn_conv_w"][i])
    g["ffn_w_up"] = _matmul_tn(sv["xn2"], dhh, name="d_ffn_w_up", tk=1024, tn=1408)
    dxn2 = _matmul(dhh, w["ffn_w_up"][i], name="d_xn_ffn", out_dtype=F32, tm=512, tn=512, trans_b=True)
    dh1, g["norm_ffn_g"] = _rms_bwd(sv["h1"], w["norm_ffn_g"][i], dxn2, dh2, name="rms_ffn_bwd")
    g["w_out"] = _matmul_tn(sv["mix"], dh1, name="d_w_out", tk=1024, tn=1024)
    dmix = _matmul(dh1, w["w_out"][i], name="d_mix", out_dtype=F32, tm=512, tn=1024, trans_b=True)
    dy_attn, dy_sgu, dy_ssm, g["branch_norm_g"] = _mix_bwd(sv["y_attn"], sv["y_sgu"], sv["y_ssm"],
                                                           w["branch_norm_g"][i], dmix)
    dqkv, dbias = _attn_bwd(sv["qkv"], bias, sv["y_attn"], sv["lse"], dy_attn)
    dzs, g["sgu_ln_g"], g["sgu_ln_b"], dsw, dsb = _sgu_bwd(sv["zs"], w["sgu_ln_g"][i], w["sgu_ln_b"][i],
                                                          c["sgu_w_mask"], c["sgu_b_t"], dy_sgu)
    causal = jnp.asarray(np.tril(np.ones((SGU_CHUNK, SGU_CHUNK), np.float32)))
    g["sgu_w"] = dsw * causal
    g["sgu_b"] = dsb.T
    dus, dbb, dcm, da, g["ssm_d"], g["ssm_glu_w"], g["ssm_glu_b"] = _ssm_bwd(
        sv["us"], sv["entry"], c["ssm_ops"], w["ssm_d"][i], w["ssm_glu_w"][i], w["ssm_glu_b"][i], dy_ssm)
    dbb5 = dbb.reshape(SSM_G, SSM_C, 2, SSM_G, SSM_N)
    dbbar = jnp.einsum("gcpgn->pgnc", dbb5)
    dcm5 = dcm.reshape(2, SSM_G, SSM_N, SSM_G, SSM_C)
    dcc = jnp.einsum("pgngc->pgcn", dcm5)
    g["ssm_c_re"] = dcc[0]
    g["ssm_c_im"] = -dcc[1]
    da2 = da.reshape(2, SSM_G, SSM_N)
    _, vjp = jax.vjp(_ssm_discretize, w["ssm_a_re"][i], w["ssm_a_im"][i], w["ssm_log_dt"][i],
                     w["ssm_b_re"][i], w["ssm_b_im"][i])
    (g["ssm_a_re"], g["ssm_a_im"], g["ssm_log_dt"], g["ssm_b_re"], g["ssm_b_im"]) = vjp(
        (da2[0], da2[1], dbbar[0], dbbar[1]))
    dz = jnp.concatenate([dqkv, dzs, dus], axis=1)
    g["w_in"] = _matmul_tn(sv["xn1"], dz, name="d_w_in", tk=1024, tn=1152)
    dxn1 = _matmul(dz, w["w_in"][i], name="d_xn_attn", out_dtype=F32, tm=512, tn=1024, trans_b=True)
    dh0, g["norm_attn_g"] = _rms_bwd(sv["h0"], w["norm_attn_g"][i], dxn1, dh1, name="rms_attn_bwd")
    for k in ("norm_ple_g", "norm_ffn_g", "branch_norm_g", "norm_attn_g", "sgu_ln_g", "sgu_ln_b", "ssm_d",
              "ssm_glu_b", "ffn_conv_b"):
        g[k] = g[k].reshape(-1)
    return dh0, g, dbias


def _local_step(x, p, target, w):
    bias = _bias_build(w["rel_bias"])
    h = x
    saved = []
    for i in range(DEPTH):
        h, sv = _layer_fwd(h, p[i], w, i, bias)
        saved.append(sv)
    loss, dh, dgf = _loss_head(h, w["final_norm_g"], target)
    layer_grads = [None] * DEPTH
    dbias = None
    for i in reversed(range(DEPTH)):
        dh, layer_grads[i], db = _layer_bwd(dh, saved[i], p[i], w, i, bias)
        dbias = db if dbias is None else dbias + db
    grads = {k: jnp.stack([layer_grads[i][k] for i in range(DEPTH)]) for k in layer_grads[0]}
    grads["rel_bias"] = _bias_reduce(dbias)
    grads["final_norm_g"] = dgf.reshape(-1)
    return loss, dh, grads


def _pad_rows(a2, mult=16):
    r = (-a2.shape[0]) % mult
    return a2 if r == 0 else jnp.concatenate([a2, jnp.zeros((r, a2.shape[1]), a2.dtype)], axis=0)


def _as_rows(a, rows=None):
    flat = a.reshape(-1)
    if rows is None:
        rows = -(-flat.shape[0] // (16 * PACK_COLS)) * 16
    return jnp.pad(flat, (0, rows * PACK_COLS - flat.shape[0])).reshape(rows, PACK_COLS)


def _shard_shape(name):
    full, ax = BIG_FULL[name]
    shp = [DEPTH] + list(full)
    shp[ax] //= N_CHIPS
    return tuple(shp)


EXACT_NAMES = ("ffn_conv_w",)


def _pack_rows_of(name):
    n = int(np.prod(_shard_shape(name))) * (2 if name in EXACT_NAMES else 1)
    rows = -(-n // PACK_COLS)
    return -(-rows // 16) * 16


def _pack_shards(shards, dtype, exact=False):
    split_words = exact and jnp.dtype(dtype).itemsize == 2
    parts = []
    for n in BIG_NAMES:
        a = shards[n]
        if split_words and n in EXACT_NAMES:
            a = lax.bitcast_convert_type(a.astype(F32), dtype)
        parts.append(_as_rows(a.astype(dtype), _pack_rows_of(n)))
    used = sum(pt.shape[0] for pt in parts)
    parts.append(jnp.zeros((PACK_ROWS - used, PACK_COLS), dtype))
    return jnp.concatenate(parts, axis=0)


def _unpack_shard(flat, name, exact=False):
    off = 0
    for n in BIG_NAMES:
        if n == name:
            break
        off += _pack_rows_of(n)
    shp = _shard_shape(name)
    cnt = int(np.prod(shp))
    vec = flat[off:off + _pack_rows_of(name)].reshape(-1)
    if exact and name in EXACT_NAMES and jnp.dtype(flat.dtype).itemsize == 2:
        return lax.bitcast_convert_type(vec[:2 * cnt].reshape(shp + (2,)), F32)
    return vec[:cnt].reshape(shp)


def _split_full(full, name):
    _, ax = BIG_FULL[name]
    return jnp.stack(jnp.split(full, N_CHIPS, axis=ax))


def _join_shards(stacked, name):
    _, ax = BIG_FULL[name]
    return jnp.concatenate([stacked[k] for k in range(N_CHIPS)], axis=ax)


def _small_shapes(w):
    return [(n, w[n].shape) for n in SMALL_NAMES]


def _pack_small(d):
    flat = jnp.concatenate([d[n].astype(F32).reshape(-1) for n in SMALL_NAMES])
    flat = jnp.concatenate([flat, jnp.zeros((SMALL_ROWS * PACK_COLS - flat.shape[0],), F32)])
    return flat.reshape(SMALL_ROWS, PACK_COLS)


def _unpack_small(flat, shapes):
    out, off = {}, 0
    v = flat.reshape(-1)
    for n, shp in shapes:
        cnt = int(np.prod(shp))
        out[n] = v[off:off + cnt].reshape(shp)
        off += cnt
    return out


MESH = pl.DeviceIdType.MESH
ANY = pl.BlockSpec(memory_space=pl.ANY)


def _me():
    return lax.axis_index("x"), lax.axis_index("y"), lax.axis_index("c")


def _other_chips(x, y):
    return [(1 - x, y), (x, 1 - y), (1 - x, 1 - y)]


def _gather_weights(wflat):
    def body(w_ref, out_ref, send_sems, recv_sems, local_sem):
        x, y, c = _me()
        sibling = (x, y, 1 - c)
        chips = _other_chips(x, y)

        def rows(chip, half):
            return out_ref.at[2 * chip[0] + chip[1], pl.ds(half * PACK_HALF, PACK_HALF), :]

        def copy(k, chip, half, to, src=None):
            return pltpu.make_async_remote_copy(
                src_ref=rows(chip, half) if src is None else src, dst_ref=rows(chip, half),
                send_sem=send_sems.at[k], recv_sem=recv_sems.at[k], device_id=to, device_id_type=MESH)

        mine = pltpu.make_async_copy(w_ref, out_ref.at[2 * x + y], local_sem)
        mine.start()
        my_half = w_ref.at[pl.ds(c * PACK_HALF, PACK_HALF), :]
        first = [copy(j, (x, y), c, (*chip, c), src=my_half) for j, chip in enumerate(chips)]
        for cp in first:
            cp.start()
        passed = [copy(3 + j, chip, c, sibling) for j, chip in enumerate(chips)]
        for j, chip in enumerate(chips):
            copy(j, chip, c, (x, y, c)).wait_recv()
            passed[j].start()
        for j, chip in enumerate(chips):
            copy(3 + j, chip, 1 - c, (x, y, c)).wait_recv()
        for cp in first + passed:
            cp.wait_send()
        mine.wait()

    return pl.pallas_call(
        body, name="gather_weights", in_specs=[ANY], out_specs=ANY,
        out_shape=jax.ShapeDtypeStruct((N_CHIPS, PACK_ROWS, PACK_COLS), wflat.dtype),
        scratch_shapes=[pltpu.SemaphoreType.DMA((6,)), pltpu.SemaphoreType.DMA((6,)), pltpu.SemaphoreType.DMA],
    )(wflat)


def _exchange_partials(gb, gs):
    def body(gb_ref, gs_ref, half_ref, small_ref, send_sems, recv_sems, local_sem):
        x, y, c = _me()
        me_idx = 4 * x + 2 * y + c
        mine = pltpu.make_async_copy(gs_ref, small_ref.at[me_idx], local_sem)
        mine.start()
        d2d = pltpu.make_async_remote_copy(
            src_ref=gb_ref.at[:, pl.ds((1 - c) * PACK_HALF, PACK_HALF), :], dst_ref=half_ref,
            send_sem=send_sems.at[0], recv_sem=recv_sems.at[0], device_id=(x, y, 1 - c), device_id_type=MESH)
        d2d.start()
        copies = []
        for k in range(1, N_DEV):
            fx, fy, fc = (k >> 2) & 1, (k >> 1) & 1, k & 1
            peer = (x ^ fx, y ^ fy, c ^ fc)
            copies.append(pltpu.make_async_remote_copy(
                src_ref=gs_ref, dst_ref=small_ref.at[me_idx], send_sem=send_sems.at[k], recv_sem=recv_sems.at[k],
                device_id=peer, device_id_type=MESH))
        for cp in copies:
            cp.start()
        for k in range(1, N_DEV):
            fx, fy, fc = (k >> 2) & 1, (k >> 1) & 1, k & 1
            peer_idx = 4 * (x ^ fx) + 2 * (y ^ fy) + (c ^ fc)
            pltpu.make_async_remote_copy(
                src_ref=gs_ref, dst_ref=small_ref.at[peer_idx], send_sem=send_sems.at[k], recv_sem=recv_sems.at[k],
                device_id=(x, y, c), device_id_type=MESH).wait_recv()
        d2d.wait_recv()
        d2d.wait_send()
        for cp in copies:
            cp.wait_send()
        mine.wait()

    return pl.pallas_call(
        body, name="exchange_partials", in_specs=[ANY, ANY], out_specs=[ANY, ANY],
        out_shape=[jax.ShapeDtypeStruct((N_CHIPS, PACK_HALF, PACK_COLS), gb.dtype),
                   jax.ShapeDtypeStruct((N_DEV, SMALL_ROWS, PACK_COLS), F32)],
        scratch_shapes=[pltpu.SemaphoreType.DMA((N_DEV,)), pltpu.SemaphoreType.DMA((N_DEV,)), pltpu.SemaphoreType.DMA],
    )(gb, gs)


RED_ROWS = 256


def _chip_partials(gb, sib, c_idx):
    nrow = PACK_HALF // RED_ROWS

    def body(c_ref, a_ref, b_ref, o_ref):
        del c_ref
        o_ref[...] = (a_ref[...].astype(F32) + b_ref[...].astype(F32)).astype(o_ref.dtype)

    blk = (1, RED_ROWS, PACK_COLS)
    return pl.pallas_call(
        body, name="chip_partials",
        grid_spec=pltpu.PrefetchScalarGridSpec(
            num_scalar_prefetch=1, grid=(N_CHIPS, nrow),
            in_specs=[pl.BlockSpec(blk, lambda k, i, c: (k, c[0] * nrow + i, 0)),
                      pl.BlockSpec(blk, lambda k, i, c: (k, i, 0))],
            out_specs=pl.BlockSpec(blk, lambda k, i, c: (k, i, 0))),
        out_shape=jax.ShapeDtypeStruct((N_CHIPS, PACK_HALF, PACK_COLS), gb.dtype),
        compiler_params=_params(("parallel", "parallel")),
    )(c_idx, gb, sib)


def _scatter_partials(pc):
    def body(pc_ref, out_ref, send_sems, recv_sems):
        x, y, c = _me()
        chips = _other_chips(x, y)
        copies = [pltpu.make_async_remote_copy(
            src_ref=pc_ref.at[2 * chip[0] + chip[1]], dst_ref=out_ref.at[k],
            send_sem=send_sems.at[k], recv_sem=recv_sems.at[k], device_id=(*chip, c), device_id_type=MESH)
            for k, chip in enumerate(chips)]
        for cp in copies:
            cp.start()
        for cp in copies:
            cp.wait_recv()
        for cp in copies:
            cp.wait_send()

    return pl.pallas_call(
        body, name="scatter_partials", in_specs=[ANY], out_specs=ANY,
        out_shape=jax.ShapeDtypeStruct((3, PACK_HALF, PACK_COLS), pc.dtype),
        scratch_shapes=[pltpu.SemaphoreType.DMA((3,)), pltpu.SemaphoreType.DMA((3,))],
    )(pc)


def _final_half(gb, sib, recv, idx):
    nrow = PACK_HALF // RED_ROWS

    def body(idx_ref, a_ref, b_ref, r_ref, o_ref):
        del idx_ref
        acc = a_ref[0].astype(F32) + b_ref[0].astype(F32)
        for k in range(3):
            acc = acc + r_ref[k].astype(F32)
        o_ref[...] = acc

    return pl.pallas_call(
        body, name="final_half",
        grid_spec=pltpu.PrefetchScalarGridSpec(
            num_scalar_prefetch=1, grid=(nrow,),
            in_specs=[pl.BlockSpec((1, RED_ROWS, PACK_COLS), lambda i, idx: (idx[0], idx[1] * nrow + i, 0)),
                      pl.BlockSpec((1, RED_ROWS, PACK_COLS), lambda i, idx: (idx[0], i, 0)),
                      pl.BlockSpec((3, RED_ROWS, PACK_COLS), lambda i, idx: (0, i, 0))],
            out_specs=pl.BlockSpec((RED_ROWS, PACK_COLS), lambda i, idx: (i, 0))),
        out_shape=jax.ShapeDtypeStruct((PACK_HALF, PACK_COLS), F32),
        compiler_params=_params(("parallel",)),
    )(idx, gb, sib, recv)


def _share_halves(half):
    def body(h_ref, out_ref, send_sem, recv_sem, local_sem):
        x, y, c = _me()
        dst = out_ref.at[pl.ds(c * PACK_HALF, PACK_HALF), :]
        mine = pltpu.make_async_copy(h_ref, dst, local_sem)
        mine.start()
        cp = pltpu.make_async_remote_copy(src_ref=h_ref, dst_ref=dst, send_sem=send_sem, recv_sem=recv_sem,
                                          device_id=(x, y, 1 - c), device_id_type=MESH)
        cp.start()
        pltpu.make_async_remote_copy(src_ref=h_ref, dst_ref=out_ref.at[pl.ds((1 - c) * PACK_HALF, PACK_HALF), :],
                                     send_sem=send_sem, recv_sem=recv_sem, device_id=(x, y, c),
                                     device_id_type=MESH).wait_recv()
        cp.wait_send()
        mine.wait()

    return pl.pallas_call(
        body, name="share_halves", in_specs=[ANY], out_specs=ANY,
        out_shape=jax.ShapeDtypeStruct((PACK_ROWS, PACK_COLS), F32),
        scratch_shapes=[pltpu.SemaphoreType.DMA, pltpu.SemaphoreType.DMA, pltpu.SemaphoreType.DMA],
    )(half)


def _sum_small(allsmall):
    def body(a_ref, o_ref):
        acc = a_ref[0]
        for k in range(1, N_DEV):
            acc = acc + a_ref[k]
        o_ref[...] = acc

    tr = 96
    return pl.pallas_call(
        body, name="sum_small", grid=(SMALL_ROWS // tr,),
        in_specs=[pl.BlockSpec((N_DEV, tr, PACK_COLS), lambda i: (0, i, 0))],
        out_specs=pl.BlockSpec((tr, PACK_COLS), lambda i: (i, 0)),
        out_shape=jax.ShapeDtypeStruct((SMALL_ROWS, PACK_COLS), F32),
        compiler_params=_params(("parallel",)),
    )(allsmall)


def _adamw(w, g, m, v, *, name):
    shape = w.shape
    cols = shape[-1]
    as2 = lambda t: t.reshape(-1, cols)
    w2, g2, m2, v2 = as2(w), as2(g), as2(m), as2(v)
    rows = w2.shape[0]
    tr = rows
    if rows * cols * 4 > (1 << 20):
        tr = _tile(rows, max(8, (1 << 20) // (cols * 4) // 8 * 8), 8)

    def body(w_ref, g_ref, m_ref, v_ref, d_ref, mo_ref, vo_ref):
        gg = g_ref[...]
        mn = ADAM_B1 * m_ref[...] + (1.0 - ADAM_B1) * gg
        vn = ADAM_B2 * v_ref[...] + (1.0 - ADAM_B2) * (gg * gg)
        m_hat = mn / (1.0 - ADAM_B1 ** ADAM_STEP)
        v_hat = vn / (1.0 - ADAM_B2 ** ADAM_STEP)
        d_ref[...] = -ADAM_LR * (m_hat / (jnp.sqrt(v_hat) + ADAM_EPS) + ADAM_WD * w_ref[...])
        mo_ref[...] = mn
        vo_ref[...] = vn

    blk = pl.BlockSpec((tr, cols), lambda i: (i, 0))
    outs = pl.pallas_call(
        body, name=name, grid=(rows // tr,), in_specs=[blk] * 4, out_specs=[blk] * 3,
        out_shape=[jax.ShapeDtypeStruct((rows, cols), F32)] * 3,
        compiler_params=_params(("parallel",)),
    )(w2, g2, m2, v2)
    return tuple(t.reshape(shape) for t in outs)


def kernel(x, p, rel_bias, norm_attn_g, w_in, sgu_ln_g, sgu_ln_b, sgu_w, sgu_b, ssm_a_re, ssm_a_im, ssm_log_dt, ssm_b_re, ssm_b_im, ssm_c_re, ssm_c_im, ssm_d, ssm_glu_w, ssm_glu_b, branch_norm_g, w_out, norm_ffn_g, ffn_w_up, ffn_conv_w, ffn_conv_b, ffn_w_down, norm_ple_g, ple_w_gate, ple_w_proj, final_norm_g, loss_target, m_rel_bias, m_norm_attn_g, m_w_in, m_sgu_ln_g, m_sgu_ln_b, m_sgu_w, m_sgu_b, m_ssm_a_re, m_ssm_a_im, m_ssm_log_dt, m_ssm_b_re, m_ssm_b_im, m_ssm_c_re, m_ssm_c_im, m_ssm_d, m_ssm_glu_w, m_ssm_glu_b, m_branch_norm_g, m_w_out, m_norm_ffn_g, m_ffn_w_up, m_ffn_conv_w, m_ffn_conv_b, m_ffn_w_down, m_norm_ple_g, m_ple_w_gate, m_ple_w_proj, m_final_norm_g, v_rel_bias, v_norm_attn_g, v_w_in, v_sgu_ln_g, v_sgu_ln_b, v_sgu_w, v_sgu_b, v_ssm_a_re, v_ssm_a_im, v_ssm_log_dt, v_ssm_b_re, v_ssm_b_im, v_ssm_c_re, v_ssm_c_im, v_ssm_d, v_ssm_glu_w, v_ssm_glu_b, v_branch_norm_g, v_w_out, v_norm_ffn_g, v_ffn_w_up, v_ffn_conv_w, v_ffn_conv_b, v_ffn_w_down, v_norm_ple_g, v_ple_w_gate, v_ple_w_proj, v_final_norm_g):
    args = dict(locals())
    wts = {n: args[n] for n in WEIGHT_NAMES}
    mom_m = {n: args["m_" + n] for n in WEIGHT_NAMES}
    mom_v = {n: args["v_" + n] for n in WEIGHT_NAMES}

    wall = _gather_weights(_pack_shards({n: wts[n] for n in BIG_NAMES}, MXU_DTYPE, exact=True))
    full = dict(wts)
    for n in BIG_NAMES:
        full[n] = _join_shards(jnp.stack([_unpack_shard(wall[k], n, exact=True) for k in range(N_CHIPS)]), n)
    full["ffn_conv_w"] = full["ffn_conv_w"].astype(F32)

    loss, dx, grads = _local_step(x[0], p[:, 0], loss_target[0], full)
    loss = lax.psum(loss[0, 0], MESH_AXES)

    xi, yi, ci = _me()
    stacked = {n: _split_full(grads[n], n) for n in BIG_NAMES}
    gb = jnp.stack([_pack_shards({n: stacked[n][k] for n in BIG_NAMES}, MXU_DTYPE) for k in range(N_CHIPS)])
    gs = _pack_small(grads)
    sib, allsmall = _exchange_partials(gb, gs)
    pc = _chip_partials(gb, sib, jnp.stack([ci]).astype(jnp.int32))
    recv = _scatter_partials(pc)
    half = _final_half(gb, sib, recv, jnp.stack([2 * xi + yi, ci]).astype(jnp.int32))
    gflat = _share_halves(half)
    gsmall = _unpack_small(_sum_small(allsmall), _small_shapes(wts))

    g_out, d_out, m_out, v_out = {}, {}, {}, {}
    for n in BIG_NAMES:
        g_out[n] = _unpack_shard(gflat, n)
        d_out[n], m_out[n], v_out[n] = _adamw(wts[n], g_out[n], mom_m[n], mom_v[n], name="adamw_" + n)
    sw = _pack_small(wts)
    d_s, m_s, v_s = _adamw(sw, _pack_small(gsmall), _pack_small(mom_m), _pack_small(mom_v), name="adamw_small")
    shapes = _small_shapes(wts)
    d_sm, m_sm, v_sm = _unpack_small(d_s, shapes), _unpack_small(m_s, shapes), _unpack_small(v_s, shapes)
    for n in SMALL_NAMES:
        g_out[n], d_out[n], m_out[n], v_out[n] = gsmall[n], d_sm[n], m_sm[n], v_sm[n]

    return (loss, dx[None], *[g_out[n] for n in WEIGHT_NAMES], *[d_out[n] for n in WEIGHT_NAMES],
            *[m_out[n] for n in WEIGHT_NAMES], *[v_out[n] for n in WEIGHT_NAMES])
```

```python
import functools
import math

import numpy as np
import jax
import jax.numpy as jnp
from jax import lax
from jax.experimental import pallas as pl
from jax.experimental.pallas import tpu as pltpu

F32 = jnp.float32
MXU_DTYPE = jnp.bfloat16
VMEM_LIMIT_BYTES = 52 * 1024 * 1024

D_MODEL = 1024
DEPTH = 2
PLE_DIM = 256
HEAD_DIM = 64
N_HEADS = 8
ATTN_W = 512
QBLK = 128
BRANCH_DIL = (1, 4, 16)
N_BUCKETS = 32
REL_MAX_DIST = 2048
SGU_W = 256
SGU_G = 4
SGU_GW = 64
SGU_CHUNK = 128
SSM_W = 256
SSM_G = 16
SSM_C = 16
SSM_N = 64
NSTATE = SSM_G * SSM_N
D_FF = 2816
EPS = 1e-6
NEG_INF = -1e30
ATTN_SCALE = HEAD_DIM ** -0.5

ADAM_LR = 0.001
ADAM_B1 = 0.9
ADAM_B2 = 0.999
ADAM_EPS = 1e-08
ADAM_WD = 0.01
ADAM_STEP = 10

SSM_NSEG = 8
SSM_TSEG = 64
SSM_TB = SSM_NSEG * SSM_TSEG
SSM_LANE_CHUNK = 512

MESH_AXES = ("x", "y", "c")
N_CHIPS = 4
N_DEV = 8

BIG_NAMES = ("w_in", "ssm_glu_w", "w_out", "ffn_w_up", "ffn_conv_w", "ffn_w_down", "ple_w_gate", "ple_w_proj")
BIG_FULL = {
    "w_in": ((D_MODEL, 2304), 2),
    "ssm_glu_w": ((SSM_W, SSM_W), 1),
    "w_out": ((D_MODEL, D_MODEL), 1),
    "ffn_w_up": ((D_MODEL, 2 * D_FF), 2),
    "ffn_conv_w": ((3, 2 * D_FF), 2),
    "ffn_w_down": ((D_FF, D_MODEL), 1),
    "ple_w_gate": ((D_MODEL, D_MODEL), 1),
    "ple_w_proj": ((PLE_DIM, D_MODEL), 2),
}
PACK_COLS = 1024
PACK_ROWS = 6656
PACK_HALF = PACK_ROWS // 2

SMALL_NAMES = ("rel_bias", "norm_attn_g", "sgu_ln_g", "sgu_ln_b", "sgu_w", "sgu_b", "ssm_a_re", "ssm_a_im",
               "ssm_log_dt", "ssm_b_re", "ssm_b_im", "ssm_c_re", "ssm_c_im", "ssm_d", "ssm_glu_b",
               "branch_norm_g", "norm_ffn_g", "ffn_conv_b", "norm_ple_g", "final_norm_g")
SMALL_ROWS = 288

WEIGHT_NAMES = ("rel_bias", "norm_attn_g", "w_in", "sgu_ln_g", "sgu_ln_b", "sgu_w", "sgu_b", "ssm_a_re", "ssm_a_im",
                "ssm_log_dt", "ssm_b_re", "ssm_b_im", "ssm_c_re", "ssm_c_im", "ssm_d", "ssm_glu_w", "ssm_glu_b",
                "branch_norm_g", "w_out", "norm_ffn_g", "ffn_w_up", "ffn_conv_w", "ffn_conv_b", "ffn_w_down",
                "norm_ple_g", "ple_w_gate", "ple_w_proj", "final_norm_g")


def _params(sem):
    return pltpu.CompilerParams(dimension_semantics=sem, vmem_limit_bytes=VMEM_LIMIT_BYTES)


def _tile(n, cap, mult=128):
    if n <= cap:
        return n
    best = None
    for t in range(mult, cap + 1, mult):
        if n % t == 0:
            best = t
    assert best is not None, (n, cap)
    return best


def _gelu(x):
    return 0.5 * x * (1.0 + jnp.tanh(0.7978845608028654 * (x + 0.044715 * x * x * x)))


def _gelu_pair(x):
    x2 = x * x
    t = jnp.tanh(0.7978845608028654 * x * (1.0 + 0.044715 * x2))
    half = 0.5 * (1.0 + t)
    return x * half, half + 0.5 * x * (1.0 - t * t) * (0.7978845608028654 + 3.0 * 0.044715 * 0.7978845608028654 * x2)


def _dot(a, b, dims):
    return lax.dot_general(a, b, (dims, ((), ())), preferred_element_type=F32)


def _dotf(a, b, dims):
    return _dot(a.astype(MXU_DTYPE), b.astype(MXU_DTYPE), dims)


NN = ((1,), (0,))
NT = ((1,), (1,))
TN = ((0,), (0,))


def _matmul(a, b, *, name, out_dtype, tm, tn, trans_b=False, residual=None):
    m, k = a.shape
    n = b.shape[0] if trans_b else b.shape[1]
    tm = _tile(m, tm, 8)
    tn = _tile(n, tn)
    dims = NT if trans_b else NN

    def body(*refs):
        if residual is None:
            a_ref, b_ref, o_ref = refs
        else:
            a_ref, b_ref, r_ref, o_ref = refs
        acc = _dot(a_ref[...].astype(MXU_DTYPE), b_ref[...].astype(MXU_DTYPE), dims)
        if residual is not None:
            acc = acc + r_ref[...]
        o_ref[...] = acc.astype(o_ref.dtype)

    b_spec = (pl.BlockSpec((tn, k), lambda i, j: (j, 0)) if trans_b
              else pl.BlockSpec((k, tn), lambda i, j: (0, j)))
    in_specs = [pl.BlockSpec((tm, k), lambda i, j: (i, 0)), b_spec]
    args = [a, b]
    if residual is not None:
        in_specs.append(pl.BlockSpec((tm, tn), lambda i, j: (i, j)))
        args.append(residual)
    return pl.pallas_call(
        body, name=name, grid=(m // tm, n // tn), in_specs=in_specs,
        out_specs=pl.BlockSpec((tm, tn), lambda i, j: (i, j)),
        out_shape=jax.ShapeDtypeStruct((m, n), out_dtype),
        compiler_params=_params(("parallel", "parallel")),
    )(*args)


def _matmul_tn(a, g, *, name, tk, tn, tm=512):
    m, k = a.shape
    n = g.shape[1]
    tk = _tile(k, tk)
    tn = _tile(n, tn)
    tm = _tile(m, tm, 8)

    def body(a_ref, g_ref, o_ref):
        @pl.when(pl.program_id(2) == 0)
        def _():
            o_ref[...] = jnp.zeros_like(o_ref)

        o_ref[...] += _dot(a_ref[...].astype(MXU_DTYPE), g_ref[...].astype(MXU_DTYPE), TN)

    return pl.pallas_call(
        body, name=name, grid=(k // tk, n // tn, m // tm),
        in_specs=[pl.BlockSpec((tm, tk), lambda i, j, s: (s, i)),
                  pl.BlockSpec((tm, tn), lambda i, j, s: (s, j))],
        out_specs=pl.BlockSpec((tk, tn), lambda i, j, s: (i, j)),
        out_shape=jax.ShapeDtypeStruct((k, n), F32),
        compiler_params=_params(("parallel", "parallel", "arbitrary")),
    )(a, g)


ROWS = 512


def _rms_fwd(h, g, *, name):
    s, d = h.shape

    def body(h_ref, g_ref, o_ref):
        x = h_ref[...]
        r = lax.rsqrt(jnp.mean(x * x, axis=-1, keepdims=True) + EPS)
        o_ref[...] = (x * r * g_ref[...]).astype(o_ref.dtype)

    return pl.pallas_call(
        body, name=name, grid=(s // ROWS,),
        in_specs=[pl.BlockSpec((ROWS, d), lambda i: (i, 0)), pl.BlockSpec((1, d), lambda i: (0, 0))],
        out_specs=pl.BlockSpec((ROWS, d), lambda i: (i, 0)),
        out_shape=jax.ShapeDtypeStruct((s, d), MXU_DTYPE),
        compiler_params=_params(("parallel",)),
    )(h, g.reshape(1, d))


def _rms_bwd(h, g, dxn, dres, *, name):
    s, d = h.shape

    def body(h_ref, g_ref, dxn_ref, dres_ref, dh_ref, dg_ref):
        @pl.when(pl.program_id(0) == 0)
        def _():
            dg_ref[...] = jnp.zeros_like(dg_ref)

        x = h_ref[...]
        r = lax.rsqrt(jnp.mean(x * x, axis=-1, keepdims=True) + EPS)
        xhat = x * r
        dxn = dxn_ref[...].astype(F32)
        dg_ref[...] += jnp.sum(dxn * xhat, axis=0, keepdims=True)
        dxh = dxn * g_ref[...]
        dh_ref[...] = dres_ref[...] + r * (dxh - xhat * jnp.mean(dxh * xhat, axis=-1, keepdims=True))

    row = pl.BlockSpec((ROWS, d), lambda i: (i, 0))
    vec = pl.BlockSpec((1, d), lambda i: (0, 0))
    return pl.pallas_call(
        body, name=name, grid=(s // ROWS,), in_specs=[row, vec, row, row], out_specs=[row, vec],
        out_shape=[jax.ShapeDtypeStruct((s, d), F32), jax.ShapeDtypeStruct((1, d), F32)],
        compiler_params=_params(("arbitrary",)),
    )(h, g.reshape(1, d), dxn, dres)


def _loss_head(h, g, target):
    s, d = h.shape

    def body(h_ref, g_ref, t_ref, loss_ref, dh_ref, dg_ref):
        @pl.when(pl.program_id(0) == 0)
        def _():
            loss_ref[...] = jnp.zeros_like(loss_ref)
            dg_ref[...] = jnp.zeros_like(dg_ref)

        x = h_ref[...]
        r = lax.rsqrt(jnp.mean(x * x, axis=-1, keepdims=True) + EPS)
        xhat = x * r
        err = xhat * g_ref[...] - t_ref[...]
        loss_ref[...] += 0.5 * jnp.sum(jnp.mean(err * err, axis=-1, keepdims=True), axis=0, keepdims=True)
        dy = err / d
        dg_ref[...] += jnp.sum(dy * xhat, axis=0, keepdims=True)
        dxh = dy * g_ref[...]
        dh_ref[...] = r * (dxh - xhat * jnp.mean(dxh * xhat, axis=-1, keepdims=True))

    row = pl.BlockSpec((ROWS, d), lambda i: (i, 0))
    vec = pl.BlockSpec((1, d), lambda i: (0, 0))
    one = pl.BlockSpec((1, 1), lambda i: (0, 0))
    return pl.pallas_call(
        body, name="loss_head", grid=(s // ROWS,), in_specs=[row, vec, row], out_specs=[one, row, vec],
        out_shape=[jax.ShapeDtypeStruct((1, 1), F32), jax.ShapeDtypeStruct((s, d), F32),
                   jax.ShapeDtypeStruct((1, d), F32)],
        compiler_params=_params(("arbitrary",)),
    )(h, g.reshape(1, d), target)


def _t5_bucket(dist):
    max_exact = N_BUCKETS // 2
    dd = np.maximum(dist, 0)
    large = max_exact + (np.log(np.maximum(dd, 1) / max_exact) / np.log(REL_MAX_DIST / max_exact)
                         * (N_BUCKETS - max_exact)).astype(np.int32)
    large = np.minimum(large, N_BUCKETS - 1)
    return np.where(dd < max_exact, dd, large).astype(np.int32)


def _bucket_table():
    qq = np.arange(QBLK)[:, None]
    kk = np.arange(QBLK)[None, :]
    out = np.zeros((len(BRANCH_DIL), 2, QBLK, QBLK), np.int32)
    for b, dil in enumerate(BRANCH_DIL):
        out[b, 0] = _t5_bucket((qq - kk + QBLK) * dil)
        out[b, 1] = _t5_bucket((qq - kk) * dil)
    return out


def _bias_build(rel_bias):
    idx = jnp.asarray(_bucket_table())

    def body(idx_ref, rb_ref, o_ref):
        ids = idx_ref[0, 0]
        for h in range(N_HEADS):
            acc = jnp.zeros((QBLK, QBLK), F32)
            for b in range(N_BUCKETS):
                acc = jnp.where(ids == b, rb_ref[b, h], acc)
            o_ref[0, 0, h] = acc

    return pl.pallas_call(
        body, name="attn_bias_build", grid=(len(BRANCH_DIL), 2),
        in_specs=[pl.BlockSpec((1, 1, QBLK, QBLK), lambda b, p: (b, p, 0, 0)),
                  pl.BlockSpec(memory_space=pltpu.SMEM)],
        out_specs=pl.BlockSpec((1, 1, N_HEADS, QBLK, QBLK), lambda b, p: (b, p, 0, 0, 0)),
        out_shape=jax.ShapeDtypeStruct((len(BRANCH_DIL), 2, N_HEADS, QBLK, QBLK), F32),
        compiler_params=_params(("parallel", "parallel")),
    )(idx, rel_bias)


def _bias_reduce(dbias):
    idx = jnp.asarray(_bucket_table())
    nb = len(BRANCH_DIL)

    def body(idx_ref, d_ref, o_ref):
        def per_bucket(b, carry):
            for h in range(N_HEADS):
                tot = jnp.zeros((), F32)
                for br in range(nb):
                    for p in range(2):
                        tot = tot + jnp.sum(jnp.where(idx_ref[br, p] == b, d_ref[br, p, h], 0.0))
                o_ref[b, h] = tot
            return carry

        lax.fori_loop(0, N_BUCKETS, per_bucket, 0)

    return pl.pallas_call(
        body, name="attn_bias_reduce",
        in_specs=[pl.BlockSpec(memory_space=pltpu.VMEM), pl.BlockSpec(memory_space=pltpu.VMEM)],
        out_specs=pl.BlockSpec(memory_space=pltpu.SMEM),
        out_shape=jax.ShapeDtypeStruct((N_BUCKETS, N_HEADS), F32),
        compiler_params=pltpu.CompilerParams(vmem_limit_bytes=VMEM_LIMIT_BYTES),
    )(idx, dbias)


def _band_masks(c):
    row = lax.broadcasted_iota(jnp.int32, (QBLK, QBLK), 0)
    col = lax.broadcasted_iota(jnp.int32, (QBLK, QBLK), 1)
    mask_cur = col <= row
    mask_prev = jnp.logical_and(col >= row, c > 0)
    return mask_prev, mask_cur


def _attn_specs(dil):
    blk = (QBLK, ATTN_W)
    q = pl.BlockSpec(blk, lambda r, c: (c, 3 * r))
    kp = pl.BlockSpec(blk, lambda r, c: (jnp.maximum(c - 1, 0), 3 * r + 1))
    kc = pl.BlockSpec(blk, lambda r, c: (c, 3 * r + 1))
    vp = pl.BlockSpec(blk, lambda r, c: (jnp.maximum(c - 1, 0), 3 * r + 2))
    vc = pl.BlockSpec(blk, lambda r, c: (c, 3 * r + 2))
    return [q, kp, kc, vp, vc]


def _attn_fwd_branch(qkv, bias, state, *, branch, last):
    dil = BRANCH_DIL[branch]
    s = qkv.shape[0]
    n = s // dil
    nblk = n // QBLK
    first = state is None

    def body(*refs):
        q_ref, kp_ref, kc_ref, vp_ref, vc_ref, b_ref = refs[:6]
        if first:
            outs = refs[6:]
        else:
            acc_ref, m_ref, l_ref = refs[6:9]
            outs = refs[9:]
        mask_prev, mask_cur = _band_masks(pl.program_id(1))
        for h in range(N_HEADS):
            sl = slice(HEAD_DIM * h, HEAD_DIM * (h + 1))
            qh = q_ref[:, sl]
            s_c = _dot(qh, kc_ref[:, sl], NT) * ATTN_SCALE + b_ref[0, 1, h]
            s_p = _dot(qh, kp_ref[:, sl], NT) * ATTN_SCALE + b_ref[0, 0, h]
            s_c = jnp.where(mask_cur, s_c, NEG_INF)
            s_p = jnp.where(mask_prev, s_p, NEG_INF)
            m_blk = jnp.maximum(jnp.max(s_c, axis=-1, keepdims=True), jnp.max(s_p, axis=-1, keepdims=True))
            if first:
                m_new = m_blk
            else:
                m_old = m_ref[:, sl][:, :1]
                m_new = jnp.maximum(m_old, m_blk)
            p_c = jnp.exp(s_c - m_new)
            p_p = jnp.exp(s_p - m_new)
            l_new = jnp.sum(p_c, axis=-1, keepdims=True) + jnp.sum(p_p, axis=-1, keepdims=True)
            acc = (_dot(p_c.astype(MXU_DTYPE), vc_ref[:, sl], NN)
                   + _dot(p_p.astype(MXU_DTYPE), vp_ref[:, sl], NN))
            if not first:
                alpha = jnp.exp(m_old - m_new)
                l_new = l_new + alpha * l_ref[:, sl][:, :1]
                acc = acc + alpha * acc_ref[:, sl]
            if last:
                outs[0][:, sl] = acc / l_new
                outs[1][:, sl] = jnp.broadcast_to(m_new + jnp.log(l_new), (QBLK, HEAD_DIM))
            else:
                outs[0][:, sl] = acc
                outs[1][:, sl] = jnp.broadcast_to(m_new, (QBLK, HEAD_DIM))
                outs[2][:, sl] = jnp.broadcast_to(l_new, (QBLK, HEAD_DIM))

    st_spec = pl.BlockSpec((QBLK, ATTN_W), lambda r, c: (c, r))
    in_specs = _attn_specs(dil) + [pl.BlockSpec((1, 2, N_HEADS, QBLK, QBLK), lambda r, c: (branch, 0, 0, 0, 0))]
    qv = qkv.reshape(n, dil * 3 * ATTN_W)
    args = [qv] * 5 + [bias]
    if not first:
        in_specs += [st_spec] * 3
        args += [t.reshape(n, dil * ATTN_W) for t in state]
    n_out = 2 if last else 3
    outs = pl.pallas_call(
        body, name=f"attn_fwd_b{branch}", grid=(dil, nblk), in_specs=in_specs,
        out_specs=[st_spec] * n_out,
        out_shape=[jax.ShapeDtypeStruct((n, dil * ATTN_W), F32)] * n_out,
        compiler_params=_params(("parallel", "parallel")),
    )(*args)
    return tuple(t.reshape(s, ATTN_W) for t in outs)


def _attn_fwd(qkv, bias):
    state = None
    for b in range(len(BRANCH_DIL)):
        state = _attn_fwd_branch(qkv, bias, state, branch=b, last=(b == len(BRANCH_DIL) - 1))
    return state


def _attn_bwd_branch(qkv, bias, o, lse, do, *, branch):
    dil = BRANCH_DIL[branch]
    s = qkv.shape[0]
    n = s // dil
    nblk = n // QBLK

    def body(q_ref, kp_ref, kc_ref, vp_ref, vc_ref, b_ref, o_ref, l_ref, do_ref,
             dq_ref, dka_ref, dkb_ref, dva_ref, dvb_ref, db_ref):
        @pl.when(jnp.logical_and(pl.program_id(0) == 0, pl.program_id(1) == 0))
        def _():
            db_ref[...] = jnp.zeros_like(db_ref)

        mask_prev, mask_cur = _band_masks(pl.program_id(1))
        for h in range(N_HEADS):
            sl = slice(HEAD_DIM * h, HEAD_DIM * (h + 1))
            qh = q_ref[:, sl]
            doh = do_ref[:, sl]
            lh = l_ref[:, sl][:, :1]
            delta = jnp.sum(doh * o_ref[:, sl], axis=-1, keepdims=True)
            do_m = doh.astype(MXU_DTYPE)
            s_c = _dot(qh, kc_ref[:, sl], NT) * ATTN_SCALE + b_ref[0, 1, h]
            s_p = _dot(qh, kp_ref[:, sl], NT) * ATTN_SCALE + b_ref[0, 0, h]
            p_c = jnp.exp(jnp.where(mask_cur, s_c, NEG_INF) - lh)
            p_p = jnp.exp(jnp.where(mask_prev, s_p, NEG_INF) - lh)
            ds_c = p_c * (_dot(do_m, vc_ref[:, sl], NT) - delta)
            ds_p = p_p * (_dot(do_m, vp_ref[:, sl], NT) - delta)
            db_ref[0, 1, h] += ds_c
            db_ref[0, 0, h] += ds_p
            ds_c_m = ds_c.astype(MXU_DTYPE)
            ds_p_m = ds_p.astype(MXU_DTYPE)
            dq = _dot(ds_c_m, kc_ref[:, sl], NN) + _dot(ds_p_m, kp_ref[:, sl], NN)
            dq_ref[:, sl] = (dq * ATTN_SCALE).astype(dq_ref.dtype)
            dka_ref[:, sl] = (_dot(ds_c_m, qh, TN) * ATTN_SCALE).astype(dka_ref.dtype)
            dkb_ref[:, sl] = (_dot(ds_p_m, qh, TN) * ATTN_SCALE).astype(dkb_ref.dtype)
            dva_ref[:, sl] = _dot(p_c.astype(MXU_DTYPE), do_m, TN).astype(dva_ref.dtype)
            dvb_ref[:, sl] = _dot(p_p.astype(MXU_DTYPE), do_m, TN).astype(dvb_ref.dtype)

    st_spec = pl.BlockSpec((QBLK, ATTN_W), lambda r, c: (c, r))
    b_in = pl.BlockSpec((1, 2, N_HEADS, QBLK, QBLK), lambda r, c: (branch, 0, 0, 0, 0))
    b_out = pl.BlockSpec((1, 2, N_HEADS, QBLK, QBLK), lambda r, c: (0, 0, 0, 0, 0))
    qv = qkv.reshape(n, dil * 3 * ATTN_W)
    view = lambda t: t.reshape(n, dil * ATTN_W)
    outs = pl.pallas_call(
        body, name=f"attn_bwd_b{branch}", grid=(dil, nblk),
        in_specs=_attn_specs(dil) + [b_in, st_spec, st_spec, st_spec],
        out_specs=[st_spec] * 5 + [b_out],
        out_shape=[jax.ShapeDtypeStruct((n, dil * ATTN_W), MXU_DTYPE)] * 5
        + [jax.ShapeDtypeStruct((1, 2, N_HEADS, QBLK, QBLK), F32)],
        compiler_params=_params(("arbitrary", "arbitrary")),
    )(qv, qv, qv, qv, qv, bias, view(o), view(lse), view(do))
    return tuple(t.reshape(s, ATTN_W) for t in outs[:5]) + (outs[5],)


def _attn_bwd(qkv, bias, o, lse, do):
    s = qkv.shape[0]
    nb = s // QBLK
    parts = [_attn_bwd_branch(qkv, bias, o, lse, do, branch=b) for b in range(len(BRANCH_DIL))]
    dbias = jnp.concatenate([p[5] for p in parts], axis=0)

    def body(*refs):
        o_ref = refs[-1]
        i = pl.program_id(0)
        dq = jnp.zeros((QBLK, ATTN_W), F32)
        dk = jnp.zeros((QBLK, ATTN_W), F32)
        dv = jnp.zeros((QBLK, ATTN_W), F32)
        for b, dil in enumerate(BRANCH_DIL):
            dq_ref, dka_ref, dkb_ref, dva_ref, dvb_ref = refs[5 * b:5 * b + 5]
            inside = i + dil < nb
            dq = dq + dq_ref[...].astype(F32)
            dk = dk + dka_ref[...].astype(F32) + jnp.where(inside, dkb_ref[...].astype(F32), 0.0)
            dv = dv + dva_ref[...].astype(F32) + jnp.where(inside, dvb_ref[...].astype(F32), 0.0)
        o_ref[:, 0:ATTN_W] = dq.astype(o_ref.dtype)
        o_ref[:, ATTN_W:2 * ATTN_W] = dk.astype(o_ref.dtype)
        o_ref[:, 2 * ATTN_W:3 * ATTN_W] = dv.astype(o_ref.dtype)

    in_specs, args = [], []
    for b, dil in enumerate(BRANCH_DIL):
        here = pl.BlockSpec((QBLK, ATTN_W), lambda i: (i, 0))
        ahead = pl.BlockSpec((QBLK, ATTN_W), functools.partial(lambda i, d: (jnp.minimum(i + d, nb - 1), 0), d=dil))
        in_specs += [here, here, ahead, here, ahead]
        args += list(parts[b][:5])
    dqkv = pl.pallas_call(
        body, name="attn_bwd_sum", grid=(nb,), in_specs=in_specs,
        out_specs=pl.BlockSpec((QBLK, 3 * ATTN_W), lambda i: (i, 0)),
        out_shape=jax.ShapeDtypeStruct((s, 3 * ATTN_W), MXU_DTYPE),
        compiler_params=_params(("parallel",)),
    )(*args)
    return dqkv, dbias


SGU_ROWS = 512


def _sgu_norm(v_g):
    mu = jnp.mean(v_g, axis=-1, keepdims=True)
    cen = v_g - mu
    var = jnp.mean(cen * cen, axis=-1, keepdims=True)
    rstd = lax.rsqrt(var + EPS)
    return cen * rstd, rstd


def _sgu_fwd(zs, ln_g, ln_b, w_mask, b_t):
    s = zs.shape[0]
    nch = SGU_ROWS // SGU_CHUNK

    def body(z_ref, g_ref, b_ref, w_ref, bt_ref, o_ref):
        gz = _gelu(z_ref[...])
        for g in range(SGU_G):
            sl = slice(SGU_GW * g, SGU_GW * (g + 1))
            u_g = gz[:, sl]
            xhat, _ = _sgu_norm(gz[:, SGU_W + SGU_GW * g:SGU_W + SGU_GW * (g + 1)])
            vn = (xhat * g_ref[:, sl] + b_ref[:, sl]).astype(MXU_DTYPE)
            wg = w_ref[g].astype(MXU_DTYPE)
            for ci in range(nch):
                rs = slice(SGU_CHUNK * ci, SGU_CHUNK * (ci + 1))
                mixed = _dot(wg, vn[rs], NN) + bt_ref[:, g:g + 1]
                o_ref[rs, sl] = u_g[rs] * mixed

    full = lambda shape: pl.BlockSpec(shape, lambda i: tuple(0 for _ in shape))
    return pl.pallas_call(
        body, name="sgu_fwd", grid=(s // SGU_ROWS,),
        in_specs=[pl.BlockSpec((SGU_ROWS, 2 * SGU_W), lambda i: (i, 0)), full((1, SGU_W)), full((1, SGU_W)),
                  full((SGU_G, SGU_CHUNK, SGU_CHUNK)), full((SGU_CHUNK, SGU_G))],
        out_specs=pl.BlockSpec((SGU_ROWS, SGU_W), lambda i: (i, 0)),
        out_shape=jax.ShapeDtypeStruct((s, SGU_W), F32),
        compiler_params=_params(("parallel",)),
    )(zs, ln_g.reshape(1, SGU_W), ln_b.reshape(1, SGU_W), w_mask, b_t)


def _sgu_bwd(zs, ln_g, ln_b, w_mask, b_t, dy):
    s = zs.shape[0]
    nch = SGU_ROWS // SGU_CHUNK

    def body(z_ref, g_ref, b_ref, w_ref, bt_ref, dy_ref, dz_ref, dg_ref, dbb_ref, dw_ref, dbt_ref):
        @pl.when(pl.program_id(0) == 0)
        def _():
            dg_ref[...] = jnp.zeros_like(dg_ref)
            dbb_ref[...] = jnp.zeros_like(dbb_ref)
            dw_ref[...] = jnp.zeros_like(dw_ref)
            dbt_ref[...] = jnp.zeros_like(dbt_ref)

        z = z_ref[...]
        gz, dgelu = _gelu_pair(z)
        dy = dy_ref[...]
        for g in range(SGU_G):
            sl = slice(SGU_GW * g, SGU_GW * (g + 1))
            sv = slice(SGU_W + SGU_GW * g, SGU_W + SGU_GW * (g + 1))
            u_g = gz[:, sl]
            xhat, rstd = _sgu_norm(gz[:, sv])
            gain = g_ref[:, sl]
            vn = (xhat * gain + b_ref[:, sl]).astype(MXU_DTYPE)
            wg = w_ref[g].astype(MXU_DTYPE)
            dy_g = dy[:, sl]
            dvn_parts = []
            for ci in range(nch):
                rs = slice(SGU_CHUNK * ci, SGU_CHUNK * (ci + 1))
                mixed = _dot(wg, vn[rs], NN) + bt_ref[:, g:g + 1]
                dz_ref[rs, sl] = (dy_g[rs] * mixed * dgelu[rs, sl]).astype(dz_ref.dtype)
                dmixed = dy_g[rs] * u_g[rs]
                dm = dmixed.astype(MXU_DTYPE)
                dvn_parts.append(_dot(wg, dm, TN))
                dw_ref[g] += _dot(dm, vn[rs], NT)
                dbt_ref[:, g:g + 1] += jnp.sum(dmixed, axis=-1, keepdims=True)
            dvn = jnp.concatenate(dvn_parts, axis=0)
            dg_ref[:, sl] += jnp.sum(dvn * xhat, axis=0, keepdims=True)
            dbb_ref[:, sl] += jnp.sum(dvn, axis=0, keepdims=True)
            dxh = dvn * gain
            dv = rstd * (dxh - jnp.mean(dxh, axis=-1, keepdims=True)
                         - xhat * jnp.mean(dxh * xhat, axis=-1, keepdims=True))
            dz_ref[:, sv] = (dv * dgelu[:, sv]).astype(dz_ref.dtype)

    full = lambda shape: pl.BlockSpec(shape, lambda i: tuple(0 for _ in shape))
    return pl.pallas_call(
        body, name="sgu_bwd", grid=(s // SGU_ROWS,),
        in_specs=[pl.BlockSpec((SGU_ROWS, 2 * SGU_W), lambda i: (i, 0)), full((1, SGU_W)), full((1, SGU_W)),
                  full((SGU_G, SGU_CHUNK, SGU_CHUNK)), full((SGU_CHUNK, SGU_G)),
                  pl.BlockSpec((SGU_ROWS, SGU_W), lambda i: (i, 0))],
        out_specs=[pl.BlockSpec((SGU_ROWS, 2 * SGU_W), lambda i: (i, 0)), full((1, SGU_W)), full((1, SGU_W)),
                   full((SGU_G, SGU_CHUNK, SGU_CHUNK)), full((SGU_CHUNK, SGU_G))],
        out_shape=[jax.ShapeDtypeStruct((s, 2 * SGU_W), MXU_DTYPE), jax.ShapeDtypeStruct((1, SGU_W), F32),
                   jax.ShapeDtypeStruct((1, SGU_W), F32), jax.ShapeDtypeStruct((SGU_G, SGU_CHUNK, SGU_CHUNK), F32),
                   jax.ShapeDtypeStruct((SGU_CHUNK, SGU_G), F32)],
        compiler_params=_params(("arbitrary",)),
    )(zs, ln_g.reshape(1, SGU_W), ln_b.reshape(1, SGU_W), w_mask, b_t, dy)


def _ssm_discretize(a_re, a_im, log_dt, b_re, b_im):
    dt = jnp.exp(log_dt)[:, None]
    mag = jnp.exp(a_re * dt)
    ab_re = mag * jnp.cos(a_im * dt)
    ab_im = mag * jnp.sin(a_im * dt)
    den = a_re * a_re + a_im * a_im
    f_re = ((ab_re - 1.0) * a_re + ab_im * a_im) / den
    f_im = (ab_im * a_re - (ab_re - 1.0) * a_im) / den
    bb_re = f_re[:, :, None] * b_re - f_im[:, :, None] * b_im
    bb_im = f_re[:, :, None] * b_im + f_im[:, :, None] * b_re
    return ab_re, ab_im, bb_re, bb_im


def _ssm_operands(a_re, a_im, log_dt, b_re, b_im, c_re, c_im):
    ab_re, ab_im, bb_re, bb_im = _ssm_discretize(a_re, a_im, log_dt, b_re, b_im)
    eye = jnp.eye(SSM_G, dtype=F32)
    b_blk = jnp.einsum("pgnc,gh->gcphn", jnp.stack([bb_re, bb_im]), eye).reshape(SSM_W, 2 * NSTATE)
    c_mat = jnp.einsum("pgcn,gh->pgnhc", jnp.stack([c_re, -c_im]), eye).reshape(2 * NSTATE, SSM_W)
    a_row = jnp.stack([ab_re.reshape(NSTATE), ab_im.reshape(NSTATE)])
    p_re, p_im = a_row[0:1], a_row[1:2]
    while p_re.shape[0] < SSM_TSEG:
        l_re, l_im = p_re[-1:], p_im[-1:]
        p_re, p_im = (jnp.concatenate([p_re, p_re * l_re - p_im * l_im]),
                      jnp.concatenate([p_im, p_re * l_im + p_im * l_re]))
    p_tab = jnp.stack([p_re, p_im])
    return b_blk.astype(MXU_DTYPE), c_mat.astype(MXU_DTYPE), a_row, p_tab


def _lane_chunks():
    return [(lo, lo + SSM_LANE_CHUNK) for lo in range(0, NSTATE, SSM_LANE_CHUNK)]


def _seg_rows(j):
    return pl.ds(pl.multiple_of(j * SSM_NSEG, SSM_NSEG), SSM_NSEG)


def _to_segments(t):
    s, w = t.shape
    return t.reshape(s // SSM_TB, SSM_NSEG, SSM_TSEG, w).transpose(0, 2, 1, 3).reshape(s, w)


def _from_segments(t):
    s, w = t.shape
    return t.reshape(s // SSM_TB, SSM_TSEG, SSM_NSEG, w).transpose(0, 2, 1, 3).reshape(s, w)


def _ssm_local_scan(buf, a_ref, *, reverse):
    ends_re, ends_im = [], []
    for lo, hi in _lane_chunks():
        are = jnp.broadcast_to(a_ref[0:1, lo:hi], (SSM_NSEG, hi - lo))
        aim = jnp.broadcast_to(a_ref[1:2, lo:hi], (SSM_NSEG, hi - lo))
        if reverse:
            aim = -aim

        def step(jj, carry, lo=lo, hi=hi, are=are, aim=aim):
            xr, xi = carry
            j = (SSM_TSEG - 1 - jj) if reverse else jj
            tr = buf[_seg_rows(j), lo:hi]
            ti = buf[_seg_rows(j), NSTATE + lo:NSTATE + hi]
            nr = are * xr - aim * xi + tr
            ni = are * xi + aim * xr + ti
            buf[_seg_rows(j), lo:hi] = nr
            buf[_seg_rows(j), NSTATE + lo:NSTATE + hi] = ni
            return nr, ni

        zero = jnp.zeros((SSM_NSEG, hi - lo), F32)
        xr, xi = lax.fori_loop(0, SSM_TSEG, step, (zero, zero), unroll=4)
        ends_re.append(xr)
        ends_im.append(xi)
    return jnp.concatenate(ends_re, axis=1), jnp.concatenate(ends_im, axis=1)


def _ssm_entry_states(ends_re, ends_im, carry_ref, p_ref, entry_ref, *, reverse):
    at_re = p_ref[0, SSM_TSEG - 1:SSM_TSEG, :]
    at_im = p_ref[1, SSM_TSEG - 1:SSM_TSEG, :]
    if reverse:
        at_im = -at_im
    cur_re = carry_ref[0:1, 0:NSTATE]
    cur_im = carry_ref[0:1, NSTATE:2 * NSTATE]
    order = range(SSM_NSEG - 1, -1, -1) if reverse else range(SSM_NSEG)
    for i in order:
        entry_ref[0, i:i + 1, 0:NSTATE] = cur_re
        entry_ref[0, i:i + 1, NSTATE:2 * NSTATE] = cur_im
        nxt_re = ends_re[i:i + 1] + at_re * cur_re - at_im * cur_im
        nxt_im = ends_im[i:i + 1] + at_re * cur_im + at_im * cur_re
        cur_re, cur_im = nxt_re, nxt_im
    carry_ref[0:1, 0:NSTATE] = cur_re
    carry_ref[0:1, NSTATE:2 * NSTATE] = cur_im


def _ssm_fixup(buf, p_ref, entry_ref, *, reverse):
    for lo, hi in _lane_chunks():
        e_re = entry_ref[0, :, lo:hi]
        e_im = entry_ref[0, :, NSTATE + lo:NSTATE + hi]

        def step(j, carry, lo=lo, hi=hi, e_re=e_re, e_im=e_im):
            jp = (SSM_TSEG - 1 - j) if reverse else j
            pr = p_ref[0, pl.ds(jp, 1), lo:hi]
            pi = p_ref[1, pl.ds(jp, 1), lo:hi]
            if reverse:
                pi = -pi
            buf[_seg_rows(j), lo:hi] = buf[_seg_rows(j), lo:hi] + pr * e_re - pi * e_im
            buf[_seg_rows(j), NSTATE + lo:NSTATE + hi] = (buf[_seg_rows(j), NSTATE + lo:NSTATE + hi]
                                                           + pr * e_im + pi * e_re)
            return carry

        lax.fori_loop(0, SSM_TSEG, step, 0, unroll=4)


def _ssm_fwd(u, ops, d_skip, glu_w, glu_b):
    b_blk, c_mat, a_row, p_tab = ops
    s = u.shape[0]
    nblk = s // SSM_TB

    def body(u_ref, bb_ref, cm_ref, a_ref, p_ref, d_ref, gw_ref, gb_ref, y_ref, entry_ref, xbuf, carry):
        @pl.when(pl.program_id(0) == 0)
        def _():
            carry[...] = jnp.zeros_like(carry)

        uu = u_ref[...]
        xbuf[...] = _dotf(uu, bb_ref[...], NN)
        ends_re, ends_im = _ssm_local_scan(xbuf, a_ref, reverse=False)
        _ssm_entry_states(ends_re, ends_im, carry, p_ref, entry_ref, reverse=False)
        _ssm_fixup(xbuf, p_ref, entry_ref, reverse=False)
        y = _dotf(xbuf[...],cm_ref[...], NN) + d_ref[...] * uu
        y2 = _gelu(y)
        gate = jax.nn.sigmoid(_dot(y2.astype(MXU_DTYPE), gw_ref[...].astype(MXU_DTYPE), NN) + gb_ref[...])
        y_ref[...] = y2 * gate

    full = lambda shape: pl.BlockSpec(shape, lambda i: tuple(0 for _ in shape))
    y_seg, entry = pl.pallas_call(
        body, name="ssm_fwd", grid=(nblk,),
        in_specs=[pl.BlockSpec((SSM_TB, SSM_W), lambda i: (i, 0)), full(b_blk.shape), full(c_mat.shape),
                  full(a_row.shape), full(p_tab.shape), full((1, SSM_W)), full((SSM_W, SSM_W)), full((1, SSM_W))],
        out_specs=[pl.BlockSpec((SSM_TB, SSM_W), lambda i: (i, 0)),
                   pl.BlockSpec((1, SSM_NSEG, 2 * NSTATE), lambda i: (i, 0, 0))],
        out_shape=[jax.ShapeDtypeStruct((s, SSM_W), F32), jax.ShapeDtypeStruct((nblk, SSM_NSEG, 2 * NSTATE), F32)],
        scratch_shapes=[pltpu.VMEM((SSM_TB, 2 * NSTATE), F32), pltpu.VMEM((SSM_NSEG, 2 * NSTATE), F32)],
        compiler_params=_params(("arbitrary",)),
    )(_to_segments(u), b_blk, c_mat, a_row, p_tab, d_skip.reshape(1, SSM_W), glu_w, glu_b.reshape(1, SSM_W))
    return _from_segments(y_seg), entry


def _ssm_bwd(u, entry, ops, d_skip, glu_w, glu_b, dout):
    b_blk, c_mat, a_row, p_tab = ops
    s = u.shape[0]
    nblk = s // SSM_TB

    def body(u_ref, en_ref, bb_ref, cm_ref, a_ref, p_ref, d_ref, gw_ref, gb_ref, do_ref,
             du_ref, dbb_ref, dcm_ref, da_ref, dd_ref, dgw_ref, dgb_ref, xbuf, gbuf, gcarry, gentry):
        @pl.when(pl.program_id(0) == 0)
        def _():
            gcarry[...] = jnp.zeros_like(gcarry)
            for r in (dbb_ref, dcm_ref, da_ref, dd_ref, dgw_ref, dgb_ref):
                r[...] = jnp.zeros_like(r)

        uu = u_ref[...]
        xbuf[...] = _dotf(uu, bb_ref[...], NN)
        _ssm_local_scan(xbuf, a_ref, reverse=False)
        _ssm_fixup(xbuf, p_ref, en_ref, reverse=False)
        y = _dotf(xbuf[...],cm_ref[...], NN) + d_ref[...] * uu
        y2, dgelu = _gelu_pair(y)
        y2m = y2.astype(MXU_DTYPE)
        gwm = gw_ref[...].astype(MXU_DTYPE)
        gate = jax.nn.sigmoid(_dot(y2m, gwm, NN) + gb_ref[...])
        dout = do_ref[...]
        dpre = dout * y2 * gate * (1.0 - gate)
        dprem = dpre.astype(MXU_DTYPE)
        dy2 = dout * gate + _dot(dprem, gwm, NT)
        dgw_ref[...] += _dot(y2m, dprem, TN)
        dgb_ref[...] += jnp.sum(dpre, axis=0, keepdims=True)
        dy = dy2 * dgelu
        dd_ref[...] += jnp.sum(dy * uu, axis=0, keepdims=True)
        dcm_ref[...] += _dotf(xbuf[...],dy, TN)
        gbuf[...] = _dotf(dy, cm_ref[...], NT)
        gs_re, gs_im = _ssm_local_scan(gbuf, a_ref, reverse=True)
        _ssm_entry_states(gs_re, gs_im, gcarry, p_ref, gentry, reverse=True)
        _ssm_fixup(gbuf, p_ref, gentry, reverse=True)
        du_ref[...] = (_dotf(gbuf[...], bb_ref[...], NT) + d_ref[...] * dy).astype(du_ref.dtype)
        dbb_ref[...] += _dotf(uu, gbuf[...], TN)
        for lo, hi in _lane_chunks():
            def step(j, carry, lo=lo, hi=hi):
                acc_re, acc_im = carry
                g_re = gbuf[_seg_rows(j), lo:hi]
                g_im = gbuf[_seg_rows(j), NSTATE + lo:NSTATE + hi]
                x_re = xbuf[_seg_rows(j - 1), lo:hi]
                x_im = xbuf[_seg_rows(j - 1), NSTATE + lo:NSTATE + hi]
                return acc_re + g_re * x_re + g_im * x_im, acc_im + g_im * x_re - g_re * x_im

            g0_re = gbuf[_seg_rows(0), lo:hi]
            g0_im = gbuf[_seg_rows(0), NSTATE + lo:NSTATE + hi]
            e_re = en_ref[0, :, lo:hi]
            e_im = en_ref[0, :, NSTATE + lo:NSTATE + hi]
            init = (g0_re * e_re + g0_im * e_im, g0_im * e_re - g0_re * e_im)
            acc_re, acc_im = lax.fori_loop(1, SSM_TSEG, step, init, unroll=4)
            da_ref[0:1, lo:hi] += jnp.sum(acc_re, axis=0, keepdims=True)
            da_ref[1:2, lo:hi] += jnp.sum(acc_im, axis=0, keepdims=True)

    full = lambda shape: pl.BlockSpec(shape, lambda i: tuple(0 for _ in shape))
    rev = pl.BlockSpec((SSM_TB, SSM_W), lambda i: (nblk - 1 - i, 0))
    outs = pl.pallas_call(
        body, name="ssm_bwd", grid=(nblk,),
        in_specs=[rev, pl.BlockSpec((1, SSM_NSEG, 2 * NSTATE), lambda i: (nblk - 1 - i, 0, 0)),
                  full(b_blk.shape), full(c_mat.shape), full(a_row.shape), full(p_tab.shape),
                  full((1, SSM_W)), full((SSM_W, SSM_W)), full((1, SSM_W)), rev],
        out_specs=[rev, full(b_blk.shape), full(c_mat.shape), full(a_row.shape), full((1, SSM_W)),
                   full((SSM_W, SSM_W)), full((1, SSM_W))],
        out_shape=[jax.ShapeDtypeStruct((s, SSM_W), MXU_DTYPE), jax.ShapeDtypeStruct(b_blk.shape, F32),
                   jax.ShapeDtypeStruct(c_mat.shape, F32), jax.ShapeDtypeStruct(a_row.shape, F32),
                   jax.ShapeDtypeStruct((1, SSM_W), F32), jax.ShapeDtypeStruct((SSM_W, SSM_W), F32),
                   jax.ShapeDtypeStruct((1, SSM_W), F32)],
        scratch_shapes=[pltpu.VMEM((SSM_TB, 2 * NSTATE), F32), pltpu.VMEM((SSM_TB, 2 * NSTATE), F32),
                        pltpu.VMEM((SSM_NSEG, 2 * NSTATE), F32), pltpu.VMEM((1, SSM_NSEG, 2 * NSTATE), F32)],
        compiler_params=_params(("arbitrary",)),
    )(_to_segments(u), entry, b_blk, c_mat, a_row, p_tab, d_skip.reshape(1, SSM_W), glu_w, glu_b.reshape(1, SSM_W),
      _to_segments(dout))
    return (_from_segments(outs[0]),) + tuple(outs[1:])


MIX_SEGS = ((0, ATTN_W), (ATTN_W, ATTN_W + SGU_W), (ATTN_W + SGU_W, D_MODEL))


def _mix_fwd(y_attn, y_sgu, y_ssm, gain):
    s = y_attn.shape[0]

    def body(a_ref, b_ref, c_ref, g_ref, o_ref):
        for ref, (lo, hi) in zip((a_ref, b_ref, c_ref), MIX_SEGS):
            x = ref[...]
            r = lax.rsqrt(jnp.mean(x * x, axis=-1, keepdims=True) + EPS)
            o_ref[:, lo:hi] = (x * r * g_ref[:, lo:hi]).astype(o_ref.dtype)

    row = lambda w: pl.BlockSpec((ROWS, w), lambda i: (i, 0))
    return pl.pallas_call(
        body, name="mix_fwd", grid=(s // ROWS,),
        in_specs=[row(ATTN_W), row(SGU_W), row(SSM_W), pl.BlockSpec((1, D_MODEL), lambda i: (0, 0))],
        out_specs=row(D_MODEL), out_shape=jax.ShapeDtypeStruct((s, D_MODEL), MXU_DTYPE),
        compiler_params=_params(("parallel",)),
    )(y_attn, y_sgu, y_ssm, gain.reshape(1, D_MODEL))


def _mix_bwd(y_attn, y_sgu, y_ssm, gain, dmix):
    s = y_attn.shape[0]

    def body(a_ref, b_ref, c_ref, g_ref, dm_ref, da_ref, db_ref, dc_ref, dg_ref):
        @pl.when(pl.program_id(0) == 0)
        def _():
            dg_ref[...] = jnp.zeros_like(dg_ref)

        for ref, dref, (lo, hi) in zip((a_ref, b_ref, c_ref), (da_ref, db_ref, dc_ref), MIX_SEGS):
            x = ref[...]
            r = lax.rsqrt(jnp.mean(x * x, axis=-1, keepdims=True) + EPS)
            xhat = x * r
            dm = dm_ref[:, lo:hi].astype(F32)
            dg_ref[:, lo:hi] += jnp.sum(dm * xhat, axis=0, keepdims=True)
            dxh = dm * g_ref[:, lo:hi]
            dref[...] = r * (dxh - xhat * jnp.mean(dxh * xhat, axis=-1, keepdims=True))

    row = lambda w: pl.BlockSpec((ROWS, w), lambda i: (i, 0))
    vec = pl.BlockSpec((1, D_MODEL), lambda i: (0, 0))
    return pl.pallas_call(
        body, name="mix_bwd", grid=(s // ROWS,),
        in_specs=[row(ATTN_W), row(SGU_W), row(SSM_W), vec, row(D_MODEL)],
        out_specs=[row(ATTN_W), row(SGU_W), row(SSM_W), vec],
        out_shape=[jax.ShapeDtypeStruct((s, ATTN_W), F32), jax.ShapeDtypeStruct((s, SGU_W), F32),
                   jax.ShapeDtypeStruct((s, SSM_W), F32), jax.ShapeDtypeStruct((1, D_MODEL), F32)],
        compiler_params=_params(("arbitrary",)),
    )(y_attn, y_sgu, y_ssm, gain.reshape(1, D_MODEL), dmix)


CONV_ROWS = 256
CONV_COLS = 1408
CONV_PAIR = 2 * CONV_COLS
HALO = 8


def _interleave_ff(t):
    lead = t.shape[:-1]
    nb = D_FF // CONV_COLS
    return jnp.swapaxes(t.reshape(lead + (2, nb, CONV_COLS)), -3, -2).reshape(lead + (2 * D_FF,))


def _deinterleave_ff(t):
    lead = t.shape[:-1]
    nb = D_FF // CONV_COLS
    return jnp.swapaxes(t.reshape(lead + (nb, 2, CONV_COLS)), -3, -2).reshape(lead + (2 * D_FF,))


def _causal_taps(main, halo, first):
    row = lax.broadcasted_iota(jnp.int32, (HALO, main.shape[1]), 0)
    h7 = jnp.where(first, 0.0, halo[HALO - 1:HALO, :])
    h6 = jnp.where(first, 0.0, halo[HALO - 2:HALO - 1, :])
    r1 = pltpu.roll(main, 1, 0)
    r2 = pltpu.roll(main, 2, 0)
    top1 = jnp.where(row == 0, h7, r1[0:HALO])
    top2 = jnp.where(row == 0, h6, jnp.where(row == 1, h7, r2[0:HALO]))
    return jnp.concatenate([top1, r1[HALO:]], axis=0), jnp.concatenate([top2, r2[HALO:]], axis=0)


def _conv_in_specs():
    halo_idx = lambda i: jnp.maximum(i * (CONV_ROWS // HALO) - 1, 0)
    return [pl.BlockSpec((CONV_ROWS, CONV_PAIR), lambda j, i: (i, j)),
            pl.BlockSpec((HALO, CONV_PAIR), lambda j, i: (halo_idx(i), j)),
            pl.BlockSpec((3, CONV_PAIR), lambda j, i: (0, j)),
            pl.BlockSpec((1, CONV_PAIR), lambda j, i: (0, j))]


def _ffn_act_fwd(hh, conv_w, conv_b):
    s = hh.shape[0]

    def body(m_ref, h_ref, w_ref, b_ref, o_ref):
        first = pl.program_id(1) == 0
        main = m_ref[...]
        x1, x2 = _causal_taps(main, h_ref[...], first)
        conv = w_ref[0:1, :] * x2 + w_ref[1:2, :] * x1 + w_ref[2:3, :] * main + b_ref[...]
        o_ref[...] = (_gelu(conv[:, CONV_COLS:]) * conv[:, :CONV_COLS]).astype(o_ref.dtype)

    return pl.pallas_call(
        body, name="ffn_act_fwd", grid=(D_FF // CONV_COLS, s // CONV_ROWS), in_specs=_conv_in_specs(),
        out_specs=pl.BlockSpec((CONV_ROWS, CONV_COLS), lambda j, i: (i, j)),
        out_shape=jax.ShapeDtypeStruct((s, D_FF), MXU_DTYPE),
        compiler_params=_params(("parallel", "parallel")),
    )(hh, hh, conv_w, conv_b.reshape(1, -1))


def _ffn_act_bwd(hh, conv_w, conv_b, da):
    s = hh.shape[0]

    def body(m_ref, h_ref, w_ref, b_ref, da_ref, d_ref, dw_ref, db_ref):
        first = pl.program_id(1) == 0

        @pl.when(first)
        def _():
            dw_ref[...] = jnp.zeros_like(dw_ref)
            db_ref[...] = jnp.zeros_like(db_ref)

        main = m_ref[...]
        x1, x2 = _causal_taps(main, h_ref[...], first)
        conv = w_ref[0:1, :] * x2 + w_ref[1:2, :] * x1 + w_ref[2:3, :] * main + b_ref[...]
        da = da_ref[...].astype(F32)
        act, dact = _gelu_pair(conv[:, CONV_COLS:])
        dconv = jnp.concatenate([da * act, da * conv[:, :CONV_COLS] * dact], axis=1)
        d_ref[...] = dconv.astype(d_ref.dtype)
        for t, tap in enumerate((x2, x1, main)):
            dw_ref[t:t + 1, :] += jnp.sum(dconv * tap, axis=0, keepdims=True)
        db_ref[...] += jnp.sum(dconv, axis=0, keepdims=True)

    return pl.pallas_call(
        body, name="ffn_act_bwd", grid=(D_FF // CONV_COLS, s // CONV_ROWS),
        in_specs=_conv_in_specs() + [pl.BlockSpec((CONV_ROWS, CONV_COLS), lambda j, i: (i, j))],
        out_specs=[pl.BlockSpec((CONV_ROWS, CONV_PAIR), lambda j, i: (i, j)),
                   pl.BlockSpec((3, CONV_PAIR), lambda j, i: (0, j)), pl.BlockSpec((1, CONV_PAIR), lambda j, i: (0, j))],
        out_shape=[jax.ShapeDtypeStruct((s, 2 * D_FF), MXU_DTYPE), jax.ShapeDtypeStruct((3, 2 * D_FF), F32),
                   jax.ShapeDtypeStruct((1, 2 * D_FF), F32)],
        compiler_params=_params(("parallel", "arbitrary")),
    )(hh, hh, conv_w, conv_b.reshape(1, -1), da)


def _conv_transpose(dconv, conv_w):
    s, n = dconv.shape
    nrow = s // CONV_ROWS
    halo_rows = 16

    def body(m_ref, nx_ref, w_ref, o_ref):
        main = m_ref[...].astype(F32)
        last = pl.program_id(1) == nrow - 1
        nx = nx_ref[...].astype(F32)
        n0 = jnp.where(last, 0.0, nx[0:1, :])
        n1 = jnp.where(last, 0.0, nx[1:2, :])
        row = lax.broadcasted_iota(jnp.int32, (HALO, main.shape[1]), 0)
        r1 = pltpu.roll(main, CONV_ROWS - 1, 0)
        r2 = pltpu.roll(main, CONV_ROWS - 2, 0)
        end1 = jnp.where(row == HALO - 1, n0, r1[CONV_ROWS - HALO:])
        end2 = jnp.where(row == HALO - 2, n0, jnp.where(row == HALO - 1, n1, r2[CONV_ROWS - HALO:]))
        y1 = jnp.concatenate([r1[:CONV_ROWS - HALO], end1], axis=0)
        y2 = jnp.concatenate([r2[:CONV_ROWS - HALO], end2], axis=0)
        o_ref[...] = (w_ref[2:3, :] * main + w_ref[1:2, :] * y1 + w_ref[0:1, :] * y2).astype(o_ref.dtype)

    nxt = lambda i: jnp.minimum((i + 1) * (CONV_ROWS // halo_rows), s // halo_rows - 1)
    return pl.pallas_call(
        body, name="ffn_conv_transpose", grid=(n // CONV_COLS, nrow),
        in_specs=[pl.BlockSpec((CONV_ROWS, CONV_COLS), lambda j, i: (i, j)),
                  pl.BlockSpec((halo_rows, CONV_COLS), lambda j, i: (nxt(i), j)),
                  pl.BlockSpec((3, CONV_COLS), lambda j, i: (0, j))],
        out_specs=pl.BlockSpec((CONV_ROWS, CONV_COLS), lambda j, i: (i, j)),
        out_shape=jax.ShapeDtypeStruct((s, n), MXU_DTYPE),
        compiler_params=_params(("parallel", "parallel")),
    )(dconv, dconv, conv_w)


def _ple_fwd(xn, p, w_gate, w_proj, h):
    s = xn.shape[0]
    tm = 512

    def body(x_ref, p_ref, wg_ref, wp_ref, h_ref, o_ref):
        gate = jax.nn.sigmoid(_dot(x_ref[...].astype(MXU_DTYPE), wg_ref[...].astype(MXU_DTYPE), NN))
        proj = _dot(p_ref[...].astype(MXU_DTYPE), wp_ref[...].astype(MXU_DTYPE), NN)
        o_ref[...] = h_ref[...] + gate * proj

    return pl.pallas_call(
        body, name="ple_fwd", grid=(s // tm,),
        in_specs=[pl.BlockSpec((tm, D_MODEL), lambda i: (i, 0)), pl.BlockSpec((tm, PLE_DIM), lambda i: (i, 0)),
                  pl.BlockSpec((D_MODEL, D_MODEL), lambda i: (0, 0)), pl.BlockSpec((PLE_DIM, D_MODEL), lambda i: (0, 0)),
                  pl.BlockSpec((tm, D_MODEL), lambda i: (i, 0))],
        out_specs=pl.BlockSpec((tm, D_MODEL), lambda i: (i, 0)),
        out_shape=jax.ShapeDtypeStruct((s, D_MODEL), F32),
        compiler_params=_params(("parallel",)),
    )(xn, p, w_gate, w_proj, h)


def _ple_bwd(xn, p, w_gate, w_proj, dh):
    s = xn.shape[0]
    tm = 512

    def body(x_ref, p_ref, wg_ref, wp_ref, dh_ref, dpre_ref, dproj_ref):
        gate = jax.nn.sigmoid(_dot(x_ref[...].astype(MXU_DTYPE), wg_ref[...].astype(MXU_DTYPE), NN))
        proj = _dot(p_ref[...].astype(MXU_DTYPE), wp_ref[...].astype(MXU_DTYPE), NN)
        dh = dh_ref[...]
        dpre_ref[...] = (dh * proj * gate * (1.0 - gate)).astype(dpre_ref.dtype)
        dproj_ref[...] = (dh * gate).astype(dproj_ref.dtype)

    row = pl.BlockSpec((tm, D_MODEL), lambda i: (i, 0))
    return pl.pallas_call(
        body, name="ple_bwd", grid=(s // tm,),
        in_specs=[row, pl.BlockSpec((tm, PLE_DIM), lambda i: (i, 0)),
                  pl.BlockSpec((D_MODEL, D_MODEL), lambda i: (0, 0)), pl.BlockSpec((PLE_DIM, D_MODEL), lambda i: (0, 0)),
                  row],
        out_specs=[row, row],
        out_shape=[jax.ShapeDtypeStruct((s, D_MODEL), MXU_DTYPE)] * 2,
        compiler_params=_params(("parallel",)),
    )(xn, p, w_gate, w_proj, dh)


O_SGU = 3 * ATTN_W
O_SSM = O_SGU + 2 * SGU_W


def _layer_consts(w, i):
    causal = jnp.asarray(np.tril(np.ones((SGU_CHUNK, SGU_CHUNK), np.float32)))
    return {
        "sgu_w_mask": w["sgu_w"][i] * causal,
        "sgu_b_t": w["sgu_b"][i].T,
        "ssm_ops": _ssm_operands(w["ssm_a_re"][i], w["ssm_a_im"][i], w["ssm_log_dt"][i], w["ssm_b_re"][i],
                                 w["ssm_b_im"][i], w["ssm_c_re"][i], w["ssm_c_im"][i]),
    }


def _layer_fwd(h0, p_i, w, i, bias):
    c = _layer_consts(w, i)
    w_in = w["w_in"][i]
    xn1 = _rms_fwd(h0, w["norm_attn_g"][i], name="rms_attn_fwd")
    qkv = _matmul(xn1, w_in[:, :O_SGU], name="in_proj_qkv", out_dtype=MXU_DTYPE, tm=1024, tn=1536)
    zs = _matmul(xn1, w_in[:, O_SGU:O_SSM], name="in_proj_sgu", out_dtype=F32, tm=1024, tn=512)
    us = _matmul(xn1, w_in[:, O_SSM:], name="in_proj_ssm", out_dtype=F32, tm=1024, tn=256)
    y_attn, lse = _attn_fwd(qkv, bias)
    y_sgu = _sgu_fwd(zs, w["sgu_ln_g"][i], w["sgu_ln_b"][i], c["sgu_w_mask"], c["sgu_b_t"])
    y_ssm, entry = _ssm_fwd(us, c["ssm_ops"], w["ssm_d"][i], w["ssm_glu_w"][i], w["ssm_glu_b"][i])
    mix = _mix_fwd(y_attn, y_sgu, y_ssm, w["branch_norm_g"][i])
    h1 = _matmul(mix, w["w_out"][i], name="out_proj", out_dtype=F32, tm=512, tn=1024, residual=h0)
    xn2 = _rms_fwd(h1, w["norm_ffn_g"][i], name="rms_ffn_fwd")
    hh = _matmul(xn2, w["ffn_w_up"][i], name="ffn_up", out_dtype=F32, tm=1024, tn=1408)
    act = _ffn_act_fwd(hh, w["ffn_conv_w"][i], w["ffn_conv_b"][i])
    h2 = _matmul(act, w["ffn_w_down"][i], name="ffn_down", out_dtype=F32, tm=512, tn=1024, residual=h1)
    xn3 = _rms_fwd(h2, w["norm_ple_g"][i], name="rms_ple_fwd")
    h3 = _ple_fwd(xn3, p_i, w["ple_w_gate"][i], w["ple_w_proj"][i], h2)
    saved = dict(h0=h0, xn1=xn1, qkv=qkv, zs=zs, us=us, y_attn=y_attn, lse=lse, y_sgu=y_sgu, y_ssm=y_ssm,
                 entry=entry, mix=mix, h1=h1, xn2=xn2, hh=hh, act=act, h2=h2, xn3=xn3, consts=c)
    return h3, saved


def _layer_bwd(dh3, sv, p_i, w, i, bias):
    c = sv["consts"]
    g = {}
    dpre, dproj = _ple_bwd(sv["xn3"], p_i, w["ple_w_gate"][i], w["ple_w_proj"][i], dh3)
    g["ple_w_gate"] = _matmul_tn(sv["xn3"], dpre, name="d_ple_w_gate", tk=1024, tn=1024)
    g["ple_w_proj"] = _matmul_tn(p_i, dproj, name="d_ple_w_proj", tk=256, tn=1024)
    dxn3 = _matmul(dpre, w["ple_w_gate"][i], name="d_xn_ple", out_dtype=F32, tm=512, tn=1024, trans_b=True)
    dh2, g["norm_ple_g"] = _rms_bwd(sv["h2"], w["norm_ple_g"][i], dxn3, dh3, name="rms_ple_bwd")
    g["ffn_w_down"] = _matmul_tn(sv["act"], dh2, name="d_ffn_w_down", tk=1408, tn=1024)
    dact = _matmul(dh2, w["ffn_w_down"][i], name="d_ffn_act", out_dtype=MXU_DTYPE, tm=512, tn=1408, trans_b=True)
    dconv, g["ffn_conv_w"], g["ffn_conv_b"] = _ffn_act_bwd(sv["hh"], w["ffn_conv_w"][i], w["ffn_conv_b"][i], dact)
    dhh = _conv_transpose(dconv, w["ffn_conv_w"][i])
    g["ffn_w_up"] = _matmul_tn(sv["xn2"], dhh, name="d_ffn_w_up", tk=1024, tn=1408)
    dxn2 = _matmul(dhh, w["ffn_w_up"][i], name="d_xn_ffn", out_dtype=F32, tm=512, tn=512, trans_b=True)
    dh1, g["norm_ffn_g"] = _rms_bwd(sv["h1"], w["norm_ffn_g"][i], dxn2, dh2, name="rms_ffn_bwd")
    g["w_out"] = _matmul_tn(sv["mix"], dh1, name="d_w_out", tk=1024, tn=1024)
    dmix = _matmul(dh1, w["w_out"][i], name="d_mix", out_dtype=F32, tm=512, tn=1024, trans_b=True)
    dy_attn, dy_sgu, dy_ssm, g["branch_norm_g"] = _mix_bwd(sv["y_attn"], sv["y_sgu"], sv["y_ssm"],
                                                           w["branch_norm_g"][i], dmix)
    dqkv, dbias = _attn_bwd(sv["qkv"], bias, sv["y_attn"], sv["lse"], dy_attn)
    dzs, g["sgu_ln_g"], g["sgu_ln_b"], dsw, dsb = _sgu_bwd(sv["zs"], w["sgu_ln_g"][i], w["sgu_ln_b"][i],
                                                          c["sgu_w_mask"], c["sgu_b_t"], dy_sgu)
    causal = jnp.asarray(np.tril(np.ones((SGU_CHUNK, SGU_CHUNK), np.float32)))
    g["sgu_w"] = dsw * causal
    g["sgu_b"] = dsb.T
    dus, dbb, dcm, da, g["ssm_d"], g["ssm_glu_w"], g["ssm_glu_b"] = _ssm_bwd(
        sv["us"], sv["entry"], c["ssm_ops"], w["ssm_d"][i], w["ssm_glu_w"][i], w["ssm_glu_b"][i], dy_ssm)
    dbb5 = dbb.reshape(SSM_G, SSM_C, 2, SSM_G, SSM_N)
    dbbar = jnp.einsum("gcpgn->pgnc", dbb5)
    dcm5 = dcm.reshape(2, SSM_G, SSM_N, SSM_G, SSM_C)
    dcc = jnp.einsum("pgngc->pgcn", dcm5)
    g["ssm_c_re"] = dcc[0]
    g["ssm_c_im"] = -dcc[1]
    da2 = da.reshape(2, SSM_G, SSM_N)
    _, vjp = jax.vjp(_ssm_discretize, w["ssm_a_re"][i], w["ssm_a_im"][i], w["ssm_log_dt"][i],
                     w["ssm_b_re"][i], w["ssm_b_im"][i])
    (g["ssm_a_re"], g["ssm_a_im"], g["ssm_log_dt"], g["ssm_b_re"], g["ssm_b_im"]) = vjp(
        (da2[0], da2[1], dbbar[0], dbbar[1]))
    dz = jnp.concatenate([dqkv, dzs, dus], axis=1)
    g["w_in"] = _matmul_tn(sv["xn1"], dz, name="d_w_in", tk=1024, tn=1152)
    dxn1 = _matmul(dz, w["w_in"][i], name="d_xn_attn", out_dtype=F32, tm=512, tn=1024, trans_b=True)
    dh0, g["norm_attn_g"] = _rms_bwd(sv["h0"], w["norm_attn_g"][i], dxn1, dh1, name="rms_attn_bwd")
    for k in ("norm_ple_g", "norm_ffn_g", "branch_norm_g", "norm_attn_g", "sgu_ln_g", "sgu_ln_b", "ssm_d",
              "ssm_glu_b", "ffn_conv_b"):
        g[k] = g[k].reshape(-1)
    return dh0, g, dbias


def _local_step(x, p, target, w):
    ff_names = ("ffn_w_up", "ffn_conv_w", "ffn_conv_b")
    w = dict(w)
    for k in ff_names:
        w[k] = _interleave_ff(w[k])
    bias = _bias_build(w["rel_bias"])
    h = x
    saved = []
    for i in range(DEPTH):
        h, sv = _layer_fwd(h, p[i], w, i, bias)
        saved.append(sv)
    loss, dh, dgf = _loss_head(h, w["final_norm_g"], target)
    layer_grads = [None] * DEPTH
    dbias = None
    for i in reversed(range(DEPTH)):
        dh, layer_grads[i], db = _layer_bwd(dh, saved[i], p[i], w, i, bias)
        dbias = db if dbias is None else dbias + db
    grads = {k: jnp.stack([layer_grads[i][k] for i in range(DEPTH)]) for k in layer_grads[0]}
    for k in ff_names:
        grads[k] = _deinterleave_ff(grads[k])
    grads["rel_bias"] = _bias_reduce(dbias)
    grads["final_norm_g"] = dgf.reshape(-1)
    return loss, dh, grads


def _pad_rows(a2, mult=16):
    r = (-a2.shape[0]) % mult
    return a2 if r == 0 else jnp.concatenate([a2, jnp.zeros((r, a2.shape[1]), a2.dtype)], axis=0)


def _as_rows(a, rows=None):
    flat = a.reshape(-1)
    if rows is None:
        rows = -(-flat.shape[0] // (16 * PACK_COLS)) * 16
    return jnp.pad(flat, (0, rows * PACK_COLS - flat.shape[0])).reshape(rows, PACK_COLS)


def _shard_shape(name):
    full, ax = BIG_FULL[name]
    shp = [DEPTH] + list(full)
    shp[ax] //= N_CHIPS
    return tuple(shp)


EXACT_NAMES = ("ffn_conv_w",)


def _pack_rows_of(name):
    n = int(np.prod(_shard_shape(name))) * (2 if name in EXACT_NAMES else 1)
    rows = -(-n // PACK_COLS)
    return -(-rows // 16) * 16


def _pack_shards(shards, dtype, exact=False):
    split_words = exact and jnp.dtype(dtype).itemsize == 2
    parts = []
    for n in BIG_NAMES:
        a = shards[n]
        if split_words and n in EXACT_NAMES:
            a = lax.bitcast_convert_type(a.astype(F32), dtype)
        parts.append(_as_rows(a.astype(dtype), _pack_rows_of(n)))
    used = sum(pt.shape[0] for pt in parts)
    parts.append(jnp.zeros((PACK_ROWS - used, PACK_COLS), dtype))
    return jnp.concatenate(parts, axis=0)


def _unpack_shard(flat, name, exact=False):
    off = 0
    for n in BIG_NAMES:
        if n == name:
            break
        off += _pack_rows_of(n)
    shp = _shard_shape(name)
    cnt = int(np.prod(shp))
    vec = flat[off:off + _pack_rows_of(name)].reshape(-1)
    if exact and name in EXACT_NAMES and jnp.dtype(flat.dtype).itemsize == 2:
        return lax.bitcast_convert_type(vec[:2 * cnt].reshape(shp + (2,)), F32)
    return vec[:cnt].reshape(shp)


def _split_full(full, name):
    _, ax = BIG_FULL[name]
    return jnp.stack(jnp.split(full, N_CHIPS, axis=ax))


def _join_shards(stacked, name):
    _, ax = BIG_FULL[name]
    return jnp.concatenate([stacked[k] for k in range(N_CHIPS)], axis=ax)


def _small_shapes(w):
    return [(n, w[n].shape) for n in SMALL_NAMES]


def _pack_small(d):
    flat = jnp.concatenate([d[n].astype(F32).reshape(-1) for n in SMALL_NAMES])
    flat = jnp.concatenate([flat, jnp.zeros((SMALL_ROWS * PACK_COLS - flat.shape[0],), F32)])
    return flat.reshape(SMALL_ROWS, PACK_COLS)


def _unpack_small(flat, shapes):
    out, off = {}, 0
    v = flat.reshape(-1)
    for n, shp in shapes:
        cnt = int(np.prod(shp))
        out[n] = v[off:off + cnt].reshape(shp)
        off += cnt
    return out


MESH = pl.DeviceIdType.MESH
ANY = pl.BlockSpec(memory_space=pl.ANY)


def _me():
    return lax.axis_index("x"), lax.axis_index("y"), lax.axis_index("c")


def _other_chips(x, y):
    return [(1 - x, y), (x, 1 - y), (1 - x, 1 - y)]


def _gather_weights(wflat):
    def body(w_ref, out_ref, send_sems, recv_sems, local_sem):
        x, y, c = _me()
        sibling = (x, y, 1 - c)
        chips = _other_chips(x, y)

        def rows(chip, half):
            return out_ref.at[2 * chip[0] + chip[1], pl.ds(half * PACK_HALF, PACK_HALF), :]

        def copy(k, chip, half, to, src=None):
            return pltpu.make_async_remote_copy(
                src_ref=rows(chip, half) if src is None else src, dst_ref=rows(chip, half),
                send_sem=send_sems.at[k], recv_sem=recv_sems.at[k], device_id=to, device_id_type=MESH)

        mine = pltpu.make_async_copy(w_ref, out_ref.at[2 * x + y], local_sem)
        mine.start()
        my_half = w_ref.at[pl.ds(c * PACK_HALF, PACK_HALF), :]
        first = [copy(j, (x, y), c, (*chip, c), src=my_half) for j, chip in enumerate(chips)]
        for cp in first:
            cp.start()
        passed = [copy(3 + j, chip, c, sibling) for j, chip in enumerate(chips)]
        for j, chip in enumerate(chips):
            copy(j, chip, c, (x, y, c)).wait_recv()
            passed[j].start()
        for j, chip in enumerate(chips):
            copy(3 + j, chip, 1 - c, (x, y, c)).wait_recv()
        for cp in first + passed:
            cp.wait_send()
        mine.wait()

    return pl.pallas_call(
        body, name="gather_weights", in_specs=[ANY], out_specs=ANY,
        out_shape=jax.ShapeDtypeStruct((N_CHIPS, PACK_ROWS, PACK_COLS), wflat.dtype),
        scratch_shapes=[pltpu.SemaphoreType.DMA((6,)), pltpu.SemaphoreType.DMA((6,)), pltpu.SemaphoreType.DMA],
    )(wflat)


def _exchange_partials(gb, gs):
    def body(gb_ref, gs_ref, half_ref, small_ref, send_sems, recv_sems, local_sem):
        x, y, c = _me()
        me_idx = 4 * x + 2 * y + c
        mine = pltpu.make_async_copy(gs_ref, small_ref.at[me_idx], local_sem)
        mine.start()
        d2d = pltpu.make_async_remote_copy(
            src_ref=gb_ref.at[:, pl.ds((1 - c) * PACK_HALF, PACK_HALF), :], dst_ref=half_ref,
            send_sem=send_sems.at[0], recv_sem=recv_sems.at[0], device_id=(x, y, 1 - c), device_id_type=MESH)
        d2d.start()
        copies = []
        for k in range(1, N_DEV):
            fx, fy, fc = (k >> 2) & 1, (k >> 1) & 1, k & 1
            peer = (x ^ fx, y ^ fy, c ^ fc)
            copies.append(pltpu.make_async_remote_copy(
                src_ref=gs_ref, dst_ref=small_ref.at[me_idx], send_sem=send_sems.at[k], recv_sem=recv_sems.at[k],
                device_id=peer, device_id_type=MESH))
        for cp in copies:
            cp.start()
        for k in range(1, N_DEV):
            fx, fy, fc = (k >> 2) & 1, (k >> 1) & 1, k & 1
            peer_idx = 4 * (x ^ fx) + 2 * (y ^ fy) + (c ^ fc)
            pltpu.make_async_remote_copy(
                src_ref=gs_ref, dst_ref=small_ref.at[peer_idx], send_sem=send_sems.at[k], recv_sem=recv_sems.at[k],
                device_id=(x, y, c), device_id_type=MESH).wait_recv()
        d2d.wait_recv()
        d2d.wait_send()
        for cp in copies:
            cp.wait_send()
        mine.wait()

    return pl.pallas_call(
        body, name="exchange_partials", in_specs=[ANY, ANY], out_specs=[ANY, ANY],
        out_shape=[jax.ShapeDtypeStruct((N_CHIPS, PACK_HALF, PACK_COLS), gb.dtype),
                   jax.ShapeDtypeStruct((N_DEV, SMALL_ROWS, PACK_COLS), F32)],
        scratch_shapes=[pltpu.SemaphoreType.DMA((N_DEV,)), pltpu.SemaphoreType.DMA((N_DEV,)), pltpu.SemaphoreType.DMA],
    )(gb, gs)


RED_ROWS = 256


def _chip_partials(gb, sib, c_idx):
    nrow = PACK_HALF // RED_ROWS

    def body(c_ref, a_ref, b_ref, o_ref):
        del c_ref
        o_ref[...] = (a_ref[...].astype(F32) + b_ref[...].astype(F32)).astype(o_ref.dtype)

    blk = (1, RED_ROWS, PACK_COLS)
    return pl.pallas_call(
        body, name="chip_partials",
        grid_spec=pltpu.PrefetchScalarGridSpec(
            num_scalar_prefetch=1, grid=(N_CHIPS, nrow),
            in_specs=[pl.BlockSpec(blk, lambda k, i, c: (k, c[0] * nrow + i, 0)),
                      pl.BlockSpec(blk, lambda k, i, c: (k, i, 0))],
            out_specs=pl.BlockSpec(blk, lambda k, i, c: (k, i, 0))),
        out_shape=jax.ShapeDtypeStruct((N_CHIPS, PACK_HALF, PACK_COLS), gb.dtype),
        compiler_params=_params(("parallel", "parallel")),
    )(c_idx, gb, sib)


def _scatter_partials(pc):
    def body(pc_ref, out_ref, send_sems, recv_sems):
        x, y, c = _me()
        chips = _other_chips(x, y)
        copies = [pltpu.make_async_remote_copy(
            src_ref=pc_ref.at[2 * chip[0] + chip[1]], dst_ref=out_ref.at[k],
            send_sem=send_sems.at[k], recv_sem=recv_sems.at[k], device_id=(*chip, c), device_id_type=MESH)
            for k, chip in enumerate(chips)]
        for cp in copies:
            cp.start()
        for cp in copies:
            cp.wait_recv()
        for cp in copies:
            cp.wait_send()

    return pl.pallas_call(
        body, name="scatter_partials", in_specs=[ANY], out_specs=ANY,
        out_shape=jax.ShapeDtypeStruct((3, PACK_HALF, PACK_COLS), pc.dtype),
        scratch_shapes=[pltpu.SemaphoreType.DMA((3,)), pltpu.SemaphoreType.DMA((3,))],
    )(pc)


def _final_half(gb, sib, recv, idx):
    nrow = PACK_HALF // RED_ROWS

    def body(idx_ref, a_ref, b_ref, r_ref, o_ref):
        del idx_ref
        acc = a_ref[0].astype(F32) + b_ref[0].astype(F32)
        for k in range(3):
            acc = acc + r_ref[k].astype(F32)
        o_ref[...] = acc

    return pl.pallas_call(
        body, name="final_half",
        grid_spec=pltpu.PrefetchScalarGridSpec(
            num_scalar_prefetch=1, grid=(nrow,),
            in_specs=[pl.BlockSpec((1, RED_ROWS, PACK_COLS), lambda i, idx: (idx[0], idx[1] * nrow + i, 0)),
                      pl.BlockSpec((1, RED_ROWS, PACK_COLS), lambda i, idx: (idx[0], i, 0)),
                      pl.BlockSpec((3, RED_ROWS, PACK_COLS), lambda i, idx: (0, i, 0))],
            out_specs=pl.BlockSpec((RED_ROWS, PACK_COLS), lambda i, idx: (i, 0))),
        out_shape=jax.ShapeDtypeStruct((PACK_HALF, PACK_COLS), F32),
        compiler_params=_params(("parallel",)),
    )(idx, gb, sib, recv)


def _share_halves(half):
    def body(h_ref, out_ref, send_sem, recv_sem, local_sem):
        x, y, c = _me()
        dst = out_ref.at[pl.ds(c * PACK_HALF, PACK_HALF), :]
        mine = pltpu.make_async_copy(h_ref, dst, local_sem)
        mine.start()
        cp = pltpu.make_async_remote_copy(src_ref=h_ref, dst_ref=dst, send_sem=send_sem, recv_sem=recv_sem,
                                          device_id=(x, y, 1 - c), device_id_type=MESH)
        cp.start()
        pltpu.make_async_remote_copy(src_ref=h_ref, dst_ref=out_ref.at[pl.ds((1 - c) * PACK_HALF, PACK_HALF), :],
                                     send_sem=send_sem, recv_sem=recv_sem, device_id=(x, y, c),
                                     device_id_type=MESH).wait_recv()
        cp.wait_send()
        mine.wait()

    return pl.pallas_call(
        body, name="share_halves", in_specs=[ANY], out_specs=ANY,
        out_shape=jax.ShapeDtypeStruct((PACK_ROWS, PACK_COLS), F32),
        scratch_shapes=[pltpu.SemaphoreType.DMA, pltpu.SemaphoreType.DMA, pltpu.SemaphoreType.DMA],
    )(half)


def _sum_small(allsmall):
    def body(a_ref, o_ref):
        acc = a_ref[0]
        for k in range(1, N_DEV):
            acc = acc + a_ref[k]
        o_ref[...] = acc

    tr = 96
    return pl.pallas_call(
        body, name="sum_small", grid=(SMALL_ROWS // tr,),
        in_specs=[pl.BlockSpec((N_DEV, tr, PACK_COLS), lambda i: (0, i, 0))],
        out_specs=pl.BlockSpec((tr, PACK_COLS), lambda i: (i, 0)),
        out_shape=jax.ShapeDtypeStruct((SMALL_ROWS, PACK_COLS), F32),
        compiler_params=_params(("parallel",)),
    )(allsmall)


def _adamw(w, g, m, v, *, name):
    shape = w.shape
    cols = shape[-1]
    as2 = lambda t: t.reshape(-1, cols)
    w2, g2, m2, v2 = as2(w), as2(g), as2(m), as2(v)
    rows = w2.shape[0]
    tr = rows
    if rows * cols * 4 > (1 << 20):
        tr = _tile(rows, max(8, (1 << 20) // (cols * 4) // 8 * 8), 8)

    def body(w_ref, g_ref, m_ref, v_ref, d_ref, mo_ref, vo_ref):
        gg = g_ref[...]
        mn = ADAM_B1 * m_ref[...] + (1.0 - ADAM_B1) * gg
        vn = ADAM_B2 * v_ref[...] + (1.0 - ADAM_B2) * (gg * gg)
        m_hat = mn / (1.0 - ADAM_B1 ** ADAM_STEP)
        v_hat = vn / (1.0 - ADAM_B2 ** ADAM_STEP)
        d_ref[...] = -ADAM_LR * (m_hat / (jnp.sqrt(v_hat) + ADAM_EPS) + ADAM_WD * w_ref[...])
        mo_ref[...] = mn
        vo_ref[...] = vn

    blk = pl.BlockSpec((tr, cols), lambda i: (i, 0))
    outs = pl.pallas_call(
        body, name=name, grid=(rows // tr,), in_specs=[blk] * 4, out_specs=[blk] * 3,
        out_shape=[jax.ShapeDtypeStruct((rows, cols), F32)] * 3,
        compiler_params=_params(("parallel",)),
    )(w2, g2, m2, v2)
    return tuple(t.reshape(shape) for t in outs)


def kernel(x, p, rel_bias, norm_attn_g, w_in, sgu_ln_g, sgu_ln_b, sgu_w, sgu_b, ssm_a_re, ssm_a_im, ssm_log_dt, ssm_b_re, ssm_b_im, ssm_c_re, ssm_c_im, ssm_d, ssm_glu_w, ssm_glu_b, branch_norm_g, w_out, norm_ffn_g, ffn_w_up, ffn_conv_w, ffn_conv_b, ffn_w_down, norm_ple_g, ple_w_gate, ple_w_proj, final_norm_g, loss_target, m_rel_bias, m_norm_attn_g, m_w_in, m_sgu_ln_g, m_sgu_ln_b, m_sgu_w, m_sgu_b, m_ssm_a_re, m_ssm_a_im, m_ssm_log_dt, m_ssm_b_re, m_ssm_b_im, m_ssm_c_re, m_ssm_c_im, m_ssm_d, m_ssm_glu_w, m_ssm_glu_b, m_branch_norm_g, m_w_out, m_norm_ffn_g, m_ffn_w_up, m_ffn_conv_w, m_ffn_conv_b, m_ffn_w_down, m_norm_ple_g, m_ple_w_gate, m_ple_w_proj, m_final_norm_g, v_rel_bias, v_norm_attn_g, v_w_in, v_sgu_ln_g, v_sgu_ln_b, v_sgu_w, v_sgu_b, v_ssm_a_re, v_ssm_a_im, v_ssm_log_dt, v_ssm_b_re, v_ssm_b_im, v_ssm_c_re, v_ssm_c_im, v_ssm_d, v_ssm_glu_w, v_ssm_glu_b, v_branch_norm_g, v_w_out, v_norm_ffn_g, v_ffn_w_up, v_ffn_conv_w, v_ffn_conv_b, v_ffn_w_down, v_norm_ple_g, v_ple_w_gate, v_ple_w_proj, v_final_norm_g):
    args = dict(locals())
    wts = {n: args[n] for n in WEIGHT_NAMES}
    mom_m = {n: args["m_" + n] for n in WEIGHT_NAMES}
    mom_v = {n: args["v_" + n] for n in WEIGHT_NAMES}

    wall = _gather_weights(_pack_shards({n: wts[n] for n in BIG_NAMES}, MXU_DTYPE, exact=True))
    full = dict(wts)
    for n in BIG_NAMES:
        full[n] = _join_shards(jnp.stack([_unpack_shard(wall[k], n, exact=True) for k in range(N_CHIPS)]), n)
    full["ffn_conv_w"] = full["ffn_conv_w"].astype(F32)

    loss, dx, grads = _local_step(x[0], p[:, 0], loss_target[0], full)
    loss = lax.psum(loss[0, 0], MESH_AXES)

    xi, yi, ci = _me()
    stacked = {n: _split_full(grads[n], n) for n in BIG_NAMES}
    gb = jnp.stack([_pack_shards({n: stacked[n][k] for n in BIG_NAMES}, MXU_DTYPE) for k in range(N_CHIPS)])
    gs = _pack_small(grads)
    sib, allsmall = _exchange_partials(gb, gs)
    pc = _chip_partials(gb, sib, jnp.stack([ci]).astype(jnp.int32))
    recv = _scatter_partials(pc)
    half = _final_half(gb, sib, recv, jnp.stack([2 * xi + yi, ci]).astype(jnp.int32))
    gflat = _share_halves(half)
    gsmall = _unpack_small(_sum_small(allsmall), _small_shapes(wts))

    g_out, d_out, m_out, v_out = {}, {}, {}, {}
    for n in BIG_NAMES:
        g_out[n] = _unpack_shard(gflat, n)
        d_out[n], m_out[n], v_out[n] = _adamw(wts[n], g_out[n], mom_m[n], mom_v[n], name="adamw_" + n)
    sw = _pack_small(wts)
    d_s, m_s, v_s = _adamw(sw, _pack_small(gsmall), _pack_small(mom_m), _pack_small(mom_v), name="adamw_small")
    shapes = _small_shapes(wts)
    d_sm, m_sm, v_sm = _unpack_small(d_s, shapes), _unpack_small(m_s, shapes), _unpack_small(v_s, shapes)
    for n in SMALL_NAMES:
        g_out[n], d_out[n], m_out[n], v_out[n] = gsmall[n], d_sm[n], m_sm[n], v_sm[n]

    return (loss, dx[None], *[g_out[n] for n in WEIGHT_NAMES], *[d_out[n] for n in WEIGHT_NAMES],
            *[m_out[n] for n in WEIGHT_NAMES], *[v_out[n] for n in WEIGHT_NAMES])
```

```python
import functools
import math

import numpy as np
import jax
import jax.numpy as jnp
from jax import lax
from jax.experimental import pallas as pl
from jax.experimental.pallas import tpu as pltpu

F32 = jnp.float32
MXU_DTYPE = jnp.bfloat16
VMEM_LIMIT_BYTES = 52 * 1024 * 1024

D_MODEL = 1024
DEPTH = 2
PLE_DIM = 256
HEAD_DIM = 64
N_HEADS = 8
ATTN_W = 512
QBLK = 128
BRANCH_DIL = (1, 4, 16)
N_BUCKETS = 32
REL_MAX_DIST = 2048
SGU_W = 256
SGU_G = 4
SGU_GW = 64
SGU_CHUNK = 128
SSM_W = 256
SSM_G = 16
SSM_C = 16
SSM_N = 64
NSTATE = SSM_G * SSM_N
D_FF = 2816
EPS = 1e-6
NEG_INF = -1e30
ATTN_SCALE = HEAD_DIM ** -0.5

ADAM_LR = 0.001
ADAM_B1 = 0.9
ADAM_B2 = 0.999
ADAM_EPS = 1e-08
ADAM_WD = 0.01
ADAM_STEP = 10

SSM_NSEG = 8
SSM_TSEG = 64
SSM_TB = SSM_NSEG * SSM_TSEG
SSM_LANE_CHUNK = 512

MESH_AXES = ("x", "y", "c")
N_CHIPS = 4
N_DEV = 8

BIG_NAMES = ("w_in", "ssm_glu_w", "w_out", "ffn_w_up", "ffn_conv_w", "ffn_w_down", "ple_w_gate", "ple_w_proj")
BIG_FULL = {
    "w_in": ((D_MODEL, 2304), 2),
    "ssm_glu_w": ((SSM_W, SSM_W), 1),
    "w_out": ((D_MODEL, D_MODEL), 1),
    "ffn_w_up": ((D_MODEL, 2 * D_FF), 2),
    "ffn_conv_w": ((3, 2 * D_FF), 2),
    "ffn_w_down": ((D_FF, D_MODEL), 1),
    "ple_w_gate": ((D_MODEL, D_MODEL), 1),
    "ple_w_proj": ((PLE_DIM, D_MODEL), 2),
}
PACK_COLS = 1024
PACK_ROWS = 6656
PACK_HALF = PACK_ROWS // 2

SMALL_NAMES = ("rel_bias", "norm_attn_g", "sgu_ln_g", "sgu_ln_b", "sgu_w", "sgu_b", "ssm_a_re", "ssm_a_im",
               "ssm_log_dt", "ssm_b_re", "ssm_b_im", "ssm_c_re", "ssm_c_im", "ssm_d", "ssm_glu_b",
               "branch_norm_g", "norm_ffn_g", "ffn_conv_b", "norm_ple_g", "final_norm_g")
SMALL_ROWS = 288

WEIGHT_NAMES = ("rel_bias", "norm_attn_g", "w_in", "sgu_ln_g", "sgu_ln_b", "sgu_w", "sgu_b", "ssm_a_re", "ssm_a_im",
                "ssm_log_dt", "ssm_b_re", "ssm_b_im", "ssm_c_re", "ssm_c_im", "ssm_d", "ssm_glu_w", "ssm_glu_b",
                "branch_norm_g", "w_out", "norm_ffn_g", "ffn_w_up", "ffn_conv_w", "ffn_conv_b", "ffn_w_down",
                "norm_ple_g", "ple_w_gate", "ple_w_proj", "final_norm_g")


def _params(sem):
    return pltpu.CompilerParams(dimension_semantics=sem, vmem_limit_bytes=VMEM_LIMIT_BYTES)


def _tile(n, cap, mult=128):
    if n <= cap:
        return n
    best = None
    for t in range(mult, cap + 1, mult):
        if n % t == 0:
            best = t
    assert best is not None, (n, cap)
    return best


def _gelu(x):
    return 0.5 * x * (1.0 + jnp.tanh(0.7978845608028654 * (x + 0.044715 * x * x * x)))


def _gelu_pair(x):
    x2 = x * x
    t = jnp.tanh(0.7978845608028654 * x * (1.0 + 0.044715 * x2))
    half = 0.5 * (1.0 + t)
    return x * half, half + 0.5 * x * (1.0 - t * t) * (0.7978845608028654 + 3.0 * 0.044715 * 0.7978845608028654 * x2)


def _dot(a, b, dims):
    return lax.dot_general(a, b, (dims, ((), ())), preferred_element_type=F32)


def _dotf(a, b, dims):
    return _dot(a.astype(MXU_DTYPE), b.astype(MXU_DTYPE), dims)


NN = ((1,), (0,))
NT = ((1,), (1,))
TN = ((0,), (0,))


def _matmul(a, b, *, name, out_dtype, tm, tn, trans_b=False, residual=None):
    m, k = a.shape
    n = b.shape[0] if trans_b else b.shape[1]
    tm = _tile(m, tm, 8)
    tn = _tile(n, tn)
    dims = NT if trans_b else NN

    def body(*refs):
        if residual is None:
            a_ref, b_ref, o_ref = refs
        else:
            a_ref, b_ref, r_ref, o_ref = refs
        acc = _dot(a_ref[...].astype(MXU_DTYPE), b_ref[...].astype(MXU_DTYPE), dims)
        if residual is not None:
            acc = acc + r_ref[...]
        o_ref[...] = acc.astype(o_ref.dtype)

    b_spec = (pl.BlockSpec((tn, k), lambda i, j: (j, 0)) if trans_b
              else pl.BlockSpec((k, tn), lambda i, j: (0, j)))
    in_specs = [pl.BlockSpec((tm, k), lambda i, j: (i, 0)), b_spec]
    args = [a, b]
    if residual is not None:
        in_specs.append(pl.BlockSpec((tm, tn), lambda i, j: (i, j)))
        args.append(residual)
    return pl.pallas_call(
        body, name=name, grid=(m // tm, n // tn), in_specs=in_specs,
        out_specs=pl.BlockSpec((tm, tn), lambda i, j: (i, j)),
        out_shape=jax.ShapeDtypeStruct((m, n), out_dtype),
        compiler_params=_params(("parallel", "parallel")),
    )(*args)


def _matmul_tn(a, g, *, name, tk, tn, tm=512):
    m, k = a.shape
    n = g.shape[1]
    tk = _tile(k, tk)
    tn = _tile(n, tn)
    tm = _tile(m, tm, 8)

    def body(a_ref, g_ref, o_ref):
        @pl.when(pl.program_id(2) == 0)
        def _():
            o_ref[...] = jnp.zeros_like(o_ref)

        o_ref[...] += _dot(a_ref[...].astype(MXU_DTYPE), g_ref[...].astype(MXU_DTYPE), TN)

    return pl.pallas_call(
        body, name=name, grid=(k // tk, n // tn, m // tm),
        in_specs=[pl.BlockSpec((tm, tk), lambda i, j, s: (s, i)),
                  pl.BlockSpec((tm, tn), lambda i, j, s: (s, j))],
        out_specs=pl.BlockSpec((tk, tn), lambda i, j, s: (i, j)),
        out_shape=jax.ShapeDtypeStruct((k, n), F32),
        compiler_params=_params(("parallel", "parallel", "arbitrary")),
    )(a, g)


ROWS = 512


def _rms_fwd(h, g, *, name):
    s, d = h.shape

    def body(h_ref, g_ref, o_ref):
        x = h_ref[...]
        r = lax.rsqrt(jnp.mean(x * x, axis=-1, keepdims=True) + EPS)
        o_ref[...] = (x * r * g_ref[...]).astype(o_ref.dtype)

    return pl.pallas_call(
        body, name=name, grid=(s // ROWS,),
        in_specs=[pl.BlockSpec((ROWS, d), lambda i: (i, 0)), pl.BlockSpec((1, d), lambda i: (0, 0))],
        out_specs=pl.BlockSpec((ROWS, d), lambda i: (i, 0)),
        out_shape=jax.ShapeDtypeStruct((s, d), MXU_DTYPE),
        compiler_params=_params(("parallel",)),
    )(h, g.reshape(1, d))


def _rms_bwd(h, g, dxn, dres, *, name):
    s, d = h.shape

    def body(h_ref, g_ref, dxn_ref, dres_ref, dh_ref, dg_ref):
        @pl.when(pl.program_id(0) == 0)
        def _():
            dg_ref[...] = jnp.zeros_like(dg_ref)

        x = h_ref[...]
        r = lax.rsqrt(jnp.mean(x * x, axis=-1, keepdims=True) + EPS)
        xhat = x * r
        dxn = dxn_ref[...].astype(F32)
        dg_ref[...] += jnp.sum(dxn * xhat, axis=0, keepdims=True)
        dxh = dxn * g_ref[...]
        dh_ref[...] = dres_ref[...] + r * (dxh - xhat * jnp.mean(dxh * xhat, axis=-1, keepdims=True))

    row = pl.BlockSpec((ROWS, d), lambda i: (i, 0))
    vec = pl.BlockSpec((1, d), lambda i: (0, 0))
    return pl.pallas_call(
        body, name=name, grid=(s // ROWS,), in_specs=[row, vec, row, row], out_specs=[row, vec],
        out_shape=[jax.ShapeDtypeStruct((s, d), F32), jax.ShapeDtypeStruct((1, d), F32)],
        compiler_params=_params(("arbitrary",)),
    )(h, g.reshape(1, d), dxn, dres)


def _loss_head(h, g, target):
    s, d = h.shape

    def body(h_ref, g_ref, t_ref, loss_ref, dh_ref, dg_ref):
        @pl.when(pl.program_id(0) == 0)
        def _():
            loss_ref[...] = jnp.zeros_like(loss_ref)
            dg_ref[...] = jnp.zeros_like(dg_ref)

        x = h_ref[...]
        r = lax.rsqrt(jnp.mean(x * x, axis=-1, keepdims=True) + EPS)
        xhat = x * r
        err = xhat * g_ref[...] - t_ref[...]
        loss_ref[...] += 0.5 * jnp.sum(jnp.mean(err * err, axis=-1, keepdims=True), axis=0, keepdims=True)
        dy = err / d
        dg_ref[...] += jnp.sum(dy * xhat, axis=0, keepdims=True)
        dxh = dy * g_ref[...]
        dh_ref[...] = r * (dxh - xhat * jnp.mean(dxh * xhat, axis=-1, keepdims=True))

    row = pl.BlockSpec((ROWS, d), lambda i: (i, 0))
    vec = pl.BlockSpec((1, d), lambda i: (0, 0))
    one = pl.BlockSpec((1, 1), lambda i: (0, 0))
    return pl.pallas_call(
        body, name="loss_head", grid=(s // ROWS,), in_specs=[row, vec, row], out_specs=[one, row, vec],
        out_shape=[jax.ShapeDtypeStruct((1, 1), F32), jax.ShapeDtypeStruct((s, d), F32),
                   jax.ShapeDtypeStruct((1, d), F32)],
        compiler_params=_params(("arbitrary",)),
    )(h, g.reshape(1, d), target)


def _t5_bucket(dist):
    max_exact = N_BUCKETS // 2
    dd = np.maximum(dist, 0)
    large = max_exact + (np.log(np.maximum(dd, 1) / max_exact) / np.log(REL_MAX_DIST / max_exact)
                         * (N_BUCKETS - max_exact)).astype(np.int32)
    large = np.minimum(large, N_BUCKETS - 1)
    return np.where(dd < max_exact, dd, large).astype(np.int32)


def _bucket_table():
    qq = np.arange(QBLK)[:, None]
    kk = np.arange(QBLK)[None, :]
    out = np.zeros((len(BRANCH_DIL), 2, QBLK, QBLK), np.int32)
    for b, dil in enumerate(BRANCH_DIL):
        out[b, 0] = _t5_bucket((qq - kk + QBLK) * dil)
        out[b, 1] = _t5_bucket((qq - kk) * dil)
    return out


def _bias_build(rel_bias):
    idx = jnp.asarray(_bucket_table())

    def body(idx_ref, rb_ref, o_ref):
        ids = idx_ref[0, 0]
        row = lax.broadcasted_iota(jnp.int32, (QBLK, QBLK), 0)
        col = lax.broadcasted_iota(jnp.int32, (QBLK, QBLK), 1)
        valid = (col - row) * jnp.where(pl.program_id(1) == 0, 1, -1) >= 0
        for h in range(N_HEADS):
            acc = jnp.zeros((QBLK, QBLK), F32)
            for b in range(N_BUCKETS):
                acc = jnp.where(ids == b, rb_ref[b, h], acc)
            o_ref[0, 0, h] = jnp.where(valid, acc, NEG_INF)

    return pl.pallas_call(
        body, name="attn_bias_build", grid=(len(BRANCH_DIL), 2),
        in_specs=[pl.BlockSpec((1, 1, QBLK, QBLK), lambda b, p: (b, p, 0, 0)),
                  pl.BlockSpec(memory_space=pltpu.SMEM)],
        out_specs=pl.BlockSpec((1, 1, N_HEADS, QBLK, QBLK), lambda b, p: (b, p, 0, 0, 0)),
        out_shape=jax.ShapeDtypeStruct((len(BRANCH_DIL), 2, N_HEADS, QBLK, QBLK), F32),
        compiler_params=_params(("parallel", "parallel")),
    )(idx, rel_bias)


def _bias_reduce(dbias):
    idx = jnp.asarray(_bucket_table())
    nb = len(BRANCH_DIL)

    def body(idx_ref, d_ref, o_ref):
        def per_bucket(b, carry):
            for h in range(N_HEADS):
                tot = jnp.zeros((), F32)
                for br in range(nb):
                    for p in range(2):
                        tot = tot + jnp.sum(jnp.where(idx_ref[br, p] == b, d_ref[br, p, h], 0.0))
                o_ref[b, h] = tot
            return carry

        lax.fori_loop(0, N_BUCKETS, per_bucket, 0)

    return pl.pallas_call(
        body, name="attn_bias_reduce",
        in_specs=[pl.BlockSpec(memory_space=pltpu.VMEM), pl.BlockSpec(memory_space=pltpu.VMEM)],
        out_specs=pl.BlockSpec(memory_space=pltpu.SMEM),
        out_shape=jax.ShapeDtypeStruct((N_BUCKETS, N_HEADS), F32),
        compiler_params=pltpu.CompilerParams(vmem_limit_bytes=VMEM_LIMIT_BYTES),
    )(idx, dbias)


def _band_masks(c):
    row = lax.broadcasted_iota(jnp.int32, (QBLK, QBLK), 0)
    col = lax.broadcasted_iota(jnp.int32, (QBLK, QBLK), 1)
    mask_cur = col <= row
    mask_prev = jnp.logical_and(col >= row, c > 0)
    return mask_prev, mask_cur


def _attn_specs(dil):
    blk = (QBLK, ATTN_W)
    q = pl.BlockSpec(blk, lambda r, c: (c, 3 * r))
    kp = pl.BlockSpec(blk, lambda r, c: (jnp.maximum(c - 1, 0), 3 * r + 1))
    kc = pl.BlockSpec(blk, lambda r, c: (c, 3 * r + 1))
    vp = pl.BlockSpec(blk, lambda r, c: (jnp.maximum(c - 1, 0), 3 * r + 2))
    vc = pl.BlockSpec(blk, lambda r, c: (c, 3 * r + 2))
    return [q, kp, kc, vp, vc]


def _attn_fwd_branch(qkv, bias, state, *, branch, last):
    dil = BRANCH_DIL[branch]
    s = qkv.shape[0]
    n = s // dil
    nblk = n // QBLK
    first = state is None

    def body(*refs):
        q_ref, kp_ref, kc_ref, vp_ref, vc_ref, b_ref = refs[:6]
        if first:
            outs = refs[6:]
        else:
            acc_ref, m_ref, l_ref = refs[6:9]
            outs = refs[9:]
        mask_prev, mask_cur = _band_masks(pl.program_id(1))
        for h in range(N_HEADS):
            sl = slice(HEAD_DIM * h, HEAD_DIM * (h + 1))
            qh = q_ref[:, sl]
            s_c = _dot(qh, kc_ref[:, sl], NT) * ATTN_SCALE + b_ref[0, 1, h]
            s_p = _dot(qh, kp_ref[:, sl], NT) * ATTN_SCALE + b_ref[0, 0, h]
            s_c = jnp.where(mask_cur, s_c, NEG_INF)
            s_p = jnp.where(mask_prev, s_p, NEG_INF)
            m_blk = jnp.maximum(jnp.max(s_c, axis=-1, keepdims=True), jnp.max(s_p, axis=-1, keepdims=True))
            if first:
                m_new = m_blk
            else:
                m_old = m_ref[:, sl][:, :1]
                m_new = jnp.maximum(m_old, m_blk)
            p_c = jnp.exp(s_c - m_new)
            p_p = jnp.exp(s_p - m_new)
            l_new = jnp.sum(p_c, axis=-1, keepdims=True) + jnp.sum(p_p, axis=-1, keepdims=True)
            acc = (_dot(p_c.astype(MXU_DTYPE), vc_ref[:, sl], NN)
                   + _dot(p_p.astype(MXU_DTYPE), vp_ref[:, sl], NN))
            if not first:
                alpha = jnp.exp(m_old - m_new)
                l_new = l_new + alpha * l_ref[:, sl][:, :1]
                acc = acc + alpha * acc_ref[:, sl]
            if last:
                outs[0][:, sl] = acc / l_new
                outs[1][:, sl] = jnp.broadcast_to(m_new + jnp.log(l_new), (QBLK, HEAD_DIM))
            else:
                outs[0][:, sl] = acc
                outs[1][:, sl] = jnp.broadcast_to(m_new, (QBLK, HEAD_DIM))
                outs[2][:, sl] = jnp.broadcast_to(l_new, (QBLK, HEAD_DIM))

    st_spec = pl.BlockSpec((QBLK, ATTN_W), lambda r, c: (c, r))
    in_specs = _attn_specs(dil) + [pl.BlockSpec((1, 2, N_HEADS, QBLK, QBLK), lambda r, c: (branch, 0, 0, 0, 0))]
    qv = qkv.reshape(n, dil * 3 * ATTN_W)
    args = [qv] * 5 + [bias]
    if not first:
        in_specs += [st_spec] * 3
        args += [t.reshape(n, dil * ATTN_W) for t in state]
    n_out = 2 if last else 3
    outs = pl.pallas_call(
        body, name=f"attn_fwd_b{branch}", grid=(dil, nblk), in_specs=in_specs,
        out_specs=[st_spec] * n_out,
        out_shape=[jax.ShapeDtypeStruct((n, dil * ATTN_W), F32)] * n_out,
        compiler_params=_params(("parallel", "parallel")),
    )(*args)
    return tuple(t.reshape(s, ATTN_W) for t in outs)


def _attn_fwd(qkv, bias):
    state = None
    for b in range(len(BRANCH_DIL)):
        state = _attn_fwd_branch(qkv, bias, state, branch=b, last=(b == len(BRANCH_DIL) - 1))
    return state


def _attn_bwd_branch(qkv, bias, o, lse, do, *, branch):
    dil = BRANCH_DIL[branch]
    s = qkv.shape[0]
    n = s // dil
    nblk = n // QBLK

    def body(q_ref, kp_ref, kc_ref, vp_ref, vc_ref, b_ref, o_ref, l_ref, do_ref,
             dq_ref, dka_ref, dkb_ref, dva_ref, dvb_ref, db_ref):
        @pl.when(jnp.logical_and(pl.program_id(0) == 0, pl.program_id(1) == 0))
        def _():
            db_ref[...] = jnp.zeros_like(db_ref)

        mask_prev, mask_cur = _band_masks(pl.program_id(1))
        for h in range(N_HEADS):
            sl = slice(HEAD_DIM * h, HEAD_DIM * (h + 1))
            qh = q_ref[:, sl]
            doh = do_ref[:, sl]
            lh = l_ref[:, sl][:, :1]
            delta = jnp.sum(doh * o_ref[:, sl], axis=-1, keepdims=True)
            do_m = doh.astype(MXU_DTYPE)
            s_c = _dot(qh, kc_ref[:, sl], NT) * ATTN_SCALE + b_ref[0, 1, h]
            s_p = _dot(qh, kp_ref[:, sl], NT) * ATTN_SCALE + b_ref[0, 0, h]
            p_c = jnp.exp(jnp.where(mask_cur, s_c, NEG_INF) - lh)
            p_p = jnp.exp(jnp.where(mask_prev, s_p, NEG_INF) - lh)
            ds_c = p_c * (_dot(do_m, vc_ref[:, sl], NT) - delta)
            ds_p = p_p * (_dot(do_m, vp_ref[:, sl], NT) - delta)
            db_ref[0, 1, h] += ds_c
            db_ref[0, 0, h] += ds_p
            ds_c_m = ds_c.astype(MXU_DTYPE)
            ds_p_m = ds_p.astype(MXU_DTYPE)
            dq = _dot(ds_c_m, kc_ref[:, sl], NN) + _dot(ds_p_m, kp_ref[:, sl], NN)
            dq_ref[:, sl] = (dq * ATTN_SCALE).astype(dq_ref.dtype)
            dka_ref[:, sl] = (_dot(ds_c_m, qh, TN) * ATTN_SCALE).astype(dka_ref.dtype)
            dkb_ref[:, sl] = (_dot(ds_p_m, qh, TN) * ATTN_SCALE).astype(dkb_ref.dtype)
            dva_ref[:, sl] = _dot(p_c.astype(MXU_DTYPE), do_m, TN).astype(dva_ref.dtype)
            dvb_ref[:, sl] = _dot(p_p.astype(MXU_DTYPE), do_m, TN).astype(dvb_ref.dtype)

    st_spec = pl.BlockSpec((QBLK, ATTN_W), lambda r, c: (c, r))
    b_in = pl.BlockSpec((1, 2, N_HEADS, QBLK, QBLK), lambda r, c: (branch, 0, 0, 0, 0))
    b_out = pl.BlockSpec((1, 2, N_HEADS, QBLK, QBLK), lambda r, c: (0, 0, 0, 0, 0))
    qv = qkv.reshape(n, dil * 3 * ATTN_W)
    view = lambda t: t.reshape(n, dil * ATTN_W)
    outs = pl.pallas_call(
        body, name=f"attn_bwd_b{branch}", grid=(dil, nblk),
        in_specs=_attn_specs(dil) + [b_in, st_spec, st_spec, st_spec],
        out_specs=[st_spec] * 5 + [b_out],
        out_shape=[jax.ShapeDtypeStruct((n, dil * ATTN_W), MXU_DTYPE)] * 5
        + [jax.ShapeDtypeStruct((1, 2, N_HEADS, QBLK, QBLK), F32)],
        compiler_params=_params(("arbitrary", "arbitrary")),
    )(qv, qv, qv, qv, qv, bias, view(o), view(lse), view(do))
    return tuple(t.reshape(s, ATTN_W) for t in outs[:5]) + (outs[5],)


def _attn_bwd(qkv, bias, o, lse, do):
    s = qkv.shape[0]
    nb = s // QBLK
    parts = [_attn_bwd_branch(qkv, bias, o, lse, do, branch=b) for b in range(len(BRANCH_DIL))]
    dbias = jnp.concatenate([p[5] for p in parts], axis=0)

    def body(*refs):
        o_ref = refs[-1]
        i = pl.program_id(0)
        dq = jnp.zeros((QBLK, ATTN_W), F32)
        dk = jnp.zeros((QBLK, ATTN_W), F32)
        dv = jnp.zeros((QBLK, ATTN_W), F32)
        for b, dil in enumerate(BRANCH_DIL):
            dq_ref, dka_ref, dkb_ref, dva_ref, dvb_ref = refs[5 * b:5 * b + 5]
            inside = i + dil < nb
            dq = dq + dq_ref[...].astype(F32)
            dk = dk + dka_ref[...].astype(F32) + jnp.where(inside, dkb_ref[...].astype(F32), 0.0)
            dv = dv + dva_ref[...].astype(F32) + jnp.where(inside, dvb_ref[...].astype(F32), 0.0)
        o_ref[:, 0:ATTN_W] = dq.astype(o_ref.dtype)
        o_ref[:, ATTN_W:2 * ATTN_W] = dk.astype(o_ref.dtype)
        o_ref[:, 2 * ATTN_W:3 * ATTN_W] = dv.astype(o_ref.dtype)

    in_specs, args = [], []
    for b, dil in enumerate(BRANCH_DIL):
        here = pl.BlockSpec((QBLK, ATTN_W), lambda i: (i, 0))
        ahead = pl.BlockSpec((QBLK, ATTN_W), functools.partial(lambda i, d: (jnp.minimum(i + d, nb - 1), 0), d=dil))
        in_specs += [here, here, ahead, here, ahead]
        args += list(parts[b][:5])
    dqkv = pl.pallas_call(
        body, name="attn_bwd_sum", grid=(nb,), in_specs=in_specs,
        out_specs=pl.BlockSpec((QBLK, 3 * ATTN_W), lambda i: (i, 0)),
        out_shape=jax.ShapeDtypeStruct((s, 3 * ATTN_W), MXU_DTYPE),
        compiler_params=_params(("parallel",)),
    )(*args)
    return dqkv, dbias


ATTN_IO_DTYPE = F32
ABLK = 2048
N_CHUNK = ATTN_W // 128


def _rows(start, dil):
    if dil > 1:
        return pl.ds(start, QBLK, stride=dil)
    return pl.ds(pl.multiple_of(start, QBLK), QBLK)


def _low_head():
    return lax.broadcasted_iota(jnp.int32, (QBLK, 128), 1) < HEAD_DIM


def _head_split(t):
    low = _low_head()
    zero = jnp.zeros_like(t)
    return jnp.where(low, t, zero), jnp.where(low, zero, t)


def _loop(n, fn):
    if n == 1:
        fn(jnp.int32(0), 0)
    elif n > 1:
        lax.fori_loop(0, n, fn, 0)


def _for_each_tile(tile, c):
    for branch, dil in enumerate(BRANCH_DIL):
        span = QBLK * dil

        def edge(r, carry, branch=branch, span=span):
            tile(branch, r, False, ABLK - span + r, c == 0)
            return carry

        def inner(t, carry, branch=branch, span=span, dil=dil):
            start = (1 + t // dil) * span + t % dil
            tile(branch, start, True, start - span, None)
            return carry

        _loop(dil, edge)
        _loop((ABLK // span - 1) * dil, inner)


def _attn_chunk_specs(nb):
    blk = (None, ABLK, 128)
    prev = lambda c: jnp.maximum(c - 1, 0)
    return [pl.BlockSpec(blk, lambda ch, c: (ch, c, 0)),
            pl.BlockSpec(blk, lambda ch, c: (N_CHUNK + ch, c, 0)),
            pl.BlockSpec(blk, lambda ch, c: (2 * N_CHUNK + ch, c, 0)),
            pl.BlockSpec(blk, lambda ch, c: (N_CHUNK + ch, prev(c), 0)),
            pl.BlockSpec(blk, lambda ch, c: (2 * N_CHUNK + ch, prev(c), 0)),
            pl.BlockSpec((len(BRANCH_DIL), 2, 2, QBLK, QBLK), lambda ch, c: (0, 0, ch, 0, 0))]


def _in_proj_qkv(xn, w_qkv):
    s, k = xn.shape
    tm = 512
    nch = w_qkv.shape[1] // 128

    def body(x_ref, w_ref, o_ref):
        acc = _dot(x_ref[...].astype(MXU_DTYPE), w_ref[...].astype(MXU_DTYPE), NN)
        for j in range(nch):
            blk = acc[:, 128 * j:128 * (j + 1)]
            if j < N_CHUNK:
                blk = blk * ATTN_SCALE
            o_ref[j] = blk.astype(o_ref.dtype)

    return pl.pallas_call(
        body, name="in_proj_qkv", grid=(s // tm,),
        in_specs=[pl.BlockSpec((tm, k), lambda i: (i, 0)), pl.BlockSpec(w_qkv.shape, lambda i: (0, 0))],
        out_specs=pl.BlockSpec((nch, tm, 128), lambda i: (0, i, 0)),
        out_shape=jax.ShapeDtypeStruct((nch, s, 128), ATTN_IO_DTYPE),
        compiler_params=_params(("parallel",)),
    )(xn, w_qkv)


def _attn2_fwd(qkv_c, bias):
    s = qkv_c.shape[1]
    nb = s // ABLK
    last = len(BRANCH_DIL) - 1

    def body(q_ref, kc_ref, vc_ref, kp_ref, vp_ref, b_ref, o_ref, l_ref, acc_s, m_s, l_s):
        low = _low_head()
        one = jnp.ones((QBLK, 128), MXU_DTYPE)
        e_h = _head_split(one)

        def tile(branch, start, prev_in_block, pstart, first):
            dil = BRANCH_DIL[branch]
            rq, rp = _rows(start, dil), _rows(pstart, dil)
            k_ref, v_ref = (kc_ref, vc_ref) if prev_in_block else (kp_ref, vp_ref)
            q_h = _head_split(q_ref[rq, :].astype(MXU_DTYPE))
            kc2 = kc_ref[rq, :].astype(MXU_DTYPE)
            kp2 = k_ref[rp, :].astype(MXU_DTYPE)
            vc_h = _head_split(vc_ref[rq, :].astype(MXU_DTYPE))
            vp_h = _head_split(v_ref[rp, :].astype(MXU_DTYPE))
            if branch > 0:
                m_old2 = m_s[rq, :]
            m_cols, a_cols = [], []
            acc2 = jnp.zeros((QBLK, 128), F32)
            sum2 = jnp.zeros((QBLK, 128), F32)
            for h in range(2):
                b_p = b_ref[branch, 0, h]
                if first is not None:
                    b_p = jnp.where(first, NEG_INF, b_p)
                s_c = _dot(q_h[h], kc2, NT) + b_ref[branch, 1, h]
                s_p = _dot(q_h[h], kp2, NT) + b_p
                m_new = jnp.max(jnp.maximum(s_c, s_p), axis=-1, keepdims=True)
                if branch > 0:
                    m_old = m_old2[:, HEAD_DIM * h:HEAD_DIM * h + 1]
                    m_new = jnp.maximum(m_old, m_new)
                    a_cols.append(jnp.exp(m_old - m_new))
                m_cols.append(m_new)
                p_c = jnp.exp(s_c - m_new).astype(MXU_DTYPE)
                p_p = jnp.exp(s_p - m_new).astype(MXU_DTYPE)
                acc2 = acc2 + _dot(p_c, vc_h[h], NN) + _dot(p_p, vp_h[h], NN)
                sum2 = sum2 + _dot(p_c, e_h[h], NN) + _dot(p_p, e_h[h], NN)
            m2 = jnp.where(low, m_cols[0], m_cols[1])
            if branch > 0:
                a2 = jnp.where(low, a_cols[0], a_cols[1])
                acc2 = acc2 + a2 * acc_s[rq, :]
                sum2 = sum2 + a2 * l_s[rq, :]
            if branch == last:
                o_ref[rq, :] = acc2 / sum2
                l_ref[rq, :] = m2 + jnp.log(sum2)
            else:
                acc_s[rq, :] = acc2
                m_s[rq, :] = m2
                l_s[rq, :] = sum2

        _for_each_tile(tile, pl.program_id(1))

    out_spec = pl.BlockSpec((None, ABLK, 128), lambda ch, c: (ch, c, 0))
    return pl.pallas_call(
        body, name="attn_fwd", grid=(N_CHUNK, nb), in_specs=_attn_chunk_specs(nb),
        out_specs=[out_spec, out_spec],
        out_shape=[jax.ShapeDtypeStruct((N_CHUNK, s, 128), F32)] * 2,
        scratch_shapes=[pltpu.VMEM((ABLK, 128), F32)] * 3,
        compiler_params=_params(("parallel", "arbitrary")),
    )(qkv_c, qkv_c, qkv_c, qkv_c, qkv_c, bias)


def _attn2_bwd(qkv_c, bias, lse_c, delta_c, do_c):
    s = qkv_c.shape[1]
    nb = s // ABLK
    nbr = len(BRANCH_DIL)

    def body(q_ref, kc_ref, vc_ref, kp_ref, vp_ref, b_ref, l_ref, dl_ref, do_ref,
             dq_ref, dk_ref, dv_ref, *rest):
        ek_refs, ev_refs, db_ref = rest[:nbr], rest[nbr:2 * nbr], rest[2 * nbr]
        c = pl.program_id(1)

        @pl.when(c == 0)
        def _():
            db_ref[...] = jnp.zeros_like(db_ref)

        for r in (dq_ref, dk_ref, dv_ref) + tuple(ek_refs) + tuple(ev_refs):
            r[...] = jnp.zeros_like(r)

        def tile(branch, start, prev_in_block, pstart, first):
            dil = BRANCH_DIL[branch]
            rq, rp = _rows(start, dil), _rows(pstart, dil)
            k_ref, v_ref = (kc_ref, vc_ref) if prev_in_block else (kp_ref, vp_ref)
            q_h = _head_split(q_ref[rq, :].astype(MXU_DTYPE))
            kc2 = kc_ref[rq, :].astype(MXU_DTYPE)
            kp2 = k_ref[rp, :].astype(MXU_DTYPE)
            kc_h, kp_h = _head_split(kc2), _head_split(kp2)
            vc_h = _head_split(vc_ref[rq, :].astype(MXU_DTYPE))
            vp_h = _head_split(v_ref[rp, :].astype(MXU_DTYPE))
            do2 = do_ref[rq, :].astype(MXU_DTYPE)
            do_h = _head_split(do2)
            lse2 = l_ref[rq, :]
            del2 = dl_ref[rq, :]
            zero = jnp.zeros((QBLK, 128), F32)
            dq2, dkc2, dkp2, dvc2, dvp2 = zero, zero, zero, zero, zero
            for h in range(2):
                b_p = b_ref[branch, 0, h]
                if first is not None:
                    b_p = jnp.where(first, NEG_INF, b_p)
                lh = lse2[:, HEAD_DIM * h:HEAD_DIM * h + 1]
                dh = del2[:, HEAD_DIM * h:HEAD_DIM * h + 1]
                p_c = jnp.exp(_dot(q_h[h], kc2, NT) + b_ref[branch, 1, h] - lh)
                p_p = jnp.exp(_dot(q_h[h], kp2, NT) + b_p - lh)
                ds_c = p_c * (_dot(do2, vc_h[h], NT) - dh)
                ds_p = p_p * (_dot(do2, vp_h[h], NT) - dh)
                db_ref[branch, 1, h] += ds_c
                db_ref[branch, 0, h] += ds_p
                ds_c, ds_p = ds_c.astype(MXU_DTYPE), ds_p.astype(MXU_DTYPE)
                p_c, p_p = p_c.astype(MXU_DTYPE), p_p.astype(MXU_DTYPE)
                dq2 = dq2 + _dot(ds_c, kc_h[h], NN) + _dot(ds_p, kp_h[h], NN)
                dkc2 = dkc2 + _dot(ds_c, q_h[h], TN)
                dkp2 = dkp2 + _dot(ds_p, q_h[h], TN)
                dvc2 = dvc2 + _dot(p_c, do_h[h], TN)
                dvp2 = dvp2 + _dot(p_p, do_h[h], TN)
            dq_ref[rq, :] += dq2
            dk_ref[rq, :] += dkc2
            dv_ref[rq, :] += dvc2
            if prev_in_block:
                dk_ref[rp, :] += dkp2
                dv_ref[rp, :] += dvp2
            else:
                ek_refs[branch][rq, :] = dkp2
                ev_refs[branch][rq, :] = dvp2

        _for_each_tile(tile, c)

    blk = pl.BlockSpec((None, ABLK, 128), lambda ch, c: (ch, c, 0))
    outs = pl.pallas_call(
        body, name="attn_bwd", grid=(N_CHUNK, nb), in_specs=_attn_chunk_specs(nb) + [blk, blk, blk],
        out_specs=[blk] * (3 + 2 * nbr) + [pl.BlockSpec((nbr, 2, 2, QBLK, QBLK), lambda ch, c: (0, 0, ch, 0, 0))],
        out_shape=[jax.ShapeDtypeStruct((N_CHUNK, s, 128), F32)] * (3 + 2 * nbr)
        + [jax.ShapeDtypeStruct((nbr, 2, N_HEADS, QBLK, QBLK), F32)],
        compiler_params=_params(("arbitrary", "arbitrary")),
    )(qkv_c, qkv_c, qkv_c, qkv_c, qkv_c, bias, lse_c, delta_c, do_c)
    return outs[0], outs[1], outs[2], outs[3:3 + nbr], outs[3 + nbr:3 + 2 * nbr], outs[3 + 2 * nbr]


def _attn2_bwd_sum(dq, dk, dv, ek, ev, dzs, dus):
    s = dq.shape[1]
    nrb = s // QBLK
    per_blk = ABLK // QBLK
    nbr = len(BRANCH_DIL)

    def body(*refs):
        dq_ref, dk_ref, dv_ref = refs[:3]
        ek_refs, ev_refs = refs[3:3 + nbr], refs[3 + nbr:3 + 2 * nbr]
        dzs_ref, dus_ref, o_ref = refs[3 + 2 * nbr:]
        i = pl.program_id(0)
        dkt, dvt = dk_ref[...], dv_ref[...]
        for b, dil in enumerate(BRANCH_DIL):
            j = i + dil
            ok = jnp.logical_and(j < nrb, j % per_blk < dil)
            dkt = dkt + jnp.where(ok, ek_refs[b][...], 0.0)
            dvt = dvt + jnp.where(ok, ev_refs[b][...], 0.0)
        for ch in range(N_CHUNK):
            o_ref[:, 128 * ch:128 * (ch + 1)] = (dq_ref[ch] * ATTN_SCALE).astype(o_ref.dtype)
            o_ref[:, ATTN_W + 128 * ch:ATTN_W + 128 * (ch + 1)] = dkt[ch].astype(o_ref.dtype)
            o_ref[:, 2 * ATTN_W + 128 * ch:2 * ATTN_W + 128 * (ch + 1)] = dvt[ch].astype(o_ref.dtype)
        o_ref[:, O_SGU:O_SSM] = dzs_ref[...].astype(o_ref.dtype)
        o_ref[:, O_SSM:] = dus_ref[...].astype(o_ref.dtype)

    here = pl.BlockSpec((N_CHUNK, QBLK, 128), lambda i: (0, i, 0))
    edge_specs = [pl.BlockSpec((N_CHUNK, QBLK, 128),
                               functools.partial(lambda i, d: (0, jnp.minimum(i + d, nrb - 1), 0), d=dil))
                  for dil in BRANCH_DIL]
    return pl.pallas_call(
        body, name="attn_bwd_sum", grid=(nrb,),
        in_specs=[here, here, here] + edge_specs + edge_specs
        + [pl.BlockSpec((QBLK, 2 * SGU_W), lambda i: (i, 0)), pl.BlockSpec((QBLK, SSM_W), lambda i: (i, 0))],
        out_specs=pl.BlockSpec((QBLK, O_SSM + SSM_W), lambda i: (i, 0)),
        out_shape=jax.ShapeDtypeStruct((s, O_SSM + SSM_W), MXU_DTYPE),
        compiler_params=_params(("parallel",)),
    )(dq, dk, dv, *ek, *ev, dzs, dus)


SGU_ROWS = 512


def _sgu_norm(v_g):
    mu = jnp.mean(v_g, axis=-1, keepdims=True)
    cen = v_g - mu
    var = jnp.mean(cen * cen, axis=-1, keepdims=True)
    rstd = lax.rsqrt(var + EPS)
    return cen * rstd, rstd


def _sgu_fwd(zs, ln_g, ln_b, w_mask, b_t):
    s = zs.shape[0]
    nch = SGU_ROWS // SGU_CHUNK

    def body(z_ref, g_ref, b_ref, w_ref, bt_ref, o_ref):
        gz = _gelu(z_ref[...])
        for g in range(SGU_G):
            sl = slice(SGU_GW * g, SGU_GW * (g + 1))
            u_g = gz[:, sl]
            xhat, _ = _sgu_norm(gz[:, SGU_W + SGU_GW * g:SGU_W + SGU_GW * (g + 1)])
            vn = (xhat * g_ref[:, sl] + b_ref[:, sl]).astype(MXU_DTYPE)
            wg = w_ref[g].astype(MXU_DTYPE)
            for ci in range(nch):
                rs = slice(SGU_CHUNK * ci, SGU_CHUNK * (ci + 1))
                mixed = _dot(wg, vn[rs], NN) + bt_ref[:, g:g + 1]
                o_ref[rs, sl] = u_g[rs] * mixed

    full = lambda shape: pl.BlockSpec(shape, lambda i: tuple(0 for _ in shape))
    return pl.pallas_call(
        body, name="sgu_fwd", grid=(s // SGU_ROWS,),
        in_specs=[pl.BlockSpec((SGU_ROWS, 2 * SGU_W), lambda i: (i, 0)), full((1, SGU_W)), full((1, SGU_W)),
                  full((SGU_G, SGU_CHUNK, SGU_CHUNK)), full((SGU_CHUNK, SGU_G))],
        out_specs=pl.BlockSpec((SGU_ROWS, SGU_W), lambda i: (i, 0)),
        out_shape=jax.ShapeDtypeStruct((s, SGU_W), F32),
        compiler_params=_params(("parallel",)),
    )(zs, ln_g.reshape(1, SGU_W), ln_b.reshape(1, SGU_W), w_mask, b_t)


def _sgu_bwd(zs, ln_g, ln_b, w_mask, b_t, dy):
    s = zs.shape[0]
    nch = SGU_ROWS // SGU_CHUNK

    def body(z_ref, g_ref, b_ref, w_ref, bt_ref, dy_ref, dz_ref, dg_ref, dbb_ref, dw_ref, dbt_ref):
        @pl.when(pl.program_id(0) == 0)
        def _():
            dg_ref[...] = jnp.zeros_like(dg_ref)
            dbb_ref[...] = jnp.zeros_like(dbb_ref)
            dw_ref[...] = jnp.zeros_like(dw_ref)
            dbt_ref[...] = jnp.zeros_like(dbt_ref)

        z = z_ref[...]
        gz, dgelu = _gelu_pair(z)
        dy = dy_ref[...]
        for g in range(SGU_G):
            sl = slice(SGU_GW * g, SGU_GW * (g + 1))
            sv = slice(SGU_W + SGU_GW * g, SGU_W + SGU_GW * (g + 1))
            u_g = gz[:, sl]
            xhat, rstd = _sgu_norm(gz[:, sv])
            gain = g_ref[:, sl]
            vn = (xhat * gain + b_ref[:, sl]).astype(MXU_DTYPE)
            wg = w_ref[g].astype(MXU_DTYPE)
            dy_g = dy[:, sl]
            dvn_parts = []
            for ci in range(nch):
                rs = slice(SGU_CHUNK * ci, SGU_CHUNK * (ci + 1))
                mixed = _dot(wg, vn[rs], NN) + bt_ref[:, g:g + 1]
                dz_ref[rs, sl] = (dy_g[rs] * mixed * dgelu[rs, sl]).astype(dz_ref.dtype)
                dmixed = dy_g[rs] * u_g[rs]
                dm = dmixed.astype(MXU_DTYPE)
                dvn_parts.append(_dot(wg, dm, TN))
                dw_ref[g] += _dot(dm, vn[rs], NT)
                dbt_ref[:, g:g + 1] += jnp.sum(dmixed, axis=-1, keepdims=True)
            dvn = jnp.concatenate(dvn_parts, axis=0)
            dg_ref[:, sl] += jnp.sum(dvn * xhat, axis=0, keepdims=True)
            dbb_ref[:, sl] += jnp.sum(dvn, axis=0, keepdims=True)
            dxh = dvn * gain
            dv = rstd * (dxh - jnp.mean(dxh, axis=-1, keepdims=True)
                         - xhat * jnp.mean(dxh * xhat, axis=-1, keepdims=True))
            dz_ref[:, sv] = (dv * dgelu[:, sv]).astype(dz_ref.dtype)

    full = lambda shape: pl.BlockSpec(shape, lambda i: tuple(0 for _ in shape))
    return pl.pallas_call(
        body, name="sgu_bwd", grid=(s // SGU_ROWS,),
        in_specs=[pl.BlockSpec((SGU_ROWS, 2 * SGU_W), lambda i: (i, 0)), full((1, SGU_W)), full((1, SGU_W)),
                  full((SGU_G, SGU_CHUNK, SGU_CHUNK)), full((SGU_CHUNK, SGU_G)),
                  pl.BlockSpec((SGU_ROWS, SGU_W), lambda i: (i, 0))],
        out_specs=[pl.BlockSpec((SGU_ROWS, 2 * SGU_W), lambda i: (i, 0)), full((1, SGU_W)), full((1, SGU_W)),
                   full((SGU_G, SGU_CHUNK, SGU_CHUNK)), full((SGU_CHUNK, SGU_G))],
        out_shape=[jax.ShapeDtypeStruct((s, 2 * SGU_W), MXU_DTYPE), jax.ShapeDtypeStruct((1, SGU_W), F32),
                   jax.ShapeDtypeStruct((1, SGU_W), F32), jax.ShapeDtypeStruct((SGU_G, SGU_CHUNK, SGU_CHUNK), F32),
                   jax.ShapeDtypeStruct((SGU_CHUNK, SGU_G), F32)],
        compiler_params=_params(("arbitrary",)),
    )(zs, ln_g.reshape(1, SGU_W), ln_b.reshape(1, SGU_W), w_mask, b_t, dy)


def _ssm_discretize(a_re, a_im, log_dt, b_re, b_im):
    dt = jnp.exp(log_dt)[:, None]
    mag = jnp.exp(a_re * dt)
    ab_re = mag * jnp.cos(a_im * dt)
    ab_im = mag * jnp.sin(a_im * dt)
    den = a_re * a_re + a_im * a_im
    f_re = ((ab_re - 1.0) * a_re + ab_im * a_im) / den
    f_im = (ab_im * a_re - (ab_re - 1.0) * a_im) / den
    bb_re = f_re[:, :, None] * b_re - f_im[:, :, None] * b_im
    bb_im = f_re[:, :, None] * b_im + f_im[:, :, None] * b_re
    return ab_re, ab_im, bb_re, bb_im


def _ssm_operands(a_re, a_im, log_dt, b_re, b_im, c_re, c_im):
    ab_re, ab_im, bb_re, bb_im = _ssm_discretize(a_re, a_im, log_dt, b_re, b_im)
    eye = jnp.eye(SSM_G, dtype=F32)
    b_blk = jnp.einsum("pgnc,gh->gcphn", jnp.stack([bb_re, bb_im]), eye).reshape(SSM_W, 2 * NSTATE)
    c_mat = jnp.einsum("pgcn,gh->pgnhc", jnp.stack([c_re, -c_im]), eye).reshape(2 * NSTATE, SSM_W)
    a_row = jnp.stack([ab_re.reshape(NSTATE), ab_im.reshape(NSTATE)])
    p_re, p_im = a_row[0:1], a_row[1:2]
    while p_re.shape[0] < SSM_TSEG:
        l_re, l_im = p_re[-1:], p_im[-1:]
        p_re, p_im = (jnp.concatenate([p_re, p_re * l_re - p_im * l_im]),
                      jnp.concatenate([p_im, p_re * l_im + p_im * l_re]))
    p_tab = jnp.stack([p_re, p_im])
    return b_blk.astype(MXU_DTYPE), c_mat.astype(MXU_DTYPE), a_row, p_tab


def _lane_chunks():
    return [(lo, lo + SSM_LANE_CHUNK) for lo in range(0, NSTATE, SSM_LANE_CHUNK)]


def _seg_rows(j):
    return pl.ds(pl.multiple_of(j * SSM_NSEG, SSM_NSEG), SSM_NSEG)


def _to_segments(t):
    s, w = t.shape
    return t.reshape(s // SSM_TB, SSM_NSEG, SSM_TSEG, w).transpose(0, 2, 1, 3).reshape(s, w)


def _from_segments(t):
    s, w = t.shape
    return t.reshape(s // SSM_TB, SSM_TSEG, SSM_NSEG, w).transpose(0, 2, 1, 3).reshape(s, w)


def _ssm_local_scan(buf, a_ref, *, reverse):
    ends_re, ends_im = [], []
    for lo, hi in _lane_chunks():
        are = jnp.broadcast_to(a_ref[0:1, lo:hi], (SSM_NSEG, hi - lo))
        aim = jnp.broadcast_to(a_ref[1:2, lo:hi], (SSM_NSEG, hi - lo))
        if reverse:
            aim = -aim

        def step(jj, carry, lo=lo, hi=hi, are=are, aim=aim):
            xr, xi = carry
            j = (SSM_TSEG - 1 - jj) if reverse else jj
            tr = buf[_seg_rows(j), lo:hi]
            ti = buf[_seg_rows(j), NSTATE + lo:NSTATE + hi]
            nr = are * xr - aim * xi + tr
            ni = are * xi + aim * xr + ti
            buf[_seg_rows(j), lo:hi] = nr
            buf[_seg_rows(j), NSTATE + lo:NSTATE + hi] = ni
            return nr, ni

        zero = jnp.zeros((SSM_NSEG, hi - lo), F32)
        xr, xi = lax.fori_loop(0, SSM_TSEG, step, (zero, zero), unroll=4)
        ends_re.append(xr)
        ends_im.append(xi)
    return jnp.concatenate(ends_re, axis=1), jnp.concatenate(ends_im, axis=1)


def _ssm_entry_states(ends_re, ends_im, carry_ref, p_ref, entry_ref, *, reverse):
    at_re = p_ref[0, SSM_TSEG - 1:SSM_TSEG, :]
    at_im = p_ref[1, SSM_TSEG - 1:SSM_TSEG, :]
    if reverse:
        at_im = -at_im
    cur_re = carry_ref[0:1, 0:NSTATE]
    cur_im = carry_ref[0:1, NSTATE:2 * NSTATE]
    order = range(SSM_NSEG - 1, -1, -1) if reverse else range(SSM_NSEG)
    for i in order:
        entry_ref[0, i:i + 1, 0:NSTATE] = cur_re
        entry_ref[0, i:i + 1, NSTATE:2 * NSTATE] = cur_im
        nxt_re = ends_re[i:i + 1] + at_re * cur_re - at_im * cur_im
        nxt_im = ends_im[i:i + 1] + at_re * cur_im + at_im * cur_re
        cur_re, cur_im = nxt_re, nxt_im
    carry_ref[0:1, 0:NSTATE] = cur_re
    carry_ref[0:1, NSTATE:2 * NSTATE] = cur_im


def _ssm_fixup(buf, p_ref, entry_ref, *, reverse):
    for lo, hi in _lane_chunks():
        e_re = entry_ref[0, :, lo:hi]
        e_im = entry_ref[0, :, NSTATE + lo:NSTATE + hi]

        def step(j, carry, lo=lo, hi=hi, e_re=e_re, e_im=e_im):
            jp = (SSM_TSEG - 1 - j) if reverse else j
            pr = p_ref[0, pl.ds(jp, 1), lo:hi]
            pi = p_ref[1, pl.ds(jp, 1), lo:hi]
            if reverse:
                pi = -pi
            buf[_seg_rows(j), lo:hi] = buf[_seg_rows(j), lo:hi] + pr * e_re - pi * e_im
            buf[_seg_rows(j), NSTATE + lo:NSTATE + hi] = (buf[_seg_rows(j), NSTATE + lo:NSTATE + hi]
                                                           + pr * e_im + pi * e_re)
            return carry

        lax.fori_loop(0, SSM_TSEG, step, 0, unroll=4)


def _ssm_fwd(u, ops, d_skip, glu_w, glu_b):
    b_blk, c_mat, a_row, p_tab = ops
    s = u.shape[0]
    nblk = s // SSM_TB

    def body(u_ref, bb_ref, cm_ref, a_ref, p_ref, d_ref, gw_ref, gb_ref, y_ref, entry_ref, xbuf, carry):
        @pl.when(pl.program_id(0) == 0)
        def _():
            carry[...] = jnp.zeros_like(carry)

        uu = u_ref[...]
        xbuf[...] = _dotf(uu, bb_ref[...], NN)
        ends_re, ends_im = _ssm_local_scan(xbuf, a_ref, reverse=False)
        _ssm_entry_states(ends_re, ends_im, carry, p_ref, entry_ref, reverse=False)
        _ssm_fixup(xbuf, p_ref, entry_ref, reverse=False)
        y = _dotf(xbuf[...],cm_ref[...], NN) + d_ref[...] * uu
        y2 = _gelu(y)
        gate = jax.nn.sigmoid(_dot(y2.astype(MXU_DTYPE), gw_ref[...].astype(MXU_DTYPE), NN) + gb_ref[...])
        y_ref[...] = y2 * gate

    full = lambda shape: pl.BlockSpec(shape, lambda i: tuple(0 for _ in shape))
    y_seg, entry = pl.pallas_call(
        body, name="ssm_fwd", grid=(nblk,),
        in_specs=[pl.BlockSpec((SSM_TB, SSM_W), lambda i: (i, 0)), full(b_blk.shape), full(c_mat.shape),
                  full(a_row.shape), full(p_tab.shape), full((1, SSM_W)), full((SSM_W, SSM_W)), full((1, SSM_W))],
        out_specs=[pl.BlockSpec((SSM_TB, SSM_W), lambda i: (i, 0)),
                   pl.BlockSpec((1, SSM_NSEG, 2 * NSTATE), lambda i: (i, 0, 0))],
        out_shape=[jax.ShapeDtypeStruct((s, SSM_W), F32), jax.ShapeDtypeStruct((nblk, SSM_NSEG, 2 * NSTATE), F32)],
        scratch_shapes=[pltpu.VMEM((SSM_TB, 2 * NSTATE), F32), pltpu.VMEM((SSM_NSEG, 2 * NSTATE), F32)],
        compiler_params=_params(("arbitrary",)),
    )(_to_segments(u), b_blk, c_mat, a_row, p_tab, d_skip.reshape(1, SSM_W), glu_w, glu_b.reshape(1, SSM_W))
    return _from_segments(y_seg), entry


def _ssm_bwd(u, entry, ops, d_skip, glu_w, glu_b, dout):
    b_blk, c_mat, a_row, p_tab = ops
    s = u.shape[0]
    nblk = s // SSM_TB

    def body(u_ref, en_ref, bb_ref, cm_ref, a_ref, p_ref, d_ref, gw_ref, gb_ref, do_ref,
             du_ref, dbb_ref, dcm_ref, da_ref, dd_ref, dgw_ref, dgb_ref, xbuf, gbuf, gcarry, gentry):
        @pl.when(pl.program_id(0) == 0)
        def _():
            gcarry[...] = jnp.zeros_like(gcarry)
            for r in (dbb_ref, dcm_ref, da_ref, dd_ref, dgw_ref, dgb_ref):
                r[...] = jnp.zeros_like(r)

        uu = u_ref[...]
        xbuf[...] = _dotf(uu, bb_ref[...], NN)
        _ssm_local_scan(xbuf, a_ref, reverse=False)
        _ssm_fixup(xbuf, p_ref, en_ref, reverse=False)
        y = _dotf(xbuf[...],cm_ref[...], NN) + d_ref[...] * uu
        y2, dgelu = _gelu_pair(y)
        y2m = y2.astype(MXU_DTYPE)
        gwm = gw_ref[...].astype(MXU_DTYPE)
        gate = jax.nn.sigmoid(_dot(y2m, gwm, NN) + gb_ref[...])
        dout = do_ref[...]
        dpre = dout * y2 * gate * (1.0 - gate)
        dprem = dpre.astype(MXU_DTYPE)
        dy2 = dout * gate + _dot(dprem, gwm, NT)
        dgw_ref[...] += _dot(y2m, dprem, TN)
        dgb_ref[...] += jnp.sum(dpre, axis=0, keepdims=True)
        dy = dy2 * dgelu
        dd_ref[...] += jnp.sum(dy * uu, axis=0, keepdims=True)
        dcm_ref[...] += _dotf(xbuf[...],dy, TN)
        gbuf[...] = _dotf(dy, cm_ref[...], NT)
        gs_re, gs_im = _ssm_local_scan(gbuf, a_ref, reverse=True)
        _ssm_entry_states(gs_re, gs_im, gcarry, p_ref, gentry, reverse=True)
        _ssm_fixup(gbuf, p_ref, gentry, reverse=True)
        du_ref[...] = (_dotf(gbuf[...], bb_ref[...], NT) + d_ref[...] * dy).astype(du_ref.dtype)
        dbb_ref[...] += _dotf(uu, gbuf[...], TN)
        for lo, hi in _lane_chunks():
            def step(j, carry, lo=lo, hi=hi):
                acc_re, acc_im = carry
                g_re = gbuf[_seg_rows(j), lo:hi]
                g_im = gbuf[_seg_rows(j), NSTATE + lo:NSTATE + hi]
                x_re = xbuf[_seg_rows(j - 1), lo:hi]
                x_im = xbuf[_seg_rows(j - 1), NSTATE + lo:NSTATE + hi]
                return acc_re + g_re * x_re + g_im * x_im, acc_im + g_im * x_re - g_re * x_im

            g0_re = gbuf[_seg_rows(0), lo:hi]
            g0_im = gbuf[_seg_rows(0), NSTATE + lo:NSTATE + hi]
            e_re = en_ref[0, :, lo:hi]
            e_im = en_ref[0, :, NSTATE + lo:NSTATE + hi]
            init = (g0_re * e_re + g0_im * e_im, g0_im * e_re - g0_re * e_im)
            acc_re, acc_im = lax.fori_loop(1, SSM_TSEG, step, init, unroll=4)
            da_ref[0:1, lo:hi] += jnp.sum(acc_re, axis=0, keepdims=True)
            da_ref[1:2, lo:hi] += jnp.sum(acc_im, axis=0, keepdims=True)

    full = lambda shape: pl.BlockSpec(shape, lambda i: tuple(0 for _ in shape))
    rev = pl.BlockSpec((SSM_TB, SSM_W), lambda i: (nblk - 1 - i, 0))
    outs = pl.pallas_call(
        body, name="ssm_bwd", grid=(nblk,),
        in_specs=[rev, pl.BlockSpec((1, SSM_NSEG, 2 * NSTATE), lambda i: (nblk - 1 - i, 0, 0)),
                  full(b_blk.shape), full(c_mat.shape), full(a_row.shape), full(p_tab.shape),
                  full((1, SSM_W)), full((SSM_W, SSM_W)), full((1, SSM_W)), rev],
        out_specs=[rev, full(b_blk.shape), full(c_mat.shape), full(a_row.shape), full((1, SSM_W)),
                   full((SSM_W, SSM_W)), full((1, SSM_W))],
        out_shape=[jax.ShapeDtypeStruct((s, SSM_W), MXU_DTYPE), jax.ShapeDtypeStruct(b_blk.shape, F32),
                   jax.ShapeDtypeStruct(c_mat.shape, F32), jax.ShapeDtypeStruct(a_row.shape, F32),
                   jax.ShapeDtypeStruct((1, SSM_W), F32), jax.ShapeDtypeStruct((SSM_W, SSM_W), F32),
                   jax.ShapeDtypeStruct((1, SSM_W), F32)],
        scratch_shapes=[pltpu.VMEM((SSM_TB, 2 * NSTATE), F32), pltpu.VMEM((SSM_TB, 2 * NSTATE), F32),
                        pltpu.VMEM((SSM_NSEG, 2 * NSTATE), F32), pltpu.VMEM((1, SSM_NSEG, 2 * NSTATE), F32)],
        compiler_params=_params(("arbitrary",)),
    )(_to_segments(u), entry, b_blk, c_mat, a_row, p_tab, d_skip.reshape(1, SSM_W), glu_w, glu_b.reshape(1, SSM_W),
      _to_segments(dout))
    return (_from_segments(outs[0]),) + tuple(outs[1:])


MIX_SEGS = ((0, ATTN_W), (ATTN_W, ATTN_W + SGU_W), (ATTN_W + SGU_W, D_MODEL))


def _chunks_to_rows(a_ref):
    return jnp.concatenate([a_ref[ch] for ch in range(N_CHUNK)], axis=1)


def _mix_fwd(y_attn_c, y_sgu, y_ssm, gain):
    s = y_sgu.shape[0]

    def body(a_ref, b_ref, c_ref, g_ref, o_ref):
        for x, (lo, hi) in zip((_chunks_to_rows(a_ref), b_ref[...], c_ref[...]), MIX_SEGS):
            r = lax.rsqrt(jnp.mean(x * x, axis=-1, keepdims=True) + EPS)
            o_ref[:, lo:hi] = (x * r * g_ref[:, lo:hi]).astype(o_ref.dtype)

    row = lambda w: pl.BlockSpec((ROWS, w), lambda i: (i, 0))
    return pl.pallas_call(
        body, name="mix_fwd", grid=(s // ROWS,),
        in_specs=[pl.BlockSpec((N_CHUNK, ROWS, 128), lambda i: (0, i, 0)), row(SGU_W), row(SSM_W),
                  pl.BlockSpec((1, D_MODEL), lambda i: (0, 0))],
        out_specs=row(D_MODEL), out_shape=jax.ShapeDtypeStruct((s, D_MODEL), MXU_DTYPE),
        compiler_params=_params(("parallel",)),
    )(y_attn_c, y_sgu, y_ssm, gain.reshape(1, D_MODEL))


def _mix_bwd(y_attn_c, y_sgu, y_ssm, gain, dmix):
    s = y_sgu.shape[0]

    def body(a_ref, b_ref, c_ref, g_ref, dm_ref, da_ref, dl_ref, db_ref, dc_ref, dg_ref):
        @pl.when(pl.program_id(0) == 0)
        def _():
            dg_ref[...] = jnp.zeros_like(dg_ref)

        grads = []
        for x, (lo, hi) in zip((_chunks_to_rows(a_ref), b_ref[...], c_ref[...]), MIX_SEGS):
            r = lax.rsqrt(jnp.mean(x * x, axis=-1, keepdims=True) + EPS)
            xhat = x * r
            dm = dm_ref[:, lo:hi].astype(F32)
            dg_ref[:, lo:hi] += jnp.sum(dm * xhat, axis=0, keepdims=True)
            dxh = dm * g_ref[:, lo:hi]
            grads.append(r * (dxh - xhat * jnp.mean(dxh * xhat, axis=-1, keepdims=True)))
        db_ref[...] = grads[1]
        dc_ref[...] = grads[2]
        low = lax.broadcasted_iota(jnp.int32, (ROWS, 128), 1) < HEAD_DIM
        for ch in range(N_CHUNK):
            d_c = grads[0][:, 128 * ch:128 * (ch + 1)]
            da_ref[ch] = d_c.astype(da_ref.dtype)
            prod = d_c * a_ref[ch]
            dl_ref[ch] = jnp.where(low, jnp.sum(prod[:, :HEAD_DIM], axis=-1, keepdims=True),
                                   jnp.sum(prod[:, HEAD_DIM:], axis=-1, keepdims=True))

    row = lambda w: pl.BlockSpec((ROWS, w), lambda i: (i, 0))
    vec = pl.BlockSpec((1, D_MODEL), lambda i: (0, 0))
    chunked = pl.BlockSpec((N_CHUNK, ROWS, 128), lambda i: (0, i, 0))
    return pl.pallas_call(
        body, name="mix_bwd", grid=(s // ROWS,),
        in_specs=[chunked, row(SGU_W), row(SSM_W), vec, row(D_MODEL)],
        out_specs=[chunked, chunked, row(SGU_W), row(SSM_W), vec],
        out_shape=[jax.ShapeDtypeStruct((N_CHUNK, s, 128), ATTN_IO_DTYPE), jax.ShapeDtypeStruct((N_CHUNK, s, 128), F32),
                   jax.ShapeDtypeStruct((s, SGU_W), F32), jax.ShapeDtypeStruct((s, SSM_W), F32),
                   jax.ShapeDtypeStruct((1, D_MODEL), F32)],
        compiler_params=_params(("arbitrary",)),
    )(y_attn_c, y_sgu, y_ssm, gain.reshape(1, D_MODEL), dmix)


CONV_ROWS = 256
CONV_COLS = 1408
CONV_PAIR = 2 * CONV_COLS
HALO = 8


def _interleave_ff(t):
    lead = t.shape[:-1]
    nb = D_FF // CONV_COLS
    return jnp.swapaxes(t.reshape(lead + (2, nb, CONV_COLS)), -3, -2).reshape(lead + (2 * D_FF,))


def _deinterleave_ff(t):
    lead = t.shape[:-1]
    nb = D_FF // CONV_COLS
    return jnp.swapaxes(t.reshape(lead + (nb, 2, CONV_COLS)), -3, -2).reshape(lead + (2 * D_FF,))


def _causal_taps(main, halo, first):
    row = lax.broadcasted_iota(jnp.int32, (HALO, main.shape[1]), 0)
    h7 = jnp.where(first, 0.0, halo[HALO - 1:HALO, :])
    h6 = jnp.where(first, 0.0, halo[HALO - 2:HALO - 1, :])
    r1 = pltpu.roll(main, 1, 0)
    r2 = pltpu.roll(main, 2, 0)
    top1 = jnp.where(row == 0, h7, r1[0:HALO])
    top2 = jnp.where(row == 0, h6, jnp.where(row == 1, h7, r2[0:HALO]))
    return jnp.concatenate([top1, r1[HALO:]], axis=0), jnp.concatenate([top2, r2[HALO:]], axis=0)


def _conv_in_specs():
    halo_idx = lambda i: jnp.maximum(i * (CONV_ROWS // HALO) - 1, 0)
    return [pl.BlockSpec((CONV_ROWS, CONV_PAIR), lambda j, i: (i, j)),
            pl.BlockSpec((HALO, CONV_PAIR), lambda j, i: (halo_idx(i), j)),
            pl.BlockSpec((3, CONV_PAIR), lambda j, i: (0, j)),
            pl.BlockSpec((1, CONV_PAIR), lambda j, i: (0, j))]


def _ffn_act_fwd(hh, conv_w, conv_b):
    s = hh.shape[0]

    def body(m_ref, h_ref, w_ref, b_ref, o_ref):
        first = pl.program_id(1) == 0
        main = m_ref[...]
        x1, x2 = _causal_taps(main, h_ref[...], first)
        conv = w_ref[0:1, :] * x2 + w_ref[1:2, :] * x1 + w_ref[2:3, :] * main + b_ref[...]
        o_ref[...] = (_gelu(conv[:, CONV_COLS:]) * conv[:, :CONV_COLS]).astype(o_ref.dtype)

    return pl.pallas_call(
        body, name="ffn_act_fwd", grid=(D_FF // CONV_COLS, s // CONV_ROWS), in_specs=_conv_in_specs(),
        out_specs=pl.BlockSpec((CONV_ROWS, CONV_COLS), lambda j, i: (i, j)),
        out_shape=jax.ShapeDtypeStruct((s, D_FF), MXU_DTYPE),
        compiler_params=_params(("parallel", "parallel")),
    )(hh, hh, conv_w, conv_b.reshape(1, -1))


def _ffn_act_bwd(hh, conv_w, conv_b, da):
    s = hh.shape[0]

    def body(m_ref, h_ref, w_ref, b_ref, da_ref, d_ref, dw_ref, db_ref):
        first = pl.program_id(1) == 0

        @pl.when(first)
        def _():
            dw_ref[...] = jnp.zeros_like(dw_ref)
            db_ref[...] = jnp.zeros_like(db_ref)

        main = m_ref[...]
        x1, x2 = _causal_taps(main, h_ref[...], first)
        conv = w_ref[0:1, :] * x2 + w_ref[1:2, :] * x1 + w_ref[2:3, :] * main + b_ref[...]
        da = da_ref[...].astype(F32)
        act, dact = _gelu_pair(conv[:, CONV_COLS:])
        dconv = jnp.concatenate([da * act, da * conv[:, :CONV_COLS] * dact], axis=1)
        d_ref[...] = dconv.astype(d_ref.dtype)
        for t, tap in enumerate((x2, x1, main)):
            dw_ref[t:t + 1, :] += jnp.sum(dconv * tap, axis=0, keepdims=True)
        db_ref[...] += jnp.sum(dconv, axis=0, keepdims=True)

    return pl.pallas_call(
        body, name="ffn_act_bwd", grid=(D_FF // CONV_COLS, s // CONV_ROWS),
        in_specs=_conv_in_specs() + [pl.BlockSpec((CONV_ROWS, CONV_COLS), lambda j, i: (i, j))],
        out_specs=[pl.BlockSpec((CONV_ROWS, CONV_PAIR), lambda j, i: (i, j)),
                   pl.BlockSpec((3, CONV_PAIR), lambda j, i: (0, j)), pl.BlockSpec((1, CONV_PAIR), lambda j, i: (0, j))],
        out_shape=[jax.ShapeDtypeStruct((s, 2 * D_FF), MXU_DTYPE), jax.ShapeDtypeStruct((3, 2 * D_FF), F32),
                   jax.ShapeDtypeStruct((1, 2 * D_FF), F32)],
        compiler_params=_params(("parallel", "arbitrary")),
    )(hh, hh, conv_w, conv_b.reshape(1, -1), da)


def _conv_transpose(dconv, conv_w):
    s, n = dconv.shape
    nrow = s // CONV_ROWS
    halo_rows = 16

    def body(m_ref, nx_ref, w_ref, o_ref):
        main = m_ref[...].astype(F32)
        last = pl.program_id(1) == nrow - 1
        nx = nx_ref[...].astype(F32)
        n0 = jnp.where(last, 0.0, nx[0:1, :])
        n1 = jnp.where(last, 0.0, nx[1:2, :])
        row = lax.broadcasted_iota(jnp.int32, (HALO, main.shape[1]), 0)
        r1 = pltpu.roll(main, CONV_ROWS - 1, 0)
        r2 = pltpu.roll(main, CONV_ROWS - 2, 0)
        end1 = jnp.where(row == HALO - 1, n0, r1[CONV_ROWS - HALO:])
        end2 = jnp.where(row == HALO - 2, n0, jnp.where(row == HALO - 1, n1, r2[CONV_ROWS - HALO:]))
        y1 = jnp.concatenate([r1[:CONV_ROWS - HALO], end1], axis=0)
        y2 = jnp.concatenate([r2[:CONV_ROWS - HALO], end2], axis=0)
        o_ref[...] = (w_ref[2:3, :] * main + w_ref[1:2, :] * y1 + w_ref[0:1, :] * y2).astype(o_ref.dtype)

    nxt = lambda i: jnp.minimum((i + 1) * (CONV_ROWS // halo_rows), s // halo_rows - 1)
    return pl.pallas_call(
        body, name="ffn_conv_transpose", grid=(n // CONV_COLS, nrow),
        in_specs=[pl.BlockSpec((CONV_ROWS, CONV_COLS), lambda j, i: (i, j)),
                  pl.BlockSpec((halo_rows, CONV_COLS), lambda j, i: (nxt(i), j)),
                  pl.BlockSpec((3, CONV_COLS), lambda j, i: (0, j))],
        out_specs=pl.BlockSpec((CONV_ROWS, CONV_COLS), lambda j, i: (i, j)),
        out_shape=jax.ShapeDtypeStruct((s, n), MXU_DTYPE),
        compiler_params=_params(("parallel", "parallel")),
    )(dconv, dconv, conv_w)


def _ple_fwd(xn, p, w_gate, w_proj, h):
    s = xn.shape[0]
    tm = 512

    def body(x_ref, p_ref, wg_ref, wp_ref, h_ref, o_ref):
        gate = jax.nn.sigmoid(_dot(x_ref[...].astype(MXU_DTYPE), wg_ref[...].astype(MXU_DTYPE), NN))
        proj = _dot(p_ref[...].astype(MXU_DTYPE), wp_ref[...].astype(MXU_DTYPE), NN)
        o_ref[...] = h_ref[...] + gate * proj

    return pl.pallas_call(
        body, name="ple_fwd", grid=(s // tm,),
        in_specs=[pl.BlockSpec((tm, D_MODEL), lambda i: (i, 0)), pl.BlockSpec((tm, PLE_DIM), lambda i: (i, 0)),
                  pl.BlockSpec((D_MODEL, D_MODEL), lambda i: (0, 0)), pl.BlockSpec((PLE_DIM, D_MODEL), lambda i: (0, 0)),
                  pl.BlockSpec((tm, D_MODEL), lambda i: (i, 0))],
        out_specs=pl.BlockSpec((tm, D_MODEL), lambda i: (i, 0)),
        out_shape=jax.ShapeDtypeStruct((s, D_MODEL), F32),
        compiler_params=_params(("parallel",)),
    )(xn, p, w_gate, w_proj, h)


def _ple_bwd(xn, p, w_gate, w_proj, dh):
    s = xn.shape[0]
    tm = 512

    def body(x_ref, p_ref, wg_ref, wp_ref, dh_ref, dpre_ref, dproj_ref):
        gate = jax.nn.sigmoid(_dot(x_ref[...].astype(MXU_DTYPE), wg_ref[...].astype(MXU_DTYPE), NN))
        proj = _dot(p_ref[...].astype(MXU_DTYPE), wp_ref[...].astype(MXU_DTYPE), NN)
        dh = dh_ref[...]
        dpre_ref[...] = (dh * proj * gate * (1.0 - gate)).astype(dpre_ref.dtype)
        dproj_ref[...] = (dh * gate).astype(dproj_ref.dtype)

    row = pl.BlockSpec((tm, D_MODEL), lambda i: (i, 0))
    return pl.pallas_call(
        body, name="ple_bwd", grid=(s // tm,),
        in_specs=[row, pl.BlockSpec((tm, PLE_DIM), lambda i: (i, 0)),
                  pl.BlockSpec((D_MODEL, D_MODEL), lambda i: (0, 0)), pl.BlockSpec((PLE_DIM, D_MODEL), lambda i: (0, 0)),
                  row],
        out_specs=[row, row],
        out_shape=[jax.ShapeDtypeStruct((s, D_MODEL), MXU_DTYPE)] * 2,
        compiler_params=_params(("parallel",)),
    )(xn, p, w_gate, w_proj, dh)


O_SGU = 3 * ATTN_W
O_SSM = O_SGU + 2 * SGU_W


def _layer_consts(w, i):
    causal = jnp.asarray(np.tril(np.ones((SGU_CHUNK, SGU_CHUNK), np.float32)))
    return {
        "sgu_w_mask": w["sgu_w"][i] * causal,
        "sgu_b_t": w["sgu_b"][i].T,
        "ssm_ops": _ssm_operands(w["ssm_a_re"][i], w["ssm_a_im"][i], w["ssm_log_dt"][i], w["ssm_b_re"][i],
                                 w["ssm_b_im"][i], w["ssm_c_re"][i], w["ssm_c_im"][i]),
    }


def _layer_fwd(h0, p_i, w, i, bias):
    c = _layer_consts(w, i)
    w_in = w["w_in"][i]
    xn1 = _rms_fwd(h0, w["norm_attn_g"][i], name="rms_attn_fwd")
    qkv = _in_proj_qkv(xn1, w_in[:, :O_SGU])
    zs = _matmul(xn1, w_in[:, O_SGU:O_SSM], name="in_proj_sgu", out_dtype=F32, tm=1024, tn=512)
    us = _matmul(xn1, w_in[:, O_SSM:], name="in_proj_ssm", out_dtype=F32, tm=1024, tn=256)
    y_attn, lse = _attn2_fwd(qkv, bias)
    y_sgu = _sgu_fwd(zs, w["sgu_ln_g"][i], w["sgu_ln_b"][i], c["sgu_w_mask"], c["sgu_b_t"])
    y_ssm, entry = _ssm_fwd(us, c["ssm_ops"], w["ssm_d"][i], w["ssm_glu_w"][i], w["ssm_glu_b"][i])
    mix = _mix_fwd(y_attn, y_sgu, y_ssm, w["branch_norm_g"][i])
    h1 = _matmul(mix, w["w_out"][i], name="out_proj", out_dtype=F32, tm=512, tn=1024, residual=h0)
    xn2 = _rms_fwd(h1, w["norm_ffn_g"][i], name="rms_ffn_fwd")
    hh = _matmul(xn2, w["ffn_w_up"][i], name="ffn_up", out_dtype=F32, tm=1024, tn=1408)
    act = _ffn_act_fwd(hh, w["ffn_conv_w"][i], w["ffn_conv_b"][i])
    h2 = _matmul(act, w["ffn_w_down"][i], name="ffn_down", out_dtype=F32, tm=512, tn=1024, residual=h1)
    xn3 = _rms_fwd(h2, w["norm_ple_g"][i], name="rms_ple_fwd")
    h3 = _ple_fwd(xn3, p_i, w["ple_w_gate"][i], w["ple_w_proj"][i], h2)
    saved = dict(h0=h0, xn1=xn1, qkv=qkv, zs=zs, us=us, y_attn=y_attn, lse=lse, y_sgu=y_sgu, y_ssm=y_ssm,
                 entry=entry, mix=mix, h1=h1, xn2=xn2, hh=hh, act=act, h2=h2, xn3=xn3, consts=c)
    return h3, saved


def _layer_bwd(dh3, sv, p_i, w, i, bias):
    c = sv["consts"]
    g = {}
    dpre, dproj = _ple_bwd(sv["xn3"], p_i, w["ple_w_gate"][i], w["ple_w_proj"][i], dh3)
    g["ple_w_gate"] = _matmul_tn(sv["xn3"], dpre, name="d_ple_w_gate", tk=1024, tn=1024)
    g["ple_w_proj"] = _matmul_tn(p_i, dproj, name="d_ple_w_proj", tk=256, tn=1024)
    dxn3 = _matmul(dpre, w["ple_w_gate"][i], name="d_xn_ple", out_dtype=F32, tm=512, tn=1024, trans_b=True)
    dh2, g["norm_ple_g"] = _rms_bwd(sv["h2"], w["norm_ple_g"][i], dxn3, dh3, name="rms_ple_bwd")
    g["ffn_w_down"] = _matmul_tn(sv["act"], dh2, name="d_ffn_w_down", tk=1408, tn=1024)
    dact = _matmul(dh2, w["ffn_w_down"][i], name="d_ffn_act", out_dtype=MXU_DTYPE, tm=512, tn=1408, trans_b=True)
    dconv, g["ffn_conv_w"], g["ffn_conv_b"] = _ffn_act_bwd(sv["hh"], w["ffn_conv_w"][i], w["ffn_conv_b"][i], dact)
    dhh = _conv_transpose(dconv, w["ffn_conv_w"][i])
    g["ffn_w_up"] = _matmul_tn(sv["xn2"], dhh, name="d_ffn_w_up", tk=1024, tn=1408)
    dxn2 = _matmul(dhh, w["ffn_w_up"][i], name="d_xn_ffn", out_dtype=F32, tm=512, tn=512, trans_b=True)
    dh1, g["norm_ffn_g"] = _rms_bwd(sv["h1"], w["norm_ffn_g"][i], dxn2, dh2, name="rms_ffn_bwd")
    g["w_out"] = _matmul_tn(sv["mix"], dh1, name="d_w_out", tk=1024, tn=1024)
    dmix = _matmul(dh1, w["w_out"][i], name="d_mix", out_dtype=F32, tm=512, tn=1024, trans_b=True)
    dy_attn, delta, dy_sgu, dy_ssm, g["branch_norm_g"] = _mix_bwd(sv["y_attn"], sv["y_sgu"], sv["y_ssm"],
                                                                  w["branch_norm_g"][i], dmix)
    dq, dk, dv, ek, ev, dbias = _attn2_bwd(sv["qkv"], bias, sv["lse"], delta, dy_attn)
    dzs, g["sgu_ln_g"], g["sgu_ln_b"], dsw, dsb = _sgu_bwd(sv["zs"], w["sgu_ln_g"][i], w["sgu_ln_b"][i],
                                                          c["sgu_w_mask"], c["sgu_b_t"], dy_sgu)
    causal = jnp.asarray(np.tril(np.ones((SGU_CHUNK, SGU_CHUNK), np.float32)))
    g["sgu_w"] = dsw * causal
    g["sgu_b"] = dsb.T
    dus, dbb, dcm, da, g["ssm_d"], g["ssm_glu_w"], g["ssm_glu_b"] = _ssm_bwd(
        sv["us"], sv["entry"], c["ssm_ops"], w["ssm_d"][i], w["ssm_glu_w"][i], w["ssm_glu_b"][i], dy_ssm)
    dbb5 = dbb.reshape(SSM_G, SSM_C, 2, SSM_G, SSM_N)
    dbbar = jnp.einsum("gcpgn->pgnc", dbb5)
    dcm5 = dcm.reshape(2, SSM_G, SSM_N, SSM_G, SSM_C)
    dcc = jnp.einsum("pgngc->pgcn", dcm5)
    g["ssm_c_re"] = dcc[0]
    g["ssm_c_im"] = -dcc[1]
    da2 = da.reshape(2, SSM_G, SSM_N)
    _, vjp = jax.vjp(_ssm_discretize, w["ssm_a_re"][i], w["ssm_a_im"][i], w["ssm_log_dt"][i],
                     w["ssm_b_re"][i], w["ssm_b_im"][i])
    (g["ssm_a_re"], g["ssm_a_im"], g["ssm_log_dt"], g["ssm_b_re"], g["ssm_b_im"]) = vjp(
        (da2[0], da2[1], dbbar[0], dbbar[1]))
    dz = _attn2_bwd_sum(dq, dk, dv, ek, ev, dzs, dus)
    g["w_in"] = _matmul_tn(sv["xn1"], dz, name="d_w_in", tk=1024, tn=1152)
    dxn1 = _matmul(dz, w["w_in"][i], name="d_xn_attn", out_dtype=F32, tm=512, tn=1024, trans_b=True)
    dh0, g["norm_attn_g"] = _rms_bwd(sv["h0"], w["norm_attn_g"][i], dxn1, dh1, name="rms_attn_bwd")
    for k in ("norm_ple_g", "norm_ffn_g", "branch_norm_g", "norm_attn_g", "sgu_ln_g", "sgu_ln_b", "ssm_d",
              "ssm_glu_b", "ffn_conv_b"):
        g[k] = g[k].reshape(-1)
    return dh0, g, dbias


def _local_step(x, p, target, w):
    ff_names = ("ffn_w_up", "ffn_conv_w", "ffn_conv_b")
    w = dict(w)
    for k in ff_names:
        w[k] = _interleave_ff(w[k])
    bias = _bias_build(w["rel_bias"])
    h = x
    saved = []
    for i in range(DEPTH):
        h, sv = _layer_fwd(h, p[i], w, i, bias)
        saved.append(sv)
    loss, dh, dgf = _loss_head(h, w["final_norm_g"], target)
    layer_grads = [None] * DEPTH
    dbias = None
    for i in reversed(range(DEPTH)):
        dh, layer_grads[i], db = _layer_bwd(dh, saved[i], p[i], w, i, bias)
        dbias = db if dbias is None else dbias + db
    grads = {k: jnp.stack([layer_grads[i][k] for i in range(DEPTH)]) for k in layer_grads[0]}
    for k in ff_names:
        grads[k] = _deinterleave_ff(grads[k])
    grads["rel_bias"] = _bias_reduce(dbias)
    grads["final_norm_g"] = dgf.reshape(-1)
    return loss, dh, grads


def _pad_rows(a2, mult=16):
    r = (-a2.shape[0]) % mult
    return a2 if r == 0 else jnp.concatenate([a2, jnp.zeros((r, a2.shape[1]), a2.dtype)], axis=0)


def _as_rows(a, rows=None):
    flat = a.reshape(-1)
    if rows is None:
        rows = -(-flat.shape[0] // (16 * PACK_COLS)) * 16
    return jnp.pad(flat, (0, rows * PACK_COLS - flat.shape[0])).reshape(rows, PACK_COLS)


def _shard_shape(name):
    full, ax = BIG_FULL[name]
    shp = [DEPTH] + list(full)
    shp[ax] //= N_CHIPS
    return tuple(shp)


EXACT_NAMES = ("ffn_conv_w",)


def _pack_rows_of(name):
    n = int(np.prod(_shard_shape(name))) * (2 if name in EXACT_NAMES else 1)
    rows = -(-n // PACK_COLS)
    return -(-rows // 16) * 16


def _pack_shards(shards, dtype, exact=False):
    split_words = exact and jnp.dtype(dtype).itemsize == 2
    parts = []
    for n in BIG_NAMES:
        a = shards[n]
        if split_words and n in EXACT_NAMES:
            a = lax.bitcast_convert_type(a.astype(F32), dtype)
        parts.append(_as_rows(a.astype(dtype), _pack_rows_of(n)))
    used = sum(pt.shape[0] for pt in parts)
    parts.append(jnp.zeros((PACK_ROWS - used, PACK_COLS), dtype))
    return jnp.concatenate(parts, axis=0)


def _unpack_shard(flat, name, exact=False):
    off = 0
    for n in BIG_NAMES:
        if n == name:
            break
        off += _pack_rows_of(n)
    shp = _shard_shape(name)
    cnt = int(np.prod(shp))
    vec = flat[off:off + _pack_rows_of(name)].reshape(-1)
    if exact and name in EXACT_NAMES and jnp.dtype(flat.dtype).itemsize == 2:
        return lax.bitcast_convert_type(vec[:2 * cnt].reshape(shp + (2,)), F32)
    return vec[:cnt].reshape(shp)


def _split_full(full, name):
    _, ax = BIG_FULL[name]
    return jnp.stack(jnp.split(full, N_CHIPS, axis=ax))


def _join_shards(stacked, name):
    _, ax = BIG_FULL[name]
    return jnp.concatenate([stacked[k] for k in range(N_CHIPS)], axis=ax)


def _small_shapes(w):
    return [(n, w[n].shape) for n in SMALL_NAMES]


def _pack_small(d):
    flat = jnp.concatenate([d[n].astype(F32).reshape(-1) for n in SMALL_NAMES])
    flat = jnp.concatenate([flat, jnp.zeros((SMALL_ROWS * PACK_COLS - flat.shape[0],), F32)])
    return flat.reshape(SMALL_ROWS, PACK_COLS)


def _unpack_small(flat, shapes):
    out, off = {}, 0
    v = flat.reshape(-1)
    for n, shp in shapes:
        cnt = int(np.prod(shp))
        out[n] = v[off:off + cnt].reshape(shp)
        off += cnt
    return out


MESH = pl.DeviceIdType.MESH
ANY = pl.BlockSpec(memory_space=pl.ANY)


def _me():
    return lax.axis_index("x"), lax.axis_index("y"), lax.axis_index("c")


def _other_chips(x, y):
    return [(1 - x, y), (x, 1 - y), (1 - x, 1 - y)]


def _gather_weights(wflat):
    def body(w_ref, out_ref, send_sems, recv_sems, local_sem):
        x, y, c = _me()
        sibling = (x, y, 1 - c)
        chips = _other_chips(x, y)

        def rows(chip, half):
            return out_ref.at[2 * chip[0] + chip[1], pl.ds(half * PACK_HALF, PACK_HALF), :]

        def copy(k, chip, half, to, src=None):
            return pltpu.make_async_remote_copy(
                src_ref=rows(chip, half) if src is None else src, dst_ref=rows(chip, half),
                send_sem=send_sems.at[k], recv_sem=recv_sems.at[k], device_id=to, device_id_type=MESH)

        mine = pltpu.make_async_copy(w_ref, out_ref.at[2 * x + y], local_sem)
        mine.start()
        my_half = w_ref.at[pl.ds(c * PACK_HALF, PACK_HALF), :]
        first = [copy(j, (x, y), c, (*chip, c), src=my_half) for j, chip in enumerate(chips)]
        for cp in first:
            cp.start()
        passed = [copy(3 + j, chip, c, sibling) for j, chip in enumerate(chips)]
        for j, chip in enumerate(chips):
            copy(j, chip, c, (x, y, c)).wait_recv()
            passed[j].start()
        for j, chip in enumerate(chips):
            copy(3 + j, chip, 1 - c, (x, y, c)).wait_recv()
        for cp in first + passed:
            cp.wait_send()
        mine.wait()

    return pl.pallas_call(
        body, name="gather_weights", in_specs=[ANY], out_specs=ANY,
        out_shape=jax.ShapeDtypeStruct((N_CHIPS, PACK_ROWS, PACK_COLS), wflat.dtype),
        scratch_shapes=[pltpu.SemaphoreType.DMA((6,)), pltpu.SemaphoreType.DMA((6,)), pltpu.SemaphoreType.DMA],
    )(wflat)


def _exchange_partials(gb, gs):
    def body(gb_ref, gs_ref, half_ref, small_ref, send_sems, recv_sems, local_sem):
        x, y, c = _me()
        me_idx = 4 * x + 2 * y + c
        mine = pltpu.make_async_copy(gs_ref, small_ref.at[me_idx], local_sem)
        mine.start()
        d2d = pltpu.make_async_remote_copy(
            src_ref=gb_ref.at[:, pl.ds((1 - c) * PACK_HALF, PACK_HALF), :], dst_ref=half_ref,
            send_sem=send_sems.at[0], recv_sem=recv_sems.at[0], device_id=(x, y, 1 - c), device_id_type=MESH)
        d2d.start()
        copies = []
        for k in range(1, N_DEV):
            fx, fy, fc = (k >> 2) & 1, (k >> 1) & 1, k & 1
            peer = (x ^ fx, y ^ fy, c ^ fc)
            copies.append(pltpu.make_async_remote_copy(
                src_ref=gs_ref, dst_ref=small_ref.at[me_idx], send_sem=send_sems.at[k], recv_sem=recv_sems.at[k],
                device_id=peer, device_id_type=MESH))
        for cp in copies:
            cp.start()
        for k in range(1, N_DEV):
            fx, fy, fc = (k >> 2) & 1, (k >> 1) & 1, k & 1
            peer_idx = 4 * (x ^ fx) + 2 * (y ^ fy) + (c ^ fc)
            pltpu.make_async_remote_copy(
                src_ref=gs_ref, dst_ref=small_ref.at[peer_idx], send_sem=send_sems.at[k], recv_sem=recv_sems.at[k],
                device_id=(x, y, c), device_id_type=MESH).wait_recv()
        d2d.wait_recv()
        d2d.wait_send()
        for cp in copies:
            cp.wait_send()
        mine.wait()

    return pl.pallas_call(
        body, name="exchange_partials", in_specs=[ANY, ANY], out_specs=[ANY, ANY],
        out_shape=[jax.ShapeDtypeStruct((N_CHIPS, PACK_HALF, PACK_COLS), gb.dtype),
                   jax.ShapeDtypeStruct((N_DEV, SMALL_ROWS, PACK_COLS), F32)],
        scratch_shapes=[pltpu.SemaphoreType.DMA((N_DEV,)), pltpu.SemaphoreType.DMA((N_DEV,)), pltpu.SemaphoreType.DMA],
    )(gb, gs)


RED_ROWS = 256


def _chip_partials(gb, sib, c_idx):
    nrow = PACK_HALF // RED_ROWS

    def body(c_ref, a_ref, b_ref, o_ref):
        del c_ref
        o_ref[...] = (a_ref[...].astype(F32) + b_ref[...].astype(F32)).astype(o_ref.dtype)

    blk = (1, RED_ROWS, PACK_COLS)
    return pl.pallas_call(
        body, name="chip_partials",
        grid_spec=pltpu.PrefetchScalarGridSpec(
            num_scalar_prefetch=1, grid=(N_CHIPS, nrow),
            in_specs=[pl.BlockSpec(blk, lambda k, i, c: (k, c[0] * nrow + i, 0)),
                      pl.BlockSpec(blk, lambda k, i, c: (k, i, 0))],
            out_specs=pl.BlockSpec(blk, lambda k, i, c: (k, i, 0))),
        out_shape=jax.ShapeDtypeStruct((N_CHIPS, PACK_HALF, PACK_COLS), gb.dtype),
        compiler_params=_params(("parallel", "parallel")),
    )(c_idx, gb, sib)


def _scatter_partials(pc):
    def body(pc_ref, out_ref, send_sems, recv_sems):
        x, y, c = _me()
        chips = _other_chips(x, y)
        copies = [pltpu.make_async_remote_copy(
            src_ref=pc_ref.at[2 * chip[0] + chip[1]], dst_ref=out_ref.at[k],
            send_sem=send_sems.at[k], recv_sem=recv_sems.at[k], device_id=(*chip, c), device_id_type=MESH)
            for k, chip in enumerate(chips)]
        for cp in copies:
            cp.start()
        for cp in copies:
            cp.wait_recv()
        for cp in copies:
            cp.wait_send()

    return pl.pallas_call(
        body, name="scatter_partials", in_specs=[ANY], out_specs=ANY,
        out_shape=jax.ShapeDtypeStruct((3, PACK_HALF, PACK_COLS), pc.dtype),
        scratch_shapes=[pltpu.SemaphoreType.DMA((3,)), pltpu.SemaphoreType.DMA((3,))],
    )(pc)


def _final_half(gb, sib, recv, idx):
    nrow = PACK_HALF // RED_ROWS

    def body(idx_ref, a_ref, b_ref, r_ref, o_ref):
        del idx_ref
        acc = a_ref[0].astype(F32) + b_ref[0].astype(F32)
        for k in range(3):
            acc = acc + r_ref[k].astype(F32)
        o_ref[...] = acc

    return pl.pallas_call(
        body, name="final_half",
        grid_spec=pltpu.PrefetchScalarGridSpec(
            num_scalar_prefetch=1, grid=(nrow,),
            in_specs=[pl.BlockSpec((1, RED_ROWS, PACK_COLS), lambda i, idx: (idx[0], idx[1] * nrow + i, 0)),
                      pl.BlockSpec((1, RED_ROWS, PACK_COLS), lambda i, idx: (idx[0], i, 0)),
                      pl.BlockSpec((3, RED_ROWS, PACK_COLS), lambda i, idx: (0, i, 0))],
            out_specs=pl.BlockSpec((RED_ROWS, PACK_COLS), lambda i, idx: (i, 0))),
        out_shape=jax.ShapeDtypeStruct((PACK_HALF, PACK_COLS), F32),
        compiler_params=_params(("parallel",)),
    )(idx, gb, sib, recv)


def _share_halves(half):
    def body(h_ref, out_ref, send_sem, recv_sem, local_sem):
        x, y, c = _me()
        dst = out_ref.at[pl.ds(c * PACK_HALF, PACK_HALF), :]
        mine = pltpu.make_async_copy(h_ref, dst, local_sem)
        mine.start()
        cp = pltpu.make_async_remote_copy(src_ref=h_ref, dst_ref=dst, send_sem=send_sem, recv_sem=recv_sem,
                                          device_id=(x, y, 1 - c), device_id_type=MESH)
        cp.start()
        pltpu.make_async_remote_copy(src_ref=h_ref, dst_ref=out_ref.at[pl.ds((1 - c) * PACK_HALF, PACK_HALF), :],
                                     send_sem=send_sem, recv_sem=recv_sem, device_id=(x, y, c),
                                     device_id_type=MESH).wait_recv()
        cp.wait_send()
        mine.wait()

    return pl.pallas_call(
        body, name="share_halves", in_specs=[ANY], out_specs=ANY,
        out_shape=jax.ShapeDtypeStruct((PACK_ROWS, PACK_COLS), F32),
        scratch_shapes=[pltpu.SemaphoreType.DMA, pltpu.SemaphoreType.DMA, pltpu.SemaphoreType.DMA],
    )(half)


def _sum_small(allsmall):
    def body(a_ref, o_ref):
        acc = a_ref[0]
        for k in range(1, N_DEV):
            acc = acc + a_ref[k]
        o_ref[...] = acc

    tr = 96
    return pl.pallas_call(
        body, name="sum_small", grid=(SMALL_ROWS // tr,),
        in_specs=[pl.BlockSpec((N_DEV, tr, PACK_COLS), lambda i: (0, i, 0))],
        out_specs=pl.BlockSpec((tr, PACK_COLS), lambda i: (i, 0)),
        out_shape=jax.ShapeDtypeStruct((SMALL_ROWS, PACK_COLS), F32),
        compiler_params=_params(("parallel",)),
    )(allsmall)


def _adamw(w, g, m, v, *, name):
    shape = w.shape
    cols = shape[-1]
    as2 = lambda t: t.reshape(-1, cols)
    w2, g2, m2, v2 = as2(w), as2(g), as2(m), as2(v)
    rows = w2.shape[0]
    tr = rows
    if rows * cols * 4 > (1 << 20):
        tr = _tile(rows, max(8, (1 << 20) // (cols * 4) // 8 * 8), 8)

    def body(w_ref, g_ref, m_ref, v_ref, d_ref, mo_ref, vo_ref):
        gg = g_ref[...]
        mn = ADAM_B1 * m_ref[...] + (1.0 - ADAM_B1) * gg
        vn = ADAM_B2 * v_ref[...] + (1.0 - ADAM_B2) * (gg * gg)
        m_hat = mn / (1.0 - ADAM_B1 ** ADAM_STEP)
        v_hat = vn / (1.0 - ADAM_B2 ** ADAM_STEP)
        d_ref[...] = -ADAM_LR * (m_hat / (jnp.sqrt(v_hat) + ADAM_EPS) + ADAM_WD * w_ref[...])
        mo_ref[...] = mn
        vo_ref[...] = vn

    blk = pl.BlockSpec((tr, cols), lambda i: (i, 0))
    outs = pl.pallas_call(
        body, name=name, grid=(rows // tr,), in_specs=[blk] * 4, out_specs=[blk] * 3,
        out_shape=[jax.ShapeDtypeStruct((rows, cols), F32)] * 3,
        compiler_params=_params(("parallel",)),
    )(w2, g2, m2, v2)
    return tuple(t.reshape(shape) for t in outs)


def kernel(x, p, rel_bias, norm_attn_g, w_in, sgu_ln_g, sgu_ln_b, sgu_w, sgu_b, ssm_a_re, ssm_a_im, ssm_log_dt, ssm_b_re, ssm_b_im, ssm_c_re, ssm_c_im, ssm_d, ssm_glu_w, ssm_glu_b, branch_norm_g, w_out, norm_ffn_g, ffn_w_up, ffn_conv_w, ffn_conv_b, ffn_w_down, norm_ple_g, ple_w_gate, ple_w_proj, final_norm_g, loss_target, m_rel_bias, m_norm_attn_g, m_w_in, m_sgu_ln_g, m_sgu_ln_b, m_sgu_w, m_sgu_b, m_ssm_a_re, m_ssm_a_im, m_ssm_log_dt, m_ssm_b_re, m_ssm_b_im, m_ssm_c_re, m_ssm_c_im, m_ssm_d, m_ssm_glu_w, m_ssm_glu_b, m_branch_norm_g, m_w_out, m_norm_ffn_g, m_ffn_w_up, m_ffn_conv_w, m_ffn_conv_b, m_ffn_w_down, m_norm_ple_g, m_ple_w_gate, m_ple_w_proj, m_final_norm_g, v_rel_bias, v_norm_attn_g, v_w_in, v_sgu_ln_g, v_sgu_ln_b, v_sgu_w, v_sgu_b, v_ssm_a_re, v_ssm_a_im, v_ssm_log_dt, v_ssm_b_re, v_ssm_b_im, v_ssm_c_re, v_ssm_c_im, v_ssm_d, v_ssm_glu_w, v_ssm_glu_b, v_branch_norm_g, v_w_out, v_norm_ffn_g, v_ffn_w_up, v_ffn_conv_w, v_ffn_conv_b, v_ffn_w_down, v_norm_ple_g, v_ple_w_gate, v_ple_w_proj, v_final_norm_g):
    args = dict(locals())
    wts = {n: args[n] for n in WEIGHT_NAMES}
    mom_m = {n: args["m_" + n] for n in WEIGHT_NAMES}
    mom_v = {n: args["v_" + n] for n in WEIGHT_NAMES}

    wall = _gather_weights(_pack_shards({n: wts[n] for n in BIG_NAMES}, MXU_DTYPE, exact=True))
    full = dict(wts)
    for n in BIG_NAMES:
        full[n] = _join_shards(jnp.stack([_unpack_shard(wall[k], n, exact=True) for k in range(N_CHIPS)]), n)
    full["ffn_conv_w"] = full["ffn_conv_w"].astype(F32)

    loss, dx, grads = _local_step(x[0], p[:, 0], loss_target[0], full)
    loss = lax.psum(loss[0, 0], MESH_AXES)

    xi, yi, ci = _me()
    stacked = {n: _split_full(grads[n], n) for n in BIG_NAMES}
    gb = jnp.stack([_pack_shards({n: stacked[n][k] for n in BIG_NAMES}, MXU_DTYPE) for k in range(N_CHIPS)])
    gs = _pack_small(grads)
    sib, allsmall = _exchange_partials(gb, gs)
    pc = _chip_partials(gb, sib, jnp.stack([ci]).astype(jnp.int32))
    recv = _scatter_partials(pc)
    half = _final_half(gb, sib, recv, jnp.stack([2 * xi + yi, ci]).astype(jnp.int32))
    gflat = _share_halves(half)
    gsmall = _unpack_small(_sum_small(allsmall), _small_shapes(wts))

    g_out, d_out, m_out, v_out = {}, {}, {}, {}
    for n in BIG_NAMES:
        g_out[n] = _unpack_shard(gflat, n)
        d_out[n], m_out[n], v_out[n] = _adamw(wts[n], g_out[n], mom_m[n], mom_v[n], name="adamw_" + n)
    sw = _pack_small(wts)
    d_s, m_s, v_s = _adamw(sw, _pack_small(gsmall), _pack_small(mom_m), _pack_small(mom_v), name="adamw_small")
    shapes = _small_shapes(wts)
    d_sm, m_sm, v_sm = _unpack_small(d_s, shapes), _unpack_small(m_s, shapes), _unpack_small(v_s, shapes)
    for n in SMALL_NAMES:
        g_out[n], d_out[n], m_out[n], v_out[n] = gsmall[n], d_sm[n], m_sm[n], v_sm[n]

    return (loss, dx[None], *[g_out[n] for n in WEIGHT_NAMES], *[d_out[n] for n in WEIGHT_NAMES],
            *[m_out[n] for n in WEIGHT_NAMES], *[v_out[n] for n in WEIGHT_NAMES])
```

```python
import functools
import math

import numpy as np
import jax
import jax.numpy as jnp
from jax import lax
from jax.experimental import pallas as pl
from jax.experimental.pallas import tpu as pltpu

F32 = jnp.float32
MXU_DTYPE = jnp.bfloat16
VMEM_LIMIT_BYTES = 52 * 1024 * 1024

D_MODEL = 1024
DEPTH = 2
PLE_DIM = 256
HEAD_DIM = 64
N_HEADS = 8
ATTN_W = 512
QBLK = 128
BRANCH_DIL = (1, 4, 16)
N_BUCKETS = 32
REL_MAX_DIST = 2048
SGU_W = 256
SGU_G = 4
SGU_GW = 64
SGU_CHUNK = 128
SSM_W = 256
SSM_G = 16
SSM_C = 16
SSM_N = 64
NSTATE = SSM_G * SSM_N
D_FF = 2816
EPS = 1e-6
NEG_INF = -1e30
ATTN_SCALE = HEAD_DIM ** -0.5

ADAM_LR = 0.001
ADAM_B1 = 0.9
ADAM_B2 = 0.999
ADAM_EPS = 1e-08
ADAM_WD = 0.01
ADAM_STEP = 10

SSM_NSEG = 8
SSM_TSEG = 64
SSM_TB = SSM_NSEG * SSM_TSEG
SSM_LANE_CHUNK = 512

MESH_AXES = ("x", "y", "c")
N_CHIPS = 4
N_DEV = 8

BIG_NAMES = ("w_in", "ssm_glu_w", "w_out", "ffn_w_up", "ffn_conv_w", "ffn_w_down", "ple_w_gate", "ple_w_proj")
BIG_FULL = {
    "w_in": ((D_MODEL, 2304), 2),
    "ssm_glu_w": ((SSM_W, SSM_W), 1),
    "w_out": ((D_MODEL, D_MODEL), 1),
    "ffn_w_up": ((D_MODEL, 2 * D_FF), 2),
    "ffn_conv_w": ((3, 2 * D_FF), 2),
    "ffn_w_down": ((D_FF, D_MODEL), 1),
    "ple_w_gate": ((D_MODEL, D_MODEL), 1),
    "ple_w_proj": ((PLE_DIM, D_MODEL), 2),
}
PACK_COLS = 1024
PACK_ROWS = 6656
PACK_HALF = PACK_ROWS // 2

SMALL_NAMES = ("rel_bias", "norm_attn_g", "sgu_ln_g", "sgu_ln_b", "sgu_w", "sgu_b", "ssm_a_re", "ssm_a_im",
               "ssm_log_dt", "ssm_b_re", "ssm_b_im", "ssm_c_re", "ssm_c_im", "ssm_d", "ssm_glu_b",
               "branch_norm_g", "norm_ffn_g", "ffn_conv_b", "norm_ple_g", "final_norm_g")
SMALL_ROWS = 288

WEIGHT_NAMES = ("rel_bias", "norm_attn_g", "w_in", "sgu_ln_g", "sgu_ln_b", "sgu_w", "sgu_b", "ssm_a_re", "ssm_a_im",
                "ssm_log_dt", "ssm_b_re", "ssm_b_im", "ssm_c_re", "ssm_c_im", "ssm_d", "ssm_glu_w", "ssm_glu_b",
                "branch_norm_g", "w_out", "norm_ffn_g", "ffn_w_up", "ffn_conv_w", "ffn_conv_b", "ffn_w_down",
                "norm_ple_g", "ple_w_gate", "ple_w_proj", "final_norm_g")


def _params(sem):
    return pltpu.CompilerParams(dimension_semantics=sem, vmem_limit_bytes=VMEM_LIMIT_BYTES)


def _tile(n, cap, mult=128):
    if n <= cap:
        return n
    best = None
    for t in range(mult, cap + 1, mult):
        if n % t == 0:
            best = t
    assert best is not None, (n, cap)
    return best


def _gelu(x):
    return 0.5 * x * (1.0 + jnp.tanh(0.7978845608028654 * (x + 0.044715 * x * x * x)))


def _gelu_pair(x):
    x2 = x * x
    t = jnp.tanh(0.7978845608028654 * x * (1.0 + 0.044715 * x2))
    half = 0.5 * (1.0 + t)
    return x * half, half + 0.5 * x * (1.0 - t * t) * (0.7978845608028654 + 3.0 * 0.044715 * 0.7978845608028654 * x2)


def _dot(a, b, dims):
    return lax.dot_general(a, b, (dims, ((), ())), preferred_element_type=F32)


def _dotf(a, b, dims):
    return _dot(a.astype(MXU_DTYPE), b.astype(MXU_DTYPE), dims)


NN = ((1,), (0,))
NT = ((1,), (1,))
TN = ((0,), (0,))


def _matmul(a, b, *, name, out_dtype, tm, tn, trans_b=False, residual=None):
    m, k = a.shape
    n = b.shape[0] if trans_b else b.shape[1]
    tm = _tile(m, tm, 8)
    tn = _tile(n, tn)
    dims = NT if trans_b else NN

    def body(*refs):
        if residual is None:
            a_ref, b_ref, o_ref = refs
        else:
            a_ref, b_ref, r_ref, o_ref = refs
        acc = _dot(a_ref[...].astype(MXU_DTYPE), b_ref[...].astype(MXU_DTYPE), dims)
        if residual is not None:
            acc = acc + r_ref[...]
        o_ref[...] = acc.astype(o_ref.dtype)

    b_spec = (pl.BlockSpec((tn, k), lambda i, j: (j, 0)) if trans_b
              else pl.BlockSpec((k, tn), lambda i, j: (0, j)))
    in_specs = [pl.BlockSpec((tm, k), lambda i, j: (i, 0)), b_spec]
    args = [a, b]
    if residual is not None:
        in_specs.append(pl.BlockSpec((tm, tn), lambda i, j: (i, j)))
        args.append(residual)
    return pl.pallas_call(
        body, name=name, grid=(m // tm, n // tn), in_specs=in_specs,
        out_specs=pl.BlockSpec((tm, tn), lambda i, j: (i, j)),
        out_shape=jax.ShapeDtypeStruct((m, n), out_dtype),
        compiler_params=_params(("parallel", "parallel")),
    )(*args)


def _matmul_tn(a, g, *, name, tk, tn, tm=512):
    m, k = a.shape
    n = g.shape[1]
    tk = _tile(k, tk)
    tn = _tile(n, tn)
    tm = _tile(m, tm, 8)

    def body(a_ref, g_ref, o_ref):
        @pl.when(pl.program_id(2) == 0)
        def _():
            o_ref[...] = jnp.zeros_like(o_ref)

        o_ref[...] += _dot(a_ref[...].astype(MXU_DTYPE), g_ref[...].astype(MXU_DTYPE), TN)

    return pl.pallas_call(
        body, name=name, grid=(k // tk, n // tn, m // tm),
        in_specs=[pl.BlockSpec((tm, tk), lambda i, j, s: (s, i)),
                  pl.BlockSpec((tm, tn), lambda i, j, s: (s, j))],
        out_specs=pl.BlockSpec((tk, tn), lambda i, j, s: (i, j)),
        out_shape=jax.ShapeDtypeStruct((k, n), F32),
        compiler_params=_params(("parallel", "parallel", "arbitrary")),
    )(a, g)


ROWS = 512


def _rms_fwd(h, g, *, name):
    s, d = h.shape

    def body(h_ref, g_ref, o_ref):
        x = h_ref[...]
        r = lax.rsqrt(jnp.mean(x * x, axis=-1, keepdims=True) + EPS)
        o_ref[...] = (x * r * g_ref[...]).astype(o_ref.dtype)

    return pl.pallas_call(
        body, name=name, grid=(s // ROWS,),
        in_specs=[pl.BlockSpec((ROWS, d), lambda i: (i, 0)), pl.BlockSpec((1, d), lambda i: (0, 0))],
        out_specs=pl.BlockSpec((ROWS, d), lambda i: (i, 0)),
        out_shape=jax.ShapeDtypeStruct((s, d), MXU_DTYPE),
        compiler_params=_params(("parallel",)),
    )(h, g.reshape(1, d))


def _rms_bwd(h, g, dxn, dres, *, name):
    s, d = h.shape

    def body(h_ref, g_ref, dxn_ref, dres_ref, dh_ref, dg_ref):
        @pl.when(pl.program_id(0) == 0)
        def _():
            dg_ref[...] = jnp.zeros_like(dg_ref)

        x = h_ref[...]
        r = lax.rsqrt(jnp.mean(x * x, axis=-1, keepdims=True) + EPS)
        xhat = x * r
        dxn = dxn_ref[...].astype(F32)
        dg_ref[...] += jnp.sum(dxn * xhat, axis=0, keepdims=True)
        dxh = dxn * g_ref[...]
        dh_ref[...] = dres_ref[...] + r * (dxh - xhat * jnp.mean(dxh * xhat, axis=-1, keepdims=True))

    row = pl.BlockSpec((ROWS, d), lambda i: (i, 0))
    vec = pl.BlockSpec((1, d), lambda i: (0, 0))
    return pl.pallas_call(
        body, name=name, grid=(s // ROWS,), in_specs=[row, vec, row, row], out_specs=[row, vec],
        out_shape=[jax.ShapeDtypeStruct((s, d), F32), jax.ShapeDtypeStruct((1, d), F32)],
        compiler_params=_params(("arbitrary",)),
    )(h, g.reshape(1, d), dxn, dres)


def _loss_head(h, g, target):
    s, d = h.shape

    def body(h_ref, g_ref, t_ref, loss_ref, dh_ref, dg_ref):
        @pl.when(pl.program_id(0) == 0)
        def _():
            loss_ref[...] = jnp.zeros_like(loss_ref)
            dg_ref[...] = jnp.zeros_like(dg_ref)

        x = h_ref[...]
        r = lax.rsqrt(jnp.mean(x * x, axis=-1, keepdims=True) + EPS)
        xhat = x * r
        err = xhat * g_ref[...] - t_ref[...]
        loss_ref[...] += 0.5 * jnp.sum(jnp.mean(err * err, axis=-1, keepdims=True), axis=0, keepdims=True)
        dy = err / d
        dg_ref[...] += jnp.sum(dy * xhat, axis=0, keepdims=True)
        dxh = dy * g_ref[...]
        dh_ref[...] = r * (dxh - xhat * jnp.mean(dxh * xhat, axis=-1, keepdims=True))

    row = pl.BlockSpec((ROWS, d), lambda i: (i, 0))
    vec = pl.BlockSpec((1, d), lambda i: (0, 0))
    one = pl.BlockSpec((1, 1), lambda i: (0, 0))
    return pl.pallas_call(
        body, name="loss_head", grid=(s // ROWS,), in_specs=[row, vec, row], out_specs=[one, row, vec],
        out_shape=[jax.ShapeDtypeStruct((1, 1), F32), jax.ShapeDtypeStruct((s, d), F32),
                   jax.ShapeDtypeStruct((1, d), F32)],
        compiler_params=_params(("arbitrary",)),
    )(h, g.reshape(1, d), target)


def _t5_bucket(dist):
    max_exact = N_BUCKETS // 2
    dd = np.maximum(dist, 0)
    large = max_exact + (np.log(np.maximum(dd, 1) / max_exact) / np.log(REL_MAX_DIST / max_exact)
                         * (N_BUCKETS - max_exact)).astype(np.int32)
    large = np.minimum(large, N_BUCKETS - 1)
    return np.where(dd < max_exact, dd, large).astype(np.int32)


def _bucket_table():
    qq = np.arange(QBLK)[:, None]
    kk = np.arange(QBLK)[None, :]
    out = np.zeros((len(BRANCH_DIL), 2, QBLK, QBLK), np.int32)
    for b, dil in enumerate(BRANCH_DIL):
        out[b, 0] = _t5_bucket((qq - kk + QBLK) * dil)
        out[b, 1] = _t5_bucket((qq - kk) * dil)
    return out


BIAS_TILE = 2 * QBLK


def _bias_build(rel_bias):
    idx = jnp.asarray(_bucket_table())

    def body(idx_ref, rb_ref, o_ref):
        ch = pl.program_id(1)
        row = lax.broadcasted_iota(jnp.int32, (QBLK, QBLK), 0)
        col = lax.broadcasted_iota(jnp.int32, (QBLK, QBLK), 1)
        for part in range(2):
            ids = idx_ref[0, 1 - part]
            valid = (col <= row) if part == 0 else (col >= row)
            for h in range(2):
                acc = jnp.zeros((QBLK, QBLK), F32)
                for b in range(N_BUCKETS):
                    acc = jnp.where(ids == b, rb_ref[b, 2 * ch + h], acc)
                o_ref[0, 0, QBLK * h:QBLK * (h + 1), QBLK * part:QBLK * (part + 1)] = jnp.where(valid, acc, NEG_INF)

    return pl.pallas_call(
        body, name="attn_bias_build", grid=(len(BRANCH_DIL), N_HEADS // 2),
        in_specs=[pl.BlockSpec((1, 2, QBLK, QBLK), lambda b, c: (b, 0, 0, 0)),
                  pl.BlockSpec(memory_space=pltpu.SMEM)],
        out_specs=pl.BlockSpec((1, 1, BIAS_TILE, BIAS_TILE), lambda b, c: (b, c, 0, 0)),
        out_shape=jax.ShapeDtypeStruct((len(BRANCH_DIL), N_HEADS // 2, BIAS_TILE, BIAS_TILE), F32),
        compiler_params=_params(("parallel", "parallel")),
    )(idx, rel_bias)


def _bias_reduce(dbias):
    idx = jnp.asarray(_bucket_table())
    nb = len(BRANCH_DIL)

    def body(idx_ref, d_ref, o_ref):
        def per_bucket(b, carry):
            for h in range(N_HEADS):
                tot = jnp.zeros((), F32)
                for br in range(nb):
                    for part in range(2):
                        tile = d_ref[br, h // 2, QBLK * (h % 2):QBLK * (h % 2 + 1), QBLK * part:QBLK * (part + 1)]
                        tot = tot + jnp.sum(jnp.where(idx_ref[br, 1 - part] == b, tile, 0.0))
                o_ref[b, h] = tot
            return carry

        lax.fori_loop(0, N_BUCKETS, per_bucket, 0)

    return pl.pallas_call(
        body, name="attn_bias_reduce",
        in_specs=[pl.BlockSpec(memory_space=pltpu.VMEM), pl.BlockSpec(memory_space=pltpu.VMEM)],
        out_specs=pl.BlockSpec(memory_space=pltpu.SMEM),
        out_shape=jax.ShapeDtypeStruct((N_BUCKETS, N_HEADS), F32),
        compiler_params=pltpu.CompilerParams(vmem_limit_bytes=VMEM_LIMIT_BYTES),
    )(idx, dbias)


def _band_masks(c):
    row = lax.broadcasted_iota(jnp.int32, (QBLK, QBLK), 0)
    col = lax.broadcasted_iota(jnp.int32, (QBLK, QBLK), 1)
    mask_cur = col <= row
    mask_prev = jnp.logical_and(col >= row, c > 0)
    return mask_prev, mask_cur


def _attn_specs(dil):
    blk = (QBLK, ATTN_W)
    q = pl.BlockSpec(blk, lambda r, c: (c, 3 * r))
    kp = pl.BlockSpec(blk, lambda r, c: (jnp.maximum(c - 1, 0), 3 * r + 1))
    kc = pl.BlockSpec(blk, lambda r, c: (c, 3 * r + 1))
    vp = pl.BlockSpec(blk, lambda r, c: (jnp.maximum(c - 1, 0), 3 * r + 2))
    vc = pl.BlockSpec(blk, lambda r, c: (c, 3 * r + 2))
    return [q, kp, kc, vp, vc]


def _attn_fwd_branch(qkv, bias, state, *, branch, last):
    dil = BRANCH_DIL[branch]
    s = qkv.shape[0]
    n = s // dil
    nblk = n // QBLK
    first = state is None

    def body(*refs):
        q_ref, kp_ref, kc_ref, vp_ref, vc_ref, b_ref = refs[:6]
        if first:
            outs = refs[6:]
        else:
            acc_ref, m_ref, l_ref = refs[6:9]
            outs = refs[9:]
        mask_prev, mask_cur = _band_masks(pl.program_id(1))
        for h in range(N_HEADS):
            sl = slice(HEAD_DIM * h, HEAD_DIM * (h + 1))
            qh = q_ref[:, sl]
            s_c = _dot(qh, kc_ref[:, sl], NT) * ATTN_SCALE + b_ref[0, 1, h]
            s_p = _dot(qh, kp_ref[:, sl], NT) * ATTN_SCALE + b_ref[0, 0, h]
            s_c = jnp.where(mask_cur, s_c, NEG_INF)
            s_p = jnp.where(mask_prev, s_p, NEG_INF)
            m_blk = jnp.maximum(jnp.max(s_c, axis=-1, keepdims=True), jnp.max(s_p, axis=-1, keepdims=True))
            if first:
                m_new = m_blk
            else:
                m_old = m_ref[:, sl][:, :1]
                m_new = jnp.maximum(m_old, m_blk)
            p_c = jnp.exp(s_c - m_new)
            p_p = jnp.exp(s_p - m_new)
            l_new = jnp.sum(p_c, axis=-1, keepdims=True) + jnp.sum(p_p, axis=-1, keepdims=True)
            acc = (_dot(p_c.astype(MXU_DTYPE), vc_ref[:, sl], NN)
                   + _dot(p_p.astype(MXU_DTYPE), vp_ref[:, sl], NN))
            if not first:
                alpha = jnp.exp(m_old - m_new)
                l_new = l_new + alpha * l_ref[:, sl][:, :1]
                acc = acc + alpha * acc_ref[:, sl]
            if last:
                outs[0][:, sl] = acc / l_new
                outs[1][:, sl] = jnp.broadcast_to(m_new + jnp.log(l_new), (QBLK, HEAD_DIM))
            else:
                outs[0][:, sl] = acc
                outs[1][:, sl] = jnp.broadcast_to(m_new, (QBLK, HEAD_DIM))
                outs[2][:, sl] = jnp.broadcast_to(l_new, (QBLK, HEAD_DIM))

    st_spec = pl.BlockSpec((QBLK, ATTN_W), lambda r, c: (c, r))
    in_specs = _attn_specs(dil) + [pl.BlockSpec((1, 2, N_HEADS, QBLK, QBLK), lambda r, c: (branch, 0, 0, 0, 0))]
    qv = qkv.reshape(n, dil * 3 * ATTN_W)
    args = [qv] * 5 + [bias]
    if not first:
        in_specs += [st_spec] * 3
        args += [t.reshape(n, dil * ATTN_W) for t in state]
    n_out = 2 if last else 3
    outs = pl.pallas_call(
        body, name=f"attn_fwd_b{branch}", grid=(dil, nblk), in_specs=in_specs,
        out_specs=[st_spec] * n_out,
        out_shape=[jax.ShapeDtypeStruct((n, dil * ATTN_W), F32)] * n_out,
        compiler_params=_params(("parallel", "parallel")),
    )(*args)
    return tuple(t.reshape(s, ATTN_W) for t in outs)


def _attn_fwd(qkv, bias):
    state = None
    for b in range(len(BRANCH_DIL)):
        state = _attn_fwd_branch(qkv, bias, state, branch=b, last=(b == len(BRANCH_DIL) - 1))
    return state


def _attn_bwd_branch(qkv, bias, o, lse, do, *, branch):
    dil = BRANCH_DIL[branch]
    s = qkv.shape[0]
    n = s // dil
    nblk = n // QBLK

    def body(q_ref, kp_ref, kc_ref, vp_ref, vc_ref, b_ref, o_ref, l_ref, do_ref,
             dq_ref, dka_ref, dkb_ref, dva_ref, dvb_ref, db_ref):
        @pl.when(jnp.logical_and(pl.program_id(0) == 0, pl.program_id(1) == 0))
        def _():
            db_ref[...] = jnp.zeros_like(db_ref)

        mask_prev, mask_cur = _band_masks(pl.program_id(1))
        for h in range(N_HEADS):
            sl = slice(HEAD_DIM * h, HEAD_DIM * (h + 1))
            qh = q_ref[:, sl]
            doh = do_ref[:, sl]
            lh = l_ref[:, sl][:, :1]
            delta = jnp.sum(doh * o_ref[:, sl], axis=-1, keepdims=True)
            do_m = doh.astype(MXU_DTYPE)
            s_c = _dot(qh, kc_ref[:, sl], NT) * ATTN_SCALE + b_ref[0, 1, h]
            s_p = _dot(qh, kp_ref[:, sl], NT) * ATTN_SCALE + b_ref[0, 0, h]
            p_c = jnp.exp(jnp.where(mask_cur, s_c, NEG_INF) - lh)
            p_p = jnp.exp(jnp.where(mask_prev, s_p, NEG_INF) - lh)
            ds_c = p_c * (_dot(do_m, vc_ref[:, sl], NT) - delta)
            ds_p = p_p * (_dot(do_m, vp_ref[:, sl], NT) - delta)
            db_ref[0, 1, h] += ds_c
            db_ref[0, 0, h] += ds_p
            ds_c_m = ds_c.astype(MXU_DTYPE)
            ds_p_m = ds_p.astype(MXU_DTYPE)
            dq = _dot(ds_c_m, kc_ref[:, sl], NN) + _dot(ds_p_m, kp_ref[:, sl], NN)
            dq_ref[:, sl] = (dq * ATTN_SCALE).astype(dq_ref.dtype)
            dka_ref[:, sl] = (_dot(ds_c_m, qh, TN) * ATTN_SCALE).astype(dka_ref.dtype)
            dkb_ref[:, sl] = (_dot(ds_p_m, qh, TN) * ATTN_SCALE).astype(dkb_ref.dtype)
            dva_ref[:, sl] = _dot(p_c.astype(MXU_DTYPE), do_m, TN).astype(dva_ref.dtype)
            dvb_ref[:, sl] = _dot(p_p.astype(MXU_DTYPE), do_m, TN).astype(dvb_ref.dtype)

    st_spec = pl.BlockSpec((QBLK, ATTN_W), lambda r, c: (c, r))
    b_in = pl.BlockSpec((1, 2, N_HEADS, QBLK, QBLK), lambda r, c: (branch, 0, 0, 0, 0))
    b_out = pl.BlockSpec((1, 2, N_HEADS, QBLK, QBLK), lambda r, c: (0, 0, 0, 0, 0))
    qv = qkv.reshape(n, dil * 3 * ATTN_W)
    view = lambda t: t.reshape(n, dil * ATTN_W)
    outs = pl.pallas_call(
        body, name=f"attn_bwd_b{branch}", grid=(dil, nblk),
        in_specs=_attn_specs(dil) + [b_in, st_spec, st_spec, st_spec],
        out_specs=[st_spec] * 5 + [b_out],
        out_shape=[jax.ShapeDtypeStruct((n, dil * ATTN_W), MXU_DTYPE)] * 5
        + [jax.ShapeDtypeStruct((1, 2, N_HEADS, QBLK, QBLK), F32)],
        compiler_params=_params(("arbitrary", "arbitrary")),
    )(qv, qv, qv, qv, qv, bias, view(o), view(lse), view(do))
    return tuple(t.reshape(s, ATTN_W) for t in outs[:5]) + (outs[5],)


def _attn_bwd(qkv, bias, o, lse, do):
    s = qkv.shape[0]
    nb = s // QBLK
    parts = [_attn_bwd_branch(qkv, bias, o, lse, do, branch=b) for b in range(len(BRANCH_DIL))]
    dbias = jnp.concatenate([p[5] for p in parts], axis=0)

    def body(*refs):
        o_ref = refs[-1]
        i = pl.program_id(0)
        dq = jnp.zeros((QBLK, ATTN_W), F32)
        dk = jnp.zeros((QBLK, ATTN_W), F32)
        dv = jnp.zeros((QBLK, ATTN_W), F32)
        for b, dil in enumerate(BRANCH_DIL):
            dq_ref, dka_ref, dkb_ref, dva_ref, dvb_ref = refs[5 * b:5 * b + 5]
            inside = i + dil < nb
            dq = dq + dq_ref[...].astype(F32)
            dk = dk + dka_ref[...].astype(F32) + jnp.where(inside, dkb_ref[...].astype(F32), 0.0)
            dv = dv + dva_ref[...].astype(F32) + jnp.where(inside, dvb_ref[...].astype(F32), 0.0)
        o_ref[:, 0:ATTN_W] = dq.astype(o_ref.dtype)
        o_ref[:, ATTN_W:2 * ATTN_W] = dk.astype(o_ref.dtype)
        o_ref[:, 2 * ATTN_W:3 * ATTN_W] = dv.astype(o_ref.dtype)

    in_specs, args = [], []
    for b, dil in enumerate(BRANCH_DIL):
        here = pl.BlockSpec((QBLK, ATTN_W), lambda i: (i, 0))
        ahead = pl.BlockSpec((QBLK, ATTN_W), functools.partial(lambda i, d: (jnp.minimum(i + d, nb - 1), 0), d=dil))
        in_specs += [here, here, ahead, here, ahead]
        args += list(parts[b][:5])
    dqkv = pl.pallas_call(
        body, name="attn_bwd_sum", grid=(nb,), in_specs=in_specs,
        out_specs=pl.BlockSpec((QBLK, 3 * ATTN_W), lambda i: (i, 0)),
        out_shape=jax.ShapeDtypeStruct((s, 3 * ATTN_W), MXU_DTYPE),
        compiler_params=_params(("parallel",)),
    )(*args)
    return dqkv, dbias


ATTN_IO_DTYPE = F32
ABLK = 2048
N_CHUNK = ATTN_W // 128


def _rows(start, dil):
    if dil > 1:
        return pl.ds(start, QBLK, stride=dil)
    return pl.ds(pl.multiple_of(start, QBLK), QBLK)


def _low_head():
    return lax.broadcasted_iota(jnp.int32, (QBLK, 128), 1) < HEAD_DIM


def _head_split(t):
    low = _low_head()
    zero = jnp.zeros_like(t)
    return jnp.where(low, t, zero), jnp.where(low, zero, t)


def _tile_bias(b_ref, branch, first):
    bias = b_ref[branch]
    if first is None:
        return bias
    col = lax.broadcasted_iota(jnp.int32, (BIAS_TILE, BIAS_TILE), 1)
    return jnp.where(jnp.logical_and(first, col >= QBLK), NEG_INF, bias)


def _loop(n, fn):
    if n == 1:
        fn(jnp.int32(0), 0)
    elif n > 1:
        lax.fori_loop(0, n, fn, 0, unroll=2)


def _for_each_tile(tile, c):
    for branch, dil in enumerate(BRANCH_DIL):
        span = QBLK * dil

        def edge(r, carry, branch=branch, span=span):
            tile(branch, r, False, ABLK - span + r, c == 0)
            return carry

        def inner(t, carry, branch=branch, span=span, dil=dil):
            start = (1 + t // dil) * span + t % dil
            tile(branch, start, True, start - span, None)
            return carry

        _loop(dil, edge)
        _loop((ABLK // span - 1) * dil, inner)


def _attn_chunk_specs(nb):
    blk = (None, ABLK, 128)
    prev = lambda c: jnp.maximum(c - 1, 0)
    return [pl.BlockSpec(blk, lambda ch, c: (ch, c, 0)),
            pl.BlockSpec(blk, lambda ch, c: (N_CHUNK + ch, c, 0)),
            pl.BlockSpec(blk, lambda ch, c: (2 * N_CHUNK + ch, c, 0)),
            pl.BlockSpec(blk, lambda ch, c: (N_CHUNK + ch, prev(c), 0)),
            pl.BlockSpec(blk, lambda ch, c: (2 * N_CHUNK + ch, prev(c), 0)),
            pl.BlockSpec((len(BRANCH_DIL), None, BIAS_TILE, BIAS_TILE), lambda ch, c: (0, ch, 0, 0))]


def _in_proj_qkv(xn, w_qkv):
    s, k = xn.shape
    tm = 512
    nch = w_qkv.shape[1] // 128

    def body(x_ref, w_ref, o_ref):
        acc = _dot(x_ref[...].astype(MXU_DTYPE), w_ref[...].astype(MXU_DTYPE), NN)
        for j in range(nch):
            blk = acc[:, 128 * j:128 * (j + 1)]
            if j < N_CHUNK:
                blk = blk * ATTN_SCALE
            o_ref[j] = blk.astype(o_ref.dtype)

    return pl.pallas_call(
        body, name="in_proj_qkv", grid=(s // tm,),
        in_specs=[pl.BlockSpec((tm, k), lambda i: (i, 0)), pl.BlockSpec(w_qkv.shape, lambda i: (0, 0))],
        out_specs=pl.BlockSpec((nch, tm, 128), lambda i: (0, i, 0)),
        out_shape=jax.ShapeDtypeStruct((nch, s, 128), ATTN_IO_DTYPE),
        compiler_params=_params(("parallel",)),
    )(xn, w_qkv)


def _attn2_fwd(qkv_c, bias):
    s = qkv_c.shape[1]
    nb = s // ABLK
    last = len(BRANCH_DIL) - 1

    def body(q_ref, kc_ref, vc_ref, kp_ref, vp_ref, b_ref, o_ref, l_ref, acc_s, m_s, l_s):
        low = _low_head()
        e_st = jnp.concatenate(_head_split(jnp.ones((QBLK, 128), MXU_DTYPE)) * 2, axis=0)

        def tile(branch, start, prev_in_block, pstart, first):
            dil = BRANCH_DIL[branch]
            rq, rp = _rows(start, dil), _rows(pstart, dil)
            k_ref, v_ref = (kc_ref, vc_ref) if prev_in_block else (kp_ref, vp_ref)
            q_st = jnp.concatenate(_head_split(q_ref[rq, :].astype(MXU_DTYPE)), axis=0)
            k_st = jnp.concatenate([kc_ref[rq, :].astype(MXU_DTYPE), k_ref[rp, :].astype(MXU_DTYPE)], axis=0)
            v_st = jnp.concatenate(_head_split(vc_ref[rq, :].astype(MXU_DTYPE))
                                   + _head_split(v_ref[rp, :].astype(MXU_DTYPE)), axis=0)
            sc = _dot(q_st, k_st, NT) + _tile_bias(b_ref, branch, first)
            m_new = jnp.max(sc, axis=-1, keepdims=True)
            if branch > 0:
                m_old2 = m_s[rq, :]
                m_old = jnp.concatenate([m_old2[:, 0:1], m_old2[:, HEAD_DIM:HEAD_DIM + 1]], axis=0)
                m_new = jnp.maximum(m_old, m_new)
                alpha = jnp.exp(m_old - m_new)
            p = jnp.exp(sc - m_new).astype(MXU_DTYPE)
            lhs = jnp.concatenate([p[:QBLK, :QBLK], p[QBLK:, :QBLK], p[:QBLK, QBLK:], p[QBLK:, QBLK:]], axis=1)
            acc2 = _dot(lhs, v_st, NN)
            sum2 = _dot(lhs, e_st, NN)
            m2 = jnp.where(low, m_new[:QBLK], m_new[QBLK:])
            if branch > 0:
                a2 = jnp.where(low, alpha[:QBLK], alpha[QBLK:])
                acc2 = acc2 + a2 * acc_s[rq, :]
                sum2 = sum2 + a2 * l_s[rq, :]
            if branch == last:
                o_ref[rq, :] = acc2 / sum2
                l_ref[rq, :] = m2 + jnp.log(sum2)
            else:
                acc_s[rq, :] = acc2
                m_s[rq, :] = m2
                l_s[rq, :] = sum2

        _for_each_tile(tile, pl.program_id(1))

    out_spec = pl.BlockSpec((None, ABLK, 128), lambda ch, c: (ch, c, 0))
    return pl.pallas_call(
        body, name="attn_fwd", grid=(N_CHUNK, nb), in_specs=_attn_chunk_specs(nb),
        out_specs=[out_spec, out_spec],
        out_shape=[jax.ShapeDtypeStruct((N_CHUNK, s, 128), F32)] * 2,
        scratch_shapes=[pltpu.VMEM((ABLK, 128), F32)] * 3,
        compiler_params=_params(("parallel", "arbitrary")),
    )(qkv_c, qkv_c, qkv_c, qkv_c, qkv_c, bias)


def _attn2_bwd(qkv_c, bias, lse_c, delta_c, do_c):
    s = qkv_c.shape[1]
    nb = s // ABLK
    nbr = len(BRANCH_DIL)

    def body(q_ref, kc_ref, vc_ref, kp_ref, vp_ref, b_ref, l_ref, dl_ref, do_ref,
             dq_ref, dk_ref, dv_ref, *rest):
        ek_refs, ev_refs, db_ref = rest[:nbr], rest[nbr:2 * nbr], rest[2 * nbr]
        c = pl.program_id(1)

        @pl.when(c == 0)
        def _():
            db_ref[...] = jnp.zeros_like(db_ref)

        for r in (dq_ref, dk_ref, dv_ref) + tuple(ek_refs) + tuple(ev_refs):
            r[...] = jnp.zeros_like(r)

        def tile(branch, start, prev_in_block, pstart, first):
            dil = BRANCH_DIL[branch]
            rq, rp = _rows(start, dil), _rows(pstart, dil)
            k_ref, v_ref = (kc_ref, vc_ref) if prev_in_block else (kp_ref, vp_ref)
            kc2 = kc_ref[rq, :].astype(MXU_DTYPE)
            kp2 = k_ref[rp, :].astype(MXU_DTYPE)
            q_st = jnp.concatenate(_head_split(q_ref[rq, :].astype(MXU_DTYPE)), axis=0)
            do_st = jnp.concatenate(_head_split(do_ref[rq, :].astype(MXU_DTYPE)), axis=0)
            k_st = jnp.concatenate([kc2, kp2], axis=0)
            v_st = jnp.concatenate([vc_ref[rq, :].astype(MXU_DTYPE), v_ref[rp, :].astype(MXU_DTYPE)], axis=0)
            kh_st = jnp.concatenate(_head_split(kc2) + _head_split(kp2), axis=0)
            lse2 = l_ref[rq, :]
            del2 = dl_ref[rq, :]
            lse_st = jnp.concatenate([lse2[:, 0:1], lse2[:, HEAD_DIM:HEAD_DIM + 1]], axis=0)
            del_st = jnp.concatenate([del2[:, 0:1], del2[:, HEAD_DIM:HEAD_DIM + 1]], axis=0)
            p = jnp.exp(_dot(q_st, k_st, NT) + _tile_bias(b_ref, branch, first) - lse_st)
            ds = p * (_dot(do_st, v_st, NT) - del_st)
            db_ref[branch] += ds
            ds = ds.astype(MXU_DTYPE)
            p = p.astype(MXU_DTYPE)
            lhs = jnp.concatenate([ds[:QBLK, :QBLK], ds[QBLK:, :QBLK], ds[:QBLK, QBLK:], ds[QBLK:, QBLK:]], axis=1)
            dk_st = _dot(ds, q_st, TN)
            dv_st = _dot(p, do_st, TN)
            dq_ref[rq, :] += _dot(lhs, kh_st, NN)
            dk_ref[rq, :] += dk_st[:QBLK]
            dv_ref[rq, :] += dv_st[:QBLK]
            if prev_in_block:
                dk_ref[rp, :] += dk_st[QBLK:]
                dv_ref[rp, :] += dv_st[QBLK:]
            else:
                ek_refs[branch][rq, :] = dk_st[QBLK:]
                ev_refs[branch][rq, :] = dv_st[QBLK:]

        _for_each_tile(tile, c)

    blk = pl.BlockSpec((None, ABLK, 128), lambda ch, c: (ch, c, 0))
    outs = pl.pallas_call(
        body, name="attn_bwd", grid=(N_CHUNK, nb), in_specs=_attn_chunk_specs(nb) + [blk, blk, blk],
        out_specs=[blk] * (3 + 2 * nbr) + [pl.BlockSpec((nbr, None, BIAS_TILE, BIAS_TILE), lambda ch, c: (0, ch, 0, 0))],
        out_shape=[jax.ShapeDtypeStruct((N_CHUNK, s, 128), F32)] * (3 + 2 * nbr)
        + [jax.ShapeDtypeStruct((nbr, N_HEADS // 2, BIAS_TILE, BIAS_TILE), F32)],
        compiler_params=_params(("arbitrary", "arbitrary")),
    )(qkv_c, qkv_c, qkv_c, qkv_c, qkv_c, bias, lse_c, delta_c, do_c)
    return outs[0], outs[1], outs[2], outs[3:3 + nbr], outs[3 + nbr:3 + 2 * nbr], outs[3 + 2 * nbr]


def _attn2_bwd_sum(dq, dk, dv, ek, ev, dzs, dus):
    s = dq.shape[1]
    nrb = s // QBLK
    per_blk = ABLK // QBLK
    nbr = len(BRANCH_DIL)

    def body(*refs):
        dq_ref, dk_ref, dv_ref = refs[:3]
        ek_refs, ev_refs = refs[3:3 + nbr], refs[3 + nbr:3 + 2 * nbr]
        dzs_ref, dus_ref, o_ref = refs[3 + 2 * nbr:]
        i = pl.program_id(0)
        dkt, dvt = dk_ref[...], dv_ref[...]
        for b, dil in enumerate(BRANCH_DIL):
            j = i + dil
            ok = jnp.logical_and(j < nrb, j % per_blk < dil)
            dkt = dkt + jnp.where(ok, ek_refs[b][...], 0.0)
            dvt = dvt + jnp.where(ok, ev_refs[b][...], 0.0)
        for ch in range(N_CHUNK):
            o_ref[:, 128 * ch:128 * (ch + 1)] = (dq_ref[ch] * ATTN_SCALE).astype(o_ref.dtype)
            o_ref[:, ATTN_W + 128 * ch:ATTN_W + 128 * (ch + 1)] = dkt[ch].astype(o_ref.dtype)
            o_ref[:, 2 * ATTN_W + 128 * ch:2 * ATTN_W + 128 * (ch + 1)] = dvt[ch].astype(o_ref.dtype)
        o_ref[:, O_SGU:O_SSM] = dzs_ref[...].astype(o_ref.dtype)
        o_ref[:, O_SSM:] = dus_ref[...].astype(o_ref.dtype)

    here = pl.BlockSpec((N_CHUNK, QBLK, 128), lambda i: (0, i, 0))
    edge_specs = [pl.BlockSpec((N_CHUNK, QBLK, 128),
                               functools.partial(lambda i, d: (0, jnp.minimum(i + d, nrb - 1), 0), d=dil))
                  for dil in BRANCH_DIL]
    return pl.pallas_call(
        body, name="attn_bwd_sum", grid=(nrb,),
        in_specs=[here, here, here] + edge_specs + edge_specs
        + [pl.BlockSpec((QBLK, 2 * SGU_W), lambda i: (i, 0)), pl.BlockSpec((QBLK, SSM_W), lambda i: (i, 0))],
        out_specs=pl.BlockSpec((QBLK, O_SSM + SSM_W), lambda i: (i, 0)),
        out_shape=jax.ShapeDtypeStruct((s, O_SSM + SSM_W), MXU_DTYPE),
        compiler_params=_params(("parallel",)),
    )(dq, dk, dv, *ek, *ev, dzs, dus)


SGU_ROWS = 512


def _sgu_norm(v_g):
    mu = jnp.mean(v_g, axis=-1, keepdims=True)
    cen = v_g - mu
    var = jnp.mean(cen * cen, axis=-1, keepdims=True)
    rstd = lax.rsqrt(var + EPS)
    return cen * rstd, rstd


def _sgu_fwd(zs, ln_g, ln_b, w_mask, b_t):
    s = zs.shape[0]
    nch = SGU_ROWS // SGU_CHUNK

    def body(z_ref, g_ref, b_ref, w_ref, bt_ref, o_ref):
        gz = _gelu(z_ref[...])
        for g in range(SGU_G):
            sl = slice(SGU_GW * g, SGU_GW * (g + 1))
            u_g = gz[:, sl]
            xhat, _ = _sgu_norm(gz[:, SGU_W + SGU_GW * g:SGU_W + SGU_GW * (g + 1)])
            vn = (xhat * g_ref[:, sl] + b_ref[:, sl]).astype(MXU_DTYPE)
            wg = w_ref[g].astype(MXU_DTYPE)
            for ci in range(nch):
                rs = slice(SGU_CHUNK * ci, SGU_CHUNK * (ci + 1))
                mixed = _dot(wg, vn[rs], NN) + bt_ref[:, g:g + 1]
                o_ref[rs, sl] = u_g[rs] * mixed

    full = lambda shape: pl.BlockSpec(shape, lambda i: tuple(0 for _ in shape))
    return pl.pallas_call(
        body, name="sgu_fwd", grid=(s // SGU_ROWS,),
        in_specs=[pl.BlockSpec((SGU_ROWS, 2 * SGU_W), lambda i: (i, 0)), full((1, SGU_W)), full((1, SGU_W)),
                  full((SGU_G, SGU_CHUNK, SGU_CHUNK)), full((SGU_CHUNK, SGU_G))],
        out_specs=pl.BlockSpec((SGU_ROWS, SGU_W), lambda i: (i, 0)),
        out_shape=jax.ShapeDtypeStruct((s, SGU_W), F32),
        compiler_params=_params(("parallel",)),
    )(zs, ln_g.reshape(1, SGU_W), ln_b.reshape(1, SGU_W), w_mask, b_t)


def _sgu_bwd(zs, ln_g, ln_b, w_mask, b_t, dy):
    s = zs.shape[0]
    nch = SGU_ROWS // SGU_CHUNK

    def body(z_ref, g_ref, b_ref, w_ref, bt_ref, dy_ref, dz_ref, dg_ref, dbb_ref, dw_ref, dbt_ref):
        @pl.when(pl.program_id(0) == 0)
        def _():
            dg_ref[...] = jnp.zeros_like(dg_ref)
            dbb_ref[...] = jnp.zeros_like(dbb_ref)
            dw_ref[...] = jnp.zeros_like(dw_ref)
            dbt_ref[...] = jnp.zeros_like(dbt_ref)

        z = z_ref[...]
        gz, dgelu = _gelu_pair(z)
        dy = dy_ref[...]
        for g in range(SGU_G):
            sl = slice(SGU_GW * g, SGU_GW * (g + 1))
            sv = slice(SGU_W + SGU_GW * g, SGU_W + SGU_GW * (g + 1))
            u_g = gz[:, sl]
            xhat, rstd = _sgu_norm(gz[:, sv])
            gain = g_ref[:, sl]
            vn = (xhat * gain + b_ref[:, sl]).astype(MXU_DTYPE)
            wg = w_ref[g].astype(MXU_DTYPE)
            dy_g = dy[:, sl]
            dvn_parts = []
            for ci in range(nch):
                rs = slice(SGU_CHUNK * ci, SGU_CHUNK * (ci + 1))
                mixed = _dot(wg, vn[rs], NN) + bt_ref[:, g:g + 1]
                dz_ref[rs, sl] = (dy_g[rs] * mixed * dgelu[rs, sl]).astype(dz_ref.dtype)
                dmixed = dy_g[rs] * u_g[rs]
                dm = dmixed.astype(MXU_DTYPE)
                dvn_parts.append(_dot(wg, dm, TN))
                dw_ref[g] += _dot(dm, vn[rs], NT)
                dbt_ref[:, g:g + 1] += jnp.sum(dmixed, axis=-1, keepdims=True)
            dvn = jnp.concatenate(dvn_parts, axis=0)
            dg_ref[:, sl] += jnp.sum(dvn * xhat, axis=0, keepdims=True)
            dbb_ref[:, sl] += jnp.sum(dvn, axis=0, keepdims=True)
            dxh = dvn * gain
            dv = rstd * (dxh - jnp.mean(dxh, axis=-1, keepdims=True)
                         - xhat * jnp.mean(dxh * xhat, axis=-1, keepdims=True))
            dz_ref[:, sv] = (dv * dgelu[:, sv]).astype(dz_ref.dtype)

    full = lambda shape: pl.BlockSpec(shape, lambda i: tuple(0 for _ in shape))
    return pl.pallas_call(
        body, name="sgu_bwd", grid=(s // SGU_ROWS,),
        in_specs=[pl.BlockSpec((SGU_ROWS, 2 * SGU_W), lambda i: (i, 0)), full((1, SGU_W)), full((1, SGU_W)),
                  full((SGU_G, SGU_CHUNK, SGU_CHUNK)), full((SGU_CHUNK, SGU_G)),
                  pl.BlockSpec((SGU_ROWS, SGU_W), lambda i: (i, 0))],
        out_specs=[pl.BlockSpec((SGU_ROWS, 2 * SGU_W), lambda i: (i, 0)), full((1, SGU_W)), full((1, SGU_W)),
                   full((SGU_G, SGU_CHUNK, SGU_CHUNK)), full((SGU_CHUNK, SGU_G))],
        out_shape=[jax.ShapeDtypeStruct((s, 2 * SGU_W), MXU_DTYPE), jax.ShapeDtypeStruct((1, SGU_W), F32),
                   jax.ShapeDtypeStruct((1, SGU_W), F32), jax.ShapeDtypeStruct((SGU_G, SGU_CHUNK, SGU_CHUNK), F32),
                   jax.ShapeDtypeStruct((SGU_CHUNK, SGU_G), F32)],
        compiler_params=_params(("arbitrary",)),
    )(zs, ln_g.reshape(1, SGU_W), ln_b.reshape(1, SGU_W), w_mask, b_t, dy)


def _ssm_discretize(a_re, a_im, log_dt, b_re, b_im):
    dt = jnp.exp(log_dt)[:, None]
    mag = jnp.exp(a_re * dt)
    ab_re = mag * jnp.cos(a_im * dt)
    ab_im = mag * jnp.sin(a_im * dt)
    den = a_re * a_re + a_im * a_im
    f_re = ((ab_re - 1.0) * a_re + ab_im * a_im) / den
    f_im = (ab_im * a_re - (ab_re - 1.0) * a_im) / den
    bb_re = f_re[:, :, None] * b_re - f_im[:, :, None] * b_im
    bb_im = f_re[:, :, None] * b_im + f_im[:, :, None] * b_re
    return ab_re, ab_im, bb_re, bb_im


def _ssm_operands(a_re, a_im, log_dt, b_re, b_im, c_re, c_im):
    ab_re, ab_im, bb_re, bb_im = _ssm_discretize(a_re, a_im, log_dt, b_re, b_im)
    eye = jnp.eye(SSM_G, dtype=F32)
    b_blk = jnp.einsum("pgnc,gh->gcphn", jnp.stack([bb_re, bb_im]), eye).reshape(SSM_W, 2 * NSTATE)
    c_mat = jnp.einsum("pgcn,gh->pgnhc", jnp.stack([c_re, -c_im]), eye).reshape(2 * NSTATE, SSM_W)
    a_row = jnp.stack([ab_re.reshape(NSTATE), ab_im.reshape(NSTATE)])
    p_re, p_im = a_row[0:1], a_row[1:2]
    while p_re.shape[0] < SSM_TSEG:
        l_re, l_im = p_re[-1:], p_im[-1:]
        p_re, p_im = (jnp.concatenate([p_re, p_re * l_re - p_im * l_im]),
                      jnp.concatenate([p_im, p_re * l_im + p_im * l_re]))
    p_tab = jnp.stack([p_re, p_im])
    return b_blk.astype(MXU_DTYPE), c_mat.astype(MXU_DTYPE), a_row, p_tab


def _lane_chunks():
    return [(lo, lo + SSM_LANE_CHUNK) for lo in range(0, NSTATE, SSM_LANE_CHUNK)]


def _seg_rows(j):
    return pl.ds(pl.multiple_of(j * SSM_NSEG, SSM_NSEG), SSM_NSEG)


def _to_segments(t):
    s, w = t.shape
    return t.reshape(s // SSM_TB, SSM_NSEG, SSM_TSEG, w).transpose(0, 2, 1, 3).reshape(s, w)


def _from_segments(t):
    s, w = t.shape
    return t.reshape(s // SSM_TB, SSM_TSEG, SSM_NSEG, w).transpose(0, 2, 1, 3).reshape(s, w)


def _ssm_local_scan(buf, a_ref, *, reverse):
    ends_re, ends_im = [], []
    for lo, hi in _lane_chunks():
        are = jnp.broadcast_to(a_ref[0:1, lo:hi], (SSM_NSEG, hi - lo))
        aim = jnp.broadcast_to(a_ref[1:2, lo:hi], (SSM_NSEG, hi - lo))
        if reverse:
            aim = -aim

        def step(jj, carry, lo=lo, hi=hi, are=are, aim=aim):
            xr, xi = carry
            j = (SSM_TSEG - 1 - jj) if reverse else jj
            tr = buf[_seg_rows(j), lo:hi]
            ti = buf[_seg_rows(j), NSTATE + lo:NSTATE + hi]
            nr = are * xr - aim * xi + tr
            ni = are * xi + aim * xr + ti
            buf[_seg_rows(j), lo:hi] = nr
            buf[_seg_rows(j), NSTATE + lo:NSTATE + hi] = ni
            return nr, ni

        zero = jnp.zeros((SSM_NSEG, hi - lo), F32)
        xr, xi = lax.fori_loop(0, SSM_TSEG, step, (zero, zero), unroll=4)
        ends_re.append(xr)
        ends_im.append(xi)
    return jnp.concatenate(ends_re, axis=1), jnp.concatenate(ends_im, axis=1)


def _ssm_entry_states(ends_re, ends_im, carry_ref, p_ref, entry_ref, *, reverse):
    at_re = p_ref[0, SSM_TSEG - 1:SSM_TSEG, :]
    at_im = p_ref[1, SSM_TSEG - 1:SSM_TSEG, :]
    if reverse:
        at_im = -at_im
    cur_re = carry_ref[0:1, 0:NSTATE]
    cur_im = carry_ref[0:1, NSTATE:2 * NSTATE]
    order = range(SSM_NSEG - 1, -1, -1) if reverse else range(SSM_NSEG)
    for i in order:
        entry_ref[0, i:i + 1, 0:NSTATE] = cur_re
        entry_ref[0, i:i + 1, NSTATE:2 * NSTATE] = cur_im
        nxt_re = ends_re[i:i + 1] + at_re * cur_re - at_im * cur_im
        nxt_im = ends_im[i:i + 1] + at_re * cur_im + at_im * cur_re
        cur_re, cur_im = nxt_re, nxt_im
    carry_ref[0:1, 0:NSTATE] = cur_re
    carry_ref[0:1, NSTATE:2 * NSTATE] = cur_im


def _ssm_fixup(buf, p_ref, entry_ref, *, reverse):
    for lo, hi in _lane_chunks():
        e_re = entry_ref[0, :, lo:hi]
        e_im = entry_ref[0, :, NSTATE + lo:NSTATE + hi]

        def step(j, carry, lo=lo, hi=hi, e_re=e_re, e_im=e_im):
            jp = (SSM_TSEG - 1 - j) if reverse else j
            pr = p_ref[0, pl.ds(jp, 1), lo:hi]
            pi = p_ref[1, pl.ds(jp, 1), lo:hi]
            if reverse:
                pi = -pi
            buf[_seg_rows(j), lo:hi] = buf[_seg_rows(j), lo:hi] + pr * e_re - pi * e_im
            buf[_seg_rows(j), NSTATE + lo:NSTATE + hi] = (buf[_seg_rows(j), NSTATE + lo:NSTATE + hi]
                                                           + pr * e_im + pi * e_re)
            return carry

        lax.fori_loop(0, SSM_TSEG, step, 0, unroll=4)


def _ssm_fwd(u, ops, d_skip, glu_w, glu_b):
    b_blk, c_mat, a_row, p_tab = ops
    s = u.shape[0]
    nblk = s // SSM_TB

    def body(u_ref, bb_ref, cm_ref, a_ref, p_ref, d_ref, gw_ref, gb_ref, y_ref, entry_ref, xbuf, carry):
        @pl.when(pl.program_id(0) == 0)
        def _():
            carry[...] = jnp.zeros_like(carry)

        uu = u_ref[...]
        xbuf[...] = _dotf(uu, bb_ref[...], NN)
        ends_re, ends_im = _ssm_local_scan(xbuf, a_ref, reverse=False)
        _ssm_entry_states(ends_re, ends_im, carry, p_ref, entry_ref, reverse=False)
        _ssm_fixup(xbuf, p_ref, entry_ref, reverse=False)
        y = _dotf(xbuf[...],cm_ref[...], NN) + d_ref[...] * uu
        y2 = _gelu(y)
        gate = jax.nn.sigmoid(_dot(y2.astype(MXU_DTYPE), gw_ref[...].astype(MXU_DTYPE), NN) + gb_ref[...])
        y_ref[...] = y2 * gate

    full = lambda shape: pl.BlockSpec(shape, lambda i: tuple(0 for _ in shape))
    y_seg, entry = pl.pallas_call(
        body, name="ssm_fwd", grid=(nblk,),
        in_specs=[pl.BlockSpec((SSM_TB, SSM_W), lambda i: (i, 0)), full(b_blk.shape), full(c_mat.shape),
                  full(a_row.shape), full(p_tab.shape), full((1, SSM_W)), full((SSM_W, SSM_W)), full((1, SSM_W))],
        out_specs=[pl.BlockSpec((SSM_TB, SSM_W), lambda i: (i, 0)),
                   pl.BlockSpec((1, SSM_NSEG, 2 * NSTATE), lambda i: (i, 0, 0))],
        out_shape=[jax.ShapeDtypeStruct((s, SSM_W), F32), jax.ShapeDtypeStruct((nblk, SSM_NSEG, 2 * NSTATE), F32)],
        scratch_shapes=[pltpu.VMEM((SSM_TB, 2 * NSTATE), F32), pltpu.VMEM((SSM_NSEG, 2 * NSTATE), F32)],
        compiler_params=_params(("arbitrary",)),
    )(_to_segments(u), b_blk, c_mat, a_row, p_tab, d_skip.reshape(1, SSM_W), glu_w, glu_b.reshape(1, SSM_W))
    return _from_segments(y_seg), entry


def _ssm_bwd(u, entry, ops, d_skip, glu_w, glu_b, dout):
    b_blk, c_mat, a_row, p_tab = ops
    s = u.shape[0]
    nblk = s // SSM_TB

    def body(u_ref, en_ref, bb_ref, cm_ref, a_ref, p_ref, d_ref, gw_ref, gb_ref, do_ref,
             du_ref, dbb_ref, dcm_ref, da_ref, dd_ref, dgw_ref, dgb_ref, xbuf, gbuf, gcarry, gentry):
        @pl.when(pl.program_id(0) == 0)
        def _():
            gcarry[...] = jnp.zeros_like(gcarry)
            for r in (dbb_ref, dcm_ref, da_ref, dd_ref, dgw_ref, dgb_ref):
                r[...] = jnp.zeros_like(r)

        uu = u_ref[...]
        xbuf[...] = _dotf(uu, bb_ref[...], NN)
        _ssm_local_scan(xbuf, a_ref, reverse=False)
        _ssm_fixup(xbuf, p_ref, en_ref, reverse=False)
        y = _dotf(xbuf[...],cm_ref[...], NN) + d_ref[...] * uu
        y2, dgelu = _gelu_pair(y)
        y2m = y2.astype(MXU_DTYPE)
        gwm = gw_ref[...].astype(MXU_DTYPE)
        gate = jax.nn.sigmoid(_dot(y2m, gwm, NN) + gb_ref[...])
        dout = do_ref[...]
        dpre = dout * y2 * gate * (1.0 - gate)
        dprem = dpre.astype(MXU_DTYPE)
        dy2 = dout * gate + _dot(dprem, gwm, NT)
        dgw_ref[...] += _dot(y2m, dprem, TN)
        dgb_ref[...] += jnp.sum(dpre, axis=0, keepdims=True)
        dy = dy2 * dgelu
        dd_ref[...] += jnp.sum(dy * uu, axis=0, keepdims=True)
        dcm_ref[...] += _dotf(xbuf[...],dy, TN)
        gbuf[...] = _dotf(dy, cm_ref[...], NT)
        gs_re, gs_im = _ssm_local_scan(gbuf, a_ref, reverse=True)
        _ssm_entry_states(gs_re, gs_im, gcarry, p_ref, gentry, reverse=True)
        _ssm_fixup(gbuf, p_ref, gentry, reverse=True)
        du_ref[...] = (_dotf(gbuf[...], bb_ref[...], NT) + d_ref[...] * dy).astype(du_ref.dtype)
        dbb_ref[...] += _dotf(uu, gbuf[...], TN)
        for lo, hi in _lane_chunks():
            def step(j, carry, lo=lo, hi=hi):
                acc_re, acc_im = carry
                g_re = gbuf[_seg_rows(j), lo:hi]
                g_im = gbuf[_seg_rows(j), NSTATE + lo:NSTATE + hi]
                x_re = xbuf[_seg_rows(j - 1), lo:hi]
                x_im = xbuf[_seg_rows(j - 1), NSTATE + lo:NSTATE + hi]
                return acc_re + g_re * x_re + g_im * x_im, acc_im + g_im * x_re - g_re * x_im

            g0_re = gbuf[_seg_rows(0), lo:hi]
            g0_im = gbuf[_seg_rows(0), NSTATE + lo:NSTATE + hi]
            e_re = en_ref[0, :, lo:hi]
            e_im = en_ref[0, :, NSTATE + lo:NSTATE + hi]
            init = (g0_re * e_re + g0_im * e_im, g0_im * e_re - g0_re * e_im)
            acc_re, acc_im = lax.fori_loop(1, SSM_TSEG, step, init, unroll=4)
            da_ref[0:1, lo:hi] += jnp.sum(acc_re, axis=0, keepdims=True)
            da_ref[1:2, lo:hi] += jnp.sum(acc_im, axis=0, keepdims=True)

    full = lambda shape: pl.BlockSpec(shape, lambda i: tuple(0 for _ in shape))
    rev = pl.BlockSpec((SSM_TB, SSM_W), lambda i: (nblk - 1 - i, 0))
    outs = pl.pallas_call(
        body, name="ssm_bwd", grid=(nblk,),
        in_specs=[rev, pl.BlockSpec((1, SSM_NSEG, 2 * NSTATE), lambda i: (nblk - 1 - i, 0, 0)),
                  full(b_blk.shape), full(c_mat.shape), full(a_row.shape), full(p_tab.shape),
                  full((1, SSM_W)), full((SSM_W, SSM_W)), full((1, SSM_W)), rev],
        out_specs=[rev, full(b_blk.shape), full(c_mat.shape), full(a_row.shape), full((1, SSM_W)),
                   full((SSM_W, SSM_W)), full((1, SSM_W))],
        out_shape=[jax.ShapeDtypeStruct((s, SSM_W), MXU_DTYPE), jax.ShapeDtypeStruct(b_blk.shape, F32),
                   jax.ShapeDtypeStruct(c_mat.shape, F32), jax.ShapeDtypeStruct(a_row.shape, F32),
                   jax.ShapeDtypeStruct((1, SSM_W), F32), jax.ShapeDtypeStruct((SSM_W, SSM_W), F32),
                   jax.ShapeDtypeStruct((1, SSM_W), F32)],
        scratch_shapes=[pltpu.VMEM((SSM_TB, 2 * NSTATE), F32), pltpu.VMEM((SSM_TB, 2 * NSTATE), F32),
                        pltpu.VMEM((SSM_NSEG, 2 * NSTATE), F32), pltpu.VMEM((1, SSM_NSEG, 2 * NSTATE), F32)],
        compiler_params=_params(("arbitrary",)),
    )(_to_segments(u), entry, b_blk, c_mat, a_row, p_tab, d_skip.reshape(1, SSM_W), glu_w, glu_b.reshape(1, SSM_W),
      _to_segments(dout))
    return (_from_segments(outs[0]),) + tuple(outs[1:])


MIX_SEGS = ((0, ATTN_W), (ATTN_W, ATTN_W + SGU_W), (ATTN_W + SGU_W, D_MODEL))


def _chunks_to_rows(a_ref):
    return jnp.concatenate([a_ref[ch] for ch in range(N_CHUNK)], axis=1)


def _mix_fwd(y_attn_c, y_sgu, y_ssm, gain):
    s = y_sgu.shape[0]

    def body(a_ref, b_ref, c_ref, g_ref, o_ref):
        for x, (lo, hi) in zip((_chunks_to_rows(a_ref), b_ref[...], c_ref[...]), MIX_SEGS):
            r = lax.rsqrt(jnp.mean(x * x, axis=-1, keepdims=True) + EPS)
            o_ref[:, lo:hi] = (x * r * g_ref[:, lo:hi]).astype(o_ref.dtype)

    row = lambda w: pl.BlockSpec((ROWS, w), lambda i: (i, 0))
    return pl.pallas_call(
        body, name="mix_fwd", grid=(s // ROWS,),
        in_specs=[pl.BlockSpec((N_CHUNK, ROWS, 128), lambda i: (0, i, 0)), row(SGU_W), row(SSM_W),
                  pl.BlockSpec((1, D_MODEL), lambda i: (0, 0))],
        out_specs=row(D_MODEL), out_shape=jax.ShapeDtypeStruct((s, D_MODEL), MXU_DTYPE),
        compiler_params=_params(("parallel",)),
    )(y_attn_c, y_sgu, y_ssm, gain.reshape(1, D_MODEL))


def _mix_bwd(y_attn_c, y_sgu, y_ssm, gain, dmix):
    s = y_sgu.shape[0]

    def body(a_ref, b_ref, c_ref, g_ref, dm_ref, da_ref, dl_ref, db_ref, dc_ref, dg_ref):
        @pl.when(pl.program_id(0) == 0)
        def _():
            dg_ref[...] = jnp.zeros_like(dg_ref)

        grads = []
        for x, (lo, hi) in zip((_chunks_to_rows(a_ref), b_ref[...], c_ref[...]), MIX_SEGS):
            r = lax.rsqrt(jnp.mean(x * x, axis=-1, keepdims=True) + EPS)
            xhat = x * r
            dm = dm_ref[:, lo:hi].astype(F32)
            dg_ref[:, lo:hi] += jnp.sum(dm * xhat, axis=0, keepdims=True)
            dxh = dm * g_ref[:, lo:hi]
            grads.append(r * (dxh - xhat * jnp.mean(dxh * xhat, axis=-1, keepdims=True)))
        db_ref[...] = grads[1]
        dc_ref[...] = grads[2]
        low = lax.broadcasted_iota(jnp.int32, (ROWS, 128), 1) < HEAD_DIM
        for ch in range(N_CHUNK):
            d_c = grads[0][:, 128 * ch:128 * (ch + 1)]
            da_ref[ch] = d_c.astype(da_ref.dtype)
            prod = d_c * a_ref[ch]
            dl_ref[ch] = jnp.where(low, jnp.sum(prod[:, :HEAD_DIM], axis=-1, keepdims=True),
                                   jnp.sum(prod[:, HEAD_DIM:], axis=-1, keepdims=True))

    row = lambda w: pl.BlockSpec((ROWS, w), lambda i: (i, 0))
    vec = pl.BlockSpec((1, D_MODEL), lambda i: (0, 0))
    chunked = pl.BlockSpec((N_CHUNK, ROWS, 128), lambda i: (0, i, 0))
    return pl.pallas_call(
        body, name="mix_bwd", grid=(s // ROWS,),
        in_specs=[chunked, row(SGU_W), row(SSM_W), vec, row(D_MODEL)],
        out_specs=[chunked, chunked, row(SGU_W), row(SSM_W), vec],
        out_shape=[jax.ShapeDtypeStruct((N_CHUNK, s, 128), ATTN_IO_DTYPE), jax.ShapeDtypeStruct((N_CHUNK, s, 128), F32),
                   jax.ShapeDtypeStruct((s, SGU_W), F32), jax.ShapeDtypeStruct((s, SSM_W), F32),
                   jax.ShapeDtypeStruct((1, D_MODEL), F32)],
        compiler_params=_params(("arbitrary",)),
    )(y_attn_c, y_sgu, y_ssm, gain.reshape(1, D_MODEL), dmix)


CONV_ROWS = 256
CONV_COLS = 1408
CONV_PAIR = 2 * CONV_COLS
HALO = 8


def _interleave_ff(t):
    lead = t.shape[:-1]
    nb = D_FF // CONV_COLS
    return jnp.swapaxes(t.reshape(lead + (2, nb, CONV_COLS)), -3, -2).reshape(lead + (2 * D_FF,))


def _deinterleave_ff(t):
    lead = t.shape[:-1]
    nb = D_FF // CONV_COLS
    return jnp.swapaxes(t.reshape(lead + (nb, 2, CONV_COLS)), -3, -2).reshape(lead + (2 * D_FF,))


def _causal_taps(main, halo, first):
    row = lax.broadcasted_iota(jnp.int32, (HALO, main.shape[1]), 0)
    h7 = jnp.where(first, 0.0, halo[HALO - 1:HALO, :])
    h6 = jnp.where(first, 0.0, halo[HALO - 2:HALO - 1, :])
    r1 = pltpu.roll(main, 1, 0)
    r2 = pltpu.roll(main, 2, 0)
    top1 = jnp.where(row == 0, h7, r1[0:HALO])
    top2 = jnp.where(row == 0, h6, jnp.where(row == 1, h7, r2[0:HALO]))
    return jnp.concatenate([top1, r1[HALO:]], axis=0), jnp.concatenate([top2, r2[HALO:]], axis=0)


def _conv_in_specs():
    halo_idx = lambda i: jnp.maximum(i * (CONV_ROWS // HALO) - 1, 0)
    return [pl.BlockSpec((CONV_ROWS, CONV_PAIR), lambda j, i: (i, j)),
            pl.BlockSpec((HALO, CONV_PAIR), lambda j, i: (halo_idx(i), j)),
            pl.BlockSpec((3, CONV_PAIR), lambda j, i: (0, j)),
            pl.BlockSpec((1, CONV_PAIR), lambda j, i: (0, j))]


def _ffn_act_fwd(hh, conv_w, conv_b):
    s = hh.shape[0]

    def body(m_ref, h_ref, w_ref, b_ref, o_ref):
        first = pl.program_id(1) == 0
        main = m_ref[...]
        x1, x2 = _causal_taps(main, h_ref[...], first)
        conv = w_ref[0:1, :] * x2 + w_ref[1:2, :] * x1 + w_ref[2:3, :] * main + b_ref[...]
        o_ref[...] = (_gelu(conv[:, CONV_COLS:]) * conv[:, :CONV_COLS]).astype(o_ref.dtype)

    return pl.pallas_call(
        body, name="ffn_act_fwd", grid=(D_FF // CONV_COLS, s // CONV_ROWS), in_specs=_conv_in_specs(),
        out_specs=pl.BlockSpec((CONV_ROWS, CONV_COLS), lambda j, i: (i, j)),
        out_shape=jax.ShapeDtypeStruct((s, D_FF), MXU_DTYPE),
        compiler_params=_params(("parallel", "parallel")),
    )(hh, hh, conv_w, conv_b.reshape(1, -1))


def _ffn_act_bwd(hh, conv_w, conv_b, da):
    s = hh.shape[0]

    def body(m_ref, h_ref, w_ref, b_ref, da_ref, d_ref, dw_ref, db_ref):
        first = pl.program_id(1) == 0

        @pl.when(first)
        def _():
            dw_ref[...] = jnp.zeros_like(dw_ref)
            db_ref[...] = jnp.zeros_like(db_ref)

        main = m_ref[...]
        x1, x2 = _causal_taps(main, h_ref[...], first)
        conv = w_ref[0:1, :] * x2 + w_ref[1:2, :] * x1 + w_ref[2:3, :] * main + b_ref[...]
        da = da_ref[...].astype(F32)
        act, dact = _gelu_pair(conv[:, CONV_COLS:])
        dconv = jnp.concatenate([da * act, da * conv[:, :CONV_COLS] * dact], axis=1)
        d_ref[...] = dconv.astype(d_ref.dtype)
        for t, tap in enumerate((x2, x1, main)):
            dw_ref[t:t + 1, :] += jnp.sum(dconv * tap, axis=0, keepdims=True)
        db_ref[...] += jnp.sum(dconv, axis=0, keepdims=True)

    return pl.pallas_call(
        body, name="ffn_act_bwd", grid=(D_FF // CONV_COLS, s // CONV_ROWS),
        in_specs=_conv_in_specs() + [pl.BlockSpec((CONV_ROWS, CONV_COLS), lambda j, i: (i, j))],
        out_specs=[pl.BlockSpec((CONV_ROWS, CONV_PAIR), lambda j, i: (i, j)),
                   pl.BlockSpec((3, CONV_PAIR), lambda j, i: (0, j)), pl.BlockSpec((1, CONV_PAIR), lambda j, i: (0, j))],
        out_shape=[jax.ShapeDtypeStruct((s, 2 * D_FF), MXU_DTYPE), jax.ShapeDtypeStruct((3, 2 * D_FF), F32),
                   jax.ShapeDtypeStruct((1, 2 * D_FF), F32)],
        compiler_params=_params(("parallel", "arbitrary")),
    )(hh, hh, conv_w, conv_b.reshape(1, -1), da)


def _conv_transpose(dconv, conv_w):
    s, n = dconv.shape
    nrow = s // CONV_ROWS
    halo_rows = 16

    def body(m_ref, nx_ref, w_ref, o_ref):
        main = m_ref[...].astype(F32)
        last = pl.program_id(1) == nrow - 1
        nx = nx_ref[...].astype(F32)
        n0 = jnp.where(last, 0.0, nx[0:1, :])
        n1 = jnp.where(last, 0.0, nx[1:2, :])
        row = lax.broadcasted_iota(jnp.int32, (HALO, main.shape[1]), 0)
        r1 = pltpu.roll(main, CONV_ROWS - 1, 0)
        r2 = pltpu.roll(main, CONV_ROWS - 2, 0)
        end1 = jnp.where(row == HALO - 1, n0, r1[CONV_ROWS - HALO:])
        end2 = jnp.where(row == HALO - 2, n0, jnp.where(row == HALO - 1, n1, r2[CONV_ROWS - HALO:]))
        y1 = jnp.concatenate([r1[:CONV_ROWS - HALO], end1], axis=0)
        y2 = jnp.concatenate([r2[:CONV_ROWS - HALO], end2], axis=0)
        o_ref[...] = (w_ref[2:3, :] * main + w_ref[1:2, :] * y1 + w_ref[0:1, :] * y2).astype(o_ref.dtype)

    nxt = lambda i: jnp.minimum((i + 1) * (CONV_ROWS // halo_rows), s // halo_rows - 1)
    return pl.pallas_call(
        body, name="ffn_conv_transpose", grid=(n // CONV_COLS, nrow),
        in_specs=[pl.BlockSpec((CONV_ROWS, CONV_COLS), lambda j, i: (i, j)),
                  pl.BlockSpec((halo_rows, CONV_COLS), lambda j, i: (nxt(i), j)),
                  pl.BlockSpec((3, CONV_COLS), lambda j, i: (0, j))],
        out_specs=pl.BlockSpec((CONV_ROWS, CONV_COLS), lambda j, i: (i, j)),
        out_shape=jax.ShapeDtypeStruct((s, n), MXU_DTYPE),
        compiler_params=_params(("parallel", "parallel")),
    )(dconv, dconv, conv_w)


def _ple_fwd(xn, p, w_gate, w_proj, h):
    s = xn.shape[0]
    tm = 512

    def body(x_ref, p_ref, wg_ref, wp_ref, h_ref, o_ref):
        gate = jax.nn.sigmoid(_dot(x_ref[...].astype(MXU_DTYPE), wg_ref[...].astype(MXU_DTYPE), NN))
        proj = _dot(p_ref[...].astype(MXU_DTYPE), wp_ref[...].astype(MXU_DTYPE), NN)
        o_ref[...] = h_ref[...] + gate * proj

    return pl.pallas_call(
        body, name="ple_fwd", grid=(s // tm,),
        in_specs=[pl.BlockSpec((tm, D_MODEL), lambda i: (i, 0)), pl.BlockSpec((tm, PLE_DIM), lambda i: (i, 0)),
                  pl.BlockSpec((D_MODEL, D_MODEL), lambda i: (0, 0)), pl.BlockSpec((PLE_DIM, D_MODEL), lambda i: (0, 0)),
                  pl.BlockSpec((tm, D_MODEL), lambda i: (i, 0))],
        out_specs=pl.BlockSpec((tm, D_MODEL), lambda i: (i, 0)),
        out_shape=jax.ShapeDtypeStruct((s, D_MODEL), F32),
        compiler_params=_params(("parallel",)),
    )(xn, p, w_gate, w_proj, h)


def _ple_bwd(xn, p, w_gate, w_proj, dh):
    s = xn.shape[0]
    tm = 512

    def body(x_ref, p_ref, wg_ref, wp_ref, dh_ref, dpre_ref, dproj_ref):
        gate = jax.nn.sigmoid(_dot(x_ref[...].astype(MXU_DTYPE), wg_ref[...].astype(MXU_DTYPE), NN))
        proj = _dot(p_ref[...].astype(MXU_DTYPE), wp_ref[...].astype(MXU_DTYPE), NN)
        dh = dh_ref[...]
        dpre_ref[...] = (dh * proj * gate * (1.0 - gate)).astype(dpre_ref.dtype)
        dproj_ref[...] = (dh * gate).astype(dproj_ref.dtype)

    row = pl.BlockSpec((tm, D_MODEL), lambda i: (i, 0))
    return pl.pallas_call(
        body, name="ple_bwd", grid=(s // tm,),
        in_specs=[row, pl.BlockSpec((tm, PLE_DIM), lambda i: (i, 0)),
                  pl.BlockSpec((D_MODEL, D_MODEL), lambda i: (0, 0)), pl.BlockSpec((PLE_DIM, D_MODEL), lambda i: (0, 0)),
                  row],
        out_specs=[row, row],
        out_shape=[jax.ShapeDtypeStruct((s, D_MODEL), MXU_DTYPE)] * 2,
        compiler_params=_params(("parallel",)),
    )(xn, p, w_gate, w_proj, dh)


O_SGU = 3 * ATTN_W
O_SSM = O_SGU + 2 * SGU_W


def _layer_consts(w, i):
    causal = jnp.asarray(np.tril(np.ones((SGU_CHUNK, SGU_CHUNK), np.float32)))
    return {
        "sgu_w_mask": w["sgu_w"][i] * causal,
        "sgu_b_t": w["sgu_b"][i].T,
        "ssm_ops": _ssm_operands(w["ssm_a_re"][i], w["ssm_a_im"][i], w["ssm_log_dt"][i], w["ssm_b_re"][i],
                                 w["ssm_b_im"][i], w["ssm_c_re"][i], w["ssm_c_im"][i]),
    }


def _layer_fwd(h0, p_i, w, i, bias):
    c = _layer_consts(w, i)
    w_in = w["w_in"][i]
    xn1 = _rms_fwd(h0, w["norm_attn_g"][i], name="rms_attn_fwd")
    qkv = _in_proj_qkv(xn1, w_in[:, :O_SGU])
    zs = _matmul(xn1, w_in[:, O_SGU:O_SSM], name="in_proj_sgu", out_dtype=F32, tm=1024, tn=512)
    us = _matmul(xn1, w_in[:, O_SSM:], name="in_proj_ssm", out_dtype=F32, tm=1024, tn=256)
    y_attn, lse = _attn2_fwd(qkv, bias)
    y_sgu = _sgu_fwd(zs, w["sgu_ln_g"][i], w["sgu_ln_b"][i], c["sgu_w_mask"], c["sgu_b_t"])
    y_ssm, entry = _ssm_fwd(us, c["ssm_ops"], w["ssm_d"][i], w["ssm_glu_w"][i], w["ssm_glu_b"][i])
    mix = _mix_fwd(y_attn, y_sgu, y_ssm, w["branch_norm_g"][i])
    h1 = _matmul(mix, w["w_out"][i], name="out_proj", out_dtype=F32, tm=512, tn=1024, residual=h0)
    xn2 = _rms_fwd(h1, w["norm_ffn_g"][i], name="rms_ffn_fwd")
    hh = _matmul(xn2, w["ffn_w_up"][i], name="ffn_up", out_dtype=F32, tm=1024, tn=1408)
    act = _ffn_act_fwd(hh, w["ffn_conv_w"][i], w["ffn_conv_b"][i])
    h2 = _matmul(act, w["ffn_w_down"][i], name="ffn_down", out_dtype=F32, tm=512, tn=1024, residual=h1)
    xn3 = _rms_fwd(h2, w["norm_ple_g"][i], name="rms_ple_fwd")
    h3 = _ple_fwd(xn3, p_i, w["ple_w_gate"][i], w["ple_w_proj"][i], h2)
    saved = dict(h0=h0, xn1=xn1, qkv=qkv, zs=zs, us=us, y_attn=y_attn, lse=lse, y_sgu=y_sgu, y_ssm=y_ssm,
                 entry=entry, mix=mix, h1=h1, xn2=xn2, hh=hh, act=act, h2=h2, xn3=xn3, consts=c)
    return h3, saved


def _layer_bwd(dh3, sv, p_i, w, i, bias):
    c = sv["consts"]
    g = {}
    dpre, dproj = _ple_bwd(sv["xn3"], p_i, w["ple_w_gate"][i], w["ple_w_proj"][i], dh3)
    g["ple_w_gate"] = _matmul_tn(sv["xn3"], dpre, name="d_ple_w_gate", tk=1024, tn=1024)
    g["ple_w_proj"] = _matmul_tn(p_i, dproj, name="d_ple_w_proj", tk=256, tn=1024)
    dxn3 = _matmul(dpre, w["ple_w_gate"][i], name="d_xn_ple", out_dtype=F32, tm=512, tn=1024, trans_b=True)
    dh2, g["norm_ple_g"] = _rms_bwd(sv["h2"], w["norm_ple_g"][i], dxn3, dh3, name="rms_ple_bwd")
    g["ffn_w_down"] = _matmul_tn(sv["act"], dh2, name="d_ffn_w_down", tk=1408, tn=1024)
    dact = _matmul(dh2, w["ffn_w_down"][i], name="d_ffn_act", out_dtype=MXU_DTYPE, tm=512, tn=1408, trans_b=True)
    dconv, g["ffn_conv_w"], g["ffn_conv_b"] = _ffn_act_bwd(sv["hh"], w["ffn_conv_w"][i], w["ffn_conv_b"][i], dact)
    dhh = _conv_transpose(dconv, w["ffn_conv_w"][i])
    g["ffn_w_up"] = _matmul_tn(sv["xn2"], dhh, name="d_ffn_w_up", tk=1024, tn=1408)
    dxn2 = _matmul(dhh, w["ffn_w_up"][i], name="d_xn_ffn", out_dtype=F32, tm=512, tn=512, trans_b=True)
    dh1, g["norm_ffn_g"] = _rms_bwd(sv["h1"], w["norm_ffn_g"][i], dxn2, dh2, name="rms_ffn_bwd")
    g["w_out"] = _matmul_tn(sv["mix"], dh1, name="d_w_out", tk=1024, tn=1024)
    dmix = _matmul(dh1, w["w_out"][i], name="d_mix", out_dtype=F32, tm=512, tn=1024, trans_b=True)
    dy_attn, delta, dy_sgu, dy_ssm, g["branch_norm_g"] = _mix_bwd(sv["y_attn"], sv["y_sgu"], sv["y_ssm"],
                                                                  w["branch_norm_g"][i], dmix)
    dq, dk, dv, ek, ev, dbias = _attn2_bwd(sv["qkv"], bias, sv["lse"], delta, dy_attn)
    dzs, g["sgu_ln_g"], g["sgu_ln_b"], dsw, dsb = _sgu_bwd(sv["zs"], w["sgu_ln_g"][i], w["sgu_ln_b"][i],
                                                          c["sgu_w_mask"], c["sgu_b_t"], dy_sgu)
    causal = jnp.asarray(np.tril(np.ones((SGU_CHUNK, SGU_CHUNK), np.float32)))
    g["sgu_w"] = dsw * causal
    g["sgu_b"] = dsb.T
    dus, dbb, dcm, da, g["ssm_d"], g["ssm_glu_w"], g["ssm_glu_b"] = _ssm_bwd(
        sv["us"], sv["entry"], c["ssm_ops"], w["ssm_d"][i], w["ssm_glu_w"][i], w["ssm_glu_b"][i], dy_ssm)
    dbb5 = dbb.reshape(SSM_G, SSM_C, 2, SSM_G, SSM_N)
    dbbar = jnp.einsum("gcpgn->pgnc", dbb5)
    dcm5 = dcm.reshape(2, SSM_G, SSM_N, SSM_G, SSM_C)
    dcc = jnp.einsum("pgngc->pgcn", dcm5)
    g["ssm_c_re"] = dcc[0]
    g["ssm_c_im"] = -dcc[1]
    da2 = da.reshape(2, SSM_G, SSM_N)
    _, vjp = jax.vjp(_ssm_discretize, w["ssm_a_re"][i], w["ssm_a_im"][i], w["ssm_log_dt"][i],
                     w["ssm_b_re"][i], w["ssm_b_im"][i])
    (g["ssm_a_re"], g["ssm_a_im"], g["ssm_log_dt"], g["ssm_b_re"], g["ssm_b_im"]) = vjp(
        (da2[0], da2[1], dbbar[0], dbbar[1]))
    dz = _attn2_bwd_sum(dq, dk, dv, ek, ev, dzs, dus)
    g["w_in"] = _matmul_tn(sv["xn1"], dz, name="d_w_in", tk=1024, tn=1152)
    dxn1 = _matmul(dz, w["w_in"][i], name="d_xn_attn", out_dtype=F32, tm=512, tn=1024, trans_b=True)
    dh0, g["norm_attn_g"] = _rms_bwd(sv["h0"], w["norm_attn_g"][i], dxn1, dh1, name="rms_attn_bwd")
    for k in ("norm_ple_g", "norm_ffn_g", "branch_norm_g", "norm_attn_g", "sgu_ln_g", "sgu_ln_b", "ssm_d",
              "ssm_glu_b", "ffn_conv_b"):
        g[k] = g[k].reshape(-1)
    return dh0, g, dbias


def _local_step(x, p, target, w):
    ff_names = ("ffn_w_up", "ffn_conv_w", "ffn_conv_b")
    w = dict(w)
    for k in ff_names:
        w[k] = _interleave_ff(w[k])
    bias = _bias_build(w["rel_bias"])
    h = x
    saved = []
    for i in range(DEPTH):
        h, sv = _layer_fwd(h, p[i], w, i, bias)
        saved.append(sv)
    loss, dh, dgf = _loss_head(h, w["final_norm_g"], target)
    layer_grads = [None] * DEPTH
    dbias = None
    for i in reversed(range(DEPTH)):
        dh, layer_grads[i], db = _layer_bwd(dh, saved[i], p[i], w, i, bias)
        dbias = db if dbias is None else dbias + db
    grads = {k: jnp.stack([layer_grads[i][k] for i in range(DEPTH)]) for k in layer_grads[0]}
    for k in ff_names:
        grads[k] = _deinterleave_ff(grads[k])
    grads["rel_bias"] = _bias_reduce(dbias)
    grads["final_norm_g"] = dgf.reshape(-1)
    return loss, dh, grads


def _pad_rows(a2, mult=16):
    r = (-a2.shape[0]) % mult
    return a2 if r == 0 else jnp.concatenate([a2, jnp.zeros((r, a2.shape[1]), a2.dtype)], axis=0)


def _as_rows(a, rows=None):
    flat = a.reshape(-1)
    if rows is None:
        rows = -(-flat.shape[0] // (16 * PACK_COLS)) * 16
    return jnp.pad(flat, (0, rows * PACK_COLS - flat.shape[0])).reshape(rows, PACK_COLS)


def _shard_shape(name):
    full, ax = BIG_FULL[name]
    shp = [DEPTH] + list(full)
    shp[ax] //= N_CHIPS
    return tuple(shp)


EXACT_NAMES = ("ffn_conv_w",)


def _pack_rows_of(name):
    n = int(np.prod(_shard_shape(name))) * (2 if name in EXACT_NAMES else 1)
    rows = -(-n // PACK_COLS)
    return -(-rows // 16) * 16


def _pack_shards(shards, dtype, exact=False):
    split_words = exact and jnp.dtype(dtype).itemsize == 2
    parts = []
    for n in BIG_NAMES:
        a = shards[n]
        if split_words and n in EXACT_NAMES:
            a = lax.bitcast_convert_type(a.astype(F32), dtype)
        parts.append(_as_rows(a.astype(dtype), _pack_rows_of(n)))
    used = sum(pt.shape[0] for pt in parts)
    parts.append(jnp.zeros((PACK_ROWS - used, PACK_COLS), dtype))
    return jnp.concatenate(parts, axis=0)


def _unpack_shard(flat, name, exact=False):
    off = 0
    for n in BIG_NAMES:
        if n == name:
            break
        off += _pack_rows_of(n)
    shp = _shard_shape(name)
    cnt = int(np.prod(shp))
    vec = flat[off:off + _pack_rows_of(name)].reshape(-1)
    if exact and name in EXACT_NAMES and jnp.dtype(flat.dtype).itemsize == 2:
        return lax.bitcast_convert_type(vec[:2 * cnt].reshape(shp + (2,)), F32)
    return vec[:cnt].reshape(shp)


def _split_full(full, name):
    _, ax = BIG_FULL[name]
    return jnp.stack(jnp.split(full, N_CHIPS, axis=ax))


def _join_shards(stacked, name):
    _, ax = BIG_FULL[name]
    return jnp.concatenate([stacked[k] for k in range(N_CHIPS)], axis=ax)


def _small_shapes(w):
    return [(n, w[n].shape) for n in SMALL_NAMES]


def _pack_small(d):
    flat = jnp.concatenate([d[n].astype(F32).reshape(-1) for n in SMALL_NAMES])
    flat = jnp.concatenate([flat, jnp.zeros((SMALL_ROWS * PACK_COLS - flat.shape[0],), F32)])
    return flat.reshape(SMALL_ROWS, PACK_COLS)


def _unpack_small(flat, shapes):
    out, off = {}, 0
    v = flat.reshape(-1)
    for n, shp in shapes:
        cnt = int(np.prod(shp))
        out[n] = v[off:off + cnt].reshape(shp)
        off += cnt
    return out


MESH = pl.DeviceIdType.MESH
ANY = pl.BlockSpec(memory_space=pl.ANY)


def _me():
    return lax.axis_index("x"), lax.axis_index("y"), lax.axis_index("c")


def _other_chips(x, y):
    return [(1 - x, y), (x, 1 - y), (1 - x, 1 - y)]


def _gather_weights(wflat):
    def body(w_ref, out_ref, send_sems, recv_sems, local_sem):
        x, y, c = _me()
        sibling = (x, y, 1 - c)
        chips = _other_chips(x, y)

        def rows(chip, half):
            return out_ref.at[2 * chip[0] + chip[1], pl.ds(half * PACK_HALF, PACK_HALF), :]

        def copy(k, chip, half, to, src=None):
            return pltpu.make_async_remote_copy(
                src_ref=rows(chip, half) if src is None else src, dst_ref=rows(chip, half),
                send_sem=send_sems.at[k], recv_sem=recv_sems.at[k], device_id=to, device_id_type=MESH)

        mine = pltpu.make_async_copy(w_ref, out_ref.at[2 * x + y], local_sem)
        mine.start()
        my_half = w_ref.at[pl.ds(c * PACK_HALF, PACK_HALF), :]
        first = [copy(j, (x, y), c, (*chip, c), src=my_half) for j, chip in enumerate(chips)]
        for cp in first:
            cp.start()
        passed = [copy(3 + j, chip, c, sibling) for j, chip in enumerate(chips)]
        for j, chip in enumerate(chips):
            copy(j, chip, c, (x, y, c)).wait_recv()
            passed[j].start()
        for j, chip in enumerate(chips):
            copy(3 + j, chip, 1 - c, (x, y, c)).wait_recv()
        for cp in first + passed:
            cp.wait_send()
        mine.wait()

    return pl.pallas_call(
        body, name="gather_weights", in_specs=[ANY], out_specs=ANY,
        out_shape=jax.ShapeDtypeStruct((N_CHIPS, PACK_ROWS, PACK_COLS), wflat.dtype),
        scratch_shapes=[pltpu.SemaphoreType.DMA((6,)), pltpu.SemaphoreType.DMA((6,)), pltpu.SemaphoreType.DMA],
    )(wflat)


def _exchange_partials(gb, gs):
    def body(gb_ref, gs_ref, half_ref, small_ref, send_sems, recv_sems, local_sem):
        x, y, c = _me()
        me_idx = 4 * x + 2 * y + c
        mine = pltpu.make_async_copy(gs_ref, small_ref.at[me_idx], local_sem)
        mine.start()
        d2d = pltpu.make_async_remote_copy(
            src_ref=gb_ref.at[:, pl.ds((1 - c) * PACK_HALF, PACK_HALF), :], dst_ref=half_ref,
            send_sem=send_sems.at[0], recv_sem=recv_sems.at[0], device_id=(x, y, 1 - c), device_id_type=MESH)
        d2d.start()
        copies = []
        for k in range(1, N_DEV):
            fx, fy, fc = (k >> 2) & 1, (k >> 1) & 1, k & 1
            peer = (x ^ fx, y ^ fy, c ^ fc)
            copies.append(pltpu.make_async_remote_copy(
                src_ref=gs_ref, dst_ref=small_ref.at[me_idx], send_sem=send_sems.at[k], recv_sem=recv_sems.at[k],
                device_id=peer, device_id_type=MESH))
        for cp in copies:
            cp.start()
        for k in range(1, N_DEV):
            fx, fy, fc = (k >> 2) & 1, (k >> 1) & 1, k & 1
            peer_idx = 4 * (x ^ fx) + 2 * (y ^ fy) + (c ^ fc)
            pltpu.make_async_remote_copy(
                src_ref=gs_ref, dst_ref=small_ref.at[peer_idx], send_sem=send_sems.at[k], recv_sem=recv_sems.at[k],
                device_id=(x, y, c), device_id_type=MESH).wait_recv()
        d2d.wait_recv()
        d2d.wait_send()
        for cp in copies:
            cp.wait_send()
        mine.wait()

    return pl.pallas_call(
        body, name="exchange_partials", in_specs=[ANY, ANY], out_specs=[ANY, ANY],
        out_shape=[jax.ShapeDtypeStruct((N_CHIPS, PACK_HALF, PACK_COLS), gb.dtype),
                   jax.ShapeDtypeStruct((N_DEV, SMALL_ROWS, PACK_COLS), F32)],
        scratch_shapes=[pltpu.SemaphoreType.DMA((N_DEV,)), pltpu.SemaphoreType.DMA((N_DEV,)), pltpu.SemaphoreType.DMA],
    )(gb, gs)


RED_ROWS = 256


def _chip_partials(gb, sib, c_idx):
    nrow = PACK_HALF // RED_ROWS

    def body(c_ref, a_ref, b_ref, o_ref):
        del c_ref
        o_ref[...] = (a_ref[...].astype(F32) + b_ref[...].astype(F32)).astype(o_ref.dtype)

    blk = (1, RED_ROWS, PACK_COLS)
    return pl.pallas_call(
        body, name="chip_partials",
        grid_spec=pltpu.PrefetchScalarGridSpec(
            num_scalar_prefetch=1, grid=(N_CHIPS, nrow),
            in_specs=[pl.BlockSpec(blk, lambda k, i, c: (k, c[0] * nrow + i, 0)),
                      pl.BlockSpec(blk, lambda k, i, c: (k, i, 0))],
            out_specs=pl.BlockSpec(blk, lambda k, i, c: (k, i, 0))),
        out_shape=jax.ShapeDtypeStruct((N_CHIPS, PACK_HALF, PACK_COLS), gb.dtype),
        compiler_params=_params(("parallel", "parallel")),
    )(c_idx, gb, sib)


def _scatter_partials(pc):
    def body(pc_ref, out_ref, send_sems, recv_sems):
        x, y, c = _me()
        chips = _other_chips(x, y)
        copies = [pltpu.make_async_remote_copy(
            src_ref=pc_ref.at[2 * chip[0] + chip[1]], dst_ref=out_ref.at[k],
            send_sem=send_sems.at[k], recv_sem=recv_sems.at[k], device_id=(*chip, c), device_id_type=MESH)
            for k, chip in enumerate(chips)]
        for cp in copies:
            cp.start()
        for cp in copies:
            cp.wait_recv()
        for cp in copies:
            cp.wait_send()

    return pl.pallas_call(
        body, name="scatter_partials", in_specs=[ANY], out_specs=ANY,
        out_shape=jax.ShapeDtypeStruct((3, PACK_HALF, PACK_COLS), pc.dtype),
        scratch_shapes=[pltpu.SemaphoreType.DMA((3,)), pltpu.SemaphoreType.DMA((3,))],
    )(pc)


def _final_half(gb, sib, recv, idx):
    nrow = PACK_HALF // RED_ROWS

    def body(idx_ref, a_ref, b_ref, r_ref, o_ref):
        del idx_ref
        acc = a_ref[0].astype(F32) + b_ref[0].astype(F32)
        for k in range(3):
            acc = acc + r_ref[k].astype(F32)
        o_ref[...] = acc

    return pl.pallas_call(
        body, name="final_half",
        grid_spec=pltpu.PrefetchScalarGridSpec(
            num_scalar_prefetch=1, grid=(nrow,),
            in_specs=[pl.BlockSpec((1, RED_ROWS, PACK_COLS), lambda i, idx: (idx[0], idx[1] * nrow + i, 0)),
                      pl.BlockSpec((1, RED_ROWS, PACK_COLS), lambda i, idx: (idx[0], i, 0)),
                      pl.BlockSpec((3, RED_ROWS, PACK_COLS), lambda i, idx: (0, i, 0))],
            out_specs=pl.BlockSpec((RED_ROWS, PACK_COLS), lambda i, idx: (i, 0))),
        out_shape=jax.ShapeDtypeStruct((PACK_HALF, PACK_COLS), F32),
        compiler_params=_params(("parallel",)),
    )(idx, gb, sib, recv)


def _share_halves(half):
    def body(h_ref, out_ref, send_sem, recv_sem, local_sem):
        x, y, c = _me()
        dst = out_ref.at[pl.ds(c * PACK_HALF, PACK_HALF), :]
        mine = pltpu.make_async_copy(h_ref, dst, local_sem)
        mine.start()
        cp = pltpu.make_async_remote_copy(src_ref=h_ref, dst_ref=dst, send_sem=send_sem, recv_sem=recv_sem,
                                          device_id=(x, y, 1 - c), device_id_type=MESH)
        cp.start()
        pltpu.make_async_remote_copy(src_ref=h_ref, dst_ref=out_ref.at[pl.ds((1 - c) * PACK_HALF, PACK_HALF), :],
                                     send_sem=send_sem, recv_sem=recv_sem, device_id=(x, y, c),
                                     device_id_type=MESH).wait_recv()
        cp.wait_send()
        mine.wait()

    return pl.pallas_call(
        body, name="share_halves", in_specs=[ANY], out_specs=ANY,
        out_shape=jax.ShapeDtypeStruct((PACK_ROWS, PACK_COLS), F32),
        scratch_shapes=[pltpu.SemaphoreType.DMA, pltpu.SemaphoreType.DMA, pltpu.SemaphoreType.DMA],
    )(half)


def _sum_small(allsmall):
    def body(a_ref, o_ref):
        acc = a_ref[0]
        for k in range(1, N_DEV):
            acc = acc + a_ref[k]
        o_ref[...] = acc

    tr = 96
    return pl.pallas_call(
        body, name="sum_small", grid=(SMALL_ROWS // tr,),
        in_specs=[pl.BlockSpec((N_DEV, tr, PACK_COLS), lambda i: (0, i, 0))],
        out_specs=pl.BlockSpec((tr, PACK_COLS), lambda i: (i, 0)),
        out_shape=jax.ShapeDtypeStruct((SMALL_ROWS, PACK_COLS), F32),
        compiler_params=_params(("parallel",)),
    )(allsmall)


def _adamw(w, g, m, v, *, name):
    shape = w.shape
    cols = shape[-1]
    as2 = lambda t: t.reshape(-1, cols)
    w2, g2, m2, v2 = as2(w), as2(g), as2(m), as2(v)
    rows = w2.shape[0]
    tr = rows
    if rows * cols * 4 > (1 << 20):
        tr = _tile(rows, max(8, (1 << 20) // (cols * 4) // 8 * 8), 8)

    def body(w_ref, g_ref, m_ref, v_ref, d_ref, mo_ref, vo_ref):
        gg = g_ref[...]
        mn = ADAM_B1 * m_ref[...] + (1.0 - ADAM_B1) * gg
        vn = ADAM_B2 * v_ref[...] + (1.0 - ADAM_B2) * (gg * gg)
        m_hat = mn / (1.0 - ADAM_B1 ** ADAM_STEP)
        v_hat = vn / (1.0 - ADAM_B2 ** ADAM_STEP)
        d_ref[...] = -ADAM_LR * (m_hat / (jnp.sqrt(v_hat) + ADAM_EPS) + ADAM_WD * w_ref[...])
        mo_ref[...] = mn
        vo_ref[...] = vn

    blk = pl.BlockSpec((tr, cols), lambda i: (i, 0))
    outs = pl.pallas_call(
        body, name=name, grid=(rows // tr,), in_specs=[blk] * 4, out_specs=[blk] * 3,
        out_shape=[jax.ShapeDtypeStruct((rows, cols), F32)] * 3,
        compiler_params=_params(("parallel",)),
    )(w2, g2, m2, v2)
    return tuple(t.reshape(shape) for t in outs)


def kernel(x, p, rel_bias, norm_attn_g, w_in, sgu_ln_g, sgu_ln_b, sgu_w, sgu_b, ssm_a_re, ssm_a_im, ssm_log_dt, ssm_b_re, ssm_b_im, ssm_c_re, ssm_c_im, ssm_d, ssm_glu_w, ssm_glu_b, branch_norm_g, w_out, norm_ffn_g, ffn_w_up, ffn_conv_w, ffn_conv_b, ffn_w_down, norm_ple_g, ple_w_gate, ple_w_proj, final_norm_g, loss_target, m_rel_bias, m_norm_attn_g, m_w_in, m_sgu_ln_g, m_sgu_ln_b, m_sgu_w, m_sgu_b, m_ssm_a_re, m_ssm_a_im, m_ssm_log_dt, m_ssm_b_re, m_ssm_b_im, m_ssm_c_re, m_ssm_c_im, m_ssm_d, m_ssm_glu_w, m_ssm_glu_b, m_branch_norm_g, m_w_out, m_norm_ffn_g, m_ffn_w_up, m_ffn_conv_w, m_ffn_conv_b, m_ffn_w_down, m_norm_ple_g, m_ple_w_gate, m_ple_w_proj, m_final_norm_g, v_rel_bias, v_norm_attn_g, v_w_in, v_sgu_ln_g, v_sgu_ln_b, v_sgu_w, v_sgu_b, v_ssm_a_re, v_ssm_a_im, v_ssm_log_dt, v_ssm_b_re, v_ssm_b_im, v_ssm_c_re, v_ssm_c_im, v_ssm_d, v_ssm_glu_w, v_ssm_glu_b, v_branch_norm_g, v_w_out, v_norm_ffn_g, v_ffn_w_up, v_ffn_conv_w, v_ffn_conv_b, v_ffn_w_down, v_norm_ple_g, v_ple_w_gate, v_ple_w_proj, v_final_norm_g):
    args = dict(locals())
    wts = {n: args[n] for n in WEIGHT_NAMES}
    mom_m = {n: args["m_" + n] for n in WEIGHT_NAMES}
    mom_v = {n: args["v_" + n] for n in WEIGHT_NAMES}

    wall = _gather_weights(_pack_shards({n: wts[n] for n in BIG_NAMES}, MXU_DTYPE, exact=True))
    full = dict(wts)
    for n in BIG_NAMES:
        full[n] = _join_shards(jnp.stack([_unpack_shard(wall[k], n, exact=True) for k in range(N_CHIPS)]), n)
    full["ffn_conv_w"] = full["ffn_conv_w"].astype(F32)

    loss, dx, grads = _local_step(x[0], p[:, 0], loss_target[0], full)
    loss = lax.psum(loss[0, 0], MESH_AXES)

    xi, yi, ci = _me()
    stacked = {n: _split_full(grads[n], n) for n in BIG_NAMES}
    gb = jnp.stack([_pack_shards({n: stacked[n][k] for n in BIG_NAMES}, MXU_DTYPE) for k in range(N_CHIPS)])
    gs = _pack_small(grads)
    sib, allsmall = _exchange_partials(gb, gs)
    pc = _chip_partials(gb, sib, jnp.stack([ci]).astype(jnp.int32))
    recv = _scatter_partials(pc)
    half = _final_half(gb, sib, recv, jnp.stack([2 * xi + yi, ci]).astype(jnp.int32))
    gflat = _share_halves(half)
    gsmall = _unpack_small(_sum_small(allsmall), _small_shapes(wts))

    g_out, d_out, m_out, v_out = {}, {}, {}, {}
    for n in BIG_NAMES:
        g_out[n] = _unpack_shard(gflat, n)
        d_out[n], m_out[n], v_out[n] = _adamw(wts[n], g_out[n], mom_m[n], mom_v[n], name="adamw_" + n)
    sw = _pack_small(wts)
    d_s, m_s, v_s = _adamw(sw, _pack_small(gsmall), _pack_small(mom_m), _pack_small(mom_v), name="adamw_small")
    shapes = _small_shapes(wts)
    d_sm, m_sm, v_sm = _unpack_small(d_s, shapes), _unpack_small(m_s, shapes), _unpack_small(v_s, shapes)
    for n in SMALL_NAMES:
        g_out[n], d_out[n], m_out[n], v_out[n] = gsmall[n], d_sm[n], m_sm[n], v_sm[n]

    return (loss, dx[None], *[g_out[n] for n in WEIGHT_NAMES], *[d_out[n] for n in WEIGHT_NAMES],
            *[m_out[n] for n in WEIGHT_NAMES], *[v_out[n] for n in WEIGHT_NAMES])
```

```python
import functools
import math

import numpy as np
import jax
import jax.numpy as jnp
from jax import lax
from jax.experimental import pallas as pl
from jax.experimental.pallas import tpu as pltpu

F32 = jnp.float32
MXU_DTYPE = jnp.bfloat16
VMEM_LIMIT_BYTES = 52 * 1024 * 1024

D_MODEL = 1024
DEPTH = 2
PLE_DIM = 256
HEAD_DIM = 64
N_HEADS = 8
ATTN_W = 512
QBLK = 128
BRANCH_DIL = (1, 4, 16)
N_BUCKETS = 32
REL_MAX_DIST = 2048
SGU_W = 256
SGU_G = 4
SGU_GW = 64
SGU_CHUNK = 128
SSM_W = 256
SSM_G = 16
SSM_C = 16
SSM_N = 64
NSTATE = SSM_G * SSM_N
D_FF = 2816
EPS = 1e-6
NEG_INF = -1e30
ATTN_SCALE = HEAD_DIM ** -0.5

ADAM_LR = 0.001
ADAM_B1 = 0.9
ADAM_B2 = 0.999
ADAM_EPS = 1e-08
ADAM_WD = 0.01
ADAM_STEP = 10

SSM_NSEG = 8
SSM_TSEG = 64
SSM_TB = SSM_NSEG * SSM_TSEG
SSM_LANE_CHUNK = 512

MESH_AXES = ("x", "y", "c")
N_CHIPS = 4
N_DEV = 8

BIG_NAMES = ("w_in", "ssm_glu_w", "w_out", "ffn_w_up", "ffn_conv_w", "ffn_w_down", "ple_w_gate", "ple_w_proj")
BIG_FULL = {
    "w_in": ((D_MODEL, 2304), 2),
    "ssm_glu_w": ((SSM_W, SSM_W), 1),
    "w_out": ((D_MODEL, D_MODEL), 1),
    "ffn_w_up": ((D_MODEL, 2 * D_FF), 2),
    "ffn_conv_w": ((3, 2 * D_FF), 2),
    "ffn_w_down": ((D_FF, D_MODEL), 1),
    "ple_w_gate": ((D_MODEL, D_MODEL), 1),
    "ple_w_proj": ((PLE_DIM, D_MODEL), 2),
}
PACK_COLS = 1024
PACK_ROWS = 6656
PACK_HALF = PACK_ROWS // 2

SMALL_NAMES = ("rel_bias", "norm_attn_g", "sgu_ln_g", "sgu_ln_b", "sgu_w", "sgu_b", "ssm_a_re", "ssm_a_im",
               "ssm_log_dt", "ssm_b_re", "ssm_b_im", "ssm_c_re", "ssm_c_im", "ssm_d", "ssm_glu_b",
               "branch_norm_g", "norm_ffn_g", "ffn_conv_b", "norm_ple_g", "final_norm_g")
SMALL_ROWS = 288

WEIGHT_NAMES = ("rel_bias", "norm_attn_g", "w_in", "sgu_ln_g", "sgu_ln_b", "sgu_w", "sgu_b", "ssm_a_re", "ssm_a_im",
                "ssm_log_dt", "ssm_b_re", "ssm_b_im", "ssm_c_re", "ssm_c_im", "ssm_d", "ssm_glu_w", "ssm_glu_b",
                "branch_norm_g", "w_out", "norm_ffn_g", "ffn_w_up", "ffn_conv_w", "ffn_conv_b", "ffn_w_down",
                "norm_ple_g", "ple_w_gate", "ple_w_proj", "final_norm_g")


def _params(sem):
    return pltpu.CompilerParams(dimension_semantics=sem, vmem_limit_bytes=VMEM_LIMIT_BYTES)


def _tile(n, cap, mult=128):
    if n <= cap:
        return n
    best = None
    for t in range(mult, cap + 1, mult):
        if n % t == 0:
            best = t
    assert best is not None, (n, cap)
    return best


def _gelu(x):
    return 0.5 * x * (1.0 + jnp.tanh(0.7978845608028654 * (x + 0.044715 * x * x * x)))


def _gelu_pair(x):
    x2 = x * x
    t = jnp.tanh(0.7978845608028654 * x * (1.0 + 0.044715 * x2))
    half = 0.5 * (1.0 + t)
    return x * half, half + 0.5 * x * (1.0 - t * t) * (0.7978845608028654 + 3.0 * 0.044715 * 0.7978845608028654 * x2)


def _dot(a, b, dims):
    return lax.dot_general(a, b, (dims, ((), ())), preferred_element_type=F32)


def _dotf(a, b, dims):
    return _dot(a.astype(MXU_DTYPE), b.astype(MXU_DTYPE), dims)


NN = ((1,), (0,))
NT = ((1,), (1,))
TN = ((0,), (0,))


def _matmul(a, b, *, name, out_dtype, tm, tn, trans_b=False, residual=None, layer=None):
    m, k = a.shape
    n = b.shape[-2] if trans_b else b.shape[-1]
    tm = _tile(m, tm, 8)
    tn = _tile(n, tn)
    dims = NT if trans_b else NN
    lead = () if layer is None else (None,)
    lidx = () if layer is None else (layer,)

    def body(*refs):
        if residual is None:
            a_ref, b_ref, o_ref = refs
        else:
            a_ref, b_ref, r_ref, o_ref = refs
        acc = _dot(a_ref[...].astype(MXU_DTYPE), b_ref[...].astype(MXU_DTYPE), dims)
        if residual is not None:
            acc = acc + r_ref[...]
        o_ref[...] = acc.astype(o_ref.dtype)

    b_spec = (pl.BlockSpec(lead + (tn, k), lambda i, j: lidx + (j, 0)) if trans_b
              else pl.BlockSpec(lead + (k, tn), lambda i, j: lidx + (0, j)))
    in_specs = [pl.BlockSpec((tm, k), lambda i, j: (i, 0)), b_spec]
    args = [a, b]
    if residual is not None:
        in_specs.append(pl.BlockSpec((tm, tn), lambda i, j: (i, j)))
        args.append(residual)
    return pl.pallas_call(
        body, name=name, grid=(m // tm, n // tn), in_specs=in_specs,
        out_specs=pl.BlockSpec((tm, tn), lambda i, j: (i, j)),
        out_shape=jax.ShapeDtypeStruct((m, n), out_dtype),
        compiler_params=_params(("parallel", "parallel")),
    )(*args)


def _matmul_tn(a, g, *, name, tk, tn, tm=512):
    m, k = a.shape
    n = g.shape[1]
    tk = _tile(k, tk)
    tn = _tile(n, tn)
    tm = _tile(m, tm, 8)

    def body(a_ref, g_ref, o_ref):
        @pl.when(pl.program_id(2) == 0)
        def _():
            o_ref[...] = jnp.zeros_like(o_ref)

        o_ref[...] += _dot(a_ref[...].astype(MXU_DTYPE), g_ref[...].astype(MXU_DTYPE), TN)

    return pl.pallas_call(
        body, name=name, grid=(k // tk, n // tn, m // tm),
        in_specs=[pl.BlockSpec((tm, tk), lambda i, j, s: (s, i)),
                  pl.BlockSpec((tm, tn), lambda i, j, s: (s, j))],
        out_specs=pl.BlockSpec((tk, tn), lambda i, j, s: (i, j)),
        out_shape=jax.ShapeDtypeStruct((k, n), F32),
        compiler_params=_params(("parallel", "parallel", "arbitrary")),
    )(a, g)


ROWS = 512


def _rms_fwd(h, g, *, name):
    s, d = h.shape

    def body(h_ref, g_ref, o_ref):
        x = h_ref[...]
        r = lax.rsqrt(jnp.mean(x * x, axis=-1, keepdims=True) + EPS)
        o_ref[...] = (x * r * g_ref[...]).astype(o_ref.dtype)

    return pl.pallas_call(
        body, name=name, grid=(s // ROWS,),
        in_specs=[pl.BlockSpec((ROWS, d), lambda i: (i, 0)), pl.BlockSpec((1, d), lambda i: (0, 0))],
        out_specs=pl.BlockSpec((ROWS, d), lambda i: (i, 0)),
        out_shape=jax.ShapeDtypeStruct((s, d), MXU_DTYPE),
        compiler_params=_params(("parallel",)),
    )(h, g.reshape(1, d))


def _rms_bwd(h, g, dxn, dres, *, name):
    s, d = h.shape

    def body(h_ref, g_ref, dxn_ref, dres_ref, dh_ref, dg_ref):
        @pl.when(pl.program_id(0) == 0)
        def _():
            dg_ref[...] = jnp.zeros_like(dg_ref)

        x = h_ref[...]
        r = lax.rsqrt(jnp.mean(x * x, axis=-1, keepdims=True) + EPS)
        xhat = x * r
        dxn = dxn_ref[...].astype(F32)
        dg_ref[...] += jnp.sum(dxn * xhat, axis=0, keepdims=True)
        dxh = dxn * g_ref[...]
        dh_ref[...] = dres_ref[...] + r * (dxh - xhat * jnp.mean(dxh * xhat, axis=-1, keepdims=True))

    row = pl.BlockSpec((ROWS, d), lambda i: (i, 0))
    vec = pl.BlockSpec((1, d), lambda i: (0, 0))
    return pl.pallas_call(
        body, name=name, grid=(s // ROWS,), in_specs=[row, vec, row, row], out_specs=[row, vec],
        out_shape=[jax.ShapeDtypeStruct((s, d), F32), jax.ShapeDtypeStruct((1, d), F32)],
        compiler_params=_params(("arbitrary",)),
    )(h, g.reshape(1, d), dxn, dres)


def _loss_head(h, g, target):
    s, d = h.shape

    def body(h_ref, g_ref, t_ref, loss_ref, dh_ref, dg_ref):
        @pl.when(pl.program_id(0) == 0)
        def _():
            loss_ref[...] = jnp.zeros_like(loss_ref)
            dg_ref[...] = jnp.zeros_like(dg_ref)

        x = h_ref[...]
        r = lax.rsqrt(jnp.mean(x * x, axis=-1, keepdims=True) + EPS)
        xhat = x * r
        err = xhat * g_ref[...] - t_ref[...]
        loss_ref[...] += 0.5 * jnp.sum(jnp.mean(err * err, axis=-1, keepdims=True), axis=0, keepdims=True)
        dy = err / d
        dg_ref[...] += jnp.sum(dy * xhat, axis=0, keepdims=True)
        dxh = dy * g_ref[...]
        dh_ref[...] = r * (dxh - xhat * jnp.mean(dxh * xhat, axis=-1, keepdims=True))

    row = pl.BlockSpec((ROWS, d), lambda i: (i, 0))
    vec = pl.BlockSpec((1, d), lambda i: (0, 0))
    one = pl.BlockSpec((1, 1), lambda i: (0, 0))
    return pl.pallas_call(
        body, name="loss_head", grid=(s // ROWS,), in_specs=[row, vec, row], out_specs=[one, row, vec],
        out_shape=[jax.ShapeDtypeStruct((1, 1), F32), jax.ShapeDtypeStruct((s, d), F32),
                   jax.ShapeDtypeStruct((1, d), F32)],
        compiler_params=_params(("arbitrary",)),
    )(h, g.reshape(1, d), target)


def _t5_bucket(dist):
    max_exact = N_BUCKETS // 2
    dd = np.maximum(dist, 0)
    large = max_exact + (np.log(np.maximum(dd, 1) / max_exact) / np.log(REL_MAX_DIST / max_exact)
                         * (N_BUCKETS - max_exact)).astype(np.int32)
    large = np.minimum(large, N_BUCKETS - 1)
    return np.where(dd < max_exact, dd, large).astype(np.int32)


def _bucket_table():
    qq = np.arange(QBLK)[:, None]
    kk = np.arange(QBLK)[None, :]
    out = np.zeros((len(BRANCH_DIL), 2, QBLK, QBLK), np.int32)
    for b, dil in enumerate(BRANCH_DIL):
        out[b, 0] = _t5_bucket((qq - kk + QBLK) * dil)
        out[b, 1] = _t5_bucket((qq - kk) * dil)
    return out


BIAS_TILE = 2 * QBLK


def _bias_build(rel_bias):
    idx = jnp.asarray(_bucket_table())

    def body(idx_ref, rb_ref, o_ref):
        ch = pl.program_id(1)
        row = lax.broadcasted_iota(jnp.int32, (QBLK, QBLK), 0)
        col = lax.broadcasted_iota(jnp.int32, (QBLK, QBLK), 1)
        for part in range(2):
            ids = idx_ref[0, 1 - part]
            valid = (col <= row) if part == 0 else (col >= row)
            for h in range(2):
                acc = jnp.zeros((QBLK, QBLK), F32)
                for b in range(N_BUCKETS):
                    acc = jnp.where(ids == b, rb_ref[b, 2 * ch + h], acc)
                o_ref[0, 0, QBLK * h:QBLK * (h + 1), QBLK * part:QBLK * (part + 1)] = jnp.where(valid, acc, NEG_INF)

    return pl.pallas_call(
        body, name="attn_bias_build", grid=(len(BRANCH_DIL), N_HEADS // 2),
        in_specs=[pl.BlockSpec((1, 2, QBLK, QBLK), lambda b, c: (b, 0, 0, 0)),
                  pl.BlockSpec(memory_space=pltpu.SMEM)],
        out_specs=pl.BlockSpec((1, 1, BIAS_TILE, BIAS_TILE), lambda b, c: (b, c, 0, 0)),
        out_shape=jax.ShapeDtypeStruct((len(BRANCH_DIL), N_HEADS // 2, BIAS_TILE, BIAS_TILE), F32),
        compiler_params=_params(("parallel", "parallel")),
    )(idx, rel_bias)


def _bias_reduce(dbias):
    idx = jnp.asarray(_bucket_table())
    nb = len(BRANCH_DIL)

    def body(idx_ref, d_ref, o_ref):
        def per_bucket(b, carry):
            for h in range(N_HEADS):
                tot = jnp.zeros((), F32)
                for br in range(nb):
                    for part in range(2):
                        tile = d_ref[br, h // 2, QBLK * (h % 2):QBLK * (h % 2 + 1), QBLK * part:QBLK * (part + 1)]
                        tot = tot + jnp.sum(jnp.where(idx_ref[br, 1 - part] == b, tile, 0.0))
                o_ref[b, h] = tot
            return carry

        lax.fori_loop(0, N_BUCKETS, per_bucket, 0)

    return pl.pallas_call(
        body, name="attn_bias_reduce",
        in_specs=[pl.BlockSpec(memory_space=pltpu.VMEM), pl.BlockSpec(memory_space=pltpu.VMEM)],
        out_specs=pl.BlockSpec(memory_space=pltpu.SMEM),
        out_shape=jax.ShapeDtypeStruct((N_BUCKETS, N_HEADS), F32),
        compiler_params=pltpu.CompilerParams(vmem_limit_bytes=VMEM_LIMIT_BYTES),
    )(idx, dbias)


def _band_masks(c):
    row = lax.broadcasted_iota(jnp.int32, (QBLK, QBLK), 0)
    col = lax.broadcasted_iota(jnp.int32, (QBLK, QBLK), 1)
    mask_cur = col <= row
    mask_prev = jnp.logical_and(col >= row, c > 0)
    return mask_prev, mask_cur


def _attn_specs(dil):
    blk = (QBLK, ATTN_W)
    q = pl.BlockSpec(blk, lambda r, c: (c, 3 * r))
    kp = pl.BlockSpec(blk, lambda r, c: (jnp.maximum(c - 1, 0), 3 * r + 1))
    kc = pl.BlockSpec(blk, lambda r, c: (c, 3 * r + 1))
    vp = pl.BlockSpec(blk, lambda r, c: (jnp.maximum(c - 1, 0), 3 * r + 2))
    vc = pl.BlockSpec(blk, lambda r, c: (c, 3 * r + 2))
    return [q, kp, kc, vp, vc]


def _attn_fwd_branch(qkv, bias, state, *, branch, last):
    dil = BRANCH_DIL[branch]
    s = qkv.shape[0]
    n = s // dil
    nblk = n // QBLK
    first = state is None

    def body(*refs):
        q_ref, kp_ref, kc_ref, vp_ref, vc_ref, b_ref = refs[:6]
        if first:
            outs = refs[6:]
        else:
            acc_ref, m_ref, l_ref = refs[6:9]
            outs = refs[9:]
        mask_prev, mask_cur = _band_masks(pl.program_id(1))
        for h in range(N_HEADS):
            sl = slice(HEAD_DIM * h, HEAD_DIM * (h + 1))
            qh = q_ref[:, sl]
            s_c = _dot(qh, kc_ref[:, sl], NT) * ATTN_SCALE + b_ref[0, 1, h]
            s_p = _dot(qh, kp_ref[:, sl], NT) * ATTN_SCALE + b_ref[0, 0, h]
            s_c = jnp.where(mask_cur, s_c, NEG_INF)
            s_p = jnp.where(mask_prev, s_p, NEG_INF)
            m_blk = jnp.maximum(jnp.max(s_c, axis=-1, keepdims=True), jnp.max(s_p, axis=-1, keepdims=True))
            if first:
                m_new = m_blk
            else:
                m_old = m_ref[:, sl][:, :1]
                m_new = jnp.maximum(m_old, m_blk)
            p_c = jnp.exp(s_c - m_new)
            p_p = jnp.exp(s_p - m_new)
            l_new = jnp.sum(p_c, axis=-1, keepdims=True) + jnp.sum(p_p, axis=-1, keepdims=True)
            acc = (_dot(p_c.astype(MXU_DTYPE), vc_ref[:, sl], NN)
                   + _dot(p_p.astype(MXU_DTYPE), vp_ref[:, sl], NN))
            if not first:
                alpha = jnp.exp(m_old - m_new)
                l_new = l_new + alpha * l_ref[:, sl][:, :1]
                acc = acc + alpha * acc_ref[:, sl]
            if last:
                outs[0][:, sl] = acc / l_new
                outs[1][:, sl] = jnp.broadcast_to(m_new + jnp.log(l_new), (QBLK, HEAD_DIM))
            else:
                outs[0][:, sl] = acc
                outs[1][:, sl] = jnp.broadcast_to(m_new, (QBLK, HEAD_DIM))
                outs[2][:, sl] = jnp.broadcast_to(l_new, (QBLK, HEAD_DIM))

    st_spec = pl.BlockSpec((QBLK, ATTN_W), lambda r, c: (c, r))
    in_specs = _attn_specs(dil) + [pl.BlockSpec((1, 2, N_HEADS, QBLK, QBLK), lambda r, c: (branch, 0, 0, 0, 0))]
    qv = qkv.reshape(n, dil * 3 * ATTN_W)
    args = [qv] * 5 + [bias]
    if not first:
        in_specs += [st_spec] * 3
        args += [t.reshape(n, dil * ATTN_W) for t in state]
    n_out = 2 if last else 3
    outs = pl.pallas_call(
        body, name=f"attn_fwd_b{branch}", grid=(dil, nblk), in_specs=in_specs,
        out_specs=[st_spec] * n_out,
        out_shape=[jax.ShapeDtypeStruct((n, dil * ATTN_W), F32)] * n_out,
        compiler_params=_params(("parallel", "parallel")),
    )(*args)
    return tuple(t.reshape(s, ATTN_W) for t in outs)


def _attn_fwd(qkv, bias):
    state = None
    for b in range(len(BRANCH_DIL)):
        state = _attn_fwd_branch(qkv, bias, state, branch=b, last=(b == len(BRANCH_DIL) - 1))
    return state


def _attn_bwd_branch(qkv, bias, o, lse, do, *, branch):
    dil = BRANCH_DIL[branch]
    s = qkv.shape[0]
    n = s // dil
    nblk = n // QBLK

    def body(q_ref, kp_ref, kc_ref, vp_ref, vc_ref, b_ref, o_ref, l_ref, do_ref,
             dq_ref, dka_ref, dkb_ref, dva_ref, dvb_ref, db_ref):
        @pl.when(jnp.logical_and(pl.program_id(0) == 0, pl.program_id(1) == 0))
        def _():
            db_ref[...] = jnp.zeros_like(db_ref)

        mask_prev, mask_cur = _band_masks(pl.program_id(1))
        for h in range(N_HEADS):
            sl = slice(HEAD_DIM * h, HEAD_DIM * (h + 1))
            qh = q_ref[:, sl]
            doh = do_ref[:, sl]
            lh = l_ref[:, sl][:, :1]
            delta = jnp.sum(doh * o_ref[:, sl], axis=-1, keepdims=True)
            do_m = doh.astype(MXU_DTYPE)
            s_c = _dot(qh, kc_ref[:, sl], NT) * ATTN_SCALE + b_ref[0, 1, h]
            s_p = _dot(qh, kp_ref[:, sl], NT) * ATTN_SCALE + b_ref[0, 0, h]
            p_c = jnp.exp(jnp.where(mask_cur, s_c, NEG_INF) - lh)
            p_p = jnp.exp(jnp.where(mask_prev, s_p, NEG_INF) - lh)
            ds_c = p_c * (_dot(do_m, vc_ref[:, sl], NT) - delta)
            ds_p = p_p * (_dot(do_m, vp_ref[:, sl], NT) - delta)
            db_ref[0, 1, h] += ds_c
            db_ref[0, 0, h] += ds_p
            ds_c_m = ds_c.astype(MXU_DTYPE)
            ds_p_m = ds_p.astype(MXU_DTYPE)
            dq = _dot(ds_c_m, kc_ref[:, sl], NN) + _dot(ds_p_m, kp_ref[:, sl], NN)
            dq_ref[:, sl] = (dq * ATTN_SCALE).astype(dq_ref.dtype)
            dka_ref[:, sl] = (_dot(ds_c_m, qh, TN) * ATTN_SCALE).astype(dka_ref.dtype)
            dkb_ref[:, sl] = (_dot(ds_p_m, qh, TN) * ATTN_SCALE).astype(dkb_ref.dtype)
            dva_ref[:, sl] = _dot(p_c.astype(MXU_DTYPE), do_m, TN).astype(dva_ref.dtype)
            dvb_ref[:, sl] = _dot(p_p.astype(MXU_DTYPE), do_m, TN).astype(dvb_ref.dtype)

    st_spec = pl.BlockSpec((QBLK, ATTN_W), lambda r, c: (c, r))
    b_in = pl.BlockSpec((1, 2, N_HEADS, QBLK, QBLK), lambda r, c: (branch, 0, 0, 0, 0))
    b_out = pl.BlockSpec((1, 2, N_HEADS, QBLK, QBLK), lambda r, c: (0, 0, 0, 0, 0))
    qv = qkv.reshape(n, dil * 3 * ATTN_W)
    view = lambda t: t.reshape(n, dil * ATTN_W)
    outs = pl.pallas_call(
        body, name=f"attn_bwd_b{branch}", grid=(dil, nblk),
        in_specs=_attn_specs(dil) + [b_in, st_spec, st_spec, st_spec],
        out_specs=[st_spec] * 5 + [b_out],
        out_shape=[jax.ShapeDtypeStruct((n, dil * ATTN_W), MXU_DTYPE)] * 5
        + [jax.ShapeDtypeStruct((1, 2, N_HEADS, QBLK, QBLK), F32)],
        compiler_params=_params(("arbitrary", "arbitrary")),
    )(qv, qv, qv, qv, qv, bias, view(o), view(lse), view(do))
    return tuple(t.reshape(s, ATTN_W) for t in outs[:5]) + (outs[5],)


def _attn_bwd(qkv, bias, o, lse, do):
    s = qkv.shape[0]
    nb = s // QBLK
    parts = [_attn_bwd_branch(qkv, bias, o, lse, do, branch=b) for b in range(len(BRANCH_DIL))]
    dbias = jnp.concatenate([p[5] for p in parts], axis=0)

    def body(*refs):
        o_ref = refs[-1]
        i = pl.program_id(0)
        dq = jnp.zeros((QBLK, ATTN_W), F32)
        dk = jnp.zeros((QBLK, ATTN_W), F32)
        dv = jnp.zeros((QBLK, ATTN_W), F32)
        for b, dil in enumerate(BRANCH_DIL):
            dq_ref, dka_ref, dkb_ref, dva_ref, dvb_ref = refs[5 * b:5 * b + 5]
            inside = i + dil < nb
            dq = dq + dq_ref[...].astype(F32)
            dk = dk + dka_ref[...].astype(F32) + jnp.where(inside, dkb_ref[...].astype(F32), 0.0)
            dv = dv + dva_ref[...].astype(F32) + jnp.where(inside, dvb_ref[...].astype(F32), 0.0)
        o_ref[:, 0:ATTN_W] = dq.astype(o_ref.dtype)
        o_ref[:, ATTN_W:2 * ATTN_W] = dk.astype(o_ref.dtype)
        o_ref[:, 2 * ATTN_W:3 * ATTN_W] = dv.astype(o_ref.dtype)

    in_specs, args = [], []
    for b, dil in enumerate(BRANCH_DIL):
        here = pl.BlockSpec((QBLK, ATTN_W), lambda i: (i, 0))
        ahead = pl.BlockSpec((QBLK, ATTN_W), functools.partial(lambda i, d: (jnp.minimum(i + d, nb - 1), 0), d=dil))
        in_specs += [here, here, ahead, here, ahead]
        args += list(parts[b][:5])
    dqkv = pl.pallas_call(
        body, name="attn_bwd_sum", grid=(nb,), in_specs=in_specs,
        out_specs=pl.BlockSpec((QBLK, 3 * ATTN_W), lambda i: (i, 0)),
        out_shape=jax.ShapeDtypeStruct((s, 3 * ATTN_W), MXU_DTYPE),
        compiler_params=_params(("parallel",)),
    )(*args)
    return dqkv, dbias


ATTN_IO_DTYPE = F32
ABLK = 2048
N_CHUNK = ATTN_W // 128


def _rows(start, dil):
    if dil > 1:
        return pl.ds(start, QBLK, stride=dil)
    return pl.ds(pl.multiple_of(start, QBLK), QBLK)


def _low_head():
    return lax.broadcasted_iota(jnp.int32, (QBLK, 128), 1) < HEAD_DIM


def _head_split(t):
    low = _low_head()
    zero = jnp.zeros_like(t)
    return jnp.where(low, t, zero), jnp.where(low, zero, t)


def _tile_bias(b_ref, branch, first):
    bias = b_ref[branch]
    if first is None:
        return bias
    col = lax.broadcasted_iota(jnp.int32, (BIAS_TILE, BIAS_TILE), 1)
    return jnp.where(jnp.logical_and(first, col >= QBLK), NEG_INF, bias)


def _loop(n, fn):
    if n == 1:
        fn(jnp.int32(0), 0)
    elif n > 1:
        lax.fori_loop(0, n, fn, 0, unroll=2)


def _for_each_tile(tile, c):
    for branch, dil in enumerate(BRANCH_DIL):
        span = QBLK * dil

        def edge(r, carry, branch=branch, span=span):
            tile(branch, r, False, ABLK - span + r, c == 0)
            return carry

        def inner(t, carry, branch=branch, span=span, dil=dil):
            start = (1 + t // dil) * span + t % dil
            tile(branch, start, True, start - span, None)
            return carry

        _loop(dil, edge)
        _loop((ABLK // span - 1) * dil, inner)


def _attn_chunk_specs(nb):
    blk = (None, ABLK, 128)
    prev = lambda c: jnp.maximum(c - 1, 0)
    return [pl.BlockSpec(blk, lambda ch, c: (ch, c, 0)),
            pl.BlockSpec(blk, lambda ch, c: (N_CHUNK + ch, c, 0)),
            pl.BlockSpec(blk, lambda ch, c: (2 * N_CHUNK + ch, c, 0)),
            pl.BlockSpec(blk, lambda ch, c: (N_CHUNK + ch, prev(c), 0)),
            pl.BlockSpec(blk, lambda ch, c: (2 * N_CHUNK + ch, prev(c), 0)),
            pl.BlockSpec((len(BRANCH_DIL), None, BIAS_TILE, BIAS_TILE), lambda ch, c: (0, ch, 0, 0))]


def _in_proj(xn, w_in, layer):
    s, k = xn.shape
    tm = 512
    nch = O_SGU // 128

    def body(x_ref, w_ref, qkv_ref, zs_ref, us_ref):
        acc = _dot(x_ref[...].astype(MXU_DTYPE), w_ref[...].astype(MXU_DTYPE), NN)
        for j in range(nch):
            blk = acc[:, 128 * j:128 * (j + 1)]
            if j < N_CHUNK:
                blk = blk * ATTN_SCALE
            qkv_ref[j] = blk.astype(qkv_ref.dtype)
        zs_ref[...] = acc[:, O_SGU:O_SSM]
        us_ref[...] = acc[:, O_SSM:]

    n = w_in.shape[-1]
    return pl.pallas_call(
        body, name="in_proj", grid=(s // tm,),
        in_specs=[pl.BlockSpec((tm, k), lambda i: (i, 0)), pl.BlockSpec((None, k, n), lambda i: (layer, 0, 0))],
        out_specs=[pl.BlockSpec((nch, tm, 128), lambda i: (0, i, 0)),
                   pl.BlockSpec((tm, O_SSM - O_SGU), lambda i: (i, 0)), pl.BlockSpec((tm, n - O_SSM), lambda i: (i, 0))],
        out_shape=[jax.ShapeDtypeStruct((nch, s, 128), ATTN_IO_DTYPE),
                   jax.ShapeDtypeStruct((s, O_SSM - O_SGU), F32), jax.ShapeDtypeStruct((s, n - O_SSM), F32)],
        compiler_params=_params(("parallel",)),
    )(xn, w_in)


def _attn2_fwd(qkv_c, bias):
    s = qkv_c.shape[1]
    nb = s // ABLK
    last = len(BRANCH_DIL) - 1

    def body(q_ref, kc_ref, vc_ref, kp_ref, vp_ref, b_ref, o_ref, l_ref, acc_s, m_s, l_s):
        low = _low_head()
        e_st = jnp.concatenate(_head_split(jnp.ones((QBLK, 128), MXU_DTYPE)) * 2, axis=0)

        def tile(branch, start, prev_in_block, pstart, first):
            dil = BRANCH_DIL[branch]
            rq, rp = _rows(start, dil), _rows(pstart, dil)
            k_ref, v_ref = (kc_ref, vc_ref) if prev_in_block else (kp_ref, vp_ref)
            q_st = jnp.concatenate(_head_split(q_ref[rq, :].astype(MXU_DTYPE)), axis=0)
            k_st = jnp.concatenate([kc_ref[rq, :].astype(MXU_DTYPE), k_ref[rp, :].astype(MXU_DTYPE)], axis=0)
            v_st = jnp.concatenate(_head_split(vc_ref[rq, :].astype(MXU_DTYPE))
                                   + _head_split(v_ref[rp, :].astype(MXU_DTYPE)), axis=0)
            sc = _dot(q_st, k_st, NT) + _tile_bias(b_ref, branch, first)
            m_new = jnp.max(sc, axis=-1, keepdims=True)
            if branch > 0:
                m_old2 = m_s[rq, :]
                m_old = jnp.concatenate([m_old2[:, 0:1], m_old2[:, HEAD_DIM:HEAD_DIM + 1]], axis=0)
                m_new = jnp.maximum(m_old, m_new)
                alpha = jnp.exp(m_old - m_new)
            p = jnp.exp(sc - m_new).astype(MXU_DTYPE)
            lhs = jnp.concatenate([p[:QBLK, :QBLK], p[QBLK:, :QBLK], p[:QBLK, QBLK:], p[QBLK:, QBLK:]], axis=1)
            acc2 = _dot(lhs, v_st, NN)
            sum2 = _dot(lhs, e_st, NN)
            m2 = jnp.where(low, m_new[:QBLK], m_new[QBLK:])
            if branch > 0:
                a2 = jnp.where(low, alpha[:QBLK], alpha[QBLK:])
                acc2 = acc2 + a2 * acc_s[rq, :]
                sum2 = sum2 + a2 * l_s[rq, :]
            if branch == last:
                o_ref[rq, :] = acc2 / sum2
                l_ref[rq, :] = m2 + jnp.log(sum2)
            else:
                acc_s[rq, :] = acc2
                m_s[rq, :] = m2
                l_s[rq, :] = sum2

        _for_each_tile(tile, pl.program_id(1))

    out_spec = pl.BlockSpec((None, ABLK, 128), lambda ch, c: (ch, c, 0))
    return pl.pallas_call(
        body, name="attn_fwd", grid=(N_CHUNK, nb), in_specs=_attn_chunk_specs(nb),
        out_specs=[out_spec, out_spec],
        out_shape=[jax.ShapeDtypeStruct((N_CHUNK, s, 128), F32)] * 2,
        scratch_shapes=[pltpu.VMEM((ABLK, 128), F32)] * 3,
        compiler_params=_params(("parallel", "arbitrary")),
    )(qkv_c, qkv_c, qkv_c, qkv_c, qkv_c, bias)


def _attn2_bwd(qkv_c, bias, lse_c, delta_c, do_c):
    s = qkv_c.shape[1]
    nb = s // ABLK
    nbr = len(BRANCH_DIL)

    def body(q_ref, kc_ref, vc_ref, kp_ref, vp_ref, b_ref, l_ref, dl_ref, do_ref,
             dq_ref, dk_ref, dv_ref, *rest):
        ek_refs, ev_refs, db_ref = rest[:nbr], rest[nbr:2 * nbr], rest[2 * nbr]
        c = pl.program_id(1)

        @pl.when(c == 0)
        def _():
            db_ref[...] = jnp.zeros_like(db_ref)

        for r in (dq_ref, dk_ref, dv_ref) + tuple(ek_refs) + tuple(ev_refs):
            r[...] = jnp.zeros_like(r)

        def tile(branch, start, prev_in_block, pstart, first):
            dil = BRANCH_DIL[branch]
            rq, rp = _rows(start, dil), _rows(pstart, dil)
            k_ref, v_ref = (kc_ref, vc_ref) if prev_in_block else (kp_ref, vp_ref)
            kc2 = kc_ref[rq, :].astype(MXU_DTYPE)
            kp2 = k_ref[rp, :].astype(MXU_DTYPE)
            q_st = jnp.concatenate(_head_split(q_ref[rq, :].astype(MXU_DTYPE)), axis=0)
            do_st = jnp.concatenate(_head_split(do_ref[rq, :].astype(MXU_DTYPE)), axis=0)
            k_st = jnp.concatenate([kc2, kp2], axis=0)
            v_st = jnp.concatenate([vc_ref[rq, :].astype(MXU_DTYPE), v_ref[rp, :].astype(MXU_DTYPE)], axis=0)
            kh_st = jnp.concatenate(_head_split(kc2) + _head_split(kp2), axis=0)
            lse2 = l_ref[rq, :]
            del2 = dl_ref[rq, :]
            lse_st = jnp.concatenate([lse2[:, 0:1], lse2[:, HEAD_DIM:HEAD_DIM + 1]], axis=0)
            del_st = jnp.concatenate([del2[:, 0:1], del2[:, HEAD_DIM:HEAD_DIM + 1]], axis=0)
            p = jnp.exp(_dot(q_st, k_st, NT) + _tile_bias(b_ref, branch, first) - lse_st)
            ds = p * (_dot(do_st, v_st, NT) - del_st)
            db_ref[branch] += ds
            ds = ds.astype(MXU_DTYPE)
            p = p.astype(MXU_DTYPE)
            lhs = jnp.concatenate([ds[:QBLK, :QBLK], ds[QBLK:, :QBLK], ds[:QBLK, QBLK:], ds[QBLK:, QBLK:]], axis=1)
            dk_st = _dot(ds, q_st, TN)
            dv_st = _dot(p, do_st, TN)
            dq_ref[rq, :] += _dot(lhs, kh_st, NN)
            dk_ref[rq, :] += dk_st[:QBLK]
            dv_ref[rq, :] += dv_st[:QBLK]
            if prev_in_block:
                dk_ref[rp, :] += dk_st[QBLK:]
                dv_ref[rp, :] += dv_st[QBLK:]
            else:
                ek_refs[branch][rq, :] = dk_st[QBLK:]
                ev_refs[branch][rq, :] = dv_st[QBLK:]

        _for_each_tile(tile, c)

    blk = pl.BlockSpec((None, ABLK, 128), lambda ch, c: (ch, c, 0))
    outs = pl.pallas_call(
        body, name="attn_bwd", grid=(N_CHUNK, nb), in_specs=_attn_chunk_specs(nb) + [blk, blk, blk],
        out_specs=[blk] * (3 + 2 * nbr) + [pl.BlockSpec((nbr, None, BIAS_TILE, BIAS_TILE), lambda ch, c: (0, ch, 0, 0))],
        out_shape=[jax.ShapeDtypeStruct((N_CHUNK, s, 128), F32)] * (3 + 2 * nbr)
        + [jax.ShapeDtypeStruct((nbr, N_HEADS // 2, BIAS_TILE, BIAS_TILE), F32)],
        compiler_params=_params(("arbitrary", "arbitrary")),
    )(qkv_c, qkv_c, qkv_c, qkv_c, qkv_c, bias, lse_c, delta_c, do_c)
    return outs[0], outs[1], outs[2], outs[3:3 + nbr], outs[3 + nbr:3 + 2 * nbr], outs[3 + 2 * nbr]


def _attn2_bwd_sum(dq, dk, dv, ek, ev, dzs, dus):
    s = dq.shape[1]
    nrb = s // QBLK
    per_blk = ABLK // QBLK
    nbr = len(BRANCH_DIL)

    def body(*refs):
        dq_ref, dk_ref, dv_ref = refs[:3]
        ek_refs, ev_refs = refs[3:3 + nbr], refs[3 + nbr:3 + 2 * nbr]
        dzs_ref, dus_ref, o_ref = refs[3 + 2 * nbr:]
        i = pl.program_id(0)
        dkt, dvt = dk_ref[...], dv_ref[...]
        for b, dil in enumerate(BRANCH_DIL):
            j = i + dil
            ok = jnp.logical_and(j < nrb, j % per_blk < dil)
            dkt = dkt + jnp.where(ok, ek_refs[b][...], 0.0)
            dvt = dvt + jnp.where(ok, ev_refs[b][...], 0.0)
        for ch in range(N_CHUNK):
            o_ref[:, 128 * ch:128 * (ch + 1)] = (dq_ref[ch] * ATTN_SCALE).astype(o_ref.dtype)
            o_ref[:, ATTN_W + 128 * ch:ATTN_W + 128 * (ch + 1)] = dkt[ch].astype(o_ref.dtype)
            o_ref[:, 2 * ATTN_W + 128 * ch:2 * ATTN_W + 128 * (ch + 1)] = dvt[ch].astype(o_ref.dtype)
        o_ref[:, O_SGU:O_SSM] = dzs_ref[...].astype(o_ref.dtype)
        o_ref[:, O_SSM:] = dus_ref[...].astype(o_ref.dtype)

    here = pl.BlockSpec((N_CHUNK, QBLK, 128), lambda i: (0, i, 0))
    edge_specs = [pl.BlockSpec((N_CHUNK, QBLK, 128),
                               functools.partial(lambda i, d: (0, jnp.minimum(i + d, nrb - 1), 0), d=dil))
                  for dil in BRANCH_DIL]
    return pl.pallas_call(
        body, name="attn_bwd_sum", grid=(nrb,),
        in_specs=[here, here, here] + edge_specs + edge_specs
        + [pl.BlockSpec((QBLK, 2 * SGU_W), lambda i: (i, 0)), pl.BlockSpec((QBLK, SSM_W), lambda i: (i, 0))],
        out_specs=pl.BlockSpec((QBLK, O_SSM + SSM_W), lambda i: (i, 0)),
        out_shape=jax.ShapeDtypeStruct((s, O_SSM + SSM_W), MXU_DTYPE),
        compiler_params=_params(("parallel",)),
    )(dq, dk, dv, *ek, *ev, dzs, dus)


SGU_ROWS = 512


def _sgu_norm(v_g):
    mu = jnp.mean(v_g, axis=-1, keepdims=True)
    cen = v_g - mu
    var = jnp.mean(cen * cen, axis=-1, keepdims=True)
    rstd = lax.rsqrt(var + EPS)
    return cen * rstd, rstd


def _sgu_fwd(zs, ln_g, ln_b, w_mask, b_t):
    s = zs.shape[0]
    nch = SGU_ROWS // SGU_CHUNK

    def body(z_ref, g_ref, b_ref, w_ref, bt_ref, o_ref):
        gz = _gelu(z_ref[...])
        for g in range(SGU_G):
            sl = slice(SGU_GW * g, SGU_GW * (g + 1))
            u_g = gz[:, sl]
            xhat, _ = _sgu_norm(gz[:, SGU_W + SGU_GW * g:SGU_W + SGU_GW * (g + 1)])
            vn = (xhat * g_ref[:, sl] + b_ref[:, sl]).astype(MXU_DTYPE)
            wg = w_ref[g].astype(MXU_DTYPE)
            for ci in range(nch):
                rs = slice(SGU_CHUNK * ci, SGU_CHUNK * (ci + 1))
                mixed = _dot(wg, vn[rs], NN) + bt_ref[:, g:g + 1]
                o_ref[rs, sl] = u_g[rs] * mixed

    full = lambda shape: pl.BlockSpec(shape, lambda i: tuple(0 for _ in shape))
    return pl.pallas_call(
        body, name="sgu_fwd", grid=(s // SGU_ROWS,),
        in_specs=[pl.BlockSpec((SGU_ROWS, 2 * SGU_W), lambda i: (i, 0)), full((1, SGU_W)), full((1, SGU_W)),
                  full((SGU_G, SGU_CHUNK, SGU_CHUNK)), full((SGU_CHUNK, SGU_G))],
        out_specs=pl.BlockSpec((SGU_ROWS, SGU_W), lambda i: (i, 0)),
        out_shape=jax.ShapeDtypeStruct((s, SGU_W), F32),
        compiler_params=_params(("parallel",)),
    )(zs, ln_g.reshape(1, SGU_W), ln_b.reshape(1, SGU_W), w_mask, b_t)


def _sgu_bwd(zs, ln_g, ln_b, w_mask, b_t, dy):
    s = zs.shape[0]
    nch = SGU_ROWS // SGU_CHUNK

    def body(z_ref, g_ref, b_ref, w_ref, bt_ref, dy_ref, dz_ref, dg_ref, dbb_ref, dw_ref, dbt_ref):
        @pl.when(pl.program_id(0) == 0)
        def _():
            dg_ref[...] = jnp.zeros_like(dg_ref)
            dbb_ref[...] = jnp.zeros_like(dbb_ref)
            dw_ref[...] = jnp.zeros_like(dw_ref)
            dbt_ref[...] = jnp.zeros_like(dbt_ref)

        z = z_ref[...]
        gz, dgelu = _gelu_pair(z)
        dy = dy_ref[...]
        for g in range(SGU_G):
            sl = slice(SGU_GW * g, SGU_GW * (g + 1))
            sv = slice(SGU_W + SGU_GW * g, SGU_W + SGU_GW * (g + 1))
            u_g = gz[:, sl]
            xhat, rstd = _sgu_norm(gz[:, sv])
            gain = g_ref[:, sl]
            vn = (xhat * gain + b_ref[:, sl]).astype(MXU_DTYPE)
            wg = w_ref[g].astype(MXU_DTYPE)
            dy_g = dy[:, sl]
            dvn_parts = []
            for ci in range(nch):
                rs = slice(SGU_CHUNK * ci, SGU_CHUNK * (ci + 1))
                mixed = _dot(wg, vn[rs], NN) + bt_ref[:, g:g + 1]
                dz_ref[rs, sl] = (dy_g[rs] * mixed * dgelu[rs, sl]).astype(dz_ref.dtype)
                dmixed = dy_g[rs] * u_g[rs]
                dm = dmixed.astype(MXU_DTYPE)
                dvn_parts.append(_dot(wg, dm, TN))
                dw_ref[g] += _dot(dm, vn[rs], NT)
                dbt_ref[:, g:g + 1] += jnp.sum(dmixed, axis=-1, keepdims=True)
            dvn = jnp.concatenate(dvn_parts, axis=0)
            dg_ref[:, sl] += jnp.sum(dvn * xhat, axis=0, keepdims=True)
            dbb_ref[:, sl] += jnp.sum(dvn, axis=0, keepdims=True)
            dxh = dvn * gain
            dv = rstd * (dxh - jnp.mean(dxh, axis=-1, keepdims=True)
                         - xhat * jnp.mean(dxh * xhat, axis=-1, keepdims=True))
            dz_ref[:, sv] = (dv * dgelu[:, sv]).astype(dz_ref.dtype)

    full = lambda shape: pl.BlockSpec(shape, lambda i: tuple(0 for _ in shape))
    return pl.pallas_call(
        body, name="sgu_bwd", grid=(s // SGU_ROWS,),
        in_specs=[pl.BlockSpec((SGU_ROWS, 2 * SGU_W), lambda i: (i, 0)), full((1, SGU_W)), full((1, SGU_W)),
                  full((SGU_G, SGU_CHUNK, SGU_CHUNK)), full((SGU_CHUNK, SGU_G)),
                  pl.BlockSpec((SGU_ROWS, SGU_W), lambda i: (i, 0))],
        out_specs=[pl.BlockSpec((SGU_ROWS, 2 * SGU_W), lambda i: (i, 0)), full((1, SGU_W)), full((1, SGU_W)),
                   full((SGU_G, SGU_CHUNK, SGU_CHUNK)), full((SGU_CHUNK, SGU_G))],
        out_shape=[jax.ShapeDtypeStruct((s, 2 * SGU_W), MXU_DTYPE), jax.ShapeDtypeStruct((1, SGU_W), F32),
                   jax.ShapeDtypeStruct((1, SGU_W), F32), jax.ShapeDtypeStruct((SGU_G, SGU_CHUNK, SGU_CHUNK), F32),
                   jax.ShapeDtypeStruct((SGU_CHUNK, SGU_G), F32)],
        compiler_params=_params(("arbitrary",)),
    )(zs, ln_g.reshape(1, SGU_W), ln_b.reshape(1, SGU_W), w_mask, b_t, dy)


def _ssm_discretize(a_re, a_im, log_dt, b_re, b_im):
    dt = jnp.exp(log_dt)[:, None]
    mag = jnp.exp(a_re * dt)
    ab_re = mag * jnp.cos(a_im * dt)
    ab_im = mag * jnp.sin(a_im * dt)
    den = a_re * a_re + a_im * a_im
    f_re = ((ab_re - 1.0) * a_re + ab_im * a_im) / den
    f_im = (ab_im * a_re - (ab_re - 1.0) * a_im) / den
    bb_re = f_re[:, :, None] * b_re - f_im[:, :, None] * b_im
    bb_im = f_re[:, :, None] * b_im + f_im[:, :, None] * b_re
    return ab_re, ab_im, bb_re, bb_im


def _ssm_operands(a_re, a_im, log_dt, b_re, b_im, c_re, c_im):
    ab_re, ab_im, bb_re, bb_im = _ssm_discretize(a_re, a_im, log_dt, b_re, b_im)
    eye = jnp.eye(SSM_G, dtype=F32)
    b_blk = jnp.einsum("pgnc,gh->gcphn", jnp.stack([bb_re, bb_im]), eye).reshape(SSM_W, 2 * NSTATE)
    c_mat = jnp.einsum("pgcn,gh->pgnhc", jnp.stack([c_re, -c_im]), eye).reshape(2 * NSTATE, SSM_W)
    a_row = jnp.stack([ab_re.reshape(NSTATE), ab_im.reshape(NSTATE)])
    p_re, p_im = a_row[0:1], a_row[1:2]
    while p_re.shape[0] < SSM_TSEG:
        l_re, l_im = p_re[-1:], p_im[-1:]
        p_re, p_im = (jnp.concatenate([p_re, p_re * l_re - p_im * l_im]),
                      jnp.concatenate([p_im, p_re * l_im + p_im * l_re]))
    p_tab = jnp.stack([p_re, p_im])
    return b_blk.astype(MXU_DTYPE), c_mat.astype(MXU_DTYPE), a_row, p_tab


def _lane_chunks():
    return [(lo, lo + SSM_LANE_CHUNK) for lo in range(0, NSTATE, SSM_LANE_CHUNK)]


def _seg_rows(j):
    return pl.ds(pl.multiple_of(j * SSM_NSEG, SSM_NSEG), SSM_NSEG)


def _to_segments(t):
    s, w = t.shape
    return t.reshape(s // SSM_TB, SSM_NSEG, SSM_TSEG, w).transpose(0, 2, 1, 3).reshape(s, w)


def _from_segments(t):
    s, w = t.shape
    return t.reshape(s // SSM_TB, SSM_TSEG, SSM_NSEG, w).transpose(0, 2, 1, 3).reshape(s, w)


def _ssm_local_scan(buf, a_ref, *, reverse):
    ends_re, ends_im = [], []
    for lo, hi in _lane_chunks():
        are = jnp.broadcast_to(a_ref[0:1, lo:hi], (SSM_NSEG, hi - lo))
        aim = jnp.broadcast_to(a_ref[1:2, lo:hi], (SSM_NSEG, hi - lo))
        if reverse:
            aim = -aim

        def step(jj, carry, lo=lo, hi=hi, are=are, aim=aim):
            xr, xi = carry
            j = (SSM_TSEG - 1 - jj) if reverse else jj
            tr = buf[_seg_rows(j), lo:hi]
            ti = buf[_seg_rows(j), NSTATE + lo:NSTATE + hi]
            nr = are * xr - aim * xi + tr
            ni = are * xi + aim * xr + ti
            buf[_seg_rows(j), lo:hi] = nr
            buf[_seg_rows(j), NSTATE + lo:NSTATE + hi] = ni
            return nr, ni

        zero = jnp.zeros((SSM_NSEG, hi - lo), F32)
        xr, xi = lax.fori_loop(0, SSM_TSEG, step, (zero, zero), unroll=4)
        ends_re.append(xr)
        ends_im.append(xi)
    return jnp.concatenate(ends_re, axis=1), jnp.concatenate(ends_im, axis=1)


def _ssm_entry_states(ends_re, ends_im, carry_ref, p_ref, entry_ref, *, reverse):
    at_re = p_ref[0, SSM_TSEG - 1:SSM_TSEG, :]
    at_im = p_ref[1, SSM_TSEG - 1:SSM_TSEG, :]
    if reverse:
        at_im = -at_im
    cur_re = carry_ref[0:1, 0:NSTATE]
    cur_im = carry_ref[0:1, NSTATE:2 * NSTATE]
    order = range(SSM_NSEG - 1, -1, -1) if reverse else range(SSM_NSEG)
    for i in order:
        entry_ref[0, i:i + 1, 0:NSTATE] = cur_re
        entry_ref[0, i:i + 1, NSTATE:2 * NSTATE] = cur_im
        nxt_re = ends_re[i:i + 1] + at_re * cur_re - at_im * cur_im
        nxt_im = ends_im[i:i + 1] + at_re * cur_im + at_im * cur_re
        cur_re, cur_im = nxt_re, nxt_im
    carry_ref[0:1, 0:NSTATE] = cur_re
    carry_ref[0:1, NSTATE:2 * NSTATE] = cur_im


def _ssm_fixup(buf, p_ref, entry_ref, *, reverse):
    for lo, hi in _lane_chunks():
        e_re = entry_ref[0, :, lo:hi]
        e_im = entry_ref[0, :, NSTATE + lo:NSTATE + hi]

        def step(j, carry, lo=lo, hi=hi, e_re=e_re, e_im=e_im):
            jp = (SSM_TSEG - 1 - j) if reverse else j
            pr = p_ref[0, pl.ds(jp, 1), lo:hi]
            pi = p_ref[1, pl.ds(jp, 1), lo:hi]
            if reverse:
                pi = -pi
            buf[_seg_rows(j), lo:hi] = buf[_seg_rows(j), lo:hi] + pr * e_re - pi * e_im
            buf[_seg_rows(j), NSTATE + lo:NSTATE + hi] = (buf[_seg_rows(j), NSTATE + lo:NSTATE + hi]
                                                           + pr * e_im + pi * e_re)
            return carry

        lax.fori_loop(0, SSM_TSEG, step, 0, unroll=4)


def _ssm_fwd(u, ops, d_skip, glu_w, glu_b):
    b_blk, c_mat, a_row, p_tab = ops
    s = u.shape[0]
    nblk = s // SSM_TB

    def body(u_ref, bb_ref, cm_ref, a_ref, p_ref, d_ref, gw_ref, gb_ref, y_ref, entry_ref, xbuf, carry):
        @pl.when(pl.program_id(0) == 0)
        def _():
            carry[...] = jnp.zeros_like(carry)

        uu = u_ref[...]
        xbuf[...] = _dotf(uu, bb_ref[...], NN)
        ends_re, ends_im = _ssm_local_scan(xbuf, a_ref, reverse=False)
        _ssm_entry_states(ends_re, ends_im, carry, p_ref, entry_ref, reverse=False)
        _ssm_fixup(xbuf, p_ref, entry_ref, reverse=False)
        y = _dotf(xbuf[...],cm_ref[...], NN) + d_ref[...] * uu
        y2 = _gelu(y)
        gate = jax.nn.sigmoid(_dot(y2.astype(MXU_DTYPE), gw_ref[...].astype(MXU_DTYPE), NN) + gb_ref[...])
        y_ref[...] = y2 * gate

    full = lambda shape: pl.BlockSpec(shape, lambda i: tuple(0 for _ in shape))
    y_seg, entry = pl.pallas_call(
        body, name="ssm_fwd", grid=(nblk,),
        in_specs=[pl.BlockSpec((SSM_TB, SSM_W), lambda i: (i, 0)), full(b_blk.shape), full(c_mat.shape),
                  full(a_row.shape), full(p_tab.shape), full((1, SSM_W)), full((SSM_W, SSM_W)), full((1, SSM_W))],
        out_specs=[pl.BlockSpec((SSM_TB, SSM_W), lambda i: (i, 0)),
                   pl.BlockSpec((1, SSM_NSEG, 2 * NSTATE), lambda i: (i, 0, 0))],
        out_shape=[jax.ShapeDtypeStruct((s, SSM_W), F32), jax.ShapeDtypeStruct((nblk, SSM_NSEG, 2 * NSTATE), F32)],
        scratch_shapes=[pltpu.VMEM((SSM_TB, 2 * NSTATE), F32), pltpu.VMEM((SSM_NSEG, 2 * NSTATE), F32)],
        compiler_params=_params(("arbitrary",)),
    )(_to_segments(u), b_blk, c_mat, a_row, p_tab, d_skip.reshape(1, SSM_W), glu_w, glu_b.reshape(1, SSM_W))
    return _from_segments(y_seg), entry


def _ssm_bwd(u, entry, ops, d_skip, glu_w, glu_b, dout):
    b_blk, c_mat, a_row, p_tab = ops
    s = u.shape[0]
    nblk = s // SSM_TB

    def body(u_ref, en_ref, bb_ref, cm_ref, a_ref, p_ref, d_ref, gw_ref, gb_ref, do_ref,
             du_ref, dbb_ref, dcm_ref, da_ref, dd_ref, dgw_ref, dgb_ref, xbuf, gbuf, gcarry, gentry):
        @pl.when(pl.program_id(0) == 0)
        def _():
            gcarry[...] = jnp.zeros_like(gcarry)
            for r in (dbb_ref, dcm_ref, da_ref, dd_ref, dgw_ref, dgb_ref):
                r[...] = jnp.zeros_like(r)

        uu = u_ref[...]
        xbuf[...] = _dotf(uu, bb_ref[...], NN)
        _ssm_local_scan(xbuf, a_ref, reverse=False)
        _ssm_fixup(xbuf, p_ref, en_ref, reverse=False)
        y = _dotf(xbuf[...],cm_ref[...], NN) + d_ref[...] * uu
        y2, dgelu = _gelu_pair(y)
        y2m = y2.astype(MXU_DTYPE)
        gwm = gw_ref[...].astype(MXU_DTYPE)
        gate = jax.nn.sigmoid(_dot(y2m, gwm, NN) + gb_ref[...])
        dout = do_ref[...]
        dpre = dout * y2 * gate * (1.0 - gate)
        dprem = dpre.astype(MXU_DTYPE)
        dy2 = dout * gate + _dot(dprem, gwm, NT)
        dgw_ref[...] += _dot(y2m, dprem, TN)
        dgb_ref[...] += jnp.sum(dpre, axis=0, keepdims=True)
        dy = dy2 * dgelu
        dd_ref[...] += jnp.sum(dy * uu, axis=0, keepdims=True)
        dcm_ref[...] += _dotf(xbuf[...],dy, TN)
        gbuf[...] = _dotf(dy, cm_ref[...], NT)
        gs_re, gs_im = _ssm_local_scan(gbuf, a_ref, reverse=True)
        _ssm_entry_states(gs_re, gs_im, gcarry, p_ref, gentry, reverse=True)
        _ssm_fixup(gbuf, p_ref, gentry, reverse=True)
        du_ref[...] = (_dotf(gbuf[...], bb_ref[...], NT) + d_ref[...] * dy).astype(du_ref.dtype)
        dbb_ref[...] += _dotf(uu, gbuf[...], TN)
        for lo, hi in _lane_chunks():
            def step(j, carry, lo=lo, hi=hi):
                acc_re, acc_im = carry
                g_re = gbuf[_seg_rows(j), lo:hi]
                g_im = gbuf[_seg_rows(j), NSTATE + lo:NSTATE + hi]
                x_re = xbuf[_seg_rows(j - 1), lo:hi]
                x_im = xbuf[_seg_rows(j - 1), NSTATE + lo:NSTATE + hi]
                return acc_re + g_re * x_re + g_im * x_im, acc_im + g_im * x_re - g_re * x_im

            g0_re = gbuf[_seg_rows(0), lo:hi]
            g0_im = gbuf[_seg_rows(0), NSTATE + lo:NSTATE + hi]
            e_re = en_ref[0, :, lo:hi]
            e_im = en_ref[0, :, NSTATE + lo:NSTATE + hi]
            init = (g0_re * e_re + g0_im * e_im, g0_im * e_re - g0_re * e_im)
            acc_re, acc_im = lax.fori_loop(1, SSM_TSEG, step, init, unroll=4)
            da_ref[0:1, lo:hi] += jnp.sum(acc_re, axis=0, keepdims=True)
            da_ref[1:2, lo:hi] += jnp.sum(acc_im, axis=0, keepdims=True)

    full = lambda shape: pl.BlockSpec(shape, lambda i: tuple(0 for _ in shape))
    rev = pl.BlockSpec((SSM_TB, SSM_W), lambda i: (nblk - 1 - i, 0))
    outs = pl.pallas_call(
        body, name="ssm_bwd", grid=(nblk,),
        in_specs=[rev, pl.BlockSpec((1, SSM_NSEG, 2 * NSTATE), lambda i: (nblk - 1 - i, 0, 0)),
                  full(b_blk.shape), full(c_mat.shape), full(a_row.shape), full(p_tab.shape),
                  full((1, SSM_W)), full((SSM_W, SSM_W)), full((1, SSM_W)), rev],
        out_specs=[rev, full(b_blk.shape), full(c_mat.shape), full(a_row.shape), full((1, SSM_W)),
                   full((SSM_W, SSM_W)), full((1, SSM_W))],
        out_shape=[jax.ShapeDtypeStruct((s, SSM_W), MXU_DTYPE), jax.ShapeDtypeStruct(b_blk.shape, F32),
                   jax.ShapeDtypeStruct(c_mat.shape, F32), jax.ShapeDtypeStruct(a_row.shape, F32),
                   jax.ShapeDtypeStruct((1, SSM_W), F32), jax.ShapeDtypeStruct((SSM_W, SSM_W), F32),
                   jax.ShapeDtypeStruct((1, SSM_W), F32)],
        scratch_shapes=[pltpu.VMEM((SSM_TB, 2 * NSTATE), F32), pltpu.VMEM((SSM_TB, 2 * NSTATE), F32),
                        pltpu.VMEM((SSM_NSEG, 2 * NSTATE), F32), pltpu.VMEM((1, SSM_NSEG, 2 * NSTATE), F32)],
        compiler_params=_params(("arbitrary",)),
    )(_to_segments(u), entry, b_blk, c_mat, a_row, p_tab, d_skip.reshape(1, SSM_W), glu_w, glu_b.reshape(1, SSM_W),
      _to_segments(dout))
    return (_from_segments(outs[0]),) + tuple(outs[1:])


MIX_SEGS = ((0, ATTN_W), (ATTN_W, ATTN_W + SGU_W), (ATTN_W + SGU_W, D_MODEL))


def _chunks_to_rows(a_ref):
    return jnp.concatenate([a_ref[ch] for ch in range(N_CHUNK)], axis=1)


def _mix_fwd(y_attn_c, y_sgu, y_ssm, gain):
    s = y_sgu.shape[0]

    def body(a_ref, b_ref, c_ref, g_ref, o_ref):
        for x, (lo, hi) in zip((_chunks_to_rows(a_ref), b_ref[...], c_ref[...]), MIX_SEGS):
            r = lax.rsqrt(jnp.mean(x * x, axis=-1, keepdims=True) + EPS)
            o_ref[:, lo:hi] = (x * r * g_ref[:, lo:hi]).astype(o_ref.dtype)

    row = lambda w: pl.BlockSpec((ROWS, w), lambda i: (i, 0))
    return pl.pallas_call(
        body, name="mix_fwd", grid=(s // ROWS,),
        in_specs=[pl.BlockSpec((N_CHUNK, ROWS, 128), lambda i: (0, i, 0)), row(SGU_W), row(SSM_W),
                  pl.BlockSpec((1, D_MODEL), lambda i: (0, 0))],
        out_specs=row(D_MODEL), out_shape=jax.ShapeDtypeStruct((s, D_MODEL), MXU_DTYPE),
        compiler_params=_params(("parallel",)),
    )(y_attn_c, y_sgu, y_ssm, gain.reshape(1, D_MODEL))


def _mix_bwd(y_attn_c, y_sgu, y_ssm, gain, dmix):
    s = y_sgu.shape[0]

    def body(a_ref, b_ref, c_ref, g_ref, dm_ref, da_ref, dl_ref, db_ref, dc_ref, dg_ref):
        @pl.when(pl.program_id(0) == 0)
        def _():
            dg_ref[...] = jnp.zeros_like(dg_ref)

        grads = []
        for x, (lo, hi) in zip((_chunks_to_rows(a_ref), b_ref[...], c_ref[...]), MIX_SEGS):
            r = lax.rsqrt(jnp.mean(x * x, axis=-1, keepdims=True) + EPS)
            xhat = x * r
            dm = dm_ref[:, lo:hi].astype(F32)
            dg_ref[:, lo:hi] += jnp.sum(dm * xhat, axis=0, keepdims=True)
            dxh = dm * g_ref[:, lo:hi]
            grads.append(r * (dxh - xhat * jnp.mean(dxh * xhat, axis=-1, keepdims=True)))
        db_ref[...] = grads[1]
        dc_ref[...] = grads[2]
        low = lax.broadcasted_iota(jnp.int32, (ROWS, 128), 1) < HEAD_DIM
        for ch in range(N_CHUNK):
            d_c = grads[0][:, 128 * ch:128 * (ch + 1)]
            da_ref[ch] = d_c.astype(da_ref.dtype)
            prod = d_c * a_ref[ch]
            dl_ref[ch] = jnp.where(low, jnp.sum(prod[:, :HEAD_DIM], axis=-1, keepdims=True),
                                   jnp.sum(prod[:, HEAD_DIM:], axis=-1, keepdims=True))

    row = lambda w: pl.BlockSpec((ROWS, w), lambda i: (i, 0))
    vec = pl.BlockSpec((1, D_MODEL), lambda i: (0, 0))
    chunked = pl.BlockSpec((N_CHUNK, ROWS, 128), lambda i: (0, i, 0))
    return pl.pallas_call(
        body, name="mix_bwd", grid=(s // ROWS,),
        in_specs=[chunked, row(SGU_W), row(SSM_W), vec, row(D_MODEL)],
        out_specs=[chunked, chunked, row(SGU_W), row(SSM_W), vec],
        out_shape=[jax.ShapeDtypeStruct((N_CHUNK, s, 128), ATTN_IO_DTYPE), jax.ShapeDtypeStruct((N_CHUNK, s, 128), F32),
                   jax.ShapeDtypeStruct((s, SGU_W), F32), jax.ShapeDtypeStruct((s, SSM_W), F32),
                   jax.ShapeDtypeStruct((1, D_MODEL), F32)],
        compiler_params=_params(("arbitrary",)),
    )(y_attn_c, y_sgu, y_ssm, gain.reshape(1, D_MODEL), dmix)


CONV_ROWS = 256
CONV_COLS = 1408
CONV_PAIR = 2 * CONV_COLS
HALO = 16


def _interleave_ff(t):
    lead = t.shape[:-1]
    nb = D_FF // CONV_COLS
    return jnp.swapaxes(t.reshape(lead + (2, nb, CONV_COLS)), -3, -2).reshape(lead + (2 * D_FF,))


def _deinterleave_ff(t):
    lead = t.shape[:-1]
    nb = D_FF // CONV_COLS
    return jnp.swapaxes(t.reshape(lead + (nb, 2, CONV_COLS)), -3, -2).reshape(lead + (2 * D_FF,))


def _causal_taps(x, halo, first):
    patch = 8
    row = lax.broadcasted_iota(jnp.int32, (patch, x.shape[1]), 0)
    h1 = jnp.where(first, 0.0, halo[HALO - 1:HALO, :])
    h2 = jnp.where(first, 0.0, halo[HALO - 2:HALO - 1, :])
    r1 = pltpu.roll(x, 1, 0)
    r2 = pltpu.roll(x, 2, 0)
    top1 = jnp.where(row == 0, h1, r1[0:patch])
    top2 = jnp.where(row == 0, h2, jnp.where(row == 1, h1, r2[0:patch]))
    return jnp.concatenate([top1, r1[patch:]], axis=0), jnp.concatenate([top2, r2[patch:]], axis=0)


def _conv_in_specs():
    halo_idx = lambda i: jnp.maximum(i * (CONV_ROWS // HALO) - 1, 0)
    return [pl.BlockSpec((CONV_ROWS, CONV_PAIR), lambda j, i: (i, j)),
            pl.BlockSpec((HALO, CONV_PAIR), lambda j, i: (halo_idx(i), j)),
            pl.BlockSpec((3, CONV_PAIR), lambda j, i: (0, j)),
            pl.BlockSpec((1, CONV_PAIR), lambda j, i: (0, j))]


def _ffn_act_fwd(hh, conv_w, conv_b):
    s = hh.shape[0]

    def body(m_ref, h_ref, w_ref, b_ref, o_ref):
        first = pl.program_id(1) == 0
        main = m_ref[...].astype(F32)
        x1, x2 = _causal_taps(main, h_ref[...].astype(F32), first)
        conv = w_ref[0:1, :] * x2 + w_ref[1:2, :] * x1 + w_ref[2:3, :] * main + b_ref[...]
        o_ref[...] = (_gelu(conv[:, CONV_COLS:]) * conv[:, :CONV_COLS]).astype(o_ref.dtype)

    return pl.pallas_call(
        body, name="ffn_act_fwd", grid=(D_FF // CONV_COLS, s // CONV_ROWS), in_specs=_conv_in_specs(),
        out_specs=pl.BlockSpec((CONV_ROWS, CONV_COLS), lambda j, i: (i, j)),
        out_shape=jax.ShapeDtypeStruct((s, D_FF), MXU_DTYPE),
        compiler_params=_params(("parallel", "parallel")),
    )(hh, hh, conv_w, conv_b.reshape(1, -1))


def _ffn_act_bwd(hh, conv_w, conv_b, da):
    s = hh.shape[0]
    nrow = s // CONV_ROWS
    ext_rows = CONV_ROWS + HALO

    def body(m_ref, h_ref, w_ref, b_ref, nx_ref, da_ref, dan_ref, o_ref, dw_ref, db_ref):
        first = pl.program_id(1) == 0
        last = pl.program_id(1) == nrow - 1

        @pl.when(first)
        def _():
            dw_ref[...] = jnp.zeros_like(dw_ref)
            db_ref[...] = jnp.zeros_like(db_ref)

        ext = jnp.concatenate([m_ref[...].astype(F32), nx_ref[...].astype(F32)], axis=0)
        x1, x2 = _causal_taps(ext, h_ref[...].astype(F32), first)
        conv = w_ref[0:1, :] * x2 + w_ref[1:2, :] * x1 + w_ref[2:3, :] * ext + b_ref[...]
        da = jnp.concatenate([da_ref[...].astype(F32), jnp.where(last, 0.0, dan_ref[...].astype(F32))], axis=0)
        act, dact = _gelu_pair(conv[:, CONV_COLS:])
        dconv = jnp.concatenate([da * act, da * conv[:, :CONV_COLS] * dact], axis=1)
        dmain = dconv[:CONV_ROWS]
        ahead1 = pltpu.roll(dconv, ext_rows - 1, 0)[:CONV_ROWS]
        ahead2 = pltpu.roll(dconv, ext_rows - 2, 0)[:CONV_ROWS]
        o_ref[...] = (w_ref[2:3, :] * dmain + w_ref[1:2, :] * ahead1 + w_ref[0:1, :] * ahead2).astype(o_ref.dtype)
        for t, tap in enumerate((x2, x1, ext)):
            dw_ref[t:t + 1, :] += jnp.sum(dmain * tap[:CONV_ROWS], axis=0, keepdims=True)
        db_ref[...] += jnp.sum(dmain, axis=0, keepdims=True)

    nxt = lambda i: jnp.minimum((i + 1) * (CONV_ROWS // HALO), s // HALO - 1)
    return pl.pallas_call(
        body, name="ffn_act_bwd", grid=(D_FF // CONV_COLS, nrow),
        in_specs=_conv_in_specs() + [pl.BlockSpec((HALO, CONV_PAIR), lambda j, i: (nxt(i), j)),
                                     pl.BlockSpec((CONV_ROWS, CONV_COLS), lambda j, i: (i, j)),
                                     pl.BlockSpec((HALO, CONV_COLS), lambda j, i: (nxt(i), j))],
        out_specs=[pl.BlockSpec((CONV_ROWS, CONV_PAIR), lambda j, i: (i, j)),
                   pl.BlockSpec((3, CONV_PAIR), lambda j, i: (0, j)), pl.BlockSpec((1, CONV_PAIR), lambda j, i: (0, j))],
        out_shape=[jax.ShapeDtypeStruct((s, 2 * D_FF), MXU_DTYPE), jax.ShapeDtypeStruct((3, 2 * D_FF), F32),
                   jax.ShapeDtypeStruct((1, 2 * D_FF), F32)],
        compiler_params=_params(("parallel", "arbitrary")),
    )(hh, hh, conv_w, conv_b.reshape(1, -1), hh, da, da)


def _ple_weight_specs(layer):
    return [pl.BlockSpec((None, D_MODEL, D_MODEL), lambda i: (layer, 0, 0)),
            pl.BlockSpec((None, PLE_DIM, D_MODEL), lambda i: (layer, 0, 0))]


def _ple_fwd(xn, p, w_gate, w_proj, h, layer):
    s = xn.shape[0]
    tm = 512

    def body(x_ref, p_ref, wg_ref, wp_ref, h_ref, o_ref):
        gate = jax.nn.sigmoid(_dot(x_ref[...].astype(MXU_DTYPE), wg_ref[...].astype(MXU_DTYPE), NN))
        proj = _dot(p_ref[...].astype(MXU_DTYPE), wp_ref[...].astype(MXU_DTYPE), NN)
        o_ref[...] = h_ref[...] + gate * proj

    return pl.pallas_call(
        body, name="ple_fwd", grid=(s // tm,),
        in_specs=[pl.BlockSpec((tm, D_MODEL), lambda i: (i, 0)), pl.BlockSpec((tm, PLE_DIM), lambda i: (i, 0))]
        + _ple_weight_specs(layer) + [pl.BlockSpec((tm, D_MODEL), lambda i: (i, 0))],
        out_specs=pl.BlockSpec((tm, D_MODEL), lambda i: (i, 0)),
        out_shape=jax.ShapeDtypeStruct((s, D_MODEL), F32),
        compiler_params=_params(("parallel",)),
    )(xn, p, w_gate, w_proj, h)


def _ple_bwd(xn, p, w_gate, w_proj, dh, layer):
    s = xn.shape[0]
    tm = 512

    def body(x_ref, p_ref, wg_ref, wp_ref, dh_ref, dpre_ref, dproj_ref):
        gate = jax.nn.sigmoid(_dot(x_ref[...].astype(MXU_DTYPE), wg_ref[...].astype(MXU_DTYPE), NN))
        proj = _dot(p_ref[...].astype(MXU_DTYPE), wp_ref[...].astype(MXU_DTYPE), NN)
        dh = dh_ref[...]
        dpre_ref[...] = (dh * proj * gate * (1.0 - gate)).astype(dpre_ref.dtype)
        dproj_ref[...] = (dh * gate).astype(dproj_ref.dtype)

    row = pl.BlockSpec((tm, D_MODEL), lambda i: (i, 0))
    return pl.pallas_call(
        body, name="ple_bwd", grid=(s // tm,),
        in_specs=[row, pl.BlockSpec((tm, PLE_DIM), lambda i: (i, 0))] + _ple_weight_specs(layer) + [row],
        out_specs=[row, row],
        out_shape=[jax.ShapeDtypeStruct((s, D_MODEL), MXU_DTYPE)] * 2,
        compiler_params=_params(("parallel",)),
    )(xn, p, w_gate, w_proj, dh)


O_SGU = 3 * ATTN_W
O_SSM = O_SGU + 2 * SGU_W


def _layer_consts(w, i):
    causal = jnp.asarray(np.tril(np.ones((SGU_CHUNK, SGU_CHUNK), np.float32)))
    return {
        "sgu_w_mask": w["sgu_w"][i] * causal,
        "sgu_b_t": w["sgu_b"][i].T,
        "ssm_ops": _ssm_operands(w["ssm_a_re"][i], w["ssm_a_im"][i], w["ssm_log_dt"][i], w["ssm_b_re"][i],
                                 w["ssm_b_im"][i], w["ssm_c_re"][i], w["ssm_c_im"][i]),
    }


def _layer_fwd(h0, p_i, w, i, bias):
    c = _layer_consts(w, i)
    xn1 = _rms_fwd(h0, w["norm_attn_g"][i], name="rms_attn_fwd")
    qkv, zs, us = _in_proj(xn1, w["w_in"], i)
    y_attn, lse = _attn2_fwd(qkv, bias)
    y_sgu = _sgu_fwd(zs, w["sgu_ln_g"][i], w["sgu_ln_b"][i], c["sgu_w_mask"], c["sgu_b_t"])
    y_ssm, entry = _ssm_fwd(us, c["ssm_ops"], w["ssm_d"][i], w["ssm_glu_w"][i], w["ssm_glu_b"][i])
    mix = _mix_fwd(y_attn, y_sgu, y_ssm, w["branch_norm_g"][i])
    h1 = _matmul(mix, w["w_out"], name="out_proj", out_dtype=F32, tm=512, tn=1024, residual=h0, layer=i)
    xn2 = _rms_fwd(h1, w["norm_ffn_g"][i], name="rms_ffn_fwd")
    hh = _matmul(xn2, w["ffn_w_up"], name="ffn_up", out_dtype=MXU_DTYPE, tm=1024, tn=1408, layer=i)
    act = _ffn_act_fwd(hh, w["ffn_conv_w"][i], w["ffn_conv_b"][i])
    h2 = _matmul(act, w["ffn_w_down"], name="ffn_down", out_dtype=F32, tm=512, tn=1024, residual=h1, layer=i)
    xn3 = _rms_fwd(h2, w["norm_ple_g"][i], name="rms_ple_fwd")
    h3 = _ple_fwd(xn3, p_i, w["ple_w_gate"], w["ple_w_proj"], h2, i)
    saved = dict(h0=h0, xn1=xn1, qkv=qkv, zs=zs, us=us, y_attn=y_attn, lse=lse, y_sgu=y_sgu, y_ssm=y_ssm,
                 entry=entry, mix=mix, h1=h1, xn2=xn2, hh=hh, act=act, h2=h2, xn3=xn3, consts=c)
    return h3, saved


def _layer_bwd(dh3, sv, p_i, w, i, bias):
    c = sv["consts"]
    g = {}
    dpre, dproj = _ple_bwd(sv["xn3"], p_i, w["ple_w_gate"], w["ple_w_proj"], dh3, i)
    g["ple_w_gate"] = _matmul_tn(sv["xn3"], dpre, name="d_ple_w_gate", tk=1024, tn=1024)
    g["ple_w_proj"] = _matmul_tn(p_i, dproj, name="d_ple_w_proj", tk=256, tn=1024)
    dxn3 = _matmul(dpre, w["ple_w_gate"], name="d_xn_ple", out_dtype=F32, tm=512, tn=1024, trans_b=True, layer=i)
    dh2, g["norm_ple_g"] = _rms_bwd(sv["h2"], w["norm_ple_g"][i], dxn3, dh3, name="rms_ple_bwd")
    g["ffn_w_down"] = _matmul_tn(sv["act"], dh2, name="d_ffn_w_down", tk=1408, tn=1024)
    dact = _matmul(dh2, w["ffn_w_down"], name="d_ffn_act", out_dtype=MXU_DTYPE, tm=512, tn=1408, trans_b=True, layer=i)
    dhh, g["ffn_conv_w"], g["ffn_conv_b"] = _ffn_act_bwd(sv["hh"], w["ffn_conv_w"][i], w["ffn_conv_b"][i], dact)
    g["ffn_w_up"] = _matmul_tn(sv["xn2"], dhh, name="d_ffn_w_up", tk=1024, tn=1408)
    dxn2 = _matmul(dhh, w["ffn_w_up"], name="d_xn_ffn", out_dtype=F32, tm=512, tn=512, trans_b=True, layer=i)
    dh1, g["norm_ffn_g"] = _rms_bwd(sv["h1"], w["norm_ffn_g"][i], dxn2, dh2, name="rms_ffn_bwd")
    g["w_out"] = _matmul_tn(sv["mix"], dh1, name="d_w_out", tk=1024, tn=1024)
    dmix = _matmul(dh1, w["w_out"], name="d_mix", out_dtype=F32, tm=512, tn=1024, trans_b=True, layer=i)
    dy_attn, delta, dy_sgu, dy_ssm, g["branch_norm_g"] = _mix_bwd(sv["y_attn"], sv["y_sgu"], sv["y_ssm"],
                                                                  w["branch_norm_g"][i], dmix)
    dq, dk, dv, ek, ev, dbias = _attn2_bwd(sv["qkv"], bias, sv["lse"], delta, dy_attn)
    dzs, g["sgu_ln_g"], g["sgu_ln_b"], dsw, dsb = _sgu_bwd(sv["zs"], w["sgu_ln_g"][i], w["sgu_ln_b"][i],
                                                          c["sgu_w_mask"], c["sgu_b_t"], dy_sgu)
    causal = jnp.asarray(np.tril(np.ones((SGU_CHUNK, SGU_CHUNK), np.float32)))
    g["sgu_w"] = dsw * causal
    g["sgu_b"] = dsb.T
    dus, dbb, dcm, da, g["ssm_d"], g["ssm_glu_w"], g["ssm_glu_b"] = _ssm_bwd(
        sv["us"], sv["entry"], c["ssm_ops"], w["ssm_d"][i], w["ssm_glu_w"][i], w["ssm_glu_b"][i], dy_ssm)
    dbb5 = dbb.reshape(SSM_G, SSM_C, 2, SSM_G, SSM_N)
    dbbar = jnp.einsum("gcpgn->pgnc", dbb5)
    dcm5 = dcm.reshape(2, SSM_G, SSM_N, SSM_G, SSM_C)
    dcc = jnp.einsum("pgngc->pgcn", dcm5)
    g["ssm_c_re"] = dcc[0]
    g["ssm_c_im"] = -dcc[1]
    da2 = da.reshape(2, SSM_G, SSM_N)
    _, vjp = jax.vjp(_ssm_discretize, w["ssm_a_re"][i], w["ssm_a_im"][i], w["ssm_log_dt"][i],
                     w["ssm_b_re"][i], w["ssm_b_im"][i])
    (g["ssm_a_re"], g["ssm_a_im"], g["ssm_log_dt"], g["ssm_b_re"], g["ssm_b_im"]) = vjp(
        (da2[0], da2[1], dbbar[0], dbbar[1]))
    dz = _attn2_bwd_sum(dq, dk, dv, ek, ev, dzs, dus)
    g["w_in"] = _matmul_tn(sv["xn1"], dz, name="d_w_in", tk=1024, tn=1152)
    dxn1 = _matmul(dz, w["w_in"], name="d_xn_attn", out_dtype=F32, tm=512, tn=1024, trans_b=True, layer=i)
    dh0, g["norm_attn_g"] = _rms_bwd(sv["h0"], w["norm_attn_g"][i], dxn1, dh1, name="rms_attn_bwd")
    for k in ("norm_ple_g", "norm_ffn_g", "branch_norm_g", "norm_attn_g", "sgu_ln_g", "sgu_ln_b", "ssm_d",
              "ssm_glu_b", "ffn_conv_b"):
        g[k] = g[k].reshape(-1)
    return dh0, g, dbias


def _local_step(x, p, target, w, ff_interleaved=False):
    ff_names = ("ffn_conv_b",) if ff_interleaved else FF_SHARDED + ("ffn_conv_b",)
    w = dict(w)
    for k in ff_names:
        w[k] = _interleave_ff(w[k])
    bias = _bias_build(w["rel_bias"])
    h = x
    saved = []
    for i in range(DEPTH):
        h, sv = _layer_fwd(h, p[i], w, i, bias)
        saved.append(sv)
    loss, dh, dgf = _loss_head(h, w["final_norm_g"], target)
    layer_grads = [None] * DEPTH
    dbias = None
    for i in reversed(range(DEPTH)):
        dh, layer_grads[i], db = _layer_bwd(dh, saved[i], p[i], w, i, bias)
        dbias = db if dbias is None else dbias + db
    grads = {k: jnp.stack([layer_grads[i][k] for i in range(DEPTH)]) for k in layer_grads[0]}
    for k in ff_names:
        grads[k] = _deinterleave_ff(grads[k])
    grads["rel_bias"] = _bias_reduce(dbias)
    grads["final_norm_g"] = dgf.reshape(-1)
    return loss, dh, grads


def _pad_rows(a2, mult=16):
    r = (-a2.shape[0]) % mult
    return a2 if r == 0 else jnp.concatenate([a2, jnp.zeros((r, a2.shape[1]), a2.dtype)], axis=0)


def _as_rows(a, rows=None):
    flat = a.reshape(-1)
    if rows is None:
        rows = -(-flat.shape[0] // (16 * PACK_COLS)) * 16
    return jnp.pad(flat, (0, rows * PACK_COLS - flat.shape[0])).reshape(rows, PACK_COLS)


def _shard_shape(name):
    full, ax = BIG_FULL[name]
    shp = [DEPTH] + list(full)
    shp[ax] //= N_CHIPS
    return tuple(shp)


EXACT_NAMES = ("ffn_conv_w",)


def _pack_rows_of(name):
    n = int(np.prod(_shard_shape(name))) * (2 if name in EXACT_NAMES else 1)
    rows = -(-n // PACK_COLS)
    return -(-rows // 16) * 16


def _pack_shards(shards, dtype, exact=False):
    split_words = exact and jnp.dtype(dtype).itemsize == 2
    parts = []
    for n in BIG_NAMES:
        a = shards[n]
        if split_words and n in EXACT_NAMES:
            a = lax.bitcast_convert_type(a.astype(F32), dtype)
        parts.append(_as_rows(a.astype(dtype), _pack_rows_of(n)))
    used = sum(pt.shape[0] for pt in parts)
    parts.append(jnp.zeros((PACK_ROWS - used, PACK_COLS), dtype))
    return jnp.concatenate(parts, axis=0)


def _unpack_shard(flat, name, exact=False):
    off = 0
    for n in BIG_NAMES:
        if n == name:
            break
        off += _pack_rows_of(n)
    shp = _shard_shape(name)
    cnt = int(np.prod(shp))
    vec = flat[off:off + _pack_rows_of(name)].reshape(-1)
    if exact and name in EXACT_NAMES and jnp.dtype(flat.dtype).itemsize == 2:
        return lax.bitcast_convert_type(vec[:2 * cnt].reshape(shp + (2,)), F32)
    return vec[:cnt].reshape(shp)


FF_SHARDED = ("ffn_w_up", "ffn_conv_w")
FF_CHIP_ORDER = (0, 2, 1, 3)


def _chip_order(name):
    return FF_CHIP_ORDER if name in FF_SHARDED else tuple(range(N_CHIPS))


def _split_full(full, name):
    _, ax = BIG_FULL[name]
    parts = jnp.split(full, N_CHIPS, axis=ax)
    out = [None] * N_CHIPS
    for j, k in enumerate(_chip_order(name)):
        out[k] = parts[j]
    return out


def _join_shards(shards, name):
    _, ax = BIG_FULL[name]
    return jnp.concatenate([shards[k] for k in _chip_order(name)], axis=ax)


def _small_shapes(w):
    return [(n, w[n].shape) for n in SMALL_NAMES]


def _pack_small(d):
    flat = jnp.concatenate([d[n].astype(F32).reshape(-1) for n in SMALL_NAMES])
    flat = jnp.concatenate([flat, jnp.zeros((SMALL_ROWS * PACK_COLS - flat.shape[0],), F32)])
    return flat.reshape(SMALL_ROWS, PACK_COLS)


def _unpack_small(flat, shapes):
    out, off = {}, 0
    v = flat.reshape(-1)
    for n, shp in shapes:
        cnt = int(np.prod(shp))
        out[n] = v[off:off + cnt].reshape(shp)
        off += cnt
    return out


MESH = pl.DeviceIdType.MESH
ANY = pl.BlockSpec(memory_space=pl.ANY)


def _me():
    return lax.axis_index("x"), lax.axis_index("y"), lax.axis_index("c")


def _other_chips(x, y):
    return [(1 - x, y), (x, 1 - y), (1 - x, 1 - y)]


def _gather_weights(wflat):
    def body(w_ref, out_ref, send_sems, recv_sems, local_sem):
        x, y, c = _me()
        sibling = (x, y, 1 - c)
        chips = _other_chips(x, y)

        def rows(chip, half):
            return out_ref.at[2 * chip[0] + chip[1], pl.ds(half * PACK_HALF, PACK_HALF), :]

        def copy(k, chip, half, to, src=None):
            return pltpu.make_async_remote_copy(
                src_ref=rows(chip, half) if src is None else src, dst_ref=rows(chip, half),
                send_sem=send_sems.at[k], recv_sem=recv_sems.at[k], device_id=to, device_id_type=MESH)

        mine = pltpu.make_async_copy(w_ref, out_ref.at[2 * x + y], local_sem)
        mine.start()
        my_half = w_ref.at[pl.ds(c * PACK_HALF, PACK_HALF), :]
        first = [copy(j, (x, y), c, (*chip, c), src=my_half) for j, chip in enumerate(chips)]
        for cp in first:
            cp.start()
        passed = [copy(3 + j, chip, c, sibling) for j, chip in enumerate(chips)]
        for j, chip in enumerate(chips):
            copy(j, chip, c, (x, y, c)).wait_recv()
            passed[j].start()
        for j, chip in enumerate(chips):
            copy(3 + j, chip, 1 - c, (x, y, c)).wait_recv()
        for cp in first + passed:
            cp.wait_send()
        mine.wait()

    return pl.pallas_call(
        body, name="gather_weights", in_specs=[ANY], out_specs=ANY,
        out_shape=jax.ShapeDtypeStruct((N_CHIPS, PACK_ROWS, PACK_COLS), wflat.dtype),
        scratch_shapes=[pltpu.SemaphoreType.DMA((6,)), pltpu.SemaphoreType.DMA((6,)), pltpu.SemaphoreType.DMA],
    )(wflat)


def _exchange_partials(gb, gs):
    def body(gb_ref, gs_ref, half_ref, small_ref, send_sems, recv_sems, local_sem):
        x, y, c = _me()
        me_idx = 4 * x + 2 * y + c
        mine = pltpu.make_async_copy(gs_ref, small_ref.at[me_idx], local_sem)
        mine.start()
        d2d = pltpu.make_async_remote_copy(
            src_ref=gb_ref.at[:, pl.ds((1 - c) * PACK_HALF, PACK_HALF), :], dst_ref=half_ref,
            send_sem=send_sems.at[0], recv_sem=recv_sems.at[0], device_id=(x, y, 1 - c), device_id_type=MESH)
        d2d.start()
        copies = []
        for k in range(1, N_DEV):
            fx, fy, fc = (k >> 2) & 1, (k >> 1) & 1, k & 1
            peer = (x ^ fx, y ^ fy, c ^ fc)
            copies.append(pltpu.make_async_remote_copy(
                src_ref=gs_ref, dst_ref=small_ref.at[me_idx], send_sem=send_sems.at[k], recv_sem=recv_sems.at[k],
                device_id=peer, device_id_type=MESH))
        for cp in copies:
            cp.start()
        for k in range(1, N_DEV):
            fx, fy, fc = (k >> 2) & 1, (k >> 1) & 1, k & 1
            peer_idx = 4 * (x ^ fx) + 2 * (y ^ fy) + (c ^ fc)
            pltpu.make_async_remote_copy(
                src_ref=gs_ref, dst_ref=small_ref.at[peer_idx], send_sem=send_sems.at[k], recv_sem=recv_sems.at[k],
                device_id=(x, y, c), device_id_type=MESH).wait_recv()
        d2d.wait_recv()
        d2d.wait_send()
        for cp in copies:
            cp.wait_send()
        mine.wait()

    return pl.pallas_call(
        body, name="exchange_partials", in_specs=[ANY, ANY], out_specs=[ANY, ANY],
        out_shape=[jax.ShapeDtypeStruct((N_CHIPS, PACK_HALF, PACK_COLS), gb.dtype),
                   jax.ShapeDtypeStruct((N_DEV, SMALL_ROWS, PACK_COLS), F32)],
        scratch_shapes=[pltpu.SemaphoreType.DMA((N_DEV,)), pltpu.SemaphoreType.DMA((N_DEV,)), pltpu.SemaphoreType.DMA],
    )(gb, gs)


RED_ROWS = 256


def _chip_partials(gb, sib, c_idx):
    nrow = PACK_HALF // RED_ROWS

    def body(c_ref, a_ref, b_ref, o_ref):
        del c_ref
        o_ref[...] = (a_ref[...].astype(F32) + b_ref[...].astype(F32)).astype(o_ref.dtype)

    blk = (1, RED_ROWS, PACK_COLS)
    return pl.pallas_call(
        body, name="chip_partials",
        grid_spec=pltpu.PrefetchScalarGridSpec(
            num_scalar_prefetch=1, grid=(N_CHIPS, nrow),
            in_specs=[pl.BlockSpec(blk, lambda k, i, c: (k, c[0] * nrow + i, 0)),
                      pl.BlockSpec(blk, lambda k, i, c: (k, i, 0))],
            out_specs=pl.BlockSpec(blk, lambda k, i, c: (k, i, 0))),
        out_shape=jax.ShapeDtypeStruct((N_CHIPS, PACK_HALF, PACK_COLS), gb.dtype),
        compiler_params=_params(("parallel", "parallel")),
    )(c_idx, gb, sib)


def _scatter_partials(pc):
    def body(pc_ref, out_ref, send_sems, recv_sems):
        x, y, c = _me()
        chips = _other_chips(x, y)
        copies = [pltpu.make_async_remote_copy(
            src_ref=pc_ref.at[2 * chip[0] + chip[1]], dst_ref=out_ref.at[k],
            send_sem=send_sems.at[k], recv_sem=recv_sems.at[k], device_id=(*chip, c), device_id_type=MESH)
            for k, chip in enumerate(chips)]
        for cp in copies:
            cp.start()
        for cp in copies:
            cp.wait_recv()
        for cp in copies:
            cp.wait_send()

    return pl.pallas_call(
        body, name="scatter_partials", in_specs=[ANY], out_specs=ANY,
        out_shape=jax.ShapeDtypeStruct((3, PACK_HALF, PACK_COLS), pc.dtype),
        scratch_shapes=[pltpu.SemaphoreType.DMA((3,)), pltpu.SemaphoreType.DMA((3,))],
    )(pc)


def _final_half(gb, sib, recv, idx):
    nrow = PACK_HALF // RED_ROWS

    def body(idx_ref, a_ref, b_ref, r_ref, o_ref):
        del idx_ref
        acc = a_ref[0].astype(F32) + b_ref[0].astype(F32)
        for k in range(3):
            acc = acc + r_ref[k].astype(F32)
        o_ref[...] = acc

    return pl.pallas_call(
        body, name="final_half",
        grid_spec=pltpu.PrefetchScalarGridSpec(
            num_scalar_prefetch=1, grid=(nrow,),
            in_specs=[pl.BlockSpec((1, RED_ROWS, PACK_COLS), lambda i, idx: (idx[0], idx[1] * nrow + i, 0)),
                      pl.BlockSpec((1, RED_ROWS, PACK_COLS), lambda i, idx: (idx[0], i, 0)),
                      pl.BlockSpec((3, RED_ROWS, PACK_COLS), lambda i, idx: (0, i, 0))],
            out_specs=pl.BlockSpec((RED_ROWS, PACK_COLS), lambda i, idx: (i, 0))),
        out_shape=jax.ShapeDtypeStruct((PACK_HALF, PACK_COLS), F32),
        compiler_params=_params(("parallel",)),
    )(idx, gb, sib, recv)


def _share_halves(half):
    def body(h_ref, out_ref, send_sem, recv_sem, local_sem):
        x, y, c = _me()
        dst = out_ref.at[pl.ds(c * PACK_HALF, PACK_HALF), :]
        mine = pltpu.make_async_copy(h_ref, dst, local_sem)
        mine.start()
        cp = pltpu.make_async_remote_copy(src_ref=h_ref, dst_ref=dst, send_sem=send_sem, recv_sem=recv_sem,
                                          device_id=(x, y, 1 - c), device_id_type=MESH)
        cp.start()
        pltpu.make_async_remote_copy(src_ref=h_ref, dst_ref=out_ref.at[pl.ds((1 - c) * PACK_HALF, PACK_HALF), :],
                                     send_sem=send_sem, recv_sem=recv_sem, device_id=(x, y, c),
                                     device_id_type=MESH).wait_recv()
        cp.wait_send()
        mine.wait()

    return pl.pallas_call(
        body, name="share_halves", in_specs=[ANY], out_specs=ANY,
        out_shape=jax.ShapeDtypeStruct((PACK_ROWS, PACK_COLS), F32),
        scratch_shapes=[pltpu.SemaphoreType.DMA, pltpu.SemaphoreType.DMA, pltpu.SemaphoreType.DMA],
    )(half)


def _sum_small(allsmall):
    def body(a_ref, o_ref):
        acc = a_ref[0]
        for k in range(1, N_DEV):
            acc = acc + a_ref[k]
        o_ref[...] = acc

    tr = 96
    return pl.pallas_call(
        body, name="sum_small", grid=(SMALL_ROWS // tr,),
        in_specs=[pl.BlockSpec((N_DEV, tr, PACK_COLS), lambda i: (0, i, 0))],
        out_specs=pl.BlockSpec((tr, PACK_COLS), lambda i: (i, 0)),
        out_shape=jax.ShapeDtypeStruct((SMALL_ROWS, PACK_COLS), F32),
        compiler_params=_params(("parallel",)),
    )(allsmall)


def _adamw(w, g, m, v, *, name):
    shape = w.shape
    cols = shape[-1]
    as2 = lambda t: t.reshape(-1, cols)
    w2, g2, m2, v2 = as2(w), as2(g), as2(m), as2(v)
    rows = w2.shape[0]
    tr = rows
    if rows * cols * 4 > (1 << 20):
        tr = _tile(rows, max(8, (1 << 20) // (cols * 4) // 8 * 8), 8)

    def body(w_ref, g_ref, m_ref, v_ref, d_ref, mo_ref, vo_ref):
        gg = g_ref[...]
        mn = ADAM_B1 * m_ref[...] + (1.0 - ADAM_B1) * gg
        vn = ADAM_B2 * v_ref[...] + (1.0 - ADAM_B2) * (gg * gg)
        m_hat = mn / (1.0 - ADAM_B1 ** ADAM_STEP)
        v_hat = vn / (1.0 - ADAM_B2 ** ADAM_STEP)
        d_ref[...] = -ADAM_LR * (m_hat / (jnp.sqrt(v_hat) + ADAM_EPS) + ADAM_WD * w_ref[...])
        mo_ref[...] = mn
        vo_ref[...] = vn

    blk = pl.BlockSpec((tr, cols), lambda i: (i, 0))
    outs = pl.pallas_call(
        body, name=name, grid=(rows // tr,), in_specs=[blk] * 4, out_specs=[blk] * 3,
        out_shape=[jax.ShapeDtypeStruct((rows, cols), F32)] * 3,
        compiler_params=_params(("parallel",)),
    )(w2, g2, m2, v2)
    return tuple(t.reshape(shape) for t in outs)


def kernel(x, p, rel_bias, norm_attn_g, w_in, sgu_ln_g, sgu_ln_b, sgu_w, sgu_b, ssm_a_re, ssm_a_im, ssm_log_dt, ssm_b_re, ssm_b_im, ssm_c_re, ssm_c_im, ssm_d, ssm_glu_w, ssm_glu_b, branch_norm_g, w_out, norm_ffn_g, ffn_w_up, ffn_conv_w, ffn_conv_b, ffn_w_down, norm_ple_g, ple_w_gate, ple_w_proj, final_norm_g, loss_target, m_rel_bias, m_norm_attn_g, m_w_in, m_sgu_ln_g, m_sgu_ln_b, m_sgu_w, m_sgu_b, m_ssm_a_re, m_ssm_a_im, m_ssm_log_dt, m_ssm_b_re, m_ssm_b_im, m_ssm_c_re, m_ssm_c_im, m_ssm_d, m_ssm_glu_w, m_ssm_glu_b, m_branch_norm_g, m_w_out, m_norm_ffn_g, m_ffn_w_up, m_ffn_conv_w, m_ffn_conv_b, m_ffn_w_down, m_norm_ple_g, m_ple_w_gate, m_ple_w_proj, m_final_norm_g, v_rel_bias, v_norm_attn_g, v_w_in, v_sgu_ln_g, v_sgu_ln_b, v_sgu_w, v_sgu_b, v_ssm_a_re, v_ssm_a_im, v_ssm_log_dt, v_ssm_b_re, v_ssm_b_im, v_ssm_c_re, v_ssm_c_im, v_ssm_d, v_ssm_glu_w, v_ssm_glu_b, v_branch_norm_g, v_w_out, v_norm_ffn_g, v_ffn_w_up, v_ffn_conv_w, v_ffn_conv_b, v_ffn_w_down, v_norm_ple_g, v_ple_w_gate, v_ple_w_proj, v_final_norm_g):
    args = dict(locals())
    wts = {n: args[n] for n in WEIGHT_NAMES}
    mom_m = {n: args["m_" + n] for n in WEIGHT_NAMES}
    mom_v = {n: args["v_" + n] for n in WEIGHT_NAMES}

    wall = _gather_weights(_pack_shards({n: wts[n] for n in BIG_NAMES}, MXU_DTYPE, exact=True))
    full = dict(wts)
    for n in BIG_NAMES:
        full[n] = _join_shards([_unpack_shard(wall[k], n, exact=True) for k in range(N_CHIPS)], n)
    full["ffn_conv_w"] = full["ffn_conv_w"].astype(F32)

    loss, dx, grads = _local_step(x[0], p[:, 0], loss_target[0], full, ff_interleaved=True)
    loss = lax.psum(loss[0, 0], MESH_AXES)

    xi, yi, ci = _me()
    stacked = {n: _split_full(grads[n], n) for n in BIG_NAMES}
    gb = jnp.stack([_pack_shards({n: stacked[n][k] for n in BIG_NAMES}, MXU_DTYPE) for k in range(N_CHIPS)])
    gs = _pack_small(grads)
    sib, allsmall = _exchange_partials(gb, gs)
    pc = _chip_partials(gb, sib, jnp.stack([ci]).astype(jnp.int32))
    recv = _scatter_partials(pc)
    half = _final_half(gb, sib, recv, jnp.stack([2 * xi + yi, ci]).astype(jnp.int32))
    gflat = _share_halves(half)
    gsmall = _unpack_small(_sum_small(allsmall), _small_shapes(wts))

    g_out, d_out, m_out, v_out = {}, {}, {}, {}
    for n in BIG_NAMES:
        g_out[n] = _unpack_shard(gflat, n)
        d_out[n], m_out[n], v_out[n] = _adamw(wts[n], g_out[n], mom_m[n], mom_v[n], name="adamw_" + n)
    sw = _pack_small(wts)
    d_s, m_s, v_s = _adamw(sw, _pack_small(gsmall), _pack_small(mom_m), _pack_small(mom_v), name="adamw_small")
    shapes = _small_shapes(wts)
    d_sm, m_sm, v_sm = _unpack_small(d_s, shapes), _unpack_small(m_s, shapes), _unpack_small(v_s, shapes)
    for n in SMALL_NAMES:
        g_out[n], d_out[n], m_out[n], v_out[n] = gsmall[n], d_sm[n], m_sm[n], v_sm[n]

    return (loss, dx[None], *[g_out[n] for n in WEIGHT_NAMES], *[d_out[n] for n in WEIGHT_NAMES],
            *[m_out[n] for n in WEIGHT_NAMES], *[v_out[n] for n in WEIGHT_NAMES])
```

```python
import functools
import math

import numpy as np
import jax
import jax.numpy as jnp
from jax import lax
from jax.experimental import pallas as pl
from jax.experimental.pallas import tpu as pltpu

F32 = jnp.float32
MXU_DTYPE = jnp.bfloat16
VMEM_LIMIT_BYTES = 52 * 1024 * 1024

D_MODEL = 1024
DEPTH = 2
PLE_DIM = 256
HEAD_DIM = 64
N_HEADS = 8
ATTN_W = 512
QBLK = 128
BRANCH_DIL = (1, 4, 16)
N_BUCKETS = 32
REL_MAX_DIST = 2048
SGU_W = 256
SGU_G = 4
SGU_GW = 64
SGU_CHUNK = 128
SSM_W = 256
SSM_G = 16
SSM_C = 16
SSM_N = 64
NSTATE = SSM_G * SSM_N
D_FF = 2816
EPS = 1e-6
NEG_INF = -1e30
ATTN_SCALE = HEAD_DIM ** -0.5

ADAM_LR = 0.001
ADAM_B1 = 0.9
ADAM_B2 = 0.999
ADAM_EPS = 1e-08
ADAM_WD = 0.01
ADAM_STEP = 10

SSM_NSEG = 8
SSM_TSEG = 64
SSM_TB = SSM_NSEG * SSM_TSEG
SSM_LANE_CHUNK = 512

MESH_AXES = ("x", "y", "c")
N_CHIPS = 4
N_DEV = 8

BIG_NAMES = ("w_in", "ssm_glu_w", "w_out", "ffn_w_up", "ffn_conv_w", "ffn_w_down", "ple_w_gate", "ple_w_proj")
BIG_FULL = {
    "w_in": ((D_MODEL, 2304), 2),
    "ssm_glu_w": ((SSM_W, SSM_W), 1),
    "w_out": ((D_MODEL, D_MODEL), 1),
    "ffn_w_up": ((D_MODEL, 2 * D_FF), 2),
    "ffn_conv_w": ((3, 2 * D_FF), 2),
    "ffn_w_down": ((D_FF, D_MODEL), 1),
    "ple_w_gate": ((D_MODEL, D_MODEL), 1),
    "ple_w_proj": ((PLE_DIM, D_MODEL), 2),
}
PACK_COLS = 1024
PACK_ROWS = 6656
PACK_HALF = PACK_ROWS // 2

SMALL_NAMES = ("rel_bias", "norm_attn_g", "sgu_ln_g", "sgu_ln_b", "sgu_w", "sgu_b", "ssm_a_re", "ssm_a_im",
               "ssm_log_dt", "ssm_b_re", "ssm_b_im", "ssm_c_re", "ssm_c_im", "ssm_d", "ssm_glu_b",
               "branch_norm_g", "norm_ffn_g", "ffn_conv_b", "norm_ple_g", "final_norm_g")
SMALL_ROWS = 288

WEIGHT_NAMES = ("rel_bias", "norm_attn_g", "w_in", "sgu_ln_g", "sgu_ln_b", "sgu_w", "sgu_b", "ssm_a_re", "ssm_a_im",
                "ssm_log_dt", "ssm_b_re", "ssm_b_im", "ssm_c_re", "ssm_c_im", "ssm_d", "ssm_glu_w", "ssm_glu_b",
                "branch_norm_g", "w_out", "norm_ffn_g", "ffn_w_up", "ffn_conv_w", "ffn_conv_b", "ffn_w_down",
                "norm_ple_g", "ple_w_gate", "ple_w_proj", "final_norm_g")


def _params(sem):
    return pltpu.CompilerParams(dimension_semantics=sem, vmem_limit_bytes=VMEM_LIMIT_BYTES)


def _tile(n, cap, mult=128):
    if n <= cap:
        return n
    best = None
    for t in range(mult, cap + 1, mult):
        if n % t == 0:
            best = t
    assert best is not None, (n, cap)
    return best


def _gelu(x):
    return 0.5 * x * (1.0 + jnp.tanh(0.7978845608028654 * (x + 0.044715 * x * x * x)))


def _gelu_pair(x):
    x2 = x * x
    t = jnp.tanh(0.7978845608028654 * x * (1.0 + 0.044715 * x2))
    half = 0.5 * (1.0 + t)
    return x * half, half + 0.5 * x * (1.0 - t * t) * (0.7978845608028654 + 3.0 * 0.044715 * 0.7978845608028654 * x2)


def _dot(a, b, dims):
    return lax.dot_general(a, b, (dims, ((), ())), preferred_element_type=F32)


def _dotf(a, b, dims):
    return _dot(a.astype(MXU_DTYPE), b.astype(MXU_DTYPE), dims)


NN = ((1,), (0,))
NT = ((1,), (1,))
TN = ((0,), (0,))


def _matmul(a, b, *, name, out_dtype, tm, tn, trans_b=False, residual=None, layer=None):
    m, k = a.shape
    n = b.shape[-2] if trans_b else b.shape[-1]
    tm = _tile(m, tm, 8)
    tn = _tile(n, tn)
    dims = NT if trans_b else NN
    lead = () if layer is None else (None,)
    lidx = () if layer is None else (layer,)

    def body(*refs):
        if residual is None:
            a_ref, b_ref, o_ref = refs
        else:
            a_ref, b_ref, r_ref, o_ref = refs
        acc = _dot(a_ref[...].astype(MXU_DTYPE), b_ref[...].astype(MXU_DTYPE), dims)
        if residual is not None:
            acc = acc + r_ref[...]
        o_ref[...] = acc.astype(o_ref.dtype)

    b_spec = (pl.BlockSpec(lead + (tn, k), lambda i, j: lidx + (j, 0)) if trans_b
              else pl.BlockSpec(lead + (k, tn), lambda i, j: lidx + (0, j)))
    in_specs = [pl.BlockSpec((tm, k), lambda i, j: (i, 0)), b_spec]
    args = [a, b]
    if residual is not None:
        in_specs.append(pl.BlockSpec((tm, tn), lambda i, j: (i, j)))
        args.append(residual)
    return pl.pallas_call(
        body, name=name, grid=(m // tm, n // tn), in_specs=in_specs,
        out_specs=pl.BlockSpec((tm, tn), lambda i, j: (i, j)),
        out_shape=jax.ShapeDtypeStruct((m, n), out_dtype),
        compiler_params=_params(("parallel", "parallel")),
    )(*args)


def _matmul_tn(a, g, *, name, tk, tn, tm=512):
    m, k = a.shape
    n = g.shape[1]
    tk = _tile(k, tk)
    tn = _tile(n, tn)
    tm = _tile(m, tm, 8)

    def body(a_ref, g_ref, o_ref):
        @pl.when(pl.program_id(2) == 0)
        def _():
            o_ref[...] = jnp.zeros_like(o_ref)

        o_ref[...] += _dot(a_ref[...].astype(MXU_DTYPE), g_ref[...].astype(MXU_DTYPE), TN)

    return pl.pallas_call(
        body, name=name, grid=(k // tk, n // tn, m // tm),
        in_specs=[pl.BlockSpec((tm, tk), lambda i, j, s: (s, i)),
                  pl.BlockSpec((tm, tn), lambda i, j, s: (s, j))],
        out_specs=pl.BlockSpec((tk, tn), lambda i, j, s: (i, j)),
        out_shape=jax.ShapeDtypeStruct((k, n), F32),
        compiler_params=_params(("parallel", "parallel", "arbitrary")),
    )(a, g)


ROWS = 512


def _rms_fwd(h, g, *, name):
    s, d = h.shape

    def body(h_ref, g_ref, o_ref):
        x = h_ref[...]
        r = lax.rsqrt(jnp.mean(x * x, axis=-1, keepdims=True) + EPS)
        o_ref[...] = (x * r * g_ref[...]).astype(o_ref.dtype)

    return pl.pallas_call(
        body, name=name, grid=(s // ROWS,),
        in_specs=[pl.BlockSpec((ROWS, d), lambda i: (i, 0)), pl.BlockSpec((1, d), lambda i: (0, 0))],
        out_specs=pl.BlockSpec((ROWS, d), lambda i: (i, 0)),
        out_shape=jax.ShapeDtypeStruct((s, d), MXU_DTYPE),
        compiler_params=_params(("parallel",)),
    )(h, g.reshape(1, d))


def _rms_bwd(h, g, dxn, dres, *, name):
    s, d = h.shape

    def body(h_ref, g_ref, dxn_ref, dres_ref, dh_ref, dg_ref):
        @pl.when(pl.program_id(0) == 0)
        def _():
            dg_ref[...] = jnp.zeros_like(dg_ref)

        x = h_ref[...]
        r = lax.rsqrt(jnp.mean(x * x, axis=-1, keepdims=True) + EPS)
        xhat = x * r
        dxn = dxn_ref[...].astype(F32)
        dg_ref[...] += jnp.sum(dxn * xhat, axis=0, keepdims=True)
        dxh = dxn * g_ref[...]
        dh_ref[...] = dres_ref[...] + r * (dxh - xhat * jnp.mean(dxh * xhat, axis=-1, keepdims=True))

    row = pl.BlockSpec((ROWS, d), lambda i: (i, 0))
    vec = pl.BlockSpec((1, d), lambda i: (0, 0))
    return pl.pallas_call(
        body, name=name, grid=(s // ROWS,), in_specs=[row, vec, row, row], out_specs=[row, vec],
        out_shape=[jax.ShapeDtypeStruct((s, d), F32), jax.ShapeDtypeStruct((1, d), F32)],
        compiler_params=_params(("arbitrary",)),
    )(h, g.reshape(1, d), dxn, dres)


def _loss_head(h, g, target):
    s, d = h.shape

    def body(h_ref, g_ref, t_ref, loss_ref, dh_ref, dg_ref):
        @pl.when(pl.program_id(0) == 0)
        def _():
            loss_ref[...] = jnp.zeros_like(loss_ref)
            dg_ref[...] = jnp.zeros_like(dg_ref)

        x = h_ref[...]
        r = lax.rsqrt(jnp.mean(x * x, axis=-1, keepdims=True) + EPS)
        xhat = x * r
        err = xhat * g_ref[...] - t_ref[...]
        loss_ref[...] += 0.5 * jnp.sum(jnp.mean(err * err, axis=-1, keepdims=True), axis=0, keepdims=True)
        dy = err / d
        dg_ref[...] += jnp.sum(dy * xhat, axis=0, keepdims=True)
        dxh = dy * g_ref[...]
        dh_ref[...] = r * (dxh - xhat * jnp.mean(dxh * xhat, axis=-1, keepdims=True))

    row = pl.BlockSpec((ROWS, d), lambda i: (i, 0))
    vec = pl.BlockSpec((1, d), lambda i: (0, 0))
    one = pl.BlockSpec((1, 1), lambda i: (0, 0))
    return pl.pallas_call(
        body, name="loss_head", grid=(s // ROWS,), in_specs=[row, vec, row], out_specs=[one, row, vec],
        out_shape=[jax.ShapeDtypeStruct((1, 1), F32), jax.ShapeDtypeStruct((s, d), F32),
                   jax.ShapeDtypeStruct((1, d), F32)],
        compiler_params=_params(("arbitrary",)),
    )(h, g.reshape(1, d), target)


def _t5_bucket(dist):
    max_exact = N_BUCKETS // 2
    dd = np.maximum(dist, 0)
    large = max_exact + (np.log(np.maximum(dd, 1) / max_exact) / np.log(REL_MAX_DIST / max_exact)
                         * (N_BUCKETS - max_exact)).astype(np.int32)
    large = np.minimum(large, N_BUCKETS - 1)
    return np.where(dd < max_exact, dd, large).astype(np.int32)


def _bucket_table():
    qq = np.arange(QBLK)[:, None]
    kk = np.arange(QBLK)[None, :]
    out = np.zeros((len(BRANCH_DIL), 2, QBLK, QBLK), np.int32)
    for b, dil in enumerate(BRANCH_DIL):
        out[b, 0] = _t5_bucket((qq - kk + QBLK) * dil)
        out[b, 1] = _t5_bucket((qq - kk) * dil)
    return out


BIAS_TILE = 2 * QBLK


def _bias_build(rel_bias):
    idx = jnp.asarray(_bucket_table())

    def body(idx_ref, rb_ref, o_ref):
        ch = pl.program_id(1)
        row = lax.broadcasted_iota(jnp.int32, (QBLK, QBLK), 0)
        col = lax.broadcasted_iota(jnp.int32, (QBLK, QBLK), 1)
        for part in range(2):
            ids = idx_ref[0, 1 - part]
            valid = (col <= row) if part == 0 else (col >= row)
            for h in range(2):
                acc = jnp.zeros((QBLK, QBLK), F32)
                for b in range(N_BUCKETS):
                    acc = jnp.where(ids == b, rb_ref[b, 2 * ch + h], acc)
                o_ref[0, 0, QBLK * h:QBLK * (h + 1), QBLK * part:QBLK * (part + 1)] = jnp.where(valid, acc, NEG_INF)

    return pl.pallas_call(
        body, name="attn_bias_build", grid=(len(BRANCH_DIL), N_HEADS // 2),
        in_specs=[pl.BlockSpec((1, 2, QBLK, QBLK), lambda b, c: (b, 0, 0, 0)),
                  pl.BlockSpec(memory_space=pltpu.SMEM)],
        out_specs=pl.BlockSpec((1, 1, BIAS_TILE, BIAS_TILE), lambda b, c: (b, c, 0, 0)),
        out_shape=jax.ShapeDtypeStruct((len(BRANCH_DIL), N_HEADS // 2, BIAS_TILE, BIAS_TILE), F32),
        compiler_params=_params(("parallel", "parallel")),
    )(idx, rel_bias)


def _bias_reduce(dbias):
    idx = jnp.asarray(_bucket_table())
    nb = len(BRANCH_DIL)

    def body(idx_ref, d_ref, o_ref):
        def per_bucket(b, carry):
            for h in range(N_HEADS):
                tot = jnp.zeros((), F32)
                for br in range(nb):
                    for part in range(2):
                        tile = d_ref[br, h // 2, QBLK * (h % 2):QBLK * (h % 2 + 1), QBLK * part:QBLK * (part + 1)]
                        tot = tot + jnp.sum(jnp.where(idx_ref[br, 1 - part] == b, tile, 0.0))
                o_ref[b, h] = tot
            return carry

        lax.fori_loop(0, N_BUCKETS, per_bucket, 0)

    return pl.pallas_call(
        body, name="attn_bias_reduce",
        in_specs=[pl.BlockSpec(memory_space=pltpu.VMEM), pl.BlockSpec(memory_space=pltpu.VMEM)],
        out_specs=pl.BlockSpec(memory_space=pltpu.SMEM),
        out_shape=jax.ShapeDtypeStruct((N_BUCKETS, N_HEADS), F32),
        compiler_params=pltpu.CompilerParams(vmem_limit_bytes=VMEM_LIMIT_BYTES),
    )(idx, dbias)


def _band_masks(c):
    row = lax.broadcasted_iota(jnp.int32, (QBLK, QBLK), 0)
    col = lax.broadcasted_iota(jnp.int32, (QBLK, QBLK), 1)
    mask_cur = col <= row
    mask_prev = jnp.logical_and(col >= row, c > 0)
    return mask_prev, mask_cur


def _attn_specs(dil):
    blk = (QBLK, ATTN_W)
    q = pl.BlockSpec(blk, lambda r, c: (c, 3 * r))
    kp = pl.BlockSpec(blk, lambda r, c: (jnp.maximum(c - 1, 0), 3 * r + 1))
    kc = pl.BlockSpec(blk, lambda r, c: (c, 3 * r + 1))
    vp = pl.BlockSpec(blk, lambda r, c: (jnp.maximum(c - 1, 0), 3 * r + 2))
    vc = pl.BlockSpec(blk, lambda r, c: (c, 3 * r + 2))
    return [q, kp, kc, vp, vc]


def _attn_fwd_branch(qkv, bias, state, *, branch, last):
    dil = BRANCH_DIL[branch]
    s = qkv.shape[0]
    n = s // dil
    nblk = n // QBLK
    first = state is None

    def body(*refs):
        q_ref, kp_ref, kc_ref, vp_ref, vc_ref, b_ref = refs[:6]
        if first:
            outs = refs[6:]
        else:
            acc_ref, m_ref, l_ref = refs[6:9]
            outs = refs[9:]
        mask_prev, mask_cur = _band_masks(pl.program_id(1))
        for h in range(N_HEADS):
            sl = slice(HEAD_DIM * h, HEAD_DIM * (h + 1))
            qh = q_ref[:, sl]
            s_c = _dot(qh, kc_ref[:, sl], NT) * ATTN_SCALE + b_ref[0, 1, h]
            s_p = _dot(qh, kp_ref[:, sl], NT) * ATTN_SCALE + b_ref[0, 0, h]
            s_c = jnp.where(mask_cur, s_c, NEG_INF)
            s_p = jnp.where(mask_prev, s_p, NEG_INF)
            m_blk = jnp.maximum(jnp.max(s_c, axis=-1, keepdims=True), jnp.max(s_p, axis=-1, keepdims=True))
            if first:
                m_new = m_blk
            else:
                m_old = m_ref[:, sl][:, :1]
                m_new = jnp.maximum(m_old, m_blk)
            p_c = jnp.exp(s_c - m_new)
            p_p = jnp.exp(s_p - m_new)
            l_new = jnp.sum(p_c, axis=-1, keepdims=True) + jnp.sum(p_p, axis=-1, keepdims=True)
            acc = (_dot(p_c.astype(MXU_DTYPE), vc_ref[:, sl], NN)
                   + _dot(p_p.astype(MXU_DTYPE), vp_ref[:, sl], NN))
            if not first:
                alpha = jnp.exp(m_old - m_new)
                l_new = l_new + alpha * l_ref[:, sl][:, :1]
                acc = acc + alpha * acc_ref[:, sl]
            if last:
                outs[0][:, sl] = acc / l_new
                outs[1][:, sl] = jnp.broadcast_to(m_new + jnp.log(l_new), (QBLK, HEAD_DIM))
            else:
                outs[0][:, sl] = acc
                outs[1][:, sl] = jnp.broadcast_to(m_new, (QBLK, HEAD_DIM))
                outs[2][:, sl] = jnp.broadcast_to(l_new, (QBLK, HEAD_DIM))

    st_spec = pl.BlockSpec((QBLK, ATTN_W), lambda r, c: (c, r))
    in_specs = _attn_specs(dil) + [pl.BlockSpec((1, 2, N_HEADS, QBLK, QBLK), lambda r, c: (branch, 0, 0, 0, 0))]
    qv = qkv.reshape(n, dil * 3 * ATTN_W)
    args = [qv] * 5 + [bias]
    if not first:
        in_specs += [st_spec] * 3
        args += [t.reshape(n, dil * ATTN_W) for t in state]
    n_out = 2 if last else 3
    outs = pl.pallas_call(
        body, name=f"attn_fwd_b{branch}", grid=(dil, nblk), in_specs=in_specs,
        out_specs=[st_spec] * n_out,
        out_shape=[jax.ShapeDtypeStruct((n, dil * ATTN_W), F32)] * n_out,
        compiler_params=_params(("parallel", "parallel")),
    )(*args)
    return tuple(t.reshape(s, ATTN_W) for t in outs)


def _attn_fwd(qkv, bias):
    state = None
    for b in range(len(BRANCH_DIL)):
        state = _attn_fwd_branch(qkv, bias, state, branch=b, last=(b == len(BRANCH_DIL) - 1))
    return state


def _attn_bwd_branch(qkv, bias, o, lse, do, *, branch):
    dil = BRANCH_DIL[branch]
    s = qkv.shape[0]
    n = s // dil
    nblk = n // QBLK

    def body(q_ref, kp_ref, kc_ref, vp_ref, vc_ref, b_ref, o_ref, l_ref, do_ref,
             dq_ref, dka_ref, dkb_ref, dva_ref, dvb_ref, db_ref):
        @pl.when(jnp.logical_and(pl.program_id(0) == 0, pl.program_id(1) == 0))
        def _():
            db_ref[...] = jnp.zeros_like(db_ref)

        mask_prev, mask_cur = _band_masks(pl.program_id(1))
        for h in range(N_HEADS):
            sl = slice(HEAD_DIM * h, HEAD_DIM * (h + 1))
            qh = q_ref[:, sl]
            doh = do_ref[:, sl]
            lh = l_ref[:, sl][:, :1]
            delta = jnp.sum(doh * o_ref[:, sl], axis=-1, keepdims=True)
            do_m = doh.astype(MXU_DTYPE)
            s_c = _dot(qh, kc_ref[:, sl], NT) * ATTN_SCALE + b_ref[0, 1, h]
            s_p = _dot(qh, kp_ref[:, sl], NT) * ATTN_SCALE + b_ref[0, 0, h]
            p_c = jnp.exp(jnp.where(mask_cur, s_c, NEG_INF) - lh)
            p_p = jnp.exp(jnp.where(mask_prev, s_p, NEG_INF) - lh)
            ds_c = p_c * (_dot(do_m, vc_ref[:, sl], NT) - delta)
            ds_p = p_p * (_dot(do_m, vp_ref[:, sl], NT) - delta)
            db_ref[0, 1, h] += ds_c
            db_ref[0, 0, h] += ds_p
            ds_c_m = ds_c.astype(MXU_DTYPE)
            ds_p_m = ds_p.astype(MXU_DTYPE)
            dq = _dot(ds_c_m, kc_ref[:, sl], NN) + _dot(ds_p_m, kp_ref[:, sl], NN)
            dq_ref[:, sl] = (dq * ATTN_SCALE).astype(dq_ref.dtype)
            dka_ref[:, sl] = (_dot(ds_c_m, qh, TN) * ATTN_SCALE).astype(dka_ref.dtype)
            dkb_ref[:, sl] = (_dot(ds_p_m, qh, TN) * ATTN_SCALE).astype(dkb_ref.dtype)
            dva_ref[:, sl] = _dot(p_c.astype(MXU_DTYPE), do_m, TN).astype(dva_ref.dtype)
            dvb_ref[:, sl] = _dot(p_p.astype(MXU_DTYPE), do_m, TN).astype(dvb_ref.dtype)

    st_spec = pl.BlockSpec((QBLK, ATTN_W), lambda r, c: (c, r))
    b_in = pl.BlockSpec((1, 2, N_HEADS, QBLK, QBLK), lambda r, c: (branch, 0, 0, 0, 0))
    b_out = pl.BlockSpec((1, 2, N_HEADS, QBLK, QBLK), lambda r, c: (0, 0, 0, 0, 0))
    qv = qkv.reshape(n, dil * 3 * ATTN_W)
    view = lambda t: t.reshape(n, dil * ATTN_W)
    outs = pl.pallas_call(
        body, name=f"attn_bwd_b{branch}", grid=(dil, nblk),
        in_specs=_attn_specs(dil) + [b_in, st_spec, st_spec, st_spec],
        out_specs=[st_spec] * 5 + [b_out],
        out_shape=[jax.ShapeDtypeStruct((n, dil * ATTN_W), MXU_DTYPE)] * 5
        + [jax.ShapeDtypeStruct((1, 2, N_HEADS, QBLK, QBLK), F32)],
        compiler_params=_params(("arbitrary", "arbitrary")),
    )(qv, qv, qv, qv, qv, bias, view(o), view(lse), view(do))
    return tuple(t.reshape(s, ATTN_W) for t in outs[:5]) + (outs[5],)


def _attn_bwd(qkv, bias, o, lse, do):
    s = qkv.shape[0]
    nb = s // QBLK
    parts = [_attn_bwd_branch(qkv, bias, o, lse, do, branch=b) for b in range(len(BRANCH_DIL))]
    dbias = jnp.concatenate([p[5] for p in parts], axis=0)

    def body(*refs):
        o_ref = refs[-1]
        i = pl.program_id(0)
        dq = jnp.zeros((QBLK, ATTN_W), F32)
        dk = jnp.zeros((QBLK, ATTN_W), F32)
        dv = jnp.zeros((QBLK, ATTN_W), F32)
        for b, dil in enumerate(BRANCH_DIL):
            dq_ref, dka_ref, dkb_ref, dva_ref, dvb_ref = refs[5 * b:5 * b + 5]
            inside = i + dil < nb
            dq = dq + dq_ref[...].astype(F32)
            dk = dk + dka_ref[...].astype(F32) + jnp.where(inside, dkb_ref[...].astype(F32), 0.0)
            dv = dv + dva_ref[...].astype(F32) + jnp.where(inside, dvb_ref[...].astype(F32), 0.0)
        o_ref[:, 0:ATTN_W] = dq.astype(o_ref.dtype)
        o_ref[:, ATTN_W:2 * ATTN_W] = dk.astype(o_ref.dtype)
        o_ref[:, 2 * ATTN_W:3 * ATTN_W] = dv.astype(o_ref.dtype)

    in_specs, args = [], []
    for b, dil in enumerate(BRANCH_DIL):
        here = pl.BlockSpec((QBLK, ATTN_W), lambda i: (i, 0))
        ahead = pl.BlockSpec((QBLK, ATTN_W), functools.partial(lambda i, d: (jnp.minimum(i + d, nb - 1), 0), d=dil))
        in_specs += [here, here, ahead, here, ahead]
        args += list(parts[b][:5])
    dqkv = pl.pallas_call(
        body, name="attn_bwd_sum", grid=(nb,), in_specs=in_specs,
        out_specs=pl.BlockSpec((QBLK, 3 * ATTN_W), lambda i: (i, 0)),
        out_shape=jax.ShapeDtypeStruct((s, 3 * ATTN_W), MXU_DTYPE),
        compiler_params=_params(("parallel",)),
    )(*args)
    return dqkv, dbias


ATTN_IO_DTYPE = F32
ABLK = 2048
N_CHUNK = ATTN_W // 128


def _rows(start, dil):
    if dil > 1:
        return pl.ds(start, QBLK, stride=dil)
    return pl.ds(pl.multiple_of(start, QBLK), QBLK)


def _low_head():
    return lax.broadcasted_iota(jnp.int32, (QBLK, 128), 1) < HEAD_DIM


def _head_split(t):
    low = _low_head()
    zero = jnp.zeros_like(t)
    return jnp.where(low, t, zero), jnp.where(low, zero, t)


def _tile_bias(b_ref, branch, first):
    bias = b_ref[branch]
    if first is None:
        return bias
    col = lax.broadcasted_iota(jnp.int32, (BIAS_TILE, BIAS_TILE), 1)
    return jnp.where(jnp.logical_and(first, col >= QBLK), NEG_INF, bias)


def _loop(n, fn):
    if n == 1:
        fn(jnp.int32(0), 0)
    elif n > 1:
        lax.fori_loop(0, n, fn, 0, unroll=2)


def _for_each_tile(tile, c):
    for branch, dil in enumerate(BRANCH_DIL):
        span = QBLK * dil

        def edge(r, carry, branch=branch, span=span):
            tile(branch, r, False, ABLK - span + r, c == 0)
            return carry

        def inner(t, carry, branch=branch, span=span, dil=dil):
            start = (1 + t // dil) * span + t % dil
            tile(branch, start, True, start - span, None)
            return carry

        _loop(dil, edge)
        _loop((ABLK // span - 1) * dil, inner)


def _attn_chunk_specs(nb):
    blk = (None, ABLK, 128)
    prev = lambda c: jnp.maximum(c - 1, 0)
    return [pl.BlockSpec(blk, lambda ch, c: (ch, c, 0)),
            pl.BlockSpec(blk, lambda ch, c: (N_CHUNK + ch, c, 0)),
            pl.BlockSpec(blk, lambda ch, c: (2 * N_CHUNK + ch, c, 0)),
            pl.BlockSpec(blk, lambda ch, c: (N_CHUNK + ch, prev(c), 0)),
            pl.BlockSpec(blk, lambda ch, c: (2 * N_CHUNK + ch, prev(c), 0)),
            pl.BlockSpec((len(BRANCH_DIL), None, BIAS_TILE, BIAS_TILE), lambda ch, c: (0, ch, 0, 0))]


def _in_proj(xn, w_in, layer):
    s, k = xn.shape
    tm = 512
    nch = O_SGU // 128

    def body(x_ref, w_ref, qkv_ref, zs_ref, us_ref):
        acc = _dot(x_ref[...].astype(MXU_DTYPE), w_ref[...].astype(MXU_DTYPE), NN)
        for j in range(nch):
            blk = acc[:, 128 * j:128 * (j + 1)]
            if j < N_CHUNK:
                blk = blk * ATTN_SCALE
            qkv_ref[j] = blk.astype(qkv_ref.dtype)
        zs_ref[...] = acc[:, O_SGU:O_SSM]
        us_ref[...] = acc[:, O_SSM:]

    n = w_in.shape[-1]
    return pl.pallas_call(
        body, name="in_proj", grid=(s // tm,),
        in_specs=[pl.BlockSpec((tm, k), lambda i: (i, 0)), pl.BlockSpec((None, k, n), lambda i: (layer, 0, 0))],
        out_specs=[pl.BlockSpec((nch, tm, 128), lambda i: (0, i, 0)),
                   pl.BlockSpec((tm, O_SSM - O_SGU), lambda i: (i, 0)), pl.BlockSpec((tm, n - O_SSM), lambda i: (i, 0))],
        out_shape=[jax.ShapeDtypeStruct((nch, s, 128), ATTN_IO_DTYPE),
                   jax.ShapeDtypeStruct((s, O_SSM - O_SGU), F32), jax.ShapeDtypeStruct((s, n - O_SSM), F32)],
        compiler_params=_params(("parallel",)),
    )(xn, w_in)


def _attn2_fwd(qkv_c, bias):
    s = qkv_c.shape[1]
    nb = s // ABLK
    last = len(BRANCH_DIL) - 1

    def body(q_ref, kc_ref, vc_ref, kp_ref, vp_ref, b_ref, o_ref, l_ref, acc_s, m_s, l_s):
        low = _low_head()
        e_st = jnp.concatenate(_head_split(jnp.ones((QBLK, 128), MXU_DTYPE)) * 2, axis=0)

        def tile(branch, start, prev_in_block, pstart, first):
            dil = BRANCH_DIL[branch]
            rq, rp = _rows(start, dil), _rows(pstart, dil)
            k_ref, v_ref = (kc_ref, vc_ref) if prev_in_block else (kp_ref, vp_ref)
            q_st = jnp.concatenate(_head_split(q_ref[rq, :].astype(MXU_DTYPE)), axis=0)
            k_st = jnp.concatenate([kc_ref[rq, :].astype(MXU_DTYPE), k_ref[rp, :].astype(MXU_DTYPE)], axis=0)
            v_st = jnp.concatenate(_head_split(vc_ref[rq, :].astype(MXU_DTYPE))
                                   + _head_split(v_ref[rp, :].astype(MXU_DTYPE)), axis=0)
            sc = _dot(q_st, k_st, NT) + _tile_bias(b_ref, branch, first)
            m_new = jnp.max(sc, axis=-1, keepdims=True)
            if branch > 0:
                m_old2 = m_s[rq, :]
                m_old = jnp.concatenate([m_old2[:, 0:1], m_old2[:, HEAD_DIM:HEAD_DIM + 1]], axis=0)
                m_new = jnp.maximum(m_old, m_new)
                alpha = jnp.exp(m_old - m_new)
            p = jnp.exp(sc - m_new).astype(MXU_DTYPE)
            lhs = jnp.concatenate([p[:QBLK, :QBLK], p[QBLK:, :QBLK], p[:QBLK, QBLK:], p[QBLK:, QBLK:]], axis=1)
            acc2 = _dot(lhs, v_st, NN)
            sum2 = _dot(lhs, e_st, NN)
            m2 = jnp.where(low, m_new[:QBLK], m_new[QBLK:])
            if branch > 0:
                a2 = jnp.where(low, alpha[:QBLK], alpha[QBLK:])
                acc2 = acc2 + a2 * acc_s[rq, :]
                sum2 = sum2 + a2 * l_s[rq, :]
            if branch == last:
                o_ref[rq, :] = acc2 / sum2
                l_ref[rq, :] = m2 + jnp.log(sum2)
            else:
                acc_s[rq, :] = acc2
                m_s[rq, :] = m2
                l_s[rq, :] = sum2

        _for_each_tile(tile, pl.program_id(1))

    out_spec = pl.BlockSpec((None, ABLK, 128), lambda ch, c: (ch, c, 0))
    return pl.pallas_call(
        body, name="attn_fwd", grid=(N_CHUNK, nb), in_specs=_attn_chunk_specs(nb),
        out_specs=[out_spec, out_spec],
        out_shape=[jax.ShapeDtypeStruct((N_CHUNK, s, 128), F32)] * 2,
        scratch_shapes=[pltpu.VMEM((ABLK, 128), F32)] * 3,
        compiler_params=_params(("parallel", "arbitrary")),
    )(qkv_c, qkv_c, qkv_c, qkv_c, qkv_c, bias)


def _attn2_bwd(qkv_c, bias, lse_c, delta_c, do_c):
    s = qkv_c.shape[1]
    nb = s // ABLK
    nbr = len(BRANCH_DIL)

    def body(q_ref, kc_ref, vc_ref, kp_ref, vp_ref, b_ref, l_ref, dl_ref, do_ref,
             dq_ref, dk_ref, dv_ref, *rest):
        ek_refs, ev_refs, db_ref = rest[:nbr], rest[nbr:2 * nbr], rest[2 * nbr]
        c = pl.program_id(1)

        @pl.when(c == 0)
        def _():
            db_ref[...] = jnp.zeros_like(db_ref)

        for r in (dq_ref, dk_ref, dv_ref) + tuple(ek_refs) + tuple(ev_refs):
            r[...] = jnp.zeros_like(r)

        def tile(branch, start, prev_in_block, pstart, first):
            dil = BRANCH_DIL[branch]
            rq, rp = _rows(start, dil), _rows(pstart, dil)
            k_ref, v_ref = (kc_ref, vc_ref) if prev_in_block else (kp_ref, vp_ref)
            kc2 = kc_ref[rq, :].astype(MXU_DTYPE)
            kp2 = k_ref[rp, :].astype(MXU_DTYPE)
            q_st = jnp.concatenate(_head_split(q_ref[rq, :].astype(MXU_DTYPE)), axis=0)
            do_st = jnp.concatenate(_head_split(do_ref[rq, :].astype(MXU_DTYPE)), axis=0)
            k_st = jnp.concatenate([kc2, kp2], axis=0)
            v_st = jnp.concatenate([vc_ref[rq, :].astype(MXU_DTYPE), v_ref[rp, :].astype(MXU_DTYPE)], axis=0)
            kh_st = jnp.concatenate(_head_split(kc2) + _head_split(kp2), axis=0)
            lse2 = l_ref[rq, :]
            del2 = dl_ref[rq, :]
            lse_st = jnp.concatenate([lse2[:, 0:1], lse2[:, HEAD_DIM:HEAD_DIM + 1]], axis=0)
            del_st = jnp.concatenate([del2[:, 0:1], del2[:, HEAD_DIM:HEAD_DIM + 1]], axis=0)
            p = jnp.exp(_dot(q_st, k_st, NT) + _tile_bias(b_ref, branch, first) - lse_st)
            ds = p * (_dot(do_st, v_st, NT) - del_st)
            db_ref[branch] += ds
            ds = ds.astype(MXU_DTYPE)
            p = p.astype(MXU_DTYPE)
            lhs = jnp.concatenate([ds[:QBLK, :QBLK], ds[QBLK:, :QBLK], ds[:QBLK, QBLK:], ds[QBLK:, QBLK:]], axis=1)
            dk_st = _dot(ds, q_st, TN)
            dv_st = _dot(p, do_st, TN)
            dq_ref[rq, :] += _dot(lhs, kh_st, NN)
            dk_ref[rq, :] += dk_st[:QBLK]
            dv_ref[rq, :] += dv_st[:QBLK]
            if prev_in_block:
                dk_ref[rp, :] += dk_st[QBLK:]
                dv_ref[rp, :] += dv_st[QBLK:]
            else:
                ek_refs[branch][rq, :] = dk_st[QBLK:]
                ev_refs[branch][rq, :] = dv_st[QBLK:]

        _for_each_tile(tile, c)

    blk = pl.BlockSpec((None, ABLK, 128), lambda ch, c: (ch, c, 0))
    outs = pl.pallas_call(
        body, name="attn_bwd", grid=(N_CHUNK, nb), in_specs=_attn_chunk_specs(nb) + [blk, blk, blk],
        out_specs=[blk] * (3 + 2 * nbr) + [pl.BlockSpec((nbr, None, BIAS_TILE, BIAS_TILE), lambda ch, c: (0, ch, 0, 0))],
        out_shape=[jax.ShapeDtypeStruct((N_CHUNK, s, 128), F32)] * (3 + 2 * nbr)
        + [jax.ShapeDtypeStruct((nbr, N_HEADS // 2, BIAS_TILE, BIAS_TILE), F32)],
        compiler_params=_params(("arbitrary", "arbitrary")),
    )(qkv_c, qkv_c, qkv_c, qkv_c, qkv_c, bias, lse_c, delta_c, do_c)
    return outs[0], outs[1], outs[2], outs[3:3 + nbr], outs[3 + nbr:3 + 2 * nbr], outs[3 + 2 * nbr]


def _attn2_bwd_sum(dq, dk, dv, ek, ev, dzs, dus):
    s = dq.shape[1]
    nrb = s // QBLK
    per_blk = ABLK // QBLK
    nbr = len(BRANCH_DIL)

    def body(*refs):
        dq_ref, dk_ref, dv_ref = refs[:3]
        ek_refs, ev_refs = refs[3:3 + nbr], refs[3 + nbr:3 + 2 * nbr]
        dzs_ref, dus_ref, o_ref = refs[3 + 2 * nbr:]
        i = pl.program_id(0)
        dkt, dvt = dk_ref[...], dv_ref[...]
        for b, dil in enumerate(BRANCH_DIL):
            j = i + dil
            ok = jnp.logical_and(j < nrb, j % per_blk < dil)
            dkt = dkt + jnp.where(ok, ek_refs[b][...], 0.0)
            dvt = dvt + jnp.where(ok, ev_refs[b][...], 0.0)
        for ch in range(N_CHUNK):
            o_ref[:, 128 * ch:128 * (ch + 1)] = (dq_ref[ch] * ATTN_SCALE).astype(o_ref.dtype)
            o_ref[:, ATTN_W + 128 * ch:ATTN_W + 128 * (ch + 1)] = dkt[ch].astype(o_ref.dtype)
            o_ref[:, 2 * ATTN_W + 128 * ch:2 * ATTN_W + 128 * (ch + 1)] = dvt[ch].astype(o_ref.dtype)
        o_ref[:, O_SGU:O_SSM] = dzs_ref[...].astype(o_ref.dtype)
        o_ref[:, O_SSM:] = dus_ref[...].astype(o_ref.dtype)

    here = pl.BlockSpec((N_CHUNK, QBLK, 128), lambda i: (0, i, 0))
    edge_specs = [pl.BlockSpec((N_CHUNK, QBLK, 128),
                               functools.partial(lambda i, d: (0, jnp.minimum(i + d, nrb - 1), 0), d=dil))
                  for dil in BRANCH_DIL]
    return pl.pallas_call(
        body, name="attn_bwd_sum", grid=(nrb,),
        in_specs=[here, here, here] + edge_specs + edge_specs
        + [pl.BlockSpec((QBLK, 2 * SGU_W), lambda i: (i, 0)), pl.BlockSpec((QBLK, SSM_W), lambda i: (i, 0))],
        out_specs=pl.BlockSpec((QBLK, O_SSM + SSM_W), lambda i: (i, 0)),
        out_shape=jax.ShapeDtypeStruct((s, O_SSM + SSM_W), MXU_DTYPE),
        compiler_params=_params(("parallel",)),
    )(dq, dk, dv, *ek, *ev, dzs, dus)


SGU_ROWS = 512


def _sgu_norm(v_g):
    mu = jnp.mean(v_g, axis=-1, keepdims=True)
    cen = v_g - mu
    var = jnp.mean(cen * cen, axis=-1, keepdims=True)
    rstd = lax.rsqrt(var + EPS)
    return cen * rstd, rstd


def _sgu_fwd(zs, ln_g, ln_b, w_mask, b_t):
    s = zs.shape[0]
    nch = SGU_ROWS // SGU_CHUNK

    def body(z_ref, g_ref, b_ref, w_ref, bt_ref, o_ref):
        gz = _gelu(z_ref[...])
        for g in range(SGU_G):
            sl = slice(SGU_GW * g, SGU_GW * (g + 1))
            u_g = gz[:, sl]
            xhat, _ = _sgu_norm(gz[:, SGU_W + SGU_GW * g:SGU_W + SGU_GW * (g + 1)])
            vn = (xhat * g_ref[:, sl] + b_ref[:, sl]).astype(MXU_DTYPE)
            wg = w_ref[g].astype(MXU_DTYPE)
            for ci in range(nch):
                rs = slice(SGU_CHUNK * ci, SGU_CHUNK * (ci + 1))
                mixed = _dot(wg, vn[rs], NN) + bt_ref[:, g:g + 1]
                o_ref[rs, sl] = u_g[rs] * mixed

    full = lambda shape: pl.BlockSpec(shape, lambda i: tuple(0 for _ in shape))
    return pl.pallas_call(
        body, name="sgu_fwd", grid=(s // SGU_ROWS,),
        in_specs=[pl.BlockSpec((SGU_ROWS, 2 * SGU_W), lambda i: (i, 0)), full((1, SGU_W)), full((1, SGU_W)),
                  full((SGU_G, SGU_CHUNK, SGU_CHUNK)), full((SGU_CHUNK, SGU_G))],
        out_specs=pl.BlockSpec((SGU_ROWS, SGU_W), lambda i: (i, 0)),
        out_shape=jax.ShapeDtypeStruct((s, SGU_W), F32),
        compiler_params=_params(("parallel",)),
    )(zs, ln_g.reshape(1, SGU_W), ln_b.reshape(1, SGU_W), w_mask, b_t)


def _sgu_bwd(zs, ln_g, ln_b, w_mask, b_t, dy):
    s = zs.shape[0]
    nch = SGU_ROWS // SGU_CHUNK

    def body(z_ref, g_ref, b_ref, w_ref, bt_ref, dy_ref, dz_ref, dg_ref, dbb_ref, dw_ref, dbt_ref):
        @pl.when(pl.program_id(0) == 0)
        def _():
            dg_ref[...] = jnp.zeros_like(dg_ref)
            dbb_ref[...] = jnp.zeros_like(dbb_ref)
            dw_ref[...] = jnp.zeros_like(dw_ref)
            dbt_ref[...] = jnp.zeros_like(dbt_ref)

        z = z_ref[...]
        gz, dgelu = _gelu_pair(z)
        dy = dy_ref[...]
        for g in range(SGU_G):
            sl = slice(SGU_GW * g, SGU_GW * (g + 1))
            sv = slice(SGU_W + SGU_GW * g, SGU_W + SGU_GW * (g + 1))
            u_g = gz[:, sl]
            xhat, rstd = _sgu_norm(gz[:, sv])
            gain = g_ref[:, sl]
            vn = (xhat * gain + b_ref[:, sl]).astype(MXU_DTYPE)
            wg = w_ref[g].astype(MXU_DTYPE)
            dy_g = dy[:, sl]
            dvn_parts = []
            for ci in range(nch):
                rs = slice(SGU_CHUNK * ci, SGU_CHUNK * (ci + 1))
                mixed = _dot(wg, vn[rs], NN) + bt_ref[:, g:g + 1]
                dz_ref[rs, sl] = (dy_g[rs] * mixed * dgelu[rs, sl]).astype(dz_ref.dtype)
                dmixed = dy_g[rs] * u_g[rs]
                dm = dmixed.astype(MXU_DTYPE)
                dvn_parts.append(_dot(wg, dm, TN))
                dw_ref[g] += _dot(dm, vn[rs], NT)
                dbt_ref[:, g:g + 1] += jnp.sum(dmixed, axis=-1, keepdims=True)
            dvn = jnp.concatenate(dvn_parts, axis=0)
            dg_ref[:, sl] += jnp.sum(dvn * xhat, axis=0, keepdims=True)
            dbb_ref[:, sl] += jnp.sum(dvn, axis=0, keepdims=True)
            dxh = dvn * gain
            dv = rstd * (dxh - jnp.mean(dxh, axis=-1, keepdims=True)
                         - xhat * jnp.mean(dxh * xhat, axis=-1, keepdims=True))
            dz_ref[:, sv] = (dv * dgelu[:, sv]).astype(dz_ref.dtype)

    full = lambda shape: pl.BlockSpec(shape, lambda i: tuple(0 for _ in shape))
    return pl.pallas_call(
        body, name="sgu_bwd", grid=(s // SGU_ROWS,),
        in_specs=[pl.BlockSpec((SGU_ROWS, 2 * SGU_W), lambda i: (i, 0)), full((1, SGU_W)), full((1, SGU_W)),
                  full((SGU_G, SGU_CHUNK, SGU_CHUNK)), full((SGU_CHUNK, SGU_G)),
                  pl.BlockSpec((SGU_ROWS, SGU_W), lambda i: (i, 0))],
        out_specs=[pl.BlockSpec((SGU_ROWS, 2 * SGU_W), lambda i: (i, 0)), full((1, SGU_W)), full((1, SGU_W)),
                   full((SGU_G, SGU_CHUNK, SGU_CHUNK)), full((SGU_CHUNK, SGU_G))],
        out_shape=[jax.ShapeDtypeStruct((s, 2 * SGU_W), MXU_DTYPE), jax.ShapeDtypeStruct((1, SGU_W), F32),
                   jax.ShapeDtypeStruct((1, SGU_W), F32), jax.ShapeDtypeStruct((SGU_G, SGU_CHUNK, SGU_CHUNK), F32),
                   jax.ShapeDtypeStruct((SGU_CHUNK, SGU_G), F32)],
        compiler_params=_params(("arbitrary",)),
    )(zs, ln_g.reshape(1, SGU_W), ln_b.reshape(1, SGU_W), w_mask, b_t, dy)


def _ssm_discretize(a_re, a_im, log_dt, b_re, b_im):
    dt = jnp.exp(log_dt)[:, None]
    mag = jnp.exp(a_re * dt)
    ab_re = mag * jnp.cos(a_im * dt)
    ab_im = mag * jnp.sin(a_im * dt)
    den = a_re * a_re + a_im * a_im
    f_re = ((ab_re - 1.0) * a_re + ab_im * a_im) / den
    f_im = (ab_im * a_re - (ab_re - 1.0) * a_im) / den
    bb_re = f_re[:, :, None] * b_re - f_im[:, :, None] * b_im
    bb_im = f_re[:, :, None] * b_im + f_im[:, :, None] * b_re
    return ab_re, ab_im, bb_re, bb_im


def _ssm_operands(a_re, a_im, log_dt, b_re, b_im, c_re, c_im):
    ab_re, ab_im, bb_re, bb_im = _ssm_discretize(a_re, a_im, log_dt, b_re, b_im)
    eye = jnp.eye(SSM_G, dtype=F32)
    b_blk = jnp.einsum("pgnc,gh->gcphn", jnp.stack([bb_re, bb_im]), eye).reshape(SSM_W, 2 * NSTATE)
    c_mat = jnp.einsum("pgcn,gh->pgnhc", jnp.stack([c_re, -c_im]), eye).reshape(2 * NSTATE, SSM_W)
    a_row = jnp.stack([ab_re.reshape(NSTATE), ab_im.reshape(NSTATE)])
    p_re, p_im = a_row[0:1], a_row[1:2]
    while p_re.shape[0] < SSM_TSEG:
        l_re, l_im = p_re[-1:], p_im[-1:]
        p_re, p_im = (jnp.concatenate([p_re, p_re * l_re - p_im * l_im]),
                      jnp.concatenate([p_im, p_re * l_im + p_im * l_re]))
    p_tab = jnp.stack([p_re, p_im])
    return b_blk.astype(MXU_DTYPE), c_mat.astype(MXU_DTYPE), a_row, p_tab


def _lane_chunks():
    return [(lo, lo + SSM_LANE_CHUNK) for lo in range(0, NSTATE, SSM_LANE_CHUNK)]


def _seg_rows(j):
    return pl.ds(pl.multiple_of(j * SSM_NSEG, SSM_NSEG), SSM_NSEG)


def _to_segments(t):
    s, w = t.shape
    return t.reshape(s // SSM_TB, SSM_NSEG, SSM_TSEG, w).transpose(0, 2, 1, 3).reshape(s, w)


def _from_segments(t):
    s, w = t.shape
    return t.reshape(s // SSM_TB, SSM_TSEG, SSM_NSEG, w).transpose(0, 2, 1, 3).reshape(s, w)


def _ssm_local_scan(buf, a_ref, *, reverse):
    ends_re, ends_im = [], []
    for lo, hi in _lane_chunks():
        are = jnp.broadcast_to(a_ref[0:1, lo:hi], (SSM_NSEG, hi - lo))
        aim = jnp.broadcast_to(a_ref[1:2, lo:hi], (SSM_NSEG, hi - lo))
        if reverse:
            aim = -aim

        def step(jj, carry, lo=lo, hi=hi, are=are, aim=aim):
            xr, xi = carry
            j = (SSM_TSEG - 1 - jj) if reverse else jj
            tr = buf[_seg_rows(j), lo:hi]
            ti = buf[_seg_rows(j), NSTATE + lo:NSTATE + hi]
            nr = are * xr - aim * xi + tr
            ni = are * xi + aim * xr + ti
            buf[_seg_rows(j), lo:hi] = nr
            buf[_seg_rows(j), NSTATE + lo:NSTATE + hi] = ni
            return nr, ni

        zero = jnp.zeros((SSM_NSEG, hi - lo), F32)
        xr, xi = lax.fori_loop(0, SSM_TSEG, step, (zero, zero), unroll=4)
        ends_re.append(xr)
        ends_im.append(xi)
    return jnp.concatenate(ends_re, axis=1), jnp.concatenate(ends_im, axis=1)


def _ssm_entry_states(ends_re, ends_im, carry_ref, p_ref, entry_ref, *, reverse):
    at_re = p_ref[0, SSM_TSEG - 1:SSM_TSEG, :]
    at_im = p_ref[1, SSM_TSEG - 1:SSM_TSEG, :]
    if reverse:
        at_im = -at_im
    cur_re = carry_ref[0:1, 0:NSTATE]
    cur_im = carry_ref[0:1, NSTATE:2 * NSTATE]
    order = range(SSM_NSEG - 1, -1, -1) if reverse else range(SSM_NSEG)
    for i in order:
        entry_ref[0, i:i + 1, 0:NSTATE] = cur_re
        entry_ref[0, i:i + 1, NSTATE:2 * NSTATE] = cur_im
        nxt_re = ends_re[i:i + 1] + at_re * cur_re - at_im * cur_im
        nxt_im = ends_im[i:i + 1] + at_re * cur_im + at_im * cur_re
        cur_re, cur_im = nxt_re, nxt_im
    carry_ref[0:1, 0:NSTATE] = cur_re
    carry_ref[0:1, NSTATE:2 * NSTATE] = cur_im


def _ssm_fixup(buf, p_ref, entry_ref, *, reverse):
    for lo, hi in _lane_chunks():
        e_re = entry_ref[0, :, lo:hi]
        e_im = entry_ref[0, :, NSTATE + lo:NSTATE + hi]

        def step(j, carry, lo=lo, hi=hi, e_re=e_re, e_im=e_im):
            jp = (SSM_TSEG - 1 - j) if reverse else j
            pr = p_ref[0, pl.ds(jp, 1), lo:hi]
            pi = p_ref[1, pl.ds(jp, 1), lo:hi]
            if reverse:
                pi = -pi
            buf[_seg_rows(j), lo:hi] = buf[_seg_rows(j), lo:hi] + pr * e_re - pi * e_im
            buf[_seg_rows(j), NSTATE + lo:NSTATE + hi] = (buf[_seg_rows(j), NSTATE + lo:NSTATE + hi]
                                                           + pr * e_im + pi * e_re)
            return carry

        lax.fori_loop(0, SSM_TSEG, step, 0, unroll=4)


def _ssm_fwd(u, ops, d_skip, glu_w, glu_b):
    b_blk, c_mat, a_row, p_tab = ops
    s = u.shape[0]
    nblk = s // SSM_TB

    def body(u_ref, bb_ref, cm_ref, a_ref, p_ref, d_ref, gw_ref, gb_ref, y_ref, entry_ref, xbuf, carry):
        @pl.when(pl.program_id(0) == 0)
        def _():
            carry[...] = jnp.zeros_like(carry)

        uu = u_ref[...]
        xbuf[...] = _dotf(uu, bb_ref[...], NN)
        ends_re, ends_im = _ssm_local_scan(xbuf, a_ref, reverse=False)
        _ssm_entry_states(ends_re, ends_im, carry, p_ref, entry_ref, reverse=False)
        _ssm_fixup(xbuf, p_ref, entry_ref, reverse=False)
        y = _dotf(xbuf[...],cm_ref[...], NN) + d_ref[...] * uu
        y2 = _gelu(y)
        gate = jax.nn.sigmoid(_dot(y2.astype(MXU_DTYPE), gw_ref[...].astype(MXU_DTYPE), NN) + gb_ref[...])
        y_ref[...] = y2 * gate

    full = lambda shape: pl.BlockSpec(shape, lambda i: tuple(0 for _ in shape))
    y_seg, entry = pl.pallas_call(
        body, name="ssm_fwd", grid=(nblk,),
        in_specs=[pl.BlockSpec((SSM_TB, SSM_W), lambda i: (i, 0)), full(b_blk.shape), full(c_mat.shape),
                  full(a_row.shape), full(p_tab.shape), full((1, SSM_W)), full((SSM_W, SSM_W)), full((1, SSM_W))],
        out_specs=[pl.BlockSpec((SSM_TB, SSM_W), lambda i: (i, 0)),
                   pl.BlockSpec((1, SSM_NSEG, 2 * NSTATE), lambda i: (i, 0, 0))],
        out_shape=[jax.ShapeDtypeStruct((s, SSM_W), F32), jax.ShapeDtypeStruct((nblk, SSM_NSEG, 2 * NSTATE), F32)],
        scratch_shapes=[pltpu.VMEM((SSM_TB, 2 * NSTATE), F32), pltpu.VMEM((SSM_NSEG, 2 * NSTATE), F32)],
        compiler_params=_params(("arbitrary",)),
    )(_to_segments(u), b_blk, c_mat, a_row, p_tab, d_skip.reshape(1, SSM_W), glu_w, glu_b.reshape(1, SSM_W))
    return _from_segments(y_seg), entry


def _ssm_bwd(u, entry, ops, d_skip, glu_w, glu_b, dout):
    b_blk, c_mat, a_row, p_tab = ops
    s = u.shape[0]
    nblk = s // SSM_TB

    def body(u_ref, en_ref, bb_ref, cm_ref, a_ref, p_ref, d_ref, gw_ref, gb_ref, do_ref,
             du_ref, dbb_ref, dcm_ref, da_ref, dd_ref, dgw_ref, dgb_ref, xbuf, gbuf, gcarry, gentry):
        @pl.when(pl.program_id(0) == 0)
        def _():
            gcarry[...] = jnp.zeros_like(gcarry)
            for r in (dbb_ref, dcm_ref, da_ref, dd_ref, dgw_ref, dgb_ref):
                r[...] = jnp.zeros_like(r)

        uu = u_ref[...]
        xbuf[...] = _dotf(uu, bb_ref[...], NN)
        _ssm_local_scan(xbuf, a_ref, reverse=False)
        _ssm_fixup(xbuf, p_ref, en_ref, reverse=False)
        y = _dotf(xbuf[...],cm_ref[...], NN) + d_ref[...] * uu
        y2, dgelu = _gelu_pair(y)
        y2m = y2.astype(MXU_DTYPE)
        gwm = gw_ref[...].astype(MXU_DTYPE)
        gate = jax.nn.sigmoid(_dot(y2m, gwm, NN) + gb_ref[...])
        dout = do_ref[...]
        dpre = dout * y2 * gate * (1.0 - gate)
        dprem = dpre.astype(MXU_DTYPE)
        dy2 = dout * gate + _dot(dprem, gwm, NT)
        dgw_ref[...] += _dot(y2m, dprem, TN)
        dgb_ref[...] += jnp.sum(dpre, axis=0, keepdims=True)
        dy = dy2 * dgelu
        dd_ref[...] += jnp.sum(dy * uu, axis=0, keepdims=True)
        dcm_ref[...] += _dotf(xbuf[...],dy, TN)
        gbuf[...] = _dotf(dy, cm_ref[...], NT)
        gs_re, gs_im = _ssm_local_scan(gbuf, a_ref, reverse=True)
        _ssm_entry_states(gs_re, gs_im, gcarry, p_ref, gentry, reverse=True)
        _ssm_fixup(gbuf, p_ref, gentry, reverse=True)
        du_ref[...] = (_dotf(gbuf[...], bb_ref[...], NT) + d_ref[...] * dy).astype(du_ref.dtype)
        dbb_ref[...] += _dotf(uu, gbuf[...], TN)
        for lo, hi in _lane_chunks():
            def step(j, carry, lo=lo, hi=hi):
                acc_re, acc_im = carry
                g_re = gbuf[_seg_rows(j), lo:hi]
                g_im = gbuf[_seg_rows(j), NSTATE + lo:NSTATE + hi]
                x_re = xbuf[_seg_rows(j - 1), lo:hi]
                x_im = xbuf[_seg_rows(j - 1), NSTATE + lo:NSTATE + hi]
                return acc_re + g_re * x_re + g_im * x_im, acc_im + g_im * x_re - g_re * x_im

            g0_re = gbuf[_seg_rows(0), lo:hi]
            g0_im = gbuf[_seg_rows(0), NSTATE + lo:NSTATE + hi]
            e_re = en_ref[0, :, lo:hi]
            e_im = en_ref[0, :, NSTATE + lo:NSTATE + hi]
            init = (g0_re * e_re + g0_im * e_im, g0_im * e_re - g0_re * e_im)
            acc_re, acc_im = lax.fori_loop(1, SSM_TSEG, step, init, unroll=4)
            da_ref[0:1, lo:hi] += jnp.sum(acc_re, axis=0, keepdims=True)
            da_ref[1:2, lo:hi] += jnp.sum(acc_im, axis=0, keepdims=True)

    full = lambda shape: pl.BlockSpec(shape, lambda i: tuple(0 for _ in shape))
    rev = pl.BlockSpec((SSM_TB, SSM_W), lambda i: (nblk - 1 - i, 0))
    outs = pl.pallas_call(
        body, name="ssm_bwd", grid=(nblk,),
        in_specs=[rev, pl.BlockSpec((1, SSM_NSEG, 2 * NSTATE), lambda i: (nblk - 1 - i, 0, 0)),
                  full(b_blk.shape), full(c_mat.shape), full(a_row.shape), full(p_tab.shape),
                  full((1, SSM_W)), full((SSM_W, SSM_W)), full((1, SSM_W)), rev],
        out_specs=[rev, full(b_blk.shape), full(c_mat.shape), full(a_row.shape), full((1, SSM_W)),
                   full((SSM_W, SSM_W)), full((1, SSM_W))],
        out_shape=[jax.ShapeDtypeStruct((s, SSM_W), MXU_DTYPE), jax.ShapeDtypeStruct(b_blk.shape, F32),
                   jax.ShapeDtypeStruct(c_mat.shape, F32), jax.ShapeDtypeStruct(a_row.shape, F32),
                   jax.ShapeDtypeStruct((1, SSM_W), F32), jax.ShapeDtypeStruct((SSM_W, SSM_W), F32),
                   jax.ShapeDtypeStruct((1, SSM_W), F32)],
        scratch_shapes=[pltpu.VMEM((SSM_TB, 2 * NSTATE), F32), pltpu.VMEM((SSM_TB, 2 * NSTATE), F32),
                        pltpu.VMEM((SSM_NSEG, 2 * NSTATE), F32), pltpu.VMEM((1, SSM_NSEG, 2 * NSTATE), F32)],
        compiler_params=_params(("arbitrary",)),
    )(_to_segments(u), entry, b_blk, c_mat, a_row, p_tab, d_skip.reshape(1, SSM_W), glu_w, glu_b.reshape(1, SSM_W),
      _to_segments(dout))
    return (_from_segments(outs[0]),) + tuple(outs[1:])


MIX_SEGS = ((0, ATTN_W), (ATTN_W, ATTN_W + SGU_W), (ATTN_W + SGU_W, D_MODEL))


def _chunks_to_rows(a_ref):
    return jnp.concatenate([a_ref[ch] for ch in range(N_CHUNK)], axis=1)


def _mix_fwd(y_attn_c, y_sgu, y_ssm, gain):
    s = y_sgu.shape[0]

    def body(a_ref, b_ref, c_ref, g_ref, o_ref):
        for x, (lo, hi) in zip((_chunks_to_rows(a_ref), b_ref[...], c_ref[...]), MIX_SEGS):
            r = lax.rsqrt(jnp.mean(x * x, axis=-1, keepdims=True) + EPS)
            o_ref[:, lo:hi] = (x * r * g_ref[:, lo:hi]).astype(o_ref.dtype)

    row = lambda w: pl.BlockSpec((ROWS, w), lambda i: (i, 0))
    return pl.pallas_call(
        body, name="mix_fwd", grid=(s // ROWS,),
        in_specs=[pl.BlockSpec((N_CHUNK, ROWS, 128), lambda i: (0, i, 0)), row(SGU_W), row(SSM_W),
                  pl.BlockSpec((1, D_MODEL), lambda i: (0, 0))],
        out_specs=row(D_MODEL), out_shape=jax.ShapeDtypeStruct((s, D_MODEL), MXU_DTYPE),
        compiler_params=_params(("parallel",)),
    )(y_attn_c, y_sgu, y_ssm, gain.reshape(1, D_MODEL))


def _mix_bwd(y_attn_c, y_sgu, y_ssm, gain, dmix):
    s = y_sgu.shape[0]

    def body(a_ref, b_ref, c_ref, g_ref, dm_ref, da_ref, dl_ref, db_ref, dc_ref, dg_ref):
        @pl.when(pl.program_id(0) == 0)
        def _():
            dg_ref[...] = jnp.zeros_like(dg_ref)

        grads = []
        for x, (lo, hi) in zip((_chunks_to_rows(a_ref), b_ref[...], c_ref[...]), MIX_SEGS):
            r = lax.rsqrt(jnp.mean(x * x, axis=-1, keepdims=True) + EPS)
            xhat = x * r
            dm = dm_ref[:, lo:hi].astype(F32)
            dg_ref[:, lo:hi] += jnp.sum(dm * xhat, axis=0, keepdims=True)
            dxh = dm * g_ref[:, lo:hi]
            grads.append(r * (dxh - xhat * jnp.mean(dxh * xhat, axis=-1, keepdims=True)))
        db_ref[...] = grads[1]
        dc_ref[...] = grads[2]
        low = lax.broadcasted_iota(jnp.int32, (ROWS, 128), 1) < HEAD_DIM
        for ch in range(N_CHUNK):
            d_c = grads[0][:, 128 * ch:128 * (ch + 1)]
            da_ref[ch] = d_c.astype(da_ref.dtype)
            prod = d_c * a_ref[ch]
            dl_ref[ch] = jnp.where(low, jnp.sum(prod[:, :HEAD_DIM], axis=-1, keepdims=True),
                                   jnp.sum(prod[:, HEAD_DIM:], axis=-1, keepdims=True))

    row = lambda w: pl.BlockSpec((ROWS, w), lambda i: (i, 0))
    vec = pl.BlockSpec((1, D_MODEL), lambda i: (0, 0))
    chunked = pl.BlockSpec((N_CHUNK, ROWS, 128), lambda i: (0, i, 0))
    return pl.pallas_call(
        body, name="mix_bwd", grid=(s // ROWS,),
        in_specs=[chunked, row(SGU_W), row(SSM_W), vec, row(D_MODEL)],
        out_specs=[chunked, chunked, row(SGU_W), row(SSM_W), vec],
        out_shape=[jax.ShapeDtypeStruct((N_CHUNK, s, 128), ATTN_IO_DTYPE), jax.ShapeDtypeStruct((N_CHUNK, s, 128), F32),
                   jax.ShapeDtypeStruct((s, SGU_W), F32), jax.ShapeDtypeStruct((s, SSM_W), F32),
                   jax.ShapeDtypeStruct((1, D_MODEL), F32)],
        compiler_params=_params(("arbitrary",)),
    )(y_attn_c, y_sgu, y_ssm, gain.reshape(1, D_MODEL), dmix)


CONV_ROWS = 256
CONV_COLS = 1408
CONV_PAIR = 2 * CONV_COLS
HALO = 16


def _interleave_ff(t):
    lead = t.shape[:-1]
    nb = D_FF // CONV_COLS
    return jnp.swapaxes(t.reshape(lead + (2, nb, CONV_COLS)), -3, -2).reshape(lead + (2 * D_FF,))


def _deinterleave_ff(t):
    lead = t.shape[:-1]
    nb = D_FF // CONV_COLS
    return jnp.swapaxes(t.reshape(lead + (nb, 2, CONV_COLS)), -3, -2).reshape(lead + (2 * D_FF,))


def _causal_taps(x, halo, first):
    patch = 8
    row = lax.broadcasted_iota(jnp.int32, (patch, x.shape[1]), 0)
    h1 = jnp.where(first, 0.0, halo[HALO - 1:HALO, :])
    h2 = jnp.where(first, 0.0, halo[HALO - 2:HALO - 1, :])
    r1 = pltpu.roll(x, 1, 0)
    r2 = pltpu.roll(x, 2, 0)
    top1 = jnp.where(row == 0, h1, r1[0:patch])
    top2 = jnp.where(row == 0, h2, jnp.where(row == 1, h1, r2[0:patch]))
    return jnp.concatenate([top1, r1[patch:]], axis=0), jnp.concatenate([top2, r2[patch:]], axis=0)


def _conv_in_specs():
    halo_idx = lambda i: jnp.maximum(i * (CONV_ROWS // HALO) - 1, 0)
    return [pl.BlockSpec((CONV_ROWS, CONV_PAIR), lambda j, i: (i, j)),
            pl.BlockSpec((HALO, CONV_PAIR), lambda j, i: (halo_idx(i), j)),
            pl.BlockSpec((3, CONV_PAIR), lambda j, i: (0, j)),
            pl.BlockSpec((1, CONV_PAIR), lambda j, i: (0, j))]


def _ffn_act_fwd(hh, conv_w, conv_b):
    s = hh.shape[0]

    def body(m_ref, h_ref, w_ref, b_ref, o_ref):
        first = pl.program_id(1) == 0
        main = m_ref[...].astype(F32)
        x1, x2 = _causal_taps(main, h_ref[...].astype(F32), first)
        conv = w_ref[0:1, :] * x2 + w_ref[1:2, :] * x1 + w_ref[2:3, :] * main + b_ref[...]
        o_ref[...] = (_gelu(conv[:, CONV_COLS:]) * conv[:, :CONV_COLS]).astype(o_ref.dtype)

    return pl.pallas_call(
        body, name="ffn_act_fwd", grid=(D_FF // CONV_COLS, s // CONV_ROWS), in_specs=_conv_in_specs(),
        out_specs=pl.BlockSpec((CONV_ROWS, CONV_COLS), lambda j, i: (i, j)),
        out_shape=jax.ShapeDtypeStruct((s, D_FF), MXU_DTYPE),
        compiler_params=_params(("parallel", "parallel")),
    )(hh, hh, conv_w, conv_b.reshape(1, -1))


def _ffn_act_bwd(hh, conv_w, conv_b, da):
    s = hh.shape[0]
    nrow = s // CONV_ROWS
    ext_rows = CONV_ROWS + HALO

    def body(m_ref, h_ref, w_ref, b_ref, nx_ref, da_ref, dan_ref, o_ref, dw_ref, db_ref):
        first = pl.program_id(1) == 0
        last = pl.program_id(1) == nrow - 1

        @pl.when(first)
        def _():
            dw_ref[...] = jnp.zeros_like(dw_ref)
            db_ref[...] = jnp.zeros_like(db_ref)

        ext = jnp.concatenate([m_ref[...].astype(F32), nx_ref[...].astype(F32)], axis=0)
        x1, x2 = _causal_taps(ext, h_ref[...].astype(F32), first)
        conv = w_ref[0:1, :] * x2 + w_ref[1:2, :] * x1 + w_ref[2:3, :] * ext + b_ref[...]
        da = jnp.concatenate([da_ref[...].astype(F32), jnp.where(last, 0.0, dan_ref[...].astype(F32))], axis=0)
        act, dact = _gelu_pair(conv[:, CONV_COLS:])
        dconv = jnp.concatenate([da * act, da * conv[:, :CONV_COLS] * dact], axis=1)
        dmain = dconv[:CONV_ROWS]
        ahead1 = pltpu.roll(dconv, ext_rows - 1, 0)[:CONV_ROWS]
        ahead2 = pltpu.roll(dconv, ext_rows - 2, 0)[:CONV_ROWS]
        o_ref[...] = (w_ref[2:3, :] * dmain + w_ref[1:2, :] * ahead1 + w_ref[0:1, :] * ahead2).astype(o_ref.dtype)
        for t, tap in enumerate((x2, x1, ext)):
            dw_ref[t:t + 1, :] += jnp.sum(dmain * tap[:CONV_ROWS], axis=0, keepdims=True)
        db_ref[...] += jnp.sum(dmain, axis=0, keepdims=True)

    nxt = lambda i: jnp.minimum((i + 1) * (CONV_ROWS // HALO), s // HALO - 1)
    return pl.pallas_call(
        body, name="ffn_act_bwd", grid=(D_FF // CONV_COLS, nrow),
        in_specs=_conv_in_specs() + [pl.BlockSpec((HALO, CONV_PAIR), lambda j, i: (nxt(i), j)),
                                     pl.BlockSpec((CONV_ROWS, CONV_COLS), lambda j, i: (i, j)),
                                     pl.BlockSpec((HALO, CONV_COLS), lambda j, i: (nxt(i), j))],
        out_specs=[pl.BlockSpec((CONV_ROWS, CONV_PAIR), lambda j, i: (i, j)),
                   pl.BlockSpec((3, CONV_PAIR), lambda j, i: (0, j)), pl.BlockSpec((1, CONV_PAIR), lambda j, i: (0, j))],
        out_shape=[jax.ShapeDtypeStruct((s, 2 * D_FF), MXU_DTYPE), jax.ShapeDtypeStruct((3, 2 * D_FF), F32),
                   jax.ShapeDtypeStruct((1, 2 * D_FF), F32)],
        compiler_params=_params(("parallel", "arbitrary")),
    )(hh, hh, conv_w, conv_b.reshape(1, -1), hh, da, da)


def _ple_weight_specs(layer):
    return [pl.BlockSpec((None, D_MODEL, D_MODEL), lambda i: (layer, 0, 0)),
            pl.BlockSpec((None, PLE_DIM, D_MODEL), lambda i: (layer, 0, 0))]


def _ple_fwd(xn, p, w_gate, w_proj, h, layer):
    s = xn.shape[0]
    tm = 512

    def body(x_ref, p_ref, wg_ref, wp_ref, h_ref, o_ref):
        gate = jax.nn.sigmoid(_dot(x_ref[...].astype(MXU_DTYPE), wg_ref[...].astype(MXU_DTYPE), NN))
        proj = _dot(p_ref[...].astype(MXU_DTYPE), wp_ref[...].astype(MXU_DTYPE), NN)
        o_ref[...] = h_ref[...] + gate * proj

    return pl.pallas_call(
        body, name="ple_fwd", grid=(s // tm,),
        in_specs=[pl.BlockSpec((tm, D_MODEL), lambda i: (i, 0)), pl.BlockSpec((tm, PLE_DIM), lambda i: (i, 0))]
        + _ple_weight_specs(layer) + [pl.BlockSpec((tm, D_MODEL), lambda i: (i, 0))],
        out_specs=pl.BlockSpec((tm, D_MODEL), lambda i: (i, 0)),
        out_shape=jax.ShapeDtypeStruct((s, D_MODEL), F32),
        compiler_params=_params(("parallel",)),
    )(xn, p, w_gate, w_proj, h)


def _ple_bwd(xn, p, w_gate, w_proj, dh, layer):
    s = xn.shape[0]
    tm = 512

    def body(x_ref, p_ref, wg_ref, wp_ref, dh_ref, dpre_ref, dproj_ref):
        gate = jax.nn.sigmoid(_dot(x_ref[...].astype(MXU_DTYPE), wg_ref[...].astype(MXU_DTYPE), NN))
        proj = _dot(p_ref[...].astype(MXU_DTYPE), wp_ref[...].astype(MXU_DTYPE), NN)
        dh = dh_ref[...]
        dpre_ref[...] = (dh * proj * gate * (1.0 - gate)).astype(dpre_ref.dtype)
        dproj_ref[...] = (dh * gate).astype(dproj_ref.dtype)

    row = pl.BlockSpec((tm, D_MODEL), lambda i: (i, 0))
    return pl.pallas_call(
        body, name="ple_bwd", grid=(s // tm,),
        in_specs=[row, pl.BlockSpec((tm, PLE_DIM), lambda i: (i, 0))] + _ple_weight_specs(layer) + [row],
        out_specs=[row, row],
        out_shape=[jax.ShapeDtypeStruct((s, D_MODEL), MXU_DTYPE)] * 2,
        compiler_params=_params(("parallel",)),
    )(xn, p, w_gate, w_proj, dh)


O_SGU = 3 * ATTN_W
O_SSM = O_SGU + 2 * SGU_W


def _layer_consts(w, i):
    causal = jnp.asarray(np.tril(np.ones((SGU_CHUNK, SGU_CHUNK), np.float32)))
    return {
        "sgu_w_mask": w["sgu_w"][i] * causal,
        "sgu_b_t": w["sgu_b"][i].T,
        "ssm_ops": _ssm_operands(w["ssm_a_re"][i], w["ssm_a_im"][i], w["ssm_log_dt"][i], w["ssm_b_re"][i],
                                 w["ssm_b_im"][i], w["ssm_c_re"][i], w["ssm_c_im"][i]),
    }


def _layer_fwd(h0, p_i, w, i, bias):
    c = _layer_consts(w, i)
    xn1 = _rms_fwd(h0, w["norm_attn_g"][i], name="rms_attn_fwd")
    qkv, zs, us = _in_proj(xn1, w["w_in"], i)
    y_attn, lse = _attn2_fwd(qkv, bias)
    y_sgu = _sgu_fwd(zs, w["sgu_ln_g"][i], w["sgu_ln_b"][i], c["sgu_w_mask"], c["sgu_b_t"])
    y_ssm, entry = _ssm_fwd(us, c["ssm_ops"], w["ssm_d"][i], w["ssm_glu_w"][i], w["ssm_glu_b"][i])
    mix = _mix_fwd(y_attn, y_sgu, y_ssm, w["branch_norm_g"][i])
    h1 = _matmul(mix, w["w_out"], name="out_proj", out_dtype=F32, tm=512, tn=1024, residual=h0, layer=i)
    xn2 = _rms_fwd(h1, w["norm_ffn_g"][i], name="rms_ffn_fwd")
    hh = _matmul(xn2, w["ffn_w_up"], name="ffn_up", out_dtype=MXU_DTYPE, tm=1024, tn=1408, layer=i)
    act = _ffn_act_fwd(hh, w["ffn_conv_w"][i], w["ffn_conv_b"][i])
    h2 = _matmul(act, w["ffn_w_down"], name="ffn_down", out_dtype=F32, tm=512, tn=1024, residual=h1, layer=i)
    xn3 = _rms_fwd(h2, w["norm_ple_g"][i], name="rms_ple_fwd")
    h3 = _ple_fwd(xn3, p_i, w["ple_w_gate"], w["ple_w_proj"], h2, i)
    saved = dict(h0=h0, xn1=xn1, qkv=qkv, zs=zs, us=us, y_attn=y_attn, lse=lse, y_sgu=y_sgu, y_ssm=y_ssm,
                 entry=entry, mix=mix, h1=h1, xn2=xn2, hh=hh, act=act, h2=h2, xn3=xn3, consts=c)
    return h3, saved


def _layer_bwd(dh3, sv, p_i, w, i, bias):
    c = sv["consts"]
    g = {}
    dpre, dproj = _ple_bwd(sv["xn3"], p_i, w["ple_w_gate"], w["ple_w_proj"], dh3, i)
    g["ple_w_gate"] = _matmul_tn(sv["xn3"], dpre, name="d_ple_w_gate", tk=1024, tn=1024)
    g["ple_w_proj"] = _matmul_tn(p_i, dproj, name="d_ple_w_proj", tk=256, tn=1024)
    dxn3 = _matmul(dpre, w["ple_w_gate"], name="d_xn_ple", out_dtype=F32, tm=512, tn=1024, trans_b=True, layer=i)
    dh2, g["norm_ple_g"] = _rms_bwd(sv["h2"], w["norm_ple_g"][i], dxn3, dh3, name="rms_ple_bwd")
    g["ffn_w_down"] = _matmul_tn(sv["act"], dh2, name="d_ffn_w_down", tk=1408, tn=1024)
    dact = _matmul(dh2, w["ffn_w_down"], name="d_ffn_act", out_dtype=MXU_DTYPE, tm=512, tn=1408, trans_b=True, layer=i)
    dhh, g["ffn_conv_w"], g["ffn_conv_b"] = _ffn_act_bwd(sv["hh"], w["ffn_conv_w"][i], w["ffn_conv_b"][i], dact)
    g["ffn_w_up"] = _matmul_tn(sv["xn2"], dhh, name="d_ffn_w_up", tk=1024, tn=1408)
    dxn2 = _matmul(dhh, w["ffn_w_up"], name="d_xn_ffn", out_dtype=F32, tm=512, tn=512, trans_b=True, layer=i)
    dh1, g["norm_ffn_g"] = _rms_bwd(sv["h1"], w["norm_ffn_g"][i], dxn2, dh2, name="rms_ffn_bwd")
    g["w_out"] = _matmul_tn(sv["mix"], dh1, name="d_w_out", tk=1024, tn=1024)
    dmix = _matmul(dh1, w["w_out"], name="d_mix", out_dtype=F32, tm=512, tn=1024, trans_b=True, layer=i)
    dy_attn, delta, dy_sgu, dy_ssm, g["branch_norm_g"] = _mix_bwd(sv["y_attn"], sv["y_sgu"], sv["y_ssm"],
                                                                  w["branch_norm_g"][i], dmix)
    dq, dk, dv, ek, ev, dbias = _attn2_bwd(sv["qkv"], bias, sv["lse"], delta, dy_attn)
    dzs, g["sgu_ln_g"], g["sgu_ln_b"], dsw, dsb = _sgu_bwd(sv["zs"], w["sgu_ln_g"][i], w["sgu_ln_b"][i],
                                                          c["sgu_w_mask"], c["sgu_b_t"], dy_sgu)
    causal = jnp.asarray(np.tril(np.ones((SGU_CHUNK, SGU_CHUNK), np.float32)))
    g["sgu_w"] = dsw * causal
    g["sgu_b"] = dsb.T
    dus, dbb, dcm, da, g["ssm_d"], g["ssm_glu_w"], g["ssm_glu_b"] = _ssm_bwd(
        sv["us"], sv["entry"], c["ssm_ops"], w["ssm_d"][i], w["ssm_glu_w"][i], w["ssm_glu_b"][i], dy_ssm)
    dbb5 = dbb.reshape(SSM_G, SSM_C, 2, SSM_G, SSM_N)
    dbbar = jnp.einsum("gcpgn->pgnc", dbb5)
    dcm5 = dcm.reshape(2, SSM_G, SSM_N, SSM_G, SSM_C)
    dcc = jnp.einsum("pgngc->pgcn", dcm5)
    g["ssm_c_re"] = dcc[0]
    g["ssm_c_im"] = -dcc[1]
    da2 = da.reshape(2, SSM_G, SSM_N)
    _, vjp = jax.vjp(_ssm_discretize, w["ssm_a_re"][i], w["ssm_a_im"][i], w["ssm_log_dt"][i],
                     w["ssm_b_re"][i], w["ssm_b_im"][i])
    (g["ssm_a_re"], g["ssm_a_im"], g["ssm_log_dt"], g["ssm_b_re"], g["ssm_b_im"]) = vjp(
        (da2[0], da2[1], dbbar[0], dbbar[1]))
    dz = _attn2_bwd_sum(dq, dk, dv, ek, ev, dzs, dus)
    g["w_in"] = _matmul_tn(sv["xn1"], dz, name="d_w_in", tk=1024, tn=1152)
    dxn1 = _matmul(dz, w["w_in"], name="d_xn_attn", out_dtype=F32, tm=512, tn=1024, trans_b=True, layer=i)
    dh0, g["norm_attn_g"] = _rms_bwd(sv["h0"], w["norm_attn_g"][i], dxn1, dh1, name="rms_attn_bwd")
    for k in ("norm_ple_g", "norm_ffn_g", "branch_norm_g", "norm_attn_g", "sgu_ln_g", "sgu_ln_b", "ssm_d",
              "ssm_glu_b", "ffn_conv_b"):
        g[k] = g[k].reshape(-1)
    return dh0, g, dbias


def _local_step(x, p, target, w, ff_interleaved=False):
    ff_names = ("ffn_conv_b",) if ff_interleaved else FF_SHARDED + ("ffn_conv_b",)
    w = dict(w)
    for k in ff_names:
        w[k] = _interleave_ff(w[k])
    bias = _bias_build(w["rel_bias"])
    h = x
    saved = []
    for i in range(DEPTH):
        h, sv = _layer_fwd(h, p[i], w, i, bias)
        saved.append(sv)
    loss, dh, dgf = _loss_head(h, w["final_norm_g"], target)
    layer_grads = [None] * DEPTH
    dbias = None
    for i in reversed(range(DEPTH)):
        dh, layer_grads[i], db = _layer_bwd(dh, saved[i], p[i], w, i, bias)
        dbias = db if dbias is None else dbias + db
    grads = {k: jnp.stack([layer_grads[i][k] for i in range(DEPTH)]) for k in layer_grads[0]}
    for k in ff_names:
        grads[k] = _deinterleave_ff(grads[k])
    grads["rel_bias"] = _bias_reduce(dbias)
    grads["final_norm_g"] = dgf.reshape(-1)
    return loss, dh, grads


def _pad_rows(a2, mult=16):
    r = (-a2.shape[0]) % mult
    return a2 if r == 0 else jnp.concatenate([a2, jnp.zeros((r, a2.shape[1]), a2.dtype)], axis=0)


def _as_rows(a, rows=None):
    flat = a.reshape(-1)
    if rows is None:
        rows = -(-flat.shape[0] // (16 * PACK_COLS)) * 16
    return jnp.pad(flat, (0, rows * PACK_COLS - flat.shape[0])).reshape(rows, PACK_COLS)


def _shard_shape(name):
    full, ax = BIG_FULL[name]
    shp = [DEPTH] + list(full)
    shp[ax] //= N_CHIPS
    return tuple(shp)


EXACT_NAMES = ("ffn_conv_w",)


def _pack_rows_of(name):
    n = int(np.prod(_shard_shape(name))) * (2 if name in EXACT_NAMES else 1)
    rows = -(-n // PACK_COLS)
    return -(-rows // 16) * 16


def _pack_shards(shards, dtype, exact=False):
    split_words = exact and jnp.dtype(dtype).itemsize == 2
    parts = []
    for n in BIG_NAMES:
        a = shards[n]
        if split_words and n in EXACT_NAMES:
            a = lax.bitcast_convert_type(a.astype(F32), dtype)
        parts.append(_as_rows(a.astype(dtype), _pack_rows_of(n)))
    used = sum(pt.shape[0] for pt in parts)
    parts.append(jnp.zeros((PACK_ROWS - used, PACK_COLS), dtype))
    return jnp.concatenate(parts, axis=0)


def _unpack_shard(flat, name, exact=False):
    off = 0
    for n in BIG_NAMES:
        if n == name:
            break
        off += _pack_rows_of(n)
    shp = _shard_shape(name)
    cnt = int(np.prod(shp))
    vec = flat[off:off + _pack_rows_of(name)].reshape(-1)
    if exact and name in EXACT_NAMES and jnp.dtype(flat.dtype).itemsize == 2:
        return lax.bitcast_convert_type(vec[:2 * cnt].reshape(shp + (2,)), F32)
    return vec[:cnt].reshape(shp)


FF_SHARDED = ("ffn_w_up", "ffn_conv_w")
FF_CHIP_ORDER = (0, 2, 1, 3)


def _chip_order(name):
    return FF_CHIP_ORDER if name in FF_SHARDED else tuple(range(N_CHIPS))


def _split_full(full, name):
    _, ax = BIG_FULL[name]
    parts = jnp.split(full, N_CHIPS, axis=ax)
    out = [None] * N_CHIPS
    for j, k in enumerate(_chip_order(name)):
        out[k] = parts[j]
    return out


def _join_shards(shards, name):
    _, ax = BIG_FULL[name]
    return jnp.concatenate([shards[k] for k in _chip_order(name)], axis=ax)


def _small_shapes(w):
    return [(n, w[n].shape) for n in SMALL_NAMES]


def _pack_small(d):
    flat = jnp.concatenate([d[n].astype(F32).reshape(-1) for n in SMALL_NAMES])
    flat = jnp.concatenate([flat, jnp.zeros((SMALL_ROWS * PACK_COLS - flat.shape[0],), F32)])
    return flat.reshape(SMALL_ROWS, PACK_COLS)


def _unpack_small(flat, shapes):
    out, off = {}, 0
    v = flat.reshape(-1)
    for n, shp in shapes:
        cnt = int(np.prod(shp))
        out[n] = v[off:off + cnt].reshape(shp)
        off += cnt
    return out


MESH = pl.DeviceIdType.MESH
ANY = pl.BlockSpec(memory_space=pl.ANY)
LOCAL_DMAS = 8


def _me():
    return lax.axis_index("x"), lax.axis_index("y"), lax.axis_index("c")


def _other_chips(x, y):
    return [(1 - x, y), (x, 1 - y), (1 - x, 1 - y)]


def _gather_weights(wflat):
    def body(w_ref, out_ref, send_sems, recv_sems, local_sems):
        x, y, c = _me()
        sibling = (x, y, 1 - c)
        chips = _other_chips(x, y)

        def rows(chip, half):
            return out_ref.at[2 * chip[0] + chip[1], pl.ds(half * PACK_HALF, PACK_HALF), :]

        def copy(k, chip, half, to, src=None):
            return pltpu.make_async_remote_copy(
                src_ref=rows(chip, half) if src is None else src, dst_ref=rows(chip, half),
                send_sem=send_sems.at[k], recv_sem=recv_sems.at[k], device_id=to, device_id_type=MESH)

        part = PACK_ROWS // LOCAL_DMAS
        mine = [pltpu.make_async_copy(w_ref.at[pl.ds(t * part, part), :],
                                      out_ref.at[2 * x + y, pl.ds(t * part, part), :], local_sems.at[t])
                for t in range(LOCAL_DMAS)]
        for cp in mine:
            cp.start()
        my_half = w_ref.at[pl.ds(c * PACK_HALF, PACK_HALF), :]
        first = [copy(j, (x, y), c, (*chip, c), src=my_half) for j, chip in enumerate(chips)]
        for cp in first:
            cp.start()
        passed = [copy(3 + j, chip, c, sibling) for j, chip in enumerate(chips)]
        for j, chip in enumerate(chips):
            copy(j, chip, c, (x, y, c)).wait_recv()
            passed[j].start()
        for j, chip in enumerate(chips):
            copy(3 + j, chip, 1 - c, (x, y, c)).wait_recv()
        for cp in first + passed:
            cp.wait_send()
        for cp in mine:
            cp.wait()

    return pl.pallas_call(
        body, name="gather_weights", in_specs=[ANY], out_specs=ANY,
        out_shape=jax.ShapeDtypeStruct((N_CHIPS, PACK_ROWS, PACK_COLS), wflat.dtype),
        scratch_shapes=[pltpu.SemaphoreType.DMA((6,)), pltpu.SemaphoreType.DMA((6,)),
                        pltpu.SemaphoreType.DMA((LOCAL_DMAS,))],
    )(wflat)


def _exchange_partials(gb, gs):
    def body(gb_ref, gs_ref, half_ref, small_ref, send_sems, recv_sems, local_sem):
        x, y, c = _me()
        me_idx = 4 * x + 2 * y + c
        mine = pltpu.make_async_copy(gs_ref, small_ref.at[me_idx], local_sem)
        mine.start()
        d2d = pltpu.make_async_remote_copy(
            src_ref=gb_ref.at[:, pl.ds((1 - c) * PACK_HALF, PACK_HALF), :], dst_ref=half_ref,
            send_sem=send_sems.at[0], recv_sem=recv_sems.at[0], device_id=(x, y, 1 - c), device_id_type=MESH)
        d2d.start()
        copies = []
        for k in range(1, N_DEV):
            fx, fy, fc = (k >> 2) & 1, (k >> 1) & 1, k & 1
            peer = (x ^ fx, y ^ fy, c ^ fc)
            copies.append(pltpu.make_async_remote_copy(
                src_ref=gs_ref, dst_ref=small_ref.at[me_idx], send_sem=send_sems.at[k], recv_sem=recv_sems.at[k],
                device_id=peer, device_id_type=MESH))
        for cp in copies:
            cp.start()
        for k in range(1, N_DEV):
            fx, fy, fc = (k >> 2) & 1, (k >> 1) & 1, k & 1
            peer_idx = 4 * (x ^ fx) + 2 * (y ^ fy) + (c ^ fc)
            pltpu.make_async_remote_copy(
                src_ref=gs_ref, dst_ref=small_ref.at[peer_idx], send_sem=send_sems.at[k], recv_sem=recv_sems.at[k],
                device_id=(x, y, c), device_id_type=MESH).wait_recv()
        d2d.wait_recv()
        d2d.wait_send()
        for cp in copies:
            cp.wait_send()
        mine.wait()

    return pl.pallas_call(
        body, name="exchange_partials", in_specs=[ANY, ANY], out_specs=[ANY, ANY],
        out_shape=[jax.ShapeDtypeStruct((N_CHIPS, PACK_HALF, PACK_COLS), gb.dtype),
                   jax.ShapeDtypeStruct((N_DEV, SMALL_ROWS, PACK_COLS), F32)],
        scratch_shapes=[pltpu.SemaphoreType.DMA((N_DEV,)), pltpu.SemaphoreType.DMA((N_DEV,)), pltpu.SemaphoreType.DMA],
    )(gb, gs)


RED_ROWS = 256


def _chip_partials(gb, sib, c_idx):
    nrow = PACK_HALF // RED_ROWS

    def body(c_ref, a_ref, b_ref, o_ref):
        del c_ref
        o_ref[...] = (a_ref[...].astype(F32) + b_ref[...].astype(F32)).astype(o_ref.dtype)

    blk = (1, RED_ROWS, PACK_COLS)
    return pl.pallas_call(
        body, name="chip_partials",
        grid_spec=pltpu.PrefetchScalarGridSpec(
            num_scalar_prefetch=1, grid=(N_CHIPS, nrow),
            in_specs=[pl.BlockSpec(blk, lambda k, i, c: (k, c[0] * nrow + i, 0)),
                      pl.BlockSpec(blk, lambda k, i, c: (k, i, 0))],
            out_specs=pl.BlockSpec(blk, lambda k, i, c: (k, i, 0))),
        out_shape=jax.ShapeDtypeStruct((N_CHIPS, PACK_HALF, PACK_COLS), gb.dtype),
        compiler_params=_params(("parallel", "parallel")),
    )(c_idx, gb, sib)


def _scatter_partials(pc):
    def body(pc_ref, out_ref, send_sems, recv_sems):
        x, y, c = _me()
        chips = _other_chips(x, y)
        copies = [pltpu.make_async_remote_copy(
            src_ref=pc_ref.at[2 * chip[0] + chip[1]], dst_ref=out_ref.at[k],
            send_sem=send_sems.at[k], recv_sem=recv_sems.at[k], device_id=(*chip, c), device_id_type=MESH)
            for k, chip in enumerate(chips)]
        for cp in copies:
            cp.start()
        for cp in copies:
            cp.wait_recv()
        for cp in copies:
            cp.wait_send()

    return pl.pallas_call(
        body, name="scatter_partials", in_specs=[ANY], out_specs=ANY,
        out_shape=jax.ShapeDtypeStruct((3, PACK_HALF, PACK_COLS), pc.dtype),
        scratch_shapes=[pltpu.SemaphoreType.DMA((3,)), pltpu.SemaphoreType.DMA((3,))],
    )(pc)


def _final_half(gb, sib, recv, idx):
    nrow = PACK_HALF // RED_ROWS

    def body(idx_ref, a_ref, b_ref, r_ref, o_ref):
        del idx_ref
        acc = a_ref[0].astype(F32) + b_ref[0].astype(F32)
        for k in range(3):
            acc = acc + r_ref[k].astype(F32)
        o_ref[...] = acc

    return pl.pallas_call(
        body, name="final_half",
        grid_spec=pltpu.PrefetchScalarGridSpec(
            num_scalar_prefetch=1, grid=(nrow,),
            in_specs=[pl.BlockSpec((1, RED_ROWS, PACK_COLS), lambda i, idx: (idx[0], idx[1] * nrow + i, 0)),
                      pl.BlockSpec((1, RED_ROWS, PACK_COLS), lambda i, idx: (idx[0], i, 0)),
                      pl.BlockSpec((3, RED_ROWS, PACK_COLS), lambda i, idx: (0, i, 0))],
            out_specs=pl.BlockSpec((RED_ROWS, PACK_COLS), lambda i, idx: (i, 0))),
        out_shape=jax.ShapeDtypeStruct((PACK_HALF, PACK_COLS), F32),
        compiler_params=_params(("parallel",)),
    )(idx, gb, sib, recv)


def _share_halves(half):
    def body(h_ref, out_ref, send_sem, recv_sem):
        x, y, c = _me()
        cp = pltpu.make_async_remote_copy(src_ref=h_ref, dst_ref=out_ref, send_sem=send_sem, recv_sem=recv_sem,
                                          device_id=(x, y, 1 - c), device_id_type=MESH)
        cp.start()
        cp.wait_recv()
        cp.wait_send()

    return pl.pallas_call(
        body, name="share_halves", in_specs=[ANY], out_specs=ANY,
        out_shape=jax.ShapeDtypeStruct((PACK_HALF, PACK_COLS), F32),
        scratch_shapes=[pltpu.SemaphoreType.DMA, pltpu.SemaphoreType.DMA],
    )(half)


def _sum_small(allsmall):
    def body(a_ref, o_ref):
        acc = a_ref[0]
        for k in range(1, N_DEV):
            acc = acc + a_ref[k]
        o_ref[...] = acc

    tr = 96
    return pl.pallas_call(
        body, name="sum_small", grid=(SMALL_ROWS // tr,),
        in_specs=[pl.BlockSpec((N_DEV, tr, PACK_COLS), lambda i: (0, i, 0))],
        out_specs=pl.BlockSpec((tr, PACK_COLS), lambda i: (i, 0)),
        out_shape=jax.ShapeDtypeStruct((SMALL_ROWS, PACK_COLS), F32),
        compiler_params=_params(("parallel",)),
    )(allsmall)


def _adamw(w, g, m, v, *, name):
    shape = w.shape
    cols = shape[-1]
    as2 = lambda t: t.reshape(-1, cols)
    w2, g2, m2, v2 = as2(w), as2(g), as2(m), as2(v)
    rows = w2.shape[0]
    tr = rows
    if rows * cols * 4 > (1 << 20):
        tr = _tile(rows, max(8, (1 << 20) // (cols * 4) // 8 * 8), 8)

    def body(w_ref, g_ref, m_ref, v_ref, d_ref, mo_ref, vo_ref):
        gg = g_ref[...]
        mn = ADAM_B1 * m_ref[...] + (1.0 - ADAM_B1) * gg
        vn = ADAM_B2 * v_ref[...] + (1.0 - ADAM_B2) * (gg * gg)
        m_hat = mn / (1.0 - ADAM_B1 ** ADAM_STEP)
        v_hat = vn / (1.0 - ADAM_B2 ** ADAM_STEP)
        d_ref[...] = -ADAM_LR * (m_hat / (jnp.sqrt(v_hat) + ADAM_EPS) + ADAM_WD * w_ref[...])
        mo_ref[...] = mn
        vo_ref[...] = vn

    blk = pl.BlockSpec((tr, cols), lambda i: (i, 0))
    outs = pl.pallas_call(
        body, name=name, grid=(rows // tr,), in_specs=[blk] * 4, out_specs=[blk] * 3,
        out_shape=[jax.ShapeDtypeStruct((rows, cols), F32)] * 3,
        compiler_params=_params(("parallel",)),
    )(w2, g2, m2, v2)
    return tuple(t.reshape(shape) for t in outs)


def kernel(x, p, rel_bias, norm_attn_g, w_in, sgu_ln_g, sgu_ln_b, sgu_w, sgu_b, ssm_a_re, ssm_a_im, ssm_log_dt, ssm_b_re, ssm_b_im, ssm_c_re, ssm_c_im, ssm_d, ssm_glu_w, ssm_glu_b, branch_norm_g, w_out, norm_ffn_g, ffn_w_up, ffn_conv_w, ffn_conv_b, ffn_w_down, norm_ple_g, ple_w_gate, ple_w_proj, final_norm_g, loss_target, m_rel_bias, m_norm_attn_g, m_w_in, m_sgu_ln_g, m_sgu_ln_b, m_sgu_w, m_sgu_b, m_ssm_a_re, m_ssm_a_im, m_ssm_log_dt, m_ssm_b_re, m_ssm_b_im, m_ssm_c_re, m_ssm_c_im, m_ssm_d, m_ssm_glu_w, m_ssm_glu_b, m_branch_norm_g, m_w_out, m_norm_ffn_g, m_ffn_w_up, m_ffn_conv_w, m_ffn_conv_b, m_ffn_w_down, m_norm_ple_g, m_ple_w_gate, m_ple_w_proj, m_final_norm_g, v_rel_bias, v_norm_attn_g, v_w_in, v_sgu_ln_g, v_sgu_ln_b, v_sgu_w, v_sgu_b, v_ssm_a_re, v_ssm_a_im, v_ssm_log_dt, v_ssm_b_re, v_ssm_b_im, v_ssm_c_re, v_ssm_c_im, v_ssm_d, v_ssm_glu_w, v_ssm_glu_b, v_branch_norm_g, v_w_out, v_norm_ffn_g, v_ffn_w_up, v_ffn_conv_w, v_ffn_conv_b, v_ffn_w_down, v_norm_ple_g, v_ple_w_gate, v_ple_w_proj, v_final_norm_g):
    args = dict(locals())
    wts = {n: args[n] for n in WEIGHT_NAMES}
    mom_m = {n: args["m_" + n] for n in WEIGHT_NAMES}
    mom_v = {n: args["v_" + n] for n in WEIGHT_NAMES}

    wall = _gather_weights(_pack_shards({n: wts[n] for n in BIG_NAMES}, MXU_DTYPE, exact=True))
    full = dict(wts)
    for n in BIG_NAMES:
        full[n] = _join_shards([_unpack_shard(wall[k], n, exact=True) for k in range(N_CHIPS)], n)
    full["ffn_conv_w"] = full["ffn_conv_w"].astype(F32)

    loss, dx, grads = _local_step(x[0], p[:, 0], loss_target[0], full, ff_interleaved=True)
    loss = lax.psum(loss[0, 0], MESH_AXES)

    xi, yi, ci = _me()
    stacked = {n: _split_full(grads[n], n) for n in BIG_NAMES}
    gb = jnp.stack([_pack_shards({n: stacked[n][k] for n in BIG_NAMES}, MXU_DTYPE) for k in range(N_CHIPS)])
    gs = _pack_small(grads)
    sib, allsmall = _exchange_partials(gb, gs)
    pc = _chip_partials(gb, sib, jnp.stack([ci]).astype(jnp.int32))
    recv = _scatter_partials(pc)
    half = _final_half(gb, sib, recv, jnp.stack([2 * xi + yi, ci]).astype(jnp.int32))
    other = _share_halves(half)
    gflat = jnp.concatenate([jnp.where(ci == 0, half, other), jnp.where(ci == 0, other, half)], axis=0)
    gsmall = _unpack_small(_sum_small(allsmall), _small_shapes(wts))

    g_out, d_out, m_out, v_out = {}, {}, {}, {}
    for n in BIG_NAMES:
        g_out[n] = _unpack_shard(gflat, n)
        d_out[n], m_out[n], v_out[n] = _adamw(wts[n], g_out[n], mom_m[n], mom_v[n], name="adamw_" + n)
    sw = _pack_small(wts)
    d_s, m_s, v_s = _adamw(sw, _pack_small(gsmall), _pack_small(mom_m), _pack_small(mom_v), name="adamw_small")
    shapes = _small_shapes(wts)
    d_sm, m_sm, v_sm = _unpack_small(d_s, shapes), _unpack_small(m_s, shapes), _unpack_small(v_s, shapes)
    for n in SMALL_NAMES:
        g_out[n], d_out[n], m_out[n], v_out[n] = gsmall[n], d_sm[n], m_sm[n], v_sm[n]

    return (loss, dx[None], *[g_out[n] for n in WEIGHT_NAMES], *[d_out[n] for n in WEIGHT_NAMES],
            *[m_out[n] for n in WEIGHT_NAMES], *[v_out[n] for n in WEIGHT_NAMES])
```

```python
import functools
import math

import numpy as np
import jax
import jax.numpy as jnp
from jax import lax
from jax.experimental import pallas as pl
from jax.experimental.pallas import tpu as pltpu

F32 = jnp.float32
MXU_DTYPE = jnp.bfloat16
VMEM_LIMIT_BYTES = 52 * 1024 * 1024

D_MODEL = 1024
DEPTH = 2
PLE_DIM = 256
HEAD_DIM = 64
N_HEADS = 8
ATTN_W = 512
QBLK = 128
BRANCH_DIL = (1, 4, 16)
N_BUCKETS = 32
REL_MAX_DIST = 2048
SGU_W = 256
SGU_G = 4
SGU_GW = 64
SGU_CHUNK = 128
SSM_W = 256
SSM_G = 16
SSM_C = 16
SSM_N = 64
NSTATE = SSM_G * SSM_N
D_FF = 2816
EPS = 1e-6
NEG_INF = -1e30
ATTN_SCALE = HEAD_DIM ** -0.5

ADAM_LR = 0.001
ADAM_B1 = 0.9
ADAM_B2 = 0.999
ADAM_EPS = 1e-08
ADAM_WD = 0.01
ADAM_STEP = 10

SSM_NSEG = 8
SSM_TSEG = 64
SSM_TB = SSM_NSEG * SSM_TSEG
SSM_LANE_CHUNK = 512

MESH_AXES = ("x", "y", "c")
N_CHIPS = 4
N_DEV = 8

BIG_NAMES = ("w_in", "ssm_glu_w", "w_out", "ffn_w_up", "ffn_conv_w", "ffn_w_down", "ple_w_gate", "ple_w_proj")
BIG_FULL = {
    "w_in": ((D_MODEL, 2304), 2),
    "ssm_glu_w": ((SSM_W, SSM_W), 1),
    "w_out": ((D_MODEL, D_MODEL), 1),
    "ffn_w_up": ((D_MODEL, 2 * D_FF), 2),
    "ffn_conv_w": ((3, 2 * D_FF), 2),
    "ffn_w_down": ((D_FF, D_MODEL), 1),
    "ple_w_gate": ((D_MODEL, D_MODEL), 1),
    "ple_w_proj": ((PLE_DIM, D_MODEL), 2),
}
PACK_COLS = 1024
PACK_ROWS = 6656
PACK_HALF = PACK_ROWS // 2

SMALL_NAMES = ("rel_bias", "norm_attn_g", "sgu_ln_g", "sgu_ln_b", "sgu_w", "sgu_b", "ssm_a_re", "ssm_a_im",
               "ssm_log_dt", "ssm_b_re", "ssm_b_im", "ssm_c_re", "ssm_c_im", "ssm_d", "ssm_glu_b",
               "branch_norm_g", "norm_ffn_g", "ffn_conv_b", "norm_ple_g", "final_norm_g")
SMALL_ROWS = 288

WEIGHT_NAMES = ("rel_bias", "norm_attn_g", "w_in", "sgu_ln_g", "sgu_ln_b", "sgu_w", "sgu_b", "ssm_a_re", "ssm_a_im",
                "ssm_log_dt", "ssm_b_re", "ssm_b_im", "ssm_c_re", "ssm_c_im", "ssm_d", "ssm_glu_w", "ssm_glu_b",
                "branch_norm_g", "w_out", "norm_ffn_g", "ffn_w_up", "ffn_conv_w", "ffn_conv_b", "ffn_w_down",
                "norm_ple_g", "ple_w_gate", "ple_w_proj", "final_norm_g")


def _params(sem):
    return pltpu.CompilerParams(dimension_semantics=sem, vmem_limit_bytes=VMEM_LIMIT_BYTES)


def _tile(n, cap, mult=128):
    if n <= cap:
        return n
    best = None
    for t in range(mult, cap + 1, mult):
        if n % t == 0:
            best = t
    assert best is not None, (n, cap)
    return best


def _gelu(x):
    return 0.5 * x * (1.0 + jnp.tanh(0.7978845608028654 * (x + 0.044715 * x * x * x)))


def _gelu_pair(x):
    x2 = x * x
    t = jnp.tanh(0.7978845608028654 * x * (1.0 + 0.044715 * x2))
    half = 0.5 * (1.0 + t)
    return x * half, half + 0.5 * x * (1.0 - t * t) * (0.7978845608028654 + 3.0 * 0.044715 * 0.7978845608028654 * x2)


def _dot(a, b, dims):
    return lax.dot_general(a, b, (dims, ((), ())), preferred_element_type=F32)


def _dotf(a, b, dims):
    return _dot(a.astype(MXU_DTYPE), b.astype(MXU_DTYPE), dims)


NN = ((1,), (0,))
NT = ((1,), (1,))
TN = ((0,), (0,))


def _matmul(a, b, *, name, out_dtype, tm, tn, trans_b=False, residual=None, layer=None):
    m, k = a.shape
    n = b.shape[-2] if trans_b else b.shape[-1]
    tm = _tile(m, tm, 8)
    tn = _tile(n, tn)
    dims = NT if trans_b else NN
    lead = () if layer is None else (None,)
    lidx = () if layer is None else (layer,)

    def body(*refs):
        if residual is None:
            a_ref, b_ref, o_ref = refs
        else:
            a_ref, b_ref, r_ref, o_ref = refs
        acc = _dot(a_ref[...].astype(MXU_DTYPE), b_ref[...].astype(MXU_DTYPE), dims)
        if residual is not None:
            acc = acc + r_ref[...]
        o_ref[...] = acc.astype(o_ref.dtype)

    b_spec = (pl.BlockSpec(lead + (tn, k), lambda i, j: lidx + (j, 0)) if trans_b
              else pl.BlockSpec(lead + (k, tn), lambda i, j: lidx + (0, j)))
    in_specs = [pl.BlockSpec((tm, k), lambda i, j: (i, 0)), b_spec]
    args = [a, b]
    if residual is not None:
        in_specs.append(pl.BlockSpec((tm, tn), lambda i, j: (i, j)))
        args.append(residual)
    return pl.pallas_call(
        body, name=name, grid=(m // tm, n // tn), in_specs=in_specs,
        out_specs=pl.BlockSpec((tm, tn), lambda i, j: (i, j)),
        out_shape=jax.ShapeDtypeStruct((m, n), out_dtype),
        compiler_params=_params(("parallel", "parallel")),
    )(*args)


def _matmul_tn(a, g, *, name, tk, tn, tm=512):
    m, k = a.shape
    n = g.shape[1]
    tk = _tile(k, tk)
    tn = _tile(n, tn)
    tm = _tile(m, tm, 8)

    def body(a_ref, g_ref, o_ref):
        @pl.when(pl.program_id(2) == 0)
        def _():
            o_ref[...] = jnp.zeros_like(o_ref)

        o_ref[...] += _dot(a_ref[...].astype(MXU_DTYPE), g_ref[...].astype(MXU_DTYPE), TN)

    return pl.pallas_call(
        body, name=name, grid=(k // tk, n // tn, m // tm),
        in_specs=[pl.BlockSpec((tm, tk), lambda i, j, s: (s, i)),
                  pl.BlockSpec((tm, tn), lambda i, j, s: (s, j))],
        out_specs=pl.BlockSpec((tk, tn), lambda i, j, s: (i, j)),
        out_shape=jax.ShapeDtypeStruct((k, n), F32),
        compiler_params=_params(("parallel", "parallel", "arbitrary")),
    )(a, g)


ROWS = 512


def _rms_fwd(h, g, *, name):
    s, d = h.shape

    def body(h_ref, g_ref, o_ref):
        x = h_ref[...]
        r = lax.rsqrt(jnp.mean(x * x, axis=-1, keepdims=True) + EPS)
        o_ref[...] = (x * r * g_ref[...]).astype(o_ref.dtype)

    return pl.pallas_call(
        body, name=name, grid=(s // ROWS,),
        in_specs=[pl.BlockSpec((ROWS, d), lambda i: (i, 0)), pl.BlockSpec((1, d), lambda i: (0, 0))],
        out_specs=pl.BlockSpec((ROWS, d), lambda i: (i, 0)),
        out_shape=jax.ShapeDtypeStruct((s, d), MXU_DTYPE),
        compiler_params=_params(("parallel",)),
    )(h, g.reshape(1, d))


def _rms_bwd(h, g, dxn, dres, *, name):
    s, d = h.shape

    def body(h_ref, g_ref, dxn_ref, dres_ref, dh_ref, dg_ref):
        @pl.when(pl.program_id(0) == 0)
        def _():
            dg_ref[...] = jnp.zeros_like(dg_ref)

        x = h_ref[...]
        r = lax.rsqrt(jnp.mean(x * x, axis=-1, keepdims=True) + EPS)
        xhat = x * r
        dxn = dxn_ref[...].astype(F32)
        dg_ref[...] += jnp.sum(dxn * xhat, axis=0, keepdims=True)
        dxh = dxn * g_ref[...]
        dh_ref[...] = dres_ref[...] + r * (dxh - xhat * jnp.mean(dxh * xhat, axis=-1, keepdims=True))

    row = pl.BlockSpec((ROWS, d), lambda i: (i, 0))
    vec = pl.BlockSpec((1, d), lambda i: (0, 0))
    return pl.pallas_call(
        body, name=name, grid=(s // ROWS,), in_specs=[row, vec, row, row], out_specs=[row, vec],
        out_shape=[jax.ShapeDtypeStruct((s, d), F32), jax.ShapeDtypeStruct((1, d), F32)],
        compiler_params=_params(("arbitrary",)),
    )(h, g.reshape(1, d), dxn, dres)


def _loss_head(h, g, target):
    s, d = h.shape

    def body(h_ref, g_ref, t_ref, loss_ref, dh_ref, dg_ref):
        @pl.when(pl.program_id(0) == 0)
        def _():
            loss_ref[...] = jnp.zeros_like(loss_ref)
            dg_ref[...] = jnp.zeros_like(dg_ref)

        x = h_ref[...]
        r = lax.rsqrt(jnp.mean(x * x, axis=-1, keepdims=True) + EPS)
        xhat = x * r
        err = xhat * g_ref[...] - t_ref[...]
        loss_ref[...] += 0.5 * jnp.sum(jnp.mean(err * err, axis=-1, keepdims=True), axis=0, keepdims=True)
        dy = err / d
        dg_ref[...] += jnp.sum(dy * xhat, axis=0, keepdims=True)
        dxh = dy * g_ref[...]
        dh_ref[...] = r * (dxh - xhat * jnp.mean(dxh * xhat, axis=-1, keepdims=True))

    row = pl.BlockSpec((ROWS, d), lambda i: (i, 0))
    vec = pl.BlockSpec((1, d), lambda i: (0, 0))
    one = pl.BlockSpec((1, 1), lambda i: (0, 0))
    return pl.pallas_call(
        body, name="loss_head", grid=(s // ROWS,), in_specs=[row, vec, row], out_specs=[one, row, vec],
        out_shape=[jax.ShapeDtypeStruct((1, 1), F32), jax.ShapeDtypeStruct((s, d), F32),
                   jax.ShapeDtypeStruct((1, d), F32)],
        compiler_params=_params(("arbitrary",)),
    )(h, g.reshape(1, d), target)


def _t5_bucket(dist):
    max_exact = N_BUCKETS // 2
    dd = np.maximum(dist, 0)
    large = max_exact + (np.log(np.maximum(dd, 1) / max_exact) / np.log(REL_MAX_DIST / max_exact)
                         * (N_BUCKETS - max_exact)).astype(np.int32)
    large = np.minimum(large, N_BUCKETS - 1)
    return np.where(dd < max_exact, dd, large).astype(np.int32)


def _bucket_table():
    qq = np.arange(QBLK)[:, None]
    kk = np.arange(QBLK)[None, :]
    out = np.zeros((len(BRANCH_DIL), 2, QBLK, QBLK), np.int32)
    for b, dil in enumerate(BRANCH_DIL):
        out[b, 0] = _t5_bucket((qq - kk + QBLK) * dil)
        out[b, 1] = _t5_bucket((qq - kk) * dil)
    return out


BIAS_TILE = 2 * QBLK


def _bias_build(rel_bias):
    idx = jnp.asarray(_bucket_table())

    def body(idx_ref, rb_ref, o_ref):
        ch = pl.program_id(1)
        row = lax.broadcasted_iota(jnp.int32, (QBLK, QBLK), 0)
        col = lax.broadcasted_iota(jnp.int32, (QBLK, QBLK), 1)
        for part in range(2):
            ids = idx_ref[0, 1 - part]
            valid = (col <= row) if part == 0 else (col >= row)
            for h in range(2):
                acc = jnp.zeros((QBLK, QBLK), F32)
                for b in range(N_BUCKETS):
                    acc = jnp.where(ids == b, rb_ref[b, 2 * ch + h], acc)
                o_ref[0, 0, QBLK * h:QBLK * (h + 1), QBLK * part:QBLK * (part + 1)] = jnp.where(valid, acc, NEG_INF)

    return pl.pallas_call(
        body, name="attn_bias_build", grid=(len(BRANCH_DIL), N_HEADS // 2),
        in_specs=[pl.BlockSpec((1, 2, QBLK, QBLK), lambda b, c: (b, 0, 0, 0)),
                  pl.BlockSpec(memory_space=pltpu.SMEM)],
        out_specs=pl.BlockSpec((1, 1, BIAS_TILE, BIAS_TILE), lambda b, c: (b, c, 0, 0)),
        out_shape=jax.ShapeDtypeStruct((len(BRANCH_DIL), N_HEADS // 2, BIAS_TILE, BIAS_TILE), F32),
        compiler_params=_params(("parallel", "parallel")),
    )(idx, rel_bias)


def _bias_reduce(dbias):
    idx = jnp.asarray(_bucket_table())
    nb = len(BRANCH_DIL)

    def body(idx_ref, d_ref, o_ref):
        def per_bucket(b, carry):
            for h in range(N_HEADS):
                tot = jnp.zeros((), F32)
                for br in range(nb):
                    for part in range(2):
                        tile = d_ref[br, h // 2, QBLK * (h % 2):QBLK * (h % 2 + 1), QBLK * part:QBLK * (part + 1)]
                        tot = tot + jnp.sum(jnp.where(idx_ref[br, 1 - part] == b, tile, 0.0))
                o_ref[b, h] = tot
            return carry

        lax.fori_loop(0, N_BUCKETS, per_bucket, 0)

    return pl.pallas_call(
        body, name="attn_bias_reduce",
        in_specs=[pl.BlockSpec(memory_space=pltpu.VMEM), pl.BlockSpec(memory_space=pltpu.VMEM)],
        out_specs=pl.BlockSpec(memory_space=pltpu.SMEM),
        out_shape=jax.ShapeDtypeStruct((N_BUCKETS, N_HEADS), F32),
        compiler_params=pltpu.CompilerParams(vmem_limit_bytes=VMEM_LIMIT_BYTES),
    )(idx, dbias)


def _band_masks(c):
    row = lax.broadcasted_iota(jnp.int32, (QBLK, QBLK), 0)
    col = lax.broadcasted_iota(jnp.int32, (QBLK, QBLK), 1)
    mask_cur = col <= row
    mask_prev = jnp.logical_and(col >= row, c > 0)
    return mask_prev, mask_cur


def _attn_specs(dil):
    blk = (QBLK, ATTN_W)
    q = pl.BlockSpec(blk, lambda r, c: (c, 3 * r))
    kp = pl.BlockSpec(blk, lambda r, c: (jnp.maximum(c - 1, 0), 3 * r + 1))
    kc = pl.BlockSpec(blk, lambda r, c: (c, 3 * r + 1))
    vp = pl.BlockSpec(blk, lambda r, c: (jnp.maximum(c - 1, 0), 3 * r + 2))
    vc = pl.BlockSpec(blk, lambda r, c: (c, 3 * r + 2))
    return [q, kp, kc, vp, vc]


def _attn_fwd_branch(qkv, bias, state, *, branch, last):
    dil = BRANCH_DIL[branch]
    s = qkv.shape[0]
    n = s // dil
    nblk = n // QBLK
    first = state is None

    def body(*refs):
        q_ref, kp_ref, kc_ref, vp_ref, vc_ref, b_ref = refs[:6]
        if first:
            outs = refs[6:]
        else:
            acc_ref, m_ref, l_ref = refs[6:9]
            outs = refs[9:]
        mask_prev, mask_cur = _band_masks(pl.program_id(1))
        for h in range(N_HEADS):
            sl = slice(HEAD_DIM * h, HEAD_DIM * (h + 1))
            qh = q_ref[:, sl]
            s_c = _dot(qh, kc_ref[:, sl], NT) * ATTN_SCALE + b_ref[0, 1, h]
            s_p = _dot(qh, kp_ref[:, sl], NT) * ATTN_SCALE + b_ref[0, 0, h]
            s_c = jnp.where(mask_cur, s_c, NEG_INF)
            s_p = jnp.where(mask_prev, s_p, NEG_INF)
            m_blk = jnp.maximum(jnp.max(s_c, axis=-1, keepdims=True), jnp.max(s_p, axis=-1, keepdims=True))
            if first:
                m_new = m_blk
            else:
                m_old = m_ref[:, sl][:, :1]
                m_new = jnp.maximum(m_old, m_blk)
            p_c = jnp.exp(s_c - m_new)
            p_p = jnp.exp(s_p - m_new)
            l_new = jnp.sum(p_c, axis=-1, keepdims=True) + jnp.sum(p_p, axis=-1, keepdims=True)
            acc = (_dot(p_c.astype(MXU_DTYPE), vc_ref[:, sl], NN)
                   + _dot(p_p.astype(MXU_DTYPE), vp_ref[:, sl], NN))
            if not first:
                alpha = jnp.exp(m_old - m_new)
                l_new = l_new + alpha * l_ref[:, sl][:, :1]
                acc = acc + alpha * acc_ref[:, sl]
            if last:
                outs[0][:, sl] = acc / l_new
                outs[1][:, sl] = jnp.broadcast_to(m_new + jnp.log(l_new), (QBLK, HEAD_DIM))
            else:
                outs[0][:, sl] = acc
                outs[1][:, sl] = jnp.broadcast_to(m_new, (QBLK, HEAD_DIM))
                outs[2][:, sl] = jnp.broadcast_to(l_new, (QBLK, HEAD_DIM))

    st_spec = pl.BlockSpec((QBLK, ATTN_W), lambda r, c: (c, r))
    in_specs = _attn_specs(dil) + [pl.BlockSpec((1, 2, N_HEADS, QBLK, QBLK), lambda r, c: (branch, 0, 0, 0, 0))]
    qv = qkv.reshape(n, dil * 3 * ATTN_W)
    args = [qv] * 5 + [bias]
    if not first:
        in_specs += [st_spec] * 3
        args += [t.reshape(n, dil * ATTN_W) for t in state]
    n_out = 2 if last else 3
    outs = pl.pallas_call(
        body, name=f"attn_fwd_b{branch}", grid=(dil, nblk), in_specs=in_specs,
        out_specs=[st_spec] * n_out,
        out_shape=[jax.ShapeDtypeStruct((n, dil * ATTN_W), F32)] * n_out,
        compiler_params=_params(("parallel", "parallel")),
    )(*args)
    return tuple(t.reshape(s, ATTN_W) for t in outs)


def _attn_fwd(qkv, bias):
    state = None
    for b in range(len(BRANCH_DIL)):
        state = _attn_fwd_branch(qkv, bias, state, branch=b, last=(b == len(BRANCH_DIL) - 1))
    return state


def _attn_bwd_branch(qkv, bias, o, lse, do, *, branch):
    dil = BRANCH_DIL[branch]
    s = qkv.shape[0]
    n = s // dil
    nblk = n // QBLK

    def body(q_ref, kp_ref, kc_ref, vp_ref, vc_ref, b_ref, o_ref, l_ref, do_ref,
             dq_ref, dka_ref, dkb_ref, dva_ref, dvb_ref, db_ref):
        @pl.when(jnp.logical_and(pl.program_id(0) == 0, pl.program_id(1) == 0))
        def _():
            db_ref[...] = jnp.zeros_like(db_ref)

        mask_prev, mask_cur = _band_masks(pl.program_id(1))
        for h in range(N_HEADS):
            sl = slice(HEAD_DIM * h, HEAD_DIM * (h + 1))
            qh = q_ref[:, sl]
            doh = do_ref[:, sl]
            lh = l_ref[:, sl][:, :1]
            delta = jnp.sum(doh * o_ref[:, sl], axis=-1, keepdims=True)
            do_m = doh.astype(MXU_DTYPE)
            s_c = _dot(qh, kc_ref[:, sl], NT) * ATTN_SCALE + b_ref[0, 1, h]
            s_p = _dot(qh, kp_ref[:, sl], NT) * ATTN_SCALE + b_ref[0, 0, h]
            p_c = jnp.exp(jnp.where(mask_cur, s_c, NEG_INF) - lh)
            p_p = jnp.exp(jnp.where(mask_prev, s_p, NEG_INF) - lh)
            ds_c = p_c * (_dot(do_m, vc_ref[:, sl], NT) - delta)
            ds_p = p_p * (_dot(do_m, vp_ref[:, sl], NT) - delta)
            db_ref[0, 1, h] += ds_c
            db_ref[0, 0, h] += ds_p
            ds_c_m = ds_c.astype(MXU_DTYPE)
            ds_p_m = ds_p.astype(MXU_DTYPE)
            dq = _dot(ds_c_m, kc_ref[:, sl], NN) + _dot(ds_p_m, kp_ref[:, sl], NN)
            dq_ref[:, sl] = (dq * ATTN_SCALE).astype(dq_ref.dtype)
            dka_ref[:, sl] = (_dot(ds_c_m, qh, TN) * ATTN_SCALE).astype(dka_ref.dtype)
            dkb_ref[:, sl] = (_dot(ds_p_m, qh, TN) * ATTN_SCALE).astype(dkb_ref.dtype)
            dva_ref[:, sl] = _dot(p_c.astype(MXU_DTYPE), do_m, TN).astype(dva_ref.dtype)
            dvb_ref[:, sl] = _dot(p_p.astype(MXU_DTYPE), do_m, TN).astype(dvb_ref.dtype)

    st_spec = pl.BlockSpec((QBLK, ATTN_W), lambda r, c: (c, r))
    b_in = pl.BlockSpec((1, 2, N_HEADS, QBLK, QBLK), lambda r, c: (branch, 0, 0, 0, 0))
    b_out = pl.BlockSpec((1, 2, N_HEADS, QBLK, QBLK), lambda r, c: (0, 0, 0, 0, 0))
    qv = qkv.reshape(n, dil * 3 * ATTN_W)
    view = lambda t: t.reshape(n, dil * ATTN_W)
    outs = pl.pallas_call(
        body, name=f"attn_bwd_b{branch}", grid=(dil, nblk),
        in_specs=_attn_specs(dil) + [b_in, st_spec, st_spec, st_spec],
        out_specs=[st_spec] * 5 + [b_out],
        out_shape=[jax.ShapeDtypeStruct((n, dil * ATTN_W), MXU_DTYPE)] * 5
        + [jax.ShapeDtypeStruct((1, 2, N_HEADS, QBLK, QBLK), F32)],
        compiler_params=_params(("arbitrary", "arbitrary")),
    )(qv, qv, qv, qv, qv, bias, view(o), view(lse), view(do))
    return tuple(t.reshape(s, ATTN_W) for t in outs[:5]) + (outs[5],)


def _attn_bwd(qkv, bias, o, lse, do):
    s = qkv.shape[0]
    nb = s // QBLK
    parts = [_attn_bwd_branch(qkv, bias, o, lse, do, branch=b) for b in range(len(BRANCH_DIL))]
    dbias = jnp.concatenate([p[5] for p in parts], axis=0)

    def body(*refs):
        o_ref = refs[-1]
        i = pl.program_id(0)
        dq = jnp.zeros((QBLK, ATTN_W), F32)
        dk = jnp.zeros((QBLK, ATTN_W), F32)
        dv = jnp.zeros((QBLK, ATTN_W), F32)
        for b, dil in enumerate(BRANCH_DIL):
            dq_ref, dka_ref, dkb_ref, dva_ref, dvb_ref = refs[5 * b:5 * b + 5]
            inside = i + dil < nb
            dq = dq + dq_ref[...].astype(F32)
            dk = dk + dka_ref[...].astype(F32) + jnp.where(inside, dkb_ref[...].astype(F32), 0.0)
            dv = dv + dva_ref[...].astype(F32) + jnp.where(inside, dvb_ref[...].astype(F32), 0.0)
        o_ref[:, 0:ATTN_W] = dq.astype(o_ref.dtype)
        o_ref[:, ATTN_W:2 * ATTN_W] = dk.astype(o_ref.dtype)
        o_ref[:, 2 * ATTN_W:3 * ATTN_W] = dv.astype(o_ref.dtype)

    in_specs, args = [], []
    for b, dil in enumerate(BRANCH_DIL):
        here = pl.BlockSpec((QBLK, ATTN_W), lambda i: (i, 0))
        ahead = pl.BlockSpec((QBLK, ATTN_W), functools.partial(lambda i, d: (jnp.minimum(i + d, nb - 1), 0), d=dil))
        in_specs += [here, here, ahead, here, ahead]
        args += list(parts[b][:5])
    dqkv = pl.pallas_call(
        body, name="attn_bwd_sum", grid=(nb,), in_specs=in_specs,
        out_specs=pl.BlockSpec((QBLK, 3 * ATTN_W), lambda i: (i, 0)),
        out_shape=jax.ShapeDtypeStruct((s, 3 * ATTN_W), MXU_DTYPE),
        compiler_params=_params(("parallel",)),
    )(*args)
    return dqkv, dbias


ATTN_IO_DTYPE = F32
ABLK = 2048
N_CHUNK = ATTN_W // 128


def _rows(start, dil):
    if dil > 1:
        return pl.ds(start, QBLK, stride=dil)
    return pl.ds(pl.multiple_of(start, QBLK), QBLK)


def _low_head():
    return lax.broadcasted_iota(jnp.int32, (QBLK, 128), 1) < HEAD_DIM


def _head_split(t):
    low = _low_head()
    zero = jnp.zeros_like(t)
    return jnp.where(low, t, zero), jnp.where(low, zero, t)


def _tile_bias(b_ref, branch, first):
    bias = b_ref[branch]
    if first is None:
        return bias
    col = lax.broadcasted_iota(jnp.int32, (BIAS_TILE, BIAS_TILE), 1)
    return jnp.where(jnp.logical_and(first, col >= QBLK), NEG_INF, bias)


def _loop(n, fn):
    if n == 1:
        fn(jnp.int32(0), 0)
    elif n > 1:
        lax.fori_loop(0, n, fn, 0, unroll=2)


def _for_each_tile(tile, c):
    for branch, dil in enumerate(BRANCH_DIL):
        span = QBLK * dil

        def edge(r, carry, branch=branch, span=span):
            tile(branch, r, False, ABLK - span + r, c == 0)
            return carry

        def inner(t, carry, branch=branch, span=span, dil=dil):
            start = (1 + t // dil) * span + t % dil
            tile(branch, start, True, start - span, None)
            return carry

        _loop(dil, edge)
        _loop((ABLK // span - 1) * dil, inner)


def _attn_chunk_specs(nb):
    blk = (None, ABLK, 128)
    prev = lambda c: jnp.maximum(c - 1, 0)
    return [pl.BlockSpec(blk, lambda ch, c: (ch, c, 0)),
            pl.BlockSpec(blk, lambda ch, c: (N_CHUNK + ch, c, 0)),
            pl.BlockSpec(blk, lambda ch, c: (2 * N_CHUNK + ch, c, 0)),
            pl.BlockSpec(blk, lambda ch, c: (N_CHUNK + ch, prev(c), 0)),
            pl.BlockSpec(blk, lambda ch, c: (2 * N_CHUNK + ch, prev(c), 0)),
            pl.BlockSpec((len(BRANCH_DIL), None, BIAS_TILE, BIAS_TILE), lambda ch, c: (0, ch, 0, 0))]


def _in_proj(xn, w_in, layer):
    s, k = xn.shape
    tm = 512
    nch = O_SGU // 128

    def body(x_ref, w_ref, qkv_ref, zs_ref, us_ref):
        acc = _dot(x_ref[...].astype(MXU_DTYPE), w_ref[...].astype(MXU_DTYPE), NN)
        for j in range(nch):
            blk = acc[:, 128 * j:128 * (j + 1)]
            if j < N_CHUNK:
                blk = blk * ATTN_SCALE
            qkv_ref[j] = blk.astype(qkv_ref.dtype)
        zs_ref[...] = acc[:, O_SGU:O_SSM]
        us_ref[...] = acc[:, O_SSM:]

    n = w_in.shape[-1]
    return pl.pallas_call(
        body, name="in_proj", grid=(s // tm,),
        in_specs=[pl.BlockSpec((tm, k), lambda i: (i, 0)), pl.BlockSpec((None, k, n), lambda i: (layer, 0, 0))],
        out_specs=[pl.BlockSpec((nch, tm, 128), lambda i: (0, i, 0)),
                   pl.BlockSpec((tm, O_SSM - O_SGU), lambda i: (i, 0)), pl.BlockSpec((tm, n - O_SSM), lambda i: (i, 0))],
        out_shape=[jax.ShapeDtypeStruct((nch, s, 128), ATTN_IO_DTYPE),
                   jax.ShapeDtypeStruct((s, O_SSM - O_SGU), F32), jax.ShapeDtypeStruct((s, n - O_SSM), F32)],
        compiler_params=_params(("parallel",)),
    )(xn, w_in)


def _attn2_fwd(qkv_c, bias):
    s = qkv_c.shape[1]
    nb = s // ABLK
    last = len(BRANCH_DIL) - 1

    def body(q_ref, kc_ref, vc_ref, kp_ref, vp_ref, b_ref, o_ref, l_ref, acc_s, m_s, l_s):
        low = _low_head()
        e_st = jnp.concatenate(_head_split(jnp.ones((QBLK, 128), MXU_DTYPE)) * 2, axis=0)

        def tile(branch, start, prev_in_block, pstart, first):
            dil = BRANCH_DIL[branch]
            rq, rp = _rows(start, dil), _rows(pstart, dil)
            k_ref, v_ref = (kc_ref, vc_ref) if prev_in_block else (kp_ref, vp_ref)
            q_st = jnp.concatenate(_head_split(q_ref[rq, :].astype(MXU_DTYPE)), axis=0)
            k_st = jnp.concatenate([kc_ref[rq, :].astype(MXU_DTYPE), k_ref[rp, :].astype(MXU_DTYPE)], axis=0)
            v_st = jnp.concatenate(_head_split(vc_ref[rq, :].astype(MXU_DTYPE))
                                   + _head_split(v_ref[rp, :].astype(MXU_DTYPE)), axis=0)
            sc = _dot(q_st, k_st, NT) + _tile_bias(b_ref, branch, first)
            m_new = jnp.max(sc, axis=-1, keepdims=True)
            if branch > 0:
                m_old2 = m_s[rq, :]
                m_old = jnp.concatenate([m_old2[:, 0:1], m_old2[:, HEAD_DIM:HEAD_DIM + 1]], axis=0)
                m_new = jnp.maximum(m_old, m_new)
                alpha = jnp.exp(m_old - m_new)
            p = jnp.exp(sc - m_new).astype(MXU_DTYPE)
            lhs = jnp.concatenate([p[:QBLK, :QBLK], p[QBLK:, :QBLK], p[:QBLK, QBLK:], p[QBLK:, QBLK:]], axis=1)
            acc2 = _dot(lhs, v_st, NN)
            sum2 = _dot(lhs, e_st, NN)
            m2 = jnp.where(low, m_new[:QBLK], m_new[QBLK:])
            if branch > 0:
                a2 = jnp.where(low, alpha[:QBLK], alpha[QBLK:])
                acc2 = acc2 + a2 * acc_s[rq, :]
                sum2 = sum2 + a2 * l_s[rq, :]
            if branch == last:
                o_ref[rq, :] = acc2 / sum2
                l_ref[rq, :] = m2 + jnp.log(sum2)
            else:
                acc_s[rq, :] = acc2
                m_s[rq, :] = m2
                l_s[rq, :] = sum2

        _for_each_tile(tile, pl.program_id(1))

    out_spec = pl.BlockSpec((None, ABLK, 128), lambda ch, c: (ch, c, 0))
    return pl.pallas_call(
        body, name="attn_fwd", grid=(N_CHUNK, nb), in_specs=_attn_chunk_specs(nb),
        out_specs=[out_spec, out_spec],
        out_shape=[jax.ShapeDtypeStruct((N_CHUNK, s, 128), F32)] * 2,
        scratch_shapes=[pltpu.VMEM((ABLK, 128), F32)] * 3,
        compiler_params=_params(("parallel", "arbitrary")),
    )(qkv_c, qkv_c, qkv_c, qkv_c, qkv_c, bias)


def _attn2_bwd(qkv_c, bias, lse_c, delta_c, do_c):
    s = qkv_c.shape[1]
    nb = s // ABLK
    nbr = len(BRANCH_DIL)

    def body(q_ref, kc_ref, vc_ref, kp_ref, vp_ref, b_ref, l_ref, dl_ref, do_ref,
             dq_ref, dk_ref, dv_ref, *rest):
        ek_refs, ev_refs, db_ref = rest[:nbr], rest[nbr:2 * nbr], rest[2 * nbr]
        c = pl.program_id(1)

        @pl.when(c == 0)
        def _():
            db_ref[...] = jnp.zeros_like(db_ref)

        for r in (dq_ref, dk_ref, dv_ref) + tuple(ek_refs) + tuple(ev_refs):
            r[...] = jnp.zeros_like(r)

        def tile(branch, start, prev_in_block, pstart, first):
            dil = BRANCH_DIL[branch]
            rq, rp = _rows(start, dil), _rows(pstart, dil)
            k_ref, v_ref = (kc_ref, vc_ref) if prev_in_block else (kp_ref, vp_ref)
            kc2 = kc_ref[rq, :].astype(MXU_DTYPE)
            kp2 = k_ref[rp, :].astype(MXU_DTYPE)
            q_st = jnp.concatenate(_head_split(q_ref[rq, :].astype(MXU_DTYPE)), axis=0)
            do_st = jnp.concatenate(_head_split(do_ref[rq, :].astype(MXU_DTYPE)), axis=0)
            k_st = jnp.concatenate([kc2, kp2], axis=0)
            v_st = jnp.concatenate([vc_ref[rq, :].astype(MXU_DTYPE), v_ref[rp, :].astype(MXU_DTYPE)], axis=0)
            kh_st = jnp.concatenate(_head_split(kc2) + _head_split(kp2), axis=0)
            lse2 = l_ref[rq, :]
            del2 = dl_ref[rq, :]
            lse_st = jnp.concatenate([lse2[:, 0:1], lse2[:, HEAD_DIM:HEAD_DIM + 1]], axis=0)
            del_st = jnp.concatenate([del2[:, 0:1], del2[:, HEAD_DIM:HEAD_DIM + 1]], axis=0)
            p = jnp.exp(_dot(q_st, k_st, NT) + _tile_bias(b_ref, branch, first) - lse_st)
            ds = p * (_dot(do_st, v_st, NT) - del_st)
            db_ref[branch] += ds
            ds = ds.astype(MXU_DTYPE)
            p = p.astype(MXU_DTYPE)
            lhs = jnp.concatenate([ds[:QBLK, :QBLK], ds[QBLK:, :QBLK], ds[:QBLK, QBLK:], ds[QBLK:, QBLK:]], axis=1)
            dk_st = _dot(ds, q_st, TN)
            dv_st = _dot(p, do_st, TN)
            dq_ref[rq, :] += _dot(lhs, kh_st, NN)
            dk_ref[rq, :] += dk_st[:QBLK]
            dv_ref[rq, :] += dv_st[:QBLK]
            if prev_in_block:
                dk_ref[rp, :] += dk_st[QBLK:]
                dv_ref[rp, :] += dv_st[QBLK:]
            else:
                ek_refs[branch][rq, :] = dk_st[QBLK:]
                ev_refs[branch][rq, :] = dv_st[QBLK:]

        _for_each_tile(tile, c)

    blk = pl.BlockSpec((None, ABLK, 128), lambda ch, c: (ch, c, 0))
    outs = pl.pallas_call(
        body, name="attn_bwd", grid=(N_CHUNK, nb), in_specs=_attn_chunk_specs(nb) + [blk, blk, blk],
        out_specs=[blk] * (3 + 2 * nbr) + [pl.BlockSpec((nbr, None, BIAS_TILE, BIAS_TILE), lambda ch, c: (0, ch, 0, 0))],
        out_shape=[jax.ShapeDtypeStruct((N_CHUNK, s, 128), F32)] * (3 + 2 * nbr)
        + [jax.ShapeDtypeStruct((nbr, N_HEADS // 2, BIAS_TILE, BIAS_TILE), F32)],
        compiler_params=_params(("arbitrary", "arbitrary")),
    )(qkv_c, qkv_c, qkv_c, qkv_c, qkv_c, bias, lse_c, delta_c, do_c)
    return outs[0], outs[1], outs[2], outs[3:3 + nbr], outs[3 + nbr:3 + 2 * nbr], outs[3 + 2 * nbr]


def _attn2_bwd_sum(dq, dk, dv, ek, ev, dzs, dus):
    s = dq.shape[1]
    nrb = s // QBLK
    per_blk = ABLK // QBLK
    nbr = len(BRANCH_DIL)

    def body(*refs):
        dq_ref, dk_ref, dv_ref = refs[:3]
        ek_refs, ev_refs = refs[3:3 + nbr], refs[3 + nbr:3 + 2 * nbr]
        dzs_ref, dus_ref, o_ref = refs[3 + 2 * nbr:]
        i = pl.program_id(0)
        dkt, dvt = dk_ref[...], dv_ref[...]
        for b, dil in enumerate(BRANCH_DIL):
            j = i + dil
            ok = jnp.logical_and(j < nrb, j % per_blk < dil)
            dkt = dkt + jnp.where(ok, ek_refs[b][...], 0.0)
            dvt = dvt + jnp.where(ok, ev_refs[b][...], 0.0)
        for ch in range(N_CHUNK):
            o_ref[:, 128 * ch:128 * (ch + 1)] = (dq_ref[ch] * ATTN_SCALE).astype(o_ref.dtype)
            o_ref[:, ATTN_W + 128 * ch:ATTN_W + 128 * (ch + 1)] = dkt[ch].astype(o_ref.dtype)
            o_ref[:, 2 * ATTN_W + 128 * ch:2 * ATTN_W + 128 * (ch + 1)] = dvt[ch].astype(o_ref.dtype)
        o_ref[:, O_SGU:O_SSM] = dzs_ref[...].astype(o_ref.dtype)
        o_ref[:, O_SSM:] = dus_ref[...].astype(o_ref.dtype)

    here = pl.BlockSpec((N_CHUNK, QBLK, 128), lambda i: (0, i, 0))
    edge_specs = [pl.BlockSpec((N_CHUNK, QBLK, 128),
                               functools.partial(lambda i, d: (0, jnp.minimum(i + d, nrb - 1), 0), d=dil))
                  for dil in BRANCH_DIL]
    return pl.pallas_call(
        body, name="attn_bwd_sum", grid=(nrb,),
        in_specs=[here, here, here] + edge_specs + edge_specs
        + [pl.BlockSpec((QBLK, 2 * SGU_W), lambda i: (i, 0)), pl.BlockSpec((QBLK, SSM_W), lambda i: (i, 0))],
        out_specs=pl.BlockSpec((QBLK, O_SSM + SSM_W), lambda i: (i, 0)),
        out_shape=jax.ShapeDtypeStruct((s, O_SSM + SSM_W), MXU_DTYPE),
        compiler_params=_params(("parallel",)),
    )(dq, dk, dv, *ek, *ev, dzs, dus)


SGU_ROWS = 512


def _sgu_norm(v_g):
    mu = jnp.mean(v_g, axis=-1, keepdims=True)
    cen = v_g - mu
    var = jnp.mean(cen * cen, axis=-1, keepdims=True)
    rstd = lax.rsqrt(var + EPS)
    return cen * rstd, rstd


def _sgu_fwd(zs, ln_g, ln_b, w_mask, b_t):
    s = zs.shape[0]
    nch = SGU_ROWS // SGU_CHUNK

    def body(z_ref, g_ref, b_ref, w_ref, bt_ref, o_ref):
        gz = _gelu(z_ref[...])
        for g in range(SGU_G):
            sl = slice(SGU_GW * g, SGU_GW * (g + 1))
            u_g = gz[:, sl]
            xhat, _ = _sgu_norm(gz[:, SGU_W + SGU_GW * g:SGU_W + SGU_GW * (g + 1)])
            vn = (xhat * g_ref[:, sl] + b_ref[:, sl]).astype(MXU_DTYPE)
            wg = w_ref[g].astype(MXU_DTYPE)
            for ci in range(nch):
                rs = slice(SGU_CHUNK * ci, SGU_CHUNK * (ci + 1))
                mixed = _dot(wg, vn[rs], NN) + bt_ref[:, g:g + 1]
                o_ref[rs, sl] = u_g[rs] * mixed

    full = lambda shape: pl.BlockSpec(shape, lambda i: tuple(0 for _ in shape))
    return pl.pallas_call(
        body, name="sgu_fwd", grid=(s // SGU_ROWS,),
        in_specs=[pl.BlockSpec((SGU_ROWS, 2 * SGU_W), lambda i: (i, 0)), full((1, SGU_W)), full((1, SGU_W)),
                  full((SGU_G, SGU_CHUNK, SGU_CHUNK)), full((SGU_CHUNK, SGU_G))],
        out_specs=pl.BlockSpec((SGU_ROWS, SGU_W), lambda i: (i, 0)),
        out_shape=jax.ShapeDtypeStruct((s, SGU_W), F32),
        compiler_params=_params(("parallel",)),
    )(zs, ln_g.reshape(1, SGU_W), ln_b.reshape(1, SGU_W), w_mask, b_t)


def _sgu_bwd(zs, ln_g, ln_b, w_mask, b_t, dy):
    s = zs.shape[0]
    nch = SGU_ROWS // SGU_CHUNK

    def body(z_ref, g_ref, b_ref, w_ref, bt_ref, dy_ref, dz_ref, dg_ref, dbb_ref, dw_ref, dbt_ref):
        @pl.when(pl.program_id(0) == 0)
        def _():
            dg_ref[...] = jnp.zeros_like(dg_ref)
            dbb_ref[...] = jnp.zeros_like(dbb_ref)
            dw_ref[...] = jnp.zeros_like(dw_ref)
            dbt_ref[...] = jnp.zeros_like(dbt_ref)

        z = z_ref[...]
        gz, dgelu = _gelu_pair(z)
        dy = dy_ref[...]
        for g in range(SGU_G):
            sl = slice(SGU_GW * g, SGU_GW * (g + 1))
            sv = slice(SGU_W + SGU_GW * g, SGU_W + SGU_GW * (g + 1))
            u_g = gz[:, sl]
            xhat, rstd = _sgu_norm(gz[:, sv])
            gain = g_ref[:, sl]
            vn = (xhat * gain + b_ref[:, sl]).astype(MXU_DTYPE)
            wg = w_ref[g].astype(MXU_DTYPE)
            dy_g = dy[:, sl]
            dvn_parts = []
            for ci in range(nch):
                rs = slice(SGU_CHUNK * ci, SGU_CHUNK * (ci + 1))
                mixed = _dot(wg, vn[rs], NN) + bt_ref[:, g:g + 1]
                dz_ref[rs, sl] = (dy_g[rs] * mixed * dgelu[rs, sl]).astype(dz_ref.dtype)
                dmixed = dy_g[rs] * u_g[rs]
                dm = dmixed.astype(MXU_DTYPE)
                dvn_parts.append(_dot(wg, dm, TN))
                dw_ref[g] += _dot(dm, vn[rs], NT)
                dbt_ref[:, g:g + 1] += jnp.sum(dmixed, axis=-1, keepdims=True)
            dvn = jnp.concatenate(dvn_parts, axis=0)
            dg_ref[:, sl] += jnp.sum(dvn * xhat, axis=0, keepdims=True)
            dbb_ref[:, sl] += jnp.sum(dvn, axis=0, keepdims=True)
            dxh = dvn * gain
            dv = rstd * (dxh - jnp.mean(dxh, axis=-1, keepdims=True)
                         - xhat * jnp.mean(dxh * xhat, axis=-1, keepdims=True))
            dz_ref[:, sv] = (dv * dgelu[:, sv]).astype(dz_ref.dtype)

    full = lambda shape: pl.BlockSpec(shape, lambda i: tuple(0 for _ in shape))
    return pl.pallas_call(
        body, name="sgu_bwd", grid=(s // SGU_ROWS,),
        in_specs=[pl.BlockSpec((SGU_ROWS, 2 * SGU_W), lambda i: (i, 0)), full((1, SGU_W)), full((1, SGU_W)),
                  full((SGU_G, SGU_CHUNK, SGU_CHUNK)), full((SGU_CHUNK, SGU_G)),
                  pl.BlockSpec((SGU_ROWS, SGU_W), lambda i: (i, 0))],
        out_specs=[pl.BlockSpec((SGU_ROWS, 2 * SGU_W), lambda i: (i, 0)), full((1, SGU_W)), full((1, SGU_W)),
                   full((SGU_G, SGU_CHUNK, SGU_CHUNK)), full((SGU_CHUNK, SGU_G))],
        out_shape=[jax.ShapeDtypeStruct((s, 2 * SGU_W), MXU_DTYPE), jax.ShapeDtypeStruct((1, SGU_W), F32),
                   jax.ShapeDtypeStruct((1, SGU_W), F32), jax.ShapeDtypeStruct((SGU_G, SGU_CHUNK, SGU_CHUNK), F32),
                   jax.ShapeDtypeStruct((SGU_CHUNK, SGU_G), F32)],
        compiler_params=_params(("arbitrary",)),
    )(zs, ln_g.reshape(1, SGU_W), ln_b.reshape(1, SGU_W), w_mask, b_t, dy)


def _ssm_discretize(a_re, a_im, log_dt, b_re, b_im):
    dt = jnp.exp(log_dt)[:, None]
    mag = jnp.exp(a_re * dt)
    ab_re = mag * jnp.cos(a_im * dt)
    ab_im = mag * jnp.sin(a_im * dt)
    den = a_re * a_re + a_im * a_im
    f_re = ((ab_re - 1.0) * a_re + ab_im * a_im) / den
    f_im = (ab_im * a_re - (ab_re - 1.0) * a_im) / den
    bb_re = f_re[:, :, None] * b_re - f_im[:, :, None] * b_im
    bb_im = f_re[:, :, None] * b_im + f_im[:, :, None] * b_re
    return ab_re, ab_im, bb_re, bb_im


def _ssm_operands(a_re, a_im, log_dt, b_re, b_im, c_re, c_im):
    ab_re, ab_im, bb_re, bb_im = _ssm_discretize(a_re, a_im, log_dt, b_re, b_im)
    eye = jnp.eye(SSM_G, dtype=F32)
    b_blk = jnp.einsum("pgnc,gh->gcphn", jnp.stack([bb_re, bb_im]), eye).reshape(SSM_W, 2 * NSTATE)
    c_mat = jnp.einsum("pgcn,gh->pgnhc", jnp.stack([c_re, -c_im]), eye).reshape(2 * NSTATE, SSM_W)
    a_row = jnp.stack([ab_re.reshape(NSTATE), ab_im.reshape(NSTATE)])
    p_re, p_im = a_row[0:1], a_row[1:2]
    while p_re.shape[0] < SSM_TSEG:
        l_re, l_im = p_re[-1:], p_im[-1:]
        p_re, p_im = (jnp.concatenate([p_re, p_re * l_re - p_im * l_im]),
                      jnp.concatenate([p_im, p_re * l_im + p_im * l_re]))
    p_tab = jnp.stack([p_re, p_im])
    return b_blk.astype(MXU_DTYPE), c_mat.astype(MXU_DTYPE), a_row, p_tab


def _lane_chunks():
    return [(lo, lo + SSM_LANE_CHUNK) for lo in range(0, NSTATE, SSM_LANE_CHUNK)]


def _seg_rows(j):
    return pl.ds(pl.multiple_of(j * SSM_NSEG, SSM_NSEG), SSM_NSEG)


def _to_segments(t):
    s, w = t.shape
    return t.reshape(s // SSM_TB, SSM_NSEG, SSM_TSEG, w).transpose(0, 2, 1, 3).reshape(s, w)


def _from_segments(t):
    s, w = t.shape
    return t.reshape(s // SSM_TB, SSM_TSEG, SSM_NSEG, w).transpose(0, 2, 1, 3).reshape(s, w)


def _ssm_local_scan(buf, a_ref, *, reverse):
    ends_re, ends_im = [], []
    for lo, hi in _lane_chunks():
        are = jnp.broadcast_to(a_ref[0:1, lo:hi], (SSM_NSEG, hi - lo))
        aim = jnp.broadcast_to(a_ref[1:2, lo:hi], (SSM_NSEG, hi - lo))
        if reverse:
            aim = -aim

        def step(jj, carry, lo=lo, hi=hi, are=are, aim=aim):
            xr, xi = carry
            j = (SSM_TSEG - 1 - jj) if reverse else jj
            tr = buf[_seg_rows(j), lo:hi]
            ti = buf[_seg_rows(j), NSTATE + lo:NSTATE + hi]
            nr = are * xr - aim * xi + tr
            ni = are * xi + aim * xr + ti
            buf[_seg_rows(j), lo:hi] = nr
            buf[_seg_rows(j), NSTATE + lo:NSTATE + hi] = ni
            return nr, ni

        zero = jnp.zeros((SSM_NSEG, hi - lo), F32)
        xr, xi = lax.fori_loop(0, SSM_TSEG, step, (zero, zero), unroll=4)
        ends_re.append(xr)
        ends_im.append(xi)
    return jnp.concatenate(ends_re, axis=1), jnp.concatenate(ends_im, axis=1)


def _ssm_entry_states(ends_re, ends_im, carry_ref, p_ref, entry_ref, *, reverse):
    at_re = p_ref[0, SSM_TSEG - 1:SSM_TSEG, :]
    at_im = p_ref[1, SSM_TSEG - 1:SSM_TSEG, :]
    if reverse:
        at_im = -at_im
    cur_re = carry_ref[0:1, 0:NSTATE]
    cur_im = carry_ref[0:1, NSTATE:2 * NSTATE]
    order = range(SSM_NSEG - 1, -1, -1) if reverse else range(SSM_NSEG)
    for i in order:
        entry_ref[0, i:i + 1, 0:NSTATE] = cur_re
        entry_ref[0, i:i + 1, NSTATE:2 * NSTATE] = cur_im
        nxt_re = ends_re[i:i + 1] + at_re * cur_re - at_im * cur_im
        nxt_im = ends_im[i:i + 1] + at_re * cur_im + at_im * cur_re
        cur_re, cur_im = nxt_re, nxt_im
    carry_ref[0:1, 0:NSTATE] = cur_re
    carry_ref[0:1, NSTATE:2 * NSTATE] = cur_im


def _ssm_fixup(buf, p_ref, entry_ref, *, reverse):
    for lo, hi in _lane_chunks():
        e_re = entry_ref[0, :, lo:hi]
        e_im = entry_ref[0, :, NSTATE + lo:NSTATE + hi]

        def step(j, carry, lo=lo, hi=hi, e_re=e_re, e_im=e_im):
            jp = (SSM_TSEG - 1 - j) if reverse else j
            pr = p_ref[0, pl.ds(jp, 1), lo:hi]
            pi = p_ref[1, pl.ds(jp, 1), lo:hi]
            if reverse:
                pi = -pi
            buf[_seg_rows(j), lo:hi] = buf[_seg_rows(j), lo:hi] + pr * e_re - pi * e_im
            buf[_seg_rows(j), NSTATE + lo:NSTATE + hi] = (buf[_seg_rows(j), NSTATE + lo:NSTATE + hi]
                                                           + pr * e_im + pi * e_re)
            return carry

        lax.fori_loop(0, SSM_TSEG, step, 0, unroll=4)


def _ssm_fwd(u, ops, d_skip, glu_w, glu_b):
    b_blk, c_mat, a_row, p_tab = ops
    s = u.shape[0]
    nblk = s // SSM_TB

    def body(u_ref, bb_ref, cm_ref, a_ref, p_ref, d_ref, gw_ref, gb_ref, y_ref, entry_ref, xbuf, carry):
        @pl.when(pl.program_id(0) == 0)
        def _():
            carry[...] = jnp.zeros_like(carry)

        uu = u_ref[...]
        xbuf[...] = _dotf(uu, bb_ref[...], NN)
        ends_re, ends_im = _ssm_local_scan(xbuf, a_ref, reverse=False)
        _ssm_entry_states(ends_re, ends_im, carry, p_ref, entry_ref, reverse=False)
        _ssm_fixup(xbuf, p_ref, entry_ref, reverse=False)
        y = _dotf(xbuf[...],cm_ref[...], NN) + d_ref[...] * uu
        y2 = _gelu(y)
        gate = jax.nn.sigmoid(_dot(y2.astype(MXU_DTYPE), gw_ref[...].astype(MXU_DTYPE), NN) + gb_ref[...])
        y_ref[...] = y2 * gate

    full = lambda shape: pl.BlockSpec(shape, lambda i: tuple(0 for _ in shape))
    y_seg, entry = pl.pallas_call(
        body, name="ssm_fwd", grid=(nblk,),
        in_specs=[pl.BlockSpec((SSM_TB, SSM_W), lambda i: (i, 0)), full(b_blk.shape), full(c_mat.shape),
                  full(a_row.shape), full(p_tab.shape), full((1, SSM_W)), full((SSM_W, SSM_W)), full((1, SSM_W))],
        out_specs=[pl.BlockSpec((SSM_TB, SSM_W), lambda i: (i, 0)),
                   pl.BlockSpec((1, SSM_NSEG, 2 * NSTATE), lambda i: (i, 0, 0))],
        out_shape=[jax.ShapeDtypeStruct((s, SSM_W), F32), jax.ShapeDtypeStruct((nblk, SSM_NSEG, 2 * NSTATE), F32)],
        scratch_shapes=[pltpu.VMEM((SSM_TB, 2 * NSTATE), F32), pltpu.VMEM((SSM_NSEG, 2 * NSTATE), F32)],
        compiler_params=_params(("arbitrary",)),
    )(_to_segments(u), b_blk, c_mat, a_row, p_tab, d_skip.reshape(1, SSM_W), glu_w, glu_b.reshape(1, SSM_W))
    return _from_segments(y_seg), entry


def _ssm_bwd(u, entry, ops, d_skip, glu_w, glu_b, dout):
    b_blk, c_mat, a_row, p_tab = ops
    s = u.shape[0]
    nblk = s // SSM_TB

    def body(u_ref, en_ref, bb_ref, cm_ref, a_ref, p_ref, d_ref, gw_ref, gb_ref, do_ref,
             du_ref, dbb_ref, dcm_ref, da_ref, dd_ref, dgw_ref, dgb_ref, xbuf, gbuf, gcarry, gentry):
        @pl.when(pl.program_id(0) == 0)
        def _():
            gcarry[...] = jnp.zeros_like(gcarry)
            for r in (dbb_ref, dcm_ref, da_ref, dd_ref, dgw_ref, dgb_ref):
                r[...] = jnp.zeros_like(r)

        uu = u_ref[...]
        xbuf[...] = _dotf(uu, bb_ref[...], NN)
        _ssm_local_scan(xbuf, a_ref, reverse=False)
        _ssm_fixup(xbuf, p_ref, en_ref, reverse=False)
        y = _dotf(xbuf[...],cm_ref[...], NN) + d_ref[...] * uu
        y2, dgelu = _gelu_pair(y)
        y2m = y2.astype(MXU_DTYPE)
        gwm = gw_ref[...].astype(MXU_DTYPE)
        gate = jax.nn.sigmoid(_dot(y2m, gwm, NN) + gb_ref[...])
        dout = do_ref[...]
        dpre = dout * y2 * gate * (1.0 - gate)
        dprem = dpre.astype(MXU_DTYPE)
        dy2 = dout * gate + _dot(dprem, gwm, NT)
        dgw_ref[...] += _dot(y2m, dprem, TN)
        dgb_ref[...] += jnp.sum(dpre, axis=0, keepdims=True)
        dy = dy2 * dgelu
        dd_ref[...] += jnp.sum(dy * uu, axis=0, keepdims=True)
        dcm_ref[...] += _dotf(xbuf[...],dy, TN)
        gbuf[...] = _dotf(dy, cm_ref[...], NT)
        gs_re, gs_im = _ssm_local_scan(gbuf, a_ref, reverse=True)
        _ssm_entry_states(gs_re, gs_im, gcarry, p_ref, gentry, reverse=True)
        _ssm_fixup(gbuf, p_ref, gentry, reverse=True)
        du_ref[...] = (_dotf(gbuf[...], bb_ref[...], NT) + d_ref[...] * dy).astype(du_ref.dtype)
        dbb_ref[...] += _dotf(uu, gbuf[...], TN)
        for lo, hi in _lane_chunks():
            def step(j, carry, lo=lo, hi=hi):
                acc_re, acc_im = carry
                g_re = gbuf[_seg_rows(j), lo:hi]
                g_im = gbuf[_seg_rows(j), NSTATE + lo:NSTATE + hi]
                x_re = xbuf[_seg_rows(j - 1), lo:hi]
                x_im = xbuf[_seg_rows(j - 1), NSTATE + lo:NSTATE + hi]
                return acc_re + g_re * x_re + g_im * x_im, acc_im + g_im * x_re - g_re * x_im

            g0_re = gbuf[_seg_rows(0), lo:hi]
            g0_im = gbuf[_seg_rows(0), NSTATE + lo:NSTATE + hi]
            e_re = en_ref[0, :, lo:hi]
            e_im = en_ref[0, :, NSTATE + lo:NSTATE + hi]
            init = (g0_re * e_re + g0_im * e_im, g0_im * e_re - g0_re * e_im)
            acc_re, acc_im = lax.fori_loop(1, SSM_TSEG, step, init, unroll=4)
            da_ref[0:1, lo:hi] += jnp.sum(acc_re, axis=0, keepdims=True)
            da_ref[1:2, lo:hi] += jnp.sum(acc_im, axis=0, keepdims=True)

    full = lambda shape: pl.BlockSpec(shape, lambda i: tuple(0 for _ in shape))
    rev = pl.BlockSpec((SSM_TB, SSM_W), lambda i: (nblk - 1 - i, 0))
    outs = pl.pallas_call(
        body, name="ssm_bwd", grid=(nblk,),
        in_specs=[rev, pl.BlockSpec((1, SSM_NSEG, 2 * NSTATE), lambda i: (nblk - 1 - i, 0, 0)),
                  full(b_blk.shape), full(c_mat.shape), full(a_row.shape), full(p_tab.shape),
                  full((1, SSM_W)), full((SSM_W, SSM_W)), full((1, SSM_W)), rev],
        out_specs=[rev, full(b_blk.shape), full(c_mat.shape), full(a_row.shape), full((1, SSM_W)),
                   full((SSM_W, SSM_W)), full((1, SSM_W))],
        out_shape=[jax.ShapeDtypeStruct((s, SSM_W), MXU_DTYPE), jax.ShapeDtypeStruct(b_blk.shape, F32),
                   jax.ShapeDtypeStruct(c_mat.shape, F32), jax.ShapeDtypeStruct(a_row.shape, F32),
                   jax.ShapeDtypeStruct((1, SSM_W), F32), jax.ShapeDtypeStruct((SSM_W, SSM_W), F32),
                   jax.ShapeDtypeStruct((1, SSM_W), F32)],
        scratch_shapes=[pltpu.VMEM((SSM_TB, 2 * NSTATE), F32), pltpu.VMEM((SSM_TB, 2 * NSTATE), F32),
                        pltpu.VMEM((SSM_NSEG, 2 * NSTATE), F32), pltpu.VMEM((1, SSM_NSEG, 2 * NSTATE), F32)],
        compiler_params=_params(("arbitrary",)),
    )(_to_segments(u), entry, b_blk, c_mat, a_row, p_tab, d_skip.reshape(1, SSM_W), glu_w, glu_b.reshape(1, SSM_W),
      _to_segments(dout))
    return (_from_segments(outs[0]),) + tuple(outs[1:])


MIX_SEGS = ((0, ATTN_W), (ATTN_W, ATTN_W + SGU_W), (ATTN_W + SGU_W, D_MODEL))


def _chunks_to_rows(a_ref):
    return jnp.concatenate([a_ref[ch] for ch in range(N_CHUNK)], axis=1)


def _mix_fwd(y_attn_c, y_sgu, y_ssm, gain):
    s = y_sgu.shape[0]

    def body(a_ref, b_ref, c_ref, g_ref, o_ref):
        for x, (lo, hi) in zip((_chunks_to_rows(a_ref), b_ref[...], c_ref[...]), MIX_SEGS):
            r = lax.rsqrt(jnp.mean(x * x, axis=-1, keepdims=True) + EPS)
            o_ref[:, lo:hi] = (x * r * g_ref[:, lo:hi]).astype(o_ref.dtype)

    row = lambda w: pl.BlockSpec((ROWS, w), lambda i: (i, 0))
    return pl.pallas_call(
        body, name="mix_fwd", grid=(s // ROWS,),
        in_specs=[pl.BlockSpec((N_CHUNK, ROWS, 128), lambda i: (0, i, 0)), row(SGU_W), row(SSM_W),
                  pl.BlockSpec((1, D_MODEL), lambda i: (0, 0))],
        out_specs=row(D_MODEL), out_shape=jax.ShapeDtypeStruct((s, D_MODEL), MXU_DTYPE),
        compiler_params=_params(("parallel",)),
    )(y_attn_c, y_sgu, y_ssm, gain.reshape(1, D_MODEL))


def _mix_bwd(y_attn_c, y_sgu, y_ssm, gain, dmix):
    s = y_sgu.shape[0]

    def body(a_ref, b_ref, c_ref, g_ref, dm_ref, da_ref, dl_ref, db_ref, dc_ref, dg_ref):
        @pl.when(pl.program_id(0) == 0)
        def _():
            dg_ref[...] = jnp.zeros_like(dg_ref)

        grads = []
        for x, (lo, hi) in zip((_chunks_to_rows(a_ref), b_ref[...], c_ref[...]), MIX_SEGS):
            r = lax.rsqrt(jnp.mean(x * x, axis=-1, keepdims=True) + EPS)
            xhat = x * r
            dm = dm_ref[:, lo:hi].astype(F32)
            dg_ref[:, lo:hi] += jnp.sum(dm * xhat, axis=0, keepdims=True)
            dxh = dm * g_ref[:, lo:hi]
            grads.append(r * (dxh - xhat * jnp.mean(dxh * xhat, axis=-1, keepdims=True)))
        db_ref[...] = grads[1]
        dc_ref[...] = grads[2]
        low = lax.broadcasted_iota(jnp.int32, (ROWS, 128), 1) < HEAD_DIM
        for ch in range(N_CHUNK):
            d_c = grads[0][:, 128 * ch:128 * (ch + 1)]
            da_ref[ch] = d_c.astype(da_ref.dtype)
            prod = d_c * a_ref[ch]
            dl_ref[ch] = jnp.where(low, jnp.sum(prod[:, :HEAD_DIM], axis=-1, keepdims=True),
                                   jnp.sum(prod[:, HEAD_DIM:], axis=-1, keepdims=True))

    row = lambda w: pl.BlockSpec((ROWS, w), lambda i: (i, 0))
    vec = pl.BlockSpec((1, D_MODEL), lambda i: (0, 0))
    chunked = pl.BlockSpec((N_CHUNK, ROWS, 128), lambda i: (0, i, 0))
    return pl.pallas_call(
        body, name="mix_bwd", grid=(s // ROWS,),
        in_specs=[chunked, row(SGU_W), row(SSM_W), vec, row(D_MODEL)],
        out_specs=[chunked, chunked, row(SGU_W), row(SSM_W), vec],
        out_shape=[jax.ShapeDtypeStruct((N_CHUNK, s, 128), ATTN_IO_DTYPE), jax.ShapeDtypeStruct((N_CHUNK, s, 128), F32),
                   jax.ShapeDtypeStruct((s, SGU_W), F32), jax.ShapeDtypeStruct((s, SSM_W), F32),
                   jax.ShapeDtypeStruct((1, D_MODEL), F32)],
        compiler_params=_params(("arbitrary",)),
    )(y_attn_c, y_sgu, y_ssm, gain.reshape(1, D_MODEL), dmix)


CONV_ROWS = 256
CONV_COLS = 1408
CONV_PAIR = 2 * CONV_COLS
HALO = 16


def _interleave_ff(t):
    lead = t.shape[:-1]
    nb = D_FF // CONV_COLS
    return jnp.swapaxes(t.reshape(lead + (2, nb, CONV_COLS)), -3, -2).reshape(lead + (2 * D_FF,))


def _deinterleave_ff(t):
    lead = t.shape[:-1]
    nb = D_FF // CONV_COLS
    return jnp.swapaxes(t.reshape(lead + (nb, 2, CONV_COLS)), -3, -2).reshape(lead + (2 * D_FF,))


def _causal_taps(x, halo, first):
    patch = 8
    row = lax.broadcasted_iota(jnp.int32, (patch, x.shape[1]), 0)
    h1 = jnp.where(first, 0.0, halo[HALO - 1:HALO, :])
    h2 = jnp.where(first, 0.0, halo[HALO - 2:HALO - 1, :])
    r1 = pltpu.roll(x, 1, 0)
    r2 = pltpu.roll(x, 2, 0)
    top1 = jnp.where(row == 0, h1, r1[0:patch])
    top2 = jnp.where(row == 0, h2, jnp.where(row == 1, h1, r2[0:patch]))
    return jnp.concatenate([top1, r1[patch:]], axis=0), jnp.concatenate([top2, r2[patch:]], axis=0)


def _conv_in_specs():
    halo_idx = lambda i: jnp.maximum(i * (CONV_ROWS // HALO) - 1, 0)
    return [pl.BlockSpec((CONV_ROWS, CONV_PAIR), lambda j, i: (i, j)),
            pl.BlockSpec((HALO, CONV_PAIR), lambda j, i: (halo_idx(i), j)),
            pl.BlockSpec((3, CONV_PAIR), lambda j, i: (0, j)),
            pl.BlockSpec((1, CONV_PAIR), lambda j, i: (0, j))]


def _ffn_act_fwd(hh, conv_w, conv_b):
    s = hh.shape[0]

    def body(m_ref, h_ref, w_ref, b_ref, o_ref):
        first = pl.program_id(1) == 0
        main = m_ref[...].astype(F32)
        x1, x2 = _causal_taps(main, h_ref[...].astype(F32), first)
        conv = w_ref[0:1, :] * x2 + w_ref[1:2, :] * x1 + w_ref[2:3, :] * main + b_ref[...]
        o_ref[...] = (_gelu(conv[:, CONV_COLS:]) * conv[:, :CONV_COLS]).astype(o_ref.dtype)

    return pl.pallas_call(
        body, name="ffn_act_fwd", grid=(D_FF // CONV_COLS, s // CONV_ROWS), in_specs=_conv_in_specs(),
        out_specs=pl.BlockSpec((CONV_ROWS, CONV_COLS), lambda j, i: (i, j)),
        out_shape=jax.ShapeDtypeStruct((s, D_FF), MXU_DTYPE),
        compiler_params=_params(("parallel", "parallel")),
    )(hh, hh, conv_w, conv_b.reshape(1, -1))


def _ffn_act_bwd(hh, conv_w, conv_b, da):
    s = hh.shape[0]
    nrow = s // CONV_ROWS
    ext_rows = CONV_ROWS + HALO

    def body(m_ref, h_ref, w_ref, b_ref, nx_ref, da_ref, dan_ref, o_ref, dw_ref, db_ref):
        first = pl.program_id(1) == 0
        last = pl.program_id(1) == nrow - 1

        @pl.when(first)
        def _():
            dw_ref[...] = jnp.zeros_like(dw_ref)
            db_ref[...] = jnp.zeros_like(db_ref)

        ext = jnp.concatenate([m_ref[...].astype(F32), nx_ref[...].astype(F32)], axis=0)
        x1, x2 = _causal_taps(ext, h_ref[...].astype(F32), first)
        conv = w_ref[0:1, :] * x2 + w_ref[1:2, :] * x1 + w_ref[2:3, :] * ext + b_ref[...]
        da = jnp.concatenate([da_ref[...].astype(F32), jnp.where(last, 0.0, dan_ref[...].astype(F32))], axis=0)
        act, dact = _gelu_pair(conv[:, CONV_COLS:])
        dconv = jnp.concatenate([da * act, da * conv[:, :CONV_COLS] * dact], axis=1)
        dmain = dconv[:CONV_ROWS]
        ahead1 = pltpu.roll(dconv, ext_rows - 1, 0)[:CONV_ROWS]
        ahead2 = pltpu.roll(dconv, ext_rows - 2, 0)[:CONV_ROWS]
        o_ref[...] = (w_ref[2:3, :] * dmain + w_ref[1:2, :] * ahead1 + w_ref[0:1, :] * ahead2).astype(o_ref.dtype)
        for t, tap in enumerate((x2, x1, ext)):
            dw_ref[t:t + 1, :] += jnp.sum(dmain * tap[:CONV_ROWS], axis=0, keepdims=True)
        db_ref[...] += jnp.sum(dmain, axis=0, keepdims=True)

    nxt = lambda i: jnp.minimum((i + 1) * (CONV_ROWS // HALO), s // HALO - 1)
    return pl.pallas_call(
        body, name="ffn_act_bwd", grid=(D_FF // CONV_COLS, nrow),
        in_specs=_conv_in_specs() + [pl.BlockSpec((HALO, CONV_PAIR), lambda j, i: (nxt(i), j)),
                                     pl.BlockSpec((CONV_ROWS, CONV_COLS), lambda j, i: (i, j)),
                                     pl.BlockSpec((HALO, CONV_COLS), lambda j, i: (nxt(i), j))],
        out_specs=[pl.BlockSpec((CONV_ROWS, CONV_PAIR), lambda j, i: (i, j)),
                   pl.BlockSpec((3, CONV_PAIR), lambda j, i: (0, j)), pl.BlockSpec((1, CONV_PAIR), lambda j, i: (0, j))],
        out_shape=[jax.ShapeDtypeStruct((s, 2 * D_FF), MXU_DTYPE), jax.ShapeDtypeStruct((3, 2 * D_FF), F32),
                   jax.ShapeDtypeStruct((1, 2 * D_FF), F32)],
        compiler_params=_params(("parallel", "arbitrary")),
    )(hh, hh, conv_w, conv_b.reshape(1, -1), hh, da, da)


def _ple_weight_specs(layer):
    return [pl.BlockSpec((None, D_MODEL, D_MODEL), lambda i: (layer, 0, 0)),
            pl.BlockSpec((None, PLE_DIM, D_MODEL), lambda i: (layer, 0, 0))]


def _ple_fwd(xn, p, w_gate, w_proj, h, layer):
    s = xn.shape[0]
    tm = 512

    def body(x_ref, p_ref, wg_ref, wp_ref, h_ref, o_ref):
        gate = jax.nn.sigmoid(_dot(x_ref[...].astype(MXU_DTYPE), wg_ref[...].astype(MXU_DTYPE), NN))
        proj = _dot(p_ref[...].astype(MXU_DTYPE), wp_ref[...].astype(MXU_DTYPE), NN)
        o_ref[...] = h_ref[...] + gate * proj

    return pl.pallas_call(
        body, name="ple_fwd", grid=(s // tm,),
        in_specs=[pl.BlockSpec((tm, D_MODEL), lambda i: (i, 0)), pl.BlockSpec((tm, PLE_DIM), lambda i: (i, 0))]
        + _ple_weight_specs(layer) + [pl.BlockSpec((tm, D_MODEL), lambda i: (i, 0))],
        out_specs=pl.BlockSpec((tm, D_MODEL), lambda i: (i, 0)),
        out_shape=jax.ShapeDtypeStruct((s, D_MODEL), F32),
        compiler_params=_params(("parallel",)),
    )(xn, p, w_gate, w_proj, h)


def _ple_bwd(xn, p, w_gate, w_proj, dh, layer):
    s = xn.shape[0]
    tm = 512

    def body(x_ref, p_ref, wg_ref, wp_ref, dh_ref, dpre_ref, dproj_ref):
        gate = jax.nn.sigmoid(_dot(x_ref[...].astype(MXU_DTYPE), wg_ref[...].astype(MXU_DTYPE), NN))
        proj = _dot(p_ref[...].astype(MXU_DTYPE), wp_ref[...].astype(MXU_DTYPE), NN)
        dh = dh_ref[...]
        dpre_ref[...] = (dh * proj * gate * (1.0 - gate)).astype(dpre_ref.dtype)
        dproj_ref[...] = (dh * gate).astype(dproj_ref.dtype)

    row = pl.BlockSpec((tm, D_MODEL), lambda i: (i, 0))
    return pl.pallas_call(
        body, name="ple_bwd", grid=(s // tm,),
        in_specs=[row, pl.BlockSpec((tm, PLE_DIM), lambda i: (i, 0))] + _ple_weight_specs(layer) + [row],
        out_specs=[row, row],
        out_shape=[jax.ShapeDtypeStruct((s, D_MODEL), MXU_DTYPE)] * 2,
        compiler_params=_params(("parallel",)),
    )(xn, p, w_gate, w_proj, dh)


O_SGU = 3 * ATTN_W
O_SSM = O_SGU + 2 * SGU_W


def _layer_consts(w, i):
    causal = jnp.asarray(np.tril(np.ones((SGU_CHUNK, SGU_CHUNK), np.float32)))
    return {
        "sgu_w_mask": w["sgu_w"][i] * causal,
        "sgu_b_t": w["sgu_b"][i].T,
        "ssm_ops": _ssm_operands(w["ssm_a_re"][i], w["ssm_a_im"][i], w["ssm_log_dt"][i], w["ssm_b_re"][i],
                                 w["ssm_b_im"][i], w["ssm_c_re"][i], w["ssm_c_im"][i]),
    }


def _layer_fwd(h0, p_i, w, i, bias):
    c = _layer_consts(w, i)
    xn1 = _rms_fwd(h0, w["norm_attn_g"][i], name="rms_attn_fwd")
    qkv, zs, us = _in_proj(xn1, w["w_in"], i)
    y_attn, lse = _attn2_fwd(qkv, bias)
    y_sgu = _sgu_fwd(zs, w["sgu_ln_g"][i], w["sgu_ln_b"][i], c["sgu_w_mask"], c["sgu_b_t"])
    y_ssm, entry = _ssm_fwd(us, c["ssm_ops"], w["ssm_d"][i], w["ssm_glu_w"][i], w["ssm_glu_b"][i])
    mix = _mix_fwd(y_attn, y_sgu, y_ssm, w["branch_norm_g"][i])
    h1 = _matmul(mix, w["w_out"], name="out_proj", out_dtype=F32, tm=512, tn=1024, residual=h0, layer=i)
    xn2 = _rms_fwd(h1, w["norm_ffn_g"][i], name="rms_ffn_fwd")
    hh = _matmul(xn2, w["ffn_w_up"], name="ffn_up", out_dtype=MXU_DTYPE, tm=1024, tn=1408, layer=i)
    act = _ffn_act_fwd(hh, w["ffn_conv_w"][i], w["ffn_conv_b"][i])
    h2 = _matmul(act, w["ffn_w_down"], name="ffn_down", out_dtype=F32, tm=512, tn=1024, residual=h1, layer=i)
    xn3 = _rms_fwd(h2, w["norm_ple_g"][i], name="rms_ple_fwd")
    h3 = _ple_fwd(xn3, p_i, w["ple_w_gate"], w["ple_w_proj"], h2, i)
    saved = dict(h0=h0, xn1=xn1, qkv=qkv, zs=zs, us=us, y_attn=y_attn, lse=lse, y_sgu=y_sgu, y_ssm=y_ssm,
                 entry=entry, mix=mix, h1=h1, xn2=xn2, hh=hh, act=act, h2=h2, xn3=xn3, consts=c)
    return h3, saved


def _layer_bwd(dh3, sv, p_i, w, i, bias):
    c = sv["consts"]
    g = {}
    dpre, dproj = _ple_bwd(sv["xn3"], p_i, w["ple_w_gate"], w["ple_w_proj"], dh3, i)
    g["ple_w_gate"] = _matmul_tn(sv["xn3"], dpre, name="d_ple_w_gate", tk=1024, tn=1024)
    g["ple_w_proj"] = _matmul_tn(p_i, dproj, name="d_ple_w_proj", tk=256, tn=1024)
    dxn3 = _matmul(dpre, w["ple_w_gate"], name="d_xn_ple", out_dtype=F32, tm=512, tn=1024, trans_b=True, layer=i)
    dh2, g["norm_ple_g"] = _rms_bwd(sv["h2"], w["norm_ple_g"][i], dxn3, dh3, name="rms_ple_bwd")
    g["ffn_w_down"] = _matmul_tn(sv["act"], dh2, name="d_ffn_w_down", tk=1408, tn=1024)
    dact = _matmul(dh2, w["ffn_w_down"], name="d_ffn_act", out_dtype=MXU_DTYPE, tm=512, tn=1408, trans_b=True, layer=i)
    dhh, g["ffn_conv_w"], g["ffn_conv_b"] = _ffn_act_bwd(sv["hh"], w["ffn_conv_w"][i], w["ffn_conv_b"][i], dact)
    g["ffn_w_up"] = _matmul_tn(sv["xn2"], dhh, name="d_ffn_w_up", tk=1024, tn=1408)
    dxn2 = _matmul(dhh, w["ffn_w_up"], name="d_xn_ffn", out_dtype=F32, tm=512, tn=512, trans_b=True, layer=i)
    dh1, g["norm_ffn_g"] = _rms_bwd(sv["h1"], w["norm_ffn_g"][i], dxn2, dh2, name="rms_ffn_bwd")
    g["w_out"] = _matmul_tn(sv["mix"], dh1, name="d_w_out", tk=1024, tn=1024)
    dmix = _matmul(dh1, w["w_out"], name="d_mix", out_dtype=F32, tm=512, tn=1024, trans_b=True, layer=i)
    dy_attn, delta, dy_sgu, dy_ssm, g["branch_norm_g"] = _mix_bwd(sv["y_attn"], sv["y_sgu"], sv["y_ssm"],
                                                                  w["branch_norm_g"][i], dmix)
    dq, dk, dv, ek, ev, dbias = _attn2_bwd(sv["qkv"], bias, sv["lse"], delta, dy_attn)
    dzs, g["sgu_ln_g"], g["sgu_ln_b"], dsw, dsb = _sgu_bwd(sv["zs"], w["sgu_ln_g"][i], w["sgu_ln_b"][i],
                                                          c["sgu_w_mask"], c["sgu_b_t"], dy_sgu)
    causal = jnp.asarray(np.tril(np.ones((SGU_CHUNK, SGU_CHUNK), np.float32)))
    g["sgu_w"] = dsw * causal
    g["sgu_b"] = dsb.T
    dus, dbb, dcm, da, g["ssm_d"], g["ssm_glu_w"], g["ssm_glu_b"] = _ssm_bwd(
        sv["us"], sv["entry"], c["ssm_ops"], w["ssm_d"][i], w["ssm_glu_w"][i], w["ssm_glu_b"][i], dy_ssm)
    dbb5 = dbb.reshape(SSM_G, SSM_C, 2, SSM_G, SSM_N)
    dbbar = jnp.einsum("gcpgn->pgnc", dbb5)
    dcm5 = dcm.reshape(2, SSM_G, SSM_N, SSM_G, SSM_C)
    dcc = jnp.einsum("pgngc->pgcn", dcm5)
    g["ssm_c_re"] = dcc[0]
    g["ssm_c_im"] = -dcc[1]
    da2 = da.reshape(2, SSM_G, SSM_N)
    _, vjp = jax.vjp(_ssm_discretize, w["ssm_a_re"][i], w["ssm_a_im"][i], w["ssm_log_dt"][i],
                     w["ssm_b_re"][i], w["ssm_b_im"][i])
    (g["ssm_a_re"], g["ssm_a_im"], g["ssm_log_dt"], g["ssm_b_re"], g["ssm_b_im"]) = vjp(
        (da2[0], da2[1], dbbar[0], dbbar[1]))
    dz = _attn2_bwd_sum(dq, dk, dv, ek, ev, dzs, dus)
    g["w_in"] = _matmul_tn(sv["xn1"], dz, name="d_w_in", tk=1024, tn=1152)
    dxn1 = _matmul(dz, w["w_in"], name="d_xn_attn", out_dtype=F32, tm=512, tn=1024, trans_b=True, layer=i)
    dh0, g["norm_attn_g"] = _rms_bwd(sv["h0"], w["norm_attn_g"][i], dxn1, dh1, name="rms_attn_bwd")
    for k in ("norm_ple_g", "norm_ffn_g", "branch_norm_g", "norm_attn_g", "sgu_ln_g", "sgu_ln_b", "ssm_d",
              "ssm_glu_b", "ffn_conv_b"):
        g[k] = g[k].reshape(-1)
    return dh0, g, dbias


def _local_step(x, p, target, w, ff_interleaved=False):
    ff_names = ("ffn_conv_b",) if ff_interleaved else FF_SHARDED + ("ffn_conv_b",)
    w = dict(w)
    for k in ff_names:
        w[k] = _interleave_ff(w[k])
    bias = _bias_build(w["rel_bias"])
    h = x
    saved = []
    for i in range(DEPTH):
        h, sv = _layer_fwd(h, p[i], w, i, bias)
        saved.append(sv)
    loss, dh, dgf = _loss_head(h, w["final_norm_g"], target)
    layer_grads = [None] * DEPTH
    dbias = None
    for i in reversed(range(DEPTH)):
        dh, layer_grads[i], db = _layer_bwd(dh, saved[i], p[i], w, i, bias)
        dbias = db if dbias is None else dbias + db
    grads = {k: jnp.stack([layer_grads[i][k] for i in range(DEPTH)]) for k in layer_grads[0]}
    for k in ff_names:
        grads[k] = _deinterleave_ff(grads[k])
    grads["rel_bias"] = _bias_reduce(dbias)
    grads["final_norm_g"] = dgf.reshape(-1)
    return loss, dh, grads


def _pad_rows(a2, mult=16):
    r = (-a2.shape[0]) % mult
    return a2 if r == 0 else jnp.concatenate([a2, jnp.zeros((r, a2.shape[1]), a2.dtype)], axis=0)


def _as_rows(a, rows=None):
    flat = a.reshape(-1)
    if rows is None:
        rows = -(-flat.shape[0] // (16 * PACK_COLS)) * 16
    return jnp.pad(flat, (0, rows * PACK_COLS - flat.shape[0])).reshape(rows, PACK_COLS)


def _shard_shape(name):
    full, ax = BIG_FULL[name]
    shp = [DEPTH] + list(full)
    shp[ax] //= N_CHIPS
    return tuple(shp)


EXACT_NAMES = ("ffn_conv_w",)


def _pack_rows_of(name):
    n = int(np.prod(_shard_shape(name))) * (2 if name in EXACT_NAMES else 1)
    rows = -(-n // PACK_COLS)
    return -(-rows // 16) * 16


def _pack_shards(shards, dtype, exact=False):
    split_words = exact and jnp.dtype(dtype).itemsize == 2
    parts = []
    for n in BIG_NAMES:
        a = shards[n]
        if split_words and n in EXACT_NAMES:
            a = lax.bitcast_convert_type(a.astype(F32), dtype)
        parts.append(_as_rows(a.astype(dtype), _pack_rows_of(n)))
    used = sum(pt.shape[0] for pt in parts)
    parts.append(jnp.zeros((PACK_ROWS - used, PACK_COLS), dtype))
    return jnp.concatenate(parts, axis=0)


def _unpack_shard(flat, name, exact=False):
    off = 0
    for n in BIG_NAMES:
        if n == name:
            break
        off += _pack_rows_of(n)
    shp = _shard_shape(name)
    cnt = int(np.prod(shp))
    vec = flat[off:off + _pack_rows_of(name)].reshape(-1)
    if exact and name in EXACT_NAMES and jnp.dtype(flat.dtype).itemsize == 2:
        return lax.bitcast_convert_type(vec[:2 * cnt].reshape(shp + (2,)), F32)
    return vec[:cnt].reshape(shp)


FF_SHARDED = ("ffn_w_up", "ffn_conv_w")
FF_CHIP_ORDER = (0, 2, 1, 3)


def _chip_order(name):
    return FF_CHIP_ORDER if name in FF_SHARDED else tuple(range(N_CHIPS))


def _split_full(full, name):
    _, ax = BIG_FULL[name]
    parts = jnp.split(full, N_CHIPS, axis=ax)
    out = [None] * N_CHIPS
    for j, k in enumerate(_chip_order(name)):
        out[k] = parts[j]
    return out


def _join_shards(shards, name):
    _, ax = BIG_FULL[name]
    return jnp.concatenate([shards[k] for k in _chip_order(name)], axis=ax)


def _small_shapes(w):
    return [(n, w[n].shape) for n in SMALL_NAMES]


def _pack_small(d):
    flat = jnp.concatenate([d[n].astype(F32).reshape(-1) for n in SMALL_NAMES])
    flat = jnp.concatenate([flat, jnp.zeros((SMALL_ROWS * PACK_COLS - flat.shape[0],), F32)])
    return flat.reshape(SMALL_ROWS, PACK_COLS)


def _unpack_small(flat, shapes):
    out, off = {}, 0
    v = flat.reshape(-1)
    for n, shp in shapes:
        cnt = int(np.prod(shp))
        out[n] = v[off:off + cnt].reshape(shp)
        off += cnt
    return out


MESH = pl.DeviceIdType.MESH
ANY = pl.BlockSpec(memory_space=pl.ANY)


def _me():
    return lax.axis_index("x"), lax.axis_index("y"), lax.axis_index("c")


def _other_chips(x, y):
    return [(1 - x, y), (x, 1 - y), (1 - x, 1 - y)]


def _gather_weights(wflat):
    def body(w_ref, out_ref, send_sems, recv_sems):
        x, y, c = _me()
        sibling = (x, y, 1 - c)
        chips = _other_chips(x, y)

        def rows(chip, half):
            return out_ref.at[2 * chip[0] + chip[1], pl.ds(half * PACK_HALF, PACK_HALF), :]

        def copy(k, chip, half, to, src=None):
            return pltpu.make_async_remote_copy(
                src_ref=rows(chip, half) if src is None else src, dst_ref=rows(chip, half),
                send_sem=send_sems.at[k], recv_sem=recv_sems.at[k], device_id=to, device_id_type=MESH)

        my_half = w_ref.at[pl.ds(c * PACK_HALF, PACK_HALF), :]
        first = [copy(j, (x, y), c, (*chip, c), src=my_half) for j, chip in enumerate(chips)]
        for cp in first:
            cp.start()
        passed = [copy(3 + j, chip, c, sibling) for j, chip in enumerate(chips)]
        for j, chip in enumerate(chips):
            copy(j, chip, c, (x, y, c)).wait_recv()
            passed[j].start()
        for j, chip in enumerate(chips):
            copy(3 + j, chip, 1 - c, (x, y, c)).wait_recv()
        for cp in first + passed:
            cp.wait_send()

    return pl.pallas_call(
        body, name="gather_weights", in_specs=[ANY], out_specs=ANY,
        out_shape=jax.ShapeDtypeStruct((N_CHIPS, PACK_ROWS, PACK_COLS), wflat.dtype),
        scratch_shapes=[pltpu.SemaphoreType.DMA((6,)), pltpu.SemaphoreType.DMA((6,))],
    )(wflat)


def _fill_own_shard(wall, wflat, chip_idx):
    rows = PACK_ROWS // 8

    def body(idx_ref, w_ref, wall_ref, o_ref):
        del idx_ref, wall_ref
        o_ref[...] = w_ref[...]

    return pl.pallas_call(
        body, name="fill_own_shard",
        grid_spec=pltpu.PrefetchScalarGridSpec(
            num_scalar_prefetch=1, grid=(PACK_ROWS // rows,),
            in_specs=[pl.BlockSpec((rows, PACK_COLS), lambda i, idx: (i, 0)), ANY],
            out_specs=pl.BlockSpec((None, rows, PACK_COLS), lambda i, idx: (idx[0], i, 0))),
        out_shape=jax.ShapeDtypeStruct(wall.shape, wall.dtype),
        input_output_aliases={2: 0},
        compiler_params=_params(("parallel",)),
    )(chip_idx, wflat, wall)


def _exchange_partials(gb, gs):
    def body(gb_ref, gs_ref, half_ref, small_ref, send_sems, recv_sems, local_sem):
        x, y, c = _me()
        me_idx = 4 * x + 2 * y + c
        mine = pltpu.make_async_copy(gs_ref, small_ref.at[me_idx], local_sem)
        mine.start()
        d2d = pltpu.make_async_remote_copy(
            src_ref=gb_ref.at[:, pl.ds((1 - c) * PACK_HALF, PACK_HALF), :], dst_ref=half_ref,
            send_sem=send_sems.at[0], recv_sem=recv_sems.at[0], device_id=(x, y, 1 - c), device_id_type=MESH)
        d2d.start()
        copies = []
        for k in range(1, N_DEV):
            fx, fy, fc = (k >> 2) & 1, (k >> 1) & 1, k & 1
            peer = (x ^ fx, y ^ fy, c ^ fc)
            copies.append(pltpu.make_async_remote_copy(
                src_ref=gs_ref, dst_ref=small_ref.at[me_idx], send_sem=send_sems.at[k], recv_sem=recv_sems.at[k],
                device_id=peer, device_id_type=MESH))
        for cp in copies:
            cp.start()
        for k in range(1, N_DEV):
            fx, fy, fc = (k >> 2) & 1, (k >> 1) & 1, k & 1
            peer_idx = 4 * (x ^ fx) + 2 * (y ^ fy) + (c ^ fc)
            pltpu.make_async_remote_copy(
                src_ref=gs_ref, dst_ref=small_ref.at[peer_idx], send_sem=send_sems.at[k], recv_sem=recv_sems.at[k],
                device_id=(x, y, c), device_id_type=MESH).wait_recv()
        d2d.wait_recv()
        d2d.wait_send()
        for cp in copies:
            cp.wait_send()
        mine.wait()

    return pl.pallas_call(
        body, name="exchange_partials", in_specs=[ANY, ANY], out_specs=[ANY, ANY],
        out_shape=[jax.ShapeDtypeStruct((N_CHIPS, PACK_HALF, PACK_COLS), gb.dtype),
                   jax.ShapeDtypeStruct((N_DEV, SMALL_ROWS, PACK_COLS), F32)],
        scratch_shapes=[pltpu.SemaphoreType.DMA((N_DEV,)), pltpu.SemaphoreType.DMA((N_DEV,)), pltpu.SemaphoreType.DMA],
    )(gb, gs)


RED_ROWS = 256


def _chip_partials(gb, sib, c_idx):
    nrow = PACK_HALF // RED_ROWS

    def body(c_ref, a_ref, b_ref, o_ref):
        del c_ref
        o_ref[...] = (a_ref[...].astype(F32) + b_ref[...].astype(F32)).astype(o_ref.dtype)

    blk = (1, RED_ROWS, PACK_COLS)
    return pl.pallas_call(
        body, name="chip_partials",
        grid_spec=pltpu.PrefetchScalarGridSpec(
            num_scalar_prefetch=1, grid=(N_CHIPS, nrow),
            in_specs=[pl.BlockSpec(blk, lambda k, i, c: (k, c[0] * nrow + i, 0)),
                      pl.BlockSpec(blk, lambda k, i, c: (k, i, 0))],
            out_specs=pl.BlockSpec(blk, lambda k, i, c: (k, i, 0))),
        out_shape=jax.ShapeDtypeStruct((N_CHIPS, PACK_HALF, PACK_COLS), gb.dtype),
        compiler_params=_params(("parallel", "parallel")),
    )(c_idx, gb, sib)


def _scatter_partials(pc):
    def body(pc_ref, out_ref, send_sems, recv_sems):
        x, y, c = _me()
        chips = _other_chips(x, y)
        copies = [pltpu.make_async_remote_copy(
            src_ref=pc_ref.at[2 * chip[0] + chip[1]], dst_ref=out_ref.at[k],
            send_sem=send_sems.at[k], recv_sem=recv_sems.at[k], device_id=(*chip, c), device_id_type=MESH)
            for k, chip in enumerate(chips)]
        for cp in copies:
            cp.start()
        for cp in copies:
            cp.wait_recv()
        for cp in copies:
            cp.wait_send()

    return pl.pallas_call(
        body, name="scatter_partials", in_specs=[ANY], out_specs=ANY,
        out_shape=jax.ShapeDtypeStruct((3, PACK_HALF, PACK_COLS), pc.dtype),
        scratch_shapes=[pltpu.SemaphoreType.DMA((3,)), pltpu.SemaphoreType.DMA((3,))],
    )(pc)


def _final_half(gb, sib, recv, idx):
    nrow = PACK_HALF // RED_ROWS

    def body(idx_ref, a_ref, b_ref, r_ref, o_ref):
        del idx_ref
        acc = a_ref[0].astype(F32) + b_ref[0].astype(F32)
        for k in range(3):
            acc = acc + r_ref[k].astype(F32)
        o_ref[...] = acc

    return pl.pallas_call(
        body, name="final_half",
        grid_spec=pltpu.PrefetchScalarGridSpec(
            num_scalar_prefetch=1, grid=(nrow,),
            in_specs=[pl.BlockSpec((1, RED_ROWS, PACK_COLS), lambda i, idx: (idx[0], idx[1] * nrow + i, 0)),
                      pl.BlockSpec((1, RED_ROWS, PACK_COLS), lambda i, idx: (idx[0], i, 0)),
                      pl.BlockSpec((3, RED_ROWS, PACK_COLS), lambda i, idx: (0, i, 0))],
            out_specs=pl.BlockSpec((RED_ROWS, PACK_COLS), lambda i, idx: (i, 0))),
        out_shape=jax.ShapeDtypeStruct((PACK_HALF, PACK_COLS), F32),
        compiler_params=_params(("parallel",)),
    )(idx, gb, sib, recv)


def _share_halves(half):
    def body(h_ref, out_ref, send_sem, recv_sem):
        x, y, c = _me()
        cp = pltpu.make_async_remote_copy(src_ref=h_ref, dst_ref=out_ref, send_sem=send_sem, recv_sem=recv_sem,
                                          device_id=(x, y, 1 - c), device_id_type=MESH)
        cp.start()
        cp.wait_recv()
        cp.wait_send()

    return pl.pallas_call(
        body, name="share_halves", in_specs=[ANY], out_specs=ANY,
        out_shape=jax.ShapeDtypeStruct((PACK_HALF, PACK_COLS), F32),
        scratch_shapes=[pltpu.SemaphoreType.DMA, pltpu.SemaphoreType.DMA],
    )(half)


def _sum_small(allsmall):
    def body(a_ref, o_ref):
        acc = a_ref[0]
        for k in range(1, N_DEV):
            acc = acc + a_ref[k]
        o_ref[...] = acc

    tr = 96
    return pl.pallas_call(
        body, name="sum_small", grid=(SMALL_ROWS // tr,),
        in_specs=[pl.BlockSpec((N_DEV, tr, PACK_COLS), lambda i: (0, i, 0))],
        out_specs=pl.BlockSpec((tr, PACK_COLS), lambda i: (i, 0)),
        out_shape=jax.ShapeDtypeStruct((SMALL_ROWS, PACK_COLS), F32),
        compiler_params=_params(("parallel",)),
    )(allsmall)


def _adamw(w, g, m, v, *, name):
    shape = w.shape
    cols = shape[-1]
    as2 = lambda t: t.reshape(-1, cols)
    w2, g2, m2, v2 = as2(w), as2(g), as2(m), as2(v)
    rows = w2.shape[0]
    tr = rows
    if rows * cols * 4 > (1 << 20):
        tr = _tile(rows, max(8, (1 << 20) // (cols * 4) // 8 * 8), 8)

    def body(w_ref, g_ref, m_ref, v_ref, d_ref, mo_ref, vo_ref):
        gg = g_ref[...]
        mn = ADAM_B1 * m_ref[...] + (1.0 - ADAM_B1) * gg
        vn = ADAM_B2 * v_ref[...] + (1.0 - ADAM_B2) * (gg * gg)
        m_hat = mn / (1.0 - ADAM_B1 ** ADAM_STEP)
        v_hat = vn / (1.0 - ADAM_B2 ** ADAM_STEP)
        d_ref[...] = -ADAM_LR * (m_hat / (jnp.sqrt(v_hat) + ADAM_EPS) + ADAM_WD * w_ref[...])
        mo_ref[...] = mn
        vo_ref[...] = vn

    blk = pl.BlockSpec((tr, cols), lambda i: (i, 0))
    outs = pl.pallas_call(
        body, name=name, grid=(rows // tr,), in_specs=[blk] * 4, out_specs=[blk] * 3,
        out_shape=[jax.ShapeDtypeStruct((rows, cols), F32)] * 3,
        compiler_params=_params(("parallel",)),
    )(w2, g2, m2, v2)
    return tuple(t.reshape(shape) for t in outs)


def kernel(x, p, rel_bias, norm_attn_g, w_in, sgu_ln_g, sgu_ln_b, sgu_w, sgu_b, ssm_a_re, ssm_a_im, ssm_log_dt, ssm_b_re, ssm_b_im, ssm_c_re, ssm_c_im, ssm_d, ssm_glu_w, ssm_glu_b, branch_norm_g, w_out, norm_ffn_g, ffn_w_up, ffn_conv_w, ffn_conv_b, ffn_w_down, norm_ple_g, ple_w_gate, ple_w_proj, final_norm_g, loss_target, m_rel_bias, m_norm_attn_g, m_w_in, m_sgu_ln_g, m_sgu_ln_b, m_sgu_w, m_sgu_b, m_ssm_a_re, m_ssm_a_im, m_ssm_log_dt, m_ssm_b_re, m_ssm_b_im, m_ssm_c_re, m_ssm_c_im, m_ssm_d, m_ssm_glu_w, m_ssm_glu_b, m_branch_norm_g, m_w_out, m_norm_ffn_g, m_ffn_w_up, m_ffn_conv_w, m_ffn_conv_b, m_ffn_w_down, m_norm_ple_g, m_ple_w_gate, m_ple_w_proj, m_final_norm_g, v_rel_bias, v_norm_attn_g, v_w_in, v_sgu_ln_g, v_sgu_ln_b, v_sgu_w, v_sgu_b, v_ssm_a_re, v_ssm_a_im, v_ssm_log_dt, v_ssm_b_re, v_ssm_b_im, v_ssm_c_re, v_ssm_c_im, v_ssm_d, v_ssm_glu_w, v_ssm_glu_b, v_branch_norm_g, v_w_out, v_norm_ffn_g, v_ffn_w_up, v_ffn_conv_w, v_ffn_conv_b, v_ffn_w_down, v_norm_ple_g, v_ple_w_gate, v_ple_w_proj, v_final_norm_g):
    args = dict(locals())
    wts = {n: args[n] for n in WEIGHT_NAMES}
    mom_m = {n: args["m_" + n] for n in WEIGHT_NAMES}
    mom_v = {n: args["v_" + n] for n in WEIGHT_NAMES}

    xi, yi, ci = _me()
    wflat = _pack_shards({n: wts[n] for n in BIG_NAMES}, MXU_DTYPE, exact=True)
    wall = _fill_own_shard(_gather_weights(wflat), wflat, jnp.stack([2 * xi + yi]).astype(jnp.int32))
    full = dict(wts)
    for n in BIG_NAMES:
        full[n] = _join_shards([_unpack_shard(wall[k], n, exact=True) for k in range(N_CHIPS)], n)
    full["ffn_conv_w"] = full["ffn_conv_w"].astype(F32)

    loss, dx, grads = _local_step(x[0], p[:, 0], loss_target[0], full, ff_interleaved=True)
    loss = lax.psum(loss[0, 0], MESH_AXES)

    xi, yi, ci = _me()
    stacked = {n: _split_full(grads[n], n) for n in BIG_NAMES}
    gb = jnp.stack([_pack_shards({n: stacked[n][k] for n in BIG_NAMES}, MXU_DTYPE) for k in range(N_CHIPS)])
    gs = _pack_small(grads)
    sib, allsmall = _exchange_partials(gb, gs)
    pc = _chip_partials(gb, sib, jnp.stack([ci]).astype(jnp.int32))
    recv = _scatter_partials(pc)
    half = _final_half(gb, sib, recv, jnp.stack([2 * xi + yi, ci]).astype(jnp.int32))
    other = _share_halves(half)
    gflat = jnp.concatenate([jnp.where(ci == 0, half, other), jnp.where(ci == 0, other, half)], axis=0)
    gsmall = _unpack_small(_sum_small(allsmall), _small_shapes(wts))

    g_out, d_out, m_out, v_out = {}, {}, {}, {}
    for n in BIG_NAMES:
        g_out[n] = _unpack_shard(gflat, n)
        d_out[n], m_out[n], v_out[n] = _adamw(wts[n], g_out[n], mom_m[n], mom_v[n], name="adamw_" + n)
    sw = _pack_small(wts)
    d_s, m_s, v_s = _adamw(sw, _pack_small(gsmall), _pack_small(mom_m), _pack_small(mom_v), name="adamw_small")
    shapes = _small_shapes(wts)
    d_sm, m_sm, v_sm = _unpack_small(d_s, shapes), _unpack_small(m_s, shapes), _unpack_small(v_s, shapes)
    for n in SMALL_NAMES:
        g_out[n], d_out[n], m_out[n], v_out[n] = gsmall[n], d_sm[n], m_sm[n], v_sm[n]

    return (loss, dx[None], *[g_out[n] for n in WEIGHT_NAMES], *[d_out[n] for n in WEIGHT_NAMES],
            *[m_out[n] for n in WEIGHT_NAMES], *[v_out[n] for n in WEIGHT_NAMES])
```

```python
import functools
import math

import numpy as np
import jax
import jax.numpy as jnp
from jax import lax
from jax.experimental import pallas as pl
from jax.experimental.pallas import tpu as pltpu

F32 = jnp.float32
MXU_DTYPE = jnp.bfloat16
VMEM_LIMIT_BYTES = 52 * 1024 * 1024

D_MODEL = 1024
DEPTH = 2
PLE_DIM = 256
HEAD_DIM = 64
N_HEADS = 8
ATTN_W = 512
QBLK = 128
BRANCH_DIL = (1, 4, 16)
N_BUCKETS = 32
REL_MAX_DIST = 2048
SGU_W = 256
SGU_G = 4
SGU_GW = 64
SGU_CHUNK = 128
SSM_W = 256
SSM_G = 16
SSM_C = 16
SSM_N = 64
NSTATE = SSM_G * SSM_N
D_FF = 2816
EPS = 1e-6
NEG_INF = -1e30
ATTN_SCALE = HEAD_DIM ** -0.5

ADAM_LR = 0.001
ADAM_B1 = 0.9
ADAM_B2 = 0.999
ADAM_EPS = 1e-08
ADAM_WD = 0.01
ADAM_STEP = 10

SSM_NSEG = 8
SSM_TSEG = 64
SSM_TB = SSM_NSEG * SSM_TSEG
SSM_LANE_CHUNK = 512

MESH_AXES = ("x", "y", "c")
N_CHIPS = 4
N_DEV = 8

BIG_NAMES = ("w_in", "ssm_glu_w", "w_out", "ffn_w_up", "ffn_conv_w", "ffn_w_down", "ple_w_gate", "ple_w_proj")
BIG_FULL = {
    "w_in": ((D_MODEL, 2304), 2),
    "ssm_glu_w": ((SSM_W, SSM_W), 1),
    "w_out": ((D_MODEL, D_MODEL), 1),
    "ffn_w_up": ((D_MODEL, 2 * D_FF), 2),
    "ffn_conv_w": ((3, 2 * D_FF), 2),
    "ffn_w_down": ((D_FF, D_MODEL), 1),
    "ple_w_gate": ((D_MODEL, D_MODEL), 1),
    "ple_w_proj": ((PLE_DIM, D_MODEL), 2),
}
PACK_COLS = 1024
PACK_ROWS = 6656
PACK_HALF = PACK_ROWS // 2

SMALL_NAMES = ("rel_bias", "norm_attn_g", "sgu_ln_g", "sgu_ln_b", "sgu_w", "sgu_b", "ssm_a_re", "ssm_a_im",
               "ssm_log_dt", "ssm_b_re", "ssm_b_im", "ssm_c_re", "ssm_c_im", "ssm_d", "ssm_glu_b",
               "branch_norm_g", "norm_ffn_g", "ffn_conv_b", "norm_ple_g", "final_norm_g")
SMALL_ROWS = 384

WEIGHT_NAMES = ("rel_bias", "norm_attn_g", "w_in", "sgu_ln_g", "sgu_ln_b", "sgu_w", "sgu_b", "ssm_a_re", "ssm_a_im",
                "ssm_log_dt", "ssm_b_re", "ssm_b_im", "ssm_c_re", "ssm_c_im", "ssm_d", "ssm_glu_w", "ssm_glu_b",
                "branch_norm_g", "w_out", "norm_ffn_g", "ffn_w_up", "ffn_conv_w", "ffn_conv_b", "ffn_w_down",
                "norm_ple_g", "ple_w_gate", "ple_w_proj", "final_norm_g")


def _params(sem):
    return pltpu.CompilerParams(dimension_semantics=sem, vmem_limit_bytes=VMEM_LIMIT_BYTES)


def _tile(n, cap, mult=128):
    if n <= cap:
        return n
    best = None
    for t in range(mult, cap + 1, mult):
        if n % t == 0:
            best = t
    assert best is not None, (n, cap)
    return best


def _gelu(x):
    return 0.5 * x * (1.0 + jnp.tanh(0.7978845608028654 * (x + 0.044715 * x * x * x)))


def _gelu_pair(x):
    x2 = x * x
    t = jnp.tanh(0.7978845608028654 * x * (1.0 + 0.044715 * x2))
    half = 0.5 * (1.0 + t)
    return x * half, half + 0.5 * x * (1.0 - t * t) * (0.7978845608028654 + 3.0 * 0.044715 * 0.7978845608028654 * x2)


def _dot(a, b, dims):
    return lax.dot_general(a, b, (dims, ((), ())), preferred_element_type=F32)


def _dotf(a, b, dims):
    return _dot(a.astype(MXU_DTYPE), b.astype(MXU_DTYPE), dims)


NN = ((1,), (0,))
NT = ((1,), (1,))
TN = ((0,), (0,))


def _matmul(a, b, *, name, out_dtype, tm, tn, trans_b=False, residual=None, layer=None):
    m, k = a.shape
    n = b.shape[-2] if trans_b else b.shape[-1]
    tm = _tile(m, tm, 8)
    tn = _tile(n, tn)
    dims = NT if trans_b else NN
    lead = () if layer is None else (None,)
    lidx = () if layer is None else (layer,)

    def body(*refs):
        if residual is None:
            a_ref, b_ref, o_ref = refs
        else:
            a_ref, b_ref, r_ref, o_ref = refs
        acc = _dot(a_ref[...].astype(MXU_DTYPE), b_ref[...].astype(MXU_DTYPE), dims)
        if residual is not None:
            acc = acc + r_ref[...]
        o_ref[...] = acc.astype(o_ref.dtype)

    b_spec = (pl.BlockSpec(lead + (tn, k), lambda i, j: lidx + (j, 0)) if trans_b
              else pl.BlockSpec(lead + (k, tn), lambda i, j: lidx + (0, j)))
    in_specs = [pl.BlockSpec((tm, k), lambda i, j: (i, 0)), b_spec]
    args = [a, b]
    if residual is not None:
        in_specs.append(pl.BlockSpec((tm, tn), lambda i, j: (i, j)))
        args.append(residual)
    return pl.pallas_call(
        body, name=name, grid=(m // tm, n // tn), in_specs=in_specs,
        out_specs=pl.BlockSpec((tm, tn), lambda i, j: (i, j)),
        out_shape=jax.ShapeDtypeStruct((m, n), out_dtype),
        compiler_params=_params(("parallel", "parallel")),
    )(*args)


def _matmul_tn(a, g, *, name, tk, tn, tm=512):
    m, k = a.shape
    n = g.shape[1]
    tk = _tile(k, tk)
    tn = _tile(n, tn)
    tm = _tile(m, tm, 8)

    def body(a_ref, g_ref, o_ref):
        @pl.when(pl.program_id(2) == 0)
        def _():
            o_ref[...] = jnp.zeros_like(o_ref)

        o_ref[...] += _dot(a_ref[...].astype(MXU_DTYPE), g_ref[...].astype(MXU_DTYPE), TN)

    return pl.pallas_call(
        body, name=name, grid=(k // tk, n // tn, m // tm),
        in_specs=[pl.BlockSpec((tm, tk), lambda i, j, s: (s, i)),
                  pl.BlockSpec((tm, tn), lambda i, j, s: (s, j))],
        out_specs=pl.BlockSpec((tk, tn), lambda i, j, s: (i, j)),
        out_shape=jax.ShapeDtypeStruct((k, n), F32),
        compiler_params=_params(("parallel", "parallel", "arbitrary")),
    )(a, g)


ROWS = 512


def _rms_fwd(h, g, *, name):
    s, d = h.shape

    def body(h_ref, g_ref, o_ref):
        x = h_ref[...]
        r = lax.rsqrt(jnp.mean(x * x, axis=-1, keepdims=True) + EPS)
        o_ref[...] = (x * r * g_ref[...]).astype(o_ref.dtype)

    return pl.pallas_call(
        body, name=name, grid=(s // ROWS,),
        in_specs=[pl.BlockSpec((ROWS, d), lambda i: (i, 0)), pl.BlockSpec((1, d), lambda i: (0, 0))],
        out_specs=pl.BlockSpec((ROWS, d), lambda i: (i, 0)),
        out_shape=jax.ShapeDtypeStruct((s, d), MXU_DTYPE),
        compiler_params=_params(("parallel",)),
    )(h, g.reshape(1, d))


def _rms_bwd(h, g, dxn, dres, *, name):
    s, d = h.shape

    def body(h_ref, g_ref, dxn_ref, dres_ref, dh_ref, dg_ref):
        @pl.when(pl.program_id(0) == 0)
        def _():
            dg_ref[...] = jnp.zeros_like(dg_ref)

        x = h_ref[...]
        r = lax.rsqrt(jnp.mean(x * x, axis=-1, keepdims=True) + EPS)
        xhat = x * r
        dxn = dxn_ref[...].astype(F32)
        dg_ref[...] += jnp.sum(dxn * xhat, axis=0, keepdims=True)
        dxh = dxn * g_ref[...]
        dh_ref[...] = dres_ref[...] + r * (dxh - xhat * jnp.mean(dxh * xhat, axis=-1, keepdims=True))

    row = pl.BlockSpec((ROWS, d), lambda i: (i, 0))
    vec = pl.BlockSpec((1, d), lambda i: (0, 0))
    return pl.pallas_call(
        body, name=name, grid=(s // ROWS,), in_specs=[row, vec, row, row], out_specs=[row, vec],
        out_shape=[jax.ShapeDtypeStruct((s, d), F32), jax.ShapeDtypeStruct((1, d), F32)],
        compiler_params=_params(("arbitrary",)),
    )(h, g.reshape(1, d), dxn, dres)


def _loss_head(h, g, target):
    s, d = h.shape

    def body(h_ref, g_ref, t_ref, loss_ref, dh_ref, dg_ref):
        @pl.when(pl.program_id(0) == 0)
        def _():
            loss_ref[...] = jnp.zeros_like(loss_ref)
            dg_ref[...] = jnp.zeros_like(dg_ref)

        x = h_ref[...]
        r = lax.rsqrt(jnp.mean(x * x, axis=-1, keepdims=True) + EPS)
        xhat = x * r
        err = xhat * g_ref[...] - t_ref[...]
        loss_ref[...] += 0.5 * jnp.sum(jnp.mean(err * err, axis=-1, keepdims=True), axis=0, keepdims=True)
        dy = err / d
        dg_ref[...] += jnp.sum(dy * xhat, axis=0, keepdims=True)
        dxh = dy * g_ref[...]
        dh_ref[...] = r * (dxh - xhat * jnp.mean(dxh * xhat, axis=-1, keepdims=True))

    row = pl.BlockSpec((ROWS, d), lambda i: (i, 0))
    vec = pl.BlockSpec((1, d), lambda i: (0, 0))
    one = pl.BlockSpec((1, 1), lambda i: (0, 0))
    return pl.pallas_call(
        body, name="loss_head", grid=(s // ROWS,), in_specs=[row, vec, row], out_specs=[one, row, vec],
        out_shape=[jax.ShapeDtypeStruct((1, 1), F32), jax.ShapeDtypeStruct((s, d), F32),
                   jax.ShapeDtypeStruct((1, d), F32)],
        compiler_params=_params(("arbitrary",)),
    )(h, g.reshape(1, d), target)


def _t5_bucket(dist):
    max_exact = N_BUCKETS // 2
    dd = np.maximum(dist, 0)
    large = max_exact + (np.log(np.maximum(dd, 1) / max_exact) / np.log(REL_MAX_DIST / max_exact)
                         * (N_BUCKETS - max_exact)).astype(np.int32)
    large = np.minimum(large, N_BUCKETS - 1)
    return np.where(dd < max_exact, dd, large).astype(np.int32)


def _bucket_table():
    qq = np.arange(QBLK)[:, None]
    kk = np.arange(QBLK)[None, :]
    out = np.zeros((len(BRANCH_DIL), 2, QBLK, QBLK), np.int32)
    for b, dil in enumerate(BRANCH_DIL):
        out[b, 0] = _t5_bucket((qq - kk + QBLK) * dil)
        out[b, 1] = _t5_bucket((qq - kk) * dil)
    return out


BIAS_TILE = 2 * QBLK


def _bias_build(rel_bias):
    idx = jnp.asarray(_bucket_table())

    def body(idx_ref, rb_ref, o_ref):
        ch = pl.program_id(1)
        row = lax.broadcasted_iota(jnp.int32, (QBLK, QBLK), 0)
        col = lax.broadcasted_iota(jnp.int32, (QBLK, QBLK), 1)
        for part in range(2):
            ids = idx_ref[0, 1 - part]
            valid = (col <= row) if part == 0 else (col >= row)
            for h in range(2):
                acc = jnp.zeros((QBLK, QBLK), F32)
                for b in range(N_BUCKETS):
                    acc = jnp.where(ids == b, rb_ref[b, 2 * ch + h], acc)
                o_ref[0, 0, QBLK * h:QBLK * (h + 1), QBLK * part:QBLK * (part + 1)] = jnp.where(valid, acc, NEG_INF)

    return pl.pallas_call(
        body, name="attn_bias_build", grid=(len(BRANCH_DIL), N_HEADS // 2),
        in_specs=[pl.BlockSpec((1, 2, QBLK, QBLK), lambda b, c: (b, 0, 0, 0)),
                  pl.BlockSpec(memory_space=pltpu.SMEM)],
        out_specs=pl.BlockSpec((1, 1, BIAS_TILE, BIAS_TILE), lambda b, c: (b, c, 0, 0)),
        out_shape=jax.ShapeDtypeStruct((len(BRANCH_DIL), N_HEADS // 2, BIAS_TILE, BIAS_TILE), F32),
        compiler_params=_params(("parallel", "parallel")),
    )(idx, rel_bias)


def _bias_reduce(dbias):
    idx = jnp.asarray(_bucket_table())
    nb = len(BRANCH_DIL)

    def body(idx_ref, d_ref, o_ref):
        def per_bucket(b, carry):
            for h in range(N_HEADS):
                tot = jnp.zeros((), F32)
                for br in range(nb):
                    for part in range(2):
                        tile = d_ref[br, h // 2, QBLK * (h % 2):QBLK * (h % 2 + 1), QBLK * part:QBLK * (part + 1)]
                        tot = tot + jnp.sum(jnp.where(idx_ref[br, 1 - part] == b, tile, 0.0))
                o_ref[b, h] = tot
            return carry

        lax.fori_loop(0, N_BUCKETS, per_bucket, 0)

    return pl.pallas_call(
        body, name="attn_bias_reduce",
        in_specs=[pl.BlockSpec(memory_space=pltpu.VMEM), pl.BlockSpec(memory_space=pltpu.VMEM)],
        out_specs=pl.BlockSpec(memory_space=pltpu.SMEM),
        out_shape=jax.ShapeDtypeStruct((N_BUCKETS, N_HEADS), F32),
        compiler_params=pltpu.CompilerParams(vmem_limit_bytes=VMEM_LIMIT_BYTES),
    )(idx, dbias)


def _band_masks(c):
    row = lax.broadcasted_iota(jnp.int32, (QBLK, QBLK), 0)
    col = lax.broadcasted_iota(jnp.int32, (QBLK, QBLK), 1)
    mask_cur = col <= row
    mask_prev = jnp.logical_and(col >= row, c > 0)
    return mask_prev, mask_cur


def _attn_specs(dil):
    blk = (QBLK, ATTN_W)
    q = pl.BlockSpec(blk, lambda r, c: (c, 3 * r))
    kp = pl.BlockSpec(blk, lambda r, c: (jnp.maximum(c - 1, 0), 3 * r + 1))
    kc = pl.BlockSpec(blk, lambda r, c: (c, 3 * r + 1))
    vp = pl.BlockSpec(blk, lambda r, c: (jnp.maximum(c - 1, 0), 3 * r + 2))
    vc = pl.BlockSpec(blk, lambda r, c: (c, 3 * r + 2))
    return [q, kp, kc, vp, vc]


def _attn_fwd_branch(qkv, bias, state, *, branch, last):
    dil = BRANCH_DIL[branch]
    s = qkv.shape[0]
    n = s // dil
    nblk = n // QBLK
    first = state is None

    def body(*refs):
        q_ref, kp_ref, kc_ref, vp_ref, vc_ref, b_ref = refs[:6]
        if first:
            outs = refs[6:]
        else:
            acc_ref, m_ref, l_ref = refs[6:9]
            outs = refs[9:]
        mask_prev, mask_cur = _band_masks(pl.program_id(1))
        for h in range(N_HEADS):
            sl = slice(HEAD_DIM * h, HEAD_DIM * (h + 1))
            qh = q_ref[:, sl]
            s_c = _dot(qh, kc_ref[:, sl], NT) * ATTN_SCALE + b_ref[0, 1, h]
            s_p = _dot(qh, kp_ref[:, sl], NT) * ATTN_SCALE + b_ref[0, 0, h]
            s_c = jnp.where(mask_cur, s_c, NEG_INF)
            s_p = jnp.where(mask_prev, s_p, NEG_INF)
            m_blk = jnp.maximum(jnp.max(s_c, axis=-1, keepdims=True), jnp.max(s_p, axis=-1, keepdims=True))
            if first:
                m_new = m_blk
            else:
                m_old = m_ref[:, sl][:, :1]
                m_new = jnp.maximum(m_old, m_blk)
            p_c = jnp.exp(s_c - m_new)
            p_p = jnp.exp(s_p - m_new)
            l_new = jnp.sum(p_c, axis=-1, keepdims=True) + jnp.sum(p_p, axis=-1, keepdims=True)
            acc = (_dot(p_c.astype(MXU_DTYPE), vc_ref[:, sl], NN)
                   + _dot(p_p.astype(MXU_DTYPE), vp_ref[:, sl], NN))
            if not first:
                alpha = jnp.exp(m_old - m_new)
                l_new = l_new + alpha * l_ref[:, sl][:, :1]
                acc = acc + alpha * acc_ref[:, sl]
            if last:
                outs[0][:, sl] = acc / l_new
                outs[1][:, sl] = jnp.broadcast_to(m_new + jnp.log(l_new), (QBLK, HEAD_DIM))
            else:
                outs[0][:, sl] = acc
                outs[1][:, sl] = jnp.broadcast_to(m_new, (QBLK, HEAD_DIM))
                outs[2][:, sl] = jnp.broadcast_to(l_new, (QBLK, HEAD_DIM))

    st_spec = pl.BlockSpec((QBLK, ATTN_W), lambda r, c: (c, r))
    in_specs = _attn_specs(dil) + [pl.BlockSpec((1, 2, N_HEADS, QBLK, QBLK), lambda r, c: (branch, 0, 0, 0, 0))]
    qv = qkv.reshape(n, dil * 3 * ATTN_W)
    args = [qv] * 5 + [bias]
    if not first:
        in_specs += [st_spec] * 3
        args += [t.reshape(n, dil * ATTN_W) for t in state]
    n_out = 2 if last else 3
    outs = pl.pallas_call(
        body, name=f"attn_fwd_b{branch}", grid=(dil, nblk), in_specs=in_specs,
        out_specs=[st_spec] * n_out,
        out_shape=[jax.ShapeDtypeStruct((n, dil * ATTN_W), F32)] * n_out,
        compiler_params=_params(("parallel", "parallel")),
    )(*args)
    return tuple(t.reshape(s, ATTN_W) for t in outs)


def _attn_fwd(qkv, bias):
    state = None
    for b in range(len(BRANCH_DIL)):
        state = _attn_fwd_branch(qkv, bias, state, branch=b, last=(b == len(BRANCH_DIL) - 1))
    return state


def _attn_bwd_branch(qkv, bias, o, lse, do, *, branch):
    dil = BRANCH_DIL[branch]
    s = qkv.shape[0]
    n = s // dil
    nblk = n // QBLK

    def body(q_ref, kp_ref, kc_ref, vp_ref, vc_ref, b_ref, o_ref, l_ref, do_ref,
             dq_ref, dka_ref, dkb_ref, dva_ref, dvb_ref, db_ref):
        @pl.when(jnp.logical_and(pl.program_id(0) == 0, pl.program_id(1) == 0))
        def _():
            db_ref[...] = jnp.zeros_like(db_ref)

        mask_prev, mask_cur = _band_masks(pl.program_id(1))
        for h in range(N_HEADS):
            sl = slice(HEAD_DIM * h, HEAD_DIM * (h + 1))
            qh = q_ref[:, sl]
            doh = do_ref[:, sl]
            lh = l_ref[:, sl][:, :1]
            delta = jnp.sum(doh * o_ref[:, sl], axis=-1, keepdims=True)
            do_m = doh.astype(MXU_DTYPE)
            s_c = _dot(qh, kc_ref[:, sl], NT) * ATTN_SCALE + b_ref[0, 1, h]
            s_p = _dot(qh, kp_ref[:, sl], NT) * ATTN_SCALE + b_ref[0, 0, h]
            p_c = jnp.exp(jnp.where(mask_cur, s_c, NEG_INF) - lh)
            p_p = jnp.exp(jnp.where(mask_prev, s_p, NEG_INF) - lh)
            ds_c = p_c * (_dot(do_m, vc_ref[:, sl], NT) - delta)
            ds_p = p_p * (_dot(do_m, vp_ref[:, sl], NT) - delta)
            db_ref[0, 1, h] += ds_c
            db_ref[0, 0, h] += ds_p
            ds_c_m = ds_c.astype(MXU_DTYPE)
            ds_p_m = ds_p.astype(MXU_DTYPE)
            dq = _dot(ds_c_m, kc_ref[:, sl], NN) + _dot(ds_p_m, kp_ref[:, sl], NN)
            dq_ref[:, sl] = (dq * ATTN_SCALE).astype(dq_ref.dtype)
            dka_ref[:, sl] = (_dot(ds_c_m, qh, TN) * ATTN_SCALE).astype(dka_ref.dtype)
            dkb_ref[:, sl] = (_dot(ds_p_m, qh, TN) * ATTN_SCALE).astype(dkb_ref.dtype)
            dva_ref[:, sl] = _dot(p_c.astype(MXU_DTYPE), do_m, TN).astype(dva_ref.dtype)
            dvb_ref[:, sl] = _dot(p_p.astype(MXU_DTYPE), do_m, TN).astype(dvb_ref.dtype)

    st_spec = pl.BlockSpec((QBLK, ATTN_W), lambda r, c: (c, r))
    b_in = pl.BlockSpec((1, 2, N_HEADS, QBLK, QBLK), lambda r, c: (branch, 0, 0, 0, 0))
    b_out = pl.BlockSpec((1, 2, N_HEADS, QBLK, QBLK), lambda r, c: (0, 0, 0, 0, 0))
    qv = qkv.reshape(n, dil * 3 * ATTN_W)
    view = lambda t: t.reshape(n, dil * ATTN_W)
    outs = pl.pallas_call(
        body, name=f"attn_bwd_b{branch}", grid=(dil, nblk),
        in_specs=_attn_specs(dil) + [b_in, st_spec, st_spec, st_spec],
        out_specs=[st_spec] * 5 + [b_out],
        out_shape=[jax.ShapeDtypeStruct((n, dil * ATTN_W), MXU_DTYPE)] * 5
        + [jax.ShapeDtypeStruct((1, 2, N_HEADS, QBLK, QBLK), F32)],
        compiler_params=_params(("arbitrary", "arbitrary")),
    )(qv, qv, qv, qv, qv, bias, view(o), view(lse), view(do))
    return tuple(t.reshape(s, ATTN_W) for t in outs[:5]) + (outs[5],)


def _attn_bwd(qkv, bias, o, lse, do):
    s = qkv.shape[0]
    nb = s // QBLK
    parts = [_attn_bwd_branch(qkv, bias, o, lse, do, branch=b) for b in range(len(BRANCH_DIL))]
    dbias = jnp.concatenate([p[5] for p in parts], axis=0)

    def body(*refs):
        o_ref = refs[-1]
        i = pl.program_id(0)
        dq = jnp.zeros((QBLK, ATTN_W), F32)
        dk = jnp.zeros((QBLK, ATTN_W), F32)
        dv = jnp.zeros((QBLK, ATTN_W), F32)
        for b, dil in enumerate(BRANCH_DIL):
            dq_ref, dka_ref, dkb_ref, dva_ref, dvb_ref = refs[5 * b:5 * b + 5]
            inside = i + dil < nb
            dq = dq + dq_ref[...].astype(F32)
            dk = dk + dka_ref[...].astype(F32) + jnp.where(inside, dkb_ref[...].astype(F32), 0.0)
            dv = dv + dva_ref[...].astype(F32) + jnp.where(inside, dvb_ref[...].astype(F32), 0.0)
        o_ref[:, 0:ATTN_W] = dq.astype(o_ref.dtype)
        o_ref[:, ATTN_W:2 * ATTN_W] = dk.astype(o_ref.dtype)
        o_ref[:, 2 * ATTN_W:3 * ATTN_W] = dv.astype(o_ref.dtype)

    in_specs, args = [], []
    for b, dil in enumerate(BRANCH_DIL):
        here = pl.BlockSpec((QBLK, ATTN_W), lambda i: (i, 0))
        ahead = pl.BlockSpec((QBLK, ATTN_W), functools.partial(lambda i, d: (jnp.minimum(i + d, nb - 1), 0), d=dil))
        in_specs += [here, here, ahead, here, ahead]
        args += list(parts[b][:5])
    dqkv = pl.pallas_call(
        body, name="attn_bwd_sum", grid=(nb,), in_specs=in_specs,
        out_specs=pl.BlockSpec((QBLK, 3 * ATTN_W), lambda i: (i, 0)),
        out_shape=jax.ShapeDtypeStruct((s, 3 * ATTN_W), MXU_DTYPE),
        compiler_params=_params(("parallel",)),
    )(*args)
    return dqkv, dbias


ATTN_IO_DTYPE = F32
ABLK = 2048
N_CHUNK = ATTN_W // 128


def _rows(start, dil):
    if dil > 1:
        return pl.ds(start, QBLK, stride=dil)
    return pl.ds(pl.multiple_of(start, QBLK), QBLK)


def _low_head():
    return lax.broadcasted_iota(jnp.int32, (QBLK, 128), 1) < HEAD_DIM


def _head_split(t):
    low = _low_head()
    zero = jnp.zeros_like(t)
    return jnp.where(low, t, zero), jnp.where(low, zero, t)


def _tile_bias(b_ref, branch, first):
    bias = b_ref[branch]
    if first is None:
        return bias
    col = lax.broadcasted_iota(jnp.int32, (BIAS_TILE, BIAS_TILE), 1)
    return jnp.where(jnp.logical_and(first, col >= QBLK), NEG_INF, bias)


def _loop(n, fn):
    if n == 1:
        fn(jnp.int32(0), 0)
    elif n > 1:
        lax.fori_loop(0, n, fn, 0, unroll=2)


def _for_each_tile(tile, c):
    for branch, dil in enumerate(BRANCH_DIL):
        span = QBLK * dil

        def edge(r, carry, branch=branch, span=span):
            tile(branch, r, False, ABLK - span + r, c == 0)
            return carry

        def inner(t, carry, branch=branch, span=span, dil=dil):
            start = (1 + t // dil) * span + t % dil
            tile(branch, start, True, start - span, None)
            return carry

        _loop(dil, edge)
        _loop((ABLK // span - 1) * dil, inner)


def _attn_chunk_specs(nb):
    blk = (None, ABLK, 128)
    prev = lambda c: jnp.maximum(c - 1, 0)
    return [pl.BlockSpec(blk, lambda ch, c: (ch, c, 0)),
            pl.BlockSpec(blk, lambda ch, c: (N_CHUNK + ch, c, 0)),
            pl.BlockSpec(blk, lambda ch, c: (2 * N_CHUNK + ch, c, 0)),
            pl.BlockSpec(blk, lambda ch, c: (N_CHUNK + ch, prev(c), 0)),
            pl.BlockSpec(blk, lambda ch, c: (2 * N_CHUNK + ch, prev(c), 0)),
            pl.BlockSpec((len(BRANCH_DIL), None, BIAS_TILE, BIAS_TILE), lambda ch, c: (0, ch, 0, 0))]


def _in_proj(xn, w_in, layer):
    s, k = xn.shape
    tm = 512
    nch = O_SGU // 128

    def body(x_ref, w_ref, qkv_ref, zs_ref, us_ref):
        acc = _dot(x_ref[...].astype(MXU_DTYPE), w_ref[...].astype(MXU_DTYPE), NN)
        for j in range(nch):
            blk = acc[:, 128 * j:128 * (j + 1)]
            if j < N_CHUNK:
                blk = blk * ATTN_SCALE
            qkv_ref[j] = blk.astype(qkv_ref.dtype)
        zs_ref[...] = acc[:, O_SGU:O_SSM]
        us_ref[...] = acc[:, O_SSM:]

    n = w_in.shape[-1]
    return pl.pallas_call(
        body, name="in_proj", grid=(s // tm,),
        in_specs=[pl.BlockSpec((tm, k), lambda i: (i, 0)), pl.BlockSpec((None, k, n), lambda i: (layer, 0, 0))],
        out_specs=[pl.BlockSpec((nch, tm, 128), lambda i: (0, i, 0)),
                   pl.BlockSpec((tm, O_SSM - O_SGU), lambda i: (i, 0)), pl.BlockSpec((tm, n - O_SSM), lambda i: (i, 0))],
        out_shape=[jax.ShapeDtypeStruct((nch, s, 128), ATTN_IO_DTYPE),
                   jax.ShapeDtypeStruct((s, O_SSM - O_SGU), F32), jax.ShapeDtypeStruct((s, n - O_SSM), F32)],
        compiler_params=_params(("parallel",)),
    )(xn, w_in)


def _attn2_fwd(qkv_c, bias):
    s = qkv_c.shape[1]
    nb = s // ABLK
    last = len(BRANCH_DIL) - 1

    def body(q_ref, kc_ref, vc_ref, kp_ref, vp_ref, b_ref, o_ref, l_ref, acc_s, m_s, l_s):
        low = _low_head()
        e_st = jnp.concatenate(_head_split(jnp.ones((QBLK, 128), MXU_DTYPE)) * 2, axis=0)

        def tile(branch, start, prev_in_block, pstart, first):
            dil = BRANCH_DIL[branch]
            rq, rp = _rows(start, dil), _rows(pstart, dil)
            k_ref, v_ref = (kc_ref, vc_ref) if prev_in_block else (kp_ref, vp_ref)
            q_st = jnp.concatenate(_head_split(q_ref[rq, :].astype(MXU_DTYPE)), axis=0)
            k_st = jnp.concatenate([kc_ref[rq, :].astype(MXU_DTYPE), k_ref[rp, :].astype(MXU_DTYPE)], axis=0)
            v_st = jnp.concatenate(_head_split(vc_ref[rq, :].astype(MXU_DTYPE))
                                   + _head_split(v_ref[rp, :].astype(MXU_DTYPE)), axis=0)
            sc = _dot(q_st, k_st, NT) + _tile_bias(b_ref, branch, first)
            m_new = jnp.max(sc, axis=-1, keepdims=True)
            if branch > 0:
                m_old2 = m_s[rq, :]
                m_old = jnp.concatenate([m_old2[:, 0:1], m_old2[:, HEAD_DIM:HEAD_DIM + 1]], axis=0)
                m_new = jnp.maximum(m_old, m_new)
                alpha = jnp.exp(m_old - m_new)
            p = jnp.exp(sc - m_new).astype(MXU_DTYPE)
            lhs = jnp.concatenate([p[:QBLK, :QBLK], p[QBLK:, :QBLK], p[:QBLK, QBLK:], p[QBLK:, QBLK:]], axis=1)
            acc2 = _dot(lhs, v_st, NN)
            sum2 = _dot(lhs, e_st, NN)
            m2 = jnp.where(low, m_new[:QBLK], m_new[QBLK:])
            if branch > 0:
                a2 = jnp.where(low, alpha[:QBLK], alpha[QBLK:])
                acc2 = acc2 + a2 * acc_s[rq, :]
                sum2 = sum2 + a2 * l_s[rq, :]
            if branch == last:
                o_ref[rq, :] = acc2 / sum2
                l_ref[rq, :] = m2 + jnp.log(sum2)
            else:
                acc_s[rq, :] = acc2
                m_s[rq, :] = m2
                l_s[rq, :] = sum2

        _for_each_tile(tile, pl.program_id(1))

    out_spec = pl.BlockSpec((None, ABLK, 128), lambda ch, c: (ch, c, 0))
    return pl.pallas_call(
        body, name="attn_fwd", grid=(N_CHUNK, nb), in_specs=_attn_chunk_specs(nb),
        out_specs=[out_spec, out_spec],
        out_shape=[jax.ShapeDtypeStruct((N_CHUNK, s, 128), F32)] * 2,
        scratch_shapes=[pltpu.VMEM((ABLK, 128), F32)] * 3,
        compiler_params=_params(("parallel", "arbitrary")),
    )(qkv_c, qkv_c, qkv_c, qkv_c, qkv_c, bias)


def _attn2_bwd(qkv_c, bias, lse_c, delta_c, do_c):
    s = qkv_c.shape[1]
    nb = s // ABLK
    nbr = len(BRANCH_DIL)

    def body(q_ref, kc_ref, vc_ref, kp_ref, vp_ref, b_ref, l_ref, dl_ref, do_ref,
             dq_ref, dk_ref, dv_ref, *rest):
        ek_refs, ev_refs, db_ref = rest[:nbr], rest[nbr:2 * nbr], rest[2 * nbr]
        c = pl.program_id(1)

        @pl.when(c == 0)
        def _():
            db_ref[...] = jnp.zeros_like(db_ref)

        for r in (dq_ref, dk_ref, dv_ref) + tuple(ek_refs) + tuple(ev_refs):
            r[...] = jnp.zeros_like(r)

        def tile(branch, start, prev_in_block, pstart, first):
            dil = BRANCH_DIL[branch]
            rq, rp = _rows(start, dil), _rows(pstart, dil)
            k_ref, v_ref = (kc_ref, vc_ref) if prev_in_block else (kp_ref, vp_ref)
            kc2 = kc_ref[rq, :].astype(MXU_DTYPE)
            kp2 = k_ref[rp, :].astype(MXU_DTYPE)
            q_st = jnp.concatenate(_head_split(q_ref[rq, :].astype(MXU_DTYPE)), axis=0)
            do_st = jnp.concatenate(_head_split(do_ref[rq, :].astype(MXU_DTYPE)), axis=0)
            k_st = jnp.concatenate([kc2, kp2], axis=0)
            v_st = jnp.concatenate([vc_ref[rq, :].astype(MXU_DTYPE), v_ref[rp, :].astype(MXU_DTYPE)], axis=0)
            kh_st = jnp.concatenate(_head_split(kc2) + _head_split(kp2), axis=0)
            lse2 = l_ref[rq, :]
            del2 = dl_ref[rq, :]
            lse_st = jnp.concatenate([lse2[:, 0:1], lse2[:, HEAD_DIM:HEAD_DIM + 1]], axis=0)
            del_st = jnp.concatenate([del2[:, 0:1], del2[:, HEAD_DIM:HEAD_DIM + 1]], axis=0)
            p = jnp.exp(_dot(q_st, k_st, NT) + _tile_bias(b_ref, branch, first) - lse_st)
            ds = p * (_dot(do_st, v_st, NT) - del_st)
            db_ref[branch] += ds
            ds = ds.astype(MXU_DTYPE)
            p = p.astype(MXU_DTYPE)
            lhs = jnp.concatenate([ds[:QBLK, :QBLK], ds[QBLK:, :QBLK], ds[:QBLK, QBLK:], ds[QBLK:, QBLK:]], axis=1)
            dk_st = _dot(ds, q_st, TN)
            dv_st = _dot(p, do_st, TN)
            dq_ref[rq, :] += _dot(lhs, kh_st, NN)
            dk_ref[rq, :] += dk_st[:QBLK]
            dv_ref[rq, :] += dv_st[:QBLK]
            if prev_in_block:
                dk_ref[rp, :] += dk_st[QBLK:]
                dv_ref[rp, :] += dv_st[QBLK:]
            else:
                ek_refs[branch][rq, :] = dk_st[QBLK:]
                ev_refs[branch][rq, :] = dv_st[QBLK:]

        _for_each_tile(tile, c)

    blk = pl.BlockSpec((None, ABLK, 128), lambda ch, c: (ch, c, 0))
    outs = pl.pallas_call(
        body, name="attn_bwd", grid=(N_CHUNK, nb), in_specs=_attn_chunk_specs(nb) + [blk, blk, blk],
        out_specs=[blk] * (3 + 2 * nbr) + [pl.BlockSpec((nbr, None, BIAS_TILE, BIAS_TILE), lambda ch, c: (0, ch, 0, 0))],
        out_shape=[jax.ShapeDtypeStruct((N_CHUNK, s, 128), F32)] * (3 + 2 * nbr)
        + [jax.ShapeDtypeStruct((nbr, N_HEADS // 2, BIAS_TILE, BIAS_TILE), F32)],
        compiler_params=_params(("arbitrary", "arbitrary")),
    )(qkv_c, qkv_c, qkv_c, qkv_c, qkv_c, bias, lse_c, delta_c, do_c)
    return outs[0], outs[1], outs[2], outs[3:3 + nbr], outs[3 + nbr:3 + 2 * nbr], outs[3 + 2 * nbr]


def _attn2_bwd_sum(dq, dk, dv, ek, ev, dzs, dus):
    s = dq.shape[1]
    nrb = s // QBLK
    per_blk = ABLK // QBLK
    nbr = len(BRANCH_DIL)

    def body(*refs):
        dq_ref, dk_ref, dv_ref = refs[:3]
        ek_refs, ev_refs = refs[3:3 + nbr], refs[3 + nbr:3 + 2 * nbr]
        dzs_ref, dus_ref, o_ref = refs[3 + 2 * nbr:]
        i = pl.program_id(0)
        dkt, dvt = dk_ref[...], dv_ref[...]
        for b, dil in enumerate(BRANCH_DIL):
            j = i + dil
            ok = jnp.logical_and(j < nrb, j % per_blk < dil)
            dkt = dkt + jnp.where(ok, ek_refs[b][...], 0.0)
            dvt = dvt + jnp.where(ok, ev_refs[b][...], 0.0)
        for ch in range(N_CHUNK):
            o_ref[:, 128 * ch:128 * (ch + 1)] = (dq_ref[ch] * ATTN_SCALE).astype(o_ref.dtype)
            o_ref[:, ATTN_W + 128 * ch:ATTN_W + 128 * (ch + 1)] = dkt[ch].astype(o_ref.dtype)
            o_ref[:, 2 * ATTN_W + 128 * ch:2 * ATTN_W + 128 * (ch + 1)] = dvt[ch].astype(o_ref.dtype)
        o_ref[:, O_SGU:O_SSM] = dzs_ref[...].astype(o_ref.dtype)
        o_ref[:, O_SSM:] = dus_ref[...].astype(o_ref.dtype)

    here = pl.BlockSpec((N_CHUNK, QBLK, 128), lambda i: (0, i, 0))
    edge_specs = [pl.BlockSpec((N_CHUNK, QBLK, 128),
                               functools.partial(lambda i, d: (0, jnp.minimum(i + d, nrb - 1), 0), d=dil))
                  for dil in BRANCH_DIL]
    return pl.pallas_call(
        body, name="attn_bwd_sum", grid=(nrb,),
        in_specs=[here, here, here] + edge_specs + edge_specs
        + [pl.BlockSpec((QBLK, 2 * SGU_W), lambda i: (i, 0)), pl.BlockSpec((QBLK, SSM_W), lambda i: (i, 0))],
        out_specs=pl.BlockSpec((QBLK, O_SSM + SSM_W), lambda i: (i, 0)),
        out_shape=jax.ShapeDtypeStruct((s, O_SSM + SSM_W), MXU_DTYPE),
        compiler_params=_params(("parallel",)),
    )(dq, dk, dv, *ek, *ev, dzs, dus)


SGU_ROWS = 512


def _sgu_norm(v_g):
    mu = jnp.mean(v_g, axis=-1, keepdims=True)
    cen = v_g - mu
    var = jnp.mean(cen * cen, axis=-1, keepdims=True)
    rstd = lax.rsqrt(var + EPS)
    return cen * rstd, rstd


def _sgu_fwd(zs, ln_g, ln_b, w_mask, b_t):
    s = zs.shape[0]
    nch = SGU_ROWS // SGU_CHUNK

    def body(z_ref, g_ref, b_ref, w_ref, bt_ref, o_ref):
        gz = _gelu(z_ref[...])
        for g in range(SGU_G):
            sl = slice(SGU_GW * g, SGU_GW * (g + 1))
            u_g = gz[:, sl]
            xhat, _ = _sgu_norm(gz[:, SGU_W + SGU_GW * g:SGU_W + SGU_GW * (g + 1)])
            vn = (xhat * g_ref[:, sl] + b_ref[:, sl]).astype(MXU_DTYPE)
            wg = w_ref[g].astype(MXU_DTYPE)
            for ci in range(nch):
                rs = slice(SGU_CHUNK * ci, SGU_CHUNK * (ci + 1))
                mixed = _dot(wg, vn[rs], NN) + bt_ref[:, g:g + 1]
                o_ref[rs, sl] = u_g[rs] * mixed

    full = lambda shape: pl.BlockSpec(shape, lambda i: tuple(0 for _ in shape))
    return pl.pallas_call(
        body, name="sgu_fwd", grid=(s // SGU_ROWS,),
        in_specs=[pl.BlockSpec((SGU_ROWS, 2 * SGU_W), lambda i: (i, 0)), full((1, SGU_W)), full((1, SGU_W)),
                  full((SGU_G, SGU_CHUNK, SGU_CHUNK)), full((SGU_CHUNK, SGU_G))],
        out_specs=pl.BlockSpec((SGU_ROWS, SGU_W), lambda i: (i, 0)),
        out_shape=jax.ShapeDtypeStruct((s, SGU_W), F32),
        compiler_params=_params(("parallel",)),
    )(zs, ln_g.reshape(1, SGU_W), ln_b.reshape(1, SGU_W), w_mask, b_t)


def _sgu_bwd(zs, ln_g, ln_b, w_mask, b_t, dy):
    s = zs.shape[0]
    nch = SGU_ROWS // SGU_CHUNK

    def body(z_ref, g_ref, b_ref, w_ref, bt_ref, dy_ref, dz_ref, dg_ref, dbb_ref, dw_ref, dbt_ref):
        @pl.when(pl.program_id(0) == 0)
        def _():
            dg_ref[...] = jnp.zeros_like(dg_ref)
            dbb_ref[...] = jnp.zeros_like(dbb_ref)
            dw_ref[...] = jnp.zeros_like(dw_ref)
            dbt_ref[...] = jnp.zeros_like(dbt_ref)

        z = z_ref[...]
        gz, dgelu = _gelu_pair(z)
        dy = dy_ref[...]
        for g in range(SGU_G):
            sl = slice(SGU_GW * g, SGU_GW * (g + 1))
            sv = slice(SGU_W + SGU_GW * g, SGU_W + SGU_GW * (g + 1))
            u_g = gz[:, sl]
            xhat, rstd = _sgu_norm(gz[:, sv])
            gain = g_ref[:, sl]
            vn = (xhat * gain + b_ref[:, sl]).astype(MXU_DTYPE)
            wg = w_ref[g].astype(MXU_DTYPE)
            dy_g = dy[:, sl]
            dvn_parts = []
            for ci in range(nch):
                rs = slice(SGU_CHUNK * ci, SGU_CHUNK * (ci + 1))
                mixed = _dot(wg, vn[rs], NN) + bt_ref[:, g:g + 1]
                dz_ref[rs, sl] = (dy_g[rs] * mixed * dgelu[rs, sl]).astype(dz_ref.dtype)
                dmixed = dy_g[rs] * u_g[rs]
                dm = dmixed.astype(MXU_DTYPE)
                dvn_parts.append(_dot(wg, dm, TN))
                dw_ref[g] += _dot(dm, vn[rs], NT)
                dbt_ref[:, g:g + 1] += jnp.sum(dmixed, axis=-1, keepdims=True)
            dvn = jnp.concatenate(dvn_parts, axis=0)
            dg_ref[:, sl] += jnp.sum(dvn * xhat, axis=0, keepdims=True)
            dbb_ref[:, sl] += jnp.sum(dvn, axis=0, keepdims=True)
            dxh = dvn * gain
            dv = rstd * (dxh - jnp.mean(dxh, axis=-1, keepdims=True)
                         - xhat * jnp.mean(dxh * xhat, axis=-1, keepdims=True))
            dz_ref[:, sv] = (dv * dgelu[:, sv]).astype(dz_ref.dtype)

    full = lambda shape: pl.BlockSpec(shape, lambda i: tuple(0 for _ in shape))
    return pl.pallas_call(
        body, name="sgu_bwd", grid=(s // SGU_ROWS,),
        in_specs=[pl.BlockSpec((SGU_ROWS, 2 * SGU_W), lambda i: (i, 0)), full((1, SGU_W)), full((1, SGU_W)),
                  full((SGU_G, SGU_CHUNK, SGU_CHUNK)), full((SGU_CHUNK, SGU_G)),
                  pl.BlockSpec((SGU_ROWS, SGU_W), lambda i: (i, 0))],
        out_specs=[pl.BlockSpec((SGU_ROWS, 2 * SGU_W), lambda i: (i, 0)), full((1, SGU_W)), full((1, SGU_W)),
                   full((SGU_G, SGU_CHUNK, SGU_CHUNK)), full((SGU_CHUNK, SGU_G))],
        out_shape=[jax.ShapeDtypeStruct((s, 2 * SGU_W), MXU_DTYPE), jax.ShapeDtypeStruct((1, SGU_W), F32),
                   jax.ShapeDtypeStruct((1, SGU_W), F32), jax.ShapeDtypeStruct((SGU_G, SGU_CHUNK, SGU_CHUNK), F32),
                   jax.ShapeDtypeStruct((SGU_CHUNK, SGU_G), F32)],
        compiler_params=_params(("arbitrary",)),
    )(zs, ln_g.reshape(1, SGU_W), ln_b.reshape(1, SGU_W), w_mask, b_t, dy)


def _ssm_discretize(a_re, a_im, log_dt, b_re, b_im):
    dt = jnp.exp(log_dt)[:, None]
    mag = jnp.exp(a_re * dt)
    ab_re = mag * jnp.cos(a_im * dt)
    ab_im = mag * jnp.sin(a_im * dt)
    den = a_re * a_re + a_im * a_im
    f_re = ((ab_re - 1.0) * a_re + ab_im * a_im) / den
    f_im = (ab_im * a_re - (ab_re - 1.0) * a_im) / den
    bb_re = f_re[:, :, None] * b_re - f_im[:, :, None] * b_im
    bb_im = f_re[:, :, None] * b_im + f_im[:, :, None] * b_re
    return ab_re, ab_im, bb_re, bb_im


def _ssm_operands(a_re, a_im, log_dt, b_re, b_im, c_re, c_im):
    ab_re, ab_im, bb_re, bb_im = _ssm_discretize(a_re, a_im, log_dt, b_re, b_im)
    eye = jnp.eye(SSM_G, dtype=F32)
    b_blk = jnp.einsum("pgnc,gh->gcphn", jnp.stack([bb_re, bb_im]), eye).reshape(SSM_W, 2 * NSTATE)
    c_mat = jnp.einsum("pgcn,gh->pgnhc", jnp.stack([c_re, -c_im]), eye).reshape(2 * NSTATE, SSM_W)
    a_row = jnp.stack([ab_re.reshape(NSTATE), ab_im.reshape(NSTATE)])
    p_re, p_im = a_row[0:1], a_row[1:2]
    while p_re.shape[0] < SSM_TSEG:
        l_re, l_im = p_re[-1:], p_im[-1:]
        p_re, p_im = (jnp.concatenate([p_re, p_re * l_re - p_im * l_im]),
                      jnp.concatenate([p_im, p_re * l_im + p_im * l_re]))
    p_tab = jnp.stack([p_re, p_im])
    return b_blk.astype(MXU_DTYPE), c_mat.astype(MXU_DTYPE), a_row, p_tab


def _lane_chunks():
    return [(lo, lo + SSM_LANE_CHUNK) for lo in range(0, NSTATE, SSM_LANE_CHUNK)]


def _seg_rows(j):
    return pl.ds(pl.multiple_of(j * SSM_NSEG, SSM_NSEG), SSM_NSEG)


def _to_segments(t):
    s, w = t.shape
    return t.reshape(s // SSM_TB, SSM_NSEG, SSM_TSEG, w).transpose(0, 2, 1, 3).reshape(s, w)


def _from_segments(t):
    s, w = t.shape
    return t.reshape(s // SSM_TB, SSM_TSEG, SSM_NSEG, w).transpose(0, 2, 1, 3).reshape(s, w)


def _ssm_local_scan(buf, a_ref, *, reverse):
    ends_re, ends_im = [], []
    for lo, hi in _lane_chunks():
        are = jnp.broadcast_to(a_ref[0:1, lo:hi], (SSM_NSEG, hi - lo))
        aim = jnp.broadcast_to(a_ref[1:2, lo:hi], (SSM_NSEG, hi - lo))
        if reverse:
            aim = -aim

        def step(jj, carry, lo=lo, hi=hi, are=are, aim=aim):
            xr, xi = carry
            j = (SSM_TSEG - 1 - jj) if reverse else jj
            tr = buf[_seg_rows(j), lo:hi]
            ti = buf[_seg_rows(j), NSTATE + lo:NSTATE + hi]
            nr = are * xr - aim * xi + tr
            ni = are * xi + aim * xr + ti
            buf[_seg_rows(j), lo:hi] = nr
            buf[_seg_rows(j), NSTATE + lo:NSTATE + hi] = ni
            return nr, ni

        zero = jnp.zeros((SSM_NSEG, hi - lo), F32)
        xr, xi = lax.fori_loop(0, SSM_TSEG, step, (zero, zero), unroll=4)
        ends_re.append(xr)
        ends_im.append(xi)
    return jnp.concatenate(ends_re, axis=1), jnp.concatenate(ends_im, axis=1)


def _ssm_entry_states(ends_re, ends_im, carry_ref, p_ref, entry_ref, *, reverse):
    at_re = p_ref[0, SSM_TSEG - 1:SSM_TSEG, :]
    at_im = p_ref[1, SSM_TSEG - 1:SSM_TSEG, :]
    if reverse:
        at_im = -at_im
    cur_re = carry_ref[0:1, 0:NSTATE]
    cur_im = carry_ref[0:1, NSTATE:2 * NSTATE]
    order = range(SSM_NSEG - 1, -1, -1) if reverse else range(SSM_NSEG)
    for i in order:
        entry_ref[0, i:i + 1, 0:NSTATE] = cur_re
        entry_ref[0, i:i + 1, NSTATE:2 * NSTATE] = cur_im
        nxt_re = ends_re[i:i + 1] + at_re * cur_re - at_im * cur_im
        nxt_im = ends_im[i:i + 1] + at_re * cur_im + at_im * cur_re
        cur_re, cur_im = nxt_re, nxt_im
    carry_ref[0:1, 0:NSTATE] = cur_re
    carry_ref[0:1, NSTATE:2 * NSTATE] = cur_im


def _ssm_fixup(buf, p_ref, entry_ref, *, reverse):
    for lo, hi in _lane_chunks():
        e_re = entry_ref[0, :, lo:hi]
        e_im = entry_ref[0, :, NSTATE + lo:NSTATE + hi]

        def step(j, carry, lo=lo, hi=hi, e_re=e_re, e_im=e_im):
            jp = (SSM_TSEG - 1 - j) if reverse else j
            pr = p_ref[0, pl.ds(jp, 1), lo:hi]
            pi = p_ref[1, pl.ds(jp, 1), lo:hi]
            if reverse:
                pi = -pi
            buf[_seg_rows(j), lo:hi] = buf[_seg_rows(j), lo:hi] + pr * e_re - pi * e_im
            buf[_seg_rows(j), NSTATE + lo:NSTATE + hi] = (buf[_seg_rows(j), NSTATE + lo:NSTATE + hi]
                                                           + pr * e_im + pi * e_re)
            return carry

        lax.fori_loop(0, SSM_TSEG, step, 0, unroll=4)


def _ssm_fwd(u, ops, d_skip, glu_w, glu_b):
    b_blk, c_mat, a_row, p_tab = ops
    s = u.shape[0]
    nblk = s // SSM_TB

    def body(u_ref, bb_ref, cm_ref, a_ref, p_ref, d_ref, gw_ref, gb_ref, y_ref, entry_ref, xbuf, carry):
        @pl.when(pl.program_id(0) == 0)
        def _():
            carry[...] = jnp.zeros_like(carry)

        uu = u_ref[...]
        xbuf[...] = _dotf(uu, bb_ref[...], NN)
        ends_re, ends_im = _ssm_local_scan(xbuf, a_ref, reverse=False)
        _ssm_entry_states(ends_re, ends_im, carry, p_ref, entry_ref, reverse=False)
        _ssm_fixup(xbuf, p_ref, entry_ref, reverse=False)
        y = _dotf(xbuf[...],cm_ref[...], NN) + d_ref[...] * uu
        y2 = _gelu(y)
        gate = jax.nn.sigmoid(_dot(y2.astype(MXU_DTYPE), gw_ref[...].astype(MXU_DTYPE), NN) + gb_ref[...])
        y_ref[...] = y2 * gate

    full = lambda shape: pl.BlockSpec(shape, lambda i: tuple(0 for _ in shape))
    y_seg, entry = pl.pallas_call(
        body, name="ssm_fwd", grid=(nblk,),
        in_specs=[pl.BlockSpec((SSM_TB, SSM_W), lambda i: (i, 0)), full(b_blk.shape), full(c_mat.shape),
                  full(a_row.shape), full(p_tab.shape), full((1, SSM_W)), full((SSM_W, SSM_W)), full((1, SSM_W))],
        out_specs=[pl.BlockSpec((SSM_TB, SSM_W), lambda i: (i, 0)),
                   pl.BlockSpec((1, SSM_NSEG, 2 * NSTATE), lambda i: (i, 0, 0))],
        out_shape=[jax.ShapeDtypeStruct((s, SSM_W), F32), jax.ShapeDtypeStruct((nblk, SSM_NSEG, 2 * NSTATE), F32)],
        scratch_shapes=[pltpu.VMEM((SSM_TB, 2 * NSTATE), F32), pltpu.VMEM((SSM_NSEG, 2 * NSTATE), F32)],
        compiler_params=_params(("arbitrary",)),
    )(_to_segments(u), b_blk, c_mat, a_row, p_tab, d_skip.reshape(1, SSM_W), glu_w, glu_b.reshape(1, SSM_W))
    return _from_segments(y_seg), entry


def _ssm_bwd(u, entry, ops, d_skip, glu_w, glu_b, dout):
    b_blk, c_mat, a_row, p_tab = ops
    s = u.shape[0]
    nblk = s // SSM_TB

    def body(u_ref, en_ref, bb_ref, cm_ref, a_ref, p_ref, d_ref, gw_ref, gb_ref, do_ref,
             du_ref, dbb_ref, dcm_ref, da_ref, dd_ref, dgw_ref, dgb_ref, xbuf, gbuf, gcarry, gentry):
        @pl.when(pl.program_id(0) == 0)
        def _():
            gcarry[...] = jnp.zeros_like(gcarry)
            for r in (dbb_ref, dcm_ref, da_ref, dd_ref, dgw_ref, dgb_ref):
                r[...] = jnp.zeros_like(r)

        uu = u_ref[...]
        xbuf[...] = _dotf(uu, bb_ref[...], NN)
        _ssm_local_scan(xbuf, a_ref, reverse=False)
        _ssm_fixup(xbuf, p_ref, en_ref, reverse=False)
        y = _dotf(xbuf[...],cm_ref[...], NN) + d_ref[...] * uu
        y2, dgelu = _gelu_pair(y)
        y2m = y2.astype(MXU_DTYPE)
        gwm = gw_ref[...].astype(MXU_DTYPE)
        gate = jax.nn.sigmoid(_dot(y2m, gwm, NN) + gb_ref[...])
        dout = do_ref[...]
        dpre = dout * y2 * gate * (1.0 - gate)
        dprem = dpre.astype(MXU_DTYPE)
        dy2 = dout * gate + _dot(dprem, gwm, NT)
        dgw_ref[...] += _dot(y2m, dprem, TN)
        dgb_ref[...] += jnp.sum(dpre, axis=0, keepdims=True)
        dy = dy2 * dgelu
        dd_ref[...] += jnp.sum(dy * uu, axis=0, keepdims=True)
        dcm_ref[...] += _dotf(xbuf[...],dy, TN)
        gbuf[...] = _dotf(dy, cm_ref[...], NT)
        gs_re, gs_im = _ssm_local_scan(gbuf, a_ref, reverse=True)
        _ssm_entry_states(gs_re, gs_im, gcarry, p_ref, gentry, reverse=True)
        _ssm_fixup(gbuf, p_ref, gentry, reverse=True)
        du_ref[...] = (_dotf(gbuf[...], bb_ref[...], NT) + d_ref[...] * dy).astype(du_ref.dtype)
        dbb_ref[...] += _dotf(uu, gbuf[...], TN)
        for lo, hi in _lane_chunks():
            def step(j, carry, lo=lo, hi=hi):
                acc_re, acc_im = carry
                g_re = gbuf[_seg_rows(j), lo:hi]
                g_im = gbuf[_seg_rows(j), NSTATE + lo:NSTATE + hi]
                x_re = xbuf[_seg_rows(j - 1), lo:hi]
                x_im = xbuf[_seg_rows(j - 1), NSTATE + lo:NSTATE + hi]
                return acc_re + g_re * x_re + g_im * x_im, acc_im + g_im * x_re - g_re * x_im

            g0_re = gbuf[_seg_rows(0), lo:hi]
            g0_im = gbuf[_seg_rows(0), NSTATE + lo:NSTATE + hi]
            e_re = en_ref[0, :, lo:hi]
            e_im = en_ref[0, :, NSTATE + lo:NSTATE + hi]
            init = (g0_re * e_re + g0_im * e_im, g0_im * e_re - g0_re * e_im)
            acc_re, acc_im = lax.fori_loop(1, SSM_TSEG, step, init, unroll=4)
            da_ref[0:1, lo:hi] += jnp.sum(acc_re, axis=0, keepdims=True)
            da_ref[1:2, lo:hi] += jnp.sum(acc_im, axis=0, keepdims=True)

    full = lambda shape: pl.BlockSpec(shape, lambda i: tuple(0 for _ in shape))
    rev = pl.BlockSpec((SSM_TB, SSM_W), lambda i: (nblk - 1 - i, 0))
    outs = pl.pallas_call(
        body, name="ssm_bwd", grid=(nblk,),
        in_specs=[rev, pl.BlockSpec((1, SSM_NSEG, 2 * NSTATE), lambda i: (nblk - 1 - i, 0, 0)),
                  full(b_blk.shape), full(c_mat.shape), full(a_row.shape), full(p_tab.shape),
                  full((1, SSM_W)), full((SSM_W, SSM_W)), full((1, SSM_W)), rev],
        out_specs=[rev, full(b_blk.shape), full(c_mat.shape), full(a_row.shape), full((1, SSM_W)),
                   full((SSM_W, SSM_W)), full((1, SSM_W))],
        out_shape=[jax.ShapeDtypeStruct((s, SSM_W), MXU_DTYPE), jax.ShapeDtypeStruct(b_blk.shape, F32),
                   jax.ShapeDtypeStruct(c_mat.shape, F32), jax.ShapeDtypeStruct(a_row.shape, F32),
                   jax.ShapeDtypeStruct((1, SSM_W), F32), jax.ShapeDtypeStruct((SSM_W, SSM_W), F32),
                   jax.ShapeDtypeStruct((1, SSM_W), F32)],
        scratch_shapes=[pltpu.VMEM((SSM_TB, 2 * NSTATE), F32), pltpu.VMEM((SSM_TB, 2 * NSTATE), F32),
                        pltpu.VMEM((SSM_NSEG, 2 * NSTATE), F32), pltpu.VMEM((1, SSM_NSEG, 2 * NSTATE), F32)],
        compiler_params=_params(("arbitrary",)),
    )(_to_segments(u), entry, b_blk, c_mat, a_row, p_tab, d_skip.reshape(1, SSM_W), glu_w, glu_b.reshape(1, SSM_W),
      _to_segments(dout))
    return (_from_segments(outs[0]),) + tuple(outs[1:])


MIX_SEGS = ((0, ATTN_W), (ATTN_W, ATTN_W + SGU_W), (ATTN_W + SGU_W, D_MODEL))


def _chunks_to_rows(a_ref):
    return jnp.concatenate([a_ref[ch] for ch in range(N_CHUNK)], axis=1)


def _mix_fwd(y_attn_c, y_sgu, y_ssm, gain):
    s = y_sgu.shape[0]

    def body(a_ref, b_ref, c_ref, g_ref, o_ref):
        for x, (lo, hi) in zip((_chunks_to_rows(a_ref), b_ref[...], c_ref[...]), MIX_SEGS):
            r = lax.rsqrt(jnp.mean(x * x, axis=-1, keepdims=True) + EPS)
            o_ref[:, lo:hi] = (x * r * g_ref[:, lo:hi]).astype(o_ref.dtype)

    row = lambda w: pl.BlockSpec((ROWS, w), lambda i: (i, 0))
    return pl.pallas_call(
        body, name="mix_fwd", grid=(s // ROWS,),
        in_specs=[pl.BlockSpec((N_CHUNK, ROWS, 128), lambda i: (0, i, 0)), row(SGU_W), row(SSM_W),
                  pl.BlockSpec((1, D_MODEL), lambda i: (0, 0))],
        out_specs=row(D_MODEL), out_shape=jax.ShapeDtypeStruct((s, D_MODEL), MXU_DTYPE),
        compiler_params=_params(("parallel",)),
    )(y_attn_c, y_sgu, y_ssm, gain.reshape(1, D_MODEL))


def _mix_bwd(y_attn_c, y_sgu, y_ssm, gain, dmix):
    s = y_sgu.shape[0]

    def body(a_ref, b_ref, c_ref, g_ref, dm_ref, da_ref, dl_ref, db_ref, dc_ref, dg_ref):
        @pl.when(pl.program_id(0) == 0)
        def _():
            dg_ref[...] = jnp.zeros_like(dg_ref)

        grads = []
        for x, (lo, hi) in zip((_chunks_to_rows(a_ref), b_ref[...], c_ref[...]), MIX_SEGS):
            r = lax.rsqrt(jnp.mean(x * x, axis=-1, keepdims=True) + EPS)
            xhat = x * r
            dm = dm_ref[:, lo:hi].astype(F32)
            dg_ref[:, lo:hi] += jnp.sum(dm * xhat, axis=0, keepdims=True)
            dxh = dm * g_ref[:, lo:hi]
            grads.append(r * (dxh - xhat * jnp.mean(dxh * xhat, axis=-1, keepdims=True)))
        db_ref[...] = grads[1]
        dc_ref[...] = grads[2]
        low = lax.broadcasted_iota(jnp.int32, (ROWS, 128), 1) < HEAD_DIM
        for ch in range(N_CHUNK):
            d_c = grads[0][:, 128 * ch:128 * (ch + 1)]
            da_ref[ch] = d_c.astype(da_ref.dtype)
            prod = d_c * a_ref[ch]
            dl_ref[ch] = jnp.where(low, jnp.sum(prod[:, :HEAD_DIM], axis=-1, keepdims=True),
                                   jnp.sum(prod[:, HEAD_DIM:], axis=-1, keepdims=True))

    row = lambda w: pl.BlockSpec((ROWS, w), lambda i: (i, 0))
    vec = pl.BlockSpec((1, D_MODEL), lambda i: (0, 0))
    chunked = pl.BlockSpec((N_CHUNK, ROWS, 128), lambda i: (0, i, 0))
    return pl.pallas_call(
        body, name="mix_bwd", grid=(s // ROWS,),
        in_specs=[chunked, row(SGU_W), row(SSM_W), vec, row(D_MODEL)],
        out_specs=[chunked, chunked, row(SGU_W), row(SSM_W), vec],
        out_shape=[jax.ShapeDtypeStruct((N_CHUNK, s, 128), ATTN_IO_DTYPE), jax.ShapeDtypeStruct((N_CHUNK, s, 128), F32),
                   jax.ShapeDtypeStruct((s, SGU_W), F32), jax.ShapeDtypeStruct((s, SSM_W), F32),
                   jax.ShapeDtypeStruct((1, D_MODEL), F32)],
        compiler_params=_params(("arbitrary",)),
    )(y_attn_c, y_sgu, y_ssm, gain.reshape(1, D_MODEL), dmix)


CONV_ROWS = 256
CONV_COLS = 1408
CONV_PAIR = 2 * CONV_COLS
HALO = 16


def _interleave_ff(t):
    lead = t.shape[:-1]
    nb = D_FF // CONV_COLS
    return jnp.swapaxes(t.reshape(lead + (2, nb, CONV_COLS)), -3, -2).reshape(lead + (2 * D_FF,))


def _deinterleave_ff(t):
    lead = t.shape[:-1]
    nb = D_FF // CONV_COLS
    return jnp.swapaxes(t.reshape(lead + (nb, 2, CONV_COLS)), -3, -2).reshape(lead + (2 * D_FF,))


def _causal_taps(x, halo, first):
    patch = 8
    row = lax.broadcasted_iota(jnp.int32, (patch, x.shape[1]), 0)
    h1 = jnp.where(first, 0.0, halo[HALO - 1:HALO, :])
    h2 = jnp.where(first, 0.0, halo[HALO - 2:HALO - 1, :])
    r1 = pltpu.roll(x, 1, 0)
    r2 = pltpu.roll(x, 2, 0)
    top1 = jnp.where(row == 0, h1, r1[0:patch])
    top2 = jnp.where(row == 0, h2, jnp.where(row == 1, h1, r2[0:patch]))
    return jnp.concatenate([top1, r1[patch:]], axis=0), jnp.concatenate([top2, r2[patch:]], axis=0)


def _conv_in_specs():
    halo_idx = lambda i: jnp.maximum(i * (CONV_ROWS // HALO) - 1, 0)
    return [pl.BlockSpec((CONV_ROWS, CONV_PAIR), lambda j, i: (i, j)),
            pl.BlockSpec((HALO, CONV_PAIR), lambda j, i: (halo_idx(i), j)),
            pl.BlockSpec((3, CONV_PAIR), lambda j, i: (0, j)),
            pl.BlockSpec((1, CONV_PAIR), lambda j, i: (0, j))]


def _ffn_act_fwd(hh, conv_w, conv_b):
    s = hh.shape[0]

    def body(m_ref, h_ref, w_ref, b_ref, o_ref):
        first = pl.program_id(1) == 0
        main = m_ref[...].astype(F32)
        x1, x2 = _causal_taps(main, h_ref[...].astype(F32), first)
        conv = w_ref[0:1, :] * x2 + w_ref[1:2, :] * x1 + w_ref[2:3, :] * main + b_ref[...]
        o_ref[...] = (_gelu(conv[:, CONV_COLS:]) * conv[:, :CONV_COLS]).astype(o_ref.dtype)

    return pl.pallas_call(
        body, name="ffn_act_fwd", grid=(D_FF // CONV_COLS, s // CONV_ROWS), in_specs=_conv_in_specs(),
        out_specs=pl.BlockSpec((CONV_ROWS, CONV_COLS), lambda j, i: (i, j)),
        out_shape=jax.ShapeDtypeStruct((s, D_FF), MXU_DTYPE),
        compiler_params=_params(("parallel", "parallel")),
    )(hh, hh, conv_w, conv_b.reshape(1, -1))


def _ffn_act_bwd(hh, conv_w, conv_b, da):
    s = hh.shape[0]
    nrow = s // CONV_ROWS
    ext_rows = CONV_ROWS + HALO

    def body(m_ref, h_ref, w_ref, b_ref, nx_ref, da_ref, dan_ref, o_ref, dw_ref, db_ref):
        first = pl.program_id(1) == 0
        last = pl.program_id(1) == nrow - 1

        @pl.when(first)
        def _():
            dw_ref[...] = jnp.zeros_like(dw_ref)
            db_ref[...] = jnp.zeros_like(db_ref)

        ext = jnp.concatenate([m_ref[...].astype(F32), nx_ref[...].astype(F32)], axis=0)
        x1, x2 = _causal_taps(ext, h_ref[...].astype(F32), first)
        conv = w_ref[0:1, :] * x2 + w_ref[1:2, :] * x1 + w_ref[2:3, :] * ext + b_ref[...]
        da = jnp.concatenate([da_ref[...].astype(F32), jnp.where(last, 0.0, dan_ref[...].astype(F32))], axis=0)
        act, dact = _gelu_pair(conv[:, CONV_COLS:])
        dconv = jnp.concatenate([da * act, da * conv[:, :CONV_COLS] * dact], axis=1)
        dmain = dconv[:CONV_ROWS]
        ahead1 = pltpu.roll(dconv, ext_rows - 1, 0)[:CONV_ROWS]
        ahead2 = pltpu.roll(dconv, ext_rows - 2, 0)[:CONV_ROWS]
        o_ref[...] = (w_ref[2:3, :] * dmain + w_ref[1:2, :] * ahead1 + w_ref[0:1, :] * ahead2).astype(o_ref.dtype)
        for t, tap in enumerate((x2, x1, ext)):
            dw_ref[t:t + 1, :] += jnp.sum(dmain * tap[:CONV_ROWS], axis=0, keepdims=True)
        db_ref[...] += jnp.sum(dmain, axis=0, keepdims=True)

    nxt = lambda i: jnp.minimum((i + 1) * (CONV_ROWS // HALO), s // HALO - 1)
    return pl.pallas_call(
        body, name="ffn_act_bwd", grid=(D_FF // CONV_COLS, nrow),
        in_specs=_conv_in_specs() + [pl.BlockSpec((HALO, CONV_PAIR), lambda j, i: (nxt(i), j)),
                                     pl.BlockSpec((CONV_ROWS, CONV_COLS), lambda j, i: (i, j)),
                                     pl.BlockSpec((HALO, CONV_COLS), lambda j, i: (nxt(i), j))],
        out_specs=[pl.BlockSpec((CONV_ROWS, CONV_PAIR), lambda j, i: (i, j)),
                   pl.BlockSpec((3, CONV_PAIR), lambda j, i: (0, j)), pl.BlockSpec((1, CONV_PAIR), lambda j, i: (0, j))],
        out_shape=[jax.ShapeDtypeStruct((s, 2 * D_FF), MXU_DTYPE), jax.ShapeDtypeStruct((3, 2 * D_FF), F32),
                   jax.ShapeDtypeStruct((1, 2 * D_FF), F32)],
        compiler_params=_params(("parallel", "arbitrary")),
    )(hh, hh, conv_w, conv_b.reshape(1, -1), hh, da, da)


def _ple_weight_specs(layer):
    return [pl.BlockSpec((None, D_MODEL, D_MODEL), lambda i: (layer, 0, 0)),
            pl.BlockSpec((None, PLE_DIM, D_MODEL), lambda i: (layer, 0, 0))]


def _ple_fwd(xn, p, w_gate, w_proj, h, layer):
    s = xn.shape[0]
    tm = 512

    def body(x_ref, p_ref, wg_ref, wp_ref, h_ref, o_ref):
        gate = jax.nn.sigmoid(_dot(x_ref[...].astype(MXU_DTYPE), wg_ref[...].astype(MXU_DTYPE), NN))
        proj = _dot(p_ref[...].astype(MXU_DTYPE), wp_ref[...].astype(MXU_DTYPE), NN)
        o_ref[...] = h_ref[...] + gate * proj

    return pl.pallas_call(
        body, name="ple_fwd", grid=(s // tm,),
        in_specs=[pl.BlockSpec((tm, D_MODEL), lambda i: (i, 0)), pl.BlockSpec((tm, PLE_DIM), lambda i: (i, 0))]
        + _ple_weight_specs(layer) + [pl.BlockSpec((tm, D_MODEL), lambda i: (i, 0))],
        out_specs=pl.BlockSpec((tm, D_MODEL), lambda i: (i, 0)),
        out_shape=jax.ShapeDtypeStruct((s, D_MODEL), F32),
        compiler_params=_params(("parallel",)),
    )(xn, p, w_gate, w_proj, h)


def _ple_bwd(xn, p, w_gate, w_proj, dh, layer):
    s = xn.shape[0]
    tm = 512

    def body(x_ref, p_ref, wg_ref, wp_ref, dh_ref, dpre_ref, dproj_ref):
        gate = jax.nn.sigmoid(_dot(x_ref[...].astype(MXU_DTYPE), wg_ref[...].astype(MXU_DTYPE), NN))
        proj = _dot(p_ref[...].astype(MXU_DTYPE), wp_ref[...].astype(MXU_DTYPE), NN)
        dh = dh_ref[...]
        dpre_ref[...] = (dh * proj * gate * (1.0 - gate)).astype(dpre_ref.dtype)
        dproj_ref[...] = (dh * gate).astype(dproj_ref.dtype)

    row = pl.BlockSpec((tm, D_MODEL), lambda i: (i, 0))
    return pl.pallas_call(
        body, name="ple_bwd", grid=(s // tm,),
        in_specs=[row, pl.BlockSpec((tm, PLE_DIM), lambda i: (i, 0))] + _ple_weight_specs(layer) + [row],
        out_specs=[row, row],
        out_shape=[jax.ShapeDtypeStruct((s, D_MODEL), MXU_DTYPE)] * 2,
        compiler_params=_params(("parallel",)),
    )(xn, p, w_gate, w_proj, dh)


O_SGU = 3 * ATTN_W
O_SSM = O_SGU + 2 * SGU_W


def _layer_consts(w, i):
    causal = jnp.asarray(np.tril(np.ones((SGU_CHUNK, SGU_CHUNK), np.float32)))
    return {
        "sgu_w_mask": w["sgu_w"][i] * causal,
        "sgu_b_t": w["sgu_b"][i].T,
        "ssm_ops": _ssm_operands(w["ssm_a_re"][i], w["ssm_a_im"][i], w["ssm_log_dt"][i], w["ssm_b_re"][i],
                                 w["ssm_b_im"][i], w["ssm_c_re"][i], w["ssm_c_im"][i]),
    }


def _layer_fwd(h0, p_i, w, i, bias):
    c = _layer_consts(w, i)
    xn1 = _rms_fwd(h0, w["norm_attn_g"][i], name="rms_attn_fwd")
    qkv, zs, us = _in_proj(xn1, w["w_in"], i)
    y_attn, lse = _attn2_fwd(qkv, bias)
    y_sgu = _sgu_fwd(zs, w["sgu_ln_g"][i], w["sgu_ln_b"][i], c["sgu_w_mask"], c["sgu_b_t"])
    y_ssm, entry = _ssm_fwd(us, c["ssm_ops"], w["ssm_d"][i], w["ssm_glu_w"][i], w["ssm_glu_b"][i])
    mix = _mix_fwd(y_attn, y_sgu, y_ssm, w["branch_norm_g"][i])
    h1 = _matmul(mix, w["w_out"], name="out_proj", out_dtype=F32, tm=512, tn=1024, residual=h0, layer=i)
    xn2 = _rms_fwd(h1, w["norm_ffn_g"][i], name="rms_ffn_fwd")
    hh = _matmul(xn2, w["ffn_w_up"], name="ffn_up", out_dtype=MXU_DTYPE, tm=1024, tn=1408, layer=i)
    act = _ffn_act_fwd(hh, w["ffn_conv_w"][i], w["ffn_conv_b"][i])
    h2 = _matmul(act, w["ffn_w_down"], name="ffn_down", out_dtype=F32, tm=512, tn=1024, residual=h1, layer=i)
    xn3 = _rms_fwd(h2, w["norm_ple_g"][i], name="rms_ple_fwd")
    h3 = _ple_fwd(xn3, p_i, w["ple_w_gate"], w["ple_w_proj"], h2, i)
    saved = dict(h0=h0, xn1=xn1, qkv=qkv, zs=zs, us=us, y_attn=y_attn, lse=lse, y_sgu=y_sgu, y_ssm=y_ssm,
                 entry=entry, mix=mix, h1=h1, xn2=xn2, hh=hh, act=act, h2=h2, xn3=xn3, consts=c)
    return h3, saved


def _layer_bwd(dh3, sv, p_i, w, i, bias):
    c = sv["consts"]
    g = {}
    dpre, dproj = _ple_bwd(sv["xn3"], p_i, w["ple_w_gate"], w["ple_w_proj"], dh3, i)
    g["ple_w_gate"] = _matmul_tn(sv["xn3"], dpre, name="d_ple_w_gate", tk=1024, tn=1024)
    g["ple_w_proj"] = _matmul_tn(p_i, dproj, name="d_ple_w_proj", tk=256, tn=1024)
    dxn3 = _matmul(dpre, w["ple_w_gate"], name="d_xn_ple", out_dtype=F32, tm=512, tn=1024, trans_b=True, layer=i)
    dh2, g["norm_ple_g"] = _rms_bwd(sv["h2"], w["norm_ple_g"][i], dxn3, dh3, name="rms_ple_bwd")
    g["ffn_w_down"] = _matmul_tn(sv["act"], dh2, name="d_ffn_w_down", tk=1408, tn=1024)
    dact = _matmul(dh2, w["ffn_w_down"], name="d_ffn_act", out_dtype=MXU_DTYPE, tm=512, tn=1408, trans_b=True, layer=i)
    dhh, g["ffn_conv_w"], g["ffn_conv_b"] = _ffn_act_bwd(sv["hh"], w["ffn_conv_w"][i], w["ffn_conv_b"][i], dact)
    g["ffn_w_up"] = _matmul_tn(sv["xn2"], dhh, name="d_ffn_w_up", tk=1024, tn=1408)
    dxn2 = _matmul(dhh, w["ffn_w_up"], name="d_xn_ffn", out_dtype=F32, tm=512, tn=512, trans_b=True, layer=i)
    dh1, g["norm_ffn_g"] = _rms_bwd(sv["h1"], w["norm_ffn_g"][i], dxn2, dh2, name="rms_ffn_bwd")
    g["w_out"] = _matmul_tn(sv["mix"], dh1, name="d_w_out", tk=1024, tn=1024)
    dmix = _matmul(dh1, w["w_out"], name="d_mix", out_dtype=F32, tm=512, tn=1024, trans_b=True, layer=i)
    dy_attn, delta, dy_sgu, dy_ssm, g["branch_norm_g"] = _mix_bwd(sv["y_attn"], sv["y_sgu"], sv["y_ssm"],
                                                                  w["branch_norm_g"][i], dmix)
    dq, dk, dv, ek, ev, dbias = _attn2_bwd(sv["qkv"], bias, sv["lse"], delta, dy_attn)
    dzs, g["sgu_ln_g"], g["sgu_ln_b"], dsw, dsb = _sgu_bwd(sv["zs"], w["sgu_ln_g"][i], w["sgu_ln_b"][i],
                                                          c["sgu_w_mask"], c["sgu_b_t"], dy_sgu)
    causal = jnp.asarray(np.tril(np.ones((SGU_CHUNK, SGU_CHUNK), np.float32)))
    g["sgu_w"] = dsw * causal
    g["sgu_b"] = dsb.T
    dus, dbb, dcm, da, g["ssm_d"], g["ssm_glu_w"], g["ssm_glu_b"] = _ssm_bwd(
        sv["us"], sv["entry"], c["ssm_ops"], w["ssm_d"][i], w["ssm_glu_w"][i], w["ssm_glu_b"][i], dy_ssm)
    dbb5 = dbb.reshape(SSM_G, SSM_C, 2, SSM_G, SSM_N)
    dbbar = jnp.einsum("gcpgn->pgnc", dbb5)
    dcm5 = dcm.reshape(2, SSM_G, SSM_N, SSM_G, SSM_C)
    dcc = jnp.einsum("pgngc->pgcn", dcm5)
    g["ssm_c_re"] = dcc[0]
    g["ssm_c_im"] = -dcc[1]
    da2 = da.reshape(2, SSM_G, SSM_N)
    _, vjp = jax.vjp(_ssm_discretize, w["ssm_a_re"][i], w["ssm_a_im"][i], w["ssm_log_dt"][i],
                     w["ssm_b_re"][i], w["ssm_b_im"][i])
    (g["ssm_a_re"], g["ssm_a_im"], g["ssm_log_dt"], g["ssm_b_re"], g["ssm_b_im"]) = vjp(
        (da2[0], da2[1], dbbar[0], dbbar[1]))
    dz = _attn2_bwd_sum(dq, dk, dv, ek, ev, dzs, dus)
    g["w_in"] = _matmul_tn(sv["xn1"], dz, name="d_w_in", tk=1024, tn=1152)
    dxn1 = _matmul(dz, w["w_in"], name="d_xn_attn", out_dtype=F32, tm=512, tn=1024, trans_b=True, layer=i)
    dh0, g["norm_attn_g"] = _rms_bwd(sv["h0"], w["norm_attn_g"][i], dxn1, dh1, name="rms_attn_bwd")
    for k in ("norm_ple_g", "norm_ffn_g", "branch_norm_g", "norm_attn_g", "sgu_ln_g", "sgu_ln_b", "ssm_d",
              "ssm_glu_b", "ffn_conv_b"):
        g[k] = g[k].reshape(-1)
    return dh0, g, dbias


def _local_step(x, p, target, w, ff_interleaved=False):
    ff_names = ("ffn_conv_b",) if ff_interleaved else FF_SHARDED + ("ffn_conv_b",)
    w = dict(w)
    for k in ff_names:
        w[k] = _interleave_ff(w[k])
    bias = _bias_build(w["rel_bias"])
    h = x
    saved = []
    for i in range(DEPTH):
        h, sv = _layer_fwd(h, p[i], w, i, bias)
        saved.append(sv)
    loss, dh, dgf = _loss_head(h, w["final_norm_g"], target)
    layer_grads = [None] * DEPTH
    dbias = None
    for i in reversed(range(DEPTH)):
        dh, layer_grads[i], db = _layer_bwd(dh, saved[i], p[i], w, i, bias)
        dbias = db if dbias is None else dbias + db
    grads = {k: jnp.stack([layer_grads[i][k] for i in range(DEPTH)]) for k in layer_grads[0]}
    for k in ff_names:
        grads[k] = _deinterleave_ff(grads[k])
    grads["rel_bias"] = _bias_reduce(dbias)
    grads["final_norm_g"] = dgf.reshape(-1)
    return loss, dh, grads


def _pad_rows(a2, mult=16):
    r = (-a2.shape[0]) % mult
    return a2 if r == 0 else jnp.concatenate([a2, jnp.zeros((r, a2.shape[1]), a2.dtype)], axis=0)


def _as_rows(a, rows=None):
    size = int(np.prod(a.shape))
    if rows is None:
        rows = -(-size // (16 * PACK_COLS)) * 16
    if size % PACK_COLS:
        a = jnp.pad(a.reshape(-1), (0, (-size) % PACK_COLS))
    a2 = a.reshape(-1, PACK_COLS)
    return jnp.pad(a2, ((0, rows - a2.shape[0]), (0, 0)))


def _shard_shape(name):
    full, ax = BIG_FULL[name]
    shp = [DEPTH] + list(full)
    shp[ax] //= N_CHIPS
    return tuple(shp)


EXACT_NAMES = ("ffn_conv_w",)


def _pack_rows_of(name):
    n = int(np.prod(_shard_shape(name))) * (2 if name in EXACT_NAMES else 1)
    rows = -(-n // PACK_COLS)
    return -(-rows // 16) * 16


def _pack_shards(shards, dtype, exact=False):
    split_words = exact and jnp.dtype(dtype).itemsize == 2
    parts = []
    for n in BIG_NAMES:
        a = shards[n]
        if split_words and n in EXACT_NAMES:
            a = lax.bitcast_convert_type(a.astype(F32), dtype)
        parts.append(_as_rows(a.astype(dtype), _pack_rows_of(n)))
    used = sum(pt.shape[0] for pt in parts)
    parts.append(jnp.zeros((PACK_ROWS - used, PACK_COLS), dtype))
    return jnp.concatenate(parts, axis=0)


def _unpack_shard(flat, name, exact=False):
    off = 0
    for n in BIG_NAMES:
        if n == name:
            break
        off += _pack_rows_of(n)
    shp = _shard_shape(name)
    cnt = int(np.prod(shp))
    if exact and name in EXACT_NAMES and jnp.dtype(flat.dtype).itemsize == 2:
        vec = flat[off:off + _pack_rows_of(name)].reshape(-1)
        return lax.bitcast_convert_type(vec[:2 * cnt].reshape(shp + (2,)), F32)
    if cnt % PACK_COLS == 0:
        return flat[off:off + cnt // PACK_COLS].reshape(shp)
    return flat[off:off + _pack_rows_of(name)].reshape(-1)[:cnt].reshape(shp)


FF_SHARDED = ("ffn_w_up", "ffn_conv_w")
FF_CHIP_ORDER = (0, 2, 1, 3)


def _chip_order(name):
    return FF_CHIP_ORDER if name in FF_SHARDED else tuple(range(N_CHIPS))


def _split_full(full, name):
    _, ax = BIG_FULL[name]
    parts = jnp.split(full, N_CHIPS, axis=ax)
    out = [None] * N_CHIPS
    for j, k in enumerate(_chip_order(name)):
        out[k] = parts[j]
    return out


def _join_shards(shards, name):
    _, ax = BIG_FULL[name]
    return jnp.concatenate([shards[k] for k in _chip_order(name)], axis=ax)


def _small_shapes(w):
    return [(n, w[n].shape) for n in SMALL_NAMES]


def _small_rows(shp):
    return -(-int(np.prod(shp)) // (8 * PACK_COLS)) * 8


def _pack_small(d):
    parts = [_as_rows(d[n].astype(F32), _small_rows(d[n].shape)) for n in SMALL_NAMES]
    used = sum(pt.shape[0] for pt in parts)
    parts.append(jnp.zeros((SMALL_ROWS - used, PACK_COLS), F32))
    return jnp.concatenate(parts, axis=0)


def _unpack_small(flat, shapes):
    out, off = {}, 0
    for n, shp in shapes:
        cnt = int(np.prod(shp))
        rows = _small_rows(shp)
        if cnt % PACK_COLS == 0:
            out[n] = flat[off:off + cnt // PACK_COLS].reshape(shp)
        else:
            out[n] = flat[off:off + rows].reshape(-1)[:cnt].reshape(shp)
        off += rows
    return out


MESH = pl.DeviceIdType.MESH
ANY = pl.BlockSpec(memory_space=pl.ANY)


def _me():
    return lax.axis_index("x"), lax.axis_index("y"), lax.axis_index("c")


def _other_chips(x, y):
    return [(1 - x, y), (x, 1 - y), (1 - x, 1 - y)]


def _gather_weights(wflat):
    def body(w_ref, out_ref, send_sems, recv_sems):
        x, y, c = _me()
        sibling = (x, y, 1 - c)
        chips = _other_chips(x, y)

        def rows(chip, half):
            return out_ref.at[2 * chip[0] + chip[1], pl.ds(half * PACK_HALF, PACK_HALF), :]

        def copy(k, chip, half, to, src=None):
            return pltpu.make_async_remote_copy(
                src_ref=rows(chip, half) if src is None else src, dst_ref=rows(chip, half),
                send_sem=send_sems.at[k], recv_sem=recv_sems.at[k], device_id=to, device_id_type=MESH)

        my_half = w_ref.at[pl.ds(c * PACK_HALF, PACK_HALF), :]
        first = [copy(j, (x, y), c, (*chip, c), src=my_half) for j, chip in enumerate(chips)]
        for cp in first:
            cp.start()
        passed = [copy(3 + j, chip, c, sibling) for j, chip in enumerate(chips)]
        for j, chip in enumerate(chips):
            copy(j, chip, c, (x, y, c)).wait_recv()
            passed[j].start()
        for j, chip in enumerate(chips):
            copy(3 + j, chip, 1 - c, (x, y, c)).wait_recv()
        for cp in first + passed:
            cp.wait_send()

    return pl.pallas_call(
        body, name="gather_weights", in_specs=[ANY], out_specs=ANY,
        out_shape=jax.ShapeDtypeStruct((N_CHIPS, PACK_ROWS, PACK_COLS), wflat.dtype),
        scratch_shapes=[pltpu.SemaphoreType.DMA((6,)), pltpu.SemaphoreType.DMA((6,))],
    )(wflat)


def _fill_own_shard(wall, wflat, chip_idx):
    rows = PACK_ROWS // 8

    def body(idx_ref, w_ref, wall_ref, o_ref):
        del idx_ref, wall_ref
        o_ref[...] = w_ref[...]

    return pl.pallas_call(
        body, name="fill_own_shard",
        grid_spec=pltpu.PrefetchScalarGridSpec(
            num_scalar_prefetch=1, grid=(PACK_ROWS // rows,),
            in_specs=[pl.BlockSpec((rows, PACK_COLS), lambda i, idx: (i, 0)), ANY],
            out_specs=pl.BlockSpec((None, rows, PACK_COLS), lambda i, idx: (idx[0], i, 0))),
        out_shape=jax.ShapeDtypeStruct(wall.shape, wall.dtype),
        input_output_aliases={2: 0},
        compiler_params=_params(("parallel",)),
    )(chip_idx, wflat, wall)


def _exchange_partials(gb, gs):
    def body(gb_ref, gs_ref, half_ref, small_ref, send_sems, recv_sems, local_sem):
        x, y, c = _me()
        me_idx = 4 * x + 2 * y + c
        mine = pltpu.make_async_copy(gs_ref, small_ref.at[me_idx], local_sem)
        mine.start()
        d2d = pltpu.make_async_remote_copy(
            src_ref=gb_ref.at[:, pl.ds((1 - c) * PACK_HALF, PACK_HALF), :], dst_ref=half_ref,
            send_sem=send_sems.at[0], recv_sem=recv_sems.at[0], device_id=(x, y, 1 - c), device_id_type=MESH)
        d2d.start()
        copies = []
        for k in range(1, N_DEV):
            fx, fy, fc = (k >> 2) & 1, (k >> 1) & 1, k & 1
            peer = (x ^ fx, y ^ fy, c ^ fc)
            copies.append(pltpu.make_async_remote_copy(
                src_ref=gs_ref, dst_ref=small_ref.at[me_idx], send_sem=send_sems.at[k], recv_sem=recv_sems.at[k],
                device_id=peer, device_id_type=MESH))
        for cp in copies:
            cp.start()
        for k in range(1, N_DEV):
            fx, fy, fc = (k >> 2) & 1, (k >> 1) & 1, k & 1
            peer_idx = 4 * (x ^ fx) + 2 * (y ^ fy) + (c ^ fc)
            pltpu.make_async_remote_copy(
                src_ref=gs_ref, dst_ref=small_ref.at[peer_idx], send_sem=send_sems.at[k], recv_sem=recv_sems.at[k],
                device_id=(x, y, c), device_id_type=MESH).wait_recv()
        d2d.wait_recv()
        d2d.wait_send()
        for cp in copies:
            cp.wait_send()
        mine.wait()

    return pl.pallas_call(
        body, name="exchange_partials", in_specs=[ANY, ANY], out_specs=[ANY, ANY],
        out_shape=[jax.ShapeDtypeStruct((N_CHIPS, PACK_HALF, PACK_COLS), gb.dtype),
                   jax.ShapeDtypeStruct((N_DEV, SMALL_ROWS, PACK_COLS), F32)],
        scratch_shapes=[pltpu.SemaphoreType.DMA((N_DEV,)), pltpu.SemaphoreType.DMA((N_DEV,)), pltpu.SemaphoreType.DMA],
    )(gb, gs)


RED_ROWS = 256


def _chip_partials(gb, sib, c_idx):
    nrow = PACK_HALF // RED_ROWS

    def body(c_ref, a_ref, b_ref, o_ref):
        del c_ref
        o_ref[...] = (a_ref[...].astype(F32) + b_ref[...].astype(F32)).astype(o_ref.dtype)

    blk = (1, RED_ROWS, PACK_COLS)
    return pl.pallas_call(
        body, name="chip_partials",
        grid_spec=pltpu.PrefetchScalarGridSpec(
            num_scalar_prefetch=1, grid=(N_CHIPS, nrow),
            in_specs=[pl.BlockSpec(blk, lambda k, i, c: (k, c[0] * nrow + i, 0)),
                      pl.BlockSpec(blk, lambda k, i, c: (k, i, 0))],
            out_specs=pl.BlockSpec(blk, lambda k, i, c: (k, i, 0))),
        out_shape=jax.ShapeDtypeStruct((N_CHIPS, PACK_HALF, PACK_COLS), gb.dtype),
        compiler_params=_params(("parallel", "parallel")),
    )(c_idx, gb, sib)


def _scatter_partials(pc):
    def body(pc_ref, out_ref, send_sems, recv_sems):
        x, y, c = _me()
        chips = _other_chips(x, y)
        copies = [pltpu.make_async_remote_copy(
            src_ref=pc_ref.at[2 * chip[0] + chip[1]], dst_ref=out_ref.at[k],
            send_sem=send_sems.at[k], recv_sem=recv_sems.at[k], device_id=(*chip, c), device_id_type=MESH)
            for k, chip in enumerate(chips)]
        for cp in copies:
            cp.start()
        for cp in copies:
            cp.wait_recv()
        for cp in copies:
            cp.wait_send()

    return pl.pallas_call(
        body, name="scatter_partials", in_specs=[ANY], out_specs=ANY,
        out_shape=jax.ShapeDtypeStruct((3, PACK_HALF, PACK_COLS), pc.dtype),
        scratch_shapes=[pltpu.SemaphoreType.DMA((3,)), pltpu.SemaphoreType.DMA((3,))],
    )(pc)


def _final_half(gb, sib, recv, idx):
    nrow = PACK_HALF // RED_ROWS

    def body(idx_ref, a_ref, b_ref, r_ref, o_ref):
        del idx_ref
        acc = a_ref[0].astype(F32) + b_ref[0].astype(F32)
        for k in range(3):
            acc = acc + r_ref[k].astype(F32)
        o_ref[...] = acc

    return pl.pallas_call(
        body, name="final_half",
        grid_spec=pltpu.PrefetchScalarGridSpec(
            num_scalar_prefetch=1, grid=(nrow,),
            in_specs=[pl.BlockSpec((1, RED_ROWS, PACK_COLS), lambda i, idx: (idx[0], idx[1] * nrow + i, 0)),
                      pl.BlockSpec((1, RED_ROWS, PACK_COLS), lambda i, idx: (idx[0], i, 0)),
                      pl.BlockSpec((3, RED_ROWS, PACK_COLS), lambda i, idx: (0, i, 0))],
            out_specs=pl.BlockSpec((RED_ROWS, PACK_COLS), lambda i, idx: (i, 0))),
        out_shape=jax.ShapeDtypeStruct((PACK_HALF, PACK_COLS), F32),
        compiler_params=_params(("parallel",)),
    )(idx, gb, sib, recv)


def _share_halves(half):
    def body(h_ref, out_ref, send_sem, recv_sem):
        x, y, c = _me()
        cp = pltpu.make_async_remote_copy(src_ref=h_ref, dst_ref=out_ref, send_sem=send_sem, recv_sem=recv_sem,
                                          device_id=(x, y, 1 - c), device_id_type=MESH)
        cp.start()
        cp.wait_recv()
        cp.wait_send()

    return pl.pallas_call(
        body, name="share_halves", in_specs=[ANY], out_specs=ANY,
        out_shape=jax.ShapeDtypeStruct((PACK_HALF, PACK_COLS), F32),
        scratch_shapes=[pltpu.SemaphoreType.DMA, pltpu.SemaphoreType.DMA],
    )(half)


def _sum_small(allsmall):
    def body(a_ref, o_ref):
        acc = a_ref[0]
        for k in range(1, N_DEV):
            acc = acc + a_ref[k]
        o_ref[...] = acc

    tr = 96
    return pl.pallas_call(
        body, name="sum_small", grid=(SMALL_ROWS // tr,),
        in_specs=[pl.BlockSpec((N_DEV, tr, PACK_COLS), lambda i: (0, i, 0))],
        out_specs=pl.BlockSpec((tr, PACK_COLS), lambda i: (i, 0)),
        out_shape=jax.ShapeDtypeStruct((SMALL_ROWS, PACK_COLS), F32),
        compiler_params=_params(("parallel",)),
    )(allsmall)


def _adamw(w, g, m, v, *, name):
    shape = w.shape
    cols = shape[-1]
    as2 = lambda t: t.reshape(-1, cols)
    w2, g2, m2, v2 = as2(w), as2(g), as2(m), as2(v)
    rows = w2.shape[0]
    tr = rows
    if rows * cols * 4 > (1 << 20):
        tr = _tile(rows, max(8, (1 << 20) // (cols * 4) // 8 * 8), 8)

    def body(w_ref, g_ref, m_ref, v_ref, d_ref, mo_ref, vo_ref):
        gg = g_ref[...]
        mn = ADAM_B1 * m_ref[...] + (1.0 - ADAM_B1) * gg
        vn = ADAM_B2 * v_ref[...] + (1.0 - ADAM_B2) * (gg * gg)
        m_hat = mn / (1.0 - ADAM_B1 ** ADAM_STEP)
        v_hat = vn / (1.0 - ADAM_B2 ** ADAM_STEP)
        d_ref[...] = -ADAM_LR * (m_hat / (jnp.sqrt(v_hat) + ADAM_EPS) + ADAM_WD * w_ref[...])
        mo_ref[...] = mn
        vo_ref[...] = vn

    blk = pl.BlockSpec((tr, cols), lambda i: (i, 0))
    outs = pl.pallas_call(
        body, name=name, grid=(rows // tr,), in_specs=[blk] * 4, out_specs=[blk] * 3,
        out_shape=[jax.ShapeDtypeStruct((rows, cols), F32)] * 3,
        compiler_params=_params(("parallel",)),
    )(w2, g2, m2, v2)
    return tuple(t.reshape(shape) for t in outs)


def kernel(x, p, rel_bias, norm_attn_g, w_in, sgu_ln_g, sgu_ln_b, sgu_w, sgu_b, ssm_a_re, ssm_a_im, ssm_log_dt, ssm_b_re, ssm_b_im, ssm_c_re, ssm_c_im, ssm_d, ssm_glu_w, ssm_glu_b, branch_norm_g, w_out, norm_ffn_g, ffn_w_up, ffn_conv_w, ffn_conv_b, ffn_w_down, norm_ple_g, ple_w_gate, ple_w_proj, final_norm_g, loss_target, m_rel_bias, m_norm_attn_g, m_w_in, m_sgu_ln_g, m_sgu_ln_b, m_sgu_w, m_sgu_b, m_ssm_a_re, m_ssm_a_im, m_ssm_log_dt, m_ssm_b_re, m_ssm_b_im, m_ssm_c_re, m_ssm_c_im, m_ssm_d, m_ssm_glu_w, m_ssm_glu_b, m_branch_norm_g, m_w_out, m_norm_ffn_g, m_ffn_w_up, m_ffn_conv_w, m_ffn_conv_b, m_ffn_w_down, m_norm_ple_g, m_ple_w_gate, m_ple_w_proj, m_final_norm_g, v_rel_bias, v_norm_attn_g, v_w_in, v_sgu_ln_g, v_sgu_ln_b, v_sgu_w, v_sgu_b, v_ssm_a_re, v_ssm_a_im, v_ssm_log_dt, v_ssm_b_re, v_ssm_b_im, v_ssm_c_re, v_ssm_c_im, v_ssm_d, v_ssm_glu_w, v_ssm_glu_b, v_branch_norm_g, v_w_out, v_norm_ffn_g, v_ffn_w_up, v_ffn_conv_w, v_ffn_conv_b, v_ffn_w_down, v_norm_ple_g, v_ple_w_gate, v_ple_w_proj, v_final_norm_g):
    args = dict(locals())
    wts = {n: args[n] for n in WEIGHT_NAMES}
    mom_m = {n: args["m_" + n] for n in WEIGHT_NAMES}
    mom_v = {n: args["v_" + n] for n in WEIGHT_NAMES}

    xi, yi, ci = _me()
    wflat = _pack_shards({n: wts[n] for n in BIG_NAMES}, MXU_DTYPE, exact=True)
    wall = _fill_own_shard(_gather_weights(wflat), wflat, jnp.stack([2 * xi + yi]).astype(jnp.int32))
    full = dict(wts)
    for n in BIG_NAMES:
        full[n] = _join_shards([_unpack_shard(wall[k], n, exact=True) for k in range(N_CHIPS)], n)
    full["ffn_conv_w"] = full["ffn_conv_w"].astype(F32)

    loss, dx, grads = _local_step(x[0], p[:, 0], loss_target[0], full, ff_interleaved=True)
    loss = lax.psum(loss[0, 0], MESH_AXES)

    xi, yi, ci = _me()
    stacked = {n: _split_full(grads[n], n) for n in BIG_NAMES}
    gb = jnp.stack([_pack_shards({n: stacked[n][k] for n in BIG_NAMES}, MXU_DTYPE) for k in range(N_CHIPS)])
    gs = _pack_small(grads)
    sib, allsmall = _exchange_partials(gb, gs)
    pc = _chip_partials(gb, sib, jnp.stack([ci]).astype(jnp.int32))
    recv = _scatter_partials(pc)
    half = _final_half(gb, sib, recv, jnp.stack([2 * xi + yi, ci]).astype(jnp.int32))
    other = _share_halves(half)
    gflat = jnp.concatenate([jnp.where(ci == 0, half, other), jnp.where(ci == 0, other, half)], axis=0)
    gsmall = _unpack_small(_sum_small(allsmall), _small_shapes(wts))

    g_out, d_out, m_out, v_out = {}, {}, {}, {}
    for n in BIG_NAMES:
        g_out[n] = _unpack_shard(gflat, n)
        d_out[n], m_out[n], v_out[n] = _adamw(wts[n], g_out[n], mom_m[n], mom_v[n], name="adamw_" + n)
    sw = _pack_small(wts)
    d_s, m_s, v_s = _adamw(sw, _pack_small(gsmall), _pack_small(mom_m), _pack_small(mom_v), name="adamw_small")
    shapes = _small_shapes(wts)
    d_sm, m_sm, v_sm = _unpack_small(d_s, shapes), _unpack_small(m_s, shapes), _unpack_small(v_s, shapes)
    for n in SMALL_NAMES:
        g_out[n], d_out[n], m_out[n], v_out[n] = gsmall[n], d_sm[n], m_sm[n], v_sm[n]

    return (loss, dx[None], *[g_out[n] for n in WEIGHT_NAMES], *[d_out[n] for n in WEIGHT_NAMES],
            *[m_out[n] for n in WEIGHT_NAMES], *[v_out[n] for n in WEIGHT_NAMES])
```

```python
import functools
import math

import numpy as np
import jax
import jax.numpy as jnp
from jax import lax
from jax.experimental import pallas as pl
from jax.experimental.pallas import tpu as pltpu

F32 = jnp.float32
MXU_DTYPE = jnp.bfloat16
VMEM_LIMIT_BYTES = 52 * 1024 * 1024

D_MODEL = 1024
DEPTH = 2
PLE_DIM = 256
HEAD_DIM = 64
N_HEADS = 8
ATTN_W = 512
QBLK = 128
BRANCH_DIL = (1, 4, 16)
N_BUCKETS = 32
REL_MAX_DIST = 2048
SGU_W = 256
SGU_G = 4
SGU_GW = 64
SGU_CHUNK = 128
SSM_W = 256
SSM_G = 16
SSM_C = 16
SSM_N = 64
NSTATE = SSM_G * SSM_N
D_FF = 2816
EPS = 1e-6
NEG_INF = -1e30
ATTN_SCALE = HEAD_DIM ** -0.5

ADAM_LR = 0.001
ADAM_B1 = 0.9
ADAM_B2 = 0.999
ADAM_EPS = 1e-08
ADAM_WD = 0.01
ADAM_STEP = 10

SSM_NSEG = 8
SSM_TSEG = 64
SSM_TB = SSM_NSEG * SSM_TSEG
SSM_LANE_CHUNK = 512

MESH_AXES = ("x", "y", "c")
N_CHIPS = 4
N_DEV = 8

BIG_NAMES = ("w_in", "ssm_glu_w", "w_out", "ffn_w_up", "ffn_conv_w", "ffn_w_down", "ple_w_gate", "ple_w_proj")
BIG_FULL = {
    "w_in": ((D_MODEL, 2304), 2),
    "ssm_glu_w": ((SSM_W, SSM_W), 1),
    "w_out": ((D_MODEL, D_MODEL), 1),
    "ffn_w_up": ((D_MODEL, 2 * D_FF), 2),
    "ffn_conv_w": ((3, 2 * D_FF), 2),
    "ffn_w_down": ((D_FF, D_MODEL), 1),
    "ple_w_gate": ((D_MODEL, D_MODEL), 1),
    "ple_w_proj": ((PLE_DIM, D_MODEL), 2),
}
PACK_COLS = 1024
PACK_ROWS = 6656
PACK_HALF = PACK_ROWS // 2

SMALL_NAMES = ("rel_bias", "norm_attn_g", "sgu_ln_g", "sgu_ln_b", "sgu_w", "sgu_b", "ssm_a_re", "ssm_a_im",
               "ssm_log_dt", "ssm_b_re", "ssm_b_im", "ssm_c_re", "ssm_c_im", "ssm_d", "ssm_glu_b",
               "branch_norm_g", "norm_ffn_g", "ffn_conv_b", "norm_ple_g", "final_norm_g")
SMALL_ROWS = 384

WEIGHT_NAMES = ("rel_bias", "norm_attn_g", "w_in", "sgu_ln_g", "sgu_ln_b", "sgu_w", "sgu_b", "ssm_a_re", "ssm_a_im",
                "ssm_log_dt", "ssm_b_re", "ssm_b_im", "ssm_c_re", "ssm_c_im", "ssm_d", "ssm_glu_w", "ssm_glu_b",
                "branch_norm_g", "w_out", "norm_ffn_g", "ffn_w_up", "ffn_conv_w", "ffn_conv_b", "ffn_w_down",
                "norm_ple_g", "ple_w_gate", "ple_w_proj", "final_norm_g")


def _params(sem):
    return pltpu.CompilerParams(dimension_semantics=sem, vmem_limit_bytes=VMEM_LIMIT_BYTES)


def _tile(n, cap, mult=128):
    if n <= cap:
        return n
    best = None
    for t in range(mult, cap + 1, mult):
        if n % t == 0:
            best = t
    assert best is not None, (n, cap)
    return best


def _gelu(x):
    return 0.5 * x * (1.0 + jnp.tanh(0.7978845608028654 * (x + 0.044715 * x * x * x)))


def _gelu_pair(x):
    x2 = x * x
    t = jnp.tanh(0.7978845608028654 * x * (1.0 + 0.044715 * x2))
    half = 0.5 * (1.0 + t)
    return x * half, half + 0.5 * x * (1.0 - t * t) * (0.7978845608028654 + 3.0 * 0.044715 * 0.7978845608028654 * x2)


def _dot(a, b, dims):
    return lax.dot_general(a, b, (dims, ((), ())), preferred_element_type=F32)


def _dotf(a, b, dims):
    return _dot(a.astype(MXU_DTYPE), b.astype(MXU_DTYPE), dims)


NN = ((1,), (0,))
NT = ((1,), (1,))
TN = ((0,), (0,))


def _matmul(a, b, *, name, out_dtype, tm, tn, trans_b=False, residual=None, layer=None):
    m, k = a.shape
    n = b.shape[-2] if trans_b else b.shape[-1]
    tm = _tile(m, tm, 8)
    tn = _tile(n, tn)
    dims = NT if trans_b else NN
    lead = () if layer is None else (None,)
    lidx = () if layer is None else (layer,)

    def body(*refs):
        if residual is None:
            a_ref, b_ref, o_ref = refs
        else:
            a_ref, b_ref, r_ref, o_ref = refs
        acc = _dot(a_ref[...].astype(MXU_DTYPE), b_ref[...].astype(MXU_DTYPE), dims)
        if residual is not None:
            acc = acc + r_ref[...]
        o_ref[...] = acc.astype(o_ref.dtype)

    b_spec = (pl.BlockSpec(lead + (tn, k), lambda i, j: lidx + (j, 0)) if trans_b
              else pl.BlockSpec(lead + (k, tn), lambda i, j: lidx + (0, j)))
    in_specs = [pl.BlockSpec((tm, k), lambda i, j: (i, 0)), b_spec]
    args = [a, b]
    if residual is not None:
        in_specs.append(pl.BlockSpec((tm, tn), lambda i, j: (i, j)))
        args.append(residual)
    return pl.pallas_call(
        body, name=name, grid=(m // tm, n // tn), in_specs=in_specs,
        out_specs=pl.BlockSpec((tm, tn), lambda i, j: (i, j)),
        out_shape=jax.ShapeDtypeStruct((m, n), out_dtype),
        compiler_params=_params(("parallel", "parallel")),
    )(*args)


def _matmul_tn(a, g, *, name, tk, tn, tm=512):
    m, k = a.shape
    n = g.shape[1]
    tk = _tile(k, tk)
    tn = _tile(n, tn)
    tm = _tile(m, tm, 8)

    def body(a_ref, g_ref, o_ref):
        @pl.when(pl.program_id(2) == 0)
        def _():
            o_ref[...] = jnp.zeros_like(o_ref)

        o_ref[...] += _dot(a_ref[...].astype(MXU_DTYPE), g_ref[...].astype(MXU_DTYPE), TN)

    return pl.pallas_call(
        body, name=name, grid=(k // tk, n // tn, m // tm),
        in_specs=[pl.BlockSpec((tm, tk), lambda i, j, s: (s, i)),
                  pl.BlockSpec((tm, tn), lambda i, j, s: (s, j))],
        out_specs=pl.BlockSpec((tk, tn), lambda i, j, s: (i, j)),
        out_shape=jax.ShapeDtypeStruct((k, n), F32),
        compiler_params=_params(("parallel", "parallel", "arbitrary")),
    )(a, g)


ROWS = 512


def _rms_fwd(h, g, *, name):
    s, d = h.shape

    def body(h_ref, g_ref, o_ref):
        x = h_ref[...]
        r = lax.rsqrt(jnp.mean(x * x, axis=-1, keepdims=True) + EPS)
        o_ref[...] = (x * r * g_ref[...]).astype(o_ref.dtype)

    return pl.pallas_call(
        body, name=name, grid=(s // ROWS,),
        in_specs=[pl.BlockSpec((ROWS, d), lambda i: (i, 0)), pl.BlockSpec((1, d), lambda i: (0, 0))],
        out_specs=pl.BlockSpec((ROWS, d), lambda i: (i, 0)),
        out_shape=jax.ShapeDtypeStruct((s, d), MXU_DTYPE),
        compiler_params=_params(("parallel",)),
    )(h, g.reshape(1, d))


def _rms_bwd(h, g, dxn, dres, *, name):
    s, d = h.shape

    def body(h_ref, g_ref, dxn_ref, dres_ref, dh_ref, dg_ref):
        @pl.when(pl.program_id(0) == 0)
        def _():
            dg_ref[...] = jnp.zeros_like(dg_ref)

        x = h_ref[...]
        r = lax.rsqrt(jnp.mean(x * x, axis=-1, keepdims=True) + EPS)
        xhat = x * r
        dxn = dxn_ref[...].astype(F32)
        dg_ref[...] += jnp.sum(dxn * xhat, axis=0, keepdims=True)
        dxh = dxn * g_ref[...]
        dh_ref[...] = dres_ref[...] + r * (dxh - xhat * jnp.mean(dxh * xhat, axis=-1, keepdims=True))

    row = pl.BlockSpec((ROWS, d), lambda i: (i, 0))
    vec = pl.BlockSpec((1, d), lambda i: (0, 0))
    return pl.pallas_call(
        body, name=name, grid=(s // ROWS,), in_specs=[row, vec, row, row], out_specs=[row, vec],
        out_shape=[jax.ShapeDtypeStruct((s, d), F32), jax.ShapeDtypeStruct((1, d), F32)],
        compiler_params=_params(("arbitrary",)),
    )(h, g.reshape(1, d), dxn, dres)


def _matmul_rms_bwd(a, b, h, g, dres, *, name, layer, tm):
    s, k = a.shape
    d = b.shape[-2]

    def body(a_ref, b_ref, h_ref, g_ref, dres_ref, dh_ref, dg_ref):
        @pl.when(pl.program_id(0) == 0)
        def _():
            dg_ref[...] = jnp.zeros_like(dg_ref)

        dxn = _dot(a_ref[...].astype(MXU_DTYPE), b_ref[...].astype(MXU_DTYPE), NT)
        x = h_ref[...]
        r = lax.rsqrt(jnp.mean(x * x, axis=-1, keepdims=True) + EPS)
        xhat = x * r
        dg_ref[...] += jnp.sum(dxn * xhat, axis=0, keepdims=True)
        dxh = dxn * g_ref[...]
        dh_ref[...] = dres_ref[...] + r * (dxh - xhat * jnp.mean(dxh * xhat, axis=-1, keepdims=True))

    row = pl.BlockSpec((tm, d), lambda i: (i, 0))
    vec = pl.BlockSpec((1, d), lambda i: (0, 0))
    return pl.pallas_call(
        body, name=name, grid=(s // tm,),
        in_specs=[pl.BlockSpec((tm, k), lambda i: (i, 0)), pl.BlockSpec((None, d, k), lambda i: (layer, 0, 0)),
                  row, vec, row],
        out_specs=[row, vec],
        out_shape=[jax.ShapeDtypeStruct((s, d), F32), jax.ShapeDtypeStruct((1, d), F32)],
        compiler_params=_params(("arbitrary",)),
    )(a, b, h, g.reshape(1, d), dres)


def _loss_head(h, g, target):
    s, d = h.shape

    def body(h_ref, g_ref, t_ref, loss_ref, dh_ref, dg_ref):
        @pl.when(pl.program_id(0) == 0)
        def _():
            loss_ref[...] = jnp.zeros_like(loss_ref)
            dg_ref[...] = jnp.zeros_like(dg_ref)

        x = h_ref[...]
        r = lax.rsqrt(jnp.mean(x * x, axis=-1, keepdims=True) + EPS)
        xhat = x * r
        err = xhat * g_ref[...] - t_ref[...]
        loss_ref[...] += 0.5 * jnp.sum(jnp.mean(err * err, axis=-1, keepdims=True), axis=0, keepdims=True)
        dy = err / d
        dg_ref[...] += jnp.sum(dy * xhat, axis=0, keepdims=True)
        dxh = dy * g_ref[...]
        dh_ref[...] = r * (dxh - xhat * jnp.mean(dxh * xhat, axis=-1, keepdims=True))

    row = pl.BlockSpec((ROWS, d), lambda i: (i, 0))
    vec = pl.BlockSpec((1, d), lambda i: (0, 0))
    one = pl.BlockSpec((1, 1), lambda i: (0, 0))
    return pl.pallas_call(
        body, name="loss_head", grid=(s // ROWS,), in_specs=[row, vec, row], out_specs=[one, row, vec],
        out_shape=[jax.ShapeDtypeStruct((1, 1), F32), jax.ShapeDtypeStruct((s, d), F32),
                   jax.ShapeDtypeStruct((1, d), F32)],
        compiler_params=_params(("arbitrary",)),
    )(h, g.reshape(1, d), target)


def _t5_bucket(dist):
    max_exact = N_BUCKETS // 2
    dd = np.maximum(dist, 0)
    large = max_exact + (np.log(np.maximum(dd, 1) / max_exact) / np.log(REL_MAX_DIST / max_exact)
                         * (N_BUCKETS - max_exact)).astype(np.int32)
    large = np.minimum(large, N_BUCKETS - 1)
    return np.where(dd < max_exact, dd, large).astype(np.int32)


def _bucket_table():
    qq = np.arange(QBLK)[:, None]
    kk = np.arange(QBLK)[None, :]
    out = np.zeros((len(BRANCH_DIL), 2, QBLK, QBLK), np.int32)
    for b, dil in enumerate(BRANCH_DIL):
        out[b, 0] = _t5_bucket((qq - kk + QBLK) * dil)
        out[b, 1] = _t5_bucket((qq - kk) * dil)
    return out


BIAS_TILE = 2 * QBLK


def _bias_build(rel_bias):
    idx = jnp.asarray(_bucket_table())

    def body(idx_ref, rb_ref, o_ref):
        ch = pl.program_id(1)
        row = lax.broadcasted_iota(jnp.int32, (QBLK, QBLK), 0)
        col = lax.broadcasted_iota(jnp.int32, (QBLK, QBLK), 1)
        for part in range(2):
            ids = idx_ref[0, 1 - part]
            valid = (col <= row) if part == 0 else (col >= row)
            for h in range(2):
                acc = jnp.zeros((QBLK, QBLK), F32)
                for b in range(N_BUCKETS):
                    acc = jnp.where(ids == b, rb_ref[b, 2 * ch + h], acc)
                o_ref[0, 0, QBLK * h:QBLK * (h + 1), QBLK * part:QBLK * (part + 1)] = jnp.where(valid, acc, NEG_INF)

    return pl.pallas_call(
        body, name="attn_bias_build", grid=(len(BRANCH_DIL), N_HEADS // 2),
        in_specs=[pl.BlockSpec((1, 2, QBLK, QBLK), lambda b, c: (b, 0, 0, 0)),
                  pl.BlockSpec(memory_space=pltpu.SMEM)],
        out_specs=pl.BlockSpec((1, 1, BIAS_TILE, BIAS_TILE), lambda b, c: (b, c, 0, 0)),
        out_shape=jax.ShapeDtypeStruct((len(BRANCH_DIL), N_HEADS // 2, BIAS_TILE, BIAS_TILE), F32),
        compiler_params=_params(("parallel", "parallel")),
    )(idx, rel_bias)


def _bias_reduce(dbias):
    idx = jnp.asarray(_bucket_table())
    nb = len(BRANCH_DIL)

    def body(idx_ref, d_ref, o_ref):
        def per_bucket(b, carry):
            for h in range(N_HEADS):
                tot = jnp.zeros((), F32)
                for br in range(nb):
                    for part in range(2):
                        tile = d_ref[br, h // 2, QBLK * (h % 2):QBLK * (h % 2 + 1), QBLK * part:QBLK * (part + 1)]
                        tot = tot + jnp.sum(jnp.where(idx_ref[br, 1 - part] == b, tile, 0.0))
                o_ref[b, h] = tot
            return carry

        lax.fori_loop(0, N_BUCKETS, per_bucket, 0)

    return pl.pallas_call(
        body, name="attn_bias_reduce",
        in_specs=[pl.BlockSpec(memory_space=pltpu.VMEM), pl.BlockSpec(memory_space=pltpu.VMEM)],
        out_specs=pl.BlockSpec(memory_space=pltpu.SMEM),
        out_shape=jax.ShapeDtypeStruct((N_BUCKETS, N_HEADS), F32),
        compiler_params=pltpu.CompilerParams(vmem_limit_bytes=VMEM_LIMIT_BYTES),
    )(idx, dbias)


def _band_masks(c):
    row = lax.broadcasted_iota(jnp.int32, (QBLK, QBLK), 0)
    col = lax.broadcasted_iota(jnp.int32, (QBLK, QBLK), 1)
    mask_cur = col <= row
    mask_prev = jnp.logical_and(col >= row, c > 0)
    return mask_prev, mask_cur


def _attn_specs(dil):
    blk = (QBLK, ATTN_W)
    q = pl.BlockSpec(blk, lambda r, c: (c, 3 * r))
    kp = pl.BlockSpec(blk, lambda r, c: (jnp.maximum(c - 1, 0), 3 * r + 1))
    kc = pl.BlockSpec(blk, lambda r, c: (c, 3 * r + 1))
    vp = pl.BlockSpec(blk, lambda r, c: (jnp.maximum(c - 1, 0), 3 * r + 2))
    vc = pl.BlockSpec(blk, lambda r, c: (c, 3 * r + 2))
    return [q, kp, kc, vp, vc]


def _attn_fwd_branch(qkv, bias, state, *, branch, last):
    dil = BRANCH_DIL[branch]
    s = qkv.shape[0]
    n = s // dil
    nblk = n // QBLK
    first = state is None

    def body(*refs):
        q_ref, kp_ref, kc_ref, vp_ref, vc_ref, b_ref = refs[:6]
        if first:
            outs = refs[6:]
        else:
            acc_ref, m_ref, l_ref = refs[6:9]
            outs = refs[9:]
        mask_prev, mask_cur = _band_masks(pl.program_id(1))
        for h in range(N_HEADS):
            sl = slice(HEAD_DIM * h, HEAD_DIM * (h + 1))
            qh = q_ref[:, sl]
            s_c = _dot(qh, kc_ref[:, sl], NT) * ATTN_SCALE + b_ref[0, 1, h]
            s_p = _dot(qh, kp_ref[:, sl], NT) * ATTN_SCALE + b_ref[0, 0, h]
            s_c = jnp.where(mask_cur, s_c, NEG_INF)
            s_p = jnp.where(mask_prev, s_p, NEG_INF)
            m_blk = jnp.maximum(jnp.max(s_c, axis=-1, keepdims=True), jnp.max(s_p, axis=-1, keepdims=True))
            if first:
                m_new = m_blk
            else:
                m_old = m_ref[:, sl][:, :1]
                m_new = jnp.maximum(m_old, m_blk)
            p_c = jnp.exp(s_c - m_new)
            p_p = jnp.exp(s_p - m_new)
            l_new = jnp.sum(p_c, axis=-1, keepdims=True) + jnp.sum(p_p, axis=-1, keepdims=True)
            acc = (_dot(p_c.astype(MXU_DTYPE), vc_ref[:, sl], NN)
                   + _dot(p_p.astype(MXU_DTYPE), vp_ref[:, sl], NN))
            if not first:
                alpha = jnp.exp(m_old - m_new)
                l_new = l_new + alpha * l_ref[:, sl][:, :1]
                acc = acc + alpha * acc_ref[:, sl]
            if last:
                outs[0][:, sl] = acc / l_new
                outs[1][:, sl] = jnp.broadcast_to(m_new + jnp.log(l_new), (QBLK, HEAD_DIM))
            else:
                outs[0][:, sl] = acc
                outs[1][:, sl] = jnp.broadcast_to(m_new, (QBLK, HEAD_DIM))
                outs[2][:, sl] = jnp.broadcast_to(l_new, (QBLK, HEAD_DIM))

    st_spec = pl.BlockSpec((QBLK, ATTN_W), lambda r, c: (c, r))
    in_specs = _attn_specs(dil) + [pl.BlockSpec((1, 2, N_HEADS, QBLK, QBLK), lambda r, c: (branch, 0, 0, 0, 0))]
    qv = qkv.reshape(n, dil * 3 * ATTN_W)
    args = [qv] * 5 + [bias]
    if not first:
        in_specs += [st_spec] * 3
        args += [t.reshape(n, dil * ATTN_W) for t in state]
    n_out = 2 if last else 3
    outs = pl.pallas_call(
        body, name=f"attn_fwd_b{branch}", grid=(dil, nblk), in_specs=in_specs,
        out_specs=[st_spec] * n_out,
        out_shape=[jax.ShapeDtypeStruct((n, dil * ATTN_W), F32)] * n_out,
        compiler_params=_params(("parallel", "parallel")),
    )(*args)
    return tuple(t.reshape(s, ATTN_W) for t in outs)


def _attn_fwd(qkv, bias):
    state = None
    for b in range(len(BRANCH_DIL)):
        state = _attn_fwd_branch(qkv, bias, state, branch=b, last=(b == len(BRANCH_DIL) - 1))
    return state


def _attn_bwd_branch(qkv, bias, o, lse, do, *, branch):
    dil = BRANCH_DIL[branch]
    s = qkv.shape[0]
    n = s // dil
    nblk = n // QBLK

    def body(q_ref, kp_ref, kc_ref, vp_ref, vc_ref, b_ref, o_ref, l_ref, do_ref,
             dq_ref, dka_ref, dkb_ref, dva_ref, dvb_ref, db_ref):
        @pl.when(jnp.logical_and(pl.program_id(0) == 0, pl.program_id(1) == 0))
        def _():
            db_ref[...] = jnp.zeros_like(db_ref)

        mask_prev, mask_cur = _band_masks(pl.program_id(1))
        for h in range(N_HEADS):
            sl = slice(HEAD_DIM * h, HEAD_DIM * (h + 1))
            qh = q_ref[:, sl]
            doh = do_ref[:, sl]
            lh = l_ref[:, sl][:, :1]
            delta = jnp.sum(doh * o_ref[:, sl], axis=-1, keepdims=True)
            do_m = doh.astype(MXU_DTYPE)
            s_c = _dot(qh, kc_ref[:, sl], NT) * ATTN_SCALE + b_ref[0, 1, h]
            s_p = _dot(qh, kp_ref[:, sl], NT) * ATTN_SCALE + b_ref[0, 0, h]
            p_c = jnp.exp(jnp.where(mask_cur, s_c, NEG_INF) - lh)
            p_p = jnp.exp(jnp.where(mask_prev, s_p, NEG_INF) - lh)
            ds_c = p_c * (_dot(do_m, vc_ref[:, sl], NT) - delta)
            ds_p = p_p * (_dot(do_m, vp_ref[:, sl], NT) - delta)
            db_ref[0, 1, h] += ds_c
            db_ref[0, 0, h] += ds_p
            ds_c_m = ds_c.astype(MXU_DTYPE)
            ds_p_m = ds_p.astype(MXU_DTYPE)
            dq = _dot(ds_c_m, kc_ref[:, sl], NN) + _dot(ds_p_m, kp_ref[:, sl], NN)
            dq_ref[:, sl] = (dq * ATTN_SCALE).astype(dq_ref.dtype)
            dka_ref[:, sl] = (_dot(ds_c_m, qh, TN) * ATTN_SCALE).astype(dka_ref.dtype)
            dkb_ref[:, sl] = (_dot(ds_p_m, qh, TN) * ATTN_SCALE).astype(dkb_ref.dtype)
            dva_ref[:, sl] = _dot(p_c.astype(MXU_DTYPE), do_m, TN).astype(dva_ref.dtype)
            dvb_ref[:, sl] = _dot(p_p.astype(MXU_DTYPE), do_m, TN).astype(dvb_ref.dtype)

    st_spec = pl.BlockSpec((QBLK, ATTN_W), lambda r, c: (c, r))
    b_in = pl.BlockSpec((1, 2, N_HEADS, QBLK, QBLK), lambda r, c: (branch, 0, 0, 0, 0))
    b_out = pl.BlockSpec((1, 2, N_HEADS, QBLK, QBLK), lambda r, c: (0, 0, 0, 0, 0))
    qv = qkv.reshape(n, dil * 3 * ATTN_W)
    view = lambda t: t.reshape(n, dil * ATTN_W)
    outs = pl.pallas_call(
        body, name=f"attn_bwd_b{branch}", grid=(dil, nblk),
        in_specs=_attn_specs(dil) + [b_in, st_spec, st_spec, st_spec],
        out_specs=[st_spec] * 5 + [b_out],
        out_shape=[jax.ShapeDtypeStruct((n, dil * ATTN_W), MXU_DTYPE)] * 5
        + [jax.ShapeDtypeStruct((1, 2, N_HEADS, QBLK, QBLK), F32)],
        compiler_params=_params(("arbitrary", "arbitrary")),
    )(qv, qv, qv, qv, qv, bias, view(o), view(lse), view(do))
    return tuple(t.reshape(s, ATTN_W) for t in outs[:5]) + (outs[5],)


def _attn_bwd(qkv, bias, o, lse, do):
    s = qkv.shape[0]
    nb = s // QBLK
    parts = [_attn_bwd_branch(qkv, bias, o, lse, do, branch=b) for b in range(len(BRANCH_DIL))]
    dbias = jnp.concatenate([p[5] for p in parts], axis=0)

    def body(*refs):
        o_ref = refs[-1]
        i = pl.program_id(0)
        dq = jnp.zeros((QBLK, ATTN_W), F32)
        dk = jnp.zeros((QBLK, ATTN_W), F32)
        dv = jnp.zeros((QBLK, ATTN_W), F32)
        for b, dil in enumerate(BRANCH_DIL):
            dq_ref, dka_ref, dkb_ref, dva_ref, dvb_ref = refs[5 * b:5 * b + 5]
            inside = i + dil < nb
            dq = dq + dq_ref[...].astype(F32)
            dk = dk + dka_ref[...].astype(F32) + jnp.where(inside, dkb_ref[...].astype(F32), 0.0)
            dv = dv + dva_ref[...].astype(F32) + jnp.where(inside, dvb_ref[...].astype(F32), 0.0)
        o_ref[:, 0:ATTN_W] = dq.astype(o_ref.dtype)
        o_ref[:, ATTN_W:2 * ATTN_W] = dk.astype(o_ref.dtype)
        o_ref[:, 2 * ATTN_W:3 * ATTN_W] = dv.astype(o_ref.dtype)

    in_specs, args = [], []
    for b, dil in enumerate(BRANCH_DIL):
        here = pl.BlockSpec((QBLK, ATTN_W), lambda i: (i, 0))
        ahead = pl.BlockSpec((QBLK, ATTN_W), functools.partial(lambda i, d: (jnp.minimum(i + d, nb - 1), 0), d=dil))
        in_specs += [here, here, ahead, here, ahead]
        args += list(parts[b][:5])
    dqkv = pl.pallas_call(
        body, name="attn_bwd_sum", grid=(nb,), in_specs=in_specs,
        out_specs=pl.BlockSpec((QBLK, 3 * ATTN_W), lambda i: (i, 0)),
        out_shape=jax.ShapeDtypeStruct((s, 3 * ATTN_W), MXU_DTYPE),
        compiler_params=_params(("parallel",)),
    )(*args)
    return dqkv, dbias


ATTN_IO_DTYPE = F32
ABLK = 2048
N_CHUNK = ATTN_W // 128


def _rows(start, dil):
    if dil > 1:
        return pl.ds(start, QBLK, stride=dil)
    return pl.ds(pl.multiple_of(start, QBLK), QBLK)


def _low_head():
    return lax.broadcasted_iota(jnp.int32, (QBLK, 128), 1) < HEAD_DIM


def _head_split(t):
    low = _low_head()
    zero = jnp.zeros_like(t)
    return jnp.where(low, t, zero), jnp.where(low, zero, t)


def _tile_bias(b_ref, branch, first):
    bias = b_ref[branch]
    if first is None:
        return bias
    col = lax.broadcasted_iota(jnp.int32, (BIAS_TILE, BIAS_TILE), 1)
    return jnp.where(jnp.logical_and(first, col >= QBLK), NEG_INF, bias)


def _loop(n, fn):
    if n == 1:
        fn(jnp.int32(0), 0)
    elif n > 1:
        lax.fori_loop(0, n, fn, 0, unroll=2)


def _for_each_tile(tile, c):
    for branch, dil in enumerate(BRANCH_DIL):
        span = QBLK * dil

        def edge(r, carry, branch=branch, span=span):
            tile(branch, r, False, ABLK - span + r, c == 0)
            return carry

        def inner(t, carry, branch=branch, span=span, dil=dil):
            start = (1 + t // dil) * span + t % dil
            tile(branch, start, True, start - span, None)
            return carry

        _loop(dil, edge)
        _loop((ABLK // span - 1) * dil, inner)


def _attn_chunk_specs(nb):
    blk = (None, ABLK, 128)
    prev = lambda c: jnp.maximum(c - 1, 0)
    return [pl.BlockSpec(blk, lambda ch, c: (ch, c, 0)),
            pl.BlockSpec(blk, lambda ch, c: (N_CHUNK + ch, c, 0)),
            pl.BlockSpec(blk, lambda ch, c: (2 * N_CHUNK + ch, c, 0)),
            pl.BlockSpec(blk, lambda ch, c: (N_CHUNK + ch, prev(c), 0)),
            pl.BlockSpec(blk, lambda ch, c: (2 * N_CHUNK + ch, prev(c), 0)),
            pl.BlockSpec((len(BRANCH_DIL), None, BIAS_TILE, BIAS_TILE), lambda ch, c: (0, ch, 0, 0))]


def _rms_rows(x, g):
    r = lax.rsqrt(jnp.mean(x * x, axis=-1, keepdims=True) + EPS)
    return (x * r * g).astype(MXU_DTYPE)


def _in_proj(h, gain, w_in, layer):
    s, k = h.shape
    tm = 512
    nch = O_SGU // 128

    def body(h_ref, g_ref, w_ref, xn_ref, qkv_ref, zs_ref, us_ref):
        xn = _rms_rows(h_ref[...], g_ref[...])
        xn_ref[...] = xn
        acc = _dot(xn, w_ref[...].astype(MXU_DTYPE), NN)
        for j in range(nch):
            blk = acc[:, 128 * j:128 * (j + 1)]
            if j < N_CHUNK:
                blk = blk * ATTN_SCALE
            qkv_ref[j] = blk.astype(qkv_ref.dtype)
        zs_ref[...] = acc[:, O_SGU:O_SSM]
        us_ref[...] = acc[:, O_SSM:]

    n = w_in.shape[-1]
    return pl.pallas_call(
        body, name="in_proj", grid=(s // tm,),
        in_specs=[pl.BlockSpec((tm, k), lambda i: (i, 0)), pl.BlockSpec((1, k), lambda i: (0, 0)),
                  pl.BlockSpec((None, k, n), lambda i: (layer, 0, 0))],
        out_specs=[pl.BlockSpec((tm, k), lambda i: (i, 0)), pl.BlockSpec((nch, tm, 128), lambda i: (0, i, 0)),
                   pl.BlockSpec((tm, O_SSM - O_SGU), lambda i: (i, 0)), pl.BlockSpec((tm, n - O_SSM), lambda i: (i, 0))],
        out_shape=[jax.ShapeDtypeStruct((s, k), MXU_DTYPE), jax.ShapeDtypeStruct((nch, s, 128), ATTN_IO_DTYPE),
                   jax.ShapeDtypeStruct((s, O_SSM - O_SGU), F32), jax.ShapeDtypeStruct((s, n - O_SSM), F32)],
        compiler_params=_params(("parallel",)),
    )(h, gain.reshape(1, k), w_in)


def _ffn_up(h, gain, w_up, layer):
    s, k = h.shape
    n = w_up.shape[-1]
    tm, tn = 1024, CONV_COLS

    def body(h_ref, g_ref, w_ref, xn_ref, o_ref):
        @pl.when(pl.program_id(1) == 0)
        def _():
            xn_ref[...] = _rms_rows(h_ref[...], g_ref[...])

        o_ref[...] = _dot(xn_ref[...], w_ref[...].astype(MXU_DTYPE), NN).astype(o_ref.dtype)

    return pl.pallas_call(
        body, name="ffn_up", grid=(s // tm, n // tn),
        in_specs=[pl.BlockSpec((tm, k), lambda i, j: (i, 0)), pl.BlockSpec((1, k), lambda i, j: (0, 0)),
                  pl.BlockSpec((None, k, tn), lambda i, j: (layer, 0, j))],
        out_specs=[pl.BlockSpec((tm, k), lambda i, j: (i, 0)), pl.BlockSpec((tm, tn), lambda i, j: (i, j))],
        out_shape=[jax.ShapeDtypeStruct((s, k), MXU_DTYPE), jax.ShapeDtypeStruct((s, n), MXU_DTYPE)],
        compiler_params=_params(("parallel", "arbitrary")),
    )(h, gain.reshape(1, k), w_up)


def _attn2_fwd(qkv_c, bias):
    s = qkv_c.shape[1]
    nb = s // ABLK
    last = len(BRANCH_DIL) - 1

    def body(q_ref, kc_ref, vc_ref, kp_ref, vp_ref, b_ref, o_ref, l_ref, acc_s, m_s, l_s):
        low = _low_head()
        e_st = jnp.concatenate(_head_split(jnp.ones((QBLK, 128), MXU_DTYPE)) * 2, axis=0)

        def tile(branch, start, prev_in_block, pstart, first):
            dil = BRANCH_DIL[branch]
            rq, rp = _rows(start, dil), _rows(pstart, dil)
            k_ref, v_ref = (kc_ref, vc_ref) if prev_in_block else (kp_ref, vp_ref)
            q_st = jnp.concatenate(_head_split(q_ref[rq, :].astype(MXU_DTYPE)), axis=0)
            k_st = jnp.concatenate([kc_ref[rq, :].astype(MXU_DTYPE), k_ref[rp, :].astype(MXU_DTYPE)], axis=0)
            v_st = jnp.concatenate(_head_split(vc_ref[rq, :].astype(MXU_DTYPE))
                                   + _head_split(v_ref[rp, :].astype(MXU_DTYPE)), axis=0)
            sc = _dot(q_st, k_st, NT) + _tile_bias(b_ref, branch, first)
            m_new = jnp.max(sc, axis=-1, keepdims=True)
            if branch > 0:
                m_old2 = m_s[rq, :]
                m_old = jnp.concatenate([m_old2[:, 0:1], m_old2[:, HEAD_DIM:HEAD_DIM + 1]], axis=0)
                m_new = jnp.maximum(m_old, m_new)
                alpha = jnp.exp(m_old - m_new)
            p = jnp.exp(sc - m_new).astype(MXU_DTYPE)
            lhs = jnp.concatenate([p[:QBLK, :QBLK], p[QBLK:, :QBLK], p[:QBLK, QBLK:], p[QBLK:, QBLK:]], axis=1)
            acc2 = _dot(lhs, v_st, NN)
            sum2 = _dot(lhs, e_st, NN)
            m2 = jnp.where(low, m_new[:QBLK], m_new[QBLK:])
            if branch > 0:
                a2 = jnp.where(low, alpha[:QBLK], alpha[QBLK:])
                acc2 = acc2 + a2 * acc_s[rq, :]
                sum2 = sum2 + a2 * l_s[rq, :]
            if branch == last:
                o_ref[rq, :] = acc2 / sum2
                l_ref[rq, :] = m2 + jnp.log(sum2)
            else:
                acc_s[rq, :] = acc2
                m_s[rq, :] = m2
                l_s[rq, :] = sum2

        _for_each_tile(tile, pl.program_id(1))

    out_spec = pl.BlockSpec((None, ABLK, 128), lambda ch, c: (ch, c, 0))
    return pl.pallas_call(
        body, name="attn_fwd", grid=(N_CHUNK, nb), in_specs=_attn_chunk_specs(nb),
        out_specs=[out_spec, out_spec],
        out_shape=[jax.ShapeDtypeStruct((N_CHUNK, s, 128), F32)] * 2,
        scratch_shapes=[pltpu.VMEM((ABLK, 128), F32)] * 3,
        compiler_params=_params(("parallel", "arbitrary")),
    )(qkv_c, qkv_c, qkv_c, qkv_c, qkv_c, bias)


def _attn2_bwd(qkv_c, bias, lse_c, delta_c, do_c):
    s = qkv_c.shape[1]
    nb = s // ABLK
    nbr = len(BRANCH_DIL)

    def body(q_ref, kc_ref, vc_ref, kp_ref, vp_ref, b_ref, l_ref, dl_ref, do_ref,
             dq_ref, dk_ref, dv_ref, *rest):
        ek_refs, ev_refs, db_ref = rest[:nbr], rest[nbr:2 * nbr], rest[2 * nbr]
        c = pl.program_id(1)

        @pl.when(c == 0)
        def _():
            db_ref[...] = jnp.zeros_like(db_ref)

        for r in (dq_ref, dk_ref, dv_ref) + tuple(ek_refs) + tuple(ev_refs):
            r[...] = jnp.zeros_like(r)

        def tile(branch, start, prev_in_block, pstart, first):
            dil = BRANCH_DIL[branch]
            rq, rp = _rows(start, dil), _rows(pstart, dil)
            k_ref, v_ref = (kc_ref, vc_ref) if prev_in_block else (kp_ref, vp_ref)
            kc2 = kc_ref[rq, :].astype(MXU_DTYPE)
            kp2 = k_ref[rp, :].astype(MXU_DTYPE)
            q_st = jnp.concatenate(_head_split(q_ref[rq, :].astype(MXU_DTYPE)), axis=0)
            do_st = jnp.concatenate(_head_split(do_ref[rq, :].astype(MXU_DTYPE)), axis=0)
            k_st = jnp.concatenate([kc2, kp2], axis=0)
            v_st = jnp.concatenate([vc_ref[rq, :].astype(MXU_DTYPE), v_ref[rp, :].astype(MXU_DTYPE)], axis=0)
            kh_st = jnp.concatenate(_head_split(kc2) + _head_split(kp2), axis=0)
            lse2 = l_ref[rq, :]
            del2 = dl_ref[rq, :]
            lse_st = jnp.concatenate([lse2[:, 0:1], lse2[:, HEAD_DIM:HEAD_DIM + 1]], axis=0)
            del_st = jnp.concatenate([del2[:, 0:1], del2[:, HEAD_DIM:HEAD_DIM + 1]], axis=0)
            p = jnp.exp(_dot(q_st, k_st, NT) + _tile_bias(b_ref, branch, first) - lse_st)
            ds = p * (_dot(do_st, v_st, NT) - del_st)
            db_ref[branch] += ds
            ds = ds.astype(MXU_DTYPE)
            p = p.astype(MXU_DTYPE)
            lhs = jnp.concatenate([ds[:QBLK, :QBLK], ds[QBLK:, :QBLK], ds[:QBLK, QBLK:], ds[QBLK:, QBLK:]], axis=1)
            dk_st = _dot(ds, q_st, TN)
            dv_st = _dot(p, do_st, TN)
            dq_ref[rq, :] += _dot(lhs, kh_st, NN)
            dk_ref[rq, :] += dk_st[:QBLK]
            dv_ref[rq, :] += dv_st[:QBLK]
            if prev_in_block:
                dk_ref[rp, :] += dk_st[QBLK:]
                dv_ref[rp, :] += dv_st[QBLK:]
            else:
                ek_refs[branch][rq, :] = dk_st[QBLK:]
                ev_refs[branch][rq, :] = dv_st[QBLK:]

        _for_each_tile(tile, c)

    blk = pl.BlockSpec((None, ABLK, 128), lambda ch, c: (ch, c, 0))
    outs = pl.pallas_call(
        body, name="attn_bwd", grid=(N_CHUNK, nb), in_specs=_attn_chunk_specs(nb) + [blk, blk, blk],
        out_specs=[blk] * (3 + 2 * nbr) + [pl.BlockSpec((nbr, None, BIAS_TILE, BIAS_TILE), lambda ch, c: (0, ch, 0, 0))],
        out_shape=[jax.ShapeDtypeStruct((N_CHUNK, s, 128), F32)] * (3 + 2 * nbr)
        + [jax.ShapeDtypeStruct((nbr, N_HEADS // 2, BIAS_TILE, BIAS_TILE), F32)],
        compiler_params=_params(("arbitrary", "arbitrary")),
    )(qkv_c, qkv_c, qkv_c, qkv_c, qkv_c, bias, lse_c, delta_c, do_c)
    return outs[0], outs[1], outs[2], outs[3:3 + nbr], outs[3 + nbr:3 + 2 * nbr], outs[3 + 2 * nbr]


def _attn2_bwd_sum(dq, dk, dv, ek, ev, dzs, dus):
    s = dq.shape[1]
    nrb = s // QBLK
    per_blk = ABLK // QBLK
    nbr = len(BRANCH_DIL)

    def body(*refs):
        dq_ref, dk_ref, dv_ref = refs[:3]
        ek_refs, ev_refs = refs[3:3 + nbr], refs[3 + nbr:3 + 2 * nbr]
        dzs_ref, dus_ref, o_ref = refs[3 + 2 * nbr:]
        i = pl.program_id(0)
        dkt, dvt = dk_ref[...], dv_ref[...]
        for b, dil in enumerate(BRANCH_DIL):
            j = i + dil
            ok = jnp.logical_and(j < nrb, j % per_blk < dil)
            dkt = dkt + jnp.where(ok, ek_refs[b][...], 0.0)
            dvt = dvt + jnp.where(ok, ev_refs[b][...], 0.0)
        for ch in range(N_CHUNK):
            o_ref[:, 128 * ch:128 * (ch + 1)] = (dq_ref[ch] * ATTN_SCALE).astype(o_ref.dtype)
            o_ref[:, ATTN_W + 128 * ch:ATTN_W + 128 * (ch + 1)] = dkt[ch].astype(o_ref.dtype)
            o_ref[:, 2 * ATTN_W + 128 * ch:2 * ATTN_W + 128 * (ch + 1)] = dvt[ch].astype(o_ref.dtype)
        o_ref[:, O_SGU:O_SSM] = dzs_ref[...].astype(o_ref.dtype)
        o_ref[:, O_SSM:] = dus_ref[...].astype(o_ref.dtype)

    here = pl.BlockSpec((N_CHUNK, QBLK, 128), lambda i: (0, i, 0))
    edge_specs = [pl.BlockSpec((N_CHUNK, QBLK, 128),
                               functools.partial(lambda i, d: (0, jnp.minimum(i + d, nrb - 1), 0), d=dil))
                  for dil in BRANCH_DIL]
    return pl.pallas_call(
        body, name="attn_bwd_sum", grid=(nrb,),
        in_specs=[here, here, here] + edge_specs + edge_specs
        + [pl.BlockSpec((QBLK, 2 * SGU_W), lambda i: (i, 0)), pl.BlockSpec((QBLK, SSM_W), lambda i: (i, 0))],
        out_specs=pl.BlockSpec((QBLK, O_SSM + SSM_W), lambda i: (i, 0)),
        out_shape=jax.ShapeDtypeStruct((s, O_SSM + SSM_W), MXU_DTYPE),
        compiler_params=_params(("parallel",)),
    )(dq, dk, dv, *ek, *ev, dzs, dus)


SGU_ROWS = 512


def _sgu_norm(v_g):
    mu = jnp.mean(v_g, axis=-1, keepdims=True)
    cen = v_g - mu
    var = jnp.mean(cen * cen, axis=-1, keepdims=True)
    rstd = lax.rsqrt(var + EPS)
    return cen * rstd, rstd


def _sgu_fwd(zs, ln_g, ln_b, w_mask, b_t):
    s = zs.shape[0]
    nch = SGU_ROWS // SGU_CHUNK

    def body(z_ref, g_ref, b_ref, w_ref, bt_ref, o_ref):
        gz = _gelu(z_ref[...])
        for g in range(SGU_G):
            sl = slice(SGU_GW * g, SGU_GW * (g + 1))
            u_g = gz[:, sl]
            xhat, _ = _sgu_norm(gz[:, SGU_W + SGU_GW * g:SGU_W + SGU_GW * (g + 1)])
            vn = (xhat * g_ref[:, sl] + b_ref[:, sl]).astype(MXU_DTYPE)
            wg = w_ref[g].astype(MXU_DTYPE)
            for ci in range(nch):
                rs = slice(SGU_CHUNK * ci, SGU_CHUNK * (ci + 1))
                mixed = _dot(wg, vn[rs], NN) + bt_ref[:, g:g + 1]
                o_ref[rs, sl] = u_g[rs] * mixed

    full = lambda shape: pl.BlockSpec(shape, lambda i: tuple(0 for _ in shape))
    return pl.pallas_call(
        body, name="sgu_fwd", grid=(s // SGU_ROWS,),
        in_specs=[pl.BlockSpec((SGU_ROWS, 2 * SGU_W), lambda i: (i, 0)), full((1, SGU_W)), full((1, SGU_W)),
                  full((SGU_G, SGU_CHUNK, SGU_CHUNK)), full((SGU_CHUNK, SGU_G))],
        out_specs=pl.BlockSpec((SGU_ROWS, SGU_W), lambda i: (i, 0)),
        out_shape=jax.ShapeDtypeStruct((s, SGU_W), F32),
        compiler_params=_params(("parallel",)),
    )(zs, ln_g.reshape(1, SGU_W), ln_b.reshape(1, SGU_W), w_mask, b_t)


def _sgu_bwd(zs, ln_g, ln_b, w_mask, b_t, dy):
    s = zs.shape[0]
    nch = SGU_ROWS // SGU_CHUNK

    def body(z_ref, g_ref, b_ref, w_ref, bt_ref, dy_ref, dz_ref, dg_ref, dbb_ref, dw_ref, dbt_ref):
        @pl.when(pl.program_id(0) == 0)
        def _():
            dg_ref[...] = jnp.zeros_like(dg_ref)
            dbb_ref[...] = jnp.zeros_like(dbb_ref)
            dw_ref[...] = jnp.zeros_like(dw_ref)
            dbt_ref[...] = jnp.zeros_like(dbt_ref)

        z = z_ref[...]
        gz, dgelu = _gelu_pair(z)
        dy = dy_ref[...]
        for g in range(SGU_G):
            sl = slice(SGU_GW * g, SGU_GW * (g + 1))
            sv = slice(SGU_W + SGU_GW * g, SGU_W + SGU_GW * (g + 1))
            u_g = gz[:, sl]
            xhat, rstd = _sgu_norm(gz[:, sv])
            gain = g_ref[:, sl]
            vn = (xhat * gain + b_ref[:, sl]).astype(MXU_DTYPE)
            wg = w_ref[g].astype(MXU_DTYPE)
            dy_g = dy[:, sl]
            dvn_parts = []
            for ci in range(nch):
                rs = slice(SGU_CHUNK * ci, SGU_CHUNK * (ci + 1))
                mixed = _dot(wg, vn[rs], NN) + bt_ref[:, g:g + 1]
                dz_ref[rs, sl] = (dy_g[rs] * mixed * dgelu[rs, sl]).astype(dz_ref.dtype)
                dmixed = dy_g[rs] * u_g[rs]
                dm = dmixed.astype(MXU_DTYPE)
                dvn_parts.append(_dot(wg, dm, TN))
                dw_ref[g] += _dot(dm, vn[rs], NT)
                dbt_ref[:, g:g + 1] += jnp.sum(dmixed, axis=-1, keepdims=True)
            dvn = jnp.concatenate(dvn_parts, axis=0)
            dg_ref[:, sl] += jnp.sum(dvn * xhat, axis=0, keepdims=True)
            dbb_ref[:, sl] += jnp.sum(dvn, axis=0, keepdims=True)
            dxh = dvn * gain
            dv = rstd * (dxh - jnp.mean(dxh, axis=-1, keepdims=True)
                         - xhat * jnp.mean(dxh * xhat, axis=-1, keepdims=True))
            dz_ref[:, sv] = (dv * dgelu[:, sv]).astype(dz_ref.dtype)

    full = lambda shape: pl.BlockSpec(shape, lambda i: tuple(0 for _ in shape))
    return pl.pallas_call(
        body, name="sgu_bwd", grid=(s // SGU_ROWS,),
        in_specs=[pl.BlockSpec((SGU_ROWS, 2 * SGU_W), lambda i: (i, 0)), full((1, SGU_W)), full((1, SGU_W)),
                  full((SGU_G, SGU_CHUNK, SGU_CHUNK)), full((SGU_CHUNK, SGU_G)),
                  pl.BlockSpec((SGU_ROWS, SGU_W), lambda i: (i, 0))],
        out_specs=[pl.BlockSpec((SGU_ROWS, 2 * SGU_W), lambda i: (i, 0)), full((1, SGU_W)), full((1, SGU_W)),
                   full((SGU_G, SGU_CHUNK, SGU_CHUNK)), full((SGU_CHUNK, SGU_G))],
        out_shape=[jax.ShapeDtypeStruct((s, 2 * SGU_W), MXU_DTYPE), jax.ShapeDtypeStruct((1, SGU_W), F32),
                   jax.ShapeDtypeStruct((1, SGU_W), F32), jax.ShapeDtypeStruct((SGU_G, SGU_CHUNK, SGU_CHUNK), F32),
                   jax.ShapeDtypeStruct((SGU_CHUNK, SGU_G), F32)],
        compiler_params=_params(("arbitrary",)),
    )(zs, ln_g.reshape(1, SGU_W), ln_b.reshape(1, SGU_W), w_mask, b_t, dy)


def _ssm_discretize(a_re, a_im, log_dt, b_re, b_im):
    dt = jnp.exp(log_dt)[:, None]
    mag = jnp.exp(a_re * dt)
    ab_re = mag * jnp.cos(a_im * dt)
    ab_im = mag * jnp.sin(a_im * dt)
    den = a_re * a_re + a_im * a_im
    f_re = ((ab_re - 1.0) * a_re + ab_im * a_im) / den
    f_im = (ab_im * a_re - (ab_re - 1.0) * a_im) / den
    bb_re = f_re[:, :, None] * b_re - f_im[:, :, None] * b_im
    bb_im = f_re[:, :, None] * b_im + f_im[:, :, None] * b_re
    return ab_re, ab_im, bb_re, bb_im


def _ssm_operands(a_re, a_im, log_dt, b_re, b_im, c_re, c_im):
    ab_re, ab_im, bb_re, bb_im = _ssm_discretize(a_re, a_im, log_dt, b_re, b_im)
    eye = jnp.eye(SSM_G, dtype=F32)
    b_blk = jnp.einsum("pgnc,gh->gcphn", jnp.stack([bb_re, bb_im]), eye).reshape(SSM_W, 2 * NSTATE)
    c_mat = jnp.einsum("pgcn,gh->pgnhc", jnp.stack([c_re, -c_im]), eye).reshape(2 * NSTATE, SSM_W)
    a_row = jnp.stack([ab_re.reshape(NSTATE), ab_im.reshape(NSTATE)])
    p_re, p_im = a_row[0:1], a_row[1:2]
    while p_re.shape[0] < SSM_TSEG:
        l_re, l_im = p_re[-1:], p_im[-1:]
        p_re, p_im = (jnp.concatenate([p_re, p_re * l_re - p_im * l_im]),
                      jnp.concatenate([p_im, p_re * l_im + p_im * l_re]))
    p_tab = jnp.stack([p_re, p_im])
    return b_blk.astype(MXU_DTYPE), c_mat.astype(MXU_DTYPE), a_row, p_tab


def _lane_chunks():
    return [(lo, lo + SSM_LANE_CHUNK) for lo in range(0, NSTATE, SSM_LANE_CHUNK)]


def _seg_rows(j):
    return pl.ds(pl.multiple_of(j * SSM_NSEG, SSM_NSEG), SSM_NSEG)


def _to_segments(t):
    s, w = t.shape
    return t.reshape(s // SSM_TB, SSM_NSEG, SSM_TSEG, w).transpose(0, 2, 1, 3).reshape(s, w)


def _from_segments(t):
    s, w = t.shape
    return t.reshape(s // SSM_TB, SSM_TSEG, SSM_NSEG, w).transpose(0, 2, 1, 3).reshape(s, w)


def _ssm_local_scan(buf, a_ref, *, reverse):
    ends_re, ends_im = [], []
    for lo, hi in _lane_chunks():
        are = jnp.broadcast_to(a_ref[0:1, lo:hi], (SSM_NSEG, hi - lo))
        aim = jnp.broadcast_to(a_ref[1:2, lo:hi], (SSM_NSEG, hi - lo))
        if reverse:
            aim = -aim

        def step(jj, carry, lo=lo, hi=hi, are=are, aim=aim):
            xr, xi = carry
            j = (SSM_TSEG - 1 - jj) if reverse else jj
            tr = buf[_seg_rows(j), lo:hi]
            ti = buf[_seg_rows(j), NSTATE + lo:NSTATE + hi]
            nr = are * xr - aim * xi + tr
            ni = are * xi + aim * xr + ti
            buf[_seg_rows(j), lo:hi] = nr
            buf[_seg_rows(j), NSTATE + lo:NSTATE + hi] = ni
            return nr, ni

        zero = jnp.zeros((SSM_NSEG, hi - lo), F32)
        xr, xi = lax.fori_loop(0, SSM_TSEG, step, (zero, zero), unroll=4)
        ends_re.append(xr)
        ends_im.append(xi)
    return jnp.concatenate(ends_re, axis=1), jnp.concatenate(ends_im, axis=1)


def _ssm_entry_states(ends_re, ends_im, carry_ref, p_ref, entry_ref, *, reverse):
    at_re = p_ref[0, SSM_TSEG - 1:SSM_TSEG, :]
    at_im = p_ref[1, SSM_TSEG - 1:SSM_TSEG, :]
    if reverse:
        at_im = -at_im
    cur_re = carry_ref[0:1, 0:NSTATE]
    cur_im = carry_ref[0:1, NSTATE:2 * NSTATE]
    order = range(SSM_NSEG - 1, -1, -1) if reverse else range(SSM_NSEG)
    for i in order:
        entry_ref[0, i:i + 1, 0:NSTATE] = cur_re
        entry_ref[0, i:i + 1, NSTATE:2 * NSTATE] = cur_im
        nxt_re = ends_re[i:i + 1] + at_re * cur_re - at_im * cur_im
        nxt_im = ends_im[i:i + 1] + at_re * cur_im + at_im * cur_re
        cur_re, cur_im = nxt_re, nxt_im
    carry_ref[0:1, 0:NSTATE] = cur_re
    carry_ref[0:1, NSTATE:2 * NSTATE] = cur_im


def _ssm_fixup(buf, p_ref, entry_ref, *, reverse):
    for lo, hi in _lane_chunks():
        e_re = entry_ref[0, :, lo:hi]
        e_im = entry_ref[0, :, NSTATE + lo:NSTATE + hi]

        def step(j, carry, lo=lo, hi=hi, e_re=e_re, e_im=e_im):
            jp = (SSM_TSEG - 1 - j) if reverse else j
            pr = p_ref[0, pl.ds(jp, 1), lo:hi]
            pi = p_ref[1, pl.ds(jp, 1), lo:hi]
            if reverse:
                pi = -pi
            buf[_seg_rows(j), lo:hi] = buf[_seg_rows(j), lo:hi] + pr * e_re - pi * e_im
            buf[_seg_rows(j), NSTATE + lo:NSTATE + hi] = (buf[_seg_rows(j), NSTATE + lo:NSTATE + hi]
                                                           + pr * e_im + pi * e_re)
            return carry

        lax.fori_loop(0, SSM_TSEG, step, 0, unroll=4)


def _ssm_fwd(u, ops, d_skip, glu_w, glu_b):
    b_blk, c_mat, a_row, p_tab = ops
    s = u.shape[0]
    nblk = s // SSM_TB

    def body(u_ref, bb_ref, cm_ref, a_ref, p_ref, d_ref, gw_ref, gb_ref, y_ref, entry_ref, xbuf, carry):
        @pl.when(pl.program_id(0) == 0)
        def _():
            carry[...] = jnp.zeros_like(carry)

        uu = u_ref[...]
        xbuf[...] = _dotf(uu, bb_ref[...], NN)
        ends_re, ends_im = _ssm_local_scan(xbuf, a_ref, reverse=False)
        _ssm_entry_states(ends_re, ends_im, carry, p_ref, entry_ref, reverse=False)
        _ssm_fixup(xbuf, p_ref, entry_ref, reverse=False)
        y = _dotf(xbuf[...],cm_ref[...], NN) + d_ref[...] * uu
        y2 = _gelu(y)
        gate = jax.nn.sigmoid(_dot(y2.astype(MXU_DTYPE), gw_ref[...].astype(MXU_DTYPE), NN) + gb_ref[...])
        y_ref[...] = y2 * gate

    full = lambda shape: pl.BlockSpec(shape, lambda i: tuple(0 for _ in shape))
    y_seg, entry = pl.pallas_call(
        body, name="ssm_fwd", grid=(nblk,),
        in_specs=[pl.BlockSpec((SSM_TB, SSM_W), lambda i: (i, 0)), full(b_blk.shape), full(c_mat.shape),
                  full(a_row.shape), full(p_tab.shape), full((1, SSM_W)), full((SSM_W, SSM_W)), full((1, SSM_W))],
        out_specs=[pl.BlockSpec((SSM_TB, SSM_W), lambda i: (i, 0)),
                   pl.BlockSpec((1, SSM_NSEG, 2 * NSTATE), lambda i: (i, 0, 0))],
        out_shape=[jax.ShapeDtypeStruct((s, SSM_W), F32), jax.ShapeDtypeStruct((nblk, SSM_NSEG, 2 * NSTATE), F32)],
        scratch_shapes=[pltpu.VMEM((SSM_TB, 2 * NSTATE), F32), pltpu.VMEM((SSM_NSEG, 2 * NSTATE), F32)],
        compiler_params=_params(("arbitrary",)),
    )(_to_segments(u), b_blk, c_mat, a_row, p_tab, d_skip.reshape(1, SSM_W), glu_w, glu_b.reshape(1, SSM_W))
    return _from_segments(y_seg), entry


def _ssm_bwd(u, entry, ops, d_skip, glu_w, glu_b, dout):
    b_blk, c_mat, a_row, p_tab = ops
    s = u.shape[0]
    nblk = s // SSM_TB

    def body(u_ref, en_ref, bb_ref, cm_ref, a_ref, p_ref, d_ref, gw_ref, gb_ref, do_ref,
             du_ref, dbb_ref, dcm_ref, da_ref, dd_ref, dgw_ref, dgb_ref, xbuf, gbuf, gcarry, gentry):
        @pl.when(pl.program_id(0) == 0)
        def _():
            gcarry[...] = jnp.zeros_like(gcarry)
            for r in (dbb_ref, dcm_ref, da_ref, dd_ref, dgw_ref, dgb_ref):
                r[...] = jnp.zeros_like(r)

        uu = u_ref[...]
        xbuf[...] = _dotf(uu, bb_ref[...], NN)
        _ssm_local_scan(xbuf, a_ref, reverse=False)
        _ssm_fixup(xbuf, p_ref, en_ref, reverse=False)
        y = _dotf(xbuf[...],cm_ref[...], NN) + d_ref[...] * uu
        y2, dgelu = _gelu_pair(y)
        y2m = y2.astype(MXU_DTYPE)
        gwm = gw_ref[...].astype(MXU_DTYPE)
        gate = jax.nn.sigmoid(_dot(y2m, gwm, NN) + gb_ref[...])
        dout = do_ref[...]
        dpre = dout * y2 * gate * (1.0 - gate)
        dprem = dpre.astype(MXU_DTYPE)
        dy2 = dout * gate + _dot(dprem, gwm, NT)
        dgw_ref[...] += _dot(y2m, dprem, TN)
        dgb_ref[...] += jnp.sum(dpre, axis=0, keepdims=True)
        dy = dy2 * dgelu
        dd_ref[...] += jnp.sum(dy * uu, axis=0, keepdims=True)
        dcm_ref[...] += _dotf(xbuf[...],dy, TN)
        gbuf[...] = _dotf(dy, cm_ref[...], NT)
        gs_re, gs_im = _ssm_local_scan(gbuf, a_ref, reverse=True)
        _ssm_entry_states(gs_re, gs_im, gcarry, p_ref, gentry, reverse=True)
        _ssm_fixup(gbuf, p_ref, gentry, reverse=True)
        du_ref[...] = (_dotf(gbuf[...], bb_ref[...], NT) + d_ref[...] * dy).astype(du_ref.dtype)
        dbb_ref[...] += _dotf(uu, gbuf[...], TN)
        for lo, hi in _lane_chunks():
            def step(j, carry, lo=lo, hi=hi):
                acc_re, acc_im = carry
                g_re = gbuf[_seg_rows(j), lo:hi]
                g_im = gbuf[_seg_rows(j), NSTATE + lo:NSTATE + hi]
                x_re = xbuf[_seg_rows(j - 1), lo:hi]
                x_im = xbuf[_seg_rows(j - 1), NSTATE + lo:NSTATE + hi]
                return acc_re + g_re * x_re + g_im * x_im, acc_im + g_im * x_re - g_re * x_im

            g0_re = gbuf[_seg_rows(0), lo:hi]
            g0_im = gbuf[_seg_rows(0), NSTATE + lo:NSTATE + hi]
            e_re = en_ref[0, :, lo:hi]
            e_im = en_ref[0, :, NSTATE + lo:NSTATE + hi]
            init = (g0_re * e_re + g0_im * e_im, g0_im * e_re - g0_re * e_im)
            acc_re, acc_im = lax.fori_loop(1, SSM_TSEG, step, init, unroll=4)
            da_ref[0:1, lo:hi] += jnp.sum(acc_re, axis=0, keepdims=True)
            da_ref[1:2, lo:hi] += jnp.sum(acc_im, axis=0, keepdims=True)

    full = lambda shape: pl.BlockSpec(shape, lambda i: tuple(0 for _ in shape))
    rev = pl.BlockSpec((SSM_TB, SSM_W), lambda i: (nblk - 1 - i, 0))
    outs = pl.pallas_call(
        body, name="ssm_bwd", grid=(nblk,),
        in_specs=[rev, pl.BlockSpec((1, SSM_NSEG, 2 * NSTATE), lambda i: (nblk - 1 - i, 0, 0)),
                  full(b_blk.shape), full(c_mat.shape), full(a_row.shape), full(p_tab.shape),
                  full((1, SSM_W)), full((SSM_W, SSM_W)), full((1, SSM_W)), rev],
        out_specs=[rev, full(b_blk.shape), full(c_mat.shape), full(a_row.shape), full((1, SSM_W)),
                   full((SSM_W, SSM_W)), full((1, SSM_W))],
        out_shape=[jax.ShapeDtypeStruct((s, SSM_W), MXU_DTYPE), jax.ShapeDtypeStruct(b_blk.shape, F32),
                   jax.ShapeDtypeStruct(c_mat.shape, F32), jax.ShapeDtypeStruct(a_row.shape, F32),
                   jax.ShapeDtypeStruct((1, SSM_W), F32), jax.ShapeDtypeStruct((SSM_W, SSM_W), F32),
                   jax.ShapeDtypeStruct((1, SSM_W), F32)],
        scratch_shapes=[pltpu.VMEM((SSM_TB, 2 * NSTATE), F32), pltpu.VMEM((SSM_TB, 2 * NSTATE), F32),
                        pltpu.VMEM((SSM_NSEG, 2 * NSTATE), F32), pltpu.VMEM((1, SSM_NSEG, 2 * NSTATE), F32)],
        compiler_params=_params(("arbitrary",)),
    )(_to_segments(u), entry, b_blk, c_mat, a_row, p_tab, d_skip.reshape(1, SSM_W), glu_w, glu_b.reshape(1, SSM_W),
      _to_segments(dout))
    return (_from_segments(outs[0]),) + tuple(outs[1:])


MIX_SEGS = ((0, ATTN_W), (ATTN_W, ATTN_W + SGU_W), (ATTN_W + SGU_W, D_MODEL))


def _chunks_to_rows(a_ref):
    return jnp.concatenate([a_ref[ch] for ch in range(N_CHUNK)], axis=1)


def _mix_fwd(y_attn_c, y_sgu, y_ssm, gain):
    s = y_sgu.shape[0]

    def body(a_ref, b_ref, c_ref, g_ref, o_ref):
        for x, (lo, hi) in zip((_chunks_to_rows(a_ref), b_ref[...], c_ref[...]), MIX_SEGS):
            r = lax.rsqrt(jnp.mean(x * x, axis=-1, keepdims=True) + EPS)
            o_ref[:, lo:hi] = (x * r * g_ref[:, lo:hi]).astype(o_ref.dtype)

    row = lambda w: pl.BlockSpec((ROWS, w), lambda i: (i, 0))
    return pl.pallas_call(
        body, name="mix_fwd", grid=(s // ROWS,),
        in_specs=[pl.BlockSpec((N_CHUNK, ROWS, 128), lambda i: (0, i, 0)), row(SGU_W), row(SSM_W),
                  pl.BlockSpec((1, D_MODEL), lambda i: (0, 0))],
        out_specs=row(D_MODEL), out_shape=jax.ShapeDtypeStruct((s, D_MODEL), MXU_DTYPE),
        compiler_params=_params(("parallel",)),
    )(y_attn_c, y_sgu, y_ssm, gain.reshape(1, D_MODEL))


def _mix_bwd(y_attn_c, y_sgu, y_ssm, gain, dmix):
    s = y_sgu.shape[0]

    def body(a_ref, b_ref, c_ref, g_ref, dm_ref, da_ref, dl_ref, db_ref, dc_ref, dg_ref):
        @pl.when(pl.program_id(0) == 0)
        def _():
            dg_ref[...] = jnp.zeros_like(dg_ref)

        grads = []
        for x, (lo, hi) in zip((_chunks_to_rows(a_ref), b_ref[...], c_ref[...]), MIX_SEGS):
            r = lax.rsqrt(jnp.mean(x * x, axis=-1, keepdims=True) + EPS)
            xhat = x * r
            dm = dm_ref[:, lo:hi].astype(F32)
            dg_ref[:, lo:hi] += jnp.sum(dm * xhat, axis=0, keepdims=True)
            dxh = dm * g_ref[:, lo:hi]
            grads.append(r * (dxh - xhat * jnp.mean(dxh * xhat, axis=-1, keepdims=True)))
        db_ref[...] = grads[1]
        dc_ref[...] = grads[2]
        low = lax.broadcasted_iota(jnp.int32, (ROWS, 128), 1) < HEAD_DIM
        for ch in range(N_CHUNK):
            d_c = grads[0][:, 128 * ch:128 * (ch + 1)]
            da_ref[ch] = d_c.astype(da_ref.dtype)
            prod = d_c * a_ref[ch]
            dl_ref[ch] = jnp.where(low, jnp.sum(prod[:, :HEAD_DIM], axis=-1, keepdims=True),
                                   jnp.sum(prod[:, HEAD_DIM:], axis=-1, keepdims=True))

    row = lambda w: pl.BlockSpec((ROWS, w), lambda i: (i, 0))
    vec = pl.BlockSpec((1, D_MODEL), lambda i: (0, 0))
    chunked = pl.BlockSpec((N_CHUNK, ROWS, 128), lambda i: (0, i, 0))
    return pl.pallas_call(
        body, name="mix_bwd", grid=(s // ROWS,),
        in_specs=[chunked, row(SGU_W), row(SSM_W), vec, row(D_MODEL)],
        out_specs=[chunked, chunked, row(SGU_W), row(SSM_W), vec],
        out_shape=[jax.ShapeDtypeStruct((N_CHUNK, s, 128), ATTN_IO_DTYPE), jax.ShapeDtypeStruct((N_CHUNK, s, 128), F32),
                   jax.ShapeDtypeStruct((s, SGU_W), F32), jax.ShapeDtypeStruct((s, SSM_W), F32),
                   jax.ShapeDtypeStruct((1, D_MODEL), F32)],
        compiler_params=_params(("arbitrary",)),
    )(y_attn_c, y_sgu, y_ssm, gain.reshape(1, D_MODEL), dmix)


CONV_ROWS = 256
CONV_COLS = 1408
CONV_PAIR = 2 * CONV_COLS
HALO = 16


def _interleave_ff(t):
    lead = t.shape[:-1]
    nb = D_FF // CONV_COLS
    return jnp.swapaxes(t.reshape(lead + (2, nb, CONV_COLS)), -3, -2).reshape(lead + (2 * D_FF,))


def _deinterleave_ff(t):
    lead = t.shape[:-1]
    nb = D_FF // CONV_COLS
    return jnp.swapaxes(t.reshape(lead + (nb, 2, CONV_COLS)), -3, -2).reshape(lead + (2 * D_FF,))


def _causal_taps(x, halo, first):
    patch = 8
    row = lax.broadcasted_iota(jnp.int32, (patch, x.shape[1]), 0)
    h1 = jnp.where(first, 0.0, halo[HALO - 1:HALO, :])
    h2 = jnp.where(first, 0.0, halo[HALO - 2:HALO - 1, :])
    r1 = pltpu.roll(x, 1, 0)
    r2 = pltpu.roll(x, 2, 0)
    top1 = jnp.where(row == 0, h1, r1[0:patch])
    top2 = jnp.where(row == 0, h2, jnp.where(row == 1, h1, r2[0:patch]))
    return jnp.concatenate([top1, r1[patch:]], axis=0), jnp.concatenate([top2, r2[patch:]], axis=0)


def _conv_in_specs():
    halo_idx = lambda i: jnp.maximum(i * (CONV_ROWS // HALO) - 1, 0)
    return [pl.BlockSpec((CONV_ROWS, CONV_PAIR), lambda j, i: (i, j)),
            pl.BlockSpec((HALO, CONV_PAIR), lambda j, i: (halo_idx(i), j)),
            pl.BlockSpec((3, CONV_PAIR), lambda j, i: (0, j)),
            pl.BlockSpec((1, CONV_PAIR), lambda j, i: (0, j))]


def _ffn_act_fwd(hh, conv_w, conv_b):
    s = hh.shape[0]

    def body(m_ref, h_ref, w_ref, b_ref, o_ref):
        first = pl.program_id(1) == 0
        main = m_ref[...].astype(F32)
        x1, x2 = _causal_taps(main, h_ref[...].astype(F32), first)
        conv = w_ref[0:1, :] * x2 + w_ref[1:2, :] * x1 + w_ref[2:3, :] * main + b_ref[...]
        o_ref[...] = (_gelu(conv[:, CONV_COLS:]) * conv[:, :CONV_COLS]).astype(o_ref.dtype)

    return pl.pallas_call(
        body, name="ffn_act_fwd", grid=(D_FF // CONV_COLS, s // CONV_ROWS), in_specs=_conv_in_specs(),
        out_specs=pl.BlockSpec((CONV_ROWS, CONV_COLS), lambda j, i: (i, j)),
        out_shape=jax.ShapeDtypeStruct((s, D_FF), MXU_DTYPE),
        compiler_params=_params(("parallel", "parallel")),
    )(hh, hh, conv_w, conv_b.reshape(1, -1))


def _ffn_act_bwd(hh, conv_w, conv_b, da):
    s = hh.shape[0]
    nrow = s // CONV_ROWS
    ext_rows = CONV_ROWS + HALO

    def body(m_ref, h_ref, w_ref, b_ref, nx_ref, da_ref, dan_ref, o_ref, dw_ref, db_ref):
        first = pl.program_id(1) == 0
        last = pl.program_id(1) == nrow - 1

        @pl.when(first)
        def _():
            dw_ref[...] = jnp.zeros_like(dw_ref)
            db_ref[...] = jnp.zeros_like(db_ref)

        ext = jnp.concatenate([m_ref[...].astype(F32), nx_ref[...].astype(F32)], axis=0)
        x1, x2 = _causal_taps(ext, h_ref[...].astype(F32), first)
        conv = w_ref[0:1, :] * x2 + w_ref[1:2, :] * x1 + w_ref[2:3, :] * ext + b_ref[...]
        da = jnp.concatenate([da_ref[...].astype(F32), jnp.where(last, 0.0, dan_ref[...].astype(F32))], axis=0)
        act, dact = _gelu_pair(conv[:, CONV_COLS:])
        dconv = jnp.concatenate([da * act, da * conv[:, :CONV_COLS] * dact], axis=1)
        dmain = dconv[:CONV_ROWS]
        ahead1 = pltpu.roll(dconv, ext_rows - 1, 0)[:CONV_ROWS]
        ahead2 = pltpu.roll(dconv, ext_rows - 2, 0)[:CONV_ROWS]
        o_ref[...] = (w_ref[2:3, :] * dmain + w_ref[1:2, :] * ahead1 + w_ref[0:1, :] * ahead2).astype(o_ref.dtype)
        for t, tap in enumerate((x2, x1, ext)):
            dw_ref[t:t + 1, :] += jnp.sum(dmain * tap[:CONV_ROWS], axis=0, keepdims=True)
        db_ref[...] += jnp.sum(dmain, axis=0, keepdims=True)

    nxt = lambda i: jnp.minimum((i + 1) * (CONV_ROWS // HALO), s // HALO - 1)
    return pl.pallas_call(
        body, name="ffn_act_bwd", grid=(D_FF // CONV_COLS, nrow),
        in_specs=_conv_in_specs() + [pl.BlockSpec((HALO, CONV_PAIR), lambda j, i: (nxt(i), j)),
                                     pl.BlockSpec((CONV_ROWS, CONV_COLS), lambda j, i: (i, j)),
                                     pl.BlockSpec((HALO, CONV_COLS), lambda j, i: (nxt(i), j))],
        out_specs=[pl.BlockSpec((CONV_ROWS, CONV_PAIR), lambda j, i: (i, j)),
                   pl.BlockSpec((3, CONV_PAIR), lambda j, i: (0, j)), pl.BlockSpec((1, CONV_PAIR), lambda j, i: (0, j))],
        out_shape=[jax.ShapeDtypeStruct((s, 2 * D_FF), MXU_DTYPE), jax.ShapeDtypeStruct((3, 2 * D_FF), F32),
                   jax.ShapeDtypeStruct((1, 2 * D_FF), F32)],
        compiler_params=_params(("parallel", "arbitrary")),
    )(hh, hh, conv_w, conv_b.reshape(1, -1), hh, da, da)


def _ple_weight_specs(layer):
    return [pl.BlockSpec((None, D_MODEL, D_MODEL), lambda i: (layer, 0, 0)),
            pl.BlockSpec((None, PLE_DIM, D_MODEL), lambda i: (layer, 0, 0))]


def _ple_fwd(h, gain, p, w_gate, w_proj, layer):
    s = h.shape[0]
    tm = 512

    def body(h_ref, g_ref, p_ref, wg_ref, wp_ref, o_ref, xn_ref):
        x = h_ref[...]
        xn = _rms_rows(x, g_ref[...])
        xn_ref[...] = xn
        gate = jax.nn.sigmoid(_dot(xn, wg_ref[...].astype(MXU_DTYPE), NN))
        proj = _dot(p_ref[...].astype(MXU_DTYPE), wp_ref[...].astype(MXU_DTYPE), NN)
        o_ref[...] = x + gate * proj

    row = pl.BlockSpec((tm, D_MODEL), lambda i: (i, 0))
    return pl.pallas_call(
        body, name="ple_fwd", grid=(s // tm,),
        in_specs=[row, pl.BlockSpec((1, D_MODEL), lambda i: (0, 0)), pl.BlockSpec((tm, PLE_DIM), lambda i: (i, 0))]
        + _ple_weight_specs(layer),
        out_specs=[row, row],
        out_shape=[jax.ShapeDtypeStruct((s, D_MODEL), F32), jax.ShapeDtypeStruct((s, D_MODEL), MXU_DTYPE)],
        compiler_params=_params(("parallel",)),
    )(h, gain.reshape(1, D_MODEL), p, w_gate, w_proj)


def _ple_bwd(xn, p, w_gate, w_proj, dh, layer):
    s = xn.shape[0]
    tm = 512

    def body(x_ref, p_ref, wg_ref, wp_ref, dh_ref, dpre_ref, dproj_ref):
        gate = jax.nn.sigmoid(_dot(x_ref[...].astype(MXU_DTYPE), wg_ref[...].astype(MXU_DTYPE), NN))
        proj = _dot(p_ref[...].astype(MXU_DTYPE), wp_ref[...].astype(MXU_DTYPE), NN)
        dh = dh_ref[...]
        dpre_ref[...] = (dh * proj * gate * (1.0 - gate)).astype(dpre_ref.dtype)
        dproj_ref[...] = (dh * gate).astype(dproj_ref.dtype)

    row = pl.BlockSpec((tm, D_MODEL), lambda i: (i, 0))
    return pl.pallas_call(
        body, name="ple_bwd", grid=(s // tm,),
        in_specs=[row, pl.BlockSpec((tm, PLE_DIM), lambda i: (i, 0))] + _ple_weight_specs(layer) + [row],
        out_specs=[row, row],
        out_shape=[jax.ShapeDtypeStruct((s, D_MODEL), MXU_DTYPE)] * 2,
        compiler_params=_params(("parallel",)),
    )(xn, p, w_gate, w_proj, dh)


O_SGU = 3 * ATTN_W
O_SSM = O_SGU + 2 * SGU_W


def _layer_consts(w, i):
    causal = jnp.asarray(np.tril(np.ones((SGU_CHUNK, SGU_CHUNK), np.float32)))
    return {
        "sgu_w_mask": w["sgu_w"][i] * causal,
        "sgu_b_t": w["sgu_b"][i].T,
        "ssm_ops": _ssm_operands(w["ssm_a_re"][i], w["ssm_a_im"][i], w["ssm_log_dt"][i], w["ssm_b_re"][i],
                                 w["ssm_b_im"][i], w["ssm_c_re"][i], w["ssm_c_im"][i]),
    }


def _layer_fwd(h0, p_i, w, i, bias):
    c = _layer_consts(w, i)
    xn1, qkv, zs, us = _in_proj(h0, w["norm_attn_g"][i], w["w_in"], i)
    y_attn, lse = _attn2_fwd(qkv, bias)
    y_sgu = _sgu_fwd(zs, w["sgu_ln_g"][i], w["sgu_ln_b"][i], c["sgu_w_mask"], c["sgu_b_t"])
    y_ssm, entry = _ssm_fwd(us, c["ssm_ops"], w["ssm_d"][i], w["ssm_glu_w"][i], w["ssm_glu_b"][i])
    mix = _mix_fwd(y_attn, y_sgu, y_ssm, w["branch_norm_g"][i])
    h1 = _matmul(mix, w["w_out"], name="out_proj", out_dtype=F32, tm=512, tn=1024, residual=h0, layer=i)
    xn2, hh = _ffn_up(h1, w["norm_ffn_g"][i], w["ffn_w_up"], i)
    act = _ffn_act_fwd(hh, w["ffn_conv_w"][i], w["ffn_conv_b"][i])
    h2 = _matmul(act, w["ffn_w_down"], name="ffn_down", out_dtype=F32, tm=512, tn=1024, residual=h1, layer=i)
    h3, xn3 = _ple_fwd(h2, w["norm_ple_g"][i], p_i, w["ple_w_gate"], w["ple_w_proj"], i)
    saved = dict(h0=h0, xn1=xn1, qkv=qkv, zs=zs, us=us, y_attn=y_attn, lse=lse, y_sgu=y_sgu, y_ssm=y_ssm,
                 entry=entry, mix=mix, h1=h1, xn2=xn2, hh=hh, act=act, h2=h2, xn3=xn3, consts=c)
    return h3, saved


def _layer_bwd(dh3, sv, p_i, w, i, bias):
    c = sv["consts"]
    g = {}
    dpre, dproj = _ple_bwd(sv["xn3"], p_i, w["ple_w_gate"], w["ple_w_proj"], dh3, i)
    g["ple_w_gate"] = _matmul_tn(sv["xn3"], dpre, name="d_ple_w_gate", tk=1024, tn=1024)
    g["ple_w_proj"] = _matmul_tn(p_i, dproj, name="d_ple_w_proj", tk=256, tn=1024)
    dh2, g["norm_ple_g"] = _matmul_rms_bwd(dpre, w["ple_w_gate"], sv["h2"], w["norm_ple_g"][i], dh3,
                                           name="d_xn_ple", layer=i, tm=512)
    g["ffn_w_down"] = _matmul_tn(sv["act"], dh2, name="d_ffn_w_down", tk=1408, tn=1024)
    dact = _matmul(dh2, w["ffn_w_down"], name="d_ffn_act", out_dtype=MXU_DTYPE, tm=512, tn=1408, trans_b=True, layer=i)
    dhh, g["ffn_conv_w"], g["ffn_conv_b"] = _ffn_act_bwd(sv["hh"], w["ffn_conv_w"][i], w["ffn_conv_b"][i], dact)
    g["ffn_w_up"] = _matmul_tn(sv["xn2"], dhh, name="d_ffn_w_up", tk=1024, tn=1408)
    dh1, g["norm_ffn_g"] = _matmul_rms_bwd(dhh, w["ffn_w_up"], sv["h1"], w["norm_ffn_g"][i], dh2,
                                           name="d_xn_ffn", layer=i, tm=256)
    g["w_out"] = _matmul_tn(sv["mix"], dh1, name="d_w_out", tk=1024, tn=1024)
    dmix = _matmul(dh1, w["w_out"], name="d_mix", out_dtype=F32, tm=512, tn=1024, trans_b=True, layer=i)
    dy_attn, delta, dy_sgu, dy_ssm, g["branch_norm_g"] = _mix_bwd(sv["y_attn"], sv["y_sgu"], sv["y_ssm"],
                                                                  w["branch_norm_g"][i], dmix)
    dq, dk, dv, ek, ev, dbias = _attn2_bwd(sv["qkv"], bias, sv["lse"], delta, dy_attn)
    dzs, g["sgu_ln_g"], g["sgu_ln_b"], dsw, dsb = _sgu_bwd(sv["zs"], w["sgu_ln_g"][i], w["sgu_ln_b"][i],
                                                          c["sgu_w_mask"], c["sgu_b_t"], dy_sgu)
    causal = jnp.asarray(np.tril(np.ones((SGU_CHUNK, SGU_CHUNK), np.float32)))
    g["sgu_w"] = dsw * causal
    g["sgu_b"] = dsb.T
    dus, dbb, dcm, da, g["ssm_d"], g["ssm_glu_w"], g["ssm_glu_b"] = _ssm_bwd(
        sv["us"], sv["entry"], c["ssm_ops"], w["ssm_d"][i], w["ssm_glu_w"][i], w["ssm_glu_b"][i], dy_ssm)
    dbb5 = dbb.reshape(SSM_G, SSM_C, 2, SSM_G, SSM_N)
    dbbar = jnp.einsum("gcpgn->pgnc", dbb5)
    dcm5 = dcm.reshape(2, SSM_G, SSM_N, SSM_G, SSM_C)
    dcc = jnp.einsum("pgngc->pgcn", dcm5)
    g["ssm_c_re"] = dcc[0]
    g["ssm_c_im"] = -dcc[1]
    da2 = da.reshape(2, SSM_G, SSM_N)
    _, vjp = jax.vjp(_ssm_discretize, w["ssm_a_re"][i], w["ssm_a_im"][i], w["ssm_log_dt"][i],
                     w["ssm_b_re"][i], w["ssm_b_im"][i])
    (g["ssm_a_re"], g["ssm_a_im"], g["ssm_log_dt"], g["ssm_b_re"], g["ssm_b_im"]) = vjp(
        (da2[0], da2[1], dbbar[0], dbbar[1]))
    dz = _attn2_bwd_sum(dq, dk, dv, ek, ev, dzs, dus)
    g["w_in"] = _matmul_tn(sv["xn1"], dz, name="d_w_in", tk=1024, tn=1152)
    dh0, g["norm_attn_g"] = _matmul_rms_bwd(dz, w["w_in"], sv["h0"], w["norm_attn_g"][i], dh1,
                                            name="d_xn_attn", layer=i, tm=512)
    for k in ("norm_ple_g", "norm_ffn_g", "branch_norm_g", "norm_attn_g", "sgu_ln_g", "sgu_ln_b", "ssm_d",
              "ssm_glu_b", "ffn_conv_b"):
        g[k] = g[k].reshape(-1)
    return dh0, g, dbias


def _local_step(x, p, target, w, ff_interleaved=False):
    ff_names = ("ffn_conv_b",) if ff_interleaved else FF_SHARDED + ("ffn_conv_b",)
    w = dict(w)
    for k in ff_names:
        w[k] = _interleave_ff(w[k])
    bias = _bias_build(w["rel_bias"])
    h = x
    saved = []
    for i in range(DEPTH):
        h, sv = _layer_fwd(h, p[i], w, i, bias)
        saved.append(sv)
    loss, dh, dgf = _loss_head(h, w["final_norm_g"], target)
    layer_grads = [None] * DEPTH
    dbias = None
    for i in reversed(range(DEPTH)):
        dh, layer_grads[i], db = _layer_bwd(dh, saved[i], p[i], w, i, bias)
        dbias = db if dbias is None else dbias + db
    grads = {k: jnp.stack([layer_grads[i][k] for i in range(DEPTH)]) for k in layer_grads[0]}
    for k in ff_names:
        grads[k] = _deinterleave_ff(grads[k])
    grads["rel_bias"] = _bias_reduce(dbias)
    grads["final_norm_g"] = dgf.reshape(-1)
    return loss, dh, grads


def _pad_rows(a2, mult=16):
    r = (-a2.shape[0]) % mult
    return a2 if r == 0 else jnp.concatenate([a2, jnp.zeros((r, a2.shape[1]), a2.dtype)], axis=0)


def _as_rows(a, rows=None):
    size = int(np.prod(a.shape))
    if rows is None:
        rows = -(-size // (16 * PACK_COLS)) * 16
    if size % PACK_COLS:
        a = jnp.pad(a.reshape(-1), (0, (-size) % PACK_COLS))
    a2 = a.reshape(-1, PACK_COLS)
    return jnp.pad(a2, ((0, rows - a2.shape[0]), (0, 0)))


def _shard_shape(name):
    full, ax = BIG_FULL[name]
    shp = [DEPTH] + list(full)
    shp[ax] //= N_CHIPS
    return tuple(shp)


EXACT_NAMES = ("ffn_conv_w",)


def _pack_rows_of(name):
    n = int(np.prod(_shard_shape(name))) * (2 if name in EXACT_NAMES else 1)
    rows = -(-n // PACK_COLS)
    return -(-rows // 16) * 16


def _pack_shards(shards, dtype, exact=False):
    split_words = exact and jnp.dtype(dtype).itemsize == 2
    parts = []
    for n in BIG_NAMES:
        a = shards[n]
        if split_words and n in EXACT_NAMES:
            a = lax.bitcast_convert_type(a.astype(F32), dtype)
        parts.append(_as_rows(a.astype(dtype), _pack_rows_of(n)))
    used = sum(pt.shape[0] for pt in parts)
    parts.append(jnp.zeros((PACK_ROWS - used, PACK_COLS), dtype))
    return jnp.concatenate(parts, axis=0)


def _unpack_shard(flat, name, exact=False):
    off = 0
    for n in BIG_NAMES:
        if n == name:
            break
        off += _pack_rows_of(n)
    shp = _shard_shape(name)
    cnt = int(np.prod(shp))
    if exact and name in EXACT_NAMES and jnp.dtype(flat.dtype).itemsize == 2:
        vec = flat[off:off + _pack_rows_of(name)].reshape(-1)
        return lax.bitcast_convert_type(vec[:2 * cnt].reshape(shp + (2,)), F32)
    if cnt % PACK_COLS == 0:
        return flat[off:off + cnt // PACK_COLS].reshape(shp)
    return flat[off:off + _pack_rows_of(name)].reshape(-1)[:cnt].reshape(shp)


FF_SHARDED = ("ffn_w_up", "ffn_conv_w")
FF_CHIP_ORDER = (0, 2, 1, 3)


def _chip_order(name):
    return FF_CHIP_ORDER if name in FF_SHARDED else tuple(range(N_CHIPS))


def _split_full(full, name):
    _, ax = BIG_FULL[name]
    parts = jnp.split(full, N_CHIPS, axis=ax)
    out = [None] * N_CHIPS
    for j, k in enumerate(_chip_order(name)):
        out[k] = parts[j]
    return out


def _join_shards(shards, name):
    _, ax = BIG_FULL[name]
    return jnp.concatenate([shards[k] for k in _chip_order(name)], axis=ax)


def _small_shapes(w):
    return [(n, w[n].shape) for n in SMALL_NAMES]


def _small_rows(shp):
    return -(-int(np.prod(shp)) // (8 * PACK_COLS)) * 8


def _pack_small(d):
    parts = [_as_rows(d[n].astype(F32), _small_rows(d[n].shape)) for n in SMALL_NAMES]
    used = sum(pt.shape[0] for pt in parts)
    parts.append(jnp.zeros((SMALL_ROWS - used, PACK_COLS), F32))
    return jnp.concatenate(parts, axis=0)


def _unpack_small(flat, shapes):
    out, off = {}, 0
    for n, shp in shapes:
        cnt = int(np.prod(shp))
        rows = _small_rows(shp)
        if cnt % PACK_COLS == 0:
            out[n] = flat[off:off + cnt // PACK_COLS].reshape(shp)
        else:
            out[n] = flat[off:off + rows].reshape(-1)[:cnt].reshape(shp)
        off += rows
    return out


MESH = pl.DeviceIdType.MESH
ANY = pl.BlockSpec(memory_space=pl.ANY)


def _me():
    return lax.axis_index("x"), lax.axis_index("y"), lax.axis_index("c")


def _other_chips(x, y):
    return [(1 - x, y), (x, 1 - y), (1 - x, 1 - y)]


def _gather_weights(wflat):
    def body(w_ref, out_ref, send_sems, recv_sems):
        x, y, c = _me()
        sibling = (x, y, 1 - c)
        chips = _other_chips(x, y)

        def rows(chip, half):
            return out_ref.at[2 * chip[0] + chip[1], pl.ds(half * PACK_HALF, PACK_HALF), :]

        def copy(k, chip, half, to, src=None):
            return pltpu.make_async_remote_copy(
                src_ref=rows(chip, half) if src is None else src, dst_ref=rows(chip, half),
                send_sem=send_sems.at[k], recv_sem=recv_sems.at[k], device_id=to, device_id_type=MESH)

        my_half = w_ref.at[pl.ds(c * PACK_HALF, PACK_HALF), :]
        first = [copy(j, (x, y), c, (*chip, c), src=my_half) for j, chip in enumerate(chips)]
        for cp in first:
            cp.start()
        passed = [copy(3 + j, chip, c, sibling) for j, chip in enumerate(chips)]
        for j, chip in enumerate(chips):
            copy(j, chip, c, (x, y, c)).wait_recv()
            passed[j].start()
        for j, chip in enumerate(chips):
            copy(3 + j, chip, 1 - c, (x, y, c)).wait_recv()
        for cp in first + passed:
            cp.wait_send()

    return pl.pallas_call(
        body, name="gather_weights", in_specs=[ANY], out_specs=ANY,
        out_shape=jax.ShapeDtypeStruct((N_CHIPS, PACK_ROWS, PACK_COLS), wflat.dtype),
        scratch_shapes=[pltpu.SemaphoreType.DMA((6,)), pltpu.SemaphoreType.DMA((6,))],
    )(wflat)


def _fill_own_shard(wall, wflat, chip_idx):
    rows = PACK_ROWS // 8

    def body(idx_ref, w_ref, wall_ref, o_ref):
        del idx_ref, wall_ref
        o_ref[...] = w_ref[...]

    return pl.pallas_call(
        body, name="fill_own_shard",
        grid_spec=pltpu.PrefetchScalarGridSpec(
            num_scalar_prefetch=1, grid=(PACK_ROWS // rows,),
            in_specs=[pl.BlockSpec((rows, PACK_COLS), lambda i, idx: (i, 0)), ANY],
            out_specs=pl.BlockSpec((None, rows, PACK_COLS), lambda i, idx: (idx[0], i, 0))),
        out_shape=jax.ShapeDtypeStruct(wall.shape, wall.dtype),
        input_output_aliases={2: 0},
        compiler_params=_params(("parallel",)),
    )(chip_idx, wflat, wall)


def _exchange_partials(gb, gs):
    def body(gb_ref, gs_ref, half_ref, small_ref, send_sems, recv_sems, local_sem):
        x, y, c = _me()
        me_idx = 4 * x + 2 * y + c
        mine = pltpu.make_async_copy(gs_ref, small_ref.at[me_idx], local_sem)
        mine.start()
        d2d = pltpu.make_async_remote_copy(
            src_ref=gb_ref.at[:, pl.ds((1 - c) * PACK_HALF, PACK_HALF), :], dst_ref=half_ref,
            send_sem=send_sems.at[0], recv_sem=recv_sems.at[0], device_id=(x, y, 1 - c), device_id_type=MESH)
        d2d.start()
        copies = []
        for k in range(1, N_DEV):
            fx, fy, fc = (k >> 2) & 1, (k >> 1) & 1, k & 1
            peer = (x ^ fx, y ^ fy, c ^ fc)
            copies.append(pltpu.make_async_remote_copy(
                src_ref=gs_ref, dst_ref=small_ref.at[me_idx], send_sem=send_sems.at[k], recv_sem=recv_sems.at[k],
                device_id=peer, device_id_type=MESH))
        for cp in copies:
            cp.start()
        for k in range(1, N_DEV):
            fx, fy, fc = (k >> 2) & 1, (k >> 1) & 1, k & 1
            peer_idx = 4 * (x ^ fx) + 2 * (y ^ fy) + (c ^ fc)
            pltpu.make_async_remote_copy(
                src_ref=gs_ref, dst_ref=small_ref.at[peer_idx], send_sem=send_sems.at[k], recv_sem=recv_sems.at[k],
                device_id=(x, y, c), device_id_type=MESH).wait_recv()
        d2d.wait_recv()
        d2d.wait_send()
        for cp in copies:
            cp.wait_send()
        mine.wait()

    return pl.pallas_call(
        body, name="exchange_partials", in_specs=[ANY, pl.BlockSpec(memory_space=pltpu.VMEM)], out_specs=[ANY, ANY],
        out_shape=[jax.ShapeDtypeStruct((N_CHIPS, PACK_HALF, PACK_COLS), gb.dtype),
                   jax.ShapeDtypeStruct((N_DEV, SMALL_ROWS, PACK_COLS), F32)],
        scratch_shapes=[pltpu.SemaphoreType.DMA((N_DEV,)), pltpu.SemaphoreType.DMA((N_DEV,)), pltpu.SemaphoreType.DMA],
    )(gb, gs)


RED_ROWS = 256


def _chip_partials(gb, sib, c_idx):
    nrow = PACK_HALF // RED_ROWS

    def body(c_ref, a_ref, b_ref, o_ref):
        del c_ref
        o_ref[...] = (a_ref[...].astype(F32) + b_ref[...].astype(F32)).astype(o_ref.dtype)

    blk = (1, RED_ROWS, PACK_COLS)
    return pl.pallas_call(
        body, name="chip_partials",
        grid_spec=pltpu.PrefetchScalarGridSpec(
            num_scalar_prefetch=1, grid=(N_CHIPS, nrow),
            in_specs=[pl.BlockSpec(blk, lambda k, i, c: (k, c[0] * nrow + i, 0)),
                      pl.BlockSpec(blk, lambda k, i, c: (k, i, 0))],
            out_specs=pl.BlockSpec(blk, lambda k, i, c: (k, i, 0))),
        out_shape=jax.ShapeDtypeStruct((N_CHIPS, PACK_HALF, PACK_COLS), gb.dtype),
        compiler_params=_params(("parallel", "parallel")),
    )(c_idx, gb, sib)


def _scatter_partials(pc):
    def body(pc_ref, out_ref, send_sems, recv_sems):
        x, y, c = _me()
        chips = _other_chips(x, y)
        copies = [pltpu.make_async_remote_copy(
            src_ref=pc_ref.at[2 * chip[0] + chip[1]], dst_ref=out_ref.at[k],
            send_sem=send_sems.at[k], recv_sem=recv_sems.at[k], device_id=(*chip, c), device_id_type=MESH)
            for k, chip in enumerate(chips)]
        for cp in copies:
            cp.start()
        for cp in copies:
            cp.wait_recv()
        for cp in copies:
            cp.wait_send()

    return pl.pallas_call(
        body, name="scatter_partials", in_specs=[ANY], out_specs=ANY,
        out_shape=jax.ShapeDtypeStruct((3, PACK_HALF, PACK_COLS), pc.dtype),
        scratch_shapes=[pltpu.SemaphoreType.DMA((3,)), pltpu.SemaphoreType.DMA((3,))],
    )(pc)


def _final_half(gb, sib, recv, idx):
    nrow = PACK_HALF // RED_ROWS

    def body(idx_ref, a_ref, b_ref, r_ref, o_ref):
        del idx_ref
        acc = a_ref[0].astype(F32) + b_ref[0].astype(F32)
        for k in range(3):
            acc = acc + r_ref[k].astype(F32)
        o_ref[...] = acc

    return pl.pallas_call(
        body, name="final_half",
        grid_spec=pltpu.PrefetchScalarGridSpec(
            num_scalar_prefetch=1, grid=(nrow,),
            in_specs=[pl.BlockSpec((1, RED_ROWS, PACK_COLS), lambda i, idx: (idx[0], idx[1] * nrow + i, 0)),
                      pl.BlockSpec((1, RED_ROWS, PACK_COLS), lambda i, idx: (idx[0], i, 0)),
                      pl.BlockSpec((3, RED_ROWS, PACK_COLS), lambda i, idx: (0, i, 0))],
            out_specs=pl.BlockSpec((RED_ROWS, PACK_COLS), lambda i, idx: (i, 0))),
        out_shape=jax.ShapeDtypeStruct((PACK_HALF, PACK_COLS), F32),
        compiler_params=_params(("parallel",)),
    )(idx, gb, sib, recv)


def _share_halves(half):
    def body(h_ref, out_ref, send_sem, recv_sem):
        x, y, c = _me()
        cp = pltpu.make_async_remote_copy(src_ref=h_ref, dst_ref=out_ref, send_sem=send_sem, recv_sem=recv_sem,
                                          device_id=(x, y, 1 - c), device_id_type=MESH)
        cp.start()
        cp.wait_recv()
        cp.wait_send()

    return pl.pallas_call(
        body, name="share_halves", in_specs=[ANY], out_specs=ANY,
        out_shape=jax.ShapeDtypeStruct((PACK_HALF, PACK_COLS), F32),
        scratch_shapes=[pltpu.SemaphoreType.DMA, pltpu.SemaphoreType.DMA],
    )(half)


def _sum_small(allsmall):
    def body(a_ref, o_ref):
        acc = a_ref[0]
        for k in range(1, N_DEV):
            acc = acc + a_ref[k]
        o_ref[...] = acc

    tr = 96
    return pl.pallas_call(
        body, name="sum_small", grid=(SMALL_ROWS // tr,),
        in_specs=[pl.BlockSpec((N_DEV, tr, PACK_COLS), lambda i: (0, i, 0))],
        out_specs=pl.BlockSpec((tr, PACK_COLS), lambda i: (i, 0)),
        out_shape=jax.ShapeDtypeStruct((SMALL_ROWS, PACK_COLS), F32),
        compiler_params=_params(("parallel",)),
    )(allsmall)


def _adamw(w, g, m, v, *, name):
    shape = w.shape
    cols = shape[-1]
    as2 = lambda t: t.reshape(-1, cols)
    w2, g2, m2, v2 = as2(w), as2(g), as2(m), as2(v)
    rows = w2.shape[0]
    tr = rows
    if rows * cols * 4 > (1 << 20):
        tr = _tile(rows, max(8, (1 << 20) // (cols * 4) // 8 * 8), 8)

    def body(w_ref, g_ref, m_ref, v_ref, d_ref, mo_ref, vo_ref):
        gg = g_ref[...]
        mn = ADAM_B1 * m_ref[...] + (1.0 - ADAM_B1) * gg
        vn = ADAM_B2 * v_ref[...] + (1.0 - ADAM_B2) * (gg * gg)
        m_hat = mn / (1.0 - ADAM_B1 ** ADAM_STEP)
        v_hat = vn / (1.0 - ADAM_B2 ** ADAM_STEP)
        d_ref[...] = -ADAM_LR * (m_hat / (jnp.sqrt(v_hat) + ADAM_EPS) + ADAM_WD * w_ref[...])
        mo_ref[...] = mn
        vo_ref[...] = vn

    blk = pl.BlockSpec((tr, cols), lambda i: (i, 0))
    outs = pl.pallas_call(
        body, name=name, grid=(rows // tr,), in_specs=[blk] * 4, out_specs=[blk] * 3,
        out_shape=[jax.ShapeDtypeStruct((rows, cols), F32)] * 3,
        compiler_params=_params(("parallel",)),
    )(w2, g2, m2, v2)
    return tuple(t.reshape(shape) for t in outs)


def kernel(x, p, rel_bias, norm_attn_g, w_in, sgu_ln_g, sgu_ln_b, sgu_w, sgu_b, ssm_a_re, ssm_a_im, ssm_log_dt, ssm_b_re, ssm_b_im, ssm_c_re, ssm_c_im, ssm_d, ssm_glu_w, ssm_glu_b, branch_norm_g, w_out, norm_ffn_g, ffn_w_up, ffn_conv_w, ffn_conv_b, ffn_w_down, norm_ple_g, ple_w_gate, ple_w_proj, final_norm_g, loss_target, m_rel_bias, m_norm_attn_g, m_w_in, m_sgu_ln_g, m_sgu_ln_b, m_sgu_w, m_sgu_b, m_ssm_a_re, m_ssm_a_im, m_ssm_log_dt, m_ssm_b_re, m_ssm_b_im, m_ssm_c_re, m_ssm_c_im, m_ssm_d, m_ssm_glu_w, m_ssm_glu_b, m_branch_norm_g, m_w_out, m_norm_ffn_g, m_ffn_w_up, m_ffn_conv_w, m_ffn_conv_b, m_ffn_w_down, m_norm_ple_g, m_ple_w_gate, m_ple_w_proj, m_final_norm_g, v_rel_bias, v_norm_attn_g, v_w_in, v_sgu_ln_g, v_sgu_ln_b, v_sgu_w, v_sgu_b, v_ssm_a_re, v_ssm_a_im, v_ssm_log_dt, v_ssm_b_re, v_ssm_b_im, v_ssm_c_re, v_ssm_c_im, v_ssm_d, v_ssm_glu_w, v_ssm_glu_b, v_branch_norm_g, v_w_out, v_norm_ffn_g, v_ffn_w_up, v_ffn_conv_w, v_ffn_conv_b, v_ffn_w_down, v_norm_ple_g, v_ple_w_gate, v_ple_w_proj, v_final_norm_g):
    args = dict(locals())
    wts = {n: args[n] for n in WEIGHT_NAMES}
    mom_m = {n: args["m_" + n] for n in WEIGHT_NAMES}
    mom_v = {n: args["v_" + n] for n in WEIGHT_NAMES}

    xi, yi, ci = _me()
    wflat = _pack_shards({n: wts[n] for n in BIG_NAMES}, MXU_DTYPE, exact=True)
    wall = _fill_own_shard(_gather_weights(wflat), wflat, jnp.stack([2 * xi + yi]).astype(jnp.int32))
    full = dict(wts)
    for n in BIG_NAMES:
        full[n] = _join_shards([_unpack_shard(wall[k], n, exact=True) for k in range(N_CHIPS)], n)
    full["ffn_conv_w"] = full["ffn_conv_w"].astype(F32)

    loss, dx, grads = _local_step(x[0], p[:, 0], loss_target[0], full, ff_interleaved=True)
    loss = lax.psum(loss[0, 0], MESH_AXES)

    xi, yi, ci = _me()
    stacked = {n: _split_full(grads[n], n) for n in BIG_NAMES}
    gb = jnp.stack([_pack_shards({n: stacked[n][k] for n in BIG_NAMES}, MXU_DTYPE) for k in range(N_CHIPS)])
    gs = _pack_small(grads)
    sib, allsmall = _exchange_partials(gb, gs)
    pc = _chip_partials(gb, sib, jnp.stack([ci]).astype(jnp.int32))
    recv = _scatter_partials(pc)
    half = _final_half(gb, sib, recv, jnp.stack([2 * xi + yi, ci]).astype(jnp.int32))
    other = _share_halves(half)
    gflat = jnp.concatenate([jnp.where(ci == 0, half, other), jnp.where(ci == 0, other, half)], axis=0)
    gsmall = _unpack_small(_sum_small(allsmall), _small_shapes(wts))

    g_out, d_out, m_out, v_out = {}, {}, {}, {}
    for n in BIG_NAMES:
        g_out[n] = _unpack_shard(gflat, n)
        d_out[n], m_out[n], v_out[n] = _adamw(wts[n], g_out[n], mom_m[n], mom_v[n], name="adamw_" + n)
    sw = _pack_small(wts)
    d_s, m_s, v_s = _adamw(sw, _pack_small(gsmall), _pack_small(mom_m), _pack_small(mom_v), name="adamw_small")
    shapes = _small_shapes(wts)
    d_sm, m_sm, v_sm = _unpack_small(d_s, shapes), _unpack_small(m_s, shapes), _unpack_small(v_s, shapes)
    for n in SMALL_NAMES:
        g_out[n], d_out[n], m_out[n], v_out[n] = gsmall[n], d_sm[n], m_sm[n], v_sm[n]

    return (loss, dx[None], *[g_out[n] for n in WEIGHT_NAMES], *[d_out[n] for n in WEIGHT_NAMES],
            *[m_out[n] for n in WEIGHT_NAMES], *[v_out[n] for n in WEIGHT_NAMES])
```

```python
import functools
import math

import numpy as np
import jax
import jax.numpy as jnp
from jax import lax
from jax.experimental import pallas as pl
from jax.experimental.pallas import tpu as pltpu

F32 = jnp.float32
MXU_DTYPE = jnp.bfloat16
VMEM_LIMIT_BYTES = 52 * 1024 * 1024

D_MODEL = 1024
DEPTH = 2
PLE_DIM = 256
HEAD_DIM = 64
N_HEADS = 8
ATTN_W = 512
QBLK = 128
BRANCH_DIL = (1, 4, 16)
N_BUCKETS = 32
REL_MAX_DIST = 2048
SGU_W = 256
SGU_G = 4
SGU_GW = 64
SGU_CHUNK = 128
SSM_W = 256
SSM_G = 16
SSM_C = 16
SSM_N = 64
NSTATE = SSM_G * SSM_N
D_FF = 2816
EPS = 1e-6
NEG_INF = -1e30
ATTN_SCALE = HEAD_DIM ** -0.5

ADAM_LR = 0.001
ADAM_B1 = 0.9
ADAM_B2 = 0.999
ADAM_EPS = 1e-08
ADAM_WD = 0.01
ADAM_STEP = 10

SSM_NSEG = 8
SSM_TSEG = 64
SSM_TB = SSM_NSEG * SSM_TSEG
SSM_LANE_CHUNK = 512

MESH_AXES = ("x", "y", "c")
N_CHIPS = 4
N_DEV = 8

BIG_NAMES = ("w_in", "ssm_glu_w", "w_out", "ffn_w_up", "ffn_conv_w", "ffn_w_down", "ple_w_gate", "ple_w_proj")
BIG_FULL = {
    "w_in": ((D_MODEL, 2304), 2),
    "ssm_glu_w": ((SSM_W, SSM_W), 1),
    "w_out": ((D_MODEL, D_MODEL), 1),
    "ffn_w_up": ((D_MODEL, 2 * D_FF), 2),
    "ffn_conv_w": ((3, 2 * D_FF), 2),
    "ffn_w_down": ((D_FF, D_MODEL), 1),
    "ple_w_gate": ((D_MODEL, D_MODEL), 1),
    "ple_w_proj": ((PLE_DIM, D_MODEL), 2),
}
PACK_COLS = 1024
PACK_ROWS = 6656
PACK_HALF = PACK_ROWS // 2

SMALL_NAMES = ("rel_bias", "norm_attn_g", "sgu_ln_g", "sgu_ln_b", "sgu_w", "sgu_b", "ssm_a_re", "ssm_a_im",
               "ssm_log_dt", "ssm_b_re", "ssm_b_im", "ssm_c_re", "ssm_c_im", "ssm_d", "ssm_glu_b",
               "branch_norm_g", "norm_ffn_g", "ffn_conv_b", "norm_ple_g", "final_norm_g")
SMALL_ROWS = 384

WEIGHT_NAMES = ("rel_bias", "norm_attn_g", "w_in", "sgu_ln_g", "sgu_ln_b", "sgu_w", "sgu_b", "ssm_a_re", "ssm_a_im",
                "ssm_log_dt", "ssm_b_re", "ssm_b_im", "ssm_c_re", "ssm_c_im", "ssm_d", "ssm_glu_w", "ssm_glu_b",
                "branch_norm_g", "w_out", "norm_ffn_g", "ffn_w_up", "ffn_conv_w", "ffn_conv_b", "ffn_w_down",
                "norm_ple_g", "ple_w_gate", "ple_w_proj", "final_norm_g")


def _params(sem):
    return pltpu.CompilerParams(dimension_semantics=sem, vmem_limit_bytes=VMEM_LIMIT_BYTES)


def _tile(n, cap, mult=128):
    if n <= cap:
        return n
    best = None
    for t in range(mult, cap + 1, mult):
        if n % t == 0:
            best = t
    assert best is not None, (n, cap)
    return best


def _gelu(x):
    return 0.5 * x * (1.0 + jnp.tanh(0.7978845608028654 * (x + 0.044715 * x * x * x)))


def _gelu_pair(x):
    x2 = x * x
    t = jnp.tanh(0.7978845608028654 * x * (1.0 + 0.044715 * x2))
    half = 0.5 * (1.0 + t)
    return x * half, half + 0.5 * x * (1.0 - t * t) * (0.7978845608028654 + 3.0 * 0.044715 * 0.7978845608028654 * x2)


def _dot(a, b, dims):
    return lax.dot_general(a, b, (dims, ((), ())), preferred_element_type=F32)


def _dotf(a, b, dims):
    return _dot(a.astype(MXU_DTYPE), b.astype(MXU_DTYPE), dims)


NN = ((1,), (0,))
NT = ((1,), (1,))
TN = ((0,), (0,))


def _matmul(a, b, *, name, out_dtype, tm, tn, trans_b=False, residual=None, layer=None):
    m, k = a.shape
    n = b.shape[-2] if trans_b else b.shape[-1]
    tm = _tile(m, tm, 8)
    tn = _tile(n, tn)
    dims = NT if trans_b else NN
    lead = () if layer is None else (None,)
    lidx = () if layer is None else (layer,)

    def body(*refs):
        if residual is None:
            a_ref, b_ref, o_ref = refs
        else:
            a_ref, b_ref, r_ref, o_ref = refs
        acc = _dot(a_ref[...].astype(MXU_DTYPE), b_ref[...].astype(MXU_DTYPE), dims)
        if residual is not None:
            acc = acc + r_ref[...]
        o_ref[...] = acc.astype(o_ref.dtype)

    b_spec = (pl.BlockSpec(lead + (tn, k), lambda i, j: lidx + (j, 0)) if trans_b
              else pl.BlockSpec(lead + (k, tn), lambda i, j: lidx + (0, j)))
    in_specs = [pl.BlockSpec((tm, k), lambda i, j: (i, 0)), b_spec]
    args = [a, b]
    if residual is not None:
        in_specs.append(pl.BlockSpec((tm, tn), lambda i, j: (i, j)))
        args.append(residual)
    return pl.pallas_call(
        body, name=name, grid=(m // tm, n // tn), in_specs=in_specs,
        out_specs=pl.BlockSpec((tm, tn), lambda i, j: (i, j)),
        out_shape=jax.ShapeDtypeStruct((m, n), out_dtype),
        compiler_params=_params(("parallel", "parallel")),
    )(*args)


def _matmul_tn(a, g, *, name, tk, tn, tm=512):
    m, k = a.shape
    n = g.shape[1]
    tk = _tile(k, tk)
    tn = _tile(n, tn)
    tm = _tile(m, tm, 8)

    def body(a_ref, g_ref, o_ref):
        @pl.when(pl.program_id(2) == 0)
        def _():
            o_ref[...] = jnp.zeros_like(o_ref)

        o_ref[...] += _dot(a_ref[...].astype(MXU_DTYPE), g_ref[...].astype(MXU_DTYPE), TN)

    return pl.pallas_call(
        body, name=name, grid=(k // tk, n // tn, m // tm),
        in_specs=[pl.BlockSpec((tm, tk), lambda i, j, s: (s, i)),
                  pl.BlockSpec((tm, tn), lambda i, j, s: (s, j))],
        out_specs=pl.BlockSpec((tk, tn), lambda i, j, s: (i, j)),
        out_shape=jax.ShapeDtypeStruct((k, n), F32),
        compiler_params=_params(("parallel", "parallel", "arbitrary")),
    )(a, g)


ROWS = 512


def _rms_fwd(h, g, *, name):
    s, d = h.shape

    def body(h_ref, g_ref, o_ref):
        x = h_ref[...]
        r = lax.rsqrt(jnp.mean(x * x, axis=-1, keepdims=True) + EPS)
        o_ref[...] = (x * r * g_ref[...]).astype(o_ref.dtype)

    return pl.pallas_call(
        body, name=name, grid=(s // ROWS,),
        in_specs=[pl.BlockSpec((ROWS, d), lambda i: (i, 0)), pl.BlockSpec((1, d), lambda i: (0, 0))],
        out_specs=pl.BlockSpec((ROWS, d), lambda i: (i, 0)),
        out_shape=jax.ShapeDtypeStruct((s, d), MXU_DTYPE),
        compiler_params=_params(("parallel",)),
    )(h, g.reshape(1, d))


def _rms_bwd(h, g, dxn, dres, *, name):
    s, d = h.shape

    def body(h_ref, g_ref, dxn_ref, dres_ref, dh_ref, dg_ref):
        @pl.when(pl.program_id(0) == 0)
        def _():
            dg_ref[...] = jnp.zeros_like(dg_ref)

        x = h_ref[...]
        r = lax.rsqrt(jnp.mean(x * x, axis=-1, keepdims=True) + EPS)
        xhat = x * r
        dxn = dxn_ref[...].astype(F32)
        dg_ref[...] += jnp.sum(dxn * xhat, axis=0, keepdims=True)
        dxh = dxn * g_ref[...]
        dh_ref[...] = dres_ref[...] + r * (dxh - xhat * jnp.mean(dxh * xhat, axis=-1, keepdims=True))

    row = pl.BlockSpec((ROWS, d), lambda i: (i, 0))
    vec = pl.BlockSpec((1, d), lambda i: (0, 0))
    return pl.pallas_call(
        body, name=name, grid=(s // ROWS,), in_specs=[row, vec, row, row], out_specs=[row, vec],
        out_shape=[jax.ShapeDtypeStruct((s, d), F32), jax.ShapeDtypeStruct((1, d), F32)],
        compiler_params=_params(("arbitrary",)),
    )(h, g.reshape(1, d), dxn, dres)


def _matmul_rms_bwd(a, b, h, g, dres, *, name, layer, tm):
    s, k = a.shape
    d = b.shape[-2]

    def body(a_ref, b_ref, h_ref, g_ref, dres_ref, dh_ref, dg_ref):
        @pl.when(pl.program_id(0) == 0)
        def _():
            dg_ref[...] = jnp.zeros_like(dg_ref)

        dxn = _dot(a_ref[...].astype(MXU_DTYPE), b_ref[...].astype(MXU_DTYPE), NT)
        x = h_ref[...]
        r = lax.rsqrt(jnp.mean(x * x, axis=-1, keepdims=True) + EPS)
        xhat = x * r
        dg_ref[...] += jnp.sum(dxn * xhat, axis=0, keepdims=True)
        dxh = dxn * g_ref[...]
        dh_ref[...] = dres_ref[...] + r * (dxh - xhat * jnp.mean(dxh * xhat, axis=-1, keepdims=True))

    row = pl.BlockSpec((tm, d), lambda i: (i, 0))
    vec = pl.BlockSpec((1, d), lambda i: (0, 0))
    return pl.pallas_call(
        body, name=name, grid=(s // tm,),
        in_specs=[pl.BlockSpec((tm, k), lambda i: (i, 0)), pl.BlockSpec((None, d, k), lambda i: (layer, 0, 0)),
                  row, vec, row],
        out_specs=[row, vec],
        out_shape=[jax.ShapeDtypeStruct((s, d), F32), jax.ShapeDtypeStruct((1, d), F32)],
        compiler_params=_params(("arbitrary",)),
    )(a, b, h, g.reshape(1, d), dres)


def _loss_head(h, g, target):
    s, d = h.shape

    def body(h_ref, g_ref, t_ref, loss_ref, dh_ref, dg_ref):
        @pl.when(pl.program_id(0) == 0)
        def _():
            loss_ref[...] = jnp.zeros_like(loss_ref)
            dg_ref[...] = jnp.zeros_like(dg_ref)

        x = h_ref[...]
        r = lax.rsqrt(jnp.mean(x * x, axis=-1, keepdims=True) + EPS)
        xhat = x * r
        err = xhat * g_ref[...] - t_ref[...]
        loss_ref[...] += 0.5 * jnp.sum(jnp.mean(err * err, axis=-1, keepdims=True), axis=0, keepdims=True)
        dy = err / d
        dg_ref[...] += jnp.sum(dy * xhat, axis=0, keepdims=True)
        dxh = dy * g_ref[...]
        dh_ref[...] = r * (dxh - xhat * jnp.mean(dxh * xhat, axis=-1, keepdims=True))

    row = pl.BlockSpec((ROWS, d), lambda i: (i, 0))
    vec = pl.BlockSpec((1, d), lambda i: (0, 0))
    one = pl.BlockSpec((1, 1), lambda i: (0, 0))
    return pl.pallas_call(
        body, name="loss_head", grid=(s // ROWS,), in_specs=[row, vec, row], out_specs=[one, row, vec],
        out_shape=[jax.ShapeDtypeStruct((1, 1), F32), jax.ShapeDtypeStruct((s, d), F32),
                   jax.ShapeDtypeStruct((1, d), F32)],
        compiler_params=_params(("arbitrary",)),
    )(h, g.reshape(1, d), target)


def _t5_bucket(dist):
    max_exact = N_BUCKETS // 2
    dd = np.maximum(dist, 0)
    large = max_exact + (np.log(np.maximum(dd, 1) / max_exact) / np.log(REL_MAX_DIST / max_exact)
                         * (N_BUCKETS - max_exact)).astype(np.int32)
    large = np.minimum(large, N_BUCKETS - 1)
    return np.where(dd < max_exact, dd, large).astype(np.int32)


def _bucket_table():
    qq = np.arange(QBLK)[:, None]
    kk = np.arange(QBLK)[None, :]
    out = np.zeros((len(BRANCH_DIL), 2, QBLK, QBLK), np.int32)
    for b, dil in enumerate(BRANCH_DIL):
        out[b, 0] = _t5_bucket((qq - kk + QBLK) * dil)
        out[b, 1] = _t5_bucket((qq - kk) * dil)
    return out


BIAS_TILE = 2 * QBLK


def _bias_build(rel_bias):
    idx = jnp.asarray(_bucket_table())

    def body(idx_ref, rb_ref, o_ref):
        ch = pl.program_id(1)
        row = lax.broadcasted_iota(jnp.int32, (QBLK, QBLK), 0)
        col = lax.broadcasted_iota(jnp.int32, (QBLK, QBLK), 1)
        for part in range(2):
            ids = idx_ref[0, 1 - part]
            valid = (col <= row) if part == 0 else (col >= row)
            for h in range(2):
                acc = jnp.zeros((QBLK, QBLK), F32)
                for b in range(N_BUCKETS):
                    acc = jnp.where(ids == b, rb_ref[b, 2 * ch + h], acc)
                o_ref[0, 0, QBLK * h:QBLK * (h + 1), QBLK * part:QBLK * (part + 1)] = jnp.where(valid, acc, NEG_INF)

    return pl.pallas_call(
        body, name="attn_bias_build", grid=(len(BRANCH_DIL), N_HEADS // 2),
        in_specs=[pl.BlockSpec((1, 2, QBLK, QBLK), lambda b, c: (b, 0, 0, 0)),
                  pl.BlockSpec(memory_space=pltpu.SMEM)],
        out_specs=pl.BlockSpec((1, 1, BIAS_TILE, BIAS_TILE), lambda b, c: (b, c, 0, 0)),
        out_shape=jax.ShapeDtypeStruct((len(BRANCH_DIL), N_HEADS // 2, BIAS_TILE, BIAS_TILE), F32),
        compiler_params=_params(("parallel", "parallel")),
    )(idx, rel_bias)


def _bias_reduce(dbias):
    idx = jnp.asarray(_bucket_table())
    nb = len(BRANCH_DIL)

    def body(idx_ref, d_ref, o_ref):
        def per_bucket(b, carry):
            for h in range(N_HEADS):
                tot = jnp.zeros((), F32)
                for br in range(nb):
                    for part in range(2):
                        tile = d_ref[br, h // 2, QBLK * (h % 2):QBLK * (h % 2 + 1), QBLK * part:QBLK * (part + 1)]
                        tot = tot + jnp.sum(jnp.where(idx_ref[br, 1 - part] == b, tile, 0.0))
                o_ref[b, h] = tot
            return carry

        lax.fori_loop(0, N_BUCKETS, per_bucket, 0)

    return pl.pallas_call(
        body, name="attn_bias_reduce",
        in_specs=[pl.BlockSpec(memory_space=pltpu.VMEM), pl.BlockSpec(memory_space=pltpu.VMEM)],
        out_specs=pl.BlockSpec(memory_space=pltpu.SMEM),
        out_shape=jax.ShapeDtypeStruct((N_BUCKETS, N_HEADS), F32),
        compiler_params=pltpu.CompilerParams(vmem_limit_bytes=VMEM_LIMIT_BYTES),
    )(idx, dbias)


def _band_masks(c):
    row = lax.broadcasted_iota(jnp.int32, (QBLK, QBLK), 0)
    col = lax.broadcasted_iota(jnp.int32, (QBLK, QBLK), 1)
    mask_cur = col <= row
    mask_prev = jnp.logical_and(col >= row, c > 0)
    return mask_prev, mask_cur


def _attn_specs(dil):
    blk = (QBLK, ATTN_W)
    q = pl.BlockSpec(blk, lambda r, c: (c, 3 * r))
    kp = pl.BlockSpec(blk, lambda r, c: (jnp.maximum(c - 1, 0), 3 * r + 1))
    kc = pl.BlockSpec(blk, lambda r, c: (c, 3 * r + 1))
    vp = pl.BlockSpec(blk, lambda r, c: (jnp.maximum(c - 1, 0), 3 * r + 2))
    vc = pl.BlockSpec(blk, lambda r, c: (c, 3 * r + 2))
    return [q, kp, kc, vp, vc]


def _attn_fwd_branch(qkv, bias, state, *, branch, last):
    dil = BRANCH_DIL[branch]
    s = qkv.shape[0]
    n = s // dil
    nblk = n // QBLK
    first = state is None

    def body(*refs):
        q_ref, kp_ref, kc_ref, vp_ref, vc_ref, b_ref = refs[:6]
        if first:
            outs = refs[6:]
        else:
            acc_ref, m_ref, l_ref = refs[6:9]
            outs = refs[9:]
        mask_prev, mask_cur = _band_masks(pl.program_id(1))
        for h in range(N_HEADS):
            sl = slice(HEAD_DIM * h, HEAD_DIM * (h + 1))
            qh = q_ref[:, sl]
            s_c = _dot(qh, kc_ref[:, sl], NT) * ATTN_SCALE + b_ref[0, 1, h]
            s_p = _dot(qh, kp_ref[:, sl], NT) * ATTN_SCALE + b_ref[0, 0, h]
            s_c = jnp.where(mask_cur, s_c, NEG_INF)
            s_p = jnp.where(mask_prev, s_p, NEG_INF)
            m_blk = jnp.maximum(jnp.max(s_c, axis=-1, keepdims=True), jnp.max(s_p, axis=-1, keepdims=True))
            if first:
                m_new = m_blk
            else:
                m_old = m_ref[:, sl][:, :1]
                m_new = jnp.maximum(m_old, m_blk)
            p_c = jnp.exp(s_c - m_new)
            p_p = jnp.exp(s_p - m_new)
            l_new = jnp.sum(p_c, axis=-1, keepdims=True) + jnp.sum(p_p, axis=-1, keepdims=True)
            acc = (_dot(p_c.astype(MXU_DTYPE), vc_ref[:, sl], NN)
                   + _dot(p_p.astype(MXU_DTYPE), vp_ref[:, sl], NN))
            if not first:
                alpha = jnp.exp(m_old - m_new)
                l_new = l_new + alpha * l_ref[:, sl][:, :1]
                acc = acc + alpha * acc_ref[:, sl]
            if last:
                outs[0][:, sl] = acc / l_new
                outs[1][:, sl] = jnp.broadcast_to(m_new + jnp.log(l_new), (QBLK, HEAD_DIM))
            else:
                outs[0][:, sl] = acc
                outs[1][:, sl] = jnp.broadcast_to(m_new, (QBLK, HEAD_DIM))
                outs[2][:, sl] = jnp.broadcast_to(l_new, (QBLK, HEAD_DIM))

    st_spec = pl.BlockSpec((QBLK, ATTN_W), lambda r, c: (c, r))
    in_specs = _attn_specs(dil) + [pl.BlockSpec((1, 2, N_HEADS, QBLK, QBLK), lambda r, c: (branch, 0, 0, 0, 0))]
    qv = qkv.reshape(n, dil * 3 * ATTN_W)
    args = [qv] * 5 + [bias]
    if not first:
        in_specs += [st_spec] * 3
        args += [t.reshape(n, dil * ATTN_W) for t in state]
    n_out = 2 if last else 3
    outs = pl.pallas_call(
        body, name=f"attn_fwd_b{branch}", grid=(dil, nblk), in_specs=in_specs,
        out_specs=[st_spec] * n_out,
        out_shape=[jax.ShapeDtypeStruct((n, dil * ATTN_W), F32)] * n_out,
        compiler_params=_params(("parallel", "parallel")),
    )(*args)
    return tuple(t.reshape(s, ATTN_W) for t in outs)


def _attn_fwd(qkv, bias):
    state = None
    for b in range(len(BRANCH_DIL)):
        state = _attn_fwd_branch(qkv, bias, state, branch=b, last=(b == len(BRANCH_DIL) - 1))
    return state


def _attn_bwd_branch(qkv, bias, o, lse, do, *, branch):
    dil = BRANCH_DIL[branch]
    s = qkv.shape[0]
    n = s // dil
    nblk = n // QBLK

    def body(q_ref, kp_ref, kc_ref, vp_ref, vc_ref, b_ref, o_ref, l_ref, do_ref,
             dq_ref, dka_ref, dkb_ref, dva_ref, dvb_ref, db_ref):
        @pl.when(jnp.logical_and(pl.program_id(0) == 0, pl.program_id(1) == 0))
        def _():
            db_ref[...] = jnp.zeros_like(db_ref)

        mask_prev, mask_cur = _band_masks(pl.program_id(1))
        for h in range(N_HEADS):
            sl = slice(HEAD_DIM * h, HEAD_DIM * (h + 1))
            qh = q_ref[:, sl]
            doh = do_ref[:, sl]
            lh = l_ref[:, sl][:, :1]
            delta = jnp.sum(doh * o_ref[:, sl], axis=-1, keepdims=True)
            do_m = doh.astype(MXU_DTYPE)
            s_c = _dot(qh, kc_ref[:, sl], NT) * ATTN_SCALE + b_ref[0, 1, h]
            s_p = _dot(qh, kp_ref[:, sl], NT) * ATTN_SCALE + b_ref[0, 0, h]
            p_c = jnp.exp(jnp.where(mask_cur, s_c, NEG_INF) - lh)
            p_p = jnp.exp(jnp.where(mask_prev, s_p, NEG_INF) - lh)
            ds_c = p_c * (_dot(do_m, vc_ref[:, sl], NT) - delta)
            ds_p = p_p * (_dot(do_m, vp_ref[:, sl], NT) - delta)
            db_ref[0, 1, h] += ds_c
            db_ref[0, 0, h] += ds_p
            ds_c_m = ds_c.astype(MXU_DTYPE)
            ds_p_m = ds_p.astype(MXU_DTYPE)
            dq = _dot(ds_c_m, kc_ref[:, sl], NN) + _dot(ds_p_m, kp_ref[:, sl], NN)
            dq_ref[:, sl] = (dq * ATTN_SCALE).astype(dq_ref.dtype)
            dka_ref[:, sl] = (_dot(ds_c_m, qh, TN) * ATTN_SCALE).astype(dka_ref.dtype)
            dkb_ref[:, sl] = (_dot(ds_p_m, qh, TN) * ATTN_SCALE).astype(dkb_ref.dtype)
            dva_ref[:, sl] = _dot(p_c.astype(MXU_DTYPE), do_m, TN).astype(dva_ref.dtype)
            dvb_ref[:, sl] = _dot(p_p.astype(MXU_DTYPE), do_m, TN).astype(dvb_ref.dtype)

    st_spec = pl.BlockSpec((QBLK, ATTN_W), lambda r, c: (c, r))
    b_in = pl.BlockSpec((1, 2, N_HEADS, QBLK, QBLK), lambda r, c: (branch, 0, 0, 0, 0))
    b_out = pl.BlockSpec((1, 2, N_HEADS, QBLK, QBLK), lambda r, c: (0, 0, 0, 0, 0))
    qv = qkv.reshape(n, dil * 3 * ATTN_W)
    view = lambda t: t.reshape(n, dil * ATTN_W)
    outs = pl.pallas_call(
        body, name=f"attn_bwd_b{branch}", grid=(dil, nblk),
        in_specs=_attn_specs(dil) + [b_in, st_spec, st_spec, st_spec],
        out_specs=[st_spec] * 5 + [b_out],
        out_shape=[jax.ShapeDtypeStruct((n, dil * ATTN_W), MXU_DTYPE)] * 5
        + [jax.ShapeDtypeStruct((1, 2, N_HEADS, QBLK, QBLK), F32)],
        compiler_params=_params(("arbitrary", "arbitrary")),
    )(qv, qv, qv, qv, qv, bias, view(o), view(lse), view(do))
    return tuple(t.reshape(s, ATTN_W) for t in outs[:5]) + (outs[5],)


def _attn_bwd(qkv, bias, o, lse, do):
    s = qkv.shape[0]
    nb = s // QBLK
    parts = [_attn_bwd_branch(qkv, bias, o, lse, do, branch=b) for b in range(len(BRANCH_DIL))]
    dbias = jnp.concatenate([p[5] for p in parts], axis=0)

    def body(*refs):
        o_ref = refs[-1]
        i = pl.program_id(0)
        dq = jnp.zeros((QBLK, ATTN_W), F32)
        dk = jnp.zeros((QBLK, ATTN_W), F32)
        dv = jnp.zeros((QBLK, ATTN_W), F32)
        for b, dil in enumerate(BRANCH_DIL):
            dq_ref, dka_ref, dkb_ref, dva_ref, dvb_ref = refs[5 * b:5 * b + 5]
            inside = i + dil < nb
            dq = dq + dq_ref[...].astype(F32)
            dk = dk + dka_ref[...].astype(F32) + jnp.where(inside, dkb_ref[...].astype(F32), 0.0)
            dv = dv + dva_ref[...].astype(F32) + jnp.where(inside, dvb_ref[...].astype(F32), 0.0)
        o_ref[:, 0:ATTN_W] = dq.astype(o_ref.dtype)
        o_ref[:, ATTN_W:2 * ATTN_W] = dk.astype(o_ref.dtype)
        o_ref[:, 2 * ATTN_W:3 * ATTN_W] = dv.astype(o_ref.dtype)

    in_specs, args = [], []
    for b, dil in enumerate(BRANCH_DIL):
        here = pl.BlockSpec((QBLK, ATTN_W), lambda i: (i, 0))
        ahead = pl.BlockSpec((QBLK, ATTN_W), functools.partial(lambda i, d: (jnp.minimum(i + d, nb - 1), 0), d=dil))
        in_specs += [here, here, ahead, here, ahead]
        args += list(parts[b][:5])
    dqkv = pl.pallas_call(
        body, name="attn_bwd_sum", grid=(nb,), in_specs=in_specs,
        out_specs=pl.BlockSpec((QBLK, 3 * ATTN_W), lambda i: (i, 0)),
        out_shape=jax.ShapeDtypeStruct((s, 3 * ATTN_W), MXU_DTYPE),
        compiler_params=_params(("parallel",)),
    )(*args)
    return dqkv, dbias


ATTN_IO_DTYPE = F32
ABLK = 2048
N_CHUNK = ATTN_W // 128


def _rows(start, dil):
    if dil > 1:
        return pl.ds(start, QBLK, stride=dil)
    return pl.ds(pl.multiple_of(start, QBLK), QBLK)


def _low_head():
    return lax.broadcasted_iota(jnp.int32, (QBLK, 128), 1) < HEAD_DIM


def _head_split(t):
    low = _low_head()
    zero = jnp.zeros_like(t)
    return jnp.where(low, t, zero), jnp.where(low, zero, t)


def _tile_bias(b_ref, branch, first):
    bias = b_ref[branch]
    if first is None:
        return bias
    col = lax.broadcasted_iota(jnp.int32, (BIAS_TILE, BIAS_TILE), 1)
    return jnp.where(jnp.logical_and(first, col >= QBLK), NEG_INF, bias)


def _loop(n, fn):
    if n == 1:
        fn(jnp.int32(0), 0)
    elif n > 1:
        lax.fori_loop(0, n, fn, 0, unroll=2)


def _for_each_tile(tile, c):
    for branch, dil in enumerate(BRANCH_DIL):
        span = QBLK * dil

        def edge(r, carry, branch=branch, span=span):
            tile(branch, r, False, ABLK - span + r, c == 0)
            return carry

        def inner(t, carry, branch=branch, span=span, dil=dil):
            start = (1 + t // dil) * span + t % dil
            tile(branch, start, True, start - span, None)
            return carry

        _loop(dil, edge)
        _loop((ABLK // span - 1) * dil, inner)


def _attn_chunk_specs(nb):
    blk = (None, ABLK, 128)
    prev = lambda c: jnp.maximum(c - 1, 0)
    return [pl.BlockSpec(blk, lambda ch, c: (ch, c, 0)),
            pl.BlockSpec(blk, lambda ch, c: (N_CHUNK + ch, c, 0)),
            pl.BlockSpec(blk, lambda ch, c: (2 * N_CHUNK + ch, c, 0)),
            pl.BlockSpec(blk, lambda ch, c: (N_CHUNK + ch, prev(c), 0)),
            pl.BlockSpec(blk, lambda ch, c: (2 * N_CHUNK + ch, prev(c), 0)),
            pl.BlockSpec((len(BRANCH_DIL), None, BIAS_TILE, BIAS_TILE), lambda ch, c: (0, ch, 0, 0))]


def _rms_rows(x, g):
    r = lax.rsqrt(jnp.mean(x * x, axis=-1, keepdims=True) + EPS)
    return (x * r * g).astype(MXU_DTYPE)


def _in_proj(h, gain, w_in, layer):
    s, k = h.shape
    tm = 512
    nch = O_SGU // 128

    def body(h_ref, g_ref, w_ref, xn_ref, qkv_ref, zs_ref, us_ref):
        xn = _rms_rows(h_ref[...], g_ref[...])
        xn_ref[...] = xn
        acc = _dot(xn, w_ref[...].astype(MXU_DTYPE), NN)
        for j in range(nch):
            blk = acc[:, 128 * j:128 * (j + 1)]
            if j < N_CHUNK:
                blk = blk * ATTN_SCALE
            qkv_ref[j] = blk.astype(qkv_ref.dtype)
        zs_ref[...] = acc[:, O_SGU:O_SSM]
        us_ref[...] = acc[:, O_SSM:]

    n = w_in.shape[-1]
    return pl.pallas_call(
        body, name="in_proj", grid=(s // tm,),
        in_specs=[pl.BlockSpec((tm, k), lambda i: (i, 0)), pl.BlockSpec((1, k), lambda i: (0, 0)),
                  pl.BlockSpec((None, k, n), lambda i: (layer, 0, 0))],
        out_specs=[pl.BlockSpec((tm, k), lambda i: (i, 0)), pl.BlockSpec((nch, tm, 128), lambda i: (0, i, 0)),
                   pl.BlockSpec((tm, O_SSM - O_SGU), lambda i: (i, 0)), pl.BlockSpec((tm, n - O_SSM), lambda i: (i, 0))],
        out_shape=[jax.ShapeDtypeStruct((s, k), MXU_DTYPE), jax.ShapeDtypeStruct((nch, s, 128), ATTN_IO_DTYPE),
                   jax.ShapeDtypeStruct((s, O_SSM - O_SGU), F32), jax.ShapeDtypeStruct((s, n - O_SSM), F32)],
        compiler_params=_params(("parallel",)),
    )(h, gain.reshape(1, k), w_in)


def _ffn_up(h, gain, w_up, layer):
    s, k = h.shape
    n = w_up.shape[-1]
    tm, tn = 1024, CONV_COLS

    def body(h_ref, g_ref, w_ref, xn_ref, o_ref):
        @pl.when(pl.program_id(1) == 0)
        def _():
            xn_ref[...] = _rms_rows(h_ref[...], g_ref[...])

        o_ref[...] = _dot(xn_ref[...], w_ref[...].astype(MXU_DTYPE), NN).astype(o_ref.dtype)

    return pl.pallas_call(
        body, name="ffn_up", grid=(s // tm, n // tn),
        in_specs=[pl.BlockSpec((tm, k), lambda i, j: (i, 0)), pl.BlockSpec((1, k), lambda i, j: (0, 0)),
                  pl.BlockSpec((None, k, tn), lambda i, j: (layer, 0, j))],
        out_specs=[pl.BlockSpec((tm, k), lambda i, j: (i, 0)), pl.BlockSpec((tm, tn), lambda i, j: (i, j))],
        out_shape=[jax.ShapeDtypeStruct((s, k), MXU_DTYPE), jax.ShapeDtypeStruct((s, n), MXU_DTYPE)],
        compiler_params=_params(("parallel", "arbitrary")),
    )(h, gain.reshape(1, k), w_up)


def _attn2_fwd(qkv_c, bias):
    s = qkv_c.shape[1]
    nb = s // ABLK
    last = len(BRANCH_DIL) - 1

    def body(q_ref, kc_ref, vc_ref, kp_ref, vp_ref, b_ref, o_ref, l_ref, acc_s, m_s, l_s):
        low = _low_head()
        e_st = jnp.concatenate(_head_split(jnp.ones((QBLK, 128), MXU_DTYPE)) * 2, axis=0)

        def tile(branch, start, prev_in_block, pstart, first):
            dil = BRANCH_DIL[branch]
            rq, rp = _rows(start, dil), _rows(pstart, dil)
            k_ref, v_ref = (kc_ref, vc_ref) if prev_in_block else (kp_ref, vp_ref)
            q_st = jnp.concatenate(_head_split(q_ref[rq, :].astype(MXU_DTYPE)), axis=0)
            k_st = jnp.concatenate([kc_ref[rq, :].astype(MXU_DTYPE), k_ref[rp, :].astype(MXU_DTYPE)], axis=0)
            v_st = jnp.concatenate(_head_split(vc_ref[rq, :].astype(MXU_DTYPE))
                                   + _head_split(v_ref[rp, :].astype(MXU_DTYPE)), axis=0)
            sc = _dot(q_st, k_st, NT) + _tile_bias(b_ref, branch, first)
            m_new = jnp.max(sc, axis=-1, keepdims=True)
            if branch > 0:
                m_old2 = m_s[rq, :]
                m_old = jnp.concatenate([m_old2[:, 0:1], m_old2[:, HEAD_DIM:HEAD_DIM + 1]], axis=0)
                m_new = jnp.maximum(m_old, m_new)
                alpha = jnp.exp(m_old - m_new)
            p = jnp.exp(sc - m_new).astype(MXU_DTYPE)
            lhs = jnp.concatenate([p[:QBLK, :QBLK], p[QBLK:, :QBLK], p[:QBLK, QBLK:], p[QBLK:, QBLK:]], axis=1)
            acc2 = _dot(lhs, v_st, NN)
            sum2 = _dot(lhs, e_st, NN)
            m2 = jnp.where(low, m_new[:QBLK], m_new[QBLK:])
            if branch > 0:
                a2 = jnp.where(low, alpha[:QBLK], alpha[QBLK:])
                acc2 = acc2 + a2 * acc_s[rq, :]
                sum2 = sum2 + a2 * l_s[rq, :]
            if branch == last:
                o_ref[rq, :] = acc2 / sum2
                l_ref[rq, :] = m2 + jnp.log(sum2)
            else:
                acc_s[rq, :] = acc2
                m_s[rq, :] = m2
                l_s[rq, :] = sum2

        _for_each_tile(tile, pl.program_id(1))

    out_spec = pl.BlockSpec((None, ABLK, 128), lambda ch, c: (ch, c, 0))
    return pl.pallas_call(
        body, name="attn_fwd", grid=(N_CHUNK, nb), in_specs=_attn_chunk_specs(nb),
        out_specs=[out_spec, out_spec],
        out_shape=[jax.ShapeDtypeStruct((N_CHUNK, s, 128), F32)] * 2,
        scratch_shapes=[pltpu.VMEM((ABLK, 128), F32)] * 3,
        compiler_params=_params(("parallel", "arbitrary")),
    )(qkv_c, qkv_c, qkv_c, qkv_c, qkv_c, bias)


def _attn2_bwd(qkv_c, bias, lse_c, delta_c, do_c):
    s = qkv_c.shape[1]
    nb = s // ABLK
    nbr = len(BRANCH_DIL)

    def body(q_ref, kc_ref, vc_ref, kp_ref, vp_ref, b_ref, l_ref, dl_ref, do_ref,
             dq_ref, dk_ref, dv_ref, *rest):
        ek_refs, ev_refs, db_ref = rest[:nbr], rest[nbr:2 * nbr], rest[2 * nbr]
        c = pl.program_id(1)

        @pl.when(c == 0)
        def _():
            db_ref[...] = jnp.zeros_like(db_ref)

        for r in (dq_ref, dk_ref, dv_ref) + tuple(ek_refs) + tuple(ev_refs):
            r[...] = jnp.zeros_like(r)

        def tile(branch, start, prev_in_block, pstart, first):
            dil = BRANCH_DIL[branch]
            rq, rp = _rows(start, dil), _rows(pstart, dil)
            k_ref, v_ref = (kc_ref, vc_ref) if prev_in_block else (kp_ref, vp_ref)
            kc2 = kc_ref[rq, :].astype(MXU_DTYPE)
            kp2 = k_ref[rp, :].astype(MXU_DTYPE)
            q_st = jnp.concatenate(_head_split(q_ref[rq, :].astype(MXU_DTYPE)), axis=0)
            do_st = jnp.concatenate(_head_split(do_ref[rq, :].astype(MXU_DTYPE)), axis=0)
            k_st = jnp.concatenate([kc2, kp2], axis=0)
            v_st = jnp.concatenate([vc_ref[rq, :].astype(MXU_DTYPE), v_ref[rp, :].astype(MXU_DTYPE)], axis=0)
            kh_st = jnp.concatenate(_head_split(kc2) + _head_split(kp2), axis=0)
            lse2 = l_ref[rq, :]
            del2 = dl_ref[rq, :]
            lse_st = jnp.concatenate([lse2[:, 0:1], lse2[:, HEAD_DIM:HEAD_DIM + 1]], axis=0)
            del_st = jnp.concatenate([del2[:, 0:1], del2[:, HEAD_DIM:HEAD_DIM + 1]], axis=0)
            p = jnp.exp(_dot(q_st, k_st, NT) + _tile_bias(b_ref, branch, first) - lse_st)
            ds = p * (_dot(do_st, v_st, NT) - del_st)
            db_ref[branch] += ds
            ds = ds.astype(MXU_DTYPE)
            p = p.astype(MXU_DTYPE)
            lhs = jnp.concatenate([ds[:QBLK, :QBLK], ds[QBLK:, :QBLK], ds[:QBLK, QBLK:], ds[QBLK:, QBLK:]], axis=1)
            dk_st = _dot(ds, q_st, TN)
            dv_st = _dot(p, do_st, TN)
            dq_ref[rq, :] += _dot(lhs, kh_st, NN)
            dk_ref[rq, :] += dk_st[:QBLK]
            dv_ref[rq, :] += dv_st[:QBLK]
            if prev_in_block:
                dk_ref[rp, :] += dk_st[QBLK:]
                dv_ref[rp, :] += dv_st[QBLK:]
            else:
                ek_refs[branch][rq, :] = dk_st[QBLK:]
                ev_refs[branch][rq, :] = dv_st[QBLK:]

        _for_each_tile(tile, c)

    blk = pl.BlockSpec((None, ABLK, 128), lambda ch, c: (ch, c, 0))
    outs = pl.pallas_call(
        body, name="attn_bwd", grid=(N_CHUNK, nb), in_specs=_attn_chunk_specs(nb) + [blk, blk, blk],
        out_specs=[blk] * (3 + 2 * nbr) + [pl.BlockSpec((nbr, None, BIAS_TILE, BIAS_TILE), lambda ch, c: (0, ch, 0, 0))],
        out_shape=[jax.ShapeDtypeStruct((N_CHUNK, s, 128), F32)] * (3 + 2 * nbr)
        + [jax.ShapeDtypeStruct((nbr, N_HEADS // 2, BIAS_TILE, BIAS_TILE), F32)],
        compiler_params=_params(("arbitrary", "arbitrary")),
    )(qkv_c, qkv_c, qkv_c, qkv_c, qkv_c, bias, lse_c, delta_c, do_c)
    return outs[0], outs[1], outs[2], outs[3:3 + nbr], outs[3 + nbr:3 + 2 * nbr], outs[3 + 2 * nbr]


def _attn2_bwd_sum(dq, dk, dv, ek, ev, dzs, dus):
    s = dq.shape[1]
    nrb = s // QBLK
    per_blk = ABLK // QBLK
    nbr = len(BRANCH_DIL)

    def body(*refs):
        dq_ref, dk_ref, dv_ref = refs[:3]
        ek_refs, ev_refs = refs[3:3 + nbr], refs[3 + nbr:3 + 2 * nbr]
        dzs_ref, dus_ref, o_ref = refs[3 + 2 * nbr:]
        i = pl.program_id(0)
        dkt, dvt = dk_ref[...], dv_ref[...]
        for b, dil in enumerate(BRANCH_DIL):
            j = i + dil
            ok = jnp.logical_and(j < nrb, j % per_blk < dil)
            dkt = dkt + jnp.where(ok, ek_refs[b][...], 0.0)
            dvt = dvt + jnp.where(ok, ev_refs[b][...], 0.0)
        for ch in range(N_CHUNK):
            o_ref[:, 128 * ch:128 * (ch + 1)] = (dq_ref[ch] * ATTN_SCALE).astype(o_ref.dtype)
            o_ref[:, ATTN_W + 128 * ch:ATTN_W + 128 * (ch + 1)] = dkt[ch].astype(o_ref.dtype)
            o_ref[:, 2 * ATTN_W + 128 * ch:2 * ATTN_W + 128 * (ch + 1)] = dvt[ch].astype(o_ref.dtype)
        o_ref[:, O_SGU:O_SSM] = dzs_ref[...].astype(o_ref.dtype)
        o_ref[:, O_SSM:] = dus_ref[...].astype(o_ref.dtype)

    here = pl.BlockSpec((N_CHUNK, QBLK, 128), lambda i: (0, i, 0))
    edge_specs = [pl.BlockSpec((N_CHUNK, QBLK, 128),
                               functools.partial(lambda i, d: (0, jnp.minimum(i + d, nrb - 1), 0), d=dil))
                  for dil in BRANCH_DIL]
    return pl.pallas_call(
        body, name="attn_bwd_sum", grid=(nrb,),
        in_specs=[here, here, here] + edge_specs + edge_specs
        + [pl.BlockSpec((QBLK, 2 * SGU_W), lambda i: (i, 0)), pl.BlockSpec((QBLK, SSM_W), lambda i: (i, 0))],
        out_specs=pl.BlockSpec((QBLK, O_SSM + SSM_W), lambda i: (i, 0)),
        out_shape=jax.ShapeDtypeStruct((s, O_SSM + SSM_W), MXU_DTYPE),
        compiler_params=_params(("parallel",)),
    )(dq, dk, dv, *ek, *ev, dzs, dus)


SGU_ROWS = 512


def _sgu_norm(v_g):
    mu = jnp.mean(v_g, axis=-1, keepdims=True)
    cen = v_g - mu
    var = jnp.mean(cen * cen, axis=-1, keepdims=True)
    rstd = lax.rsqrt(var + EPS)
    return cen * rstd, rstd


def _sgu_fwd(zs, ln_g, ln_b, w_mask, b_t):
    s = zs.shape[0]
    nch = SGU_ROWS // SGU_CHUNK

    def body(z_ref, g_ref, b_ref, w_ref, bt_ref, o_ref):
        gz = _gelu(z_ref[...])
        for g in range(SGU_G):
            sl = slice(SGU_GW * g, SGU_GW * (g + 1))
            u_g = gz[:, sl]
            xhat, _ = _sgu_norm(gz[:, SGU_W + SGU_GW * g:SGU_W + SGU_GW * (g + 1)])
            vn = (xhat * g_ref[:, sl] + b_ref[:, sl]).astype(MXU_DTYPE)
            wg = w_ref[g].astype(MXU_DTYPE)
            for ci in range(nch):
                rs = slice(SGU_CHUNK * ci, SGU_CHUNK * (ci + 1))
                mixed = _dot(wg, vn[rs], NN) + bt_ref[:, g:g + 1]
                o_ref[rs, sl] = u_g[rs] * mixed

    full = lambda shape: pl.BlockSpec(shape, lambda i: tuple(0 for _ in shape))
    return pl.pallas_call(
        body, name="sgu_fwd", grid=(s // SGU_ROWS,),
        in_specs=[pl.BlockSpec((SGU_ROWS, 2 * SGU_W), lambda i: (i, 0)), full((1, SGU_W)), full((1, SGU_W)),
                  full((SGU_G, SGU_CHUNK, SGU_CHUNK)), full((SGU_CHUNK, SGU_G))],
        out_specs=pl.BlockSpec((SGU_ROWS, SGU_W), lambda i: (i, 0)),
        out_shape=jax.ShapeDtypeStruct((s, SGU_W), F32),
        compiler_params=_params(("parallel",)),
    )(zs, ln_g.reshape(1, SGU_W), ln_b.reshape(1, SGU_W), w_mask, b_t)


def _sgu_bwd(zs, ln_g, ln_b, w_mask, b_t, dy):
    s = zs.shape[0]
    nch = SGU_ROWS // SGU_CHUNK

    def body(z_ref, g_ref, b_ref, w_ref, bt_ref, dy_ref, dz_ref, dg_ref, dbb_ref, dw_ref, dbt_ref):
        @pl.when(pl.program_id(0) == 0)
        def _():
            dg_ref[...] = jnp.zeros_like(dg_ref)
            dbb_ref[...] = jnp.zeros_like(dbb_ref)
            dw_ref[...] = jnp.zeros_like(dw_ref)
            dbt_ref[...] = jnp.zeros_like(dbt_ref)

        z = z_ref[...]
        gz, dgelu = _gelu_pair(z)
        dy = dy_ref[...]
        for g in range(SGU_G):
            sl = slice(SGU_GW * g, SGU_GW * (g + 1))
            sv = slice(SGU_W + SGU_GW * g, SGU_W + SGU_GW * (g + 1))
            u_g = gz[:, sl]
            xhat, rstd = _sgu_norm(gz[:, sv])
            gain = g_ref[:, sl]
            vn = (xhat * gain + b_ref[:, sl]).astype(MXU_DTYPE)
            wg = w_ref[g].astype(MXU_DTYPE)
            dy_g = dy[:, sl]
            dvn_parts = []
            for ci in range(nch):
                rs = slice(SGU_CHUNK * ci, SGU_CHUNK * (ci + 1))
                mixed = _dot(wg, vn[rs], NN) + bt_ref[:, g:g + 1]
                dz_ref[rs, sl] = (dy_g[rs] * mixed * dgelu[rs, sl]).astype(dz_ref.dtype)
                dmixed = dy_g[rs] * u_g[rs]
                dm = dmixed.astype(MXU_DTYPE)
                dvn_parts.append(_dot(wg, dm, TN))
                dw_ref[g] += _dot(dm, vn[rs], NT)
                dbt_ref[:, g:g + 1] += jnp.sum(dmixed, axis=-1, keepdims=True)
            dvn = jnp.concatenate(dvn_parts, axis=0)
            dg_ref[:, sl] += jnp.sum(dvn * xhat, axis=0, keepdims=True)
            dbb_ref[:, sl] += jnp.sum(dvn, axis=0, keepdims=True)
            dxh = dvn * gain
            dv = rstd * (dxh - jnp.mean(dxh, axis=-1, keepdims=True)
                         - xhat * jnp.mean(dxh * xhat, axis=-1, keepdims=True))
            dz_ref[:, sv] = (dv * dgelu[:, sv]).astype(dz_ref.dtype)

    full = lambda shape: pl.BlockSpec(shape, lambda i: tuple(0 for _ in shape))
    return pl.pallas_call(
        body, name="sgu_bwd", grid=(s // SGU_ROWS,),
        in_specs=[pl.BlockSpec((SGU_ROWS, 2 * SGU_W), lambda i: (i, 0)), full((1, SGU_W)), full((1, SGU_W)),
                  full((SGU_G, SGU_CHUNK, SGU_CHUNK)), full((SGU_CHUNK, SGU_G)),
                  pl.BlockSpec((SGU_ROWS, SGU_W), lambda i: (i, 0))],
        out_specs=[pl.BlockSpec((SGU_ROWS, 2 * SGU_W), lambda i: (i, 0)), full((1, SGU_W)), full((1, SGU_W)),
                   full((SGU_G, SGU_CHUNK, SGU_CHUNK)), full((SGU_CHUNK, SGU_G))],
        out_shape=[jax.ShapeDtypeStruct((s, 2 * SGU_W), MXU_DTYPE), jax.ShapeDtypeStruct((1, SGU_W), F32),
                   jax.ShapeDtypeStruct((1, SGU_W), F32), jax.ShapeDtypeStruct((SGU_G, SGU_CHUNK, SGU_CHUNK), F32),
                   jax.ShapeDtypeStruct((SGU_CHUNK, SGU_G), F32)],
        compiler_params=_params(("arbitrary",)),
    )(zs, ln_g.reshape(1, SGU_W), ln_b.reshape(1, SGU_W), w_mask, b_t, dy)


def _ssm_discretize(a_re, a_im, log_dt, b_re, b_im):
    dt = jnp.exp(log_dt)[:, None]
    mag = jnp.exp(a_re * dt)
    ab_re = mag * jnp.cos(a_im * dt)
    ab_im = mag * jnp.sin(a_im * dt)
    den = a_re * a_re + a_im * a_im
    f_re = ((ab_re - 1.0) * a_re + ab_im * a_im) / den
    f_im = (ab_im * a_re - (ab_re - 1.0) * a_im) / den
    bb_re = f_re[:, :, None] * b_re - f_im[:, :, None] * b_im
    bb_im = f_re[:, :, None] * b_im + f_im[:, :, None] * b_re
    return ab_re, ab_im, bb_re, bb_im


def _ssm_operands(a_re, a_im, log_dt, b_re, b_im, c_re, c_im):
    ab_re, ab_im, bb_re, bb_im = _ssm_discretize(a_re, a_im, log_dt, b_re, b_im)
    eye = jnp.eye(SSM_G, dtype=F32)
    b_blk = jnp.einsum("pgnc,gh->gcphn", jnp.stack([bb_re, bb_im]), eye).reshape(SSM_W, 2 * NSTATE)
    c_mat = jnp.einsum("pgcn,gh->pgnhc", jnp.stack([c_re, -c_im]), eye).reshape(2 * NSTATE, SSM_W)
    a_row = jnp.stack([ab_re.reshape(NSTATE), ab_im.reshape(NSTATE)])
    p_re, p_im = a_row[0:1], a_row[1:2]
    while p_re.shape[0] < SSM_TSEG:
        l_re, l_im = p_re[-1:], p_im[-1:]
        p_re, p_im = (jnp.concatenate([p_re, p_re * l_re - p_im * l_im]),
                      jnp.concatenate([p_im, p_re * l_im + p_im * l_re]))
    p_tab = jnp.stack([p_re, p_im])
    return b_blk.astype(MXU_DTYPE), c_mat.astype(MXU_DTYPE), a_row, p_tab


def _lane_chunks():
    return [(lo, lo + SSM_LANE_CHUNK) for lo in range(0, NSTATE, SSM_LANE_CHUNK)]


def _seg_rows(j):
    return pl.ds(pl.multiple_of(j * SSM_NSEG, SSM_NSEG), SSM_NSEG)


def _to_segments(t):
    s, w = t.shape
    return t.reshape(s // SSM_TB, SSM_NSEG, SSM_TSEG, w).transpose(0, 2, 1, 3).reshape(s, w)


def _from_segments(t):
    s, w = t.shape
    return t.reshape(s // SSM_TB, SSM_TSEG, SSM_NSEG, w).transpose(0, 2, 1, 3).reshape(s, w)


def _ssm_local_scan(buf, a_ref, *, reverse):
    ends_re, ends_im = [], []
    for lo, hi in _lane_chunks():
        are = jnp.broadcast_to(a_ref[0:1, lo:hi], (SSM_NSEG, hi - lo))
        aim = jnp.broadcast_to(a_ref[1:2, lo:hi], (SSM_NSEG, hi - lo))
        if reverse:
            aim = -aim

        def step(jj, carry, lo=lo, hi=hi, are=are, aim=aim):
            xr, xi = carry
            j = (SSM_TSEG - 1 - jj) if reverse else jj
            tr = buf[_seg_rows(j), lo:hi]
            ti = buf[_seg_rows(j), NSTATE + lo:NSTATE + hi]
            nr = are * xr - aim * xi + tr
            ni = are * xi + aim * xr + ti
            buf[_seg_rows(j), lo:hi] = nr
            buf[_seg_rows(j), NSTATE + lo:NSTATE + hi] = ni
            return nr, ni

        zero = jnp.zeros((SSM_NSEG, hi - lo), F32)
        xr, xi = lax.fori_loop(0, SSM_TSEG, step, (zero, zero), unroll=4)
        ends_re.append(xr)
        ends_im.append(xi)
    return jnp.concatenate(ends_re, axis=1), jnp.concatenate(ends_im, axis=1)


def _ssm_entry_states(ends_re, ends_im, carry_ref, p_ref, entry_ref, *, reverse):
    at_re = p_ref[0, SSM_TSEG - 1:SSM_TSEG, :]
    at_im = p_ref[1, SSM_TSEG - 1:SSM_TSEG, :]
    if reverse:
        at_im = -at_im
    cur_re = carry_ref[0:1, 0:NSTATE]
    cur_im = carry_ref[0:1, NSTATE:2 * NSTATE]
    order = range(SSM_NSEG - 1, -1, -1) if reverse else range(SSM_NSEG)
    for i in order:
        entry_ref[0, i:i + 1, 0:NSTATE] = cur_re
        entry_ref[0, i:i + 1, NSTATE:2 * NSTATE] = cur_im
        nxt_re = ends_re[i:i + 1] + at_re * cur_re - at_im * cur_im
        nxt_im = ends_im[i:i + 1] + at_re * cur_im + at_im * cur_re
        cur_re, cur_im = nxt_re, nxt_im
    carry_ref[0:1, 0:NSTATE] = cur_re
    carry_ref[0:1, NSTATE:2 * NSTATE] = cur_im


def _ssm_fixup(buf, p_ref, entry_ref, *, reverse):
    for lo, hi in _lane_chunks():
        e_re = entry_ref[0, :, lo:hi]
        e_im = entry_ref[0, :, NSTATE + lo:NSTATE + hi]

        def step(j, carry, lo=lo, hi=hi, e_re=e_re, e_im=e_im):
            jp = (SSM_TSEG - 1 - j) if reverse else j
            pr = p_ref[0, pl.ds(jp, 1), lo:hi]
            pi = p_ref[1, pl.ds(jp, 1), lo:hi]
            if reverse:
                pi = -pi
            buf[_seg_rows(j), lo:hi] = buf[_seg_rows(j), lo:hi] + pr * e_re - pi * e_im
            buf[_seg_rows(j), NSTATE + lo:NSTATE + hi] = (buf[_seg_rows(j), NSTATE + lo:NSTATE + hi]
                                                           + pr * e_im + pi * e_re)
            return carry

        lax.fori_loop(0, SSM_TSEG, step, 0, unroll=4)


def _ssm_fwd(u, ops, d_skip, glu_w, glu_b):
    b_blk, c_mat, a_row, p_tab = ops
    s = u.shape[0]
    nblk = s // SSM_TB

    def body(u_ref, bb_ref, cm_ref, a_ref, p_ref, d_ref, gw_ref, gb_ref, y_ref, entry_ref, xbuf, carry):
        @pl.when(pl.program_id(0) == 0)
        def _():
            carry[...] = jnp.zeros_like(carry)

        uu = u_ref[...]
        xbuf[...] = _dotf(uu, bb_ref[...], NN)
        ends_re, ends_im = _ssm_local_scan(xbuf, a_ref, reverse=False)
        _ssm_entry_states(ends_re, ends_im, carry, p_ref, entry_ref, reverse=False)
        _ssm_fixup(xbuf, p_ref, entry_ref, reverse=False)
        y = _dotf(xbuf[...],cm_ref[...], NN) + d_ref[...] * uu
        y2 = _gelu(y)
        gate = jax.nn.sigmoid(_dot(y2.astype(MXU_DTYPE), gw_ref[...].astype(MXU_DTYPE), NN) + gb_ref[...])
        y_ref[...] = y2 * gate

    full = lambda shape: pl.BlockSpec(shape, lambda i: tuple(0 for _ in shape))
    y_seg, entry = pl.pallas_call(
        body, name="ssm_fwd", grid=(nblk,),
        in_specs=[pl.BlockSpec((SSM_TB, SSM_W), lambda i: (i, 0)), full(b_blk.shape), full(c_mat.shape),
                  full(a_row.shape), full(p_tab.shape), full((1, SSM_W)), full((SSM_W, SSM_W)), full((1, SSM_W))],
        out_specs=[pl.BlockSpec((SSM_TB, SSM_W), lambda i: (i, 0)),
                   pl.BlockSpec((1, SSM_NSEG, 2 * NSTATE), lambda i: (i, 0, 0))],
        out_shape=[jax.ShapeDtypeStruct((s, SSM_W), F32), jax.ShapeDtypeStruct((nblk, SSM_NSEG, 2 * NSTATE), F32)],
        scratch_shapes=[pltpu.VMEM((SSM_TB, 2 * NSTATE), F32), pltpu.VMEM((SSM_NSEG, 2 * NSTATE), F32)],
        compiler_params=_params(("arbitrary",)),
    )(_to_segments(u), b_blk, c_mat, a_row, p_tab, d_skip.reshape(1, SSM_W), glu_w, glu_b.reshape(1, SSM_W))
    return _from_segments(y_seg), entry


def _ssm_bwd(u, entry, ops, d_skip, glu_w, glu_b, dout):
    b_blk, c_mat, a_row, p_tab = ops
    s = u.shape[0]
    nblk = s // SSM_TB

    def body(u_ref, en_ref, bb_ref, cm_ref, a_ref, p_ref, d_ref, gw_ref, gb_ref, do_ref,
             du_ref, dbb_ref, dcm_ref, da_ref, dd_ref, dgw_ref, dgb_ref, xbuf, gbuf, gcarry, gentry):
        @pl.when(pl.program_id(0) == 0)
        def _():
            gcarry[...] = jnp.zeros_like(gcarry)
            for r in (dbb_ref, dcm_ref, da_ref, dd_ref, dgw_ref, dgb_ref):
                r[...] = jnp.zeros_like(r)

        uu = u_ref[...]
        xbuf[...] = _dotf(uu, bb_ref[...], NN)
        _ssm_local_scan(xbuf, a_ref, reverse=False)
        _ssm_fixup(xbuf, p_ref, en_ref, reverse=False)
        y = _dotf(xbuf[...],cm_ref[...], NN) + d_ref[...] * uu
        y2, dgelu = _gelu_pair(y)
        y2m = y2.astype(MXU_DTYPE)
        gwm = gw_ref[...].astype(MXU_DTYPE)
        gate = jax.nn.sigmoid(_dot(y2m, gwm, NN) + gb_ref[...])
        dout = do_ref[...]
        dpre = dout * y2 * gate * (1.0 - gate)
        dprem = dpre.astype(MXU_DTYPE)
        dy2 = dout * gate + _dot(dprem, gwm, NT)
        dgw_ref[...] += _dot(y2m, dprem, TN)
        dgb_ref[...] += jnp.sum(dpre, axis=0, keepdims=True)
        dy = dy2 * dgelu
        dd_ref[...] += jnp.sum(dy * uu, axis=0, keepdims=True)
        dcm_ref[...] += _dotf(xbuf[...],dy, TN)
        gbuf[...] = _dotf(dy, cm_ref[...], NT)
        gs_re, gs_im = _ssm_local_scan(gbuf, a_ref, reverse=True)
        _ssm_entry_states(gs_re, gs_im, gcarry, p_ref, gentry, reverse=True)
        _ssm_fixup(gbuf, p_ref, gentry, reverse=True)
        du_ref[...] = (_dotf(gbuf[...], bb_ref[...], NT) + d_ref[...] * dy).astype(du_ref.dtype)
        dbb_ref[...] += _dotf(uu, gbuf[...], TN)
        for lo, hi in _lane_chunks():
            def step(j, carry, lo=lo, hi=hi):
                acc_re, acc_im = carry
                g_re = gbuf[_seg_rows(j), lo:hi]
                g_im = gbuf[_seg_rows(j), NSTATE + lo:NSTATE + hi]
                x_re = xbuf[_seg_rows(j - 1), lo:hi]
                x_im = xbuf[_seg_rows(j - 1), NSTATE + lo:NSTATE + hi]
                return acc_re + g_re * x_re + g_im * x_im, acc_im + g_im * x_re - g_re * x_im

            g0_re = gbuf[_seg_rows(0), lo:hi]
            g0_im = gbuf[_seg_rows(0), NSTATE + lo:NSTATE + hi]
            e_re = en_ref[0, :, lo:hi]
            e_im = en_ref[0, :, NSTATE + lo:NSTATE + hi]
            init = (g0_re * e_re + g0_im * e_im, g0_im * e_re - g0_re * e_im)
            acc_re, acc_im = lax.fori_loop(1, SSM_TSEG, step, init, unroll=4)
            da_ref[0:1, lo:hi] += jnp.sum(acc_re, axis=0, keepdims=True)
            da_ref[1:2, lo:hi] += jnp.sum(acc_im, axis=0, keepdims=True)

    full = lambda shape: pl.BlockSpec(shape, lambda i: tuple(0 for _ in shape))
    rev = pl.BlockSpec((SSM_TB, SSM_W), lambda i: (nblk - 1 - i, 0))
    outs = pl.pallas_call(
        body, name="ssm_bwd", grid=(nblk,),
        in_specs=[rev, pl.BlockSpec((1, SSM_NSEG, 2 * NSTATE), lambda i: (nblk - 1 - i, 0, 0)),
                  full(b_blk.shape), full(c_mat.shape), full(a_row.shape), full(p_tab.shape),
                  full((1, SSM_W)), full((SSM_W, SSM_W)), full((1, SSM_W)), rev],
        out_specs=[rev, full(b_blk.shape), full(c_mat.shape), full(a_row.shape), full((1, SSM_W)),
                   full((SSM_W, SSM_W)), full((1, SSM_W))],
        out_shape=[jax.ShapeDtypeStruct((s, SSM_W), MXU_DTYPE), jax.ShapeDtypeStruct(b_blk.shape, F32),
                   jax.ShapeDtypeStruct(c_mat.shape, F32), jax.ShapeDtypeStruct(a_row.shape, F32),
                   jax.ShapeDtypeStruct((1, SSM_W), F32), jax.ShapeDtypeStruct((SSM_W, SSM_W), F32),
                   jax.ShapeDtypeStruct((1, SSM_W), F32)],
        scratch_shapes=[pltpu.VMEM((SSM_TB, 2 * NSTATE), F32), pltpu.VMEM((SSM_TB, 2 * NSTATE), F32),
                        pltpu.VMEM((SSM_NSEG, 2 * NSTATE), F32), pltpu.VMEM((1, SSM_NSEG, 2 * NSTATE), F32)],
        compiler_params=_params(("arbitrary",)),
    )(_to_segments(u), entry, b_blk, c_mat, a_row, p_tab, d_skip.reshape(1, SSM_W), glu_w, glu_b.reshape(1, SSM_W),
      _to_segments(dout))
    return (_from_segments(outs[0]),) + tuple(outs[1:])


MIX_SEGS = ((0, ATTN_W), (ATTN_W, ATTN_W + SGU_W), (ATTN_W + SGU_W, D_MODEL))


def _chunks_to_rows(a_ref):
    return jnp.concatenate([a_ref[ch] for ch in range(N_CHUNK)], axis=1)


def _mix_fwd(y_attn_c, y_sgu, y_ssm, gain):
    s = y_sgu.shape[0]

    def body(a_ref, b_ref, c_ref, g_ref, o_ref):
        for x, (lo, hi) in zip((_chunks_to_rows(a_ref), b_ref[...], c_ref[...]), MIX_SEGS):
            r = lax.rsqrt(jnp.mean(x * x, axis=-1, keepdims=True) + EPS)
            o_ref[:, lo:hi] = (x * r * g_ref[:, lo:hi]).astype(o_ref.dtype)

    row = lambda w: pl.BlockSpec((ROWS, w), lambda i: (i, 0))
    return pl.pallas_call(
        body, name="mix_fwd", grid=(s // ROWS,),
        in_specs=[pl.BlockSpec((N_CHUNK, ROWS, 128), lambda i: (0, i, 0)), row(SGU_W), row(SSM_W),
                  pl.BlockSpec((1, D_MODEL), lambda i: (0, 0))],
        out_specs=row(D_MODEL), out_shape=jax.ShapeDtypeStruct((s, D_MODEL), MXU_DTYPE),
        compiler_params=_params(("parallel",)),
    )(y_attn_c, y_sgu, y_ssm, gain.reshape(1, D_MODEL))


def _mix_bwd(y_attn_c, y_sgu, y_ssm, gain, dmix):
    s = y_sgu.shape[0]

    def body(a_ref, b_ref, c_ref, g_ref, dm_ref, da_ref, dl_ref, db_ref, dc_ref, dg_ref):
        @pl.when(pl.program_id(0) == 0)
        def _():
            dg_ref[...] = jnp.zeros_like(dg_ref)

        grads = []
        for x, (lo, hi) in zip((_chunks_to_rows(a_ref), b_ref[...], c_ref[...]), MIX_SEGS):
            r = lax.rsqrt(jnp.mean(x * x, axis=-1, keepdims=True) + EPS)
            xhat = x * r
            dm = dm_ref[:, lo:hi].astype(F32)
            dg_ref[:, lo:hi] += jnp.sum(dm * xhat, axis=0, keepdims=True)
            dxh = dm * g_ref[:, lo:hi]
            grads.append(r * (dxh - xhat * jnp.mean(dxh * xhat, axis=-1, keepdims=True)))
        db_ref[...] = grads[1]
        dc_ref[...] = grads[2]
        low = lax.broadcasted_iota(jnp.int32, (ROWS, 128), 1) < HEAD_DIM
        for ch in range(N_CHUNK):
            d_c = grads[0][:, 128 * ch:128 * (ch + 1)]
            da_ref[ch] = d_c.astype(da_ref.dtype)
            prod = d_c * a_ref[ch]
            dl_ref[ch] = jnp.where(low, jnp.sum(prod[:, :HEAD_DIM], axis=-1, keepdims=True),
                                   jnp.sum(prod[:, HEAD_DIM:], axis=-1, keepdims=True))

    row = lambda w: pl.BlockSpec((ROWS, w), lambda i: (i, 0))
    vec = pl.BlockSpec((1, D_MODEL), lambda i: (0, 0))
    chunked = pl.BlockSpec((N_CHUNK, ROWS, 128), lambda i: (0, i, 0))
    return pl.pallas_call(
        body, name="mix_bwd", grid=(s // ROWS,),
        in_specs=[chunked, row(SGU_W), row(SSM_W), vec, row(D_MODEL)],
        out_specs=[chunked, chunked, row(SGU_W), row(SSM_W), vec],
        out_shape=[jax.ShapeDtypeStruct((N_CHUNK, s, 128), ATTN_IO_DTYPE), jax.ShapeDtypeStruct((N_CHUNK, s, 128), F32),
                   jax.ShapeDtypeStruct((s, SGU_W), F32), jax.ShapeDtypeStruct((s, SSM_W), F32),
                   jax.ShapeDtypeStruct((1, D_MODEL), F32)],
        compiler_params=_params(("arbitrary",)),
    )(y_attn_c, y_sgu, y_ssm, gain.reshape(1, D_MODEL), dmix)


CONV_ROWS = 256
CONV_COLS = 1408
CONV_PAIR = 2 * CONV_COLS
HALO = 16


def _interleave_ff(t):
    lead = t.shape[:-1]
    nb = D_FF // CONV_COLS
    return jnp.swapaxes(t.reshape(lead + (2, nb, CONV_COLS)), -3, -2).reshape(lead + (2 * D_FF,))


def _deinterleave_ff(t):
    lead = t.shape[:-1]
    nb = D_FF // CONV_COLS
    return jnp.swapaxes(t.reshape(lead + (nb, 2, CONV_COLS)), -3, -2).reshape(lead + (2 * D_FF,))


def _causal_taps(x, halo, first):
    patch = 8
    row = lax.broadcasted_iota(jnp.int32, (patch, x.shape[1]), 0)
    h1 = jnp.where(first, 0.0, halo[HALO - 1:HALO, :])
    h2 = jnp.where(first, 0.0, halo[HALO - 2:HALO - 1, :])
    r1 = pltpu.roll(x, 1, 0)
    r2 = pltpu.roll(x, 2, 0)
    top1 = jnp.where(row == 0, h1, r1[0:patch])
    top2 = jnp.where(row == 0, h2, jnp.where(row == 1, h1, r2[0:patch]))
    return jnp.concatenate([top1, r1[patch:]], axis=0), jnp.concatenate([top2, r2[patch:]], axis=0)


def _conv_in_specs():
    halo_idx = lambda i: jnp.maximum(i * (CONV_ROWS // HALO) - 1, 0)
    return [pl.BlockSpec((CONV_ROWS, CONV_PAIR), lambda j, i: (i, j)),
            pl.BlockSpec((HALO, CONV_PAIR), lambda j, i: (halo_idx(i), j)),
            pl.BlockSpec((3, CONV_PAIR), lambda j, i: (0, j)),
            pl.BlockSpec((1, CONV_PAIR), lambda j, i: (0, j))]


def _ffn_act_fwd(hh, conv_w, conv_b):
    s = hh.shape[0]

    def body(m_ref, h_ref, w_ref, b_ref, o_ref):
        first = pl.program_id(1) == 0
        main = m_ref[...].astype(F32)
        x1, x2 = _causal_taps(main, h_ref[...].astype(F32), first)
        conv = w_ref[0:1, :] * x2 + w_ref[1:2, :] * x1 + w_ref[2:3, :] * main + b_ref[...]
        o_ref[...] = (_gelu(conv[:, CONV_COLS:]) * conv[:, :CONV_COLS]).astype(o_ref.dtype)

    return pl.pallas_call(
        body, name="ffn_act_fwd", grid=(D_FF // CONV_COLS, s // CONV_ROWS), in_specs=_conv_in_specs(),
        out_specs=pl.BlockSpec((CONV_ROWS, CONV_COLS), lambda j, i: (i, j)),
        out_shape=jax.ShapeDtypeStruct((s, D_FF), MXU_DTYPE),
        compiler_params=_params(("parallel", "parallel")),
    )(hh, hh, conv_w, conv_b.reshape(1, -1))


def _ffn_act_bwd(hh, conv_w, conv_b, da):
    s = hh.shape[0]
    nrow = s // CONV_ROWS
    ext_rows = CONV_ROWS + HALO

    def body(m_ref, h_ref, w_ref, b_ref, nx_ref, da_ref, dan_ref, o_ref, dw_ref, db_ref):
        first = pl.program_id(1) == 0
        last = pl.program_id(1) == nrow - 1

        @pl.when(first)
        def _():
            dw_ref[...] = jnp.zeros_like(dw_ref)
            db_ref[...] = jnp.zeros_like(db_ref)

        ext = jnp.concatenate([m_ref[...].astype(F32), nx_ref[...].astype(F32)], axis=0)
        x1, x2 = _causal_taps(ext, h_ref[...].astype(F32), first)
        conv = w_ref[0:1, :] * x2 + w_ref[1:2, :] * x1 + w_ref[2:3, :] * ext + b_ref[...]
        da = jnp.concatenate([da_ref[...].astype(F32), jnp.where(last, 0.0, dan_ref[...].astype(F32))], axis=0)
        act, dact = _gelu_pair(conv[:, CONV_COLS:])
        dconv = jnp.concatenate([da * act, da * conv[:, :CONV_COLS] * dact], axis=1)
        dmain = dconv[:CONV_ROWS]
        ahead1 = pltpu.roll(dconv, ext_rows - 1, 0)[:CONV_ROWS]
        ahead2 = pltpu.roll(dconv, ext_rows - 2, 0)[:CONV_ROWS]
        o_ref[...] = (w_ref[2:3, :] * dmain + w_ref[1:2, :] * ahead1 + w_ref[0:1, :] * ahead2).astype(o_ref.dtype)
        for t, tap in enumerate((x2, x1, ext)):
            dw_ref[t:t + 1, :] += jnp.sum(dmain * tap[:CONV_ROWS], axis=0, keepdims=True)
        db_ref[...] += jnp.sum(dmain, axis=0, keepdims=True)

    nxt = lambda i: jnp.minimum((i + 1) * (CONV_ROWS // HALO), s // HALO - 1)
    return pl.pallas_call(
        body, name="ffn_act_bwd", grid=(D_FF // CONV_COLS, nrow),
        in_specs=_conv_in_specs() + [pl.BlockSpec((HALO, CONV_PAIR), lambda j, i: (nxt(i), j)),
                                     pl.BlockSpec((CONV_ROWS, CONV_COLS), lambda j, i: (i, j)),
                                     pl.BlockSpec((HALO, CONV_COLS), lambda j, i: (nxt(i), j))],
        out_specs=[pl.BlockSpec((CONV_ROWS, CONV_PAIR), lambda j, i: (i, j)),
                   pl.BlockSpec((3, CONV_PAIR), lambda j, i: (0, j)), pl.BlockSpec((1, CONV_PAIR), lambda j, i: (0, j))],
        out_shape=[jax.ShapeDtypeStruct((s, 2 * D_FF), MXU_DTYPE), jax.ShapeDtypeStruct((3, 2 * D_FF), F32),
                   jax.ShapeDtypeStruct((1, 2 * D_FF), F32)],
        compiler_params=_params(("parallel", "arbitrary")),
    )(hh, hh, conv_w, conv_b.reshape(1, -1), hh, da, da)


def _ple_weight_specs(layer):
    return [pl.BlockSpec((None, D_MODEL, D_MODEL), lambda i: (layer, 0, 0)),
            pl.BlockSpec((None, PLE_DIM, D_MODEL), lambda i: (layer, 0, 0))]


def _ple_fwd(h, gain, p, w_gate, w_proj, layer):
    s = h.shape[0]
    tm = 512

    def body(h_ref, g_ref, p_ref, wg_ref, wp_ref, o_ref, xn_ref):
        x = h_ref[...]
        xn = _rms_rows(x, g_ref[...])
        xn_ref[...] = xn
        gate = jax.nn.sigmoid(_dot(xn, wg_ref[...].astype(MXU_DTYPE), NN))
        proj = _dot(p_ref[...].astype(MXU_DTYPE), wp_ref[...].astype(MXU_DTYPE), NN)
        o_ref[...] = x + gate * proj

    row = pl.BlockSpec((tm, D_MODEL), lambda i: (i, 0))
    return pl.pallas_call(
        body, name="ple_fwd", grid=(s // tm,),
        in_specs=[row, pl.BlockSpec((1, D_MODEL), lambda i: (0, 0)), pl.BlockSpec((tm, PLE_DIM), lambda i: (i, 0))]
        + _ple_weight_specs(layer),
        out_specs=[row, row],
        out_shape=[jax.ShapeDtypeStruct((s, D_MODEL), F32), jax.ShapeDtypeStruct((s, D_MODEL), MXU_DTYPE)],
        compiler_params=_params(("parallel",)),
    )(h, gain.reshape(1, D_MODEL), p, w_gate, w_proj)


def _ple_bwd(xn, p, w_gate, w_proj, dh, layer):
    s = xn.shape[0]
    tm = 512

    def body(x_ref, p_ref, wg_ref, wp_ref, dh_ref, dpre_ref, dproj_ref):
        gate = jax.nn.sigmoid(_dot(x_ref[...].astype(MXU_DTYPE), wg_ref[...].astype(MXU_DTYPE), NN))
        proj = _dot(p_ref[...].astype(MXU_DTYPE), wp_ref[...].astype(MXU_DTYPE), NN)
        dh = dh_ref[...]
        dpre_ref[...] = (dh * proj * gate * (1.0 - gate)).astype(dpre_ref.dtype)
        dproj_ref[...] = (dh * gate).astype(dproj_ref.dtype)

    row = pl.BlockSpec((tm, D_MODEL), lambda i: (i, 0))
    return pl.pallas_call(
        body, name="ple_bwd", grid=(s // tm,),
        in_specs=[row, pl.BlockSpec((tm, PLE_DIM), lambda i: (i, 0))] + _ple_weight_specs(layer) + [row],
        out_specs=[row, row],
        out_shape=[jax.ShapeDtypeStruct((s, D_MODEL), MXU_DTYPE)] * 2,
        compiler_params=_params(("parallel",)),
    )(xn, p, w_gate, w_proj, dh)


O_SGU = 3 * ATTN_W
O_SSM = O_SGU + 2 * SGU_W


def _layer_consts(w, i):
    causal = jnp.asarray(np.tril(np.ones((SGU_CHUNK, SGU_CHUNK), np.float32)))
    return {
        "sgu_w_mask": w["sgu_w"][i] * causal,
        "sgu_b_t": w["sgu_b"][i].T,
        "ssm_ops": _ssm_operands(w["ssm_a_re"][i], w["ssm_a_im"][i], w["ssm_log_dt"][i], w["ssm_b_re"][i],
                                 w["ssm_b_im"][i], w["ssm_c_re"][i], w["ssm_c_im"][i]),
    }


def _layer_fwd(h0, p_i, w, i, bias):
    c = _layer_consts(w, i)
    xn1, qkv, zs, us = _in_proj(h0, w["norm_attn_g"][i], w["w_in"], i)
    y_attn, lse = _attn2_fwd(qkv, bias)
    y_sgu = _sgu_fwd(zs, w["sgu_ln_g"][i], w["sgu_ln_b"][i], c["sgu_w_mask"], c["sgu_b_t"])
    y_ssm, entry = _ssm_fwd(us, c["ssm_ops"], w["ssm_d"][i], w["ssm_glu_w"][i], w["ssm_glu_b"][i])
    mix = _mix_fwd(y_attn, y_sgu, y_ssm, w["branch_norm_g"][i])
    h1 = _matmul(mix, w["w_out"], name="out_proj", out_dtype=F32, tm=512, tn=1024, residual=h0, layer=i)
    xn2, hh = _ffn_up(h1, w["norm_ffn_g"][i], w["ffn_w_up"], i)
    act = _ffn_act_fwd(hh, w["ffn_conv_w"][i], w["ffn_conv_b"][i])
    h2 = _matmul(act, w["ffn_w_down"], name="ffn_down", out_dtype=F32, tm=512, tn=1024, residual=h1, layer=i)
    h3, xn3 = _ple_fwd(h2, w["norm_ple_g"][i], p_i, w["ple_w_gate"], w["ple_w_proj"], i)
    saved = dict(h0=h0, xn1=xn1, qkv=qkv, zs=zs, us=us, y_attn=y_attn, lse=lse, y_sgu=y_sgu, y_ssm=y_ssm,
                 entry=entry, mix=mix, h1=h1, xn2=xn2, hh=hh, act=act, h2=h2, xn3=xn3, consts=c)
    return h3, saved


def _layer_bwd(dh3, sv, p_i, w, i, bias):
    c = sv["consts"]
    g = {}
    dpre, dproj = _ple_bwd(sv["xn3"], p_i, w["ple_w_gate"], w["ple_w_proj"], dh3, i)
    g["ple_w_gate"] = _matmul_tn(sv["xn3"], dpre, name="d_ple_w_gate", tk=1024, tn=1024)
    g["ple_w_proj"] = _matmul_tn(p_i, dproj, name="d_ple_w_proj", tk=256, tn=1024)
    dh2, g["norm_ple_g"] = _matmul_rms_bwd(dpre, w["ple_w_gate"], sv["h2"], w["norm_ple_g"][i], dh3,
                                           name="d_xn_ple", layer=i, tm=512)
    g["ffn_w_down"] = _matmul_tn(sv["act"], dh2, name="d_ffn_w_down", tk=1408, tn=1024)
    dact = _matmul(dh2, w["ffn_w_down"], name="d_ffn_act", out_dtype=MXU_DTYPE, tm=512, tn=1408, trans_b=True, layer=i)
    dhh, g["ffn_conv_w"], g["ffn_conv_b"] = _ffn_act_bwd(sv["hh"], w["ffn_conv_w"][i], w["ffn_conv_b"][i], dact)
    g["ffn_w_up"] = _matmul_tn(sv["xn2"], dhh, name="d_ffn_w_up", tk=1024, tn=1408)
    dh1, g["norm_ffn_g"] = _matmul_rms_bwd(dhh, w["ffn_w_up"], sv["h1"], w["norm_ffn_g"][i], dh2,
                                           name="d_xn_ffn", layer=i, tm=256)
    g["w_out"] = _matmul_tn(sv["mix"], dh1, name="d_w_out", tk=1024, tn=1024)
    dmix = _matmul(dh1, w["w_out"], name="d_mix", out_dtype=F32, tm=512, tn=1024, trans_b=True, layer=i)
    dy_attn, delta, dy_sgu, dy_ssm, g["branch_norm_g"] = _mix_bwd(sv["y_attn"], sv["y_sgu"], sv["y_ssm"],
                                                                  w["branch_norm_g"][i], dmix)
    dq, dk, dv, ek, ev, dbias = _attn2_bwd(sv["qkv"], bias, sv["lse"], delta, dy_attn)
    dzs, g["sgu_ln_g"], g["sgu_ln_b"], dsw, dsb = _sgu_bwd(sv["zs"], w["sgu_ln_g"][i], w["sgu_ln_b"][i],
                                                          c["sgu_w_mask"], c["sgu_b_t"], dy_sgu)
    causal = jnp.asarray(np.tril(np.ones((SGU_CHUNK, SGU_CHUNK), np.float32)))
    g["sgu_w"] = dsw * causal
    g["sgu_b"] = dsb.T
    dus, dbb, dcm, da, g["ssm_d"], g["ssm_glu_w"], g["ssm_glu_b"] = _ssm_bwd(
        sv["us"], sv["entry"], c["ssm_ops"], w["ssm_d"][i], w["ssm_glu_w"][i], w["ssm_glu_b"][i], dy_ssm)
    dbb5 = dbb.reshape(SSM_G, SSM_C, 2, SSM_G, SSM_N)
    dbbar = jnp.einsum("gcpgn->pgnc", dbb5)
    dcm5 = dcm.reshape(2, SSM_G, SSM_N, SSM_G, SSM_C)
    dcc = jnp.einsum("pgngc->pgcn", dcm5)
    g["ssm_c_re"] = dcc[0]
    g["ssm_c_im"] = -dcc[1]
    da2 = da.reshape(2, SSM_G, SSM_N)
    _, vjp = jax.vjp(_ssm_discretize, w["ssm_a_re"][i], w["ssm_a_im"][i], w["ssm_log_dt"][i],
                     w["ssm_b_re"][i], w["ssm_b_im"][i])
    (g["ssm_a_re"], g["ssm_a_im"], g["ssm_log_dt"], g["ssm_b_re"], g["ssm_b_im"]) = vjp(
        (da2[0], da2[1], dbbar[0], dbbar[1]))
    dz = _attn2_bwd_sum(dq, dk, dv, ek, ev, dzs, dus)
    g["w_in"] = _matmul_tn(sv["xn1"], dz, name="d_w_in", tk=1024, tn=1152)
    dh0, g["norm_attn_g"] = _matmul_rms_bwd(dz, w["w_in"], sv["h0"], w["norm_attn_g"][i], dh1,
                                            name="d_xn_attn", layer=i, tm=512)
    for k in ("norm_ple_g", "norm_ffn_g", "branch_norm_g", "norm_attn_g", "sgu_ln_g", "sgu_ln_b", "ssm_d",
              "ssm_glu_b", "ffn_conv_b"):
        g[k] = g[k].reshape(-1)
    return dh0, g, dbias


def _local_step(x, p, target, w, ff_interleaved=False):
    ff_names = ("ffn_conv_b",) if ff_interleaved else FF_SHARDED + ("ffn_conv_b",)
    w = dict(w)
    for k in ff_names:
        w[k] = _interleave_ff(w[k])
    bias = _bias_build(w["rel_bias"])
    h = x
    saved = []
    for i in range(DEPTH):
        h, sv = _layer_fwd(h, p[i], w, i, bias)
        saved.append(sv)
    loss, dh, dgf = _loss_head(h, w["final_norm_g"], target)
    layer_grads = [None] * DEPTH
    dbias = None
    for i in reversed(range(DEPTH)):
        dh, layer_grads[i], db = _layer_bwd(dh, saved[i], p[i], w, i, bias)
        dbias = db if dbias is None else dbias + db
    grads = {k: jnp.stack([layer_grads[i][k] for i in range(DEPTH)]) for k in layer_grads[0]}
    for k in ff_names:
        grads[k] = _deinterleave_ff(grads[k])
    grads["rel_bias"] = _bias_reduce(dbias)
    grads["final_norm_g"] = dgf.reshape(-1)
    return loss, dh, grads


def _pad_rows(a2, mult=16):
    r = (-a2.shape[0]) % mult
    return a2 if r == 0 else jnp.concatenate([a2, jnp.zeros((r, a2.shape[1]), a2.dtype)], axis=0)


def _as_rows(a, rows=None):
    size = int(np.prod(a.shape))
    if rows is None:
        rows = -(-size // (16 * PACK_COLS)) * 16
    if size % PACK_COLS:
        a = jnp.pad(a.reshape(-1), (0, (-size) % PACK_COLS))
    a2 = a.reshape(-1, PACK_COLS)
    return jnp.pad(a2, ((0, rows - a2.shape[0]), (0, 0)))


def _shard_shape(name):
    full, ax = BIG_FULL[name]
    shp = [DEPTH] + list(full)
    shp[ax] //= N_CHIPS
    return tuple(shp)


EXACT_NAMES = ("ffn_conv_w",)


def _pack_rows_of(name):
    n = int(np.prod(_shard_shape(name))) * (2 if name in EXACT_NAMES else 1)
    rows = -(-n // PACK_COLS)
    return -(-rows // 16) * 16


def _pack_shards(shards, dtype, exact=False):
    split_words = exact and jnp.dtype(dtype).itemsize == 2
    parts = []
    for n in BIG_NAMES:
        a = shards[n]
        if split_words and n in EXACT_NAMES:
            a = lax.bitcast_convert_type(a.astype(F32), dtype)
        parts.append(_as_rows(a.astype(dtype), _pack_rows_of(n)))
    used = sum(pt.shape[0] for pt in parts)
    parts.append(jnp.zeros((PACK_ROWS - used, PACK_COLS), dtype))
    return jnp.concatenate(parts, axis=0)


def _unpack_shard(flat, name, exact=False):
    off = 0
    for n in BIG_NAMES:
        if n == name:
            break
        off += _pack_rows_of(n)
    shp = _shard_shape(name)
    cnt = int(np.prod(shp))
    if exact and name in EXACT_NAMES and jnp.dtype(flat.dtype).itemsize == 2:
        vec = flat[off:off + _pack_rows_of(name)].reshape(-1)
        return lax.bitcast_convert_type(vec[:2 * cnt].reshape(shp + (2,)), F32)
    if cnt % PACK_COLS == 0:
        return flat[off:off + cnt // PACK_COLS].reshape(shp)
    return flat[off:off + _pack_rows_of(name)].reshape(-1)[:cnt].reshape(shp)


FF_SHARDED = ("ffn_w_up", "ffn_conv_w")
FF_CHIP_ORDER = (0, 2, 1, 3)


def _chip_order(name):
    return FF_CHIP_ORDER if name in FF_SHARDED else tuple(range(N_CHIPS))


def _split_full(full, name):
    _, ax = BIG_FULL[name]
    parts = jnp.split(full, N_CHIPS, axis=ax)
    out = [None] * N_CHIPS
    for j, k in enumerate(_chip_order(name)):
        out[k] = parts[j]
    return out


def _join_shards(shards, name):
    _, ax = BIG_FULL[name]
    return jnp.concatenate([shards[k] for k in _chip_order(name)], axis=ax)


def _small_shapes(w):
    return [(n, w[n].shape) for n in SMALL_NAMES]


def _small_rows(shp):
    return -(-int(np.prod(shp)) // (8 * PACK_COLS)) * 8


def _pack_small(d):
    parts = [_as_rows(d[n].astype(F32), _small_rows(d[n].shape)) for n in SMALL_NAMES]
    used = sum(pt.shape[0] for pt in parts)
    parts.append(jnp.zeros((SMALL_ROWS - used, PACK_COLS), F32))
    return jnp.concatenate(parts, axis=0)


def _unpack_small(flat, shapes):
    out, off = {}, 0
    for n, shp in shapes:
        cnt = int(np.prod(shp))
        rows = _small_rows(shp)
        if cnt % PACK_COLS == 0:
            out[n] = flat[off:off + cnt // PACK_COLS].reshape(shp)
        else:
            out[n] = flat[off:off + rows].reshape(-1)[:cnt].reshape(shp)
        off += rows
    return out


MESH = pl.DeviceIdType.MESH
ANY = pl.BlockSpec(memory_space=pl.ANY)


def _me():
    return lax.axis_index("x"), lax.axis_index("y"), lax.axis_index("c")


def _other_chips(x, y):
    return [(1 - x, y), (x, 1 - y), (1 - x, 1 - y)]


def _gather_weights(wflat):
    def body(w_ref, out_ref, send_sems, recv_sems):
        x, y, c = _me()
        sibling = (x, y, 1 - c)
        chips = _other_chips(x, y)

        def rows(chip, half):
            return out_ref.at[2 * chip[0] + chip[1], pl.ds(half * PACK_HALF, PACK_HALF), :]

        def copy(k, chip, half, to, src=None):
            return pltpu.make_async_remote_copy(
                src_ref=rows(chip, half) if src is None else src, dst_ref=rows(chip, half),
                send_sem=send_sems.at[k], recv_sem=recv_sems.at[k], device_id=to, device_id_type=MESH)

        my_half = w_ref.at[pl.ds(c * PACK_HALF, PACK_HALF), :]
        first = [copy(j, (x, y), c, (*chip, c), src=my_half) for j, chip in enumerate(chips)]
        for cp in first:
            cp.start()
        passed = [copy(3 + j, chip, c, sibling) for j, chip in enumerate(chips)]
        for j, chip in enumerate(chips):
            copy(j, chip, c, (x, y, c)).wait_recv()
            passed[j].start()
        for j, chip in enumerate(chips):
            copy(3 + j, chip, 1 - c, (x, y, c)).wait_recv()
        for cp in first + passed:
            cp.wait_send()

    return pl.pallas_call(
        body, name="gather_weights", in_specs=[ANY], out_specs=ANY,
        out_shape=jax.ShapeDtypeStruct((N_CHIPS, PACK_ROWS, PACK_COLS), wflat.dtype),
        scratch_shapes=[pltpu.SemaphoreType.DMA((6,)), pltpu.SemaphoreType.DMA((6,))],
    )(wflat)


def _fill_own_shard(wall, wflat, chip_idx):
    rows = PACK_ROWS // 8

    def body(idx_ref, w_ref, wall_ref, o_ref):
        del idx_ref, wall_ref
        o_ref[...] = w_ref[...]

    return pl.pallas_call(
        body, name="fill_own_shard",
        grid_spec=pltpu.PrefetchScalarGridSpec(
            num_scalar_prefetch=1, grid=(PACK_ROWS // rows,),
            in_specs=[pl.BlockSpec((rows, PACK_COLS), lambda i, idx: (i, 0)), ANY],
            out_specs=pl.BlockSpec((None, rows, PACK_COLS), lambda i, idx: (idx[0], i, 0))),
        out_shape=jax.ShapeDtypeStruct(wall.shape, wall.dtype),
        input_output_aliases={2: 0},
        compiler_params=_params(("parallel",)),
    )(chip_idx, wflat, wall)


def _exchange_partials(gb, gs):
    def body(gb_ref, gs_ref, half_ref, small_ref, send_sems, recv_sems, local_sem):
        x, y, c = _me()
        me_idx = 4 * x + 2 * y + c
        mine = pltpu.make_async_copy(gs_ref, small_ref.at[me_idx], local_sem)
        mine.start()
        d2d = pltpu.make_async_remote_copy(
            src_ref=gb_ref.at[:, pl.ds((1 - c) * PACK_HALF, PACK_HALF), :], dst_ref=half_ref,
            send_sem=send_sems.at[0], recv_sem=recv_sems.at[0], device_id=(x, y, 1 - c), device_id_type=MESH)
        d2d.start()
        copies = []
        for k in range(1, N_DEV):
            fx, fy, fc = (k >> 2) & 1, (k >> 1) & 1, k & 1
            peer = (x ^ fx, y ^ fy, c ^ fc)
            copies.append(pltpu.make_async_remote_copy(
                src_ref=gs_ref, dst_ref=small_ref.at[me_idx], send_sem=send_sems.at[k], recv_sem=recv_sems.at[k],
                device_id=peer, device_id_type=MESH))
        for cp in copies:
            cp.start()
        for k in range(1, N_DEV):
            fx, fy, fc = (k >> 2) & 1, (k >> 1) & 1, k & 1
            peer_idx = 4 * (x ^ fx) + 2 * (y ^ fy) + (c ^ fc)
            pltpu.make_async_remote_copy(
                src_ref=gs_ref, dst_ref=small_ref.at[peer_idx], send_sem=send_sems.at[k], recv_sem=recv_sems.at[k],
                device_id=(x, y, c), device_id_type=MESH).wait_recv()
        d2d.wait_recv()
        d2d.wait_send()
        for cp in copies:
            cp.wait_send()
        mine.wait()

    return pl.pallas_call(
        body, name="exchange_partials", in_specs=[ANY, pl.BlockSpec(memory_space=pltpu.VMEM)], out_specs=[ANY, ANY],
        out_shape=[jax.ShapeDtypeStruct((N_CHIPS, PACK_HALF, PACK_COLS), gb.dtype),
                   jax.ShapeDtypeStruct((N_DEV, SMALL_ROWS, PACK_COLS), F32)],
        scratch_shapes=[pltpu.SemaphoreType.DMA((N_DEV,)), pltpu.SemaphoreType.DMA((N_DEV,)), pltpu.SemaphoreType.DMA],
    )(gb, gs)


RED_ROWS = 256


def _chip_partials(gb, sib, c_idx):
    nrow = PACK_HALF // RED_ROWS

    def body(c_ref, a_ref, b_ref, o_ref):
        del c_ref
        o_ref[...] = (a_ref[...].astype(F32) + b_ref[...].astype(F32)).astype(o_ref.dtype)

    blk = (1, RED_ROWS, PACK_COLS)
    return pl.pallas_call(
        body, name="chip_partials",
        grid_spec=pltpu.PrefetchScalarGridSpec(
            num_scalar_prefetch=1, grid=(N_CHIPS, nrow),
            in_specs=[pl.BlockSpec(blk, lambda k, i, c: (k, c[0] * nrow + i, 0)),
                      pl.BlockSpec(blk, lambda k, i, c: (k, i, 0))],
            out_specs=pl.BlockSpec(blk, lambda k, i, c: (k, i, 0))),
        out_shape=jax.ShapeDtypeStruct((N_CHIPS, PACK_HALF, PACK_COLS), gb.dtype),
        compiler_params=_params(("parallel", "parallel")),
    )(c_idx, gb, sib)


def _scatter_partials(pc):
    def body(pc_ref, out_ref, send_sems, recv_sems):
        x, y, c = _me()
        chips = _other_chips(x, y)
        copies = [pltpu.make_async_remote_copy(
            src_ref=pc_ref.at[2 * chip[0] + chip[1]], dst_ref=out_ref.at[k],
            send_sem=send_sems.at[k], recv_sem=recv_sems.at[k], device_id=(*chip, c), device_id_type=MESH)
            for k, chip in enumerate(chips)]
        for cp in copies:
            cp.start()
        for cp in copies:
            cp.wait_recv()
        for cp in copies:
            cp.wait_send()

    return pl.pallas_call(
        body, name="scatter_partials", in_specs=[ANY], out_specs=ANY,
        out_shape=jax.ShapeDtypeStruct((3, PACK_HALF, PACK_COLS), pc.dtype),
        scratch_shapes=[pltpu.SemaphoreType.DMA((3,)), pltpu.SemaphoreType.DMA((3,))],
    )(pc)


def _final_half(gb, sib, recv, idx):
    nrow = PACK_HALF // RED_ROWS

    def body(idx_ref, a_ref, b_ref, r_ref, o_ref):
        del idx_ref
        acc = a_ref[0].astype(F32) + b_ref[0].astype(F32)
        for k in range(3):
            acc = acc + r_ref[k].astype(F32)
        o_ref[...] = acc

    return pl.pallas_call(
        body, name="final_half",
        grid_spec=pltpu.PrefetchScalarGridSpec(
            num_scalar_prefetch=1, grid=(nrow,),
            in_specs=[pl.BlockSpec((1, RED_ROWS, PACK_COLS), lambda i, idx: (idx[0], idx[1] * nrow + i, 0)),
                      pl.BlockSpec((1, RED_ROWS, PACK_COLS), lambda i, idx: (idx[0], i, 0)),
                      pl.BlockSpec((3, RED_ROWS, PACK_COLS), lambda i, idx: (0, i, 0))],
            out_specs=pl.BlockSpec((RED_ROWS, PACK_COLS), lambda i, idx: (i, 0))),
        out_shape=jax.ShapeDtypeStruct((PACK_HALF, PACK_COLS), F32),
        compiler_params=_params(("parallel",)),
    )(idx, gb, sib, recv)


def _share_halves(half):
    def body(h_ref, out_ref, send_sem, recv_sem):
        x, y, c = _me()
        cp = pltpu.make_async_remote_copy(src_ref=h_ref, dst_ref=out_ref, send_sem=send_sem, recv_sem=recv_sem,
                                          device_id=(x, y, 1 - c), device_id_type=MESH)
        cp.start()
        cp.wait_recv()
        cp.wait_send()

    return pl.pallas_call(
        body, name="share_halves", in_specs=[ANY], out_specs=ANY,
        out_shape=jax.ShapeDtypeStruct((PACK_HALF, PACK_COLS), F32),
        scratch_shapes=[pltpu.SemaphoreType.DMA, pltpu.SemaphoreType.DMA],
    )(half)


def _sum_small(allsmall):
    def body(a_ref, o_ref):
        acc = a_ref[0]
        for k in range(1, N_DEV):
            acc = acc + a_ref[k]
        o_ref[...] = acc

    tr = 96
    return pl.pallas_call(
        body, name="sum_small", grid=(SMALL_ROWS // tr,),
        in_specs=[pl.BlockSpec((N_DEV, tr, PACK_COLS), lambda i: (0, i, 0))],
        out_specs=pl.BlockSpec((tr, PACK_COLS), lambda i: (i, 0)),
        out_shape=jax.ShapeDtypeStruct((SMALL_ROWS, PACK_COLS), F32),
        compiler_params=_params(("parallel",)),
    )(allsmall)


def _adamw(w, g, m, v, *, name):
    shape = w.shape
    cols = shape[-1]
    as2 = lambda t: t.reshape(-1, cols)
    w2, g2, m2, v2 = as2(w), as2(g), as2(m), as2(v)
    rows = w2.shape[0]
    tr = rows
    if rows * cols * 4 > (1 << 20):
        tr = _tile(rows, max(8, (1 << 20) // (cols * 4) // 8 * 8), 8)

    def body(w_ref, g_ref, m_ref, v_ref, d_ref, mo_ref, vo_ref):
        gg = g_ref[...]
        mn = ADAM_B1 * m_ref[...] + (1.0 - ADAM_B1) * gg
        vn = ADAM_B2 * v_ref[...] + (1.0 - ADAM_B2) * (gg * gg)
        m_hat = mn / (1.0 - ADAM_B1 ** ADAM_STEP)
        v_hat = vn / (1.0 - ADAM_B2 ** ADAM_STEP)
        d_ref[...] = -ADAM_LR * (m_hat / (jnp.sqrt(v_hat) + ADAM_EPS) + ADAM_WD * w_ref[...])
        mo_ref[...] = mn
        vo_ref[...] = vn

    blk = pl.BlockSpec((tr, cols), lambda i: (i, 0))
    outs = pl.pallas_call(
        body, name=name, grid=(rows // tr,), in_specs=[blk] * 4, out_specs=[blk] * 3,
        out_shape=[jax.ShapeDtypeStruct((rows, cols), F32)] * 3,
        compiler_params=_params(("parallel",)),
    )(w2, g2, m2, v2)
    return tuple(t.reshape(shape) for t in outs)


def _adamw_many(ws, gs, ms, vs):
    n = len(ws)

    def body(*refs):
        for t in range(n):
            w_ref, g_ref, m_ref, v_ref = refs[t], refs[n + t], refs[2 * n + t], refs[3 * n + t]
            d_ref, mo_ref, vo_ref = refs[4 * n + t], refs[5 * n + t], refs[6 * n + t]
            gg = g_ref[...]
            mn = ADAM_B1 * m_ref[...] + (1.0 - ADAM_B1) * gg
            vn = ADAM_B2 * v_ref[...] + (1.0 - ADAM_B2) * (gg * gg)
            m_hat = mn / (1.0 - ADAM_B1 ** ADAM_STEP)
            v_hat = vn / (1.0 - ADAM_B2 ** ADAM_STEP)
            d_ref[...] = -ADAM_LR * (m_hat / (jnp.sqrt(v_hat) + ADAM_EPS) + ADAM_WD * w_ref[...])
            mo_ref[...] = mn
            vo_ref[...] = vn

    vmem = pl.BlockSpec(memory_space=pltpu.VMEM)
    outs = pl.pallas_call(
        body, name="adamw_small", in_specs=[vmem] * (4 * n), out_specs=[vmem] * (3 * n),
        out_shape=[jax.ShapeDtypeStruct(w.shape, F32) for w in ws] * 3,
        compiler_params=pltpu.CompilerParams(vmem_limit_bytes=VMEM_LIMIT_BYTES),
    )(*ws, *gs, *ms, *vs)
    return outs[:n], outs[n:2 * n], outs[2 * n:]


def kernel(x, p, rel_bias, norm_attn_g, w_in, sgu_ln_g, sgu_ln_b, sgu_w, sgu_b, ssm_a_re, ssm_a_im, ssm_log_dt, ssm_b_re, ssm_b_im, ssm_c_re, ssm_c_im, ssm_d, ssm_glu_w, ssm_glu_b, branch_norm_g, w_out, norm_ffn_g, ffn_w_up, ffn_conv_w, ffn_conv_b, ffn_w_down, norm_ple_g, ple_w_gate, ple_w_proj, final_norm_g, loss_target, m_rel_bias, m_norm_attn_g, m_w_in, m_sgu_ln_g, m_sgu_ln_b, m_sgu_w, m_sgu_b, m_ssm_a_re, m_ssm_a_im, m_ssm_log_dt, m_ssm_b_re, m_ssm_b_im, m_ssm_c_re, m_ssm_c_im, m_ssm_d, m_ssm_glu_w, m_ssm_glu_b, m_branch_norm_g, m_w_out, m_norm_ffn_g, m_ffn_w_up, m_ffn_conv_w, m_ffn_conv_b, m_ffn_w_down, m_norm_ple_g, m_ple_w_gate, m_ple_w_proj, m_final_norm_g, v_rel_bias, v_norm_attn_g, v_w_in, v_sgu_ln_g, v_sgu_ln_b, v_sgu_w, v_sgu_b, v_ssm_a_re, v_ssm_a_im, v_ssm_log_dt, v_ssm_b_re, v_ssm_b_im, v_ssm_c_re, v_ssm_c_im, v_ssm_d, v_ssm_glu_w, v_ssm_glu_b, v_branch_norm_g, v_w_out, v_norm_ffn_g, v_ffn_w_up, v_ffn_conv_w, v_ffn_conv_b, v_ffn_w_down, v_norm_ple_g, v_ple_w_gate, v_ple_w_proj, v_final_norm_g):
    args = dict(locals())
    wts = {n: args[n] for n in WEIGHT_NAMES}
    mom_m = {n: args["m_" + n] for n in WEIGHT_NAMES}
    mom_v = {n: args["v_" + n] for n in WEIGHT_NAMES}

    xi, yi, ci = _me()
    wflat = _pack_shards({n: wts[n] for n in BIG_NAMES}, MXU_DTYPE, exact=True)
    wall = _fill_own_shard(_gather_weights(wflat), wflat, jnp.stack([2 * xi + yi]).astype(jnp.int32))
    full = dict(wts)
    for n in BIG_NAMES:
        full[n] = _join_shards([_unpack_shard(wall[k], n, exact=True) for k in range(N_CHIPS)], n)
    full["ffn_conv_w"] = full["ffn_conv_w"].astype(F32)

    loss, dx, grads = _local_step(x[0], p[:, 0], loss_target[0], full, ff_interleaved=True)
    loss = lax.psum(loss[0, 0], MESH_AXES)

    xi, yi, ci = _me()
    stacked = {n: _split_full(grads[n], n) for n in BIG_NAMES}
    gb = jnp.stack([_pack_shards({n: stacked[n][k] for n in BIG_NAMES}, MXU_DTYPE) for k in range(N_CHIPS)])
    gs = _pack_small(grads)
    sib, allsmall = _exchange_partials(gb, gs)
    pc = _chip_partials(gb, sib, jnp.stack([ci]).astype(jnp.int32))
    recv = _scatter_partials(pc)
    half = _final_half(gb, sib, recv, jnp.stack([2 * xi + yi, ci]).astype(jnp.int32))
    other = _share_halves(half)
    gflat = jnp.concatenate([jnp.where(ci == 0, half, other), jnp.where(ci == 0, other, half)], axis=0)
    gsmall = _unpack_small(_sum_small(allsmall), _small_shapes(wts))

    g_out, d_out, m_out, v_out = {}, {}, {}, {}
    for n in BIG_NAMES:
        g_out[n] = _unpack_shard(gflat, n)
        d_out[n], m_out[n], v_out[n] = _adamw(wts[n], g_out[n], mom_m[n], mom_v[n], name="adamw_" + n)
    d_sm, m_sm, v_sm = _adamw_many([wts[n] for n in SMALL_NAMES], [gsmall[n] for n in SMALL_NAMES],
                                   [mom_m[n] for n in SMALL_NAMES], [mom_v[n] for n in SMALL_NAMES])
    for t, n in enumerate(SMALL_NAMES):
        g_out[n], d_out[n], m_out[n], v_out[n] = gsmall[n], d_sm[t], m_sm[t], v_sm[t]

    return (loss, dx[None], *[g_out[n] for n in WEIGHT_NAMES], *[d_out[n] for n in WEIGHT_NAMES],
            *[m_out[n] for n in WEIGHT_NAMES], *[v_out[n] for n in WEIGHT_NAMES])
```

```python
import functools
import math

import numpy as np
import jax
import jax.numpy as jnp
from jax import lax
from jax.experimental import pallas as pl
from jax.experimental.pallas import tpu as pltpu

F32 = jnp.float32
MXU_DTYPE = jnp.bfloat16
VMEM_LIMIT_BYTES = 52 * 1024 * 1024

D_MODEL = 1024
DEPTH = 2
PLE_DIM = 256
HEAD_DIM = 64
N_HEADS = 8
ATTN_W = 512
QBLK = 128
BRANCH_DIL = (1, 4, 16)
N_BUCKETS = 32
REL_MAX_DIST = 2048
SGU_W = 256
SGU_G = 4
SGU_GW = 64
SGU_CHUNK = 128
SSM_W = 256
SSM_G = 16
SSM_C = 16
SSM_N = 64
NSTATE = SSM_G * SSM_N
D_FF = 2816
EPS = 1e-6
NEG_INF = -1e30
ATTN_SCALE = HEAD_DIM ** -0.5

ADAM_LR = 0.001
ADAM_B1 = 0.9
ADAM_B2 = 0.999
ADAM_EPS = 1e-08
ADAM_WD = 0.01
ADAM_STEP = 10

SSM_NSEG = 8
SSM_TSEG = 64
SSM_TB = SSM_NSEG * SSM_TSEG
SSM_LANE_CHUNK = 512

MESH_AXES = ("x", "y", "c")
N_CHIPS = 4
N_DEV = 8

BIG_NAMES = ("w_in", "ssm_glu_w", "w_out", "ffn_w_up", "ffn_conv_w", "ffn_w_down", "ple_w_gate", "ple_w_proj")
BIG_FULL = {
    "w_in": ((D_MODEL, 2304), 2),
    "ssm_glu_w": ((SSM_W, SSM_W), 1),
    "w_out": ((D_MODEL, D_MODEL), 1),
    "ffn_w_up": ((D_MODEL, 2 * D_FF), 2),
    "ffn_conv_w": ((3, 2 * D_FF), 2),
    "ffn_w_down": ((D_FF, D_MODEL), 1),
    "ple_w_gate": ((D_MODEL, D_MODEL), 1),
    "ple_w_proj": ((PLE_DIM, D_MODEL), 2),
}
PACK_COLS = 1024
PACK_ROWS = 6656
PACK_HALF = PACK_ROWS // 2

SMALL_NAMES = ("rel_bias", "norm_attn_g", "sgu_ln_g", "sgu_ln_b", "sgu_w", "sgu_b", "ssm_a_re", "ssm_a_im",
               "ssm_log_dt", "ssm_b_re", "ssm_b_im", "ssm_c_re", "ssm_c_im", "ssm_d", "ssm_glu_b",
               "branch_norm_g", "norm_ffn_g", "ffn_conv_b", "norm_ple_g", "final_norm_g")
SMALL_ROWS = 384

WEIGHT_NAMES = ("rel_bias", "norm_attn_g", "w_in", "sgu_ln_g", "sgu_ln_b", "sgu_w", "sgu_b", "ssm_a_re", "ssm_a_im",
                "ssm_log_dt", "ssm_b_re", "ssm_b_im", "ssm_c_re", "ssm_c_im", "ssm_d", "ssm_glu_w", "ssm_glu_b",
                "branch_norm_g", "w_out", "norm_ffn_g", "ffn_w_up", "ffn_conv_w", "ffn_conv_b", "ffn_w_down",
                "norm_ple_g", "ple_w_gate", "ple_w_proj", "final_norm_g")


def _params(sem):
    return pltpu.CompilerParams(dimension_semantics=sem, vmem_limit_bytes=VMEM_LIMIT_BYTES)


def _tile(n, cap, mult=128):
    if n <= cap:
        return n
    best = None
    for t in range(mult, cap + 1, mult):
        if n % t == 0:
            best = t
    assert best is not None, (n, cap)
    return best


def _gelu(x):
    return 0.5 * x * (1.0 + jnp.tanh(0.7978845608028654 * (x + 0.044715 * x * x * x)))


def _gelu_pair(x):
    x2 = x * x
    t = jnp.tanh(0.7978845608028654 * x * (1.0 + 0.044715 * x2))
    half = 0.5 * (1.0 + t)
    return x * half, half + 0.5 * x * (1.0 - t * t) * (0.7978845608028654 + 3.0 * 0.044715 * 0.7978845608028654 * x2)


def _dot(a, b, dims):
    return lax.dot_general(a, b, (dims, ((), ())), preferred_element_type=F32)


def _dotf(a, b, dims):
    return _dot(a.astype(MXU_DTYPE), b.astype(MXU_DTYPE), dims)


NN = ((1,), (0,))
NT = ((1,), (1,))
TN = ((0,), (0,))


def _matmul(a, b, *, name, out_dtype, tm, tn, trans_b=False, residual=None, layer=None):
    m, k = a.shape
    n = b.shape[-2] if trans_b else b.shape[-1]
    tm = _tile(m, tm, 8)
    tn = _tile(n, tn)
    dims = NT if trans_b else NN
    lead = () if layer is None else (None,)
    lidx = () if layer is None else (layer,)

    def body(*refs):
        if residual is None:
            a_ref, b_ref, o_ref = refs
        else:
            a_ref, b_ref, r_ref, o_ref = refs
        acc = _dot(a_ref[...].astype(MXU_DTYPE), b_ref[...].astype(MXU_DTYPE), dims)
        if residual is not None:
            acc = acc + r_ref[...]
        o_ref[...] = acc.astype(o_ref.dtype)

    b_spec = (pl.BlockSpec(lead + (tn, k), lambda i, j: lidx + (j, 0)) if trans_b
              else pl.BlockSpec(lead + (k, tn), lambda i, j: lidx + (0, j)))
    in_specs = [pl.BlockSpec((tm, k), lambda i, j: (i, 0)), b_spec]
    args = [a, b]
    if residual is not None:
        in_specs.append(pl.BlockSpec((tm, tn), lambda i, j: (i, j)))
        args.append(residual)
    return pl.pallas_call(
        body, name=name, grid=(m // tm, n // tn), in_specs=in_specs,
        out_specs=pl.BlockSpec((tm, tn), lambda i, j: (i, j)),
        out_shape=jax.ShapeDtypeStruct((m, n), out_dtype),
        compiler_params=_params(("parallel", "parallel")),
    )(*args)


def _matmul_tn(a, g, *, name, tk, tn, tm=512):
    m, k = a.shape
    n = g.shape[1]
    tk = _tile(k, tk)
    tn = _tile(n, tn)
    tm = _tile(m, tm, 8)

    def body(a_ref, g_ref, o_ref):
        @pl.when(pl.program_id(2) == 0)
        def _():
            o_ref[...] = jnp.zeros_like(o_ref)

        o_ref[...] += _dot(a_ref[...].astype(MXU_DTYPE), g_ref[...].astype(MXU_DTYPE), TN)

    return pl.pallas_call(
        body, name=name, grid=(k // tk, n // tn, m // tm),
        in_specs=[pl.BlockSpec((tm, tk), lambda i, j, s: (s, i)),
                  pl.BlockSpec((tm, tn), lambda i, j, s: (s, j))],
        out_specs=pl.BlockSpec((tk, tn), lambda i, j, s: (i, j)),
        out_shape=jax.ShapeDtypeStruct((k, n), F32),
        compiler_params=_params(("parallel", "parallel", "arbitrary")),
    )(a, g)


ROWS = 512


def _rms_fwd(h, g, *, name):
    s, d = h.shape

    def body(h_ref, g_ref, o_ref):
        x = h_ref[...]
        r = lax.rsqrt(jnp.mean(x * x, axis=-1, keepdims=True) + EPS)
        o_ref[...] = (x * r * g_ref[...]).astype(o_ref.dtype)

    return pl.pallas_call(
        body, name=name, grid=(s // ROWS,),
        in_specs=[pl.BlockSpec((ROWS, d), lambda i: (i, 0)), pl.BlockSpec((1, d), lambda i: (0, 0))],
        out_specs=pl.BlockSpec((ROWS, d), lambda i: (i, 0)),
        out_shape=jax.ShapeDtypeStruct((s, d), MXU_DTYPE),
        compiler_params=_params(("parallel",)),
    )(h, g.reshape(1, d))


def _rms_bwd(h, g, dxn, dres, *, name):
    s, d = h.shape

    def body(h_ref, g_ref, dxn_ref, dres_ref, dh_ref, dg_ref):
        @pl.when(pl.program_id(0) == 0)
        def _():
            dg_ref[...] = jnp.zeros_like(dg_ref)

        x = h_ref[...]
        r = lax.rsqrt(jnp.mean(x * x, axis=-1, keepdims=True) + EPS)
        xhat = x * r
        dxn = dxn_ref[...].astype(F32)
        dg_ref[...] += jnp.sum(dxn * xhat, axis=0, keepdims=True)
        dxh = dxn * g_ref[...]
        dh_ref[...] = dres_ref[...] + r * (dxh - xhat * jnp.mean(dxh * xhat, axis=-1, keepdims=True))

    row = pl.BlockSpec((ROWS, d), lambda i: (i, 0))
    vec = pl.BlockSpec((1, d), lambda i: (0, 0))
    return pl.pallas_call(
        body, name=name, grid=(s // ROWS,), in_specs=[row, vec, row, row], out_specs=[row, vec],
        out_shape=[jax.ShapeDtypeStruct((s, d), F32), jax.ShapeDtypeStruct((1, d), F32)],
        compiler_params=_params(("arbitrary",)),
    )(h, g.reshape(1, d), dxn, dres)


def _matmul_rms_bwd(a, b, h, g, dres, *, name, layer, tm):
    s, k = a.shape
    d = b.shape[-2]

    def body(a_ref, b_ref, h_ref, g_ref, dres_ref, dh_ref, dg_ref):
        @pl.when(pl.program_id(0) == 0)
        def _():
            dg_ref[...] = jnp.zeros_like(dg_ref)

        dxn = _dot(a_ref[...].astype(MXU_DTYPE), b_ref[...].astype(MXU_DTYPE), NT)
        x = h_ref[...]
        r = lax.rsqrt(jnp.mean(x * x, axis=-1, keepdims=True) + EPS)
        xhat = x * r
        dg_ref[...] += jnp.sum(dxn * xhat, axis=0, keepdims=True)
        dxh = dxn * g_ref[...]
        dh_ref[...] = dres_ref[...] + r * (dxh - xhat * jnp.mean(dxh * xhat, axis=-1, keepdims=True))

    row = pl.BlockSpec((tm, d), lambda i: (i, 0))
    vec = pl.BlockSpec((1, d), lambda i: (0, 0))
    return pl.pallas_call(
        body, name=name, grid=(s // tm,),
        in_specs=[pl.BlockSpec((tm, k), lambda i: (i, 0)), pl.BlockSpec((None, d, k), lambda i: (layer, 0, 0)),
                  row, vec, row],
        out_specs=[row, vec],
        out_shape=[jax.ShapeDtypeStruct((s, d), F32), jax.ShapeDtypeStruct((1, d), F32)],
        compiler_params=_params(("arbitrary",)),
    )(a, b, h, g.reshape(1, d), dres)


def _loss_head(h, g, target):
    s, d = h.shape

    def body(h_ref, g_ref, t_ref, loss_ref, dh_ref, dg_ref):
        @pl.when(pl.program_id(0) == 0)
        def _():
            loss_ref[...] = jnp.zeros_like(loss_ref)
            dg_ref[...] = jnp.zeros_like(dg_ref)

        x = h_ref[...]
        r = lax.rsqrt(jnp.mean(x * x, axis=-1, keepdims=True) + EPS)
        xhat = x * r
        err = xhat * g_ref[...] - t_ref[...]
        loss_ref[...] += 0.5 * jnp.sum(jnp.mean(err * err, axis=-1, keepdims=True), axis=0, keepdims=True)
        dy = err / d
        dg_ref[...] += jnp.sum(dy * xhat, axis=0, keepdims=True)
        dxh = dy * g_ref[...]
        dh_ref[...] = r * (dxh - xhat * jnp.mean(dxh * xhat, axis=-1, keepdims=True))

    row = pl.BlockSpec((ROWS, d), lambda i: (i, 0))
    vec = pl.BlockSpec((1, d), lambda i: (0, 0))
    one = pl.BlockSpec((1, 1), lambda i: (0, 0))
    return pl.pallas_call(
        body, name="loss_head", grid=(s // ROWS,), in_specs=[row, vec, row], out_specs=[one, row, vec],
        out_shape=[jax.ShapeDtypeStruct((1, 1), F32), jax.ShapeDtypeStruct((s, d), F32),
                   jax.ShapeDtypeStruct((1, d), F32)],
        compiler_params=_params(("arbitrary",)),
    )(h, g.reshape(1, d), target)


def _t5_bucket(dist):
    max_exact = N_BUCKETS // 2
    dd = np.maximum(dist, 0)
    large = max_exact + (np.log(np.maximum(dd, 1) / max_exact) / np.log(REL_MAX_DIST / max_exact)
                         * (N_BUCKETS - max_exact)).astype(np.int32)
    large = np.minimum(large, N_BUCKETS - 1)
    return np.where(dd < max_exact, dd, large).astype(np.int32)


def _bucket_table():
    qq = np.arange(QBLK)[:, None]
    kk = np.arange(QBLK)[None, :]
    out = np.zeros((len(BRANCH_DIL), 2, QBLK, QBLK), np.int32)
    for b, dil in enumerate(BRANCH_DIL):
        out[b, 0] = _t5_bucket((qq - kk + QBLK) * dil)
        out[b, 1] = _t5_bucket((qq - kk) * dil)
    return out


BIAS_TILE = 2 * QBLK


def _bias_build(rel_bias):
    idx = jnp.asarray(_bucket_table())

    def body(idx_ref, rb_ref, o_ref):
        ch = pl.program_id(1)
        row = lax.broadcasted_iota(jnp.int32, (QBLK, QBLK), 0)
        col = lax.broadcasted_iota(jnp.int32, (QBLK, QBLK), 1)
        for part in range(2):
            ids = idx_ref[0, 1 - part]
            valid = (col <= row) if part == 0 else (col >= row)
            for h in range(2):
                acc = jnp.zeros((QBLK, QBLK), F32)
                for b in range(N_BUCKETS):
                    acc = jnp.where(ids == b, rb_ref[b, 2 * ch + h], acc)
                o_ref[0, 0, QBLK * h:QBLK * (h + 1), QBLK * part:QBLK * (part + 1)] = jnp.where(valid, acc, NEG_INF)

    return pl.pallas_call(
        body, name="attn_bias_build", grid=(len(BRANCH_DIL), N_HEADS // 2),
        in_specs=[pl.BlockSpec((1, 2, QBLK, QBLK), lambda b, c: (b, 0, 0, 0)),
                  pl.BlockSpec(memory_space=pltpu.SMEM)],
        out_specs=pl.BlockSpec((1, 1, BIAS_TILE, BIAS_TILE), lambda b, c: (b, c, 0, 0)),
        out_shape=jax.ShapeDtypeStruct((len(BRANCH_DIL), N_HEADS // 2, BIAS_TILE, BIAS_TILE), F32),
        compiler_params=_params(("parallel", "parallel")),
    )(idx, rel_bias)


def _bias_reduce(dbias):
    idx = jnp.asarray(_bucket_table())
    nb = len(BRANCH_DIL)

    def body(idx_ref, d_ref, o_ref):
        def per_bucket(b, carry):
            for h in range(N_HEADS):
                tot = jnp.zeros((), F32)
                for br in range(nb):
                    for part in range(2):
                        tile = d_ref[br, h // 2, QBLK * (h % 2):QBLK * (h % 2 + 1), QBLK * part:QBLK * (part + 1)]
                        tot = tot + jnp.sum(jnp.where(idx_ref[br, 1 - part] == b, tile, 0.0))
                o_ref[b, h] = tot
            return carry

        lax.fori_loop(0, N_BUCKETS, per_bucket, 0)

    return pl.pallas_call(
        body, name="attn_bias_reduce",
        in_specs=[pl.BlockSpec(memory_space=pltpu.VMEM), pl.BlockSpec(memory_space=pltpu.VMEM)],
        out_specs=pl.BlockSpec(memory_space=pltpu.SMEM),
        out_shape=jax.ShapeDtypeStruct((N_BUCKETS, N_HEADS), F32),
        compiler_params=pltpu.CompilerParams(vmem_limit_bytes=VMEM_LIMIT_BYTES),
    )(idx, dbias)


def _band_masks(c):
    row = lax.broadcasted_iota(jnp.int32, (QBLK, QBLK), 0)
    col = lax.broadcasted_iota(jnp.int32, (QBLK, QBLK), 1)
    mask_cur = col <= row
    mask_prev = jnp.logical_and(col >= row, c > 0)
    return mask_prev, mask_cur


def _attn_specs(dil):
    blk = (QBLK, ATTN_W)
    q = pl.BlockSpec(blk, lambda r, c: (c, 3 * r))
    kp = pl.BlockSpec(blk, lambda r, c: (jnp.maximum(c - 1, 0), 3 * r + 1))
    kc = pl.BlockSpec(blk, lambda r, c: (c, 3 * r + 1))
    vp = pl.BlockSpec(blk, lambda r, c: (jnp.maximum(c - 1, 0), 3 * r + 2))
    vc = pl.BlockSpec(blk, lambda r, c: (c, 3 * r + 2))
    return [q, kp, kc, vp, vc]


def _attn_fwd_branch(qkv, bias, state, *, branch, last):
    dil = BRANCH_DIL[branch]
    s = qkv.shape[0]
    n = s // dil
    nblk = n // QBLK
    first = state is None

    def body(*refs):
        q_ref, kp_ref, kc_ref, vp_ref, vc_ref, b_ref = refs[:6]
        if first:
            outs = refs[6:]
        else:
            acc_ref, m_ref, l_ref = refs[6:9]
            outs = refs[9:]
        mask_prev, mask_cur = _band_masks(pl.program_id(1))
        for h in range(N_HEADS):
            sl = slice(HEAD_DIM * h, HEAD_DIM * (h + 1))
            qh = q_ref[:, sl]
            s_c = _dot(qh, kc_ref[:, sl], NT) * ATTN_SCALE + b_ref[0, 1, h]
            s_p = _dot(qh, kp_ref[:, sl], NT) * ATTN_SCALE + b_ref[0, 0, h]
            s_c = jnp.where(mask_cur, s_c, NEG_INF)
            s_p = jnp.where(mask_prev, s_p, NEG_INF)
            m_blk = jnp.maximum(jnp.max(s_c, axis=-1, keepdims=True), jnp.max(s_p, axis=-1, keepdims=True))
            if first:
                m_new = m_blk
            else:
                m_old = m_ref[:, sl][:, :1]
                m_new = jnp.maximum(m_old, m_blk)
            p_c = jnp.exp(s_c - m_new)
            p_p = jnp.exp(s_p - m_new)
            l_new = jnp.sum(p_c, axis=-1, keepdims=True) + jnp.sum(p_p, axis=-1, keepdims=True)
            acc = (_dot(p_c.astype(MXU_DTYPE), vc_ref[:, sl], NN)
                   + _dot(p_p.astype(MXU_DTYPE), vp_ref[:, sl], NN))
            if not first:
                alpha = jnp.exp(m_old - m_new)
                l_new = l_new + alpha * l_ref[:, sl][:, :1]
                acc = acc + alpha * acc_ref[:, sl]
            if last:
                outs[0][:, sl] = acc / l_new
                outs[1][:, sl] = jnp.broadcast_to(m_new + jnp.log(l_new), (QBLK, HEAD_DIM))
            else:
                outs[0][:, sl] = acc
                outs[1][:, sl] = jnp.broadcast_to(m_new, (QBLK, HEAD_DIM))
                outs[2][:, sl] = jnp.broadcast_to(l_new, (QBLK, HEAD_DIM))

    st_spec = pl.BlockSpec((QBLK, ATTN_W), lambda r, c: (c, r))
    in_specs = _attn_specs(dil) + [pl.BlockSpec((1, 2, N_HEADS, QBLK, QBLK), lambda r, c: (branch, 0, 0, 0, 0))]
    qv = qkv.reshape(n, dil * 3 * ATTN_W)
    args = [qv] * 5 + [bias]
    if not first:
        in_specs += [st_spec] * 3
        args += [t.reshape(n, dil * ATTN_W) for t in state]
    n_out = 2 if last else 3
    outs = pl.pallas_call(
        body, name=f"attn_fwd_b{branch}", grid=(dil, nblk), in_specs=in_specs,
        out_specs=[st_spec] * n_out,
        out_shape=[jax.ShapeDtypeStruct((n, dil * ATTN_W), F32)] * n_out,
        compiler_params=_params(("parallel", "parallel")),
    )(*args)
    return tuple(t.reshape(s, ATTN_W) for t in outs)


def _attn_fwd(qkv, bias):
    state = None
    for b in range(len(BRANCH_DIL)):
        state = _attn_fwd_branch(qkv, bias, state, branch=b, last=(b == len(BRANCH_DIL) - 1))
    return state


def _attn_bwd_branch(qkv, bias, o, lse, do, *, branch):
    dil = BRANCH_DIL[branch]
    s = qkv.shape[0]
    n = s // dil
    nblk = n // QBLK

    def body(q_ref, kp_ref, kc_ref, vp_ref, vc_ref, b_ref, o_ref, l_ref, do_ref,
             dq_ref, dka_ref, dkb_ref, dva_ref, dvb_ref, db_ref):
        @pl.when(jnp.logical_and(pl.program_id(0) == 0, pl.program_id(1) == 0))
        def _():
            db_ref[...] = jnp.zeros_like(db_ref)

        mask_prev, mask_cur = _band_masks(pl.program_id(1))
        for h in range(N_HEADS):
            sl = slice(HEAD_DIM * h, HEAD_DIM * (h + 1))
            qh = q_ref[:, sl]
            doh = do_ref[:, sl]
            lh = l_ref[:, sl][:, :1]
            delta = jnp.sum(doh * o_ref[:, sl], axis=-1, keepdims=True)
            do_m = doh.astype(MXU_DTYPE)
            s_c = _dot(qh, kc_ref[:, sl], NT) * ATTN_SCALE + b_ref[0, 1, h]
            s_p = _dot(qh, kp_ref[:, sl], NT) * ATTN_SCALE + b_ref[0, 0, h]
            p_c = jnp.exp(jnp.where(mask_cur, s_c, NEG_INF) - lh)
            p_p = jnp.exp(jnp.where(mask_prev, s_p, NEG_INF) - lh)
            ds_c = p_c * (_dot(do_m, vc_ref[:, sl], NT) - delta)
            ds_p = p_p * (_dot(do_m, vp_ref[:, sl], NT) - delta)
            db_ref[0, 1, h] += ds_c
            db_ref[0, 0, h] += ds_p
            ds_c_m = ds_c.astype(MXU_DTYPE)
            ds_p_m = ds_p.astype(MXU_DTYPE)
            dq = _dot(ds_c_m, kc_ref[:, sl], NN) + _dot(ds_p_m, kp_ref[:, sl], NN)
            dq_ref[:, sl] = (dq * ATTN_SCALE).astype(dq_ref.dtype)
            dka_ref[:, sl] = (_dot(ds_c_m, qh, TN) * ATTN_SCALE).astype(dka_ref.dtype)
            dkb_ref[:, sl] = (_dot(ds_p_m, qh, TN) * ATTN_SCALE).astype(dkb_ref.dtype)
            dva_ref[:, sl] = _dot(p_c.astype(MXU_DTYPE), do_m, TN).astype(dva_ref.dtype)
            dvb_ref[:, sl] = _dot(p_p.astype(MXU_DTYPE), do_m, TN).astype(dvb_ref.dtype)

    st_spec = pl.BlockSpec((QBLK, ATTN_W), lambda r, c: (c, r))
    b_in = pl.BlockSpec((1, 2, N_HEADS, QBLK, QBLK), lambda r, c: (branch, 0, 0, 0, 0))
    b_out = pl.BlockSpec((1, 2, N_HEADS, QBLK, QBLK), lambda r, c: (0, 0, 0, 0, 0))
    qv = qkv.reshape(n, dil * 3 * ATTN_W)
    view = lambda t: t.reshape(n, dil * ATTN_W)
    outs = pl.pallas_call(
        body, name=f"attn_bwd_b{branch}", grid=(dil, nblk),
        in_specs=_attn_specs(dil) + [b_in, st_spec, st_spec, st_spec],
        out_specs=[st_spec] * 5 + [b_out],
        out_shape=[jax.ShapeDtypeStruct((n, dil * ATTN_W), MXU_DTYPE)] * 5
        + [jax.ShapeDtypeStruct((1, 2, N_HEADS, QBLK, QBLK), F32)],
        compiler_params=_params(("arbitrary", "arbitrary")),
    )(qv, qv, qv, qv, qv, bias, view(o), view(lse), view(do))
    return tuple(t.reshape(s, ATTN_W) for t in outs[:5]) + (outs[5],)


def _attn_bwd(qkv, bias, o, lse, do):
    s = qkv.shape[0]
    nb = s // QBLK
    parts = [_attn_bwd_branch(qkv, bias, o, lse, do, branch=b) for b in range(len(BRANCH_DIL))]
    dbias = jnp.concatenate([p[5] for p in parts], axis=0)

    def body(*refs):
        o_ref = refs[-1]
        i = pl.program_id(0)
        dq = jnp.zeros((QBLK, ATTN_W), F32)
        dk = jnp.zeros((QBLK, ATTN_W), F32)
        dv = jnp.zeros((QBLK, ATTN_W), F32)
        for b, dil in enumerate(BRANCH_DIL):
            dq_ref, dka_ref, dkb_ref, dva_ref, dvb_ref = refs[5 * b:5 * b + 5]
            inside = i + dil < nb
            dq = dq + dq_ref[...].astype(F32)
            dk = dk + dka_ref[...].astype(F32) + jnp.where(inside, dkb_ref[...].astype(F32), 0.0)
            dv = dv + dva_ref[...].astype(F32) + jnp.where(inside, dvb_ref[...].astype(F32), 0.0)
        o_ref[:, 0:ATTN_W] = dq.astype(o_ref.dtype)
        o_ref[:, ATTN_W:2 * ATTN_W] = dk.astype(o_ref.dtype)
        o_ref[:, 2 * ATTN_W:3 * ATTN_W] = dv.astype(o_ref.dtype)

    in_specs, args = [], []
    for b, dil in enumerate(BRANCH_DIL):
        here = pl.BlockSpec((QBLK, ATTN_W), lambda i: (i, 0))
        ahead = pl.BlockSpec((QBLK, ATTN_W), functools.partial(lambda i, d: (jnp.minimum(i + d, nb - 1), 0), d=dil))
        in_specs += [here, here, ahead, here, ahead]
        args += list(parts[b][:5])
    dqkv = pl.pallas_call(
        body, name="attn_bwd_sum", grid=(nb,), in_specs=in_specs,
        out_specs=pl.BlockSpec((QBLK, 3 * ATTN_W), lambda i: (i, 0)),
        out_shape=jax.ShapeDtypeStruct((s, 3 * ATTN_W), MXU_DTYPE),
        compiler_params=_params(("parallel",)),
    )(*args)
    return dqkv, dbias


ATTN_IO_DTYPE = F32
ABLK = 2048
N_CHUNK = ATTN_W // 128


def _rows(start, dil):
    if dil > 1:
        return pl.ds(start, QBLK, stride=dil)
    return pl.ds(pl.multiple_of(start, QBLK), QBLK)


def _low_head():
    return lax.broadcasted_iota(jnp.int32, (QBLK, 128), 1) < HEAD_DIM


def _head_split(t):
    low = _low_head()
    zero = jnp.zeros_like(t)
    return jnp.where(low, t, zero), jnp.where(low, zero, t)


def _tile_bias(b_ref, branch, first):
    bias = b_ref[branch]
    if first is None:
        return bias
    col = lax.broadcasted_iota(jnp.int32, (BIAS_TILE, BIAS_TILE), 1)
    return jnp.where(jnp.logical_and(first, col >= QBLK), NEG_INF, bias)


def _loop(n, fn):
    if n == 1:
        fn(jnp.int32(0), 0)
    elif n > 1:
        lax.fori_loop(0, n, fn, 0, unroll=2)


def _for_each_tile(tile, c):
    for branch, dil in enumerate(BRANCH_DIL):
        span = QBLK * dil

        def edge(r, carry, branch=branch, span=span):
            tile(branch, r, False, ABLK - span + r, c == 0)
            return carry

        def inner(t, carry, branch=branch, span=span, dil=dil):
            start = (1 + t // dil) * span + t % dil
            tile(branch, start, True, start - span, None)
            return carry

        _loop(dil, edge)
        _loop((ABLK // span - 1) * dil, inner)


def _attn_chunk_specs(nb):
    blk = (None, ABLK, 128)
    prev = lambda c: jnp.maximum(c - 1, 0)
    return [pl.BlockSpec(blk, lambda ch, c: (ch, c, 0)),
            pl.BlockSpec(blk, lambda ch, c: (N_CHUNK + ch, c, 0)),
            pl.BlockSpec(blk, lambda ch, c: (2 * N_CHUNK + ch, c, 0)),
            pl.BlockSpec(blk, lambda ch, c: (N_CHUNK + ch, prev(c), 0)),
            pl.BlockSpec(blk, lambda ch, c: (2 * N_CHUNK + ch, prev(c), 0)),
            pl.BlockSpec((len(BRANCH_DIL), None, BIAS_TILE, BIAS_TILE), lambda ch, c: (0, ch, 0, 0))]


def _rms_rows(x, g):
    r = lax.rsqrt(jnp.mean(x * x, axis=-1, keepdims=True) + EPS)
    return (x * r * g).astype(MXU_DTYPE)


def _in_proj(h, gain, w_in, layer):
    s, k = h.shape
    tm = 512
    nch = O_SGU // 128

    def body(h_ref, g_ref, w_ref, xn_ref, qkv_ref, zs_ref, us_ref):
        xn = _rms_rows(h_ref[...], g_ref[...])
        xn_ref[...] = xn
        acc = _dot(xn, w_ref[...].astype(MXU_DTYPE), NN)
        for j in range(nch):
            blk = acc[:, 128 * j:128 * (j + 1)]
            if j < N_CHUNK:
                blk = blk * ATTN_SCALE
            qkv_ref[j] = blk.astype(qkv_ref.dtype)
        zs_ref[...] = acc[:, O_SGU:O_SSM]
        us_ref[...] = acc[:, O_SSM:]

    n = w_in.shape[-1]
    return pl.pallas_call(
        body, name="in_proj", grid=(s // tm,),
        in_specs=[pl.BlockSpec((tm, k), lambda i: (i, 0)), pl.BlockSpec((1, k), lambda i: (0, 0)),
                  pl.BlockSpec((None, k, n), lambda i: (layer, 0, 0))],
        out_specs=[pl.BlockSpec((tm, k), lambda i: (i, 0)), pl.BlockSpec((nch, tm, 128), lambda i: (0, i, 0)),
                   pl.BlockSpec((tm, O_SSM - O_SGU), lambda i: (i, 0)), pl.BlockSpec((tm, n - O_SSM), lambda i: (i, 0))],
        out_shape=[jax.ShapeDtypeStruct((s, k), MXU_DTYPE), jax.ShapeDtypeStruct((nch, s, 128), ATTN_IO_DTYPE),
                   jax.ShapeDtypeStruct((s, O_SSM - O_SGU), F32), jax.ShapeDtypeStruct((s, n - O_SSM), F32)],
        compiler_params=_params(("parallel",)),
    )(h, gain.reshape(1, k), w_in)


def _ffn_up(h, gain, w_up, layer):
    s, k = h.shape
    n = w_up.shape[-1]
    tm, tn = 1024, CONV_COLS

    def body(h_ref, g_ref, w_ref, xn_ref, o_ref):
        @pl.when(pl.program_id(1) == 0)
        def _():
            xn_ref[...] = _rms_rows(h_ref[...], g_ref[...])

        o_ref[...] = _dot(xn_ref[...], w_ref[...].astype(MXU_DTYPE), NN).astype(o_ref.dtype)

    return pl.pallas_call(
        body, name="ffn_up", grid=(s // tm, n // tn),
        in_specs=[pl.BlockSpec((tm, k), lambda i, j: (i, 0)), pl.BlockSpec((1, k), lambda i, j: (0, 0)),
                  pl.BlockSpec((None, k, tn), lambda i, j: (layer, 0, j))],
        out_specs=[pl.BlockSpec((tm, k), lambda i, j: (i, 0)), pl.BlockSpec((tm, tn), lambda i, j: (i, j))],
        out_shape=[jax.ShapeDtypeStruct((s, k), MXU_DTYPE), jax.ShapeDtypeStruct((s, n), MXU_DTYPE)],
        compiler_params=_params(("parallel", "arbitrary")),
    )(h, gain.reshape(1, k), w_up)


def _attn2_fwd(qkv_c, bias):
    s = qkv_c.shape[1]
    nb = s // ABLK
    last = len(BRANCH_DIL) - 1

    def body(q_ref, kc_ref, vc_ref, kp_ref, vp_ref, b_ref, o_ref, l_ref, acc_s, m_s, l_s):
        low = _low_head()
        e_st = jnp.concatenate(_head_split(jnp.ones((QBLK, 128), MXU_DTYPE)) * 2, axis=0)

        def tile(branch, start, prev_in_block, pstart, first):
            dil = BRANCH_DIL[branch]
            rq, rp = _rows(start, dil), _rows(pstart, dil)
            k_ref, v_ref = (kc_ref, vc_ref) if prev_in_block else (kp_ref, vp_ref)
            q_st = jnp.concatenate(_head_split(q_ref[rq, :].astype(MXU_DTYPE)), axis=0)
            k_st = jnp.concatenate([kc_ref[rq, :].astype(MXU_DTYPE), k_ref[rp, :].astype(MXU_DTYPE)], axis=0)
            v_st = jnp.concatenate(_head_split(vc_ref[rq, :].astype(MXU_DTYPE))
                                   + _head_split(v_ref[rp, :].astype(MXU_DTYPE)), axis=0)
            sc = _dot(q_st, k_st, NT) + _tile_bias(b_ref, branch, first)
            m_new = jnp.max(sc, axis=-1, keepdims=True)
            if branch > 0:
                m_old2 = m_s[rq, :]
                m_old = jnp.concatenate([m_old2[:, 0:1], m_old2[:, HEAD_DIM:HEAD_DIM + 1]], axis=0)
                m_new = jnp.maximum(m_old, m_new)
                alpha = jnp.exp(m_old - m_new)
            p = jnp.exp(sc - m_new).astype(MXU_DTYPE)
            lhs = jnp.concatenate([p[:QBLK, :QBLK], p[QBLK:, :QBLK], p[:QBLK, QBLK:], p[QBLK:, QBLK:]], axis=1)
            acc2 = _dot(lhs, v_st, NN)
            sum2 = _dot(lhs, e_st, NN)
            m2 = jnp.where(low, m_new[:QBLK], m_new[QBLK:])
            if branch > 0:
                a2 = jnp.where(low, alpha[:QBLK], alpha[QBLK:])
                acc2 = acc2 + a2 * acc_s[rq, :]
                sum2 = sum2 + a2 * l_s[rq, :]
            if branch == last:
                o_ref[rq, :] = acc2 / sum2
                l_ref[rq, :] = m2 + jnp.log(sum2)
            else:
                acc_s[rq, :] = acc2
                m_s[rq, :] = m2
                l_s[rq, :] = sum2

        _for_each_tile(tile, pl.program_id(1))

    out_spec = pl.BlockSpec((None, ABLK, 128), lambda ch, c: (ch, c, 0))
    return pl.pallas_call(
        body, name="attn_fwd", grid=(N_CHUNK, nb), in_specs=_attn_chunk_specs(nb),
        out_specs=[out_spec, out_spec],
        out_shape=[jax.ShapeDtypeStruct((N_CHUNK, s, 128), F32)] * 2,
        scratch_shapes=[pltpu.VMEM((ABLK, 128), F32)] * 3,
        compiler_params=_params(("parallel", "arbitrary")),
    )(qkv_c, qkv_c, qkv_c, qkv_c, qkv_c, bias)


def _attn2_bwd(qkv_c, bias, lse_c, delta_c, do_c):
    s = qkv_c.shape[1]
    nb = s // ABLK
    nbr = len(BRANCH_DIL)

    def body(q_ref, kc_ref, vc_ref, kp_ref, vp_ref, b_ref, l_ref, dl_ref, do_ref,
             dq_ref, dk_ref, dv_ref, *rest):
        ek_refs, ev_refs, db_ref = rest[:nbr], rest[nbr:2 * nbr], rest[2 * nbr]
        c = pl.program_id(1)

        @pl.when(c == 0)
        def _():
            db_ref[...] = jnp.zeros_like(db_ref)

        for r in (dq_ref, dk_ref, dv_ref) + tuple(ek_refs) + tuple(ev_refs):
            r[...] = jnp.zeros_like(r)

        def tile(branch, start, prev_in_block, pstart, first):
            dil = BRANCH_DIL[branch]
            rq, rp = _rows(start, dil), _rows(pstart, dil)
            k_ref, v_ref = (kc_ref, vc_ref) if prev_in_block else (kp_ref, vp_ref)
            kc2 = kc_ref[rq, :].astype(MXU_DTYPE)
            kp2 = k_ref[rp, :].astype(MXU_DTYPE)
            q_st = jnp.concatenate(_head_split(q_ref[rq, :].astype(MXU_DTYPE)), axis=0)
            do_st = jnp.concatenate(_head_split(do_ref[rq, :].astype(MXU_DTYPE)), axis=0)
            k_st = jnp.concatenate([kc2, kp2], axis=0)
            v_st = jnp.concatenate([vc_ref[rq, :].astype(MXU_DTYPE), v_ref[rp, :].astype(MXU_DTYPE)], axis=0)
            kh_st = jnp.concatenate(_head_split(kc2) + _head_split(kp2), axis=0)
            lse2 = l_ref[rq, :]
            del2 = dl_ref[rq, :]
            lse_st = jnp.concatenate([lse2[:, 0:1], lse2[:, HEAD_DIM:HEAD_DIM + 1]], axis=0)
            del_st = jnp.concatenate([del2[:, 0:1], del2[:, HEAD_DIM:HEAD_DIM + 1]], axis=0)
            p = jnp.exp(_dot(q_st, k_st, NT) + _tile_bias(b_ref, branch, first) - lse_st)
            ds = p * (_dot(do_st, v_st, NT) - del_st)
            db_ref[branch] += ds
            ds = ds.astype(MXU_DTYPE)
            p = p.astype(MXU_DTYPE)
            lhs = jnp.concatenate([ds[:QBLK, :QBLK], ds[QBLK:, :QBLK], ds[:QBLK, QBLK:], ds[QBLK:, QBLK:]], axis=1)
            dk_st = _dot(ds, q_st, TN)
            dv_st = _dot(p, do_st, TN)
            dq_ref[rq, :] += _dot(lhs, kh_st, NN)
            dk_ref[rq, :] += dk_st[:QBLK]
            dv_ref[rq, :] += dv_st[:QBLK]
            if prev_in_block:
                dk_ref[rp, :] += dk_st[QBLK:]
                dv_ref[rp, :] += dv_st[QBLK:]
            else:
                ek_refs[branch][rq, :] = dk_st[QBLK:]
                ev_refs[branch][rq, :] = dv_st[QBLK:]

        _for_each_tile(tile, c)

    blk = pl.BlockSpec((None, ABLK, 128), lambda ch, c: (ch, c, 0))
    outs = pl.pallas_call(
        body, name="attn_bwd", grid=(N_CHUNK, nb), in_specs=_attn_chunk_specs(nb) + [blk, blk, blk],
        out_specs=[blk] * (3 + 2 * nbr) + [pl.BlockSpec((nbr, None, BIAS_TILE, BIAS_TILE), lambda ch, c: (0, ch, 0, 0))],
        out_shape=[jax.ShapeDtypeStruct((N_CHUNK, s, 128), F32)] * (3 + 2 * nbr)
        + [jax.ShapeDtypeStruct((nbr, N_HEADS // 2, BIAS_TILE, BIAS_TILE), F32)],
        compiler_params=_params(("arbitrary", "arbitrary")),
    )(qkv_c, qkv_c, qkv_c, qkv_c, qkv_c, bias, lse_c, delta_c, do_c)
    return outs[0], outs[1], outs[2], outs[3:3 + nbr], outs[3 + nbr:3 + 2 * nbr], outs[3 + 2 * nbr]


def _attn2_bwd_sum(dq, dk, dv, ek, ev, dzs, dus):
    s = dq.shape[1]
    nrb = s // QBLK
    per_blk = ABLK // QBLK
    nbr = len(BRANCH_DIL)

    def body(*refs):
        dq_ref, dk_ref, dv_ref = refs[:3]
        ek_refs, ev_refs = refs[3:3 + nbr], refs[3 + nbr:3 + 2 * nbr]
        dzs_ref, dus_ref, o_ref = refs[3 + 2 * nbr:]
        i = pl.program_id(0)
        dkt, dvt = dk_ref[...], dv_ref[...]
        for b, dil in enumerate(BRANCH_DIL):
            j = i + dil
            ok = jnp.logical_and(j < nrb, j % per_blk < dil)
            dkt = dkt + jnp.where(ok, ek_refs[b][...], 0.0)
            dvt = dvt + jnp.where(ok, ev_refs[b][...], 0.0)
        for ch in range(N_CHUNK):
            o_ref[:, 128 * ch:128 * (ch + 1)] = (dq_ref[ch] * ATTN_SCALE).astype(o_ref.dtype)
            o_ref[:, ATTN_W + 128 * ch:ATTN_W + 128 * (ch + 1)] = dkt[ch].astype(o_ref.dtype)
            o_ref[:, 2 * ATTN_W + 128 * ch:2 * ATTN_W + 128 * (ch + 1)] = dvt[ch].astype(o_ref.dtype)
        o_ref[:, O_SGU:O_SSM] = dzs_ref[...].astype(o_ref.dtype)
        o_ref[:, O_SSM:] = dus_ref[...].astype(o_ref.dtype)

    here = pl.BlockSpec((N_CHUNK, QBLK, 128), lambda i: (0, i, 0))
    edge_specs = [pl.BlockSpec((N_CHUNK, QBLK, 128),
                               functools.partial(lambda i, d: (0, jnp.minimum(i + d, nrb - 1), 0), d=dil))
                  for dil in BRANCH_DIL]
    return pl.pallas_call(
        body, name="attn_bwd_sum", grid=(nrb,),
        in_specs=[here, here, here] + edge_specs + edge_specs
        + [pl.BlockSpec((QBLK, 2 * SGU_W), lambda i: (i, 0)), pl.BlockSpec((QBLK, SSM_W), lambda i: (i, 0))],
        out_specs=pl.BlockSpec((QBLK, O_SSM + SSM_W), lambda i: (i, 0)),
        out_shape=jax.ShapeDtypeStruct((s, O_SSM + SSM_W), MXU_DTYPE),
        compiler_params=_params(("parallel",)),
    )(dq, dk, dv, *ek, *ev, dzs, dus)


SGU_ROWS = 512


def _sgu_norm(v_g):
    mu = jnp.mean(v_g, axis=-1, keepdims=True)
    cen = v_g - mu
    var = jnp.mean(cen * cen, axis=-1, keepdims=True)
    rstd = lax.rsqrt(var + EPS)
    return cen * rstd, rstd


def _sgu_fwd(zs, ln_g, ln_b, w_mask, b_t):
    s = zs.shape[0]
    nch = SGU_ROWS // SGU_CHUNK

    def body(z_ref, g_ref, b_ref, w_ref, bt_ref, o_ref):
        gz = _gelu(z_ref[...])
        for g in range(SGU_G):
            sl = slice(SGU_GW * g, SGU_GW * (g + 1))
            u_g = gz[:, sl]
            xhat, _ = _sgu_norm(gz[:, SGU_W + SGU_GW * g:SGU_W + SGU_GW * (g + 1)])
            vn = (xhat * g_ref[:, sl] + b_ref[:, sl]).astype(MXU_DTYPE)
            wg = w_ref[g].astype(MXU_DTYPE)
            for ci in range(nch):
                rs = slice(SGU_CHUNK * ci, SGU_CHUNK * (ci + 1))
                mixed = _dot(wg, vn[rs], NN) + bt_ref[:, g:g + 1]
                o_ref[rs, sl] = u_g[rs] * mixed

    full = lambda shape: pl.BlockSpec(shape, lambda i: tuple(0 for _ in shape))
    return pl.pallas_call(
        body, name="sgu_fwd", grid=(s // SGU_ROWS,),
        in_specs=[pl.BlockSpec((SGU_ROWS, 2 * SGU_W), lambda i: (i, 0)), full((1, SGU_W)), full((1, SGU_W)),
                  full((SGU_G, SGU_CHUNK, SGU_CHUNK)), full((SGU_CHUNK, SGU_G))],
        out_specs=pl.BlockSpec((SGU_ROWS, SGU_W), lambda i: (i, 0)),
        out_shape=jax.ShapeDtypeStruct((s, SGU_W), F32),
        compiler_params=_params(("parallel",)),
    )(zs, ln_g.reshape(1, SGU_W), ln_b.reshape(1, SGU_W), w_mask, b_t)


def _sgu_bwd(zs, ln_g, ln_b, w_mask, b_t, dy):
    s = zs.shape[0]
    nch = SGU_ROWS // SGU_CHUNK

    def body(z_ref, g_ref, b_ref, w_ref, bt_ref, dy_ref, dz_ref, dg_ref, dbb_ref, dw_ref, dbt_ref):
        @pl.when(pl.program_id(0) == 0)
        def _():
            dg_ref[...] = jnp.zeros_like(dg_ref)
            dbb_ref[...] = jnp.zeros_like(dbb_ref)
            dw_ref[...] = jnp.zeros_like(dw_ref)
            dbt_ref[...] = jnp.zeros_like(dbt_ref)

        z = z_ref[...]
        gz, dgelu = _gelu_pair(z)
        dy = dy_ref[...]
        for g in range(SGU_G):
            sl = slice(SGU_GW * g, SGU_GW * (g + 1))
            sv = slice(SGU_W + SGU_GW * g, SGU_W + SGU_GW * (g + 1))
            u_g = gz[:, sl]
            xhat, rstd = _sgu_norm(gz[:, sv])
            gain = g_ref[:, sl]
            vn = (xhat * gain + b_ref[:, sl]).astype(MXU_DTYPE)
            wg = w_ref[g].astype(MXU_DTYPE)
            dy_g = dy[:, sl]
            dvn_parts = []
            for ci in range(nch):
                rs = slice(SGU_CHUNK * ci, SGU_CHUNK * (ci + 1))
                mixed = _dot(wg, vn[rs], NN) + bt_ref[:, g:g + 1]
                dz_ref[rs, sl] = (dy_g[rs] * mixed * dgelu[rs, sl]).astype(dz_ref.dtype)
                dmixed = dy_g[rs] * u_g[rs]
                dm = dmixed.astype(MXU_DTYPE)
                dvn_parts.append(_dot(wg, dm, TN))
                dw_ref[g] += _dot(dm, vn[rs], NT)
                dbt_ref[:, g:g + 1] += jnp.sum(dmixed, axis=-1, keepdims=True)
            dvn = jnp.concatenate(dvn_parts, axis=0)
            dg_ref[:, sl] += jnp.sum(dvn * xhat, axis=0, keepdims=True)
            dbb_ref[:, sl] += jnp.sum(dvn, axis=0, keepdims=True)
            dxh = dvn * gain
            dv = rstd * (dxh - jnp.mean(dxh, axis=-1, keepdims=True)
                         - xhat * jnp.mean(dxh * xhat, axis=-1, keepdims=True))
            dz_ref[:, sv] = (dv * dgelu[:, sv]).astype(dz_ref.dtype)

    full = lambda shape: pl.BlockSpec(shape, lambda i: tuple(0 for _ in shape))
    return pl.pallas_call(
        body, name="sgu_bwd", grid=(s // SGU_ROWS,),
        in_specs=[pl.BlockSpec((SGU_ROWS, 2 * SGU_W), lambda i: (i, 0)), full((1, SGU_W)), full((1, SGU_W)),
                  full((SGU_G, SGU_CHUNK, SGU_CHUNK)), full((SGU_CHUNK, SGU_G)),
                  pl.BlockSpec((SGU_ROWS, SGU_W), lambda i: (i, 0))],
        out_specs=[pl.BlockSpec((SGU_ROWS, 2 * SGU_W), lambda i: (i, 0)), full((1, SGU_W)), full((1, SGU_W)),
                   full((SGU_G, SGU_CHUNK, SGU_CHUNK)), full((SGU_CHUNK, SGU_G))],
        out_shape=[jax.ShapeDtypeStruct((s, 2 * SGU_W), MXU_DTYPE), jax.ShapeDtypeStruct((1, SGU_W), F32),
                   jax.ShapeDtypeStruct((1, SGU_W), F32), jax.ShapeDtypeStruct((SGU_G, SGU_CHUNK, SGU_CHUNK), F32),
                   jax.ShapeDtypeStruct((SGU_CHUNK, SGU_G), F32)],
        compiler_params=_params(("arbitrary",)),
    )(zs, ln_g.reshape(1, SGU_W), ln_b.reshape(1, SGU_W), w_mask, b_t, dy)


def _ssm_discretize(a_re, a_im, log_dt, b_re, b_im):
    dt = jnp.exp(log_dt)[:, None]
    mag = jnp.exp(a_re * dt)
    ab_re = mag * jnp.cos(a_im * dt)
    ab_im = mag * jnp.sin(a_im * dt)
    den = a_re * a_re + a_im * a_im
    f_re = ((ab_re - 1.0) * a_re + ab_im * a_im) / den
    f_im = (ab_im * a_re - (ab_re - 1.0) * a_im) / den
    bb_re = f_re[:, :, None] * b_re - f_im[:, :, None] * b_im
    bb_im = f_re[:, :, None] * b_im + f_im[:, :, None] * b_re
    return ab_re, ab_im, bb_re, bb_im


def _ssm_operands(a_re, a_im, log_dt, b_re, b_im, c_re, c_im):
    ab_re, ab_im, bb_re, bb_im = _ssm_discretize(a_re, a_im, log_dt, b_re, b_im)
    eye = jnp.eye(SSM_G, dtype=F32)
    b_blk = jnp.einsum("pgnc,gh->gcphn", jnp.stack([bb_re, bb_im]), eye).reshape(SSM_W, 2 * NSTATE)
    c_mat = jnp.einsum("pgcn,gh->pgnhc", jnp.stack([c_re, -c_im]), eye).reshape(2 * NSTATE, SSM_W)
    a_row = jnp.stack([ab_re.reshape(NSTATE), ab_im.reshape(NSTATE)])
    p_re, p_im = a_row[0:1], a_row[1:2]
    while p_re.shape[0] < SSM_TSEG:
        l_re, l_im = p_re[-1:], p_im[-1:]
        p_re, p_im = (jnp.concatenate([p_re, p_re * l_re - p_im * l_im]),
                      jnp.concatenate([p_im, p_re * l_im + p_im * l_re]))
    p_tab = jnp.stack([p_re, p_im])
    return b_blk.astype(MXU_DTYPE), c_mat.astype(MXU_DTYPE), a_row, p_tab


def _lane_chunks():
    return [(lo, lo + SSM_LANE_CHUNK) for lo in range(0, NSTATE, SSM_LANE_CHUNK)]


def _seg_rows(j):
    return pl.ds(pl.multiple_of(j * SSM_NSEG, SSM_NSEG), SSM_NSEG)


def _to_segments(t):
    s, w = t.shape
    return t.reshape(s // SSM_TB, SSM_NSEG, SSM_TSEG, w).transpose(0, 2, 1, 3).reshape(s, w)


def _from_segments(t):
    s, w = t.shape
    return t.reshape(s // SSM_TB, SSM_TSEG, SSM_NSEG, w).transpose(0, 2, 1, 3).reshape(s, w)


def _ssm_local_scan(buf, a_ref, *, reverse):
    ends_re, ends_im = [], []
    for lo, hi in _lane_chunks():
        are = jnp.broadcast_to(a_ref[0:1, lo:hi], (SSM_NSEG, hi - lo))
        aim = jnp.broadcast_to(a_ref[1:2, lo:hi], (SSM_NSEG, hi - lo))
        if reverse:
            aim = -aim

        def step(jj, carry, lo=lo, hi=hi, are=are, aim=aim):
            xr, xi = carry
            j = (SSM_TSEG - 1 - jj) if reverse else jj
            tr = buf[_seg_rows(j), lo:hi]
            ti = buf[_seg_rows(j), NSTATE + lo:NSTATE + hi]
            nr = are * xr - aim * xi + tr
            ni = are * xi + aim * xr + ti
            buf[_seg_rows(j), lo:hi] = nr
            buf[_seg_rows(j), NSTATE + lo:NSTATE + hi] = ni
            return nr, ni

        zero = jnp.zeros((SSM_NSEG, hi - lo), F32)
        xr, xi = lax.fori_loop(0, SSM_TSEG, step, (zero, zero), unroll=4)
        ends_re.append(xr)
        ends_im.append(xi)
    return jnp.concatenate(ends_re, axis=1), jnp.concatenate(ends_im, axis=1)


def _ssm_entry_states(ends_re, ends_im, carry_ref, p_ref, entry_ref, *, reverse):
    at_re = p_ref[0, SSM_TSEG - 1:SSM_TSEG, :]
    at_im = p_ref[1, SSM_TSEG - 1:SSM_TSEG, :]
    if reverse:
        at_im = -at_im
    cur_re = carry_ref[0:1, 0:NSTATE]
    cur_im = carry_ref[0:1, NSTATE:2 * NSTATE]
    order = range(SSM_NSEG - 1, -1, -1) if reverse else range(SSM_NSEG)
    for i in order:
        entry_ref[0, i:i + 1, 0:NSTATE] = cur_re
        entry_ref[0, i:i + 1, NSTATE:2 * NSTATE] = cur_im
        nxt_re = ends_re[i:i + 1] + at_re * cur_re - at_im * cur_im
        nxt_im = ends_im[i:i + 1] + at_re * cur_im + at_im * cur_re
        cur_re, cur_im = nxt_re, nxt_im
    carry_ref[0:1, 0:NSTATE] = cur_re
    carry_ref[0:1, NSTATE:2 * NSTATE] = cur_im


def _ssm_fixup(buf, p_ref, entry_ref, *, reverse):
    for lo, hi in _lane_chunks():
        e_re = entry_ref[0, :, lo:hi]
        e_im = entry_ref[0, :, NSTATE + lo:NSTATE + hi]

        def step(j, carry, lo=lo, hi=hi, e_re=e_re, e_im=e_im):
            jp = (SSM_TSEG - 1 - j) if reverse else j
            pr = p_ref[0, pl.ds(jp, 1), lo:hi]
            pi = p_ref[1, pl.ds(jp, 1), lo:hi]
            if reverse:
                pi = -pi
            buf[_seg_rows(j), lo:hi] = buf[_seg_rows(j), lo:hi] + pr * e_re - pi * e_im
            buf[_seg_rows(j), NSTATE + lo:NSTATE + hi] = (buf[_seg_rows(j), NSTATE + lo:NSTATE + hi]
                                                           + pr * e_im + pi * e_re)
            return carry

        lax.fori_loop(0, SSM_TSEG, step, 0, unroll=4)


def _ssm_fwd(u, ops, d_skip, glu_w, glu_b):
    b_blk, c_mat, a_row, p_tab = ops
    s = u.shape[0]
    nblk = s // SSM_TB

    def body(u_ref, bb_ref, cm_ref, a_ref, p_ref, d_ref, gw_ref, gb_ref, y_ref, entry_ref, xbuf, carry):
        @pl.when(pl.program_id(0) == 0)
        def _():
            carry[...] = jnp.zeros_like(carry)

        uu = u_ref[...]
        xbuf[...] = _dotf(uu, bb_ref[...], NN)
        ends_re, ends_im = _ssm_local_scan(xbuf, a_ref, reverse=False)
        _ssm_entry_states(ends_re, ends_im, carry, p_ref, entry_ref, reverse=False)
        _ssm_fixup(xbuf, p_ref, entry_ref, reverse=False)
        y = _dotf(xbuf[...],cm_ref[...], NN) + d_ref[...] * uu
        y2 = _gelu(y)
        gate = jax.nn.sigmoid(_dot(y2.astype(MXU_DTYPE), gw_ref[...].astype(MXU_DTYPE), NN) + gb_ref[...])
        y_ref[...] = y2 * gate

    full = lambda shape: pl.BlockSpec(shape, lambda i: tuple(0 for _ in shape))
    y_seg, entry = pl.pallas_call(
        body, name="ssm_fwd", grid=(nblk,),
        in_specs=[pl.BlockSpec((SSM_TB, SSM_W), lambda i: (i, 0)), full(b_blk.shape), full(c_mat.shape),
                  full(a_row.shape), full(p_tab.shape), full((1, SSM_W)), full((SSM_W, SSM_W)), full((1, SSM_W))],
        out_specs=[pl.BlockSpec((SSM_TB, SSM_W), lambda i: (i, 0)),
                   pl.BlockSpec((1, SSM_NSEG, 2 * NSTATE), lambda i: (i, 0, 0))],
        out_shape=[jax.ShapeDtypeStruct((s, SSM_W), F32), jax.ShapeDtypeStruct((nblk, SSM_NSEG, 2 * NSTATE), F32)],
        scratch_shapes=[pltpu.VMEM((SSM_TB, 2 * NSTATE), F32), pltpu.VMEM((SSM_NSEG, 2 * NSTATE), F32)],
        compiler_params=_params(("arbitrary",)),
    )(_to_segments(u), b_blk, c_mat, a_row, p_tab, d_skip.reshape(1, SSM_W), glu_w, glu_b.reshape(1, SSM_W))
    return _from_segments(y_seg), entry


def _ssm_bwd(u, entry, ops, d_skip, glu_w, glu_b, dout):
    b_blk, c_mat, a_row, p_tab = ops
    s = u.shape[0]
    nblk = s // SSM_TB

    def body(u_ref, en_ref, bb_ref, cm_ref, a_ref, p_ref, d_ref, gw_ref, gb_ref, do_ref,
             du_ref, dbb_ref, dcm_ref, da_ref, dd_ref, dgw_ref, dgb_ref, xbuf, gbuf, gcarry, gentry):
        @pl.when(pl.program_id(0) == 0)
        def _():
            gcarry[...] = jnp.zeros_like(gcarry)
            for r in (dbb_ref, dcm_ref, da_ref, dd_ref, dgw_ref, dgb_ref):
                r[...] = jnp.zeros_like(r)

        uu = u_ref[...]
        xbuf[...] = _dotf(uu, bb_ref[...], NN)
        _ssm_local_scan(xbuf, a_ref, reverse=False)
        _ssm_fixup(xbuf, p_ref, en_ref, reverse=False)
        y = _dotf(xbuf[...],cm_ref[...], NN) + d_ref[...] * uu
        y2, dgelu = _gelu_pair(y)
        y2m = y2.astype(MXU_DTYPE)
        gwm = gw_ref[...].astype(MXU_DTYPE)
        gate = jax.nn.sigmoid(_dot(y2m, gwm, NN) + gb_ref[...])
        dout = do_ref[...]
        dpre = dout * y2 * gate * (1.0 - gate)
        dprem = dpre.astype(MXU_DTYPE)
        dy2 = dout * gate + _dot(dprem, gwm, NT)
        dgw_ref[...] += _dot(y2m, dprem, TN)
        dgb_ref[...] += jnp.sum(dpre, axis=0, keepdims=True)
        dy = dy2 * dgelu
        dd_ref[...] += jnp.sum(dy * uu, axis=0, keepdims=True)
        dcm_ref[...] += _dotf(xbuf[...],dy, TN)
        gbuf[...] = _dotf(dy, cm_ref[...], NT)
        gs_re, gs_im = _ssm_local_scan(gbuf, a_ref, reverse=True)
        _ssm_entry_states(gs_re, gs_im, gcarry, p_ref, gentry, reverse=True)
        _ssm_fixup(gbuf, p_ref, gentry, reverse=True)
        du_ref[...] = (_dotf(gbuf[...], bb_ref[...], NT) + d_ref[...] * dy).astype(du_ref.dtype)
        dbb_ref[...] += _dotf(uu, gbuf[...], TN)
        for lo, hi in _lane_chunks():
            def step(j, carry, lo=lo, hi=hi):
                acc_re, acc_im = carry
                g_re = gbuf[_seg_rows(j), lo:hi]
                g_im = gbuf[_seg_rows(j), NSTATE + lo:NSTATE + hi]
                x_re = xbuf[_seg_rows(j - 1), lo:hi]
                x_im = xbuf[_seg_rows(j - 1), NSTATE + lo:NSTATE + hi]
                return acc_re + g_re * x_re + g_im * x_im, acc_im + g_im * x_re - g_re * x_im

            g0_re = gbuf[_seg_rows(0), lo:hi]
            g0_im = gbuf[_seg_rows(0), NSTATE + lo:NSTATE + hi]
            e_re = en_ref[0, :, lo:hi]
            e_im = en_ref[0, :, NSTATE + lo:NSTATE + hi]
            init = (g0_re * e_re + g0_im * e_im, g0_im * e_re - g0_re * e_im)
            acc_re, acc_im = lax.fori_loop(1, SSM_TSEG, step, init, unroll=4)
            da_ref[0:1, lo:hi] += jnp.sum(acc_re, axis=0, keepdims=True)
            da_ref[1:2, lo:hi] += jnp.sum(acc_im, axis=0, keepdims=True)

    full = lambda shape: pl.BlockSpec(shape, lambda i: tuple(0 for _ in shape))
    rev = pl.BlockSpec((SSM_TB, SSM_W), lambda i: (nblk - 1 - i, 0))
    outs = pl.pallas_call(
        body, name="ssm_bwd", grid=(nblk,),
        in_specs=[rev, pl.BlockSpec((1, SSM_NSEG, 2 * NSTATE), lambda i: (nblk - 1 - i, 0, 0)),
                  full(b_blk.shape), full(c_mat.shape), full(a_row.shape), full(p_tab.shape),
                  full((1, SSM_W)), full((SSM_W, SSM_W)), full((1, SSM_W)), rev],
        out_specs=[rev, full(b_blk.shape), full(c_mat.shape), full(a_row.shape), full((1, SSM_W)),
                   full((SSM_W, SSM_W)), full((1, SSM_W))],
        out_shape=[jax.ShapeDtypeStruct((s, SSM_W), MXU_DTYPE), jax.ShapeDtypeStruct(b_blk.shape, F32),
                   jax.ShapeDtypeStruct(c_mat.shape, F32), jax.ShapeDtypeStruct(a_row.shape, F32),
                   jax.ShapeDtypeStruct((1, SSM_W), F32), jax.ShapeDtypeStruct((SSM_W, SSM_W), F32),
                   jax.ShapeDtypeStruct((1, SSM_W), F32)],
        scratch_shapes=[pltpu.VMEM((SSM_TB, 2 * NSTATE), F32), pltpu.VMEM((SSM_TB, 2 * NSTATE), F32),
                        pltpu.VMEM((SSM_NSEG, 2 * NSTATE), F32), pltpu.VMEM((1, SSM_NSEG, 2 * NSTATE), F32)],
        compiler_params=_params(("arbitrary",)),
    )(_to_segments(u), entry, b_blk, c_mat, a_row, p_tab, d_skip.reshape(1, SSM_W), glu_w, glu_b.reshape(1, SSM_W),
      _to_segments(dout))
    return (_from_segments(outs[0]),) + tuple(outs[1:])


MIX_SEGS = ((0, ATTN_W), (ATTN_W, ATTN_W + SGU_W), (ATTN_W + SGU_W, D_MODEL))


def _chunks_to_rows(a_ref):
    return jnp.concatenate([a_ref[ch] for ch in range(N_CHUNK)], axis=1)


def _mix_fwd(y_attn_c, y_sgu, y_ssm, gain):
    s = y_sgu.shape[0]

    def body(a_ref, b_ref, c_ref, g_ref, o_ref):
        for x, (lo, hi) in zip((_chunks_to_rows(a_ref), b_ref[...], c_ref[...]), MIX_SEGS):
            r = lax.rsqrt(jnp.mean(x * x, axis=-1, keepdims=True) + EPS)
            o_ref[:, lo:hi] = (x * r * g_ref[:, lo:hi]).astype(o_ref.dtype)

    row = lambda w: pl.BlockSpec((ROWS, w), lambda i: (i, 0))
    return pl.pallas_call(
        body, name="mix_fwd", grid=(s // ROWS,),
        in_specs=[pl.BlockSpec((N_CHUNK, ROWS, 128), lambda i: (0, i, 0)), row(SGU_W), row(SSM_W),
                  pl.BlockSpec((1, D_MODEL), lambda i: (0, 0))],
        out_specs=row(D_MODEL), out_shape=jax.ShapeDtypeStruct((s, D_MODEL), MXU_DTYPE),
        compiler_params=_params(("parallel",)),
    )(y_attn_c, y_sgu, y_ssm, gain.reshape(1, D_MODEL))


def _mix_bwd(y_attn_c, y_sgu, y_ssm, gain, dmix):
    s = y_sgu.shape[0]

    def body(a_ref, b_ref, c_ref, g_ref, dm_ref, da_ref, dl_ref, db_ref, dc_ref, dg_ref):
        @pl.when(pl.program_id(0) == 0)
        def _():
            dg_ref[...] = jnp.zeros_like(dg_ref)

        grads = []
        for x, (lo, hi) in zip((_chunks_to_rows(a_ref), b_ref[...], c_ref[...]), MIX_SEGS):
            r = lax.rsqrt(jnp.mean(x * x, axis=-1, keepdims=True) + EPS)
            xhat = x * r
            dm = dm_ref[:, lo:hi].astype(F32)
            dg_ref[:, lo:hi] += jnp.sum(dm * xhat, axis=0, keepdims=True)
            dxh = dm * g_ref[:, lo:hi]
            grads.append(r * (dxh - xhat * jnp.mean(dxh * xhat, axis=-1, keepdims=True)))
        db_ref[...] = grads[1]
        dc_ref[...] = grads[2]
        low = lax.broadcasted_iota(jnp.int32, (ROWS, 128), 1) < HEAD_DIM
        for ch in range(N_CHUNK):
            d_c = grads[0][:, 128 * ch:128 * (ch + 1)]
            da_ref[ch] = d_c.astype(da_ref.dtype)
            prod = d_c * a_ref[ch]
            dl_ref[ch] = jnp.where(low, jnp.sum(prod[:, :HEAD_DIM], axis=-1, keepdims=True),
                                   jnp.sum(prod[:, HEAD_DIM:], axis=-1, keepdims=True))

    row = lambda w: pl.BlockSpec((ROWS, w), lambda i: (i, 0))
    vec = pl.BlockSpec((1, D_MODEL), lambda i: (0, 0))
    chunked = pl.BlockSpec((N_CHUNK, ROWS, 128), lambda i: (0, i, 0))
    return pl.pallas_call(
        body, name="mix_bwd", grid=(s // ROWS,),
        in_specs=[chunked, row(SGU_W), row(SSM_W), vec, row(D_MODEL)],
        out_specs=[chunked, chunked, row(SGU_W), row(SSM_W), vec],
        out_shape=[jax.ShapeDtypeStruct((N_CHUNK, s, 128), ATTN_IO_DTYPE), jax.ShapeDtypeStruct((N_CHUNK, s, 128), F32),
                   jax.ShapeDtypeStruct((s, SGU_W), F32), jax.ShapeDtypeStruct((s, SSM_W), F32),
                   jax.ShapeDtypeStruct((1, D_MODEL), F32)],
        compiler_params=_params(("arbitrary",)),
    )(y_attn_c, y_sgu, y_ssm, gain.reshape(1, D_MODEL), dmix)


CONV_ROWS = 256
CONV_COLS = 1408
CONV_PAIR = 2 * CONV_COLS
HALO = 16


def _interleave_ff(t):
    lead = t.shape[:-1]
    nb = D_FF // CONV_COLS
    return jnp.swapaxes(t.reshape(lead + (2, nb, CONV_COLS)), -3, -2).reshape(lead + (2 * D_FF,))


def _deinterleave_ff(t):
    lead = t.shape[:-1]
    nb = D_FF // CONV_COLS
    return jnp.swapaxes(t.reshape(lead + (nb, 2, CONV_COLS)), -3, -2).reshape(lead + (2 * D_FF,))


def _causal_taps(x, halo, first):
    patch = 8
    row = lax.broadcasted_iota(jnp.int32, (patch, x.shape[1]), 0)
    h1 = jnp.where(first, 0.0, halo[HALO - 1:HALO, :])
    h2 = jnp.where(first, 0.0, halo[HALO - 2:HALO - 1, :])
    r1 = pltpu.roll(x, 1, 0)
    r2 = pltpu.roll(x, 2, 0)
    top1 = jnp.where(row == 0, h1, r1[0:patch])
    top2 = jnp.where(row == 0, h2, jnp.where(row == 1, h1, r2[0:patch]))
    return jnp.concatenate([top1, r1[patch:]], axis=0), jnp.concatenate([top2, r2[patch:]], axis=0)


def _conv_in_specs():
    halo_idx = lambda i: jnp.maximum(i * (CONV_ROWS // HALO) - 1, 0)
    return [pl.BlockSpec((CONV_ROWS, CONV_PAIR), lambda j, i: (i, j)),
            pl.BlockSpec((HALO, CONV_PAIR), lambda j, i: (halo_idx(i), j)),
            pl.BlockSpec((3, CONV_PAIR), lambda j, i: (0, j)),
            pl.BlockSpec((1, CONV_PAIR), lambda j, i: (0, j))]


def _ffn_act_fwd(hh, conv_w, conv_b):
    s = hh.shape[0]

    def body(m_ref, h_ref, w_ref, b_ref, o_ref):
        first = pl.program_id(1) == 0
        main = m_ref[...].astype(F32)
        x1, x2 = _causal_taps(main, h_ref[...].astype(F32), first)
        conv = w_ref[0:1, :] * x2 + w_ref[1:2, :] * x1 + w_ref[2:3, :] * main + b_ref[...]
        o_ref[...] = (_gelu(conv[:, CONV_COLS:]) * conv[:, :CONV_COLS]).astype(o_ref.dtype)

    return pl.pallas_call(
        body, name="ffn_act_fwd", grid=(D_FF // CONV_COLS, s // CONV_ROWS), in_specs=_conv_in_specs(),
        out_specs=pl.BlockSpec((CONV_ROWS, CONV_COLS), lambda j, i: (i, j)),
        out_shape=jax.ShapeDtypeStruct((s, D_FF), MXU_DTYPE),
        compiler_params=_params(("parallel", "parallel")),
    )(hh, hh, conv_w, conv_b.reshape(1, -1))


def _ffn_act_bwd(hh, conv_w, conv_b, da):
    s = hh.shape[0]
    nrow = s // CONV_ROWS
    ext_rows = CONV_ROWS + HALO

    def body(m_ref, h_ref, w_ref, b_ref, nx_ref, da_ref, dan_ref, o_ref, dw_ref, db_ref):
        first = pl.program_id(1) == 0
        last = pl.program_id(1) == nrow - 1

        @pl.when(first)
        def _():
            dw_ref[...] = jnp.zeros_like(dw_ref)
            db_ref[...] = jnp.zeros_like(db_ref)

        ext = jnp.concatenate([m_ref[...].astype(F32), nx_ref[...].astype(F32)], axis=0)
        x1, x2 = _causal_taps(ext, h_ref[...].astype(F32), first)
        conv = w_ref[0:1, :] * x2 + w_ref[1:2, :] * x1 + w_ref[2:3, :] * ext + b_ref[...]
        da = jnp.concatenate([da_ref[...].astype(F32), jnp.where(last, 0.0, dan_ref[...].astype(F32))], axis=0)
        act, dact = _gelu_pair(conv[:, CONV_COLS:])
        dconv = jnp.concatenate([da * act, da * conv[:, :CONV_COLS] * dact], axis=1)
        dmain = dconv[:CONV_ROWS]
        ahead1 = pltpu.roll(dconv, ext_rows - 1, 0)[:CONV_ROWS]
        ahead2 = pltpu.roll(dconv, ext_rows - 2, 0)[:CONV_ROWS]
        o_ref[...] = (w_ref[2:3, :] * dmain + w_ref[1:2, :] * ahead1 + w_ref[0:1, :] * ahead2).astype(o_ref.dtype)
        for t, tap in enumerate((x2, x1, ext)):
            dw_ref[t:t + 1, :] += jnp.sum(dmain * tap[:CONV_ROWS], axis=0, keepdims=True)
        db_ref[...] += jnp.sum(dmain, axis=0, keepdims=True)

    nxt = lambda i: jnp.minimum((i + 1) * (CONV_ROWS // HALO), s // HALO - 1)
    return pl.pallas_call(
        body, name="ffn_act_bwd", grid=(D_FF // CONV_COLS, nrow),
        in_specs=_conv_in_specs() + [pl.BlockSpec((HALO, CONV_PAIR), lambda j, i: (nxt(i), j)),
                                     pl.BlockSpec((CONV_ROWS, CONV_COLS), lambda j, i: (i, j)),
                                     pl.BlockSpec((HALO, CONV_COLS), lambda j, i: (nxt(i), j))],
        out_specs=[pl.BlockSpec((CONV_ROWS, CONV_PAIR), lambda j, i: (i, j)),
                   pl.BlockSpec((3, CONV_PAIR), lambda j, i: (0, j)), pl.BlockSpec((1, CONV_PAIR), lambda j, i: (0, j))],
        out_shape=[jax.ShapeDtypeStruct((s, 2 * D_FF), MXU_DTYPE), jax.ShapeDtypeStruct((3, 2 * D_FF), F32),
                   jax.ShapeDtypeStruct((1, 2 * D_FF), F32)],
        compiler_params=_params(("parallel", "arbitrary")),
    )(hh, hh, conv_w, conv_b.reshape(1, -1), hh, da, da)


def _shift_matrix(rows, back):
    r = lax.broadcasted_iota(jnp.int32, (2 * rows, rows), 0)
    c = lax.broadcasted_iota(jnp.int32, (2 * rows, rows), 1)
    step = jnp.where(r < rows, 1, 2)
    t = jnp.where(r < rows, r, r - rows)
    src = t - step if back else t + step
    return jnp.where(c == src, 1.0, 0.0).astype(MXU_DTYPE)


def _patch_rows(x, at_end, rows):
    tile = 8
    n = x.shape[0]
    idx = lax.broadcasted_iota(jnp.int32, (tile, x.shape[1]), 0)
    piece = x[n - tile:] if at_end else x[:tile]
    for k, row in enumerate(rows):
        where_row = (tile - len(rows) + k) if at_end else k
        piece = jnp.where(idx == where_row, row, piece)
    return jnp.concatenate([x[:n - tile], piece], axis=0) if at_end else jnp.concatenate([piece, x[tile:]], axis=0)


def _mxu_taps(main_m, halo, first):
    shifted = _dot(_shift_matrix(main_m.shape[0], True), main_m, NN)
    h1 = jnp.where(first, 0.0, halo[HALO - 1:HALO, :])
    h2 = jnp.where(first, 0.0, halo[HALO - 2:HALO - 1, :])
    x1 = _patch_rows(shifted[:main_m.shape[0]], False, [h1])
    x2 = _patch_rows(shifted[main_m.shape[0]:], False, [h2, h1])
    return x1, x2


def _conv_gate(w_ref, b_ref, x2, x1, x0):
    return w_ref[0:1, :] * x2 + w_ref[1:2, :] * x1 + w_ref[2:3, :] * x0 + b_ref[...]


def _ffn_gate_fwd(hh, conv_w, conv_b):
    s = hh.shape[0]

    def body(m_ref, h_ref, w_ref, b_ref, o_ref):
        first = pl.program_id(1) == 0
        main_m = m_ref[...]
        x1, x2 = _mxu_taps(main_m, h_ref[...].astype(F32), first)
        conv = _conv_gate(w_ref, b_ref, x2, x1, main_m.astype(F32))
        o_ref[...] = (_gelu(conv[:, CONV_COLS:]) * conv[:, :CONV_COLS]).astype(o_ref.dtype)

    return pl.pallas_call(
        body, name="ffn_act_fwd", grid=(D_FF // CONV_COLS, s // CONV_ROWS), in_specs=_conv_in_specs(),
        out_specs=pl.BlockSpec((CONV_ROWS, CONV_COLS), lambda j, i: (i, j)),
        out_shape=jax.ShapeDtypeStruct((s, D_FF), MXU_DTYPE),
        compiler_params=_params(("parallel", "parallel")),
    )(hh, hh, conv_w, conv_b.reshape(1, -1))


def _ffn_gate_bwd(hh, conv_w, conv_b, da):
    s = hh.shape[0]
    nrow = s // CONV_ROWS

    def gate_grad(conv, da):
        act, dact = _gelu_pair(conv[:, CONV_COLS:])
        return jnp.concatenate([da * act, da * conv[:, :CONV_COLS] * dact], axis=1)

    def body(m_ref, h_ref, w_ref, b_ref, nx_ref, da_ref, dan_ref, o_ref, dw_ref, db_ref):
        first = pl.program_id(1) == 0
        last = pl.program_id(1) == nrow - 1

        @pl.when(first)
        def _():
            dw_ref[...] = jnp.zeros_like(dw_ref)
            db_ref[...] = jnp.zeros_like(db_ref)

        main_m = m_ref[...]
        main = main_m.astype(F32)
        x1, x2 = _mxu_taps(main_m, h_ref[...].astype(F32), first)
        dconv = gate_grad(_conv_gate(w_ref, b_ref, x2, x1, main), da_ref[...].astype(F32))
        nx = nx_ref[...].astype(F32)
        nx1 = _patch_rows(pltpu.roll(nx, 1, 0), False, [main[CONV_ROWS - 1:]])
        nx2 = _patch_rows(pltpu.roll(nx, 2, 0), False, [main[CONV_ROWS - 2:CONV_ROWS - 1], main[CONV_ROWS - 1:]])
        dnext = gate_grad(_conv_gate(w_ref, b_ref, nx2, nx1, nx), jnp.where(last, 0.0, dan_ref[...].astype(F32)))
        dnext = dnext.astype(MXU_DTYPE).astype(F32)
        ahead = _dot(_shift_matrix(CONV_ROWS, False), dconv.astype(MXU_DTYPE), NN)
        ahead1 = _patch_rows(ahead[:CONV_ROWS], True, [dnext[0:1]])
        ahead2 = _patch_rows(ahead[CONV_ROWS:], True, [dnext[0:1], dnext[1:2]])
        o_ref[...] = (w_ref[2:3, :] * dconv + w_ref[1:2, :] * ahead1 + w_ref[0:1, :] * ahead2).astype(o_ref.dtype)
        for t, tap in enumerate((x2, x1, main)):
            dw_ref[t:t + 1, :] += jnp.sum(dconv * tap, axis=0, keepdims=True)
        db_ref[...] += jnp.sum(dconv, axis=0, keepdims=True)

    nxt = lambda i: jnp.minimum((i + 1) * (CONV_ROWS // HALO), s // HALO - 1)
    return pl.pallas_call(
        body, name="ffn_act_bwd", grid=(D_FF // CONV_COLS, nrow),
        in_specs=_conv_in_specs() + [pl.BlockSpec((HALO, CONV_PAIR), lambda j, i: (nxt(i), j)),
                                     pl.BlockSpec((CONV_ROWS, CONV_COLS), lambda j, i: (i, j)),
                                     pl.BlockSpec((HALO, CONV_COLS), lambda j, i: (nxt(i), j))],
        out_specs=[pl.BlockSpec((CONV_ROWS, CONV_PAIR), lambda j, i: (i, j)),
                   pl.BlockSpec((3, CONV_PAIR), lambda j, i: (0, j)), pl.BlockSpec((1, CONV_PAIR), lambda j, i: (0, j))],
        out_shape=[jax.ShapeDtypeStruct((s, 2 * D_FF), MXU_DTYPE), jax.ShapeDtypeStruct((3, 2 * D_FF), F32),
                   jax.ShapeDtypeStruct((1, 2 * D_FF), F32)],
        compiler_params=_params(("parallel", "arbitrary")),
    )(hh, hh, conv_w, conv_b.reshape(1, -1), hh, da, da)


def _ple_weight_specs(layer):
    return [pl.BlockSpec((None, D_MODEL, D_MODEL), lambda i: (layer, 0, 0)),
            pl.BlockSpec((None, PLE_DIM, D_MODEL), lambda i: (layer, 0, 0))]


def _ple_fwd(h, gain, p, w_gate, w_proj, layer):
    s = h.shape[0]
    tm = 512

    def body(h_ref, g_ref, p_ref, wg_ref, wp_ref, o_ref, xn_ref):
        x = h_ref[...]
        xn = _rms_rows(x, g_ref[...])
        xn_ref[...] = xn
        gate = jax.nn.sigmoid(_dot(xn, wg_ref[...].astype(MXU_DTYPE), NN))
        proj = _dot(p_ref[...].astype(MXU_DTYPE), wp_ref[...].astype(MXU_DTYPE), NN)
        o_ref[...] = x + gate * proj

    row = pl.BlockSpec((tm, D_MODEL), lambda i: (i, 0))
    return pl.pallas_call(
        body, name="ple_fwd", grid=(s // tm,),
        in_specs=[row, pl.BlockSpec((1, D_MODEL), lambda i: (0, 0)), pl.BlockSpec((tm, PLE_DIM), lambda i: (i, 0))]
        + _ple_weight_specs(layer),
        out_specs=[row, row],
        out_shape=[jax.ShapeDtypeStruct((s, D_MODEL), F32), jax.ShapeDtypeStruct((s, D_MODEL), MXU_DTYPE)],
        compiler_params=_params(("parallel",)),
    )(h, gain.reshape(1, D_MODEL), p, w_gate, w_proj)


def _ple_bwd(xn, p, w_gate, w_proj, dh, layer):
    s = xn.shape[0]
    tm = 512

    def body(x_ref, p_ref, wg_ref, wp_ref, dh_ref, dpre_ref, dproj_ref):
        gate = jax.nn.sigmoid(_dot(x_ref[...].astype(MXU_DTYPE), wg_ref[...].astype(MXU_DTYPE), NN))
        proj = _dot(p_ref[...].astype(MXU_DTYPE), wp_ref[...].astype(MXU_DTYPE), NN)
        dh = dh_ref[...]
        dpre_ref[...] = (dh * proj * gate * (1.0 - gate)).astype(dpre_ref.dtype)
        dproj_ref[...] = (dh * gate).astype(dproj_ref.dtype)

    row = pl.BlockSpec((tm, D_MODEL), lambda i: (i, 0))
    return pl.pallas_call(
        body, name="ple_bwd", grid=(s // tm,),
        in_specs=[row, pl.BlockSpec((tm, PLE_DIM), lambda i: (i, 0))] + _ple_weight_specs(layer) + [row],
        out_specs=[row, row],
        out_shape=[jax.ShapeDtypeStruct((s, D_MODEL), MXU_DTYPE)] * 2,
        compiler_params=_params(("parallel",)),
    )(xn, p, w_gate, w_proj, dh)


O_SGU = 3 * ATTN_W
O_SSM = O_SGU + 2 * SGU_W


def _layer_consts(w, i):
    causal = jnp.asarray(np.tril(np.ones((SGU_CHUNK, SGU_CHUNK), np.float32)))
    return {
        "sgu_w_mask": w["sgu_w"][i] * causal,
        "sgu_b_t": w["sgu_b"][i].T,
        "ssm_ops": _ssm_operands(w["ssm_a_re"][i], w["ssm_a_im"][i], w["ssm_log_dt"][i], w["ssm_b_re"][i],
                                 w["ssm_b_im"][i], w["ssm_c_re"][i], w["ssm_c_im"][i]),
    }


def _layer_fwd(h0, p_i, w, i, bias):
    c = _layer_consts(w, i)
    xn1, qkv, zs, us = _in_proj(h0, w["norm_attn_g"][i], w["w_in"], i)
    y_attn, lse = _attn2_fwd(qkv, bias)
    y_sgu = _sgu_fwd(zs, w["sgu_ln_g"][i], w["sgu_ln_b"][i], c["sgu_w_mask"], c["sgu_b_t"])
    y_ssm, entry = _ssm_fwd(us, c["ssm_ops"], w["ssm_d"][i], w["ssm_glu_w"][i], w["ssm_glu_b"][i])
    mix = _mix_fwd(y_attn, y_sgu, y_ssm, w["branch_norm_g"][i])
    h1 = _matmul(mix, w["w_out"], name="out_proj", out_dtype=F32, tm=512, tn=1024, residual=h0, layer=i)
    xn2, hh = _ffn_up(h1, w["norm_ffn_g"][i], w["ffn_w_up"], i)
    act = _ffn_gate_fwd(hh, w["ffn_conv_w"][i], w["ffn_conv_b"][i])
    h2 = _matmul(act, w["ffn_w_down"], name="ffn_down", out_dtype=F32, tm=512, tn=1024, residual=h1, layer=i)
    h3, xn3 = _ple_fwd(h2, w["norm_ple_g"][i], p_i, w["ple_w_gate"], w["ple_w_proj"], i)
    saved = dict(h0=h0, xn1=xn1, qkv=qkv, zs=zs, us=us, y_attn=y_attn, lse=lse, y_sgu=y_sgu, y_ssm=y_ssm,
                 entry=entry, mix=mix, h1=h1, xn2=xn2, hh=hh, act=act, h2=h2, xn3=xn3, consts=c)
    return h3, saved


def _layer_bwd(dh3, sv, p_i, w, i, bias):
    c = sv["consts"]
    g = {}
    dpre, dproj = _ple_bwd(sv["xn3"], p_i, w["ple_w_gate"], w["ple_w_proj"], dh3, i)
    g["ple_w_gate"] = _matmul_tn(sv["xn3"], dpre, name="d_ple_w_gate", tk=1024, tn=1024)
    g["ple_w_proj"] = _matmul_tn(p_i, dproj, name="d_ple_w_proj", tk=256, tn=1024)
    dh2, g["norm_ple_g"] = _matmul_rms_bwd(dpre, w["ple_w_gate"], sv["h2"], w["norm_ple_g"][i], dh3,
                                           name="d_xn_ple", layer=i, tm=512)
    g["ffn_w_down"] = _matmul_tn(sv["act"], dh2, name="d_ffn_w_down", tk=1408, tn=1024)
    dact = _matmul(dh2, w["ffn_w_down"], name="d_ffn_act", out_dtype=MXU_DTYPE, tm=512, tn=1408, trans_b=True, layer=i)
    dhh, g["ffn_conv_w"], g["ffn_conv_b"] = _ffn_gate_bwd(sv["hh"], w["ffn_conv_w"][i], w["ffn_conv_b"][i], dact)
    g["ffn_w_up"] = _matmul_tn(sv["xn2"], dhh, name="d_ffn_w_up", tk=1024, tn=1408)
    dh1, g["norm_ffn_g"] = _matmul_rms_bwd(dhh, w["ffn_w_up"], sv["h1"], w["norm_ffn_g"][i], dh2,
                                           name="d_xn_ffn", layer=i, tm=256)
    g["w_out"] = _matmul_tn(sv["mix"], dh1, name="d_w_out", tk=1024, tn=1024)
    dmix = _matmul(dh1, w["w_out"], name="d_mix", out_dtype=F32, tm=512, tn=1024, trans_b=True, layer=i)
    dy_attn, delta, dy_sgu, dy_ssm, g["branch_norm_g"] = _mix_bwd(sv["y_attn"], sv["y_sgu"], sv["y_ssm"],
                                                                  w["branch_norm_g"][i], dmix)
    dq, dk, dv, ek, ev, dbias = _attn2_bwd(sv["qkv"], bias, sv["lse"], delta, dy_attn)
    dzs, g["sgu_ln_g"], g["sgu_ln_b"], dsw, dsb = _sgu_bwd(sv["zs"], w["sgu_ln_g"][i], w["sgu_ln_b"][i],
                                                          c["sgu_w_mask"], c["sgu_b_t"], dy_sgu)
    causal = jnp.asarray(np.tril(np.ones((SGU_CHUNK, SGU_CHUNK), np.float32)))
    g["sgu_w"] = dsw * causal
    g["sgu_b"] = dsb.T
    dus, dbb, dcm, da, g["ssm_d"], g["ssm_glu_w"], g["ssm_glu_b"] = _ssm_bwd(
        sv["us"], sv["entry"], c["ssm_ops"], w["ssm_d"][i], w["ssm_glu_w"][i], w["ssm_glu_b"][i], dy_ssm)
    dbb5 = dbb.reshape(SSM_G, SSM_C, 2, SSM_G, SSM_N)
    dbbar = jnp.einsum("gcpgn->pgnc", dbb5)
    dcm5 = dcm.reshape(2, SSM_G, SSM_N, SSM_G, SSM_C)
    dcc = jnp.einsum("pgngc->pgcn", dcm5)
    g["ssm_c_re"] = dcc[0]
    g["ssm_c_im"] = -dcc[1]
    da2 = da.reshape(2, SSM_G, SSM_N)
    _, vjp = jax.vjp(_ssm_discretize, w["ssm_a_re"][i], w["ssm_a_im"][i], w["ssm_log_dt"][i],
                     w["ssm_b_re"][i], w["ssm_b_im"][i])
    (g["ssm_a_re"], g["ssm_a_im"], g["ssm_log_dt"], g["ssm_b_re"], g["ssm_b_im"]) = vjp(
        (da2[0], da2[1], dbbar[0], dbbar[1]))
    dz = _attn2_bwd_sum(dq, dk, dv, ek, ev, dzs, dus)
    g["w_in"] = _matmul_tn(sv["xn1"], dz, name="d_w_in", tk=1024, tn=1152)
    dh0, g["norm_attn_g"] = _matmul_rms_bwd(dz, w["w_in"], sv["h0"], w["norm_attn_g"][i], dh1,
                                            name="d_xn_attn", layer=i, tm=512)
    for k in ("norm_ple_g", "norm_ffn_g", "branch_norm_g", "norm_attn_g", "sgu_ln_g", "sgu_ln_b", "ssm_d",
              "ssm_glu_b", "ffn_conv_b"):
        g[k] = g[k].reshape(-1)
    return dh0, g, dbias


def _local_step(x, p, target, w, ff_interleaved=False):
    ff_names = ("ffn_conv_b",) if ff_interleaved else FF_SHARDED + ("ffn_conv_b",)
    w = dict(w)
    for k in ff_names:
        w[k] = _interleave_ff(w[k])
    bias = _bias_build(w["rel_bias"])
    h = x
    saved = []
    for i in range(DEPTH):
        h, sv = _layer_fwd(h, p[i], w, i, bias)
        saved.append(sv)
    loss, dh, dgf = _loss_head(h, w["final_norm_g"], target)
    layer_grads = [None] * DEPTH
    dbias = None
    for i in reversed(range(DEPTH)):
        dh, layer_grads[i], db = _layer_bwd(dh, saved[i], p[i], w, i, bias)
        dbias = db if dbias is None else dbias + db
    grads = {k: jnp.stack([layer_grads[i][k] for i in range(DEPTH)]) for k in layer_grads[0]}
    for k in ff_names:
        grads[k] = _deinterleave_ff(grads[k])
    grads["rel_bias"] = _bias_reduce(dbias)
    grads["final_norm_g"] = dgf.reshape(-1)
    return loss, dh, grads


def _pad_rows(a2, mult=16):
    r = (-a2.shape[0]) % mult
    return a2 if r == 0 else jnp.concatenate([a2, jnp.zeros((r, a2.shape[1]), a2.dtype)], axis=0)


def _as_rows(a, rows=None):
    size = int(np.prod(a.shape))
    if rows is None:
        rows = -(-size // (16 * PACK_COLS)) * 16
    if size % PACK_COLS:
        a = jnp.pad(a.reshape(-1), (0, (-size) % PACK_COLS))
    a2 = a.reshape(-1, PACK_COLS)
    return jnp.pad(a2, ((0, rows - a2.shape[0]), (0, 0)))


def _shard_shape(name):
    full, ax = BIG_FULL[name]
    shp = [DEPTH] + list(full)
    shp[ax] //= N_CHIPS
    return tuple(shp)


EXACT_NAMES = ("ffn_conv_w",)


def _pack_rows_of(name):
    n = int(np.prod(_shard_shape(name))) * (2 if name in EXACT_NAMES else 1)
    rows = -(-n // PACK_COLS)
    return -(-rows // 16) * 16


def _pack_shards(shards, dtype, exact=False):
    split_words = exact and jnp.dtype(dtype).itemsize == 2
    parts = []
    for n in BIG_NAMES:
        a = shards[n]
        if split_words and n in EXACT_NAMES:
            a = lax.bitcast_convert_type(a.astype(F32), dtype)
        parts.append(_as_rows(a.astype(dtype), _pack_rows_of(n)))
    used = sum(pt.shape[0] for pt in parts)
    parts.append(jnp.zeros((PACK_ROWS - used, PACK_COLS), dtype))
    return jnp.concatenate(parts, axis=0)


def _unpack_shard(flat, name, exact=False):
    off = 0
    for n in BIG_NAMES:
        if n == name:
            break
        off += _pack_rows_of(n)
    shp = _shard_shape(name)
    cnt = int(np.prod(shp))
    if exact and name in EXACT_NAMES and jnp.dtype(flat.dtype).itemsize == 2:
        vec = flat[off:off + _pack_rows_of(name)].reshape(-1)
        return lax.bitcast_convert_type(vec[:2 * cnt].reshape(shp + (2,)), F32)
    if cnt % PACK_COLS == 0:
        return flat[off:off + cnt // PACK_COLS].reshape(shp)
    return flat[off:off + _pack_rows_of(name)].reshape(-1)[:cnt].reshape(shp)


FF_SHARDED = ("ffn_w_up", "ffn_conv_w")
FF_CHIP_ORDER = (0, 2, 1, 3)


def _chip_order(name):
    return FF_CHIP_ORDER if name in FF_SHARDED else tuple(range(N_CHIPS))


def _split_full(full, name):
    _, ax = BIG_FULL[name]
    parts = jnp.split(full, N_CHIPS, axis=ax)
    out = [None] * N_CHIPS
    for j, k in enumerate(_chip_order(name)):
        out[k] = parts[j]
    return out


def _join_shards(shards, name):
    _, ax = BIG_FULL[name]
    return jnp.concatenate([shards[k] for k in _chip_order(name)], axis=ax)


def _small_shapes(w):
    return [(n, w[n].shape) for n in SMALL_NAMES]


def _small_rows(shp):
    return -(-int(np.prod(shp)) // (8 * PACK_COLS)) * 8


def _pack_small(d):
    parts = [_as_rows(d[n].astype(F32), _small_rows(d[n].shape)) for n in SMALL_NAMES]
    used = sum(pt.shape[0] for pt in parts)
    parts.append(jnp.zeros((SMALL_ROWS - used, PACK_COLS), F32))
    return jnp.concatenate(parts, axis=0)


def _unpack_small(flat, shapes):
    out, off = {}, 0
    for n, shp in shapes:
        cnt = int(np.prod(shp))
        rows = _small_rows(shp)
        if cnt % PACK_COLS == 0:
            out[n] = flat[off:off + cnt // PACK_COLS].reshape(shp)
        else:
            out[n] = flat[off:off + rows].reshape(-1)[:cnt].reshape(shp)
        off += rows
    return out


MESH = pl.DeviceIdType.MESH
ANY = pl.BlockSpec(memory_space=pl.ANY)


def _me():
    return lax.axis_index("x"), lax.axis_index("y"), lax.axis_index("c")


def _other_chips(x, y):
    return [(1 - x, y), (x, 1 - y), (1 - x, 1 - y)]


def _gather_weights(wflat):
    def body(w_ref, out_ref, send_sems, recv_sems):
        x, y, c = _me()
        sibling = (x, y, 1 - c)
        chips = _other_chips(x, y)

        def rows(chip, half):
            return out_ref.at[2 * chip[0] + chip[1], pl.ds(half * PACK_HALF, PACK_HALF), :]

        def copy(k, chip, half, to, src=None):
            return pltpu.make_async_remote_copy(
                src_ref=rows(chip, half) if src is None else src, dst_ref=rows(chip, half),
                send_sem=send_sems.at[k], recv_sem=recv_sems.at[k], device_id=to, device_id_type=MESH)

        my_half = w_ref.at[pl.ds(c * PACK_HALF, PACK_HALF), :]
        first = [copy(j, (x, y), c, (*chip, c), src=my_half) for j, chip in enumerate(chips)]
        for cp in first:
            cp.start()
        passed = [copy(3 + j, chip, c, sibling) for j, chip in enumerate(chips)]
        for j, chip in enumerate(chips):
            copy(j, chip, c, (x, y, c)).wait_recv()
            passed[j].start()
        for j, chip in enumerate(chips):
            copy(3 + j, chip, 1 - c, (x, y, c)).wait_recv()
        for cp in first + passed:
            cp.wait_send()

    return pl.pallas_call(
        body, name="gather_weights", in_specs=[ANY], out_specs=ANY,
        out_shape=jax.ShapeDtypeStruct((N_CHIPS, PACK_ROWS, PACK_COLS), wflat.dtype),
        scratch_shapes=[pltpu.SemaphoreType.DMA((6,)), pltpu.SemaphoreType.DMA((6,))],
    )(wflat)


def _fill_own_shard(wall, wflat, chip_idx):
    rows = PACK_ROWS // 8

    def body(idx_ref, w_ref, wall_ref, o_ref):
        del idx_ref, wall_ref
        o_ref[...] = w_ref[...]

    return pl.pallas_call(
        body, name="fill_own_shard",
        grid_spec=pltpu.PrefetchScalarGridSpec(
            num_scalar_prefetch=1, grid=(PACK_ROWS // rows,),
            in_specs=[pl.BlockSpec((rows, PACK_COLS), lambda i, idx: (i, 0)), ANY],
            out_specs=pl.BlockSpec((None, rows, PACK_COLS), lambda i, idx: (idx[0], i, 0))),
        out_shape=jax.ShapeDtypeStruct(wall.shape, wall.dtype),
        input_output_aliases={2: 0},
        compiler_params=_params(("parallel",)),
    )(chip_idx, wflat, wall)


def _exchange_partials(gb, gs):
    def body(gb_ref, gs_ref, half_ref, small_ref, send_sems, recv_sems, local_sem):
        x, y, c = _me()
        me_idx = 4 * x + 2 * y + c
        mine = pltpu.make_async_copy(gs_ref, small_ref.at[me_idx], local_sem)
        mine.start()
        d2d = pltpu.make_async_remote_copy(
            src_ref=gb_ref.at[:, pl.ds((1 - c) * PACK_HALF, PACK_HALF), :], dst_ref=half_ref,
            send_sem=send_sems.at[0], recv_sem=recv_sems.at[0], device_id=(x, y, 1 - c), device_id_type=MESH)
        d2d.start()
        copies = []
        for k in range(1, N_DEV):
            fx, fy, fc = (k >> 2) & 1, (k >> 1) & 1, k & 1
            peer = (x ^ fx, y ^ fy, c ^ fc)
            copies.append(pltpu.make_async_remote_copy(
                src_ref=gs_ref, dst_ref=small_ref.at[me_idx], send_sem=send_sems.at[k], recv_sem=recv_sems.at[k],
                device_id=peer, device_id_type=MESH))
        for cp in copies:
            cp.start()
        for k in range(1, N_DEV):
            fx, fy, fc = (k >> 2) & 1, (k >> 1) & 1, k & 1
            peer_idx = 4 * (x ^ fx) + 2 * (y ^ fy) + (c ^ fc)
            pltpu.make_async_remote_copy(
                src_ref=gs_ref, dst_ref=small_ref.at[peer_idx], send_sem=send_sems.at[k], recv_sem=recv_sems.at[k],
                device_id=(x, y, c), device_id_type=MESH).wait_recv()
        d2d.wait_recv()
        d2d.wait_send()
        for cp in copies:
            cp.wait_send()
        mine.wait()

    return pl.pallas_call(
        body, name="exchange_partials", in_specs=[ANY, pl.BlockSpec(memory_space=pltpu.VMEM)], out_specs=[ANY, ANY],
        out_shape=[jax.ShapeDtypeStruct((N_CHIPS, PACK_HALF, PACK_COLS), gb.dtype),
                   jax.ShapeDtypeStruct((N_DEV, SMALL_ROWS, PACK_COLS), F32)],
        scratch_shapes=[pltpu.SemaphoreType.DMA((N_DEV,)), pltpu.SemaphoreType.DMA((N_DEV,)), pltpu.SemaphoreType.DMA],
    )(gb, gs)


RED_ROWS = 256


def _chip_partials(gb, sib, c_idx):
    nrow = PACK_HALF // RED_ROWS

    def body(c_ref, a_ref, b_ref, o_ref):
        del c_ref
        o_ref[...] = (a_ref[...].astype(F32) + b_ref[...].astype(F32)).astype(o_ref.dtype)

    blk = (1, RED_ROWS, PACK_COLS)
    return pl.pallas_call(
        body, name="chip_partials",
        grid_spec=pltpu.PrefetchScalarGridSpec(
            num_scalar_prefetch=1, grid=(N_CHIPS, nrow),
            in_specs=[pl.BlockSpec(blk, lambda k, i, c: (k, c[0] * nrow + i, 0)),
                      pl.BlockSpec(blk, lambda k, i, c: (k, i, 0))],
            out_specs=pl.BlockSpec(blk, lambda k, i, c: (k, i, 0))),
        out_shape=jax.ShapeDtypeStruct((N_CHIPS, PACK_HALF, PACK_COLS), gb.dtype),
        compiler_params=_params(("parallel", "parallel")),
    )(c_idx, gb, sib)


def _scatter_partials(pc):
    def body(pc_ref, out_ref, send_sems, recv_sems):
        x, y, c = _me()
        chips = _other_chips(x, y)
        copies = [pltpu.make_async_remote_copy(
            src_ref=pc_ref.at[2 * chip[0] + chip[1]], dst_ref=out_ref.at[k],
            send_sem=send_sems.at[k], recv_sem=recv_sems.at[k], device_id=(*chip, c), device_id_type=MESH)
            for k, chip in enumerate(chips)]
        for cp in copies:
            cp.start()
        for cp in copies:
            cp.wait_recv()
        for cp in copies:
            cp.wait_send()

    return pl.pallas_call(
        body, name="scatter_partials", in_specs=[ANY], out_specs=ANY,
        out_shape=jax.ShapeDtypeStruct((3, PACK_HALF, PACK_COLS), pc.dtype),
        scratch_shapes=[pltpu.SemaphoreType.DMA((3,)), pltpu.SemaphoreType.DMA((3,))],
    )(pc)


def _final_half(gb, sib, recv, idx):
    nrow = PACK_HALF // RED_ROWS

    def body(idx_ref, a_ref, b_ref, r_ref, o_ref):
        del idx_ref
        acc = a_ref[0].astype(F32) + b_ref[0].astype(F32)
        for k in range(3):
            acc = acc + r_ref[k].astype(F32)
        o_ref[...] = acc

    return pl.pallas_call(
        body, name="final_half",
        grid_spec=pltpu.PrefetchScalarGridSpec(
            num_scalar_prefetch=1, grid=(nrow,),
            in_specs=[pl.BlockSpec((1, RED_ROWS, PACK_COLS), lambda i, idx: (idx[0], idx[1] * nrow + i, 0)),
                      pl.BlockSpec((1, RED_ROWS, PACK_COLS), lambda i, idx: (idx[0], i, 0)),
                      pl.BlockSpec((3, RED_ROWS, PACK_COLS), lambda i, idx: (0, i, 0))],
            out_specs=pl.BlockSpec((RED_ROWS, PACK_COLS), lambda i, idx: (i, 0))),
        out_shape=jax.ShapeDtypeStruct((PACK_HALF, PACK_COLS), F32),
        compiler_params=_params(("parallel",)),
    )(idx, gb, sib, recv)


def _share_halves(half):
    def body(h_ref, out_ref, send_sem, recv_sem):
        x, y, c = _me()
        cp = pltpu.make_async_remote_copy(src_ref=h_ref, dst_ref=out_ref, send_sem=send_sem, recv_sem=recv_sem,
                                          device_id=(x, y, 1 - c), device_id_type=MESH)
        cp.start()
        cp.wait_recv()
        cp.wait_send()

    return pl.pallas_call(
        body, name="share_halves", in_specs=[ANY], out_specs=ANY,
        out_shape=jax.ShapeDtypeStruct((PACK_HALF, PACK_COLS), F32),
        scratch_shapes=[pltpu.SemaphoreType.DMA, pltpu.SemaphoreType.DMA],
    )(half)


def _sum_small(allsmall):
    def body(a_ref, o_ref):
        acc = a_ref[0]
        for k in range(1, N_DEV):
            acc = acc + a_ref[k]
        o_ref[...] = acc

    tr = 96
    return pl.pallas_call(
        body, name="sum_small", grid=(SMALL_ROWS // tr,),
        in_specs=[pl.BlockSpec((N_DEV, tr, PACK_COLS), lambda i: (0, i, 0))],
        out_specs=pl.BlockSpec((tr, PACK_COLS), lambda i: (i, 0)),
        out_shape=jax.ShapeDtypeStruct((SMALL_ROWS, PACK_COLS), F32),
        compiler_params=_params(("parallel",)),
    )(allsmall)


def _adamw(w, g, m, v, *, name):
    shape = w.shape
    cols = shape[-1]
    as2 = lambda t: t.reshape(-1, cols)
    w2, g2, m2, v2 = as2(w), as2(g), as2(m), as2(v)
    rows = w2.shape[0]
    tr = rows
    if rows * cols * 4 > (1 << 20):
        tr = _tile(rows, max(8, (1 << 20) // (cols * 4) // 8 * 8), 8)

    def body(w_ref, g_ref, m_ref, v_ref, d_ref, mo_ref, vo_ref):
        gg = g_ref[...]
        mn = ADAM_B1 * m_ref[...] + (1.0 - ADAM_B1) * gg
        vn = ADAM_B2 * v_ref[...] + (1.0 - ADAM_B2) * (gg * gg)
        m_hat = mn / (1.0 - ADAM_B1 ** ADAM_STEP)
        v_hat = vn / (1.0 - ADAM_B2 ** ADAM_STEP)
        d_ref[...] = -ADAM_LR * (m_hat / (jnp.sqrt(v_hat) + ADAM_EPS) + ADAM_WD * w_ref[...])
        mo_ref[...] = mn
        vo_ref[...] = vn

    blk = pl.BlockSpec((tr, cols), lambda i: (i, 0))
    outs = pl.pallas_call(
        body, name=name, grid=(rows // tr,), in_specs=[blk] * 4, out_specs=[blk] * 3,
        out_shape=[jax.ShapeDtypeStruct((rows, cols), F32)] * 3,
        compiler_params=_params(("parallel",)),
    )(w2, g2, m2, v2)
    return tuple(t.reshape(shape) for t in outs)


def _adamw_many(ws, gs, ms, vs):
    n = len(ws)

    def body(*refs):
        for t in range(n):
            w_ref, g_ref, m_ref, v_ref = refs[t], refs[n + t], refs[2 * n + t], refs[3 * n + t]
            d_ref, mo_ref, vo_ref = refs[4 * n + t], refs[5 * n + t], refs[6 * n + t]
            gg = g_ref[...]
            mn = ADAM_B1 * m_ref[...] + (1.0 - ADAM_B1) * gg
            vn = ADAM_B2 * v_ref[...] + (1.0 - ADAM_B2) * (gg * gg)
            m_hat = mn / (1.0 - ADAM_B1 ** ADAM_STEP)
            v_hat = vn / (1.0 - ADAM_B2 ** ADAM_STEP)
            d_ref[...] = -ADAM_LR * (m_hat / (jnp.sqrt(v_hat) + ADAM_EPS) + ADAM_WD * w_ref[...])
            mo_ref[...] = mn
            vo_ref[...] = vn

    vmem = pl.BlockSpec(memory_space=pltpu.VMEM)
    outs = pl.pallas_call(
        body, name="adamw_small", in_specs=[vmem] * (4 * n), out_specs=[vmem] * (3 * n),
        out_shape=[jax.ShapeDtypeStruct(w.shape, F32) for w in ws] * 3,
        compiler_params=pltpu.CompilerParams(vmem_limit_bytes=VMEM_LIMIT_BYTES),
    )(*ws, *gs, *ms, *vs)
    return outs[:n], outs[n:2 * n], outs[2 * n:]


def kernel(x, p, rel_bias, norm_attn_g, w_in, sgu_ln_g, sgu_ln_b, sgu_w, sgu_b, ssm_a_re, ssm_a_im, ssm_log_dt, ssm_b_re, ssm_b_im, ssm_c_re, ssm_c_im, ssm_d, ssm_glu_w, ssm_glu_b, branch_norm_g, w_out, norm_ffn_g, ffn_w_up, ffn_conv_w, ffn_conv_b, ffn_w_down, norm_ple_g, ple_w_gate, ple_w_proj, final_norm_g, loss_target, m_rel_bias, m_norm_attn_g, m_w_in, m_sgu_ln_g, m_sgu_ln_b, m_sgu_w, m_sgu_b, m_ssm_a_re, m_ssm_a_im, m_ssm_log_dt, m_ssm_b_re, m_ssm_b_im, m_ssm_c_re, m_ssm_c_im, m_ssm_d, m_ssm_glu_w, m_ssm_glu_b, m_branch_norm_g, m_w_out, m_norm_ffn_g, m_ffn_w_up, m_ffn_conv_w, m_ffn_conv_b, m_ffn_w_down, m_norm_ple_g, m_ple_w_gate, m_ple_w_proj, m_final_norm_g, v_rel_bias, v_norm_attn_g, v_w_in, v_sgu_ln_g, v_sgu_ln_b, v_sgu_w, v_sgu_b, v_ssm_a_re, v_ssm_a_im, v_ssm_log_dt, v_ssm_b_re, v_ssm_b_im, v_ssm_c_re, v_ssm_c_im, v_ssm_d, v_ssm_glu_w, v_ssm_glu_b, v_branch_norm_g, v_w_out, v_norm_ffn_g, v_ffn_w_up, v_ffn_conv_w, v_ffn_conv_b, v_ffn_w_down, v_norm_ple_g, v_ple_w_gate, v_ple_w_proj, v_final_norm_g):
    args = dict(locals())
    wts = {n: args[n] for n in WEIGHT_NAMES}
    mom_m = {n: args["m_" + n] for n in WEIGHT_NAMES}
    mom_v = {n: args["v_" + n] for n in WEIGHT_NAMES}

    xi, yi, ci = _me()
    wflat = _pack_shards({n: wts[n] for n in BIG_NAMES}, MXU_DTYPE, exact=True)
    wall = _fill_own_shard(_gather_weights(wflat), wflat, jnp.stack([2 * xi + yi]).astype(jnp.int32))
    full = dict(wts)
    for n in BIG_NAMES:
        full[n] = _join_shards([_unpack_shard(wall[k], n, exact=True) for k in range(N_CHIPS)], n)
    full["ffn_conv_w"] = full["ffn_conv_w"].astype(F32)

    loss, dx, grads = _local_step(x[0], p[:, 0], loss_target[0], full, ff_interleaved=True)
    loss = lax.psum(loss[0, 0], MESH_AXES)

    xi, yi, ci = _me()
    stacked = {n: _split_full(grads[n], n) for n in BIG_NAMES}
    gb = jnp.stack([_pack_shards({n: stacked[n][k] for n in BIG_NAMES}, MXU_DTYPE) for k in range(N_CHIPS)])
    gs = _pack_small(grads)
    sib, allsmall = _exchange_partials(gb, gs)
    pc = _chip_partials(gb, sib, jnp.stack([ci]).astype(jnp.int32))
    recv = _scatter_partials(pc)
    half = _final_half(gb, sib, recv, jnp.stack([2 * xi + yi, ci]).astype(jnp.int32))
    other = _share_halves(half)
    gflat = jnp.concatenate([jnp.where(ci == 0, half, other), jnp.where(ci == 0, other, half)], axis=0)
    gsmall = _unpack_small(_sum_small(allsmall), _small_shapes(wts))

    g_out, d_out, m_out, v_out = {}, {}, {}, {}
    for n in BIG_NAMES:
        g_out[n] = _unpack_shard(gflat, n)
        d_out[n], m_out[n], v_out[n] = _adamw(wts[n], g_out[n], mom_m[n], mom_v[n], name="adamw_" + n)
    d_sm, m_sm, v_sm = _adamw_many([wts[n] for n in SMALL_NAMES], [gsmall[n] for n in SMALL_NAMES],
                                   [mom_m[n] for n in SMALL_NAMES], [mom_v[n] for n in SMALL_NAMES])
    for t, n in enumerate(SMALL_NAMES):
        g_out[n], d_out[n], m_out[n], v_out[n] = gsmall[n], d_sm[t], m_sm[t], v_sm[t]

    return (loss, dx[None], *[g_out[n] for n in WEIGHT_NAMES], *[d_out[n] for n in WEIGHT_NAMES],
            *[m_out[n] for n in WEIGHT_NAMES], *[v_out[n] for n in WEIGHT_NAMES])
```

```python
import functools
import math

import numpy as np
import jax
import jax.numpy as jnp
from jax import lax
from jax.experimental import pallas as pl
from jax.experimental.pallas import tpu as pltpu

F32 = jnp.float32
MXU_DTYPE = jnp.bfloat16
VMEM_LIMIT_BYTES = 52 * 1024 * 1024

D_MODEL = 1024
DEPTH = 2
PLE_DIM = 256
HEAD_DIM = 64
N_HEADS = 8
ATTN_W = 512
QBLK = 128
BRANCH_DIL = (1, 4, 16)
N_BUCKETS = 32
REL_MAX_DIST = 2048
SGU_W = 256
SGU_G = 4
SGU_GW = 64
SGU_CHUNK = 128
SSM_W = 256
SSM_G = 16
SSM_C = 16
SSM_N = 64
NSTATE = SSM_G * SSM_N
D_FF = 2816
EPS = 1e-6
NEG_INF = -1e30
ATTN_SCALE = HEAD_DIM ** -0.5

ADAM_LR = 0.001
ADAM_B1 = 0.9
ADAM_B2 = 0.999
ADAM_EPS = 1e-08
ADAM_WD = 0.01
ADAM_STEP = 10

SSM_NSEG = 8
SSM_TSEG = 64
SSM_TB = SSM_NSEG * SSM_TSEG
SSM_LANE_CHUNK = 512

MESH_AXES = ("x", "y", "c")
N_CHIPS = 4
N_DEV = 8

BIG_NAMES = ("w_in", "ssm_glu_w", "w_out", "ffn_w_up", "ffn_conv_w", "ffn_w_down", "ple_w_gate", "ple_w_proj")
BIG_FULL = {
    "w_in": ((D_MODEL, 2304), 2),
    "ssm_glu_w": ((SSM_W, SSM_W), 1),
    "w_out": ((D_MODEL, D_MODEL), 1),
    "ffn_w_up": ((D_MODEL, 2 * D_FF), 2),
    "ffn_conv_w": ((3, 2 * D_FF), 2),
    "ffn_w_down": ((D_FF, D_MODEL), 1),
    "ple_w_gate": ((D_MODEL, D_MODEL), 1),
    "ple_w_proj": ((PLE_DIM, D_MODEL), 2),
}
PACK_COLS = 1024
PACK_ROWS = 6656
PACK_HALF = PACK_ROWS // 2

SMALL_NAMES = ("rel_bias", "norm_attn_g", "sgu_ln_g", "sgu_ln_b", "sgu_w", "sgu_b", "ssm_a_re", "ssm_a_im",
               "ssm_log_dt", "ssm_b_re", "ssm_b_im", "ssm_c_re", "ssm_c_im", "ssm_d", "ssm_glu_b",
               "branch_norm_g", "norm_ffn_g", "ffn_conv_b", "norm_ple_g", "final_norm_g")
SMALL_ROWS = 384

WEIGHT_NAMES = ("rel_bias", "norm_attn_g", "w_in", "sgu_ln_g", "sgu_ln_b", "sgu_w", "sgu_b", "ssm_a_re", "ssm_a_im",
                "ssm_log_dt", "ssm_b_re", "ssm_b_im", "ssm_c_re", "ssm_c_im", "ssm_d", "ssm_glu_w", "ssm_glu_b",
                "branch_norm_g", "w_out", "norm_ffn_g", "ffn_w_up", "ffn_conv_w", "ffn_conv_b", "ffn_w_down",
                "norm_ple_g", "ple_w_gate", "ple_w_proj", "final_norm_g")


def _params(sem):
    return pltpu.CompilerParams(dimension_semantics=sem, vmem_limit_bytes=VMEM_LIMIT_BYTES)


def _tile(n, cap, mult=128):
    if n <= cap:
        return n
    best = None
    for t in range(mult, cap + 1, mult):
        if n % t == 0:
            best = t
    assert best is not None, (n, cap)
    return best


def _gelu(x):
    return 0.5 * x * (1.0 + jnp.tanh(0.7978845608028654 * (x + 0.044715 * x * x * x)))


def _gelu_pair(x):
    x2 = x * x
    t = jnp.tanh(0.7978845608028654 * x * (1.0 + 0.044715 * x2))
    half = 0.5 * (1.0 + t)
    return x * half, half + 0.5 * x * (1.0 - t * t) * (0.7978845608028654 + 3.0 * 0.044715 * 0.7978845608028654 * x2)


def _dot(a, b, dims):
    return lax.dot_general(a, b, (dims, ((), ())), preferred_element_type=F32)


def _dotf(a, b, dims):
    return _dot(a.astype(MXU_DTYPE), b.astype(MXU_DTYPE), dims)


NN = ((1,), (0,))
NT = ((1,), (1,))
TN = ((0,), (0,))


def _matmul(a, b, *, name, out_dtype, tm, tn, trans_b=False, residual=None, layer=None):
    m, k = a.shape
    n = b.shape[-2] if trans_b else b.shape[-1]
    tm = _tile(m, tm, 8)
    tn = _tile(n, tn)
    dims = NT if trans_b else NN
    lead = () if layer is None else (None,)
    lidx = () if layer is None else (layer,)

    def body(*refs):
        if residual is None:
            a_ref, b_ref, o_ref = refs
        else:
            a_ref, b_ref, r_ref, o_ref = refs
        acc = _dot(a_ref[...].astype(MXU_DTYPE), b_ref[...].astype(MXU_DTYPE), dims)
        if residual is not None:
            acc = acc + r_ref[...]
        o_ref[...] = acc.astype(o_ref.dtype)

    b_spec = (pl.BlockSpec(lead + (tn, k), lambda i, j: lidx + (j, 0)) if trans_b
              else pl.BlockSpec(lead + (k, tn), lambda i, j: lidx + (0, j)))
    in_specs = [pl.BlockSpec((tm, k), lambda i, j: (i, 0)), b_spec]
    args = [a, b]
    if residual is not None:
        in_specs.append(pl.BlockSpec((tm, tn), lambda i, j: (i, j)))
        args.append(residual)
    return pl.pallas_call(
        body, name=name, grid=(m // tm, n // tn), in_specs=in_specs,
        out_specs=pl.BlockSpec((tm, tn), lambda i, j: (i, j)),
        out_shape=jax.ShapeDtypeStruct((m, n), out_dtype),
        compiler_params=_params(("parallel", "parallel")),
    )(*args)


def _matmul_tn(a, g, *, name, tk, tn, tm=512):
    m, k = a.shape
    n = g.shape[1]
    tk = _tile(k, tk)
    tn = _tile(n, tn)
    tm = _tile(m, tm, 8)

    def body(a_ref, g_ref, o_ref):
        @pl.when(pl.program_id(2) == 0)
        def _():
            o_ref[...] = jnp.zeros_like(o_ref)

        o_ref[...] += _dot(a_ref[...].astype(MXU_DTYPE), g_ref[...].astype(MXU_DTYPE), TN)

    return pl.pallas_call(
        body, name=name, grid=(k // tk, n // tn, m // tm),
        in_specs=[pl.BlockSpec((tm, tk), lambda i, j, s: (s, i)),
                  pl.BlockSpec((tm, tn), lambda i, j, s: (s, j))],
        out_specs=pl.BlockSpec((tk, tn), lambda i, j, s: (i, j)),
        out_shape=jax.ShapeDtypeStruct((k, n), F32),
        compiler_params=_params(("parallel", "parallel", "arbitrary")),
    )(a, g)


ROWS = 512


def _rms_fwd(h, g, *, name):
    s, d = h.shape

    def body(h_ref, g_ref, o_ref):
        x = h_ref[...]
        r = lax.rsqrt(jnp.mean(x * x, axis=-1, keepdims=True) + EPS)
        o_ref[...] = (x * r * g_ref[...]).astype(o_ref.dtype)

    return pl.pallas_call(
        body, name=name, grid=(s // ROWS,),
        in_specs=[pl.BlockSpec((ROWS, d), lambda i: (i, 0)), pl.BlockSpec((1, d), lambda i: (0, 0))],
        out_specs=pl.BlockSpec((ROWS, d), lambda i: (i, 0)),
        out_shape=jax.ShapeDtypeStruct((s, d), MXU_DTYPE),
        compiler_params=_params(("parallel",)),
    )(h, g.reshape(1, d))


def _rms_bwd(h, g, dxn, dres, *, name):
    s, d = h.shape

    def body(h_ref, g_ref, dxn_ref, dres_ref, dh_ref, dg_ref):
        @pl.when(pl.program_id(0) == 0)
        def _():
            dg_ref[...] = jnp.zeros_like(dg_ref)

        x = h_ref[...]
        r = lax.rsqrt(jnp.mean(x * x, axis=-1, keepdims=True) + EPS)
        xhat = x * r
        dxn = dxn_ref[...].astype(F32)
        dg_ref[...] += jnp.sum(dxn * xhat, axis=0, keepdims=True)
        dxh = dxn * g_ref[...]
        dh_ref[...] = dres_ref[...] + r * (dxh - xhat * jnp.mean(dxh * xhat, axis=-1, keepdims=True))

    row = pl.BlockSpec((ROWS, d), lambda i: (i, 0))
    vec = pl.BlockSpec((1, d), lambda i: (0, 0))
    return pl.pallas_call(
        body, name=name, grid=(s // ROWS,), in_specs=[row, vec, row, row], out_specs=[row, vec],
        out_shape=[jax.ShapeDtypeStruct((s, d), F32), jax.ShapeDtypeStruct((1, d), F32)],
        compiler_params=_params(("arbitrary",)),
    )(h, g.reshape(1, d), dxn, dres)


def _matmul_rms_bwd(a, b, h, g, dres, *, name, layer, tm):
    s, k = a.shape
    d = b.shape[-2]

    def body(a_ref, b_ref, h_ref, g_ref, dres_ref, dh_ref, dg_ref):
        @pl.when(pl.program_id(0) == 0)
        def _():
            dg_ref[...] = jnp.zeros_like(dg_ref)

        dxn = _dot(a_ref[...].astype(MXU_DTYPE), b_ref[...].astype(MXU_DTYPE), NT)
        x = h_ref[...]
        r = lax.rsqrt(jnp.mean(x * x, axis=-1, keepdims=True) + EPS)
        xhat = x * r
        dg_ref[...] += jnp.sum(dxn * xhat, axis=0, keepdims=True)
        dxh = dxn * g_ref[...]
        dh_ref[...] = dres_ref[...] + r * (dxh - xhat * jnp.mean(dxh * xhat, axis=-1, keepdims=True))

    row = pl.BlockSpec((tm, d), lambda i: (i, 0))
    vec = pl.BlockSpec((1, d), lambda i: (0, 0))
    return pl.pallas_call(
        body, name=name, grid=(s // tm,),
        in_specs=[pl.BlockSpec((tm, k), lambda i: (i, 0)), pl.BlockSpec((None, d, k), lambda i: (layer, 0, 0)),
                  row, vec, row],
        out_specs=[row, vec],
        out_shape=[jax.ShapeDtypeStruct((s, d), F32), jax.ShapeDtypeStruct((1, d), F32)],
        compiler_params=_params(("arbitrary",)),
    )(a, b, h, g.reshape(1, d), dres)


def _loss_head(h, g, target):
    s, d = h.shape

    def body(h_ref, g_ref, t_ref, loss_ref, dh_ref, dg_ref):
        @pl.when(pl.program_id(0) == 0)
        def _():
            loss_ref[...] = jnp.zeros_like(loss_ref)
            dg_ref[...] = jnp.zeros_like(dg_ref)

        x = h_ref[...]
        r = lax.rsqrt(jnp.mean(x * x, axis=-1, keepdims=True) + EPS)
        xhat = x * r
        err = xhat * g_ref[...] - t_ref[...]
        loss_ref[...] += 0.5 * jnp.sum(jnp.mean(err * err, axis=-1, keepdims=True), axis=0, keepdims=True)
        dy = err / d
        dg_ref[...] += jnp.sum(dy * xhat, axis=0, keepdims=True)
        dxh = dy * g_ref[...]
        dh_ref[...] = r * (dxh - xhat * jnp.mean(dxh * xhat, axis=-1, keepdims=True))

    row = pl.BlockSpec((ROWS, d), lambda i: (i, 0))
    vec = pl.BlockSpec((1, d), lambda i: (0, 0))
    one = pl.BlockSpec((1, 1), lambda i: (0, 0))
    return pl.pallas_call(
        body, name="loss_head", grid=(s // ROWS,), in_specs=[row, vec, row], out_specs=[one, row, vec],
        out_shape=[jax.ShapeDtypeStruct((1, 1), F32), jax.ShapeDtypeStruct((s, d), F32),
                   jax.ShapeDtypeStruct((1, d), F32)],
        compiler_params=_params(("arbitrary",)),
    )(h, g.reshape(1, d), target)


def _t5_bucket(dist):
    max_exact = N_BUCKETS // 2
    dd = np.maximum(dist, 0)
    large = max_exact + (np.log(np.maximum(dd, 1) / max_exact) / np.log(REL_MAX_DIST / max_exact)
                         * (N_BUCKETS - max_exact)).astype(np.int32)
    large = np.minimum(large, N_BUCKETS - 1)
    return np.where(dd < max_exact, dd, large).astype(np.int32)


def _bucket_table():
    qq = np.arange(QBLK)[:, None]
    kk = np.arange(QBLK)[None, :]
    out = np.zeros((len(BRANCH_DIL), 2, QBLK, QBLK), np.int32)
    for b, dil in enumerate(BRANCH_DIL):
        out[b, 0] = _t5_bucket((qq - kk + QBLK) * dil)
        out[b, 1] = _t5_bucket((qq - kk) * dil)
    return out


BIAS_TILE = 2 * QBLK


def _bias_build(rel_bias):
    idx = jnp.asarray(_bucket_table())

    def body(idx_ref, rb_ref, o_ref):
        ch = pl.program_id(1)
        row = lax.broadcasted_iota(jnp.int32, (QBLK, QBLK), 0)
        col = lax.broadcasted_iota(jnp.int32, (QBLK, QBLK), 1)
        for part in range(2):
            ids = idx_ref[0, 1 - part]
            valid = (col <= row) if part == 0 else (col >= row)
            for h in range(2):
                acc = jnp.zeros((QBLK, QBLK), F32)
                for b in range(N_BUCKETS):
                    acc = jnp.where(ids == b, rb_ref[b, 2 * ch + h], acc)
                o_ref[0, 0, QBLK * h:QBLK * (h + 1), QBLK * part:QBLK * (part + 1)] = jnp.where(valid, acc, NEG_INF)

    return pl.pallas_call(
        body, name="attn_bias_build", grid=(len(BRANCH_DIL), N_HEADS // 2),
        in_specs=[pl.BlockSpec((1, 2, QBLK, QBLK), lambda b, c: (b, 0, 0, 0)),
                  pl.BlockSpec(memory_space=pltpu.SMEM)],
        out_specs=pl.BlockSpec((1, 1, BIAS_TILE, BIAS_TILE), lambda b, c: (b, c, 0, 0)),
        out_shape=jax.ShapeDtypeStruct((len(BRANCH_DIL), N_HEADS // 2, BIAS_TILE, BIAS_TILE), F32),
        compiler_params=_params(("parallel", "parallel")),
    )(idx, rel_bias)


def _bias_reduce(dbias):
    idx = jnp.asarray(_bucket_table())
    nb = len(BRANCH_DIL)

    def body(idx_ref, d_ref, o_ref):
        def per_bucket(b, carry):
            for h in range(N_HEADS):
                tot = jnp.zeros((), F32)
                for br in range(nb):
                    for part in range(2):
                        tile = d_ref[br, h // 2, QBLK * (h % 2):QBLK * (h % 2 + 1), QBLK * part:QBLK * (part + 1)]
                        tot = tot + jnp.sum(jnp.where(idx_ref[br, 1 - part] == b, tile, 0.0))
                o_ref[b, h] = tot
            return carry

        lax.fori_loop(0, N_BUCKETS, per_bucket, 0)

    return pl.pallas_call(
        body, name="attn_bias_reduce",
        in_specs=[pl.BlockSpec(memory_space=pltpu.VMEM), pl.BlockSpec(memory_space=pltpu.VMEM)],
        out_specs=pl.BlockSpec(memory_space=pltpu.SMEM),
        out_shape=jax.ShapeDtypeStruct((N_BUCKETS, N_HEADS), F32),
        compiler_params=pltpu.CompilerParams(vmem_limit_bytes=VMEM_LIMIT_BYTES),
    )(idx, dbias)


def _band_masks(c):
    row = lax.broadcasted_iota(jnp.int32, (QBLK, QBLK), 0)
    col = lax.broadcasted_iota(jnp.int32, (QBLK, QBLK), 1)
    mask_cur = col <= row
    mask_prev = jnp.logical_and(col >= row, c > 0)
    return mask_prev, mask_cur


def _attn_specs(dil):
    blk = (QBLK, ATTN_W)
    q = pl.BlockSpec(blk, lambda r, c: (c, 3 * r))
    kp = pl.BlockSpec(blk, lambda r, c: (jnp.maximum(c - 1, 0), 3 * r + 1))
    kc = pl.BlockSpec(blk, lambda r, c: (c, 3 * r + 1))
    vp = pl.BlockSpec(blk, lambda r, c: (jnp.maximum(c - 1, 0), 3 * r + 2))
    vc = pl.BlockSpec(blk, lambda r, c: (c, 3 * r + 2))
    return [q, kp, kc, vp, vc]


def _attn_fwd_branch(qkv, bias, state, *, branch, last):
    dil = BRANCH_DIL[branch]
    s = qkv.shape[0]
    n = s // dil
    nblk = n // QBLK
    first = state is None

    def body(*refs):
        q_ref, kp_ref, kc_ref, vp_ref, vc_ref, b_ref = refs[:6]
        if first:
            outs = refs[6:]
        else:
            acc_ref, m_ref, l_ref = refs[6:9]
            outs = refs[9:]
        mask_prev, mask_cur = _band_masks(pl.program_id(1))
        for h in range(N_HEADS):
            sl = slice(HEAD_DIM * h, HEAD_DIM * (h + 1))
            qh = q_ref[:, sl]
            s_c = _dot(qh, kc_ref[:, sl], NT) * ATTN_SCALE + b_ref[0, 1, h]
            s_p = _dot(qh, kp_ref[:, sl], NT) * ATTN_SCALE + b_ref[0, 0, h]
            s_c = jnp.where(mask_cur, s_c, NEG_INF)
            s_p = jnp.where(mask_prev, s_p, NEG_INF)
            m_blk = jnp.maximum(jnp.max(s_c, axis=-1, keepdims=True), jnp.max(s_p, axis=-1, keepdims=True))
            if first:
                m_new = m_blk
            else:
                m_old = m_ref[:, sl][:, :1]
                m_new = jnp.maximum(m_old, m_blk)
            p_c = jnp.exp(s_c - m_new)
            p_p = jnp.exp(s_p - m_new)
            l_new = jnp.sum(p_c, axis=-1, keepdims=True) + jnp.sum(p_p, axis=-1, keepdims=True)
            acc = (_dot(p_c.astype(MXU_DTYPE), vc_ref[:, sl], NN)
                   + _dot(p_p.astype(MXU_DTYPE), vp_ref[:, sl], NN))
            if not first:
                alpha = jnp.exp(m_old - m_new)
                l_new = l_new + alpha * l_ref[:, sl][:, :1]
                acc = acc + alpha * acc_ref[:, sl]
            if last:
                outs[0][:, sl] = acc / l_new
                outs[1][:, sl] = jnp.broadcast_to(m_new + jnp.log(l_new), (QBLK, HEAD_DIM))
            else:
                outs[0][:, sl] = acc
                outs[1][:, sl] = jnp.broadcast_to(m_new, (QBLK, HEAD_DIM))
                outs[2][:, sl] = jnp.broadcast_to(l_new, (QBLK, HEAD_DIM))

    st_spec = pl.BlockSpec((QBLK, ATTN_W), lambda r, c: (c, r))
    in_specs = _attn_specs(dil) + [pl.BlockSpec((1, 2, N_HEADS, QBLK, QBLK), lambda r, c: (branch, 0, 0, 0, 0))]
    qv = qkv.reshape(n, dil * 3 * ATTN_W)
    args = [qv] * 5 + [bias]
    if not first:
        in_specs += [st_spec] * 3
        args += [t.reshape(n, dil * ATTN_W) for t in state]
    n_out = 2 if last else 3
    outs = pl.pallas_call(
        body, name=f"attn_fwd_b{branch}", grid=(dil, nblk), in_specs=in_specs,
        out_specs=[st_spec] * n_out,
        out_shape=[jax.ShapeDtypeStruct((n, dil * ATTN_W), F32)] * n_out,
        compiler_params=_params(("parallel", "parallel")),
    )(*args)
    return tuple(t.reshape(s, ATTN_W) for t in outs)


def _attn_fwd(qkv, bias):
    state = None
    for b in range(len(BRANCH_DIL)):
        state = _attn_fwd_branch(qkv, bias, state, branch=b, last=(b == len(BRANCH_DIL) - 1))
    return state


def _attn_bwd_branch(qkv, bias, o, lse, do, *, branch):
    dil = BRANCH_DIL[branch]
    s = qkv.shape[0]
    n = s // dil
    nblk = n // QBLK

    def body(q_ref, kp_ref, kc_ref, vp_ref, vc_ref, b_ref, o_ref, l_ref, do_ref,
             dq_ref, dka_ref, dkb_ref, dva_ref, dvb_ref, db_ref):
        @pl.when(jnp.logical_and(pl.program_id(0) == 0, pl.program_id(1) == 0))
        def _():
            db_ref[...] = jnp.zeros_like(db_ref)

        mask_prev, mask_cur = _band_masks(pl.program_id(1))
        for h in range(N_HEADS):
            sl = slice(HEAD_DIM * h, HEAD_DIM * (h + 1))
            qh = q_ref[:, sl]
            doh = do_ref[:, sl]
            lh = l_ref[:, sl][:, :1]
            delta = jnp.sum(doh * o_ref[:, sl], axis=-1, keepdims=True)
            do_m = doh.astype(MXU_DTYPE)
            s_c = _dot(qh, kc_ref[:, sl], NT) * ATTN_SCALE + b_ref[0, 1, h]
            s_p = _dot(qh, kp_ref[:, sl], NT) * ATTN_SCALE + b_ref[0, 0, h]
            p_c = jnp.exp(jnp.where(mask_cur, s_c, NEG_INF) - lh)
            p_p = jnp.exp(jnp.where(mask_prev, s_p, NEG_INF) - lh)
            ds_c = p_c * (_dot(do_m, vc_ref[:, sl], NT) - delta)
            ds_p = p_p * (_dot(do_m, vp_ref[:, sl], NT) - delta)
            db_ref[0, 1, h] += ds_c
            db_ref[0, 0, h] += ds_p
            ds_c_m = ds_c.astype(MXU_DTYPE)
            ds_p_m = ds_p.astype(MXU_DTYPE)
            dq = _dot(ds_c_m, kc_ref[:, sl], NN) + _dot(ds_p_m, kp_ref[:, sl], NN)
            dq_ref[:, sl] = (dq * ATTN_SCALE).astype(dq_ref.dtype)
            dka_ref[:, sl] = (_dot(ds_c_m, qh, TN) * ATTN_SCALE).astype(dka_ref.dtype)
            dkb_ref[:, sl] = (_dot(ds_p_m, qh, TN) * ATTN_SCALE).astype(dkb_ref.dtype)
            dva_ref[:, sl] = _dot(p_c.astype(MXU_DTYPE), do_m, TN).astype(dva_ref.dtype)
            dvb_ref[:, sl] = _dot(p_p.astype(MXU_DTYPE), do_m, TN).astype(dvb_ref.dtype)

    st_spec = pl.BlockSpec((QBLK, ATTN_W), lambda r, c: (c, r))
    b_in = pl.BlockSpec((1, 2, N_HEADS, QBLK, QBLK), lambda r, c: (branch, 0, 0, 0, 0))
    b_out = pl.BlockSpec((1, 2, N_HEADS, QBLK, QBLK), lambda r, c: (0, 0, 0, 0, 0))
    qv = qkv.reshape(n, dil * 3 * ATTN_W)
    view = lambda t: t.reshape(n, dil * ATTN_W)
    outs = pl.pallas_call(
        body, name=f"attn_bwd_b{branch}", grid=(dil, nblk),
        in_specs=_attn_specs(dil) + [b_in, st_spec, st_spec, st_spec],
        out_specs=[st_spec] * 5 + [b_out],
        out_shape=[jax.ShapeDtypeStruct((n, dil * ATTN_W), MXU_DTYPE)] * 5
        + [jax.ShapeDtypeStruct((1, 2, N_HEADS, QBLK, QBLK), F32)],
        compiler_params=_params(("arbitrary", "arbitrary")),
    )(qv, qv, qv, qv, qv, bias, view(o), view(lse), view(do))
    return tuple(t.reshape(s, ATTN_W) for t in outs[:5]) + (outs[5],)


def _attn_bwd(qkv, bias, o, lse, do):
    s = qkv.shape[0]
    nb = s // QBLK
    parts = [_attn_bwd_branch(qkv, bias, o, lse, do, branch=b) for b in range(len(BRANCH_DIL))]
    dbias = jnp.concatenate([p[5] for p in parts], axis=0)

    def body(*refs):
        o_ref = refs[-1]
        i = pl.program_id(0)
        dq = jnp.zeros((QBLK, ATTN_W), F32)
        dk = jnp.zeros((QBLK, ATTN_W), F32)
        dv = jnp.zeros((QBLK, ATTN_W), F32)
        for b, dil in enumerate(BRANCH_DIL):
            dq_ref, dka_ref, dkb_ref, dva_ref, dvb_ref = refs[5 * b:5 * b + 5]
            inside = i + dil < nb
            dq = dq + dq_ref[...].astype(F32)
            dk = dk + dka_ref[...].astype(F32) + jnp.where(inside, dkb_ref[...].astype(F32), 0.0)
            dv = dv + dva_ref[...].astype(F32) + jnp.where(inside, dvb_ref[...].astype(F32), 0.0)
        o_ref[:, 0:ATTN_W] = dq.astype(o_ref.dtype)
        o_ref[:, ATTN_W:2 * ATTN_W] = dk.astype(o_ref.dtype)
        o_ref[:, 2 * ATTN_W:3 * ATTN_W] = dv.astype(o_ref.dtype)

    in_specs, args = [], []
    for b, dil in enumerate(BRANCH_DIL):
        here = pl.BlockSpec((QBLK, ATTN_W), lambda i: (i, 0))
        ahead = pl.BlockSpec((QBLK, ATTN_W), functools.partial(lambda i, d: (jnp.minimum(i + d, nb - 1), 0), d=dil))
        in_specs += [here, here, ahead, here, ahead]
        args += list(parts[b][:5])
    dqkv = pl.pallas_call(
        body, name="attn_bwd_sum", grid=(nb,), in_specs=in_specs,
        out_specs=pl.BlockSpec((QBLK, 3 * ATTN_W), lambda i: (i, 0)),
        out_shape=jax.ShapeDtypeStruct((s, 3 * ATTN_W), MXU_DTYPE),
        compiler_params=_params(("parallel",)),
    )(*args)
    return dqkv, dbias


ATTN_IO_DTYPE = F32
ABLK = 2048
N_CHUNK = ATTN_W // 128


def _rows(start, dil):
    if dil > 1:
        return pl.ds(start, QBLK, stride=dil)
    return pl.ds(pl.multiple_of(start, QBLK), QBLK)


def _low_head():
    return lax.broadcasted_iota(jnp.int32, (QBLK, 128), 1) < HEAD_DIM


def _head_split(t):
    low = _low_head()
    zero = jnp.zeros_like(t)
    return jnp.where(low, t, zero), jnp.where(low, zero, t)


def _tile_bias(b_ref, branch, first):
    bias = b_ref[branch]
    if first is None:
        return bias
    col = lax.broadcasted_iota(jnp.int32, (BIAS_TILE, BIAS_TILE), 1)
    return jnp.where(jnp.logical_and(first, col >= QBLK), NEG_INF, bias)


def _loop(n, fn):
    if n == 1:
        fn(jnp.int32(0), 0)
    elif n > 1:
        lax.fori_loop(0, n, fn, 0, unroll=4)


def _for_each_tile(tile, c):
    for branch, dil in enumerate(BRANCH_DIL):
        span = QBLK * dil

        def edge(r, carry, branch=branch, span=span):
            tile(branch, r, False, ABLK - span + r, c == 0)
            return carry

        def inner(t, carry, branch=branch, span=span, dil=dil):
            start = (1 + t // dil) * span + t % dil
            tile(branch, start, True, start - span, None)
            return carry

        _loop(dil, edge)
        _loop((ABLK // span - 1) * dil, inner)


def _attn_chunk_specs(nb):
    blk = (None, ABLK, 128)
    prev = lambda c: jnp.maximum(c - 1, 0)
    return [pl.BlockSpec(blk, lambda ch, c: (ch, c, 0)),
            pl.BlockSpec(blk, lambda ch, c: (N_CHUNK + ch, c, 0)),
            pl.BlockSpec(blk, lambda ch, c: (2 * N_CHUNK + ch, c, 0)),
            pl.BlockSpec(blk, lambda ch, c: (N_CHUNK + ch, prev(c), 0)),
            pl.BlockSpec(blk, lambda ch, c: (2 * N_CHUNK + ch, prev(c), 0)),
            pl.BlockSpec((len(BRANCH_DIL), None, BIAS_TILE, BIAS_TILE), lambda ch, c: (0, ch, 0, 0))]


def _rms_rows(x, g):
    r = lax.rsqrt(jnp.mean(x * x, axis=-1, keepdims=True) + EPS)
    return (x * r * g).astype(MXU_DTYPE)


def _in_proj(h, gain, w_in, layer):
    s, k = h.shape
    tm = 512
    nch = O_SGU // 128

    def body(h_ref, g_ref, w_ref, xn_ref, qkv_ref, zs_ref, us_ref):
        xn = _rms_rows(h_ref[...], g_ref[...])
        xn_ref[...] = xn
        acc = _dot(xn, w_ref[...].astype(MXU_DTYPE), NN)
        for j in range(nch):
            blk = acc[:, 128 * j:128 * (j + 1)]
            if j < N_CHUNK:
                blk = blk * ATTN_SCALE
            qkv_ref[j] = blk.astype(qkv_ref.dtype)
        zs_ref[...] = acc[:, O_SGU:O_SSM]
        us_ref[...] = acc[:, O_SSM:]

    n = w_in.shape[-1]
    return pl.pallas_call(
        body, name="in_proj", grid=(s // tm,),
        in_specs=[pl.BlockSpec((tm, k), lambda i: (i, 0)), pl.BlockSpec((1, k), lambda i: (0, 0)),
                  pl.BlockSpec((None, k, n), lambda i: (layer, 0, 0))],
        out_specs=[pl.BlockSpec((tm, k), lambda i: (i, 0)), pl.BlockSpec((nch, tm, 128), lambda i: (0, i, 0)),
                   pl.BlockSpec((tm, O_SSM - O_SGU), lambda i: (i, 0)), pl.BlockSpec((tm, n - O_SSM), lambda i: (i, 0))],
        out_shape=[jax.ShapeDtypeStruct((s, k), MXU_DTYPE), jax.ShapeDtypeStruct((nch, s, 128), ATTN_IO_DTYPE),
                   jax.ShapeDtypeStruct((s, O_SSM - O_SGU), F32), jax.ShapeDtypeStruct((s, n - O_SSM), F32)],
        compiler_params=_params(("parallel",)),
    )(h, gain.reshape(1, k), w_in)


def _ffn_up(h, gain, w_up, layer):
    s, k = h.shape
    n = w_up.shape[-1]
    tm, tn = 1024, CONV_COLS

    def body(h_ref, g_ref, w_ref, xn_ref, o_ref):
        @pl.when(pl.program_id(1) == 0)
        def _():
            xn_ref[...] = _rms_rows(h_ref[...], g_ref[...])

        o_ref[...] = _dot(xn_ref[...], w_ref[...].astype(MXU_DTYPE), NN).astype(o_ref.dtype)

    return pl.pallas_call(
        body, name="ffn_up", grid=(s // tm, n // tn),
        in_specs=[pl.BlockSpec((tm, k), lambda i, j: (i, 0)), pl.BlockSpec((1, k), lambda i, j: (0, 0)),
                  pl.BlockSpec((None, k, tn), lambda i, j: (layer, 0, j))],
        out_specs=[pl.BlockSpec((tm, k), lambda i, j: (i, 0)), pl.BlockSpec((tm, tn), lambda i, j: (i, j))],
        out_shape=[jax.ShapeDtypeStruct((s, k), MXU_DTYPE), jax.ShapeDtypeStruct((s, n), MXU_DTYPE)],
        compiler_params=_params(("parallel", "arbitrary")),
    )(h, gain.reshape(1, k), w_up)


def _attn2_fwd(qkv_c, bias):
    s = qkv_c.shape[1]
    nb = s // ABLK
    last = len(BRANCH_DIL) - 1

    def body(q_ref, kc_ref, vc_ref, kp_ref, vp_ref, b_ref, o_ref, l_ref, acc_s, m_s, l_s):
        low = _low_head()
        e_st = jnp.concatenate(_head_split(jnp.ones((QBLK, 128), MXU_DTYPE)) * 2, axis=0)

        def tile(branch, start, prev_in_block, pstart, first):
            dil = BRANCH_DIL[branch]
            rq, rp = _rows(start, dil), _rows(pstart, dil)
            k_ref, v_ref = (kc_ref, vc_ref) if prev_in_block else (kp_ref, vp_ref)
            q_st = jnp.concatenate(_head_split(q_ref[rq, :].astype(MXU_DTYPE)), axis=0)
            k_st = jnp.concatenate([kc_ref[rq, :].astype(MXU_DTYPE), k_ref[rp, :].astype(MXU_DTYPE)], axis=0)
            v_st = jnp.concatenate(_head_split(vc_ref[rq, :].astype(MXU_DTYPE))
                                   + _head_split(v_ref[rp, :].astype(MXU_DTYPE)), axis=0)
            sc = _dot(q_st, k_st, NT) + _tile_bias(b_ref, branch, first)
            m_new = jnp.max(sc, axis=-1, keepdims=True)
            if branch > 0:
                m_old2 = m_s[rq, :]
                m_old = jnp.concatenate([m_old2[:, 0:1], m_old2[:, HEAD_DIM:HEAD_DIM + 1]], axis=0)
                m_new = jnp.maximum(m_old, m_new)
                alpha = jnp.exp(m_old - m_new)
            p = jnp.exp(sc - m_new).astype(MXU_DTYPE)
            lhs = jnp.concatenate([p[:QBLK, :QBLK], p[QBLK:, :QBLK], p[:QBLK, QBLK:], p[QBLK:, QBLK:]], axis=1)
            acc2 = _dot(lhs, v_st, NN)
            sum2 = _dot(lhs, e_st, NN)
            m2 = jnp.where(low, m_new[:QBLK], m_new[QBLK:])
            if branch > 0:
                a2 = jnp.where(low, alpha[:QBLK], alpha[QBLK:])
                acc2 = acc2 + a2 * acc_s[rq, :]
                sum2 = sum2 + a2 * l_s[rq, :]
            if branch == last:
                o_ref[rq, :] = acc2 / sum2
                l_ref[rq, :] = m2 + jnp.log(sum2)
            else:
                acc_s[rq, :] = acc2
                m_s[rq, :] = m2
                l_s[rq, :] = sum2

        _for_each_tile(tile, pl.program_id(1))

    out_spec = pl.BlockSpec((None, ABLK, 128), lambda ch, c: (ch, c, 0))
    return pl.pallas_call(
        body, name="attn_fwd", grid=(N_CHUNK, nb), in_specs=_attn_chunk_specs(nb),
        out_specs=[out_spec, out_spec],
        out_shape=[jax.ShapeDtypeStruct((N_CHUNK, s, 128), F32)] * 2,
        scratch_shapes=[pltpu.VMEM((ABLK, 128), F32)] * 3,
        compiler_params=_params(("parallel", "arbitrary")),
    )(qkv_c, qkv_c, qkv_c, qkv_c, qkv_c, bias)


def _attn2_bwd(qkv_c, bias, lse_c, delta_c, do_c):
    s = qkv_c.shape[1]
    nb = s // ABLK
    nbr = len(BRANCH_DIL)

    def body(q_ref, kc_ref, vc_ref, kp_ref, vp_ref, b_ref, l_ref, dl_ref, do_ref,
             dq_ref, dk_ref, dv_ref, *rest):
        ek_refs, ev_refs, db_ref = rest[:nbr], rest[nbr:2 * nbr], rest[2 * nbr]
        c = pl.program_id(1)

        @pl.when(c == 0)
        def _():
            db_ref[...] = jnp.zeros_like(db_ref)

        for r in (dq_ref, dk_ref, dv_ref) + tuple(ek_refs) + tuple(ev_refs):
            r[...] = jnp.zeros_like(r)

        def tile(branch, start, prev_in_block, pstart, first):
            dil = BRANCH_DIL[branch]
            rq, rp = _rows(start, dil), _rows(pstart, dil)
            k_ref, v_ref = (kc_ref, vc_ref) if prev_in_block else (kp_ref, vp_ref)
            kc2 = kc_ref[rq, :].astype(MXU_DTYPE)
            kp2 = k_ref[rp, :].astype(MXU_DTYPE)
            q_st = jnp.concatenate(_head_split(q_ref[rq, :].astype(MXU_DTYPE)), axis=0)
            do_st = jnp.concatenate(_head_split(do_ref[rq, :].astype(MXU_DTYPE)), axis=0)
            k_st = jnp.concatenate([kc2, kp2], axis=0)
            v_st = jnp.concatenate([vc_ref[rq, :].astype(MXU_DTYPE), v_ref[rp, :].astype(MXU_DTYPE)], axis=0)
            kh_st = jnp.concatenate(_head_split(kc2) + _head_split(kp2), axis=0)
            lse2 = l_ref[rq, :]
            del2 = dl_ref[rq, :]
            lse_st = jnp.concatenate([lse2[:, 0:1], lse2[:, HEAD_DIM:HEAD_DIM + 1]], axis=0)
            del_st = jnp.concatenate([del2[:, 0:1], del2[:, HEAD_DIM:HEAD_DIM + 1]], axis=0)
            p = jnp.exp(_dot(q_st, k_st, NT) + _tile_bias(b_ref, branch, first) - lse_st)
            ds = p * (_dot(do_st, v_st, NT) - del_st)
            db_ref[branch] += ds
            ds = ds.astype(MXU_DTYPE)
            p = p.astype(MXU_DTYPE)
            lhs = jnp.concatenate([ds[:QBLK, :QBLK], ds[QBLK:, :QBLK], ds[:QBLK, QBLK:], ds[QBLK:, QBLK:]], axis=1)
            dk_st = _dot(ds, q_st, TN)
            dv_st = _dot(p, do_st, TN)
            dq_ref[rq, :] += _dot(lhs, kh_st, NN)
            dk_ref[rq, :] += dk_st[:QBLK]
            dv_ref[rq, :] += dv_st[:QBLK]
            if prev_in_block:
                dk_ref[rp, :] += dk_st[QBLK:]
                dv_ref[rp, :] += dv_st[QBLK:]
            else:
                ek_refs[branch][rq, :] = dk_st[QBLK:]
                ev_refs[branch][rq, :] = dv_st[QBLK:]

        _for_each_tile(tile, c)

    blk = pl.BlockSpec((None, ABLK, 128), lambda ch, c: (ch, c, 0))
    outs = pl.pallas_call(
        body, name="attn_bwd", grid=(N_CHUNK, nb), in_specs=_attn_chunk_specs(nb) + [blk, blk, blk],
        out_specs=[blk] * (3 + 2 * nbr) + [pl.BlockSpec((nbr, None, BIAS_TILE, BIAS_TILE), lambda ch, c: (0, ch, 0, 0))],
        out_shape=[jax.ShapeDtypeStruct((N_CHUNK, s, 128), F32)] * (3 + 2 * nbr)
        + [jax.ShapeDtypeStruct((nbr, N_HEADS // 2, BIAS_TILE, BIAS_TILE), F32)],
        compiler_params=_params(("arbitrary", "arbitrary")),
    )(qkv_c, qkv_c, qkv_c, qkv_c, qkv_c, bias, lse_c, delta_c, do_c)
    return outs[0], outs[1], outs[2], outs[3:3 + nbr], outs[3 + nbr:3 + 2 * nbr], outs[3 + 2 * nbr]


def _attn2_bwd_sum(dq, dk, dv, ek, ev, dzs, dus):
    s = dq.shape[1]
    nrb = s // QBLK
    per_blk = ABLK // QBLK
    nbr = len(BRANCH_DIL)

    def body(*refs):
        dq_ref, dk_ref, dv_ref = refs[:3]
        ek_refs, ev_refs = refs[3:3 + nbr], refs[3 + nbr:3 + 2 * nbr]
        dzs_ref, dus_ref, o_ref = refs[3 + 2 * nbr:]
        i = pl.program_id(0)
        dkt, dvt = dk_ref[...], dv_ref[...]
        for b, dil in enumerate(BRANCH_DIL):
            j = i + dil
            ok = jnp.logical_and(j < nrb, j % per_blk < dil)
            dkt = dkt + jnp.where(ok, ek_refs[b][...], 0.0)
            dvt = dvt + jnp.where(ok, ev_refs[b][...], 0.0)
        for ch in range(N_CHUNK):
            o_ref[:, 128 * ch:128 * (ch + 1)] = (dq_ref[ch] * ATTN_SCALE).astype(o_ref.dtype)
            o_ref[:, ATTN_W + 128 * ch:ATTN_W + 128 * (ch + 1)] = dkt[ch].astype(o_ref.dtype)
            o_ref[:, 2 * ATTN_W + 128 * ch:2 * ATTN_W + 128 * (ch + 1)] = dvt[ch].astype(o_ref.dtype)
        o_ref[:, O_SGU:O_SSM] = dzs_ref[...].astype(o_ref.dtype)
        o_ref[:, O_SSM:] = dus_ref[...].astype(o_ref.dtype)

    here = pl.BlockSpec((N_CHUNK, QBLK, 128), lambda i: (0, i, 0))
    edge_specs = [pl.BlockSpec((N_CHUNK, QBLK, 128),
                               functools.partial(lambda i, d: (0, jnp.minimum(i + d, nrb - 1), 0), d=dil))
                  for dil in BRANCH_DIL]
    return pl.pallas_call(
        body, name="attn_bwd_sum", grid=(nrb,),
        in_specs=[here, here, here] + edge_specs + edge_specs
        + [pl.BlockSpec((QBLK, 2 * SGU_W), lambda i: (i, 0)), pl.BlockSpec((QBLK, SSM_W), lambda i: (i, 0))],
        out_specs=pl.BlockSpec((QBLK, O_SSM + SSM_W), lambda i: (i, 0)),
        out_shape=jax.ShapeDtypeStruct((s, O_SSM + SSM_W), MXU_DTYPE),
        compiler_params=_params(("parallel",)),
    )(dq, dk, dv, *ek, *ev, dzs, dus)


SGU_ROWS = 512


def _sgu_norm(v_g):
    mu = jnp.mean(v_g, axis=-1, keepdims=True)
    cen = v_g - mu
    var = jnp.mean(cen * cen, axis=-1, keepdims=True)
    rstd = lax.rsqrt(var + EPS)
    return cen * rstd, rstd


def _sgu_fwd(zs, ln_g, ln_b, w_mask, b_t):
    s = zs.shape[0]
    nch = SGU_ROWS // SGU_CHUNK

    def body(z_ref, g_ref, b_ref, w_ref, bt_ref, o_ref):
        gz = _gelu(z_ref[...])
        for g in range(SGU_G):
            sl = slice(SGU_GW * g, SGU_GW * (g + 1))
            u_g = gz[:, sl]
            xhat, _ = _sgu_norm(gz[:, SGU_W + SGU_GW * g:SGU_W + SGU_GW * (g + 1)])
            vn = (xhat * g_ref[:, sl] + b_ref[:, sl]).astype(MXU_DTYPE)
            wg = w_ref[g].astype(MXU_DTYPE)
            for ci in range(nch):
                rs = slice(SGU_CHUNK * ci, SGU_CHUNK * (ci + 1))
                mixed = _dot(wg, vn[rs], NN) + bt_ref[:, g:g + 1]
                o_ref[rs, sl] = u_g[rs] * mixed

    full = lambda shape: pl.BlockSpec(shape, lambda i: tuple(0 for _ in shape))
    return pl.pallas_call(
        body, name="sgu_fwd", grid=(s // SGU_ROWS,),
        in_specs=[pl.BlockSpec((SGU_ROWS, 2 * SGU_W), lambda i: (i, 0)), full((1, SGU_W)), full((1, SGU_W)),
                  full((SGU_G, SGU_CHUNK, SGU_CHUNK)), full((SGU_CHUNK, SGU_G))],
        out_specs=pl.BlockSpec((SGU_ROWS, SGU_W), lambda i: (i, 0)),
        out_shape=jax.ShapeDtypeStruct((s, SGU_W), F32),
        compiler_params=_params(("parallel",)),
    )(zs, ln_g.reshape(1, SGU_W), ln_b.reshape(1, SGU_W), w_mask, b_t)


def _sgu_bwd(zs, ln_g, ln_b, w_mask, b_t, dy):
    s = zs.shape[0]
    nch = SGU_ROWS // SGU_CHUNK

    def body(z_ref, g_ref, b_ref, w_ref, bt_ref, dy_ref, dz_ref, dg_ref, dbb_ref, dw_ref, dbt_ref):
        @pl.when(pl.program_id(0) == 0)
        def _():
            dg_ref[...] = jnp.zeros_like(dg_ref)
            dbb_ref[...] = jnp.zeros_like(dbb_ref)
            dw_ref[...] = jnp.zeros_like(dw_ref)
            dbt_ref[...] = jnp.zeros_like(dbt_ref)

        z = z_ref[...]
        gz, dgelu = _gelu_pair(z)
        dy = dy_ref[...]
        for g in range(SGU_G):
            sl = slice(SGU_GW * g, SGU_GW * (g + 1))
            sv = slice(SGU_W + SGU_GW * g, SGU_W + SGU_GW * (g + 1))
            u_g = gz[:, sl]
            xhat, rstd = _sgu_norm(gz[:, sv])
            gain = g_ref[:, sl]
            vn = (xhat * gain + b_ref[:, sl]).astype(MXU_DTYPE)
            wg = w_ref[g].astype(MXU_DTYPE)
            dy_g = dy[:, sl]
            dvn_parts = []
            for ci in range(nch):
                rs = slice(SGU_CHUNK * ci, SGU_CHUNK * (ci + 1))
                mixed = _dot(wg, vn[rs], NN) + bt_ref[:, g:g + 1]
                dz_ref[rs, sl] = (dy_g[rs] * mixed * dgelu[rs, sl]).astype(dz_ref.dtype)
                dmixed = dy_g[rs] * u_g[rs]
                dm = dmixed.astype(MXU_DTYPE)
                dvn_parts.append(_dot(wg, dm, TN))
                dw_ref[g] += _dot(dm, vn[rs], NT)
                dbt_ref[:, g:g + 1] += jnp.sum(dmixed, axis=-1, keepdims=True)
            dvn = jnp.concatenate(dvn_parts, axis=0)
            dg_ref[:, sl] += jnp.sum(dvn * xhat, axis=0, keepdims=True)
            dbb_ref[:, sl] += jnp.sum(dvn, axis=0, keepdims=True)
            dxh = dvn * gain
            dv = rstd * (dxh - jnp.mean(dxh, axis=-1, keepdims=True)
                         - xhat * jnp.mean(dxh * xhat, axis=-1, keepdims=True))
            dz_ref[:, sv] = (dv * dgelu[:, sv]).astype(dz_ref.dtype)

    full = lambda shape: pl.BlockSpec(shape, lambda i: tuple(0 for _ in shape))
    return pl.pallas_call(
        body, name="sgu_bwd", grid=(s // SGU_ROWS,),
        in_specs=[pl.BlockSpec((SGU_ROWS, 2 * SGU_W), lambda i: (i, 0)), full((1, SGU_W)), full((1, SGU_W)),
                  full((SGU_G, SGU_CHUNK, SGU_CHUNK)), full((SGU_CHUNK, SGU_G)),
                  pl.BlockSpec((SGU_ROWS, SGU_W), lambda i: (i, 0))],
        out_specs=[pl.BlockSpec((SGU_ROWS, 2 * SGU_W), lambda i: (i, 0)), full((1, SGU_W)), full((1, SGU_W)),
                   full((SGU_G, SGU_CHUNK, SGU_CHUNK)), full((SGU_CHUNK, SGU_G))],
        out_shape=[jax.ShapeDtypeStruct((s, 2 * SGU_W), MXU_DTYPE), jax.ShapeDtypeStruct((1, SGU_W), F32),
                   jax.ShapeDtypeStruct((1, SGU_W), F32), jax.ShapeDtypeStruct((SGU_G, SGU_CHUNK, SGU_CHUNK), F32),
                   jax.ShapeDtypeStruct((SGU_CHUNK, SGU_G), F32)],
        compiler_params=_params(("arbitrary",)),
    )(zs, ln_g.reshape(1, SGU_W), ln_b.reshape(1, SGU_W), w_mask, b_t, dy)


def _ssm_discretize(a_re, a_im, log_dt, b_re, b_im):
    dt = jnp.exp(log_dt)[:, None]
    mag = jnp.exp(a_re * dt)
    ab_re = mag * jnp.cos(a_im * dt)
    ab_im = mag * jnp.sin(a_im * dt)
    den = a_re * a_re + a_im * a_im
    f_re = ((ab_re - 1.0) * a_re + ab_im * a_im) / den
    f_im = (ab_im * a_re - (ab_re - 1.0) * a_im) / den
    bb_re = f_re[:, :, None] * b_re - f_im[:, :, None] * b_im
    bb_im = f_re[:, :, None] * b_im + f_im[:, :, None] * b_re
    return ab_re, ab_im, bb_re, bb_im


def _ssm_operands(a_re, a_im, log_dt, b_re, b_im, c_re, c_im):
    ab_re, ab_im, bb_re, bb_im = _ssm_discretize(a_re, a_im, log_dt, b_re, b_im)
    eye = jnp.eye(SSM_G, dtype=F32)
    b_blk = jnp.einsum("pgnc,gh->gcphn", jnp.stack([bb_re, bb_im]), eye).reshape(SSM_W, 2 * NSTATE)
    c_mat = jnp.einsum("pgcn,gh->pgnhc", jnp.stack([c_re, -c_im]), eye).reshape(2 * NSTATE, SSM_W)
    a_row = jnp.stack([ab_re.reshape(NSTATE), ab_im.reshape(NSTATE)])
    p_re, p_im = a_row[0:1], a_row[1:2]
    while p_re.shape[0] < SSM_TSEG:
        l_re, l_im = p_re[-1:], p_im[-1:]
        p_re, p_im = (jnp.concatenate([p_re, p_re * l_re - p_im * l_im]),
                      jnp.concatenate([p_im, p_re * l_im + p_im * l_re]))
    p_tab = jnp.stack([p_re, p_im])
    return b_blk.astype(MXU_DTYPE), c_mat.astype(MXU_DTYPE), a_row, p_tab


def _lane_chunks():
    return [(lo, lo + SSM_LANE_CHUNK) for lo in range(0, NSTATE, SSM_LANE_CHUNK)]


def _seg_rows(j):
    return pl.ds(pl.multiple_of(j * SSM_NSEG, SSM_NSEG), SSM_NSEG)


def _to_segments(t):
    s, w = t.shape
    return t.reshape(s // SSM_TB, SSM_NSEG, SSM_TSEG, w).transpose(0, 2, 1, 3).reshape(s, w)


def _from_segments(t):
    s, w = t.shape
    return t.reshape(s // SSM_TB, SSM_TSEG, SSM_NSEG, w).transpose(0, 2, 1, 3).reshape(s, w)


def _ssm_local_scan(buf, a_ref, *, reverse):
    ends_re, ends_im = [], []
    for lo, hi in _lane_chunks():
        are = jnp.broadcast_to(a_ref[0:1, lo:hi], (SSM_NSEG, hi - lo))
        aim = jnp.broadcast_to(a_ref[1:2, lo:hi], (SSM_NSEG, hi - lo))
        if reverse:
            aim = -aim

        def step(jj, carry, lo=lo, hi=hi, are=are, aim=aim):
            xr, xi = carry
            j = (SSM_TSEG - 1 - jj) if reverse else jj
            tr = buf[_seg_rows(j), lo:hi]
            ti = buf[_seg_rows(j), NSTATE + lo:NSTATE + hi]
            nr = are * xr - aim * xi + tr
            ni = are * xi + aim * xr + ti
            buf[_seg_rows(j), lo:hi] = nr
            buf[_seg_rows(j), NSTATE + lo:NSTATE + hi] = ni
            return nr, ni

        zero = jnp.zeros((SSM_NSEG, hi - lo), F32)
        xr, xi = lax.fori_loop(0, SSM_TSEG, step, (zero, zero), unroll=4)
        ends_re.append(xr)
        ends_im.append(xi)
    return jnp.concatenate(ends_re, axis=1), jnp.concatenate(ends_im, axis=1)


def _ssm_entry_states(ends_re, ends_im, carry_ref, p_ref, entry_ref, *, reverse):
    at_re = p_ref[0, SSM_TSEG - 1:SSM_TSEG, :]
    at_im = p_ref[1, SSM_TSEG - 1:SSM_TSEG, :]
    if reverse:
        at_im = -at_im
    cur_re = carry_ref[0:1, 0:NSTATE]
    cur_im = carry_ref[0:1, NSTATE:2 * NSTATE]
    order = range(SSM_NSEG - 1, -1, -1) if reverse else range(SSM_NSEG)
    for i in order:
        entry_ref[0, i:i + 1, 0:NSTATE] = cur_re
        entry_ref[0, i:i + 1, NSTATE:2 * NSTATE] = cur_im
        nxt_re = ends_re[i:i + 1] + at_re * cur_re - at_im * cur_im
        nxt_im = ends_im[i:i + 1] + at_re * cur_im + at_im * cur_re
        cur_re, cur_im = nxt_re, nxt_im
    carry_ref[0:1, 0:NSTATE] = cur_re
    carry_ref[0:1, NSTATE:2 * NSTATE] = cur_im


def _ssm_fixup(buf, p_ref, entry_ref, *, reverse):
    for lo, hi in _lane_chunks():
        e_re = entry_ref[0, :, lo:hi]
        e_im = entry_ref[0, :, NSTATE + lo:NSTATE + hi]

        def step(j, carry, lo=lo, hi=hi, e_re=e_re, e_im=e_im):
            jp = (SSM_TSEG - 1 - j) if reverse else j
            pr = p_ref[0, pl.ds(jp, 1), lo:hi]
            pi = p_ref[1, pl.ds(jp, 1), lo:hi]
            if reverse:
                pi = -pi
            buf[_seg_rows(j), lo:hi] = buf[_seg_rows(j), lo:hi] + pr * e_re - pi * e_im
            buf[_seg_rows(j), NSTATE + lo:NSTATE + hi] = (buf[_seg_rows(j), NSTATE + lo:NSTATE + hi]
                                                           + pr * e_im + pi * e_re)
            return carry

        lax.fori_loop(0, SSM_TSEG, step, 0, unroll=4)


def _ssm_fwd(u, ops, d_skip, glu_w, glu_b):
    b_blk, c_mat, a_row, p_tab = ops
    s = u.shape[0]
    nblk = s // SSM_TB

    def body(u_ref, bb_ref, cm_ref, a_ref, p_ref, d_ref, gw_ref, gb_ref, y_ref, entry_ref, xbuf, carry):
        @pl.when(pl.program_id(0) == 0)
        def _():
            carry[...] = jnp.zeros_like(carry)

        uu = u_ref[...]
        xbuf[...] = _dotf(uu, bb_ref[...], NN)
        ends_re, ends_im = _ssm_local_scan(xbuf, a_ref, reverse=False)
        _ssm_entry_states(ends_re, ends_im, carry, p_ref, entry_ref, reverse=False)
        _ssm_fixup(xbuf, p_ref, entry_ref, reverse=False)
        y = _dotf(xbuf[...],cm_ref[...], NN) + d_ref[...] * uu
        y2 = _gelu(y)
        gate = jax.nn.sigmoid(_dot(y2.astype(MXU_DTYPE), gw_ref[...].astype(MXU_DTYPE), NN) + gb_ref[...])
        y_ref[...] = y2 * gate

    full = lambda shape: pl.BlockSpec(shape, lambda i: tuple(0 for _ in shape))
    y_seg, entry = pl.pallas_call(
        body, name="ssm_fwd", grid=(nblk,),
        in_specs=[pl.BlockSpec((SSM_TB, SSM_W), lambda i: (i, 0)), full(b_blk.shape), full(c_mat.shape),
                  full(a_row.shape), full(p_tab.shape), full((1, SSM_W)), full((SSM_W, SSM_W)), full((1, SSM_W))],
        out_specs=[pl.BlockSpec((SSM_TB, SSM_W), lambda i: (i, 0)),
                   pl.BlockSpec((1, SSM_NSEG, 2 * NSTATE), lambda i: (i, 0, 0))],
        out_shape=[jax.ShapeDtypeStruct((s, SSM_W), F32), jax.ShapeDtypeStruct((nblk, SSM_NSEG, 2 * NSTATE), F32)],
        scratch_shapes=[pltpu.VMEM((SSM_TB, 2 * NSTATE), F32), pltpu.VMEM((SSM_NSEG, 2 * NSTATE), F32)],
        compiler_params=_params(("arbitrary",)),
    )(_to_segments(u), b_blk, c_mat, a_row, p_tab, d_skip.reshape(1, SSM_W), glu_w, glu_b.reshape(1, SSM_W))
    return _from_segments(y_seg), entry


def _ssm_bwd(u, entry, ops, d_skip, glu_w, glu_b, dout):
    b_blk, c_mat, a_row, p_tab = ops
    s = u.shape[0]
    nblk = s // SSM_TB

    def body(u_ref, en_ref, bb_ref, cm_ref, a_ref, p_ref, d_ref, gw_ref, gb_ref, do_ref,
             du_ref, dbb_ref, dcm_ref, da_ref, dd_ref, dgw_ref, dgb_ref, xbuf, gbuf, gcarry, gentry):
        @pl.when(pl.program_id(0) == 0)
        def _():
            gcarry[...] = jnp.zeros_like(gcarry)
            for r in (dbb_ref, dcm_ref, da_ref, dd_ref, dgw_ref, dgb_ref):
                r[...] = jnp.zeros_like(r)

        uu = u_ref[...]
        xbuf[...] = _dotf(uu, bb_ref[...], NN)
        _ssm_local_scan(xbuf, a_ref, reverse=False)
        _ssm_fixup(xbuf, p_ref, en_ref, reverse=False)
        y = _dotf(xbuf[...],cm_ref[...], NN) + d_ref[...] * uu
        y2, dgelu = _gelu_pair(y)
        y2m = y2.astype(MXU_DTYPE)
        gwm = gw_ref[...].astype(MXU_DTYPE)
        gate = jax.nn.sigmoid(_dot(y2m, gwm, NN) + gb_ref[...])
        dout = do_ref[...]
        dpre = dout * y2 * gate * (1.0 - gate)
        dprem = dpre.astype(MXU_DTYPE)
        dy2 = dout * gate + _dot(dprem, gwm, NT)
        dgw_ref[...] += _dot(y2m, dprem, TN)
        dgb_ref[...] += jnp.sum(dpre, axis=0, keepdims=True)
        dy = dy2 * dgelu
        dd_ref[...] += jnp.sum(dy * uu, axis=0, keepdims=True)
        dcm_ref[...] += _dotf(xbuf[...],dy, TN)
        gbuf[...] = _dotf(dy, cm_ref[...], NT)
        gs_re, gs_im = _ssm_local_scan(gbuf, a_ref, reverse=True)
        _ssm_entry_states(gs_re, gs_im, gcarry, p_ref, gentry, reverse=True)
        _ssm_fixup(gbuf, p_ref, gentry, reverse=True)
        du_ref[...] = (_dotf(gbuf[...], bb_ref[...], NT) + d_ref[...] * dy).astype(du_ref.dtype)
        dbb_ref[...] += _dotf(uu, gbuf[...], TN)
        for lo, hi in _lane_chunks():
            def step(j, carry, lo=lo, hi=hi):
                acc_re, acc_im = carry
                g_re = gbuf[_seg_rows(j), lo:hi]
                g_im = gbuf[_seg_rows(j), NSTATE + lo:NSTATE + hi]
                x_re = xbuf[_seg_rows(j - 1), lo:hi]
                x_im = xbuf[_seg_rows(j - 1), NSTATE + lo:NSTATE + hi]
                return acc_re + g_re * x_re + g_im * x_im, acc_im + g_im * x_re - g_re * x_im

            g0_re = gbuf[_seg_rows(0), lo:hi]
            g0_im = gbuf[_seg_rows(0), NSTATE + lo:NSTATE + hi]
            e_re = en_ref[0, :, lo:hi]
            e_im = en_ref[0, :, NSTATE + lo:NSTATE + hi]
            init = (g0_re * e_re + g0_im * e_im, g0_im * e_re - g0_re * e_im)
            acc_re, acc_im = lax.fori_loop(1, SSM_TSEG, step, init, unroll=4)
            da_ref[0:1, lo:hi] += jnp.sum(acc_re, axis=0, keepdims=True)
            da_ref[1:2, lo:hi] += jnp.sum(acc_im, axis=0, keepdims=True)

    full = lambda shape: pl.BlockSpec(shape, lambda i: tuple(0 for _ in shape))
    rev = pl.BlockSpec((SSM_TB, SSM_W), lambda i: (nblk - 1 - i, 0))
    outs = pl.pallas_call(
        body, name="ssm_bwd", grid=(nblk,),
        in_specs=[rev, pl.BlockSpec((1, SSM_NSEG, 2 * NSTATE), lambda i: (nblk - 1 - i, 0, 0)),
                  full(b_blk.shape), full(c_mat.shape), full(a_row.shape), full(p_tab.shape),
                  full((1, SSM_W)), full((SSM_W, SSM_W)), full((1, SSM_W)), rev],
        out_specs=[rev, full(b_blk.shape), full(c_mat.shape), full(a_row.shape), full((1, SSM_W)),
                   full((SSM_W, SSM_W)), full((1, SSM_W))],
        out_shape=[jax.ShapeDtypeStruct((s, SSM_W), MXU_DTYPE), jax.ShapeDtypeStruct(b_blk.shape, F32),
                   jax.ShapeDtypeStruct(c_mat.shape, F32), jax.ShapeDtypeStruct(a_row.shape, F32),
                   jax.ShapeDtypeStruct((1, SSM_W), F32), jax.ShapeDtypeStruct((SSM_W, SSM_W), F32),
                   jax.ShapeDtypeStruct((1, SSM_W), F32)],
        scratch_shapes=[pltpu.VMEM((SSM_TB, 2 * NSTATE), F32), pltpu.VMEM((SSM_TB, 2 * NSTATE), F32),
                        pltpu.VMEM((SSM_NSEG, 2 * NSTATE), F32), pltpu.VMEM((1, SSM_NSEG, 2 * NSTATE), F32)],
        compiler_params=_params(("arbitrary",)),
    )(_to_segments(u), entry, b_blk, c_mat, a_row, p_tab, d_skip.reshape(1, SSM_W), glu_w, glu_b.reshape(1, SSM_W),
      _to_segments(dout))
    return (_from_segments(outs[0]),) + tuple(outs[1:])


MIX_SEGS = ((0, ATTN_W), (ATTN_W, ATTN_W + SGU_W), (ATTN_W + SGU_W, D_MODEL))


def _chunks_to_rows(a_ref):
    return jnp.concatenate([a_ref[ch] for ch in range(N_CHUNK)], axis=1)


def _mix_fwd(y_attn_c, y_sgu, y_ssm, gain):
    s = y_sgu.shape[0]

    def body(a_ref, b_ref, c_ref, g_ref, o_ref):
        for x, (lo, hi) in zip((_chunks_to_rows(a_ref), b_ref[...], c_ref[...]), MIX_SEGS):
            r = lax.rsqrt(jnp.mean(x * x, axis=-1, keepdims=True) + EPS)
            o_ref[:, lo:hi] = (x * r * g_ref[:, lo:hi]).astype(o_ref.dtype)

    row = lambda w: pl.BlockSpec((ROWS, w), lambda i: (i, 0))
    return pl.pallas_call(
        body, name="mix_fwd", grid=(s // ROWS,),
        in_specs=[pl.BlockSpec((N_CHUNK, ROWS, 128), lambda i: (0, i, 0)), row(SGU_W), row(SSM_W),
                  pl.BlockSpec((1, D_MODEL), lambda i: (0, 0))],
        out_specs=row(D_MODEL), out_shape=jax.ShapeDtypeStruct((s, D_MODEL), MXU_DTYPE),
        compiler_params=_params(("parallel",)),
    )(y_attn_c, y_sgu, y_ssm, gain.reshape(1, D_MODEL))


def _mix_bwd(y_attn_c, y_sgu, y_ssm, gain, dmix):
    s = y_sgu.shape[0]

    def body(a_ref, b_ref, c_ref, g_ref, dm_ref, da_ref, dl_ref, db_ref, dc_ref, dg_ref):
        @pl.when(pl.program_id(0) == 0)
        def _():
            dg_ref[...] = jnp.zeros_like(dg_ref)

        grads = []
        for x, (lo, hi) in zip((_chunks_to_rows(a_ref), b_ref[...], c_ref[...]), MIX_SEGS):
            r = lax.rsqrt(jnp.mean(x * x, axis=-1, keepdims=True) + EPS)
            xhat = x * r
            dm = dm_ref[:, lo:hi].astype(F32)
            dg_ref[:, lo:hi] += jnp.sum(dm * xhat, axis=0, keepdims=True)
            dxh = dm * g_ref[:, lo:hi]
            grads.append(r * (dxh - xhat * jnp.mean(dxh * xhat, axis=-1, keepdims=True)))
        db_ref[...] = grads[1]
        dc_ref[...] = grads[2]
        low = lax.broadcasted_iota(jnp.int32, (ROWS, 128), 1) < HEAD_DIM
        for ch in range(N_CHUNK):
            d_c = grads[0][:, 128 * ch:128 * (ch + 1)]
            da_ref[ch] = d_c.astype(da_ref.dtype)
            prod = d_c * a_ref[ch]
            dl_ref[ch] = jnp.where(low, jnp.sum(prod[:, :HEAD_DIM], axis=-1, keepdims=True),
                                   jnp.sum(prod[:, HEAD_DIM:], axis=-1, keepdims=True))

    row = lambda w: pl.BlockSpec((ROWS, w), lambda i: (i, 0))
    vec = pl.BlockSpec((1, D_MODEL), lambda i: (0, 0))
    chunked = pl.BlockSpec((N_CHUNK, ROWS, 128), lambda i: (0, i, 0))
    return pl.pallas_call(
        body, name="mix_bwd", grid=(s // ROWS,),
        in_specs=[chunked, row(SGU_W), row(SSM_W), vec, row(D_MODEL)],
        out_specs=[chunked, chunked, row(SGU_W), row(SSM_W), vec],
        out_shape=[jax.ShapeDtypeStruct((N_CHUNK, s, 128), ATTN_IO_DTYPE), jax.ShapeDtypeStruct((N_CHUNK, s, 128), F32),
                   jax.ShapeDtypeStruct((s, SGU_W), F32), jax.ShapeDtypeStruct((s, SSM_W), F32),
                   jax.ShapeDtypeStruct((1, D_MODEL), F32)],
        compiler_params=_params(("arbitrary",)),
    )(y_attn_c, y_sgu, y_ssm, gain.reshape(1, D_MODEL), dmix)


CONV_ROWS = 256
CONV_COLS = 1408
CONV_PAIR = 2 * CONV_COLS
HALO = 16


def _interleave_ff(t):
    lead = t.shape[:-1]
    nb = D_FF // CONV_COLS
    return jnp.swapaxes(t.reshape(lead + (2, nb, CONV_COLS)), -3, -2).reshape(lead + (2 * D_FF,))


def _deinterleave_ff(t):
    lead = t.shape[:-1]
    nb = D_FF // CONV_COLS
    return jnp.swapaxes(t.reshape(lead + (nb, 2, CONV_COLS)), -3, -2).reshape(lead + (2 * D_FF,))


def _causal_taps(x, halo, first):
    patch = 8
    row = lax.broadcasted_iota(jnp.int32, (patch, x.shape[1]), 0)
    h1 = jnp.where(first, 0.0, halo[HALO - 1:HALO, :])
    h2 = jnp.where(first, 0.0, halo[HALO - 2:HALO - 1, :])
    r1 = pltpu.roll(x, 1, 0)
    r2 = pltpu.roll(x, 2, 0)
    top1 = jnp.where(row == 0, h1, r1[0:patch])
    top2 = jnp.where(row == 0, h2, jnp.where(row == 1, h1, r2[0:patch]))
    return jnp.concatenate([top1, r1[patch:]], axis=0), jnp.concatenate([top2, r2[patch:]], axis=0)


def _conv_in_specs():
    halo_idx = lambda i: jnp.maximum(i * (CONV_ROWS // HALO) - 1, 0)
    return [pl.BlockSpec((CONV_ROWS, CONV_PAIR), lambda j, i: (i, j)),
            pl.BlockSpec((HALO, CONV_PAIR), lambda j, i: (halo_idx(i), j)),
            pl.BlockSpec((3, CONV_PAIR), lambda j, i: (0, j)),
            pl.BlockSpec((1, CONV_PAIR), lambda j, i: (0, j))]


def _ffn_act_fwd(hh, conv_w, conv_b):
    s = hh.shape[0]

    def body(m_ref, h_ref, w_ref, b_ref, o_ref):
        first = pl.program_id(1) == 0
        main = m_ref[...].astype(F32)
        x1, x2 = _causal_taps(main, h_ref[...].astype(F32), first)
        conv = w_ref[0:1, :] * x2 + w_ref[1:2, :] * x1 + w_ref[2:3, :] * main + b_ref[...]
        o_ref[...] = (_gelu(conv[:, CONV_COLS:]) * conv[:, :CONV_COLS]).astype(o_ref.dtype)

    return pl.pallas_call(
        body, name="ffn_act_fwd", grid=(D_FF // CONV_COLS, s // CONV_ROWS), in_specs=_conv_in_specs(),
        out_specs=pl.BlockSpec((CONV_ROWS, CONV_COLS), lambda j, i: (i, j)),
        out_shape=jax.ShapeDtypeStruct((s, D_FF), MXU_DTYPE),
        compiler_params=_params(("parallel", "parallel")),
    )(hh, hh, conv_w, conv_b.reshape(1, -1))


def _ffn_act_bwd(hh, conv_w, conv_b, da):
    s = hh.shape[0]
    nrow = s // CONV_ROWS
    ext_rows = CONV_ROWS + HALO

    def body(m_ref, h_ref, w_ref, b_ref, nx_ref, da_ref, dan_ref, o_ref, dw_ref, db_ref):
        first = pl.program_id(1) == 0
        last = pl.program_id(1) == nrow - 1

        @pl.when(first)
        def _():
            dw_ref[...] = jnp.zeros_like(dw_ref)
            db_ref[...] = jnp.zeros_like(db_ref)

        ext = jnp.concatenate([m_ref[...].astype(F32), nx_ref[...].astype(F32)], axis=0)
        x1, x2 = _causal_taps(ext, h_ref[...].astype(F32), first)
        conv = w_ref[0:1, :] * x2 + w_ref[1:2, :] * x1 + w_ref[2:3, :] * ext + b_ref[...]
        da = jnp.concatenate([da_ref[...].astype(F32), jnp.where(last, 0.0, dan_ref[...].astype(F32))], axis=0)
        act, dact = _gelu_pair(conv[:, CONV_COLS:])
        dconv = jnp.concatenate([da * act, da * conv[:, :CONV_COLS] * dact], axis=1)
        dmain = dconv[:CONV_ROWS]
        ahead1 = pltpu.roll(dconv, ext_rows - 1, 0)[:CONV_ROWS]
        ahead2 = pltpu.roll(dconv, ext_rows - 2, 0)[:CONV_ROWS]
        o_ref[...] = (w_ref[2:3, :] * dmain + w_ref[1:2, :] * ahead1 + w_ref[0:1, :] * ahead2).astype(o_ref.dtype)
        for t, tap in enumerate((x2, x1, ext)):
            dw_ref[t:t + 1, :] += jnp.sum(dmain * tap[:CONV_ROWS], axis=0, keepdims=True)
        db_ref[...] += jnp.sum(dmain, axis=0, keepdims=True)

    nxt = lambda i: jnp.minimum((i + 1) * (CONV_ROWS // HALO), s // HALO - 1)
    return pl.pallas_call(
        body, name="ffn_act_bwd", grid=(D_FF // CONV_COLS, nrow),
        in_specs=_conv_in_specs() + [pl.BlockSpec((HALO, CONV_PAIR), lambda j, i: (nxt(i), j)),
                                     pl.BlockSpec((CONV_ROWS, CONV_COLS), lambda j, i: (i, j)),
                                     pl.BlockSpec((HALO, CONV_COLS), lambda j, i: (nxt(i), j))],
        out_specs=[pl.BlockSpec((CONV_ROWS, CONV_PAIR), lambda j, i: (i, j)),
                   pl.BlockSpec((3, CONV_PAIR), lambda j, i: (0, j)), pl.BlockSpec((1, CONV_PAIR), lambda j, i: (0, j))],
        out_shape=[jax.ShapeDtypeStruct((s, 2 * D_FF), MXU_DTYPE), jax.ShapeDtypeStruct((3, 2 * D_FF), F32),
                   jax.ShapeDtypeStruct((1, 2 * D_FF), F32)],
        compiler_params=_params(("parallel", "arbitrary")),
    )(hh, hh, conv_w, conv_b.reshape(1, -1), hh, da, da)


def _shift_matrix(rows, back):
    r = lax.broadcasted_iota(jnp.int32, (2 * rows, rows), 0)
    c = lax.broadcasted_iota(jnp.int32, (2 * rows, rows), 1)
    step = jnp.where(r < rows, 1, 2)
    t = jnp.where(r < rows, r, r - rows)
    src = t - step if back else t + step
    return jnp.where(c == src, 1.0, 0.0).astype(MXU_DTYPE)


def _patch_rows(x, at_end, rows):
    tile = 8
    n = x.shape[0]
    idx = lax.broadcasted_iota(jnp.int32, (tile, x.shape[1]), 0)
    piece = x[n - tile:] if at_end else x[:tile]
    for k, row in enumerate(rows):
        where_row = (tile - len(rows) + k) if at_end else k
        piece = jnp.where(idx == where_row, row, piece)
    return jnp.concatenate([x[:n - tile], piece], axis=0) if at_end else jnp.concatenate([piece, x[tile:]], axis=0)


def _mxu_taps(main_m, halo, first):
    shifted = _dot(_shift_matrix(main_m.shape[0], True), main_m, NN)
    h1 = jnp.where(first, 0.0, halo[HALO - 1:HALO, :])
    h2 = jnp.where(first, 0.0, halo[HALO - 2:HALO - 1, :])
    x1 = _patch_rows(shifted[:main_m.shape[0]], False, [h1])
    x2 = _patch_rows(shifted[main_m.shape[0]:], False, [h2, h1])
    return x1, x2


def _conv_gate(w_ref, b_ref, x2, x1, x0):
    return w_ref[0:1, :] * x2 + w_ref[1:2, :] * x1 + w_ref[2:3, :] * x0 + b_ref[...]


def _ffn_gate_fwd(hh, conv_w, conv_b):
    s = hh.shape[0]

    def body(m_ref, h_ref, w_ref, b_ref, o_ref):
        first = pl.program_id(1) == 0
        main_m = m_ref[...]
        x1, x2 = _mxu_taps(main_m, h_ref[...].astype(F32), first)
        conv = _conv_gate(w_ref, b_ref, x2, x1, main_m.astype(F32))
        o_ref[...] = (_gelu(conv[:, CONV_COLS:]) * conv[:, :CONV_COLS]).astype(o_ref.dtype)

    return pl.pallas_call(
        body, name="ffn_act_fwd", grid=(D_FF // CONV_COLS, s // CONV_ROWS), in_specs=_conv_in_specs(),
        out_specs=pl.BlockSpec((CONV_ROWS, CONV_COLS), lambda j, i: (i, j)),
        out_shape=jax.ShapeDtypeStruct((s, D_FF), MXU_DTYPE),
        compiler_params=_params(("parallel", "parallel")),
    )(hh, hh, conv_w, conv_b.reshape(1, -1))


def _ffn_gate_bwd(hh, conv_w, conv_b, da):
    s = hh.shape[0]
    nrow = s // CONV_ROWS

    def gate_grad(conv, da):
        act, dact = _gelu_pair(conv[:, CONV_COLS:])
        return jnp.concatenate([da * act, da * conv[:, :CONV_COLS] * dact], axis=1)

    def body(m_ref, h_ref, w_ref, b_ref, nx_ref, da_ref, dan_ref, o_ref, dw_ref, db_ref):
        first = pl.program_id(1) == 0
        last = pl.program_id(1) == nrow - 1

        @pl.when(first)
        def _():
            dw_ref[...] = jnp.zeros_like(dw_ref)
            db_ref[...] = jnp.zeros_like(db_ref)

        main_m = m_ref[...]
        main = main_m.astype(F32)
        x1, x2 = _mxu_taps(main_m, h_ref[...].astype(F32), first)
        dconv = gate_grad(_conv_gate(w_ref, b_ref, x2, x1, main), da_ref[...].astype(F32))
        nx = nx_ref[...].astype(F32)
        nx1 = _patch_rows(pltpu.roll(nx, 1, 0), False, [main[CONV_ROWS - 1:]])
        nx2 = _patch_rows(pltpu.roll(nx, 2, 0), False, [main[CONV_ROWS - 2:CONV_ROWS - 1], main[CONV_ROWS - 1:]])
        dnext = gate_grad(_conv_gate(w_ref, b_ref, nx2, nx1, nx), jnp.where(last, 0.0, dan_ref[...].astype(F32)))
        dnext = dnext.astype(MXU_DTYPE).astype(F32)
        ahead = _dot(_shift_matrix(CONV_ROWS, False), dconv.astype(MXU_DTYPE), NN)
        ahead1 = _patch_rows(ahead[:CONV_ROWS], True, [dnext[0:1]])
        ahead2 = _patch_rows(ahead[CONV_ROWS:], True, [dnext[0:1], dnext[1:2]])
        o_ref[...] = (w_ref[2:3, :] * dconv + w_ref[1:2, :] * ahead1 + w_ref[0:1, :] * ahead2).astype(o_ref.dtype)
        for t, tap in enumerate((x2, x1, main)):
            dw_ref[t:t + 1, :] += jnp.sum(dconv * tap, axis=0, keepdims=True)
        db_ref[...] += jnp.sum(dconv, axis=0, keepdims=True)

    nxt = lambda i: jnp.minimum((i + 1) * (CONV_ROWS // HALO), s // HALO - 1)
    return pl.pallas_call(
        body, name="ffn_act_bwd", grid=(D_FF // CONV_COLS, nrow),
        in_specs=_conv_in_specs() + [pl.BlockSpec((HALO, CONV_PAIR), lambda j, i: (nxt(i), j)),
                                     pl.BlockSpec((CONV_ROWS, CONV_COLS), lambda j, i: (i, j)),
                                     pl.BlockSpec((HALO, CONV_COLS), lambda j, i: (nxt(i), j))],
        out_specs=[pl.BlockSpec((CONV_ROWS, CONV_PAIR), lambda j, i: (i, j)),
                   pl.BlockSpec((3, CONV_PAIR), lambda j, i: (0, j)), pl.BlockSpec((1, CONV_PAIR), lambda j, i: (0, j))],
        out_shape=[jax.ShapeDtypeStruct((s, 2 * D_FF), MXU_DTYPE), jax.ShapeDtypeStruct((3, 2 * D_FF), F32),
                   jax.ShapeDtypeStruct((1, 2 * D_FF), F32)],
        compiler_params=_params(("parallel", "arbitrary")),
    )(hh, hh, conv_w, conv_b.reshape(1, -1), hh, da, da)


def _ple_weight_specs(layer):
    return [pl.BlockSpec((None, D_MODEL, D_MODEL), lambda i: (layer, 0, 0)),
            pl.BlockSpec((None, PLE_DIM, D_MODEL), lambda i: (layer, 0, 0))]


def _ple_fwd(h, gain, p, w_gate, w_proj, layer):
    s = h.shape[0]
    tm = 512

    def body(h_ref, g_ref, p_ref, wg_ref, wp_ref, o_ref, xn_ref):
        x = h_ref[...]
        xn = _rms_rows(x, g_ref[...])
        xn_ref[...] = xn
        gate = jax.nn.sigmoid(_dot(xn, wg_ref[...].astype(MXU_DTYPE), NN))
        proj = _dot(p_ref[...].astype(MXU_DTYPE), wp_ref[...].astype(MXU_DTYPE), NN)
        o_ref[...] = x + gate * proj

    row = pl.BlockSpec((tm, D_MODEL), lambda i: (i, 0))
    return pl.pallas_call(
        body, name="ple_fwd", grid=(s // tm,),
        in_specs=[row, pl.BlockSpec((1, D_MODEL), lambda i: (0, 0)), pl.BlockSpec((tm, PLE_DIM), lambda i: (i, 0))]
        + _ple_weight_specs(layer),
        out_specs=[row, row],
        out_shape=[jax.ShapeDtypeStruct((s, D_MODEL), F32), jax.ShapeDtypeStruct((s, D_MODEL), MXU_DTYPE)],
        compiler_params=_params(("parallel",)),
    )(h, gain.reshape(1, D_MODEL), p, w_gate, w_proj)


def _ple_bwd(xn, p, w_gate, w_proj, dh, layer):
    s = xn.shape[0]
    tm = 512

    def body(x_ref, p_ref, wg_ref, wp_ref, dh_ref, dpre_ref, dproj_ref):
        gate = jax.nn.sigmoid(_dot(x_ref[...].astype(MXU_DTYPE), wg_ref[...].astype(MXU_DTYPE), NN))
        proj = _dot(p_ref[...].astype(MXU_DTYPE), wp_ref[...].astype(MXU_DTYPE), NN)
        dh = dh_ref[...]
        dpre_ref[...] = (dh * proj * gate * (1.0 - gate)).astype(dpre_ref.dtype)
        dproj_ref[...] = (dh * gate).astype(dproj_ref.dtype)

    row = pl.BlockSpec((tm, D_MODEL), lambda i: (i, 0))
    return pl.pallas_call(
        body, name="ple_bwd", grid=(s // tm,),
        in_specs=[row, pl.BlockSpec((tm, PLE_DIM), lambda i: (i, 0))] + _ple_weight_specs(layer) + [row],
        out_specs=[row, row],
        out_shape=[jax.ShapeDtypeStruct((s, D_MODEL), MXU_DTYPE)] * 2,
        compiler_params=_params(("parallel",)),
    )(xn, p, w_gate, w_proj, dh)


O_SGU = 3 * ATTN_W
O_SSM = O_SGU + 2 * SGU_W


def _layer_consts(w, i):
    causal = jnp.asarray(np.tril(np.ones((SGU_CHUNK, SGU_CHUNK), np.float32)))
    return {
        "sgu_w_mask": w["sgu_w"][i] * causal,
        "sgu_b_t": w["sgu_b"][i].T,
        "ssm_ops": _ssm_operands(w["ssm_a_re"][i], w["ssm_a_im"][i], w["ssm_log_dt"][i], w["ssm_b_re"][i],
                                 w["ssm_b_im"][i], w["ssm_c_re"][i], w["ssm_c_im"][i]),
    }


def _layer_fwd(h0, p_i, w, i, bias):
    c = _layer_consts(w, i)
    xn1, qkv, zs, us = _in_proj(h0, w["norm_attn_g"][i], w["w_in"], i)
    y_attn, lse = _attn2_fwd(qkv, bias)
    y_sgu = _sgu_fwd(zs, w["sgu_ln_g"][i], w["sgu_ln_b"][i], c["sgu_w_mask"], c["sgu_b_t"])
    y_ssm, entry = _ssm_fwd(us, c["ssm_ops"], w["ssm_d"][i], w["ssm_glu_w"][i], w["ssm_glu_b"][i])
    mix = _mix_fwd(y_attn, y_sgu, y_ssm, w["branch_norm_g"][i])
    h1 = _matmul(mix, w["w_out"], name="out_proj", out_dtype=F32, tm=512, tn=1024, residual=h0, layer=i)
    xn2, hh = _ffn_up(h1, w["norm_ffn_g"][i], w["ffn_w_up"], i)
    act = _ffn_gate_fwd(hh, w["ffn_conv_w"][i], w["ffn_conv_b"][i])
    h2 = _matmul(act, w["ffn_w_down"], name="ffn_down", out_dtype=F32, tm=512, tn=1024, residual=h1, layer=i)
    h3, xn3 = _ple_fwd(h2, w["norm_ple_g"][i], p_i, w["ple_w_gate"], w["ple_w_proj"], i)
    saved = dict(h0=h0, xn1=xn1, qkv=qkv, zs=zs, us=us, y_attn=y_attn, lse=lse, y_sgu=y_sgu, y_ssm=y_ssm,
                 entry=entry, mix=mix, h1=h1, xn2=xn2, hh=hh, act=act, h2=h2, xn3=xn3, consts=c)
    return h3, saved


def _layer_bwd(dh3, sv, p_i, w, i, bias):
    c = sv["consts"]
    g = {}
    dpre, dproj = _ple_bwd(sv["xn3"], p_i, w["ple_w_gate"], w["ple_w_proj"], dh3, i)
    g["ple_w_gate"] = _matmul_tn(sv["xn3"], dpre, name="d_ple_w_gate", tk=1024, tn=1024)
    g["ple_w_proj"] = _matmul_tn(p_i, dproj, name="d_ple_w_proj", tk=256, tn=1024)
    dh2, g["norm_ple_g"] = _matmul_rms_bwd(dpre, w["ple_w_gate"], sv["h2"], w["norm_ple_g"][i], dh3,
                                           name="d_xn_ple", layer=i, tm=512)
    g["ffn_w_down"] = _matmul_tn(sv["act"], dh2, name="d_ffn_w_down", tk=1408, tn=1024)
    dact = _matmul(dh2, w["ffn_w_down"], name="d_ffn_act", out_dtype=MXU_DTYPE, tm=512, tn=1408, trans_b=True, layer=i)
    dhh, g["ffn_conv_w"], g["ffn_conv_b"] = _ffn_gate_bwd(sv["hh"], w["ffn_conv_w"][i], w["ffn_conv_b"][i], dact)
    g["ffn_w_up"] = _matmul_tn(sv["xn2"], dhh, name="d_ffn_w_up", tk=1024, tn=1408)
    dh1, g["norm_ffn_g"] = _matmul_rms_bwd(dhh, w["ffn_w_up"], sv["h1"], w["norm_ffn_g"][i], dh2,
                                           name="d_xn_ffn", layer=i, tm=256)
    g["w_out"] = _matmul_tn(sv["mix"], dh1, name="d_w_out", tk=1024, tn=1024)
    dmix = _matmul(dh1, w["w_out"], name="d_mix", out_dtype=F32, tm=512, tn=1024, trans_b=True, layer=i)
    dy_attn, delta, dy_sgu, dy_ssm, g["branch_norm_g"] = _mix_bwd(sv["y_attn"], sv["y_sgu"], sv["y_ssm"],
                                                                  w["branch_norm_g"][i], dmix)
    dq, dk, dv, ek, ev, dbias = _attn2_bwd(sv["qkv"], bias, sv["lse"], delta, dy_attn)
    dzs, g["sgu_ln_g"], g["sgu_ln_b"], dsw, dsb = _sgu_bwd(sv["zs"], w["sgu_ln_g"][i], w["sgu_ln_b"][i],
                                                          c["sgu_w_mask"], c["sgu_b_t"], dy_sgu)
    causal = jnp.asarray(np.tril(np.ones((SGU_CHUNK, SGU_CHUNK), np.float32)))
    g["sgu_w"] = dsw * causal
    g["sgu_b"] = dsb.T
    dus, dbb, dcm, da, g["ssm_d"], g["ssm_glu_w"], g["ssm_glu_b"] = _ssm_bwd(
        sv["us"], sv["entry"], c["ssm_ops"], w["ssm_d"][i], w["ssm_glu_w"][i], w["ssm_glu_b"][i], dy_ssm)
    dbb5 = dbb.reshape(SSM_G, SSM_C, 2, SSM_G, SSM_N)
    dbbar = jnp.einsum("gcpgn->pgnc", dbb5)
    dcm5 = dcm.reshape(2, SSM_G, SSM_N, SSM_G, SSM_C)
    dcc = jnp.einsum("pgngc->pgcn", dcm5)
    g["ssm_c_re"] = dcc[0]
    g["ssm_c_im"] = -dcc[1]
    da2 = da.reshape(2, SSM_G, SSM_N)
    _, vjp = jax.vjp(_ssm_discretize, w["ssm_a_re"][i], w["ssm_a_im"][i], w["ssm_log_dt"][i],
                     w["ssm_b_re"][i], w["ssm_b_im"][i])
    (g["ssm_a_re"], g["ssm_a_im"], g["ssm_log_dt"], g["ssm_b_re"], g["ssm_b_im"]) = vjp(
        (da2[0], da2[1], dbbar[0], dbbar[1]))
    dz = _attn2_bwd_sum(dq, dk, dv, ek, ev, dzs, dus)
    g["w_in"] = _matmul_tn(sv["xn1"], dz, name="d_w_in", tk=1024, tn=1152)
    dh0, g["norm_attn_g"] = _matmul_rms_bwd(dz, w["w_in"], sv["h0"], w["norm_attn_g"][i], dh1,
                                            name="d_xn_attn", layer=i, tm=512)
    for k in ("norm_ple_g", "norm_ffn_g", "branch_norm_g", "norm_attn_g", "sgu_ln_g", "sgu_ln_b", "ssm_d",
              "ssm_glu_b", "ffn_conv_b"):
        g[k] = g[k].reshape(-1)
    return dh0, g, dbias


def _local_step(x, p, target, w, ff_interleaved=False):
    ff_names = ("ffn_conv_b",) if ff_interleaved else FF_SHARDED + ("ffn_conv_b",)
    w = dict(w)
    for k in ff_names:
        w[k] = _interleave_ff(w[k])
    bias = _bias_build(w["rel_bias"])
    h = x
    saved = []
    for i in range(DEPTH):
        h, sv = _layer_fwd(h, p[i], w, i, bias)
        saved.append(sv)
    loss, dh, dgf = _loss_head(h, w["final_norm_g"], target)
    layer_grads = [None] * DEPTH
    dbias = None
    for i in reversed(range(DEPTH)):
        dh, layer_grads[i], db = _layer_bwd(dh, saved[i], p[i], w, i, bias)
        dbias = db if dbias is None else dbias + db
    grads = {k: jnp.stack([layer_grads[i][k] for i in range(DEPTH)]) for k in layer_grads[0]}
    for k in ff_names:
        grads[k] = _deinterleave_ff(grads[k])
    grads["rel_bias"] = _bias_reduce(dbias)
    grads["final_norm_g"] = dgf.reshape(-1)
    return loss, dh, grads


def _pad_rows(a2, mult=16):
    r = (-a2.shape[0]) % mult
    return a2 if r == 0 else jnp.concatenate([a2, jnp.zeros((r, a2.shape[1]), a2.dtype)], axis=0)


def _as_rows(a, rows=None):
    size = int(np.prod(a.shape))
    if rows is None:
        rows = -(-size // (16 * PACK_COLS)) * 16
    if size % PACK_COLS:
        a = jnp.pad(a.reshape(-1), (0, (-size) % PACK_COLS))
    a2 = a.reshape(-1, PACK_COLS)
    return jnp.pad(a2, ((0, rows - a2.shape[0]), (0, 0)))


def _shard_shape(name):
    full, ax = BIG_FULL[name]
    shp = [DEPTH] + list(full)
    shp[ax] //= N_CHIPS
    return tuple(shp)


EXACT_NAMES = ("ffn_conv_w",)


def _pack_rows_of(name):
    n = int(np.prod(_shard_shape(name))) * (2 if name in EXACT_NAMES else 1)
    rows = -(-n // PACK_COLS)
    return -(-rows // 16) * 16


def _pack_shards(shards, dtype, exact=False):
    split_words = exact and jnp.dtype(dtype).itemsize == 2
    parts = []
    for n in BIG_NAMES:
        a = shards[n]
        if split_words and n in EXACT_NAMES:
            a = lax.bitcast_convert_type(a.astype(F32), dtype)
        parts.append(_as_rows(a.astype(dtype), _pack_rows_of(n)))
    used = sum(pt.shape[0] for pt in parts)
    parts.append(jnp.zeros((PACK_ROWS - used, PACK_COLS), dtype))
    return jnp.concatenate(parts, axis=0)


def _unpack_shard(flat, name, exact=False):
    off = 0
    for n in BIG_NAMES:
        if n == name:
            break
        off += _pack_rows_of(n)
    shp = _shard_shape(name)
    cnt = int(np.prod(shp))
    if exact and name in EXACT_NAMES and jnp.dtype(flat.dtype).itemsize == 2:
        vec = flat[off:off + _pack_rows_of(name)].reshape(-1)
        return lax.bitcast_convert_type(vec[:2 * cnt].reshape(shp + (2,)), F32)
    if cnt % PACK_COLS == 0:
        return flat[off:off + cnt // PACK_COLS].reshape(shp)
    return flat[off:off + _pack_rows_of(name)].reshape(-1)[:cnt].reshape(shp)


FF_SHARDED = ("ffn_w_up", "ffn_conv_w")
FF_CHIP_ORDER = (0, 2, 1, 3)


def _chip_order(name):
    return FF_CHIP_ORDER if name in FF_SHARDED else tuple(range(N_CHIPS))


def _split_full(full, name):
    _, ax = BIG_FULL[name]
    parts = jnp.split(full, N_CHIPS, axis=ax)
    out = [None] * N_CHIPS
    for j, k in enumerate(_chip_order(name)):
        out[k] = parts[j]
    return out


def _join_shards(shards, name):
    _, ax = BIG_FULL[name]
    return jnp.concatenate([shards[k] for k in _chip_order(name)], axis=ax)


def _small_shapes(w):
    return [(n, w[n].shape) for n in SMALL_NAMES]


def _small_rows(shp):
    return -(-int(np.prod(shp)) // (8 * PACK_COLS)) * 8


def _pack_small(d):
    parts = [_as_rows(d[n].astype(F32), _small_rows(d[n].shape)) for n in SMALL_NAMES]
    used = sum(pt.shape[0] for pt in parts)
    parts.append(jnp.zeros((SMALL_ROWS - used, PACK_COLS), F32))
    return jnp.concatenate(parts, axis=0)


def _unpack_small(flat, shapes):
    out, off = {}, 0
    for n, shp in shapes:
        cnt = int(np.prod(shp))
        rows = _small_rows(shp)
        if cnt % PACK_COLS == 0:
            out[n] = flat[off:off + cnt // PACK_COLS].reshape(shp)
        else:
            out[n] = flat[off:off + rows].reshape(-1)[:cnt].reshape(shp)
        off += rows
    return out


MESH = pl.DeviceIdType.MESH
ANY = pl.BlockSpec(memory_space=pl.ANY)


def _me():
    return lax.axis_index("x"), lax.axis_index("y"), lax.axis_index("c")


def _other_chips(x, y):
    return [(1 - x, y), (x, 1 - y), (1 - x, 1 - y)]


def _gather_weights(wflat):
    def body(w_ref, out_ref, send_sems, recv_sems):
        x, y, c = _me()
        sibling = (x, y, 1 - c)
        chips = _other_chips(x, y)

        def rows(chip, half):
            return out_ref.at[2 * chip[0] + chip[1], pl.ds(half * PACK_HALF, PACK_HALF), :]

        def copy(k, chip, half, to, src=None):
            return pltpu.make_async_remote_copy(
                src_ref=rows(chip, half) if src is None else src, dst_ref=rows(chip, half),
                send_sem=send_sems.at[k], recv_sem=recv_sems.at[k], device_id=to, device_id_type=MESH)

        my_half = w_ref.at[pl.ds(c * PACK_HALF, PACK_HALF), :]
        first = [copy(j, (x, y), c, (*chip, c), src=my_half) for j, chip in enumerate(chips)]
        for cp in first:
            cp.start()
        passed = [copy(3 + j, chip, c, sibling) for j, chip in enumerate(chips)]
        for j, chip in enumerate(chips):
            copy(j, chip, c, (x, y, c)).wait_recv()
            passed[j].start()
        for j, chip in enumerate(chips):
            copy(3 + j, chip, 1 - c, (x, y, c)).wait_recv()
        for cp in first + passed:
            cp.wait_send()

    return pl.pallas_call(
        body, name="gather_weights", in_specs=[ANY], out_specs=ANY,
        out_shape=jax.ShapeDtypeStruct((N_CHIPS, PACK_ROWS, PACK_COLS), wflat.dtype),
        scratch_shapes=[pltpu.SemaphoreType.DMA((6,)), pltpu.SemaphoreType.DMA((6,))],
    )(wflat)


def _fill_own_shard(wall, wflat, chip_idx):
    rows = PACK_ROWS // 8

    def body(idx_ref, w_ref, wall_ref, o_ref):
        del idx_ref, wall_ref
        o_ref[...] = w_ref[...]

    return pl.pallas_call(
        body, name="fill_own_shard",
        grid_spec=pltpu.PrefetchScalarGridSpec(
            num_scalar_prefetch=1, grid=(PACK_ROWS // rows,),
            in_specs=[pl.BlockSpec((rows, PACK_COLS), lambda i, idx: (i, 0)), ANY],
            out_specs=pl.BlockSpec((None, rows, PACK_COLS), lambda i, idx: (idx[0], i, 0))),
        out_shape=jax.ShapeDtypeStruct(wall.shape, wall.dtype),
        input_output_aliases={2: 0},
        compiler_params=_params(("parallel",)),
    )(chip_idx, wflat, wall)


def _exchange_partials(gb, gs):
    def body(gb_ref, gs_ref, half_ref, small_ref, send_sems, recv_sems, local_sem):
        x, y, c = _me()
        me_idx = 4 * x + 2 * y + c
        mine = pltpu.make_async_copy(gs_ref, small_ref.at[me_idx], local_sem)
        mine.start()
        d2d = pltpu.make_async_remote_copy(
            src_ref=gb_ref.at[:, pl.ds((1 - c) * PACK_HALF, PACK_HALF), :], dst_ref=half_ref,
            send_sem=send_sems.at[0], recv_sem=recv_sems.at[0], device_id=(x, y, 1 - c), device_id_type=MESH)
        d2d.start()
        copies = []
        for k in range(1, N_DEV):
            fx, fy, fc = (k >> 2) & 1, (k >> 1) & 1, k & 1
            peer = (x ^ fx, y ^ fy, c ^ fc)
            copies.append(pltpu.make_async_remote_copy(
                src_ref=gs_ref, dst_ref=small_ref.at[me_idx], send_sem=send_sems.at[k], recv_sem=recv_sems.at[k],
                device_id=peer, device_id_type=MESH))
        for cp in copies:
            cp.start()
        for k in range(1, N_DEV):
            fx, fy, fc = (k >> 2) & 1, (k >> 1) & 1, k & 1
            peer_idx = 4 * (x ^ fx) + 2 * (y ^ fy) + (c ^ fc)
            pltpu.make_async_remote_copy(
                src_ref=gs_ref, dst_ref=small_ref.at[peer_idx], send_sem=send_sems.at[k], recv_sem=recv_sems.at[k],
                device_id=(x, y, c), device_id_type=MESH).wait_recv()
        d2d.wait_recv()
        d2d.wait_send()
        for cp in copies:
            cp.wait_send()
        mine.wait()

    return pl.pallas_call(
        body, name="exchange_partials", in_specs=[ANY, pl.BlockSpec(memory_space=pltpu.VMEM)], out_specs=[ANY, ANY],
        out_shape=[jax.ShapeDtypeStruct((N_CHIPS, PACK_HALF, PACK_COLS), gb.dtype),
                   jax.ShapeDtypeStruct((N_DEV, SMALL_ROWS, PACK_COLS), F32)],
        scratch_shapes=[pltpu.SemaphoreType.DMA((N_DEV,)), pltpu.SemaphoreType.DMA((N_DEV,)), pltpu.SemaphoreType.DMA],
    )(gb, gs)


RED_ROWS = 256


def _chip_partials(gb, sib, c_idx):
    nrow = PACK_HALF // RED_ROWS

    def body(c_ref, a_ref, b_ref, o_ref):
        del c_ref
        o_ref[...] = (a_ref[...].astype(F32) + b_ref[...].astype(F32)).astype(o_ref.dtype)

    blk = (1, RED_ROWS, PACK_COLS)
    return pl.pallas_call(
        body, name="chip_partials",
        grid_spec=pltpu.PrefetchScalarGridSpec(
            num_scalar_prefetch=1, grid=(N_CHIPS, nrow),
            in_specs=[pl.BlockSpec(blk, lambda k, i, c: (k, c[0] * nrow + i, 0)),
                      pl.BlockSpec(blk, lambda k, i, c: (k, i, 0))],
            out_specs=pl.BlockSpec(blk, lambda k, i, c: (k, i, 0))),
        out_shape=jax.ShapeDtypeStruct((N_CHIPS, PACK_HALF, PACK_COLS), gb.dtype),
        compiler_params=_params(("parallel", "parallel")),
    )(c_idx, gb, sib)


def _scatter_partials(pc):
    def body(pc_ref, out_ref, send_sems, recv_sems):
        x, y, c = _me()
        chips = _other_chips(x, y)
        copies = [pltpu.make_async_remote_copy(
            src_ref=pc_ref.at[2 * chip[0] + chip[1]], dst_ref=out_ref.at[k],
            send_sem=send_sems.at[k], recv_sem=recv_sems.at[k], device_id=(*chip, c), device_id_type=MESH)
            for k, chip in enumerate(chips)]
        for cp in copies:
            cp.start()
        for cp in copies:
            cp.wait_recv()
        for cp in copies:
            cp.wait_send()

    return pl.pallas_call(
        body, name="scatter_partials", in_specs=[ANY], out_specs=ANY,
        out_shape=jax.ShapeDtypeStruct((3, PACK_HALF, PACK_COLS), pc.dtype),
        scratch_shapes=[pltpu.SemaphoreType.DMA((3,)), pltpu.SemaphoreType.DMA((3,))],
    )(pc)


def _final_half(gb, sib, recv, idx):
    nrow = PACK_HALF // RED_ROWS

    def body(idx_ref, a_ref, b_ref, r_ref, o_ref):
        del idx_ref
        acc = a_ref[0].astype(F32) + b_ref[0].astype(F32)
        for k in range(3):
            acc = acc + r_ref[k].astype(F32)
        o_ref[...] = acc

    return pl.pallas_call(
        body, name="final_half",
        grid_spec=pltpu.PrefetchScalarGridSpec(
            num_scalar_prefetch=1, grid=(nrow,),
            in_specs=[pl.BlockSpec((1, RED_ROWS, PACK_COLS), lambda i, idx: (idx[0], idx[1] * nrow + i, 0)),
                      pl.BlockSpec((1, RED_ROWS, PACK_COLS), lambda i, idx: (idx[0], i, 0)),
                      pl.BlockSpec((3, RED_ROWS, PACK_COLS), lambda i, idx: (0, i, 0))],
            out_specs=pl.BlockSpec((RED_ROWS, PACK_COLS), lambda i, idx: (i, 0))),
        out_shape=jax.ShapeDtypeStruct((PACK_HALF, PACK_COLS), F32),
        compiler_params=_params(("parallel",)),
    )(idx, gb, sib, recv)


def _share_halves(half):
    def body(h_ref, out_ref, send_sem, recv_sem):
        x, y, c = _me()
        cp = pltpu.make_async_remote_copy(src_ref=h_ref, dst_ref=out_ref, send_sem=send_sem, recv_sem=recv_sem,
                                          device_id=(x, y, 1 - c), device_id_type=MESH)
        cp.start()
        cp.wait_recv()
        cp.wait_send()

    return pl.pallas_call(
        body, name="share_halves", in_specs=[ANY], out_specs=ANY,
        out_shape=jax.ShapeDtypeStruct((PACK_HALF, PACK_COLS), F32),
        scratch_shapes=[pltpu.SemaphoreType.DMA, pltpu.SemaphoreType.DMA],
    )(half)


def _sum_small(allsmall):
    def body(a_ref, o_ref):
        acc = a_ref[0]
        for k in range(1, N_DEV):
            acc = acc + a_ref[k]
        o_ref[...] = acc

    tr = 96
    return pl.pallas_call(
        body, name="sum_small", grid=(SMALL_ROWS // tr,),
        in_specs=[pl.BlockSpec((N_DEV, tr, PACK_COLS), lambda i: (0, i, 0))],
        out_specs=pl.BlockSpec((tr, PACK_COLS), lambda i: (i, 0)),
        out_shape=jax.ShapeDtypeStruct((SMALL_ROWS, PACK_COLS), F32),
        compiler_params=_params(("parallel",)),
    )(allsmall)


def _adamw(w, g, m, v, *, name):
    shape = w.shape
    cols = shape[-1]
    as2 = lambda t: t.reshape(-1, cols)
    w2, g2, m2, v2 = as2(w), as2(g), as2(m), as2(v)
    rows = w2.shape[0]
    tr = rows
    if rows * cols * 4 > (1 << 20):
        tr = _tile(rows, max(8, (1 << 20) // (cols * 4) // 8 * 8), 8)

    def body(w_ref, g_ref, m_ref, v_ref, d_ref, mo_ref, vo_ref):
        gg = g_ref[...]
        mn = ADAM_B1 * m_ref[...] + (1.0 - ADAM_B1) * gg
        vn = ADAM_B2 * v_ref[...] + (1.0 - ADAM_B2) * (gg * gg)
        m_hat = mn / (1.0 - ADAM_B1 ** ADAM_STEP)
        v_hat = vn / (1.0 - ADAM_B2 ** ADAM_STEP)
        d_ref[...] = -ADAM_LR * (m_hat / (jnp.sqrt(v_hat) + ADAM_EPS) + ADAM_WD * w_ref[...])
        mo_ref[...] = mn
        vo_ref[...] = vn

    blk = pl.BlockSpec((tr, cols), lambda i: (i, 0))
    outs = pl.pallas_call(
        body, name=name, grid=(rows // tr,), in_specs=[blk] * 4, out_specs=[blk] * 3,
        out_shape=[jax.ShapeDtypeStruct((rows, cols), F32)] * 3,
        compiler_params=_params(("parallel",)),
    )(w2, g2, m2, v2)
    return tuple(t.reshape(shape) for t in outs)


def _adamw_many(ws, gs, ms, vs):
    n = len(ws)

    def body(*refs):
        for t in range(n):
            w_ref, g_ref, m_ref, v_ref = refs[t], refs[n + t], refs[2 * n + t], refs[3 * n + t]
            d_ref, mo_ref, vo_ref = refs[4 * n + t], refs[5 * n + t], refs[6 * n + t]
            gg = g_ref[...]
            mn = ADAM_B1 * m_ref[...] + (1.0 - ADAM_B1) * gg
            vn = ADAM_B2 * v_ref[...] + (1.0 - ADAM_B2) * (gg * gg)
            m_hat = mn / (1.0 - ADAM_B1 ** ADAM_STEP)
            v_hat = vn / (1.0 - ADAM_B2 ** ADAM_STEP)
            d_ref[...] = -ADAM_LR * (m_hat / (jnp.sqrt(v_hat) + ADAM_EPS) + ADAM_WD * w_ref[...])
            mo_ref[...] = mn
            vo_ref[...] = vn

    vmem = pl.BlockSpec(memory_space=pltpu.VMEM)
    outs = pl.pallas_call(
        body, name="adamw_small", in_specs=[vmem] * (4 * n), out_specs=[vmem] * (3 * n),
        out_shape=[jax.ShapeDtypeStruct(w.shape, F32) for w in ws] * 3,
        compiler_params=pltpu.CompilerParams(vmem_limit_bytes=VMEM_LIMIT_BYTES),
    )(*ws, *gs, *ms, *vs)
    return outs[:n], outs[n:2 * n], outs[2 * n:]


def kernel(x, p, rel_bias, norm_attn_g, w_in, sgu_ln_g, sgu_ln_b, sgu_w, sgu_b, ssm_a_re, ssm_a_im, ssm_log_dt, ssm_b_re, ssm_b_im, ssm_c_re, ssm_c_im, ssm_d, ssm_glu_w, ssm_glu_b, branch_norm_g, w_out, norm_ffn_g, ffn_w_up, ffn_conv_w, ffn_conv_b, ffn_w_down, norm_ple_g, ple_w_gate, ple_w_proj, final_norm_g, loss_target, m_rel_bias, m_norm_attn_g, m_w_in, m_sgu_ln_g, m_sgu_ln_b, m_sgu_w, m_sgu_b, m_ssm_a_re, m_ssm_a_im, m_ssm_log_dt, m_ssm_b_re, m_ssm_b_im, m_ssm_c_re, m_ssm_c_im, m_ssm_d, m_ssm_glu_w, m_ssm_glu_b, m_branch_norm_g, m_w_out, m_norm_ffn_g, m_ffn_w_up, m_ffn_conv_w, m_ffn_conv_b, m_ffn_w_down, m_norm_ple_g, m_ple_w_gate, m_ple_w_proj, m_final_norm_g, v_rel_bias, v_norm_attn_g, v_w_in, v_sgu_ln_g, v_sgu_ln_b, v_sgu_w, v_sgu_b, v_ssm_a_re, v_ssm_a_im, v_ssm_log_dt, v_ssm_b_re, v_ssm_b_im, v_ssm_c_re, v_ssm_c_im, v_ssm_d, v_ssm_glu_w, v_ssm_glu_b, v_branch_norm_g, v_w_out, v_norm_ffn_g, v_ffn_w_up, v_ffn_conv_w, v_ffn_conv_b, v_ffn_w_down, v_norm_ple_g, v_ple_w_gate, v_ple_w_proj, v_final_norm_g):
    args = dict(locals())
    wts = {n: args[n] for n in WEIGHT_NAMES}
    mom_m = {n: args["m_" + n] for n in WEIGHT_NAMES}
    mom_v = {n: args["v_" + n] for n in WEIGHT_NAMES}

    xi, yi, ci = _me()
    wflat = _pack_shards({n: wts[n] for n in BIG_NAMES}, MXU_DTYPE, exact=True)
    wall = _fill_own_shard(_gather_weights(wflat), wflat, jnp.stack([2 * xi + yi]).astype(jnp.int32))
    full = dict(wts)
    for n in BIG_NAMES:
        full[n] = _join_shards([_unpack_shard(wall[k], n, exact=True) for k in range(N_CHIPS)], n)
    full["ffn_conv_w"] = full["ffn_conv_w"].astype(F32)

    loss, dx, grads = _local_step(x[0], p[:, 0], loss_target[0], full, ff_interleaved=True)

    xi, yi, ci = _me()
    stacked = {n: _split_full(grads[n], n) for n in BIG_NAMES}
    gb = jnp.stack([_pack_shards({n: stacked[n][k] for n in BIG_NAMES}, MXU_DTYPE) for k in range(N_CHIPS)])
    gs = _pack_small(grads).at[SMALL_ROWS - 1, 0].set(loss[0, 0])
    sib, allsmall = _exchange_partials(gb, gs)
    pc = _chip_partials(gb, sib, jnp.stack([ci]).astype(jnp.int32))
    recv = _scatter_partials(pc)
    half = _final_half(gb, sib, recv, jnp.stack([2 * xi + yi, ci]).astype(jnp.int32))
    other = _share_halves(half)
    gflat = jnp.concatenate([jnp.where(ci == 0, half, other), jnp.where(ci == 0, other, half)], axis=0)
    small_sum = _sum_small(allsmall)
    loss = small_sum[SMALL_ROWS - 1, 0]
    gsmall = _unpack_small(small_sum, _small_shapes(wts))

    g_out, d_out, m_out, v_out = {}, {}, {}, {}
    for n in BIG_NAMES:
        g_out[n] = _unpack_shard(gflat, n)
        d_out[n], m_out[n], v_out[n] = _adamw(wts[n], g_out[n], mom_m[n], mom_v[n], name="adamw_" + n)
    d_sm, m_sm, v_sm = _adamw_many([wts[n] for n in SMALL_NAMES], [gsmall[n] for n in SMALL_NAMES],
                                   [mom_m[n] for n in SMALL_NAMES], [mom_v[n] for n in SMALL_NAMES])
    for t, n in enumerate(SMALL_NAMES):
        g_out[n], d_out[n], m_out[n], v_out[n] = gsmall[n], d_sm[t], m_sm[t], v_sm[t]

    return (loss, dx[None], *[g_out[n] for n in WEIGHT_NAMES], *[d_out[n] for n in WEIGHT_NAMES],
            *[m_out[n] for n in WEIGHT_NAMES], *[v_out[n] for n in WEIGHT_NAMES])
```

```python
import functools
import math

import numpy as np
import jax
import jax.numpy as jnp
from jax import lax
from jax.experimental import pallas as pl
from jax.experimental.pallas import tpu as pltpu

F32 = jnp.float32
MXU_DTYPE = jnp.bfloat16
VMEM_LIMIT_BYTES = 52 * 1024 * 1024

D_MODEL = 1024
DEPTH = 2
PLE_DIM = 256
HEAD_DIM = 64
N_HEADS = 8
ATTN_W = 512
QBLK = 128
BRANCH_DIL = (1, 4, 16)
N_BUCKETS = 32
REL_MAX_DIST = 2048
SGU_W = 256
SGU_G = 4
SGU_GW = 64
SGU_CHUNK = 128
SSM_W = 256
SSM_G = 16
SSM_C = 16
SSM_N = 64
NSTATE = SSM_G * SSM_N
D_FF = 2816
EPS = 1e-6
NEG_INF = -1e30
ATTN_SCALE = HEAD_DIM ** -0.5

ADAM_LR = 0.001
ADAM_B1 = 0.9
ADAM_B2 = 0.999
ADAM_EPS = 1e-08
ADAM_WD = 0.01
ADAM_STEP = 10

SSM_NSEG = 8
SSM_TSEG = 64
SSM_TB = SSM_NSEG * SSM_TSEG
SSM_LANE_CHUNK = 512

N_CHIPS = 4
N_DEV = 8

BIG_NAMES = ("w_in", "ssm_glu_w", "w_out", "ffn_w_up", "ffn_conv_w", "ffn_w_down", "ple_w_gate", "ple_w_proj")
BIG_FULL = {
    "w_in": ((D_MODEL, 2304), 2),
    "ssm_glu_w": ((SSM_W, SSM_W), 1),
    "w_out": ((D_MODEL, D_MODEL), 1),
    "ffn_w_up": ((D_MODEL, 2 * D_FF), 2),
    "ffn_conv_w": ((3, 2 * D_FF), 2),
    "ffn_w_down": ((D_FF, D_MODEL), 1),
    "ple_w_gate": ((D_MODEL, D_MODEL), 1),
    "ple_w_proj": ((PLE_DIM, D_MODEL), 2),
}
PACK_COLS = 1024
PACK_ROWS = 6656
PACK_HALF = PACK_ROWS // 2

SMALL_NAMES = ("rel_bias", "norm_attn_g", "sgu_ln_g", "sgu_ln_b", "sgu_w", "sgu_b", "ssm_a_re", "ssm_a_im",
               "ssm_log_dt", "ssm_b_re", "ssm_b_im", "ssm_c_re", "ssm_c_im", "ssm_d", "ssm_glu_b",
               "branch_norm_g", "norm_ffn_g", "ffn_conv_b", "norm_ple_g", "final_norm_g")
SMALL_ROWS = 384

WEIGHT_NAMES = ("rel_bias", "norm_attn_g", "w_in", "sgu_ln_g", "sgu_ln_b", "sgu_w", "sgu_b", "ssm_a_re", "ssm_a_im",
                "ssm_log_dt", "ssm_b_re", "ssm_b_im", "ssm_c_re", "ssm_c_im", "ssm_d", "ssm_glu_w", "ssm_glu_b",
                "branch_norm_g", "w_out", "norm_ffn_g", "ffn_w_up", "ffn_conv_w", "ffn_conv_b", "ffn_w_down",
                "norm_ple_g", "ple_w_gate", "ple_w_proj", "final_norm_g")


def _params(sem):
    return pltpu.CompilerParams(dimension_semantics=sem, vmem_limit_bytes=VMEM_LIMIT_BYTES)


def _tile(n, cap, mult=128):
    if n <= cap:
        return n
    best = None
    for t in range(mult, cap + 1, mult):
        if n % t == 0:
            best = t
    assert best is not None, (n, cap)
    return best


def _gelu(x):
    return 0.5 * x * (1.0 + jnp.tanh(0.7978845608028654 * (x + 0.044715 * x * x * x)))


def _gelu_pair(x):
    x2 = x * x
    t = jnp.tanh(0.7978845608028654 * x * (1.0 + 0.044715 * x2))
    half = 0.5 * (1.0 + t)
    return x * half, half + 0.5 * x * (1.0 - t * t) * (0.7978845608028654 + 3.0 * 0.044715 * 0.7978845608028654 * x2)


def _dot(a, b, dims):
    return lax.dot_general(a, b, (dims, ((), ())), preferred_element_type=F32)


def _dotf(a, b, dims):
    return _dot(a.astype(MXU_DTYPE), b.astype(MXU_DTYPE), dims)


NN = ((1,), (0,))
NT = ((1,), (1,))
TN = ((0,), (0,))


def _matmul(a, b, *, name, out_dtype, tm, tn, trans_b=False, residual=None, layer=None):
    m, k = a.shape
    n = b.shape[-2] if trans_b else b.shape[-1]
    tm = _tile(m, tm, 8)
    tn = _tile(n, tn)
    dims = NT if trans_b else NN
    lead = () if layer is None else (None,)
    lidx = () if layer is None else (layer,)

    def body(*refs):
        if residual is None:
            a_ref, b_ref, o_ref = refs
        else:
            a_ref, b_ref, r_ref, o_ref = refs
        acc = _dot(a_ref[...].astype(MXU_DTYPE), b_ref[...].astype(MXU_DTYPE), dims)
        if residual is not None:
            acc = acc + r_ref[...]
        o_ref[...] = acc.astype(o_ref.dtype)

    b_spec = (pl.BlockSpec(lead + (tn, k), lambda i, j: lidx + (j, 0)) if trans_b
              else pl.BlockSpec(lead + (k, tn), lambda i, j: lidx + (0, j)))
    in_specs = [pl.BlockSpec((tm, k), lambda i, j: (i, 0)), b_spec]
    args = [a, b]
    if residual is not None:
        in_specs.append(pl.BlockSpec((tm, tn), lambda i, j: (i, j)))
        args.append(residual)
    return pl.pallas_call(
        body, name=name, grid=(m // tm, n // tn), in_specs=in_specs,
        out_specs=pl.BlockSpec((tm, tn), lambda i, j: (i, j)),
        out_shape=jax.ShapeDtypeStruct((m, n), out_dtype),
        compiler_params=_params(("parallel", "parallel")),
    )(*args)


def _matmul_tn(a, g, *, name, tk, tn, tm=1024):
    m, k = a.shape
    n = g.shape[1]
    tk = _tile(k, tk)
    tn = _tile(n, tn)
    tm = _tile(m, tm, 8)

    def body(a_ref, g_ref, o_ref):
        @pl.when(pl.program_id(2) == 0)
        def _():
            o_ref[...] = jnp.zeros_like(o_ref)

        o_ref[...] += _dot(a_ref[...].astype(MXU_DTYPE), g_ref[...].astype(MXU_DTYPE), TN)

    return pl.pallas_call(
        body, name=name, grid=(k // tk, n // tn, m // tm),
        in_specs=[pl.BlockSpec((tm, tk), lambda i, j, s: (s, i)),
                  pl.BlockSpec((tm, tn), lambda i, j, s: (s, j))],
        out_specs=pl.BlockSpec((tk, tn), lambda i, j, s: (i, j)),
        out_shape=jax.ShapeDtypeStruct((k, n), F32),
        compiler_params=_params(("parallel", "parallel", "arbitrary")),
    )(a, g)


ROWS = 512


def _matmul_rms_bwd(a, b, h, g, dres, *, name, layer, tm):
    s, k = a.shape
    d = b.shape[-2]

    def body(a_ref, b_ref, h_ref, g_ref, dres_ref, dh_ref, dg_ref):
        @pl.when(pl.program_id(0) == 0)
        def _():
            dg_ref[...] = jnp.zeros_like(dg_ref)

        dxn = _dot(a_ref[...].astype(MXU_DTYPE), b_ref[...].astype(MXU_DTYPE), NT)
        x = h_ref[...]
        r = lax.rsqrt(jnp.mean(x * x, axis=-1, keepdims=True) + EPS)
        xhat = x * r
        dg_ref[...] += jnp.sum(dxn * xhat, axis=0, keepdims=True)
        dxh = dxn * g_ref[...]
        dh_ref[...] = dres_ref[...] + r * (dxh - xhat * jnp.mean(dxh * xhat, axis=-1, keepdims=True))

    row = pl.BlockSpec((tm, d), lambda i: (i, 0))
    vec = pl.BlockSpec((1, d), lambda i: (0, 0))
    return pl.pallas_call(
        body, name=name, grid=(s // tm,),
        in_specs=[pl.BlockSpec((tm, k), lambda i: (i, 0)), pl.BlockSpec((None, d, k), lambda i: (layer, 0, 0)),
                  row, vec, row],
        out_specs=[row, vec],
        out_shape=[jax.ShapeDtypeStruct((s, d), F32), jax.ShapeDtypeStruct((1, d), F32)],
        compiler_params=_params(("arbitrary",)),
    )(a, b, h, g.reshape(1, d), dres)


def _loss_head(h, g, target):
    s, d = h.shape

    def body(h_ref, g_ref, t_ref, loss_ref, dh_ref, dg_ref):
        @pl.when(pl.program_id(0) == 0)
        def _():
            loss_ref[...] = jnp.zeros_like(loss_ref)
            dg_ref[...] = jnp.zeros_like(dg_ref)

        x = h_ref[...]
        r = lax.rsqrt(jnp.mean(x * x, axis=-1, keepdims=True) + EPS)
        xhat = x * r
        err = xhat * g_ref[...] - t_ref[...]
        loss_ref[...] += 0.5 * jnp.sum(jnp.mean(err * err, axis=-1, keepdims=True), axis=0, keepdims=True)
        dy = err / d
        dg_ref[...] += jnp.sum(dy * xhat, axis=0, keepdims=True)
        dxh = dy * g_ref[...]
        dh_ref[...] = r * (dxh - xhat * jnp.mean(dxh * xhat, axis=-1, keepdims=True))

    row = pl.BlockSpec((ROWS, d), lambda i: (i, 0))
    vec = pl.BlockSpec((1, d), lambda i: (0, 0))
    one = pl.BlockSpec((1, 1), lambda i: (0, 0))
    return pl.pallas_call(
        body, name="loss_head", grid=(s // ROWS,), in_specs=[row, vec, row], out_specs=[one, row, vec],
        out_shape=[jax.ShapeDtypeStruct((1, 1), F32), jax.ShapeDtypeStruct((s, d), F32),
                   jax.ShapeDtypeStruct((1, d), F32)],
        compiler_params=_params(("arbitrary",)),
    )(h, g.reshape(1, d), target)


def _t5_bucket(dist):
    max_exact = N_BUCKETS // 2
    dd = np.maximum(dist, 0)
    large = max_exact + (np.log(np.maximum(dd, 1) / max_exact) / np.log(REL_MAX_DIST / max_exact)
                         * (N_BUCKETS - max_exact)).astype(np.int32)
    large = np.minimum(large, N_BUCKETS - 1)
    return np.where(dd < max_exact, dd, large).astype(np.int32)


def _bucket_table():
    qq = np.arange(QBLK)[:, None]
    kk = np.arange(QBLK)[None, :]
    out = np.zeros((len(BRANCH_DIL), 2, QBLK, QBLK), np.int32)
    for b, dil in enumerate(BRANCH_DIL):
        out[b, 0] = _t5_bucket((qq - kk + QBLK) * dil)
        out[b, 1] = _t5_bucket((qq - kk) * dil)
    return out


BIAS_TILE = 2 * QBLK


def _bias_build(rel_bias):
    idx = jnp.asarray(_bucket_table())

    def body(idx_ref, rb_ref, o_ref):
        ch = pl.program_id(1)
        row = lax.broadcasted_iota(jnp.int32, (QBLK, QBLK), 0)
        col = lax.broadcasted_iota(jnp.int32, (QBLK, QBLK), 1)
        for part in range(2):
            ids = idx_ref[0, 1 - part]
            valid = (col <= row) if part == 0 else (col >= row)
            for h in range(2):
                acc = jnp.zeros((QBLK, QBLK), F32)
                for b in range(N_BUCKETS):
                    acc = jnp.where(ids == b, rb_ref[b, 2 * ch + h], acc)
                o_ref[0, 0, QBLK * h:QBLK * (h + 1), QBLK * part:QBLK * (part + 1)] = jnp.where(valid, acc, NEG_INF)

    return pl.pallas_call(
        body, name="attn_bias_build", grid=(len(BRANCH_DIL), N_HEADS // 2),
        in_specs=[pl.BlockSpec((1, 2, QBLK, QBLK), lambda b, c: (b, 0, 0, 0)),
                  pl.BlockSpec(memory_space=pltpu.SMEM)],
        out_specs=pl.BlockSpec((1, 1, BIAS_TILE, BIAS_TILE), lambda b, c: (b, c, 0, 0)),
        out_shape=jax.ShapeDtypeStruct((len(BRANCH_DIL), N_HEADS // 2, BIAS_TILE, BIAS_TILE), F32),
        compiler_params=_params(("parallel", "parallel")),
    )(idx, rel_bias)


def _bias_reduce(dbias):
    idx = jnp.asarray(_bucket_table())
    nb = len(BRANCH_DIL)

    def body(idx_ref, d_ref, o_ref):
        def per_bucket(b, carry):
            for h in range(N_HEADS):
                tot = jnp.zeros((), F32)
                for br in range(nb):
                    for part in range(2):
                        tile = d_ref[br, h // 2, QBLK * (h % 2):QBLK * (h % 2 + 1), QBLK * part:QBLK * (part + 1)]
                        tot = tot + jnp.sum(jnp.where(idx_ref[br, 1 - part] == b, tile, 0.0))
                o_ref[b, h] = tot
            return carry

        lax.fori_loop(0, N_BUCKETS, per_bucket, 0)

    return pl.pallas_call(
        body, name="attn_bias_reduce",
        in_specs=[pl.BlockSpec(memory_space=pltpu.VMEM), pl.BlockSpec(memory_space=pltpu.VMEM)],
        out_specs=pl.BlockSpec(memory_space=pltpu.SMEM),
        out_shape=jax.ShapeDtypeStruct((N_BUCKETS, N_HEADS), F32),
        compiler_params=pltpu.CompilerParams(vmem_limit_bytes=VMEM_LIMIT_BYTES),
    )(idx, dbias)


ATTN_IO_DTYPE = F32
ABLK = 2048
N_CHUNK = ATTN_W // 128


def _rows(start, dil):
    if dil > 1:
        return pl.ds(start, QBLK, stride=dil)
    return pl.ds(pl.multiple_of(start, QBLK), QBLK)


def _low_head():
    return lax.broadcasted_iota(jnp.int32, (QBLK, 128), 1) < HEAD_DIM


def _head_split(t):
    low = _low_head()
    zero = jnp.zeros_like(t)
    return jnp.where(low, t, zero), jnp.where(low, zero, t)


def _tile_bias(b_ref, branch, first):
    bias = b_ref[branch]
    if first is None:
        return bias
    col = lax.broadcasted_iota(jnp.int32, (BIAS_TILE, BIAS_TILE), 1)
    return jnp.where(jnp.logical_and(first, col >= QBLK), NEG_INF, bias)


def _loop(n, fn):
    if n == 1:
        fn(jnp.int32(0), 0)
    elif n > 1:
        lax.fori_loop(0, n, fn, 0, unroll=4)


def _for_each_tile(tile, c):
    for branch, dil in enumerate(BRANCH_DIL):
        span = QBLK * dil

        def edge(r, carry, branch=branch, span=span):
            tile(branch, r, False, ABLK - span + r, c == 0)
            return carry

        def inner(t, carry, branch=branch, span=span, dil=dil):
            start = (1 + t // dil) * span + t % dil
            tile(branch, start, True, start - span, None)
            return carry

        _loop(dil, edge)
        _loop((ABLK // span - 1) * dil, inner)


def _attn_chunk_specs(nb):
    blk = (None, ABLK, 128)
    prev = lambda c: jnp.maximum(c - 1, 0)
    return [pl.BlockSpec(blk, lambda ch, c: (ch, c, 0)),
            pl.BlockSpec(blk, lambda ch, c: (N_CHUNK + ch, c, 0)),
            pl.BlockSpec(blk, lambda ch, c: (2 * N_CHUNK + ch, c, 0)),
            pl.BlockSpec(blk, lambda ch, c: (N_CHUNK + ch, prev(c), 0)),
            pl.BlockSpec(blk, lambda ch, c: (2 * N_CHUNK + ch, prev(c), 0)),
            pl.BlockSpec((len(BRANCH_DIL), None, BIAS_TILE, BIAS_TILE), lambda ch, c: (0, ch, 0, 0))]


def _rms_rows(x, g):
    r = lax.rsqrt(jnp.mean(x * x, axis=-1, keepdims=True) + EPS)
    return (x * r * g).astype(MXU_DTYPE)


def _in_proj(h, gain, w_in, layer):
    s, k = h.shape
    tm = 512
    nch = O_SGU // 128

    def body(h_ref, g_ref, w_ref, xn_ref, qkv_ref, zs_ref, us_ref):
        xn = _rms_rows(h_ref[...], g_ref[...])
        xn_ref[...] = xn
        acc = _dot(xn, w_ref[...].astype(MXU_DTYPE), NN)
        for j in range(nch):
            blk = acc[:, 128 * j:128 * (j + 1)]
            if j < N_CHUNK:
                blk = blk * ATTN_SCALE
            qkv_ref[j] = blk.astype(qkv_ref.dtype)
        zs_ref[...] = acc[:, O_SGU:O_SSM]
        us_ref[...] = acc[:, O_SSM:]

    n = w_in.shape[-1]
    return pl.pallas_call(
        body, name="in_proj", grid=(s // tm,),
        in_specs=[pl.BlockSpec((tm, k), lambda i: (i, 0)), pl.BlockSpec((1, k), lambda i: (0, 0)),
                  pl.BlockSpec((None, k, n), lambda i: (layer, 0, 0))],
        out_specs=[pl.BlockSpec((tm, k), lambda i: (i, 0)), pl.BlockSpec((nch, tm, 128), lambda i: (0, i, 0)),
                   pl.BlockSpec((tm, O_SSM - O_SGU), lambda i: (i, 0)), pl.BlockSpec((tm, n - O_SSM), lambda i: (i, 0))],
        out_shape=[jax.ShapeDtypeStruct((s, k), MXU_DTYPE), jax.ShapeDtypeStruct((nch, s, 128), ATTN_IO_DTYPE),
                   jax.ShapeDtypeStruct((s, O_SSM - O_SGU), F32), jax.ShapeDtypeStruct((s, n - O_SSM), F32)],
        compiler_params=_params(("parallel",)),
    )(h, gain.reshape(1, k), w_in)


def _ffn_up(h, gain, w_up, layer):
    s, k = h.shape
    n = w_up.shape[-1]
    tm, tn = 1024, CONV_COLS

    def body(h_ref, g_ref, w_ref, xn_ref, o_ref):
        @pl.when(pl.program_id(1) == 0)
        def _():
            xn_ref[...] = _rms_rows(h_ref[...], g_ref[...])

        o_ref[...] = _dot(xn_ref[...], w_ref[...].astype(MXU_DTYPE), NN).astype(o_ref.dtype)

    return pl.pallas_call(
        body, name="ffn_up", grid=(s // tm, n // tn),
        in_specs=[pl.BlockSpec((tm, k), lambda i, j: (i, 0)), pl.BlockSpec((1, k), lambda i, j: (0, 0)),
                  pl.BlockSpec((None, k, tn), lambda i, j: (layer, 0, j))],
        out_specs=[pl.BlockSpec((tm, k), lambda i, j: (i, 0)), pl.BlockSpec((tm, tn), lambda i, j: (i, j))],
        out_shape=[jax.ShapeDtypeStruct((s, k), MXU_DTYPE), jax.ShapeDtypeStruct((s, n), MXU_DTYPE)],
        compiler_params=_params(("parallel", "arbitrary")),
    )(h, gain.reshape(1, k), w_up)


def _attn2_fwd(qkv_c, bias):
    s = qkv_c.shape[1]
    nb = s // ABLK
    last = len(BRANCH_DIL) - 1

    def body(q_ref, kc_ref, vc_ref, kp_ref, vp_ref, b_ref, o_ref, l_ref, acc_s, m_s, l_s):
        low = _low_head()
        e_st = jnp.concatenate(_head_split(jnp.ones((QBLK, 128), MXU_DTYPE)) * 2, axis=0)

        def tile(branch, start, prev_in_block, pstart, first):
            dil = BRANCH_DIL[branch]
            rq, rp = _rows(start, dil), _rows(pstart, dil)
            k_ref, v_ref = (kc_ref, vc_ref) if prev_in_block else (kp_ref, vp_ref)
            q_st = jnp.concatenate(_head_split(q_ref[rq, :].astype(MXU_DTYPE)), axis=0)
            k_st = jnp.concatenate([kc_ref[rq, :].astype(MXU_DTYPE), k_ref[rp, :].astype(MXU_DTYPE)], axis=0)
            v_st = jnp.concatenate(_head_split(vc_ref[rq, :].astype(MXU_DTYPE))
                                   + _head_split(v_ref[rp, :].astype(MXU_DTYPE)), axis=0)
            sc = _dot(q_st, k_st, NT) + _tile_bias(b_ref, branch, first)
            m_new = jnp.max(sc, axis=-1, keepdims=True)
            if branch > 0:
                m_old2 = m_s[rq, :]
                m_old = jnp.concatenate([m_old2[:, 0:1], m_old2[:, HEAD_DIM:HEAD_DIM + 1]], axis=0)
                m_new = jnp.maximum(m_old, m_new)
                alpha = jnp.exp(m_old - m_new)
            p = jnp.exp(sc - m_new).astype(MXU_DTYPE)
            lhs = jnp.concatenate([p[:QBLK, :QBLK], p[QBLK:, :QBLK], p[:QBLK, QBLK:], p[QBLK:, QBLK:]], axis=1)
            acc2 = _dot(lhs, v_st, NN)
            sum2 = _dot(lhs, e_st, NN)
            m2 = jnp.where(low, m_new[:QBLK], m_new[QBLK:])
            if branch > 0:
                a2 = jnp.where(low, alpha[:QBLK], alpha[QBLK:])
                acc2 = acc2 + a2 * acc_s[rq, :]
                sum2 = sum2 + a2 * l_s[rq, :]
            if branch == last:
                o_ref[rq, :] = acc2 / sum2
                l_ref[rq, :] = m2 + jnp.log(sum2)
            else:
                acc_s[rq, :] = acc2
                m_s[rq, :] = m2
                l_s[rq, :] = sum2

        _for_each_tile(tile, pl.program_id(1))

    out_spec = pl.BlockSpec((None, ABLK, 128), lambda ch, c: (ch, c, 0))
    return pl.pallas_call(
        body, name="attn_fwd", grid=(N_CHUNK, nb), in_specs=_attn_chunk_specs(nb),
        out_specs=[out_spec, out_spec],
        out_shape=[jax.ShapeDtypeStruct((N_CHUNK, s, 128), F32)] * 2,
        scratch_shapes=[pltpu.VMEM((ABLK, 128), F32)] * 3,
        compiler_params=_params(("parallel", "arbitrary")),
    )(qkv_c, qkv_c, qkv_c, qkv_c, qkv_c, bias)


def _attn2_bwd(qkv_c, bias, lse_c, delta_c, do_c):
    s = qkv_c.shape[1]
    nb = s // ABLK
    nbr = len(BRANCH_DIL)

    def body(q_ref, kc_ref, vc_ref, kp_ref, vp_ref, b_ref, l_ref, dl_ref, do_ref,
             dq_ref, dk_ref, dv_ref, *rest):
        ek_refs, ev_refs, db_ref = rest[:nbr], rest[nbr:2 * nbr], rest[2 * nbr]
        c = pl.program_id(1)

        @pl.when(c == 0)
        def _():
            db_ref[...] = jnp.zeros_like(db_ref)

        for r in (dq_ref, dk_ref, dv_ref) + tuple(ek_refs) + tuple(ev_refs):
            r[...] = jnp.zeros_like(r)

        def tile(branch, start, prev_in_block, pstart, first):
            dil = BRANCH_DIL[branch]
            rq, rp = _rows(start, dil), _rows(pstart, dil)
            k_ref, v_ref = (kc_ref, vc_ref) if prev_in_block else (kp_ref, vp_ref)
            kc2 = kc_ref[rq, :].astype(MXU_DTYPE)
            kp2 = k_ref[rp, :].astype(MXU_DTYPE)
            q_st = jnp.concatenate(_head_split(q_ref[rq, :].astype(MXU_DTYPE)), axis=0)
            do_st = jnp.concatenate(_head_split(do_ref[rq, :].astype(MXU_DTYPE)), axis=0)
            k_st = jnp.concatenate([kc2, kp2], axis=0)
            v_st = jnp.concatenate([vc_ref[rq, :].astype(MXU_DTYPE), v_ref[rp, :].astype(MXU_DTYPE)], axis=0)
            kh_st = jnp.concatenate(_head_split(kc2) + _head_split(kp2), axis=0)
            lse2 = l_ref[rq, :]
            del2 = dl_ref[rq, :]
            lse_st = jnp.concatenate([lse2[:, 0:1], lse2[:, HEAD_DIM:HEAD_DIM + 1]], axis=0)
            del_st = jnp.concatenate([del2[:, 0:1], del2[:, HEAD_DIM:HEAD_DIM + 1]], axis=0)
            p = jnp.exp(_dot(q_st, k_st, NT) + _tile_bias(b_ref, branch, first) - lse_st)
            ds = p * (_dot(do_st, v_st, NT) - del_st)
            db_ref[branch] += ds
            ds = ds.astype(MXU_DTYPE)
            p = p.astype(MXU_DTYPE)
            lhs = jnp.concatenate([ds[:QBLK, :QBLK], ds[QBLK:, :QBLK], ds[:QBLK, QBLK:], ds[QBLK:, QBLK:]], axis=1)
            dk_st = _dot(ds, q_st, TN)
            dv_st = _dot(p, do_st, TN)
            dq_ref[rq, :] += _dot(lhs, kh_st, NN)
            dk_ref[rq, :] += dk_st[:QBLK]
            dv_ref[rq, :] += dv_st[:QBLK]
            if prev_in_block:
                dk_ref[rp, :] += dk_st[QBLK:]
                dv_ref[rp, :] += dv_st[QBLK:]
            else:
                ek_refs[branch][rq, :] = dk_st[QBLK:]
                ev_refs[branch][rq, :] = dv_st[QBLK:]

        _for_each_tile(tile, c)

    blk = pl.BlockSpec((None, ABLK, 128), lambda ch, c: (ch, c, 0))
    outs = pl.pallas_call(
        body, name="attn_bwd", grid=(N_CHUNK, nb), in_specs=_attn_chunk_specs(nb) + [blk, blk, blk],
        out_specs=[blk] * (3 + 2 * nbr) + [pl.BlockSpec((nbr, None, BIAS_TILE, BIAS_TILE), lambda ch, c: (0, ch, 0, 0))],
        out_shape=[jax.ShapeDtypeStruct((N_CHUNK, s, 128), F32)] * (3 + 2 * nbr)
        + [jax.ShapeDtypeStruct((nbr, N_HEADS // 2, BIAS_TILE, BIAS_TILE), F32)],
        compiler_params=_params(("arbitrary", "arbitrary")),
    )(qkv_c, qkv_c, qkv_c, qkv_c, qkv_c, bias, lse_c, delta_c, do_c)
    return outs[0], outs[1], outs[2], outs[3:3 + nbr], outs[3 + nbr:3 + 2 * nbr], outs[3 + 2 * nbr]


def _attn2_bwd_sum(dq, dk, dv, ek, ev, dzs, dus):
    s = dq.shape[1]
    nrb = s // QBLK
    per_blk = ABLK // QBLK
    nbr = len(BRANCH_DIL)

    def body(*refs):
        dq_ref, dk_ref, dv_ref = refs[:3]
        ek_refs, ev_refs = refs[3:3 + nbr], refs[3 + nbr:3 + 2 * nbr]
        dzs_ref, dus_ref, o_ref = refs[3 + 2 * nbr:]
        i = pl.program_id(0)
        dkt, dvt = dk_ref[...], dv_ref[...]
        for b, dil in enumerate(BRANCH_DIL):
            j = i + dil
            ok = jnp.logical_and(j < nrb, j % per_blk < dil)
            dkt = dkt + jnp.where(ok, ek_refs[b][...], 0.0)
            dvt = dvt + jnp.where(ok, ev_refs[b][...], 0.0)
        for ch in range(N_CHUNK):
            o_ref[:, 128 * ch:128 * (ch + 1)] = (dq_ref[ch] * ATTN_SCALE).astype(o_ref.dtype)
            o_ref[:, ATTN_W + 128 * ch:ATTN_W + 128 * (ch + 1)] = dkt[ch].astype(o_ref.dtype)
            o_ref[:, 2 * ATTN_W + 128 * ch:2 * ATTN_W + 128 * (ch + 1)] = dvt[ch].astype(o_ref.dtype)
        o_ref[:, O_SGU:O_SSM] = dzs_ref[...].astype(o_ref.dtype)
        o_ref[:, O_SSM:] = dus_ref[...].astype(o_ref.dtype)

    here = pl.BlockSpec((N_CHUNK, QBLK, 128), lambda i: (0, i, 0))
    edge_specs = [pl.BlockSpec((N_CHUNK, QBLK, 128),
                               functools.partial(lambda i, d: (0, jnp.minimum(i + d, nrb - 1), 0), d=dil))
                  for dil in BRANCH_DIL]
    return pl.pallas_call(
        body, name="attn_bwd_sum", grid=(nrb,),
        in_specs=[here, here, here] + edge_specs + edge_specs
        + [pl.BlockSpec((QBLK, 2 * SGU_W), lambda i: (i, 0)), pl.BlockSpec((QBLK, SSM_W), lambda i: (i, 0))],
        out_specs=pl.BlockSpec((QBLK, O_SSM + SSM_W), lambda i: (i, 0)),
        out_shape=jax.ShapeDtypeStruct((s, O_SSM + SSM_W), MXU_DTYPE),
        compiler_params=_params(("parallel",)),
    )(dq, dk, dv, *ek, *ev, dzs, dus)


SGU_ROWS = 512


def _sgu_norm(v_g):
    mu = jnp.mean(v_g, axis=-1, keepdims=True)
    cen = v_g - mu
    var = jnp.mean(cen * cen, axis=-1, keepdims=True)
    rstd = lax.rsqrt(var + EPS)
    return cen * rstd, rstd


def _sgu_fwd(zs, ln_g, ln_b, w_mask, b_t):
    s = zs.shape[0]
    nch = SGU_ROWS // SGU_CHUNK

    def body(z_ref, g_ref, b_ref, w_ref, bt_ref, o_ref):
        gz = _gelu(z_ref[...])
        for g in range(SGU_G):
            sl = slice(SGU_GW * g, SGU_GW * (g + 1))
            u_g = gz[:, sl]
            xhat, _ = _sgu_norm(gz[:, SGU_W + SGU_GW * g:SGU_W + SGU_GW * (g + 1)])
            vn = (xhat * g_ref[:, sl] + b_ref[:, sl]).astype(MXU_DTYPE)
            wg = w_ref[g].astype(MXU_DTYPE)
            for ci in range(nch):
                rs = slice(SGU_CHUNK * ci, SGU_CHUNK * (ci + 1))
                mixed = _dot(wg, vn[rs], NN) + bt_ref[:, g:g + 1]
                o_ref[rs, sl] = u_g[rs] * mixed

    full = lambda shape: pl.BlockSpec(shape, lambda i: tuple(0 for _ in shape))
    return pl.pallas_call(
        body, name="sgu_fwd", grid=(s // SGU_ROWS,),
        in_specs=[pl.BlockSpec((SGU_ROWS, 2 * SGU_W), lambda i: (i, 0)), full((1, SGU_W)), full((1, SGU_W)),
                  full((SGU_G, SGU_CHUNK, SGU_CHUNK)), full((SGU_CHUNK, SGU_G))],
        out_specs=pl.BlockSpec((SGU_ROWS, SGU_W), lambda i: (i, 0)),
        out_shape=jax.ShapeDtypeStruct((s, SGU_W), F32),
        compiler_params=_params(("parallel",)),
    )(zs, ln_g.reshape(1, SGU_W), ln_b.reshape(1, SGU_W), w_mask, b_t)


def _sgu_bwd(zs, ln_g, ln_b, w_mask, b_t, dy):
    s = zs.shape[0]
    nch = SGU_ROWS // SGU_CHUNK

    def body(z_ref, g_ref, b_ref, w_ref, bt_ref, dy_ref, dz_ref, dg_ref, dbb_ref, dw_ref, dbt_ref):
        @pl.when(pl.program_id(0) == 0)
        def _():
            dg_ref[...] = jnp.zeros_like(dg_ref)
            dbb_ref[...] = jnp.zeros_like(dbb_ref)
            dw_ref[...] = jnp.zeros_like(dw_ref)
            dbt_ref[...] = jnp.zeros_like(dbt_ref)

        z = z_ref[...]
        gz, dgelu = _gelu_pair(z)
        dy = dy_ref[...]
        for g in range(SGU_G):
            sl = slice(SGU_GW * g, SGU_GW * (g + 1))
            sv = slice(SGU_W + SGU_GW * g, SGU_W + SGU_GW * (g + 1))
            u_g = gz[:, sl]
            xhat, rstd = _sgu_norm(gz[:, sv])
            gain = g_ref[:, sl]
            vn = (xhat * gain + b_ref[:, sl]).astype(MXU_DTYPE)
            wg = w_ref[g].astype(MXU_DTYPE)
            dy_g = dy[:, sl]
            dvn_parts = []
            for ci in range(nch):
                rs = slice(SGU_CHUNK * ci, SGU_CHUNK * (ci + 1))
                mixed = _dot(wg, vn[rs], NN) + bt_ref[:, g:g + 1]
                dz_ref[rs, sl] = (dy_g[rs] * mixed * dgelu[rs, sl]).astype(dz_ref.dtype)
                dmixed = dy_g[rs] * u_g[rs]
                dm = dmixed.astype(MXU_DTYPE)
                dvn_parts.append(_dot(wg, dm, TN))
                dw_ref[g] += _dot(dm, vn[rs], NT)
                dbt_ref[:, g:g + 1] += jnp.sum(dmixed, axis=-1, keepdims=True)
            dvn = jnp.concatenate(dvn_parts, axis=0)
            dg_ref[:, sl] += jnp.sum(dvn * xhat, axis=0, keepdims=True)
            dbb_ref[:, sl] += jnp.sum(dvn, axis=0, keepdims=True)
            dxh = dvn * gain
            dv = rstd * (dxh - jnp.mean(dxh, axis=-1, keepdims=True)
                         - xhat * jnp.mean(dxh * xhat, axis=-1, keepdims=True))
            dz_ref[:, sv] = (dv * dgelu[:, sv]).astype(dz_ref.dtype)

    full = lambda shape: pl.BlockSpec(shape, lambda i: tuple(0 for _ in shape))
    return pl.pallas_call(
        body, name="sgu_bwd", grid=(s // SGU_ROWS,),
        in_specs=[pl.BlockSpec((SGU_ROWS, 2 * SGU_W), lambda i: (i, 0)), full((1, SGU_W)), full((1, SGU_W)),
                  full((SGU_G, SGU_CHUNK, SGU_CHUNK)), full((SGU_CHUNK, SGU_G)),
                  pl.BlockSpec((SGU_ROWS, SGU_W), lambda i: (i, 0))],
        out_specs=[pl.BlockSpec((SGU_ROWS, 2 * SGU_W), lambda i: (i, 0)), full((1, SGU_W)), full((1, SGU_W)),
                   full((SGU_G, SGU_CHUNK, SGU_CHUNK)), full((SGU_CHUNK, SGU_G))],
        out_shape=[jax.ShapeDtypeStruct((s, 2 * SGU_W), MXU_DTYPE), jax.ShapeDtypeStruct((1, SGU_W), F32),
                   jax.ShapeDtypeStruct((1, SGU_W), F32), jax.ShapeDtypeStruct((SGU_G, SGU_CHUNK, SGU_CHUNK), F32),
                   jax.ShapeDtypeStruct((SGU_CHUNK, SGU_G), F32)],
        compiler_params=_params(("arbitrary",)),
    )(zs, ln_g.reshape(1, SGU_W), ln_b.reshape(1, SGU_W), w_mask, b_t, dy)


def _ssm_discretize(a_re, a_im, log_dt, b_re, b_im):
    dt = jnp.exp(log_dt)[:, None]
    mag = jnp.exp(a_re * dt)
    ab_re = mag * jnp.cos(a_im * dt)
    ab_im = mag * jnp.sin(a_im * dt)
    den = a_re * a_re + a_im * a_im
    f_re = ((ab_re - 1.0) * a_re + ab_im * a_im) / den
    f_im = (ab_im * a_re - (ab_re - 1.0) * a_im) / den
    bb_re = f_re[:, :, None] * b_re - f_im[:, :, None] * b_im
    bb_im = f_re[:, :, None] * b_im + f_im[:, :, None] * b_re
    return ab_re, ab_im, bb_re, bb_im


def _ssm_operands(a_re, a_im, log_dt, b_re, b_im, c_re, c_im):
    ab_re, ab_im, bb_re, bb_im = _ssm_discretize(a_re, a_im, log_dt, b_re, b_im)
    eye = jnp.eye(SSM_G, dtype=F32)
    b_blk = jnp.einsum("pgnc,gh->gcphn", jnp.stack([bb_re, bb_im]), eye).reshape(SSM_W, 2 * NSTATE)
    c_mat = jnp.einsum("pgcn,gh->pgnhc", jnp.stack([c_re, -c_im]), eye).reshape(2 * NSTATE, SSM_W)
    a_row = jnp.stack([ab_re.reshape(NSTATE), ab_im.reshape(NSTATE)])
    p_re, p_im = a_row[0:1], a_row[1:2]
    while p_re.shape[0] < SSM_TSEG:
        l_re, l_im = p_re[-1:], p_im[-1:]
        p_re, p_im = (jnp.concatenate([p_re, p_re * l_re - p_im * l_im]),
                      jnp.concatenate([p_im, p_re * l_im + p_im * l_re]))
    p_tab = jnp.stack([p_re, p_im])
    return b_blk.astype(MXU_DTYPE), c_mat.astype(MXU_DTYPE), a_row, p_tab


def _lane_chunks():
    return [(lo, lo + SSM_LANE_CHUNK) for lo in range(0, NSTATE, SSM_LANE_CHUNK)]


def _seg_rows(j):
    return pl.ds(pl.multiple_of(j * SSM_NSEG, SSM_NSEG), SSM_NSEG)


def _to_segments(t):
    s, w = t.shape
    return t.reshape(s // SSM_TB, SSM_NSEG, SSM_TSEG, w).transpose(0, 2, 1, 3).reshape(s, w)


def _from_segments(t):
    s, w = t.shape
    return t.reshape(s // SSM_TB, SSM_TSEG, SSM_NSEG, w).transpose(0, 2, 1, 3).reshape(s, w)


def _ssm_local_scan(buf, a_ref, *, reverse):
    ends_re, ends_im = [], []
    for lo, hi in _lane_chunks():
        are = jnp.broadcast_to(a_ref[0:1, lo:hi], (SSM_NSEG, hi - lo))
        aim = jnp.broadcast_to(a_ref[1:2, lo:hi], (SSM_NSEG, hi - lo))
        if reverse:
            aim = -aim

        def step(jj, carry, lo=lo, hi=hi, are=are, aim=aim):
            xr, xi = carry
            j = (SSM_TSEG - 1 - jj) if reverse else jj
            tr = buf[_seg_rows(j), lo:hi]
            ti = buf[_seg_rows(j), NSTATE + lo:NSTATE + hi]
            nr = are * xr - aim * xi + tr
            ni = are * xi + aim * xr + ti
            buf[_seg_rows(j), lo:hi] = nr
            buf[_seg_rows(j), NSTATE + lo:NSTATE + hi] = ni
            return nr, ni

        zero = jnp.zeros((SSM_NSEG, hi - lo), F32)
        xr, xi = lax.fori_loop(0, SSM_TSEG, step, (zero, zero), unroll=4)
        ends_re.append(xr)
        ends_im.append(xi)
    return jnp.concatenate(ends_re, axis=1), jnp.concatenate(ends_im, axis=1)


def _ssm_entry_states(ends_re, ends_im, carry_ref, p_ref, entry_ref, *, reverse):
    at_re = p_ref[0, SSM_TSEG - 1:SSM_TSEG, :]
    at_im = p_ref[1, SSM_TSEG - 1:SSM_TSEG, :]
    if reverse:
        at_im = -at_im
    cur_re = carry_ref[0:1, 0:NSTATE]
    cur_im = carry_ref[0:1, NSTATE:2 * NSTATE]
    order = range(SSM_NSEG - 1, -1, -1) if reverse else range(SSM_NSEG)
    for i in order:
        entry_ref[0, i:i + 1, 0:NSTATE] = cur_re
        entry_ref[0, i:i + 1, NSTATE:2 * NSTATE] = cur_im
        nxt_re = ends_re[i:i + 1] + at_re * cur_re - at_im * cur_im
        nxt_im = ends_im[i:i + 1] + at_re * cur_im + at_im * cur_re
        cur_re, cur_im = nxt_re, nxt_im
    carry_ref[0:1, 0:NSTATE] = cur_re
    carry_ref[0:1, NSTATE:2 * NSTATE] = cur_im


def _ssm_fixup(buf, p_ref, entry_ref, *, reverse):
    for lo, hi in _lane_chunks():
        e_re = entry_ref[0, :, lo:hi]
        e_im = entry_ref[0, :, NSTATE + lo:NSTATE + hi]

        def step(j, carry, lo=lo, hi=hi, e_re=e_re, e_im=e_im):
            jp = (SSM_TSEG - 1 - j) if reverse else j
            pr = p_ref[0, pl.ds(jp, 1), lo:hi]
            pi = p_ref[1, pl.ds(jp, 1), lo:hi]
            if reverse:
                pi = -pi
            buf[_seg_rows(j), lo:hi] = buf[_seg_rows(j), lo:hi] + pr * e_re - pi * e_im
            buf[_seg_rows(j), NSTATE + lo:NSTATE + hi] = (buf[_seg_rows(j), NSTATE + lo:NSTATE + hi]
                                                           + pr * e_im + pi * e_re)
            return carry

        lax.fori_loop(0, SSM_TSEG, step, 0, unroll=4)


def _ssm_fwd(u, ops, d_skip, glu_w, glu_b):
    b_blk, c_mat, a_row, p_tab = ops
    s = u.shape[0]
    nblk = s // SSM_TB

    def body(u_ref, bb_ref, cm_ref, a_ref, p_ref, d_ref, gw_ref, gb_ref, y_ref, entry_ref, xbuf, carry):
        @pl.when(pl.program_id(0) == 0)
        def _():
            carry[...] = jnp.zeros_like(carry)

        uu = u_ref[...]
        xbuf[...] = _dotf(uu, bb_ref[...], NN)
        ends_re, ends_im = _ssm_local_scan(xbuf, a_ref, reverse=False)
        _ssm_entry_states(ends_re, ends_im, carry, p_ref, entry_ref, reverse=False)
        _ssm_fixup(xbuf, p_ref, entry_ref, reverse=False)
        y = _dotf(xbuf[...],cm_ref[...], NN) + d_ref[...] * uu
        y2 = _gelu(y)
        gate = jax.nn.sigmoid(_dot(y2.astype(MXU_DTYPE), gw_ref[...].astype(MXU_DTYPE), NN) + gb_ref[...])
        y_ref[...] = y2 * gate

    full = lambda shape: pl.BlockSpec(shape, lambda i: tuple(0 for _ in shape))
    y_seg, entry = pl.pallas_call(
        body, name="ssm_fwd", grid=(nblk,),
        in_specs=[pl.BlockSpec((SSM_TB, SSM_W), lambda i: (i, 0)), full(b_blk.shape), full(c_mat.shape),
                  full(a_row.shape), full(p_tab.shape), full((1, SSM_W)), full((SSM_W, SSM_W)), full((1, SSM_W))],
        out_specs=[pl.BlockSpec((SSM_TB, SSM_W), lambda i: (i, 0)),
                   pl.BlockSpec((1, SSM_NSEG, 2 * NSTATE), lambda i: (i, 0, 0))],
        out_shape=[jax.ShapeDtypeStruct((s, SSM_W), F32), jax.ShapeDtypeStruct((nblk, SSM_NSEG, 2 * NSTATE), F32)],
        scratch_shapes=[pltpu.VMEM((SSM_TB, 2 * NSTATE), F32), pltpu.VMEM((SSM_NSEG, 2 * NSTATE), F32)],
        compiler_params=_params(("arbitrary",)),
    )(_to_segments(u), b_blk, c_mat, a_row, p_tab, d_skip.reshape(1, SSM_W), glu_w, glu_b.reshape(1, SSM_W))
    return _from_segments(y_seg), entry


def _ssm_bwd(u, entry, ops, d_skip, glu_w, glu_b, dout):
    b_blk, c_mat, a_row, p_tab = ops
    s = u.shape[0]
    nblk = s // SSM_TB

    def body(u_ref, en_ref, bb_ref, cm_ref, a_ref, p_ref, d_ref, gw_ref, gb_ref, do_ref,
             du_ref, dbb_ref, dcm_ref, da_ref, dd_ref, dgw_ref, dgb_ref, xbuf, gbuf, gcarry, gentry):
        @pl.when(pl.program_id(0) == 0)
        def _():
            gcarry[...] = jnp.zeros_like(gcarry)
            for r in (dbb_ref, dcm_ref, da_ref, dd_ref, dgw_ref, dgb_ref):
                r[...] = jnp.zeros_like(r)

        uu = u_ref[...]
        xbuf[...] = _dotf(uu, bb_ref[...], NN)
        _ssm_local_scan(xbuf, a_ref, reverse=False)
        _ssm_fixup(xbuf, p_ref, en_ref, reverse=False)
        y = _dotf(xbuf[...],cm_ref[...], NN) + d_ref[...] * uu
        y2, dgelu = _gelu_pair(y)
        y2m = y2.astype(MXU_DTYPE)
        gwm = gw_ref[...].astype(MXU_DTYPE)
        gate = jax.nn.sigmoid(_dot(y2m, gwm, NN) + gb_ref[...])
        dout = do_ref[...]
        dpre = dout * y2 * gate * (1.0 - gate)
        dprem = dpre.astype(MXU_DTYPE)
        dy2 = dout * gate + _dot(dprem, gwm, NT)
        dgw_ref[...] += _dot(y2m, dprem, TN)
        dgb_ref[...] += jnp.sum(dpre, axis=0, keepdims=True)
        dy = dy2 * dgelu
        dd_ref[...] += jnp.sum(dy * uu, axis=0, keepdims=True)
        dcm_ref[...] += _dotf(xbuf[...],dy, TN)
        gbuf[...] = _dotf(dy, cm_ref[...], NT)
        gs_re, gs_im = _ssm_local_scan(gbuf, a_ref, reverse=True)
        _ssm_entry_states(gs_re, gs_im, gcarry, p_ref, gentry, reverse=True)
        _ssm_fixup(gbuf, p_ref, gentry, reverse=True)
        du_ref[...] = (_dotf(gbuf[...], bb_ref[...], NT) + d_ref[...] * dy).astype(du_ref.dtype)
        dbb_ref[...] += _dotf(uu, gbuf[...], TN)
        for lo, hi in _lane_chunks():
            def step(j, carry, lo=lo, hi=hi):
                acc_re, acc_im = carry
                g_re = gbuf[_seg_rows(j), lo:hi]
                g_im = gbuf[_seg_rows(j), NSTATE + lo:NSTATE + hi]
                x_re = xbuf[_seg_rows(j - 1), lo:hi]
                x_im = xbuf[_seg_rows(j - 1), NSTATE + lo:NSTATE + hi]
                return acc_re + g_re * x_re + g_im * x_im, acc_im + g_im * x_re - g_re * x_im

            g0_re = gbuf[_seg_rows(0), lo:hi]
            g0_im = gbuf[_seg_rows(0), NSTATE + lo:NSTATE + hi]
            e_re = en_ref[0, :, lo:hi]
            e_im = en_ref[0, :, NSTATE + lo:NSTATE + hi]
            init = (g0_re * e_re + g0_im * e_im, g0_im * e_re - g0_re * e_im)
            acc_re, acc_im = lax.fori_loop(1, SSM_TSEG, step, init, unroll=4)
            da_ref[0:1, lo:hi] += jnp.sum(acc_re, axis=0, keepdims=True)
            da_ref[1:2, lo:hi] += jnp.sum(acc_im, axis=0, keepdims=True)

    full = lambda shape: pl.BlockSpec(shape, lambda i: tuple(0 for _ in shape))
    rev = pl.BlockSpec((SSM_TB, SSM_W), lambda i: (nblk - 1 - i, 0))
    outs = pl.pallas_call(
        body, name="ssm_bwd", grid=(nblk,),
        in_specs=[rev, pl.BlockSpec((1, SSM_NSEG, 2 * NSTATE), lambda i: (nblk - 1 - i, 0, 0)),
                  full(b_blk.shape), full(c_mat.shape), full(a_row.shape), full(p_tab.shape),
                  full((1, SSM_W)), full((SSM_W, SSM_W)), full((1, SSM_W)), rev],
        out_specs=[rev, full(b_blk.shape), full(c_mat.shape), full(a_row.shape), full((1, SSM_W)),
                   full((SSM_W, SSM_W)), full((1, SSM_W))],
        out_shape=[jax.ShapeDtypeStruct((s, SSM_W), MXU_DTYPE), jax.ShapeDtypeStruct(b_blk.shape, F32),
                   jax.ShapeDtypeStruct(c_mat.shape, F32), jax.ShapeDtypeStruct(a_row.shape, F32),
                   jax.ShapeDtypeStruct((1, SSM_W), F32), jax.ShapeDtypeStruct((SSM_W, SSM_W), F32),
                   jax.ShapeDtypeStruct((1, SSM_W), F32)],
        scratch_shapes=[pltpu.VMEM((SSM_TB, 2 * NSTATE), F32), pltpu.VMEM((SSM_TB, 2 * NSTATE), F32),
                        pltpu.VMEM((SSM_NSEG, 2 * NSTATE), F32), pltpu.VMEM((1, SSM_NSEG, 2 * NSTATE), F32)],
        compiler_params=_params(("arbitrary",)),
    )(_to_segments(u), entry, b_blk, c_mat, a_row, p_tab, d_skip.reshape(1, SSM_W), glu_w, glu_b.reshape(1, SSM_W),
      _to_segments(dout))
    return (_from_segments(outs[0]),) + tuple(outs[1:])


MIX_SEGS = ((0, ATTN_W), (ATTN_W, ATTN_W + SGU_W), (ATTN_W + SGU_W, D_MODEL))


def _chunks_to_rows(a_ref):
    return jnp.concatenate([a_ref[ch] for ch in range(N_CHUNK)], axis=1)


def _mix_fwd(y_attn_c, y_sgu, y_ssm, gain):
    s = y_sgu.shape[0]

    def body(a_ref, b_ref, c_ref, g_ref, o_ref):
        for x, (lo, hi) in zip((_chunks_to_rows(a_ref), b_ref[...], c_ref[...]), MIX_SEGS):
            r = lax.rsqrt(jnp.mean(x * x, axis=-1, keepdims=True) + EPS)
            o_ref[:, lo:hi] = (x * r * g_ref[:, lo:hi]).astype(o_ref.dtype)

    row = lambda w: pl.BlockSpec((ROWS, w), lambda i: (i, 0))
    return pl.pallas_call(
        body, name="mix_fwd", grid=(s // ROWS,),
        in_specs=[pl.BlockSpec((N_CHUNK, ROWS, 128), lambda i: (0, i, 0)), row(SGU_W), row(SSM_W),
                  pl.BlockSpec((1, D_MODEL), lambda i: (0, 0))],
        out_specs=row(D_MODEL), out_shape=jax.ShapeDtypeStruct((s, D_MODEL), MXU_DTYPE),
        compiler_params=_params(("parallel",)),
    )(y_attn_c, y_sgu, y_ssm, gain.reshape(1, D_MODEL))


def _mix_bwd(y_attn_c, y_sgu, y_ssm, gain, dmix):
    s = y_sgu.shape[0]

    def body(a_ref, b_ref, c_ref, g_ref, dm_ref, da_ref, dl_ref, db_ref, dc_ref, dg_ref):
        @pl.when(pl.program_id(0) == 0)
        def _():
            dg_ref[...] = jnp.zeros_like(dg_ref)

        grads = []
        for x, (lo, hi) in zip((_chunks_to_rows(a_ref), b_ref[...], c_ref[...]), MIX_SEGS):
            r = lax.rsqrt(jnp.mean(x * x, axis=-1, keepdims=True) + EPS)
            xhat = x * r
            dm = dm_ref[:, lo:hi].astype(F32)
            dg_ref[:, lo:hi] += jnp.sum(dm * xhat, axis=0, keepdims=True)
            dxh = dm * g_ref[:, lo:hi]
            grads.append(r * (dxh - xhat * jnp.mean(dxh * xhat, axis=-1, keepdims=True)))
        db_ref[...] = grads[1]
        dc_ref[...] = grads[2]
        low = lax.broadcasted_iota(jnp.int32, (ROWS, 128), 1) < HEAD_DIM
        for ch in range(N_CHUNK):
            d_c = grads[0][:, 128 * ch:128 * (ch + 1)]
            da_ref[ch] = d_c.astype(da_ref.dtype)
            prod = d_c * a_ref[ch]
            dl_ref[ch] = jnp.where(low, jnp.sum(prod[:, :HEAD_DIM], axis=-1, keepdims=True),
                                   jnp.sum(prod[:, HEAD_DIM:], axis=-1, keepdims=True))

    row = lambda w: pl.BlockSpec((ROWS, w), lambda i: (i, 0))
    vec = pl.BlockSpec((1, D_MODEL), lambda i: (0, 0))
    chunked = pl.BlockSpec((N_CHUNK, ROWS, 128), lambda i: (0, i, 0))
    return pl.pallas_call(
        body, name="mix_bwd", grid=(s // ROWS,),
        in_specs=[chunked, row(SGU_W), row(SSM_W), vec, row(D_MODEL)],
        out_specs=[chunked, chunked, row(SGU_W), row(SSM_W), vec],
        out_shape=[jax.ShapeDtypeStruct((N_CHUNK, s, 128), ATTN_IO_DTYPE), jax.ShapeDtypeStruct((N_CHUNK, s, 128), F32),
                   jax.ShapeDtypeStruct((s, SGU_W), F32), jax.ShapeDtypeStruct((s, SSM_W), F32),
                   jax.ShapeDtypeStruct((1, D_MODEL), F32)],
        compiler_params=_params(("arbitrary",)),
    )(y_attn_c, y_sgu, y_ssm, gain.reshape(1, D_MODEL), dmix)


CONV_ROWS = 256
CONV_COLS = 1408
CONV_PAIR = 2 * CONV_COLS
HALO = 16


def _interleave_ff(t):
    lead = t.shape[:-1]
    nb = D_FF // CONV_COLS
    return jnp.swapaxes(t.reshape(lead + (2, nb, CONV_COLS)), -3, -2).reshape(lead + (2 * D_FF,))


def _deinterleave_ff(t):
    lead = t.shape[:-1]
    nb = D_FF // CONV_COLS
    return jnp.swapaxes(t.reshape(lead + (nb, 2, CONV_COLS)), -3, -2).reshape(lead + (2 * D_FF,))


def _conv_in_specs():
    halo_idx = lambda i: jnp.maximum(i * (CONV_ROWS // HALO) - 1, 0)
    return [pl.BlockSpec((CONV_ROWS, CONV_PAIR), lambda j, i: (i, j)),
            pl.BlockSpec((HALO, CONV_PAIR), lambda j, i: (halo_idx(i), j)),
            pl.BlockSpec((3, CONV_PAIR), lambda j, i: (0, j)),
            pl.BlockSpec((1, CONV_PAIR), lambda j, i: (0, j))]


def _shift_matrix(rows, back):
    r = lax.broadcasted_iota(jnp.int32, (2 * rows, rows), 0)
    c = lax.broadcasted_iota(jnp.int32, (2 * rows, rows), 1)
    step = jnp.where(r < rows, 1, 2)
    t = jnp.where(r < rows, r, r - rows)
    src = t - step if back else t + step
    return jnp.where(c == src, 1.0, 0.0).astype(MXU_DTYPE)


def _patch_rows(x, at_end, rows):
    tile = 8
    n = x.shape[0]
    idx = lax.broadcasted_iota(jnp.int32, (tile, x.shape[1]), 0)
    piece = x[n - tile:] if at_end else x[:tile]
    for k, row in enumerate(rows):
        where_row = (tile - len(rows) + k) if at_end else k
        piece = jnp.where(idx == where_row, row, piece)
    return jnp.concatenate([x[:n - tile], piece], axis=0) if at_end else jnp.concatenate([piece, x[tile:]], axis=0)


def _mxu_taps(main_m, halo, first):
    shifted = _dot(_shift_matrix(main_m.shape[0], True), main_m, NN)
    h1 = jnp.where(first, 0.0, halo[HALO - 1:HALO, :])
    h2 = jnp.where(first, 0.0, halo[HALO - 2:HALO - 1, :])
    x1 = _patch_rows(shifted[:main_m.shape[0]], False, [h1])
    x2 = _patch_rows(shifted[main_m.shape[0]:], False, [h2, h1])
    return x1, x2


def _conv_gate(w_ref, b_ref, x2, x1, x0):
    return w_ref[0:1, :] * x2 + w_ref[1:2, :] * x1 + w_ref[2:3, :] * x0 + b_ref[...]


def _ffn_gate_fwd(hh, conv_w, conv_b):
    s = hh.shape[0]

    def body(m_ref, h_ref, w_ref, b_ref, o_ref):
        first = pl.program_id(1) == 0
        main_m = m_ref[...]
        x1, x2 = _mxu_taps(main_m, h_ref[...].astype(F32), first)
        conv = _conv_gate(w_ref, b_ref, x2, x1, main_m.astype(F32))
        o_ref[...] = (_gelu(conv[:, CONV_COLS:]) * conv[:, :CONV_COLS]).astype(o_ref.dtype)

    return pl.pallas_call(
        body, name="ffn_act_fwd", grid=(D_FF // CONV_COLS, s // CONV_ROWS), in_specs=_conv_in_specs(),
        out_specs=pl.BlockSpec((CONV_ROWS, CONV_COLS), lambda j, i: (i, j)),
        out_shape=jax.ShapeDtypeStruct((s, D_FF), MXU_DTYPE),
        compiler_params=_params(("parallel", "parallel")),
    )(hh, hh, conv_w, conv_b.reshape(1, -1))


def _ffn_gate_bwd(hh, conv_w, conv_b, da):
    s = hh.shape[0]
    nrow = s // CONV_ROWS

    def gate_grad(conv, da):
        act, dact = _gelu_pair(conv[:, CONV_COLS:])
        return jnp.concatenate([da * act, da * conv[:, :CONV_COLS] * dact], axis=1)

    def body(m_ref, h_ref, w_ref, b_ref, nx_ref, da_ref, dan_ref, o_ref, dw_ref, db_ref):
        first = pl.program_id(1) == 0
        last = pl.program_id(1) == nrow - 1

        @pl.when(first)
        def _():
            dw_ref[...] = jnp.zeros_like(dw_ref)
            db_ref[...] = jnp.zeros_like(db_ref)

        main_m = m_ref[...]
        main = main_m.astype(F32)
        x1, x2 = _mxu_taps(main_m, h_ref[...].astype(F32), first)
        dconv = gate_grad(_conv_gate(w_ref, b_ref, x2, x1, main), da_ref[...].astype(F32))
        nx = nx_ref[...].astype(F32)
        nx1 = _patch_rows(pltpu.roll(nx, 1, 0), False, [main[CONV_ROWS - 1:]])
        nx2 = _patch_rows(pltpu.roll(nx, 2, 0), False, [main[CONV_ROWS - 2:CONV_ROWS - 1], main[CONV_ROWS - 1:]])
        dnext = gate_grad(_conv_gate(w_ref, b_ref, nx2, nx1, nx), jnp.where(last, 0.0, dan_ref[...].astype(F32)))
        dnext = dnext.astype(MXU_DTYPE).astype(F32)
        ahead = _dot(_shift_matrix(CONV_ROWS, False), dconv.astype(MXU_DTYPE), NN)
        ahead1 = _patch_rows(ahead[:CONV_ROWS], True, [dnext[0:1]])
        ahead2 = _patch_rows(ahead[CONV_ROWS:], True, [dnext[0:1], dnext[1:2]])
        o_ref[...] = (w_ref[2:3, :] * dconv + w_ref[1:2, :] * ahead1 + w_ref[0:1, :] * ahead2).astype(o_ref.dtype)
        for t, tap in enumerate((x2, x1, main)):
            dw_ref[t:t + 1, :] += jnp.sum(dconv * tap, axis=0, keepdims=True)
        db_ref[...] += jnp.sum(dconv, axis=0, keepdims=True)

    nxt = lambda i: jnp.minimum((i + 1) * (CONV_ROWS // HALO), s // HALO - 1)
    return pl.pallas_call(
        body, name="ffn_act_bwd", grid=(D_FF // CONV_COLS, nrow),
        in_specs=_conv_in_specs() + [pl.BlockSpec((HALO, CONV_PAIR), lambda j, i: (nxt(i), j)),
                                     pl.BlockSpec((CONV_ROWS, CONV_COLS), lambda j, i: (i, j)),
                                     pl.BlockSpec((HALO, CONV_COLS), lambda j, i: (nxt(i), j))],
        out_specs=[pl.BlockSpec((CONV_ROWS, CONV_PAIR), lambda j, i: (i, j)),
                   pl.BlockSpec((3, CONV_PAIR), lambda j, i: (0, j)), pl.BlockSpec((1, CONV_PAIR), lambda j, i: (0, j))],
        out_shape=[jax.ShapeDtypeStruct((s, 2 * D_FF), MXU_DTYPE), jax.ShapeDtypeStruct((3, 2 * D_FF), F32),
                   jax.ShapeDtypeStruct((1, 2 * D_FF), F32)],
        compiler_params=_params(("parallel", "arbitrary")),
    )(hh, hh, conv_w, conv_b.reshape(1, -1), hh, da, da)


def _ple_weight_specs(layer):
    return [pl.BlockSpec((None, D_MODEL, D_MODEL), lambda i: (layer, 0, 0)),
            pl.BlockSpec((None, PLE_DIM, D_MODEL), lambda i: (layer, 0, 0))]


def _ple_fwd(h, gain, p, w_gate, w_proj, layer):
    s = h.shape[0]
    tm = 512

    def body(h_ref, g_ref, p_ref, wg_ref, wp_ref, o_ref, xn_ref):
        x = h_ref[...]
        xn = _rms_rows(x, g_ref[...])
        xn_ref[...] = xn
        gate = jax.nn.sigmoid(_dot(xn, wg_ref[...].astype(MXU_DTYPE), NN))
        proj = _dot(p_ref[...].astype(MXU_DTYPE), wp_ref[...].astype(MXU_DTYPE), NN)
        o_ref[...] = x + gate * proj

    row = pl.BlockSpec((tm, D_MODEL), lambda i: (i, 0))
    return pl.pallas_call(
        body, name="ple_fwd", grid=(s // tm,),
        in_specs=[row, pl.BlockSpec((1, D_MODEL), lambda i: (0, 0)), pl.BlockSpec((tm, PLE_DIM), lambda i: (i, 0))]
        + _ple_weight_specs(layer),
        out_specs=[row, row],
        out_shape=[jax.ShapeDtypeStruct((s, D_MODEL), F32), jax.ShapeDtypeStruct((s, D_MODEL), MXU_DTYPE)],
        compiler_params=_params(("parallel",)),
    )(h, gain.reshape(1, D_MODEL), p, w_gate, w_proj)


def _ple_bwd(xn, p, w_gate, w_proj, dh, layer):
    s = xn.shape[0]
    tm = 512

    def body(x_ref, p_ref, wg_ref, wp_ref, dh_ref, dpre_ref, dproj_ref):
        gate = jax.nn.sigmoid(_dot(x_ref[...].astype(MXU_DTYPE), wg_ref[...].astype(MXU_DTYPE), NN))
        proj = _dot(p_ref[...].astype(MXU_DTYPE), wp_ref[...].astype(MXU_DTYPE), NN)
        dh = dh_ref[...]
        dpre_ref[...] = (dh * proj * gate * (1.0 - gate)).astype(dpre_ref.dtype)
        dproj_ref[...] = (dh * gate).astype(dproj_ref.dtype)

    row = pl.BlockSpec((tm, D_MODEL), lambda i: (i, 0))
    return pl.pallas_call(
        body, name="ple_bwd", grid=(s // tm,),
        in_specs=[row, pl.BlockSpec((tm, PLE_DIM), lambda i: (i, 0))] + _ple_weight_specs(layer) + [row],
        out_specs=[row, row],
        out_shape=[jax.ShapeDtypeStruct((s, D_MODEL), MXU_DTYPE)] * 2,
        compiler_params=_params(("parallel",)),
    )(xn, p, w_gate, w_proj, dh)


O_SGU = 3 * ATTN_W
O_SSM = O_SGU + 2 * SGU_W


def _layer_consts(w, i):
    causal = jnp.asarray(np.tril(np.ones((SGU_CHUNK, SGU_CHUNK), np.float32)))
    return {
        "sgu_w_mask": w["sgu_w"][i] * causal,
        "sgu_b_t": w["sgu_b"][i].T,
        "ssm_ops": _ssm_operands(w["ssm_a_re"][i], w["ssm_a_im"][i], w["ssm_log_dt"][i], w["ssm_b_re"][i],
                                 w["ssm_b_im"][i], w["ssm_c_re"][i], w["ssm_c_im"][i]),
    }


def _layer_fwd(h0, p_i, w, i, bias):
    c = _layer_consts(w, i)
    xn1, qkv, zs, us = _in_proj(h0, w["norm_attn_g"][i], w["w_in"], i)
    y_attn, lse = _attn2_fwd(qkv, bias)
    y_sgu = _sgu_fwd(zs, w["sgu_ln_g"][i], w["sgu_ln_b"][i], c["sgu_w_mask"], c["sgu_b_t"])
    y_ssm, entry = _ssm_fwd(us, c["ssm_ops"], w["ssm_d"][i], w["ssm_glu_w"][i], w["ssm_glu_b"][i])
    mix = _mix_fwd(y_attn, y_sgu, y_ssm, w["branch_norm_g"][i])
    h1 = _matmul(mix, w["w_out"], name="out_proj", out_dtype=F32, tm=512, tn=1024, residual=h0, layer=i)
    xn2, hh = _ffn_up(h1, w["norm_ffn_g"][i], w["ffn_w_up"], i)
    act = _ffn_gate_fwd(hh, w["ffn_conv_w"][i], w["ffn_conv_b"][i])
    h2 = _matmul(act, w["ffn_w_down"], name="ffn_down", out_dtype=F32, tm=512, tn=1024, residual=h1, layer=i)
    h3, xn3 = _ple_fwd(h2, w["norm_ple_g"][i], p_i, w["ple_w_gate"], w["ple_w_proj"], i)
    saved = dict(h0=h0, xn1=xn1, qkv=qkv, zs=zs, us=us, y_attn=y_attn, lse=lse, y_sgu=y_sgu, y_ssm=y_ssm,
                 entry=entry, mix=mix, h1=h1, xn2=xn2, hh=hh, act=act, h2=h2, xn3=xn3, consts=c)
    return h3, saved


def _layer_bwd(dh3, sv, p_i, w, i, bias):
    c = sv["consts"]
    g = {}
    dpre, dproj = _ple_bwd(sv["xn3"], p_i, w["ple_w_gate"], w["ple_w_proj"], dh3, i)
    g["ple_w_gate"] = _matmul_tn(sv["xn3"], dpre, name="d_ple_w_gate", tk=1024, tn=1024)
    g["ple_w_proj"] = _matmul_tn(p_i, dproj, name="d_ple_w_proj", tk=256, tn=1024)
    dh2, g["norm_ple_g"] = _matmul_rms_bwd(dpre, w["ple_w_gate"], sv["h2"], w["norm_ple_g"][i], dh3,
                                           name="d_xn_ple", layer=i, tm=512)
    g["ffn_w_down"] = _matmul_tn(sv["act"], dh2, name="d_ffn_w_down", tk=1408, tn=1024)
    dact = _matmul(dh2, w["ffn_w_down"], name="d_ffn_act", out_dtype=MXU_DTYPE, tm=512, tn=1408, trans_b=True, layer=i)
    dhh, g["ffn_conv_w"], g["ffn_conv_b"] = _ffn_gate_bwd(sv["hh"], w["ffn_conv_w"][i], w["ffn_conv_b"][i], dact)
    g["ffn_w_up"] = _matmul_tn(sv["xn2"], dhh, name="d_ffn_w_up", tk=1024, tn=1408)
    dh1, g["norm_ffn_g"] = _matmul_rms_bwd(dhh, w["ffn_w_up"], sv["h1"], w["norm_ffn_g"][i], dh2,
                                           name="d_xn_ffn", layer=i, tm=256)
    g["w_out"] = _matmul_tn(sv["mix"], dh1, name="d_w_out", tk=1024, tn=1024)
    dmix = _matmul(dh1, w["w_out"], name="d_mix", out_dtype=F32, tm=512, tn=1024, trans_b=True, layer=i)
    dy_attn, delta, dy_sgu, dy_ssm, g["branch_norm_g"] = _mix_bwd(sv["y_attn"], sv["y_sgu"], sv["y_ssm"],
                                                                  w["branch_norm_g"][i], dmix)
    dq, dk, dv, ek, ev, dbias = _attn2_bwd(sv["qkv"], bias, sv["lse"], delta, dy_attn)
    dzs, g["sgu_ln_g"], g["sgu_ln_b"], dsw, dsb = _sgu_bwd(sv["zs"], w["sgu_ln_g"][i], w["sgu_ln_b"][i],
                                                          c["sgu_w_mask"], c["sgu_b_t"], dy_sgu)
    causal = jnp.asarray(np.tril(np.ones((SGU_CHUNK, SGU_CHUNK), np.float32)))
    g["sgu_w"] = dsw * causal
    g["sgu_b"] = dsb.T
    dus, dbb, dcm, da, g["ssm_d"], g["ssm_glu_w"], g["ssm_glu_b"] = _ssm_bwd(
        sv["us"], sv["entry"], c["ssm_ops"], w["ssm_d"][i], w["ssm_glu_w"][i], w["ssm_glu_b"][i], dy_ssm)
    dbb5 = dbb.reshape(SSM_G, SSM_C, 2, SSM_G, SSM_N)
    dbbar = jnp.einsum("gcpgn->pgnc", dbb5)
    dcm5 = dcm.reshape(2, SSM_G, SSM_N, SSM_G, SSM_C)
    dcc = jnp.einsum("pgngc->pgcn", dcm5)
    g["ssm_c_re"] = dcc[0]
    g["ssm_c_im"] = -dcc[1]
    da2 = da.reshape(2, SSM_G, SSM_N)
    _, vjp = jax.vjp(_ssm_discretize, w["ssm_a_re"][i], w["ssm_a_im"][i], w["ssm_log_dt"][i],
                     w["ssm_b_re"][i], w["ssm_b_im"][i])
    (g["ssm_a_re"], g["ssm_a_im"], g["ssm_log_dt"], g["ssm_b_re"], g["ssm_b_im"]) = vjp(
        (da2[0], da2[1], dbbar[0], dbbar[1]))
    dz = _attn2_bwd_sum(dq, dk, dv, ek, ev, dzs, dus)
    g["w_in"] = _matmul_tn(sv["xn1"], dz, name="d_w_in", tk=1024, tn=1152)
    dh0, g["norm_attn_g"] = _matmul_rms_bwd(dz, w["w_in"], sv["h0"], w["norm_attn_g"][i], dh1,
                                            name="d_xn_attn", layer=i, tm=512)
    for k in ("norm_ple_g", "norm_ffn_g", "branch_norm_g", "norm_attn_g", "sgu_ln_g", "sgu_ln_b", "ssm_d",
              "ssm_glu_b", "ffn_conv_b"):
        g[k] = g[k].reshape(-1)
    return dh0, g, dbias


def _local_step(x, p, target, w, ff_interleaved=False):
    ff_names = ("ffn_conv_b",) if ff_interleaved else FF_SHARDED + ("ffn_conv_b",)
    w = dict(w)
    for k in ff_names:
        w[k] = _interleave_ff(w[k])
    bias = _bias_build(w["rel_bias"])
    h = x
    saved = []
    for i in range(DEPTH):
        h, sv = _layer_fwd(h, p[i], w, i, bias)
        saved.append(sv)
    loss, dh, dgf = _loss_head(h, w["final_norm_g"], target)
    layer_grads = [None] * DEPTH
    dbias = None
    for i in reversed(range(DEPTH)):
        dh, layer_grads[i], db = _layer_bwd(dh, saved[i], p[i], w, i, bias)
        dbias = db if dbias is None else dbias + db
    grads = {k: jnp.stack([layer_grads[i][k] for i in range(DEPTH)]) for k in layer_grads[0]}
    for k in ff_names:
        grads[k] = _deinterleave_ff(grads[k])
    grads["rel_bias"] = _bias_reduce(dbias)
    grads["final_norm_g"] = dgf.reshape(-1)
    return loss, dh, grads


def _as_rows(a, rows=None):
    size = int(np.prod(a.shape))
    if rows is None:
        rows = -(-size // (16 * PACK_COLS)) * 16
    if size % PACK_COLS:
        a = jnp.pad(a.reshape(-1), (0, (-size) % PACK_COLS))
    a2 = a.reshape(-1, PACK_COLS)
    return jnp.pad(a2, ((0, rows - a2.shape[0]), (0, 0)))


def _shard_shape(name):
    full, ax = BIG_FULL[name]
    shp = [DEPTH] + list(full)
    shp[ax] //= N_CHIPS
    return tuple(shp)


EXACT_NAMES = ("ffn_conv_w",)


def _pack_rows_of(name):
    n = int(np.prod(_shard_shape(name))) * (2 if name in EXACT_NAMES else 1)
    rows = -(-n // PACK_COLS)
    return -(-rows // 16) * 16


def _pack_shards(shards, dtype, exact=False):
    split_words = exact and jnp.dtype(dtype).itemsize == 2
    parts = []
    for n in BIG_NAMES:
        a = shards[n]
        if split_words and n in EXACT_NAMES:
            a = lax.bitcast_convert_type(a.astype(F32), dtype)
        parts.append(_as_rows(a.astype(dtype), _pack_rows_of(n)))
    used = sum(pt.shape[0] for pt in parts)
    parts.append(jnp.zeros((PACK_ROWS - used, PACK_COLS), dtype))
    return jnp.concatenate(parts, axis=0)


def _unpack_shard(flat, name, exact=False):
    off = 0
    for n in BIG_NAMES:
        if n == name:
            break
        off += _pack_rows_of(n)
    shp = _shard_shape(name)
    cnt = int(np.prod(shp))
    if exact and name in EXACT_NAMES and jnp.dtype(flat.dtype).itemsize == 2:
        vec = flat[off:off + _pack_rows_of(name)].reshape(-1)
        return lax.bitcast_convert_type(vec[:2 * cnt].reshape(shp + (2,)), F32)
    if cnt % PACK_COLS == 0:
        return flat[off:off + cnt // PACK_COLS].reshape(shp)
    return flat[off:off + _pack_rows_of(name)].reshape(-1)[:cnt].reshape(shp)


FF_SHARDED = ("ffn_w_up", "ffn_conv_w")
FF_CHIP_ORDER = (0, 2, 1, 3)


def _chip_order(name):
    return FF_CHIP_ORDER if name in FF_SHARDED else tuple(range(N_CHIPS))


def _split_full(full, name):
    _, ax = BIG_FULL[name]
    parts = jnp.split(full, N_CHIPS, axis=ax)
    out = [None] * N_CHIPS
    for j, k in enumerate(_chip_order(name)):
        out[k] = parts[j]
    return out


def _join_shards(shards, name):
    _, ax = BIG_FULL[name]
    return jnp.concatenate([shards[k] for k in _chip_order(name)], axis=ax)


def _small_shapes(w):
    return [(n, w[n].shape) for n in SMALL_NAMES]


def _small_rows(shp):
    return -(-int(np.prod(shp)) // (8 * PACK_COLS)) * 8


def _pack_small(d):
    parts = [_as_rows(d[n].astype(F32), _small_rows(d[n].shape)) for n in SMALL_NAMES]
    used = sum(pt.shape[0] for pt in parts)
    parts.append(jnp.zeros((SMALL_ROWS - used, PACK_COLS), F32))
    return jnp.concatenate(parts, axis=0)


def _unpack_small(flat, shapes):
    out, off = {}, 0
    for n, shp in shapes:
        cnt = int(np.prod(shp))
        rows = _small_rows(shp)
        if cnt % PACK_COLS == 0:
            out[n] = flat[off:off + cnt // PACK_COLS].reshape(shp)
        else:
            out[n] = flat[off:off + rows].reshape(-1)[:cnt].reshape(shp)
        off += rows
    return out


MESH = pl.DeviceIdType.MESH
ANY = pl.BlockSpec(memory_space=pl.ANY)


def _me():
    return lax.axis_index("x"), lax.axis_index("y"), lax.axis_index("c")


def _other_chips(x, y):
    return [(1 - x, y), (x, 1 - y), (1 - x, 1 - y)]


def _gather_weights(wflat):
    def body(w_ref, out_ref, send_sems, recv_sems):
        x, y, c = _me()
        sibling = (x, y, 1 - c)
        chips = _other_chips(x, y)

        def rows(chip, half):
            return out_ref.at[2 * chip[0] + chip[1], pl.ds(half * PACK_HALF, PACK_HALF), :]

        def copy(k, chip, half, to, src=None):
            return pltpu.make_async_remote_copy(
                src_ref=rows(chip, half) if src is None else src, dst_ref=rows(chip, half),
                send_sem=send_sems.at[k], recv_sem=recv_sems.at[k], device_id=to, device_id_type=MESH)

        my_half = w_ref.at[pl.ds(c * PACK_HALF, PACK_HALF), :]
        first = [copy(j, (x, y), c, (*chip, c), src=my_half) for j, chip in enumerate(chips)]
        for cp in first:
            cp.start()
        passed = [copy(3 + j, chip, c, sibling) for j, chip in enumerate(chips)]
        for j, chip in enumerate(chips):
            copy(j, chip, c, (x, y, c)).wait_recv()
            passed[j].start()
        for j, chip in enumerate(chips):
            copy(3 + j, chip, 1 - c, (x, y, c)).wait_recv()
        for cp in first + passed:
            cp.wait_send()

    return pl.pallas_call(
        body, name="gather_weights", in_specs=[ANY], out_specs=ANY,
        out_shape=jax.ShapeDtypeStruct((N_CHIPS, PACK_ROWS, PACK_COLS), wflat.dtype),
        scratch_shapes=[pltpu.SemaphoreType.DMA((6,)), pltpu.SemaphoreType.DMA((6,))],
    )(wflat)


def _fill_own_shard(wall, wflat, chip_idx):
    rows = PACK_ROWS // 8

    def body(idx_ref, w_ref, wall_ref, o_ref):
        del idx_ref, wall_ref
        o_ref[...] = w_ref[...]

    return pl.pallas_call(
        body, name="fill_own_shard",
        grid_spec=pltpu.PrefetchScalarGridSpec(
            num_scalar_prefetch=1, grid=(PACK_ROWS // rows,),
            in_specs=[pl.BlockSpec((rows, PACK_COLS), lambda i, idx: (i, 0)), ANY],
            out_specs=pl.BlockSpec((None, rows, PACK_COLS), lambda i, idx: (idx[0], i, 0))),
        out_shape=jax.ShapeDtypeStruct(wall.shape, wall.dtype),
        input_output_aliases={2: 0},
        compiler_params=_params(("parallel",)),
    )(chip_idx, wflat, wall)


def _exchange_partials(gb, gs):
    def body(gb_ref, gs_ref, half_ref, small_ref, send_sems, recv_sems, local_sem):
        x, y, c = _me()
        me_idx = 4 * x + 2 * y + c
        mine = pltpu.make_async_copy(gs_ref, small_ref.at[me_idx], local_sem)
        mine.start()
        d2d = pltpu.make_async_remote_copy(
            src_ref=gb_ref.at[:, pl.ds((1 - c) * PACK_HALF, PACK_HALF), :], dst_ref=half_ref,
            send_sem=send_sems.at[0], recv_sem=recv_sems.at[0], device_id=(x, y, 1 - c), device_id_type=MESH)
        d2d.start()
        copies = []
        for k in range(1, N_DEV):
            fx, fy, fc = (k >> 2) & 1, (k >> 1) & 1, k & 1
            peer = (x ^ fx, y ^ fy, c ^ fc)
            copies.append(pltpu.make_async_remote_copy(
                src_ref=gs_ref, dst_ref=small_ref.at[me_idx], send_sem=send_sems.at[k], recv_sem=recv_sems.at[k],
                device_id=peer, device_id_type=MESH))
        for cp in copies:
            cp.start()
        for k in range(1, N_DEV):
            fx, fy, fc = (k >> 2) & 1, (k >> 1) & 1, k & 1
            peer_idx = 4 * (x ^ fx) + 2 * (y ^ fy) + (c ^ fc)
            pltpu.make_async_remote_copy(
                src_ref=gs_ref, dst_ref=small_ref.at[peer_idx], send_sem=send_sems.at[k], recv_sem=recv_sems.at[k],
                device_id=(x, y, c), device_id_type=MESH).wait_recv()
        d2d.wait_recv()
        d2d.wait_send()
        for cp in copies:
            cp.wait_send()
        mine.wait()

    return pl.pallas_call(
        body, name="exchange_partials", in_specs=[ANY, pl.BlockSpec(memory_space=pltpu.VMEM)], out_specs=[ANY, ANY],
        out_shape=[jax.ShapeDtypeStruct((N_CHIPS, PACK_HALF, PACK_COLS), gb.dtype),
                   jax.ShapeDtypeStruct((N_DEV, SMALL_ROWS, PACK_COLS), F32)],
        scratch_shapes=[pltpu.SemaphoreType.DMA((N_DEV,)), pltpu.SemaphoreType.DMA((N_DEV,)), pltpu.SemaphoreType.DMA],
    )(gb, gs)


RED_ROWS = 256


def _chip_partials(gb, sib, c_idx):
    nrow = PACK_HALF // RED_ROWS

    def body(c_ref, a_ref, b_ref, o_ref):
        del c_ref
        o_ref[...] = (a_ref[...].astype(F32) + b_ref[...].astype(F32)).astype(o_ref.dtype)

    blk = (1, RED_ROWS, PACK_COLS)
    return pl.pallas_call(
        body, name="chip_partials",
        grid_spec=pltpu.PrefetchScalarGridSpec(
            num_scalar_prefetch=1, grid=(N_CHIPS, nrow),
            in_specs=[pl.BlockSpec(blk, lambda k, i, c: (k, c[0] * nrow + i, 0)),
                      pl.BlockSpec(blk, lambda k, i, c: (k, i, 0))],
            out_specs=pl.BlockSpec(blk, lambda k, i, c: (k, i, 0))),
        out_shape=jax.ShapeDtypeStruct((N_CHIPS, PACK_HALF, PACK_COLS), gb.dtype),
        compiler_params=_params(("parallel", "parallel")),
    )(c_idx, gb, sib)


def _scatter_partials(pc):
    def body(pc_ref, out_ref, send_sems, recv_sems):
        x, y, c = _me()
        chips = _other_chips(x, y)
        copies = [pltpu.make_async_remote_copy(
            src_ref=pc_ref.at[2 * chip[0] + chip[1]], dst_ref=out_ref.at[k],
            send_sem=send_sems.at[k], recv_sem=recv_sems.at[k], device_id=(*chip, c), device_id_type=MESH)
            for k, chip in enumerate(chips)]
        for cp in copies:
            cp.start()
        for cp in copies:
            cp.wait_recv()
        for cp in copies:
            cp.wait_send()

    return pl.pallas_call(
        body, name="scatter_partials", in_specs=[ANY], out_specs=ANY,
        out_shape=jax.ShapeDtypeStruct((3, PACK_HALF, PACK_COLS), pc.dtype),
        scratch_shapes=[pltpu.SemaphoreType.DMA((3,)), pltpu.SemaphoreType.DMA((3,))],
    )(pc)


def _final_half(gb, sib, recv, idx):
    nrow = PACK_HALF // RED_ROWS

    def body(idx_ref, a_ref, b_ref, r_ref, o_ref):
        del idx_ref
        acc = a_ref[0].astype(F32) + b_ref[0].astype(F32)
        for k in range(3):
            acc = acc + r_ref[k].astype(F32)
        o_ref[...] = acc

    return pl.pallas_call(
        body, name="final_half",
        grid_spec=pltpu.PrefetchScalarGridSpec(
            num_scalar_prefetch=1, grid=(nrow,),
            in_specs=[pl.BlockSpec((1, RED_ROWS, PACK_COLS), lambda i, idx: (idx[0], idx[1] * nrow + i, 0)),
                      pl.BlockSpec((1, RED_ROWS, PACK_COLS), lambda i, idx: (idx[0], i, 0)),
                      pl.BlockSpec((3, RED_ROWS, PACK_COLS), lambda i, idx: (0, i, 0))],
            out_specs=pl.BlockSpec((RED_ROWS, PACK_COLS), lambda i, idx: (i, 0))),
        out_shape=jax.ShapeDtypeStruct((PACK_HALF, PACK_COLS), F32),
        compiler_params=_params(("parallel",)),
    )(idx, gb, sib, recv)


def _share_halves(half):
    def body(h_ref, out_ref, send_sem, recv_sem):
        x, y, c = _me()
        cp = pltpu.make_async_remote_copy(src_ref=h_ref, dst_ref=out_ref, send_sem=send_sem, recv_sem=recv_sem,
                                          device_id=(x, y, 1 - c), device_id_type=MESH)
        cp.start()
        cp.wait_recv()
        cp.wait_send()

    return pl.pallas_call(
        body, name="share_halves", in_specs=[ANY], out_specs=ANY,
        out_shape=jax.ShapeDtypeStruct((PACK_HALF, PACK_COLS), F32),
        scratch_shapes=[pltpu.SemaphoreType.DMA, pltpu.SemaphoreType.DMA],
    )(half)


def _sum_small(allsmall):
    def body(a_ref, o_ref):
        acc = a_ref[0]
        for k in range(1, N_DEV):
            acc = acc + a_ref[k]
        o_ref[...] = acc

    tr = 96
    return pl.pallas_call(
        body, name="sum_small", grid=(SMALL_ROWS // tr,),
        in_specs=[pl.BlockSpec((N_DEV, tr, PACK_COLS), lambda i: (0, i, 0))],
        out_specs=pl.BlockSpec((tr, PACK_COLS), lambda i: (i, 0)),
        out_shape=jax.ShapeDtypeStruct((SMALL_ROWS, PACK_COLS), F32),
        compiler_params=_params(("parallel",)),
    )(allsmall)


def _adamw(w, g, m, v, *, name):
    shape = w.shape
    cols = shape[-1]
    as2 = lambda t: t.reshape(-1, cols)
    w2, g2, m2, v2 = as2(w), as2(g), as2(m), as2(v)
    rows = w2.shape[0]
    tr = rows
    if rows * cols * 4 > (1 << 20):
        tr = _tile(rows, max(8, (1 << 20) // (cols * 4) // 8 * 8), 8)

    def body(w_ref, g_ref, m_ref, v_ref, d_ref, mo_ref, vo_ref):
        gg = g_ref[...]
        mn = ADAM_B1 * m_ref[...] + (1.0 - ADAM_B1) * gg
        vn = ADAM_B2 * v_ref[...] + (1.0 - ADAM_B2) * (gg * gg)
        m_hat = mn / (1.0 - ADAM_B1 ** ADAM_STEP)
        v_hat = vn / (1.0 - ADAM_B2 ** ADAM_STEP)
        d_ref[...] = -ADAM_LR * (m_hat / (jnp.sqrt(v_hat) + ADAM_EPS) + ADAM_WD * w_ref[...])
        mo_ref[...] = mn
        vo_ref[...] = vn

    blk = pl.BlockSpec((tr, cols), lambda i: (i, 0))
    outs = pl.pallas_call(
        body, name=name, grid=(rows // tr,), in_specs=[blk] * 4, out_specs=[blk] * 3,
        out_shape=[jax.ShapeDtypeStruct((rows, cols), F32)] * 3,
        compiler_params=_params(("parallel",)),
    )(w2, g2, m2, v2)
    return tuple(t.reshape(shape) for t in outs)


def _adamw_many(ws, gs, ms, vs):
    n = len(ws)

    def body(*refs):
        for t in range(n):
            w_ref, g_ref, m_ref, v_ref = refs[t], refs[n + t], refs[2 * n + t], refs[3 * n + t]
            d_ref, mo_ref, vo_ref = refs[4 * n + t], refs[5 * n + t], refs[6 * n + t]
            gg = g_ref[...]
            mn = ADAM_B1 * m_ref[...] + (1.0 - ADAM_B1) * gg
            vn = ADAM_B2 * v_ref[...] + (1.0 - ADAM_B2) * (gg * gg)
            m_hat = mn / (1.0 - ADAM_B1 ** ADAM_STEP)
            v_hat = vn / (1.0 - ADAM_B2 ** ADAM_STEP)
            d_ref[...] = -ADAM_LR * (m_hat / (jnp.sqrt(v_hat) + ADAM_EPS) + ADAM_WD * w_ref[...])
            mo_ref[...] = mn
            vo_ref[...] = vn

    vmem = pl.BlockSpec(memory_space=pltpu.VMEM)
    outs = pl.pallas_call(
        body, name="adamw_small", in_specs=[vmem] * (4 * n), out_specs=[vmem] * (3 * n),
        out_shape=[jax.ShapeDtypeStruct(w.shape, F32) for w in ws] * 3,
        compiler_params=pltpu.CompilerParams(vmem_limit_bytes=VMEM_LIMIT_BYTES),
    )(*ws, *gs, *ms, *vs)
    return outs[:n], outs[n:2 * n], outs[2 * n:]


def kernel(x, p, rel_bias, norm_attn_g, w_in, sgu_ln_g, sgu_ln_b, sgu_w, sgu_b, ssm_a_re, ssm_a_im, ssm_log_dt, ssm_b_re, ssm_b_im, ssm_c_re, ssm_c_im, ssm_d, ssm_glu_w, ssm_glu_b, branch_norm_g, w_out, norm_ffn_g, ffn_w_up, ffn_conv_w, ffn_conv_b, ffn_w_down, norm_ple_g, ple_w_gate, ple_w_proj, final_norm_g, loss_target, m_rel_bias, m_norm_attn_g, m_w_in, m_sgu_ln_g, m_sgu_ln_b, m_sgu_w, m_sgu_b, m_ssm_a_re, m_ssm_a_im, m_ssm_log_dt, m_ssm_b_re, m_ssm_b_im, m_ssm_c_re, m_ssm_c_im, m_ssm_d, m_ssm_glu_w, m_ssm_glu_b, m_branch_norm_g, m_w_out, m_norm_ffn_g, m_ffn_w_up, m_ffn_conv_w, m_ffn_conv_b, m_ffn_w_down, m_norm_ple_g, m_ple_w_gate, m_ple_w_proj, m_final_norm_g, v_rel_bias, v_norm_attn_g, v_w_in, v_sgu_ln_g, v_sgu_ln_b, v_sgu_w, v_sgu_b, v_ssm_a_re, v_ssm_a_im, v_ssm_log_dt, v_ssm_b_re, v_ssm_b_im, v_ssm_c_re, v_ssm_c_im, v_ssm_d, v_ssm_glu_w, v_ssm_glu_b, v_branch_norm_g, v_w_out, v_norm_ffn_g, v_ffn_w_up, v_ffn_conv_w, v_ffn_conv_b, v_ffn_w_down, v_norm_ple_g, v_ple_w_gate, v_ple_w_proj, v_final_norm_g):
    args = dict(locals())
    wts = {n: args[n] for n in WEIGHT_NAMES}
    mom_m = {n: args["m_" + n] for n in WEIGHT_NAMES}
    mom_v = {n: args["v_" + n] for n in WEIGHT_NAMES}

    xi, yi, ci = _me()
    wflat = _pack_shards({n: wts[n] for n in BIG_NAMES}, MXU_DTYPE, exact=True)
    wall = _fill_own_shard(_gather_weights(wflat), wflat, jnp.stack([2 * xi + yi]).astype(jnp.int32))
    full = dict(wts)
    for n in BIG_NAMES:
        full[n] = _join_shards([_unpack_shard(wall[k], n, exact=True) for k in range(N_CHIPS)], n)
    full["ffn_conv_w"] = full["ffn_conv_w"].astype(F32)

    loss, dx, grads = _local_step(x[0], p[:, 0], loss_target[0], full, ff_interleaved=True)

    xi, yi, ci = _me()
    stacked = {n: _split_full(grads[n], n) for n in BIG_NAMES}
    gb = jnp.stack([_pack_shards({n: stacked[n][k] for n in BIG_NAMES}, MXU_DTYPE) for k in range(N_CHIPS)])
    gs = _pack_small(grads).at[SMALL_ROWS - 1, 0].set(loss[0, 0])
    sib, allsmall = _exchange_partials(gb, gs)
    pc = _chip_partials(gb, sib, jnp.stack([ci]).astype(jnp.int32))
    recv = _scatter_partials(pc)
    half = _final_half(gb, sib, recv, jnp.stack([2 * xi + yi, ci]).astype(jnp.int32))
    other = _share_halves(half)
    gflat = jnp.concatenate([jnp.where(ci == 0, half, other), jnp.where(ci == 0, other, half)], axis=0)
    small_sum = _sum_small(allsmall)
    loss = small_sum[SMALL_ROWS - 1, 0]
    gsmall = _unpack_small(small_sum, _small_shapes(wts))

    g_out, d_out, m_out, v_out = {}, {}, {}, {}
    for n in BIG_NAMES:
        g_out[n] = _unpack_shard(gflat, n)
        d_out[n], m_out[n], v_out[n] = _adamw(wts[n], g_out[n], mom_m[n], mom_v[n], name="adamw_" + n)
    d_sm, m_sm, v_sm = _adamw_many([wts[n] for n in SMALL_NAMES], [gsmall[n] for n in SMALL_NAMES],
                                   [mom_m[n] for n in SMALL_NAMES], [mom_v[n] for n in SMALL_NAMES])
    for t, n in enumerate(SMALL_NAMES):
        g_out[n], d_out[n], m_out[n], v_out[n] = gsmall[n], d_sm[t], m_sm[t], v_sm[t]

    return (loss, dx[None], *[g_out[n] for n in WEIGHT_NAMES], *[d_out[n] for n in WEIGHT_NAMES],
            *[m_out[n] for n in WEIGHT_NAMES], *[v_out[n] for n in WEIGHT_NAMES])
```

```python
import functools
import math

import numpy as np
import jax
import jax.numpy as jnp
from jax import lax
from jax.experimental import pallas as pl
from jax.experimental.pallas import tpu as pltpu

F32 = jnp.float32
MXU_DTYPE = jnp.bfloat16
VMEM_LIMIT_BYTES = 52 * 1024 * 1024

D_MODEL = 1024
DEPTH = 2
PLE_DIM = 256
HEAD_DIM = 64
N_HEADS = 8
ATTN_W = 512
QBLK = 128
BRANCH_DIL = (1, 4, 16)
N_BUCKETS = 32
REL_MAX_DIST = 2048
SGU_W = 256
SGU_G = 4
SGU_GW = 64
SGU_CHUNK = 128
SSM_W = 256
SSM_G = 16
SSM_C = 16
SSM_N = 64
NSTATE = SSM_G * SSM_N
D_FF = 2816
EPS = 1e-6
NEG_INF = -1e30
ATTN_SCALE = HEAD_DIM ** -0.5

ADAM_LR = 0.001
ADAM_B1 = 0.9
ADAM_B2 = 0.999
ADAM_EPS = 1e-08
ADAM_WD = 0.01
ADAM_STEP = 10

SSM_NSEG = 8
SSM_TSEG = 64
SSM_TB = SSM_NSEG * SSM_TSEG
SSM_LANE_CHUNK = 512

N_CHIPS = 4
N_DEV = 8

BIG_NAMES = ("w_in", "ssm_glu_w", "w_out", "ffn_w_up", "ffn_conv_w", "ffn_w_down", "ple_w_gate", "ple_w_proj")
BIG_FULL = {
    "w_in": ((D_MODEL, 2304), 2),
    "ssm_glu_w": ((SSM_W, SSM_W), 1),
    "w_out": ((D_MODEL, D_MODEL), 1),
    "ffn_w_up": ((D_MODEL, 2 * D_FF), 2),
    "ffn_conv_w": ((3, 2 * D_FF), 2),
    "ffn_w_down": ((D_FF, D_MODEL), 1),
    "ple_w_gate": ((D_MODEL, D_MODEL), 1),
    "ple_w_proj": ((PLE_DIM, D_MODEL), 2),
}
PACK_COLS = 1024
PACK_ROWS = 6656
PACK_HALF = PACK_ROWS // 2

SMALL_NAMES = ("rel_bias", "norm_attn_g", "sgu_ln_g", "sgu_ln_b", "sgu_w", "sgu_b", "ssm_a_re", "ssm_a_im",
               "ssm_log_dt", "ssm_b_re", "ssm_b_im", "ssm_c_re", "ssm_c_im", "ssm_d", "ssm_glu_b",
               "branch_norm_g", "norm_ffn_g", "ffn_conv_b", "norm_ple_g", "final_norm_g")
SMALL_ROWS = 384

WEIGHT_NAMES = ("rel_bias", "norm_attn_g", "w_in", "sgu_ln_g", "sgu_ln_b", "sgu_w", "sgu_b", "ssm_a_re", "ssm_a_im",
                "ssm_log_dt", "ssm_b_re", "ssm_b_im", "ssm_c_re", "ssm_c_im", "ssm_d", "ssm_glu_w", "ssm_glu_b",
                "branch_norm_g", "w_out", "norm_ffn_g", "ffn_w_up", "ffn_conv_w", "ffn_conv_b", "ffn_w_down",
                "norm_ple_g", "ple_w_gate", "ple_w_proj", "final_norm_g")


def _params(sem):
    return pltpu.CompilerParams(dimension_semantics=sem, vmem_limit_bytes=VMEM_LIMIT_BYTES)


def _tile(n, cap, mult=128):
    if n <= cap:
        return n
    best = None
    for t in range(mult, cap + 1, mult):
        if n % t == 0:
            best = t
    assert best is not None, (n, cap)
    return best


def _gelu(x):
    return 0.5 * x * (1.0 + jnp.tanh(0.7978845608028654 * (x + 0.044715 * x * x * x)))


def _gelu_pair(x):
    x2 = x * x
    t = jnp.tanh(0.7978845608028654 * x * (1.0 + 0.044715 * x2))
    half = 0.5 * (1.0 + t)
    return x * half, half + 0.5 * x * (1.0 - t * t) * (0.7978845608028654 + 3.0 * 0.044715 * 0.7978845608028654 * x2)


def _dot(a, b, dims):
    return lax.dot_general(a, b, (dims, ((), ())), preferred_element_type=F32)


def _dotf(a, b, dims):
    return _dot(a.astype(MXU_DTYPE), b.astype(MXU_DTYPE), dims)


NN = ((1,), (0,))
NT = ((1,), (1,))
TN = ((0,), (0,))


def _matmul(a, b, *, name, out_dtype, tm, tn, trans_b=False, residual=None, layer=None):
    m, k = a.shape
    n = b.shape[-2] if trans_b else b.shape[-1]
    tm = _tile(m, tm, 8)
    tn = _tile(n, tn)
    dims = NT if trans_b else NN
    lead = () if layer is None else (None,)
    lidx = () if layer is None else (layer,)

    def body(*refs):
        if residual is None:
            a_ref, b_ref, o_ref = refs
        else:
            a_ref, b_ref, r_ref, o_ref = refs
        acc = _dot(a_ref[...].astype(MXU_DTYPE), b_ref[...].astype(MXU_DTYPE), dims)
        if residual is not None:
            acc = acc + r_ref[...]
        o_ref[...] = acc.astype(o_ref.dtype)

    b_spec = (pl.BlockSpec(lead + (tn, k), lambda i, j: lidx + (j, 0)) if trans_b
              else pl.BlockSpec(lead + (k, tn), lambda i, j: lidx + (0, j)))
    in_specs = [pl.BlockSpec((tm, k), lambda i, j: (i, 0)), b_spec]
    args = [a, b]
    if residual is not None:
        in_specs.append(pl.BlockSpec((tm, tn), lambda i, j: (i, j)))
        args.append(residual)
    return pl.pallas_call(
        body, name=name, grid=(m // tm, n // tn), in_specs=in_specs,
        out_specs=pl.BlockSpec((tm, tn), lambda i, j: (i, j)),
        out_shape=jax.ShapeDtypeStruct((m, n), out_dtype),
        compiler_params=_params(("parallel", "parallel")),
    )(*args)


def _matmul_tn(a, g, *, name, tk, tn, tm=1024):
    m, k = a.shape
    n = g.shape[1]
    tk = _tile(k, tk)
    tn = _tile(n, tn)
    tm = _tile(m, tm, 8)

    def body(a_ref, g_ref, o_ref):
        @pl.when(pl.program_id(2) == 0)
        def _():
            o_ref[...] = jnp.zeros_like(o_ref)

        o_ref[...] += _dot(a_ref[...].astype(MXU_DTYPE), g_ref[...].astype(MXU_DTYPE), TN)

    return pl.pallas_call(
        body, name=name, grid=(k // tk, n // tn, m // tm),
        in_specs=[pl.BlockSpec((tm, tk), lambda i, j, s: (s, i)),
                  pl.BlockSpec((tm, tn), lambda i, j, s: (s, j))],
        out_specs=pl.BlockSpec((tk, tn), lambda i, j, s: (i, j)),
        out_shape=jax.ShapeDtypeStruct((k, n), F32),
        compiler_params=_params(("parallel", "parallel", "arbitrary")),
    )(a, g)


ROWS = 512


def _matmul_rms_bwd(a, b, h, g, dres, *, name, layer, tm):
    s, k = a.shape
    d = b.shape[-2]

    def body(a_ref, b_ref, h_ref, g_ref, dres_ref, dh_ref, dg_ref):
        @pl.when(pl.program_id(0) == 0)
        def _():
            dg_ref[...] = jnp.zeros_like(dg_ref)

        dxn = _dot(a_ref[...].astype(MXU_DTYPE), b_ref[...].astype(MXU_DTYPE), NT)
        x = h_ref[...]
        r = lax.rsqrt(jnp.mean(x * x, axis=-1, keepdims=True) + EPS)
        xhat = x * r
        dg_ref[...] += jnp.sum(dxn * xhat, axis=0, keepdims=True)
        dxh = dxn * g_ref[...]
        dh_ref[...] = dres_ref[...] + r * (dxh - xhat * jnp.mean(dxh * xhat, axis=-1, keepdims=True))

    row = pl.BlockSpec((tm, d), lambda i: (i, 0))
    vec = pl.BlockSpec((1, d), lambda i: (0, 0))
    return pl.pallas_call(
        body, name=name, grid=(s // tm,),
        in_specs=[pl.BlockSpec((tm, k), lambda i: (i, 0)), pl.BlockSpec((None, d, k), lambda i: (layer, 0, 0)),
                  row, vec, row],
        out_specs=[row, vec],
        out_shape=[jax.ShapeDtypeStruct((s, d), F32), jax.ShapeDtypeStruct((1, d), F32)],
        compiler_params=_params(("arbitrary",)),
    )(a, b, h, g.reshape(1, d), dres)


def _loss_head(h, g, target):
    s, d = h.shape

    def body(h_ref, g_ref, t_ref, loss_ref, dh_ref, dg_ref):
        @pl.when(pl.program_id(0) == 0)
        def _():
            loss_ref[...] = jnp.zeros_like(loss_ref)
            dg_ref[...] = jnp.zeros_like(dg_ref)

        x = h_ref[...]
        r = lax.rsqrt(jnp.mean(x * x, axis=-1, keepdims=True) + EPS)
        xhat = x * r
        err = xhat * g_ref[...] - t_ref[...]
        loss_ref[...] += 0.5 * jnp.sum(jnp.mean(err * err, axis=-1, keepdims=True), axis=0, keepdims=True)
        dy = err / d
        dg_ref[...] += jnp.sum(dy * xhat, axis=0, keepdims=True)
        dxh = dy * g_ref[...]
        dh_ref[...] = r * (dxh - xhat * jnp.mean(dxh * xhat, axis=-1, keepdims=True))

    row = pl.BlockSpec((ROWS, d), lambda i: (i, 0))
    vec = pl.BlockSpec((1, d), lambda i: (0, 0))
    one = pl.BlockSpec((1, 1), lambda i: (0, 0))
    return pl.pallas_call(
        body, name="loss_head", grid=(s // ROWS,), in_specs=[row, vec, row], out_specs=[one, row, vec],
        out_shape=[jax.ShapeDtypeStruct((1, 1), F32), jax.ShapeDtypeStruct((s, d), F32),
                   jax.ShapeDtypeStruct((1, d), F32)],
        compiler_params=_params(("arbitrary",)),
    )(h, g.reshape(1, d), target)


def _t5_bucket(dist):
    max_exact = N_BUCKETS // 2
    dd = np.maximum(dist, 0)
    large = max_exact + (np.log(np.maximum(dd, 1) / max_exact) / np.log(REL_MAX_DIST / max_exact)
                         * (N_BUCKETS - max_exact)).astype(np.int32)
    large = np.minimum(large, N_BUCKETS - 1)
    return np.where(dd < max_exact, dd, large).astype(np.int32)


def _bucket_table():
    qq = np.arange(QBLK)[:, None]
    kk = np.arange(QBLK)[None, :]
    out = np.zeros((len(BRANCH_DIL), 2, QBLK, QBLK), np.int32)
    for b, dil in enumerate(BRANCH_DIL):
        out[b, 0] = _t5_bucket((qq - kk + QBLK) * dil)
        out[b, 1] = _t5_bucket((qq - kk) * dil)
    return out


BIAS_TILE = 2 * QBLK


def _bias_build(rel_bias):
    idx = jnp.asarray(_bucket_table())

    def body(idx_ref, rb_ref, o_ref):
        ch = pl.program_id(1)
        row = lax.broadcasted_iota(jnp.int32, (QBLK, QBLK), 0)
        col = lax.broadcasted_iota(jnp.int32, (QBLK, QBLK), 1)
        for part in range(2):
            ids = idx_ref[0, 1 - part]
            valid = (col <= row) if part == 0 else (col >= row)
            for h in range(2):
                acc = jnp.zeros((QBLK, QBLK), F32)
                for b in range(N_BUCKETS):
                    acc = jnp.where(ids == b, rb_ref[b, 2 * ch + h], acc)
                o_ref[0, 0, QBLK * h:QBLK * (h + 1), QBLK * part:QBLK * (part + 1)] = jnp.where(valid, acc, NEG_INF)

    return pl.pallas_call(
        body, name="attn_bias_build", grid=(len(BRANCH_DIL), N_HEADS // 2),
        in_specs=[pl.BlockSpec((1, 2, QBLK, QBLK), lambda b, c: (b, 0, 0, 0)),
                  pl.BlockSpec(memory_space=pltpu.SMEM)],
        out_specs=pl.BlockSpec((1, 1, BIAS_TILE, BIAS_TILE), lambda b, c: (b, c, 0, 0)),
        out_shape=jax.ShapeDtypeStruct((len(BRANCH_DIL), N_HEADS // 2, BIAS_TILE, BIAS_TILE), F32),
        compiler_params=_params(("parallel", "parallel")),
    )(idx, rel_bias)


def _bias_reduce(dbias):
    idx = jnp.asarray(_bucket_table())
    nb = len(BRANCH_DIL)

    def body(idx_ref, d_ref, o_ref):
        def per_bucket(b, carry):
            for h in range(N_HEADS):
                tot = jnp.zeros((), F32)
                for br in range(nb):
                    for part in range(2):
                        tile = d_ref[br, h // 2, QBLK * (h % 2):QBLK * (h % 2 + 1), QBLK * part:QBLK * (part + 1)]
                        tot = tot + jnp.sum(jnp.where(idx_ref[br, 1 - part] == b, tile, 0.0))
                o_ref[b, h] = tot
            return carry

        lax.fori_loop(0, N_BUCKETS, per_bucket, 0)

    return pl.pallas_call(
        body, name="attn_bias_reduce",
        in_specs=[pl.BlockSpec(memory_space=pltpu.VMEM), pl.BlockSpec(memory_space=pltpu.VMEM)],
        out_specs=pl.BlockSpec(memory_space=pltpu.SMEM),
        out_shape=jax.ShapeDtypeStruct((N_BUCKETS, N_HEADS), F32),
        compiler_params=pltpu.CompilerParams(vmem_limit_bytes=VMEM_LIMIT_BYTES),
    )(idx, dbias)


ATTN_IO_DTYPE = F32
ABLK = 2048
N_CHUNK = ATTN_W // 128


def _rows(start, dil):
    if dil > 1:
        return pl.ds(start, QBLK, stride=dil)
    return pl.ds(pl.multiple_of(start, QBLK), QBLK)


def _low_head():
    return lax.broadcasted_iota(jnp.int32, (QBLK, 128), 1) < HEAD_DIM


def _head_split(t):
    low = _low_head()
    zero = jnp.zeros_like(t)
    return jnp.where(low, t, zero), jnp.where(low, zero, t)


def _tile_bias(b_ref, branch, first):
    bias = b_ref[branch]
    if first is None:
        return bias
    col = lax.broadcasted_iota(jnp.int32, (BIAS_TILE, BIAS_TILE), 1)
    return jnp.where(jnp.logical_and(first, col >= QBLK), NEG_INF, bias)


def _loop(n, fn):
    if n == 1:
        fn(jnp.int32(0), 0)
    elif n > 1:
        lax.fori_loop(0, n, fn, 0, unroll=4)


def _for_each_tile(tile, c):
    for branch, dil in enumerate(BRANCH_DIL):
        span = QBLK * dil

        def edge(r, carry, branch=branch, span=span):
            tile(branch, r, False, ABLK - span + r, c == 0)
            return carry

        def inner(t, carry, branch=branch, span=span, dil=dil):
            start = (1 + t // dil) * span + t % dil
            tile(branch, start, True, start - span, None)
            return carry

        _loop(dil, edge)
        _loop((ABLK // span - 1) * dil, inner)


def _attn_chunk_specs(nb):
    blk = (None, ABLK, 128)
    prev = lambda c: jnp.maximum(c - 1, 0)
    return [pl.BlockSpec(blk, lambda ch, c: (ch, c, 0)),
            pl.BlockSpec(blk, lambda ch, c: (N_CHUNK + ch, c, 0)),
            pl.BlockSpec(blk, lambda ch, c: (2 * N_CHUNK + ch, c, 0)),
            pl.BlockSpec(blk, lambda ch, c: (N_CHUNK + ch, prev(c), 0)),
            pl.BlockSpec(blk, lambda ch, c: (2 * N_CHUNK + ch, prev(c), 0)),
            pl.BlockSpec((len(BRANCH_DIL), None, BIAS_TILE, BIAS_TILE), lambda ch, c: (0, ch, 0, 0))]


def _rms_rows(x, g):
    r = lax.rsqrt(jnp.mean(x * x, axis=-1, keepdims=True) + EPS)
    return (x * r * g).astype(MXU_DTYPE)


def _in_proj(h, gain, w_in, layer):
    s, k = h.shape
    tm = 512
    nch = O_SGU // 128

    def body(h_ref, g_ref, w_ref, xn_ref, qkv_ref, zs_ref, us_ref):
        xn = _rms_rows(h_ref[...], g_ref[...])
        xn_ref[...] = xn
        acc = _dot(xn, w_ref[...].astype(MXU_DTYPE), NN)
        for j in range(nch):
            blk = acc[:, 128 * j:128 * (j + 1)]
            if j < N_CHUNK:
                blk = blk * ATTN_SCALE
            qkv_ref[j] = blk.astype(qkv_ref.dtype)
        zs_ref[...] = acc[:, O_SGU:O_SSM]
        us_ref[...] = acc[:, O_SSM:]

    n = w_in.shape[-1]
    return pl.pallas_call(
        body, name="in_proj", grid=(s // tm,),
        in_specs=[pl.BlockSpec((tm, k), lambda i: (i, 0)), pl.BlockSpec((1, k), lambda i: (0, 0)),
                  pl.BlockSpec((None, k, n), lambda i: (layer, 0, 0))],
        out_specs=[pl.BlockSpec((tm, k), lambda i: (i, 0)), pl.BlockSpec((nch, tm, 128), lambda i: (0, i, 0)),
                   pl.BlockSpec((tm, O_SSM - O_SGU), lambda i: (i, 0)), pl.BlockSpec((tm, n - O_SSM), lambda i: (i, 0))],
        out_shape=[jax.ShapeDtypeStruct((s, k), MXU_DTYPE), jax.ShapeDtypeStruct((nch, s, 128), ATTN_IO_DTYPE),
                   jax.ShapeDtypeStruct((s, O_SSM - O_SGU), F32), jax.ShapeDtypeStruct((s, n - O_SSM), F32)],
        compiler_params=_params(("parallel",)),
    )(h, gain.reshape(1, k), w_in)


def _ffn_up(h, gain, w_up, layer):
    s, k = h.shape
    n = w_up.shape[-1]
    tm, tn = 1024, CONV_COLS

    def body(h_ref, g_ref, w_ref, xn_ref, o_ref):
        @pl.when(pl.program_id(1) == 0)
        def _():
            xn_ref[...] = _rms_rows(h_ref[...], g_ref[...])

        o_ref[...] = _dot(xn_ref[...], w_ref[...].astype(MXU_DTYPE), NN).astype(o_ref.dtype)

    return pl.pallas_call(
        body, name="ffn_up", grid=(s // tm, n // tn),
        in_specs=[pl.BlockSpec((tm, k), lambda i, j: (i, 0)), pl.BlockSpec((1, k), lambda i, j: (0, 0)),
                  pl.BlockSpec((None, k, tn), lambda i, j: (layer, 0, j))],
        out_specs=[pl.BlockSpec((tm, k), lambda i, j: (i, 0)), pl.BlockSpec((tm, tn), lambda i, j: (i, j))],
        out_shape=[jax.ShapeDtypeStruct((s, k), MXU_DTYPE), jax.ShapeDtypeStruct((s, n), MXU_DTYPE)],
        compiler_params=_params(("parallel", "arbitrary")),
    )(h, gain.reshape(1, k), w_up)


def _attn2_fwd(qkv_c, bias):
    s = qkv_c.shape[1]
    nb = s // ABLK
    last = len(BRANCH_DIL) - 1

    def body(q_ref, kc_ref, vc_ref, kp_ref, vp_ref, b_ref, o_ref, l_ref, acc_s, m_s, l_s):
        low = _low_head()
        e_st = jnp.concatenate(_head_split(jnp.ones((QBLK, 128), MXU_DTYPE)) * 2, axis=0)

        def tile(branch, start, prev_in_block, pstart, first):
            dil = BRANCH_DIL[branch]
            rq, rp = _rows(start, dil), _rows(pstart, dil)
            k_ref, v_ref = (kc_ref, vc_ref) if prev_in_block else (kp_ref, vp_ref)
            q_st = jnp.concatenate(_head_split(q_ref[rq, :].astype(MXU_DTYPE)), axis=0)
            k_st = jnp.concatenate([kc_ref[rq, :].astype(MXU_DTYPE), k_ref[rp, :].astype(MXU_DTYPE)], axis=0)
            v_st = jnp.concatenate(_head_split(vc_ref[rq, :].astype(MXU_DTYPE))
                                   + _head_split(v_ref[rp, :].astype(MXU_DTYPE)), axis=0)
            sc = _dot(q_st, k_st, NT) + _tile_bias(b_ref, branch, first)
            m_new = jnp.max(sc, axis=-1, keepdims=True)
            if branch > 0:
                m_old2 = m_s[rq, :]
                m_old = jnp.concatenate([m_old2[:, 0:1], m_old2[:, HEAD_DIM:HEAD_DIM + 1]], axis=0)
                m_new = jnp.maximum(m_old, m_new)
                alpha = jnp.exp(m_old - m_new)
            p = jnp.exp(sc - m_new).astype(MXU_DTYPE)
            lhs = jnp.concatenate([p[:QBLK, :QBLK], p[QBLK:, :QBLK], p[:QBLK, QBLK:], p[QBLK:, QBLK:]], axis=1)
            acc2 = _dot(lhs, v_st, NN)
            sum2 = _dot(lhs, e_st, NN)
            m2 = jnp.where(low, m_new[:QBLK], m_new[QBLK:])
            if branch > 0:
                a2 = jnp.where(low, alpha[:QBLK], alpha[QBLK:])
                acc2 = acc2 + a2 * acc_s[rq, :]
                sum2 = sum2 + a2 * l_s[rq, :]
            if branch == last:
                o_ref[rq, :] = acc2 / sum2
                l_ref[rq, :] = m2 + jnp.log(sum2)
            else:
                acc_s[rq, :] = acc2
                m_s[rq, :] = m2
                l_s[rq, :] = sum2

        _for_each_tile(tile, pl.program_id(1))

    out_spec = pl.BlockSpec((None, ABLK, 128), lambda ch, c: (ch, c, 0))
    return pl.pallas_call(
        body, name="attn_fwd", grid=(N_CHUNK, nb), in_specs=_attn_chunk_specs(nb),
        out_specs=[out_spec, out_spec],
        out_shape=[jax.ShapeDtypeStruct((N_CHUNK, s, 128), F32)] * 2,
        scratch_shapes=[pltpu.VMEM((ABLK, 128), F32)] * 3,
        compiler_params=_params(("parallel", "arbitrary")),
    )(qkv_c, qkv_c, qkv_c, qkv_c, qkv_c, bias)


def _attn2_bwd(qkv_c, bias, lse_c, delta_c, do_c):
    s = qkv_c.shape[1]
    nb = s // ABLK
    nbr = len(BRANCH_DIL)

    def body(q_ref, kc_ref, vc_ref, kp_ref, vp_ref, b_ref, l_ref, dl_ref, do_ref,
             dq_ref, dk_ref, dv_ref, *rest):
        ek_refs, ev_refs, db_ref = rest[:nbr], rest[nbr:2 * nbr], rest[2 * nbr]
        c = pl.program_id(1)

        @pl.when(c == 0)
        def _():
            db_ref[...] = jnp.zeros_like(db_ref)

        for r in (dq_ref, dk_ref, dv_ref) + tuple(ek_refs) + tuple(ev_refs):
            r[...] = jnp.zeros_like(r)

        def tile(branch, start, prev_in_block, pstart, first):
            dil = BRANCH_DIL[branch]
            rq, rp = _rows(start, dil), _rows(pstart, dil)
            k_ref, v_ref = (kc_ref, vc_ref) if prev_in_block else (kp_ref, vp_ref)
            kc2 = kc_ref[rq, :].astype(MXU_DTYPE)
            kp2 = k_ref[rp, :].astype(MXU_DTYPE)
            q_st = jnp.concatenate(_head_split(q_ref[rq, :].astype(MXU_DTYPE)), axis=0)
            do_st = jnp.concatenate(_head_split(do_ref[rq, :].astype(MXU_DTYPE)), axis=0)
            k_st = jnp.concatenate([kc2, kp2], axis=0)
            v_st = jnp.concatenate([vc_ref[rq, :].astype(MXU_DTYPE), v_ref[rp, :].astype(MXU_DTYPE)], axis=0)
            kh_st = jnp.concatenate(_head_split(kc2) + _head_split(kp2), axis=0)
            lse2 = l_ref[rq, :]
            del2 = dl_ref[rq, :]
            lse_st = jnp.concatenate([lse2[:, 0:1], lse2[:, HEAD_DIM:HEAD_DIM + 1]], axis=0)
            del_st = jnp.concatenate([del2[:, 0:1], del2[:, HEAD_DIM:HEAD_DIM + 1]], axis=0)
            p = jnp.exp(_dot(q_st, k_st, NT) + _tile_bias(b_ref, branch, first) - lse_st)
            ds = p * (_dot(do_st, v_st, NT) - del_st)
            db_ref[branch] += ds
            ds = ds.astype(MXU_DTYPE)
            p = p.astype(MXU_DTYPE)
            lhs = jnp.concatenate([ds[:QBLK, :QBLK], ds[QBLK:, :QBLK], ds[:QBLK, QBLK:], ds[QBLK:, QBLK:]], axis=1)
            dk_st = _dot(ds, q_st, TN)
            dv_st = _dot(p, do_st, TN)
            dq_ref[rq, :] += _dot(lhs, kh_st, NN)
            dk_ref[rq, :] += dk_st[:QBLK]
            dv_ref[rq, :] += dv_st[:QBLK]
            if prev_in_block:
                dk_ref[rp, :] += dk_st[QBLK:]
                dv_ref[rp, :] += dv_st[QBLK:]
            else:
                ek_refs[branch][rq, :] = dk_st[QBLK:]
                ev_refs[branch][rq, :] = dv_st[QBLK:]

        _for_each_tile(tile, c)

    blk = pl.BlockSpec((None, ABLK, 128), lambda ch, c: (ch, c, 0))
    edge_specs = [pl.BlockSpec((None, QBLK * dil, 128), lambda ch, c: (ch, c, 0)) for dil in BRANCH_DIL]
    edge_shapes = [jax.ShapeDtypeStruct((N_CHUNK, nb * QBLK * dil, 128), F32) for dil in BRANCH_DIL]
    outs = pl.pallas_call(
        body, name="attn_bwd", grid=(N_CHUNK, nb), in_specs=_attn_chunk_specs(nb) + [blk, blk, blk],
        out_specs=[blk] * 3 + edge_specs * 2
        + [pl.BlockSpec((nbr, None, BIAS_TILE, BIAS_TILE), lambda ch, c: (0, ch, 0, 0))],
        out_shape=[jax.ShapeDtypeStruct((N_CHUNK, s, 128), F32)] * 3 + edge_shapes * 2
        + [jax.ShapeDtypeStruct((nbr, N_HEADS // 2, BIAS_TILE, BIAS_TILE), F32)],
        compiler_params=_params(("arbitrary", "arbitrary")),
    )(qkv_c, qkv_c, qkv_c, qkv_c, qkv_c, bias, lse_c, delta_c, do_c)
    return outs[0], outs[1], outs[2], outs[3:3 + nbr], outs[3 + nbr:3 + 2 * nbr], outs[3 + 2 * nbr]


def _attn2_bwd_sum(dq, dk, dv, ek, ev, dzs, dus):
    s = dq.shape[1]
    nrb = s // QBLK
    per_blk = ABLK // QBLK
    nbr = len(BRANCH_DIL)

    def body(*refs):
        dq_ref, dk_ref, dv_ref = refs[:3]
        ek_refs, ev_refs = refs[3:3 + nbr], refs[3 + nbr:3 + 2 * nbr]
        dzs_ref, dus_ref, o_ref = refs[3 + 2 * nbr:]
        i = pl.program_id(0)
        dkt, dvt = dk_ref[...], dv_ref[...]
        for b, dil in enumerate(BRANCH_DIL):
            j = i + dil
            ok = jnp.logical_and(j < nrb, j % per_blk < dil)
            dkt = dkt + jnp.where(ok, ek_refs[b][...], 0.0)
            dvt = dvt + jnp.where(ok, ev_refs[b][...], 0.0)
        for ch in range(N_CHUNK):
            o_ref[:, 128 * ch:128 * (ch + 1)] = (dq_ref[ch] * ATTN_SCALE).astype(o_ref.dtype)
            o_ref[:, ATTN_W + 128 * ch:ATTN_W + 128 * (ch + 1)] = dkt[ch].astype(o_ref.dtype)
            o_ref[:, 2 * ATTN_W + 128 * ch:2 * ATTN_W + 128 * (ch + 1)] = dvt[ch].astype(o_ref.dtype)
        o_ref[:, O_SGU:O_SSM] = dzs_ref[...].astype(o_ref.dtype)
        o_ref[:, O_SSM:] = dus_ref[...].astype(o_ref.dtype)

    here = pl.BlockSpec((N_CHUNK, QBLK, 128), lambda i: (0, i, 0))

    def edge_index(i, d):
        j = i + d
        blk = jnp.minimum(j // per_blk, s // ABLK - 1)
        return 0, blk * d + jnp.minimum(j % per_blk, d - 1), 0

    edge_specs = [pl.BlockSpec((N_CHUNK, QBLK, 128), functools.partial(edge_index, d=dil)) for dil in BRANCH_DIL]
    return pl.pallas_call(
        body, name="attn_bwd_sum", grid=(nrb,),
        in_specs=[here, here, here] + edge_specs + edge_specs
        + [pl.BlockSpec((QBLK, 2 * SGU_W), lambda i: (i, 0)), pl.BlockSpec((QBLK, SSM_W), lambda i: (i, 0))],
        out_specs=pl.BlockSpec((QBLK, O_SSM + SSM_W), lambda i: (i, 0)),
        out_shape=jax.ShapeDtypeStruct((s, O_SSM + SSM_W), MXU_DTYPE),
        compiler_params=_params(("parallel",)),
    )(dq, dk, dv, *ek, *ev, dzs, dus)


SGU_ROWS = 512


def _sgu_norm(v_g):
    mu = jnp.mean(v_g, axis=-1, keepdims=True)
    cen = v_g - mu
    var = jnp.mean(cen * cen, axis=-1, keepdims=True)
    rstd = lax.rsqrt(var + EPS)
    return cen * rstd, rstd


def _sgu_fwd(zs, ln_g, ln_b, w_mask, b_t):
    s = zs.shape[0]
    nch = SGU_ROWS // SGU_CHUNK

    def body(z_ref, g_ref, b_ref, w_ref, bt_ref, o_ref):
        gz = _gelu(z_ref[...])
        for g in range(SGU_G):
            sl = slice(SGU_GW * g, SGU_GW * (g + 1))
            u_g = gz[:, sl]
            xhat, _ = _sgu_norm(gz[:, SGU_W + SGU_GW * g:SGU_W + SGU_GW * (g + 1)])
            vn = (xhat * g_ref[:, sl] + b_ref[:, sl]).astype(MXU_DTYPE)
            wg = w_ref[g].astype(MXU_DTYPE)
            for ci in range(nch):
                rs = slice(SGU_CHUNK * ci, SGU_CHUNK * (ci + 1))
                mixed = _dot(wg, vn[rs], NN) + bt_ref[:, g:g + 1]
                o_ref[rs, sl] = u_g[rs] * mixed

    full = lambda shape: pl.BlockSpec(shape, lambda i: tuple(0 for _ in shape))
    return pl.pallas_call(
        body, name="sgu_fwd", grid=(s // SGU_ROWS,),
        in_specs=[pl.BlockSpec((SGU_ROWS, 2 * SGU_W), lambda i: (i, 0)), full((1, SGU_W)), full((1, SGU_W)),
                  full((SGU_G, SGU_CHUNK, SGU_CHUNK)), full((SGU_CHUNK, SGU_G))],
        out_specs=pl.BlockSpec((SGU_ROWS, SGU_W), lambda i: (i, 0)),
        out_shape=jax.ShapeDtypeStruct((s, SGU_W), F32),
        compiler_params=_params(("parallel",)),
    )(zs, ln_g.reshape(1, SGU_W), ln_b.reshape(1, SGU_W), w_mask, b_t)


def _sgu_bwd(zs, ln_g, ln_b, w_mask, b_t, dy):
    s = zs.shape[0]
    nch = SGU_ROWS // SGU_CHUNK

    def body(z_ref, g_ref, b_ref, w_ref, bt_ref, dy_ref, dz_ref, dg_ref, dbb_ref, dw_ref, dbt_ref):
        @pl.when(pl.program_id(0) == 0)
        def _():
            dg_ref[...] = jnp.zeros_like(dg_ref)
            dbb_ref[...] = jnp.zeros_like(dbb_ref)
            dw_ref[...] = jnp.zeros_like(dw_ref)
            dbt_ref[...] = jnp.zeros_like(dbt_ref)

        z = z_ref[...]
        gz, dgelu = _gelu_pair(z)
        dy = dy_ref[...]
        for g in range(SGU_G):
            sl = slice(SGU_GW * g, SGU_GW * (g + 1))
            sv = slice(SGU_W + SGU_GW * g, SGU_W + SGU_GW * (g + 1))
            u_g = gz[:, sl]
            xhat, rstd = _sgu_norm(gz[:, sv])
            gain = g_ref[:, sl]
            vn = (xhat * gain + b_ref[:, sl]).astype(MXU_DTYPE)
            wg = w_ref[g].astype(MXU_DTYPE)
            dy_g = dy[:, sl]
            dvn_parts = []
            for ci in range(nch):
                rs = slice(SGU_CHUNK * ci, SGU_CHUNK * (ci + 1))
                mixed = _dot(wg, vn[rs], NN) + bt_ref[:, g:g + 1]
                dz_ref[rs, sl] = (dy_g[rs] * mixed * dgelu[rs, sl]).astype(dz_ref.dtype)
                dmixed = dy_g[rs] * u_g[rs]
                dm = dmixed.astype(MXU_DTYPE)
                dvn_parts.append(_dot(wg, dm, TN))
                dw_ref[g] += _dot(dm, vn[rs], NT)
                dbt_ref[:, g:g + 1] += jnp.sum(dmixed, axis=-1, keepdims=True)
            dvn = jnp.concatenate(dvn_parts, axis=0)
            dg_ref[:, sl] += jnp.sum(dvn * xhat, axis=0, keepdims=True)
            dbb_ref[:, sl] += jnp.sum(dvn, axis=0, keepdims=True)
            dxh = dvn * gain
            dv = rstd * (dxh - jnp.mean(dxh, axis=-1, keepdims=True)
                         - xhat * jnp.mean(dxh * xhat, axis=-1, keepdims=True))
            dz_ref[:, sv] = (dv * dgelu[:, sv]).astype(dz_ref.dtype)

    full = lambda shape: pl.BlockSpec(shape, lambda i: tuple(0 for _ in shape))
    return pl.pallas_call(
        body, name="sgu_bwd", grid=(s // SGU_ROWS,),
        in_specs=[pl.BlockSpec((SGU_ROWS, 2 * SGU_W), lambda i: (i, 0)), full((1, SGU_W)), full((1, SGU_W)),
                  full((SGU_G, SGU_CHUNK, SGU_CHUNK)), full((SGU_CHUNK, SGU_G)),
                  pl.BlockSpec((SGU_ROWS, SGU_W), lambda i: (i, 0))],
        out_specs=[pl.BlockSpec((SGU_ROWS, 2 * SGU_W), lambda i: (i, 0)), full((1, SGU_W)), full((1, SGU_W)),
                   full((SGU_G, SGU_CHUNK, SGU_CHUNK)), full((SGU_CHUNK, SGU_G))],
        out_shape=[jax.ShapeDtypeStruct((s, 2 * SGU_W), MXU_DTYPE), jax.ShapeDtypeStruct((1, SGU_W), F32),
                   jax.ShapeDtypeStruct((1, SGU_W), F32), jax.ShapeDtypeStruct((SGU_G, SGU_CHUNK, SGU_CHUNK), F32),
                   jax.ShapeDtypeStruct((SGU_CHUNK, SGU_G), F32)],
        compiler_params=_params(("arbitrary",)),
    )(zs, ln_g.reshape(1, SGU_W), ln_b.reshape(1, SGU_W), w_mask, b_t, dy)


def _ssm_discretize(a_re, a_im, log_dt, b_re, b_im):
    dt = jnp.exp(log_dt)[:, None]
    mag = jnp.exp(a_re * dt)
    ab_re = mag * jnp.cos(a_im * dt)
    ab_im = mag * jnp.sin(a_im * dt)
    den = a_re * a_re + a_im * a_im
    f_re = ((ab_re - 1.0) * a_re + ab_im * a_im) / den
    f_im = (ab_im * a_re - (ab_re - 1.0) * a_im) / den
    bb_re = f_re[:, :, None] * b_re - f_im[:, :, None] * b_im
    bb_im = f_re[:, :, None] * b_im + f_im[:, :, None] * b_re
    return ab_re, ab_im, bb_re, bb_im


def _ssm_operands(a_re, a_im, log_dt, b_re, b_im, c_re, c_im):
    ab_re, ab_im, bb_re, bb_im = _ssm_discretize(a_re, a_im, log_dt, b_re, b_im)
    eye = jnp.eye(SSM_G, dtype=F32)
    b_blk = jnp.einsum("pgnc,gh->gcphn", jnp.stack([bb_re, bb_im]), eye).reshape(SSM_W, 2 * NSTATE)
    c_mat = jnp.einsum("pgcn,gh->pgnhc", jnp.stack([c_re, -c_im]), eye).reshape(2 * NSTATE, SSM_W)
    a_row = jnp.stack([ab_re.reshape(NSTATE), ab_im.reshape(NSTATE)])
    p_re, p_im = a_row[0:1], a_row[1:2]
    while p_re.shape[0] < SSM_TSEG:
        l_re, l_im = p_re[-1:], p_im[-1:]
        p_re, p_im = (jnp.concatenate([p_re, p_re * l_re - p_im * l_im]),
                      jnp.concatenate([p_im, p_re * l_im + p_im * l_re]))
    p_tab = jnp.stack([p_re, p_im])
    return b_blk.astype(MXU_DTYPE), c_mat.astype(MXU_DTYPE), a_row, p_tab


def _lane_chunks():
    return [(lo, lo + SSM_LANE_CHUNK) for lo in range(0, NSTATE, SSM_LANE_CHUNK)]


def _seg_rows(j):
    return pl.ds(pl.multiple_of(j * SSM_NSEG, SSM_NSEG), SSM_NSEG)


def _to_segments(t):
    s, w = t.shape
    return t.reshape(s // SSM_TB, SSM_NSEG, SSM_TSEG, w).transpose(0, 2, 1, 3).reshape(s, w)


def _from_segments(t):
    s, w = t.shape
    return t.reshape(s // SSM_TB, SSM_TSEG, SSM_NSEG, w).transpose(0, 2, 1, 3).reshape(s, w)


def _ssm_local_scan(buf, a_ref, *, reverse):
    ends_re, ends_im = [], []
    for lo, hi in _lane_chunks():
        are = jnp.broadcast_to(a_ref[0:1, lo:hi], (SSM_NSEG, hi - lo))
        aim = jnp.broadcast_to(a_ref[1:2, lo:hi], (SSM_NSEG, hi - lo))
        if reverse:
            aim = -aim

        def step(jj, carry, lo=lo, hi=hi, are=are, aim=aim):
            xr, xi = carry
            j = (SSM_TSEG - 1 - jj) if reverse else jj
            tr = buf[_seg_rows(j), lo:hi]
            ti = buf[_seg_rows(j), NSTATE + lo:NSTATE + hi]
            nr = are * xr - aim * xi + tr
            ni = are * xi + aim * xr + ti
            buf[_seg_rows(j), lo:hi] = nr
            buf[_seg_rows(j), NSTATE + lo:NSTATE + hi] = ni
            return nr, ni

        zero = jnp.zeros((SSM_NSEG, hi - lo), F32)
        xr, xi = lax.fori_loop(0, SSM_TSEG, step, (zero, zero), unroll=4)
        ends_re.append(xr)
        ends_im.append(xi)
    return jnp.concatenate(ends_re, axis=1), jnp.concatenate(ends_im, axis=1)


def _ssm_entry_states(ends_re, ends_im, carry_ref, p_ref, entry_ref, *, reverse):
    at_re = p_ref[0, SSM_TSEG - 1:SSM_TSEG, :]
    at_im = p_ref[1, SSM_TSEG - 1:SSM_TSEG, :]
    if reverse:
        at_im = -at_im
    cur_re = carry_ref[0:1, 0:NSTATE]
    cur_im = carry_ref[0:1, NSTATE:2 * NSTATE]
    order = range(SSM_NSEG - 1, -1, -1) if reverse else range(SSM_NSEG)
    for i in order:
        entry_ref[0, i:i + 1, 0:NSTATE] = cur_re
        entry_ref[0, i:i + 1, NSTATE:2 * NSTATE] = cur_im
        nxt_re = ends_re[i:i + 1] + at_re * cur_re - at_im * cur_im
        nxt_im = ends_im[i:i + 1] + at_re * cur_im + at_im * cur_re
        cur_re, cur_im = nxt_re, nxt_im
    carry_ref[0:1, 0:NSTATE] = cur_re
    carry_ref[0:1, NSTATE:2 * NSTATE] = cur_im


def _ssm_fixup(buf, p_ref, entry_ref, *, reverse):
    for lo, hi in _lane_chunks():
        e_re = entry_ref[0, :, lo:hi]
        e_im = entry_ref[0, :, NSTATE + lo:NSTATE + hi]

        def step(j, carry, lo=lo, hi=hi, e_re=e_re, e_im=e_im):
            jp = (SSM_TSEG - 1 - j) if reverse else j
            pr = p_ref[0, pl.ds(jp, 1), lo:hi]
            pi = p_ref[1, pl.ds(jp, 1), lo:hi]
            if reverse:
                pi = -pi
            buf[_seg_rows(j), lo:hi] = buf[_seg_rows(j), lo:hi] + pr * e_re - pi * e_im
            buf[_seg_rows(j), NSTATE + lo:NSTATE + hi] = (buf[_seg_rows(j), NSTATE + lo:NSTATE + hi]
                                                           + pr * e_im + pi * e_re)
            return carry

        lax.fori_loop(0, SSM_TSEG, step, 0, unroll=4)


def _ssm_fwd(u, ops, d_skip, glu_w, glu_b):
    b_blk, c_mat, a_row, p_tab = ops
    s = u.shape[0]
    nblk = s // SSM_TB

    def body(u_ref, bb_ref, cm_ref, a_ref, p_ref, d_ref, gw_ref, gb_ref, y_ref, entry_ref, xbuf, carry):
        @pl.when(pl.program_id(0) == 0)
        def _():
            carry[...] = jnp.zeros_like(carry)

        uu = u_ref[...]
        xbuf[...] = _dotf(uu, bb_ref[...], NN)
        ends_re, ends_im = _ssm_local_scan(xbuf, a_ref, reverse=False)
        _ssm_entry_states(ends_re, ends_im, carry, p_ref, entry_ref, reverse=False)
        _ssm_fixup(xbuf, p_ref, entry_ref, reverse=False)
        y = _dotf(xbuf[...],cm_ref[...], NN) + d_ref[...] * uu
        y2 = _gelu(y)
        gate = jax.nn.sigmoid(_dot(y2.astype(MXU_DTYPE), gw_ref[...].astype(MXU_DTYPE), NN) + gb_ref[...])
        y_ref[...] = y2 * gate

    full = lambda shape: pl.BlockSpec(shape, lambda i: tuple(0 for _ in shape))
    y_seg, entry = pl.pallas_call(
        body, name="ssm_fwd", grid=(nblk,),
        in_specs=[pl.BlockSpec((SSM_TB, SSM_W), lambda i: (i, 0)), full(b_blk.shape), full(c_mat.shape),
                  full(a_row.shape), full(p_tab.shape), full((1, SSM_W)), full((SSM_W, SSM_W)), full((1, SSM_W))],
        out_specs=[pl.BlockSpec((SSM_TB, SSM_W), lambda i: (i, 0)),
                   pl.BlockSpec((1, SSM_NSEG, 2 * NSTATE), lambda i: (i, 0, 0))],
        out_shape=[jax.ShapeDtypeStruct((s, SSM_W), F32), jax.ShapeDtypeStruct((nblk, SSM_NSEG, 2 * NSTATE), F32)],
        scratch_shapes=[pltpu.VMEM((SSM_TB, 2 * NSTATE), F32), pltpu.VMEM((SSM_NSEG, 2 * NSTATE), F32)],
        compiler_params=_params(("arbitrary",)),
    )(_to_segments(u), b_blk, c_mat, a_row, p_tab, d_skip.reshape(1, SSM_W), glu_w, glu_b.reshape(1, SSM_W))
    return _from_segments(y_seg), entry


def _ssm_bwd(u, entry, ops, d_skip, glu_w, glu_b, dout):
    b_blk, c_mat, a_row, p_tab = ops
    s = u.shape[0]
    nblk = s // SSM_TB

    def body(u_ref, en_ref, bb_ref, cm_ref, a_ref, p_ref, d_ref, gw_ref, gb_ref, do_ref,
             du_ref, dbb_ref, dcm_ref, da_ref, dd_ref, dgw_ref, dgb_ref, xbuf, gbuf, gcarry, gentry):
        @pl.when(pl.program_id(0) == 0)
        def _():
            gcarry[...] = jnp.zeros_like(gcarry)
            for r in (dbb_ref, dcm_ref, da_ref, dd_ref, dgw_ref, dgb_ref):
                r[...] = jnp.zeros_like(r)

        uu = u_ref[...]
        xbuf[...] = _dotf(uu, bb_ref[...], NN)
        _ssm_local_scan(xbuf, a_ref, reverse=False)
        _ssm_fixup(xbuf, p_ref, en_ref, reverse=False)
        y = _dotf(xbuf[...],cm_ref[...], NN) + d_ref[...] * uu
        y2, dgelu = _gelu_pair(y)
        y2m = y2.astype(MXU_DTYPE)
        gwm = gw_ref[...].astype(MXU_DTYPE)
        gate = jax.nn.sigmoid(_dot(y2m, gwm, NN) + gb_ref[...])
        dout = do_ref[...]
        dpre = dout * y2 * gate * (1.0 - gate)
        dprem = dpre.astype(MXU_DTYPE)
        dy2 = dout * gate + _dot(dprem, gwm, NT)
        dgw_ref[...] += _dot(y2m, dprem, TN)
        dgb_ref[...] += jnp.sum(dpre, axis=0, keepdims=True)
        dy = dy2 * dgelu
        dd_ref[...] += jnp.sum(dy * uu, axis=0, keepdims=True)
        dcm_ref[...] += _dotf(xbuf[...],dy, TN)
        gbuf[...] = _dotf(dy, cm_ref[...], NT)
        gs_re, gs_im = _ssm_local_scan(gbuf, a_ref, reverse=True)
        _ssm_entry_states(gs_re, gs_im, gcarry, p_ref, gentry, reverse=True)
        _ssm_fixup(gbuf, p_ref, gentry, reverse=True)
        du_ref[...] = (_dotf(gbuf[...], bb_ref[...], NT) + d_ref[...] * dy).astype(du_ref.dtype)
        dbb_ref[...] += _dotf(uu, gbuf[...], TN)
        for lo, hi in _lane_chunks():
            def step(j, carry, lo=lo, hi=hi):
                acc_re, acc_im = carry
                g_re = gbuf[_seg_rows(j), lo:hi]
                g_im = gbuf[_seg_rows(j), NSTATE + lo:NSTATE + hi]
                x_re = xbuf[_seg_rows(j - 1), lo:hi]
                x_im = xbuf[_seg_rows(j - 1), NSTATE + lo:NSTATE + hi]
                return acc_re + g_re * x_re + g_im * x_im, acc_im + g_im * x_re - g_re * x_im

            g0_re = gbuf[_seg_rows(0), lo:hi]
            g0_im = gbuf[_seg_rows(0), NSTATE + lo:NSTATE + hi]
            e_re = en_ref[0, :, lo:hi]
            e_im = en_ref[0, :, NSTATE + lo:NSTATE + hi]
            init = (g0_re * e_re + g0_im * e_im, g0_im * e_re - g0_re * e_im)
            acc_re, acc_im = lax.fori_loop(1, SSM_TSEG, step, init, unroll=4)
            da_ref[0:1, lo:hi] += jnp.sum(acc_re, axis=0, keepdims=True)
            da_ref[1:2, lo:hi] += jnp.sum(acc_im, axis=0, keepdims=True)

    full = lambda shape: pl.BlockSpec(shape, lambda i: tuple(0 for _ in shape))
    rev = pl.BlockSpec((SSM_TB, SSM_W), lambda i: (nblk - 1 - i, 0))
    outs = pl.pallas_call(
        body, name="ssm_bwd", grid=(nblk,),
        in_specs=[rev, pl.BlockSpec((1, SSM_NSEG, 2 * NSTATE), lambda i: (nblk - 1 - i, 0, 0)),
                  full(b_blk.shape), full(c_mat.shape), full(a_row.shape), full(p_tab.shape),
                  full((1, SSM_W)), full((SSM_W, SSM_W)), full((1, SSM_W)), rev],
        out_specs=[rev, full(b_blk.shape), full(c_mat.shape), full(a_row.shape), full((1, SSM_W)),
                   full((SSM_W, SSM_W)), full((1, SSM_W))],
        out_shape=[jax.ShapeDtypeStruct((s, SSM_W), MXU_DTYPE), jax.ShapeDtypeStruct(b_blk.shape, F32),
                   jax.ShapeDtypeStruct(c_mat.shape, F32), jax.ShapeDtypeStruct(a_row.shape, F32),
                   jax.ShapeDtypeStruct((1, SSM_W), F32), jax.ShapeDtypeStruct((SSM_W, SSM_W), F32),
                   jax.ShapeDtypeStruct((1, SSM_W), F32)],
        scratch_shapes=[pltpu.VMEM((SSM_TB, 2 * NSTATE), F32), pltpu.VMEM((SSM_TB, 2 * NSTATE), F32),
                        pltpu.VMEM((SSM_NSEG, 2 * NSTATE), F32), pltpu.VMEM((1, SSM_NSEG, 2 * NSTATE), F32)],
        compiler_params=_params(("arbitrary",)),
    )(_to_segments(u), entry, b_blk, c_mat, a_row, p_tab, d_skip.reshape(1, SSM_W), glu_w, glu_b.reshape(1, SSM_W),
      _to_segments(dout))
    return (_from_segments(outs[0]),) + tuple(outs[1:])


MIX_SEGS = ((0, ATTN_W), (ATTN_W, ATTN_W + SGU_W), (ATTN_W + SGU_W, D_MODEL))


def _chunks_to_rows(a_ref):
    return jnp.concatenate([a_ref[ch] for ch in range(N_CHUNK)], axis=1)


def _mix_fwd(y_attn_c, y_sgu, y_ssm, gain):
    s = y_sgu.shape[0]

    def body(a_ref, b_ref, c_ref, g_ref, o_ref):
        for x, (lo, hi) in zip((_chunks_to_rows(a_ref), b_ref[...], c_ref[...]), MIX_SEGS):
            r = lax.rsqrt(jnp.mean(x * x, axis=-1, keepdims=True) + EPS)
            o_ref[:, lo:hi] = (x * r * g_ref[:, lo:hi]).astype(o_ref.dtype)

    row = lambda w: pl.BlockSpec((ROWS, w), lambda i: (i, 0))
    return pl.pallas_call(
        body, name="mix_fwd", grid=(s // ROWS,),
        in_specs=[pl.BlockSpec((N_CHUNK, ROWS, 128), lambda i: (0, i, 0)), row(SGU_W), row(SSM_W),
                  pl.BlockSpec((1, D_MODEL), lambda i: (0, 0))],
        out_specs=row(D_MODEL), out_shape=jax.ShapeDtypeStruct((s, D_MODEL), MXU_DTYPE),
        compiler_params=_params(("parallel",)),
    )(y_attn_c, y_sgu, y_ssm, gain.reshape(1, D_MODEL))


def _mix_bwd(y_attn_c, y_sgu, y_ssm, gain, dmix):
    s = y_sgu.shape[0]

    def body(a_ref, b_ref, c_ref, g_ref, dm_ref, da_ref, dl_ref, db_ref, dc_ref, dg_ref):
        @pl.when(pl.program_id(0) == 0)
        def _():
            dg_ref[...] = jnp.zeros_like(dg_ref)

        grads = []
        for x, (lo, hi) in zip((_chunks_to_rows(a_ref), b_ref[...], c_ref[...]), MIX_SEGS):
            r = lax.rsqrt(jnp.mean(x * x, axis=-1, keepdims=True) + EPS)
            xhat = x * r
            dm = dm_ref[:, lo:hi].astype(F32)
            dg_ref[:, lo:hi] += jnp.sum(dm * xhat, axis=0, keepdims=True)
            dxh = dm * g_ref[:, lo:hi]
            grads.append(r * (dxh - xhat * jnp.mean(dxh * xhat, axis=-1, keepdims=True)))
        db_ref[...] = grads[1]
        dc_ref[...] = grads[2]
        low = lax.broadcasted_iota(jnp.int32, (ROWS, 128), 1) < HEAD_DIM
        for ch in range(N_CHUNK):
            d_c = grads[0][:, 128 * ch:128 * (ch + 1)]
            da_ref[ch] = d_c.astype(da_ref.dtype)
            prod = d_c * a_ref[ch]
            dl_ref[ch] = jnp.where(low, jnp.sum(prod[:, :HEAD_DIM], axis=-1, keepdims=True),
                                   jnp.sum(prod[:, HEAD_DIM:], axis=-1, keepdims=True))

    row = lambda w: pl.BlockSpec((ROWS, w), lambda i: (i, 0))
    vec = pl.BlockSpec((1, D_MODEL), lambda i: (0, 0))
    chunked = pl.BlockSpec((N_CHUNK, ROWS, 128), lambda i: (0, i, 0))
    return pl.pallas_call(
        body, name="mix_bwd", grid=(s // ROWS,),
        in_specs=[chunked, row(SGU_W), row(SSM_W), vec, row(D_MODEL)],
        out_specs=[chunked, chunked, row(SGU_W), row(SSM_W), vec],
        out_shape=[jax.ShapeDtypeStruct((N_CHUNK, s, 128), ATTN_IO_DTYPE), jax.ShapeDtypeStruct((N_CHUNK, s, 128), F32),
                   jax.ShapeDtypeStruct((s, SGU_W), F32), jax.ShapeDtypeStruct((s, SSM_W), F32),
                   jax.ShapeDtypeStruct((1, D_MODEL), F32)],
        compiler_params=_params(("arbitrary",)),
    )(y_attn_c, y_sgu, y_ssm, gain.reshape(1, D_MODEL), dmix)


CONV_ROWS = 256
CONV_COLS = 1408
CONV_PAIR = 2 * CONV_COLS
HALO = 16


def _interleave_ff(t):
    lead = t.shape[:-1]
    nb = D_FF // CONV_COLS
    return jnp.swapaxes(t.reshape(lead + (2, nb, CONV_COLS)), -3, -2).reshape(lead + (2 * D_FF,))


def _deinterleave_ff(t):
    lead = t.shape[:-1]
    nb = D_FF // CONV_COLS
    return jnp.swapaxes(t.reshape(lead + (nb, 2, CONV_COLS)), -3, -2).reshape(lead + (2 * D_FF,))


def _conv_in_specs():
    halo_idx = lambda i: jnp.maximum(i * (CONV_ROWS // HALO) - 1, 0)
    return [pl.BlockSpec((CONV_ROWS, CONV_PAIR), lambda j, i: (i, j)),
            pl.BlockSpec((HALO, CONV_PAIR), lambda j, i: (halo_idx(i), j)),
            pl.BlockSpec((3, CONV_PAIR), lambda j, i: (0, j)),
            pl.BlockSpec((1, CONV_PAIR), lambda j, i: (0, j))]


def _shift_matrix(rows, back):
    r = lax.broadcasted_iota(jnp.int32, (2 * rows, rows), 0)
    c = lax.broadcasted_iota(jnp.int32, (2 * rows, rows), 1)
    step = jnp.where(r < rows, 1, 2)
    t = jnp.where(r < rows, r, r - rows)
    src = t - step if back else t + step
    return jnp.where(c == src, 1.0, 0.0).astype(MXU_DTYPE)


def _patch_rows(x, at_end, rows):
    tile = 8
    n = x.shape[0]
    idx = lax.broadcasted_iota(jnp.int32, (tile, x.shape[1]), 0)
    piece = x[n - tile:] if at_end else x[:tile]
    for k, row in enumerate(rows):
        where_row = (tile - len(rows) + k) if at_end else k
        piece = jnp.where(idx == where_row, row, piece)
    return jnp.concatenate([x[:n - tile], piece], axis=0) if at_end else jnp.concatenate([piece, x[tile:]], axis=0)


def _mxu_taps(main_m, halo, first):
    shifted = _dot(_shift_matrix(main_m.shape[0], True), main_m, NN)
    h1 = jnp.where(first, 0.0, halo[HALO - 1:HALO, :])
    h2 = jnp.where(first, 0.0, halo[HALO - 2:HALO - 1, :])
    x1 = _patch_rows(shifted[:main_m.shape[0]], False, [h1])
    x2 = _patch_rows(shifted[main_m.shape[0]:], False, [h2, h1])
    return x1, x2


def _conv_gate(w_ref, b_ref, x2, x1, x0):
    return w_ref[0:1, :] * x2 + w_ref[1:2, :] * x1 + w_ref[2:3, :] * x0 + b_ref[...]


def _ffn_gate_fwd(hh, conv_w, conv_b):
    s = hh.shape[0]

    def body(m_ref, h_ref, w_ref, b_ref, o_ref):
        first = pl.program_id(1) == 0
        main_m = m_ref[...]
        x1, x2 = _mxu_taps(main_m, h_ref[...].astype(F32), first)
        conv = _conv_gate(w_ref, b_ref, x2, x1, main_m.astype(F32))
        o_ref[...] = (_gelu(conv[:, CONV_COLS:]) * conv[:, :CONV_COLS]).astype(o_ref.dtype)

    return pl.pallas_call(
        body, name="ffn_act_fwd", grid=(D_FF // CONV_COLS, s // CONV_ROWS), in_specs=_conv_in_specs(),
        out_specs=pl.BlockSpec((CONV_ROWS, CONV_COLS), lambda j, i: (i, j)),
        out_shape=jax.ShapeDtypeStruct((s, D_FF), MXU_DTYPE),
        compiler_params=_params(("parallel", "parallel")),
    )(hh, hh, conv_w, conv_b.reshape(1, -1))


def _ffn_gate_bwd(hh, conv_w, conv_b, da):
    s = hh.shape[0]
    nrow = s // CONV_ROWS

    def gate_grad(conv, da):
        act, dact = _gelu_pair(conv[:, CONV_COLS:])
        return jnp.concatenate([da * act, da * conv[:, :CONV_COLS] * dact], axis=1)

    def body(m_ref, h_ref, w_ref, b_ref, nx_ref, da_ref, dan_ref, o_ref, dw_ref, db_ref):
        first = pl.program_id(1) == 0
        last = pl.program_id(1) == nrow - 1

        @pl.when(first)
        def _():
            dw_ref[...] = jnp.zeros_like(dw_ref)
            db_ref[...] = jnp.zeros_like(db_ref)

        main_m = m_ref[...]
        main = main_m.astype(F32)
        x1, x2 = _mxu_taps(main_m, h_ref[...].astype(F32), first)
        dconv = gate_grad(_conv_gate(w_ref, b_ref, x2, x1, main), da_ref[...].astype(F32))
        nx = nx_ref[...].astype(F32)
        nx1 = _patch_rows(pltpu.roll(nx, 1, 0), False, [main[CONV_ROWS - 1:]])
        nx2 = _patch_rows(pltpu.roll(nx, 2, 0), False, [main[CONV_ROWS - 2:CONV_ROWS - 1], main[CONV_ROWS - 1:]])
        dnext = gate_grad(_conv_gate(w_ref, b_ref, nx2, nx1, nx), jnp.where(last, 0.0, dan_ref[...].astype(F32)))
        dnext = dnext.astype(MXU_DTYPE).astype(F32)
        ahead = _dot(_shift_matrix(CONV_ROWS, False), dconv.astype(MXU_DTYPE), NN)
        ahead1 = _patch_rows(ahead[:CONV_ROWS], True, [dnext[0:1]])
        ahead2 = _patch_rows(ahead[CONV_ROWS:], True, [dnext[0:1], dnext[1:2]])
        o_ref[...] = (w_ref[2:3, :] * dconv + w_ref[1:2, :] * ahead1 + w_ref[0:1, :] * ahead2).astype(o_ref.dtype)
        for t, tap in enumerate((x2, x1, main)):
            dw_ref[t:t + 1, :] += jnp.sum(dconv * tap, axis=0, keepdims=True)
        db_ref[...] += jnp.sum(dconv, axis=0, keepdims=True)

    nxt = lambda i: jnp.minimum((i + 1) * (CONV_ROWS // HALO), s // HALO - 1)
    return pl.pallas_call(
        body, name="ffn_act_bwd", grid=(D_FF // CONV_COLS, nrow),
        in_specs=_conv_in_specs() + [pl.BlockSpec((HALO, CONV_PAIR), lambda j, i: (nxt(i), j)),
                                     pl.BlockSpec((CONV_ROWS, CONV_COLS), lambda j, i: (i, j)),
                                     pl.BlockSpec((HALO, CONV_COLS), lambda j, i: (nxt(i), j))],
        out_specs=[pl.BlockSpec((CONV_ROWS, CONV_PAIR), lambda j, i: (i, j)),
                   pl.BlockSpec((3, CONV_PAIR), lambda j, i: (0, j)), pl.BlockSpec((1, CONV_PAIR), lambda j, i: (0, j))],
        out_shape=[jax.ShapeDtypeStruct((s, 2 * D_FF), MXU_DTYPE), jax.ShapeDtypeStruct((3, 2 * D_FF), F32),
                   jax.ShapeDtypeStruct((1, 2 * D_FF), F32)],
        compiler_params=_params(("parallel", "arbitrary")),
    )(hh, hh, conv_w, conv_b.reshape(1, -1), hh, da, da)


def _ple_weight_specs(layer):
    return [pl.BlockSpec((None, D_MODEL, D_MODEL), lambda i: (layer, 0, 0)),
            pl.BlockSpec((None, PLE_DIM, D_MODEL), lambda i: (layer, 0, 0))]


def _ple_fwd(h, gain, p, w_gate, w_proj, layer):
    s = h.shape[0]
    tm = 512

    def body(h_ref, g_ref, p_ref, wg_ref, wp_ref, o_ref, xn_ref):
        x = h_ref[...]
        xn = _rms_rows(x, g_ref[...])
        xn_ref[...] = xn
        gate = jax.nn.sigmoid(_dot(xn, wg_ref[...].astype(MXU_DTYPE), NN))
        proj = _dot(p_ref[...].astype(MXU_DTYPE), wp_ref[...].astype(MXU_DTYPE), NN)
        o_ref[...] = x + gate * proj

    row = pl.BlockSpec((tm, D_MODEL), lambda i: (i, 0))
    return pl.pallas_call(
        body, name="ple_fwd", grid=(s // tm,),
        in_specs=[row, pl.BlockSpec((1, D_MODEL), lambda i: (0, 0)), pl.BlockSpec((tm, PLE_DIM), lambda i: (i, 0))]
        + _ple_weight_specs(layer),
        out_specs=[row, row],
        out_shape=[jax.ShapeDtypeStruct((s, D_MODEL), F32), jax.ShapeDtypeStruct((s, D_MODEL), MXU_DTYPE)],
        compiler_params=_params(("parallel",)),
    )(h, gain.reshape(1, D_MODEL), p, w_gate, w_proj)


def _ple_bwd(xn, p, w_gate, w_proj, dh, layer):
    s = xn.shape[0]
    tm = 512

    def body(x_ref, p_ref, wg_ref, wp_ref, dh_ref, dpre_ref, dproj_ref):
        gate = jax.nn.sigmoid(_dot(x_ref[...].astype(MXU_DTYPE), wg_ref[...].astype(MXU_DTYPE), NN))
        proj = _dot(p_ref[...].astype(MXU_DTYPE), wp_ref[...].astype(MXU_DTYPE), NN)
        dh = dh_ref[...]
        dpre_ref[...] = (dh * proj * gate * (1.0 - gate)).astype(dpre_ref.dtype)
        dproj_ref[...] = (dh * gate).astype(dproj_ref.dtype)

    row = pl.BlockSpec((tm, D_MODEL), lambda i: (i, 0))
    return pl.pallas_call(
        body, name="ple_bwd", grid=(s // tm,),
        in_specs=[row, pl.BlockSpec((tm, PLE_DIM), lambda i: (i, 0))] + _ple_weight_specs(layer) + [row],
        out_specs=[row, row],
        out_shape=[jax.ShapeDtypeStruct((s, D_MODEL), MXU_DTYPE)] * 2,
        compiler_params=_params(("parallel",)),
    )(xn, p, w_gate, w_proj, dh)


O_SGU = 3 * ATTN_W
O_SSM = O_SGU + 2 * SGU_W


def _layer_consts(w, i):
    causal = jnp.asarray(np.tril(np.ones((SGU_CHUNK, SGU_CHUNK), np.float32)))
    return {
        "sgu_w_mask": w["sgu_w"][i] * causal,
        "sgu_b_t": w["sgu_b"][i].T,
        "ssm_ops": _ssm_operands(w["ssm_a_re"][i], w["ssm_a_im"][i], w["ssm_log_dt"][i], w["ssm_b_re"][i],
                                 w["ssm_b_im"][i], w["ssm_c_re"][i], w["ssm_c_im"][i]),
    }


def _layer_fwd(h0, p_i, w, i, bias):
    c = _layer_consts(w, i)
    xn1, qkv, zs, us = _in_proj(h0, w["norm_attn_g"][i], w["w_in"], i)
    y_attn, lse = _attn2_fwd(qkv, bias)
    y_sgu = _sgu_fwd(zs, w["sgu_ln_g"][i], w["sgu_ln_b"][i], c["sgu_w_mask"], c["sgu_b_t"])
    y_ssm, entry = _ssm_fwd(us, c["ssm_ops"], w["ssm_d"][i], w["ssm_glu_w"][i], w["ssm_glu_b"][i])
    mix = _mix_fwd(y_attn, y_sgu, y_ssm, w["branch_norm_g"][i])
    h1 = _matmul(mix, w["w_out"], name="out_proj", out_dtype=F32, tm=512, tn=1024, residual=h0, layer=i)
    xn2, hh = _ffn_up(h1, w["norm_ffn_g"][i], w["ffn_w_up"], i)
    act = _ffn_gate_fwd(hh, w["ffn_conv_w"][i], w["ffn_conv_b"][i])
    h2 = _matmul(act, w["ffn_w_down"], name="ffn_down", out_dtype=F32, tm=512, tn=1024, residual=h1, layer=i)
    h3, xn3 = _ple_fwd(h2, w["norm_ple_g"][i], p_i, w["ple_w_gate"], w["ple_w_proj"], i)
    saved = dict(h0=h0, xn1=xn1, qkv=qkv, zs=zs, us=us, y_attn=y_attn, lse=lse, y_sgu=y_sgu, y_ssm=y_ssm,
                 entry=entry, mix=mix, h1=h1, xn2=xn2, hh=hh, act=act, h2=h2, xn3=xn3, consts=c)
    return h3, saved


def _layer_bwd(dh3, sv, p_i, w, i, bias):
    c = sv["consts"]
    g = {}
    dpre, dproj = _ple_bwd(sv["xn3"], p_i, w["ple_w_gate"], w["ple_w_proj"], dh3, i)
    g["ple_w_gate"] = _matmul_tn(sv["xn3"], dpre, name="d_ple_w_gate", tk=1024, tn=1024)
    g["ple_w_proj"] = _matmul_tn(p_i, dproj, name="d_ple_w_proj", tk=256, tn=1024)
    dh2, g["norm_ple_g"] = _matmul_rms_bwd(dpre, w["ple_w_gate"], sv["h2"], w["norm_ple_g"][i], dh3,
                                           name="d_xn_ple", layer=i, tm=512)
    g["ffn_w_down"] = _matmul_tn(sv["act"], dh2, name="d_ffn_w_down", tk=1408, tn=1024)
    dact = _matmul(dh2, w["ffn_w_down"], name="d_ffn_act", out_dtype=MXU_DTYPE, tm=512, tn=1408, trans_b=True, layer=i)
    dhh, g["ffn_conv_w"], g["ffn_conv_b"] = _ffn_gate_bwd(sv["hh"], w["ffn_conv_w"][i], w["ffn_conv_b"][i], dact)
    g["ffn_w_up"] = _matmul_tn(sv["xn2"], dhh, name="d_ffn_w_up", tk=1024, tn=1408)
    dh1, g["norm_ffn_g"] = _matmul_rms_bwd(dhh, w["ffn_w_up"], sv["h1"], w["norm_ffn_g"][i], dh2,
                                           name="d_xn_ffn", layer=i, tm=256)
    g["w_out"] = _matmul_tn(sv["mix"], dh1, name="d_w_out", tk=1024, tn=1024)
    dmix = _matmul(dh1, w["w_out"], name="d_mix", out_dtype=F32, tm=512, tn=1024, trans_b=True, layer=i)
    dy_attn, delta, dy_sgu, dy_ssm, g["branch_norm_g"] = _mix_bwd(sv["y_attn"], sv["y_sgu"], sv["y_ssm"],
                                                                  w["branch_norm_g"][i], dmix)
    dq, dk, dv, ek, ev, dbias = _attn2_bwd(sv["qkv"], bias, sv["lse"], delta, dy_attn)
    dzs, g["sgu_ln_g"], g["sgu_ln_b"], dsw, dsb = _sgu_bwd(sv["zs"], w["sgu_ln_g"][i], w["sgu_ln_b"][i],
                                                          c["sgu_w_mask"], c["sgu_b_t"], dy_sgu)
    causal = jnp.asarray(np.tril(np.ones((SGU_CHUNK, SGU_CHUNK), np.float32)))
    g["sgu_w"] = dsw * causal
    g["sgu_b"] = dsb.T
    dus, dbb, dcm, da, g["ssm_d"], g["ssm_glu_w"], g["ssm_glu_b"] = _ssm_bwd(
        sv["us"], sv["entry"], c["ssm_ops"], w["ssm_d"][i], w["ssm_glu_w"][i], w["ssm_glu_b"][i], dy_ssm)
    dbb5 = dbb.reshape(SSM_G, SSM_C, 2, SSM_G, SSM_N)
    dbbar = jnp.einsum("gcpgn->pgnc", dbb5)
    dcm5 = dcm.reshape(2, SSM_G, SSM_N, SSM_G, SSM_C)
    dcc = jnp.einsum("pgngc->pgcn", dcm5)
    g["ssm_c_re"] = dcc[0]
    g["ssm_c_im"] = -dcc[1]
    da2 = da.reshape(2, SSM_G, SSM_N)
    _, vjp = jax.vjp(_ssm_discretize, w["ssm_a_re"][i], w["ssm_a_im"][i], w["ssm_log_dt"][i],
                     w["ssm_b_re"][i], w["ssm_b_im"][i])
    (g["ssm_a_re"], g["ssm_a_im"], g["ssm_log_dt"], g["ssm_b_re"], g["ssm_b_im"]) = vjp(
        (da2[0], da2[1], dbbar[0], dbbar[1]))
    dz = _attn2_bwd_sum(dq, dk, dv, ek, ev, dzs, dus)
    g["w_in"] = _matmul_tn(sv["xn1"], dz, name="d_w_in", tk=1024, tn=1152)
    dh0, g["norm_attn_g"] = _matmul_rms_bwd(dz, w["w_in"], sv["h0"], w["norm_attn_g"][i], dh1,
                                            name="d_xn_attn", layer=i, tm=512)
    for k in ("norm_ple_g", "norm_ffn_g", "branch_norm_g", "norm_attn_g", "sgu_ln_g", "sgu_ln_b", "ssm_d",
              "ssm_glu_b", "ffn_conv_b"):
        g[k] = g[k].reshape(-1)
    return dh0, g, dbias


def _local_step(x, p, target, w, ff_interleaved=False):
    ff_names = ("ffn_conv_b",) if ff_interleaved else FF_SHARDED + ("ffn_conv_b",)
    w = dict(w)
    for k in ff_names:
        w[k] = _interleave_ff(w[k])
    bias = _bias_build(w["rel_bias"])
    h = x
    saved = []
    for i in range(DEPTH):
        h, sv = _layer_fwd(h, p[i], w, i, bias)
        saved.append(sv)
    loss, dh, dgf = _loss_head(h, w["final_norm_g"], target)
    layer_grads = [None] * DEPTH
    dbias = None
    for i in reversed(range(DEPTH)):
        dh, layer_grads[i], db = _layer_bwd(dh, saved[i], p[i], w, i, bias)
        dbias = db if dbias is None else dbias + db
    grads = {k: jnp.stack([layer_grads[i][k] for i in range(DEPTH)]) for k in layer_grads[0]}
    for k in ff_names:
        grads[k] = _deinterleave_ff(grads[k])
    grads["rel_bias"] = _bias_reduce(dbias)
    grads["final_norm_g"] = dgf.reshape(-1)
    return loss, dh, grads


def _as_rows(a, rows=None):
    size = int(np.prod(a.shape))
    if rows is None:
        rows = -(-size // (16 * PACK_COLS)) * 16
    if size % PACK_COLS:
        a = jnp.pad(a.reshape(-1), (0, (-size) % PACK_COLS))
    a2 = a.reshape(-1, PACK_COLS)
    return jnp.pad(a2, ((0, rows - a2.shape[0]), (0, 0)))


def _shard_shape(name):
    full, ax = BIG_FULL[name]
    shp = [DEPTH] + list(full)
    shp[ax] //= N_CHIPS
    return tuple(shp)


EXACT_NAMES = ("ffn_conv_w",)


def _pack_rows_of(name):
    n = int(np.prod(_shard_shape(name))) * (2 if name in EXACT_NAMES else 1)
    rows = -(-n // PACK_COLS)
    return -(-rows // 16) * 16


def _pack_shards(shards, dtype, exact=False):
    split_words = exact and jnp.dtype(dtype).itemsize == 2
    parts = []
    for n in BIG_NAMES:
        a = shards[n]
        if split_words and n in EXACT_NAMES:
            a = lax.bitcast_convert_type(a.astype(F32), dtype)
        parts.append(_as_rows(a.astype(dtype), _pack_rows_of(n)))
    used = sum(pt.shape[0] for pt in parts)
    parts.append(jnp.zeros((PACK_ROWS - used, PACK_COLS), dtype))
    return jnp.concatenate(parts, axis=0)


def _unpack_shard(flat, name, exact=False):
    off = 0
    for n in BIG_NAMES:
        if n == name:
            break
        off += _pack_rows_of(n)
    shp = _shard_shape(name)
    cnt = int(np.prod(shp))
    if exact and name in EXACT_NAMES and jnp.dtype(flat.dtype).itemsize == 2:
        vec = flat[off:off + _pack_rows_of(name)].reshape(-1)
        return lax.bitcast_convert_type(vec[:2 * cnt].reshape(shp + (2,)), F32)
    if cnt % PACK_COLS == 0:
        return flat[off:off + cnt // PACK_COLS].reshape(shp)
    return flat[off:off + _pack_rows_of(name)].reshape(-1)[:cnt].reshape(shp)


FF_SHARDED = ("ffn_w_up", "ffn_conv_w")
FF_CHIP_ORDER = (0, 2, 1, 3)


def _chip_order(name):
    return FF_CHIP_ORDER if name in FF_SHARDED else tuple(range(N_CHIPS))


def _split_full(full, name):
    _, ax = BIG_FULL[name]
    parts = jnp.split(full, N_CHIPS, axis=ax)
    out = [None] * N_CHIPS
    for j, k in enumerate(_chip_order(name)):
        out[k] = parts[j]
    return out


def _join_shards(shards, name):
    _, ax = BIG_FULL[name]
    return jnp.concatenate([shards[k] for k in _chip_order(name)], axis=ax)


def _small_shapes(w):
    return [(n, w[n].shape) for n in SMALL_NAMES]


def _small_rows(shp):
    return -(-int(np.prod(shp)) // (8 * PACK_COLS)) * 8


def _pack_small(d):
    parts = [_as_rows(d[n].astype(F32), _small_rows(d[n].shape)) for n in SMALL_NAMES]
    used = sum(pt.shape[0] for pt in parts)
    parts.append(jnp.zeros((SMALL_ROWS - used, PACK_COLS), F32))
    return jnp.concatenate(parts, axis=0)


def _unpack_small(flat, shapes):
    out, off = {}, 0
    for n, shp in shapes:
        cnt = int(np.prod(shp))
        rows = _small_rows(shp)
        if cnt % PACK_COLS == 0:
            out[n] = flat[off:off + cnt // PACK_COLS].reshape(shp)
        else:
            out[n] = flat[off:off + rows].reshape(-1)[:cnt].reshape(shp)
        off += rows
    return out


MESH = pl.DeviceIdType.MESH
ANY = pl.BlockSpec(memory_space=pl.ANY)


def _me():
    return lax.axis_index("x"), lax.axis_index("y"), lax.axis_index("c")


def _other_chips(x, y):
    return [(1 - x, y), (x, 1 - y), (1 - x, 1 - y)]


def _gather_weights(wflat):
    def body(w_ref, out_ref, send_sems, recv_sems):
        x, y, c = _me()
        sibling = (x, y, 1 - c)
        chips = _other_chips(x, y)

        def rows(chip, half):
            return out_ref.at[2 * chip[0] + chip[1], pl.ds(half * PACK_HALF, PACK_HALF), :]

        def copy(k, chip, half, to, src=None):
            return pltpu.make_async_remote_copy(
                src_ref=rows(chip, half) if src is None else src, dst_ref=rows(chip, half),
                send_sem=send_sems.at[k], recv_sem=recv_sems.at[k], device_id=to, device_id_type=MESH)

        my_half = w_ref.at[pl.ds(c * PACK_HALF, PACK_HALF), :]
        first = [copy(j, (x, y), c, (*chip, c), src=my_half) for j, chip in enumerate(chips)]
        for cp in first:
            cp.start()
        passed = [copy(3 + j, chip, c, sibling) for j, chip in enumerate(chips)]
        for j, chip in enumerate(chips):
            copy(j, chip, c, (x, y, c)).wait_recv()
            passed[j].start()
        for j, chip in enumerate(chips):
            copy(3 + j, chip, 1 - c, (x, y, c)).wait_recv()
        for cp in first + passed:
            cp.wait_send()

    return pl.pallas_call(
        body, name="gather_weights", in_specs=[ANY], out_specs=ANY,
        out_shape=jax.ShapeDtypeStruct((N_CHIPS, PACK_ROWS, PACK_COLS), wflat.dtype),
        scratch_shapes=[pltpu.SemaphoreType.DMA((6,)), pltpu.SemaphoreType.DMA((6,))],
    )(wflat)


def _fill_own_shard(wall, wflat, chip_idx):
    rows = PACK_ROWS // 8

    def body(idx_ref, w_ref, wall_ref, o_ref):
        del idx_ref, wall_ref
        o_ref[...] = w_ref[...]

    return pl.pallas_call(
        body, name="fill_own_shard",
        grid_spec=pltpu.PrefetchScalarGridSpec(
            num_scalar_prefetch=1, grid=(PACK_ROWS // rows,),
            in_specs=[pl.BlockSpec((rows, PACK_COLS), lambda i, idx: (i, 0)), ANY],
            out_specs=pl.BlockSpec((None, rows, PACK_COLS), lambda i, idx: (idx[0], i, 0))),
        out_shape=jax.ShapeDtypeStruct(wall.shape, wall.dtype),
        input_output_aliases={2: 0},
        compiler_params=_params(("parallel",)),
    )(chip_idx, wflat, wall)


def _exchange_partials(gb, gs):
    def body(gb_ref, gs_ref, half_ref, small_ref, send_sems, recv_sems, local_sem):
        x, y, c = _me()
        me_idx = 4 * x + 2 * y + c
        mine = pltpu.make_async_copy(gs_ref, small_ref.at[me_idx], local_sem)
        mine.start()
        d2d = pltpu.make_async_remote_copy(
            src_ref=gb_ref.at[:, pl.ds((1 - c) * PACK_HALF, PACK_HALF), :], dst_ref=half_ref,
            send_sem=send_sems.at[0], recv_sem=recv_sems.at[0], device_id=(x, y, 1 - c), device_id_type=MESH)
        d2d.start()
        copies = []
        for k in range(1, N_DEV):
            fx, fy, fc = (k >> 2) & 1, (k >> 1) & 1, k & 1
            peer = (x ^ fx, y ^ fy, c ^ fc)
            copies.append(pltpu.make_async_remote_copy(
                src_ref=gs_ref, dst_ref=small_ref.at[me_idx], send_sem=send_sems.at[k], recv_sem=recv_sems.at[k],
                device_id=peer, device_id_type=MESH))
        for cp in copies:
            cp.start()
        for k in range(1, N_DEV):
            fx, fy, fc = (k >> 2) & 1, (k >> 1) & 1, k & 1
            peer_idx = 4 * (x ^ fx) + 2 * (y ^ fy) + (c ^ fc)
            pltpu.make_async_remote_copy(
                src_ref=gs_ref, dst_ref=small_ref.at[peer_idx], send_sem=send_sems.at[k], recv_sem=recv_sems.at[k],
                device_id=(x, y, c), device_id_type=MESH).wait_recv()
        d2d.wait_recv()
        d2d.wait_send()
        for cp in copies:
            cp.wait_send()
        mine.wait()

    return pl.pallas_call(
        body, name="exchange_partials", in_specs=[ANY, pl.BlockSpec(memory_space=pltpu.VMEM)], out_specs=[ANY, ANY],
        out_shape=[jax.ShapeDtypeStruct((N_CHIPS, PACK_HALF, PACK_COLS), gb.dtype),
                   jax.ShapeDtypeStruct((N_DEV, SMALL_ROWS, PACK_COLS), F32)],
        scratch_shapes=[pltpu.SemaphoreType.DMA((N_DEV,)), pltpu.SemaphoreType.DMA((N_DEV,)), pltpu.SemaphoreType.DMA],
    )(gb, gs)


RED_ROWS = 256


def _chip_partials(gb, sib, c_idx):
    nrow = PACK_HALF // RED_ROWS

    def body(c_ref, a_ref, b_ref, o_ref):
        del c_ref
        o_ref[...] = (a_ref[...].astype(F32) + b_ref[...].astype(F32)).astype(o_ref.dtype)

    blk = (1, RED_ROWS, PACK_COLS)
    return pl.pallas_call(
        body, name="chip_partials",
        grid_spec=pltpu.PrefetchScalarGridSpec(
            num_scalar_prefetch=1, grid=(N_CHIPS, nrow),
            in_specs=[pl.BlockSpec(blk, lambda k, i, c: (k, c[0] * nrow + i, 0)),
                      pl.BlockSpec(blk, lambda k, i, c: (k, i, 0))],
            out_specs=pl.BlockSpec(blk, lambda k, i, c: (k, i, 0))),
        out_shape=jax.ShapeDtypeStruct((N_CHIPS, PACK_HALF, PACK_COLS), gb.dtype),
        compiler_params=_params(("parallel", "parallel")),
    )(c_idx, gb, sib)


def _scatter_partials(pc):
    def body(pc_ref, out_ref, send_sems, recv_sems):
        x, y, c = _me()
        chips = _other_chips(x, y)
        copies = [pltpu.make_async_remote_copy(
            src_ref=pc_ref.at[2 * chip[0] + chip[1]], dst_ref=out_ref.at[k],
            send_sem=send_sems.at[k], recv_sem=recv_sems.at[k], device_id=(*chip, c), device_id_type=MESH)
            for k, chip in enumerate(chips)]
        for cp in copies:
            cp.start()
        for cp in copies:
            cp.wait_recv()
        for cp in copies:
            cp.wait_send()

    return pl.pallas_call(
        body, name="scatter_partials", in_specs=[ANY], out_specs=ANY,
        out_shape=jax.ShapeDtypeStruct((3, PACK_HALF, PACK_COLS), pc.dtype),
        scratch_shapes=[pltpu.SemaphoreType.DMA((3,)), pltpu.SemaphoreType.DMA((3,))],
    )(pc)


def _final_half(gb, sib, recv, idx):
    nrow = PACK_HALF // RED_ROWS

    def body(idx_ref, a_ref, b_ref, r_ref, o_ref):
        del idx_ref
        acc = a_ref[0].astype(F32) + b_ref[0].astype(F32)
        for k in range(3):
            acc = acc + r_ref[k].astype(F32)
        o_ref[...] = acc

    return pl.pallas_call(
        body, name="final_half",
        grid_spec=pltpu.PrefetchScalarGridSpec(
            num_scalar_prefetch=1, grid=(nrow,),
            in_specs=[pl.BlockSpec((1, RED_ROWS, PACK_COLS), lambda i, idx: (idx[0], idx[1] * nrow + i, 0)),
                      pl.BlockSpec((1, RED_ROWS, PACK_COLS), lambda i, idx: (idx[0], i, 0)),
                      pl.BlockSpec((3, RED_ROWS, PACK_COLS), lambda i, idx: (0, i, 0))],
            out_specs=pl.BlockSpec((RED_ROWS, PACK_COLS), lambda i, idx: (i, 0))),
        out_shape=jax.ShapeDtypeStruct((PACK_HALF, PACK_COLS), F32),
        compiler_params=_params(("parallel",)),
    )(idx, gb, sib, recv)


def _share_halves(half):
    def body(h_ref, out_ref, send_sem, recv_sem):
        x, y, c = _me()
        cp = pltpu.make_async_remote_copy(src_ref=h_ref, dst_ref=out_ref, send_sem=send_sem, recv_sem=recv_sem,
                                          device_id=(x, y, 1 - c), device_id_type=MESH)
        cp.start()
        cp.wait_recv()
        cp.wait_send()

    return pl.pallas_call(
        body, name="share_halves", in_specs=[ANY], out_specs=ANY,
        out_shape=jax.ShapeDtypeStruct((PACK_HALF, PACK_COLS), F32),
        scratch_shapes=[pltpu.SemaphoreType.DMA, pltpu.SemaphoreType.DMA],
    )(half)


def _sum_small(allsmall):
    def body(a_ref, o_ref):
        acc = a_ref[0]
        for k in range(1, N_DEV):
            acc = acc + a_ref[k]
        o_ref[...] = acc

    tr = 96
    return pl.pallas_call(
        body, name="sum_small", grid=(SMALL_ROWS // tr,),
        in_specs=[pl.BlockSpec((N_DEV, tr, PACK_COLS), lambda i: (0, i, 0))],
        out_specs=pl.BlockSpec((tr, PACK_COLS), lambda i: (i, 0)),
        out_shape=jax.ShapeDtypeStruct((SMALL_ROWS, PACK_COLS), F32),
        compiler_params=_params(("parallel",)),
    )(allsmall)


def _adamw(w, g, m, v, *, name):
    shape = w.shape
    cols = shape[-1]
    as2 = lambda t: t.reshape(-1, cols)
    w2, g2, m2, v2 = as2(w), as2(g), as2(m), as2(v)
    rows = w2.shape[0]
    tr = rows
    if rows * cols * 4 > (1 << 20):
        tr = _tile(rows, max(8, (1 << 20) // (cols * 4) // 8 * 8), 8)

    def body(w_ref, g_ref, m_ref, v_ref, d_ref, mo_ref, vo_ref):
        gg = g_ref[...]
        mn = ADAM_B1 * m_ref[...] + (1.0 - ADAM_B1) * gg
        vn = ADAM_B2 * v_ref[...] + (1.0 - ADAM_B2) * (gg * gg)
        m_hat = mn / (1.0 - ADAM_B1 ** ADAM_STEP)
        v_hat = vn / (1.0 - ADAM_B2 ** ADAM_STEP)
        d_ref[...] = -ADAM_LR * (m_hat / (jnp.sqrt(v_hat) + ADAM_EPS) + ADAM_WD * w_ref[...])
        mo_ref[...] = mn
        vo_ref[...] = vn

    blk = pl.BlockSpec((tr, cols), lambda i: (i, 0))
    outs = pl.pallas_call(
        body, name=name, grid=(rows // tr,), in_specs=[blk] * 4, out_specs=[blk] * 3,
        out_shape=[jax.ShapeDtypeStruct((rows, cols), F32)] * 3,
        compiler_params=_params(("parallel",)),
    )(w2, g2, m2, v2)
    return tuple(t.reshape(shape) for t in outs)


def _adamw_many(ws, gs, ms, vs):
    n = len(ws)

    def body(*refs):
        for t in range(n):
            w_ref, g_ref, m_ref, v_ref = refs[t], refs[n + t], refs[2 * n + t], refs[3 * n + t]
            d_ref, mo_ref, vo_ref = refs[4 * n + t], refs[5 * n + t], refs[6 * n + t]
            gg = g_ref[...]
            mn = ADAM_B1 * m_ref[...] + (1.0 - ADAM_B1) * gg
            vn = ADAM_B2 * v_ref[...] + (1.0 - ADAM_B2) * (gg * gg)
            m_hat = mn / (1.0 - ADAM_B1 ** ADAM_STEP)
            v_hat = vn / (1.0 - ADAM_B2 ** ADAM_STEP)
            d_ref[...] = -ADAM_LR * (m_hat / (jnp.sqrt(v_hat) + ADAM_EPS) + ADAM_WD * w_ref[...])
            mo_ref[...] = mn
            vo_ref[...] = vn

    vmem = pl.BlockSpec(memory_space=pltpu.VMEM)
    outs = pl.pallas_call(
        body, name="adamw_small", in_specs=[vmem] * (4 * n), out_specs=[vmem] * (3 * n),
        out_shape=[jax.ShapeDtypeStruct(w.shape, F32) for w in ws] * 3,
        compiler_params=pltpu.CompilerParams(vmem_limit_bytes=VMEM_LIMIT_BYTES),
    )(*ws, *gs, *ms, *vs)
    return outs[:n], outs[n:2 * n], outs[2 * n:]


def kernel(x, p, rel_bias, norm_attn_g, w_in, sgu_ln_g, sgu_ln_b, sgu_w, sgu_b, ssm_a_re, ssm_a_im, ssm_log_dt, ssm_b_re, ssm_b_im, ssm_c_re, ssm_c_im, ssm_d, ssm_glu_w, ssm_glu_b, branch_norm_g, w_out, norm_ffn_g, ffn_w_up, ffn_conv_w, ffn_conv_b, ffn_w_down, norm_ple_g, ple_w_gate, ple_w_proj, final_norm_g, loss_target, m_rel_bias, m_norm_attn_g, m_w_in, m_sgu_ln_g, m_sgu_ln_b, m_sgu_w, m_sgu_b, m_ssm_a_re, m_ssm_a_im, m_ssm_log_dt, m_ssm_b_re, m_ssm_b_im, m_ssm_c_re, m_ssm_c_im, m_ssm_d, m_ssm_glu_w, m_ssm_glu_b, m_branch_norm_g, m_w_out, m_norm_ffn_g, m_ffn_w_up, m_ffn_conv_w, m_ffn_conv_b, m_ffn_w_down, m_norm_ple_g, m_ple_w_gate, m_ple_w_proj, m_final_norm_g, v_rel_bias, v_norm_attn_g, v_w_in, v_sgu_ln_g, v_sgu_ln_b, v_sgu_w, v_sgu_b, v_ssm_a_re, v_ssm_a_im, v_ssm_log_dt, v_ssm_b_re, v_ssm_b_im, v_ssm_c_re, v_ssm_c_im, v_ssm_d, v_ssm_glu_w, v_ssm_glu_b, v_branch_norm_g, v_w_out, v_norm_ffn_g, v_ffn_w_up, v_ffn_conv_w, v_ffn_conv_b, v_ffn_w_down, v_norm_ple_g, v_ple_w_gate, v_ple_w_proj, v_final_norm_g):
    args = dict(locals())
    wts = {n: args[n] for n in WEIGHT_NAMES}
    mom_m = {n: args["m_" + n] for n in WEIGHT_NAMES}
    mom_v = {n: args["v_" + n] for n in WEIGHT_NAMES}

    xi, yi, ci = _me()
    wflat = _pack_shards({n: wts[n] for n in BIG_NAMES}, MXU_DTYPE, exact=True)
    wall = _fill_own_shard(_gather_weights(wflat), wflat, jnp.stack([2 * xi + yi]).astype(jnp.int32))
    full = dict(wts)
    for n in BIG_NAMES:
        full[n] = _join_shards([_unpack_shard(wall[k], n, exact=True) for k in range(N_CHIPS)], n)
    full["ffn_conv_w"] = full["ffn_conv_w"].astype(F32)

    loss, dx, grads = _local_step(x[0], p[:, 0], loss_target[0], full, ff_interleaved=True)

    xi, yi, ci = _me()
    stacked = {n: _split_full(grads[n], n) for n in BIG_NAMES}
    gb = jnp.stack([_pack_shards({n: stacked[n][k] for n in BIG_NAMES}, MXU_DTYPE) for k in range(N_CHIPS)])
    gs = _pack_small(grads).at[SMALL_ROWS - 1, 0].set(loss[0, 0])
    sib, allsmall = _exchange_partials(gb, gs)
    pc = _chip_partials(gb, sib, jnp.stack([ci]).astype(jnp.int32))
    recv = _scatter_partials(pc)
    half = _final_half(gb, sib, recv, jnp.stack([2 * xi + yi, ci]).astype(jnp.int32))
    other = _share_halves(half)
    gflat = jnp.concatenate([jnp.where(ci == 0, half, other), jnp.where(ci == 0, other, half)], axis=0)
    small_sum = _sum_small(allsmall)
    loss = small_sum[SMALL_ROWS - 1, 0]
    gsmall = _unpack_small(small_sum, _small_shapes(wts))

    g_out, d_out, m_out, v_out = {}, {}, {}, {}
    for n in BIG_NAMES:
        g_out[n] = _unpack_shard(gflat, n)
        d_out[n], m_out[n], v_out[n] = _adamw(wts[n], g_out[n], mom_m[n], mom_v[n], name="adamw_" + n)
    d_sm, m_sm, v_sm = _adamw_many([wts[n] for n in SMALL_NAMES], [gsmall[n] for n in SMALL_NAMES],
                                   [mom_m[n] for n in SMALL_NAMES], [mom_v[n] for n in SMALL_NAMES])
    for t, n in enumerate(SMALL_NAMES):
        g_out[n], d_out[n], m_out[n], v_out[n] = gsmall[n], d_sm[t], m_sm[t], v_sm[t]

    return (loss, dx[None], *[g_out[n] for n in WEIGHT_NAMES], *[d_out[n] for n in WEIGHT_NAMES],
            *[m_out[n] for n in WEIGHT_NAMES], *[v_out[n] for n in WEIGHT_NAMES])
```

```python
import functools
import math

import numpy as np
import jax
import jax.numpy as jnp
from jax import lax
from jax.experimental import pallas as pl
from jax.experimental.pallas import tpu as pltpu

F32 = jnp.float32
MXU_DTYPE = jnp.bfloat16
VMEM_LIMIT_BYTES = 52 * 1024 * 1024

D_MODEL = 1024
DEPTH = 2
PLE_DIM = 256
HEAD_DIM = 64
N_HEADS = 8
ATTN_W = 512
QBLK = 128
BRANCH_DIL = (1, 4, 16)
N_BUCKETS = 32
REL_MAX_DIST = 2048
SGU_W = 256
SGU_G = 4
SGU_GW = 64
SGU_CHUNK = 128
SSM_W = 256
SSM_G = 16
SSM_C = 16
SSM_N = 64
NSTATE = SSM_G * SSM_N
D_FF = 2816
EPS = 1e-6
NEG_INF = -1e30
ATTN_SCALE = HEAD_DIM ** -0.5

ADAM_LR = 0.001
ADAM_B1 = 0.9
ADAM_B2 = 0.999
ADAM_EPS = 1e-08
ADAM_WD = 0.01
ADAM_STEP = 10

SSM_NSEG = 8
SSM_TSEG = 64
SSM_TB = SSM_NSEG * SSM_TSEG
SSM_LANE_CHUNK = 512

N_CHIPS = 4
N_DEV = 8

BIG_NAMES = ("w_in", "ssm_glu_w", "w_out", "ffn_w_up", "ffn_conv_w", "ffn_w_down", "ple_w_gate", "ple_w_proj")
BIG_FULL = {
    "w_in": ((D_MODEL, 2304), 2),
    "ssm_glu_w": ((SSM_W, SSM_W), 1),
    "w_out": ((D_MODEL, D_MODEL), 1),
    "ffn_w_up": ((D_MODEL, 2 * D_FF), 2),
    "ffn_conv_w": ((3, 2 * D_FF), 2),
    "ffn_w_down": ((D_FF, D_MODEL), 1),
    "ple_w_gate": ((D_MODEL, D_MODEL), 1),
    "ple_w_proj": ((PLE_DIM, D_MODEL), 2),
}
PACK_COLS = 1024
PACK_ROWS = 6656
PACK_HALF = PACK_ROWS // 2

SMALL_NAMES = ("rel_bias", "norm_attn_g", "sgu_ln_g", "sgu_ln_b", "sgu_w", "sgu_b", "ssm_a_re", "ssm_a_im",
               "ssm_log_dt", "ssm_b_re", "ssm_b_im", "ssm_c_re", "ssm_c_im", "ssm_d", "ssm_glu_b",
               "branch_norm_g", "norm_ffn_g", "ffn_conv_b", "norm_ple_g", "final_norm_g")
SMALL_ROWS = 384

WEIGHT_NAMES = ("rel_bias", "norm_attn_g", "w_in", "sgu_ln_g", "sgu_ln_b", "sgu_w", "sgu_b", "ssm_a_re", "ssm_a_im",
                "ssm_log_dt", "ssm_b_re", "ssm_b_im", "ssm_c_re", "ssm_c_im", "ssm_d", "ssm_glu_w", "ssm_glu_b",
                "branch_norm_g", "w_out", "norm_ffn_g", "ffn_w_up", "ffn_conv_w", "ffn_conv_b", "ffn_w_down",
                "norm_ple_g", "ple_w_gate", "ple_w_proj", "final_norm_g")


def _params(sem):
    return pltpu.CompilerParams(dimension_semantics=sem, vmem_limit_bytes=VMEM_LIMIT_BYTES)


def _tile(n, cap, mult=128):
    if n <= cap:
        return n
    best = None
    for t in range(mult, cap + 1, mult):
        if n % t == 0:
            best = t
    assert best is not None, (n, cap)
    return best


def _gelu(x):
    return 0.5 * x * (1.0 + jnp.tanh(0.7978845608028654 * (x + 0.044715 * x * x * x)))


def _gelu_pair(x):
    x2 = x * x
    t = jnp.tanh(0.7978845608028654 * x * (1.0 + 0.044715 * x2))
    half = 0.5 * (1.0 + t)
    return x * half, half + 0.5 * x * (1.0 - t * t) * (0.7978845608028654 + 3.0 * 0.044715 * 0.7978845608028654 * x2)


def _dot(a, b, dims):
    return lax.dot_general(a, b, (dims, ((), ())), preferred_element_type=F32)


def _dotf(a, b, dims):
    return _dot(a.astype(MXU_DTYPE), b.astype(MXU_DTYPE), dims)


NN = ((1,), (0,))
NT = ((1,), (1,))
TN = ((0,), (0,))


def _matmul(a, b, *, name, out_dtype, tm, tn, trans_b=False, residual=None, layer=None):
    m, k = a.shape
    n = b.shape[-2] if trans_b else b.shape[-1]
    tm = _tile(m, tm, 8)
    tn = _tile(n, tn)
    dims = NT if trans_b else NN
    lead = () if layer is None else (None,)
    lidx = () if layer is None else (layer,)

    def body(*refs):
        if residual is None:
            a_ref, b_ref, o_ref = refs
        else:
            a_ref, b_ref, r_ref, o_ref = refs
        acc = _dot(a_ref[...].astype(MXU_DTYPE), b_ref[...].astype(MXU_DTYPE), dims)
        if residual is not None:
            acc = acc + r_ref[...]
        o_ref[...] = acc.astype(o_ref.dtype)

    b_spec = (pl.BlockSpec(lead + (tn, k), lambda i, j: lidx + (j, 0)) if trans_b
              else pl.BlockSpec(lead + (k, tn), lambda i, j: lidx + (0, j)))
    in_specs = [pl.BlockSpec((tm, k), lambda i, j: (i, 0)), b_spec]
    args = [a, b]
    if residual is not None:
        in_specs.append(pl.BlockSpec((tm, tn), lambda i, j: (i, j)))
        args.append(residual)
    return pl.pallas_call(
        body, name=name, grid=(m // tm, n // tn), in_specs=in_specs,
        out_specs=pl.BlockSpec((tm, tn), lambda i, j: (i, j)),
        out_shape=jax.ShapeDtypeStruct((m, n), out_dtype),
        compiler_params=_params(("parallel", "parallel")),
    )(*args)


def _matmul_tn(a, g, *, name, tk, tn, tm=1024):
    m, k = a.shape
    n = g.shape[1]
    tk = _tile(k, tk)
    tn = _tile(n, tn)
    tm = _tile(m, tm, 8)

    def body(a_ref, g_ref, o_ref):
        @pl.when(pl.program_id(2) == 0)
        def _():
            o_ref[...] = jnp.zeros_like(o_ref)

        o_ref[...] += _dot(a_ref[...].astype(MXU_DTYPE), g_ref[...].astype(MXU_DTYPE), TN)

    return pl.pallas_call(
        body, name=name, grid=(k // tk, n // tn, m // tm),
        in_specs=[pl.BlockSpec((tm, tk), lambda i, j, s: (s, i)),
                  pl.BlockSpec((tm, tn), lambda i, j, s: (s, j))],
        out_specs=pl.BlockSpec((tk, tn), lambda i, j, s: (i, j)),
        out_shape=jax.ShapeDtypeStruct((k, n), F32),
        compiler_params=_params(("parallel", "parallel", "arbitrary")),
    )(a, g)


ROWS = 512


def _matmul_rms_bwd(a, b, h, g, dres, *, name, layer, tm):
    s, k = a.shape
    d = b.shape[-2]

    def body(a_ref, b_ref, h_ref, g_ref, dres_ref, dh_ref, dg_ref):
        @pl.when(pl.program_id(0) == 0)
        def _():
            dg_ref[...] = jnp.zeros_like(dg_ref)

        dxn = _dot(a_ref[...].astype(MXU_DTYPE), b_ref[...].astype(MXU_DTYPE), NT)
        x = h_ref[...]
        r = lax.rsqrt(jnp.mean(x * x, axis=-1, keepdims=True) + EPS)
        xhat = x * r
        dg_ref[...] += jnp.sum(dxn * xhat, axis=0, keepdims=True)
        dxh = dxn * g_ref[...]
        dh_ref[...] = dres_ref[...] + r * (dxh - xhat * jnp.mean(dxh * xhat, axis=-1, keepdims=True))

    row = pl.BlockSpec((tm, d), lambda i: (i, 0))
    vec = pl.BlockSpec((1, d), lambda i: (0, 0))
    return pl.pallas_call(
        body, name=name, grid=(s // tm,),
        in_specs=[pl.BlockSpec((tm, k), lambda i: (i, 0)), pl.BlockSpec((None, d, k), lambda i: (layer, 0, 0)),
                  row, vec, row],
        out_specs=[row, vec],
        out_shape=[jax.ShapeDtypeStruct((s, d), F32), jax.ShapeDtypeStruct((1, d), F32)],
        compiler_params=_params(("arbitrary",)),
    )(a, b, h, g.reshape(1, d), dres)


def _loss_head(h, g, target):
    s, d = h.shape

    def body(h_ref, g_ref, t_ref, loss_ref, dh_ref, dg_ref):
        @pl.when(pl.program_id(0) == 0)
        def _():
            loss_ref[...] = jnp.zeros_like(loss_ref)
            dg_ref[...] = jnp.zeros_like(dg_ref)

        x = h_ref[...]
        r = lax.rsqrt(jnp.mean(x * x, axis=-1, keepdims=True) + EPS)
        xhat = x * r
        err = xhat * g_ref[...] - t_ref[...]
        loss_ref[...] += 0.5 * jnp.sum(jnp.mean(err * err, axis=-1, keepdims=True), axis=0, keepdims=True)
        dy = err / d
        dg_ref[...] += jnp.sum(dy * xhat, axis=0, keepdims=True)
        dxh = dy * g_ref[...]
        dh_ref[...] = r * (dxh - xhat * jnp.mean(dxh * xhat, axis=-1, keepdims=True))

    row = pl.BlockSpec((ROWS, d), lambda i: (i, 0))
    vec = pl.BlockSpec((1, d), lambda i: (0, 0))
    one = pl.BlockSpec((1, 1), lambda i: (0, 0))
    return pl.pallas_call(
        body, name="loss_head", grid=(s // ROWS,), in_specs=[row, vec, row], out_specs=[one, row, vec],
        out_shape=[jax.ShapeDtypeStruct((1, 1), F32), jax.ShapeDtypeStruct((s, d), F32),
                   jax.ShapeDtypeStruct((1, d), F32)],
        compiler_params=_params(("arbitrary",)),
    )(h, g.reshape(1, d), target)


def _t5_bucket(dist):
    max_exact = N_BUCKETS // 2
    dd = np.maximum(dist, 0)
    large = max_exact + (np.log(np.maximum(dd, 1) / max_exact) / np.log(REL_MAX_DIST / max_exact)
                         * (N_BUCKETS - max_exact)).astype(np.int32)
    large = np.minimum(large, N_BUCKETS - 1)
    return np.where(dd < max_exact, dd, large).astype(np.int32)


def _bucket_table():
    qq = np.arange(QBLK)[:, None]
    kk = np.arange(QBLK)[None, :]
    out = np.zeros((len(BRANCH_DIL), 2, QBLK, QBLK), np.int32)
    for b, dil in enumerate(BRANCH_DIL):
        out[b, 0] = _t5_bucket((qq - kk + QBLK) * dil)
        out[b, 1] = _t5_bucket((qq - kk) * dil)
    return out


BIAS_TILE = 2 * QBLK


def _bias_build(rel_bias):
    idx = jnp.asarray(_bucket_table())

    def body(idx_ref, rb_ref, o_ref):
        ch = pl.program_id(1)
        row = lax.broadcasted_iota(jnp.int32, (QBLK, QBLK), 0)
        col = lax.broadcasted_iota(jnp.int32, (QBLK, QBLK), 1)
        for part in range(2):
            ids = idx_ref[0, 1 - part]
            valid = (col <= row) if part == 0 else (col >= row)
            for h in range(2):
                acc = jnp.zeros((QBLK, QBLK), F32)
                for b in range(N_BUCKETS):
                    acc = jnp.where(ids == b, rb_ref[b, 2 * ch + h], acc)
                o_ref[0, 0, QBLK * h:QBLK * (h + 1), QBLK * part:QBLK * (part + 1)] = jnp.where(valid, acc, NEG_INF)

    return pl.pallas_call(
        body, name="attn_bias_build", grid=(len(BRANCH_DIL), N_HEADS // 2),
        in_specs=[pl.BlockSpec((1, 2, QBLK, QBLK), lambda b, c: (b, 0, 0, 0)),
                  pl.BlockSpec(memory_space=pltpu.SMEM)],
        out_specs=pl.BlockSpec((1, 1, BIAS_TILE, BIAS_TILE), lambda b, c: (b, c, 0, 0)),
        out_shape=jax.ShapeDtypeStruct((len(BRANCH_DIL), N_HEADS // 2, BIAS_TILE, BIAS_TILE), F32),
        compiler_params=_params(("parallel", "parallel")),
    )(idx, rel_bias)


def _bias_reduce(dbias):
    idx = jnp.asarray(_bucket_table())
    nb = len(BRANCH_DIL)

    def body(idx_ref, d_ref, o_ref):
        def per_bucket(b, carry):
            for h in range(N_HEADS):
                tot = jnp.zeros((), F32)
                for br in range(nb):
                    for part in range(2):
                        tile = d_ref[br, h // 2, QBLK * (h % 2):QBLK * (h % 2 + 1), QBLK * part:QBLK * (part + 1)]
                        tot = tot + jnp.sum(jnp.where(idx_ref[br, 1 - part] == b, tile, 0.0))
                o_ref[b, h] = tot
            return carry

        lax.fori_loop(0, N_BUCKETS, per_bucket, 0)

    return pl.pallas_call(
        body, name="attn_bias_reduce",
        in_specs=[pl.BlockSpec(memory_space=pltpu.VMEM), pl.BlockSpec(memory_space=pltpu.VMEM)],
        out_specs=pl.BlockSpec(memory_space=pltpu.SMEM),
        out_shape=jax.ShapeDtypeStruct((N_BUCKETS, N_HEADS), F32),
        compiler_params=pltpu.CompilerParams(vmem_limit_bytes=VMEM_LIMIT_BYTES),
    )(idx, dbias)


ATTN_IO_DTYPE = F32
ABLK = 2048
N_CHUNK = ATTN_W // 128


def _rows(start, dil):
    if dil > 1:
        return pl.ds(start, QBLK, stride=dil)
    return pl.ds(pl.multiple_of(start, QBLK), QBLK)


def _low_head():
    return lax.broadcasted_iota(jnp.int32, (QBLK, 128), 1) < HEAD_DIM


def _head_split(t):
    low = _low_head()
    zero = jnp.zeros_like(t)
    return jnp.where(low, t, zero), jnp.where(low, zero, t)


def _tile_bias(b_ref, branch, first):
    bias = b_ref[branch]
    if first is None:
        return bias
    col = lax.broadcasted_iota(jnp.int32, (BIAS_TILE, BIAS_TILE), 1)
    return jnp.where(jnp.logical_and(first, col >= QBLK), NEG_INF, bias)


def _loop(n, fn):
    if n == 1:
        fn(jnp.int32(0), 0)
    elif n > 1:
        lax.fori_loop(0, n, fn, 0, unroll=4)


def _for_each_tile(tile, c):
    for branch, dil in enumerate(BRANCH_DIL):
        span = QBLK * dil

        def edge(r, carry, branch=branch, span=span):
            tile(branch, r, False, ABLK - span + r, c == 0)
            return carry

        def inner(t, carry, branch=branch, span=span, dil=dil):
            start = (1 + t // dil) * span + t % dil
            tile(branch, start, True, start - span, None)
            return carry

        _loop(dil, edge)
        _loop((ABLK // span - 1) * dil, inner)


def _attn_chunk_specs(nb):
    blk = (None, ABLK, 128)
    prev = lambda c: jnp.maximum(c - 1, 0)
    return [pl.BlockSpec(blk, lambda ch, c: (ch, c, 0)),
            pl.BlockSpec(blk, lambda ch, c: (N_CHUNK + ch, c, 0)),
            pl.BlockSpec(blk, lambda ch, c: (2 * N_CHUNK + ch, c, 0)),
            pl.BlockSpec(blk, lambda ch, c: (N_CHUNK + ch, prev(c), 0)),
            pl.BlockSpec(blk, lambda ch, c: (2 * N_CHUNK + ch, prev(c), 0)),
            pl.BlockSpec((len(BRANCH_DIL), None, BIAS_TILE, BIAS_TILE), lambda ch, c: (0, ch, 0, 0))]


def _rms_rows(x, g):
    r = lax.rsqrt(jnp.mean(x * x, axis=-1, keepdims=True) + EPS)
    return (x * r * g).astype(MXU_DTYPE)


def _in_proj(h, gain, w_in, layer):
    s, k = h.shape
    tm = 512
    nch = O_SGU // 128

    def body(h_ref, g_ref, w_ref, xn_ref, qkv_ref, zs_ref, us_ref):
        xn = _rms_rows(h_ref[...], g_ref[...])
        xn_ref[...] = xn
        acc = _dot(xn, w_ref[...].astype(MXU_DTYPE), NN)
        for j in range(nch):
            blk = acc[:, 128 * j:128 * (j + 1)]
            if j < N_CHUNK:
                blk = blk * ATTN_SCALE
            qkv_ref[j] = blk.astype(qkv_ref.dtype)
        zs_ref[...] = acc[:, O_SGU:O_SSM]
        us_ref[...] = acc[:, O_SSM:]

    n = w_in.shape[-1]
    return pl.pallas_call(
        body, name="in_proj", grid=(s // tm,),
        in_specs=[pl.BlockSpec((tm, k), lambda i: (i, 0)), pl.BlockSpec((1, k), lambda i: (0, 0)),
                  pl.BlockSpec((None, k, n), lambda i: (layer, 0, 0))],
        out_specs=[pl.BlockSpec((tm, k), lambda i: (i, 0)), pl.BlockSpec((nch, tm, 128), lambda i: (0, i, 0)),
                   pl.BlockSpec((tm, O_SSM - O_SGU), lambda i: (i, 0)), pl.BlockSpec((tm, n - O_SSM), lambda i: (i, 0))],
        out_shape=[jax.ShapeDtypeStruct((s, k), MXU_DTYPE), jax.ShapeDtypeStruct((nch, s, 128), ATTN_IO_DTYPE),
                   jax.ShapeDtypeStruct((s, O_SSM - O_SGU), F32), jax.ShapeDtypeStruct((s, n - O_SSM), F32)],
        compiler_params=_params(("parallel",)),
    )(h, gain.reshape(1, k), w_in)


def _ffn_up(h, gain, w_up, layer):
    s, k = h.shape
    n = w_up.shape[-1]
    tm, tn = 1024, CONV_COLS

    def body(h_ref, g_ref, w_ref, xn_ref, o_ref):
        @pl.when(pl.program_id(1) == 0)
        def _():
            xn_ref[...] = _rms_rows(h_ref[...], g_ref[...])

        o_ref[...] = _dot(xn_ref[...], w_ref[...].astype(MXU_DTYPE), NN).astype(o_ref.dtype)

    return pl.pallas_call(
        body, name="ffn_up", grid=(s // tm, n // tn),
        in_specs=[pl.BlockSpec((tm, k), lambda i, j: (i, 0)), pl.BlockSpec((1, k), lambda i, j: (0, 0)),
                  pl.BlockSpec((None, k, tn), lambda i, j: (layer, 0, j))],
        out_specs=[pl.BlockSpec((tm, k), lambda i, j: (i, 0)), pl.BlockSpec((tm, tn), lambda i, j: (i, j))],
        out_shape=[jax.ShapeDtypeStruct((s, k), MXU_DTYPE), jax.ShapeDtypeStruct((s, n), MXU_DTYPE)],
        compiler_params=_params(("parallel", "arbitrary")),
    )(h, gain.reshape(1, k), w_up)


def _attn2_fwd(qkv_c, bias):
    s = qkv_c.shape[1]
    nb = s // ABLK
    last = len(BRANCH_DIL) - 1

    def body(q_ref, kc_ref, vc_ref, kp_ref, vp_ref, b_ref, o_ref, l_ref, acc_s, m_s, l_s):
        low = _low_head()
        e_st = jnp.concatenate(_head_split(jnp.ones((QBLK, 128), MXU_DTYPE)) * 2, axis=0)

        def tile(branch, start, prev_in_block, pstart, first):
            dil = BRANCH_DIL[branch]
            rq, rp = _rows(start, dil), _rows(pstart, dil)
            k_ref, v_ref = (kc_ref, vc_ref) if prev_in_block else (kp_ref, vp_ref)
            q_st = jnp.concatenate(_head_split(q_ref[rq, :].astype(MXU_DTYPE)), axis=0)
            k_st = jnp.concatenate([kc_ref[rq, :].astype(MXU_DTYPE), k_ref[rp, :].astype(MXU_DTYPE)], axis=0)
            v_st = jnp.concatenate(_head_split(vc_ref[rq, :].astype(MXU_DTYPE))
                                   + _head_split(v_ref[rp, :].astype(MXU_DTYPE)), axis=0)
            sc = _dot(q_st, k_st, NT) + _tile_bias(b_ref, branch, first)
            m_new = jnp.max(sc, axis=-1, keepdims=True)
            if branch > 0:
                m_old2 = m_s[rq, :]
                m_old = jnp.concatenate([m_old2[:, 0:1], m_old2[:, HEAD_DIM:HEAD_DIM + 1]], axis=0)
                m_new = jnp.maximum(m_old, m_new)
                alpha = jnp.exp(m_old - m_new)
            p = jnp.exp(sc - m_new).astype(MXU_DTYPE)
            lhs = jnp.concatenate([p[:QBLK, :QBLK], p[QBLK:, :QBLK], p[:QBLK, QBLK:], p[QBLK:, QBLK:]], axis=1)
            acc2 = _dot(lhs, v_st, NN)
            sum2 = _dot(lhs, e_st, NN)
            m2 = jnp.where(low, m_new[:QBLK], m_new[QBLK:])
            if branch > 0:
                a2 = jnp.where(low, alpha[:QBLK], alpha[QBLK:])
                acc2 = acc2 + a2 * acc_s[rq, :]
                sum2 = sum2 + a2 * l_s[rq, :]
            if branch == last:
                o_ref[rq, :] = acc2 / sum2
                l_ref[rq, :] = m2 + jnp.log(sum2)
            else:
                acc_s[rq, :] = acc2
                m_s[rq, :] = m2
                l_s[rq, :] = sum2

        _for_each_tile(tile, pl.program_id(1))

    out_spec = pl.BlockSpec((None, ABLK, 128), lambda ch, c: (ch, c, 0))
    return pl.pallas_call(
        body, name="attn_fwd", grid=(N_CHUNK, nb), in_specs=_attn_chunk_specs(nb),
        out_specs=[out_spec, out_spec],
        out_shape=[jax.ShapeDtypeStruct((N_CHUNK, s, 128), F32)] * 2,
        scratch_shapes=[pltpu.VMEM((ABLK, 128), F32)] * 3,
        compiler_params=_params(("parallel", "arbitrary")),
    )(qkv_c, qkv_c, qkv_c, qkv_c, qkv_c, bias)


def _attn2_bwd(qkv_c, bias, lse_c, delta_c, do_c):
    s = qkv_c.shape[1]
    nb = s // ABLK
    nbr = len(BRANCH_DIL)

    def body(q_ref, kc_ref, vc_ref, kp_ref, vp_ref, b_ref, l_ref, dl_ref, do_ref,
             dq_ref, dk_ref, dv_ref, *rest):
        ek_refs, ev_refs, db_ref = rest[:nbr], rest[nbr:2 * nbr], rest[2 * nbr]
        c = pl.program_id(1)

        @pl.when(c == 0)
        def _():
            db_ref[...] = jnp.zeros_like(db_ref)

        for r in (dq_ref, dk_ref, dv_ref) + tuple(ek_refs) + tuple(ev_refs):
            r[...] = jnp.zeros_like(r)

        def tile(branch, start, prev_in_block, pstart, first):
            dil = BRANCH_DIL[branch]
            rq, rp = _rows(start, dil), _rows(pstart, dil)
            k_ref, v_ref = (kc_ref, vc_ref) if prev_in_block else (kp_ref, vp_ref)
            kc2 = kc_ref[rq, :].astype(MXU_DTYPE)
            kp2 = k_ref[rp, :].astype(MXU_DTYPE)
            q_st = jnp.concatenate(_head_split(q_ref[rq, :].astype(MXU_DTYPE)), axis=0)
            do_st = jnp.concatenate(_head_split(do_ref[rq, :].astype(MXU_DTYPE)), axis=0)
            k_st = jnp.concatenate([kc2, kp2], axis=0)
            v_st = jnp.concatenate([vc_ref[rq, :].astype(MXU_DTYPE), v_ref[rp, :].astype(MXU_DTYPE)], axis=0)
            kh_st = jnp.concatenate(_head_split(kc2) + _head_split(kp2), axis=0)
            lse2 = l_ref[rq, :]
            del2 = dl_ref[rq, :]
            lse_st = jnp.concatenate([lse2[:, 0:1], lse2[:, HEAD_DIM:HEAD_DIM + 1]], axis=0)
            del_st = jnp.concatenate([del2[:, 0:1], del2[:, HEAD_DIM:HEAD_DIM + 1]], axis=0)
            p = jnp.exp(_dot(q_st, k_st, NT) + _tile_bias(b_ref, branch, first) - lse_st)
            ds = p * (_dot(do_st, v_st, NT) - del_st)
            db_ref[branch] += ds
            ds = ds.astype(MXU_DTYPE)
            p = p.astype(MXU_DTYPE)
            lhs = jnp.concatenate([ds[:QBLK, :QBLK], ds[QBLK:, :QBLK], ds[:QBLK, QBLK:], ds[QBLK:, QBLK:]], axis=1)
            dk_st = _dot(ds, q_st, TN)
            dv_st = _dot(p, do_st, TN)
            dq_ref[rq, :] += _dot(lhs, kh_st, NN)
            dk_ref[rq, :] += dk_st[:QBLK]
            dv_ref[rq, :] += dv_st[:QBLK]
            if prev_in_block:
                dk_ref[rp, :] += dk_st[QBLK:]
                dv_ref[rp, :] += dv_st[QBLK:]
            else:
                ek_refs[branch][rq, :] = dk_st[QBLK:]
                ev_refs[branch][rq, :] = dv_st[QBLK:]

        _for_each_tile(tile, c)

    blk = pl.BlockSpec((None, ABLK, 128), lambda ch, c: (ch, c, 0))
    outs = pl.pallas_call(
        body, name="attn_bwd", grid=(N_CHUNK, nb), in_specs=_attn_chunk_specs(nb) + [blk, blk, blk],
        out_specs=[blk] * (3 + 2 * nbr) + [pl.BlockSpec((nbr, None, BIAS_TILE, BIAS_TILE), lambda ch, c: (0, ch, 0, 0))],
        out_shape=[jax.ShapeDtypeStruct((N_CHUNK, s, 128), F32)] * (3 + 2 * nbr)
        + [jax.ShapeDtypeStruct((nbr, N_HEADS // 2, BIAS_TILE, BIAS_TILE), F32)],
        compiler_params=_params(("arbitrary", "arbitrary")),
    )(qkv_c, qkv_c, qkv_c, qkv_c, qkv_c, bias, lse_c, delta_c, do_c)
    return outs[0], outs[1], outs[2], outs[3:3 + nbr], outs[3 + nbr:3 + 2 * nbr], outs[3 + 2 * nbr]


def _attn2_bwd_sum(dq, dk, dv, ek, ev, dzs, dus):
    s = dq.shape[1]
    nrb = s // QBLK
    per_blk = ABLK // QBLK
    nbr = len(BRANCH_DIL)

    def body(*refs):
        dq_ref, dk_ref, dv_ref = refs[:3]
        ek_refs, ev_refs = refs[3:3 + nbr], refs[3 + nbr:3 + 2 * nbr]
        dzs_ref, dus_ref, o_ref = refs[3 + 2 * nbr:]
        i = pl.program_id(0)
        dkt, dvt = dk_ref[...], dv_ref[...]
        for b, dil in enumerate(BRANCH_DIL):
            j = i + dil
            ok = jnp.logical_and(j < nrb, j % per_blk < dil)
            dkt = dkt + jnp.where(ok, ek_refs[b][...], 0.0)
            dvt = dvt + jnp.where(ok, ev_refs[b][...], 0.0)
        for ch in range(N_CHUNK):
            o_ref[:, 128 * ch:128 * (ch + 1)] = (dq_ref[ch] * ATTN_SCALE).astype(o_ref.dtype)
            o_ref[:, ATTN_W + 128 * ch:ATTN_W + 128 * (ch + 1)] = dkt[ch].astype(o_ref.dtype)
            o_ref[:, 2 * ATTN_W + 128 * ch:2 * ATTN_W + 128 * (ch + 1)] = dvt[ch].astype(o_ref.dtype)
        o_ref[:, O_SGU:O_SSM] = dzs_ref[...].astype(o_ref.dtype)
        o_ref[:, O_SSM:] = dus_ref[...].astype(o_ref.dtype)

    here = pl.BlockSpec((N_CHUNK, QBLK, 128), lambda i: (0, i, 0))
    edge_specs = [pl.BlockSpec((N_CHUNK, QBLK, 128),
                               functools.partial(lambda i, d: (0, jnp.minimum(i + d, nrb - 1), 0), d=dil))
                  for dil in BRANCH_DIL]
    return pl.pallas_call(
        body, name="attn_bwd_sum", grid=(nrb,),
        in_specs=[here, here, here] + edge_specs + edge_specs
        + [pl.BlockSpec((QBLK, 2 * SGU_W), lambda i: (i, 0)), pl.BlockSpec((QBLK, SSM_W), lambda i: (i, 0))],
        out_specs=pl.BlockSpec((QBLK, O_SSM + SSM_W), lambda i: (i, 0)),
        out_shape=jax.ShapeDtypeStruct((s, O_SSM + SSM_W), MXU_DTYPE),
        compiler_params=_params(("parallel",)),
    )(dq, dk, dv, *ek, *ev, dzs, dus)


SGU_ROWS = 512


def _sgu_norm(v_g):
    mu = jnp.mean(v_g, axis=-1, keepdims=True)
    cen = v_g - mu
    var = jnp.mean(cen * cen, axis=-1, keepdims=True)
    rstd = lax.rsqrt(var + EPS)
    return cen * rstd, rstd


def _sgu_fwd(zs, ln_g, ln_b, w_mask, b_t):
    s = zs.shape[0]
    nch = SGU_ROWS // SGU_CHUNK

    def body(z_ref, g_ref, b_ref, w_ref, bt_ref, o_ref):
        gz = _gelu(z_ref[...])
        for g in range(SGU_G):
            sl = slice(SGU_GW * g, SGU_GW * (g + 1))
            u_g = gz[:, sl]
            xhat, _ = _sgu_norm(gz[:, SGU_W + SGU_GW * g:SGU_W + SGU_GW * (g + 1)])
            vn = (xhat * g_ref[:, sl] + b_ref[:, sl]).astype(MXU_DTYPE)
            wg = w_ref[g].astype(MXU_DTYPE)
            for ci in range(nch):
                rs = slice(SGU_CHUNK * ci, SGU_CHUNK * (ci + 1))
                mixed = _dot(wg, vn[rs], NN) + bt_ref[:, g:g + 1]
                o_ref[rs, sl] = u_g[rs] * mixed

    full = lambda shape: pl.BlockSpec(shape, lambda i: tuple(0 for _ in shape))
    return pl.pallas_call(
        body, name="sgu_fwd", grid=(s // SGU_ROWS,),
        in_specs=[pl.BlockSpec((SGU_ROWS, 2 * SGU_W), lambda i: (i, 0)), full((1, SGU_W)), full((1, SGU_W)),
                  full((SGU_G, SGU_CHUNK, SGU_CHUNK)), full((SGU_CHUNK, SGU_G))],
        out_specs=pl.BlockSpec((SGU_ROWS, SGU_W), lambda i: (i, 0)),
        out_shape=jax.ShapeDtypeStruct((s, SGU_W), F32),
        compiler_params=_params(("parallel",)),
    )(zs, ln_g.reshape(1, SGU_W), ln_b.reshape(1, SGU_W), w_mask, b_t)


def _sgu_bwd(zs, ln_g, ln_b, w_mask, b_t, dy):
    s = zs.shape[0]
    nch = SGU_ROWS // SGU_CHUNK

    def body(z_ref, g_ref, b_ref, w_ref, bt_ref, dy_ref, dz_ref, dg_ref, dbb_ref, dw_ref, dbt_ref):
        @pl.when(pl.program_id(0) == 0)
        def _():
            dg_ref[...] = jnp.zeros_like(dg_ref)
            dbb_ref[...] = jnp.zeros_like(dbb_ref)
            dw_ref[...] = jnp.zeros_like(dw_ref)
            dbt_ref[...] = jnp.zeros_like(dbt_ref)

        z = z_ref[...]
        gz, dgelu = _gelu_pair(z)
        dy = dy_ref[...]
        for g in range(SGU_G):
            sl = slice(SGU_GW * g, SGU_GW * (g + 1))
            sv = slice(SGU_W + SGU_GW * g, SGU_W + SGU_GW * (g + 1))
            u_g = gz[:, sl]
            xhat, rstd = _sgu_norm(gz[:, sv])
            gain = g_ref[:, sl]
            vn = (xhat * gain + b_ref[:, sl]).astype(MXU_DTYPE)
            wg = w_ref[g].astype(MXU_DTYPE)
            dy_g = dy[:, sl]
            dvn_parts = []
            for ci in range(nch):
                rs = slice(SGU_CHUNK * ci, SGU_CHUNK * (ci + 1))
                mixed = _dot(wg, vn[rs], NN) + bt_ref[:, g:g + 1]
                dz_ref[rs, sl] = (dy_g[rs] * mixed * dgelu[rs, sl]).astype(dz_ref.dtype)
                dmixed = dy_g[rs] * u_g[rs]
                dm = dmixed.astype(MXU_DTYPE)
                dvn_parts.append(_dot(wg, dm, TN))
                dw_ref[g] += _dot(dm, vn[rs], NT)
                dbt_ref[:, g:g + 1] += jnp.sum(dmixed, axis=-1, keepdims=True)
            dvn = jnp.concatenate(dvn_parts, axis=0)
            dg_ref[:, sl] += jnp.sum(dvn * xhat, axis=0, keepdims=True)
            dbb_ref[:, sl] += jnp.sum(dvn, axis=0, keepdims=True)
            dxh = dvn * gain
            dv = rstd * (dxh - jnp.mean(dxh, axis=-1, keepdims=True)
                         - xhat * jnp.mean(dxh * xhat, axis=-1, keepdims=True))
            dz_ref[:, sv] = (dv * dgelu[:, sv]).astype(dz_ref.dtype)

    full = lambda shape: pl.BlockSpec(shape, lambda i: tuple(0 for _ in shape))
    return pl.pallas_call(
        body, name="sgu_bwd", grid=(s // SGU_ROWS,),
        in_specs=[pl.BlockSpec((SGU_ROWS, 2 * SGU_W), lambda i: (i, 0)), full((1, SGU_W)), full((1, SGU_W)),
                  full((SGU_G, SGU_CHUNK, SGU_CHUNK)), full((SGU_CHUNK, SGU_G)),
                  pl.BlockSpec((SGU_ROWS, SGU_W), lambda i: (i, 0))],
        out_specs=[pl.BlockSpec((SGU_ROWS, 2 * SGU_W), lambda i: (i, 0)), full((1, SGU_W)), full((1, SGU_W)),
                   full((SGU_G, SGU_CHUNK, SGU_CHUNK)), full((SGU_CHUNK, SGU_G))],
        out_shape=[jax.ShapeDtypeStruct((s, 2 * SGU_W), MXU_DTYPE), jax.ShapeDtypeStruct((1, SGU_W), F32),
                   jax.ShapeDtypeStruct((1, SGU_W), F32), jax.ShapeDtypeStruct((SGU_G, SGU_CHUNK, SGU_CHUNK), F32),
                   jax.ShapeDtypeStruct((SGU_CHUNK, SGU_G), F32)],
        compiler_params=_params(("arbitrary",)),
    )(zs, ln_g.reshape(1, SGU_W), ln_b.reshape(1, SGU_W), w_mask, b_t, dy)


def _group_avg():
    r = lax.broadcasted_iota(jnp.int32, (SGU_W, SGU_W), 0) // SGU_GW
    c = lax.broadcasted_iota(jnp.int32, (SGU_W, SGU_W), 1) // SGU_GW
    return jnp.where(r == c, 1.0 / SGU_GW, 0.0).astype(F32)


def _group_mean(x, avg):
    return lax.dot_general(x, avg, (NN, ((), ())), preferred_element_type=F32, precision=lax.Precision.HIGHEST)


def _sgu_prep(z, g_ref, b_ref):
    gz, dgelu = _gelu_pair(z)
    u, v = gz[:, :SGU_W], gz[:, SGU_W:]
    avg = _group_avg()
    cen = v - _group_mean(v, avg)
    rstd = lax.rsqrt(_group_mean(cen * cen, avg) + EPS)
    xhat = cen * rstd
    vn = (xhat * g_ref[...] + b_ref[...]).astype(MXU_DTYPE)
    return u, dgelu, avg, rstd, xhat, vn


def _sgu_mix(w_ref, bt_ref, vn_rows, lane_group):
    mixed = jnp.zeros((SGU_CHUNK, SGU_W), F32)
    for g in range(SGU_G):
        m_g = _dot(w_ref[g].astype(MXU_DTYPE), vn_rows, NN) + bt_ref[:, g:g + 1]
        mixed = jnp.where(lane_group == g, m_g, mixed)
    return mixed


def _sgu2_fwd(zs, ln_g, ln_b, w_mask, b_t):
    s = zs.shape[0]
    nch = SGU_ROWS // SGU_CHUNK

    def body(z_ref, g_ref, b_ref, w_ref, bt_ref, o_ref):
        u, _, _, _, _, vn = _sgu_prep(z_ref[...], g_ref, b_ref)
        lane_group = lax.broadcasted_iota(jnp.int32, (SGU_CHUNK, SGU_W), 1) // SGU_GW
        for ci in range(nch):
            rs = slice(SGU_CHUNK * ci, SGU_CHUNK * (ci + 1))
            o_ref[rs, :] = u[rs] * _sgu_mix(w_ref, bt_ref, vn[rs], lane_group)

    full = lambda shape: pl.BlockSpec(shape, lambda i: tuple(0 for _ in shape))
    return pl.pallas_call(
        body, name="sgu_fwd", grid=(s // SGU_ROWS,),
        in_specs=[pl.BlockSpec((SGU_ROWS, 2 * SGU_W), lambda i: (i, 0)), full((1, SGU_W)), full((1, SGU_W)),
                  full((SGU_G, SGU_CHUNK, SGU_CHUNK)), full((SGU_CHUNK, SGU_G))],
        out_specs=pl.BlockSpec((SGU_ROWS, SGU_W), lambda i: (i, 0)),
        out_shape=jax.ShapeDtypeStruct((s, SGU_W), F32),
        compiler_params=_params(("parallel",)),
    )(zs, ln_g.reshape(1, SGU_W), ln_b.reshape(1, SGU_W), w_mask, b_t)


def _sgu2_bwd(zs, ln_g, ln_b, w_mask, b_t, dy):
    s = zs.shape[0]
    nch = SGU_ROWS // SGU_CHUNK

    def body(z_ref, g_ref, b_ref, w_ref, bt_ref, dy_ref, dz_ref, dg_ref, dbb_ref, dw_ref, dbt_ref):
        @pl.when(pl.program_id(0) == 0)
        def _():
            dg_ref[...] = jnp.zeros_like(dg_ref)
            dbb_ref[...] = jnp.zeros_like(dbb_ref)
            dw_ref[...] = jnp.zeros_like(dw_ref)
            dbt_ref[...] = jnp.zeros_like(dbt_ref)

        u, dgelu, avg, rstd, xhat, vn = _sgu_prep(z_ref[...], g_ref, b_ref)
        lane_group = lax.broadcasted_iota(jnp.int32, (SGU_CHUNK, SGU_W), 1) // SGU_GW
        dy = dy_ref[...]
        dvn_parts = []
        for ci in range(nch):
            rs = slice(SGU_CHUNK * ci, SGU_CHUNK * (ci + 1))
            mixed = _sgu_mix(w_ref, bt_ref, vn[rs], lane_group)
            dz_ref[rs, 0:SGU_W] = (dy[rs] * mixed * dgelu[rs, :SGU_W]).astype(dz_ref.dtype)
            dmixed = dy[rs] * u[rs]
            dvn = jnp.zeros((SGU_CHUNK, SGU_W), F32)
            for g in range(SGU_G):
                own = lane_group == g
                dm_g = jnp.where(own, dmixed, 0.0)
                dvn = jnp.where(own, _dot(w_ref[g].astype(MXU_DTYPE), dm_g.astype(MXU_DTYPE), TN), dvn)
                dw_ref[g] += _dot(dm_g.astype(MXU_DTYPE), vn[rs], NT)
                dbt_ref[:, g:g + 1] += jnp.sum(dm_g, axis=-1, keepdims=True)
            dvn_parts.append(dvn)
        dvn = jnp.concatenate(dvn_parts, axis=0)
        dg_ref[...] += jnp.sum(dvn * xhat, axis=0, keepdims=True)
        dbb_ref[...] += jnp.sum(dvn, axis=0, keepdims=True)
        dxh = dvn * g_ref[...]
        dv = rstd * (dxh - _group_mean(dxh, avg) - xhat * _group_mean(dxh * xhat, avg))
        dz_ref[:, SGU_W:] = (dv * dgelu[:, SGU_W:]).astype(dz_ref.dtype)

    full = lambda shape: pl.BlockSpec(shape, lambda i: tuple(0 for _ in shape))
    return pl.pallas_call(
        body, name="sgu_bwd", grid=(s // SGU_ROWS,),
        in_specs=[pl.BlockSpec((SGU_ROWS, 2 * SGU_W), lambda i: (i, 0)), full((1, SGU_W)), full((1, SGU_W)),
                  full((SGU_G, SGU_CHUNK, SGU_CHUNK)), full((SGU_CHUNK, SGU_G)),
                  pl.BlockSpec((SGU_ROWS, SGU_W), lambda i: (i, 0))],
        out_specs=[pl.BlockSpec((SGU_ROWS, 2 * SGU_W), lambda i: (i, 0)), full((1, SGU_W)), full((1, SGU_W)),
                   full((SGU_G, SGU_CHUNK, SGU_CHUNK)), full((SGU_CHUNK, SGU_G))],
        out_shape=[jax.ShapeDtypeStruct((s, 2 * SGU_W), MXU_DTYPE), jax.ShapeDtypeStruct((1, SGU_W), F32),
                   jax.ShapeDtypeStruct((1, SGU_W), F32), jax.ShapeDtypeStruct((SGU_G, SGU_CHUNK, SGU_CHUNK), F32),
                   jax.ShapeDtypeStruct((SGU_CHUNK, SGU_G), F32)],
        compiler_params=_params(("arbitrary",)),
    )(zs, ln_g.reshape(1, SGU_W), ln_b.reshape(1, SGU_W), w_mask, b_t, dy)


def _ssm_discretize(a_re, a_im, log_dt, b_re, b_im):
    dt = jnp.exp(log_dt)[:, None]
    mag = jnp.exp(a_re * dt)
    ab_re = mag * jnp.cos(a_im * dt)
    ab_im = mag * jnp.sin(a_im * dt)
    den = a_re * a_re + a_im * a_im
    f_re = ((ab_re - 1.0) * a_re + ab_im * a_im) / den
    f_im = (ab_im * a_re - (ab_re - 1.0) * a_im) / den
    bb_re = f_re[:, :, None] * b_re - f_im[:, :, None] * b_im
    bb_im = f_re[:, :, None] * b_im + f_im[:, :, None] * b_re
    return ab_re, ab_im, bb_re, bb_im


def _ssm_operands(a_re, a_im, log_dt, b_re, b_im, c_re, c_im):
    ab_re, ab_im, bb_re, bb_im = _ssm_discretize(a_re, a_im, log_dt, b_re, b_im)
    eye = jnp.eye(SSM_G, dtype=F32)
    b_blk = jnp.einsum("pgnc,gh->gcphn", jnp.stack([bb_re, bb_im]), eye).reshape(SSM_W, 2 * NSTATE)
    c_mat = jnp.einsum("pgcn,gh->pgnhc", jnp.stack([c_re, -c_im]), eye).reshape(2 * NSTATE, SSM_W)
    a_row = jnp.stack([ab_re.reshape(NSTATE), ab_im.reshape(NSTATE)])
    p_re, p_im = a_row[0:1], a_row[1:2]
    while p_re.shape[0] < SSM_TSEG:
        l_re, l_im = p_re[-1:], p_im[-1:]
        p_re, p_im = (jnp.concatenate([p_re, p_re * l_re - p_im * l_im]),
                      jnp.concatenate([p_im, p_re * l_im + p_im * l_re]))
    p_tab = jnp.stack([p_re, p_im])
    return b_blk.astype(MXU_DTYPE), c_mat.astype(MXU_DTYPE), a_row, p_tab


def _lane_chunks():
    return [(lo, lo + SSM_LANE_CHUNK) for lo in range(0, NSTATE, SSM_LANE_CHUNK)]


def _seg_rows(j):
    return pl.ds(pl.multiple_of(j * SSM_NSEG, SSM_NSEG), SSM_NSEG)


def _to_segments(t):
    s, w = t.shape
    return t.reshape(s // SSM_TB, SSM_NSEG, SSM_TSEG, w).transpose(0, 2, 1, 3).reshape(s, w)


def _from_segments(t):
    s, w = t.shape
    return t.reshape(s // SSM_TB, SSM_TSEG, SSM_NSEG, w).transpose(0, 2, 1, 3).reshape(s, w)


def _ssm_local_scan(buf, a_ref, *, reverse):
    ends_re, ends_im = [], []
    for lo, hi in _lane_chunks():
        are = jnp.broadcast_to(a_ref[0:1, lo:hi], (SSM_NSEG, hi - lo))
        aim = jnp.broadcast_to(a_ref[1:2, lo:hi], (SSM_NSEG, hi - lo))
        if reverse:
            aim = -aim

        def step(jj, carry, lo=lo, hi=hi, are=are, aim=aim):
            xr, xi = carry
            j = (SSM_TSEG - 1 - jj) if reverse else jj
            tr = buf[_seg_rows(j), lo:hi]
            ti = buf[_seg_rows(j), NSTATE + lo:NSTATE + hi]
            nr = are * xr - aim * xi + tr
            ni = are * xi + aim * xr + ti
            buf[_seg_rows(j), lo:hi] = nr
            buf[_seg_rows(j), NSTATE + lo:NSTATE + hi] = ni
            return nr, ni

        zero = jnp.zeros((SSM_NSEG, hi - lo), F32)
        xr, xi = lax.fori_loop(0, SSM_TSEG, step, (zero, zero), unroll=4)
        ends_re.append(xr)
        ends_im.append(xi)
    return jnp.concatenate(ends_re, axis=1), jnp.concatenate(ends_im, axis=1)


def _ssm_entry_states(ends_re, ends_im, carry_ref, p_ref, entry_ref, *, reverse):
    at_re = p_ref[0, SSM_TSEG - 1:SSM_TSEG, :]
    at_im = p_ref[1, SSM_TSEG - 1:SSM_TSEG, :]
    if reverse:
        at_im = -at_im
    cur_re = carry_ref[0:1, 0:NSTATE]
    cur_im = carry_ref[0:1, NSTATE:2 * NSTATE]
    order = range(SSM_NSEG - 1, -1, -1) if reverse else range(SSM_NSEG)
    for i in order:
        entry_ref[0, i:i + 1, 0:NSTATE] = cur_re
        entry_ref[0, i:i + 1, NSTATE:2 * NSTATE] = cur_im
        nxt_re = ends_re[i:i + 1] + at_re * cur_re - at_im * cur_im
        nxt_im = ends_im[i:i + 1] + at_re * cur_im + at_im * cur_re
        cur_re, cur_im = nxt_re, nxt_im
    carry_ref[0:1, 0:NSTATE] = cur_re
    carry_ref[0:1, NSTATE:2 * NSTATE] = cur_im


def _ssm_fixup(buf, p_ref, entry_ref, *, reverse):
    for lo, hi in _lane_chunks():
        e_re = entry_ref[0, :, lo:hi]
        e_im = entry_ref[0, :, NSTATE + lo:NSTATE + hi]

        def step(j, carry, lo=lo, hi=hi, e_re=e_re, e_im=e_im):
            jp = (SSM_TSEG - 1 - j) if reverse else j
            pr = p_ref[0, pl.ds(jp, 1), lo:hi]
            pi = p_ref[1, pl.ds(jp, 1), lo:hi]
            if reverse:
                pi = -pi
            buf[_seg_rows(j), lo:hi] = buf[_seg_rows(j), lo:hi] + pr * e_re - pi * e_im
            buf[_seg_rows(j), NSTATE + lo:NSTATE + hi] = (buf[_seg_rows(j), NSTATE + lo:NSTATE + hi]
                                                           + pr * e_im + pi * e_re)
            return carry

        lax.fori_loop(0, SSM_TSEG, step, 0, unroll=4)


def _ssm_fwd(u, ops, d_skip, glu_w, glu_b):
    b_blk, c_mat, a_row, p_tab = ops
    s = u.shape[0]
    nblk = s // SSM_TB

    def body(u_ref, bb_ref, cm_ref, a_ref, p_ref, d_ref, gw_ref, gb_ref, y_ref, entry_ref, xbuf, carry):
        @pl.when(pl.program_id(0) == 0)
        def _():
            carry[...] = jnp.zeros_like(carry)

        uu = u_ref[...]
        xbuf[...] = _dotf(uu, bb_ref[...], NN)
        ends_re, ends_im = _ssm_local_scan(xbuf, a_ref, reverse=False)
        _ssm_entry_states(ends_re, ends_im, carry, p_ref, entry_ref, reverse=False)
        _ssm_fixup(xbuf, p_ref, entry_ref, reverse=False)
        y = _dotf(xbuf[...],cm_ref[...], NN) + d_ref[...] * uu
        y2 = _gelu(y)
        gate = jax.nn.sigmoid(_dot(y2.astype(MXU_DTYPE), gw_ref[...].astype(MXU_DTYPE), NN) + gb_ref[...])
        y_ref[...] = y2 * gate

    full = lambda shape: pl.BlockSpec(shape, lambda i: tuple(0 for _ in shape))
    y_seg, entry = pl.pallas_call(
        body, name="ssm_fwd", grid=(nblk,),
        in_specs=[pl.BlockSpec((SSM_TB, SSM_W), lambda i: (i, 0)), full(b_blk.shape), full(c_mat.shape),
                  full(a_row.shape), full(p_tab.shape), full((1, SSM_W)), full((SSM_W, SSM_W)), full((1, SSM_W))],
        out_specs=[pl.BlockSpec((SSM_TB, SSM_W), lambda i: (i, 0)),
                   pl.BlockSpec((1, SSM_NSEG, 2 * NSTATE), lambda i: (i, 0, 0))],
        out_shape=[jax.ShapeDtypeStruct((s, SSM_W), F32), jax.ShapeDtypeStruct((nblk, SSM_NSEG, 2 * NSTATE), F32)],
        scratch_shapes=[pltpu.VMEM((SSM_TB, 2 * NSTATE), F32), pltpu.VMEM((SSM_NSEG, 2 * NSTATE), F32)],
        compiler_params=_params(("arbitrary",)),
    )(_to_segments(u), b_blk, c_mat, a_row, p_tab, d_skip.reshape(1, SSM_W), glu_w, glu_b.reshape(1, SSM_W))
    return _from_segments(y_seg), entry


def _ssm_bwd(u, entry, ops, d_skip, glu_w, glu_b, dout):
    b_blk, c_mat, a_row, p_tab = ops
    s = u.shape[0]
    nblk = s // SSM_TB

    def body(u_ref, en_ref, bb_ref, cm_ref, a_ref, p_ref, d_ref, gw_ref, gb_ref, do_ref,
             du_ref, dbb_ref, dcm_ref, da_ref, dd_ref, dgw_ref, dgb_ref, xbuf, gbuf, gcarry, gentry):
        @pl.when(pl.program_id(0) == 0)
        def _():
            gcarry[...] = jnp.zeros_like(gcarry)
            for r in (dbb_ref, dcm_ref, da_ref, dd_ref, dgw_ref, dgb_ref):
                r[...] = jnp.zeros_like(r)

        uu = u_ref[...]
        xbuf[...] = _dotf(uu, bb_ref[...], NN)
        _ssm_local_scan(xbuf, a_ref, reverse=False)
        _ssm_fixup(xbuf, p_ref, en_ref, reverse=False)
        y = _dotf(xbuf[...],cm_ref[...], NN) + d_ref[...] * uu
        y2, dgelu = _gelu_pair(y)
        y2m = y2.astype(MXU_DTYPE)
        gwm = gw_ref[...].astype(MXU_DTYPE)
        gate = jax.nn.sigmoid(_dot(y2m, gwm, NN) + gb_ref[...])
        dout = do_ref[...]
        dpre = dout * y2 * gate * (1.0 - gate)
        dprem = dpre.astype(MXU_DTYPE)
        dy2 = dout * gate + _dot(dprem, gwm, NT)
        dgw_ref[...] += _dot(y2m, dprem, TN)
        dgb_ref[...] += jnp.sum(dpre, axis=0, keepdims=True)
        dy = dy2 * dgelu
        dd_ref[...] += jnp.sum(dy * uu, axis=0, keepdims=True)
        dcm_ref[...] += _dotf(xbuf[...],dy, TN)
        gbuf[...] = _dotf(dy, cm_ref[...], NT)
        gs_re, gs_im = _ssm_local_scan(gbuf, a_ref, reverse=True)
        _ssm_entry_states(gs_re, gs_im, gcarry, p_ref, gentry, reverse=True)
        _ssm_fixup(gbuf, p_ref, gentry, reverse=True)
        du_ref[...] = (_dotf(gbuf[...], bb_ref[...], NT) + d_ref[...] * dy).astype(du_ref.dtype)
        dbb_ref[...] += _dotf(uu, gbuf[...], TN)
        for lo, hi in _lane_chunks():
            def step(j, carry, lo=lo, hi=hi):
                acc_re, acc_im = carry
                g_re = gbuf[_seg_rows(j), lo:hi]
                g_im = gbuf[_seg_rows(j), NSTATE + lo:NSTATE + hi]
                x_re = xbuf[_seg_rows(j - 1), lo:hi]
                x_im = xbuf[_seg_rows(j - 1), NSTATE + lo:NSTATE + hi]
                return acc_re + g_re * x_re + g_im * x_im, acc_im + g_im * x_re - g_re * x_im

            g0_re = gbuf[_seg_rows(0), lo:hi]
            g0_im = gbuf[_seg_rows(0), NSTATE + lo:NSTATE + hi]
            e_re = en_ref[0, :, lo:hi]
            e_im = en_ref[0, :, NSTATE + lo:NSTATE + hi]
            init = (g0_re * e_re + g0_im * e_im, g0_im * e_re - g0_re * e_im)
            acc_re, acc_im = lax.fori_loop(1, SSM_TSEG, step, init, unroll=4)
            da_ref[0:1, lo:hi] += jnp.sum(acc_re, axis=0, keepdims=True)
            da_ref[1:2, lo:hi] += jnp.sum(acc_im, axis=0, keepdims=True)

    full = lambda shape: pl.BlockSpec(shape, lambda i: tuple(0 for _ in shape))
    rev = pl.BlockSpec((SSM_TB, SSM_W), lambda i: (nblk - 1 - i, 0))
    outs = pl.pallas_call(
        body, name="ssm_bwd", grid=(nblk,),
        in_specs=[rev, pl.BlockSpec((1, SSM_NSEG, 2 * NSTATE), lambda i: (nblk - 1 - i, 0, 0)),
                  full(b_blk.shape), full(c_mat.shape), full(a_row.shape), full(p_tab.shape),
                  full((1, SSM_W)), full((SSM_W, SSM_W)), full((1, SSM_W)), rev],
        out_specs=[rev, full(b_blk.shape), full(c_mat.shape), full(a_row.shape), full((1, SSM_W)),
                   full((SSM_W, SSM_W)), full((1, SSM_W))],
        out_shape=[jax.ShapeDtypeStruct((s, SSM_W), MXU_DTYPE), jax.ShapeDtypeStruct(b_blk.shape, F32),
                   jax.ShapeDtypeStruct(c_mat.shape, F32), jax.ShapeDtypeStruct(a_row.shape, F32),
                   jax.ShapeDtypeStruct((1, SSM_W), F32), jax.ShapeDtypeStruct((SSM_W, SSM_W), F32),
                   jax.ShapeDtypeStruct((1, SSM_W), F32)],
        scratch_shapes=[pltpu.VMEM((SSM_TB, 2 * NSTATE), F32), pltpu.VMEM((SSM_TB, 2 * NSTATE), F32),
                        pltpu.VMEM((SSM_NSEG, 2 * NSTATE), F32), pltpu.VMEM((1, SSM_NSEG, 2 * NSTATE), F32)],
        compiler_params=_params(("arbitrary",)),
    )(_to_segments(u), entry, b_blk, c_mat, a_row, p_tab, d_skip.reshape(1, SSM_W), glu_w, glu_b.reshape(1, SSM_W),
      _to_segments(dout))
    return (_from_segments(outs[0]),) + tuple(outs[1:])


MIX_SEGS = ((0, ATTN_W), (ATTN_W, ATTN_W + SGU_W), (ATTN_W + SGU_W, D_MODEL))


def _chunks_to_rows(a_ref):
    return jnp.concatenate([a_ref[ch] for ch in range(N_CHUNK)], axis=1)


def _mix_fwd(y_attn_c, y_sgu, y_ssm, gain):
    s = y_sgu.shape[0]

    def body(a_ref, b_ref, c_ref, g_ref, o_ref):
        for x, (lo, hi) in zip((_chunks_to_rows(a_ref), b_ref[...], c_ref[...]), MIX_SEGS):
            r = lax.rsqrt(jnp.mean(x * x, axis=-1, keepdims=True) + EPS)
            o_ref[:, lo:hi] = (x * r * g_ref[:, lo:hi]).astype(o_ref.dtype)

    row = lambda w: pl.BlockSpec((ROWS, w), lambda i: (i, 0))
    return pl.pallas_call(
        body, name="mix_fwd", grid=(s // ROWS,),
        in_specs=[pl.BlockSpec((N_CHUNK, ROWS, 128), lambda i: (0, i, 0)), row(SGU_W), row(SSM_W),
                  pl.BlockSpec((1, D_MODEL), lambda i: (0, 0))],
        out_specs=row(D_MODEL), out_shape=jax.ShapeDtypeStruct((s, D_MODEL), MXU_DTYPE),
        compiler_params=_params(("parallel",)),
    )(y_attn_c, y_sgu, y_ssm, gain.reshape(1, D_MODEL))


def _mix_bwd(y_attn_c, y_sgu, y_ssm, gain, dmix):
    s = y_sgu.shape[0]

    def body(a_ref, b_ref, c_ref, g_ref, dm_ref, da_ref, dl_ref, db_ref, dc_ref, dg_ref):
        @pl.when(pl.program_id(0) == 0)
        def _():
            dg_ref[...] = jnp.zeros_like(dg_ref)

        grads = []
        for x, (lo, hi) in zip((_chunks_to_rows(a_ref), b_ref[...], c_ref[...]), MIX_SEGS):
            r = lax.rsqrt(jnp.mean(x * x, axis=-1, keepdims=True) + EPS)
            xhat = x * r
            dm = dm_ref[:, lo:hi].astype(F32)
            dg_ref[:, lo:hi] += jnp.sum(dm * xhat, axis=0, keepdims=True)
            dxh = dm * g_ref[:, lo:hi]
            grads.append(r * (dxh - xhat * jnp.mean(dxh * xhat, axis=-1, keepdims=True)))
        db_ref[...] = grads[1]
        dc_ref[...] = grads[2]
        low = lax.broadcasted_iota(jnp.int32, (ROWS, 128), 1) < HEAD_DIM
        for ch in range(N_CHUNK):
            d_c = grads[0][:, 128 * ch:128 * (ch + 1)]
            da_ref[ch] = d_c.astype(da_ref.dtype)
            prod = d_c * a_ref[ch]
            dl_ref[ch] = jnp.where(low, jnp.sum(prod[:, :HEAD_DIM], axis=-1, keepdims=True),
                                   jnp.sum(prod[:, HEAD_DIM:], axis=-1, keepdims=True))

    row = lambda w: pl.BlockSpec((ROWS, w), lambda i: (i, 0))
    vec = pl.BlockSpec((1, D_MODEL), lambda i: (0, 0))
    chunked = pl.BlockSpec((N_CHUNK, ROWS, 128), lambda i: (0, i, 0))
    return pl.pallas_call(
        body, name="mix_bwd", grid=(s // ROWS,),
        in_specs=[chunked, row(SGU_W), row(SSM_W), vec, row(D_MODEL)],
        out_specs=[chunked, chunked, row(SGU_W), row(SSM_W), vec],
        out_shape=[jax.ShapeDtypeStruct((N_CHUNK, s, 128), ATTN_IO_DTYPE), jax.ShapeDtypeStruct((N_CHUNK, s, 128), F32),
                   jax.ShapeDtypeStruct((s, SGU_W), F32), jax.ShapeDtypeStruct((s, SSM_W), F32),
                   jax.ShapeDtypeStruct((1, D_MODEL), F32)],
        compiler_params=_params(("arbitrary",)),
    )(y_attn_c, y_sgu, y_ssm, gain.reshape(1, D_MODEL), dmix)


CONV_ROWS = 256
CONV_COLS = 1408
CONV_PAIR = 2 * CONV_COLS
HALO = 16


def _interleave_ff(t):
    lead = t.shape[:-1]
    nb = D_FF // CONV_COLS
    return jnp.swapaxes(t.reshape(lead + (2, nb, CONV_COLS)), -3, -2).reshape(lead + (2 * D_FF,))


def _deinterleave_ff(t):
    lead = t.shape[:-1]
    nb = D_FF // CONV_COLS
    return jnp.swapaxes(t.reshape(lead + (nb, 2, CONV_COLS)), -3, -2).reshape(lead + (2 * D_FF,))


def _conv_in_specs():
    halo_idx = lambda i: jnp.maximum(i * (CONV_ROWS // HALO) - 1, 0)
    return [pl.BlockSpec((CONV_ROWS, CONV_PAIR), lambda j, i: (i, j)),
            pl.BlockSpec((HALO, CONV_PAIR), lambda j, i: (halo_idx(i), j)),
            pl.BlockSpec((3, CONV_PAIR), lambda j, i: (0, j)),
            pl.BlockSpec((1, CONV_PAIR), lambda j, i: (0, j))]


def _shift_matrix(rows, back):
    r = lax.broadcasted_iota(jnp.int32, (2 * rows, rows), 0)
    c = lax.broadcasted_iota(jnp.int32, (2 * rows, rows), 1)
    step = jnp.where(r < rows, 1, 2)
    t = jnp.where(r < rows, r, r - rows)
    src = t - step if back else t + step
    return jnp.where(c == src, 1.0, 0.0).astype(MXU_DTYPE)


def _patch_rows(x, at_end, rows):
    tile = 8
    n = x.shape[0]
    idx = lax.broadcasted_iota(jnp.int32, (tile, x.shape[1]), 0)
    piece = x[n - tile:] if at_end else x[:tile]
    for k, row in enumerate(rows):
        where_row = (tile - len(rows) + k) if at_end else k
        piece = jnp.where(idx == where_row, row, piece)
    return jnp.concatenate([x[:n - tile], piece], axis=0) if at_end else jnp.concatenate([piece, x[tile:]], axis=0)


def _mxu_taps(main_m, halo, first):
    shifted = _dot(_shift_matrix(main_m.shape[0], True), main_m, NN)
    h1 = jnp.where(first, 0.0, halo[HALO - 1:HALO, :])
    h2 = jnp.where(first, 0.0, halo[HALO - 2:HALO - 1, :])
    x1 = _patch_rows(shifted[:main_m.shape[0]], False, [h1])
    x2 = _patch_rows(shifted[main_m.shape[0]:], False, [h2, h1])
    return x1, x2


def _conv_gate(w_ref, b_ref, x2, x1, x0):
    return w_ref[0:1, :] * x2 + w_ref[1:2, :] * x1 + w_ref[2:3, :] * x0 + b_ref[...]


def _ffn_gate_fwd(hh, conv_w, conv_b):
    s = hh.shape[0]

    def body(m_ref, h_ref, w_ref, b_ref, o_ref):
        first = pl.program_id(1) == 0
        main_m = m_ref[...]
        x1, x2 = _mxu_taps(main_m, h_ref[...].astype(F32), first)
        conv = _conv_gate(w_ref, b_ref, x2, x1, main_m.astype(F32))
        o_ref[...] = (_gelu(conv[:, CONV_COLS:]) * conv[:, :CONV_COLS]).astype(o_ref.dtype)

    return pl.pallas_call(
        body, name="ffn_act_fwd", grid=(D_FF // CONV_COLS, s // CONV_ROWS), in_specs=_conv_in_specs(),
        out_specs=pl.BlockSpec((CONV_ROWS, CONV_COLS), lambda j, i: (i, j)),
        out_shape=jax.ShapeDtypeStruct((s, D_FF), MXU_DTYPE),
        compiler_params=_params(("parallel", "parallel")),
    )(hh, hh, conv_w, conv_b.reshape(1, -1))


def _ffn_gate_bwd(hh, conv_w, conv_b, da):
    s = hh.shape[0]
    nrow = s // CONV_ROWS

    def gate_grad(conv, da):
        act, dact = _gelu_pair(conv[:, CONV_COLS:])
        return jnp.concatenate([da * act, da * conv[:, :CONV_COLS] * dact], axis=1)

    def body(m_ref, h_ref, w_ref, b_ref, nx_ref, da_ref, dan_ref, o_ref, dw_ref, db_ref):
        first = pl.program_id(1) == 0
        last = pl.program_id(1) == nrow - 1

        @pl.when(first)
        def _():
            dw_ref[...] = jnp.zeros_like(dw_ref)
            db_ref[...] = jnp.zeros_like(db_ref)

        main_m = m_ref[...]
        main = main_m.astype(F32)
        x1, x2 = _mxu_taps(main_m, h_ref[...].astype(F32), first)
        dconv = gate_grad(_conv_gate(w_ref, b_ref, x2, x1, main), da_ref[...].astype(F32))
        nx = nx_ref[...].astype(F32)
        nx1 = _patch_rows(pltpu.roll(nx, 1, 0), False, [main[CONV_ROWS - 1:]])
        nx2 = _patch_rows(pltpu.roll(nx, 2, 0), False, [main[CONV_ROWS - 2:CONV_ROWS - 1], main[CONV_ROWS - 1:]])
        dnext = gate_grad(_conv_gate(w_ref, b_ref, nx2, nx1, nx), jnp.where(last, 0.0, dan_ref[...].astype(F32)))
        dnext = dnext.astype(MXU_DTYPE).astype(F32)
        ahead = _dot(_shift_matrix(CONV_ROWS, False), dconv.astype(MXU_DTYPE), NN)
        ahead1 = _patch_rows(ahead[:CONV_ROWS], True, [dnext[0:1]])
        ahead2 = _patch_rows(ahead[CONV_ROWS:], True, [dnext[0:1], dnext[1:2]])
        o_ref[...] = (w_ref[2:3, :] * dconv + w_ref[1:2, :] * ahead1 + w_ref[0:1, :] * ahead2).astype(o_ref.dtype)
        for t, tap in enumerate((x2, x1, main)):
            dw_ref[t:t + 1, :] += jnp.sum(dconv * tap, axis=0, keepdims=True)
        db_ref[...] += jnp.sum(dconv, axis=0, keepdims=True)

    nxt = lambda i: jnp.minimum((i + 1) * (CONV_ROWS // HALO), s // HALO - 1)
    return pl.pallas_call(
        body, name="ffn_act_bwd", grid=(D_FF // CONV_COLS, nrow),
        in_specs=_conv_in_specs() + [pl.BlockSpec((HALO, CONV_PAIR), lambda j, i: (nxt(i), j)),
                                     pl.BlockSpec((CONV_ROWS, CONV_COLS), lambda j, i: (i, j)),
                                     pl.BlockSpec((HALO, CONV_COLS), lambda j, i: (nxt(i), j))],
        out_specs=[pl.BlockSpec((CONV_ROWS, CONV_PAIR), lambda j, i: (i, j)),
                   pl.BlockSpec((3, CONV_PAIR), lambda j, i: (0, j)), pl.BlockSpec((1, CONV_PAIR), lambda j, i: (0, j))],
        out_shape=[jax.ShapeDtypeStruct((s, 2 * D_FF), MXU_DTYPE), jax.ShapeDtypeStruct((3, 2 * D_FF), F32),
                   jax.ShapeDtypeStruct((1, 2 * D_FF), F32)],
        compiler_params=_params(("parallel", "arbitrary")),
    )(hh, hh, conv_w, conv_b.reshape(1, -1), hh, da, da)


def _ple_weight_specs(layer):
    return [pl.BlockSpec((None, D_MODEL, D_MODEL), lambda i: (layer, 0, 0)),
            pl.BlockSpec((None, PLE_DIM, D_MODEL), lambda i: (layer, 0, 0))]


def _ple_fwd(h, gain, p, w_gate, w_proj, layer):
    s = h.shape[0]
    tm = 512

    def body(h_ref, g_ref, p_ref, wg_ref, wp_ref, o_ref, xn_ref):
        x = h_ref[...]
        xn = _rms_rows(x, g_ref[...])
        xn_ref[...] = xn
        gate = jax.nn.sigmoid(_dot(xn, wg_ref[...].astype(MXU_DTYPE), NN))
        proj = _dot(p_ref[...].astype(MXU_DTYPE), wp_ref[...].astype(MXU_DTYPE), NN)
        o_ref[...] = x + gate * proj

    row = pl.BlockSpec((tm, D_MODEL), lambda i: (i, 0))
    return pl.pallas_call(
        body, name="ple_fwd", grid=(s // tm,),
        in_specs=[row, pl.BlockSpec((1, D_MODEL), lambda i: (0, 0)), pl.BlockSpec((tm, PLE_DIM), lambda i: (i, 0))]
        + _ple_weight_specs(layer),
        out_specs=[row, row],
        out_shape=[jax.ShapeDtypeStruct((s, D_MODEL), F32), jax.ShapeDtypeStruct((s, D_MODEL), MXU_DTYPE)],
        compiler_params=_params(("parallel",)),
    )(h, gain.reshape(1, D_MODEL), p, w_gate, w_proj)


def _ple_bwd(xn, p, w_gate, w_proj, dh, layer):
    s = xn.shape[0]
    tm = 512

    def body(x_ref, p_ref, wg_ref, wp_ref, dh_ref, dpre_ref, dproj_ref):
        gate = jax.nn.sigmoid(_dot(x_ref[...].astype(MXU_DTYPE), wg_ref[...].astype(MXU_DTYPE), NN))
        proj = _dot(p_ref[...].astype(MXU_DTYPE), wp_ref[...].astype(MXU_DTYPE), NN)
        dh = dh_ref[...]
        dpre_ref[...] = (dh * proj * gate * (1.0 - gate)).astype(dpre_ref.dtype)
        dproj_ref[...] = (dh * gate).astype(dproj_ref.dtype)

    row = pl.BlockSpec((tm, D_MODEL), lambda i: (i, 0))
    return pl.pallas_call(
        body, name="ple_bwd", grid=(s // tm,),
        in_specs=[row, pl.BlockSpec((tm, PLE_DIM), lambda i: (i, 0))] + _ple_weight_specs(layer) + [row],
        out_specs=[row, row],
        out_shape=[jax.ShapeDtypeStruct((s, D_MODEL), MXU_DTYPE)] * 2,
        compiler_params=_params(("parallel",)),
    )(xn, p, w_gate, w_proj, dh)


O_SGU = 3 * ATTN_W
O_SSM = O_SGU + 2 * SGU_W


def _layer_consts(w, i):
    causal = jnp.asarray(np.tril(np.ones((SGU_CHUNK, SGU_CHUNK), np.float32)))
    return {
        "sgu_w_mask": w["sgu_w"][i] * causal,
        "sgu_b_t": w["sgu_b"][i].T,
        "ssm_ops": _ssm_operands(w["ssm_a_re"][i], w["ssm_a_im"][i], w["ssm_log_dt"][i], w["ssm_b_re"][i],
                                 w["ssm_b_im"][i], w["ssm_c_re"][i], w["ssm_c_im"][i]),
    }


def _layer_fwd(h0, p_i, w, i, bias):
    c = _layer_consts(w, i)
    xn1, qkv, zs, us = _in_proj(h0, w["norm_attn_g"][i], w["w_in"], i)
    y_attn, lse = _attn2_fwd(qkv, bias)
    y_sgu = _sgu2_fwd(zs, w["sgu_ln_g"][i], w["sgu_ln_b"][i], c["sgu_w_mask"], c["sgu_b_t"])
    y_ssm, entry = _ssm_fwd(us, c["ssm_ops"], w["ssm_d"][i], w["ssm_glu_w"][i], w["ssm_glu_b"][i])
    mix = _mix_fwd(y_attn, y_sgu, y_ssm, w["branch_norm_g"][i])
    h1 = _matmul(mix, w["w_out"], name="out_proj", out_dtype=F32, tm=512, tn=1024, residual=h0, layer=i)
    xn2, hh = _ffn_up(h1, w["norm_ffn_g"][i], w["ffn_w_up"], i)
    act = _ffn_gate_fwd(hh, w["ffn_conv_w"][i], w["ffn_conv_b"][i])
    h2 = _matmul(act, w["ffn_w_down"], name="ffn_down", out_dtype=F32, tm=512, tn=1024, residual=h1, layer=i)
    h3, xn3 = _ple_fwd(h2, w["norm_ple_g"][i], p_i, w["ple_w_gate"], w["ple_w_proj"], i)
    saved = dict(h0=h0, xn1=xn1, qkv=qkv, zs=zs, us=us, y_attn=y_attn, lse=lse, y_sgu=y_sgu, y_ssm=y_ssm,
                 entry=entry, mix=mix, h1=h1, xn2=xn2, hh=hh, act=act, h2=h2, xn3=xn3, consts=c)
    return h3, saved


def _layer_bwd(dh3, sv, p_i, w, i, bias):
    c = sv["consts"]
    g = {}
    dpre, dproj = _ple_bwd(sv["xn3"], p_i, w["ple_w_gate"], w["ple_w_proj"], dh3, i)
    g["ple_w_gate"] = _matmul_tn(sv["xn3"], dpre, name="d_ple_w_gate", tk=1024, tn=1024)
    g["ple_w_proj"] = _matmul_tn(p_i, dproj, name="d_ple_w_proj", tk=256, tn=1024)
    dh2, g["norm_ple_g"] = _matmul_rms_bwd(dpre, w["ple_w_gate"], sv["h2"], w["norm_ple_g"][i], dh3,
                                           name="d_xn_ple", layer=i, tm=512)
    g["ffn_w_down"] = _matmul_tn(sv["act"], dh2, name="d_ffn_w_down", tk=1408, tn=1024)
    dact = _matmul(dh2, w["ffn_w_down"], name="d_ffn_act", out_dtype=MXU_DTYPE, tm=512, tn=1408, trans_b=True, layer=i)
    dhh, g["ffn_conv_w"], g["ffn_conv_b"] = _ffn_gate_bwd(sv["hh"], w["ffn_conv_w"][i], w["ffn_conv_b"][i], dact)
    g["ffn_w_up"] = _matmul_tn(sv["xn2"], dhh, name="d_ffn_w_up", tk=1024, tn=1408)
    dh1, g["norm_ffn_g"] = _matmul_rms_bwd(dhh, w["ffn_w_up"], sv["h1"], w["norm_ffn_g"][i], dh2,
                                           name="d_xn_ffn", layer=i, tm=256)
    g["w_out"] = _matmul_tn(sv["mix"], dh1, name="d_w_out", tk=1024, tn=1024)
    dmix = _matmul(dh1, w["w_out"], name="d_mix", out_dtype=F32, tm=512, tn=1024, trans_b=True, layer=i)
    dy_attn, delta, dy_sgu, dy_ssm, g["branch_norm_g"] = _mix_bwd(sv["y_attn"], sv["y_sgu"], sv["y_ssm"],
                                                                  w["branch_norm_g"][i], dmix)
    dq, dk, dv, ek, ev, dbias = _attn2_bwd(sv["qkv"], bias, sv["lse"], delta, dy_attn)
    dzs, g["sgu_ln_g"], g["sgu_ln_b"], dsw, dsb = _sgu2_bwd(sv["zs"], w["sgu_ln_g"][i], w["sgu_ln_b"][i],
                                                          c["sgu_w_mask"], c["sgu_b_t"], dy_sgu)
    causal = jnp.asarray(np.tril(np.ones((SGU_CHUNK, SGU_CHUNK), np.float32)))
    g["sgu_w"] = dsw * causal
    g["sgu_b"] = dsb.T
    dus, dbb, dcm, da, g["ssm_d"], g["ssm_glu_w"], g["ssm_glu_b"] = _ssm_bwd(
        sv["us"], sv["entry"], c["ssm_ops"], w["ssm_d"][i], w["ssm_glu_w"][i], w["ssm_glu_b"][i], dy_ssm)
    dbb5 = dbb.reshape(SSM_G, SSM_C, 2, SSM_G, SSM_N)
    dbbar = jnp.einsum("gcpgn->pgnc", dbb5)
    dcm5 = dcm.reshape(2, SSM_G, SSM_N, SSM_G, SSM_C)
    dcc = jnp.einsum("pgngc->pgcn", dcm5)
    g["ssm_c_re"] = dcc[0]
    g["ssm_c_im"] = -dcc[1]
    da2 = da.reshape(2, SSM_G, SSM_N)
    _, vjp = jax.vjp(_ssm_discretize, w["ssm_a_re"][i], w["ssm_a_im"][i], w["ssm_log_dt"][i],
                     w["ssm_b_re"][i], w["ssm_b_im"][i])
    (g["ssm_a_re"], g["ssm_a_im"], g["ssm_log_dt"], g["ssm_b_re"], g["ssm_b_im"]) = vjp(
        (da2[0], da2[1], dbbar[0], dbbar[1]))
    dz = _attn2_bwd_sum(dq, dk, dv, ek, ev, dzs, dus)
    g["w_in"] = _matmul_tn(sv["xn1"], dz, name="d_w_in", tk=1024, tn=1152)
    dh0, g["norm_attn_g"] = _matmul_rms_bwd(dz, w["w_in"], sv["h0"], w["norm_attn_g"][i], dh1,
                                            name="d_xn_attn", layer=i, tm=512)
    for k in ("norm_ple_g", "norm_ffn_g", "branch_norm_g", "norm_attn_g", "sgu_ln_g", "sgu_ln_b", "ssm_d",
              "ssm_glu_b", "ffn_conv_b"):
        g[k] = g[k].reshape(-1)
    return dh0, g, dbias


def _local_step(x, p, target, w, ff_interleaved=False):
    ff_names = ("ffn_conv_b",) if ff_interleaved else FF_SHARDED + ("ffn_conv_b",)
    w = dict(w)
    for k in ff_names:
        w[k] = _interleave_ff(w[k])
    bias = _bias_build(w["rel_bias"])
    h = x
    saved = []
    for i in range(DEPTH):
        h, sv = _layer_fwd(h, p[i], w, i, bias)
        saved.append(sv)
    loss, dh, dgf = _loss_head(h, w["final_norm_g"], target)
    layer_grads = [None] * DEPTH
    dbias = None
    for i in reversed(range(DEPTH)):
        dh, layer_grads[i], db = _layer_bwd(dh, saved[i], p[i], w, i, bias)
        dbias = db if dbias is None else dbias + db
    grads = {k: jnp.stack([layer_grads[i][k] for i in range(DEPTH)]) for k in layer_grads[0]}
    for k in ff_names:
        grads[k] = _deinterleave_ff(grads[k])
    grads["rel_bias"] = _bias_reduce(dbias)
    grads["final_norm_g"] = dgf.reshape(-1)
    return loss, dh, grads


def _as_rows(a, rows=None):
    size = int(np.prod(a.shape))
    if rows is None:
        rows = -(-size // (16 * PACK_COLS)) * 16
    if size % PACK_COLS:
        a = jnp.pad(a.reshape(-1), (0, (-size) % PACK_COLS))
    a2 = a.reshape(-1, PACK_COLS)
    return jnp.pad(a2, ((0, rows - a2.shape[0]), (0, 0)))


def _shard_shape(name):
    full, ax = BIG_FULL[name]
    shp = [DEPTH] + list(full)
    shp[ax] //= N_CHIPS
    return tuple(shp)


EXACT_NAMES = ("ffn_conv_w",)


def _pack_rows_of(name):
    n = int(np.prod(_shard_shape(name))) * (2 if name in EXACT_NAMES else 1)
    rows = -(-n // PACK_COLS)
    return -(-rows // 16) * 16


def _pack_shards(shards, dtype, exact=False):
    split_words = exact and jnp.dtype(dtype).itemsize == 2
    parts = []
    for n in BIG_NAMES:
        a = shards[n]
        if split_words and n in EXACT_NAMES:
            a = lax.bitcast_convert_type(a.astype(F32), dtype)
        parts.append(_as_rows(a.astype(dtype), _pack_rows_of(n)))
    used = sum(pt.shape[0] for pt in parts)
    parts.append(jnp.zeros((PACK_ROWS - used, PACK_COLS), dtype))
    return jnp.concatenate(parts, axis=0)


def _unpack_shard(flat, name, exact=False):
    off = 0
    for n in BIG_NAMES:
        if n == name:
            break
        off += _pack_rows_of(n)
    shp = _shard_shape(name)
    cnt = int(np.prod(shp))
    if exact and name in EXACT_NAMES and jnp.dtype(flat.dtype).itemsize == 2:
        vec = flat[off:off + _pack_rows_of(name)].reshape(-1)
        return lax.bitcast_convert_type(vec[:2 * cnt].reshape(shp + (2,)), F32)
    if cnt % PACK_COLS == 0:
        return flat[off:off + cnt // PACK_COLS].reshape(shp)
    return flat[off:off + _pack_rows_of(name)].reshape(-1)[:cnt].reshape(shp)


FF_SHARDED = ("ffn_w_up", "ffn_conv_w")
FF_CHIP_ORDER = (0, 2, 1, 3)


def _chip_order(name):
    return FF_CHIP_ORDER if name in FF_SHARDED else tuple(range(N_CHIPS))


def _split_full(full, name):
    _, ax = BIG_FULL[name]
    parts = jnp.split(full, N_CHIPS, axis=ax)
    out = [None] * N_CHIPS
    for j, k in enumerate(_chip_order(name)):
        out[k] = parts[j]
    return out


def _join_shards(shards, name):
    _, ax = BIG_FULL[name]
    return jnp.concatenate([shards[k] for k in _chip_order(name)], axis=ax)


def _small_shapes(w):
    return [(n, w[n].shape) for n in SMALL_NAMES]


def _small_rows(shp):
    return -(-int(np.prod(shp)) // (8 * PACK_COLS)) * 8


def _pack_small(d):
    parts = [_as_rows(d[n].astype(F32), _small_rows(d[n].shape)) for n in SMALL_NAMES]
    used = sum(pt.shape[0] for pt in parts)
    parts.append(jnp.zeros((SMALL_ROWS - used, PACK_COLS), F32))
    return jnp.concatenate(parts, axis=0)


def _unpack_small(flat, shapes):
    out, off = {}, 0
    for n, shp in shapes:
        cnt = int(np.prod(shp))
        rows = _small_rows(shp)
        if cnt % PACK_COLS == 0:
            out[n] = flat[off:off + cnt // PACK_COLS].reshape(shp)
        else:
            out[n] = flat[off:off + rows].reshape(-1)[:cnt].reshape(shp)
        off += rows
    return out


MESH = pl.DeviceIdType.MESH
ANY = pl.BlockSpec(memory_space=pl.ANY)


def _me():
    return lax.axis_index("x"), lax.axis_index("y"), lax.axis_index("c")


def _other_chips(x, y):
    return [(1 - x, y), (x, 1 - y), (1 - x, 1 - y)]


def _gather_weights(wflat):
    def body(w_ref, out_ref, send_sems, recv_sems):
        x, y, c = _me()
        sibling = (x, y, 1 - c)
        chips = _other_chips(x, y)

        def rows(chip, half):
            return out_ref.at[2 * chip[0] + chip[1], pl.ds(half * PACK_HALF, PACK_HALF), :]

        def copy(k, chip, half, to, src=None):
            return pltpu.make_async_remote_copy(
                src_ref=rows(chip, half) if src is None else src, dst_ref=rows(chip, half),
                send_sem=send_sems.at[k], recv_sem=recv_sems.at[k], device_id=to, device_id_type=MESH)

        my_half = w_ref.at[pl.ds(c * PACK_HALF, PACK_HALF), :]
        first = [copy(j, (x, y), c, (*chip, c), src=my_half) for j, chip in enumerate(chips)]
        for cp in first:
            cp.start()
        passed = [copy(3 + j, chip, c, sibling) for j, chip in enumerate(chips)]
        for j, chip in enumerate(chips):
            copy(j, chip, c, (x, y, c)).wait_recv()
            passed[j].start()
        for j, chip in enumerate(chips):
            copy(3 + j, chip, 1 - c, (x, y, c)).wait_recv()
        for cp in first + passed:
            cp.wait_send()

    return pl.pallas_call(
        body, name="gather_weights", in_specs=[ANY], out_specs=ANY,
        out_shape=jax.ShapeDtypeStruct((N_CHIPS, PACK_ROWS, PACK_COLS), wflat.dtype),
        scratch_shapes=[pltpu.SemaphoreType.DMA((6,)), pltpu.SemaphoreType.DMA((6,))],
    )(wflat)


def _fill_own_shard(wall, wflat, chip_idx):
    rows = PACK_ROWS // 8

    def body(idx_ref, w_ref, wall_ref, o_ref):
        del idx_ref, wall_ref
        o_ref[...] = w_ref[...]

    return pl.pallas_call(
        body, name="fill_own_shard",
        grid_spec=pltpu.PrefetchScalarGridSpec(
            num_scalar_prefetch=1, grid=(PACK_ROWS // rows,),
            in_specs=[pl.BlockSpec((rows, PACK_COLS), lambda i, idx: (i, 0)), ANY],
            out_specs=pl.BlockSpec((None, rows, PACK_COLS), lambda i, idx: (idx[0], i, 0))),
        out_shape=jax.ShapeDtypeStruct(wall.shape, wall.dtype),
        input_output_aliases={2: 0},
        compiler_params=_params(("parallel",)),
    )(chip_idx, wflat, wall)


def _exchange_partials(gb, gs):
    def body(gb_ref, gs_ref, half_ref, small_ref, send_sems, recv_sems, local_sem):
        x, y, c = _me()
        me_idx = 4 * x + 2 * y + c
        mine = pltpu.make_async_copy(gs_ref, small_ref.at[me_idx], local_sem)
        mine.start()
        d2d = pltpu.make_async_remote_copy(
            src_ref=gb_ref.at[:, pl.ds((1 - c) * PACK_HALF, PACK_HALF), :], dst_ref=half_ref,
            send_sem=send_sems.at[0], recv_sem=recv_sems.at[0], device_id=(x, y, 1 - c), device_id_type=MESH)
        d2d.start()
        copies = []
        for k in range(1, N_DEV):
            fx, fy, fc = (k >> 2) & 1, (k >> 1) & 1, k & 1
            peer = (x ^ fx, y ^ fy, c ^ fc)
            copies.append(pltpu.make_async_remote_copy(
                src_ref=gs_ref, dst_ref=small_ref.at[me_idx], send_sem=send_sems.at[k], recv_sem=recv_sems.at[k],
                device_id=peer, device_id_type=MESH))
        for cp in copies:
            cp.start()
        for k in range(1, N_DEV):
            fx, fy, fc = (k >> 2) & 1, (k >> 1) & 1, k & 1
            peer_idx = 4 * (x ^ fx) + 2 * (y ^ fy) + (c ^ fc)
            pltpu.make_async_remote_copy(
                src_ref=gs_ref, dst_ref=small_ref.at[peer_idx], send_sem=send_sems.at[k], recv_sem=recv_sems.at[k],
                device_id=(x, y, c), device_id_type=MESH).wait_recv()
        d2d.wait_recv()
        d2d.wait_send()
        for cp in copies:
            cp.wait_send()
        mine.wait()

    return pl.pallas_call(
        body, name="exchange_partials", in_specs=[ANY, pl.BlockSpec(memory_space=pltpu.VMEM)], out_specs=[ANY, ANY],
        out_shape=[jax.ShapeDtypeStruct((N_CHIPS, PACK_HALF, PACK_COLS), gb.dtype),
                   jax.ShapeDtypeStruct((N_DEV, SMALL_ROWS, PACK_COLS), F32)],
        scratch_shapes=[pltpu.SemaphoreType.DMA((N_DEV,)), pltpu.SemaphoreType.DMA((N_DEV,)), pltpu.SemaphoreType.DMA],
    )(gb, gs)


RED_ROWS = 256


def _chip_partials(gb, sib, c_idx):
    nrow = PACK_HALF // RED_ROWS

    def body(c_ref, a_ref, b_ref, o_ref):
        del c_ref
        o_ref[...] = (a_ref[...].astype(F32) + b_ref[...].astype(F32)).astype(o_ref.dtype)

    blk = (1, RED_ROWS, PACK_COLS)
    return pl.pallas_call(
        body, name="chip_partials",
        grid_spec=pltpu.PrefetchScalarGridSpec(
            num_scalar_prefetch=1, grid=(N_CHIPS, nrow),
            in_specs=[pl.BlockSpec(blk, lambda k, i, c: (k, c[0] * nrow + i, 0)),
                      pl.BlockSpec(blk, lambda k, i, c: (k, i, 0))],
            out_specs=pl.BlockSpec(blk, lambda k, i, c: (k, i, 0))),
        out_shape=jax.ShapeDtypeStruct((N_CHIPS, PACK_HALF, PACK_COLS), gb.dtype),
        compiler_params=_params(("parallel", "parallel")),
    )(c_idx, gb, sib)


def _scatter_partials(pc):
    def body(pc_ref, out_ref, send_sems, recv_sems):
        x, y, c = _me()
        chips = _other_chips(x, y)
        copies = [pltpu.make_async_remote_copy(
            src_ref=pc_ref.at[2 * chip[0] + chip[1]], dst_ref=out_ref.at[k],
            send_sem=send_sems.at[k], recv_sem=recv_sems.at[k], device_id=(*chip, c), device_id_type=MESH)
            for k, chip in enumerate(chips)]
        for cp in copies:
            cp.start()
        for cp in copies:
            cp.wait_recv()
        for cp in copies:
            cp.wait_send()

    return pl.pallas_call(
        body, name="scatter_partials", in_specs=[ANY], out_specs=ANY,
        out_shape=jax.ShapeDtypeStruct((3, PACK_HALF, PACK_COLS), pc.dtype),
        scratch_shapes=[pltpu.SemaphoreType.DMA((3,)), pltpu.SemaphoreType.DMA((3,))],
    )(pc)


def _final_half(gb, sib, recv, idx):
    nrow = PACK_HALF // RED_ROWS

    def body(idx_ref, a_ref, b_ref, r_ref, o_ref):
        del idx_ref
        acc = a_ref[0].astype(F32) + b_ref[0].astype(F32)
        for k in range(3):
            acc = acc + r_ref[k].astype(F32)
        o_ref[...] = acc

    return pl.pallas_call(
        body, name="final_half",
        grid_spec=pltpu.PrefetchScalarGridSpec(
            num_scalar_prefetch=1, grid=(nrow,),
            in_specs=[pl.BlockSpec((1, RED_ROWS, PACK_COLS), lambda i, idx: (idx[0], idx[1] * nrow + i, 0)),
                      pl.BlockSpec((1, RED_ROWS, PACK_COLS), lambda i, idx: (idx[0], i, 0)),
                      pl.BlockSpec((3, RED_ROWS, PACK_COLS), lambda i, idx: (0, i, 0))],
            out_specs=pl.BlockSpec((RED_ROWS, PACK_COLS), lambda i, idx: (i, 0))),
        out_shape=jax.ShapeDtypeStruct((PACK_HALF, PACK_COLS), F32),
        compiler_params=_params(("parallel",)),
    )(idx, gb, sib, recv)


def _share_halves(half):
    def body(h_ref, out_ref, send_sem, recv_sem):
        x, y, c = _me()
        cp = pltpu.make_async_remote_copy(src_ref=h_ref, dst_ref=out_ref, send_sem=send_sem, recv_sem=recv_sem,
                                          device_id=(x, y, 1 - c), device_id_type=MESH)
        cp.start()
        cp.wait_recv()
        cp.wait_send()

    return pl.pallas_call(
        body, name="share_halves", in_specs=[ANY], out_specs=ANY,
        out_shape=jax.ShapeDtypeStruct((PACK_HALF, PACK_COLS), F32),
        scratch_shapes=[pltpu.SemaphoreType.DMA, pltpu.SemaphoreType.DMA],
    )(half)


def _sum_small(allsmall):
    def body(a_ref, o_ref):
        acc = a_ref[0]
        for k in range(1, N_DEV):
            acc = acc + a_ref[k]
        o_ref[...] = acc

    tr = 96
    return pl.pallas_call(
        body, name="sum_small", grid=(SMALL_ROWS // tr,),
        in_specs=[pl.BlockSpec((N_DEV, tr, PACK_COLS), lambda i: (0, i, 0))],
        out_specs=pl.BlockSpec((tr, PACK_COLS), lambda i: (i, 0)),
        out_shape=jax.ShapeDtypeStruct((SMALL_ROWS, PACK_COLS), F32),
        compiler_params=_params(("parallel",)),
    )(allsmall)


def _adamw(w, g, m, v, *, name):
    shape = w.shape
    cols = shape[-1]
    as2 = lambda t: t.reshape(-1, cols)
    w2, g2, m2, v2 = as2(w), as2(g), as2(m), as2(v)
    rows = w2.shape[0]
    tr = rows
    if rows * cols * 4 > (1 << 20):
        tr = _tile(rows, max(8, (1 << 20) // (cols * 4) // 8 * 8), 8)

    def body(w_ref, g_ref, m_ref, v_ref, d_ref, mo_ref, vo_ref):
        gg = g_ref[...]
        mn = ADAM_B1 * m_ref[...] + (1.0 - ADAM_B1) * gg
        vn = ADAM_B2 * v_ref[...] + (1.0 - ADAM_B2) * (gg * gg)
        m_hat = mn / (1.0 - ADAM_B1 ** ADAM_STEP)
        v_hat = vn / (1.0 - ADAM_B2 ** ADAM_STEP)
        d_ref[...] = -ADAM_LR * (m_hat / (jnp.sqrt(v_hat) + ADAM_EPS) + ADAM_WD * w_ref[...])
        mo_ref[...] = mn
        vo_ref[...] = vn

    blk = pl.BlockSpec((tr, cols), lambda i: (i, 0))
    outs = pl.pallas_call(
        body, name=name, grid=(rows // tr,), in_specs=[blk] * 4, out_specs=[blk] * 3,
        out_shape=[jax.ShapeDtypeStruct((rows, cols), F32)] * 3,
        compiler_params=_params(("parallel",)),
    )(w2, g2, m2, v2)
    return tuple(t.reshape(shape) for t in outs)


def _adamw_many(ws, gs, ms, vs):
    n = len(ws)

    def body(*refs):
        for t in range(n):
            w_ref, g_ref, m_ref, v_ref = refs[t], refs[n + t], refs[2 * n + t], refs[3 * n + t]
            d_ref, mo_ref, vo_ref = refs[4 * n + t], refs[5 * n + t], refs[6 * n + t]
            gg = g_ref[...]
            mn = ADAM_B1 * m_ref[...] + (1.0 - ADAM_B1) * gg
            vn = ADAM_B2 * v_ref[...] + (1.0 - ADAM_B2) * (gg * gg)
            m_hat = mn / (1.0 - ADAM_B1 ** ADAM_STEP)
            v_hat = vn / (1.0 - ADAM_B2 ** ADAM_STEP)
            d_ref[...] = -ADAM_LR * (m_hat / (jnp.sqrt(v_hat) + ADAM_EPS) + ADAM_WD * w_ref[...])
            mo_ref[...] = mn
            vo_ref[...] = vn

    vmem = pl.BlockSpec(memory_space=pltpu.VMEM)
    outs = pl.pallas_call(
        body, name="adamw_small", in_specs=[vmem] * (4 * n), out_specs=[vmem] * (3 * n),
        out_shape=[jax.ShapeDtypeStruct(w.shape, F32) for w in ws] * 3,
        compiler_params=pltpu.CompilerParams(vmem_limit_bytes=VMEM_LIMIT_BYTES),
    )(*ws, *gs, *ms, *vs)
    return outs[:n], outs[n:2 * n], outs[2 * n:]


def kernel(x, p, rel_bias, norm_attn_g, w_in, sgu_ln_g, sgu_ln_b, sgu_w, sgu_b, ssm_a_re, ssm_a_im, ssm_log_dt, ssm_b_re, ssm_b_im, ssm_c_re, ssm_c_im, ssm_d, ssm_glu_w, ssm_glu_b, branch_norm_g, w_out, norm_ffn_g, ffn_w_up, ffn_conv_w, ffn_conv_b, ffn_w_down, norm_ple_g, ple_w_gate, ple_w_proj, final_norm_g, loss_target, m_rel_bias, m_norm_attn_g, m_w_in, m_sgu_ln_g, m_sgu_ln_b, m_sgu_w, m_sgu_b, m_ssm_a_re, m_ssm_a_im, m_ssm_log_dt, m_ssm_b_re, m_ssm_b_im, m_ssm_c_re, m_ssm_c_im, m_ssm_d, m_ssm_glu_w, m_ssm_glu_b, m_branch_norm_g, m_w_out, m_norm_ffn_g, m_ffn_w_up, m_ffn_conv_w, m_ffn_conv_b, m_ffn_w_down, m_norm_ple_g, m_ple_w_gate, m_ple_w_proj, m_final_norm_g, v_rel_bias, v_norm_attn_g, v_w_in, v_sgu_ln_g, v_sgu_ln_b, v_sgu_w, v_sgu_b, v_ssm_a_re, v_ssm_a_im, v_ssm_log_dt, v_ssm_b_re, v_ssm_b_im, v_ssm_c_re, v_ssm_c_im, v_ssm_d, v_ssm_glu_w, v_ssm_glu_b, v_branch_norm_g, v_w_out, v_norm_ffn_g, v_ffn_w_up, v_ffn_conv_w, v_ffn_conv_b, v_ffn_w_down, v_norm_ple_g, v_ple_w_gate, v_ple_w_proj, v_final_norm_g):
    args = dict(locals())
    wts = {n: args[n] for n in WEIGHT_NAMES}
    mom_m = {n: args["m_" + n] for n in WEIGHT_NAMES}
    mom_v = {n: args["v_" + n] for n in WEIGHT_NAMES}

    xi, yi, ci = _me()
    wflat = _pack_shards({n: wts[n] for n in BIG_NAMES}, MXU_DTYPE, exact=True)
    wall = _fill_own_shard(_gather_weights(wflat), wflat, jnp.stack([2 * xi + yi]).astype(jnp.int32))
    full = dict(wts)
    for n in BIG_NAMES:
        full[n] = _join_shards([_unpack_shard(wall[k], n, exact=True) for k in range(N_CHIPS)], n)
    full["ffn_conv_w"] = full["ffn_conv_w"].astype(F32)

    loss, dx, grads = _local_step(x[0], p[:, 0], loss_target[0], full, ff_interleaved=True)

    xi, yi, ci = _me()
    stacked = {n: _split_full(grads[n], n) for n in BIG_NAMES}
    gb = jnp.stack([_pack_shards({n: stacked[n][k] for n in BIG_NAMES}, MXU_DTYPE) for k in range(N_CHIPS)])
    gs = _pack_small(grads).at[SMALL_ROWS - 1, 0].set(loss[0, 0])
    sib, allsmall = _exchange_partials(gb, gs)
    pc = _chip_partials(gb, sib, jnp.stack([ci]).astype(jnp.int32))
    recv = _scatter_partials(pc)
    half = _final_half(gb, sib, recv, jnp.stack([2 * xi + yi, ci]).astype(jnp.int32))
    other = _share_halves(half)
    gflat = jnp.concatenate([jnp.where(ci == 0, half, other), jnp.where(ci == 0, other, half)], axis=0)
    small_sum = _sum_small(allsmall)
    loss = small_sum[SMALL_ROWS - 1, 0]
    gsmall = _unpack_small(small_sum, _small_shapes(wts))

    g_out, d_out, m_out, v_out = {}, {}, {}, {}
    for n in BIG_NAMES:
        g_out[n] = _unpack_shard(gflat, n)
        d_out[n], m_out[n], v_out[n] = _adamw(wts[n], g_out[n], mom_m[n], mom_v[n], name="adamw_" + n)
    d_sm, m_sm, v_sm = _adamw_many([wts[n] for n in SMALL_NAMES], [gsmall[n] for n in SMALL_NAMES],
                                   [mom_m[n] for n in SMALL_NAMES], [mom_v[n] for n in SMALL_NAMES])
    for t, n in enumerate(SMALL_NAMES):
        g_out[n], d_out[n], m_out[n], v_out[n] = gsmall[n], d_sm[t], m_sm[t], v_sm[t]

    return (loss, dx[None], *[g_out[n] for n in WEIGHT_NAMES], *[d_out[n] for n in WEIGHT_NAMES],
            *[m_out[n] for n in WEIGHT_NAMES], *[v_out[n] for n in WEIGHT_NAMES])
```

```python
import functools
import math

import numpy as np
import jax
import jax.numpy as jnp
from jax import lax
from jax.experimental import pallas as pl
from jax.experimental.pallas import tpu as pltpu

F32 = jnp.float32
MXU_DTYPE = jnp.bfloat16
VMEM_LIMIT_BYTES = 52 * 1024 * 1024

D_MODEL = 1024
DEPTH = 2
PLE_DIM = 256
HEAD_DIM = 64
N_HEADS = 8
ATTN_W = 512
QBLK = 128
BRANCH_DIL = (1, 4, 16)
N_BUCKETS = 32
REL_MAX_DIST = 2048
SGU_W = 256
SGU_G = 4
SGU_GW = 64
SGU_CHUNK = 128
SSM_W = 256
SSM_G = 16
SSM_C = 16
SSM_N = 64
NSTATE = SSM_G * SSM_N
D_FF = 2816
EPS = 1e-6
NEG_INF = -1e30
ATTN_SCALE = HEAD_DIM ** -0.5

ADAM_LR = 0.001
ADAM_B1 = 0.9
ADAM_B2 = 0.999
ADAM_EPS = 1e-08
ADAM_WD = 0.01
ADAM_STEP = 10

SSM_NSEG = 8
SSM_TSEG = 64
SSM_TB = SSM_NSEG * SSM_TSEG
SSM_LANE_CHUNK = 512

N_CHIPS = 4
N_DEV = 8

BIG_NAMES = ("w_in", "ssm_glu_w", "w_out", "ffn_w_up", "ffn_conv_w", "ffn_w_down", "ple_w_gate", "ple_w_proj")
BIG_FULL = {
    "w_in": ((D_MODEL, 2304), 2),
    "ssm_glu_w": ((SSM_W, SSM_W), 1),
    "w_out": ((D_MODEL, D_MODEL), 1),
    "ffn_w_up": ((D_MODEL, 2 * D_FF), 2),
    "ffn_conv_w": ((3, 2 * D_FF), 2),
    "ffn_w_down": ((D_FF, D_MODEL), 1),
    "ple_w_gate": ((D_MODEL, D_MODEL), 1),
    "ple_w_proj": ((PLE_DIM, D_MODEL), 2),
}
PACK_COLS = 1024
PACK_ROWS = 6656
PACK_HALF = PACK_ROWS // 2

SMALL_NAMES = ("rel_bias", "norm_attn_g", "sgu_ln_g", "sgu_ln_b", "sgu_w", "sgu_b", "ssm_a_re", "ssm_a_im",
               "ssm_log_dt", "ssm_b_re", "ssm_b_im", "ssm_c_re", "ssm_c_im", "ssm_d", "ssm_glu_b",
               "branch_norm_g", "norm_ffn_g", "ffn_conv_b", "norm_ple_g", "final_norm_g")
SMALL_ROWS = 384

WEIGHT_NAMES = ("rel_bias", "norm_attn_g", "w_in", "sgu_ln_g", "sgu_ln_b", "sgu_w", "sgu_b", "ssm_a_re", "ssm_a_im",
                "ssm_log_dt", "ssm_b_re", "ssm_b_im", "ssm_c_re", "ssm_c_im", "ssm_d", "ssm_glu_w", "ssm_glu_b",
                "branch_norm_g", "w_out", "norm_ffn_g", "ffn_w_up", "ffn_conv_w", "ffn_conv_b", "ffn_w_down",
                "norm_ple_g", "ple_w_gate", "ple_w_proj", "final_norm_g")


def _params(sem):
    return pltpu.CompilerParams(dimension_semantics=sem, vmem_limit_bytes=VMEM_LIMIT_BYTES)


def _tile(n, cap, mult=128):
    if n <= cap:
        return n
    best = None
    for t in range(mult, cap + 1, mult):
        if n % t == 0:
            best = t
    assert best is not None, (n, cap)
    return best


def _gelu(x):
    return 0.5 * x * (1.0 + jnp.tanh(0.7978845608028654 * (x + 0.044715 * x * x * x)))


def _gelu_pair(x):
    x2 = x * x
    t = jnp.tanh(0.7978845608028654 * x * (1.0 + 0.044715 * x2))
    half = 0.5 * (1.0 + t)
    return x * half, half + 0.5 * x * (1.0 - t * t) * (0.7978845608028654 + 3.0 * 0.044715 * 0.7978845608028654 * x2)


def _dot(a, b, dims):
    return lax.dot_general(a, b, (dims, ((), ())), preferred_element_type=F32)


def _dotf(a, b, dims):
    return _dot(a.astype(MXU_DTYPE), b.astype(MXU_DTYPE), dims)


NN = ((1,), (0,))
NT = ((1,), (1,))
TN = ((0,), (0,))


def _matmul(a, b, *, name, out_dtype, tm, tn, trans_b=False, residual=None, layer=None):
    m, k = a.shape
    n = b.shape[-2] if trans_b else b.shape[-1]
    tm = _tile(m, tm, 8)
    tn = _tile(n, tn)
    dims = NT if trans_b else NN
    lead = () if layer is None else (None,)
    lidx = () if layer is None else (layer,)

    def body(*refs):
        if residual is None:
            a_ref, b_ref, o_ref = refs
        else:
            a_ref, b_ref, r_ref, o_ref = refs
        acc = _dot(a_ref[...].astype(MXU_DTYPE), b_ref[...].astype(MXU_DTYPE), dims)
        if residual is not None:
            acc = acc + r_ref[...]
        o_ref[...] = acc.astype(o_ref.dtype)

    b_spec = (pl.BlockSpec(lead + (tn, k), lambda i, j: lidx + (j, 0)) if trans_b
              else pl.BlockSpec(lead + (k, tn), lambda i, j: lidx + (0, j)))
    in_specs = [pl.BlockSpec((tm, k), lambda i, j: (i, 0)), b_spec]
    args = [a, b]
    if residual is not None:
        in_specs.append(pl.BlockSpec((tm, tn), lambda i, j: (i, j)))
        args.append(residual)
    return pl.pallas_call(
        body, name=name, grid=(m // tm, n // tn), in_specs=in_specs,
        out_specs=pl.BlockSpec((tm, tn), lambda i, j: (i, j)),
        out_shape=jax.ShapeDtypeStruct((m, n), out_dtype),
        compiler_params=_params(("parallel", "parallel")),
    )(*args)


def _matmul_tn(a, g, *, name, tk, tn, tm=1024):
    m, k = a.shape
    n = g.shape[1]
    tk = _tile(k, tk)
    tn = _tile(n, tn)
    tm = _tile(m, tm, 8)

    def body(a_ref, g_ref, o_ref):
        @pl.when(pl.program_id(2) == 0)
        def _():
            o_ref[...] = jnp.zeros_like(o_ref)

        o_ref[...] += _dot(a_ref[...].astype(MXU_DTYPE), g_ref[...].astype(MXU_DTYPE), TN)

    return pl.pallas_call(
        body, name=name, grid=(k // tk, n // tn, m // tm),
        in_specs=[pl.BlockSpec((tm, tk), lambda i, j, s: (s, i)),
                  pl.BlockSpec((tm, tn), lambda i, j, s: (s, j))],
        out_specs=pl.BlockSpec((tk, tn), lambda i, j, s: (i, j)),
        out_shape=jax.ShapeDtypeStruct((k, n), F32),
        compiler_params=_params(("parallel", "parallel", "arbitrary")),
    )(a, g)


ROWS = 512


def _matmul_rms_bwd(a, b, h, g, dres, *, name, layer, tm):
    s, k = a.shape
    d = b.shape[-2]

    def body(a_ref, b_ref, h_ref, g_ref, dres_ref, dh_ref, dg_ref):
        @pl.when(pl.program_id(0) == 0)
        def _():
            dg_ref[...] = jnp.zeros_like(dg_ref)

        dxn = _dot(a_ref[...].astype(MXU_DTYPE), b_ref[...].astype(MXU_DTYPE), NT)
        x = h_ref[...]
        r = lax.rsqrt(jnp.mean(x * x, axis=-1, keepdims=True) + EPS)
        xhat = x * r
        dg_ref[...] += jnp.sum(dxn * xhat, axis=0, keepdims=True)
        dxh = dxn * g_ref[...]
        dh_ref[...] = dres_ref[...] + r * (dxh - xhat * jnp.mean(dxh * xhat, axis=-1, keepdims=True))

    row = pl.BlockSpec((tm, d), lambda i: (i, 0))
    vec = pl.BlockSpec((1, d), lambda i: (0, 0))
    return pl.pallas_call(
        body, name=name, grid=(s // tm,),
        in_specs=[pl.BlockSpec((tm, k), lambda i: (i, 0)), pl.BlockSpec((None, d, k), lambda i: (layer, 0, 0)),
                  row, vec, row],
        out_specs=[row, vec],
        out_shape=[jax.ShapeDtypeStruct((s, d), F32), jax.ShapeDtypeStruct((1, d), F32)],
        compiler_params=_params(("arbitrary",)),
    )(a, b, h, g.reshape(1, d), dres)


def _loss_head(h, g, target):
    s, d = h.shape

    def body(h_ref, g_ref, t_ref, loss_ref, dh_ref, dg_ref):
        @pl.when(pl.program_id(0) == 0)
        def _():
            loss_ref[...] = jnp.zeros_like(loss_ref)
            dg_ref[...] = jnp.zeros_like(dg_ref)

        x = h_ref[...]
        r = lax.rsqrt(jnp.mean(x * x, axis=-1, keepdims=True) + EPS)
        xhat = x * r
        err = xhat * g_ref[...] - t_ref[...]
        loss_ref[...] += 0.5 * jnp.sum(jnp.mean(err * err, axis=-1, keepdims=True), axis=0, keepdims=True)
        dy = err / d
        dg_ref[...] += jnp.sum(dy * xhat, axis=0, keepdims=True)
        dxh = dy * g_ref[...]
        dh_ref[...] = r * (dxh - xhat * jnp.mean(dxh * xhat, axis=-1, keepdims=True))

    row = pl.BlockSpec((ROWS, d), lambda i: (i, 0))
    vec = pl.BlockSpec((1, d), lambda i: (0, 0))
    one = pl.BlockSpec((1, 1), lambda i: (0, 0))
    return pl.pallas_call(
        body, name="loss_head", grid=(s // ROWS,), in_specs=[row, vec, row], out_specs=[one, row, vec],
        out_shape=[jax.ShapeDtypeStruct((1, 1), F32), jax.ShapeDtypeStruct((s, d), F32),
                   jax.ShapeDtypeStruct((1, d), F32)],
        compiler_params=_params(("arbitrary",)),
    )(h, g.reshape(1, d), target)


def _t5_bucket(dist):
    max_exact = N_BUCKETS // 2
    dd = np.maximum(dist, 0)
    large = max_exact + (np.log(np.maximum(dd, 1) / max_exact) / np.log(REL_MAX_DIST / max_exact)
                         * (N_BUCKETS - max_exact)).astype(np.int32)
    large = np.minimum(large, N_BUCKETS - 1)
    return np.where(dd < max_exact, dd, large).astype(np.int32)


def _bucket_table():
    qq = np.arange(QBLK)[:, None]
    kk = np.arange(QBLK)[None, :]
    out = np.zeros((len(BRANCH_DIL), 2, QBLK, QBLK), np.int32)
    for b, dil in enumerate(BRANCH_DIL):
        out[b, 0] = _t5_bucket((qq - kk + QBLK) * dil)
        out[b, 1] = _t5_bucket((qq - kk) * dil)
    return out


BIAS_TILE = 2 * QBLK


def _bias_build(rel_bias):
    idx = jnp.asarray(_bucket_table())

    def body(idx_ref, rb_ref, o_ref):
        ch = pl.program_id(1)
        row = lax.broadcasted_iota(jnp.int32, (QBLK, QBLK), 0)
        col = lax.broadcasted_iota(jnp.int32, (QBLK, QBLK), 1)
        for part in range(2):
            ids = idx_ref[0, 1 - part]
            valid = (col <= row) if part == 0 else (col >= row)
            for h in range(2):
                acc = jnp.zeros((QBLK, QBLK), F32)
                for b in range(N_BUCKETS):
                    acc = jnp.where(ids == b, rb_ref[b, 2 * ch + h], acc)
                o_ref[0, 0, QBLK * h:QBLK * (h + 1), QBLK * part:QBLK * (part + 1)] = jnp.where(valid, acc, NEG_INF)

    return pl.pallas_call(
        body, name="attn_bias_build", grid=(len(BRANCH_DIL), N_HEADS // 2),
        in_specs=[pl.BlockSpec((1, 2, QBLK, QBLK), lambda b, c: (b, 0, 0, 0)),
                  pl.BlockSpec(memory_space=pltpu.SMEM)],
        out_specs=pl.BlockSpec((1, 1, BIAS_TILE, BIAS_TILE), lambda b, c: (b, c, 0, 0)),
        out_shape=jax.ShapeDtypeStruct((len(BRANCH_DIL), N_HEADS // 2, BIAS_TILE, BIAS_TILE), F32),
        compiler_params=_params(("parallel", "parallel")),
    )(idx, rel_bias)


def _bias_reduce(dbias):
    idx = jnp.asarray(_bucket_table())
    nb = len(BRANCH_DIL)

    def body(idx_ref, d_ref, o_ref):
        def per_bucket(b, carry):
            for h in range(N_HEADS):
                tot = jnp.zeros((), F32)
                for br in range(nb):
                    for part in range(2):
                        tile = d_ref[br, h // 2, QBLK * (h % 2):QBLK * (h % 2 + 1), QBLK * part:QBLK * (part + 1)]
                        tot = tot + jnp.sum(jnp.where(idx_ref[br, 1 - part] == b, tile, 0.0))
                o_ref[b, h] = tot
            return carry

        lax.fori_loop(0, N_BUCKETS, per_bucket, 0)

    return pl.pallas_call(
        body, name="attn_bias_reduce",
        in_specs=[pl.BlockSpec(memory_space=pltpu.VMEM), pl.BlockSpec(memory_space=pltpu.VMEM)],
        out_specs=pl.BlockSpec(memory_space=pltpu.SMEM),
        out_shape=jax.ShapeDtypeStruct((N_BUCKETS, N_HEADS), F32),
        compiler_params=pltpu.CompilerParams(vmem_limit_bytes=VMEM_LIMIT_BYTES),
    )(idx, dbias)


ATTN_IO_DTYPE = F32
ABLK = 2048
N_CHUNK = ATTN_W // 128


def _rows(start, dil):
    if dil > 1:
        return pl.ds(start, QBLK, stride=dil)
    return pl.ds(pl.multiple_of(start, QBLK), QBLK)


def _low_head():
    return lax.broadcasted_iota(jnp.int32, (QBLK, 128), 1) < HEAD_DIM


def _head_split(t):
    low = _low_head()
    zero = jnp.zeros_like(t)
    return jnp.where(low, t, zero), jnp.where(low, zero, t)


def _tile_bias(b_ref, branch, first):
    bias = b_ref[branch]
    if first is None:
        return bias
    col = lax.broadcasted_iota(jnp.int32, (BIAS_TILE, BIAS_TILE), 1)
    return jnp.where(jnp.logical_and(first, col >= QBLK), NEG_INF, bias)


def _loop(n, fn):
    if n == 1:
        fn(jnp.int32(0), 0)
    elif n > 1:
        lax.fori_loop(0, n, fn, 0, unroll=4)


def _for_each_tile(tile, c):
    for branch, dil in enumerate(BRANCH_DIL):
        span = QBLK * dil

        def edge(r, carry, branch=branch, span=span):
            tile(branch, r, False, ABLK - span + r, c == 0)
            return carry

        def inner(t, carry, branch=branch, span=span, dil=dil):
            start = (1 + t // dil) * span + t % dil
            tile(branch, start, True, start - span, None)
            return carry

        _loop(dil, edge)
        _loop((ABLK // span - 1) * dil, inner)


def _attn_chunk_specs(nb):
    blk = (None, ABLK, 128)
    prev = lambda c: jnp.maximum(c - 1, 0)
    return [pl.BlockSpec(blk, lambda ch, c: (ch, c, 0)),
            pl.BlockSpec(blk, lambda ch, c: (N_CHUNK + ch, c, 0)),
            pl.BlockSpec(blk, lambda ch, c: (2 * N_CHUNK + ch, c, 0)),
            pl.BlockSpec(blk, lambda ch, c: (N_CHUNK + ch, prev(c), 0)),
            pl.BlockSpec(blk, lambda ch, c: (2 * N_CHUNK + ch, prev(c), 0)),
            pl.BlockSpec((len(BRANCH_DIL), None, BIAS_TILE, BIAS_TILE), lambda ch, c: (0, ch, 0, 0))]


def _rms_rows(x, g):
    r = lax.rsqrt(jnp.mean(x * x, axis=-1, keepdims=True) + EPS)
    return (x * r * g).astype(MXU_DTYPE)


def _in_proj(h, gain, w_in, layer):
    s, k = h.shape
    tm = 512
    nch = O_SGU // 128

    def body(h_ref, g_ref, w_ref, xn_ref, qkv_ref, zs_ref, us_ref):
        xn = _rms_rows(h_ref[...], g_ref[...])
        xn_ref[...] = xn
        acc = _dot(xn, w_ref[...].astype(MXU_DTYPE), NN)
        for j in range(nch):
            blk = acc[:, 128 * j:128 * (j + 1)]
            if j < N_CHUNK:
                blk = blk * ATTN_SCALE
            qkv_ref[j] = blk.astype(qkv_ref.dtype)
        zs_ref[...] = acc[:, O_SGU:O_SSM]
        us_ref[...] = acc[:, O_SSM:]

    n = w_in.shape[-1]
    return pl.pallas_call(
        body, name="in_proj", grid=(s // tm,),
        in_specs=[pl.BlockSpec((tm, k), lambda i: (i, 0)), pl.BlockSpec((1, k), lambda i: (0, 0)),
                  pl.BlockSpec((None, k, n), lambda i: (layer, 0, 0))],
        out_specs=[pl.BlockSpec((tm, k), lambda i: (i, 0)), pl.BlockSpec((nch, tm, 128), lambda i: (0, i, 0)),
                   pl.BlockSpec((tm, O_SSM - O_SGU), lambda i: (i, 0)), pl.BlockSpec((tm, n - O_SSM), lambda i: (i, 0))],
        out_shape=[jax.ShapeDtypeStruct((s, k), MXU_DTYPE), jax.ShapeDtypeStruct((nch, s, 128), ATTN_IO_DTYPE),
                   jax.ShapeDtypeStruct((s, O_SSM - O_SGU), F32), jax.ShapeDtypeStruct((s, n - O_SSM), F32)],
        compiler_params=_params(("parallel",)),
    )(h, gain.reshape(1, k), w_in)


def _ffn_up(h, gain, w_up, layer):
    s, k = h.shape
    n = w_up.shape[-1]
    tm, tn = 1024, CONV_COLS

    def body(h_ref, g_ref, w_ref, xn_ref, o_ref):
        @pl.when(pl.program_id(1) == 0)
        def _():
            xn_ref[...] = _rms_rows(h_ref[...], g_ref[...])

        o_ref[...] = _dot(xn_ref[...], w_ref[...].astype(MXU_DTYPE), NN).astype(o_ref.dtype)

    return pl.pallas_call(
        body, name="ffn_up", grid=(s // tm, n // tn),
        in_specs=[pl.BlockSpec((tm, k), lambda i, j: (i, 0)), pl.BlockSpec((1, k), lambda i, j: (0, 0)),
                  pl.BlockSpec((None, k, tn), lambda i, j: (layer, 0, j))],
        out_specs=[pl.BlockSpec((tm, k), lambda i, j: (i, 0)), pl.BlockSpec((tm, tn), lambda i, j: (i, j))],
        out_shape=[jax.ShapeDtypeStruct((s, k), MXU_DTYPE), jax.ShapeDtypeStruct((s, n), MXU_DTYPE)],
        compiler_params=_params(("parallel", "arbitrary")),
    )(h, gain.reshape(1, k), w_up)


def _attn2_fwd(qkv_c, bias):
    s = qkv_c.shape[1]
    nb = s // ABLK
    last = len(BRANCH_DIL) - 1

    def body(q_ref, kc_ref, vc_ref, kp_ref, vp_ref, b_ref, o_ref, l_ref, acc_s, m_s, l_s):
        low = _low_head()
        e_st = jnp.concatenate(_head_split(jnp.ones((QBLK, 128), MXU_DTYPE)) * 2, axis=0)

        def tile(branch, start, prev_in_block, pstart, first):
            dil = BRANCH_DIL[branch]
            rq, rp = _rows(start, dil), _rows(pstart, dil)
            k_ref, v_ref = (kc_ref, vc_ref) if prev_in_block else (kp_ref, vp_ref)
            q_st = jnp.concatenate(_head_split(q_ref[rq, :].astype(MXU_DTYPE)), axis=0)
            k_st = jnp.concatenate([kc_ref[rq, :].astype(MXU_DTYPE), k_ref[rp, :].astype(MXU_DTYPE)], axis=0)
            v_st = jnp.concatenate(_head_split(vc_ref[rq, :].astype(MXU_DTYPE))
                                   + _head_split(v_ref[rp, :].astype(MXU_DTYPE)), axis=0)
            sc = _dot(q_st, k_st, NT) + _tile_bias(b_ref, branch, first)
            m_new = jnp.max(sc, axis=-1, keepdims=True)
            if branch > 0:
                m_old2 = m_s[rq, :]
                m_old = jnp.concatenate([m_old2[:, 0:1], m_old2[:, HEAD_DIM:HEAD_DIM + 1]], axis=0)
                m_new = jnp.maximum(m_old, m_new)
                alpha = jnp.exp(m_old - m_new)
            p = jnp.exp(sc - m_new).astype(MXU_DTYPE)
            lhs = jnp.concatenate([p[:QBLK, :QBLK], p[QBLK:, :QBLK], p[:QBLK, QBLK:], p[QBLK:, QBLK:]], axis=1)
            acc2 = _dot(lhs, v_st, NN)
            sum2 = _dot(lhs, e_st, NN)
            m2 = jnp.where(low, m_new[:QBLK], m_new[QBLK:])
            if branch > 0:
                a2 = jnp.where(low, alpha[:QBLK], alpha[QBLK:])
                acc2 = acc2 + a2 * acc_s[rq, :]
                sum2 = sum2 + a2 * l_s[rq, :]
            if branch == last:
                o_ref[rq, :] = acc2 / sum2
                l_ref[rq, :] = m2 + jnp.log(sum2)
            else:
                acc_s[rq, :] = acc2
                m_s[rq, :] = m2
                l_s[rq, :] = sum2

        _for_each_tile(tile, pl.program_id(1))

    out_spec = pl.BlockSpec((None, ABLK, 128), lambda ch, c: (ch, c, 0))
    return pl.pallas_call(
        body, name="attn_fwd", grid=(N_CHUNK, nb), in_specs=_attn_chunk_specs(nb),
        out_specs=[out_spec, out_spec],
        out_shape=[jax.ShapeDtypeStruct((N_CHUNK, s, 128), F32)] * 2,
        scratch_shapes=[pltpu.VMEM((ABLK, 128), F32)] * 3,
        compiler_params=_params(("parallel", "arbitrary")),
    )(qkv_c, qkv_c, qkv_c, qkv_c, qkv_c, bias)


def _attn2_bwd(qkv_c, bias, lse_c, delta_c, do_c):
    s = qkv_c.shape[1]
    nb = s // ABLK
    nbr = len(BRANCH_DIL)

    def body(q_ref, kc_ref, vc_ref, kp_ref, vp_ref, b_ref, l_ref, dl_ref, do_ref,
             dq_ref, dk_ref, dv_ref, *rest):
        ek_refs, ev_refs, db_ref = rest[:nbr], rest[nbr:2 * nbr], rest[2 * nbr]
        c = pl.program_id(1)

        @pl.when(c == 0)
        def _():
            db_ref[...] = jnp.zeros_like(db_ref)

        for r in (dq_ref, dk_ref, dv_ref) + tuple(ek_refs) + tuple(ev_refs):
            r[...] = jnp.zeros_like(r)

        def tile(branch, start, prev_in_block, pstart, first):
            dil = BRANCH_DIL[branch]
            rq, rp = _rows(start, dil), _rows(pstart, dil)
            k_ref, v_ref = (kc_ref, vc_ref) if prev_in_block else (kp_ref, vp_ref)
            kc2 = kc_ref[rq, :].astype(MXU_DTYPE)
            kp2 = k_ref[rp, :].astype(MXU_DTYPE)
            q_st = jnp.concatenate(_head_split(q_ref[rq, :].astype(MXU_DTYPE)), axis=0)
            do_st = jnp.concatenate(_head_split(do_ref[rq, :].astype(MXU_DTYPE)), axis=0)
            k_st = jnp.concatenate([kc2, kp2], axis=0)
            v_st = jnp.concatenate([vc_ref[rq, :].astype(MXU_DTYPE), v_ref[rp, :].astype(MXU_DTYPE)], axis=0)
            kh_st = jnp.concatenate(_head_split(kc2) + _head_split(kp2), axis=0)
            lse2 = l_ref[rq, :]
            del2 = dl_ref[rq, :]
            lse_st = jnp.concatenate([lse2[:, 0:1], lse2[:, HEAD_DIM:HEAD_DIM + 1]], axis=0)
            del_st = jnp.concatenate([del2[:, 0:1], del2[:, HEAD_DIM:HEAD_DIM + 1]], axis=0)
            p = jnp.exp(_dot(q_st, k_st, NT) + _tile_bias(b_ref, branch, first) - lse_st)
            ds = p * (_dot(do_st, v_st, NT) - del_st)
            db_ref[branch] += ds
            ds = ds.astype(MXU_DTYPE)
            p = p.astype(MXU_DTYPE)
            lhs = jnp.concatenate([ds[:QBLK, :QBLK], ds[QBLK:, :QBLK], ds[:QBLK, QBLK:], ds[QBLK:, QBLK:]], axis=1)
            dk_st = _dot(ds, q_st, TN)
            dv_st = _dot(p, do_st, TN)
            dq_ref[rq, :] += _dot(lhs, kh_st, NN)
            dk_ref[rq, :] += dk_st[:QBLK]
            dv_ref[rq, :] += dv_st[:QBLK]
            if prev_in_block:
                dk_ref[rp, :] += dk_st[QBLK:]
                dv_ref[rp, :] += dv_st[QBLK:]
            else:
                ek_refs[branch][rq, :] = dk_st[QBLK:]
                ev_refs[branch][rq, :] = dv_st[QBLK:]

        _for_each_tile(tile, c)

    blk = pl.BlockSpec((None, ABLK, 128), lambda ch, c: (ch, c, 0))
    outs = pl.pallas_call(
        body, name="attn_bwd", grid=(N_CHUNK, nb), in_specs=_attn_chunk_specs(nb) + [blk, blk, blk],
        out_specs=[blk] * (3 + 2 * nbr) + [pl.BlockSpec((nbr, None, BIAS_TILE, BIAS_TILE), lambda ch, c: (0, ch, 0, 0))],
        out_shape=[jax.ShapeDtypeStruct((N_CHUNK, s, 128), F32)] * (3 + 2 * nbr)
        + [jax.ShapeDtypeStruct((nbr, N_HEADS // 2, BIAS_TILE, BIAS_TILE), F32)],
        compiler_params=_params(("arbitrary", "arbitrary")),
    )(qkv_c, qkv_c, qkv_c, qkv_c, qkv_c, bias, lse_c, delta_c, do_c)
    return outs[0], outs[1], outs[2], outs[3:3 + nbr], outs[3 + nbr:3 + 2 * nbr], outs[3 + 2 * nbr]


def _attn2_bwd_sum(dq, dk, dv, ek, ev, dzs, dus):
    s = dq.shape[1]
    nrb = s // QBLK
    per_blk = ABLK // QBLK
    nbr = len(BRANCH_DIL)

    def body(*refs):
        dq_ref, dk_ref, dv_ref = refs[:3]
        ek_refs, ev_refs = refs[3:3 + nbr], refs[3 + nbr:3 + 2 * nbr]
        dzs_ref, dus_ref, o_ref = refs[3 + 2 * nbr:]
        i = pl.program_id(0)
        dkt, dvt = dk_ref[...], dv_ref[...]
        for b, dil in enumerate(BRANCH_DIL):
            j = i + dil
            ok = jnp.logical_and(j < nrb, j % per_blk < dil)
            dkt = dkt + jnp.where(ok, ek_refs[b][...], 0.0)
            dvt = dvt + jnp.where(ok, ev_refs[b][...], 0.0)
        for ch in range(N_CHUNK):
            o_ref[:, 128 * ch:128 * (ch + 1)] = (dq_ref[ch] * ATTN_SCALE).astype(o_ref.dtype)
            o_ref[:, ATTN_W + 128 * ch:ATTN_W + 128 * (ch + 1)] = dkt[ch].astype(o_ref.dtype)
            o_ref[:, 2 * ATTN_W + 128 * ch:2 * ATTN_W + 128 * (ch + 1)] = dvt[ch].astype(o_ref.dtype)
        o_ref[:, O_SGU:O_SSM] = dzs_ref[...].astype(o_ref.dtype)
        o_ref[:, O_SSM:] = dus_ref[...].astype(o_ref.dtype)

    here = pl.BlockSpec((N_CHUNK, QBLK, 128), lambda i: (0, i, 0))
    edge_specs = [pl.BlockSpec((N_CHUNK, QBLK, 128),
                               functools.partial(lambda i, d: (0, jnp.minimum(i + d, nrb - 1), 0), d=dil))
                  for dil in BRANCH_DIL]
    return pl.pallas_call(
        body, name="attn_bwd_sum", grid=(nrb,),
        in_specs=[here, here, here] + edge_specs + edge_specs
        + [pl.BlockSpec((QBLK, 2 * SGU_W), lambda i: (i, 0)), pl.BlockSpec((QBLK, SSM_W), lambda i: (i, 0))],
        out_specs=pl.BlockSpec((QBLK, O_SSM + SSM_W), lambda i: (i, 0)),
        out_shape=jax.ShapeDtypeStruct((s, O_SSM + SSM_W), MXU_DTYPE),
        compiler_params=_params(("parallel",)),
    )(dq, dk, dv, *ek, *ev, dzs, dus)


SGU_ROWS = 512


def _group_avg():
    r = lax.broadcasted_iota(jnp.int32, (SGU_W, SGU_W), 0) // SGU_GW
    c = lax.broadcasted_iota(jnp.int32, (SGU_W, SGU_W), 1) // SGU_GW
    return jnp.where(r == c, 1.0 / SGU_GW, 0.0).astype(F32)


def _group_mean(x, avg):
    return lax.dot_general(x, avg, (NN, ((), ())), preferred_element_type=F32, precision=lax.Precision.HIGHEST)


def _sgu_prep(z, g_ref, b_ref):
    gz, dgelu = _gelu_pair(z)
    u, v = gz[:, :SGU_W], gz[:, SGU_W:]
    avg = _group_avg()
    cen = v - _group_mean(v, avg)
    rstd = lax.rsqrt(_group_mean(cen * cen, avg) + EPS)
    xhat = cen * rstd
    vn = (xhat * g_ref[...] + b_ref[...]).astype(MXU_DTYPE)
    return u, dgelu, avg, rstd, xhat, vn


def _sgu_mix(w_ref, bt_ref, vn_rows, lane_group):
    mixed = jnp.zeros((SGU_CHUNK, SGU_W), F32)
    for g in range(SGU_G):
        m_g = _dot(w_ref[g].astype(MXU_DTYPE), vn_rows, NN) + bt_ref[:, g:g + 1]
        mixed = jnp.where(lane_group == g, m_g, mixed)
    return mixed


def _sgu2_fwd(zs, ln_g, ln_b, w_mask, b_t):
    s = zs.shape[0]
    nch = SGU_ROWS // SGU_CHUNK

    def body(z_ref, g_ref, b_ref, w_ref, bt_ref, o_ref):
        u, _, _, _, _, vn = _sgu_prep(z_ref[...], g_ref, b_ref)
        lane_group = lax.broadcasted_iota(jnp.int32, (SGU_CHUNK, SGU_W), 1) // SGU_GW
        for ci in range(nch):
            rs = slice(SGU_CHUNK * ci, SGU_CHUNK * (ci + 1))
            o_ref[rs, :] = u[rs] * _sgu_mix(w_ref, bt_ref, vn[rs], lane_group)

    full = lambda shape: pl.BlockSpec(shape, lambda i: tuple(0 for _ in shape))
    return pl.pallas_call(
        body, name="sgu_fwd", grid=(s // SGU_ROWS,),
        in_specs=[pl.BlockSpec((SGU_ROWS, 2 * SGU_W), lambda i: (i, 0)), full((1, SGU_W)), full((1, SGU_W)),
                  full((SGU_G, SGU_CHUNK, SGU_CHUNK)), full((SGU_CHUNK, SGU_G))],
        out_specs=pl.BlockSpec((SGU_ROWS, SGU_W), lambda i: (i, 0)),
        out_shape=jax.ShapeDtypeStruct((s, SGU_W), F32),
        compiler_params=_params(("parallel",)),
    )(zs, ln_g.reshape(1, SGU_W), ln_b.reshape(1, SGU_W), w_mask, b_t)


def _sgu2_bwd(zs, ln_g, ln_b, w_mask, b_t, dy):
    s = zs.shape[0]
    nch = SGU_ROWS // SGU_CHUNK

    def body(z_ref, g_ref, b_ref, w_ref, bt_ref, dy_ref, dz_ref, dg_ref, dbb_ref, dw_ref, dbt_ref):
        @pl.when(pl.program_id(0) == 0)
        def _():
            dg_ref[...] = jnp.zeros_like(dg_ref)
            dbb_ref[...] = jnp.zeros_like(dbb_ref)
            dw_ref[...] = jnp.zeros_like(dw_ref)
            dbt_ref[...] = jnp.zeros_like(dbt_ref)

        u, dgelu, avg, rstd, xhat, vn = _sgu_prep(z_ref[...], g_ref, b_ref)
        lane_group = lax.broadcasted_iota(jnp.int32, (SGU_CHUNK, SGU_W), 1) // SGU_GW
        dy = dy_ref[...]
        dvn_parts = []
        for ci in range(nch):
            rs = slice(SGU_CHUNK * ci, SGU_CHUNK * (ci + 1))
            mixed = _sgu_mix(w_ref, bt_ref, vn[rs], lane_group)
            dz_ref[rs, 0:SGU_W] = (dy[rs] * mixed * dgelu[rs, :SGU_W]).astype(dz_ref.dtype)
            dmixed = dy[rs] * u[rs]
            dvn = jnp.zeros((SGU_CHUNK, SGU_W), F32)
            for g in range(SGU_G):
                own = lane_group == g
                dm_g = jnp.where(own, dmixed, 0.0)
                dvn = jnp.where(own, _dot(w_ref[g].astype(MXU_DTYPE), dm_g.astype(MXU_DTYPE), TN), dvn)
                dw_ref[g] += _dot(dm_g.astype(MXU_DTYPE), vn[rs], NT)
                dbt_ref[:, g:g + 1] += jnp.sum(dm_g, axis=-1, keepdims=True)
            dvn_parts.append(dvn)
        dvn = jnp.concatenate(dvn_parts, axis=0)
        dg_ref[...] += jnp.sum(dvn * xhat, axis=0, keepdims=True)
        dbb_ref[...] += jnp.sum(dvn, axis=0, keepdims=True)
        dxh = dvn * g_ref[...]
        dv = rstd * (dxh - _group_mean(dxh, avg) - xhat * _group_mean(dxh * xhat, avg))
        dz_ref[:, SGU_W:] = (dv * dgelu[:, SGU_W:]).astype(dz_ref.dtype)

    full = lambda shape: pl.BlockSpec(shape, lambda i: tuple(0 for _ in shape))
    return pl.pallas_call(
        body, name="sgu_bwd", grid=(s // SGU_ROWS,),
        in_specs=[pl.BlockSpec((SGU_ROWS, 2 * SGU_W), lambda i: (i, 0)), full((1, SGU_W)), full((1, SGU_W)),
                  full((SGU_G, SGU_CHUNK, SGU_CHUNK)), full((SGU_CHUNK, SGU_G)),
                  pl.BlockSpec((SGU_ROWS, SGU_W), lambda i: (i, 0))],
        out_specs=[pl.BlockSpec((SGU_ROWS, 2 * SGU_W), lambda i: (i, 0)), full((1, SGU_W)), full((1, SGU_W)),
                   full((SGU_G, SGU_CHUNK, SGU_CHUNK)), full((SGU_CHUNK, SGU_G))],
        out_shape=[jax.ShapeDtypeStruct((s, 2 * SGU_W), MXU_DTYPE), jax.ShapeDtypeStruct((1, SGU_W), F32),
                   jax.ShapeDtypeStruct((1, SGU_W), F32), jax.ShapeDtypeStruct((SGU_G, SGU_CHUNK, SGU_CHUNK), F32),
                   jax.ShapeDtypeStruct((SGU_CHUNK, SGU_G), F32)],
        compiler_params=_params(("arbitrary",)),
    )(zs, ln_g.reshape(1, SGU_W), ln_b.reshape(1, SGU_W), w_mask, b_t, dy)


def _ssm_discretize(a_re, a_im, log_dt, b_re, b_im):
    dt = jnp.exp(log_dt)[:, None]
    mag = jnp.exp(a_re * dt)
    ab_re = mag * jnp.cos(a_im * dt)
    ab_im = mag * jnp.sin(a_im * dt)
    den = a_re * a_re + a_im * a_im
    f_re = ((ab_re - 1.0) * a_re + ab_im * a_im) / den
    f_im = (ab_im * a_re - (ab_re - 1.0) * a_im) / den
    bb_re = f_re[:, :, None] * b_re - f_im[:, :, None] * b_im
    bb_im = f_re[:, :, None] * b_im + f_im[:, :, None] * b_re
    return ab_re, ab_im, bb_re, bb_im


def _ssm_operands(a_re, a_im, log_dt, b_re, b_im, c_re, c_im):
    ab_re, ab_im, bb_re, bb_im = _ssm_discretize(a_re, a_im, log_dt, b_re, b_im)
    eye = jnp.eye(SSM_G, dtype=F32)
    b_blk = jnp.einsum("pgnc,gh->gcphn", jnp.stack([bb_re, bb_im]), eye).reshape(SSM_W, 2 * NSTATE)
    c_mat = jnp.einsum("pgcn,gh->pgnhc", jnp.stack([c_re, -c_im]), eye).reshape(2 * NSTATE, SSM_W)
    a_row = jnp.stack([ab_re.reshape(NSTATE), ab_im.reshape(NSTATE)])
    p_re, p_im = a_row[0:1], a_row[1:2]
    while p_re.shape[0] < SSM_TSEG:
        l_re, l_im = p_re[-1:], p_im[-1:]
        p_re, p_im = (jnp.concatenate([p_re, p_re * l_re - p_im * l_im]),
                      jnp.concatenate([p_im, p_re * l_im + p_im * l_re]))
    p_tab = jnp.stack([p_re, p_im])
    return b_blk.astype(MXU_DTYPE), c_mat.astype(MXU_DTYPE), a_row, p_tab


def _lane_chunks():
    return [(lo, lo + SSM_LANE_CHUNK) for lo in range(0, NSTATE, SSM_LANE_CHUNK)]


def _seg_rows(j):
    return pl.ds(pl.multiple_of(j * SSM_NSEG, SSM_NSEG), SSM_NSEG)


def _to_segments(t):
    s, w = t.shape
    return t.reshape(s // SSM_TB, SSM_NSEG, SSM_TSEG, w).transpose(0, 2, 1, 3).reshape(s, w)


def _from_segments(t):
    s, w = t.shape
    return t.reshape(s // SSM_TB, SSM_TSEG, SSM_NSEG, w).transpose(0, 2, 1, 3).reshape(s, w)


def _ssm_local_scan(buf, a_ref, *, reverse):
    ends_re, ends_im = [], []
    for lo, hi in _lane_chunks():
        are = jnp.broadcast_to(a_ref[0:1, lo:hi], (SSM_NSEG, hi - lo))
        aim = jnp.broadcast_to(a_ref[1:2, lo:hi], (SSM_NSEG, hi - lo))
        if reverse:
            aim = -aim

        def step(jj, carry, lo=lo, hi=hi, are=are, aim=aim):
            xr, xi = carry
            j = (SSM_TSEG - 1 - jj) if reverse else jj
            tr = buf[_seg_rows(j), lo:hi]
            ti = buf[_seg_rows(j), NSTATE + lo:NSTATE + hi]
            nr = are * xr - aim * xi + tr
            ni = are * xi + aim * xr + ti
            buf[_seg_rows(j), lo:hi] = nr
            buf[_seg_rows(j), NSTATE + lo:NSTATE + hi] = ni
            return nr, ni

        zero = jnp.zeros((SSM_NSEG, hi - lo), F32)
        xr, xi = lax.fori_loop(0, SSM_TSEG, step, (zero, zero), unroll=4)
        ends_re.append(xr)
        ends_im.append(xi)
    return jnp.concatenate(ends_re, axis=1), jnp.concatenate(ends_im, axis=1)


def _ssm_entry_states(ends_re, ends_im, carry_ref, p_ref, entry_ref, *, reverse):
    at_re = p_ref[0, SSM_TSEG - 1:SSM_TSEG, :]
    at_im = p_ref[1, SSM_TSEG - 1:SSM_TSEG, :]
    if reverse:
        at_im = -at_im
    cur_re = carry_ref[0:1, 0:NSTATE]
    cur_im = carry_ref[0:1, NSTATE:2 * NSTATE]
    order = range(SSM_NSEG - 1, -1, -1) if reverse else range(SSM_NSEG)
    for i in order:
        entry_ref[0, i:i + 1, 0:NSTATE] = cur_re
        entry_ref[0, i:i + 1, NSTATE:2 * NSTATE] = cur_im
        nxt_re = ends_re[i:i + 1] + at_re * cur_re - at_im * cur_im
        nxt_im = ends_im[i:i + 1] + at_re * cur_im + at_im * cur_re
        cur_re, cur_im = nxt_re, nxt_im
    carry_ref[0:1, 0:NSTATE] = cur_re
    carry_ref[0:1, NSTATE:2 * NSTATE] = cur_im


def _ssm_fixup(buf, p_ref, entry_ref, *, reverse):
    for lo, hi in _lane_chunks():
        e_re = entry_ref[0, :, lo:hi]
        e_im = entry_ref[0, :, NSTATE + lo:NSTATE + hi]

        def step(j, carry, lo=lo, hi=hi, e_re=e_re, e_im=e_im):
            jp = (SSM_TSEG - 1 - j) if reverse else j
            pr = p_ref[0, pl.ds(jp, 1), lo:hi]
            pi = p_ref[1, pl.ds(jp, 1), lo:hi]
            if reverse:
                pi = -pi
            buf[_seg_rows(j), lo:hi] = buf[_seg_rows(j), lo:hi] + pr * e_re - pi * e_im
            buf[_seg_rows(j), NSTATE + lo:NSTATE + hi] = (buf[_seg_rows(j), NSTATE + lo:NSTATE + hi]
                                                           + pr * e_im + pi * e_re)
            return carry

        lax.fori_loop(0, SSM_TSEG, step, 0, unroll=4)


def _ssm_fwd(u, ops, d_skip, glu_w, glu_b):
    b_blk, c_mat, a_row, p_tab = ops
    s = u.shape[0]
    nblk = s // SSM_TB

    def body(u_ref, bb_ref, cm_ref, a_ref, p_ref, d_ref, gw_ref, gb_ref, y_ref, entry_ref, xbuf, carry):
        @pl.when(pl.program_id(0) == 0)
        def _():
            carry[...] = jnp.zeros_like(carry)

        uu = u_ref[...]
        xbuf[...] = _dotf(uu, bb_ref[...], NN)
        ends_re, ends_im = _ssm_local_scan(xbuf, a_ref, reverse=False)
        _ssm_entry_states(ends_re, ends_im, carry, p_ref, entry_ref, reverse=False)
        _ssm_fixup(xbuf, p_ref, entry_ref, reverse=False)
        y = _dotf(xbuf[...],cm_ref[...], NN) + d_ref[...] * uu
        y2 = _gelu(y)
        gate = jax.nn.sigmoid(_dot(y2.astype(MXU_DTYPE), gw_ref[...].astype(MXU_DTYPE), NN) + gb_ref[...])
        y_ref[...] = y2 * gate

    full = lambda shape: pl.BlockSpec(shape, lambda i: tuple(0 for _ in shape))
    y_seg, entry = pl.pallas_call(
        body, name="ssm_fwd", grid=(nblk,),
        in_specs=[pl.BlockSpec((SSM_TB, SSM_W), lambda i: (i, 0)), full(b_blk.shape), full(c_mat.shape),
                  full(a_row.shape), full(p_tab.shape), full((1, SSM_W)), full((SSM_W, SSM_W)), full((1, SSM_W))],
        out_specs=[pl.BlockSpec((SSM_TB, SSM_W), lambda i: (i, 0)),
                   pl.BlockSpec((1, SSM_NSEG, 2 * NSTATE), lambda i: (i, 0, 0))],
        out_shape=[jax.ShapeDtypeStruct((s, SSM_W), F32), jax.ShapeDtypeStruct((nblk, SSM_NSEG, 2 * NSTATE), F32)],
        scratch_shapes=[pltpu.VMEM((SSM_TB, 2 * NSTATE), F32), pltpu.VMEM((SSM_NSEG, 2 * NSTATE), F32)],
        compiler_params=_params(("arbitrary",)),
    )(_to_segments(u), b_blk, c_mat, a_row, p_tab, d_skip.reshape(1, SSM_W), glu_w, glu_b.reshape(1, SSM_W))
    return _from_segments(y_seg), entry


def _ssm_bwd(u, entry, ops, d_skip, glu_w, glu_b, dout):
    b_blk, c_mat, a_row, p_tab = ops
    s = u.shape[0]
    nblk = s // SSM_TB

    def body(u_ref, en_ref, bb_ref, cm_ref, a_ref, p_ref, d_ref, gw_ref, gb_ref, do_ref,
             du_ref, dbb_ref, dcm_ref, da_ref, dd_ref, dgw_ref, dgb_ref, xbuf, gbuf, gcarry, gentry):
        @pl.when(pl.program_id(0) == 0)
        def _():
            gcarry[...] = jnp.zeros_like(gcarry)
            for r in (dbb_ref, dcm_ref, da_ref, dd_ref, dgw_ref, dgb_ref):
                r[...] = jnp.zeros_like(r)

        uu = u_ref[...]
        xbuf[...] = _dotf(uu, bb_ref[...], NN)
        _ssm_local_scan(xbuf, a_ref, reverse=False)
        _ssm_fixup(xbuf, p_ref, en_ref, reverse=False)
        y = _dotf(xbuf[...],cm_ref[...], NN) + d_ref[...] * uu
        y2, dgelu = _gelu_pair(y)
        y2m = y2.astype(MXU_DTYPE)
        gwm = gw_ref[...].astype(MXU_DTYPE)
        gate = jax.nn.sigmoid(_dot(y2m, gwm, NN) + gb_ref[...])
        dout = do_ref[...]
        dpre = dout * y2 * gate * (1.0 - gate)
        dprem = dpre.astype(MXU_DTYPE)
        dy2 = dout * gate + _dot(dprem, gwm, NT)
        dgw_ref[...] += _dot(y2m, dprem, TN)
        dgb_ref[...] += jnp.sum(dpre, axis=0, keepdims=True)
        dy = dy2 * dgelu
        dd_ref[...] += jnp.sum(dy * uu, axis=0, keepdims=True)
        dcm_ref[...] += _dotf(xbuf[...],dy, TN)
        gbuf[...] = _dotf(dy, cm_ref[...], NT)
        gs_re, gs_im = _ssm_local_scan(gbuf, a_ref, reverse=True)
        _ssm_entry_states(gs_re, gs_im, gcarry, p_ref, gentry, reverse=True)
        _ssm_fixup(gbuf, p_ref, gentry, reverse=True)
        du_ref[...] = (_dotf(gbuf[...], bb_ref[...], NT) + d_ref[...] * dy).astype(du_ref.dtype)
        dbb_ref[...] += _dotf(uu, gbuf[...], TN)
        for lo, hi in _lane_chunks():
            def step(j, carry, lo=lo, hi=hi):
                acc_re, acc_im = carry
                g_re = gbuf[_seg_rows(j), lo:hi]
                g_im = gbuf[_seg_rows(j), NSTATE + lo:NSTATE + hi]
                x_re = xbuf[_seg_rows(j - 1), lo:hi]
                x_im = xbuf[_seg_rows(j - 1), NSTATE + lo:NSTATE + hi]
                return acc_re + g_re * x_re + g_im * x_im, acc_im + g_im * x_re - g_re * x_im

            g0_re = gbuf[_seg_rows(0), lo:hi]
            g0_im = gbuf[_seg_rows(0), NSTATE + lo:NSTATE + hi]
            e_re = en_ref[0, :, lo:hi]
            e_im = en_ref[0, :, NSTATE + lo:NSTATE + hi]
            init = (g0_re * e_re + g0_im * e_im, g0_im * e_re - g0_re * e_im)
            acc_re, acc_im = lax.fori_loop(1, SSM_TSEG, step, init, unroll=4)
            da_ref[0:1, lo:hi] += jnp.sum(acc_re, axis=0, keepdims=True)
            da_ref[1:2, lo:hi] += jnp.sum(acc_im, axis=0, keepdims=True)

    full = lambda shape: pl.BlockSpec(shape, lambda i: tuple(0 for _ in shape))
    rev = pl.BlockSpec((SSM_TB, SSM_W), lambda i: (nblk - 1 - i, 0))
    outs = pl.pallas_call(
        body, name="ssm_bwd", grid=(nblk,),
        in_specs=[rev, pl.BlockSpec((1, SSM_NSEG, 2 * NSTATE), lambda i: (nblk - 1 - i, 0, 0)),
                  full(b_blk.shape), full(c_mat.shape), full(a_row.shape), full(p_tab.shape),
                  full((1, SSM_W)), full((SSM_W, SSM_W)), full((1, SSM_W)), rev],
        out_specs=[rev, full(b_blk.shape), full(c_mat.shape), full(a_row.shape), full((1, SSM_W)),
                   full((SSM_W, SSM_W)), full((1, SSM_W))],
        out_shape=[jax.ShapeDtypeStruct((s, SSM_W), MXU_DTYPE), jax.ShapeDtypeStruct(b_blk.shape, F32),
                   jax.ShapeDtypeStruct(c_mat.shape, F32), jax.ShapeDtypeStruct(a_row.shape, F32),
                   jax.ShapeDtypeStruct((1, SSM_W), F32), jax.ShapeDtypeStruct((SSM_W, SSM_W), F32),
                   jax.ShapeDtypeStruct((1, SSM_W), F32)],
        scratch_shapes=[pltpu.VMEM((SSM_TB, 2 * NSTATE), F32), pltpu.VMEM((SSM_TB, 2 * NSTATE), F32),
                        pltpu.VMEM((SSM_NSEG, 2 * NSTATE), F32), pltpu.VMEM((1, SSM_NSEG, 2 * NSTATE), F32)],
        compiler_params=_params(("arbitrary",)),
    )(_to_segments(u), entry, b_blk, c_mat, a_row, p_tab, d_skip.reshape(1, SSM_W), glu_w, glu_b.reshape(1, SSM_W),
      _to_segments(dout))
    return (_from_segments(outs[0]),) + tuple(outs[1:])


MIX_SEGS = ((0, ATTN_W), (ATTN_W, ATTN_W + SGU_W), (ATTN_W + SGU_W, D_MODEL))


def _chunks_to_rows(a_ref):
    return jnp.concatenate([a_ref[ch] for ch in range(N_CHUNK)], axis=1)


def _mix_fwd(y_attn_c, y_sgu, y_ssm, gain):
    s = y_sgu.shape[0]

    def body(a_ref, b_ref, c_ref, g_ref, o_ref):
        for x, (lo, hi) in zip((_chunks_to_rows(a_ref), b_ref[...], c_ref[...]), MIX_SEGS):
            r = lax.rsqrt(jnp.mean(x * x, axis=-1, keepdims=True) + EPS)
            o_ref[:, lo:hi] = (x * r * g_ref[:, lo:hi]).astype(o_ref.dtype)

    row = lambda w: pl.BlockSpec((ROWS, w), lambda i: (i, 0))
    return pl.pallas_call(
        body, name="mix_fwd", grid=(s // ROWS,),
        in_specs=[pl.BlockSpec((N_CHUNK, ROWS, 128), lambda i: (0, i, 0)), row(SGU_W), row(SSM_W),
                  pl.BlockSpec((1, D_MODEL), lambda i: (0, 0))],
        out_specs=row(D_MODEL), out_shape=jax.ShapeDtypeStruct((s, D_MODEL), MXU_DTYPE),
        compiler_params=_params(("parallel",)),
    )(y_attn_c, y_sgu, y_ssm, gain.reshape(1, D_MODEL))


def _mix_bwd(y_attn_c, y_sgu, y_ssm, gain, dmix):
    s = y_sgu.shape[0]

    def body(a_ref, b_ref, c_ref, g_ref, dm_ref, da_ref, dl_ref, db_ref, dc_ref, dg_ref):
        @pl.when(pl.program_id(0) == 0)
        def _():
            dg_ref[...] = jnp.zeros_like(dg_ref)

        grads = []
        for x, (lo, hi) in zip((_chunks_to_rows(a_ref), b_ref[...], c_ref[...]), MIX_SEGS):
            r = lax.rsqrt(jnp.mean(x * x, axis=-1, keepdims=True) + EPS)
            xhat = x * r
            dm = dm_ref[:, lo:hi].astype(F32)
            dg_ref[:, lo:hi] += jnp.sum(dm * xhat, axis=0, keepdims=True)
            dxh = dm * g_ref[:, lo:hi]
            grads.append(r * (dxh - xhat * jnp.mean(dxh * xhat, axis=-1, keepdims=True)))
        db_ref[...] = grads[1]
        dc_ref[...] = grads[2]
        low = lax.broadcasted_iota(jnp.int32, (ROWS, 128), 1) < HEAD_DIM
        for ch in range(N_CHUNK):
            d_c = grads[0][:, 128 * ch:128 * (ch + 1)]
            da_ref[ch] = d_c.astype(da_ref.dtype)
            prod = d_c * a_ref[ch]
            dl_ref[ch] = jnp.where(low, jnp.sum(prod[:, :HEAD_DIM], axis=-1, keepdims=True),
                                   jnp.sum(prod[:, HEAD_DIM:], axis=-1, keepdims=True))

    row = lambda w: pl.BlockSpec((ROWS, w), lambda i: (i, 0))
    vec = pl.BlockSpec((1, D_MODEL), lambda i: (0, 0))
    chunked = pl.BlockSpec((N_CHUNK, ROWS, 128), lambda i: (0, i, 0))
    return pl.pallas_call(
        body, name="mix_bwd", grid=(s // ROWS,),
        in_specs=[chunked, row(SGU_W), row(SSM_W), vec, row(D_MODEL)],
        out_specs=[chunked, chunked, row(SGU_W), row(SSM_W), vec],
        out_shape=[jax.ShapeDtypeStruct((N_CHUNK, s, 128), ATTN_IO_DTYPE), jax.ShapeDtypeStruct((N_CHUNK, s, 128), F32),
                   jax.ShapeDtypeStruct((s, SGU_W), F32), jax.ShapeDtypeStruct((s, SSM_W), F32),
                   jax.ShapeDtypeStruct((1, D_MODEL), F32)],
        compiler_params=_params(("arbitrary",)),
    )(y_attn_c, y_sgu, y_ssm, gain.reshape(1, D_MODEL), dmix)


CONV_ROWS = 128
CONV_COLS = 1408
CONV_PAIR = 2 * CONV_COLS
HALO = 16


def _interleave_ff(t):
    lead = t.shape[:-1]
    nb = D_FF // CONV_COLS
    return jnp.swapaxes(t.reshape(lead + (2, nb, CONV_COLS)), -3, -2).reshape(lead + (2 * D_FF,))


def _deinterleave_ff(t):
    lead = t.shape[:-1]
    nb = D_FF // CONV_COLS
    return jnp.swapaxes(t.reshape(lead + (nb, 2, CONV_COLS)), -3, -2).reshape(lead + (2 * D_FF,))


def _conv_in_specs():
    halo_idx = lambda i: jnp.maximum(i * (CONV_ROWS // HALO) - 1, 0)
    return [pl.BlockSpec((CONV_ROWS, CONV_PAIR), lambda j, i: (i, j)),
            pl.BlockSpec((HALO, CONV_PAIR), lambda j, i: (halo_idx(i), j)),
            pl.BlockSpec((3, CONV_PAIR), lambda j, i: (0, j)),
            pl.BlockSpec((1, CONV_PAIR), lambda j, i: (0, j))]


def _shift_matrix(rows, back):
    r = lax.broadcasted_iota(jnp.int32, (2 * rows, rows), 0)
    c = lax.broadcasted_iota(jnp.int32, (2 * rows, rows), 1)
    step = jnp.where(r < rows, 1, 2)
    t = jnp.where(r < rows, r, r - rows)
    src = t - step if back else t + step
    return jnp.where(c == src, 1.0, 0.0).astype(MXU_DTYPE)


def _patch_rows(x, at_end, rows):
    tile = 8
    n = x.shape[0]
    idx = lax.broadcasted_iota(jnp.int32, (tile, x.shape[1]), 0)
    piece = x[n - tile:] if at_end else x[:tile]
    for k, row in enumerate(rows):
        where_row = (tile - len(rows) + k) if at_end else k
        piece = jnp.where(idx == where_row, row, piece)
    return jnp.concatenate([x[:n - tile], piece], axis=0) if at_end else jnp.concatenate([piece, x[tile:]], axis=0)


def _mxu_taps(main_m, halo, first):
    shifted = _dot(_shift_matrix(main_m.shape[0], True), main_m, NN)
    h1 = jnp.where(first, 0.0, halo[HALO - 1:HALO, :])
    h2 = jnp.where(first, 0.0, halo[HALO - 2:HALO - 1, :])
    x1 = _patch_rows(shifted[:main_m.shape[0]], False, [h1])
    x2 = _patch_rows(shifted[main_m.shape[0]:], False, [h2, h1])
    return x1, x2


def _conv_gate(w_ref, b_ref, x2, x1, x0):
    return w_ref[0:1, :] * x2 + w_ref[1:2, :] * x1 + w_ref[2:3, :] * x0 + b_ref[...]


def _ffn_gate_fwd(hh, conv_w, conv_b):
    s = hh.shape[0]

    def body(m_ref, h_ref, w_ref, b_ref, o_ref):
        first = pl.program_id(1) == 0
        main_m = m_ref[...]
        x1, x2 = _mxu_taps(main_m, h_ref[...].astype(F32), first)
        conv = _conv_gate(w_ref, b_ref, x2, x1, main_m.astype(F32))
        o_ref[...] = (_gelu(conv[:, CONV_COLS:]) * conv[:, :CONV_COLS]).astype(o_ref.dtype)

    return pl.pallas_call(
        body, name="ffn_act_fwd", grid=(D_FF // CONV_COLS, s // CONV_ROWS), in_specs=_conv_in_specs(),
        out_specs=pl.BlockSpec((CONV_ROWS, CONV_COLS), lambda j, i: (i, j)),
        out_shape=jax.ShapeDtypeStruct((s, D_FF), MXU_DTYPE),
        compiler_params=_params(("parallel", "parallel")),
    )(hh, hh, conv_w, conv_b.reshape(1, -1))


def _ffn_gate_bwd(hh, conv_w, conv_b, da):
    s = hh.shape[0]
    nrow = s // CONV_ROWS

    def gate_grad(conv, da):
        act, dact = _gelu_pair(conv[:, CONV_COLS:])
        return jnp.concatenate([da * act, da * conv[:, :CONV_COLS] * dact], axis=1)

    def body(m_ref, h_ref, w_ref, b_ref, nx_ref, da_ref, dan_ref, o_ref, dw_ref, db_ref):
        first = pl.program_id(1) == 0
        last = pl.program_id(1) == nrow - 1

        @pl.when(first)
        def _():
            dw_ref[...] = jnp.zeros_like(dw_ref)
            db_ref[...] = jnp.zeros_like(db_ref)

        main_m = m_ref[...]
        main = main_m.astype(F32)
        x1, x2 = _mxu_taps(main_m, h_ref[...].astype(F32), first)
        dconv = gate_grad(_conv_gate(w_ref, b_ref, x2, x1, main), da_ref[...].astype(F32))
        nx = nx_ref[...].astype(F32)
        nx1 = _patch_rows(pltpu.roll(nx, 1, 0), False, [main[CONV_ROWS - 1:]])
        nx2 = _patch_rows(pltpu.roll(nx, 2, 0), False, [main[CONV_ROWS - 2:CONV_ROWS - 1], main[CONV_ROWS - 1:]])
        dnext = gate_grad(_conv_gate(w_ref, b_ref, nx2, nx1, nx), jnp.where(last, 0.0, dan_ref[...].astype(F32)))
        dnext = dnext.astype(MXU_DTYPE).astype(F32)
        ahead = _dot(_shift_matrix(CONV_ROWS, False), dconv.astype(MXU_DTYPE), NN)
        ahead1 = _patch_rows(ahead[:CONV_ROWS], True, [dnext[0:1]])
        ahead2 = _patch_rows(ahead[CONV_ROWS:], True, [dnext[0:1], dnext[1:2]])
        o_ref[...] = (w_ref[2:3, :] * dconv + w_ref[1:2, :] * ahead1 + w_ref[0:1, :] * ahead2).astype(o_ref.dtype)
        for t, tap in enumerate((x2, x1, main)):
            dw_ref[t:t + 1, :] += jnp.sum(dconv * tap, axis=0, keepdims=True)
        db_ref[...] += jnp.sum(dconv, axis=0, keepdims=True)

    nxt = lambda i: jnp.minimum((i + 1) * (CONV_ROWS // HALO), s // HALO - 1)
    return pl.pallas_call(
        body, name="ffn_act_bwd", grid=(D_FF // CONV_COLS, nrow),
        in_specs=_conv_in_specs() + [pl.BlockSpec((HALO, CONV_PAIR), lambda j, i: (nxt(i), j)),
                                     pl.BlockSpec((CONV_ROWS, CONV_COLS), lambda j, i: (i, j)),
                                     pl.BlockSpec((HALO, CONV_COLS), lambda j, i: (nxt(i), j))],
        out_specs=[pl.BlockSpec((CONV_ROWS, CONV_PAIR), lambda j, i: (i, j)),
                   pl.BlockSpec((3, CONV_PAIR), lambda j, i: (0, j)), pl.BlockSpec((1, CONV_PAIR), lambda j, i: (0, j))],
        out_shape=[jax.ShapeDtypeStruct((s, 2 * D_FF), MXU_DTYPE), jax.ShapeDtypeStruct((3, 2 * D_FF), F32),
                   jax.ShapeDtypeStruct((1, 2 * D_FF), F32)],
        compiler_params=_params(("parallel", "arbitrary")),
    )(hh, hh, conv_w, conv_b.reshape(1, -1), hh, da, da)


def _ple_weight_specs(layer):
    return [pl.BlockSpec((None, D_MODEL, D_MODEL), lambda i: (layer, 0, 0)),
            pl.BlockSpec((None, PLE_DIM, D_MODEL), lambda i: (layer, 0, 0))]


def _ple_fwd(h, gain, p, w_gate, w_proj, layer):
    s = h.shape[0]
    tm = 512

    def body(h_ref, g_ref, p_ref, wg_ref, wp_ref, o_ref, xn_ref):
        x = h_ref[...]
        xn = _rms_rows(x, g_ref[...])
        xn_ref[...] = xn
        gate = jax.nn.sigmoid(_dot(xn, wg_ref[...].astype(MXU_DTYPE), NN))
        proj = _dot(p_ref[...].astype(MXU_DTYPE), wp_ref[...].astype(MXU_DTYPE), NN)
        o_ref[...] = x + gate * proj

    row = pl.BlockSpec((tm, D_MODEL), lambda i: (i, 0))
    return pl.pallas_call(
        body, name="ple_fwd", grid=(s // tm,),
        in_specs=[row, pl.BlockSpec((1, D_MODEL), lambda i: (0, 0)), pl.BlockSpec((tm, PLE_DIM), lambda i: (i, 0))]
        + _ple_weight_specs(layer),
        out_specs=[row, row],
        out_shape=[jax.ShapeDtypeStruct((s, D_MODEL), F32), jax.ShapeDtypeStruct((s, D_MODEL), MXU_DTYPE)],
        compiler_params=_params(("parallel",)),
    )(h, gain.reshape(1, D_MODEL), p, w_gate, w_proj)


def _ple_bwd(xn, p, w_gate, w_proj, dh, layer):
    s = xn.shape[0]
    tm = 512

    def body(x_ref, p_ref, wg_ref, wp_ref, dh_ref, dpre_ref, dproj_ref):
        gate = jax.nn.sigmoid(_dot(x_ref[...].astype(MXU_DTYPE), wg_ref[...].astype(MXU_DTYPE), NN))
        proj = _dot(p_ref[...].astype(MXU_DTYPE), wp_ref[...].astype(MXU_DTYPE), NN)
        dh = dh_ref[...]
        dpre_ref[...] = (dh * proj * gate * (1.0 - gate)).astype(dpre_ref.dtype)
        dproj_ref[...] = (dh * gate).astype(dproj_ref.dtype)

    row = pl.BlockSpec((tm, D_MODEL), lambda i: (i, 0))
    return pl.pallas_call(
        body, name="ple_bwd", grid=(s // tm,),
        in_specs=[row, pl.BlockSpec((tm, PLE_DIM), lambda i: (i, 0))] + _ple_weight_specs(layer) + [row],
        out_specs=[row, row],
        out_shape=[jax.ShapeDtypeStruct((s, D_MODEL), MXU_DTYPE)] * 2,
        compiler_params=_params(("parallel",)),
    )(xn, p, w_gate, w_proj, dh)


O_SGU = 3 * ATTN_W
O_SSM = O_SGU + 2 * SGU_W


def _layer_consts(w, i):
    causal = jnp.asarray(np.tril(np.ones((SGU_CHUNK, SGU_CHUNK), np.float32)))
    return {
        "sgu_w_mask": w["sgu_w"][i] * causal,
        "sgu_b_t": w["sgu_b"][i].T,
        "ssm_ops": _ssm_operands(w["ssm_a_re"][i], w["ssm_a_im"][i], w["ssm_log_dt"][i], w["ssm_b_re"][i],
                                 w["ssm_b_im"][i], w["ssm_c_re"][i], w["ssm_c_im"][i]),
    }


def _layer_fwd(h0, p_i, w, i, bias):
    c = _layer_consts(w, i)
    xn1, qkv, zs, us = _in_proj(h0, w["norm_attn_g"][i], w["w_in"], i)
    y_attn, lse = _attn2_fwd(qkv, bias)
    y_sgu = _sgu2_fwd(zs, w["sgu_ln_g"][i], w["sgu_ln_b"][i], c["sgu_w_mask"], c["sgu_b_t"])
    y_ssm, entry = _ssm_fwd(us, c["ssm_ops"], w["ssm_d"][i], w["ssm_glu_w"][i], w["ssm_glu_b"][i])
    mix = _mix_fwd(y_attn, y_sgu, y_ssm, w["branch_norm_g"][i])
    h1 = _matmul(mix, w["w_out"], name="out_proj", out_dtype=F32, tm=512, tn=1024, residual=h0, layer=i)
    xn2, hh = _ffn_up(h1, w["norm_ffn_g"][i], w["ffn_w_up"], i)
    act = _ffn_gate_fwd(hh, w["ffn_conv_w"][i], w["ffn_conv_b"][i])
    h2 = _matmul(act, w["ffn_w_down"], name="ffn_down", out_dtype=F32, tm=512, tn=1024, residual=h1, layer=i)
    h3, xn3 = _ple_fwd(h2, w["norm_ple_g"][i], p_i, w["ple_w_gate"], w["ple_w_proj"], i)
    saved = dict(h0=h0, xn1=xn1, qkv=qkv, zs=zs, us=us, y_attn=y_attn, lse=lse, y_sgu=y_sgu, y_ssm=y_ssm,
                 entry=entry, mix=mix, h1=h1, xn2=xn2, hh=hh, act=act, h2=h2, xn3=xn3, consts=c)
    return h3, saved


def _layer_bwd(dh3, sv, p_i, w, i, bias):
    c = sv["consts"]
    g = {}
    dpre, dproj = _ple_bwd(sv["xn3"], p_i, w["ple_w_gate"], w["ple_w_proj"], dh3, i)
    g["ple_w_gate"] = _matmul_tn(sv["xn3"], dpre, name="d_ple_w_gate", tk=1024, tn=1024)
    g["ple_w_proj"] = _matmul_tn(p_i, dproj, name="d_ple_w_proj", tk=256, tn=1024)
    dh2, g["norm_ple_g"] = _matmul_rms_bwd(dpre, w["ple_w_gate"], sv["h2"], w["norm_ple_g"][i], dh3,
                                           name="d_xn_ple", layer=i, tm=512)
    g["ffn_w_down"] = _matmul_tn(sv["act"], dh2, name="d_ffn_w_down", tk=1408, tn=1024)
    dact = _matmul(dh2, w["ffn_w_down"], name="d_ffn_act", out_dtype=MXU_DTYPE, tm=512, tn=1408, trans_b=True, layer=i)
    dhh, g["ffn_conv_w"], g["ffn_conv_b"] = _ffn_gate_bwd(sv["hh"], w["ffn_conv_w"][i], w["ffn_conv_b"][i], dact)
    g["ffn_w_up"] = _matmul_tn(sv["xn2"], dhh, name="d_ffn_w_up", tk=1024, tn=1408)
    dh1, g["norm_ffn_g"] = _matmul_rms_bwd(dhh, w["ffn_w_up"], sv["h1"], w["norm_ffn_g"][i], dh2,
                                           name="d_xn_ffn", layer=i, tm=256)
    g["w_out"] = _matmul_tn(sv["mix"], dh1, name="d_w_out", tk=1024, tn=1024)
    dmix = _matmul(dh1, w["w_out"], name="d_mix", out_dtype=F32, tm=512, tn=1024, trans_b=True, layer=i)
    dy_attn, delta, dy_sgu, dy_ssm, g["branch_norm_g"] = _mix_bwd(sv["y_attn"], sv["y_sgu"], sv["y_ssm"],
                                                                  w["branch_norm_g"][i], dmix)
    dq, dk, dv, ek, ev, dbias = _attn2_bwd(sv["qkv"], bias, sv["lse"], delta, dy_attn)
    dzs, g["sgu_ln_g"], g["sgu_ln_b"], dsw, dsb = _sgu2_bwd(sv["zs"], w["sgu_ln_g"][i], w["sgu_ln_b"][i],
                                                          c["sgu_w_mask"], c["sgu_b_t"], dy_sgu)
    causal = jnp.asarray(np.tril(np.ones((SGU_CHUNK, SGU_CHUNK), np.float32)))
    g["sgu_w"] = dsw * causal
    g["sgu_b"] = dsb.T
    dus, dbb, dcm, da, g["ssm_d"], g["ssm_glu_w"], g["ssm_glu_b"] = _ssm_bwd(
        sv["us"], sv["entry"], c["ssm_ops"], w["ssm_d"][i], w["ssm_glu_w"][i], w["ssm_glu_b"][i], dy_ssm)
    dbb5 = dbb.reshape(SSM_G, SSM_C, 2, SSM_G, SSM_N)
    dbbar = jnp.einsum("gcpgn->pgnc", dbb5)
    dcm5 = dcm.reshape(2, SSM_G, SSM_N, SSM_G, SSM_C)
    dcc = jnp.einsum("pgngc->pgcn", dcm5)
    g["ssm_c_re"] = dcc[0]
    g["ssm_c_im"] = -dcc[1]
    da2 = da.reshape(2, SSM_G, SSM_N)
    _, vjp = jax.vjp(_ssm_discretize, w["ssm_a_re"][i], w["ssm_a_im"][i], w["ssm_log_dt"][i],
                     w["ssm_b_re"][i], w["ssm_b_im"][i])
    (g["ssm_a_re"], g["ssm_a_im"], g["ssm_log_dt"], g["ssm_b_re"], g["ssm_b_im"]) = vjp(
        (da2[0], da2[1], dbbar[0], dbbar[1]))
    dz = _attn2_bwd_sum(dq, dk, dv, ek, ev, dzs, dus)
    g["w_in"] = _matmul_tn(sv["xn1"], dz, name="d_w_in", tk=1024, tn=1152)
    dh0, g["norm_attn_g"] = _matmul_rms_bwd(dz, w["w_in"], sv["h0"], w["norm_attn_g"][i], dh1,
                                            name="d_xn_attn", layer=i, tm=512)
    for k in ("norm_ple_g", "norm_ffn_g", "branch_norm_g", "norm_attn_g", "sgu_ln_g", "sgu_ln_b", "ssm_d",
              "ssm_glu_b", "ffn_conv_b"):
        g[k] = g[k].reshape(-1)
    return dh0, g, dbias


def _local_step(x, p, target, w, ff_interleaved=False):
    ff_names = ("ffn_conv_b",) if ff_interleaved else FF_SHARDED + ("ffn_conv_b",)
    w = dict(w)
    for k in ff_names:
        w[k] = _interleave_ff(w[k])
    bias = _bias_build(w["rel_bias"])
    h = x
    saved = []
    for i in range(DEPTH):
        h, sv = _layer_fwd(h, p[i], w, i, bias)
        saved.append(sv)
    loss, dh, dgf = _loss_head(h, w["final_norm_g"], target)
    layer_grads = [None] * DEPTH
    dbias = None
    for i in reversed(range(DEPTH)):
        dh, layer_grads[i], db = _layer_bwd(dh, saved[i], p[i], w, i, bias)
        dbias = db if dbias is None else dbias + db
    grads = {k: jnp.stack([layer_grads[i][k] for i in range(DEPTH)]) for k in layer_grads[0]}
    for k in ff_names:
        grads[k] = _deinterleave_ff(grads[k])
    grads["rel_bias"] = _bias_reduce(dbias)
    grads["final_norm_g"] = dgf.reshape(-1)
    return loss, dh, grads


def _as_rows(a, rows=None):
    size = int(np.prod(a.shape))
    if rows is None:
        rows = -(-size // (16 * PACK_COLS)) * 16
    if size % PACK_COLS:
        a = jnp.pad(a.reshape(-1), (0, (-size) % PACK_COLS))
    a2 = a.reshape(-1, PACK_COLS)
    return jnp.pad(a2, ((0, rows - a2.shape[0]), (0, 0)))


def _shard_shape(name):
    full, ax = BIG_FULL[name]
    shp = [DEPTH] + list(full)
    shp[ax] //= N_CHIPS
    return tuple(shp)


EXACT_NAMES = ("ffn_conv_w",)


def _pack_rows_of(name):
    n = int(np.prod(_shard_shape(name))) * (2 if name in EXACT_NAMES else 1)
    rows = -(-n // PACK_COLS)
    return -(-rows // 16) * 16


def _pack_shards(shards, dtype, exact=False):
    split_words = exact and jnp.dtype(dtype).itemsize == 2
    parts = []
    for n in BIG_NAMES:
        a = shards[n]
        if split_words and n in EXACT_NAMES:
            a = lax.bitcast_convert_type(a.astype(F32), dtype)
        parts.append(_as_rows(a.astype(dtype), _pack_rows_of(n)))
    used = sum(pt.shape[0] for pt in parts)
    parts.append(jnp.zeros((PACK_ROWS - used, PACK_COLS), dtype))
    return jnp.concatenate(parts, axis=0)


def _unpack_shard(flat, name, exact=False):
    off = 0
    for n in BIG_NAMES:
        if n == name:
            break
        off += _pack_rows_of(n)
    shp = _shard_shape(name)
    cnt = int(np.prod(shp))
    if exact and name in EXACT_NAMES and jnp.dtype(flat.dtype).itemsize == 2:
        vec = flat[off:off + _pack_rows_of(name)].reshape(-1)
        return lax.bitcast_convert_type(vec[:2 * cnt].reshape(shp + (2,)), F32)
    if cnt % PACK_COLS == 0:
        return flat[off:off + cnt // PACK_COLS].reshape(shp)
    return flat[off:off + _pack_rows_of(name)].reshape(-1)[:cnt].reshape(shp)


FF_SHARDED = ("ffn_w_up", "ffn_conv_w")
FF_CHIP_ORDER = (0, 2, 1, 3)


def _chip_order(name):
    return FF_CHIP_ORDER if name in FF_SHARDED else tuple(range(N_CHIPS))


def _split_full(full, name):
    _, ax = BIG_FULL[name]
    parts = jnp.split(full, N_CHIPS, axis=ax)
    out = [None] * N_CHIPS
    for j, k in enumerate(_chip_order(name)):
        out[k] = parts[j]
    return out


def _join_shards(shards, name):
    _, ax = BIG_FULL[name]
    return jnp.concatenate([shards[k] for k in _chip_order(name)], axis=ax)


def _small_shapes(w):
    return [(n, w[n].shape) for n in SMALL_NAMES]


def _small_rows(shp):
    return -(-int(np.prod(shp)) // (8 * PACK_COLS)) * 8


def _pack_small(d):
    parts = [_as_rows(d[n].astype(F32), _small_rows(d[n].shape)) for n in SMALL_NAMES]
    used = sum(pt.shape[0] for pt in parts)
    parts.append(jnp.zeros((SMALL_ROWS - used, PACK_COLS), F32))
    return jnp.concatenate(parts, axis=0)


def _unpack_small(flat, shapes):
    out, off = {}, 0
    for n, shp in shapes:
        cnt = int(np.prod(shp))
        rows = _small_rows(shp)
        if cnt % PACK_COLS == 0:
            out[n] = flat[off:off + cnt // PACK_COLS].reshape(shp)
        else:
            out[n] = flat[off:off + rows].reshape(-1)[:cnt].reshape(shp)
        off += rows
    return out


MESH = pl.DeviceIdType.MESH
ANY = pl.BlockSpec(memory_space=pl.ANY)


def _me():
    return lax.axis_index("x"), lax.axis_index("y"), lax.axis_index("c")


def _other_chips(x, y):
    return [(1 - x, y), (x, 1 - y), (1 - x, 1 - y)]


def _gather_weights(wflat):
    def body(w_ref, out_ref, send_sems, recv_sems):
        x, y, c = _me()
        sibling = (x, y, 1 - c)
        chips = _other_chips(x, y)

        def rows(chip, half):
            return out_ref.at[2 * chip[0] + chip[1], pl.ds(half * PACK_HALF, PACK_HALF), :]

        def copy(k, chip, half, to, src=None):
            return pltpu.make_async_remote_copy(
                src_ref=rows(chip, half) if src is None else src, dst_ref=rows(chip, half),
                send_sem=send_sems.at[k], recv_sem=recv_sems.at[k], device_id=to, device_id_type=MESH)

        my_half = w_ref.at[pl.ds(c * PACK_HALF, PACK_HALF), :]
        first = [copy(j, (x, y), c, (*chip, c), src=my_half) for j, chip in enumerate(chips)]
        for cp in first:
            cp.start()
        passed = [copy(3 + j, chip, c, sibling) for j, chip in enumerate(chips)]
        for j, chip in enumerate(chips):
            copy(j, chip, c, (x, y, c)).wait_recv()
            passed[j].start()
        for j, chip in enumerate(chips):
            copy(3 + j, chip, 1 - c, (x, y, c)).wait_recv()
        for cp in first + passed:
            cp.wait_send()

    return pl.pallas_call(
        body, name="gather_weights", in_specs=[ANY], out_specs=ANY,
        out_shape=jax.ShapeDtypeStruct((N_CHIPS, PACK_ROWS, PACK_COLS), wflat.dtype),
        scratch_shapes=[pltpu.SemaphoreType.DMA((6,)), pltpu.SemaphoreType.DMA((6,))],
    )(wflat)


def _fill_own_shard(wall, wflat, chip_idx):
    rows = PACK_ROWS // 8

    def body(idx_ref, w_ref, wall_ref, o_ref):
        del idx_ref, wall_ref
        o_ref[...] = w_ref[...]

    return pl.pallas_call(
        body, name="fill_own_shard",
        grid_spec=pltpu.PrefetchScalarGridSpec(
            num_scalar_prefetch=1, grid=(PACK_ROWS // rows,),
            in_specs=[pl.BlockSpec((rows, PACK_COLS), lambda i, idx: (i, 0)), ANY],
            out_specs=pl.BlockSpec((None, rows, PACK_COLS), lambda i, idx: (idx[0], i, 0))),
        out_shape=jax.ShapeDtypeStruct(wall.shape, wall.dtype),
        input_output_aliases={2: 0},
        compiler_params=_params(("parallel",)),
    )(chip_idx, wflat, wall)


def _exchange_partials(gb, gs):
    def body(gb_ref, gs_ref, half_ref, small_ref, send_sems, recv_sems, local_sem):
        x, y, c = _me()
        me_idx = 4 * x + 2 * y + c
        mine = pltpu.make_async_copy(gs_ref, small_ref.at[me_idx], local_sem)
        mine.start()
        d2d = pltpu.make_async_remote_copy(
            src_ref=gb_ref.at[:, pl.ds((1 - c) * PACK_HALF, PACK_HALF), :], dst_ref=half_ref,
            send_sem=send_sems.at[0], recv_sem=recv_sems.at[0], device_id=(x, y, 1 - c), device_id_type=MESH)
        d2d.start()
        copies = []
        for k in range(1, N_DEV):
            fx, fy, fc = (k >> 2) & 1, (k >> 1) & 1, k & 1
            peer = (x ^ fx, y ^ fy, c ^ fc)
            copies.append(pltpu.make_async_remote_copy(
                src_ref=gs_ref, dst_ref=small_ref.at[me_idx], send_sem=send_sems.at[k], recv_sem=recv_sems.at[k],
                device_id=peer, device_id_type=MESH))
        for cp in copies:
            cp.start()
        for k in range(1, N_DEV):
            fx, fy, fc = (k >> 2) & 1, (k >> 1) & 1, k & 1
            peer_idx = 4 * (x ^ fx) + 2 * (y ^ fy) + (c ^ fc)
            pltpu.make_async_remote_copy(
                src_ref=gs_ref, dst_ref=small_ref.at[peer_idx], send_sem=send_sems.at[k], recv_sem=recv_sems.at[k],
                device_id=(x, y, c), device_id_type=MESH).wait_recv()
        d2d.wait_recv()
        d2d.wait_send()
        for cp in copies:
            cp.wait_send()
        mine.wait()

    return pl.pallas_call(
        body, name="exchange_partials", in_specs=[ANY, pl.BlockSpec(memory_space=pltpu.VMEM)], out_specs=[ANY, ANY],
        out_shape=[jax.ShapeDtypeStruct((N_CHIPS, PACK_HALF, PACK_COLS), gb.dtype),
                   jax.ShapeDtypeStruct((N_DEV, SMALL_ROWS, PACK_COLS), F32)],
        scratch_shapes=[pltpu.SemaphoreType.DMA((N_DEV,)), pltpu.SemaphoreType.DMA((N_DEV,)), pltpu.SemaphoreType.DMA],
    )(gb, gs)


RED_ROWS = 256


def _chip_partials(gb, sib, c_idx):
    nrow = PACK_HALF // RED_ROWS

    def body(c_ref, a_ref, b_ref, o_ref):
        del c_ref
        o_ref[...] = (a_ref[...].astype(F32) + b_ref[...].astype(F32)).astype(o_ref.dtype)

    blk = (1, RED_ROWS, PACK_COLS)
    return pl.pallas_call(
        body, name="chip_partials",
        grid_spec=pltpu.PrefetchScalarGridSpec(
            num_scalar_prefetch=1, grid=(N_CHIPS, nrow),
            in_specs=[pl.BlockSpec(blk, lambda k, i, c: (k, c[0] * nrow + i, 0)),
                      pl.BlockSpec(blk, lambda k, i, c: (k, i, 0))],
            out_specs=pl.BlockSpec(blk, lambda k, i, c: (k, i, 0))),
        out_shape=jax.ShapeDtypeStruct((N_CHIPS, PACK_HALF, PACK_COLS), gb.dtype),
        compiler_params=_params(("parallel", "parallel")),
    )(c_idx, gb, sib)


def _scatter_partials(pc):
    def body(pc_ref, out_ref, send_sems, recv_sems):
        x, y, c = _me()
        chips = _other_chips(x, y)
        copies = [pltpu.make_async_remote_copy(
            src_ref=pc_ref.at[2 * chip[0] + chip[1]], dst_ref=out_ref.at[k],
            send_sem=send_sems.at[k], recv_sem=recv_sems.at[k], device_id=(*chip, c), device_id_type=MESH)
            for k, chip in enumerate(chips)]
        for cp in copies:
            cp.start()
        for cp in copies:
            cp.wait_recv()
        for cp in copies:
            cp.wait_send()

    return pl.pallas_call(
        body, name="scatter_partials", in_specs=[ANY], out_specs=ANY,
        out_shape=jax.ShapeDtypeStruct((3, PACK_HALF, PACK_COLS), pc.dtype),
        scratch_shapes=[pltpu.SemaphoreType.DMA((3,)), pltpu.SemaphoreType.DMA((3,))],
    )(pc)


def _final_half(gb, sib, recv, idx):
    nrow = PACK_HALF // RED_ROWS

    def body(idx_ref, a_ref, b_ref, r_ref, o_ref):
        del idx_ref
        acc = a_ref[0].astype(F32) + b_ref[0].astype(F32)
        for k in range(3):
            acc = acc + r_ref[k].astype(F32)
        o_ref[...] = acc

    return pl.pallas_call(
        body, name="final_half",
        grid_spec=pltpu.PrefetchScalarGridSpec(
            num_scalar_prefetch=1, grid=(nrow,),
            in_specs=[pl.BlockSpec((1, RED_ROWS, PACK_COLS), lambda i, idx: (idx[0], idx[1] * nrow + i, 0)),
                      pl.BlockSpec((1, RED_ROWS, PACK_COLS), lambda i, idx: (idx[0], i, 0)),
                      pl.BlockSpec((3, RED_ROWS, PACK_COLS), lambda i, idx: (0, i, 0))],
            out_specs=pl.BlockSpec((RED_ROWS, PACK_COLS), lambda i, idx: (i, 0))),
        out_shape=jax.ShapeDtypeStruct((PACK_HALF, PACK_COLS), F32),
        compiler_params=_params(("parallel",)),
    )(idx, gb, sib, recv)


def _share_halves(half):
    def body(h_ref, out_ref, send_sem, recv_sem):
        x, y, c = _me()
        cp = pltpu.make_async_remote_copy(src_ref=h_ref, dst_ref=out_ref, send_sem=send_sem, recv_sem=recv_sem,
                                          device_id=(x, y, 1 - c), device_id_type=MESH)
        cp.start()
        cp.wait_recv()
        cp.wait_send()

    return pl.pallas_call(
        body, name="share_halves", in_specs=[ANY], out_specs=ANY,
        out_shape=jax.ShapeDtypeStruct((PACK_HALF, PACK_COLS), F32),
        scratch_shapes=[pltpu.SemaphoreType.DMA, pltpu.SemaphoreType.DMA],
    )(half)


def _sum_small(allsmall):
    def body(a_ref, o_ref):
        acc = a_ref[0]
        for k in range(1, N_DEV):
            acc = acc + a_ref[k]
        o_ref[...] = acc

    tr = 96
    return pl.pallas_call(
        body, name="sum_small", grid=(SMALL_ROWS // tr,),
        in_specs=[pl.BlockSpec((N_DEV, tr, PACK_COLS), lambda i: (0, i, 0))],
        out_specs=pl.BlockSpec((tr, PACK_COLS), lambda i: (i, 0)),
        out_shape=jax.ShapeDtypeStruct((SMALL_ROWS, PACK_COLS), F32),
        compiler_params=_params(("parallel",)),
    )(allsmall)


def _adamw(w, g, m, v, *, name):
    shape = w.shape
    cols = shape[-1]
    as2 = lambda t: t.reshape(-1, cols)
    w2, g2, m2, v2 = as2(w), as2(g), as2(m), as2(v)
    rows = w2.shape[0]
    tr = rows
    if rows * cols * 4 > (1 << 20):
        tr = _tile(rows, max(8, (1 << 20) // (cols * 4) // 8 * 8), 8)

    def body(w_ref, g_ref, m_ref, v_ref, d_ref, mo_ref, vo_ref):
        gg = g_ref[...]
        mn = ADAM_B1 * m_ref[...] + (1.0 - ADAM_B1) * gg
        vn = ADAM_B2 * v_ref[...] + (1.0 - ADAM_B2) * (gg * gg)
        m_hat = mn / (1.0 - ADAM_B1 ** ADAM_STEP)
        v_hat = vn / (1.0 - ADAM_B2 ** ADAM_STEP)
        d_ref[...] = -ADAM_LR * (m_hat / (jnp.sqrt(v_hat) + ADAM_EPS) + ADAM_WD * w_ref[...])
        mo_ref[...] = mn
        vo_ref[...] = vn

    blk = pl.BlockSpec((tr, cols), lambda i: (i, 0))
    outs = pl.pallas_call(
        body, name=name, grid=(rows // tr,), in_specs=[blk] * 4, out_specs=[blk] * 3,
        out_shape=[jax.ShapeDtypeStruct((rows, cols), F32)] * 3,
        compiler_params=_params(("parallel",)),
    )(w2, g2, m2, v2)
    return tuple(t.reshape(shape) for t in outs)


def _adamw_many(ws, gs, ms, vs):
    n = len(ws)

    def body(*refs):
        for t in range(n):
            w_ref, g_ref, m_ref, v_ref = refs[t], refs[n + t], refs[2 * n + t], refs[3 * n + t]
            d_ref, mo_ref, vo_ref = refs[4 * n + t], refs[5 * n + t], refs[6 * n + t]
            gg = g_ref[...]
            mn = ADAM_B1 * m_ref[...] + (1.0 - ADAM_B1) * gg
            vn = ADAM_B2 * v_ref[...] + (1.0 - ADAM_B2) * (gg * gg)
            m_hat = mn / (1.0 - ADAM_B1 ** ADAM_STEP)
            v_hat = vn / (1.0 - ADAM_B2 ** ADAM_STEP)
            d_ref[...] = -ADAM_LR * (m_hat / (jnp.sqrt(v_hat) + ADAM_EPS) + ADAM_WD * w_ref[...])
            mo_ref[...] = mn
            vo_ref[...] = vn

    vmem = pl.BlockSpec(memory_space=pltpu.VMEM)
    outs = pl.pallas_call(
        body, name="adamw_small", in_specs=[vmem] * (4 * n), out_specs=[vmem] * (3 * n),
        out_shape=[jax.ShapeDtypeStruct(w.shape, F32) for w in ws] * 3,
        compiler_params=pltpu.CompilerParams(vmem_limit_bytes=VMEM_LIMIT_BYTES),
    )(*ws, *gs, *ms, *vs)
    return outs[:n], outs[n:2 * n], outs[2 * n:]


def kernel(x, p, rel_bias, norm_attn_g, w_in, sgu_ln_g, sgu_ln_b, sgu_w, sgu_b, ssm_a_re, ssm_a_im, ssm_log_dt, ssm_b_re, ssm_b_im, ssm_c_re, ssm_c_im, ssm_d, ssm_glu_w, ssm_glu_b, branch_norm_g, w_out, norm_ffn_g, ffn_w_up, ffn_conv_w, ffn_conv_b, ffn_w_down, norm_ple_g, ple_w_gate, ple_w_proj, final_norm_g, loss_target, m_rel_bias, m_norm_attn_g, m_w_in, m_sgu_ln_g, m_sgu_ln_b, m_sgu_w, m_sgu_b, m_ssm_a_re, m_ssm_a_im, m_ssm_log_dt, m_ssm_b_re, m_ssm_b_im, m_ssm_c_re, m_ssm_c_im, m_ssm_d, m_ssm_glu_w, m_ssm_glu_b, m_branch_norm_g, m_w_out, m_norm_ffn_g, m_ffn_w_up, m_ffn_conv_w, m_ffn_conv_b, m_ffn_w_down, m_norm_ple_g, m_ple_w_gate, m_ple_w_proj, m_final_norm_g, v_rel_bias, v_norm_attn_g, v_w_in, v_sgu_ln_g, v_sgu_ln_b, v_sgu_w, v_sgu_b, v_ssm_a_re, v_ssm_a_im, v_ssm_log_dt, v_ssm_b_re, v_ssm_b_im, v_ssm_c_re, v_ssm_c_im, v_ssm_d, v_ssm_glu_w, v_ssm_glu_b, v_branch_norm_g, v_w_out, v_norm_ffn_g, v_ffn_w_up, v_ffn_conv_w, v_ffn_conv_b, v_ffn_w_down, v_norm_ple_g, v_ple_w_gate, v_ple_w_proj, v_final_norm_g):
    args = dict(locals())
    wts = {n: args[n] for n in WEIGHT_NAMES}
    mom_m = {n: args["m_" + n] for n in WEIGHT_NAMES}
    mom_v = {n: args["v_" + n] for n in WEIGHT_NAMES}

    xi, yi, ci = _me()
    wflat = _pack_shards({n: wts[n] for n in BIG_NAMES}, MXU_DTYPE, exact=True)
    wall = _fill_own_shard(_gather_weights(wflat), wflat, jnp.stack([2 * xi + yi]).astype(jnp.int32))
    full = dict(wts)
    for n in BIG_NAMES:
        full[n] = _join_shards([_unpack_shard(wall[k], n, exact=True) for k in range(N_CHIPS)], n)
    full["ffn_conv_w"] = full["ffn_conv_w"].astype(F32)

    loss, dx, grads = _local_step(x[0], p[:, 0], loss_target[0], full, ff_interleaved=True)

    xi, yi, ci = _me()
    stacked = {n: _split_full(grads[n], n) for n in BIG_NAMES}
    gb = jnp.stack([_pack_shards({n: stacked[n][k] for n in BIG_NAMES}, MXU_DTYPE) for k in range(N_CHIPS)])
    gs = _pack_small(grads).at[SMALL_ROWS - 1, 0].set(loss[0, 0])
    sib, allsmall = _exchange_partials(gb, gs)
    pc = _chip_partials(gb, sib, jnp.stack([ci]).astype(jnp.int32))
    recv = _scatter_partials(pc)
    half = _final_half(gb, sib, recv, jnp.stack([2 * xi + yi, ci]).astype(jnp.int32))
    other = _share_halves(half)
    gflat = jnp.concatenate([jnp.where(ci == 0, half, other), jnp.where(ci == 0, other, half)], axis=0)
    small_sum = _sum_small(allsmall)
    loss = small_sum[SMALL_ROWS - 1, 0]
    gsmall = _unpack_small(small_sum, _small_shapes(wts))

    g_out, d_out, m_out, v_out = {}, {}, {}, {}
    for n in BIG_NAMES:
        g_out[n] = _unpack_shard(gflat, n)
        d_out[n], m_out[n], v_out[n] = _adamw(wts[n], g_out[n], mom_m[n], mom_v[n], name="adamw_" + n)
    d_sm, m_sm, v_sm = _adamw_many([wts[n] for n in SMALL_NAMES], [gsmall[n] for n in SMALL_NAMES],
                                   [mom_m[n] for n in SMALL_NAMES], [mom_v[n] for n in SMALL_NAMES])
    for t, n in enumerate(SMALL_NAMES):
        g_out[n], d_out[n], m_out[n], v_out[n] = gsmall[n], d_sm[t], m_sm[t], v_sm[t]

    return (loss, dx[None], *[g_out[n] for n in WEIGHT_NAMES], *[d_out[n] for n in WEIGHT_NAMES],
            *[m_out[n] for n in WEIGHT_NAMES], *[v_out[n] for n in WEIGHT_NAMES])
```
